```python
import jax, jax.numpy as jnp
from jax import lax
import numpy as np

D_MODEL = 1024
BATCH = 16
SEQ = 4096
DEPTH = 4

HEAD_DIM = 64
N_Q_A = 16
N_KV_A = 2
GROUP_A = N_Q_A // N_KV_A
WINDOW = 128
BLOCK = 128
N_H_B = 16
ROT_DIM = HEAD_DIM // 4
ROPE_THETA = 500000.0
D_FF = -(-8 * D_MODEL // (3 * 256)) * 256
N_MIXERS = 2
N_A = (DEPTH + 1) // 2
N_B = DEPTH // 2
QKV_A = (N_Q_A + 2 * N_KV_A) * HEAD_DIM
QKV_B = 3 * N_H_B * HEAD_DIM
EPS = 1e-6

kernel_name = 'hybrid_swa_sink_stickbreak_block'


def rmsnorm(x, gain):
    xf = x.astype(jnp.float32)
    y = xf * lax.rsqrt(jnp.mean(xf * xf, axis=-1, keepdims=True) + EPS)
    return (y * gain.astype(jnp.float32)).astype(x.dtype)


def partial_rope(x, positions):
    half = ROT_DIM // 2
    inv_freq = jnp.power(jnp.float32(ROPE_THETA), -jnp.arange(half, dtype=jnp.float32) * 2.0 / ROT_DIM)
    ang = positions.astype(jnp.float32)[:, :, None, None] * inv_freq
    cos, sin = jnp.cos(ang), jnp.sin(ang)
    xr = x[..., :ROT_DIM].astype(jnp.float32)
    x1, x2 = xr[..., :half], xr[..., half:]
    rot = jnp.concatenate([x1 * cos - x2 * sin, x2 * cos + x1 * sin], axis=-1).astype(x.dtype)
    return jnp.concatenate([rot, x[..., ROT_DIM:]], axis=-1)


def sliding_window_sink_attention(h, positions, w_qkv, q_gain, k_gain, sinks, w_o):
    B, S, _ = h.shape
    qkv = h @ w_qkv
    q, k, v = jnp.split(qkv, [N_Q_A * HEAD_DIM, (N_Q_A + N_KV_A) * HEAD_DIM], axis=-1)
    q = q.reshape(B, S, N_Q_A, HEAD_DIM)
    k = k.reshape(B, S, N_KV_A, HEAD_DIM)
    v = v.reshape(B, S, N_KV_A, HEAD_DIM)
    q = partial_rope(rmsnorm(q, q_gain), positions)
    k = partial_rope(rmsnorm(k, k_gain), positions)
    q = q.reshape(B, S, N_KV_A, GROUP_A, HEAD_DIM)
    pad = jnp.zeros((B, BLOCK, N_KV_A, HEAD_DIM), k.dtype)
    kp = jnp.concatenate([pad, k], axis=1)
    vp = jnp.concatenate([pad, v], axis=1)
    scale = HEAD_DIM ** -0.5
    q_idx = jnp.arange(BLOCK)[:, None] + BLOCK
    k_idx = jnp.arange(2 * BLOCK)[None, :]
    rel = q_idx - k_idx
    band = (rel >= 0) & (rel < WINDOW)
    sink_logit = sinks.astype(jnp.float32).reshape(1, N_KV_A, GROUP_A, 1, 1)

    def block_fn(i):
        start = i * BLOCK
        qb = lax.dynamic_slice_in_dim(q, start, BLOCK, axis=1)
        kb = lax.dynamic_slice_in_dim(kp, start, 2 * BLOCK, axis=1)
        vb = lax.dynamic_slice_in_dim(vp, start, 2 * BLOCK, axis=1)
        s = jnp.einsum('bqkgd,bskd->bkgqs', qb, kb).astype(jnp.float32) * scale
        valid = band & (start - BLOCK + k_idx >= 0)
        s = jnp.where(valid, s, -jnp.inf)
        sink_col = jnp.broadcast_to(sink_logit, s.shape[:-1] + (1,))
        p = jax.nn.softmax(jnp.concatenate([s, sink_col], axis=-1), axis=-1)[..., :-1]
        o = jnp.einsum('bkgqs,bskd->bqkgd', p.astype(vb.dtype), vb)
        return o.reshape(B, BLOCK, N_Q_A * HEAD_DIM)

    out = lax.map(block_fn, jnp.arange(S // BLOCK))
    out = jnp.moveaxis(out, 0, 1).reshape(B, S, N_Q_A * HEAD_DIM)
    return out @ w_o


def stick_breaking_attention(h, w_qkv, w_o):
    B, S, _ = h.shape
    qkv = h @ w_qkv
    q, k, v = jnp.split(qkv, 3, axis=-1)
    q = q.reshape(B, S, N_H_B, HEAD_DIM)
    k = k.reshape(B, S, N_H_B, HEAD_DIM)
    v = v.reshape(B, S, N_H_B, HEAD_DIM)
    scale = HEAD_DIM ** -0.5
    k_idx = jnp.arange(S)[None, :]

    def block_fn(i):
        start = i * BLOCK
        qb = lax.dynamic_slice_in_dim(q, start, BLOCK, axis=1)
        z = jnp.einsum('bqhd,bshd->bhqs', qb, k).astype(jnp.float32) * scale
        t_idx = start + jnp.arange(BLOCK)[:, None]
        strict = k_idx < t_idx
        log_beta = jax.nn.log_sigmoid(z)
        log_one_minus = jnp.where(strict, jax.nn.log_sigmoid(-z), 0.0)
        rc = lax.cumsum(log_one_minus, axis=3, reverse=True)
        after = jnp.pad(rc[..., 1:], ((0, 0), (0, 0), (0, 0), (0, 1)))
        a = jnp.where(strict, jnp.exp(log_beta + after), 0.0)
        o = jnp.einsum('bhqs,bshd->bqhd', a.astype(v.dtype), v)
        return o.reshape(B, BLOCK, N_H_B * HEAD_DIM)

    out = lax.map(block_fn, jnp.arange(S // BLOCK))
    out = jnp.moveaxis(out, 0, 1).reshape(B, S, N_H_B * HEAD_DIM)
    return out @ w_o


def swiglu(h, w_gate, w_up, w_down):
    return (jax.nn.silu(h @ w_gate) * (h @ w_up)) @ w_down


def _fwd_setup_inputs(seed: int = 0) -> dict:
    key = jax.random.key(seed)
    ks = jax.random.split(key, 20)
    f32 = jnp.float32
    nrm = lambda k, shape, s: jax.random.normal(k, shape, f32) * s
    x = jax.random.normal(ks[0], (BATCH, SEQ, D_MODEL), f32)
    c = jax.random.normal(ks[1], (BATCH, D_MODEL), f32)
    offset = jax.random.randint(ks[2], (BATCH, 1), 0, 4096, dtype=jnp.int32)
    positions = offset + jnp.arange(SEQ, dtype=jnp.int32)[None, :]
    return {
        'x': x,
        'c': c,
        'positions': positions,
        'ada_w': nrm(ks[3], (DEPTH, D_MODEL, 6 * D_MODEL), 0.5 * D_MODEL ** -0.5),
        'ada_b': nrm(ks[4], (DEPTH, 6 * D_MODEL), 0.01),
        'norm1_g': 1.0 + nrm(ks[5], (DEPTH, D_MODEL), 0.05),
        'norm2_g': 1.0 + nrm(ks[6], (DEPTH, D_MODEL), 0.05),
        'wqkv_a': nrm(ks[7], (N_A, D_MODEL, QKV_A), D_MODEL ** -0.5),
        'q_norm_a': 1.0 + nrm(ks[8], (N_A, HEAD_DIM), 0.05),
        'k_norm_a': 1.0 + nrm(ks[9], (N_A, HEAD_DIM), 0.05),
        'sinks_a': nrm(ks[10], (N_A, N_Q_A), 1.0),
        'wo_a': nrm(ks[11], (N_A, N_Q_A * HEAD_DIM, D_MODEL), (N_Q_A * HEAD_DIM) ** -0.5),
        'wqkv_b': nrm(ks[12], (N_B, D_MODEL, QKV_B), D_MODEL ** -0.5),
        'wo_b': nrm(ks[13], (N_B, N_H_B * HEAD_DIM, D_MODEL), (N_H_B * HEAD_DIM) ** -0.5),
        'w_gate': nrm(ks[14], (DEPTH, D_MODEL, D_FF), D_MODEL ** -0.5),
        'w_up': nrm(ks[15], (DEPTH, D_MODEL, D_FF), D_MODEL ** -0.5),
        'w_down': nrm(ks[16], (DEPTH, D_FF, D_MODEL), D_FF ** -0.5),
    }


def _fwd_reference(x, c, positions, ada_w, ada_b, norm1_g, norm2_g, wqkv_a, q_norm_a, k_norm_a,
              sinks_a, wo_a, wqkv_b, wo_b, w_gate, w_up, w_down):
    cond = jax.nn.silu(c)
    for i in range(DEPTH):
        mod = (cond @ ada_w[i] + ada_b[i])[:, None, :]
        sh1, sc1, g1, sh2, sc2, g2 = jnp.split(mod, 6, axis=-1)
        h = rmsnorm(x, norm1_g[i]) * (1.0 + sc1) + sh1
        j = i // N_MIXERS
        if i % N_MIXERS == 0:
            y = sliding_window_sink_attention(h, positions, wqkv_a[j], q_norm_a[j], k_norm_a[j],
                                              sinks_a[j], wo_a[j])
        else:
            y = stick_breaking_attention(h, wqkv_b[j], wo_b[j])
        x = x + g1 * y
        h = rmsnorm(x, norm2_g[i]) * (1.0 + sc2) + sh2
        x = x + g2 * swiglu(h, w_gate[i], w_up[i], w_down[i])
    return x


import jax as _jax
import jax.numpy as _jnp

TWIN_FORMAT = 'train_step'
FWD_PARAMS = ['x', 'c', 'positions', 'ada_w', 'ada_b', 'norm1_g', 'norm2_g', 'wqkv_a', 'q_norm_a', 'k_norm_a', 'sinks_a', 'wo_a', 'wqkv_b', 'wo_b', 'w_gate', 'w_up', 'w_down']
TWIN_WEIGHTS = ['ada_w', 'ada_b', 'norm1_g', 'norm2_g', 'wqkv_a', 'q_norm_a', 'k_norm_a', 'sinks_a', 'wo_a', 'wqkv_b', 'wo_b', 'w_gate', 'w_up', 'w_down']
TWIN_DIFF_INPUT = 'x'
TWIN_INPUTS = ['x', 'c', 'positions', 'ada_w', 'ada_b', 'norm1_g', 'norm2_g', 'wqkv_a', 'q_norm_a', 'k_norm_a', 'sinks_a', 'wo_a', 'wqkv_b', 'wo_b', 'w_gate', 'w_up', 'w_down', 'loss_target', 'm_ada_w', 'm_ada_b', 'm_norm1_g', 'm_norm2_g', 'm_wqkv_a', 'm_q_norm_a', 'm_k_norm_a', 'm_sinks_a', 'm_wo_a', 'm_wqkv_b', 'm_wo_b', 'm_w_gate', 'm_w_up', 'm_w_down', 'v_ada_w', 'v_ada_b', 'v_norm1_g', 'v_norm2_g', 'v_wqkv_a', 'v_q_norm_a', 'v_k_norm_a', 'v_sinks_a', 'v_wo_a', 'v_wqkv_b', 'v_wo_b', 'v_w_gate', 'v_w_up', 'v_w_down']
TWIN_OUTPUTS = ['loss', 'grad_x', 'grad_ada_w', 'grad_ada_b', 'grad_norm1_g', 'grad_norm2_g', 'grad_wqkv_a', 'grad_q_norm_a', 'grad_k_norm_a', 'grad_sinks_a', 'grad_wo_a', 'grad_wqkv_b', 'grad_wo_b', 'grad_w_gate', 'grad_w_up', 'grad_w_down', 'delta_ada_w', 'delta_ada_b', 'delta_norm1_g', 'delta_norm2_g', 'delta_wqkv_a', 'delta_q_norm_a', 'delta_k_norm_a', 'delta_sinks_a', 'delta_wo_a', 'delta_wqkv_b', 'delta_wo_b', 'delta_w_gate', 'delta_w_up', 'delta_w_down', 'new_m_ada_w', 'new_m_ada_b', 'new_m_norm1_g', 'new_m_norm2_g', 'new_m_wqkv_a', 'new_m_q_norm_a', 'new_m_k_norm_a', 'new_m_sinks_a', 'new_m_wo_a', 'new_m_wqkv_b', 'new_m_wo_b', 'new_m_w_gate', 'new_m_w_up', 'new_m_w_down', 'new_v_ada_w', 'new_v_ada_b', 'new_v_norm1_g', 'new_v_norm2_g', 'new_v_wqkv_a', 'new_v_q_norm_a', 'new_v_k_norm_a', 'new_v_sinks_a', 'new_v_wo_a', 'new_v_wqkv_b', 'new_v_wo_b', 'new_v_w_gate', 'new_v_w_up', 'new_v_w_down']
TWIN_LEAF_KINDS = {'loss': 'loss', 'grad_x': 'grad_x', 'grad_ada_w': 'grad_w', 'grad_ada_b': 'grad_w', 'grad_norm1_g': 'grad_w', 'grad_norm2_g': 'grad_w', 'grad_wqkv_a': 'grad_w', 'grad_q_norm_a': 'grad_w', 'grad_k_norm_a': 'grad_w', 'grad_sinks_a': 'grad_w', 'grad_wo_a': 'grad_w', 'grad_wqkv_b': 'grad_w', 'grad_wo_b': 'grad_w', 'grad_w_gate': 'grad_w', 'grad_w_up': 'grad_w', 'grad_w_down': 'grad_w', 'delta_ada_w': 'delta_w', 'delta_ada_b': 'delta_w', 'delta_norm1_g': 'delta_w', 'delta_norm2_g': 'delta_w', 'delta_wqkv_a': 'delta_w', 'delta_q_norm_a': 'delta_w', 'delta_k_norm_a': 'delta_w', 'delta_sinks_a': 'delta_w', 'delta_wo_a': 'delta_w', 'delta_wqkv_b': 'delta_w', 'delta_wo_b': 'delta_w', 'delta_w_gate': 'delta_w', 'delta_w_up': 'delta_w', 'delta_w_down': 'delta_w', 'new_m_ada_w': 'new_m', 'new_m_ada_b': 'new_m', 'new_m_norm1_g': 'new_m', 'new_m_norm2_g': 'new_m', 'new_m_wqkv_a': 'new_m', 'new_m_q_norm_a': 'new_m', 'new_m_k_norm_a': 'new_m', 'new_m_sinks_a': 'new_m', 'new_m_wo_a': 'new_m', 'new_m_wqkv_b': 'new_m', 'new_m_wo_b': 'new_m', 'new_m_w_gate': 'new_m', 'new_m_w_up': 'new_m', 'new_m_w_down': 'new_m', 'new_v_ada_w': 'new_v', 'new_v_ada_b': 'new_v', 'new_v_norm1_g': 'new_v', 'new_v_norm2_g': 'new_v', 'new_v_wqkv_a': 'new_v', 'new_v_q_norm_a': 'new_v', 'new_v_k_norm_a': 'new_v', 'new_v_sinks_a': 'new_v', 'new_v_wo_a': 'new_v', 'new_v_wqkv_b': 'new_v', 'new_v_wo_b': 'new_v', 'new_v_w_gate': 'new_v', 'new_v_w_up': 'new_v', 'new_v_w_down': 'new_v'}


def _forward(args):
    return _fwd_reference(*[args[k] for k in FWD_PARAMS])


def _output_shape():
    out = _jax.eval_shape(lambda: _forward(_fwd_setup_inputs(0)))
    return out.shape, out.dtype

N_MICROBATCH = 1
ADAM_LR = 0.001
ADAM_B1 = 0.9
ADAM_B2 = 0.999
ADAM_EPS = 1e-08
ADAM_WD = 0.01
ADAM_STEP = 10
PER_EXAMPLE_BATCH_AXIS = {'x': 0, 'c': 0, 'positions': 0, 'loss_target': 0}
SHARED_INPUTS = []
_WEIGHT_DTYPES = {'ada_w': _jnp.float32, 'ada_b': _jnp.float32, 'norm1_g': _jnp.float32, 'norm2_g': _jnp.float32, 'wqkv_a': _jnp.float32, 'q_norm_a': _jnp.float32, 'k_norm_a': _jnp.float32, 'sinks_a': _jnp.float32, 'wo_a': _jnp.float32, 'wqkv_b': _jnp.float32, 'wo_b': _jnp.float32, 'w_gate': _jnp.float32, 'w_up': _jnp.float32, 'w_down': _jnp.float32}
MOMENT_SCALE = {'ada_w': 2.220434e+00, 'ada_b': 4.488866e+00, 'norm1_g': 2.783652e+00, 'norm2_g': 6.825950e+00, 'wqkv_a': 1.098551e+00, 'q_norm_a': 1.020198e+00, 'k_norm_a': 1.021547e+00, 'sinks_a': 2.313945e-01, 'wo_a': 8.946244e-01, 'wqkv_b': 4.508990e-01, 'wo_b': 7.254651e-01, 'w_gate': 1.281927e-01, 'w_up': 1.175110e-01, 'w_down': 1.833380e-01}


def _to_microbatches(a, axis):
    t = _jnp.moveaxis(a, axis, 0)
    t = t.reshape((N_MICROBATCH, t.shape[0] // N_MICROBATCH) + t.shape[1:])
    return _jnp.moveaxis(t, 1, axis + 1)


def setup_inputs(seed: int = 0) -> dict:
    inp = _fwd_setup_inputs(seed)
    key = _jax.random.fold_in(_jax.random.key(seed), 7919)
    shape, _ = _output_shape()
    out = dict(inp)
    out["loss_target"] = _jax.random.normal(_jax.random.fold_in(key, 0), shape, _jnp.float32)
    for i, name in enumerate(TWIN_WEIGHTS):
        w = inp[name].astype(_jnp.float32)
        if MOMENT_SCALE is None:
            s = _jnp.sqrt(_jnp.mean(_jnp.square(w)) + 1e-30)
        else:
            s = MOMENT_SCALE[name]
        km, kv = _jax.random.split(_jax.random.fold_in(key, i + 1))
        out[name] = w
        out["m_" + name] = s * _jax.random.normal(km, w.shape, _jnp.float32)
        out["v_" + name] = (s * s) * _jax.random.uniform(kv, w.shape, _jnp.float32, 0.5, 1.5)
    if N_MICROBATCH > 1:
        for name, axis in PER_EXAMPLE_BATCH_AXIS.items():
            out[name] = _to_microbatches(out[name], axis)
    return {'x': out['x'], 'c': out['c'], 'positions': out['positions'], 'ada_w': out['ada_w'], 'ada_b': out['ada_b'], 'norm1_g': out['norm1_g'], 'norm2_g': out['norm2_g'], 'wqkv_a': out['wqkv_a'], 'q_norm_a': out['q_norm_a'], 'k_norm_a': out['k_norm_a'], 'sinks_a': out['sinks_a'], 'wo_a': out['wo_a'], 'wqkv_b': out['wqkv_b'], 'wo_b': out['wo_b'], 'w_gate': out['w_gate'], 'w_up': out['w_up'], 'w_down': out['w_down'], 'loss_target': out['loss_target'], 'm_ada_w': out['m_ada_w'], 'm_ada_b': out['m_ada_b'], 'm_norm1_g': out['m_norm1_g'], 'm_norm2_g': out['m_norm2_g'], 'm_wqkv_a': out['m_wqkv_a'], 'm_q_norm_a': out['m_q_norm_a'], 'm_k_norm_a': out['m_k_norm_a'], 'm_sinks_a': out['m_sinks_a'], 'm_wo_a': out['m_wo_a'], 'm_wqkv_b': out['m_wqkv_b'], 'm_wo_b': out['m_wo_b'], 'm_w_gate': out['m_w_gate'], 'm_w_up': out['m_w_up'], 'm_w_down': out['m_w_down'], 'v_ada_w': out['v_ada_w'], 'v_ada_b': out['v_ada_b'], 'v_norm1_g': out['v_norm1_g'], 'v_norm2_g': out['v_norm2_g'], 'v_wqkv_a': out['v_wqkv_a'], 'v_q_norm_a': out['v_q_norm_a'], 'v_k_norm_a': out['v_k_norm_a'], 'v_sinks_a': out['v_sinks_a'], 'v_wo_a': out['v_wo_a'], 'v_wqkv_b': out['v_wqkv_b'], 'v_wo_b': out['v_wo_b'], 'v_w_gate': out['v_w_gate'], 'v_w_up': out['v_w_up'], 'v_w_down': out['v_w_down']}


def _loss(weights, diff, rest, loss_target):
    with _jax.named_scope("forward"):
        args = {**rest, TWIN_DIFF_INPUT: diff, **{k: w.astype(_WEIGHT_DTYPES[k]) for k, w in weights.items()}}
        y = _forward(args)
    with _jax.named_scope("loss_head"):
        err = _jnp.square(y.astype(_jnp.float32) - loss_target)
        return 0.5 * _jnp.sum(_jnp.mean(err, axis=-1)) if err.ndim else 0.5 * err


def _adamw(w, g, m, v):
    m = ADAM_B1 * m + (1.0 - ADAM_B1) * g
    v = ADAM_B2 * v + (1.0 - ADAM_B2) * _jnp.square(g)
    m_hat = m / (1.0 - ADAM_B1 ** ADAM_STEP)
    v_hat = v / (1.0 - ADAM_B2 ** ADAM_STEP)
    delta = -ADAM_LR * (m_hat / (_jnp.sqrt(v_hat) + ADAM_EPS) + ADAM_WD * w)
    return delta, m, v


def reference(x, c, positions, ada_w, ada_b, norm1_g, norm2_g, wqkv_a, q_norm_a, k_norm_a, sinks_a, wo_a, wqkv_b, wo_b, w_gate, w_up, w_down, loss_target, m_ada_w, m_ada_b, m_norm1_g, m_norm2_g, m_wqkv_a, m_q_norm_a, m_k_norm_a, m_sinks_a, m_wo_a, m_wqkv_b, m_wo_b, m_w_gate, m_w_up, m_w_down, v_ada_w, v_ada_b, v_norm1_g, v_norm2_g, v_wqkv_a, v_q_norm_a, v_k_norm_a, v_sinks_a, v_wo_a, v_wqkv_b, v_wo_b, v_w_gate, v_w_up, v_w_down):
    given = dict(x=x, c=c, positions=positions, ada_w=ada_w, ada_b=ada_b, norm1_g=norm1_g, norm2_g=norm2_g, wqkv_a=wqkv_a, q_norm_a=q_norm_a, k_norm_a=k_norm_a, sinks_a=sinks_a, wo_a=wo_a, wqkv_b=wqkv_b, wo_b=wo_b, w_gate=w_gate, w_up=w_up, w_down=w_down, loss_target=loss_target, m_ada_w=m_ada_w, m_ada_b=m_ada_b, m_norm1_g=m_norm1_g, m_norm2_g=m_norm2_g, m_wqkv_a=m_wqkv_a, m_q_norm_a=m_q_norm_a, m_k_norm_a=m_k_norm_a, m_sinks_a=m_sinks_a, m_wo_a=m_wo_a, m_wqkv_b=m_wqkv_b, m_wo_b=m_wo_b, m_w_gate=m_w_gate, m_w_up=m_w_up, m_w_down=m_w_down, v_ada_w=v_ada_w, v_ada_b=v_ada_b, v_norm1_g=v_norm1_g, v_norm2_g=v_norm2_g, v_wqkv_a=v_wqkv_a, v_q_norm_a=v_q_norm_a, v_k_norm_a=v_k_norm_a, v_sinks_a=v_sinks_a, v_wo_a=v_wo_a, v_wqkv_b=v_wqkv_b, v_wo_b=v_wo_b, v_w_gate=v_w_gate, v_w_up=v_w_up, v_w_down=v_w_down)
    weights = {n: given[n] for n in TWIN_WEIGHTS}
    shared = {n: given[n] for n in SHARED_INPUTS}
    per_example = {n: given[n] for n in ['x', 'c', 'positions']}
    grad_fn = _jax.value_and_grad(_loss, argnums=(0, 1))

    def one_microbatch(ex, loss_target):
        ex = dict(ex)
        diff = ex.pop(TWIN_DIFF_INPUT)
        return grad_fn(weights, diff, {**shared, **ex}, loss_target)

    if N_MICROBATCH == 1:
        loss, (grad_w, grad_x) = one_microbatch(per_example, given["loss_target"])
    else:
        def body(carry, xs):
            loss_sum, grad_sum = carry
            l_k, (gw_k, gx_k) = one_microbatch(xs[0], xs[1])
            with _jax.named_scope("update"):
                return (loss_sum + l_k, _jax.tree.map(_jnp.add, grad_sum, gw_k)), gx_k

        init = (_jnp.zeros((), _jnp.float32), _jax.tree.map(_jnp.zeros_like, weights))
        (loss, grad_w), grad_x = _jax.lax.scan(body, init, (per_example, given["loss_target"]))
    with _jax.named_scope("update"):
        delta_w, new_m, new_v = {}, {}, {}
        for n in TWIN_WEIGHTS:
            delta_w[n], new_m[n], new_v[n] = _adamw(weights[n], grad_w[n], given["m_" + n], given["v_" + n])
    return (loss, grad_x, *[grad_w[n] for n in TWIN_WEIGHTS], *[delta_w[n] for n in TWIN_WEIGHTS],
            *[new_m[n] for n in TWIN_WEIGHTS], *[new_v[n] for n in TWIN_WEIGHTS])
```

```python
import jax
import jax.numpy as jnp
from jax import lax
from jax.experimental import pallas as pl
from jax.experimental.pallas import tpu as pltpu

F32 = jnp.float32
BF16 = jnp.bfloat16

DEPTH = 4
HEAD_DIM = 64
N_Q_A = 16
N_KV_A = 2
GROUP_A = N_Q_A // N_KV_A
N_H_B = 16
BLOCK = 128
ROT_DIM = HEAD_DIM // 4
ROPE_THETA = 500000.0
EPS = 1e-6
ATTN_SCALE = HEAD_DIM ** -0.5
NEG_BIG = -1e30

ADAM_LR = 0.001
ADAM_B1 = 0.9
ADAM_B2 = 0.999
ADAM_EPS = 1e-08
ADAM_WD = 0.01
ADAM_STEP = 10

N_DEV = 8
LANES = 128
PACK_COLS = 1024
VMEM_LIMIT_BYTES = 48 * 1024 * 1024
MESH = pl.DeviceIdType.MESH

_NT = (((1,), (1,)), ((), ()))
_TN = (((0,), (0,)), ((), ()))
_NN = (((1,), (0,)), ((), ()))


def _params(sem=None):
    return pltpu.CompilerParams(vmem_limit_bytes=VMEM_LIMIT_BYTES, dimension_semantics=sem)


def _pick(n, cap, mult):
    best = None
    for t in range(mult, min(n, cap) + 1, mult):
        if n % t == 0:
            best = t
    return n if best is None else best


def _all_gather8(x, name):
    def body(x_ref, out_ref, send_sems, recv_sems, local_sem):
        xi, yi, ci = lax.axis_index("x"), lax.axis_index("y"), lax.axis_index("c")
        me, sibling = (xi, yi, ci), (xi, yi, 1 - ci)
        chips = [(1 - xi, yi), (xi, 1 - yi), (1 - xi, 1 - yi)]

        def slab(px, py, pc):
            return out_ref.at[4 * px + 2 * py + pc]

        def copy(k, block, to, src=None):
            return pltpu.make_async_remote_copy(
                src_ref=slab(*block) if src is None else src, dst_ref=slab(*block),
                send_sem=send_sems.at[k], recv_sem=recv_sems.at[k], device_id=to, device_id_type=MESH)

        mine = pltpu.make_async_copy(x_ref, slab(*me), local_sem)
        mine.start()
        first = [copy(0, me, sibling, src=x_ref)]
        first += [copy(1 + j, me, (*chip, ci), src=x_ref) for j, chip in enumerate(chips)]
        for cp in first:
            cp.start()
        passed = [copy(4 + j, (*chip, ci), sibling) for j, chip in enumerate(chips)]
        for j, chip in enumerate(chips):
            copy(1 + j, (*chip, ci), me).wait_recv()
            passed[j].start()
        copy(0, sibling, me).wait_recv()
        for j, chip in enumerate(chips):
            copy(4 + j, (*chip, 1 - ci), me).wait_recv()
        for cp in first + passed:
            cp.wait_send()
        mine.wait()

    return pl.pallas_call(
        body, name=name,
        out_shape=jax.ShapeDtypeStruct((N_DEV,) + x.shape, x.dtype),
        in_specs=[pl.BlockSpec(memory_space=pl.ANY)],
        out_specs=pl.BlockSpec(memory_space=pl.ANY),
        scratch_shapes=[pltpu.SemaphoreType.DMA((7,)), pltpu.SemaphoreType.DMA((7,)), pltpu.SemaphoreType.DMA],
    )(x)


def _exchange(x, group, name):
    n_peers = 1 if group == "c" else 3

    def body(x_ref, out_ref, send_sems, recv_sems, local_sem):
        xi, yi, ci = lax.axis_index("x"), lax.axis_index("y"), lax.axis_index("c")
        if group == "c":
            me = ci
            peers = [(1 - ci, (xi, yi, 1 - ci))]
        else:
            me = 2 * xi + yi
            peers = [(2 * (1 - xi) + yi, (1 - xi, yi, ci)),
                     (2 * xi + (1 - yi), (xi, 1 - yi, ci)),
                     (2 * (1 - xi) + (1 - yi), (1 - xi, 1 - yi, ci))]
        local = pltpu.make_async_copy(x_ref.at[me], out_ref.at[me], local_sem)
        local.start()
        copies = []
        for k, (p, dev) in enumerate(peers):
            cp = pltpu.make_async_remote_copy(
                src_ref=x_ref.at[p], dst_ref=out_ref.at[me],
                send_sem=send_sems.at[k], recv_sem=recv_sems.at[k], device_id=dev, device_id_type=MESH)
            cp.start()
            copies.append(cp)
        for cp in copies:
            cp.wait()
        local.wait()

    return pl.pallas_call(
        body, name=name,
        out_shape=jax.ShapeDtypeStruct(x.shape, x.dtype),
        in_specs=[pl.BlockSpec(memory_space=pl.ANY)],
        out_specs=pl.BlockSpec(memory_space=pl.ANY),
        scratch_shapes=[pltpu.SemaphoreType.DMA((n_peers,)), pltpu.SemaphoreType.DMA((n_peers,)),
                        pltpu.SemaphoreType.DMA],
    )(x)


def _sum_leading(x, name):
    P, R, C = x.shape
    tr = _pick(R, max(8, (1 << 19) // (C * P)), 8)

    def body(x_ref, o_ref):
        acc = x_ref[0]
        for p in range(1, P):
            acc = acc + x_ref[p]
        o_ref[...] = acc

    return pl.pallas_call(
        body, name=name, grid=(R // tr,),
        in_specs=[pl.BlockSpec((P, tr, C), lambda r: (0, r, 0))],
        out_specs=pl.BlockSpec((tr, C), lambda r: (r, 0)),
        out_shape=jax.ShapeDtypeStruct((R, C), F32),
        compiler_params=_params(("arbitrary",)),
    )(x)


def _matmul(a, b, mode, out_dtype, name):
    if mode == "nn":
        (M, K), N = a.shape, b.shape[1]
    elif mode == "nt":
        (M, K), N = a.shape, b.shape[0]
    else:
        (K, M), N = a.shape, b.shape[1]
    tm = _pick(M, 1024 if mode != "tn" else 1536, 128)
    tn = _pick(N, 1536, 128)
    tk = _pick(K, 512, 128)
    nk = K // tk
    dims = {"nn": _NN, "nt": _NT, "tn": _TN}[mode]

    def body(a_ref, b_ref, o_ref, acc_ref):
        k = pl.program_id(2)

        @pl.when(k == 0)
        def _():
            acc_ref[...] = jnp.zeros_like(acc_ref)

        acc_ref[...] += lax.dot_general(a_ref[...].astype(BF16), b_ref[...].astype(BF16), dims,
                                        preferred_element_type=F32)

        @pl.when(k == nk - 1)
        def _():
            o_ref[...] = acc_ref[...].astype(o_ref.dtype)

    if mode == "tn":
        a_spec = pl.BlockSpec((tk, tm), lambda i, j, k: (k, i))
    else:
        a_spec = pl.BlockSpec((tm, tk), lambda i, j, k: (i, k))
    if mode == "nt":
        b_spec = pl.BlockSpec((tn, tk), lambda i, j, k: (j, k))
    else:
        b_spec = pl.BlockSpec((tk, tn), lambda i, j, k: (k, j))
    return pl.pallas_call(
        body, name=name, grid=(M // tm, N // tn, nk),
        in_specs=[a_spec, b_spec],
        out_specs=pl.BlockSpec((tm, tn), lambda i, j, k: (i, j)),
        out_shape=jax.ShapeDtypeStruct((M, N), out_dtype),
        scratch_shapes=[pltpu.VMEM((tm, tn), F32)],
        compiler_params=_params(("parallel", "parallel", "arbitrary")),
    )(a, b)


def _row_tile(S):
    return _pick(S, 512, 8)


def _norm_mod_fwd(x, gain, sc, sh):
    NB, S, D = x.shape
    tr = _row_tile(S)

    def body(x_ref, g_ref, sc_ref, sh_ref, h_ref):
        xv = x_ref[...]
        ms = jnp.mean(xv * xv, axis=-1, keepdims=True)
        n = xv * lax.rsqrt(ms + EPS) * g_ref[...]
        h_ref[...] = (n * (1.0 + sc_ref[...]) + sh_ref[...]).astype(BF16)

    tok = pl.BlockSpec((None, tr, D), lambda b, r: (b, r, 0))
    per_ex = pl.BlockSpec((None, 1, D), lambda b, r: (b, 0, 0))
    return pl.pallas_call(
        body, name="norm_mod_fwd", grid=(NB, S // tr),
        in_specs=[tok, pl.BlockSpec((1, D), lambda b, r: (0, 0)), per_ex, per_ex],
        out_specs=tok, out_shape=jax.ShapeDtypeStruct((NB, S, D), BF16),
        compiler_params=_params(("parallel", "parallel")),
    )(x, gain, sc, sh)


def _norm_mod_bwd(x, gain, sc, dh, dres):
    NB, S, D = x.shape
    tr = _row_tile(S)

    def body(x_ref, g_ref, sc_ref, dh_ref, dres_ref, dx_ref, dsh_ref, dsc_ref, dg_ref):
        b, r = pl.program_id(0), pl.program_id(1)

        @pl.when(r == 0)
        def _():
            dsh_ref[...] = jnp.zeros_like(dsh_ref)
            dsc_ref[...] = jnp.zeros_like(dsc_ref)

        @pl.when((r == 0) & (b == 0))
        def _():
            dg_ref[...] = jnp.zeros_like(dg_ref)

        xv = x_ref[...]
        rstd = lax.rsqrt(jnp.mean(xv * xv, axis=-1, keepdims=True) + EPS)
        xh = xv * rstd
        g = g_ref[...]
        dh = dh_ref[...]
        dsh_ref[...] += jnp.sum(dh, axis=0, keepdims=True)
        dsc_ref[...] += jnp.sum(dh * (xh * g), axis=0, keepdims=True)
        dn = dh * (1.0 + sc_ref[...])
        dg_ref[...] += jnp.sum(dn * xh, axis=0, keepdims=True)
        dxh = dn * g
        proj = jnp.mean(dxh * xh, axis=-1, keepdims=True)
        dx_ref[...] = rstd * (dxh - xh * proj) + dres_ref[...]

    tok = pl.BlockSpec((None, tr, D), lambda b, r: (b, r, 0))
    per_ex = pl.BlockSpec((None, 1, D), lambda b, r: (b, 0, 0))
    row = pl.BlockSpec((1, D), lambda b, r: (0, 0))
    return pl.pallas_call(
        body, name="norm_mod_bwd", grid=(NB, S // tr),
        in_specs=[tok, row, per_ex, tok, tok],
        out_specs=[tok, per_ex, per_ex, row],
        out_shape=[jax.ShapeDtypeStruct((NB, S, D), F32), jax.ShapeDtypeStruct((NB, 1, D), F32),
                   jax.ShapeDtypeStruct((NB, 1, D), F32), jax.ShapeDtypeStruct((1, D), F32)],
        compiler_params=_params(("arbitrary", "arbitrary")),
    )(x, gain, sc, dh, dres)


def _gate_res(x, y, g):
    NB, S, D = x.shape
    tr = _row_tile(S)

    def body(x_ref, y_ref, g_ref, o_ref):
        o_ref[...] = x_ref[...] + g_ref[...] * y_ref[...]

    tok = pl.BlockSpec((None, tr, D), lambda b, r: (b, r, 0))
    per_ex = pl.BlockSpec((None, 1, D), lambda b, r: (b, 0, 0))
    return pl.pallas_call(
        body, name="gate_res", grid=(NB, S // tr), in_specs=[tok, tok, per_ex], out_specs=tok,
        out_shape=jax.ShapeDtypeStruct((NB, S, D), F32),
        compiler_params=_params(("parallel", "parallel")),
    )(x, y, g)


def _gate_res_bwd(dxo, y, g):
    NB, S, D = dxo.shape
    tr = _row_tile(S)

    def body(d_ref, y_ref, g_ref, dy_ref, dg_ref):
        @pl.when(pl.program_id(1) == 0)
        def _():
            dg_ref[...] = jnp.zeros_like(dg_ref)

        d = d_ref[...]
        dy_ref[...] = (d * g_ref[...]).astype(BF16)
        dg_ref[...] += jnp.sum(d * y_ref[...], axis=0, keepdims=True)

    tok = pl.BlockSpec((None, tr, D), lambda b, r: (b, r, 0))
    per_ex = pl.BlockSpec((None, 1, D), lambda b, r: (b, 0, 0))
    return pl.pallas_call(
        body, name="gate_res_bwd", grid=(NB, S // tr), in_specs=[tok, tok, per_ex], out_specs=[tok, per_ex],
        out_shape=[jax.ShapeDtypeStruct((NB, S, D), BF16), jax.ShapeDtypeStruct((NB, 1, D), F32)],
        compiler_params=_params(("arbitrary", "arbitrary")),
    )(dxo, y, g)


def _sigmoid(v):
    return 1.0 / (1.0 + jnp.exp(-v))


def _swiglu_fwd(gu):
    T, F2 = gu.shape
    F = F2 // 2
    tf = _pick(F, 1536, 128)
    nf = F // tf
    tr = _pick(T, 256, 8)

    def body(g_ref, u_ref, o_ref):
        g = g_ref[...]
        o_ref[...] = (g * _sigmoid(g) * u_ref[...]).astype(BF16)

    return pl.pallas_call(
        body, name="swiglu_fwd", grid=(T // tr, nf),
        in_specs=[pl.BlockSpec((tr, tf), lambda i, j: (i, j)), pl.BlockSpec((tr, tf), lambda i, j: (i, j + nf))],
        out_specs=pl.BlockSpec((tr, tf), lambda i, j: (i, j)),
        out_shape=jax.ShapeDtypeStruct((T, F), BF16),
        compiler_params=_params(("parallel", "parallel")),
    )(gu, gu)


def _swiglu_bwd(gu, dact):
    T, F2 = gu.shape
    F = F2 // 2
    tf = _pick(F, 1536, 128)
    nf = F // tf
    tr = _pick(T, 256, 8)

    def body(g_ref, u_ref, d_ref, o_ref):
        j = pl.program_id(1)
        g, u, d = g_ref[...], u_ref[...], d_ref[...]
        s = _sigmoid(g)
        dg = d * u * (s * (1.0 + g * (1.0 - s)))
        du = d * (g * s)
        o_ref[...] = jnp.where(j < nf, dg, du).astype(BF16)

    return pl.pallas_call(
        body, name="swiglu_bwd", grid=(T // tr, 2 * nf),
        in_specs=[pl.BlockSpec((tr, tf), lambda i, j: (i, j % nf)),
                  pl.BlockSpec((tr, tf), lambda i, j: (i, nf + j % nf)),
                  pl.BlockSpec((tr, tf), lambda i, j: (i, j % nf))],
        out_specs=pl.BlockSpec((tr, tf), lambda i, j: (i, j)),
        out_shape=jax.ShapeDtypeStruct((T, F2), BF16),
        compiler_params=_params(("parallel", "parallel")),
    )(gu, gu, dact)


def _loss_fwd_bwd(y, target):
    NB, S, D = y.shape
    tr = _row_tile(S)

    def body(y_ref, t_ref, l_ref, d_ref):
        @pl.when((pl.program_id(0) == 0) & (pl.program_id(1) == 0))
        def _():
            l_ref[...] = jnp.zeros_like(l_ref)

        e = y_ref[...] - t_ref[...]
        d_ref[...] = e / D
        l_ref[...] += 0.5 * jnp.sum(jnp.mean(e * e, axis=-1, keepdims=True), axis=0, keepdims=True)

    tok = pl.BlockSpec((None, tr, D), lambda b, r: (b, r, 0))
    return pl.pallas_call(
        body, name="loss", grid=(NB, S // tr), in_specs=[tok, tok],
        out_specs=[pl.BlockSpec((1, 1), lambda b, r: (0, 0)), tok],
        out_shape=[jax.ShapeDtypeStruct((1, 1), F32), jax.ShapeDtypeStruct((NB, S, D), F32)],
        compiler_params=_params(("arbitrary", "arbitrary")),
    )(y, target)


def _half_sums(v, lo):
    sa = jnp.sum(jnp.where(lo, v, 0.0), axis=-1, keepdims=True)
    sb = jnp.sum(jnp.where(lo, 0.0, v), axis=-1, keepdims=True)
    return jnp.where(lo, sa, sb)


def _rope_swap(v, lane64):
    up = pltpu.roll(v, LANES - ROT_DIM // 2, 2)
    down = pltpu.roll(v, ROT_DIM // 2, 2)
    return jnp.where(lane64 < ROT_DIM // 2, up, jnp.where(lane64 < ROT_DIM, down, 0.0))


def _qk_prep_fwd(qkv3, tab_c, tab_s, gains):
    T = qkv3.shape[0]
    R = qkv3.shape[1]
    tt = _pick(T, 256, 8)

    def body(x_ref, c_ref, s_ref, g_ref, o_ref):
        xv = x_ref[...]
        lane = lax.broadcasted_iota(jnp.int32, xv.shape, 2)
        row = lax.broadcasted_iota(jnp.int32, xv.shape, 1)
        lo = lane < HEAD_DIM
        rstd = lax.rsqrt(_half_sums(xv * xv, lo) / HEAD_DIM + EPS)
        yn = xv * rstd * g_ref[...]
        roped = yn * c_ref[...] + _rope_swap(yn, lane & (HEAD_DIM - 1)) * s_ref[...]
        o_ref[...] = jnp.where(row == R - 1, xv, roped).astype(BF16)

    tok = pl.BlockSpec((tt, R, LANES), lambda t: (t, 0, 0))
    tab = pl.BlockSpec((tt, 1, LANES), lambda t: (t, 0, 0))
    return pl.pallas_call(
        body, name="qk_prep_fwd", grid=(T // tt,),
        in_specs=[tok, tab, tab, pl.BlockSpec((R, LANES), lambda t: (0, 0))],
        out_specs=tok, out_shape=jax.ShapeDtypeStruct(qkv3.shape, BF16),
        compiler_params=_params(("parallel",)),
    )(qkv3, tab_c, tab_s, gains)


def _qk_prep_bwd(qkv3, d3, tab_c, tab_s, gains):
    T = qkv3.shape[0]
    R = qkv3.shape[1]
    tt = _pick(T, 256, 8)

    def body(x_ref, d_ref, c_ref, s_ref, g_ref, o_ref, dg_ref):
        @pl.when(pl.program_id(0) == 0)
        def _():
            dg_ref[...] = jnp.zeros_like(dg_ref)

        xv, d = x_ref[...], d_ref[...]
        lane = lax.broadcasted_iota(jnp.int32, xv.shape, 2)
        row = lax.broadcasted_iota(jnp.int32, xv.shape, 1)
        lo = lane < HEAD_DIM
        rstd = lax.rsqrt(_half_sums(xv * xv, lo) / HEAD_DIM + EPS)
        xh = xv * rstd
        dyn = d * c_ref[...] + _rope_swap(d * s_ref[...], lane & (HEAD_DIM - 1))
        dg_ref[...] += jnp.sum(dyn * xh, axis=0)
        dxh = dyn * g_ref[...]
        proj = _half_sums(dxh * xh, lo) / HEAD_DIM
        dx = rstd * (dxh - xh * proj)
        o_ref[...] = jnp.where(row == R - 1, d, dx).astype(BF16)

    tok = pl.BlockSpec((tt, R, LANES), lambda t: (t, 0, 0))
    tab = pl.BlockSpec((tt, 1, LANES), lambda t: (t, 0, 0))
    gsp = pl.BlockSpec((R, LANES), lambda t: (0, 0))
    return pl.pallas_call(
        body, name="qk_prep_bwd", grid=(T // tt,),
        in_specs=[tok, tok, tab, tab, gsp], out_specs=[tok, gsp],
        out_shape=[jax.ShapeDtypeStruct(qkv3.shape, BF16), jax.ShapeDtypeStruct((R, LANES), F32)],
        compiler_params=_params(("arbitrary",)),
    )(qkv3, d3, tab_c, tab_s, gains)


def _band_mask(i):
    r = lax.broadcasted_iota(jnp.int32, (BLOCK, 2 * BLOCK), 0)
    c = lax.broadcasted_iota(jnp.int32, (BLOCK, 2 * BLOCK), 1)
    rel = r + BLOCK - c
    return (rel >= 0) & (rel < BLOCK) & ((c >= BLOCK) | (i > 0))


def _swa_probs(qg, k2, valid, sink):
    s = lax.dot_general(qg, k2, _NT, preferred_element_type=F32) * ATTN_SCALE
    s = jnp.where(valid, s, NEG_BIG)
    m = jnp.maximum(jnp.max(s, axis=1, keepdims=True), sink)
    p = jnp.exp(s - m)
    ps = jnp.exp(sink - m)
    denom = jnp.sum(p, axis=1, keepdims=True) + ps
    return p / denom, ps / denom


def _swa_specs(S):
    qs = pl.BlockSpec((None, None, GROUP_A, BLOCK, HEAD_DIM), lambda b, h, i: (b, h, 0, i, 0))
    prev = pl.BlockSpec((None, None, BLOCK, HEAD_DIM), lambda b, h, i: (b, h, jnp.maximum(i - 1, 0), 0))
    cur = pl.BlockSpec((None, None, BLOCK, HEAD_DIM), lambda b, h, i: (b, h, i, 0))
    return qs, prev, cur


def _attn_a_fwd(q, k, v, sinks):
    NB, _, _, S, _ = q.shape
    qs, prev, cur = _swa_specs(S)

    def body(q_ref, kp_ref, kc_ref, vp_ref, vc_ref, sink_ref, o_ref):
        h, i = pl.program_id(1), pl.program_id(2)
        k2 = jnp.concatenate([kp_ref[...], kc_ref[...]], axis=0)
        v2 = jnp.concatenate([vp_ref[...], vc_ref[...]], axis=0)
        valid = _band_mask(i)
        for g in range(GROUP_A):
            pn, _ = _swa_probs(q_ref[g], k2, valid, sink_ref[h * GROUP_A + g])
            o_ref[g] = jnp.dot(pn.astype(BF16), v2, preferred_element_type=F32).astype(BF16)

    return pl.pallas_call(
        body, name="attn_a_fwd", grid=(NB, N_KV_A, S // BLOCK),
        in_specs=[qs, prev, cur, prev, cur, pl.BlockSpec(memory_space=pltpu.SMEM)],
        out_specs=qs, out_shape=jax.ShapeDtypeStruct(q.shape, BF16),
        compiler_params=_params(("parallel", "parallel", "arbitrary")),
    )(q, k, k, v, v, sinks)


def _attn_a_bwd(q, k, v, do, sinks):
    NB, _, _, S, _ = q.shape
    qs, prev, cur = _swa_specs(S)
    full = pl.BlockSpec((None, None, S, HEAD_DIM), lambda b, h, i: (b, h, 0, 0))
    sink_out = pl.BlockSpec((None, None, GROUP_A, LANES), lambda b, h, i: (b, h, 0, 0))

    def body(q_ref, do_ref, kp_ref, kc_ref, vp_ref, vc_ref, sink_ref, dq_ref, dk_ref, dv_ref, ds_ref):
        h, i = pl.program_id(1), pl.program_id(2)

        @pl.when(i == 0)
        def _():
            dk_ref[...] = jnp.zeros_like(dk_ref)
            dv_ref[...] = jnp.zeros_like(dv_ref)
            ds_ref[...] = jnp.zeros_like(ds_ref)

        k2 = jnp.concatenate([kp_ref[...], kc_ref[...]], axis=0)
        v2 = jnp.concatenate([vp_ref[...], vc_ref[...]], axis=0)
        valid = _band_mask(i)
        dk2 = jnp.zeros((2 * BLOCK, HEAD_DIM), F32)
        dv2 = jnp.zeros((2 * BLOCK, HEAD_DIM), F32)
        for g in range(GROUP_A):
            qg, dog = q_ref[g], do_ref[g]
            pn, psink = _swa_probs(qg, k2, valid, sink_ref[h * GROUP_A + g])
            dp = lax.dot_general(dog, v2, _NT, preferred_element_type=F32)
            delta = jnp.sum(pn * dp, axis=1, keepdims=True)
            dsb = (pn * (dp - delta) * ATTN_SCALE).astype(BF16)
            dq_ref[g] = jnp.dot(dsb, k2, preferred_element_type=F32)
            dk2 = dk2 + lax.dot_general(dsb, qg, _TN, preferred_element_type=F32)
            dv2 = dv2 + lax.dot_general(pn.astype(BF16), dog, _TN, preferred_element_type=F32)
            dsink = -jnp.sum(psink * delta, axis=0, keepdims=True)
            ds_ref[g:g + 1, :] += jnp.broadcast_to(dsink, (1, LANES))

        @pl.when(i > 0)
        def _():
            start = pl.multiple_of((i - 1) * BLOCK, BLOCK)
            dk_ref[pl.ds(start, 2 * BLOCK), :] += dk2
            dv_ref[pl.ds(start, 2 * BLOCK), :] += dv2

        @pl.when(i == 0)
        def _():
            dk_ref[0:BLOCK, :] += dk2[BLOCK:, :]
            dv_ref[0:BLOCK, :] += dv2[BLOCK:, :]

    return pl.pallas_call(
        body, name="attn_a_bwd", grid=(NB, N_KV_A, S // BLOCK),
        in_specs=[qs, qs, prev, cur, prev, cur, pl.BlockSpec(memory_space=pltpu.SMEM)],
        out_specs=[qs, full, full, sink_out],
        out_shape=[jax.ShapeDtypeStruct(q.shape, F32), jax.ShapeDtypeStruct(k.shape, F32),
                   jax.ShapeDtypeStruct(k.shape, F32), jax.ShapeDtypeStruct((NB, N_KV_A, GROUP_A, LANES), F32)],
        compiler_params=_params(("parallel", "parallel", "arbitrary")),
    )(q, do, k, k, v, v, sinks)


def _cumsum_mats():
    src = lax.broadcasted_iota(jnp.int32, (2 * BLOCK, 2 * BLOCK), 0) % BLOCK
    dst = lax.broadcasted_iota(jnp.int32, (2 * BLOCK, 2 * BLOCK), 1)
    ones = dst >= BLOCK
    rev = ((src > dst) | ones).astype(BF16)
    fwd = ((src < dst) | ones).astype(BF16)
    return rev, fwd


def _cumsum_mxu(v, mat):
    hi = v.astype(BF16)
    lo = (v - hi.astype(F32)).astype(BF16)
    r = jnp.dot(jnp.concatenate([hi, lo], axis=1), mat, preferred_element_type=F32)
    return r[:, :BLOCK], r[:, BLOCK:]


def _sb_logs(qv, kj):
    z = lax.dot_general(qv, kj, _NT, preferred_element_type=F32) * ATTN_SCALE
    sp = jnp.log1p(jnp.exp(-jnp.abs(z)))
    return jnp.minimum(z, 0.0) - sp, -(jnp.maximum(z, 0.0) + sp)


def _strict_mask():
    r = lax.broadcasted_iota(jnp.int32, (BLOCK, BLOCK), 0)
    c = lax.broadcasted_iota(jnp.int32, (BLOCK, BLOCK), 1)
    return c < r


def _tile(ref, j):
    return ref[pl.ds(pl.multiple_of(j * BLOCK, BLOCK), BLOCK), :]


def _attn_b_fwd(q, k, v, rev):
    NB, H, S, _ = q.shape
    blk = pl.BlockSpec((None, None, BLOCK, HEAD_DIM), lambda b, h, i: (b, h, i, 0))
    full = pl.BlockSpec((None, None, S, HEAD_DIM), lambda b, h, i: (b, h, 0, 0))
    mat = pl.BlockSpec((2 * BLOCK, 2 * BLOCK), lambda b, h, i: (0, 0))

    def body(q_ref, k_ref, v_ref, rev_ref, o_ref):
        i = pl.program_id(2)
        qv = q_ref[...]
        rv = rev_ref[...]
        mask = _strict_mask()
        lb, lm = _sb_logs(qv, _tile(k_ref, i))
        after, rs = _cumsum_mxu(jnp.where(mask, lm, 0.0), rv)
        a = jnp.where(mask, jnp.exp(lb + after), 0.0)
        acc = jnp.dot(a.astype(BF16), _tile(v_ref, i), preferred_element_type=F32)

        def step(jj, st):
            carry, acc = st
            j = i - 1 - jj
            lb, lm = _sb_logs(qv, _tile(k_ref, j))
            after, rs = _cumsum_mxu(lm, rv)
            a = jnp.exp(lb + after + carry)
            acc = acc + jnp.dot(a.astype(BF16), _tile(v_ref, j), preferred_element_type=F32)
            return carry + rs, acc

        _, acc = lax.fori_loop(0, i, step, (rs, acc))
        o_ref[...] = acc.astype(BF16)

    return pl.pallas_call(
        body, name="attn_b_fwd", grid=(NB, H, S // BLOCK),
        in_specs=[blk, full, full, mat], out_specs=blk,
        out_shape=jax.ShapeDtypeStruct(q.shape, BF16),
        compiler_params=_params(("parallel", "parallel", "arbitrary")),
    )(q, k, v, rev)


def _attn_b_bwd(q, k, v, do, rev, fwd):
    NB, H, S, _ = q.shape
    nj = S // BLOCK
    blk = pl.BlockSpec((None, None, BLOCK, HEAD_DIM), lambda b, h, i: (b, h, i, 0))
    full = pl.BlockSpec((None, None, S, HEAD_DIM), lambda b, h, i: (b, h, 0, 0))
    mat = pl.BlockSpec((2 * BLOCK, 2 * BLOCK), lambda b, h, i: (0, 0))

    def body(q_ref, do_ref, k_ref, v_ref, rev_ref, fwd_ref, dq_ref, dk_ref, dv_ref, sig_s, a_s, e_s):
        i = pl.program_id(2)

        @pl.when(i == 0)
        def _():
            dk_ref[...] = jnp.zeros_like(dk_ref)
            dv_ref[...] = jnp.zeros_like(dv_ref)

        qv, dov = q_ref[...], do_ref[...]
        rv, fw = rev_ref[...], fwd_ref[...]
        mask = _strict_mask()

        def stash(j, lb, a):
            da = lax.dot_general(dov, _tile(v_ref, j), _NT, preferred_element_type=F32)
            sig_s[j] = jnp.exp(lb)
            a_s[j] = a
            e_s[j] = da * a

        lb, lm = _sb_logs(qv, _tile(k_ref, i))
        after, rs = _cumsum_mxu(jnp.where(mask, lm, 0.0), rv)
        stash(i, lb, jnp.where(mask, jnp.exp(lb + after), 0.0))

        def sweep1(jj, carry):
            j = i - 1 - jj
            lb, lm = _sb_logs(qv, _tile(k_ref, j))
            after, rs = _cumsum_mxu(lm, rv)
            stash(j, lb, jnp.exp(lb + after + carry))
            return carry + rs

        lax.fori_loop(0, i, sweep1, rs)

        def grads(j, prefix, dq, diagonal):
            e, a, sg = e_s[j], a_s[j], sig_s[j]
            e_before, rs = _cumsum_mxu(e, fw)
            dz = (e * (1.0 - sg) - (e_before + prefix) * sg) * ATTN_SCALE
            if diagonal:
                dz = jnp.where(mask, dz, 0.0)
            dzb = dz.astype(BF16)
            rows = pl.ds(pl.multiple_of(j * BLOCK, BLOCK), BLOCK)
            dk_ref[rows, :] += lax.dot_general(dzb, qv, _TN, preferred_element_type=F32)
            dv_ref[rows, :] += lax.dot_general(a.astype(BF16), dov, _TN, preferred_element_type=F32)
            return prefix + rs, dq + jnp.dot(dzb, _tile(k_ref, j), preferred_element_type=F32)

        init = (jnp.zeros((BLOCK, BLOCK), F32), jnp.zeros((BLOCK, HEAD_DIM), F32))
        prefix, dq = lax.fori_loop(0, i, lambda j, st: grads(j, st[0], st[1], False), init)
        _, dq = grads(i, prefix, dq, True)
        dq_ref[...] = dq

    tile_stash = pltpu.VMEM((nj, BLOCK, BLOCK), F32)
    return pl.pallas_call(
        body, name="attn_b_bwd", grid=(NB, H, nj),
        in_specs=[blk, blk, full, full, mat, mat], out_specs=[blk, full, full],
        out_shape=[jax.ShapeDtypeStruct(q.shape, F32)] * 3,
        scratch_shapes=[tile_stash, tile_stash, tile_stash],
        compiler_params=_params(("parallel", "parallel", "arbitrary")),
    )(q, do, k, v, rev, fwd)


def _ada_fwd(c_all, w, b):
    L, D, N = w.shape
    B = c_all.shape[0]

    def body(c_ref, w_ref, b_ref, o_ref):
        cv = c_ref[...]
        cond = (cv * _sigmoid(cv)).astype(BF16)
        o_ref[...] = jnp.dot(cond, w_ref[...].astype(BF16), preferred_element_type=F32) + b_ref[...]

    return pl.pallas_call(
        body, name="ada_fwd", grid=(L,),
        in_specs=[pl.BlockSpec((B, D), lambda l: (0, 0)), pl.BlockSpec((None, D, N), lambda l: (l, 0, 0)),
                  pl.BlockSpec((None, 1, N), lambda l: (l, 0, 0))],
        out_specs=pl.BlockSpec((None, B, N), lambda l: (l, 0, 0)),
        out_shape=jax.ShapeDtypeStruct((L, B, N), F32),
        compiler_params=_params(("parallel",)),
    )(c_all, w, b)


def _ada_bwd(c_all, dmod_all, dmod_shard):
    L, B, N = dmod_shard.shape
    D = c_all.shape[1]
    N_all = dmod_all.shape[2]

    def body(c_ref, da_ref, ds_ref, gw_ref, gb_ref):
        cv = c_ref[...]
        cond = (cv * _sigmoid(cv)).astype(BF16)
        gw_ref[...] = lax.dot_general(cond, ds_ref[...].astype(BF16), _TN, preferred_element_type=F32)
        gb_ref[...] = jnp.sum(da_ref[...], axis=0, keepdims=True)

    return pl.pallas_call(
        body, name="ada_bwd", grid=(L,),
        in_specs=[pl.BlockSpec((B, D), lambda l: (0, 0)), pl.BlockSpec((None, B, N_all), lambda l: (l, 0, 0)),
                  pl.BlockSpec((None, B, N), lambda l: (l, 0, 0))],
        out_specs=[pl.BlockSpec((None, D, N), lambda l: (l, 0, 0)), pl.BlockSpec((None, 1, N_all), lambda l: (l, 0, 0))],
        out_shape=[jax.ShapeDtypeStruct((L, D, N), F32), jax.ShapeDtypeStruct((L, 1, N_all), F32)],
        compiler_params=_params(("parallel",)),
    )(c_all, dmod_all, dmod_shard)


def _adamw(w, g, m, v, name):
    shape = w.shape
    C = shape[-1]
    R = w.size // C
    tr = _pick(R, max(8, (1 << 18) // C), 8)
    c1 = 1.0 - ADAM_B1 ** ADAM_STEP
    c2 = 1.0 - ADAM_B2 ** ADAM_STEP

    def body(w_ref, g_ref, m_ref, v_ref, d_ref, nm_ref, nv_ref):
        gv = g_ref[...]
        nm = ADAM_B1 * m_ref[...] + (1.0 - ADAM_B1) * gv
        nv = ADAM_B2 * v_ref[...] + (1.0 - ADAM_B2) * (gv * gv)
        d_ref[...] = -ADAM_LR * ((nm / c1) / (jnp.sqrt(nv / c2) + ADAM_EPS) + ADAM_WD * w_ref[...])
        nm_ref[...] = nm
        nv_ref[...] = nv

    spec = pl.BlockSpec((tr, C), lambda r: (r, 0))
    out = pl.pallas_call(
        body, name=name, grid=(R // tr,), in_specs=[spec] * 4, out_specs=[spec] * 3,
        out_shape=[jax.ShapeDtypeStruct((R, C), F32)] * 3,
        compiler_params=_params(("parallel",)),
    )(*[t.reshape(R, C) for t in (w, g, m, v)])
    return [t.reshape(shape) for t in out]


_SHARDED = (("wqkv_a", 2), ("wo_a", 1), ("wqkv_b", 2), ("wo_b", 1), ("w_gate", 2), ("w_up", 2), ("w_down", 1))


def _pack_full(full, axis):
    L, R, C = full.shape
    if axis == 2:
        t = full.reshape(2, L // 2, R, 4, C // 4).transpose(0, 3, 1, 2, 4)
    else:
        t = full.reshape(2, L // 2, 4, R // 4, C).transpose(0, 2, 1, 3, 4)
    return t.reshape(2, 4, -1)


def _unpack_full(flat, shard_shape, axis):
    L, Rs, Cs = shard_shape
    t = flat.reshape(4, 2, L // 2, Rs, Cs)
    if axis == 2:
        return t.transpose(1, 2, 3, 0, 4).reshape(L, Rs, 4 * Cs)
    return t.transpose(1, 2, 0, 3, 4).reshape(L, 4 * Rs, Cs)


def _rope_tables(positions):
    half = ROT_DIM // 2
    inv_freq = jnp.power(jnp.float32(ROPE_THETA), -jnp.arange(half, dtype=F32) * 2.0 / ROT_DIM)
    ang = positions.astype(F32).reshape(-1, 1) * inv_freq
    cos, sin = jnp.cos(ang), jnp.sin(ang)
    T = ang.shape[0]
    rest = HEAD_DIM - ROT_DIM
    c64 = jnp.concatenate([cos, cos, jnp.ones((T, rest), F32)], axis=1)
    s64 = jnp.concatenate([-sin, sin, jnp.zeros((T, rest), F32)], axis=1)
    return jnp.tile(c64, (1, 2)).reshape(T, 1, LANES), jnp.tile(s64, (1, 2)).reshape(T, 1, LANES)


def _gain_rows(q_gain, k_gain):
    q2 = jnp.tile(q_gain.reshape(1, HEAD_DIM), (GROUP_A, 2))
    k2 = jnp.tile(k_gain.reshape(1, HEAD_DIM), (1, 2))
    return jnp.concatenate([q2, k2, jnp.ones((1, LANES), F32)], axis=0)


def _local_step(x, positions, mod, norm1_g, norm2_g, q_norm_a, k_norm_a, sinks_a,
                wqkv_a, wo_a, wqkv_b, wo_b, wgu, wd, loss_target):
    NB, S, D = x.shape
    T = NB * S
    QA = N_Q_A * HEAD_DIM
    rows_a = wqkv_a.shape[2] // LANES
    tab_c, tab_s = _rope_tables(positions)
    rev, fwd = _cumsum_mats()

    saved = []
    xc = x
    for i in range(DEPTH):
        j = i // 2
        sh1, sc1, g1, sh2, sc2, g2 = [mod[i][:, k * D:(k + 1) * D].reshape(NB, 1, D) for k in range(6)]
        st = dict(x=xc, sc1=sc1, g1=g1, sc2=sc2, g2=g2)
        h = _norm_mod_fwd(xc, norm1_g[i:i + 1], sc1, sh1)
        st["h"] = h.reshape(T, D)
        if i % 2 == 0:
            qkv = _matmul(st["h"], wqkv_a[j], "nn", F32, "qkv_a")
            st["qkv3"] = qkv.reshape(T, rows_a, LANES)
            st["gains"] = _gain_rows(q_norm_a[j], k_norm_a[j])
            qkn = _qk_prep_fwd(st["qkv3"], tab_c, tab_s, st["gains"])
            st["q"] = qkn[:, :GROUP_A].reshape(NB, S, N_KV_A, GROUP_A, HEAD_DIM).transpose(0, 2, 3, 1, 4)
            st["k"] = qkn[:, GROUP_A].reshape(NB, S, N_KV_A, HEAD_DIM).transpose(0, 2, 1, 3)
            st["v"] = qkn[:, GROUP_A + 1].reshape(NB, S, N_KV_A, HEAD_DIM).transpose(0, 2, 1, 3)
            o = _attn_a_fwd(st["q"], st["k"], st["v"], sinks_a[j])
            st["o"] = o.transpose(0, 3, 1, 2, 4).reshape(T, QA)
            y = _matmul(st["o"], wo_a[j], "nn", F32, "wo_a")
        else:
            qkv = _matmul(st["h"], wqkv_b[j], "nn", BF16, "qkv_b")
            qkv5 = qkv.reshape(NB, S, 3, N_H_B, HEAD_DIM).transpose(2, 0, 3, 1, 4)
            st["q"], st["k"], st["v"] = qkv5[0], qkv5[1], qkv5[2]
            o = _attn_b_fwd(st["q"], st["k"], st["v"], rev)
            st["o"] = o.transpose(0, 2, 1, 3).reshape(T, N_H_B * HEAD_DIM)
            y = _matmul(st["o"], wo_b[j], "nn", F32, "wo_b")
        st["y"] = y.reshape(NB, S, D)
        x1 = _gate_res(xc, st["y"], g1)
        st["x1"] = x1
        h2 = _norm_mod_fwd(x1, norm2_g[i:i + 1], sc2, sh2)
        st["h2"] = h2.reshape(T, D)
        st["gu"] = _matmul(st["h2"], wgu[i], "nn", F32, "gate_up")
        st["act"] = _swiglu_fwd(st["gu"])
        st["m"] = _matmul(st["act"], wd[i], "nn", F32, "down").reshape(NB, S, D)
        xc = _gate_res(x1, st["m"], g2)
        saved.append(st)

    loss, dx = _loss_fwd_bwd(xc, loss_target)

    grads = {name: [None] * n for name, n in
             (("wqkv_a", 2), ("wo_a", 2), ("wqkv_b", 2), ("wo_b", 2), ("wgu", DEPTH), ("wd", DEPTH),
              ("norm1_g", DEPTH), ("norm2_g", DEPTH), ("q_norm_a", 2), ("k_norm_a", 2), ("sinks_a", 2))}
    dmod = [None] * DEPTH
    for i in reversed(range(DEPTH)):
        j = i // 2
        st = saved[i]
        dm, dg2 = _gate_res_bwd(dx, st["m"], st["g2"])
        dm = dm.reshape(T, D)
        dact = _matmul(dm, wd[i], "nt", F32, "d_act")
        grads["wd"][i] = _matmul(st["act"], dm, "tn", F32, "d_wd")
        dgu = _swiglu_bwd(st["gu"], dact)
        dh2 = _matmul(dgu, wgu[i], "nt", F32, "d_h2")
        grads["wgu"][i] = _matmul(st["h2"], dgu, "tn", F32, "d_wgu")
        dx1, dsh2, dsc2, grads["norm2_g"][i] = _norm_mod_bwd(
            st["x1"], norm2_g[i:i + 1], st["sc2"], dh2.reshape(NB, S, D), dx)
        dy, dg1 = _gate_res_bwd(dx1, st["y"], st["g1"])
        dy = dy.reshape(T, D)
        if i % 2 == 0:
            do = _matmul(dy, wo_a[j], "nt", BF16, "d_o_a")
            grads["wo_a"][j] = _matmul(st["o"], dy, "tn", F32, "d_wo_a")
            do5 = do.reshape(NB, S, N_KV_A, GROUP_A, HEAD_DIM).transpose(0, 2, 3, 1, 4)
            dq, dk, dv, dsink = _attn_a_bwd(st["q"], st["k"], st["v"], do5, sinks_a[j])
            d3 = jnp.concatenate([
                dq.transpose(0, 3, 1, 2, 4).reshape(T, GROUP_A, LANES),
                dk.transpose(0, 2, 1, 3).reshape(T, 1, LANES),
                dv.transpose(0, 2, 1, 3).reshape(T, 1, LANES)], axis=1)
            dqkv, dgain = _qk_prep_bwd(st["qkv3"], d3, tab_c, tab_s, st["gains"])
            dqkv = dqkv.reshape(T, rows_a * LANES)
            dh = _matmul(dqkv, wqkv_a[j], "nt", F32, "d_h_a")
            grads["wqkv_a"][j] = _matmul(st["h"], dqkv, "tn", F32, "d_wqkv_a")
            grads["q_norm_a"][j] = jnp.sum(dgain[:GROUP_A].reshape(2 * GROUP_A, HEAD_DIM), axis=0)
            grads["k_norm_a"][j] = jnp.sum(dgain[GROUP_A].reshape(2, HEAD_DIM), axis=0)
            grads["sinks_a"][j] = jnp.sum(dsink[..., 0], axis=0).reshape(N_Q_A)
        else:
            do = _matmul(dy, wo_b[j], "nt", BF16, "d_o_b")
            grads["wo_b"][j] = _matmul(st["o"], dy, "tn", F32, "d_wo_b")
            do4 = do.reshape(NB, S, N_H_B, HEAD_DIM).transpose(0, 2, 1, 3)
            dq, dk, dv = _attn_b_bwd(st["q"], st["k"], st["v"], do4, rev, fwd)
            dqkv = jnp.stack([dq, dk, dv]).transpose(1, 3, 0, 2, 4).reshape(T, 3 * N_H_B * HEAD_DIM).astype(BF16)
            dh = _matmul(dqkv, wqkv_b[j], "nt", F32, "d_h_b")
            grads["wqkv_b"][j] = _matmul(st["h"], dqkv, "tn", F32, "d_wqkv_b")
        dx, dsh1, dsc1, grads["norm1_g"][i] = _norm_mod_bwd(
            st["x"], norm1_g[i:i + 1], st["sc1"], dh.reshape(NB, S, D), dx1)
        dmod[i] = jnp.concatenate([dsh1, dsc1, dg1, dsh2, dsc2, dg2], axis=-1).reshape(NB, 6 * D)

    grads = {name: jnp.stack(parts) for name, parts in grads.items()}
    return loss, dx, grads, jnp.stack(dmod)


def _rows_of(flat, cols=PACK_COLS):
    n = flat.shape[0]
    pad = (-n) % (8 * cols)
    if pad:
        flat = jnp.concatenate([flat, jnp.zeros((pad,), flat.dtype)])
    return flat.reshape(-1, cols)


def kernel(x, c, positions, ada_w, ada_b, norm1_g, norm2_g, wqkv_a, q_norm_a, k_norm_a, sinks_a, wo_a, wqkv_b, wo_b, w_gate, w_up, w_down, loss_target, m_ada_w, m_ada_b, m_norm1_g, m_norm2_g, m_wqkv_a, m_q_norm_a, m_k_norm_a, m_sinks_a, m_wo_a, m_wqkv_b, m_wo_b, m_w_gate, m_w_up, m_w_down, v_ada_w, v_ada_b, v_norm1_g, v_norm2_g, v_wqkv_a, v_q_norm_a, v_k_norm_a, v_sinks_a, v_wo_a, v_wqkv_b, v_wo_b, v_w_gate, v_w_up, v_w_down):
    xi, yi, ci = lax.axis_index("x"), lax.axis_index("y"), lax.axis_index("c")
    dev = 4 * xi + 2 * yi + ci
    chip = 2 * xi + yi
    NB, S, D = x.shape
    B_all = N_DEV * NB
    L = ada_w.shape[0]
    n_mod = ada_w.shape[2] // 2

    c_all = _all_gather8(_rows_of(c.reshape(-1), LANES), "gather_c").reshape(N_DEV, -1)[:, :NB * D].reshape(B_all, D)
    ada_w_half = lax.dynamic_slice_in_dim(ada_w, ci * n_mod, n_mod, axis=2)
    ada_b_half = lax.dynamic_slice_in_dim(ada_b, dev * n_mod, n_mod, axis=1).reshape(L, 1, n_mod)
    mod_part = _ada_fwd(c_all, ada_w_half, ada_b_half)
    n_part = L * B_all * n_mod
    mod_all = _all_gather8(_rows_of(mod_part.reshape(-1)), "gather_mod").reshape(N_DEV, -1)[:, :n_part]
    mod_all = mod_all.reshape(N_DEV, L, B_all, n_mod).transpose(1, 2, 0, 3).reshape(L, B_all, N_DEV * n_mod)
    mod = lax.dynamic_slice_in_dim(mod_all, dev * NB, NB, axis=1)

    shards = dict(wqkv_a=wqkv_a, wo_a=wo_a, wqkv_b=wqkv_b, wo_b=wo_b, w_gate=w_gate, w_up=w_up, w_down=w_down)
    halves = []
    for name, _ in _SHARDED:
        w = shards[name]
        half = lax.dynamic_index_in_dim(w.reshape((2, w.shape[0] // 2) + w.shape[1:]), ci, 0, keepdims=False)
        halves.append(half.reshape(-1).astype(BF16))
    sizes = [h.shape[0] for h in halves]
    gathered = _all_gather8(_rows_of(jnp.concatenate(halves)), "gather_weights").reshape(N_DEV, -1)
    full = {}
    off = 0
    for (name, axis), n in zip(_SHARDED, sizes):
        full[name] = _unpack_full(gathered[:, off:off + n], shards[name].shape, axis)
        off += n
    wgu = jnp.concatenate([full["w_gate"], full["w_up"]], axis=2)

    loss, grad_x, g, dmod = _local_step(
        x, positions, mod, norm1_g, norm2_g, q_norm_a, k_norm_a, sinks_a,
        full["wqkv_a"], full["wo_a"], full["wqkv_b"], full["wo_b"], wgu, full["w_down"], loss_target)

    F = w_down.shape[1] * 4
    g_full = dict(wqkv_a=g["wqkv_a"], wo_a=g["wo_a"], wqkv_b=g["wqkv_b"], wo_b=g["wo_b"],
                  w_gate=g["wgu"][:, :, :F], w_up=g["wgu"][:, :, F:], w_down=g["wd"])
    packed = jnp.concatenate([_pack_full(g_full[name], axis) for name, axis in _SHARDED], axis=2)
    n = packed.shape[2]
    rows = n // PACK_COLS
    from_cores = _exchange(packed.reshape(2, 4 * rows, PACK_COLS), "c", "rs_cores")
    chip_part = _sum_leading(from_cores, "rs_add_cores").reshape(4, rows, PACK_COLS)
    from_chips = _exchange(chip_part, "xy", "rs_chips")
    mine = _sum_leading(from_chips, "rs_add_chips")
    halves_g = _exchange(jnp.stack([mine, mine]), "c", "rs_halves").reshape(2, n)
    grad = {}
    off = 0
    for (name, _), sz in zip(_SHARDED, sizes):
        grad[name] = halves_g[:, off:off + sz].reshape(shards[name].shape)
        off += sz

    small_names = ("norm1_g", "norm2_g", "q_norm_a", "k_norm_a", "sinks_a")
    small = [dmod.reshape(-1)] + [g[name].reshape(-1) for name in small_names] + [loss.reshape(-1)]
    small_sizes = [t.shape[0] for t in small]
    small_rows = _rows_of(jnp.concatenate(small))
    small_all = _all_gather8(small_rows, "gather_small")
    small_sum = _sum_leading(small_all, "sum_small").reshape(-1)
    n_dmod = small_sizes[0]
    dmod_all = small_all.reshape(N_DEV, -1)[:, :n_dmod].reshape(N_DEV, L, NB, 6 * D)
    dmod_all = dmod_all.transpose(1, 0, 2, 3).reshape(L, B_all, 6 * D)
    off = n_dmod
    for name, sz in zip(small_names + ("loss",), small_sizes[1:]):
        grad[name] = small_sum[off:off + sz]
        off += sz
    loss_total = grad.pop("loss").reshape(())
    for name, ref in (("norm1_g", norm1_g), ("norm2_g", norm2_g), ("q_norm_a", q_norm_a),
                      ("k_norm_a", k_norm_a), ("sinks_a", sinks_a)):
        grad[name] = grad[name].reshape(ref.shape)

    n_shard = ada_w.shape[2]
    dmod_shard = lax.dynamic_slice_in_dim(dmod_all, chip * n_shard, n_shard, axis=2)
    grad["ada_w"], gb = _ada_bwd(c_all, dmod_all, dmod_shard)
    grad["ada_b"] = gb.reshape(ada_b.shape)

    weights = dict(ada_w=ada_w, ada_b=ada_b, norm1_g=norm1_g, norm2_g=norm2_g, wqkv_a=wqkv_a, q_norm_a=q_norm_a,
                   k_norm_a=k_norm_a, sinks_a=sinks_a, wo_a=wo_a, wqkv_b=wqkv_b, wo_b=wo_b, w_gate=w_gate,
                   w_up=w_up, w_down=w_down)
    m_in = dict(ada_w=m_ada_w, ada_b=m_ada_b, norm1_g=m_norm1_g, norm2_g=m_norm2_g, wqkv_a=m_wqkv_a,
                q_norm_a=m_q_norm_a, k_norm_a=m_k_norm_a, sinks_a=m_sinks_a, wo_a=m_wo_a, wqkv_b=m_wqkv_b,
                wo_b=m_wo_b, w_gate=m_w_gate, w_up=m_w_up, w_down=m_w_down)
    v_in = dict(ada_w=v_ada_w, ada_b=v_ada_b, norm1_g=v_norm1_g, norm2_g=v_norm2_g, wqkv_a=v_wqkv_a,
                q_norm_a=v_q_norm_a, k_norm_a=v_k_norm_a, sinks_a=v_sinks_a, wo_a=v_wo_a, wqkv_b=v_wqkv_b,
                wo_b=v_wo_b, w_gate=v_w_gate, w_up=v_w_up, w_down=v_w_down)
    names = list(weights)
    delta, new_m, new_v = {}, {}, {}
    for name in names:
        delta[name], new_m[name], new_v[name] = _adamw(weights[name], grad[name], m_in[name], v_in[name],
                                                       "adamw_" + name)
    return (loss_total, grad_x, *[grad[k] for k in names], *[delta[k] for k in names],
            *[new_m[k] for k in names], *[new_v[k] for k in names])
```

```python
import jax
import jax.numpy as jnp
from jax import lax
from jax.experimental import pallas as pl
from jax.experimental.pallas import tpu as pltpu

F32 = jnp.float32
BF16 = jnp.bfloat16

DEPTH = 4
HEAD_DIM = 64
N_Q_A = 16
N_KV_A = 2
GROUP_A = N_Q_A // N_KV_A
N_H_B = 16
BLOCK = 128
ROT_DIM = HEAD_DIM // 4
ROPE_THETA = 500000.0
EPS = 1e-6
ATTN_SCALE = HEAD_DIM ** -0.5
NEG_BIG = -1e30

ADAM_LR = 0.001
ADAM_B1 = 0.9
ADAM_B2 = 0.999
ADAM_EPS = 1e-08
ADAM_WD = 0.01
ADAM_STEP = 10

N_DEV = 8
LANES = 128
PACK_COLS = 1024
VMEM_LIMIT_BYTES = 48 * 1024 * 1024
MESH = pl.DeviceIdType.MESH

_NT = (((1,), (1,)), ((), ()))
_TN = (((0,), (0,)), ((), ()))
_NN = (((1,), (0,)), ((), ()))


def _params(sem=None):
    return pltpu.CompilerParams(vmem_limit_bytes=VMEM_LIMIT_BYTES, dimension_semantics=sem)


def _pick(n, cap, mult):
    best = None
    for t in range(mult, min(n, cap) + 1, mult):
        if n % t == 0:
            best = t
    return n if best is None else best


_ANY = pl.BlockSpec(memory_space=pl.ANY)


def _window(index, axis, q, n, shape):
    rest = [slice(None)] * len(shape)
    size = shape[axis] // n
    rest[axis] = pl.ds(q * size, size)
    return tuple(index) + tuple(rest)


def _all_gather8(xs, name):
    n = len(xs)

    def body(*refs):
        x_refs, out_refs = refs[:n], refs[n:2 * n]
        send_sems, recv_sems, local_sems = refs[2 * n:]
        xi, yi, ci = lax.axis_index("x"), lax.axis_index("y"), lax.axis_index("c")
        me, sibling = (xi, yi, ci), (xi, yi, 1 - ci)
        chips = [(1 - xi, yi), (xi, 1 - yi), (1 - xi, 1 - yi)]

        def slab(w, px, py, pc):
            return out_refs[w].at[4 * px + 2 * py + pc]

        def copy(w, k, block, to, src=None):
            return pltpu.make_async_remote_copy(
                src_ref=slab(w, *block) if src is None else src, dst_ref=slab(w, *block),
                send_sem=send_sems.at[k, w], recv_sem=recv_sems.at[k, w], device_id=to, device_id_type=MESH)

        mine = [pltpu.make_async_copy(x_refs[w], slab(w, *me), local_sems.at[w]) for w in range(n)]
        for cp in mine:
            cp.start()
        first = [copy(w, 0, me, sibling, src=x_refs[w]) for w in range(n)]
        first += [copy(w, 1 + j, me, (*chip, ci), src=x_refs[w]) for j, chip in enumerate(chips) for w in range(n)]
        for cp in first:
            cp.start()
        passed = []
        for j, chip in enumerate(chips):
            for w in range(n):
                copy(w, 1 + j, (*chip, ci), me).wait_recv()
                passed.append(copy(w, 4 + j, (*chip, ci), sibling))
                passed[-1].start()
        for w in range(n):
            copy(w, 0, sibling, me).wait_recv()
        for j, chip in enumerate(chips):
            for w in range(n):
                copy(w, 4 + j, (*chip, 1 - ci), me).wait_recv()
        for cp in first + passed:
            cp.wait_send()
        for cp in mine:
            cp.wait()

    return pl.pallas_call(
        body, name=name,
        out_shape=[jax.ShapeDtypeStruct((N_DEV,) + x.shape, x.dtype) for x in xs],
        in_specs=[_ANY] * n, out_specs=[_ANY] * n,
        scratch_shapes=[pltpu.SemaphoreType.DMA((7, n)), pltpu.SemaphoreType.DMA((7, n)),
                        pltpu.SemaphoreType.DMA((n,))],
    )(*xs)


def _exchange(xs, group, name, chunk_axis=0, chunks=1):
    n = len(xs)
    n_peers = 1 if group == "c" else 3

    def body(*refs):
        x_refs, out_refs = refs[:n], refs[n:2 * n]
        send_sems, recv_sems, local_sems = refs[2 * n:]
        xi, yi, ci = lax.axis_index("x"), lax.axis_index("y"), lax.axis_index("c")
        if group == "c":
            me = ci
            peers = [(1 - ci, (xi, yi, 1 - ci))]
        else:
            me = 2 * xi + yi
            peers = [(2 * (1 - xi) + yi, (1 - xi, yi, ci)),
                     (2 * xi + (1 - yi), (xi, 1 - yi, ci)),
                     (2 * (1 - xi) + (1 - yi), (1 - xi, 1 - yi, ci))]
        local = [pltpu.make_async_copy(x_refs[w].at[me], out_refs[w].at[me], local_sems.at[w]) for w in range(n)]
        for cp in local:
            cp.start()
        copies = []
        for k, (p, dev) in enumerate(peers):
            for w in range(n):
                slab_shape = xs[w].shape[1:]
                for q in range(chunks):
                    copies.append(pltpu.make_async_remote_copy(
                        src_ref=x_refs[w].at[_window((p,), chunk_axis, q, chunks, slab_shape)],
                        dst_ref=out_refs[w].at[_window((me,), chunk_axis, q, chunks, slab_shape)],
                        send_sem=send_sems.at[k, w, q], recv_sem=recv_sems.at[k, w, q],
                        device_id=dev, device_id_type=MESH))
                    copies[-1].start()
        for cp in copies:
            cp.wait()
        for cp in local:
            cp.wait()

    return pl.pallas_call(
        body, name=name,
        out_shape=[jax.ShapeDtypeStruct(x.shape, x.dtype) for x in xs],
        in_specs=[_ANY] * n, out_specs=[_ANY] * n,
        scratch_shapes=[pltpu.SemaphoreType.DMA((n_peers, n, chunks)), pltpu.SemaphoreType.DMA((n_peers, n, chunks)),
                        pltpu.SemaphoreType.DMA((n,))],
    )(*xs)


def _sibling_gather(xs, name, chunk_axis=1, chunks=4):
    n = len(xs)

    def body(*refs):
        x_refs, out_refs = refs[:n], refs[n:2 * n]
        send_sems, recv_sems, local_sems = refs[2 * n:]
        xi, yi, ci = lax.axis_index("x"), lax.axis_index("y"), lax.axis_index("c")
        local = [pltpu.make_async_copy(x_refs[w], out_refs[w].at[ci], local_sems.at[w]) for w in range(n)]
        for cp in local:
            cp.start()
        copies = []
        for w in range(n):
            for q in range(chunks):
                copies.append(pltpu.make_async_remote_copy(
                    src_ref=x_refs[w].at[_window((), chunk_axis, q, chunks, xs[w].shape)],
                    dst_ref=out_refs[w].at[_window((ci,), chunk_axis, q, chunks, xs[w].shape)],
                    send_sem=send_sems.at[w, q], recv_sem=recv_sems.at[w, q],
                    device_id=(xi, yi, 1 - ci), device_id_type=MESH))
                copies[-1].start()
        for cp in copies:
            cp.wait()
        for cp in local:
            cp.wait()

    return pl.pallas_call(
        body, name=name,
        out_shape=[jax.ShapeDtypeStruct((2,) + x.shape, x.dtype) for x in xs],
        in_specs=[_ANY] * n, out_specs=[_ANY] * n,
        scratch_shapes=[pltpu.SemaphoreType.DMA((n, chunks)), pltpu.SemaphoreType.DMA((n, chunks)),
                        pltpu.SemaphoreType.DMA((n,))],
    )(*xs)


def _sum_leading(x, name):
    P, R, C = x.shape
    tr = _pick(R, max(8, (1 << 19) // (C * P)), 8)

    def body(x_ref, o_ref):
        acc = x_ref[0]
        for p in range(1, P):
            acc = acc + x_ref[p]
        o_ref[...] = acc

    return pl.pallas_call(
        body, name=name, grid=(R // tr,),
        in_specs=[pl.BlockSpec((P, tr, C), lambda r: (0, r, 0))],
        out_specs=pl.BlockSpec((tr, C), lambda r: (r, 0)),
        out_shape=jax.ShapeDtypeStruct((R, C), F32),
        compiler_params=_params(("arbitrary",)),
    )(x)


def _matmul(a, b, mode, out_dtype, name):
    if mode == "nn":
        (M, K), N = a.shape, b.shape[1]
    elif mode == "nt":
        (M, K), N = a.shape, b.shape[0]
    else:
        (K, M), N = a.shape, b.shape[1]
    tm = _pick(M, 1024 if mode != "tn" else 1536, 128)
    tn = _pick(N, 1536, 128)
    tk = _pick(K, 512, 128)
    nk = K // tk
    dims = {"nn": _NN, "nt": _NT, "tn": _TN}[mode]

    def body(a_ref, b_ref, o_ref, acc_ref):
        k = pl.program_id(2)

        @pl.when(k == 0)
        def _():
            acc_ref[...] = jnp.zeros_like(acc_ref)

        acc_ref[...] += lax.dot_general(a_ref[...].astype(BF16), b_ref[...].astype(BF16), dims,
                                        preferred_element_type=F32)

        @pl.when(k == nk - 1)
        def _():
            o_ref[...] = acc_ref[...].astype(o_ref.dtype)

    if mode == "tn":
        a_spec = pl.BlockSpec((tk, tm), lambda i, j, k: (k, i))
    else:
        a_spec = pl.BlockSpec((tm, tk), lambda i, j, k: (i, k))
    if mode == "nt":
        b_spec = pl.BlockSpec((tn, tk), lambda i, j, k: (j, k))
    else:
        b_spec = pl.BlockSpec((tk, tn), lambda i, j, k: (k, j))
    return pl.pallas_call(
        body, name=name, grid=(M // tm, N // tn, nk),
        in_specs=[a_spec, b_spec],
        out_specs=pl.BlockSpec((tm, tn), lambda i, j, k: (i, j)),
        out_shape=jax.ShapeDtypeStruct((M, N), out_dtype),
        scratch_shapes=[pltpu.VMEM((tm, tn), F32)],
        compiler_params=_params(("parallel", "parallel", "arbitrary")),
    )(a, b)


def _row_tile(S):
    return _pick(S, 512, 8)


def _norm_mod_fwd(x, gain, sc, sh):
    NB, S, D = x.shape
    tr = _row_tile(S)

    def body(x_ref, g_ref, sc_ref, sh_ref, h_ref):
        xv = x_ref[...]
        ms = jnp.mean(xv * xv, axis=-1, keepdims=True)
        n = xv * lax.rsqrt(ms + EPS) * g_ref[...]
        h_ref[...] = (n * (1.0 + sc_ref[...]) + sh_ref[...]).astype(BF16)

    tok = pl.BlockSpec((None, tr, D), lambda b, r: (b, r, 0))
    per_ex = pl.BlockSpec((None, 1, D), lambda b, r: (b, 0, 0))
    return pl.pallas_call(
        body, name="norm_mod_fwd", grid=(NB, S // tr),
        in_specs=[tok, pl.BlockSpec((1, D), lambda b, r: (0, 0)), per_ex, per_ex],
        out_specs=tok, out_shape=jax.ShapeDtypeStruct((NB, S, D), BF16),
        compiler_params=_params(("parallel", "parallel")),
    )(x, gain, sc, sh)


def _norm_mod_bwd(x, gain, sc, dh, dres):
    NB, S, D = x.shape
    tr = _row_tile(S)

    def body(x_ref, g_ref, sc_ref, dh_ref, dres_ref, dx_ref, dsh_ref, dsc_ref, dg_ref):
        b, r = pl.program_id(0), pl.program_id(1)

        @pl.when(r == 0)
        def _():
            dsh_ref[...] = jnp.zeros_like(dsh_ref)
            dsc_ref[...] = jnp.zeros_like(dsc_ref)

        @pl.when((r == 0) & (b == 0))
        def _():
            dg_ref[...] = jnp.zeros_like(dg_ref)

        xv = x_ref[...]
        rstd = lax.rsqrt(jnp.mean(xv * xv, axis=-1, keepdims=True) + EPS)
        xh = xv * rstd
        g = g_ref[...]
        dh = dh_ref[...]
        dsh_ref[...] += jnp.sum(dh, axis=0, keepdims=True)
        dsc_ref[...] += jnp.sum(dh * (xh * g), axis=0, keepdims=True)
        dn = dh * (1.0 + sc_ref[...])
        dg_ref[...] += jnp.sum(dn * xh, axis=0, keepdims=True)
        dxh = dn * g
        proj = jnp.mean(dxh * xh, axis=-1, keepdims=True)
        dx_ref[...] = rstd * (dxh - xh * proj) + dres_ref[...]

    tok = pl.BlockSpec((None, tr, D), lambda b, r: (b, r, 0))
    per_ex = pl.BlockSpec((None, 1, D), lambda b, r: (b, 0, 0))
    row = pl.BlockSpec((1, D), lambda b, r: (0, 0))
    return pl.pallas_call(
        body, name="norm_mod_bwd", grid=(NB, S // tr),
        in_specs=[tok, row, per_ex, tok, tok],
        out_specs=[tok, per_ex, per_ex, row],
        out_shape=[jax.ShapeDtypeStruct((NB, S, D), F32), jax.ShapeDtypeStruct((NB, 1, D), F32),
                   jax.ShapeDtypeStruct((NB, 1, D), F32), jax.ShapeDtypeStruct((1, D), F32)],
        compiler_params=_params(("arbitrary", "arbitrary")),
    )(x, gain, sc, dh, dres)


def _gate_res(x, y, g):
    NB, S, D = x.shape
    tr = _row_tile(S)

    def body(x_ref, y_ref, g_ref, o_ref):
        o_ref[...] = x_ref[...] + g_ref[...] * y_ref[...]

    tok = pl.BlockSpec((None, tr, D), lambda b, r: (b, r, 0))
    per_ex = pl.BlockSpec((None, 1, D), lambda b, r: (b, 0, 0))
    return pl.pallas_call(
        body, name="gate_res", grid=(NB, S // tr), in_specs=[tok, tok, per_ex], out_specs=tok,
        out_shape=jax.ShapeDtypeStruct((NB, S, D), F32),
        compiler_params=_params(("parallel", "parallel")),
    )(x, y, g)


def _gate_res_bwd(dxo, y, g):
    NB, S, D = dxo.shape
    tr = _row_tile(S)

    def body(d_ref, y_ref, g_ref, dy_ref, dg_ref):
        @pl.when(pl.program_id(1) == 0)
        def _():
            dg_ref[...] = jnp.zeros_like(dg_ref)

        d = d_ref[...]
        dy_ref[...] = (d * g_ref[...]).astype(BF16)
        dg_ref[...] += jnp.sum(d * y_ref[...], axis=0, keepdims=True)

    tok = pl.BlockSpec((None, tr, D), lambda b, r: (b, r, 0))
    per_ex = pl.BlockSpec((None, 1, D), lambda b, r: (b, 0, 0))
    return pl.pallas_call(
        body, name="gate_res_bwd", grid=(NB, S // tr), in_specs=[tok, tok, per_ex], out_specs=[tok, per_ex],
        out_shape=[jax.ShapeDtypeStruct((NB, S, D), BF16), jax.ShapeDtypeStruct((NB, 1, D), F32)],
        compiler_params=_params(("arbitrary", "arbitrary")),
    )(dxo, y, g)


def _sigmoid(v):
    return 1.0 / (1.0 + jnp.exp(-v))


def _swiglu_fwd(gu):
    T, F2 = gu.shape
    F = F2 // 2
    tf = _pick(F, 1536, 128)
    nf = F // tf
    tr = _pick(T, 256, 8)

    def body(g_ref, u_ref, o_ref):
        g = g_ref[...]
        o_ref[...] = (g * _sigmoid(g) * u_ref[...]).astype(BF16)

    return pl.pallas_call(
        body, name="swiglu_fwd", grid=(T // tr, nf),
        in_specs=[pl.BlockSpec((tr, tf), lambda i, j: (i, j)), pl.BlockSpec((tr, tf), lambda i, j: (i, j + nf))],
        out_specs=pl.BlockSpec((tr, tf), lambda i, j: (i, j)),
        out_shape=jax.ShapeDtypeStruct((T, F), BF16),
        compiler_params=_params(("parallel", "parallel")),
    )(gu, gu)


def _swiglu_bwd(gu, dact):
    T, F2 = gu.shape
    F = F2 // 2
    tf = _pick(F, 1536, 128)
    nf = F // tf
    tr = _pick(T, 256, 8)

    def body(g_ref, u_ref, d_ref, o_ref):
        j = pl.program_id(1)
        g, u, d = g_ref[...], u_ref[...], d_ref[...]
        s = _sigmoid(g)
        dg = d * u * (s * (1.0 + g * (1.0 - s)))
        du = d * (g * s)
        o_ref[...] = jnp.where(j < nf, dg, du).astype(BF16)

    return pl.pallas_call(
        body, name="swiglu_bwd", grid=(T // tr, 2 * nf),
        in_specs=[pl.BlockSpec((tr, tf), lambda i, j: (i, j % nf)),
                  pl.BlockSpec((tr, tf), lambda i, j: (i, nf + j % nf)),
                  pl.BlockSpec((tr, tf), lambda i, j: (i, j % nf))],
        out_specs=pl.BlockSpec((tr, tf), lambda i, j: (i, j)),
        out_shape=jax.ShapeDtypeStruct((T, F2), BF16),
        compiler_params=_params(("parallel", "parallel")),
    )(gu, gu, dact)


def _loss_fwd_bwd(y, target):
    NB, S, D = y.shape
    tr = _row_tile(S)

    def body(y_ref, t_ref, l_ref, d_ref):
        @pl.when((pl.program_id(0) == 0) & (pl.program_id(1) == 0))
        def _():
            l_ref[...] = jnp.zeros_like(l_ref)

        e = y_ref[...] - t_ref[...]
        d_ref[...] = e / D
        l_ref[...] += 0.5 * jnp.sum(jnp.mean(e * e, axis=-1, keepdims=True), axis=0, keepdims=True)

    tok = pl.BlockSpec((None, tr, D), lambda b, r: (b, r, 0))
    return pl.pallas_call(
        body, name="loss", grid=(NB, S // tr), in_specs=[tok, tok],
        out_specs=[pl.BlockSpec((1, 1), lambda b, r: (0, 0)), tok],
        out_shape=[jax.ShapeDtypeStruct((1, 1), F32), jax.ShapeDtypeStruct((NB, S, D), F32)],
        compiler_params=_params(("arbitrary", "arbitrary")),
    )(y, target)


def _half_sums(v, lo):
    sa = jnp.sum(jnp.where(lo, v, 0.0), axis=-1, keepdims=True)
    sb = jnp.sum(jnp.where(lo, 0.0, v), axis=-1, keepdims=True)
    return jnp.where(lo, sa, sb)


def _rope_swap(v, lane64):
    up = pltpu.roll(v, LANES - ROT_DIM // 2, 2)
    down = pltpu.roll(v, ROT_DIM // 2, 2)
    return jnp.where(lane64 < ROT_DIM // 2, up, jnp.where(lane64 < ROT_DIM, down, 0.0))


def _qk_prep_fwd(qkv3, tab_c, tab_s, gains):
    T = qkv3.shape[0]
    R = qkv3.shape[1]
    tt = _pick(T, 256, 8)

    def body(x_ref, c_ref, s_ref, g_ref, o_ref):
        xv = x_ref[...]
        lane = lax.broadcasted_iota(jnp.int32, xv.shape, 2)
        row = lax.broadcasted_iota(jnp.int32, xv.shape, 1)
        lo = lane < HEAD_DIM
        rstd = lax.rsqrt(_half_sums(xv * xv, lo) / HEAD_DIM + EPS)
        yn = xv * rstd * g_ref[...]
        roped = yn * c_ref[...] + _rope_swap(yn, lane & (HEAD_DIM - 1)) * s_ref[...]
        o_ref[...] = jnp.where(row == R - 1, xv, roped).astype(BF16)

    tok = pl.BlockSpec((tt, R, LANES), lambda t: (t, 0, 0))
    tab = pl.BlockSpec((tt, 1, LANES), lambda t: (t, 0, 0))
    return pl.pallas_call(
        body, name="qk_prep_fwd", grid=(T // tt,),
        in_specs=[tok, tab, tab, pl.BlockSpec((R, LANES), lambda t: (0, 0))],
        out_specs=tok, out_shape=jax.ShapeDtypeStruct(qkv3.shape, BF16),
        compiler_params=_params(("parallel",)),
    )(qkv3, tab_c, tab_s, gains)


def _qk_prep_bwd(qkv3, d3, tab_c, tab_s, gains):
    T = qkv3.shape[0]
    R = qkv3.shape[1]
    tt = _pick(T, 256, 8)

    def body(x_ref, d_ref, c_ref, s_ref, g_ref, o_ref, dg_ref):
        @pl.when(pl.program_id(0) == 0)
        def _():
            dg_ref[...] = jnp.zeros_like(dg_ref)

        xv, d = x_ref[...], d_ref[...]
        lane = lax.broadcasted_iota(jnp.int32, xv.shape, 2)
        row = lax.broadcasted_iota(jnp.int32, xv.shape, 1)
        lo = lane < HEAD_DIM
        rstd = lax.rsqrt(_half_sums(xv * xv, lo) / HEAD_DIM + EPS)
        xh = xv * rstd
        dyn = d * c_ref[...] + _rope_swap(d * s_ref[...], lane & (HEAD_DIM - 1))
        dg_ref[...] += jnp.sum(dyn * xh, axis=0)
        dxh = dyn * g_ref[...]
        proj = _half_sums(dxh * xh, lo) / HEAD_DIM
        dx = rstd * (dxh - xh * proj)
        o_ref[...] = jnp.where(row == R - 1, d, dx).astype(BF16)

    tok = pl.BlockSpec((tt, R, LANES), lambda t: (t, 0, 0))
    tab = pl.BlockSpec((tt, 1, LANES), lambda t: (t, 0, 0))
    gsp = pl.BlockSpec((R, LANES), lambda t: (0, 0))
    return pl.pallas_call(
        body, name="qk_prep_bwd", grid=(T // tt,),
        in_specs=[tok, tok, tab, tab, gsp], out_specs=[tok, gsp],
        out_shape=[jax.ShapeDtypeStruct(qkv3.shape, BF16), jax.ShapeDtypeStruct((R, LANES), F32)],
        compiler_params=_params(("arbitrary",)),
    )(qkv3, d3, tab_c, tab_s, gains)


def _band_mask(i):
    r = lax.broadcasted_iota(jnp.int32, (BLOCK, 2 * BLOCK), 0)
    c = lax.broadcasted_iota(jnp.int32, (BLOCK, 2 * BLOCK), 1)
    rel = r + BLOCK - c
    return (rel >= 0) & (rel < BLOCK) & ((c >= BLOCK) | (i > 0))


def _swa_probs(qg, k2, valid, sink):
    s = lax.dot_general(qg, k2, _NT, preferred_element_type=F32) * ATTN_SCALE
    s = jnp.where(valid, s, NEG_BIG)
    m = jnp.maximum(jnp.max(s, axis=1, keepdims=True), sink)
    p = jnp.exp(s - m)
    ps = jnp.exp(sink - m)
    denom = jnp.sum(p, axis=1, keepdims=True) + ps
    return p / denom, ps / denom


def _swa_specs(S):
    qs = pl.BlockSpec((None, None, GROUP_A, BLOCK, HEAD_DIM), lambda b, h, i: (b, h, 0, i, 0))
    prev = pl.BlockSpec((None, None, BLOCK, HEAD_DIM), lambda b, h, i: (b, h, jnp.maximum(i - 1, 0), 0))
    cur = pl.BlockSpec((None, None, BLOCK, HEAD_DIM), lambda b, h, i: (b, h, i, 0))
    return qs, prev, cur


def _attn_a_fwd(q, k, v, sinks):
    NB, _, _, S, _ = q.shape
    qs, prev, cur = _swa_specs(S)

    def body(q_ref, kp_ref, kc_ref, vp_ref, vc_ref, sink_ref, o_ref):
        h, i = pl.program_id(1), pl.program_id(2)
        k2 = jnp.concatenate([kp_ref[...], kc_ref[...]], axis=0)
        v2 = jnp.concatenate([vp_ref[...], vc_ref[...]], axis=0)
        valid = _band_mask(i)
        for g in range(GROUP_A):
            pn, _ = _swa_probs(q_ref[g], k2, valid, sink_ref[h * GROUP_A + g])
            o_ref[g] = jnp.dot(pn.astype(BF16), v2, preferred_element_type=F32).astype(BF16)

    return pl.pallas_call(
        body, name="attn_a_fwd", grid=(NB, N_KV_A, S // BLOCK),
        in_specs=[qs, prev, cur, prev, cur, pl.BlockSpec(memory_space=pltpu.SMEM)],
        out_specs=qs, out_shape=jax.ShapeDtypeStruct(q.shape, BF16),
        compiler_params=_params(("parallel", "parallel", "arbitrary")),
    )(q, k, k, v, v, sinks)


def _attn_a_bwd(q, k, v, do, sinks):
    NB, _, _, S, _ = q.shape
    qs, prev, cur = _swa_specs(S)
    full = pl.BlockSpec((None, None, S, HEAD_DIM), lambda b, h, i: (b, h, 0, 0))
    sink_out = pl.BlockSpec((None, None, GROUP_A, LANES), lambda b, h, i: (b, h, 0, 0))

    def body(q_ref, do_ref, kp_ref, kc_ref, vp_ref, vc_ref, sink_ref, dq_ref, dk_ref, dv_ref, ds_ref):
        h, i = pl.program_id(1), pl.program_id(2)

        @pl.when(i == 0)
        def _():
            dk_ref[...] = jnp.zeros_like(dk_ref)
            dv_ref[...] = jnp.zeros_like(dv_ref)
            ds_ref[...] = jnp.zeros_like(ds_ref)

        k2 = jnp.concatenate([kp_ref[...], kc_ref[...]], axis=0)
        v2 = jnp.concatenate([vp_ref[...], vc_ref[...]], axis=0)
        valid = _band_mask(i)
        dk2 = jnp.zeros((2 * BLOCK, HEAD_DIM), F32)
        dv2 = jnp.zeros((2 * BLOCK, HEAD_DIM), F32)
        for g in range(GROUP_A):
            qg, dog = q_ref[g], do_ref[g]
            pn, psink = _swa_probs(qg, k2, valid, sink_ref[h * GROUP_A + g])
            dp = lax.dot_general(dog, v2, _NT, preferred_element_type=F32)
            delta = jnp.sum(pn * dp, axis=1, keepdims=True)
            dsb = (pn * (dp - delta) * ATTN_SCALE).astype(BF16)
            dq_ref[g] = jnp.dot(dsb, k2, preferred_element_type=F32)
            dk2 = dk2 + lax.dot_general(dsb, qg, _TN, preferred_element_type=F32)
            dv2 = dv2 + lax.dot_general(pn.astype(BF16), dog, _TN, preferred_element_type=F32)
            dsink = -jnp.sum(psink * delta, axis=0, keepdims=True)
            ds_ref[g:g + 1, :] += jnp.broadcast_to(dsink, (1, LANES))

        @pl.when(i > 0)
        def _():
            start = pl.multiple_of((i - 1) * BLOCK, BLOCK)
            dk_ref[pl.ds(start, 2 * BLOCK), :] += dk2
            dv_ref[pl.ds(start, 2 * BLOCK), :] += dv2

        @pl.when(i == 0)
        def _():
            dk_ref[0:BLOCK, :] += dk2[BLOCK:, :]
            dv_ref[0:BLOCK, :] += dv2[BLOCK:, :]

    return pl.pallas_call(
        body, name="attn_a_bwd", grid=(NB, N_KV_A, S // BLOCK),
        in_specs=[qs, qs, prev, cur, prev, cur, pl.BlockSpec(memory_space=pltpu.SMEM)],
        out_specs=[qs, full, full, sink_out],
        out_shape=[jax.ShapeDtypeStruct(q.shape, F32), jax.ShapeDtypeStruct(k.shape, F32),
                   jax.ShapeDtypeStruct(k.shape, F32), jax.ShapeDtypeStruct((NB, N_KV_A, GROUP_A, LANES), F32)],
        compiler_params=_params(("parallel", "parallel", "arbitrary")),
    )(q, do, k, k, v, v, sinks)


def _cumsum_mats():
    src = lax.broadcasted_iota(jnp.int32, (2 * BLOCK, 2 * BLOCK), 0) % BLOCK
    dst = lax.broadcasted_iota(jnp.int32, (2 * BLOCK, 2 * BLOCK), 1)
    ones = dst >= BLOCK
    rev = ((src > dst) | ones).astype(BF16)
    fwd = ((src < dst) | ones).astype(BF16)
    return rev, fwd


def _cumsum_mxu(v, mat):
    hi = v.astype(BF16)
    lo = (v - hi.astype(F32)).astype(BF16)
    r = jnp.dot(jnp.concatenate([hi, lo], axis=1), mat, preferred_element_type=F32)
    return r[:, :BLOCK], r[:, BLOCK:]


def _sb_logs(qv, kj):
    z = lax.dot_general(qv, kj, _NT, preferred_element_type=F32) * ATTN_SCALE
    sp = jnp.log1p(jnp.exp(-jnp.abs(z)))
    return jnp.minimum(z, 0.0) - sp, -(jnp.maximum(z, 0.0) + sp)


def _strict_mask():
    r = lax.broadcasted_iota(jnp.int32, (BLOCK, BLOCK), 0)
    c = lax.broadcasted_iota(jnp.int32, (BLOCK, BLOCK), 1)
    return c < r


def _tile(ref, j):
    return ref[pl.ds(pl.multiple_of(j * BLOCK, BLOCK), BLOCK), :]


SWEEP_EXIT = -110.0


def _head_halves(t, lo):
    zero = jnp.zeros_like(t)
    return jnp.where(lo, t, zero), jnp.where(lo, zero, t)


def _sb_specs(S, n_pair):
    blk = pl.BlockSpec((None, BLOCK, LANES), lambda b, p, i: (b, i, p))
    k_full = pl.BlockSpec((None, S, LANES), lambda b, p, i: (b, 0, n_pair + p))
    v_full = pl.BlockSpec((None, S, LANES), lambda b, p, i: (b, 0, 2 * n_pair + p))
    mat = pl.BlockSpec((2 * BLOCK, 2 * BLOCK), lambda b, p, i: (0, 0))
    return blk, k_full, v_full, mat


def _attn_b_fwd(qkv, rev):
    NB, S, W = qkv.shape
    HD = W // 3
    n_pair = HD // LANES
    blk, k_full, v_full, mat = _sb_specs(S, n_pair)

    def body(q_ref, k_ref, v_ref, rev_ref, o_ref):
        i = pl.program_id(2)
        rv = rev_ref[...]
        mask = _strict_mask()
        lo = lax.broadcasted_iota(jnp.int32, (BLOCK, LANES), 1) < HEAD_DIM
        qh = _head_halves(q_ref[...], lo)

        kj, vj = _tile(k_ref, i), _tile(v_ref, i)
        carries, outs = [], []
        for h in range(2):
            lb, lm = _sb_logs(qh[h], kj)
            after, rs = _cumsum_mxu(jnp.where(mask, lm, 0.0), rv)
            a = jnp.where(mask, jnp.exp(lb + after), 0.0)
            outs.append(jnp.dot(a.astype(BF16), vj, preferred_element_type=F32))
            carries.append(rs)
        acc = jnp.where(lo, outs[0], outs[1])

        def live(c0, c1):
            return jnp.max(jnp.maximum(c0, c1)) > SWEEP_EXIT

        def cond(st):
            return (st[0] < i) & st[1]

        def step(st):
            jj, _, c0, c1, acc = st
            j = i - 1 - jj
            kj, vj = _tile(k_ref, j), _tile(v_ref, j)
            new_c, outs = [], []
            for h, carry in enumerate((c0, c1)):
                lb, lm = _sb_logs(qh[h], kj)
                after, rs = _cumsum_mxu(lm, rv)
                a = jnp.exp(lb + after + carry)
                outs.append(jnp.dot(a.astype(BF16), vj, preferred_element_type=F32))
                new_c.append(carry + rs)
            acc = acc + jnp.where(lo, outs[0], outs[1])
            return jj + 1, live(*new_c), new_c[0], new_c[1], acc

        st = lax.while_loop(cond, step, (jnp.int32(0), live(*carries), carries[0], carries[1], acc))
        o_ref[...] = st[4].astype(BF16)

    return pl.pallas_call(
        body, name="attn_b_fwd", grid=(NB, n_pair, S // BLOCK),
        in_specs=[blk, k_full, v_full, mat], out_specs=blk,
        out_shape=jax.ShapeDtypeStruct((NB, S, HD), BF16),
        compiler_params=_params(("parallel", "parallel", "arbitrary")),
    )(qkv, qkv, qkv, rev)


def _attn_b_bwd(qkv, do, rev, fwd):
    NB, S, W = qkv.shape
    HD = W // 3
    n_pair = HD // LANES
    nj = S // BLOCK
    blk, k_full, v_full, mat = _sb_specs(S, n_pair)
    acc_full = pl.BlockSpec((None, S, LANES), lambda b, p, i: (b, 0, p))

    def body(q_ref, do_ref, k_ref, v_ref, rev_ref, fwd_ref, dq_ref, dk_ref, dv_ref, sig_s, a_s, e_s):
        i = pl.program_id(2)

        @pl.when(i == 0)
        def _():
            dk_ref[...] = jnp.zeros_like(dk_ref)
            dv_ref[...] = jnp.zeros_like(dv_ref)

        q2, do2 = q_ref[...], do_ref[...]
        rv, fw = rev_ref[...], fwd_ref[...]
        mask = _strict_mask()
        lo = lax.broadcasted_iota(jnp.int32, (BLOCK, LANES), 1) < HEAD_DIM
        qh = _head_halves(q2, lo)
        doh = _head_halves(do2, lo)

        def stash(h, j, vj, lb, a):
            da = lax.dot_general(doh[h], vj, _NT, preferred_element_type=F32)
            sig_s[h, j] = jnp.exp(lb)
            a_s[h, j] = a
            e_s[h, j] = da * a

        kj, vj = _tile(k_ref, i), _tile(v_ref, i)
        carries = []
        for h in range(2):
            lb, lm = _sb_logs(qh[h], kj)
            after, rs = _cumsum_mxu(jnp.where(mask, lm, 0.0), rv)
            stash(h, i, vj, lb, jnp.where(mask, jnp.exp(lb + after), 0.0))
            carries.append(rs)

        def live(c0, c1):
            return jnp.max(jnp.maximum(c0, c1)) > SWEEP_EXIT

        def cond(st):
            return (st[0] < i) & st[1]

        def sweep1(st):
            jj, _, c0, c1 = st
            j = i - 1 - jj
            kj, vj = _tile(k_ref, j), _tile(v_ref, j)
            new_c = []
            for h, carry in enumerate((c0, c1)):
                lb, lm = _sb_logs(qh[h], kj)
                after, rs = _cumsum_mxu(lm, rv)
                stash(h, j, vj, lb, jnp.exp(lb + after + carry))
                new_c.append(carry + rs)
            return jj + 1, live(*new_c), new_c[0], new_c[1]

        visited = lax.while_loop(cond, sweep1, (jnp.int32(0), live(*carries), carries[0], carries[1]))[0]

        def grads(j, st, diagonal):
            p0, p1, dq = st
            kj = _tile(k_ref, j)
            new_p, dqs, dks, dvs = [], [], [], []
            for h, prefix in enumerate((p0, p1)):
                e, a, sg = e_s[h, j], a_s[h, j], sig_s[h, j]
                e_before, rs = _cumsum_mxu(e, fw)
                dz = (e * (1.0 - sg) - (e_before + prefix) * sg) * ATTN_SCALE
                if diagonal:
                    dz = jnp.where(mask, dz, 0.0)
                dzb = dz.astype(BF16)
                dqs.append(jnp.dot(dzb, kj, preferred_element_type=F32))
                dks.append(lax.dot_general(dzb, q2, _TN, preferred_element_type=F32))
                dvs.append(lax.dot_general(a.astype(BF16), do2, _TN, preferred_element_type=F32))
                new_p.append(prefix + rs)
            rows = pl.ds(pl.multiple_of(j * BLOCK, BLOCK), BLOCK)
            dk_ref[rows, :] += jnp.where(lo, dks[0], dks[1])
            dv_ref[rows, :] += jnp.where(lo, dvs[0], dvs[1])
            return new_p[0], new_p[1], dq + jnp.where(lo, dqs[0], dqs[1])

        zeros = jnp.zeros((BLOCK, BLOCK), F32)
        st = lax.fori_loop(i - visited, i, lambda j, st: grads(j, st, False), (zeros, zeros, zeros))
        dq_ref[...] = grads(i, st, True)[2]

    tile_stash = pltpu.VMEM((2, nj, BLOCK, BLOCK), F32)
    return pl.pallas_call(
        body, name="attn_b_bwd", grid=(NB, n_pair, nj),
        in_specs=[blk, blk, k_full, v_full, mat, mat], out_specs=[blk, acc_full, acc_full],
        out_shape=[jax.ShapeDtypeStruct((NB, S, HD), F32)] * 3,
        scratch_shapes=[tile_stash, tile_stash, tile_stash],
        compiler_params=_params(("parallel", "parallel", "arbitrary")),
    )(qkv, do, qkv, qkv, rev, fwd)


def _ada_fwd(c_all, w, b):
    L, D, N = w.shape
    B = c_all.shape[0]

    def body(c_ref, w_ref, b_ref, o_ref):
        cv = c_ref[...]
        cond = (cv * _sigmoid(cv)).astype(BF16)
        o_ref[...] = jnp.dot(cond, w_ref[...].astype(BF16), preferred_element_type=F32) + b_ref[...]

    return pl.pallas_call(
        body, name="ada_fwd", grid=(L,),
        in_specs=[pl.BlockSpec((B, D), lambda l: (0, 0)), pl.BlockSpec((None, D, N), lambda l: (l, 0, 0)),
                  pl.BlockSpec((None, 1, N), lambda l: (l, 0, 0))],
        out_specs=pl.BlockSpec((None, B, N), lambda l: (l, 0, 0)),
        out_shape=jax.ShapeDtypeStruct((L, B, N), F32),
        compiler_params=_params(("parallel",)),
    )(c_all, w, b)


def _ada_bwd(c_all, dmod_all, dmod_shard):
    L, B, N = dmod_shard.shape
    D = c_all.shape[1]
    N_all = dmod_all.shape[2]

    def body(c_ref, da_ref, ds_ref, gw_ref, gb_ref):
        cv = c_ref[...]
        cond = (cv * _sigmoid(cv)).astype(BF16)
        gw_ref[...] = lax.dot_general(cond, ds_ref[...].astype(BF16), _TN, preferred_element_type=F32)
        gb_ref[...] = jnp.sum(da_ref[...], axis=0, keepdims=True)

    return pl.pallas_call(
        body, name="ada_bwd", grid=(L,),
        in_specs=[pl.BlockSpec((B, D), lambda l: (0, 0)), pl.BlockSpec((None, B, N_all), lambda l: (l, 0, 0)),
                  pl.BlockSpec((None, B, N), lambda l: (l, 0, 0))],
        out_specs=[pl.BlockSpec((None, D, N), lambda l: (l, 0, 0)), pl.BlockSpec((None, 1, N_all), lambda l: (l, 0, 0))],
        out_shape=[jax.ShapeDtypeStruct((L, D, N), F32), jax.ShapeDtypeStruct((L, 1, N_all), F32)],
        compiler_params=_params(("parallel",)),
    )(c_all, dmod_all, dmod_shard)


def _adamw(w, g, m, v, name):
    shape = w.shape
    C = shape[-1]
    R = w.size // C
    tr = _pick(R, max(8, (1 << 18) // C), 8)
    c1 = 1.0 - ADAM_B1 ** ADAM_STEP
    c2 = 1.0 - ADAM_B2 ** ADAM_STEP

    def body(w_ref, g_ref, m_ref, v_ref, d_ref, nm_ref, nv_ref):
        gv = g_ref[...]
        nm = ADAM_B1 * m_ref[...] + (1.0 - ADAM_B1) * gv
        nv = ADAM_B2 * v_ref[...] + (1.0 - ADAM_B2) * (gv * gv)
        d_ref[...] = -ADAM_LR * ((nm / c1) / (jnp.sqrt(nv / c2) + ADAM_EPS) + ADAM_WD * w_ref[...])
        nm_ref[...] = nm
        nv_ref[...] = nv

    spec = pl.BlockSpec((tr, C), lambda r: (r, 0))
    out = pl.pallas_call(
        body, name=name, grid=(R // tr,), in_specs=[spec] * 4, out_specs=[spec] * 3,
        out_shape=[jax.ShapeDtypeStruct((R, C), F32)] * 3,
        compiler_params=_params(("parallel",)),
    )(*[t.reshape(R, C) for t in (w, g, m, v)])
    return [t.reshape(shape) for t in out]


_SHARDED = (("wqkv_a", 2), ("wo_a", 1), ("wqkv_b", 2), ("wo_b", 1), ("w_gate", 2), ("w_up", 2), ("w_down", 1))


def _pack_full(full, axis):
    L, R, C = full.shape
    if axis == 2:
        return full.reshape(2, L // 2, R, 4, C // 4).transpose(0, 3, 1, 2, 4)
    return full.reshape(2, L // 2, 4, R // 4, C).transpose(0, 2, 1, 3, 4)


def _unpack_full(gathered, axis):
    _, Lh, Rs, Cs = gathered.shape
    t = gathered.reshape(4, 2, Lh, Rs, Cs)
    if axis == 2:
        return t.transpose(1, 2, 3, 0, 4).reshape(2 * Lh, Rs, 4 * Cs)
    return t.transpose(1, 2, 0, 3, 4).reshape(2 * Lh, 4 * Rs, Cs)


def _sum_slabs(t, name):
    return _sum_leading(t.reshape(t.shape[0], -1, t.shape[-1]), name).reshape(t.shape[1:])


def _rope_tables(positions):
    half = ROT_DIM // 2
    inv_freq = jnp.power(jnp.float32(ROPE_THETA), -jnp.arange(half, dtype=F32) * 2.0 / ROT_DIM)
    ang = positions.astype(F32).reshape(-1, 1) * inv_freq
    cos, sin = jnp.cos(ang), jnp.sin(ang)
    T = ang.shape[0]
    rest = HEAD_DIM - ROT_DIM
    c64 = jnp.concatenate([cos, cos, jnp.ones((T, rest), F32)], axis=1)
    s64 = jnp.concatenate([-sin, sin, jnp.zeros((T, rest), F32)], axis=1)
    return jnp.tile(c64, (1, 2)).reshape(T, 1, LANES), jnp.tile(s64, (1, 2)).reshape(T, 1, LANES)


def _gain_rows(q_gain, k_gain):
    q2 = jnp.tile(q_gain.reshape(1, HEAD_DIM), (GROUP_A, 2))
    k2 = jnp.tile(k_gain.reshape(1, HEAD_DIM), (1, 2))
    return jnp.concatenate([q2, k2, jnp.ones((1, LANES), F32)], axis=0)


def _local_step(x, positions, mod, norm1_g, norm2_g, q_norm_a, k_norm_a, sinks_a,
                wqkv_a, wo_a, wqkv_b, wo_b, wgu, wd, loss_target):
    NB, S, D = x.shape
    T = NB * S
    QA = N_Q_A * HEAD_DIM
    rows_a = wqkv_a.shape[2] // LANES
    tab_c, tab_s = _rope_tables(positions)
    rev, fwd = _cumsum_mats()

    saved = []
    xc = x
    for i in range(DEPTH):
        j = i // 2
        sh1, sc1, g1, sh2, sc2, g2 = [mod[i][:, k * D:(k + 1) * D].reshape(NB, 1, D) for k in range(6)]
        st = dict(x=xc, sc1=sc1, g1=g1, sc2=sc2, g2=g2)
        h = _norm_mod_fwd(xc, norm1_g[i:i + 1], sc1, sh1)
        st["h"] = h.reshape(T, D)
        if i % 2 == 0:
            qkv = _matmul(st["h"], wqkv_a[j], "nn", F32, "qkv_a")
            st["qkv3"] = qkv.reshape(T, rows_a, LANES)
            st["gains"] = _gain_rows(q_norm_a[j], k_norm_a[j])
            qkn = _qk_prep_fwd(st["qkv3"], tab_c, tab_s, st["gains"])
            st["q"] = qkn[:, :GROUP_A].reshape(NB, S, N_KV_A, GROUP_A, HEAD_DIM).transpose(0, 2, 3, 1, 4)
            st["k"] = qkn[:, GROUP_A].reshape(NB, S, N_KV_A, HEAD_DIM).transpose(0, 2, 1, 3)
            st["v"] = qkn[:, GROUP_A + 1].reshape(NB, S, N_KV_A, HEAD_DIM).transpose(0, 2, 1, 3)
            o = _attn_a_fwd(st["q"], st["k"], st["v"], sinks_a[j])
            st["o"] = o.transpose(0, 3, 1, 2, 4).reshape(T, QA)
            y = _matmul(st["o"], wo_a[j], "nn", F32, "wo_a")
        else:
            st["qkv"] = _matmul(st["h"], wqkv_b[j], "nn", BF16, "qkv_b").reshape(NB, S, -1)
            st["o"] = _attn_b_fwd(st["qkv"], rev).reshape(T, N_H_B * HEAD_DIM)
            y = _matmul(st["o"], wo_b[j], "nn", F32, "wo_b")
        st["y"] = y.reshape(NB, S, D)
        x1 = _gate_res(xc, st["y"], g1)
        st["x1"] = x1
        h2 = _norm_mod_fwd(x1, norm2_g[i:i + 1], sc2, sh2)
        st["h2"] = h2.reshape(T, D)
        st["gu"] = _matmul(st["h2"], wgu[i], "nn", F32, "gate_up")
        st["act"] = _swiglu_fwd(st["gu"])
        st["m"] = _matmul(st["act"], wd[i], "nn", F32, "down").reshape(NB, S, D)
        xc = _gate_res(x1, st["m"], g2)
        saved.append(st)

    loss, dx = _loss_fwd_bwd(xc, loss_target)

    grads = {name: [None] * n for name, n in
             (("wqkv_a", 2), ("wo_a", 2), ("wqkv_b", 2), ("wo_b", 2), ("wgu", DEPTH), ("wd", DEPTH),
              ("norm1_g", DEPTH), ("norm2_g", DEPTH), ("q_norm_a", 2), ("k_norm_a", 2), ("sinks_a", 2))}
    dmod = [None] * DEPTH
    for i in reversed(range(DEPTH)):
        j = i // 2
        st = saved[i]
        dm, dg2 = _gate_res_bwd(dx, st["m"], st["g2"])
        dm = dm.reshape(T, D)
        dact = _matmul(dm, wd[i], "nt", F32, "d_act")
        grads["wd"][i] = _matmul(st["act"], dm, "tn", F32, "d_wd")
        dgu = _swiglu_bwd(st["gu"], dact)
        dh2 = _matmul(dgu, wgu[i], "nt", F32, "d_h2")
        grads["wgu"][i] = _matmul(st["h2"], dgu, "tn", F32, "d_wgu")
        dx1, dsh2, dsc2, grads["norm2_g"][i] = _norm_mod_bwd(
            st["x1"], norm2_g[i:i + 1], st["sc2"], dh2.reshape(NB, S, D), dx)
        dy, dg1 = _gate_res_bwd(dx1, st["y"], st["g1"])
        dy = dy.reshape(T, D)
        if i % 2 == 0:
            do = _matmul(dy, wo_a[j], "nt", BF16, "d_o_a")
            grads["wo_a"][j] = _matmul(st["o"], dy, "tn", F32, "d_wo_a")
            do5 = do.reshape(NB, S, N_KV_A, GROUP_A, HEAD_DIM).transpose(0, 2, 3, 1, 4)
            dq, dk, dv, dsink = _attn_a_bwd(st["q"], st["k"], st["v"], do5, sinks_a[j])
            d3 = jnp.concatenate([
                dq.transpose(0, 3, 1, 2, 4).reshape(T, GROUP_A, LANES),
                dk.transpose(0, 2, 1, 3).reshape(T, 1, LANES),
                dv.transpose(0, 2, 1, 3).reshape(T, 1, LANES)], axis=1)
            dqkv, dgain = _qk_prep_bwd(st["qkv3"], d3, tab_c, tab_s, st["gains"])
            dqkv = dqkv.reshape(T, rows_a * LANES)
            dh = _matmul(dqkv, wqkv_a[j], "nt", F32, "d_h_a")
            grads["wqkv_a"][j] = _matmul(st["h"], dqkv, "tn", F32, "d_wqkv_a")
            grads["q_norm_a"][j] = jnp.sum(dgain[:GROUP_A].reshape(2 * GROUP_A, HEAD_DIM), axis=0)
            grads["k_norm_a"][j] = jnp.sum(dgain[GROUP_A].reshape(2, HEAD_DIM), axis=0)
            grads["sinks_a"][j] = jnp.sum(dsink[..., 0], axis=0).reshape(N_Q_A)
        else:
            do = _matmul(dy, wo_b[j], "nt", BF16, "d_o_b").reshape(NB, S, -1)
            grads["wo_b"][j] = _matmul(st["o"], dy, "tn", F32, "d_wo_b")
            dq, dk, dv = _attn_b_bwd(st["qkv"], do, rev, fwd)
            dqkv = jnp.concatenate([dq, dk, dv], axis=-1).reshape(T, -1).astype(BF16)
            dh = _matmul(dqkv, wqkv_b[j], "nt", F32, "d_h_b")
            grads["wqkv_b"][j] = _matmul(st["h"], dqkv, "tn", F32, "d_wqkv_b")
        dx, dsh1, dsc1, grads["norm1_g"][i] = _norm_mod_bwd(
            st["x"], norm1_g[i:i + 1], st["sc1"], dh.reshape(NB, S, D), dx1)
        dmod[i] = jnp.concatenate([dsh1, dsc1, dg1, dsh2, dsc2, dg2], axis=-1).reshape(NB, 6 * D)

    grads = {name: jnp.stack(parts) for name, parts in grads.items()}
    return loss, dx, grads, jnp.stack(dmod)


def _rows_of(flat, cols=PACK_COLS):
    n = flat.shape[0]
    pad = (-n) % (8 * cols)
    if pad:
        flat = jnp.concatenate([flat, jnp.zeros((pad,), flat.dtype)])
    return flat.reshape(-1, cols)


def kernel(x, c, positions, ada_w, ada_b, norm1_g, norm2_g, wqkv_a, q_norm_a, k_norm_a, sinks_a, wo_a, wqkv_b, wo_b, w_gate, w_up, w_down, loss_target, m_ada_w, m_ada_b, m_norm1_g, m_norm2_g, m_wqkv_a, m_q_norm_a, m_k_norm_a, m_sinks_a, m_wo_a, m_wqkv_b, m_wo_b, m_w_gate, m_w_up, m_w_down, v_ada_w, v_ada_b, v_norm1_g, v_norm2_g, v_wqkv_a, v_q_norm_a, v_k_norm_a, v_sinks_a, v_wo_a, v_wqkv_b, v_wo_b, v_w_gate, v_w_up, v_w_down):
    xi, yi, ci = lax.axis_index("x"), lax.axis_index("y"), lax.axis_index("c")
    dev = 4 * xi + 2 * yi + ci
    chip = 2 * xi + yi
    NB, S, D = x.shape
    B_all = N_DEV * NB
    L = ada_w.shape[0]
    n_mod = ada_w.shape[2] // 2

    c_all = _all_gather8([_rows_of(c.reshape(-1), LANES)], "gather_c")[0].reshape(N_DEV, -1)[:, :NB * D].reshape(B_all, D)
    ada_w_half = lax.dynamic_slice_in_dim(ada_w, ci * n_mod, n_mod, axis=2)
    ada_b_half = lax.dynamic_slice_in_dim(ada_b, dev * n_mod, n_mod, axis=1).reshape(L, 1, n_mod)
    mod_part = _ada_fwd(c_all, ada_w_half, ada_b_half)
    n_part = L * B_all * n_mod
    mod_all = _all_gather8([_rows_of(mod_part.reshape(-1))], "gather_mod")[0].reshape(N_DEV, -1)[:, :n_part]
    mod_all = mod_all.reshape(N_DEV, L, B_all, n_mod).transpose(1, 2, 0, 3).reshape(L, B_all, N_DEV * n_mod)
    mod = lax.dynamic_slice_in_dim(mod_all, dev * NB, NB, axis=1)

    shards = dict(wqkv_a=wqkv_a, wo_a=wo_a, wqkv_b=wqkv_b, wo_b=wo_b, w_gate=w_gate, w_up=w_up, w_down=w_down)
    halves = []
    for name, _ in _SHARDED:
        w = shards[name]
        half = lax.dynamic_index_in_dim(w.reshape((2, w.shape[0] // 2) + w.shape[1:]), ci, 0, keepdims=False)
        halves.append(half.astype(BF16))
    gathered = _all_gather8(halves, "gather_weights")
    full = {name: _unpack_full(t, axis) for (name, axis), t in zip(_SHARDED, gathered)}
    wgu = jnp.concatenate([full["w_gate"], full["w_up"]], axis=2)

    loss, grad_x, g, dmod = _local_step(
        x, positions, mod, norm1_g, norm2_g, q_norm_a, k_norm_a, sinks_a,
        full["wqkv_a"], full["wo_a"], full["wqkv_b"], full["wo_b"], wgu, full["w_down"], loss_target)

    F = w_down.shape[1] * 4
    g_full = dict(wqkv_a=g["wqkv_a"], wo_a=g["wo_a"], wqkv_b=g["wqkv_b"], wo_b=g["wo_b"],
                  w_gate=g["wgu"][:, :, :F], w_up=g["wgu"][:, :, F:], w_down=g["wd"])
    packed = [_pack_full(g_full[name], axis) for name, axis in _SHARDED]
    from_cores = _exchange(packed, "c", "rs_cores", chunk_axis=0, chunks=4)
    chip_part = [_sum_slabs(t, "rs_add_cores") for t in from_cores]
    from_chips = _exchange(chip_part, "xy", "rs_chips")
    mine = [_sum_slabs(t, "rs_add_chips") for t in from_chips]
    both = _sibling_gather(mine, "rs_halves")
    grad = {name: t.reshape(shards[name].shape) for (name, _), t in zip(_SHARDED, both)}

    small_names = ("norm1_g", "norm2_g", "q_norm_a", "k_norm_a", "sinks_a")
    small = [dmod.reshape(-1)] + [g[name].reshape(-1) for name in small_names] + [loss.reshape(-1)]
    small_sizes = [t.shape[0] for t in small]
    small_rows = _rows_of(jnp.concatenate(small))
    small_all = _all_gather8([small_rows], "gather_small")[0]
    small_sum = _sum_leading(small_all, "sum_small").reshape(-1)
    n_dmod = small_sizes[0]
    dmod_all = small_all.reshape(N_DEV, -1)[:, :n_dmod].reshape(N_DEV, L, NB, 6 * D)
    dmod_all = dmod_all.transpose(1, 0, 2, 3).reshape(L, B_all, 6 * D)
    off = n_dmod
    for name, sz in zip(small_names + ("loss",), small_sizes[1:]):
        grad[name] = small_sum[off:off + sz]
        off += sz
    loss_total = grad.pop("loss").reshape(())
    for name, ref in (("norm1_g", norm1_g), ("norm2_g", norm2_g), ("q_norm_a", q_norm_a),
                      ("k_norm_a", k_norm_a), ("sinks_a", sinks_a)):
        grad[name] = grad[name].reshape(ref.shape)

    n_shard = ada_w.shape[2]
    dmod_shard = lax.dynamic_slice_in_dim(dmod_all, chip * n_shard, n_shard, axis=2)
    grad["ada_w"], gb = _ada_bwd(c_all, dmod_all, dmod_shard)
    grad["ada_b"] = gb.reshape(ada_b.shape)

    weights = dict(ada_w=ada_w, ada_b=ada_b, norm1_g=norm1_g, norm2_g=norm2_g, wqkv_a=wqkv_a, q_norm_a=q_norm_a,
                   k_norm_a=k_norm_a, sinks_a=sinks_a, wo_a=wo_a, wqkv_b=wqkv_b, wo_b=wo_b, w_gate=w_gate,
                   w_up=w_up, w_down=w_down)
    m_in = dict(ada_w=m_ada_w, ada_b=m_ada_b, norm1_g=m_norm1_g, norm2_g=m_norm2_g, wqkv_a=m_wqkv_a,
                q_norm_a=m_q_norm_a, k_norm_a=m_k_norm_a, sinks_a=m_sinks_a, wo_a=m_wo_a, wqkv_b=m_wqkv_b,
                wo_b=m_wo_b, w_gate=m_w_gate, w_up=m_w_up, w_down=m_w_down)
    v_in = dict(ada_w=v_ada_w, ada_b=v_ada_b, norm1_g=v_norm1_g, norm2_g=v_norm2_g, wqkv_a=v_wqkv_a,
                q_norm_a=v_q_norm_a, k_norm_a=v_k_norm_a, sinks_a=v_sinks_a, wo_a=v_wo_a, wqkv_b=v_wqkv_b,
                wo_b=v_wo_b, w_gate=v_w_gate, w_up=v_w_up, w_down=v_w_down)
    names = list(weights)
    delta, new_m, new_v = {}, {}, {}
    for name in names:
        delta[name], new_m[name], new_v[name] = _adamw(weights[name], grad[name], m_in[name], v_in[name],
                                                       "adamw_" + name)
    return (loss_total, grad_x, *[grad[k] for k in names], *[delta[k] for k in names],
            *[new_m[k] for k in names], *[new_v[k] for k in names])
```

```python
import jax
import jax.numpy as jnp
from jax import lax
from jax.experimental import pallas as pl
from jax.experimental.pallas import tpu as pltpu

F32 = jnp.float32
BF16 = jnp.bfloat16

DEPTH = 4
HEAD_DIM = 64
N_Q_A = 16
N_KV_A = 2
GROUP_A = N_Q_A // N_KV_A
N_H_B = 16
BLOCK = 128
ROT_DIM = HEAD_DIM // 4
ROPE_THETA = 500000.0
EPS = 1e-6
ATTN_SCALE = HEAD_DIM ** -0.5
NEG_BIG = -1e30

ADAM_LR = 0.001
ADAM_B1 = 0.9
ADAM_B2 = 0.999
ADAM_EPS = 1e-08
ADAM_WD = 0.01
ADAM_STEP = 10

N_DEV = 8
LANES = 128
PACK_COLS = 1024
VMEM_LIMIT_BYTES = 48 * 1024 * 1024
MESH = pl.DeviceIdType.MESH

_NT = (((1,), (1,)), ((), ()))
_TN = (((0,), (0,)), ((), ()))
_NN = (((1,), (0,)), ((), ()))


def _params(sem=None):
    return pltpu.CompilerParams(vmem_limit_bytes=VMEM_LIMIT_BYTES, dimension_semantics=sem)


def _pick(n, cap, mult):
    best = None
    for t in range(mult, min(n, cap) + 1, mult):
        if n % t == 0:
            best = t
    return n if best is None else best


_ANY = pl.BlockSpec(memory_space=pl.ANY)


def _window(index, axis, q, n, shape):
    rest = [slice(None)] * len(shape)
    size = shape[axis] // n
    rest[axis] = pl.ds(q * size, size)
    return tuple(index) + tuple(rest)


def _all_gather8(xs, name):
    n = len(xs)

    def body(*refs):
        x_refs, out_refs = refs[:n], refs[n:2 * n]
        send_sems, recv_sems = refs[2 * n:]
        xi, yi, ci = lax.axis_index("x"), lax.axis_index("y"), lax.axis_index("c")
        me, sibling = (xi, yi, ci), (xi, yi, 1 - ci)
        chips = [(1 - xi, yi), (xi, 1 - yi), (1 - xi, 1 - yi)]

        def slab(w, px, py, pc):
            return out_refs[w].at[4 * px + 2 * py + pc]

        def copy(w, k, block, to, src=None):
            return pltpu.make_async_remote_copy(
                src_ref=slab(w, *block) if src is None else src, dst_ref=slab(w, *block),
                send_sem=send_sems.at[k, w], recv_sem=recv_sems.at[k, w], device_id=to, device_id_type=MESH)

        first = [copy(w, 0, me, sibling, src=x_refs[w]) for w in range(n)]
        first += [copy(w, 1 + j, me, (*chip, ci), src=x_refs[w]) for j, chip in enumerate(chips) for w in range(n)]
        for cp in first:
            cp.start()
        passed = []
        for j, chip in enumerate(chips):
            for w in range(n):
                copy(w, 1 + j, (*chip, ci), me).wait_recv()
                passed.append(copy(w, 4 + j, (*chip, ci), sibling))
                passed[-1].start()
        for w in range(n):
            copy(w, 0, sibling, me).wait_recv()
        for j, chip in enumerate(chips):
            for w in range(n):
                copy(w, 4 + j, (*chip, 1 - ci), me).wait_recv()
        for cp in first + passed:
            cp.wait_send()

    return pl.pallas_call(
        body, name=name,
        out_shape=[jax.ShapeDtypeStruct((N_DEV,) + x.shape, x.dtype) for x in xs],
        in_specs=[_ANY] * n, out_specs=[_ANY] * n,
        scratch_shapes=[pltpu.SemaphoreType.DMA((7, n)), pltpu.SemaphoreType.DMA((7, n))],
    )(*xs)


def _with_own_slab(gathered, own, dev):
    return lax.dynamic_update_index_in_dim(gathered, own, dev, 0)


def _exchange(xs, group, name, chunk_axis=0, chunks=1):
    n = len(xs)
    n_peers = 1 if group == "c" else 3

    def body(*refs):
        x_refs, out_refs = refs[:n], refs[n:2 * n]
        send_sems, recv_sems = refs[2 * n:]
        xi, yi, ci = lax.axis_index("x"), lax.axis_index("y"), lax.axis_index("c")
        if group == "c":
            peers = [(1 - ci, (xi, yi, 1 - ci))]
        else:
            peers = [(2 * (1 - xi) + yi, (1 - xi, yi, ci)),
                     (2 * xi + (1 - yi), (xi, 1 - yi, ci)),
                     (2 * (1 - xi) + (1 - yi), (1 - xi, 1 - yi, ci))]
        copies = []
        for k, (p, dev) in enumerate(peers):
            for w in range(n):
                slab_shape = xs[w].shape[1:]
                for q in range(chunks):
                    copies.append(pltpu.make_async_remote_copy(
                        src_ref=x_refs[w].at[_window((p,), chunk_axis, q, chunks, slab_shape)],
                        dst_ref=out_refs[w].at[_window((k,), chunk_axis, q, chunks, slab_shape)],
                        send_sem=send_sems.at[k, w, q], recv_sem=recv_sems.at[k, w, q],
                        device_id=dev, device_id_type=MESH))
                    copies[-1].start()
        for cp in copies:
            cp.wait()

    return pl.pallas_call(
        body, name=name,
        out_shape=[jax.ShapeDtypeStruct((n_peers,) + x.shape[1:], x.dtype) for x in xs],
        in_specs=[_ANY] * n, out_specs=[_ANY] * n,
        scratch_shapes=[pltpu.SemaphoreType.DMA((n_peers, n, chunks)), pltpu.SemaphoreType.DMA((n_peers, n, chunks))],
    )(*xs)


def _sibling_send(xs, name, chunk_axis=1, chunks=4):
    n = len(xs)

    def body(*refs):
        x_refs, out_refs = refs[:n], refs[n:2 * n]
        send_sems, recv_sems = refs[2 * n:]
        xi, yi, ci = lax.axis_index("x"), lax.axis_index("y"), lax.axis_index("c")
        copies = []
        for w in range(n):
            for q in range(chunks):
                part = _window((), chunk_axis, q, chunks, xs[w].shape)
                copies.append(pltpu.make_async_remote_copy(
                    src_ref=x_refs[w].at[part], dst_ref=out_refs[w].at[part],
                    send_sem=send_sems.at[w, q], recv_sem=recv_sems.at[w, q],
                    device_id=(xi, yi, 1 - ci), device_id_type=MESH))
                copies[-1].start()
        for cp in copies:
            cp.wait()

    return pl.pallas_call(
        body, name=name,
        out_shape=[jax.ShapeDtypeStruct(x.shape, x.dtype) for x in xs],
        in_specs=[_ANY] * n, out_specs=[_ANY] * n,
        scratch_shapes=[pltpu.SemaphoreType.DMA((n, chunks)), pltpu.SemaphoreType.DMA((n, chunks))],
    )(*xs)


def _sum_leading(x, name, own=None):
    P, R, C = x.shape
    tr = _pick(R, max(8, (1 << 19) // (C * (P + 1))), 8)

    def body(*refs):
        x_ref, o_ref = refs[-2], refs[-1]
        acc = x_ref[0] if own is None else refs[0][...] + x_ref[0]
        for p in range(1, P):
            acc = acc + x_ref[p]
        o_ref[...] = acc

    flat = pl.BlockSpec((tr, C), lambda r: (r, 0))
    slabs = pl.BlockSpec((P, tr, C), lambda r: (0, r, 0))
    return pl.pallas_call(
        body, name=name, grid=(R // tr,),
        in_specs=[slabs] if own is None else [flat, slabs],
        out_specs=flat, out_shape=jax.ShapeDtypeStruct((R, C), F32),
        compiler_params=_params(("arbitrary",)),
    )(*([x] if own is None else [own, x]))


def _matmul(a, b, mode, out_dtype, name):
    if mode == "nn":
        (M, K), N = a.shape, b.shape[1]
    elif mode == "nt":
        (M, K), N = a.shape, b.shape[0]
    else:
        (K, M), N = a.shape, b.shape[1]
    tm = _pick(M, 1024 if mode != "tn" else 1536, 128)
    tn = _pick(N, 1536, 128)
    tk = _pick(K, 512, 128)
    nk = K // tk
    dims = {"nn": _NN, "nt": _NT, "tn": _TN}[mode]

    def body(a_ref, b_ref, o_ref, acc_ref):
        k = pl.program_id(2)

        @pl.when(k == 0)
        def _():
            acc_ref[...] = jnp.zeros_like(acc_ref)

        acc_ref[...] += lax.dot_general(a_ref[...].astype(BF16), b_ref[...].astype(BF16), dims,
                                        preferred_element_type=F32)

        @pl.when(k == nk - 1)
        def _():
            o_ref[...] = acc_ref[...].astype(o_ref.dtype)

    if mode == "tn":
        a_spec = pl.BlockSpec((tk, tm), lambda i, j, k: (k, i))
    else:
        a_spec = pl.BlockSpec((tm, tk), lambda i, j, k: (i, k))
    if mode == "nt":
        b_spec = pl.BlockSpec((tn, tk), lambda i, j, k: (j, k))
    else:
        b_spec = pl.BlockSpec((tk, tn), lambda i, j, k: (k, j))
    return pl.pallas_call(
        body, name=name, grid=(M // tm, N // tn, nk),
        in_specs=[a_spec, b_spec],
        out_specs=pl.BlockSpec((tm, tn), lambda i, j, k: (i, j)),
        out_shape=jax.ShapeDtypeStruct((M, N), out_dtype),
        scratch_shapes=[pltpu.VMEM((tm, tn), F32)],
        compiler_params=_params(("parallel", "parallel", "arbitrary")),
    )(a, b)


def _row_tile(S):
    return _pick(S, 512, 8)


def _norm_mod_fwd(x, gain, sc, sh):
    NB, S, D = x.shape
    tr = _row_tile(S)

    def body(x_ref, g_ref, sc_ref, sh_ref, h_ref):
        xv = x_ref[...]
        ms = jnp.mean(xv * xv, axis=-1, keepdims=True)
        n = xv * lax.rsqrt(ms + EPS) * g_ref[...]
        h_ref[...] = (n * (1.0 + sc_ref[...]) + sh_ref[...]).astype(BF16)

    tok = pl.BlockSpec((None, tr, D), lambda b, r: (b, r, 0))
    per_ex = pl.BlockSpec((None, 1, D), lambda b, r: (b, 0, 0))
    return pl.pallas_call(
        body, name="norm_mod_fwd", grid=(NB, S // tr),
        in_specs=[tok, pl.BlockSpec((1, D), lambda b, r: (0, 0)), per_ex, per_ex],
        out_specs=tok, out_shape=jax.ShapeDtypeStruct((NB, S, D), BF16),
        compiler_params=_params(("parallel", "parallel")),
    )(x, gain, sc, sh)


def _norm_mod_bwd(x, gain, sc, dh, dres):
    NB, S, D = x.shape
    tr = _row_tile(S)

    def body(x_ref, g_ref, sc_ref, dh_ref, dres_ref, dx_ref, dsh_ref, dsc_ref, dg_ref):
        b, r = pl.program_id(0), pl.program_id(1)

        @pl.when(r == 0)
        def _():
            dsh_ref[...] = jnp.zeros_like(dsh_ref)
            dsc_ref[...] = jnp.zeros_like(dsc_ref)

        @pl.when((r == 0) & (b == 0))
        def _():
            dg_ref[...] = jnp.zeros_like(dg_ref)

        xv = x_ref[...]
        rstd = lax.rsqrt(jnp.mean(xv * xv, axis=-1, keepdims=True) + EPS)
        xh = xv * rstd
        g = g_ref[...]
        dh = dh_ref[...]
        dsh_ref[...] += jnp.sum(dh, axis=0, keepdims=True)
        dsc_ref[...] += jnp.sum(dh * (xh * g), axis=0, keepdims=True)
        dn = dh * (1.0 + sc_ref[...])
        dg_ref[...] += jnp.sum(dn * xh, axis=0, keepdims=True)
        dxh = dn * g
        proj = jnp.mean(dxh * xh, axis=-1, keepdims=True)
        dx_ref[...] = rstd * (dxh - xh * proj) + dres_ref[...]

    tok = pl.BlockSpec((None, tr, D), lambda b, r: (b, r, 0))
    per_ex = pl.BlockSpec((None, 1, D), lambda b, r: (b, 0, 0))
    row = pl.BlockSpec((1, D), lambda b, r: (0, 0))
    return pl.pallas_call(
        body, name="norm_mod_bwd", grid=(NB, S // tr),
        in_specs=[tok, row, per_ex, tok, tok],
        out_specs=[tok, per_ex, per_ex, row],
        out_shape=[jax.ShapeDtypeStruct((NB, S, D), F32), jax.ShapeDtypeStruct((NB, 1, D), F32),
                   jax.ShapeDtypeStruct((NB, 1, D), F32), jax.ShapeDtypeStruct((1, D), F32)],
        compiler_params=_params(("arbitrary", "arbitrary")),
    )(x, gain, sc, dh, dres)


def _gate_res(x, y, g):
    NB, S, D = x.shape
    tr = _row_tile(S)

    def body(x_ref, y_ref, g_ref, o_ref):
        o_ref[...] = x_ref[...] + g_ref[...] * y_ref[...]

    tok = pl.BlockSpec((None, tr, D), lambda b, r: (b, r, 0))
    per_ex = pl.BlockSpec((None, 1, D), lambda b, r: (b, 0, 0))
    return pl.pallas_call(
        body, name="gate_res", grid=(NB, S // tr), in_specs=[tok, tok, per_ex], out_specs=tok,
        out_shape=jax.ShapeDtypeStruct((NB, S, D), F32),
        compiler_params=_params(("parallel", "parallel")),
    )(x, y, g)


def _gate_res_bwd(dxo, y, g):
    NB, S, D = dxo.shape
    tr = _row_tile(S)

    def body(d_ref, y_ref, g_ref, dy_ref, dg_ref):
        @pl.when(pl.program_id(1) == 0)
        def _():
            dg_ref[...] = jnp.zeros_like(dg_ref)

        d = d_ref[...]
        dy_ref[...] = (d * g_ref[...]).astype(BF16)
        dg_ref[...] += jnp.sum(d * y_ref[...], axis=0, keepdims=True)

    tok = pl.BlockSpec((None, tr, D), lambda b, r: (b, r, 0))
    per_ex = pl.BlockSpec((None, 1, D), lambda b, r: (b, 0, 0))
    return pl.pallas_call(
        body, name="gate_res_bwd", grid=(NB, S // tr), in_specs=[tok, tok, per_ex], out_specs=[tok, per_ex],
        out_shape=[jax.ShapeDtypeStruct((NB, S, D), BF16), jax.ShapeDtypeStruct((NB, 1, D), F32)],
        compiler_params=_params(("arbitrary", "arbitrary")),
    )(dxo, y, g)


def _sigmoid(v):
    return 1.0 / (1.0 + jnp.exp(-v))


def _ff_tile(F):
    return _pick(F, 1536, 128)


def _interleave(gate, up):
    F = gate.shape[-1]
    tf = _ff_tile(F)
    lead = gate.shape[:-1]
    parts = jnp.stack([gate.reshape(lead + (F // tf, tf)), up.reshape(lead + (F // tf, tf))], axis=-2)
    return parts.reshape(lead + (2 * F,))


def _deinterleave(gu):
    F = gu.shape[-1] // 2
    tf = _ff_tile(F)
    lead = gu.shape[:-1]
    parts = gu.reshape(lead + (F // tf, 2, tf))
    return parts[..., 0, :].reshape(lead + (F,)), parts[..., 1, :].reshape(lead + (F,))


def _swiglu_fwd(gu):
    T, F2 = gu.shape
    F = F2 // 2
    tf = _ff_tile(F)
    tr = _pick(T, 256, 8)

    def body(gu_ref, o_ref):
        g = gu_ref[:, :tf]
        o_ref[...] = (g * _sigmoid(g) * gu_ref[:, tf:]).astype(BF16)

    return pl.pallas_call(
        body, name="swiglu_fwd", grid=(T // tr, F // tf),
        in_specs=[pl.BlockSpec((tr, 2 * tf), lambda i, j: (i, j))],
        out_specs=pl.BlockSpec((tr, tf), lambda i, j: (i, j)),
        out_shape=jax.ShapeDtypeStruct((T, F), BF16),
        compiler_params=_params(("parallel", "parallel")),
    )(gu)


def _swiglu_bwd(gu, dact):
    T, F2 = gu.shape
    F = F2 // 2
    tf = _ff_tile(F)
    tr = _pick(T, 256, 8)

    def body(gu_ref, d_ref, o_ref):
        g, u, d = gu_ref[:, :tf], gu_ref[:, tf:], d_ref[...]
        s = _sigmoid(g)
        o_ref[:, :tf] = (d * u * (s * (1.0 + g * (1.0 - s)))).astype(BF16)
        o_ref[:, tf:] = (d * (g * s)).astype(BF16)

    return pl.pallas_call(
        body, name="swiglu_bwd", grid=(T // tr, F // tf),
        in_specs=[pl.BlockSpec((tr, 2 * tf), lambda i, j: (i, j)), pl.BlockSpec((tr, tf), lambda i, j: (i, j))],
        out_specs=pl.BlockSpec((tr, 2 * tf), lambda i, j: (i, j)),
        out_shape=jax.ShapeDtypeStruct((T, F2), BF16),
        compiler_params=_params(("parallel", "parallel")),
    )(gu, dact)


def _loss_fwd_bwd(y, target):
    NB, S, D = y.shape
    tr = _row_tile(S)

    def body(y_ref, t_ref, l_ref, d_ref):
        @pl.when((pl.program_id(0) == 0) & (pl.program_id(1) == 0))
        def _():
            l_ref[...] = jnp.zeros_like(l_ref)

        e = y_ref[...] - t_ref[...]
        d_ref[...] = e / D
        l_ref[...] += 0.5 * jnp.sum(jnp.mean(e * e, axis=-1, keepdims=True), axis=0, keepdims=True)

    tok = pl.BlockSpec((None, tr, D), lambda b, r: (b, r, 0))
    return pl.pallas_call(
        body, name="loss", grid=(NB, S // tr), in_specs=[tok, tok],
        out_specs=[pl.BlockSpec((1, 1), lambda b, r: (0, 0)), tok],
        out_shape=[jax.ShapeDtypeStruct((1, 1), F32), jax.ShapeDtypeStruct((NB, S, D), F32)],
        compiler_params=_params(("arbitrary", "arbitrary")),
    )(y, target)


def _half_sums(v, lo):
    sa = jnp.sum(jnp.where(lo, v, 0.0), axis=-1, keepdims=True)
    sb = jnp.sum(jnp.where(lo, 0.0, v), axis=-1, keepdims=True)
    return jnp.where(lo, sa, sb)


def _rope_swap(v, lane64):
    up = pltpu.roll(v, LANES - ROT_DIM // 2, 2)
    down = pltpu.roll(v, ROT_DIM // 2, 2)
    return jnp.where(lane64 < ROT_DIM // 2, up, jnp.where(lane64 < ROT_DIM, down, 0.0))


def _qk_prep_fwd(qkv3, tab_c, tab_s, gains):
    T = qkv3.shape[0]
    R = qkv3.shape[1]
    tt = _pick(T, 256, 8)

    def body(x_ref, c_ref, s_ref, g_ref, o_ref):
        xv = x_ref[...]
        lane = lax.broadcasted_iota(jnp.int32, xv.shape, 2)
        row = lax.broadcasted_iota(jnp.int32, xv.shape, 1)
        lo = lane < HEAD_DIM
        rstd = lax.rsqrt(_half_sums(xv * xv, lo) / HEAD_DIM + EPS)
        yn = xv * rstd * g_ref[...]
        roped = yn * c_ref[...] + _rope_swap(yn, lane & (HEAD_DIM - 1)) * s_ref[...]
        o_ref[...] = jnp.where(row == R - 1, xv, roped).astype(BF16)

    tok = pl.BlockSpec((tt, R, LANES), lambda t: (t, 0, 0))
    tab = pl.BlockSpec((tt, 1, LANES), lambda t: (t, 0, 0))
    return pl.pallas_call(
        body, name="qk_prep_fwd", grid=(T // tt,),
        in_specs=[tok, tab, tab, pl.BlockSpec((R, LANES), lambda t: (0, 0))],
        out_specs=tok, out_shape=jax.ShapeDtypeStruct(qkv3.shape, BF16),
        compiler_params=_params(("parallel",)),
    )(qkv3, tab_c, tab_s, gains)


def _qk_prep_bwd(qkv3, d3, tab_c, tab_s, gains):
    T = qkv3.shape[0]
    R = qkv3.shape[1]
    tt = _pick(T, 256, 8)

    def body(x_ref, d_ref, c_ref, s_ref, g_ref, o_ref, dg_ref):
        @pl.when(pl.program_id(0) == 0)
        def _():
            dg_ref[...] = jnp.zeros_like(dg_ref)

        xv, d = x_ref[...], d_ref[...]
        lane = lax.broadcasted_iota(jnp.int32, xv.shape, 2)
        row = lax.broadcasted_iota(jnp.int32, xv.shape, 1)
        lo = lane < HEAD_DIM
        rstd = lax.rsqrt(_half_sums(xv * xv, lo) / HEAD_DIM + EPS)
        xh = xv * rstd
        dyn = d * c_ref[...] + _rope_swap(d * s_ref[...], lane & (HEAD_DIM - 1))
        dg_ref[...] += jnp.sum(dyn * xh, axis=0)
        dxh = dyn * g_ref[...]
        proj = _half_sums(dxh * xh, lo) / HEAD_DIM
        dx = rstd * (dxh - xh * proj)
        o_ref[...] = jnp.where(row == R - 1, d, dx).astype(BF16)

    tok = pl.BlockSpec((tt, R, LANES), lambda t: (t, 0, 0))
    tab = pl.BlockSpec((tt, 1, LANES), lambda t: (t, 0, 0))
    gsp = pl.BlockSpec((R, LANES), lambda t: (0, 0))
    return pl.pallas_call(
        body, name="qk_prep_bwd", grid=(T // tt,),
        in_specs=[tok, tok, tab, tab, gsp], out_specs=[tok, gsp],
        out_shape=[jax.ShapeDtypeStruct(qkv3.shape, BF16), jax.ShapeDtypeStruct((R, LANES), F32)],
        compiler_params=_params(("arbitrary",)),
    )(qkv3, d3, tab_c, tab_s, gains)


def _band_mask(i):
    r = lax.broadcasted_iota(jnp.int32, (BLOCK, 2 * BLOCK), 0)
    c = lax.broadcasted_iota(jnp.int32, (BLOCK, 2 * BLOCK), 1)
    rel = r + BLOCK - c
    return (rel >= 0) & (rel < BLOCK) & ((c >= BLOCK) | (i > 0))


def _swa_probs(qg, k2, valid, sink):
    s = lax.dot_general(qg, k2, _NT, preferred_element_type=F32) * ATTN_SCALE
    s = jnp.where(valid, s, NEG_BIG)
    m = jnp.maximum(jnp.max(s, axis=1, keepdims=True), sink)
    p = jnp.exp(s - m)
    ps = jnp.exp(sink - m)
    denom = jnp.sum(p, axis=1, keepdims=True) + ps
    return p / denom, ps / denom


def _swa_specs(S):
    qs = pl.BlockSpec((None, None, GROUP_A, BLOCK, HEAD_DIM), lambda b, h, i: (b, h, 0, i, 0))
    prev = pl.BlockSpec((None, None, BLOCK, HEAD_DIM), lambda b, h, i: (b, h, jnp.maximum(i - 1, 0), 0))
    cur = pl.BlockSpec((None, None, BLOCK, HEAD_DIM), lambda b, h, i: (b, h, i, 0))
    return qs, prev, cur


def _attn_a_fwd(q, k, v, sinks):
    NB, _, _, S, _ = q.shape
    qs, prev, cur = _swa_specs(S)

    def body(q_ref, kp_ref, kc_ref, vp_ref, vc_ref, sink_ref, o_ref):
        h, i = pl.program_id(1), pl.program_id(2)
        k2 = jnp.concatenate([kp_ref[...], kc_ref[...]], axis=0)
        v2 = jnp.concatenate([vp_ref[...], vc_ref[...]], axis=0)
        valid = _band_mask(i)
        for g in range(GROUP_A):
            pn, _ = _swa_probs(q_ref[g], k2, valid, sink_ref[h * GROUP_A + g])
            o_ref[g] = jnp.dot(pn.astype(BF16), v2, preferred_element_type=F32).astype(BF16)

    return pl.pallas_call(
        body, name="attn_a_fwd", grid=(NB, N_KV_A, S // BLOCK),
        in_specs=[qs, prev, cur, prev, cur, pl.BlockSpec(memory_space=pltpu.SMEM)],
        out_specs=qs, out_shape=jax.ShapeDtypeStruct(q.shape, BF16),
        compiler_params=_params(("parallel", "parallel", "arbitrary")),
    )(q, k, k, v, v, sinks)


def _attn_a_bwd(q, k, v, do, sinks):
    NB, _, _, S, _ = q.shape
    qs, prev, cur = _swa_specs(S)
    full = pl.BlockSpec((None, None, S, HEAD_DIM), lambda b, h, i: (b, h, 0, 0))
    sink_out = pl.BlockSpec((None, None, GROUP_A, LANES), lambda b, h, i: (b, h, 0, 0))

    def body(q_ref, do_ref, kp_ref, kc_ref, vp_ref, vc_ref, sink_ref, dq_ref, dk_ref, dv_ref, ds_ref):
        h, i = pl.program_id(1), pl.program_id(2)

        @pl.when(i == 0)
        def _():
            dk_ref[...] = jnp.zeros_like(dk_ref)
            dv_ref[...] = jnp.zeros_like(dv_ref)
            ds_ref[...] = jnp.zeros_like(ds_ref)

        k2 = jnp.concatenate([kp_ref[...], kc_ref[...]], axis=0)
        v2 = jnp.concatenate([vp_ref[...], vc_ref[...]], axis=0)
        valid = _band_mask(i)
        dk2 = jnp.zeros((2 * BLOCK, HEAD_DIM), F32)
        dv2 = jnp.zeros((2 * BLOCK, HEAD_DIM), F32)
        for g in range(GROUP_A):
            qg, dog = q_ref[g], do_ref[g]
            pn, psink = _swa_probs(qg, k2, valid, sink_ref[h * GROUP_A + g])
            dp = lax.dot_general(dog, v2, _NT, preferred_element_type=F32)
            delta = jnp.sum(pn * dp, axis=1, keepdims=True)
            dsb = (pn * (dp - delta) * ATTN_SCALE).astype(BF16)
            dq_ref[g] = jnp.dot(dsb, k2, preferred_element_type=F32)
            dk2 = dk2 + lax.dot_general(dsb, qg, _TN, preferred_element_type=F32)
            dv2 = dv2 + lax.dot_general(pn.astype(BF16), dog, _TN, preferred_element_type=F32)
            dsink = -jnp.sum(psink * delta, axis=0, keepdims=True)
            ds_ref[g:g + 1, :] += jnp.broadcast_to(dsink, (1, LANES))

        @pl.when(i > 0)
        def _():
            start = pl.multiple_of((i - 1) * BLOCK, BLOCK)
            dk_ref[pl.ds(start, 2 * BLOCK), :] += dk2
            dv_ref[pl.ds(start, 2 * BLOCK), :] += dv2

        @pl.when(i == 0)
        def _():
            dk_ref[0:BLOCK, :] += dk2[BLOCK:, :]
            dv_ref[0:BLOCK, :] += dv2[BLOCK:, :]

    return pl.pallas_call(
        body, name="attn_a_bwd", grid=(NB, N_KV_A, S // BLOCK),
        in_specs=[qs, qs, prev, cur, prev, cur, pl.BlockSpec(memory_space=pltpu.SMEM)],
        out_specs=[qs, full, full, sink_out],
        out_shape=[jax.ShapeDtypeStruct(q.shape, F32), jax.ShapeDtypeStruct(k.shape, F32),
                   jax.ShapeDtypeStruct(k.shape, F32), jax.ShapeDtypeStruct((NB, N_KV_A, GROUP_A, LANES), F32)],
        compiler_params=_params(("parallel", "parallel", "arbitrary")),
    )(q, do, k, k, v, v, sinks)


def _cumsum_mats():
    src = lax.broadcasted_iota(jnp.int32, (2 * BLOCK, 2 * BLOCK), 0) % BLOCK
    dst = lax.broadcasted_iota(jnp.int32, (2 * BLOCK, 2 * BLOCK), 1)
    ones = dst >= BLOCK
    rev = ((src > dst) | ones).astype(BF16)
    fwd = ((src < dst) | ones).astype(BF16)
    return rev, fwd


def _cumsum_mxu(v, mat):
    hi = v.astype(BF16)
    lo = (v - hi.astype(F32)).astype(BF16)
    r = jnp.dot(jnp.concatenate([hi, lo], axis=1), mat, preferred_element_type=F32)
    return r[:, :BLOCK], r[:, BLOCK:]


def _sb_logs(qv, kj):
    z = lax.dot_general(qv, kj, _NT, preferred_element_type=F32) * ATTN_SCALE
    sp = jnp.log1p(jnp.exp(-jnp.abs(z)))
    return jnp.minimum(z, 0.0) - sp, -(jnp.maximum(z, 0.0) + sp)


def _strict_mask():
    r = lax.broadcasted_iota(jnp.int32, (BLOCK, BLOCK), 0)
    c = lax.broadcasted_iota(jnp.int32, (BLOCK, BLOCK), 1)
    return c < r


def _tile(ref, j):
    return ref[pl.ds(pl.multiple_of(j * BLOCK, BLOCK), BLOCK), :]


SWEEP_EXIT = -110.0


def _head_halves(t, lo):
    zero = jnp.zeros_like(t)
    return jnp.where(lo, t, zero), jnp.where(lo, zero, t)


def _sb_specs(S, HD, width):
    n = HD // width
    blk = pl.BlockSpec((None, BLOCK, width), lambda b, p, i: (b, i, p))
    k_full = pl.BlockSpec((None, S, width), lambda b, p, i: (b, 0, n + p))
    v_full = pl.BlockSpec((None, S, width), lambda b, p, i: (b, 0, 2 * n + p))
    mat = pl.BlockSpec((2 * BLOCK, 2 * BLOCK), lambda b, p, i: (0, 0))
    return blk, k_full, v_full, mat


SB_FWD_PAIRS = 2


def _attn_b_fwd(qkv, rev):
    NB, S, W = qkv.shape
    HD = W // 3
    width = SB_FWD_PAIRS * LANES
    n_heads = 2 * SB_FWD_PAIRS
    blk, k_full, v_full, mat = _sb_specs(S, HD, width)

    def body(q_ref, k_ref, v_ref, rev_ref, o_ref):
        i = pl.program_id(2)
        rv = rev_ref[...]
        mask = _strict_mask()
        lo = lax.broadcasted_iota(jnp.int32, (BLOCK, LANES), 1) < HEAD_DIM
        q_all = q_ref[...]
        qh = []
        for p in range(SB_FWD_PAIRS):
            qh.extend(_head_halves(q_all[:, p * LANES:(p + 1) * LANES], lo))

        def pair_tiles(ref, j):
            t = _tile(ref, j)
            return [t[:, p * LANES:(p + 1) * LANES] for p in range(SB_FWD_PAIRS)]

        def merge(outs):
            return [jnp.where(lo, outs[2 * p], outs[2 * p + 1]) for p in range(SB_FWD_PAIRS)]

        ks, vs = pair_tiles(k_ref, i), pair_tiles(v_ref, i)
        carries, outs = [], []
        for h in range(n_heads):
            lb, lm = _sb_logs(qh[h], ks[h // 2])
            after, rs = _cumsum_mxu(jnp.where(mask, lm, 0.0), rv)
            a = jnp.where(mask, jnp.exp(lb + after), 0.0)
            outs.append(jnp.dot(a.astype(BF16), vs[h // 2], preferred_element_type=F32))
            carries.append(rs)
        accs = merge(outs)

        def live(cs):
            top = cs[0]
            for c in cs[1:]:
                top = jnp.maximum(top, c)
            return jnp.max(top) > SWEEP_EXIT

        def cond(st):
            return (st[0] < i) & st[1]

        def step(st):
            jj, _, cs, accs = st
            j = i - 1 - jj
            ks, vs = pair_tiles(k_ref, j), pair_tiles(v_ref, j)
            new_c, outs = [], []
            for h in range(n_heads):
                lb, lm = _sb_logs(qh[h], ks[h // 2])
                after, rs = _cumsum_mxu(lm, rv)
                a = jnp.exp(lb + after + cs[h])
                outs.append(jnp.dot(a.astype(BF16), vs[h // 2], preferred_element_type=F32))
                new_c.append(cs[h] + rs)
            accs = [acc + o for acc, o in zip(accs, merge(outs))]
            return jj + 1, live(new_c), new_c, accs

        st = lax.while_loop(cond, step, (jnp.int32(0), live(carries), carries, accs))
        for p in range(SB_FWD_PAIRS):
            o_ref[:, p * LANES:(p + 1) * LANES] = st[3][p].astype(BF16)

    return pl.pallas_call(
        body, name="attn_b_fwd", grid=(NB, HD // width, S // BLOCK),
        in_specs=[blk, k_full, v_full, mat], out_specs=blk,
        out_shape=jax.ShapeDtypeStruct((NB, S, HD), BF16),
        compiler_params=_params(("parallel", "parallel", "arbitrary")),
    )(qkv, qkv, qkv, rev)


def _attn_b_bwd(qkv, do, rev, fwd):
    NB, S, W = qkv.shape
    HD = W // 3
    n_pair = HD // LANES
    nj = S // BLOCK
    blk, k_full, v_full, mat = _sb_specs(S, HD, LANES)
    acc_full = pl.BlockSpec((None, S, LANES), lambda b, p, i: (b, 0, p))

    def body(q_ref, do_ref, k_ref, v_ref, rev_ref, fwd_ref, dq_ref, dk_ref, dv_ref, sig_s, a_s, e_s):
        i = pl.program_id(2)

        @pl.when(i == 0)
        def _():
            dk_ref[...] = jnp.zeros_like(dk_ref)
            dv_ref[...] = jnp.zeros_like(dv_ref)

        q2, do2 = q_ref[...], do_ref[...]
        rv, fw = rev_ref[...], fwd_ref[...]
        mask = _strict_mask()
        lo = lax.broadcasted_iota(jnp.int32, (BLOCK, LANES), 1) < HEAD_DIM
        qh = _head_halves(q2, lo)
        doh = _head_halves(do2, lo)

        def stash(h, j, vj, lb, a):
            da = lax.dot_general(doh[h], vj, _NT, preferred_element_type=F32)
            sig_s[h, j] = jnp.exp(lb)
            a_s[h, j] = a
            e_s[h, j] = da * a

        kj, vj = _tile(k_ref, i), _tile(v_ref, i)
        carries = []
        for h in range(2):
            lb, lm = _sb_logs(qh[h], kj)
            after, rs = _cumsum_mxu(jnp.where(mask, lm, 0.0), rv)
            stash(h, i, vj, lb, jnp.where(mask, jnp.exp(lb + after), 0.0))
            carries.append(rs)

        def live(c0, c1):
            return jnp.max(jnp.maximum(c0, c1)) > SWEEP_EXIT

        def cond(st):
            return (st[0] < i) & st[1]

        def sweep1(st):
            jj, _, c0, c1 = st
            j = i - 1 - jj
            kj, vj = _tile(k_ref, j), _tile(v_ref, j)
            new_c = []
            for h, carry in enumerate((c0, c1)):
                lb, lm = _sb_logs(qh[h], kj)
                after, rs = _cumsum_mxu(lm, rv)
                stash(h, j, vj, lb, jnp.exp(lb + after + carry))
                new_c.append(carry + rs)
            return jj + 1, live(*new_c), new_c[0], new_c[1]

        visited = lax.while_loop(cond, sweep1, (jnp.int32(0), live(*carries), carries[0], carries[1]))[0]

        def grads(j, st, diagonal):
            p0, p1, dq = st
            kj = _tile(k_ref, j)
            new_p, dqs, dks, dvs = [], [], [], []
            for h, prefix in enumerate((p0, p1)):
                e, a, sg = e_s[h, j], a_s[h, j], sig_s[h, j]
                e_before, rs = _cumsum_mxu(e, fw)
                dz = (e * (1.0 - sg) - (e_before + prefix) * sg) * ATTN_SCALE
                if diagonal:
                    dz = jnp.where(mask, dz, 0.0)
                dzb = dz.astype(BF16)
                dqs.append(jnp.dot(dzb, kj, preferred_element_type=F32))
                dks.append(lax.dot_general(dzb, q2, _TN, preferred_element_type=F32))
                dvs.append(lax.dot_general(a.astype(BF16), do2, _TN, preferred_element_type=F32))
                new_p.append(prefix + rs)
            rows = pl.ds(pl.multiple_of(j * BLOCK, BLOCK), BLOCK)
            dk_ref[rows, :] += jnp.where(lo, dks[0], dks[1])
            dv_ref[rows, :] += jnp.where(lo, dvs[0], dvs[1])
            return new_p[0], new_p[1], dq + jnp.where(lo, dqs[0], dqs[1])

        zeros = jnp.zeros((BLOCK, BLOCK), F32)
        st = lax.fori_loop(i - visited, i, lambda j, st: grads(j, st, False), (zeros, zeros, zeros))
        dq_ref[...] = grads(i, st, True)[2]

    tile_stash = pltpu.VMEM((2, nj, BLOCK, BLOCK), F32)
    return pl.pallas_call(
        body, name="attn_b_bwd", grid=(NB, n_pair, nj),
        in_specs=[blk, blk, k_full, v_full, mat, mat], out_specs=[blk, acc_full, acc_full],
        out_shape=[jax.ShapeDtypeStruct((NB, S, HD), F32)] * 3,
        scratch_shapes=[tile_stash, tile_stash, tile_stash],
        compiler_params=_params(("parallel", "parallel", "arbitrary")),
    )(qkv, do, qkv, qkv, rev, fwd)


def _ada_fwd(c_all, w, b):
    L, D, N = w.shape
    B = c_all.shape[0]

    def body(c_ref, w_ref, b_ref, o_ref):
        cv = c_ref[...]
        cond = (cv * _sigmoid(cv)).astype(BF16)
        o_ref[...] = jnp.dot(cond, w_ref[...].astype(BF16), preferred_element_type=F32) + b_ref[...]

    return pl.pallas_call(
        body, name="ada_fwd", grid=(L,),
        in_specs=[pl.BlockSpec((B, D), lambda l: (0, 0)), pl.BlockSpec((None, D, N), lambda l: (l, 0, 0)),
                  pl.BlockSpec((None, 1, N), lambda l: (l, 0, 0))],
        out_specs=pl.BlockSpec((None, B, N), lambda l: (l, 0, 0)),
        out_shape=jax.ShapeDtypeStruct((L, B, N), F32),
        compiler_params=_params(("parallel",)),
    )(c_all, w, b)


def _ada_bwd(c_all, dmod_all, dmod_shard):
    L, B, N = dmod_shard.shape
    D = c_all.shape[1]
    N_all = dmod_all.shape[2]

    def body(c_ref, da_ref, ds_ref, gw_ref, gb_ref):
        cv = c_ref[...]
        cond = (cv * _sigmoid(cv)).astype(BF16)
        gw_ref[...] = lax.dot_general(cond, ds_ref[...].astype(BF16), _TN, preferred_element_type=F32)
        gb_ref[...] = jnp.sum(da_ref[...], axis=0, keepdims=True)

    return pl.pallas_call(
        body, name="ada_bwd", grid=(L,),
        in_specs=[pl.BlockSpec((B, D), lambda l: (0, 0)), pl.BlockSpec((None, B, N_all), lambda l: (l, 0, 0)),
                  pl.BlockSpec((None, B, N), lambda l: (l, 0, 0))],
        out_specs=[pl.BlockSpec((None, D, N), lambda l: (l, 0, 0)), pl.BlockSpec((None, 1, N_all), lambda l: (l, 0, 0))],
        out_shape=[jax.ShapeDtypeStruct((L, D, N), F32), jax.ShapeDtypeStruct((L, 1, N_all), F32)],
        compiler_params=_params(("parallel",)),
    )(c_all, dmod_all, dmod_shard)


def _adamw(w, g, m, v, name):
    shape = w.shape
    C = shape[-1]
    R = w.size // C
    tr = _pick(R, max(8, (1 << 18) // C), 8)
    c1 = 1.0 - ADAM_B1 ** ADAM_STEP
    c2 = 1.0 - ADAM_B2 ** ADAM_STEP

    def body(w_ref, g_ref, m_ref, v_ref, d_ref, nm_ref, nv_ref):
        gv = g_ref[...]
        nm = ADAM_B1 * m_ref[...] + (1.0 - ADAM_B1) * gv
        nv = ADAM_B2 * v_ref[...] + (1.0 - ADAM_B2) * (gv * gv)
        d_ref[...] = -ADAM_LR * ((nm / c1) / (jnp.sqrt(nv / c2) + ADAM_EPS) + ADAM_WD * w_ref[...])
        nm_ref[...] = nm
        nv_ref[...] = nv

    spec = pl.BlockSpec((tr, C), lambda r: (r, 0))
    out = pl.pallas_call(
        body, name=name, grid=(R // tr,), in_specs=[spec] * 4, out_specs=[spec] * 3,
        out_shape=[jax.ShapeDtypeStruct((R, C), F32)] * 3,
        compiler_params=_params(("parallel",)),
    )(*[t.reshape(R, C) for t in (w, g, m, v)])
    return [t.reshape(shape) for t in out]


_SHARDED = (("wqkv_a", 2), ("wo_a", 1), ("wqkv_b", 2), ("wo_b", 1), ("w_gate", 2), ("w_up", 2), ("w_down", 1))


def _pack_full(full, axis):
    L, R, C = full.shape
    if axis == 2:
        return full.reshape(2, L // 2, R, 4, C // 4).transpose(0, 3, 1, 2, 4)
    return full.reshape(2, L // 2, 4, R // 4, C).transpose(0, 2, 1, 3, 4)


def _unpack_full(gathered, axis):
    _, Lh, Rs, Cs = gathered.shape
    t = gathered.reshape(4, 2, Lh, Rs, Cs)
    if axis == 2:
        return t.transpose(1, 2, 3, 0, 4).reshape(2 * Lh, Rs, 4 * Cs)
    return t.transpose(1, 2, 0, 3, 4).reshape(2 * Lh, 4 * Rs, Cs)


def _sum_slabs(own, recv, name):
    C = own.shape[-1]
    return _sum_leading(recv.reshape(recv.shape[0], -1, C), name, own=own.reshape(-1, C)).reshape(own.shape)


def _gather8(x, dev, name):
    return _with_own_slab(_all_gather8([x], name)[0], x, dev)


def _rope_tables(positions):
    half = ROT_DIM // 2
    inv_freq = jnp.power(jnp.float32(ROPE_THETA), -jnp.arange(half, dtype=F32) * 2.0 / ROT_DIM)
    ang = positions.astype(F32).reshape(-1, 1) * inv_freq
    cos, sin = jnp.cos(ang), jnp.sin(ang)
    T = ang.shape[0]
    rest = HEAD_DIM - ROT_DIM
    c64 = jnp.concatenate([cos, cos, jnp.ones((T, rest), F32)], axis=1)
    s64 = jnp.concatenate([-sin, sin, jnp.zeros((T, rest), F32)], axis=1)
    return jnp.tile(c64, (1, 2)).reshape(T, 1, LANES), jnp.tile(s64, (1, 2)).reshape(T, 1, LANES)


def _gain_rows(q_gain, k_gain):
    q2 = jnp.tile(q_gain.reshape(1, HEAD_DIM), (GROUP_A, 2))
    k2 = jnp.tile(k_gain.reshape(1, HEAD_DIM), (1, 2))
    return jnp.concatenate([q2, k2, jnp.ones((1, LANES), F32)], axis=0)


def _local_step(x, positions, mod, norm1_g, norm2_g, q_norm_a, k_norm_a, sinks_a,
                wqkv_a, wo_a, wqkv_b, wo_b, wgu, wd, loss_target):
    NB, S, D = x.shape
    T = NB * S
    QA = N_Q_A * HEAD_DIM
    rows_a = wqkv_a.shape[2] // LANES
    tab_c, tab_s = _rope_tables(positions)
    rev, fwd = _cumsum_mats()

    saved = []
    xc = x
    for i in range(DEPTH):
        j = i // 2
        sh1, sc1, g1, sh2, sc2, g2 = [mod[i][:, k * D:(k + 1) * D].reshape(NB, 1, D) for k in range(6)]
        st = dict(x=xc, sc1=sc1, g1=g1, sc2=sc2, g2=g2)
        h = _norm_mod_fwd(xc, norm1_g[i:i + 1], sc1, sh1)
        st["h"] = h.reshape(T, D)
        if i % 2 == 0:
            qkv = _matmul(st["h"], wqkv_a[j], "nn", F32, "qkv_a")
            st["qkv3"] = qkv.reshape(T, rows_a, LANES)
            st["gains"] = _gain_rows(q_norm_a[j], k_norm_a[j])
            qkn = _qk_prep_fwd(st["qkv3"], tab_c, tab_s, st["gains"])
            st["q"] = qkn[:, :GROUP_A].reshape(NB, S, N_KV_A, GROUP_A, HEAD_DIM).transpose(0, 2, 3, 1, 4)
            st["k"] = qkn[:, GROUP_A].reshape(NB, S, N_KV_A, HEAD_DIM).transpose(0, 2, 1, 3)
            st["v"] = qkn[:, GROUP_A + 1].reshape(NB, S, N_KV_A, HEAD_DIM).transpose(0, 2, 1, 3)
            o = _attn_a_fwd(st["q"], st["k"], st["v"], sinks_a[j])
            st["o"] = o.transpose(0, 3, 1, 2, 4).reshape(T, QA)
            y = _matmul(st["o"], wo_a[j], "nn", F32, "wo_a")
        else:
            st["qkv"] = _matmul(st["h"], wqkv_b[j], "nn", BF16, "qkv_b").reshape(NB, S, -1)
            st["o"] = _attn_b_fwd(st["qkv"], rev).reshape(T, N_H_B * HEAD_DIM)
            y = _matmul(st["o"], wo_b[j], "nn", F32, "wo_b")
        st["y"] = y.reshape(NB, S, D)
        x1 = _gate_res(xc, st["y"], g1)
        st["x1"] = x1
        h2 = _norm_mod_fwd(x1, norm2_g[i:i + 1], sc2, sh2)
        st["h2"] = h2.reshape(T, D)
        st["gu"] = _matmul(st["h2"], wgu[i], "nn", F32, "gate_up")
        st["act"] = _swiglu_fwd(st["gu"])
        st["m"] = _matmul(st["act"], wd[i], "nn", F32, "down").reshape(NB, S, D)
        xc = _gate_res(x1, st["m"], g2)
        saved.append(st)

    loss, dx = _loss_fwd_bwd(xc, loss_target)

    grads = {name: [None] * n for name, n in
             (("wqkv_a", 2), ("wo_a", 2), ("wqkv_b", 2), ("wo_b", 2), ("wgu", DEPTH), ("wd", DEPTH),
              ("norm1_g", DEPTH), ("norm2_g", DEPTH), ("q_norm_a", 2), ("k_norm_a", 2), ("sinks_a", 2))}
    dmod = [None] * DEPTH
    for i in reversed(range(DEPTH)):
        j = i // 2
        st = saved[i]
        dm, dg2 = _gate_res_bwd(dx, st["m"], st["g2"])
        dm = dm.reshape(T, D)
        dact = _matmul(dm, wd[i], "nt", F32, "d_act")
        grads["wd"][i] = _matmul(st["act"], dm, "tn", F32, "d_wd")
        dgu = _swiglu_bwd(st["gu"], dact)
        dh2 = _matmul(dgu, wgu[i], "nt", F32, "d_h2")
        grads["wgu"][i] = _matmul(st["h2"], dgu, "tn", F32, "d_wgu")
        dx1, dsh2, dsc2, grads["norm2_g"][i] = _norm_mod_bwd(
            st["x1"], norm2_g[i:i + 1], st["sc2"], dh2.reshape(NB, S, D), dx)
        dy, dg1 = _gate_res_bwd(dx1, st["y"], st["g1"])
        dy = dy.reshape(T, D)
        if i % 2 == 0:
            do = _matmul(dy, wo_a[j], "nt", BF16, "d_o_a")
            grads["wo_a"][j] = _matmul(st["o"], dy, "tn", F32, "d_wo_a")
            do5 = do.reshape(NB, S, N_KV_A, GROUP_A, HEAD_DIM).transpose(0, 2, 3, 1, 4)
            dq, dk, dv, dsink = _attn_a_bwd(st["q"], st["k"], st["v"], do5, sinks_a[j])
            d3 = jnp.concatenate([
                dq.transpose(0, 3, 1, 2, 4).reshape(T, GROUP_A, LANES),
                dk.transpose(0, 2, 1, 3).reshape(T, 1, LANES),
                dv.transpose(0, 2, 1, 3).reshape(T, 1, LANES)], axis=1)
            dqkv, dgain = _qk_prep_bwd(st["qkv3"], d3, tab_c, tab_s, st["gains"])
            dqkv = dqkv.reshape(T, rows_a * LANES)
            dh = _matmul(dqkv, wqkv_a[j], "nt", F32, "d_h_a")
            grads["wqkv_a"][j] = _matmul(st["h"], dqkv, "tn", F32, "d_wqkv_a")
            grads["q_norm_a"][j] = jnp.sum(dgain[:GROUP_A].reshape(2 * GROUP_A, HEAD_DIM), axis=0)
            grads["k_norm_a"][j] = jnp.sum(dgain[GROUP_A].reshape(2, HEAD_DIM), axis=0)
            grads["sinks_a"][j] = jnp.sum(dsink[..., 0], axis=0).reshape(N_Q_A)
        else:
            do = _matmul(dy, wo_b[j], "nt", BF16, "d_o_b").reshape(NB, S, -1)
            grads["wo_b"][j] = _matmul(st["o"], dy, "tn", F32, "d_wo_b")
            dq, dk, dv = _attn_b_bwd(st["qkv"], do, rev, fwd)
            dqkv = jnp.concatenate([dq, dk, dv], axis=-1).reshape(T, -1).astype(BF16)
            dh = _matmul(dqkv, wqkv_b[j], "nt", F32, "d_h_b")
            grads["wqkv_b"][j] = _matmul(st["h"], dqkv, "tn", F32, "d_wqkv_b")
        dx, dsh1, dsc1, grads["norm1_g"][i] = _norm_mod_bwd(
            st["x"], norm1_g[i:i + 1], st["sc1"], dh.reshape(NB, S, D), dx1)
        dmod[i] = jnp.concatenate([dsh1, dsc1, dg1, dsh2, dsc2, dg2], axis=-1).reshape(NB, 6 * D)

    grads = {name: jnp.stack(parts) for name, parts in grads.items()}
    return loss, dx, grads, jnp.stack(dmod)


def _rows_of(flat, cols=PACK_COLS):
    n = flat.shape[0]
    pad = (-n) % (8 * cols)
    if pad:
        flat = jnp.concatenate([flat, jnp.zeros((pad,), flat.dtype)])
    return flat.reshape(-1, cols)


def kernel(x, c, positions, ada_w, ada_b, norm1_g, norm2_g, wqkv_a, q_norm_a, k_norm_a, sinks_a, wo_a, wqkv_b, wo_b, w_gate, w_up, w_down, loss_target, m_ada_w, m_ada_b, m_norm1_g, m_norm2_g, m_wqkv_a, m_q_norm_a, m_k_norm_a, m_sinks_a, m_wo_a, m_wqkv_b, m_wo_b, m_w_gate, m_w_up, m_w_down, v_ada_w, v_ada_b, v_norm1_g, v_norm2_g, v_wqkv_a, v_q_norm_a, v_k_norm_a, v_sinks_a, v_wo_a, v_wqkv_b, v_wo_b, v_w_gate, v_w_up, v_w_down):
    xi, yi, ci = lax.axis_index("x"), lax.axis_index("y"), lax.axis_index("c")
    dev = 4 * xi + 2 * yi + ci
    chip = 2 * xi + yi
    NB, S, D = x.shape
    B_all = N_DEV * NB
    L = ada_w.shape[0]
    n_mod = ada_w.shape[2] // 2

    c_all = _gather8(_rows_of(c.reshape(-1), LANES), dev, "gather_c").reshape(N_DEV, -1)[:, :NB * D].reshape(B_all, D)
    ada_w_half = lax.dynamic_slice_in_dim(ada_w, ci * n_mod, n_mod, axis=2)
    ada_b_half = lax.dynamic_slice_in_dim(ada_b, dev * n_mod, n_mod, axis=1).reshape(L, 1, n_mod)
    mod_part = _ada_fwd(c_all, ada_w_half, ada_b_half)
    n_part = L * B_all * n_mod
    mod_all = _gather8(_rows_of(mod_part.reshape(-1)), dev, "gather_mod").reshape(N_DEV, -1)[:, :n_part]
    mod_all = mod_all.reshape(N_DEV, L, B_all, n_mod).transpose(1, 2, 0, 3).reshape(L, B_all, N_DEV * n_mod)
    mod = lax.dynamic_slice_in_dim(mod_all, dev * NB, NB, axis=1)

    shards = dict(wqkv_a=wqkv_a, wo_a=wo_a, wqkv_b=wqkv_b, wo_b=wo_b, w_gate=w_gate, w_up=w_up, w_down=w_down)
    halves = []
    for name, _ in _SHARDED:
        w = shards[name]
        half = lax.dynamic_index_in_dim(w.reshape((2, w.shape[0] // 2) + w.shape[1:]), ci, 0, keepdims=False)
        halves.append(half.astype(BF16))
    gathered = _all_gather8(halves, "gather_weights")
    full = {name: _unpack_full(_with_own_slab(t, h, dev), axis) for (name, axis), t, h in zip(_SHARDED, gathered, halves)}
    wgu = _interleave(full["w_gate"], full["w_up"])

    loss, grad_x, g, dmod = _local_step(
        x, positions, mod, norm1_g, norm2_g, q_norm_a, k_norm_a, sinks_a,
        full["wqkv_a"], full["wo_a"], full["wqkv_b"], full["wo_b"], wgu, full["w_down"], loss_target)

    g_gate, g_up = _deinterleave(g["wgu"])
    g_full = dict(wqkv_a=g["wqkv_a"], wo_a=g["wo_a"], wqkv_b=g["wqkv_b"], wo_b=g["wo_b"],
                  w_gate=g_gate, w_up=g_up, w_down=g["wd"])
    packed = [_pack_full(g_full[name], axis) for name, axis in _SHARDED]
    def own(t, index):
        return lax.dynamic_index_in_dim(t, index, 0, keepdims=False)

    from_cores = _exchange(packed, "c", "rs_cores", chunk_axis=0, chunks=4)
    chip_part = [_sum_slabs(own(p, ci), r, "rs_add_cores") for p, r in zip(packed, from_cores)]
    from_chips = _exchange(chip_part, "xy", "rs_chips")
    mine = [_sum_slabs(own(p, chip), r, "rs_add_chips") for p, r in zip(chip_part, from_chips)]
    theirs = _sibling_send(mine, "rs_halves")
    grad = {}
    for (name, _), m, t in zip(_SHARDED, mine, theirs):
        first, second = jnp.where(ci == 0, m, t), jnp.where(ci == 0, t, m)
        grad[name] = jnp.stack([first, second]).reshape(shards[name].shape)

    small_names = ("norm1_g", "norm2_g", "q_norm_a", "k_norm_a", "sinks_a")
    small = [dmod.reshape(-1)] + [g[name].reshape(-1) for name in small_names] + [loss.reshape(-1)]
    small_sizes = [t.shape[0] for t in small]
    small_rows = _rows_of(jnp.concatenate(small))
    small_all = _gather8(small_rows, dev, "gather_small")
    small_sum = _sum_leading(small_all, "sum_small").reshape(-1)
    n_dmod = small_sizes[0]
    dmod_all = small_all.reshape(N_DEV, -1)[:, :n_dmod].reshape(N_DEV, L, NB, 6 * D)
    dmod_all = dmod_all.transpose(1, 0, 2, 3).reshape(L, B_all, 6 * D)
    off = n_dmod
    for name, sz in zip(small_names + ("loss",), small_sizes[1:]):
        grad[name] = small_sum[off:off + sz]
        off += sz
    loss_total = grad.pop("loss").reshape(())
    for name, ref in (("norm1_g", norm1_g), ("norm2_g", norm2_g), ("q_norm_a", q_norm_a),
                      ("k_norm_a", k_norm_a), ("sinks_a", sinks_a)):
        grad[name] = grad[name].reshape(ref.shape)

    n_shard = ada_w.shape[2]
    dmod_shard = lax.dynamic_slice_in_dim(dmod_all, chip * n_shard, n_shard, axis=2)
    grad["ada_w"], gb = _ada_bwd(c_all, dmod_all, dmod_shard)
    grad["ada_b"] = gb.reshape(ada_b.shape)

    weights = dict(ada_w=ada_w, ada_b=ada_b, norm1_g=norm1_g, norm2_g=norm2_g, wqkv_a=wqkv_a, q_norm_a=q_norm_a,
                   k_norm_a=k_norm_a, sinks_a=sinks_a, wo_a=wo_a, wqkv_b=wqkv_b, wo_b=wo_b, w_gate=w_gate,
                   w_up=w_up, w_down=w_down)
    m_in = dict(ada_w=m_ada_w, ada_b=m_ada_b, norm1_g=m_norm1_g, norm2_g=m_norm2_g, wqkv_a=m_wqkv_a,
                q_norm_a=m_q_norm_a, k_norm_a=m_k_norm_a, sinks_a=m_sinks_a, wo_a=m_wo_a, wqkv_b=m_wqkv_b,
                wo_b=m_wo_b, w_gate=m_w_gate, w_up=m_w_up, w_down=m_w_down)
    v_in = dict(ada_w=v_ada_w, ada_b=v_ada_b, norm1_g=v_norm1_g, norm2_g=v_norm2_g, wqkv_a=v_wqkv_a,
                q_norm_a=v_q_norm_a, k_norm_a=v_k_norm_a, sinks_a=v_sinks_a, wo_a=v_wo_a, wqkv_b=v_wqkv_b,
                wo_b=v_wo_b, w_gate=v_w_gate, w_up=v_w_up, w_down=v_w_down)
    names = list(weights)
    delta, new_m, new_v = {}, {}, {}
    for name in names:
        delta[name], new_m[name], new_v[name] = _adamw(weights[name], grad[name], m_in[name], v_in[name],
                                                       "adamw_" + name)
    return (loss_total, grad_x, *[grad[k] for k in names], *[delta[k] for k in names],
            *[new_m[k] for k in names], *[new_v[k] for k in names])
```

```python
import jax
import jax.numpy as jnp
from jax import lax
from jax.experimental import pallas as pl
from jax.experimental.pallas import tpu as pltpu

F32 = jnp.float32
BF16 = jnp.bfloat16

DEPTH = 4
HEAD_DIM = 64
N_Q_A = 16
N_KV_A = 2
GROUP_A = N_Q_A // N_KV_A
N_H_B = 16
BLOCK = 128
ROT_DIM = HEAD_DIM // 4
ROPE_THETA = 500000.0
EPS = 1e-6
ATTN_SCALE = HEAD_DIM ** -0.5
NEG_BIG = -1e30

ADAM_LR = 0.001
ADAM_B1 = 0.9
ADAM_B2 = 0.999
ADAM_EPS = 1e-08
ADAM_WD = 0.01
ADAM_STEP = 10

N_DEV = 8
LANES = 128
PACK_COLS = 1024
VMEM_LIMIT_BYTES = 48 * 1024 * 1024
MESH = pl.DeviceIdType.MESH

_NT = (((1,), (1,)), ((), ()))
_TN = (((0,), (0,)), ((), ()))
_NN = (((1,), (0,)), ((), ()))


def _params(sem=None):
    return pltpu.CompilerParams(vmem_limit_bytes=VMEM_LIMIT_BYTES, dimension_semantics=sem)


def _pick(n, cap, mult):
    best = None
    for t in range(mult, min(n, cap) + 1, mult):
        if n % t == 0:
            best = t
    return n if best is None else best


_ANY = pl.BlockSpec(memory_space=pl.ANY)


def _window(index, axis, q, n, shape):
    rest = [slice(None)] * len(shape)
    size = shape[axis] // n
    rest[axis] = pl.ds(q * size, size)
    return tuple(index) + tuple(rest)


def _all_gather8(xs, name):
    n = len(xs)

    def body(*refs):
        x_refs, out_refs = refs[:n], refs[n:2 * n]
        send_sems, recv_sems = refs[2 * n:]
        xi, yi, ci = lax.axis_index("x"), lax.axis_index("y"), lax.axis_index("c")
        me, sibling = (xi, yi, ci), (xi, yi, 1 - ci)
        chips = [(1 - xi, yi), (xi, 1 - yi), (1 - xi, 1 - yi)]

        def slab(w, px, py, pc):
            return out_refs[w].at[4 * px + 2 * py + pc]

        def copy(w, k, block, to, src=None):
            return pltpu.make_async_remote_copy(
                src_ref=slab(w, *block) if src is None else src, dst_ref=slab(w, *block),
                send_sem=send_sems.at[k, w], recv_sem=recv_sems.at[k, w], device_id=to, device_id_type=MESH)

        first = [copy(w, 0, me, sibling, src=x_refs[w]) for w in range(n)]
        first += [copy(w, 1 + j, me, (*chip, ci), src=x_refs[w]) for j, chip in enumerate(chips) for w in range(n)]
        for cp in first:
            cp.start()
        passed = []
        for j, chip in enumerate(chips):
            for w in range(n):
                copy(w, 1 + j, (*chip, ci), me).wait_recv()
                passed.append(copy(w, 4 + j, (*chip, ci), sibling))
                passed[-1].start()
        for w in range(n):
            copy(w, 0, sibling, me).wait_recv()
        for j, chip in enumerate(chips):
            for w in range(n):
                copy(w, 4 + j, (*chip, 1 - ci), me).wait_recv()
        for cp in first + passed:
            cp.wait_send()

    return pl.pallas_call(
        body, name=name,
        out_shape=[jax.ShapeDtypeStruct((N_DEV,) + x.shape, x.dtype) for x in xs],
        in_specs=[_ANY] * n, out_specs=[_ANY] * n,
        scratch_shapes=[pltpu.SemaphoreType.DMA((7, n)), pltpu.SemaphoreType.DMA((7, n))],
    )(*xs)


def _with_own_slab(gathered, own, dev):
    return lax.dynamic_update_index_in_dim(gathered, own, dev, 0)


def _exchange(xs, group, name, chunk_axis=0, chunks=1):
    n = len(xs)
    n_peers = 1 if group == "c" else 3

    def body(*refs):
        x_refs, out_refs = refs[:n], refs[n:2 * n]
        send_sems, recv_sems = refs[2 * n:]
        xi, yi, ci = lax.axis_index("x"), lax.axis_index("y"), lax.axis_index("c")
        if group == "c":
            peers = [(1 - ci, (xi, yi, 1 - ci))]
        else:
            peers = [(2 * (1 - xi) + yi, (1 - xi, yi, ci)),
                     (2 * xi + (1 - yi), (xi, 1 - yi, ci)),
                     (2 * (1 - xi) + (1 - yi), (1 - xi, 1 - yi, ci))]
        copies = []
        for k, (p, dev) in enumerate(peers):
            for w in range(n):
                slab_shape = xs[w].shape[1:]
                for q in range(chunks):
                    copies.append(pltpu.make_async_remote_copy(
                        src_ref=x_refs[w].at[_window((p,), chunk_axis, q, chunks, slab_shape)],
                        dst_ref=out_refs[w].at[_window((k,), chunk_axis, q, chunks, slab_shape)],
                        send_sem=send_sems.at[k, w, q], recv_sem=recv_sems.at[k, w, q],
                        device_id=dev, device_id_type=MESH))
                    copies[-1].start()
        for cp in copies:
            cp.wait()

    return pl.pallas_call(
        body, name=name,
        out_shape=[jax.ShapeDtypeStruct((n_peers,) + x.shape[1:], x.dtype) for x in xs],
        in_specs=[_ANY] * n, out_specs=[_ANY] * n,
        scratch_shapes=[pltpu.SemaphoreType.DMA((n_peers, n, chunks)), pltpu.SemaphoreType.DMA((n_peers, n, chunks))],
    )(*xs)


def _sibling_send(xs, name, chunk_axis=1, chunks=4):
    n = len(xs)

    def body(*refs):
        x_refs, out_refs = refs[:n], refs[n:2 * n]
        send_sems, recv_sems = refs[2 * n:]
        xi, yi, ci = lax.axis_index("x"), lax.axis_index("y"), lax.axis_index("c")
        copies = []
        for w in range(n):
            for q in range(chunks):
                part = _window((), chunk_axis, q, chunks, xs[w].shape)
                copies.append(pltpu.make_async_remote_copy(
                    src_ref=x_refs[w].at[part], dst_ref=out_refs[w].at[part],
                    send_sem=send_sems.at[w, q], recv_sem=recv_sems.at[w, q],
                    device_id=(xi, yi, 1 - ci), device_id_type=MESH))
                copies[-1].start()
        for cp in copies:
            cp.wait()

    return pl.pallas_call(
        body, name=name,
        out_shape=[jax.ShapeDtypeStruct(x.shape, x.dtype) for x in xs],
        in_specs=[_ANY] * n, out_specs=[_ANY] * n,
        scratch_shapes=[pltpu.SemaphoreType.DMA((n, chunks)), pltpu.SemaphoreType.DMA((n, chunks))],
    )(*xs)


def _sum_leading(x, name, own=None, with_bf16=False):
    P, R, C = x.shape
    tr = _pick(R, max(16, (1 << 19) // (C * (P + 1))), 16)

    def body(*refs):
        n_in = 1 if own is None else 2
        x_ref = refs[n_in - 1]
        acc = x_ref[0].astype(F32) if own is None else refs[0][...] + x_ref[0].astype(F32)
        for p in range(1, P):
            acc = acc + x_ref[p].astype(F32)
        refs[n_in][...] = acc
        if with_bf16:
            refs[n_in + 1][...] = acc.astype(BF16)

    flat = pl.BlockSpec((tr, C), lambda r: (r, 0))
    slabs = pl.BlockSpec((P, tr, C), lambda r: (0, r, 0))
    out = pl.pallas_call(
        body, name=name, grid=(R // tr,),
        in_specs=[slabs] if own is None else [flat, slabs],
        out_specs=[flat, flat] if with_bf16 else [flat],
        out_shape=[jax.ShapeDtypeStruct((R, C), F32)] + ([jax.ShapeDtypeStruct((R, C), BF16)] if with_bf16 else []),
        compiler_params=_params(("arbitrary",)),
    )(*([x] if own is None else [own, x]))
    return out if with_bf16 else out[0]


def _matmul(a, b, mode, out_dtype, name):
    if mode == "nn":
        (M, K), N = a.shape, b.shape[1]
    elif mode == "nt":
        (M, K), N = a.shape, b.shape[0]
    else:
        (K, M), N = a.shape, b.shape[1]
    tm = _pick(M, 1024 if mode != "tn" else 1536, 128)
    tn = _pick(N, 1536, 128)
    tk = _pick(K, 512, 128)
    nk = K // tk
    dims = {"nn": _NN, "nt": _NT, "tn": _TN}[mode]

    def body(a_ref, b_ref, o_ref, acc_ref):
        k = pl.program_id(2)

        @pl.when(k == 0)
        def _():
            acc_ref[...] = jnp.zeros_like(acc_ref)

        acc_ref[...] += lax.dot_general(a_ref[...].astype(BF16), b_ref[...].astype(BF16), dims,
                                        preferred_element_type=F32)

        @pl.when(k == nk - 1)
        def _():
            o_ref[...] = acc_ref[...].astype(o_ref.dtype)

    if mode == "tn":
        a_spec = pl.BlockSpec((tk, tm), lambda i, j, k: (k, i))
    else:
        a_spec = pl.BlockSpec((tm, tk), lambda i, j, k: (i, k))
    if mode == "nt":
        b_spec = pl.BlockSpec((tn, tk), lambda i, j, k: (j, k))
    else:
        b_spec = pl.BlockSpec((tk, tn), lambda i, j, k: (k, j))
    return pl.pallas_call(
        body, name=name, grid=(M // tm, N // tn, nk),
        in_specs=[a_spec, b_spec],
        out_specs=pl.BlockSpec((tm, tn), lambda i, j, k: (i, j)),
        out_shape=jax.ShapeDtypeStruct((M, N), out_dtype),
        scratch_shapes=[pltpu.VMEM((tm, tn), F32)],
        compiler_params=_params(("parallel", "parallel", "arbitrary")),
    )(a, b)


def _row_tile(S):
    return _pick(S, 512, 8)


def _norm_mod_fwd(x, gain, sc, sh):
    NB, S, D = x.shape
    tr = _row_tile(S)

    def body(x_ref, g_ref, sc_ref, sh_ref, h_ref):
        xv = x_ref[...]
        ms = jnp.mean(xv * xv, axis=-1, keepdims=True)
        n = xv * lax.rsqrt(ms + EPS) * g_ref[...]
        h_ref[...] = (n * (1.0 + sc_ref[...]) + sh_ref[...]).astype(BF16)

    tok = pl.BlockSpec((None, tr, D), lambda b, r: (b, r, 0))
    per_ex = pl.BlockSpec((None, 1, D), lambda b, r: (b, 0, 0))
    return pl.pallas_call(
        body, name="norm_mod_fwd", grid=(NB, S // tr),
        in_specs=[tok, pl.BlockSpec((1, D), lambda b, r: (0, 0)), per_ex, per_ex],
        out_specs=tok, out_shape=jax.ShapeDtypeStruct((NB, S, D), BF16),
        compiler_params=_params(("parallel", "parallel")),
    )(x, gain, sc, sh)


def _norm_mod_bwd(x, gain, sc, dh, dres):
    NB, S, D = x.shape
    tr = _row_tile(S)

    def body(x_ref, g_ref, sc_ref, dh_ref, dres_ref, dx_ref, dsh_ref, dsc_ref, dg_ref):
        b, r = pl.program_id(0), pl.program_id(1)

        @pl.when(r == 0)
        def _():
            dsh_ref[...] = jnp.zeros_like(dsh_ref)
            dsc_ref[...] = jnp.zeros_like(dsc_ref)

        @pl.when((r == 0) & (b == 0))
        def _():
            dg_ref[...] = jnp.zeros_like(dg_ref)

        xv = x_ref[...]
        rstd = lax.rsqrt(jnp.mean(xv * xv, axis=-1, keepdims=True) + EPS)
        xh = xv * rstd
        g = g_ref[...]
        dh = dh_ref[...]
        dsh_ref[...] += jnp.sum(dh, axis=0, keepdims=True)
        dsc_ref[...] += jnp.sum(dh * (xh * g), axis=0, keepdims=True)
        dn = dh * (1.0 + sc_ref[...])
        dg_ref[...] += jnp.sum(dn * xh, axis=0, keepdims=True)
        dxh = dn * g
        proj = jnp.mean(dxh * xh, axis=-1, keepdims=True)
        dx_ref[...] = rstd * (dxh - xh * proj) + dres_ref[...]

    tok = pl.BlockSpec((None, tr, D), lambda b, r: (b, r, 0))
    per_ex = pl.BlockSpec((None, 1, D), lambda b, r: (b, 0, 0))
    row = pl.BlockSpec((1, D), lambda b, r: (0, 0))
    return pl.pallas_call(
        body, name="norm_mod_bwd", grid=(NB, S // tr),
        in_specs=[tok, row, per_ex, tok, tok],
        out_specs=[tok, per_ex, per_ex, row],
        out_shape=[jax.ShapeDtypeStruct((NB, S, D), F32), jax.ShapeDtypeStruct((NB, 1, D), F32),
                   jax.ShapeDtypeStruct((NB, 1, D), F32), jax.ShapeDtypeStruct((1, D), F32)],
        compiler_params=_params(("arbitrary", "arbitrary")),
    )(x, gain, sc, dh, dres)


def _gate_res(x, y, g):
    NB, S, D = x.shape
    tr = _row_tile(S)

    def body(x_ref, y_ref, g_ref, o_ref):
        o_ref[...] = x_ref[...] + g_ref[...] * y_ref[...]

    tok = pl.BlockSpec((None, tr, D), lambda b, r: (b, r, 0))
    per_ex = pl.BlockSpec((None, 1, D), lambda b, r: (b, 0, 0))
    return pl.pallas_call(
        body, name="gate_res", grid=(NB, S // tr), in_specs=[tok, tok, per_ex], out_specs=tok,
        out_shape=jax.ShapeDtypeStruct((NB, S, D), F32),
        compiler_params=_params(("parallel", "parallel")),
    )(x, y, g)


def _gate_res_bwd(dxo, y, g):
    NB, S, D = dxo.shape
    tr = _row_tile(S)

    def body(d_ref, y_ref, g_ref, dy_ref, dg_ref):
        @pl.when(pl.program_id(1) == 0)
        def _():
            dg_ref[...] = jnp.zeros_like(dg_ref)

        d = d_ref[...]
        dy_ref[...] = (d * g_ref[...]).astype(BF16)
        dg_ref[...] += jnp.sum(d * y_ref[...], axis=0, keepdims=True)

    tok = pl.BlockSpec((None, tr, D), lambda b, r: (b, r, 0))
    per_ex = pl.BlockSpec((None, 1, D), lambda b, r: (b, 0, 0))
    return pl.pallas_call(
        body, name="gate_res_bwd", grid=(NB, S // tr), in_specs=[tok, tok, per_ex], out_specs=[tok, per_ex],
        out_shape=[jax.ShapeDtypeStruct((NB, S, D), BF16), jax.ShapeDtypeStruct((NB, 1, D), F32)],
        compiler_params=_params(("arbitrary", "arbitrary")),
    )(dxo, y, g)


def _sigmoid(v):
    return 1.0 / (1.0 + jnp.exp(-v))


def _ff_tile(F):
    return _pick(F, 1536, 128)


def _interleave(gate, up):
    F = gate.shape[-1]
    tf = _ff_tile(F)
    lead = gate.shape[:-1]
    parts = jnp.stack([gate.reshape(lead + (F // tf, tf)), up.reshape(lead + (F // tf, tf))], axis=-2)
    return parts.reshape(lead + (2 * F,))


def _deinterleave(gu):
    F = gu.shape[-1] // 2
    tf = _ff_tile(F)
    lead = gu.shape[:-1]
    parts = gu.reshape(lead + (F // tf, 2, tf))
    return parts[..., 0, :].reshape(lead + (F,)), parts[..., 1, :].reshape(lead + (F,))


def _swiglu_fwd(gu):
    T, F2 = gu.shape
    F = F2 // 2
    tf = _ff_tile(F)
    tr = _pick(T, 256, 8)

    def body(gu_ref, o_ref):
        g = gu_ref[:, :tf]
        o_ref[...] = (g * _sigmoid(g) * gu_ref[:, tf:]).astype(BF16)

    return pl.pallas_call(
        body, name="swiglu_fwd", grid=(T // tr, F // tf),
        in_specs=[pl.BlockSpec((tr, 2 * tf), lambda i, j: (i, j))],
        out_specs=pl.BlockSpec((tr, tf), lambda i, j: (i, j)),
        out_shape=jax.ShapeDtypeStruct((T, F), BF16),
        compiler_params=_params(("parallel", "parallel")),
    )(gu)


def _swiglu_bwd(gu, dact):
    T, F2 = gu.shape
    F = F2 // 2
    tf = _ff_tile(F)
    tr = _pick(T, 256, 8)

    def body(gu_ref, d_ref, o_ref):
        g, u, d = gu_ref[:, :tf], gu_ref[:, tf:], d_ref[...]
        s = _sigmoid(g)
        o_ref[:, :tf] = (d * u * (s * (1.0 + g * (1.0 - s)))).astype(BF16)
        o_ref[:, tf:] = (d * (g * s)).astype(BF16)

    return pl.pallas_call(
        body, name="swiglu_bwd", grid=(T // tr, F // tf),
        in_specs=[pl.BlockSpec((tr, 2 * tf), lambda i, j: (i, j)), pl.BlockSpec((tr, tf), lambda i, j: (i, j))],
        out_specs=pl.BlockSpec((tr, 2 * tf), lambda i, j: (i, j)),
        out_shape=jax.ShapeDtypeStruct((T, F2), BF16),
        compiler_params=_params(("parallel", "parallel")),
    )(gu, dact)


def _loss_fwd_bwd(y, target):
    NB, S, D = y.shape
    tr = _row_tile(S)

    def body(y_ref, t_ref, l_ref, d_ref):
        @pl.when((pl.program_id(0) == 0) & (pl.program_id(1) == 0))
        def _():
            l_ref[...] = jnp.zeros_like(l_ref)

        e = y_ref[...] - t_ref[...]
        d_ref[...] = e / D
        l_ref[...] += 0.5 * jnp.sum(jnp.mean(e * e, axis=-1, keepdims=True), axis=0, keepdims=True)

    tok = pl.BlockSpec((None, tr, D), lambda b, r: (b, r, 0))
    return pl.pallas_call(
        body, name="loss", grid=(NB, S // tr), in_specs=[tok, tok],
        out_specs=[pl.BlockSpec((1, 1), lambda b, r: (0, 0)), tok],
        out_shape=[jax.ShapeDtypeStruct((1, 1), F32), jax.ShapeDtypeStruct((NB, S, D), F32)],
        compiler_params=_params(("arbitrary", "arbitrary")),
    )(y, target)


def _half_sums(v, lo):
    sa = jnp.sum(jnp.where(lo, v, 0.0), axis=-1, keepdims=True)
    sb = jnp.sum(jnp.where(lo, 0.0, v), axis=-1, keepdims=True)
    return jnp.where(lo, sa, sb)


def _rope_swap(v, lane64):
    up = pltpu.roll(v, LANES - ROT_DIM // 2, 2)
    down = pltpu.roll(v, ROT_DIM // 2, 2)
    return jnp.where(lane64 < ROT_DIM // 2, up, jnp.where(lane64 < ROT_DIM, down, 0.0))


def _qk_prep_fwd(qkv3, tab_c, tab_s, gains):
    T = qkv3.shape[0]
    R = qkv3.shape[1]
    tt = _pick(T, 256, 8)

    def body(x_ref, c_ref, s_ref, g_ref, o_ref):
        xv = x_ref[...]
        lane = lax.broadcasted_iota(jnp.int32, xv.shape, 2)
        row = lax.broadcasted_iota(jnp.int32, xv.shape, 1)
        lo = lane < HEAD_DIM
        rstd = lax.rsqrt(_half_sums(xv * xv, lo) / HEAD_DIM + EPS)
        yn = xv * rstd * g_ref[...]
        roped = yn * c_ref[...] + _rope_swap(yn, lane & (HEAD_DIM - 1)) * s_ref[...]
        o_ref[...] = jnp.where(row == R - 1, xv, roped).astype(BF16)

    tok = pl.BlockSpec((tt, R, LANES), lambda t: (t, 0, 0))
    tab = pl.BlockSpec((tt, 1, LANES), lambda t: (t, 0, 0))
    return pl.pallas_call(
        body, name="qk_prep_fwd", grid=(T // tt,),
        in_specs=[tok, tab, tab, pl.BlockSpec((R, LANES), lambda t: (0, 0))],
        out_specs=tok, out_shape=jax.ShapeDtypeStruct(qkv3.shape, BF16),
        compiler_params=_params(("parallel",)),
    )(qkv3, tab_c, tab_s, gains)


def _qk_prep_bwd(qkv3, d3, tab_c, tab_s, gains):
    T = qkv3.shape[0]
    R = qkv3.shape[1]
    tt = _pick(T, 256, 8)

    def body(x_ref, d_ref, c_ref, s_ref, g_ref, o_ref, dg_ref):
        @pl.when(pl.program_id(0) == 0)
        def _():
            dg_ref[...] = jnp.zeros_like(dg_ref)

        xv, d = x_ref[...], d_ref[...]
        lane = lax.broadcasted_iota(jnp.int32, xv.shape, 2)
        row = lax.broadcasted_iota(jnp.int32, xv.shape, 1)
        lo = lane < HEAD_DIM
        rstd = lax.rsqrt(_half_sums(xv * xv, lo) / HEAD_DIM + EPS)
        xh = xv * rstd
        dyn = d * c_ref[...] + _rope_swap(d * s_ref[...], lane & (HEAD_DIM - 1))
        dg_ref[...] += jnp.sum(dyn * xh, axis=0)
        dxh = dyn * g_ref[...]
        proj = _half_sums(dxh * xh, lo) / HEAD_DIM
        dx = rstd * (dxh - xh * proj)
        o_ref[...] = jnp.where(row == R - 1, d, dx).astype(BF16)

    tok = pl.BlockSpec((tt, R, LANES), lambda t: (t, 0, 0))
    tab = pl.BlockSpec((tt, 1, LANES), lambda t: (t, 0, 0))
    gsp = pl.BlockSpec((R, LANES), lambda t: (0, 0))
    return pl.pallas_call(
        body, name="qk_prep_bwd", grid=(T // tt,),
        in_specs=[tok, tok, tab, tab, gsp], out_specs=[tok, gsp],
        out_shape=[jax.ShapeDtypeStruct(qkv3.shape, BF16), jax.ShapeDtypeStruct((R, LANES), F32)],
        compiler_params=_params(("arbitrary",)),
    )(qkv3, d3, tab_c, tab_s, gains)


def _band_mask(i):
    r = lax.broadcasted_iota(jnp.int32, (BLOCK, 2 * BLOCK), 0)
    c = lax.broadcasted_iota(jnp.int32, (BLOCK, 2 * BLOCK), 1)
    rel = r + BLOCK - c
    return (rel >= 0) & (rel < BLOCK) & ((c >= BLOCK) | (i > 0))


def _swa_probs(qg, k2, valid, sink):
    s = lax.dot_general(qg, k2, _NT, preferred_element_type=F32) * ATTN_SCALE
    s = jnp.where(valid, s, NEG_BIG)
    m = jnp.maximum(jnp.max(s, axis=1, keepdims=True), sink)
    p = jnp.exp(s - m)
    ps = jnp.exp(sink - m)
    denom = jnp.sum(p, axis=1, keepdims=True) + ps
    return p / denom, ps / denom


def _swa_specs(S):
    qs = pl.BlockSpec((None, None, GROUP_A, BLOCK, HEAD_DIM), lambda b, h, i: (b, h, 0, i, 0))
    prev = pl.BlockSpec((None, None, BLOCK, HEAD_DIM), lambda b, h, i: (b, h, jnp.maximum(i - 1, 0), 0))
    cur = pl.BlockSpec((None, None, BLOCK, HEAD_DIM), lambda b, h, i: (b, h, i, 0))
    return qs, prev, cur


def _attn_a_fwd(q, k, v, sinks):
    NB, _, _, S, _ = q.shape
    qs, prev, cur = _swa_specs(S)

    def body(q_ref, kp_ref, kc_ref, vp_ref, vc_ref, sink_ref, o_ref):
        h, i = pl.program_id(1), pl.program_id(2)
        k2 = jnp.concatenate([kp_ref[...], kc_ref[...]], axis=0)
        v2 = jnp.concatenate([vp_ref[...], vc_ref[...]], axis=0)
        valid = _band_mask(i)
        for g in range(GROUP_A):
            pn, _ = _swa_probs(q_ref[g], k2, valid, sink_ref[h * GROUP_A + g])
            o_ref[g] = jnp.dot(pn.astype(BF16), v2, preferred_element_type=F32).astype(BF16)

    return pl.pallas_call(
        body, name="attn_a_fwd", grid=(NB, N_KV_A, S // BLOCK),
        in_specs=[qs, prev, cur, prev, cur, pl.BlockSpec(memory_space=pltpu.SMEM)],
        out_specs=qs, out_shape=jax.ShapeDtypeStruct(q.shape, BF16),
        compiler_params=_params(("parallel", "parallel", "arbitrary")),
    )(q, k, k, v, v, sinks)


def _attn_a_bwd(q, k, v, do, sinks):
    NB, _, _, S, _ = q.shape
    qs, prev, cur = _swa_specs(S)
    full = pl.BlockSpec((None, None, S, HEAD_DIM), lambda b, h, i: (b, h, 0, 0))
    sink_out = pl.BlockSpec((None, None, GROUP_A, LANES), lambda b, h, i: (b, h, 0, 0))

    def body(q_ref, do_ref, kp_ref, kc_ref, vp_ref, vc_ref, sink_ref, dq_ref, dk_ref, dv_ref, ds_ref):
        h, i = pl.program_id(1), pl.program_id(2)

        @pl.when(i == 0)
        def _():
            dk_ref[...] = jnp.zeros_like(dk_ref)
            dv_ref[...] = jnp.zeros_like(dv_ref)
            ds_ref[...] = jnp.zeros_like(ds_ref)

        k2 = jnp.concatenate([kp_ref[...], kc_ref[...]], axis=0)
        v2 = jnp.concatenate([vp_ref[...], vc_ref[...]], axis=0)
        valid = _band_mask(i)
        dk2 = jnp.zeros((2 * BLOCK, HEAD_DIM), F32)
        dv2 = jnp.zeros((2 * BLOCK, HEAD_DIM), F32)
        for g in range(GROUP_A):
            qg, dog = q_ref[g], do_ref[g]
            pn, psink = _swa_probs(qg, k2, valid, sink_ref[h * GROUP_A + g])
            dp = lax.dot_general(dog, v2, _NT, preferred_element_type=F32)
            delta = jnp.sum(pn * dp, axis=1, keepdims=True)
            dsb = (pn * (dp - delta) * ATTN_SCALE).astype(BF16)
            dq_ref[g] = jnp.dot(dsb, k2, preferred_element_type=F32)
            dk2 = dk2 + lax.dot_general(dsb, qg, _TN, preferred_element_type=F32)
            dv2 = dv2 + lax.dot_general(pn.astype(BF16), dog, _TN, preferred_element_type=F32)
            dsink = -jnp.sum(psink * delta, axis=0, keepdims=True)
            ds_ref[g:g + 1, :] += jnp.broadcast_to(dsink, (1, LANES))

        @pl.when(i > 0)
        def _():
            start = pl.multiple_of((i - 1) * BLOCK, BLOCK)
            dk_ref[pl.ds(start, 2 * BLOCK), :] += dk2
            dv_ref[pl.ds(start, 2 * BLOCK), :] += dv2

        @pl.when(i == 0)
        def _():
            dk_ref[0:BLOCK, :] += dk2[BLOCK:, :]
            dv_ref[0:BLOCK, :] += dv2[BLOCK:, :]

    return pl.pallas_call(
        body, name="attn_a_bwd", grid=(NB, N_KV_A, S // BLOCK),
        in_specs=[qs, qs, prev, cur, prev, cur, pl.BlockSpec(memory_space=pltpu.SMEM)],
        out_specs=[qs, full, full, sink_out],
        out_shape=[jax.ShapeDtypeStruct(q.shape, F32), jax.ShapeDtypeStruct(k.shape, F32),
                   jax.ShapeDtypeStruct(k.shape, F32), jax.ShapeDtypeStruct((NB, N_KV_A, GROUP_A, LANES), F32)],
        compiler_params=_params(("parallel", "parallel", "arbitrary")),
    )(q, do, k, k, v, v, sinks)


def _cumsum_mats():
    src = lax.broadcasted_iota(jnp.int32, (2 * BLOCK, 2 * BLOCK), 0) % BLOCK
    dst = lax.broadcasted_iota(jnp.int32, (2 * BLOCK, 2 * BLOCK), 1)
    ones = dst >= BLOCK
    rev = ((src > dst) | ones).astype(BF16)
    fwd = ((src < dst) | ones).astype(BF16)
    return rev, fwd


def _cumsum_mxu(v, mat):
    hi = v.astype(BF16)
    lo = (v - hi.astype(F32)).astype(BF16)
    r = jnp.dot(jnp.concatenate([hi, lo], axis=1), mat, preferred_element_type=F32)
    return r[:, :BLOCK], r[:, BLOCK:]


def _sb_logs(qs, kj):
    z = lax.dot_general(qs, kj, _NT, preferred_element_type=F32)
    sp = jnp.log(1.0 + jnp.exp(-jnp.abs(z)))
    return jnp.minimum(z, 0.0) - sp, -(jnp.maximum(z, 0.0) + sp)


def _strict_mask():
    r = lax.broadcasted_iota(jnp.int32, (BLOCK, BLOCK), 0)
    c = lax.broadcasted_iota(jnp.int32, (BLOCK, BLOCK), 1)
    return c < r


def _tile(ref, j):
    return ref[pl.ds(pl.multiple_of(j * BLOCK, BLOCK), BLOCK), :]


SWEEP_EXIT = -88.0


def _head_halves(t, lo):
    zero = jnp.zeros_like(t)
    return jnp.where(lo, t, zero), jnp.where(lo, zero, t)


def _sb_specs(S, HD, width):
    n = HD // width
    blk = pl.BlockSpec((None, BLOCK, width), lambda b, p, i: (b, i, p))
    k_full = pl.BlockSpec((None, S, width), lambda b, p, i: (b, 0, n + p))
    v_full = pl.BlockSpec((None, S, width), lambda b, p, i: (b, 0, 2 * n + p))
    mat = pl.BlockSpec((2 * BLOCK, 2 * BLOCK), lambda b, p, i: (0, 0))
    return blk, k_full, v_full, mat


SB_FWD_PAIRS = 2


def _attn_b_fwd(qkv, rev):
    NB, S, W = qkv.shape
    HD = W // 3
    width = SB_FWD_PAIRS * LANES
    n_heads = 2 * SB_FWD_PAIRS
    blk, k_full, v_full, mat = _sb_specs(S, HD, width)

    def body(q_ref, k_ref, v_ref, rev_ref, o_ref):
        i = pl.program_id(2)
        rv = rev_ref[...]
        mask = _strict_mask()
        lo = lax.broadcasted_iota(jnp.int32, (BLOCK, LANES), 1) < HEAD_DIM
        q_all = q_ref[...]
        qh = []
        for p in range(SB_FWD_PAIRS):
            qh.extend(_head_halves(q_all[:, p * LANES:(p + 1) * LANES] * ATTN_SCALE, lo))

        def pair_tiles(ref, j):
            t = _tile(ref, j)
            return [t[:, p * LANES:(p + 1) * LANES] for p in range(SB_FWD_PAIRS)]

        def merge(outs):
            return [jnp.where(lo, outs[2 * p], outs[2 * p + 1]) for p in range(SB_FWD_PAIRS)]

        ks, vs = pair_tiles(k_ref, i), pair_tiles(v_ref, i)
        carries, outs = [], []
        for h in range(n_heads):
            lb, lm = _sb_logs(qh[h], ks[h // 2])
            after, rs = _cumsum_mxu(jnp.where(mask, lm, 0.0), rv)
            a = jnp.where(mask, jnp.exp(lb + after), 0.0)
            outs.append(jnp.dot(a.astype(BF16), vs[h // 2], preferred_element_type=F32))
            carries.append(rs)
        accs = merge(outs)

        def live(cs):
            top = cs[0]
            for c in cs[1:]:
                top = jnp.maximum(top, c)
            return jnp.max(top) > SWEEP_EXIT

        def cond(st):
            return (st[0] < i) & st[1]

        def step(st):
            jj, _, cs, accs = st
            j = i - 1 - jj
            ks, vs = pair_tiles(k_ref, j), pair_tiles(v_ref, j)
            new_c, outs = [], []
            for h in range(n_heads):
                lb, lm = _sb_logs(qh[h], ks[h // 2])
                after, rs = _cumsum_mxu(lm, rv)
                a = jnp.exp(lb + after + cs[h])
                outs.append(jnp.dot(a.astype(BF16), vs[h // 2], preferred_element_type=F32))
                new_c.append(cs[h] + rs)
            accs = [acc + o for acc, o in zip(accs, merge(outs))]
            return jj + 1, live(new_c), new_c, accs

        st = lax.while_loop(cond, step, (jnp.int32(0), live(carries), carries, accs))
        for p in range(SB_FWD_PAIRS):
            o_ref[:, p * LANES:(p + 1) * LANES] = st[3][p].astype(BF16)

    return pl.pallas_call(
        body, name="attn_b_fwd", grid=(NB, HD // width, S // BLOCK),
        in_specs=[blk, k_full, v_full, mat], out_specs=blk,
        out_shape=jax.ShapeDtypeStruct((NB, S, HD), BF16),
        compiler_params=_params(("parallel", "parallel", "arbitrary")),
    )(qkv, qkv, qkv, rev)


def _attn_b_bwd(qkv, do, rev, fwd):
    NB, S, W = qkv.shape
    HD = W // 3
    n_pair = HD // LANES
    nj = S // BLOCK
    blk, k_full, v_full, mat = _sb_specs(S, HD, LANES)
    acc_full = pl.BlockSpec((None, S, LANES), lambda b, p, i: (b, 0, p))

    def body(q_ref, do_ref, k_ref, v_ref, rev_ref, fwd_ref, dq_ref, dk_ref, dv_ref, sig_s, a_s, e_s):
        i = pl.program_id(2)

        @pl.when(i == 0)
        def _():
            dk_ref[...] = jnp.zeros_like(dk_ref)
            dv_ref[...] = jnp.zeros_like(dv_ref)

        q2, do2 = q_ref[...], do_ref[...]
        rv, fw = rev_ref[...], fwd_ref[...]
        mask = _strict_mask()
        lo = lax.broadcasted_iota(jnp.int32, (BLOCK, LANES), 1) < HEAD_DIM
        qh = _head_halves(q2 * ATTN_SCALE, lo)
        doh = _head_halves(do2, lo)

        def stash(h, j, vj, lb, a):
            da = lax.dot_general(doh[h], vj, _NT, preferred_element_type=F32)
            sig_s[h, j] = jnp.exp(lb)
            a_s[h, j] = a
            e_s[h, j] = da * a

        kj, vj = _tile(k_ref, i), _tile(v_ref, i)
        carries = []
        for h in range(2):
            lb, lm = _sb_logs(qh[h], kj)
            after, rs = _cumsum_mxu(jnp.where(mask, lm, 0.0), rv)
            stash(h, i, vj, lb, jnp.where(mask, jnp.exp(lb + after), 0.0))
            carries.append(rs)

        def live(c0, c1):
            return jnp.max(jnp.maximum(c0, c1)) > SWEEP_EXIT

        def cond(st):
            return (st[0] < i) & st[1]

        def sweep1(st):
            jj, _, c0, c1 = st
            j = i - 1 - jj
            kj, vj = _tile(k_ref, j), _tile(v_ref, j)
            new_c = []
            for h, carry in enumerate((c0, c1)):
                lb, lm = _sb_logs(qh[h], kj)
                after, rs = _cumsum_mxu(lm, rv)
                stash(h, j, vj, lb, jnp.exp(lb + after + carry))
                new_c.append(carry + rs)
            return jj + 1, live(*new_c), new_c[0], new_c[1]

        visited = lax.while_loop(cond, sweep1, (jnp.int32(0), live(*carries), carries[0], carries[1]))[0]

        def grads(j, st, diagonal):
            p0, p1, dq = st
            kj = _tile(k_ref, j)
            new_p, dqs, dks, dvs = [], [], [], []
            for h, prefix in enumerate((p0, p1)):
                e, a, sg = e_s[h, j], a_s[h, j], sig_s[h, j]
                e_before, rs = _cumsum_mxu(e, fw)
                dz = (e * (1.0 - sg) - (e_before + prefix) * sg) * ATTN_SCALE
                if diagonal:
                    dz = jnp.where(mask, dz, 0.0)
                dzb = dz.astype(BF16)
                dqs.append(jnp.dot(dzb, kj, preferred_element_type=F32))
                dks.append(lax.dot_general(dzb, q2, _TN, preferred_element_type=F32))
                dvs.append(lax.dot_general(a.astype(BF16), do2, _TN, preferred_element_type=F32))
                new_p.append(prefix + rs)
            rows = pl.ds(pl.multiple_of(j * BLOCK, BLOCK), BLOCK)
            dk_ref[rows, :] += jnp.where(lo, dks[0], dks[1])
            dv_ref[rows, :] += jnp.where(lo, dvs[0], dvs[1])
            return new_p[0], new_p[1], dq + jnp.where(lo, dqs[0], dqs[1])

        zeros = jnp.zeros((BLOCK, BLOCK), F32)
        st = lax.fori_loop(i - visited, i, lambda j, st: grads(j, st, False), (zeros, zeros, zeros))
        dq_ref[...] = grads(i, st, True)[2]

    tile_stash = pltpu.VMEM((2, nj, BLOCK, BLOCK), F32)
    return pl.pallas_call(
        body, name="attn_b_bwd", grid=(NB, n_pair, nj),
        in_specs=[blk, blk, k_full, v_full, mat, mat], out_specs=[blk, acc_full, acc_full],
        out_shape=[jax.ShapeDtypeStruct((NB, S, HD), F32)] * 3,
        scratch_shapes=[tile_stash, tile_stash, tile_stash],
        compiler_params=_params(("parallel", "parallel", "arbitrary")),
    )(qkv, do, qkv, qkv, rev, fwd)


def _ada_fwd(c_all, w, b):
    L, D, N = w.shape
    B = c_all.shape[0]

    def body(c_ref, w_ref, b_ref, o_ref):
        cv = c_ref[...]
        cond = (cv * _sigmoid(cv)).astype(BF16)
        o_ref[...] = jnp.dot(cond, w_ref[...].astype(BF16), preferred_element_type=F32) + b_ref[...]

    return pl.pallas_call(
        body, name="ada_fwd", grid=(L,),
        in_specs=[pl.BlockSpec((B, D), lambda l: (0, 0)), pl.BlockSpec((None, D, N), lambda l: (l, 0, 0)),
                  pl.BlockSpec((None, 1, N), lambda l: (l, 0, 0))],
        out_specs=pl.BlockSpec((None, B, N), lambda l: (l, 0, 0)),
        out_shape=jax.ShapeDtypeStruct((L, B, N), F32),
        compiler_params=_params(("parallel",)),
    )(c_all, w, b)


def _ada_bwd(c_all, dmod_all, dmod_shard):
    L, B, N = dmod_shard.shape
    D = c_all.shape[1]
    N_all = dmod_all.shape[2]

    def body(c_ref, da_ref, ds_ref, gw_ref, gb_ref):
        cv = c_ref[...]
        cond = (cv * _sigmoid(cv)).astype(BF16)
        gw_ref[...] = lax.dot_general(cond, ds_ref[...].astype(BF16), _TN, preferred_element_type=F32)
        gb_ref[...] = jnp.sum(da_ref[...], axis=0, keepdims=True)

    return pl.pallas_call(
        body, name="ada_bwd", grid=(L,),
        in_specs=[pl.BlockSpec((B, D), lambda l: (0, 0)), pl.BlockSpec((None, B, N_all), lambda l: (l, 0, 0)),
                  pl.BlockSpec((None, B, N), lambda l: (l, 0, 0))],
        out_specs=[pl.BlockSpec((None, D, N), lambda l: (l, 0, 0)), pl.BlockSpec((None, 1, N_all), lambda l: (l, 0, 0))],
        out_shape=[jax.ShapeDtypeStruct((L, D, N), F32), jax.ShapeDtypeStruct((L, 1, N_all), F32)],
        compiler_params=_params(("parallel",)),
    )(c_all, dmod_all, dmod_shard)


def _adamw(w, g, m, v, name):
    shape = w.shape
    C = shape[-1]
    R = w.size // C
    tr = _pick(R, max(8, (1 << 18) // C), 8)
    c1 = 1.0 - ADAM_B1 ** ADAM_STEP
    c2 = 1.0 - ADAM_B2 ** ADAM_STEP

    def body(w_ref, g_ref, m_ref, v_ref, d_ref, nm_ref, nv_ref):
        gv = g_ref[...]
        nm = ADAM_B1 * m_ref[...] + (1.0 - ADAM_B1) * gv
        nv = ADAM_B2 * v_ref[...] + (1.0 - ADAM_B2) * (gv * gv)
        d_ref[...] = -ADAM_LR * ((nm / c1) / (jnp.sqrt(nv / c2) + ADAM_EPS) + ADAM_WD * w_ref[...])
        nm_ref[...] = nm
        nv_ref[...] = nv

    spec = pl.BlockSpec((tr, C), lambda r: (r, 0))
    out = pl.pallas_call(
        body, name=name, grid=(R // tr,), in_specs=[spec] * 4, out_specs=[spec] * 3,
        out_shape=[jax.ShapeDtypeStruct((R, C), F32)] * 3,
        compiler_params=_params(("parallel",)),
    )(*[t.reshape(R, C) for t in (w, g, m, v)])
    return [t.reshape(shape) for t in out]


_SHARDED = (("wqkv_a", 2), ("wo_a", 1), ("wqkv_b", 2), ("wo_b", 1), ("w_gate", 2), ("w_up", 2), ("w_down", 1))


def _pack_full(full, axis):
    L, R, C = full.shape
    if axis == 2:
        return full.reshape(2, L // 2, R, 4, C // 4).transpose(0, 3, 1, 2, 4)
    return full.reshape(2, L // 2, 4, R // 4, C).transpose(0, 2, 1, 3, 4)


def _unpack_full(gathered, axis):
    _, Lh, Rs, Cs = gathered.shape
    t = gathered.reshape(4, 2, Lh, Rs, Cs)
    if axis == 2:
        return t.transpose(1, 2, 3, 0, 4).reshape(2 * Lh, Rs, 4 * Cs)
    return t.transpose(1, 2, 0, 3, 4).reshape(2 * Lh, 4 * Rs, Cs)


def _sum_slabs(own, recv, name, with_bf16=False):
    C = own.shape[-1]
    out = _sum_leading(recv.reshape(recv.shape[0], -1, C), name, own=own.reshape(-1, C), with_bf16=with_bf16)
    if with_bf16:
        return out[0].reshape(own.shape), out[1].reshape(own.shape)
    return out.reshape(own.shape)


def _gather8(x, dev, name):
    return _with_own_slab(_all_gather8([x], name)[0], x, dev)


def _rope_tables(positions):
    half = ROT_DIM // 2
    inv_freq = jnp.power(jnp.float32(ROPE_THETA), -jnp.arange(half, dtype=F32) * 2.0 / ROT_DIM)
    ang = positions.astype(F32).reshape(-1, 1) * inv_freq
    cos, sin = jnp.cos(ang), jnp.sin(ang)
    T = ang.shape[0]
    rest = HEAD_DIM - ROT_DIM
    c64 = jnp.concatenate([cos, cos, jnp.ones((T, rest), F32)], axis=1)
    s64 = jnp.concatenate([-sin, sin, jnp.zeros((T, rest), F32)], axis=1)
    return jnp.tile(c64, (1, 2)).reshape(T, 1, LANES), jnp.tile(s64, (1, 2)).reshape(T, 1, LANES)


def _gain_rows(q_gain, k_gain):
    q2 = jnp.tile(q_gain.reshape(1, HEAD_DIM), (GROUP_A, 2))
    k2 = jnp.tile(k_gain.reshape(1, HEAD_DIM), (1, 2))
    return jnp.concatenate([q2, k2, jnp.ones((1, LANES), F32)], axis=0)


def _local_step(x, positions, mod, norm1_g, norm2_g, q_norm_a, k_norm_a, sinks_a,
                wqkv_a, wo_a, wqkv_b, wo_b, wgu, wd, loss_target):
    NB, S, D = x.shape
    T = NB * S
    QA = N_Q_A * HEAD_DIM
    rows_a = wqkv_a.shape[2] // LANES
    tab_c, tab_s = _rope_tables(positions)
    rev, fwd = _cumsum_mats()

    saved = []
    xc = x
    for i in range(DEPTH):
        j = i // 2
        sh1, sc1, g1, sh2, sc2, g2 = [mod[i][:, k * D:(k + 1) * D].reshape(NB, 1, D) for k in range(6)]
        st = dict(x=xc, sc1=sc1, g1=g1, sc2=sc2, g2=g2)
        h = _norm_mod_fwd(xc, norm1_g[i:i + 1], sc1, sh1)
        st["h"] = h.reshape(T, D)
        if i % 2 == 0:
            qkv = _matmul(st["h"], wqkv_a[j], "nn", F32, "qkv_a")
            st["qkv3"] = qkv.reshape(T, rows_a, LANES)
            st["gains"] = _gain_rows(q_norm_a[j], k_norm_a[j])
            qkn = _qk_prep_fwd(st["qkv3"], tab_c, tab_s, st["gains"])
            st["q"] = qkn[:, :GROUP_A].reshape(NB, S, N_KV_A, GROUP_A, HEAD_DIM).transpose(0, 2, 3, 1, 4)
            st["k"] = qkn[:, GROUP_A].reshape(NB, S, N_KV_A, HEAD_DIM).transpose(0, 2, 1, 3)
            st["v"] = qkn[:, GROUP_A + 1].reshape(NB, S, N_KV_A, HEAD_DIM).transpose(0, 2, 1, 3)
            o = _attn_a_fwd(st["q"], st["k"], st["v"], sinks_a[j])
            st["o"] = o.transpose(0, 3, 1, 2, 4).reshape(T, QA)
            y = _matmul(st["o"], wo_a[j], "nn", F32, "wo_a")
        else:
            st["qkv"] = _matmul(st["h"], wqkv_b[j], "nn", BF16, "qkv_b").reshape(NB, S, -1)
            st["o"] = _attn_b_fwd(st["qkv"], rev).reshape(T, N_H_B * HEAD_DIM)
            y = _matmul(st["o"], wo_b[j], "nn", F32, "wo_b")
        st["y"] = y.reshape(NB, S, D)
        x1 = _gate_res(xc, st["y"], g1)
        st["x1"] = x1
        h2 = _norm_mod_fwd(x1, norm2_g[i:i + 1], sc2, sh2)
        st["h2"] = h2.reshape(T, D)
        st["gu"] = _matmul(st["h2"], wgu[i], "nn", F32, "gate_up")
        st["act"] = _swiglu_fwd(st["gu"])
        st["m"] = _matmul(st["act"], wd[i], "nn", F32, "down").reshape(NB, S, D)
        xc = _gate_res(x1, st["m"], g2)
        saved.append(st)

    loss, dx = _loss_fwd_bwd(xc, loss_target)

    grads = {name: [None] * n for name, n in
             (("wqkv_a", 2), ("wo_a", 2), ("wqkv_b", 2), ("wo_b", 2), ("wgu", DEPTH), ("wd", DEPTH),
              ("norm1_g", DEPTH), ("norm2_g", DEPTH), ("q_norm_a", 2), ("k_norm_a", 2), ("sinks_a", 2))}
    dmod = [None] * DEPTH
    for i in reversed(range(DEPTH)):
        j = i // 2
        st = saved[i]
        dm, dg2 = _gate_res_bwd(dx, st["m"], st["g2"])
        dm = dm.reshape(T, D)
        dact = _matmul(dm, wd[i], "nt", F32, "d_act")
        grads["wd"][i] = _matmul(st["act"], dm, "tn", F32, "d_wd")
        dgu = _swiglu_bwd(st["gu"], dact)
        dh2 = _matmul(dgu, wgu[i], "nt", F32, "d_h2")
        grads["wgu"][i] = _matmul(st["h2"], dgu, "tn", F32, "d_wgu")
        dx1, dsh2, dsc2, grads["norm2_g"][i] = _norm_mod_bwd(
            st["x1"], norm2_g[i:i + 1], st["sc2"], dh2.reshape(NB, S, D), dx)
        dy, dg1 = _gate_res_bwd(dx1, st["y"], st["g1"])
        dy = dy.reshape(T, D)
        if i % 2 == 0:
            do = _matmul(dy, wo_a[j], "nt", BF16, "d_o_a")
            grads["wo_a"][j] = _matmul(st["o"], dy, "tn", F32, "d_wo_a")
            do5 = do.reshape(NB, S, N_KV_A, GROUP_A, HEAD_DIM).transpose(0, 2, 3, 1, 4)
            dq, dk, dv, dsink = _attn_a_bwd(st["q"], st["k"], st["v"], do5, sinks_a[j])
            d3 = jnp.concatenate([
                dq.transpose(0, 3, 1, 2, 4).reshape(T, GROUP_A, LANES),
                dk.transpose(0, 2, 1, 3).reshape(T, 1, LANES),
                dv.transpose(0, 2, 1, 3).reshape(T, 1, LANES)], axis=1)
            dqkv, dgain = _qk_prep_bwd(st["qkv3"], d3, tab_c, tab_s, st["gains"])
            dqkv = dqkv.reshape(T, rows_a * LANES)
            dh = _matmul(dqkv, wqkv_a[j], "nt", F32, "d_h_a")
            grads["wqkv_a"][j] = _matmul(st["h"], dqkv, "tn", F32, "d_wqkv_a")
            grads["q_norm_a"][j] = jnp.sum(dgain[:GROUP_A].reshape(2 * GROUP_A, HEAD_DIM), axis=0)
            grads["k_norm_a"][j] = jnp.sum(dgain[GROUP_A].reshape(2, HEAD_DIM), axis=0)
            grads["sinks_a"][j] = jnp.sum(dsink[..., 0], axis=0).reshape(N_Q_A)
        else:
            do = _matmul(dy, wo_b[j], "nt", BF16, "d_o_b").reshape(NB, S, -1)
            grads["wo_b"][j] = _matmul(st["o"], dy, "tn", F32, "d_wo_b")
            dq, dk, dv = _attn_b_bwd(st["qkv"], do, rev, fwd)
            dqkv = jnp.concatenate([dq, dk, dv], axis=-1).reshape(T, -1).astype(BF16)
            dh = _matmul(dqkv, wqkv_b[j], "nt", F32, "d_h_b")
            grads["wqkv_b"][j] = _matmul(st["h"], dqkv, "tn", F32, "d_wqkv_b")
        dx, dsh1, dsc1, grads["norm1_g"][i] = _norm_mod_bwd(
            st["x"], norm1_g[i:i + 1], st["sc1"], dh.reshape(NB, S, D), dx1)
        dmod[i] = jnp.concatenate([dsh1, dsc1, dg1, dsh2, dsc2, dg2], axis=-1).reshape(NB, 6 * D)

    grads = {name: jnp.stack(parts) for name, parts in grads.items()}
    return loss, dx, grads, jnp.stack(dmod)


def _rows_of(flat, cols=PACK_COLS):
    n = flat.shape[0]
    pad = (-n) % (8 * cols)
    if pad:
        flat = jnp.concatenate([flat, jnp.zeros((pad,), flat.dtype)])
    return flat.reshape(-1, cols)


def kernel(x, c, positions, ada_w, ada_b, norm1_g, norm2_g, wqkv_a, q_norm_a, k_norm_a, sinks_a, wo_a, wqkv_b, wo_b, w_gate, w_up, w_down, loss_target, m_ada_w, m_ada_b, m_norm1_g, m_norm2_g, m_wqkv_a, m_q_norm_a, m_k_norm_a, m_sinks_a, m_wo_a, m_wqkv_b, m_wo_b, m_w_gate, m_w_up, m_w_down, v_ada_w, v_ada_b, v_norm1_g, v_norm2_g, v_wqkv_a, v_q_norm_a, v_k_norm_a, v_sinks_a, v_wo_a, v_wqkv_b, v_wo_b, v_w_gate, v_w_up, v_w_down):
    xi, yi, ci = lax.axis_index("x"), lax.axis_index("y"), lax.axis_index("c")
    dev = 4 * xi + 2 * yi + ci
    chip = 2 * xi + yi
    NB, S, D = x.shape
    B_all = N_DEV * NB
    L = ada_w.shape[0]
    n_mod = ada_w.shape[2] // 2

    c_all = _gather8(_rows_of(c.reshape(-1), LANES), dev, "gather_c").reshape(N_DEV, -1)[:, :NB * D].reshape(B_all, D)
    ada_w_half = lax.dynamic_slice_in_dim(ada_w, ci * n_mod, n_mod, axis=2)
    ada_b_half = lax.dynamic_slice_in_dim(ada_b, dev * n_mod, n_mod, axis=1).reshape(L, 1, n_mod)
    mod_part = _ada_fwd(c_all, ada_w_half, ada_b_half)
    n_part = L * B_all * n_mod
    mod_all = _gather8(_rows_of(mod_part.reshape(-1)), dev, "gather_mod").reshape(N_DEV, -1)[:, :n_part]
    mod_all = mod_all.reshape(N_DEV, L, B_all, n_mod).transpose(1, 2, 0, 3).reshape(L, B_all, N_DEV * n_mod)
    mod = lax.dynamic_slice_in_dim(mod_all, dev * NB, NB, axis=1)

    shards = dict(wqkv_a=wqkv_a, wo_a=wo_a, wqkv_b=wqkv_b, wo_b=wo_b, w_gate=w_gate, w_up=w_up, w_down=w_down)
    halves = []
    for name, _ in _SHARDED:
        w = shards[name]
        half = lax.dynamic_index_in_dim(w.reshape((2, w.shape[0] // 2) + w.shape[1:]), ci, 0, keepdims=False)
        halves.append(half.astype(BF16))
    gathered = _all_gather8(halves, "gather_weights")
    full = {name: _unpack_full(_with_own_slab(t, h, dev), axis) for (name, axis), t, h in zip(_SHARDED, gathered, halves)}
    wgu = _interleave(full["w_gate"], full["w_up"])

    loss, grad_x, g, dmod = _local_step(
        x, positions, mod, norm1_g, norm2_g, q_norm_a, k_norm_a, sinks_a,
        full["wqkv_a"], full["wo_a"], full["wqkv_b"], full["wo_b"], wgu, full["w_down"], loss_target)

    g_gate, g_up = _deinterleave(g["wgu"])
    g_full = dict(wqkv_a=g["wqkv_a"], wo_a=g["wo_a"], wqkv_b=g["wqkv_b"], wo_b=g["wo_b"],
                  w_gate=g_gate, w_up=g_up, w_down=g["wd"])
    packed = [_pack_full(g_full[name], axis) for name, axis in _SHARDED]
    def own(t, index):
        return lax.dynamic_index_in_dim(t, index, 0, keepdims=False)

    from_cores = _exchange(packed, "c", "rs_cores", chunk_axis=0, chunks=4)
    chip_part = [_sum_slabs(own(p, ci), r, "rs_add_cores", with_bf16=True) for p, r in zip(packed, from_cores)]
    from_chips = _exchange([b for _, b in chip_part], "xy", "rs_chips")
    mine = [_sum_slabs(own(p, chip), r, "rs_add_chips") for (p, _), r in zip(chip_part, from_chips)]
    theirs = _sibling_send(mine, "rs_halves")
    grad = {}
    for (name, _), m, t in zip(_SHARDED, mine, theirs):
        first, second = jnp.where(ci == 0, m, t), jnp.where(ci == 0, t, m)
        grad[name] = jnp.stack([first, second]).reshape(shards[name].shape)

    small_names = ("norm1_g", "norm2_g", "q_norm_a", "k_norm_a", "sinks_a")
    small = [dmod.reshape(-1)] + [g[name].reshape(-1) for name in small_names] + [loss.reshape(-1)]
    small_sizes = [t.shape[0] for t in small]
    small_rows = _rows_of(jnp.concatenate(small))
    small_all = _gather8(small_rows, dev, "gather_small")
    small_sum = _sum_leading(small_all, "sum_small").reshape(-1)
    n_dmod = small_sizes[0]
    dmod_all = small_all.reshape(N_DEV, -1)[:, :n_dmod].reshape(N_DEV, L, NB, 6 * D)
    dmod_all = dmod_all.transpose(1, 0, 2, 3).reshape(L, B_all, 6 * D)
    off = n_dmod
    for name, sz in zip(small_names + ("loss",), small_sizes[1:]):
        grad[name] = small_sum[off:off + sz]
        off += sz
    loss_total = grad.pop("loss").reshape(())
    for name, ref in (("norm1_g", norm1_g), ("norm2_g", norm2_g), ("q_norm_a", q_norm_a),
                      ("k_norm_a", k_norm_a), ("sinks_a", sinks_a)):
        grad[name] = grad[name].reshape(ref.shape)

    n_shard = ada_w.shape[2]
    dmod_shard = lax.dynamic_slice_in_dim(dmod_all, chip * n_shard, n_shard, axis=2)
    grad["ada_w"], gb = _ada_bwd(c_all, dmod_all, dmod_shard)
    grad["ada_b"] = gb.reshape(ada_b.shape)

    weights = dict(ada_w=ada_w, ada_b=ada_b, norm1_g=norm1_g, norm2_g=norm2_g, wqkv_a=wqkv_a, q_norm_a=q_norm_a,
                   k_norm_a=k_norm_a, sinks_a=sinks_a, wo_a=wo_a, wqkv_b=wqkv_b, wo_b=wo_b, w_gate=w_gate,
                   w_up=w_up, w_down=w_down)
    m_in = dict(ada_w=m_ada_w, ada_b=m_ada_b, norm1_g=m_norm1_g, norm2_g=m_norm2_g, wqkv_a=m_wqkv_a,
                q_norm_a=m_q_norm_a, k_norm_a=m_k_norm_a, sinks_a=m_sinks_a, wo_a=m_wo_a, wqkv_b=m_wqkv_b,
                wo_b=m_wo_b, w_gate=m_w_gate, w_up=m_w_up, w_down=m_w_down)
    v_in = dict(ada_w=v_ada_w, ada_b=v_ada_b, norm1_g=v_norm1_g, norm2_g=v_norm2_g, wqkv_a=v_wqkv_a,
                q_norm_a=v_q_norm_a, k_norm_a=v_k_norm_a, sinks_a=v_sinks_a, wo_a=v_wo_a, wqkv_b=v_wqkv_b,
                wo_b=v_wo_b, w_gate=v_w_gate, w_up=v_w_up, w_down=v_w_down)
    names = list(weights)
    delta, new_m, new_v = {}, {}, {}
    for name in names:
        delta[name], new_m[name], new_v[name] = _adamw(weights[name], grad[name], m_in[name], v_in[name],
                                                       "adamw_" + name)
    return (loss_total, grad_x, *[grad[k] for k in names], *[delta[k] for k in names],
            *[new_m[k] for k in names], *[new_v[k] for k in names])
```

```python
import jax
import jax.numpy as jnp
from jax import lax
from jax.experimental import pallas as pl
from jax.experimental.pallas import tpu as pltpu

F32 = jnp.float32
BF16 = jnp.bfloat16

DEPTH = 4
HEAD_DIM = 64
N_Q_A = 16
N_KV_A = 2
GROUP_A = N_Q_A // N_KV_A
N_H_B = 16
BLOCK = 128
ROT_DIM = HEAD_DIM // 4
ROPE_THETA = 500000.0
EPS = 1e-6
ATTN_SCALE = HEAD_DIM ** -0.5
NEG_BIG = -1e30

ADAM_LR = 0.001
ADAM_B1 = 0.9
ADAM_B2 = 0.999
ADAM_EPS = 1e-08
ADAM_WD = 0.01
ADAM_STEP = 10

N_DEV = 8
LANES = 128
PACK_COLS = 1024
VMEM_LIMIT_BYTES = 48 * 1024 * 1024
MESH = pl.DeviceIdType.MESH

_NT = (((1,), (1,)), ((), ()))
_TN = (((0,), (0,)), ((), ()))
_NN = (((1,), (0,)), ((), ()))


def _params(sem=None):
    return pltpu.CompilerParams(vmem_limit_bytes=VMEM_LIMIT_BYTES, dimension_semantics=sem)


def _pick(n, cap, mult):
    best = None
    for t in range(mult, min(n, cap) + 1, mult):
        if n % t == 0:
            best = t
    return n if best is None else best


_ANY = pl.BlockSpec(memory_space=pl.ANY)


def _window(index, axis, q, n, shape):
    rest = [slice(None)] * len(shape)
    size = shape[axis] // n
    rest[axis] = pl.ds(q * size, size)
    return tuple(index) + tuple(rest)


def _all_gather8(xs, name, local_axis=0, local_chunks=1):
    n = len(xs)

    def body(*refs):
        x_refs, out_refs = refs[:n], refs[n:2 * n]
        send_sems, recv_sems, local_sems = refs[2 * n:]
        xi, yi, ci = lax.axis_index("x"), lax.axis_index("y"), lax.axis_index("c")
        me, sibling = (xi, yi, ci), (xi, yi, 1 - ci)
        chips = [(1 - xi, yi), (xi, 1 - yi), (1 - xi, 1 - yi)]

        def slab(w, px, py, pc):
            return out_refs[w].at[4 * px + 2 * py + pc]

        def copy(w, k, block, to, src=None):
            return pltpu.make_async_remote_copy(
                src_ref=slab(w, *block) if src is None else src, dst_ref=slab(w, *block),
                send_sem=send_sems.at[k, w], recv_sem=recv_sems.at[k, w], device_id=to, device_id_type=MESH)

        mine = []
        for w in range(n):
            for q in range(local_chunks):
                part = _window((), local_axis, q, local_chunks, xs[w].shape)
                mine.append(pltpu.make_async_copy(x_refs[w].at[part], slab(w, *me).at[part], local_sems.at[w, q]))
                mine[-1].start()
        first = [copy(w, 0, me, sibling, src=x_refs[w]) for w in range(n)]
        first += [copy(w, 1 + j, me, (*chip, ci), src=x_refs[w]) for j, chip in enumerate(chips) for w in range(n)]
        for cp in first:
            cp.start()
        passed = []
        for j, chip in enumerate(chips):
            for w in range(n):
                copy(w, 1 + j, (*chip, ci), me).wait_recv()
                passed.append(copy(w, 4 + j, (*chip, ci), sibling))
                passed[-1].start()
        for w in range(n):
            copy(w, 0, sibling, me).wait_recv()
        for j, chip in enumerate(chips):
            for w in range(n):
                copy(w, 4 + j, (*chip, 1 - ci), me).wait_recv()
        for cp in first + passed:
            cp.wait_send()
        for cp in mine:
            cp.wait()

    return pl.pallas_call(
        body, name=name,
        out_shape=[jax.ShapeDtypeStruct((N_DEV,) + x.shape, x.dtype) for x in xs],
        in_specs=[_ANY] * n, out_specs=[_ANY] * n,
        scratch_shapes=[pltpu.SemaphoreType.DMA((7, n)), pltpu.SemaphoreType.DMA((7, n)),
                        pltpu.SemaphoreType.DMA((n, local_chunks))],
    )(*xs)


def _exchange(xs, group, name, chunk_axis=0, chunks=1):
    n = len(xs)
    n_peers = 1 if group == "c" else 3

    def body(*refs):
        x_refs, out_refs = refs[:n], refs[n:2 * n]
        send_sems, recv_sems = refs[2 * n:]
        xi, yi, ci = lax.axis_index("x"), lax.axis_index("y"), lax.axis_index("c")
        if group == "c":
            peers = [(1 - ci, (xi, yi, 1 - ci))]
        else:
            peers = [(2 * (1 - xi) + yi, (1 - xi, yi, ci)),
                     (2 * xi + (1 - yi), (xi, 1 - yi, ci)),
                     (2 * (1 - xi) + (1 - yi), (1 - xi, 1 - yi, ci))]
        copies = []
        for k, (p, dev) in enumerate(peers):
            for w in range(n):
                slab_shape = xs[w].shape[1:]
                for q in range(chunks):
                    copies.append(pltpu.make_async_remote_copy(
                        src_ref=x_refs[w].at[_window((p,), chunk_axis, q, chunks, slab_shape)],
                        dst_ref=out_refs[w].at[_window((k,), chunk_axis, q, chunks, slab_shape)],
                        send_sem=send_sems.at[k, w, q], recv_sem=recv_sems.at[k, w, q],
                        device_id=dev, device_id_type=MESH))
                    copies[-1].start()
        for cp in copies:
            cp.wait()

    return pl.pallas_call(
        body, name=name,
        out_shape=[jax.ShapeDtypeStruct((n_peers,) + x.shape[1:], x.dtype) for x in xs],
        in_specs=[_ANY] * n, out_specs=[_ANY] * n,
        scratch_shapes=[pltpu.SemaphoreType.DMA((n_peers, n, chunks)), pltpu.SemaphoreType.DMA((n_peers, n, chunks))],
    )(*xs)


def _sibling_send(xs, name, chunk_axis=1, chunks=4):
    n = len(xs)

    def body(*refs):
        x_refs, out_refs = refs[:n], refs[n:2 * n]
        send_sems, recv_sems = refs[2 * n:]
        xi, yi, ci = lax.axis_index("x"), lax.axis_index("y"), lax.axis_index("c")
        copies = []
        for w in range(n):
            for q in range(chunks):
                part = _window((), chunk_axis, q, chunks, xs[w].shape)
                copies.append(pltpu.make_async_remote_copy(
                    src_ref=x_refs[w].at[part], dst_ref=out_refs[w].at[part],
                    send_sem=send_sems.at[w, q], recv_sem=recv_sems.at[w, q],
                    device_id=(xi, yi, 1 - ci), device_id_type=MESH))
                copies[-1].start()
        for cp in copies:
            cp.wait()

    return pl.pallas_call(
        body, name=name,
        out_shape=[jax.ShapeDtypeStruct(x.shape, x.dtype) for x in xs],
        in_specs=[_ANY] * n, out_specs=[_ANY] * n,
        scratch_shapes=[pltpu.SemaphoreType.DMA((n, chunks)), pltpu.SemaphoreType.DMA((n, chunks))],
    )(*xs)


def _sum_leading(x, name, own=None, with_bf16=False):
    P, R, C = x.shape
    tr = _pick(R, max(16, (1 << 19) // (C * (P + 1))), 16)

    def body(*refs):
        n_in = 1 if own is None else 2
        x_ref = refs[n_in - 1]
        acc = x_ref[0].astype(F32) if own is None else refs[0][...] + x_ref[0].astype(F32)
        for p in range(1, P):
            acc = acc + x_ref[p].astype(F32)
        refs[n_in][...] = acc
        if with_bf16:
            refs[n_in + 1][...] = acc.astype(BF16)

    flat = pl.BlockSpec((tr, C), lambda r: (r, 0))
    slabs = pl.BlockSpec((P, tr, C), lambda r: (0, r, 0))
    out = pl.pallas_call(
        body, name=name, grid=(R // tr,),
        in_specs=[slabs] if own is None else [flat, slabs],
        out_specs=[flat, flat] if with_bf16 else [flat],
        out_shape=[jax.ShapeDtypeStruct((R, C), F32)] + ([jax.ShapeDtypeStruct((R, C), BF16)] if with_bf16 else []),
        compiler_params=_params(("arbitrary",)),
    )(*([x] if own is None else [own, x]))
    return out if with_bf16 else out[0]


def _matmul(a, b, mode, out_dtype, name):
    if mode == "nn":
        (M, K), N = a.shape, b.shape[1]
    elif mode == "nt":
        (M, K), N = a.shape, b.shape[0]
    else:
        (K, M), N = a.shape, b.shape[1]
    tm = _pick(M, 1024 if mode != "tn" else 1536, 128)
    tn = _pick(N, 1536, 128)
    tk = _pick(K, 512, 128)
    nk = K // tk
    dims = {"nn": _NN, "nt": _NT, "tn": _TN}[mode]

    def body(a_ref, b_ref, o_ref, acc_ref):
        k = pl.program_id(2)

        @pl.when(k == 0)
        def _():
            acc_ref[...] = jnp.zeros_like(acc_ref)

        acc_ref[...] += lax.dot_general(a_ref[...].astype(BF16), b_ref[...].astype(BF16), dims,
                                        preferred_element_type=F32)

        @pl.when(k == nk - 1)
        def _():
            o_ref[...] = acc_ref[...].astype(o_ref.dtype)

    if mode == "tn":
        a_spec = pl.BlockSpec((tk, tm), lambda i, j, k: (k, i))
    else:
        a_spec = pl.BlockSpec((tm, tk), lambda i, j, k: (i, k))
    if mode == "nt":
        b_spec = pl.BlockSpec((tn, tk), lambda i, j, k: (j, k))
    else:
        b_spec = pl.BlockSpec((tk, tn), lambda i, j, k: (k, j))
    return pl.pallas_call(
        body, name=name, grid=(M // tm, N // tn, nk),
        in_specs=[a_spec, b_spec],
        out_specs=pl.BlockSpec((tm, tn), lambda i, j, k: (i, j)),
        out_shape=jax.ShapeDtypeStruct((M, N), out_dtype),
        scratch_shapes=[pltpu.VMEM((tm, tn), F32)],
        compiler_params=_params(("parallel", "parallel", "arbitrary")),
    )(a, b)


def _row_tile(S):
    return _pick(S, 512, 8)


def _norm_mod_fwd(x, gain, sc, sh):
    NB, S, D = x.shape
    tr = _row_tile(S)

    def body(x_ref, g_ref, sc_ref, sh_ref, h_ref):
        xv = x_ref[...]
        ms = jnp.mean(xv * xv, axis=-1, keepdims=True)
        n = xv * lax.rsqrt(ms + EPS) * g_ref[...]
        h_ref[...] = (n * (1.0 + sc_ref[...]) + sh_ref[...]).astype(BF16)

    tok = pl.BlockSpec((None, tr, D), lambda b, r: (b, r, 0))
    per_ex = pl.BlockSpec((None, 1, D), lambda b, r: (b, 0, 0))
    return pl.pallas_call(
        body, name="norm_mod_fwd", grid=(NB, S // tr),
        in_specs=[tok, pl.BlockSpec((1, D), lambda b, r: (0, 0)), per_ex, per_ex],
        out_specs=tok, out_shape=jax.ShapeDtypeStruct((NB, S, D), BF16),
        compiler_params=_params(("parallel", "parallel")),
    )(x, gain, sc, sh)


def _norm_mod_bwd(x, gain, sc, dh, dres):
    NB, S, D = x.shape
    tr = _row_tile(S)

    def body(x_ref, g_ref, sc_ref, dh_ref, dres_ref, dx_ref, dsh_ref, dsc_ref, dg_ref):
        b, r = pl.program_id(0), pl.program_id(1)

        @pl.when(r == 0)
        def _():
            dsh_ref[...] = jnp.zeros_like(dsh_ref)
            dsc_ref[...] = jnp.zeros_like(dsc_ref)

        @pl.when((r == 0) & (b == 0))
        def _():
            dg_ref[...] = jnp.zeros_like(dg_ref)

        xv = x_ref[...]
        rstd = lax.rsqrt(jnp.mean(xv * xv, axis=-1, keepdims=True) + EPS)
        xh = xv * rstd
        g = g_ref[...]
        dh = dh_ref[...]
        dsh_ref[...] += jnp.sum(dh, axis=0, keepdims=True)
        dsc_ref[...] += jnp.sum(dh * (xh * g), axis=0, keepdims=True)
        dn = dh * (1.0 + sc_ref[...])
        dg_ref[...] += jnp.sum(dn * xh, axis=0, keepdims=True)
        dxh = dn * g
        proj = jnp.mean(dxh * xh, axis=-1, keepdims=True)
        dx_ref[...] = rstd * (dxh - xh * proj) + dres_ref[...]

    tok = pl.BlockSpec((None, tr, D), lambda b, r: (b, r, 0))
    per_ex = pl.BlockSpec((None, 1, D), lambda b, r: (b, 0, 0))
    row = pl.BlockSpec((1, D), lambda b, r: (0, 0))
    return pl.pallas_call(
        body, name="norm_mod_bwd", grid=(NB, S // tr),
        in_specs=[tok, row, per_ex, tok, tok],
        out_specs=[tok, per_ex, per_ex, row],
        out_shape=[jax.ShapeDtypeStruct((NB, S, D), F32), jax.ShapeDtypeStruct((NB, 1, D), F32),
                   jax.ShapeDtypeStruct((NB, 1, D), F32), jax.ShapeDtypeStruct((1, D), F32)],
        compiler_params=_params(("arbitrary", "arbitrary")),
    )(x, gain, sc, dh, dres)


def _gate_res(x, y, g):
    NB, S, D = x.shape
    tr = _row_tile(S)

    def body(x_ref, y_ref, g_ref, o_ref):
        o_ref[...] = x_ref[...] + g_ref[...] * y_ref[...]

    tok = pl.BlockSpec((None, tr, D), lambda b, r: (b, r, 0))
    per_ex = pl.BlockSpec((None, 1, D), lambda b, r: (b, 0, 0))
    return pl.pallas_call(
        body, name="gate_res", grid=(NB, S // tr), in_specs=[tok, tok, per_ex], out_specs=tok,
        out_shape=jax.ShapeDtypeStruct((NB, S, D), F32),
        compiler_params=_params(("parallel", "parallel")),
    )(x, y, g)


def _gate_res_bwd(dxo, y, g):
    NB, S, D = dxo.shape
    tr = _row_tile(S)

    def body(d_ref, y_ref, g_ref, dy_ref, dg_ref):
        @pl.when(pl.program_id(1) == 0)
        def _():
            dg_ref[...] = jnp.zeros_like(dg_ref)

        d = d_ref[...]
        dy_ref[...] = (d * g_ref[...]).astype(BF16)
        dg_ref[...] += jnp.sum(d * y_ref[...], axis=0, keepdims=True)

    tok = pl.BlockSpec((None, tr, D), lambda b, r: (b, r, 0))
    per_ex = pl.BlockSpec((None, 1, D), lambda b, r: (b, 0, 0))
    return pl.pallas_call(
        body, name="gate_res_bwd", grid=(NB, S // tr), in_specs=[tok, tok, per_ex], out_specs=[tok, per_ex],
        out_shape=[jax.ShapeDtypeStruct((NB, S, D), BF16), jax.ShapeDtypeStruct((NB, 1, D), F32)],
        compiler_params=_params(("arbitrary", "arbitrary")),
    )(dxo, y, g)


def _sigmoid(v):
    return 1.0 / (1.0 + jnp.exp(-v))


def _ff_tile(F):
    return _pick(F, 1536, 128)


def _interleave(gate, up):
    F = gate.shape[-1]
    tf = _ff_tile(F)
    lead = gate.shape[:-1]
    parts = jnp.stack([gate.reshape(lead + (F // tf, tf)), up.reshape(lead + (F // tf, tf))], axis=-2)
    return parts.reshape(lead + (2 * F,))


def _deinterleave(gu):
    F = gu.shape[-1] // 2
    tf = _ff_tile(F)
    lead = gu.shape[:-1]
    parts = gu.reshape(lead + (F // tf, 2, tf))
    return parts[..., 0, :].reshape(lead + (F,)), parts[..., 1, :].reshape(lead + (F,))


def _swiglu_fwd(gu):
    T, F2 = gu.shape
    F = F2 // 2
    tf = _ff_tile(F)
    tr = _pick(T, 256, 8)

    def body(gu_ref, o_ref):
        g = gu_ref[:, :tf]
        o_ref[...] = (g * _sigmoid(g) * gu_ref[:, tf:]).astype(BF16)

    return pl.pallas_call(
        body, name="swiglu_fwd", grid=(T // tr, F // tf),
        in_specs=[pl.BlockSpec((tr, 2 * tf), lambda i, j: (i, j))],
        out_specs=pl.BlockSpec((tr, tf), lambda i, j: (i, j)),
        out_shape=jax.ShapeDtypeStruct((T, F), BF16),
        compiler_params=_params(("parallel", "parallel")),
    )(gu)


def _swiglu_bwd(gu, dact):
    T, F2 = gu.shape
    F = F2 // 2
    tf = _ff_tile(F)
    tr = _pick(T, 256, 8)

    def body(gu_ref, d_ref, o_ref):
        g, u, d = gu_ref[:, :tf], gu_ref[:, tf:], d_ref[...]
        s = _sigmoid(g)
        o_ref[:, :tf] = (d * u * (s * (1.0 + g * (1.0 - s)))).astype(BF16)
        o_ref[:, tf:] = (d * (g * s)).astype(BF16)

    return pl.pallas_call(
        body, name="swiglu_bwd", grid=(T // tr, F // tf),
        in_specs=[pl.BlockSpec((tr, 2 * tf), lambda i, j: (i, j)), pl.BlockSpec((tr, tf), lambda i, j: (i, j))],
        out_specs=pl.BlockSpec((tr, 2 * tf), lambda i, j: (i, j)),
        out_shape=jax.ShapeDtypeStruct((T, F2), BF16),
        compiler_params=_params(("parallel", "parallel")),
    )(gu, dact)


def _loss_fwd_bwd(y, target):
    NB, S, D = y.shape
    tr = _row_tile(S)

    def body(y_ref, t_ref, l_ref, d_ref):
        @pl.when((pl.program_id(0) == 0) & (pl.program_id(1) == 0))
        def _():
            l_ref[...] = jnp.zeros_like(l_ref)

        e = y_ref[...] - t_ref[...]
        d_ref[...] = e / D
        l_ref[...] += 0.5 * jnp.sum(jnp.mean(e * e, axis=-1, keepdims=True), axis=0, keepdims=True)

    tok = pl.BlockSpec((None, tr, D), lambda b, r: (b, r, 0))
    return pl.pallas_call(
        body, name="loss", grid=(NB, S // tr), in_specs=[tok, tok],
        out_specs=[pl.BlockSpec((1, 1), lambda b, r: (0, 0)), tok],
        out_shape=[jax.ShapeDtypeStruct((1, 1), F32), jax.ShapeDtypeStruct((NB, S, D), F32)],
        compiler_params=_params(("arbitrary", "arbitrary")),
    )(y, target)


def _half_sums(v, lo):
    sa = jnp.sum(jnp.where(lo, v, 0.0), axis=-1, keepdims=True)
    sb = jnp.sum(jnp.where(lo, 0.0, v), axis=-1, keepdims=True)
    return jnp.where(lo, sa, sb)


def _rope_swap(v, lane64):
    up = pltpu.roll(v, LANES - ROT_DIM // 2, 2)
    down = pltpu.roll(v, ROT_DIM // 2, 2)
    return jnp.where(lane64 < ROT_DIM // 2, up, jnp.where(lane64 < ROT_DIM, down, 0.0))


def _qk_prep_fwd(qkv3, tab_c, tab_s, gains):
    T = qkv3.shape[0]
    R = qkv3.shape[1]
    tt = _pick(T, 256, 8)

    def body(x_ref, c_ref, s_ref, g_ref, o_ref):
        xv = x_ref[...]
        lane = lax.broadcasted_iota(jnp.int32, xv.shape, 2)
        row = lax.broadcasted_iota(jnp.int32, xv.shape, 1)
        lo = lane < HEAD_DIM
        rstd = lax.rsqrt(_half_sums(xv * xv, lo) / HEAD_DIM + EPS)
        yn = xv * rstd * g_ref[...]
        roped = yn * c_ref[...] + _rope_swap(yn, lane & (HEAD_DIM - 1)) * s_ref[...]
        o_ref[...] = jnp.where(row == R - 1, xv, roped).astype(BF16)

    tok = pl.BlockSpec((tt, R, LANES), lambda t: (t, 0, 0))
    tab = pl.BlockSpec((tt, 1, LANES), lambda t: (t, 0, 0))
    return pl.pallas_call(
        body, name="qk_prep_fwd", grid=(T // tt,),
        in_specs=[tok, tab, tab, pl.BlockSpec((R, LANES), lambda t: (0, 0))],
        out_specs=tok, out_shape=jax.ShapeDtypeStruct(qkv3.shape, BF16),
        compiler_params=_params(("parallel",)),
    )(qkv3, tab_c, tab_s, gains)


def _qk_prep_bwd(qkv3, d3, tab_c, tab_s, gains):
    T = qkv3.shape[0]
    R = qkv3.shape[1]
    tt = _pick(T, 256, 8)

    def body(x_ref, d_ref, c_ref, s_ref, g_ref, o_ref, dg_ref):
        @pl.when(pl.program_id(0) == 0)
        def _():
            dg_ref[...] = jnp.zeros_like(dg_ref)

        xv, d = x_ref[...], d_ref[...]
        lane = lax.broadcasted_iota(jnp.int32, xv.shape, 2)
        row = lax.broadcasted_iota(jnp.int32, xv.shape, 1)
        lo = lane < HEAD_DIM
        rstd = lax.rsqrt(_half_sums(xv * xv, lo) / HEAD_DIM + EPS)
        xh = xv * rstd
        dyn = d * c_ref[...] + _rope_swap(d * s_ref[...], lane & (HEAD_DIM - 1))
        dg_ref[...] += jnp.sum(dyn * xh, axis=0)
        dxh = dyn * g_ref[...]
        proj = _half_sums(dxh * xh, lo) / HEAD_DIM
        dx = rstd * (dxh - xh * proj)
        o_ref[...] = jnp.where(row == R - 1, d, dx).astype(BF16)

    tok = pl.BlockSpec((tt, R, LANES), lambda t: (t, 0, 0))
    tab = pl.BlockSpec((tt, 1, LANES), lambda t: (t, 0, 0))
    gsp = pl.BlockSpec((R, LANES), lambda t: (0, 0))
    return pl.pallas_call(
        body, name="qk_prep_bwd", grid=(T // tt,),
        in_specs=[tok, tok, tab, tab, gsp], out_specs=[tok, gsp],
        out_shape=[jax.ShapeDtypeStruct(qkv3.shape, BF16), jax.ShapeDtypeStruct((R, LANES), F32)],
        compiler_params=_params(("arbitrary",)),
    )(qkv3, d3, tab_c, tab_s, gains)


def _band_mask(i):
    r = lax.broadcasted_iota(jnp.int32, (BLOCK, 2 * BLOCK), 0)
    c = lax.broadcasted_iota(jnp.int32, (BLOCK, 2 * BLOCK), 1)
    rel = r + BLOCK - c
    return (rel >= 0) & (rel < BLOCK) & ((c >= BLOCK) | (i > 0))


def _swa_probs(qg, k2, valid, sink):
    s = lax.dot_general(qg, k2, _NT, preferred_element_type=F32) * ATTN_SCALE
    s = jnp.where(valid, s, NEG_BIG)
    m = jnp.maximum(jnp.max(s, axis=1, keepdims=True), sink)
    p = jnp.exp(s - m)
    ps = jnp.exp(sink - m)
    denom = jnp.sum(p, axis=1, keepdims=True) + ps
    return p / denom, ps / denom


Q_WIDTH_A = N_Q_A * HEAD_DIM
N_PAIR_A = Q_WIDTH_A // LANES


def _swa_specs():
    qs = pl.BlockSpec((None, BLOCK, Q_WIDTH_A), lambda b, i: (b, i, 0))

    def kv(col, back):
        return pl.BlockSpec((None, BLOCK, LANES), lambda b, i: (b, jnp.maximum(i - back, 0), col))

    return qs, kv(N_PAIR_A, 1), kv(N_PAIR_A, 0), kv(N_PAIR_A + 1, 1), kv(N_PAIR_A + 1, 0)


def _dup_heads(t):
    lo = lax.broadcasted_iota(jnp.int32, t.shape, 1) < HEAD_DIM
    sw = pltpu.roll(t.astype(F32), HEAD_DIM, 1).astype(BF16)
    return jnp.where(lo, t, sw), jnp.where(lo, sw, t)


def _kv_tiles(kp_ref, kc_ref, vp_ref, vc_ref):
    kd = _dup_heads(jnp.concatenate([kp_ref[...], kc_ref[...]], axis=0))
    vd = _dup_heads(jnp.concatenate([vp_ref[...], vc_ref[...]], axis=0))
    return kd, vd


def _attn_a_fwd(qkn, sinks):
    NB, S, _ = qkn.shape
    qs, kp, kc, vp, vc = _swa_specs()

    def body(q_ref, kp_ref, kc_ref, vp_ref, vc_ref, sink_ref, o_ref):
        i = pl.program_id(1)
        kd, vd = _kv_tiles(kp_ref, kc_ref, vp_ref, vc_ref)
        valid = _band_mask(i)
        lo = lax.broadcasted_iota(jnp.int32, (BLOCK, LANES), 1) < HEAD_DIM
        for pair in range(N_PAIR_A):
            cols = slice(pair * LANES, (pair + 1) * LANES)
            qh = _head_halves(q_ref[:, cols], lo)
            kvh = 2 * pair // GROUP_A
            outs = []
            for hh in range(2):
                pn, _ = _swa_probs(qh[hh], kd[kvh], valid, sink_ref[2 * pair + hh])
                outs.append(jnp.dot(pn.astype(BF16), vd[kvh], preferred_element_type=F32))
            o_ref[:, cols] = jnp.where(lo, outs[0], outs[1]).astype(BF16)

    return pl.pallas_call(
        body, name="attn_a_fwd", grid=(NB, S // BLOCK),
        in_specs=[qs, kp, kc, vp, vc, pl.BlockSpec(memory_space=pltpu.SMEM)],
        out_specs=qs, out_shape=jax.ShapeDtypeStruct((NB, S, Q_WIDTH_A), BF16),
        compiler_params=_params(("parallel", "arbitrary")),
    )(qkn, qkn, qkn, qkn, qkn, sinks)


def _attn_a_bwd(qkn, do, sinks):
    NB, S, _ = qkn.shape
    qs, kp, kc, vp, vc = _swa_specs()
    full = pl.BlockSpec((None, S, LANES), lambda b, i: (b, 0, 0))
    sink_out = pl.BlockSpec((None, N_Q_A, LANES), lambda b, i: (b, 0, 0))

    def body(q_ref, do_ref, kp_ref, kc_ref, vp_ref, vc_ref, sink_ref, dq_ref, dk_ref, dv_ref, ds_ref, dk_s, dv_s):
        i = pl.program_id(1)

        @pl.when(i == 0)
        def _():
            dk_ref[...] = jnp.zeros_like(dk_ref)
            dv_ref[...] = jnp.zeros_like(dv_ref)
            ds_ref[...] = jnp.zeros_like(ds_ref)

        dk_s[...] = jnp.zeros_like(dk_s)
        dv_s[...] = jnp.zeros_like(dv_s)
        kd, vd = _kv_tiles(kp_ref, kc_ref, vp_ref, vc_ref)
        valid = _band_mask(i)
        lo = lax.broadcasted_iota(jnp.int32, (BLOCK, LANES), 1) < HEAD_DIM
        for pair in range(N_PAIR_A):
            cols = slice(pair * LANES, (pair + 1) * LANES)
            q2, do2 = q_ref[:, cols], do_ref[:, cols]
            qh, doh = _head_halves(q2, lo), _head_halves(do2, lo)
            kvh = 2 * pair // GROUP_A
            dqs = []
            for hh in range(2):
                h = 2 * pair + hh
                pn, psink = _swa_probs(qh[hh], kd[kvh], valid, sink_ref[h])
                dp = lax.dot_general(doh[hh], vd[kvh], _NT, preferred_element_type=F32)
                delta = jnp.sum(pn * dp, axis=1, keepdims=True)
                dsb = (pn * (dp - delta) * ATTN_SCALE).astype(BF16)
                dqs.append(jnp.dot(dsb, kd[kvh], preferred_element_type=F32))
                dk_s[2 * kvh + hh] += lax.dot_general(dsb, q2, _TN, preferred_element_type=F32)
                dv_s[2 * kvh + hh] += lax.dot_general(pn.astype(BF16), do2, _TN, preferred_element_type=F32)
                dsink = -jnp.sum(psink * delta, axis=0, keepdims=True)
                ds_ref[h:h + 1, :] += jnp.broadcast_to(dsink, (1, LANES))
            dq_ref[:, cols] = jnp.where(lo, dqs[0], dqs[1])

        lo2 = lax.broadcasted_iota(jnp.int32, (2 * BLOCK, LANES), 1) < HEAD_DIM

        def fold(acc):
            head0 = acc[0] + pltpu.roll(acc[1], HEAD_DIM, 1)
            head1 = pltpu.roll(acc[2], HEAD_DIM, 1) + acc[3]
            return jnp.where(lo2, head0, head1)

        dk2, dv2 = fold(dk_s), fold(dv_s)

        @pl.when(i > 0)
        def _():
            start = pl.multiple_of((i - 1) * BLOCK, BLOCK)
            dk_ref[pl.ds(start, 2 * BLOCK), :] += dk2
            dv_ref[pl.ds(start, 2 * BLOCK), :] += dv2

        @pl.when(i == 0)
        def _():
            dk_ref[0:BLOCK, :] += dk2[BLOCK:, :]
            dv_ref[0:BLOCK, :] += dv2[BLOCK:, :]

    slots = pltpu.VMEM((2 * N_KV_A, 2 * BLOCK, LANES), F32)
    return pl.pallas_call(
        body, name="attn_a_bwd", grid=(NB, S // BLOCK),
        in_specs=[qs, qs, kp, kc, vp, vc, pl.BlockSpec(memory_space=pltpu.SMEM)],
        out_specs=[qs, full, full, sink_out],
        out_shape=[jax.ShapeDtypeStruct((NB, S, Q_WIDTH_A), F32), jax.ShapeDtypeStruct((NB, S, LANES), F32),
                   jax.ShapeDtypeStruct((NB, S, LANES), F32), jax.ShapeDtypeStruct((NB, N_Q_A, LANES), F32)],
        scratch_shapes=[slots, slots],
        compiler_params=_params(("parallel", "arbitrary")),
    )(qkn, do, qkn, qkn, qkn, qkn, sinks)


def _cumsum_mats():
    src = lax.broadcasted_iota(jnp.int32, (2 * BLOCK, 2 * BLOCK), 0) % BLOCK
    dst = lax.broadcasted_iota(jnp.int32, (2 * BLOCK, 2 * BLOCK), 1)
    ones = dst >= BLOCK
    rev = ((src > dst) | ones).astype(BF16)
    fwd = ((src < dst) | ones).astype(BF16)
    return rev, fwd


def _cumsum_mxu(v, mat):
    hi = v.astype(BF16)
    lo = (v - hi.astype(F32)).astype(BF16)
    r = jnp.dot(jnp.concatenate([hi, lo], axis=1), mat, preferred_element_type=F32)
    return r[:, :BLOCK], r[:, BLOCK:]


def _sb_logs(qs, kj):
    z = lax.dot_general(qs, kj, _NT, preferred_element_type=F32)
    sp = jnp.log(1.0 + jnp.exp(-jnp.abs(z)))
    return jnp.minimum(z, 0.0) - sp, -(jnp.maximum(z, 0.0) + sp)


def _strict_mask():
    r = lax.broadcasted_iota(jnp.int32, (BLOCK, BLOCK), 0)
    c = lax.broadcasted_iota(jnp.int32, (BLOCK, BLOCK), 1)
    return c < r


def _tile(ref, j):
    return ref[pl.ds(pl.multiple_of(j * BLOCK, BLOCK), BLOCK), :]


SWEEP_EXIT = -88.0


def _head_halves(t, lo):
    zero = jnp.zeros_like(t)
    return jnp.where(lo, t, zero), jnp.where(lo, zero, t)


def _sb_specs(S, HD, width):
    n = HD // width
    blk = pl.BlockSpec((None, BLOCK, width), lambda b, p, i: (b, i, p))
    k_full = pl.BlockSpec((None, S, width), lambda b, p, i: (b, 0, n + p))
    v_full = pl.BlockSpec((None, S, width), lambda b, p, i: (b, 0, 2 * n + p))
    mat = pl.BlockSpec((2 * BLOCK, 2 * BLOCK), lambda b, p, i: (0, 0))
    return blk, k_full, v_full, mat


SB_FWD_PAIRS = 2


def _attn_b_fwd(qkv, rev):
    NB, S, W = qkv.shape
    HD = W // 3
    width = SB_FWD_PAIRS * LANES
    n_heads = 2 * SB_FWD_PAIRS
    blk, k_full, v_full, mat = _sb_specs(S, HD, width)

    def body(q_ref, k_ref, v_ref, rev_ref, o_ref):
        i = pl.program_id(2)
        rv = rev_ref[...]
        mask = _strict_mask()
        lo = lax.broadcasted_iota(jnp.int32, (BLOCK, LANES), 1) < HEAD_DIM
        q_all = q_ref[...]
        qh = []
        for p in range(SB_FWD_PAIRS):
            qh.extend(_head_halves(q_all[:, p * LANES:(p + 1) * LANES] * ATTN_SCALE, lo))

        def pair_tiles(ref, j):
            t = _tile(ref, j)
            return [t[:, p * LANES:(p + 1) * LANES] for p in range(SB_FWD_PAIRS)]

        def merge(outs):
            return [jnp.where(lo, outs[2 * p], outs[2 * p + 1]) for p in range(SB_FWD_PAIRS)]

        ks, vs = pair_tiles(k_ref, i), pair_tiles(v_ref, i)
        carries, outs = [], []
        for h in range(n_heads):
            lb, lm = _sb_logs(qh[h], ks[h // 2])
            after, rs = _cumsum_mxu(jnp.where(mask, lm, 0.0), rv)
            a = jnp.where(mask, jnp.exp(lb + after), 0.0)
            outs.append(jnp.dot(a.astype(BF16), vs[h // 2], preferred_element_type=F32))
            carries.append(rs)
        accs = merge(outs)

        def live(cs):
            top = cs[0]
            for c in cs[1:]:
                top = jnp.maximum(top, c)
            return jnp.max(top) > SWEEP_EXIT

        def cond(st):
            return (st[0] < i) & st[1]

        def step(st):
            jj, _, cs, accs = st
            j = i - 1 - jj
            ks, vs = pair_tiles(k_ref, j), pair_tiles(v_ref, j)
            new_c, outs = [], []
            for h in range(n_heads):
                lb, lm = _sb_logs(qh[h], ks[h // 2])
                after, rs = _cumsum_mxu(lm, rv)
                a = jnp.exp(lb + after + cs[h])
                outs.append(jnp.dot(a.astype(BF16), vs[h // 2], preferred_element_type=F32))
                new_c.append(cs[h] + rs)
            accs = [acc + o for acc, o in zip(accs, merge(outs))]
            return jj + 1, live(new_c), new_c, accs

        st = lax.while_loop(cond, step, (jnp.int32(0), live(carries), carries, accs))
        for p in range(SB_FWD_PAIRS):
            o_ref[:, p * LANES:(p + 1) * LANES] = st[3][p].astype(BF16)

    return pl.pallas_call(
        body, name="attn_b_fwd", grid=(NB, HD // width, S // BLOCK),
        in_specs=[blk, k_full, v_full, mat], out_specs=blk,
        out_shape=jax.ShapeDtypeStruct((NB, S, HD), BF16),
        compiler_params=_params(("parallel", "parallel", "arbitrary")),
    )(qkv, qkv, qkv, rev)


def _attn_b_bwd(qkv, do, rev, fwd):
    NB, S, W = qkv.shape
    HD = W // 3
    n_pair = HD // LANES
    nj = S // BLOCK
    blk, k_full, v_full, mat = _sb_specs(S, HD, LANES)
    acc_full = pl.BlockSpec((None, S, LANES), lambda b, p, i: (b, 0, p))

    def body(q_ref, do_ref, k_ref, v_ref, rev_ref, fwd_ref, dq_ref, dk_ref, dv_ref, sig_s, a_s, e_s):
        i = pl.program_id(2)

        @pl.when(i == 0)
        def _():
            dk_ref[...] = jnp.zeros_like(dk_ref)
            dv_ref[...] = jnp.zeros_like(dv_ref)

        q2, do2 = q_ref[...], do_ref[...]
        rv, fw = rev_ref[...], fwd_ref[...]
        mask = _strict_mask()
        lo = lax.broadcasted_iota(jnp.int32, (BLOCK, LANES), 1) < HEAD_DIM
        qh = _head_halves(q2 * ATTN_SCALE, lo)
        doh = _head_halves(do2, lo)

        def stash(h, j, vj, lb, a):
            da = lax.dot_general(doh[h], vj, _NT, preferred_element_type=F32)
            sig_s[h, j] = jnp.exp(lb)
            a_s[h, j] = a
            e_s[h, j] = da * a

        kj, vj = _tile(k_ref, i), _tile(v_ref, i)
        carries = []
        for h in range(2):
            lb, lm = _sb_logs(qh[h], kj)
            after, rs = _cumsum_mxu(jnp.where(mask, lm, 0.0), rv)
            stash(h, i, vj, lb, jnp.where(mask, jnp.exp(lb + after), 0.0))
            carries.append(rs)

        def live(c0, c1):
            return jnp.max(jnp.maximum(c0, c1)) > SWEEP_EXIT

        def cond(st):
            return (st[0] < i) & st[1]

        def sweep1(st):
            jj, _, c0, c1 = st
            j = i - 1 - jj
            kj, vj = _tile(k_ref, j), _tile(v_ref, j)
            new_c = []
            for h, carry in enumerate((c0, c1)):
                lb, lm = _sb_logs(qh[h], kj)
                after, rs = _cumsum_mxu(lm, rv)
                stash(h, j, vj, lb, jnp.exp(lb + after + carry))
                new_c.append(carry + rs)
            return jj + 1, live(*new_c), new_c[0], new_c[1]

        visited = lax.while_loop(cond, sweep1, (jnp.int32(0), live(*carries), carries[0], carries[1]))[0]

        def grads(j, st, diagonal):
            p0, p1, dq = st
            kj = _tile(k_ref, j)
            new_p, dqs, dks, dvs = [], [], [], []
            for h, prefix in enumerate((p0, p1)):
                e, a, sg = e_s[h, j], a_s[h, j], sig_s[h, j]
                e_before, rs = _cumsum_mxu(e, fw)
                dz = (e * (1.0 - sg) - (e_before + prefix) * sg) * ATTN_SCALE
                if diagonal:
                    dz = jnp.where(mask, dz, 0.0)
                dzb = dz.astype(BF16)
                dqs.append(jnp.dot(dzb, kj, preferred_element_type=F32))
                dks.append(lax.dot_general(dzb, q2, _TN, preferred_element_type=F32))
                dvs.append(lax.dot_general(a.astype(BF16), do2, _TN, preferred_element_type=F32))
                new_p.append(prefix + rs)
            rows = pl.ds(pl.multiple_of(j * BLOCK, BLOCK), BLOCK)
            dk_ref[rows, :] += jnp.where(lo, dks[0], dks[1])
            dv_ref[rows, :] += jnp.where(lo, dvs[0], dvs[1])
            return new_p[0], new_p[1], dq + jnp.where(lo, dqs[0], dqs[1])

        zeros = jnp.zeros((BLOCK, BLOCK), F32)
        st = lax.fori_loop(i - visited, i, lambda j, st: grads(j, st, False), (zeros, zeros, zeros))
        dq_ref[...] = grads(i, st, True)[2]

    tile_stash = pltpu.VMEM((2, nj, BLOCK, BLOCK), F32)
    return pl.pallas_call(
        body, name="attn_b_bwd", grid=(NB, n_pair, nj),
        in_specs=[blk, blk, k_full, v_full, mat, mat], out_specs=[blk, acc_full, acc_full],
        out_shape=[jax.ShapeDtypeStruct((NB, S, HD), F32)] * 3,
        scratch_shapes=[tile_stash, tile_stash, tile_stash],
        compiler_params=_params(("parallel", "parallel", "arbitrary")),
    )(qkv, do, qkv, qkv, rev, fwd)


def _ada_fwd(c_all, w, b):
    L, D, N = w.shape
    B = c_all.shape[0]

    def body(c_ref, w_ref, b_ref, o_ref):
        cv = c_ref[...]
        cond = (cv * _sigmoid(cv)).astype(BF16)
        o_ref[...] = jnp.dot(cond, w_ref[...].astype(BF16), preferred_element_type=F32) + b_ref[...]

    return pl.pallas_call(
        body, name="ada_fwd", grid=(L,),
        in_specs=[pl.BlockSpec((B, D), lambda l: (0, 0)), pl.BlockSpec((None, D, N), lambda l: (l, 0, 0)),
                  pl.BlockSpec((None, 1, N), lambda l: (l, 0, 0))],
        out_specs=pl.BlockSpec((None, B, N), lambda l: (l, 0, 0)),
        out_shape=jax.ShapeDtypeStruct((L, B, N), F32),
        compiler_params=_params(("parallel",)),
    )(c_all, w, b)


def _ada_bwd(c_all, dmod_all, dmod_shard):
    L, B, N = dmod_shard.shape
    D = c_all.shape[1]
    N_all = dmod_all.shape[2]

    def body(c_ref, da_ref, ds_ref, gw_ref, gb_ref):
        cv = c_ref[...]
        cond = (cv * _sigmoid(cv)).astype(BF16)
        gw_ref[...] = lax.dot_general(cond, ds_ref[...].astype(BF16), _TN, preferred_element_type=F32)
        gb_ref[...] = jnp.sum(da_ref[...], axis=0, keepdims=True)

    return pl.pallas_call(
        body, name="ada_bwd", grid=(L,),
        in_specs=[pl.BlockSpec((B, D), lambda l: (0, 0)), pl.BlockSpec((None, B, N_all), lambda l: (l, 0, 0)),
                  pl.BlockSpec((None, B, N), lambda l: (l, 0, 0))],
        out_specs=[pl.BlockSpec((None, D, N), lambda l: (l, 0, 0)), pl.BlockSpec((None, 1, N_all), lambda l: (l, 0, 0))],
        out_shape=[jax.ShapeDtypeStruct((L, D, N), F32), jax.ShapeDtypeStruct((L, 1, N_all), F32)],
        compiler_params=_params(("parallel",)),
    )(c_all, dmod_all, dmod_shard)


def _adamw(w, g, m, v, name):
    shape = w.shape
    C = shape[-1]
    R = w.size // C
    tr = _pick(R, max(8, (1 << 18) // C), 8)
    c1 = 1.0 - ADAM_B1 ** ADAM_STEP
    c2 = 1.0 - ADAM_B2 ** ADAM_STEP

    def body(w_ref, g_ref, m_ref, v_ref, d_ref, nm_ref, nv_ref):
        gv = g_ref[...]
        nm = ADAM_B1 * m_ref[...] + (1.0 - ADAM_B1) * gv
        nv = ADAM_B2 * v_ref[...] + (1.0 - ADAM_B2) * (gv * gv)
        d_ref[...] = -ADAM_LR * ((nm / c1) / (jnp.sqrt(nv / c2) + ADAM_EPS) + ADAM_WD * w_ref[...])
        nm_ref[...] = nm
        nv_ref[...] = nv

    spec = pl.BlockSpec((tr, C), lambda r: (r, 0))
    out = pl.pallas_call(
        body, name=name, grid=(R // tr,), in_specs=[spec] * 4, out_specs=[spec] * 3,
        out_shape=[jax.ShapeDtypeStruct((R, C), F32)] * 3,
        compiler_params=_params(("parallel",)),
    )(*[t.reshape(R, C) for t in (w, g, m, v)])
    return [t.reshape(shape) for t in out]


_SHARDED = (("wqkv_a", 2), ("wo_a", 1), ("wqkv_b", 2), ("wo_b", 1), ("w_gate", 2), ("w_up", 2), ("w_down", 1))


def _pack_full(layers, axis):
    L = len(layers)
    R, C = layers[0].shape

    def shards(m):
        if axis == 2:
            return m.reshape(R, 4, C // 4).transpose(1, 0, 2)
        return m.reshape(4, R // 4, C)

    halves = [jnp.stack([shards(m) for m in layers[h * (L // 2):(h + 1) * (L // 2)]], axis=1) for h in range(2)]
    return jnp.stack(halves)


def _unpack_full(gathered, axis):
    _, Lh, Rs, Cs = gathered.shape
    t = gathered.reshape(4, 2, Lh, Rs, Cs)
    if axis == 2:
        return t.transpose(1, 2, 3, 0, 4).reshape(2 * Lh, Rs, 4 * Cs)
    return t.transpose(1, 2, 0, 3, 4).reshape(2 * Lh, 4 * Rs, Cs)


def _sum_slabs(own, recv, name, with_bf16=False):
    C = own.shape[-1]
    out = _sum_leading(recv.reshape(recv.shape[0], -1, C), name, own=own.reshape(-1, C), with_bf16=with_bf16)
    if with_bf16:
        return out[0].reshape(own.shape), out[1].reshape(own.shape)
    return out.reshape(own.shape)


def _gather8(x, name):
    return _all_gather8([x], name)[0]


def _rope_tables(positions):
    half = ROT_DIM // 2
    inv_freq = jnp.power(jnp.float32(ROPE_THETA), -jnp.arange(half, dtype=F32) * 2.0 / ROT_DIM)
    ang = positions.astype(F32).reshape(-1, 1) * inv_freq
    cos, sin = jnp.cos(ang), jnp.sin(ang)
    T = ang.shape[0]
    rest = HEAD_DIM - ROT_DIM
    c64 = jnp.concatenate([cos, cos, jnp.ones((T, rest), F32)], axis=1)
    s64 = jnp.concatenate([-sin, sin, jnp.zeros((T, rest), F32)], axis=1)
    return jnp.tile(c64, (1, 2)).reshape(T, 1, LANES), jnp.tile(s64, (1, 2)).reshape(T, 1, LANES)


def _gain_rows(q_gain, k_gain):
    q2 = jnp.tile(q_gain.reshape(1, HEAD_DIM), (GROUP_A, 2))
    k2 = jnp.tile(k_gain.reshape(1, HEAD_DIM), (1, 2))
    return jnp.concatenate([q2, k2, jnp.ones((1, LANES), F32)], axis=0)


def _local_step(x, positions, mod, norm1_g, norm2_g, q_norm_a, k_norm_a, sinks_a,
                wqkv_a, wo_a, wqkv_b, wo_b, wgu, wd, loss_target):
    NB, S, D = x.shape
    T = NB * S
    QA = N_Q_A * HEAD_DIM
    rows_a = wqkv_a.shape[2] // LANES
    tab_c, tab_s = _rope_tables(positions)
    rev, fwd = _cumsum_mats()

    saved = []
    xc = x
    for i in range(DEPTH):
        j = i // 2
        sh1, sc1, g1, sh2, sc2, g2 = [mod[i][:, k * D:(k + 1) * D].reshape(NB, 1, D) for k in range(6)]
        st = dict(x=xc, sc1=sc1, g1=g1, sc2=sc2, g2=g2)
        h = _norm_mod_fwd(xc, norm1_g[i:i + 1], sc1, sh1)
        st["h"] = h.reshape(T, D)
        if i % 2 == 0:
            qkv = _matmul(st["h"], wqkv_a[j], "nn", F32, "qkv_a")
            st["qkv3"] = qkv.reshape(T, rows_a, LANES)
            st["gains"] = _gain_rows(q_norm_a[j], k_norm_a[j])
            qkn = _qk_prep_fwd(st["qkv3"], tab_c, tab_s, st["gains"])
            st["qkn"] = qkn.reshape(NB, S, rows_a * LANES)
            st["o"] = _attn_a_fwd(st["qkn"], sinks_a[j]).reshape(T, QA)
            y = _matmul(st["o"], wo_a[j], "nn", F32, "wo_a")
        else:
            st["qkv"] = _matmul(st["h"], wqkv_b[j], "nn", BF16, "qkv_b").reshape(NB, S, -1)
            st["o"] = _attn_b_fwd(st["qkv"], rev).reshape(T, N_H_B * HEAD_DIM)
            y = _matmul(st["o"], wo_b[j], "nn", F32, "wo_b")
        st["y"] = y.reshape(NB, S, D)
        x1 = _gate_res(xc, st["y"], g1)
        st["x1"] = x1
        h2 = _norm_mod_fwd(x1, norm2_g[i:i + 1], sc2, sh2)
        st["h2"] = h2.reshape(T, D)
        st["gu"] = _matmul(st["h2"], wgu[i], "nn", F32, "gate_up")
        st["act"] = _swiglu_fwd(st["gu"])
        st["m"] = _matmul(st["act"], wd[i], "nn", F32, "down").reshape(NB, S, D)
        xc = _gate_res(x1, st["m"], g2)
        saved.append(st)

    loss, dx = _loss_fwd_bwd(xc, loss_target)

    grads = {name: [None] * n for name, n in
             (("wqkv_a", 2), ("wo_a", 2), ("wqkv_b", 2), ("wo_b", 2), ("wgu", DEPTH), ("wd", DEPTH),
              ("norm1_g", DEPTH), ("norm2_g", DEPTH), ("q_norm_a", 2), ("k_norm_a", 2), ("sinks_a", 2))}
    dmod = [None] * DEPTH
    for i in reversed(range(DEPTH)):
        j = i // 2
        st = saved[i]
        dm, dg2 = _gate_res_bwd(dx, st["m"], st["g2"])
        dm = dm.reshape(T, D)
        dact = _matmul(dm, wd[i], "nt", F32, "d_act")
        grads["wd"][i] = _matmul(st["act"], dm, "tn", F32, "d_wd")
        dgu = _swiglu_bwd(st["gu"], dact)
        dh2 = _matmul(dgu, wgu[i], "nt", F32, "d_h2")
        grads["wgu"][i] = _matmul(st["h2"], dgu, "tn", F32, "d_wgu")
        dx1, dsh2, dsc2, grads["norm2_g"][i] = _norm_mod_bwd(
            st["x1"], norm2_g[i:i + 1], st["sc2"], dh2.reshape(NB, S, D), dx)
        dy, dg1 = _gate_res_bwd(dx1, st["y"], st["g1"])
        dy = dy.reshape(T, D)
        if i % 2 == 0:
            do = _matmul(dy, wo_a[j], "nt", BF16, "d_o_a").reshape(NB, S, QA)
            grads["wo_a"][j] = _matmul(st["o"], dy, "tn", F32, "d_wo_a")
            dq, dk, dv, dsink = _attn_a_bwd(st["qkn"], do, sinks_a[j])
            d3 = jnp.concatenate([dq, dk, dv], axis=-1).reshape(T, rows_a, LANES)
            dqkv, dgain = _qk_prep_bwd(st["qkv3"], d3, tab_c, tab_s, st["gains"])
            dqkv = dqkv.reshape(T, rows_a * LANES)
            dh = _matmul(dqkv, wqkv_a[j], "nt", F32, "d_h_a")
            grads["wqkv_a"][j] = _matmul(st["h"], dqkv, "tn", F32, "d_wqkv_a")
            grads["q_norm_a"][j] = jnp.sum(dgain[:GROUP_A].reshape(2 * GROUP_A, HEAD_DIM), axis=0)
            grads["k_norm_a"][j] = jnp.sum(dgain[GROUP_A].reshape(2, HEAD_DIM), axis=0)
            grads["sinks_a"][j] = jnp.sum(dsink[..., 0], axis=0)
        else:
            do = _matmul(dy, wo_b[j], "nt", BF16, "d_o_b").reshape(NB, S, -1)
            grads["wo_b"][j] = _matmul(st["o"], dy, "tn", F32, "d_wo_b")
            dq, dk, dv = _attn_b_bwd(st["qkv"], do, rev, fwd)
            dqkv = jnp.concatenate([dq, dk, dv], axis=-1).reshape(T, -1).astype(BF16)
            dh = _matmul(dqkv, wqkv_b[j], "nt", F32, "d_h_b")
            grads["wqkv_b"][j] = _matmul(st["h"], dqkv, "tn", F32, "d_wqkv_b")
        dx, dsh1, dsc1, grads["norm1_g"][i] = _norm_mod_bwd(
            st["x"], norm1_g[i:i + 1], st["sc1"], dh.reshape(NB, S, D), dx1)
        dmod[i] = jnp.concatenate([dsh1, dsc1, dg1, dsh2, dsc2, dg2], axis=-1).reshape(NB, 6 * D)

    matrices = ("wqkv_a", "wo_a", "wqkv_b", "wo_b", "wgu", "wd")
    grads = {name: parts if name in matrices else jnp.stack(parts) for name, parts in grads.items()}
    return loss, dx, grads, jnp.stack(dmod)


def _rows_of(flat, cols=PACK_COLS):
    n = flat.shape[0]
    pad = (-n) % (8 * cols)
    if pad:
        flat = jnp.concatenate([flat, jnp.zeros((pad,), flat.dtype)])
    return flat.reshape(-1, cols)


def kernel(x, c, positions, ada_w, ada_b, norm1_g, norm2_g, wqkv_a, q_norm_a, k_norm_a, sinks_a, wo_a, wqkv_b, wo_b, w_gate, w_up, w_down, loss_target, m_ada_w, m_ada_b, m_norm1_g, m_norm2_g, m_wqkv_a, m_q_norm_a, m_k_norm_a, m_sinks_a, m_wo_a, m_wqkv_b, m_wo_b, m_w_gate, m_w_up, m_w_down, v_ada_w, v_ada_b, v_norm1_g, v_norm2_g, v_wqkv_a, v_q_norm_a, v_k_norm_a, v_sinks_a, v_wo_a, v_wqkv_b, v_wo_b, v_w_gate, v_w_up, v_w_down):
    xi, yi, ci = lax.axis_index("x"), lax.axis_index("y"), lax.axis_index("c")
    dev = 4 * xi + 2 * yi + ci
    chip = 2 * xi + yi
    NB, S, D = x.shape
    B_all = N_DEV * NB
    L = ada_w.shape[0]
    n_mod = ada_w.shape[2] // 2

    c_all = _gather8(_rows_of(c.reshape(-1), LANES), "gather_c").reshape(N_DEV, -1)[:, :NB * D].reshape(B_all, D)
    ada_w_half = lax.dynamic_slice_in_dim(ada_w, ci * n_mod, n_mod, axis=2)
    ada_b_half = lax.dynamic_slice_in_dim(ada_b, dev * n_mod, n_mod, axis=1).reshape(L, 1, n_mod)
    mod_part = _ada_fwd(c_all, ada_w_half, ada_b_half)
    n_part = L * B_all * n_mod
    mod_all = _gather8(_rows_of(mod_part.reshape(-1)), "gather_mod").reshape(N_DEV, -1)[:, :n_part]
    mod_all = mod_all.reshape(N_DEV, L, B_all, n_mod).transpose(1, 2, 0, 3).reshape(L, B_all, N_DEV * n_mod)
    mod = lax.dynamic_slice_in_dim(mod_all, dev * NB, NB, axis=1)

    shards = dict(wqkv_a=wqkv_a, wo_a=wo_a, wqkv_b=wqkv_b, wo_b=wo_b, w_gate=w_gate, w_up=w_up, w_down=w_down)
    halves = []
    for name, _ in _SHARDED:
        w = shards[name]
        half = lax.dynamic_index_in_dim(w.reshape((2, w.shape[0] // 2) + w.shape[1:]), ci, 0, keepdims=False)
        halves.append(half.astype(BF16))
    gathered = _all_gather8(halves, "gather_weights", local_axis=1, local_chunks=8)
    full = {name: _unpack_full(t, axis) for (name, axis), t in zip(_SHARDED, gathered)}
    wgu = _interleave(full["w_gate"], full["w_up"])

    loss, grad_x, g, dmod = _local_step(
        x, positions, mod, norm1_g, norm2_g, q_norm_a, k_norm_a, sinks_a,
        full["wqkv_a"], full["wo_a"], full["wqkv_b"], full["wo_b"], wgu, full["w_down"], loss_target)

    gate_up = [_deinterleave(t) for t in g["wgu"]]
    g_full = dict(wqkv_a=g["wqkv_a"], wo_a=g["wo_a"], wqkv_b=g["wqkv_b"], wo_b=g["wo_b"],
                  w_gate=[t[0] for t in gate_up], w_up=[t[1] for t in gate_up], w_down=g["wd"])
    packed = [_pack_full(g_full[name], axis) for name, axis in _SHARDED]
    def own(t, index):
        return lax.dynamic_index_in_dim(t, index, 0, keepdims=False)

    from_cores = _exchange(packed, "c", "rs_cores", chunk_axis=0, chunks=4)
    chip_part = [_sum_slabs(own(p, ci), r, "rs_add_cores", with_bf16=True) for p, r in zip(packed, from_cores)]
    from_chips = _exchange([b for _, b in chip_part], "xy", "rs_chips")
    mine = [_sum_slabs(own(p, chip), r, "rs_add_chips") for (p, _), r in zip(chip_part, from_chips)]
    theirs = _sibling_send(mine, "rs_halves")
    grad = {}
    for (name, _), m, t in zip(_SHARDED, mine, theirs):
        first, second = jnp.where(ci == 0, m, t), jnp.where(ci == 0, t, m)
        grad[name] = jnp.stack([first, second]).reshape(shards[name].shape)

    small_names = ("norm1_g", "norm2_g", "q_norm_a", "k_norm_a", "sinks_a")
    small = [dmod.reshape(-1)] + [g[name].reshape(-1) for name in small_names] + [loss.reshape(-1)]
    small_sizes = [t.shape[0] for t in small]
    small_rows = _rows_of(jnp.concatenate(small))
    small_all = _gather8(small_rows, "gather_small")
    small_sum = _sum_leading(small_all, "sum_small").reshape(-1)
    n_dmod = small_sizes[0]
    dmod_all = small_all.reshape(N_DEV, -1)[:, :n_dmod].reshape(N_DEV, L, NB, 6 * D)
    dmod_all = dmod_all.transpose(1, 0, 2, 3).reshape(L, B_all, 6 * D)
    off = n_dmod
    for name, sz in zip(small_names + ("loss",), small_sizes[1:]):
        grad[name] = small_sum[off:off + sz]
        off += sz
    loss_total = grad.pop("loss").reshape(())
    for name, ref in (("norm1_g", norm1_g), ("norm2_g", norm2_g), ("q_norm_a", q_norm_a),
                      ("k_norm_a", k_norm_a), ("sinks_a", sinks_a)):
        grad[name] = grad[name].reshape(ref.shape)

    n_shard = ada_w.shape[2]
    dmod_shard = lax.dynamic_slice_in_dim(dmod_all, chip * n_shard, n_shard, axis=2)
    grad["ada_w"], gb = _ada_bwd(c_all, dmod_all, dmod_shard)
    grad["ada_b"] = gb.reshape(ada_b.shape)

    weights = dict(ada_w=ada_w, ada_b=ada_b, norm1_g=norm1_g, norm2_g=norm2_g, wqkv_a=wqkv_a, q_norm_a=q_norm_a,
                   k_norm_a=k_norm_a, sinks_a=sinks_a, wo_a=wo_a, wqkv_b=wqkv_b, wo_b=wo_b, w_gate=w_gate,
                   w_up=w_up, w_down=w_down)
    m_in = dict(ada_w=m_ada_w, ada_b=m_ada_b, norm1_g=m_norm1_g, norm2_g=m_norm2_g, wqkv_a=m_wqkv_a,
                q_norm_a=m_q_norm_a, k_norm_a=m_k_norm_a, sinks_a=m_sinks_a, wo_a=m_wo_a, wqkv_b=m_wqkv_b,
                wo_b=m_wo_b, w_gate=m_w_gate, w_up=m_w_up, w_down=m_w_down)
    v_in = dict(ada_w=v_ada_w, ada_b=v_ada_b, norm1_g=v_norm1_g, norm2_g=v_norm2_g, wqkv_a=v_wqkv_a,
                q_norm_a=v_q_norm_a, k_norm_a=v_k_norm_a, sinks_a=v_sinks_a, wo_a=v_wo_a, wqkv_b=v_wqkv_b,
                wo_b=v_wo_b, w_gate=v_w_gate, w_up=v_w_up, w_down=v_w_down)
    names = list(weights)
    delta, new_m, new_v = {}, {}, {}
    for name in names:
        delta[name], new_m[name], new_v[name] = _adamw(weights[name], grad[name], m_in[name], v_in[name],
                                                       "adamw_" + name)
    return (loss_total, grad_x, *[grad[k] for k in names], *[delta[k] for k in names],
            *[new_m[k] for k in names], *[new_v[k] for k in names])
```

```python
import jax
import jax.numpy as jnp
from jax import lax
from jax.experimental import pallas as pl
from jax.experimental.pallas import tpu as pltpu

F32 = jnp.float32
BF16 = jnp.bfloat16

DEPTH = 4
HEAD_DIM = 64
N_Q_A = 16
N_KV_A = 2
GROUP_A = N_Q_A // N_KV_A
N_H_B = 16
BLOCK = 128
ROT_DIM = HEAD_DIM // 4
ROPE_THETA = 500000.0
EPS = 1e-6
ATTN_SCALE = HEAD_DIM ** -0.5
NEG_BIG = -1e30

ADAM_LR = 0.001
ADAM_B1 = 0.9
ADAM_B2 = 0.999
ADAM_EPS = 1e-08
ADAM_WD = 0.01
ADAM_STEP = 10

N_DEV = 8
LANES = 128
PACK_COLS = 1024
VMEM_LIMIT_BYTES = 48 * 1024 * 1024
MESH = pl.DeviceIdType.MESH

_NT = (((1,), (1,)), ((), ()))
_TN = (((0,), (0,)), ((), ()))
_NN = (((1,), (0,)), ((), ()))


def _params(sem=None):
    return pltpu.CompilerParams(vmem_limit_bytes=VMEM_LIMIT_BYTES, dimension_semantics=sem)


def _pick(n, cap, mult):
    best = None
    for t in range(mult, min(n, cap) + 1, mult):
        if n % t == 0:
            best = t
    return n if best is None else best


_ANY = pl.BlockSpec(memory_space=pl.ANY)


def _window(index, axis, q, n, shape):
    rest = [slice(None)] * len(shape)
    size = shape[axis] // n
    rest[axis] = pl.ds(q * size, size)
    return tuple(index) + tuple(rest)


def _all_gather8(xs, name, local_axis=0, local_chunks=1):
    n = len(xs)

    def body(*refs):
        x_refs, out_refs = refs[:n], refs[n:2 * n]
        send_sems, recv_sems, local_sems = refs[2 * n:]
        xi, yi, ci = lax.axis_index("x"), lax.axis_index("y"), lax.axis_index("c")
        me, sibling = (xi, yi, ci), (xi, yi, 1 - ci)
        chips = [(1 - xi, yi), (xi, 1 - yi), (1 - xi, 1 - yi)]

        def slab(w, px, py, pc):
            return out_refs[w].at[4 * px + 2 * py + pc]

        def copy(w, k, block, to, src=None):
            return pltpu.make_async_remote_copy(
                src_ref=slab(w, *block) if src is None else src, dst_ref=slab(w, *block),
                send_sem=send_sems.at[k, w], recv_sem=recv_sems.at[k, w], device_id=to, device_id_type=MESH)

        mine = []
        for w in range(n):
            for q in range(local_chunks):
                part = _window((), local_axis, q, local_chunks, xs[w].shape)
                mine.append(pltpu.make_async_copy(x_refs[w].at[part], slab(w, *me).at[part], local_sems.at[w, q]))
                mine[-1].start()
        first = [copy(w, 0, me, sibling, src=x_refs[w]) for w in range(n)]
        first += [copy(w, 1 + j, me, (*chip, ci), src=x_refs[w]) for j, chip in enumerate(chips) for w in range(n)]
        for cp in first:
            cp.start()
        passed = []
        for j, chip in enumerate(chips):
            for w in range(n):
                copy(w, 1 + j, (*chip, ci), me).wait_recv()
                passed.append(copy(w, 4 + j, (*chip, ci), sibling))
                passed[-1].start()
        for w in range(n):
            copy(w, 0, sibling, me).wait_recv()
        for j, chip in enumerate(chips):
            for w in range(n):
                copy(w, 4 + j, (*chip, 1 - ci), me).wait_recv()
        for cp in first + passed:
            cp.wait_send()
        for cp in mine:
            cp.wait()

    return pl.pallas_call(
        body, name=name,
        out_shape=[jax.ShapeDtypeStruct((N_DEV,) + x.shape, x.dtype) for x in xs],
        in_specs=[_ANY] * n, out_specs=[_ANY] * n,
        scratch_shapes=[pltpu.SemaphoreType.DMA((7, n)), pltpu.SemaphoreType.DMA((7, n)),
                        pltpu.SemaphoreType.DMA((n, local_chunks))],
    )(*xs)


def _exchange(xs, group, name, chunk_axis=0, chunks=1):
    n = len(xs)
    n_peers = 1 if group == "c" else 3

    def body(*refs):
        x_refs, out_refs = refs[:n], refs[n:2 * n]
        send_sems, recv_sems = refs[2 * n:]
        xi, yi, ci = lax.axis_index("x"), lax.axis_index("y"), lax.axis_index("c")
        if group == "c":
            peers = [(1 - ci, (xi, yi, 1 - ci))]
        else:
            peers = [(2 * (1 - xi) + yi, (1 - xi, yi, ci)),
                     (2 * xi + (1 - yi), (xi, 1 - yi, ci)),
                     (2 * (1 - xi) + (1 - yi), (1 - xi, 1 - yi, ci))]
        copies = []
        for k, (p, dev) in enumerate(peers):
            for w in range(n):
                slab_shape = xs[w].shape[1:]
                for q in range(chunks):
                    copies.append(pltpu.make_async_remote_copy(
                        src_ref=x_refs[w].at[_window((p,), chunk_axis, q, chunks, slab_shape)],
                        dst_ref=out_refs[w].at[_window((k,), chunk_axis, q, chunks, slab_shape)],
                        send_sem=send_sems.at[k, w, q], recv_sem=recv_sems.at[k, w, q],
                        device_id=dev, device_id_type=MESH))
                    copies[-1].start()
        for cp in copies:
            cp.wait()

    return pl.pallas_call(
        body, name=name,
        out_shape=[jax.ShapeDtypeStruct((n_peers,) + x.shape[1:], x.dtype) for x in xs],
        in_specs=[_ANY] * n, out_specs=[_ANY] * n,
        scratch_shapes=[pltpu.SemaphoreType.DMA((n_peers, n, chunks)), pltpu.SemaphoreType.DMA((n_peers, n, chunks))],
    )(*xs)


def _sibling_send(xs, name, chunk_axis=1, chunks=4):
    n = len(xs)

    def body(*refs):
        x_refs, out_refs = refs[:n], refs[n:2 * n]
        send_sems, recv_sems = refs[2 * n:]
        xi, yi, ci = lax.axis_index("x"), lax.axis_index("y"), lax.axis_index("c")
        copies = []
        for w in range(n):
            for q in range(chunks):
                part = _window((), chunk_axis, q, chunks, xs[w].shape)
                copies.append(pltpu.make_async_remote_copy(
                    src_ref=x_refs[w].at[part], dst_ref=out_refs[w].at[part],
                    send_sem=send_sems.at[w, q], recv_sem=recv_sems.at[w, q],
                    device_id=(xi, yi, 1 - ci), device_id_type=MESH))
                copies[-1].start()
        for cp in copies:
            cp.wait()

    return pl.pallas_call(
        body, name=name,
        out_shape=[jax.ShapeDtypeStruct(x.shape, x.dtype) for x in xs],
        in_specs=[_ANY] * n, out_specs=[_ANY] * n,
        scratch_shapes=[pltpu.SemaphoreType.DMA((n, chunks)), pltpu.SemaphoreType.DMA((n, chunks))],
    )(*xs)


def _sum_leading(x, name, own=None, with_bf16=False):
    P, R, C = x.shape
    tr = _pick(R, max(16, (1 << 19) // (C * (P + 1))), 16)

    def body(*refs):
        n_in = 1 if own is None else 2
        x_ref = refs[n_in - 1]
        acc = x_ref[0].astype(F32) if own is None else refs[0][...] + x_ref[0].astype(F32)
        for p in range(1, P):
            acc = acc + x_ref[p].astype(F32)
        refs[n_in][...] = acc
        if with_bf16:
            refs[n_in + 1][...] = acc.astype(BF16)

    flat = pl.BlockSpec((tr, C), lambda r: (r, 0))
    slabs = pl.BlockSpec((P, tr, C), lambda r: (0, r, 0))
    out = pl.pallas_call(
        body, name=name, grid=(R // tr,),
        in_specs=[slabs] if own is None else [flat, slabs],
        out_specs=[flat, flat] if with_bf16 else [flat],
        out_shape=[jax.ShapeDtypeStruct((R, C), F32)] + ([jax.ShapeDtypeStruct((R, C), BF16)] if with_bf16 else []),
        compiler_params=_params(("arbitrary",)),
    )(*([x] if own is None else [own, x]))
    return out if with_bf16 else out[0]


def _matmul(a, b, mode, out_dtype, name):
    if mode == "nn":
        (M, K), N = a.shape, b.shape[1]
    elif mode == "nt":
        (M, K), N = a.shape, b.shape[0]
    else:
        (K, M), N = a.shape, b.shape[1]
    tm = _pick(M, 1024 if mode != "tn" else 1536, 128)
    tn = _pick(N, 1536, 128)
    tk = _pick(K, 512, 128)
    nk = K // tk
    dims = {"nn": _NN, "nt": _NT, "tn": _TN}[mode]

    def body(a_ref, b_ref, o_ref, acc_ref):
        k = pl.program_id(2)

        @pl.when(k == 0)
        def _():
            acc_ref[...] = jnp.zeros_like(acc_ref)

        acc_ref[...] += lax.dot_general(a_ref[...].astype(BF16), b_ref[...].astype(BF16), dims,
                                        preferred_element_type=F32)

        @pl.when(k == nk - 1)
        def _():
            o_ref[...] = acc_ref[...].astype(o_ref.dtype)

    if mode == "tn":
        a_spec = pl.BlockSpec((tk, tm), lambda i, j, k: (k, i))
    else:
        a_spec = pl.BlockSpec((tm, tk), lambda i, j, k: (i, k))
    if mode == "nt":
        b_spec = pl.BlockSpec((tn, tk), lambda i, j, k: (j, k))
    else:
        b_spec = pl.BlockSpec((tk, tn), lambda i, j, k: (k, j))
    return pl.pallas_call(
        body, name=name, grid=(M // tm, N // tn, nk),
        in_specs=[a_spec, b_spec],
        out_specs=pl.BlockSpec((tm, tn), lambda i, j, k: (i, j)),
        out_shape=jax.ShapeDtypeStruct((M, N), out_dtype),
        scratch_shapes=[pltpu.VMEM((tm, tn), F32)],
        compiler_params=_params(("parallel", "parallel", "arbitrary")),
    )(a, b)


def _row_tile(S):
    return _pick(S, 512, 8)


def _norm_mod_fwd(x, gain, sc, sh):
    NB, S, D = x.shape
    tr = _row_tile(S)

    def body(x_ref, g_ref, sc_ref, sh_ref, h_ref):
        xv = x_ref[...]
        ms = jnp.mean(xv * xv, axis=-1, keepdims=True)
        n = xv * lax.rsqrt(ms + EPS) * g_ref[...]
        h_ref[...] = (n * (1.0 + sc_ref[...]) + sh_ref[...]).astype(BF16)

    tok = pl.BlockSpec((None, tr, D), lambda b, r: (b, r, 0))
    per_ex = pl.BlockSpec((None, 1, D), lambda b, r: (b, 0, 0))
    return pl.pallas_call(
        body, name="norm_mod_fwd", grid=(NB, S // tr),
        in_specs=[tok, pl.BlockSpec((1, D), lambda b, r: (0, 0)), per_ex, per_ex],
        out_specs=tok, out_shape=jax.ShapeDtypeStruct((NB, S, D), BF16),
        compiler_params=_params(("parallel", "parallel")),
    )(x, gain, sc, sh)


def _norm_mod_bwd(x, gain, sc, dh, dres):
    NB, S, D = x.shape
    tr = _row_tile(S)

    def body(x_ref, g_ref, sc_ref, dh_ref, dres_ref, dx_ref, dsh_ref, dsc_ref, dg_ref):
        b, r = pl.program_id(0), pl.program_id(1)

        @pl.when(r == 0)
        def _():
            dsh_ref[...] = jnp.zeros_like(dsh_ref)
            dsc_ref[...] = jnp.zeros_like(dsc_ref)

        @pl.when((r == 0) & (b == 0))
        def _():
            dg_ref[...] = jnp.zeros_like(dg_ref)

        xv = x_ref[...]
        rstd = lax.rsqrt(jnp.mean(xv * xv, axis=-1, keepdims=True) + EPS)
        xh = xv * rstd
        g = g_ref[...]
        dh = dh_ref[...]
        dsh_ref[...] += jnp.sum(dh, axis=0, keepdims=True)
        dsc_ref[...] += jnp.sum(dh * (xh * g), axis=0, keepdims=True)
        dn = dh * (1.0 + sc_ref[...])
        dg_ref[...] += jnp.sum(dn * xh, axis=0, keepdims=True)
        dxh = dn * g
        proj = jnp.mean(dxh * xh, axis=-1, keepdims=True)
        dx_ref[...] = rstd * (dxh - xh * proj) + dres_ref[...]

    tok = pl.BlockSpec((None, tr, D), lambda b, r: (b, r, 0))
    per_ex = pl.BlockSpec((None, 1, D), lambda b, r: (b, 0, 0))
    row = pl.BlockSpec((1, D), lambda b, r: (0, 0))
    return pl.pallas_call(
        body, name="norm_mod_bwd", grid=(NB, S // tr),
        in_specs=[tok, row, per_ex, tok, tok],
        out_specs=[tok, per_ex, per_ex, row],
        out_shape=[jax.ShapeDtypeStruct((NB, S, D), F32), jax.ShapeDtypeStruct((NB, 1, D), F32),
                   jax.ShapeDtypeStruct((NB, 1, D), F32), jax.ShapeDtypeStruct((1, D), F32)],
        compiler_params=_params(("arbitrary", "arbitrary")),
    )(x, gain, sc, dh, dres)


def _gate_res(x, y, g):
    NB, S, D = x.shape
    tr = _row_tile(S)

    def body(x_ref, y_ref, g_ref, o_ref):
        o_ref[...] = x_ref[...] + g_ref[...] * y_ref[...]

    tok = pl.BlockSpec((None, tr, D), lambda b, r: (b, r, 0))
    per_ex = pl.BlockSpec((None, 1, D), lambda b, r: (b, 0, 0))
    return pl.pallas_call(
        body, name="gate_res", grid=(NB, S // tr), in_specs=[tok, tok, per_ex], out_specs=tok,
        out_shape=jax.ShapeDtypeStruct((NB, S, D), F32),
        compiler_params=_params(("parallel", "parallel")),
    )(x, y, g)


def _gate_res_bwd(dxo, y, g):
    NB, S, D = dxo.shape
    tr = _row_tile(S)

    def body(d_ref, y_ref, g_ref, dy_ref, dg_ref):
        @pl.when(pl.program_id(1) == 0)
        def _():
            dg_ref[...] = jnp.zeros_like(dg_ref)

        d = d_ref[...]
        dy_ref[...] = (d * g_ref[...]).astype(BF16)
        dg_ref[...] += jnp.sum(d * y_ref[...], axis=0, keepdims=True)

    tok = pl.BlockSpec((None, tr, D), lambda b, r: (b, r, 0))
    per_ex = pl.BlockSpec((None, 1, D), lambda b, r: (b, 0, 0))
    return pl.pallas_call(
        body, name="gate_res_bwd", grid=(NB, S // tr), in_specs=[tok, tok, per_ex], out_specs=[tok, per_ex],
        out_shape=[jax.ShapeDtypeStruct((NB, S, D), BF16), jax.ShapeDtypeStruct((NB, 1, D), F32)],
        compiler_params=_params(("arbitrary", "arbitrary")),
    )(dxo, y, g)


def _sigmoid(v):
    return 1.0 / (1.0 + jnp.exp(-v))


def _ff_tile(F):
    return _pick(F, 1536, 128)


def _interleave(gate, up):
    F = gate.shape[-1]
    tf = _ff_tile(F)
    parts = []
    for j in range(F // tf):
        parts += [gate[..., j * tf:(j + 1) * tf], up[..., j * tf:(j + 1) * tf]]
    return jnp.concatenate(parts, axis=-1)


def _deinterleave(gu):
    F = gu.shape[-1] // 2
    tf = _ff_tile(F)
    gate = [gu[..., 2 * j * tf:(2 * j + 1) * tf] for j in range(F // tf)]
    up = [gu[..., (2 * j + 1) * tf:(2 * j + 2) * tf] for j in range(F // tf)]
    return jnp.concatenate(gate, axis=-1), jnp.concatenate(up, axis=-1)


def _swiglu_fwd(gu):
    T, F2 = gu.shape
    F = F2 // 2
    tf = _ff_tile(F)
    tr = _pick(T, 256, 8)

    def body(gu_ref, o_ref):
        g = gu_ref[:, :tf]
        o_ref[...] = (g * _sigmoid(g) * gu_ref[:, tf:]).astype(BF16)

    return pl.pallas_call(
        body, name="swiglu_fwd", grid=(T // tr, F // tf),
        in_specs=[pl.BlockSpec((tr, 2 * tf), lambda i, j: (i, j))],
        out_specs=pl.BlockSpec((tr, tf), lambda i, j: (i, j)),
        out_shape=jax.ShapeDtypeStruct((T, F), BF16),
        compiler_params=_params(("parallel", "parallel")),
    )(gu)


def _swiglu_bwd(gu, dact):
    T, F2 = gu.shape
    F = F2 // 2
    tf = _ff_tile(F)
    tr = _pick(T, 256, 8)

    def body(gu_ref, d_ref, o_ref):
        g, u, d = gu_ref[:, :tf], gu_ref[:, tf:], d_ref[...]
        s = _sigmoid(g)
        o_ref[:, :tf] = (d * u * (s * (1.0 + g * (1.0 - s)))).astype(BF16)
        o_ref[:, tf:] = (d * (g * s)).astype(BF16)

    return pl.pallas_call(
        body, name="swiglu_bwd", grid=(T // tr, F // tf),
        in_specs=[pl.BlockSpec((tr, 2 * tf), lambda i, j: (i, j)), pl.BlockSpec((tr, tf), lambda i, j: (i, j))],
        out_specs=pl.BlockSpec((tr, 2 * tf), lambda i, j: (i, j)),
        out_shape=jax.ShapeDtypeStruct((T, F2), BF16),
        compiler_params=_params(("parallel", "parallel")),
    )(gu, dact)


def _loss_fwd_bwd(y, target):
    NB, S, D = y.shape
    tr = _row_tile(S)

    def body(y_ref, t_ref, l_ref, d_ref):
        @pl.when((pl.program_id(0) == 0) & (pl.program_id(1) == 0))
        def _():
            l_ref[...] = jnp.zeros_like(l_ref)

        e = y_ref[...] - t_ref[...]
        d_ref[...] = e / D
        l_ref[...] += 0.5 * jnp.sum(jnp.mean(e * e, axis=-1, keepdims=True), axis=0, keepdims=True)

    tok = pl.BlockSpec((None, tr, D), lambda b, r: (b, r, 0))
    return pl.pallas_call(
        body, name="loss", grid=(NB, S // tr), in_specs=[tok, tok],
        out_specs=[pl.BlockSpec((1, 1), lambda b, r: (0, 0)), tok],
        out_shape=[jax.ShapeDtypeStruct((1, 1), F32), jax.ShapeDtypeStruct((NB, S, D), F32)],
        compiler_params=_params(("arbitrary", "arbitrary")),
    )(y, target)


def _half_sums(v, lo):
    sa = jnp.sum(jnp.where(lo, v, 0.0), axis=-1, keepdims=True)
    sb = jnp.sum(jnp.where(lo, 0.0, v), axis=-1, keepdims=True)
    return jnp.where(lo, sa, sb)


def _rope_swap(v, lane64):
    up = pltpu.roll(v, LANES - ROT_DIM // 2, 1)
    down = pltpu.roll(v, ROT_DIM // 2, 1)
    return jnp.where(lane64 < ROT_DIM // 2, up, jnp.where(lane64 < ROT_DIM, down, 0.0))


def _qk_prep_fwd(qkv, tab_c, tab_s, gains):
    T, W = qkv.shape
    R = W // LANES
    tt = _pick(T, 256, 8)

    def body(x_ref, c_ref, s_ref, g_ref, o_ref):
        lane = lax.broadcasted_iota(jnp.int32, (tt, LANES), 1)
        lo = lane < HEAD_DIM
        lane64 = lane & (HEAD_DIM - 1)
        c, s = c_ref[...], s_ref[...]
        for j in range(R - 1):
            cols = slice(j * LANES, (j + 1) * LANES)
            xv = x_ref[:, cols]
            rstd = lax.rsqrt(_half_sums(xv * xv, lo) / HEAD_DIM + EPS)
            yn = xv * rstd * g_ref[j:j + 1, :]
            o_ref[:, cols] = (yn * c + _rope_swap(yn, lane64) * s).astype(BF16)
        o_ref[:, (R - 1) * LANES:] = x_ref[:, (R - 1) * LANES:].astype(BF16)

    tok = pl.BlockSpec((tt, W), lambda t: (t, 0))
    tab = pl.BlockSpec((tt, LANES), lambda t: (t, 0))
    return pl.pallas_call(
        body, name="qk_prep_fwd", grid=(T // tt,),
        in_specs=[tok, tab, tab, pl.BlockSpec((R, LANES), lambda t: (0, 0))],
        out_specs=tok, out_shape=jax.ShapeDtypeStruct((T, W), BF16),
        compiler_params=_params(("parallel",)),
    )(qkv, tab_c, tab_s, gains)


def _qk_prep_bwd(qkv, dq, dk, dv, tab_c, tab_s, gains):
    T, W = qkv.shape
    R = W // LANES
    QW = dq.shape[1]
    tt = _pick(T, 256, 8)

    def body(x_ref, dq_ref, dk_ref, dv_ref, c_ref, s_ref, g_ref, o_ref, dg_ref):
        @pl.when(pl.program_id(0) == 0)
        def _():
            dg_ref[...] = jnp.zeros_like(dg_ref)

        lane = lax.broadcasted_iota(jnp.int32, (tt, LANES), 1)
        lo = lane < HEAD_DIM
        lane64 = lane & (HEAD_DIM - 1)
        c, s = c_ref[...], s_ref[...]
        for j in range(R - 1):
            cols = slice(j * LANES, (j + 1) * LANES)
            xv = x_ref[:, cols]
            d = dq_ref[:, cols] if j < R - 2 else dk_ref[...]
            rstd = lax.rsqrt(_half_sums(xv * xv, lo) / HEAD_DIM + EPS)
            xh = xv * rstd
            dyn = d * c + _rope_swap(d * s, lane64)
            dg_ref[j:j + 1, :] += jnp.sum(dyn * xh, axis=0, keepdims=True)
            dxh = dyn * g_ref[j:j + 1, :]
            proj = _half_sums(dxh * xh, lo) / HEAD_DIM
            o_ref[:, cols] = (rstd * (dxh - xh * proj)).astype(BF16)
        o_ref[:, (R - 1) * LANES:] = dv_ref[...].astype(BF16)

    tok = pl.BlockSpec((tt, W), lambda t: (t, 0))
    tab = pl.BlockSpec((tt, LANES), lambda t: (t, 0))
    gsp = pl.BlockSpec((R, LANES), lambda t: (0, 0))
    return pl.pallas_call(
        body, name="qk_prep_bwd", grid=(T // tt,),
        in_specs=[tok, pl.BlockSpec((tt, QW), lambda t: (t, 0)), tab, tab, tab, tab, gsp], out_specs=[tok, gsp],
        out_shape=[jax.ShapeDtypeStruct((T, W), BF16), jax.ShapeDtypeStruct((R, LANES), F32)],
        compiler_params=_params(("arbitrary",)),
    )(qkv, dq, dk, dv, tab_c, tab_s, gains)


def _band_mask(i):
    r = lax.broadcasted_iota(jnp.int32, (BLOCK, 2 * BLOCK), 0)
    c = lax.broadcasted_iota(jnp.int32, (BLOCK, 2 * BLOCK), 1)
    rel = r + BLOCK - c
    return (rel >= 0) & (rel < BLOCK) & ((c >= BLOCK) | (i > 0))


def _swa_probs(qg, k2, valid, sink):
    s = lax.dot_general(qg, k2, _NT, preferred_element_type=F32) * ATTN_SCALE
    s = jnp.where(valid, s, NEG_BIG)
    m = jnp.maximum(jnp.max(s, axis=1, keepdims=True), sink)
    p = jnp.exp(s - m)
    ps = jnp.exp(sink - m)
    denom = jnp.sum(p, axis=1, keepdims=True) + ps
    return p / denom, ps / denom


Q_WIDTH_A = N_Q_A * HEAD_DIM
N_PAIR_A = Q_WIDTH_A // LANES


def _swa_specs():
    qs = pl.BlockSpec((None, BLOCK, Q_WIDTH_A), lambda b, i: (b, i, 0))

    def kv(col, back):
        return pl.BlockSpec((None, BLOCK, LANES), lambda b, i: (b, jnp.maximum(i - back, 0), col))

    return qs, kv(N_PAIR_A, 1), kv(N_PAIR_A, 0), kv(N_PAIR_A + 1, 1), kv(N_PAIR_A + 1, 0)


def _dup_heads(t):
    lo = lax.broadcasted_iota(jnp.int32, t.shape, 1) < HEAD_DIM
    sw = pltpu.roll(t.astype(F32), HEAD_DIM, 1).astype(BF16)
    return jnp.where(lo, t, sw), jnp.where(lo, sw, t)


def _kv_tiles(kp_ref, kc_ref, vp_ref, vc_ref):
    kd = _dup_heads(jnp.concatenate([kp_ref[...], kc_ref[...]], axis=0))
    vd = _dup_heads(jnp.concatenate([vp_ref[...], vc_ref[...]], axis=0))
    return kd, vd


def _attn_a_fwd(qkn, sinks):
    NB, S, _ = qkn.shape
    qs, kp, kc, vp, vc = _swa_specs()

    def body(q_ref, kp_ref, kc_ref, vp_ref, vc_ref, sink_ref, o_ref):
        i = pl.program_id(1)
        kd, vd = _kv_tiles(kp_ref, kc_ref, vp_ref, vc_ref)
        valid = _band_mask(i)
        lo = lax.broadcasted_iota(jnp.int32, (BLOCK, LANES), 1) < HEAD_DIM
        for pair in range(N_PAIR_A):
            cols = slice(pair * LANES, (pair + 1) * LANES)
            qh = _head_halves(q_ref[:, cols], lo)
            kvh = 2 * pair // GROUP_A
            outs = []
            for hh in range(2):
                pn, _ = _swa_probs(qh[hh], kd[kvh], valid, sink_ref[2 * pair + hh])
                outs.append(jnp.dot(pn.astype(BF16), vd[kvh], preferred_element_type=F32))
            o_ref[:, cols] = jnp.where(lo, outs[0], outs[1]).astype(BF16)

    return pl.pallas_call(
        body, name="attn_a_fwd", grid=(NB, S // BLOCK),
        in_specs=[qs, kp, kc, vp, vc, pl.BlockSpec(memory_space=pltpu.SMEM)],
        out_specs=qs, out_shape=jax.ShapeDtypeStruct((NB, S, Q_WIDTH_A), BF16),
        compiler_params=_params(("parallel", "arbitrary")),
    )(qkn, qkn, qkn, qkn, qkn, sinks)


def _attn_a_bwd(qkn, do, sinks):
    NB, S, _ = qkn.shape
    qs, kp, kc, vp, vc = _swa_specs()
    full = pl.BlockSpec((None, S, LANES), lambda b, i: (b, 0, 0))
    sink_out = pl.BlockSpec((None, N_Q_A, LANES), lambda b, i: (b, 0, 0))

    def body(q_ref, do_ref, kp_ref, kc_ref, vp_ref, vc_ref, sink_ref, dq_ref, dk_ref, dv_ref, ds_ref, dk_s, dv_s):
        i = pl.program_id(1)

        @pl.when(i == 0)
        def _():
            dk_ref[...] = jnp.zeros_like(dk_ref)
            dv_ref[...] = jnp.zeros_like(dv_ref)
            ds_ref[...] = jnp.zeros_like(ds_ref)

        dk_s[...] = jnp.zeros_like(dk_s)
        dv_s[...] = jnp.zeros_like(dv_s)
        kd, vd = _kv_tiles(kp_ref, kc_ref, vp_ref, vc_ref)
        valid = _band_mask(i)
        lo = lax.broadcasted_iota(jnp.int32, (BLOCK, LANES), 1) < HEAD_DIM
        for pair in range(N_PAIR_A):
            cols = slice(pair * LANES, (pair + 1) * LANES)
            q2, do2 = q_ref[:, cols], do_ref[:, cols]
            qh, doh = _head_halves(q2, lo), _head_halves(do2, lo)
            kvh = 2 * pair // GROUP_A
            dqs = []
            for hh in range(2):
                h = 2 * pair + hh
                pn, psink = _swa_probs(qh[hh], kd[kvh], valid, sink_ref[h])
                dp = lax.dot_general(doh[hh], vd[kvh], _NT, preferred_element_type=F32)
                delta = jnp.sum(pn * dp, axis=1, keepdims=True)
                dsb = (pn * (dp - delta) * ATTN_SCALE).astype(BF16)
                dqs.append(jnp.dot(dsb, kd[kvh], preferred_element_type=F32))
                dk_s[2 * kvh + hh] += lax.dot_general(dsb, q2, _TN, preferred_element_type=F32)
                dv_s[2 * kvh + hh] += lax.dot_general(pn.astype(BF16), do2, _TN, preferred_element_type=F32)
                dsink = -jnp.sum(psink * delta, axis=0, keepdims=True)
                ds_ref[h:h + 1, :] += jnp.broadcast_to(dsink, (1, LANES))
            dq_ref[:, cols] = jnp.where(lo, dqs[0], dqs[1])

        lo2 = lax.broadcasted_iota(jnp.int32, (2 * BLOCK, LANES), 1) < HEAD_DIM

        def fold(acc):
            head0 = acc[0] + pltpu.roll(acc[1], HEAD_DIM, 1)
            head1 = pltpu.roll(acc[2], HEAD_DIM, 1) + acc[3]
            return jnp.where(lo2, head0, head1)

        dk2, dv2 = fold(dk_s), fold(dv_s)

        @pl.when(i > 0)
        def _():
            start = pl.multiple_of((i - 1) * BLOCK, BLOCK)
            dk_ref[pl.ds(start, 2 * BLOCK), :] += dk2
            dv_ref[pl.ds(start, 2 * BLOCK), :] += dv2

        @pl.when(i == 0)
        def _():
            dk_ref[0:BLOCK, :] += dk2[BLOCK:, :]
            dv_ref[0:BLOCK, :] += dv2[BLOCK:, :]

    slots = pltpu.VMEM((2 * N_KV_A, 2 * BLOCK, LANES), F32)
    return pl.pallas_call(
        body, name="attn_a_bwd", grid=(NB, S // BLOCK),
        in_specs=[qs, qs, kp, kc, vp, vc, pl.BlockSpec(memory_space=pltpu.SMEM)],
        out_specs=[qs, full, full, sink_out],
        out_shape=[jax.ShapeDtypeStruct((NB, S, Q_WIDTH_A), F32), jax.ShapeDtypeStruct((NB, S, LANES), F32),
                   jax.ShapeDtypeStruct((NB, S, LANES), F32), jax.ShapeDtypeStruct((NB, N_Q_A, LANES), F32)],
        scratch_shapes=[slots, slots],
        compiler_params=_params(("parallel", "arbitrary")),
    )(qkn, do, qkn, qkn, qkn, qkn, sinks)


def _cumsum_mats():
    src = lax.broadcasted_iota(jnp.int32, (2 * BLOCK, 2 * BLOCK), 0) % BLOCK
    dst = lax.broadcasted_iota(jnp.int32, (2 * BLOCK, 2 * BLOCK), 1)
    ones = dst >= BLOCK
    rev = ((src > dst) | ones).astype(BF16)
    fwd = ((src < dst) | ones).astype(BF16)
    return rev, fwd


def _cumsum_mxu(v, mat):
    hi = v.astype(BF16)
    lo = (v - hi.astype(F32)).astype(BF16)
    r = jnp.dot(jnp.concatenate([hi, lo], axis=1), mat, preferred_element_type=F32)
    return r[:, :BLOCK], r[:, BLOCK:]


def _sb_logs(qs, kj):
    z = lax.dot_general(qs, kj, _NT, preferred_element_type=F32)
    sp = jnp.log(1.0 + jnp.exp(-jnp.abs(z)))
    return jnp.minimum(z, 0.0) - sp, -(jnp.maximum(z, 0.0) + sp)


def _strict_mask():
    r = lax.broadcasted_iota(jnp.int32, (BLOCK, BLOCK), 0)
    c = lax.broadcasted_iota(jnp.int32, (BLOCK, BLOCK), 1)
    return c < r


def _tile(ref, j):
    return ref[pl.ds(pl.multiple_of(j * BLOCK, BLOCK), BLOCK), :]


SWEEP_EXIT = -88.0


def _head_halves(t, lo):
    zero = jnp.zeros_like(t)
    return jnp.where(lo, t, zero), jnp.where(lo, zero, t)


def _sb_specs(S, HD, width):
    n = HD // width
    blk = pl.BlockSpec((None, BLOCK, width), lambda b, p, i: (b, i, p))
    k_full = pl.BlockSpec((None, S, width), lambda b, p, i: (b, 0, n + p))
    v_full = pl.BlockSpec((None, S, width), lambda b, p, i: (b, 0, 2 * n + p))
    mat = pl.BlockSpec((2 * BLOCK, 2 * BLOCK), lambda b, p, i: (0, 0))
    return blk, k_full, v_full, mat


SB_FWD_PAIRS = 2


def _attn_b_fwd(qkv, rev):
    NB, S, W = qkv.shape
    HD = W // 3
    width = SB_FWD_PAIRS * LANES
    n_heads = 2 * SB_FWD_PAIRS
    blk, k_full, v_full, mat = _sb_specs(S, HD, width)

    def body(q_ref, k_ref, v_ref, rev_ref, o_ref):
        i = pl.program_id(2)
        rv = rev_ref[...]
        mask = _strict_mask()
        lo = lax.broadcasted_iota(jnp.int32, (BLOCK, LANES), 1) < HEAD_DIM
        q_all = q_ref[...]
        qh = []
        for p in range(SB_FWD_PAIRS):
            qh.extend(_head_halves(q_all[:, p * LANES:(p + 1) * LANES] * ATTN_SCALE, lo))

        def pair_tiles(ref, j):
            t = _tile(ref, j)
            return [t[:, p * LANES:(p + 1) * LANES] for p in range(SB_FWD_PAIRS)]

        def merge(outs):
            return [jnp.where(lo, outs[2 * p], outs[2 * p + 1]) for p in range(SB_FWD_PAIRS)]

        ks, vs = pair_tiles(k_ref, i), pair_tiles(v_ref, i)
        carries, outs = [], []
        for h in range(n_heads):
            lb, lm = _sb_logs(qh[h], ks[h // 2])
            after, rs = _cumsum_mxu(jnp.where(mask, lm, 0.0), rv)
            a = jnp.where(mask, jnp.exp(lb + after), 0.0)
            outs.append(jnp.dot(a.astype(BF16), vs[h // 2], preferred_element_type=F32))
            carries.append(rs)
        accs = merge(outs)

        def live(cs):
            top = cs[0]
            for c in cs[1:]:
                top = jnp.maximum(top, c)
            return jnp.max(top) > SWEEP_EXIT

        def cond(st):
            return (st[0] < i) & st[1]

        def step(st):
            jj, _, cs, accs = st
            j = i - 1 - jj
            ks, vs = pair_tiles(k_ref, j), pair_tiles(v_ref, j)
            new_c, outs = [], []
            for h in range(n_heads):
                lb, lm = _sb_logs(qh[h], ks[h // 2])
                after, rs = _cumsum_mxu(lm, rv)
                a = jnp.exp(lb + after + cs[h])
                outs.append(jnp.dot(a.astype(BF16), vs[h // 2], preferred_element_type=F32))
                new_c.append(cs[h] + rs)
            accs = [acc + o for acc, o in zip(accs, merge(outs))]
            return jj + 1, live(new_c), new_c, accs

        st = lax.while_loop(cond, step, (jnp.int32(0), live(carries), carries, accs))
        for p in range(SB_FWD_PAIRS):
            o_ref[:, p * LANES:(p + 1) * LANES] = st[3][p].astype(BF16)

    return pl.pallas_call(
        body, name="attn_b_fwd", grid=(NB, HD // width, S // BLOCK),
        in_specs=[blk, k_full, v_full, mat], out_specs=blk,
        out_shape=jax.ShapeDtypeStruct((NB, S, HD), BF16),
        compiler_params=_params(("parallel", "parallel", "arbitrary")),
    )(qkv, qkv, qkv, rev)


def _attn_b_bwd(qkv, do, rev, fwd):
    NB, S, W = qkv.shape
    HD = W // 3
    n_pair = HD // LANES
    nj = S // BLOCK
    blk, k_full, v_full, mat = _sb_specs(S, HD, LANES)
    acc_full = pl.BlockSpec((None, S, LANES), lambda b, p, i: (b, 0, p))

    def body(q_ref, do_ref, k_ref, v_ref, rev_ref, fwd_ref, dq_ref, dk_ref, dv_ref, sig_s, a_s, e_s):
        i = pl.program_id(2)

        @pl.when(i == 0)
        def _():
            dk_ref[...] = jnp.zeros_like(dk_ref)
            dv_ref[...] = jnp.zeros_like(dv_ref)

        q2, do2 = q_ref[...], do_ref[...]
        rv, fw = rev_ref[...], fwd_ref[...]
        mask = _strict_mask()
        lo = lax.broadcasted_iota(jnp.int32, (BLOCK, LANES), 1) < HEAD_DIM
        qh = _head_halves(q2 * ATTN_SCALE, lo)
        doh = _head_halves(do2, lo)

        def stash(h, j, vj, lb, a):
            da = lax.dot_general(doh[h], vj, _NT, preferred_element_type=F32)
            sig_s[h, j] = jnp.exp(lb)
            a_s[h, j] = a
            e_s[h, j] = da * a

        kj, vj = _tile(k_ref, i), _tile(v_ref, i)
        carries = []
        for h in range(2):
            lb, lm = _sb_logs(qh[h], kj)
            after, rs = _cumsum_mxu(jnp.where(mask, lm, 0.0), rv)
            stash(h, i, vj, lb, jnp.where(mask, jnp.exp(lb + after), 0.0))
            carries.append(rs)

        def live(c0, c1):
            return jnp.max(jnp.maximum(c0, c1)) > SWEEP_EXIT

        def cond(st):
            return (st[0] < i) & st[1]

        def sweep1(st):
            jj, _, c0, c1 = st
            j = i - 1 - jj
            kj, vj = _tile(k_ref, j), _tile(v_ref, j)
            new_c = []
            for h, carry in enumerate((c0, c1)):
                lb, lm = _sb_logs(qh[h], kj)
                after, rs = _cumsum_mxu(lm, rv)
                stash(h, j, vj, lb, jnp.exp(lb + after + carry))
                new_c.append(carry + rs)
            return jj + 1, live(*new_c), new_c[0], new_c[1]

        visited = lax.while_loop(cond, sweep1, (jnp.int32(0), live(*carries), carries[0], carries[1]))[0]

        def grads(j, st, diagonal):
            p0, p1, dq = st
            kj = _tile(k_ref, j)
            new_p, dqs, dks, dvs = [], [], [], []
            for h, prefix in enumerate((p0, p1)):
                e, a, sg = e_s[h, j], a_s[h, j], sig_s[h, j]
                e_before, rs = _cumsum_mxu(e, fw)
                dz = (e * (1.0 - sg) - (e_before + prefix) * sg) * ATTN_SCALE
                if diagonal:
                    dz = jnp.where(mask, dz, 0.0)
                dzb = dz.astype(BF16)
                dqs.append(jnp.dot(dzb, kj, preferred_element_type=F32))
                dks.append(lax.dot_general(dzb, q2, _TN, preferred_element_type=F32))
                dvs.append(lax.dot_general(a.astype(BF16), do2, _TN, preferred_element_type=F32))
                new_p.append(prefix + rs)
            rows = pl.ds(pl.multiple_of(j * BLOCK, BLOCK), BLOCK)
            dk_ref[rows, :] += jnp.where(lo, dks[0], dks[1])
            dv_ref[rows, :] += jnp.where(lo, dvs[0], dvs[1])
            return new_p[0], new_p[1], dq + jnp.where(lo, dqs[0], dqs[1])

        zeros = jnp.zeros((BLOCK, BLOCK), F32)
        st = lax.fori_loop(i - visited, i, lambda j, st: grads(j, st, False), (zeros, zeros, zeros))
        dq_ref[...] = grads(i, st, True)[2]

    tile_stash = pltpu.VMEM((2, nj, BLOCK, BLOCK), F32)
    return pl.pallas_call(
        body, name="attn_b_bwd", grid=(NB, n_pair, nj),
        in_specs=[blk, blk, k_full, v_full, mat, mat], out_specs=[blk, acc_full, acc_full],
        out_shape=[jax.ShapeDtypeStruct((NB, S, HD), F32)] * 3,
        scratch_shapes=[tile_stash, tile_stash, tile_stash],
        compiler_params=_params(("parallel", "parallel", "arbitrary")),
    )(qkv, do, qkv, qkv, rev, fwd)


def _ada_fwd(c_all, w, b):
    L, D, N = w.shape
    B = c_all.shape[0]

    def body(c_ref, w_ref, b_ref, o_ref):
        cv = c_ref[...]
        cond = (cv * _sigmoid(cv)).astype(BF16)
        o_ref[...] = jnp.dot(cond, w_ref[...].astype(BF16), preferred_element_type=F32) + b_ref[...]

    return pl.pallas_call(
        body, name="ada_fwd", grid=(L,),
        in_specs=[pl.BlockSpec((B, D), lambda l: (0, 0)), pl.BlockSpec((None, D, N), lambda l: (l, 0, 0)),
                  pl.BlockSpec((None, 1, N), lambda l: (l, 0, 0))],
        out_specs=pl.BlockSpec((None, B, N), lambda l: (l, 0, 0)),
        out_shape=jax.ShapeDtypeStruct((L, B, N), F32),
        compiler_params=_params(("parallel",)),
    )(c_all, w, b)


def _ada_bwd(c_all, dmod_all, dmod_shard):
    L, B, N = dmod_shard.shape
    D = c_all.shape[1]
    N_all = dmod_all.shape[2]

    def body(c_ref, da_ref, ds_ref, gw_ref, gb_ref):
        cv = c_ref[...]
        cond = (cv * _sigmoid(cv)).astype(BF16)
        gw_ref[...] = lax.dot_general(cond, ds_ref[...].astype(BF16), _TN, preferred_element_type=F32)
        gb_ref[...] = jnp.sum(da_ref[...], axis=0, keepdims=True)

    return pl.pallas_call(
        body, name="ada_bwd", grid=(L,),
        in_specs=[pl.BlockSpec((B, D), lambda l: (0, 0)), pl.BlockSpec((None, B, N_all), lambda l: (l, 0, 0)),
                  pl.BlockSpec((None, B, N), lambda l: (l, 0, 0))],
        out_specs=[pl.BlockSpec((None, D, N), lambda l: (l, 0, 0)), pl.BlockSpec((None, 1, N_all), lambda l: (l, 0, 0))],
        out_shape=[jax.ShapeDtypeStruct((L, D, N), F32), jax.ShapeDtypeStruct((L, 1, N_all), F32)],
        compiler_params=_params(("parallel",)),
    )(c_all, dmod_all, dmod_shard)


def _adamw(w, g, m, v, name):
    shape = w.shape
    C = shape[-1]
    R = w.size // C
    tr = _pick(R, max(8, (1 << 18) // C), 8)
    c1 = 1.0 - ADAM_B1 ** ADAM_STEP
    c2 = 1.0 - ADAM_B2 ** ADAM_STEP

    def body(w_ref, g_ref, m_ref, v_ref, d_ref, nm_ref, nv_ref):
        gv = g_ref[...]
        nm = ADAM_B1 * m_ref[...] + (1.0 - ADAM_B1) * gv
        nv = ADAM_B2 * v_ref[...] + (1.0 - ADAM_B2) * (gv * gv)
        d_ref[...] = -ADAM_LR * ((nm / c1) / (jnp.sqrt(nv / c2) + ADAM_EPS) + ADAM_WD * w_ref[...])
        nm_ref[...] = nm
        nv_ref[...] = nv

    spec = pl.BlockSpec((tr, C), lambda r: (r, 0))
    out = pl.pallas_call(
        body, name=name, grid=(R // tr,), in_specs=[spec] * 4, out_specs=[spec] * 3,
        out_shape=[jax.ShapeDtypeStruct((R, C), F32)] * 3,
        compiler_params=_params(("parallel",)),
    )(*[t.reshape(R, C) for t in (w, g, m, v)])
    return [t.reshape(shape) for t in out]


_SHARDED = (("wqkv_a", 2), ("wo_a", 1), ("wqkv_b", 2), ("wo_b", 1), ("w_gate", 2), ("w_up", 2), ("w_down", 1))


def _pack_full(layers, axis):
    L = len(layers)
    R, C = layers[0].shape

    def shards(m):
        if axis == 2:
            return m.reshape(R, 4, C // 4).transpose(1, 0, 2)
        return m.reshape(4, R // 4, C)

    halves = [jnp.stack([shards(m) for m in layers[h * (L // 2):(h + 1) * (L // 2)]], axis=1) for h in range(2)]
    return jnp.stack(halves)


def _unpack_full(gathered, axis):
    _, Lh, Rs, Cs = gathered.shape
    t = gathered.reshape(4, 2, Lh, Rs, Cs)
    layers = []
    for h in range(2):
        for l in range(Lh):
            piece = t[:, h, l]
            if axis == 2:
                layers.append(piece.transpose(1, 0, 2).reshape(Rs, 4 * Cs))
            else:
                layers.append(piece.reshape(4 * Rs, Cs))
    return layers


def _sum_slabs(own, recv, name, with_bf16=False):
    C = own.shape[-1]
    out = _sum_leading(recv.reshape(recv.shape[0], -1, C), name, own=own.reshape(-1, C), with_bf16=with_bf16)
    if with_bf16:
        return out[0].reshape(own.shape), out[1].reshape(own.shape)
    return out.reshape(own.shape)


def _gather8(x, name):
    return _all_gather8([x], name)[0]


def _rope_tables(positions):
    half = ROT_DIM // 2
    inv_freq = jnp.power(jnp.float32(ROPE_THETA), -jnp.arange(half, dtype=F32) * 2.0 / ROT_DIM)
    ang = positions.astype(F32).reshape(-1, 1) * inv_freq
    cos, sin = jnp.cos(ang), jnp.sin(ang)
    T = ang.shape[0]
    rest = HEAD_DIM - ROT_DIM
    c64 = jnp.concatenate([cos, cos, jnp.ones((T, rest), F32)], axis=1)
    s64 = jnp.concatenate([-sin, sin, jnp.zeros((T, rest), F32)], axis=1)
    return jnp.tile(c64, (1, 2)), jnp.tile(s64, (1, 2))


def _gain_rows(q_gain, k_gain):
    q2 = jnp.tile(q_gain.reshape(1, HEAD_DIM), (GROUP_A, 2))
    k2 = jnp.tile(k_gain.reshape(1, HEAD_DIM), (1, 2))
    return jnp.concatenate([q2, k2, jnp.ones((1, LANES), F32)], axis=0)


def _local_step(x, positions, mod, norm1_g, norm2_g, q_norm_a, k_norm_a, sinks_a,
                wqkv_a, wo_a, wqkv_b, wo_b, wgu, wd, loss_target):
    NB, S, D = x.shape
    T = NB * S
    QA = N_Q_A * HEAD_DIM
    tab_c, tab_s = _rope_tables(positions)
    rev, fwd = _cumsum_mats()

    saved = []
    xc = x
    for i in range(DEPTH):
        j = i // 2
        sh1, sc1, g1, sh2, sc2, g2 = [mod[i][:, k * D:(k + 1) * D].reshape(NB, 1, D) for k in range(6)]
        st = dict(x=xc, sc1=sc1, g1=g1, sc2=sc2, g2=g2)
        h = _norm_mod_fwd(xc, norm1_g[i:i + 1], sc1, sh1)
        st["h"] = h.reshape(T, D)
        if i % 2 == 0:
            st["qkv"] = _matmul(st["h"], wqkv_a[j], "nn", F32, "qkv_a")
            st["gains"] = _gain_rows(q_norm_a[j], k_norm_a[j])
            st["qkn"] = _qk_prep_fwd(st["qkv"], tab_c, tab_s, st["gains"]).reshape(NB, S, -1)
            st["o"] = _attn_a_fwd(st["qkn"], sinks_a[j]).reshape(T, QA)
            y = _matmul(st["o"], wo_a[j], "nn", F32, "wo_a")
        else:
            st["qkv"] = _matmul(st["h"], wqkv_b[j], "nn", BF16, "qkv_b").reshape(NB, S, -1)
            st["o"] = _attn_b_fwd(st["qkv"], rev).reshape(T, N_H_B * HEAD_DIM)
            y = _matmul(st["o"], wo_b[j], "nn", F32, "wo_b")
        st["y"] = y.reshape(NB, S, D)
        x1 = _gate_res(xc, st["y"], g1)
        st["x1"] = x1
        h2 = _norm_mod_fwd(x1, norm2_g[i:i + 1], sc2, sh2)
        st["h2"] = h2.reshape(T, D)
        st["gu"] = _matmul(st["h2"], wgu[i], "nn", F32, "gate_up")
        st["act"] = _swiglu_fwd(st["gu"])
        st["m"] = _matmul(st["act"], wd[i], "nn", F32, "down").reshape(NB, S, D)
        xc = _gate_res(x1, st["m"], g2)
        saved.append(st)

    loss, dx = _loss_fwd_bwd(xc, loss_target)

    grads = {name: [None] * n for name, n in
             (("wqkv_a", 2), ("wo_a", 2), ("wqkv_b", 2), ("wo_b", 2), ("wgu", DEPTH), ("wd", DEPTH),
              ("norm1_g", DEPTH), ("norm2_g", DEPTH), ("q_norm_a", 2), ("k_norm_a", 2), ("sinks_a", 2))}
    dmod = [None] * DEPTH
    for i in reversed(range(DEPTH)):
        j = i // 2
        st = saved[i]
        dm, dg2 = _gate_res_bwd(dx, st["m"], st["g2"])
        dm = dm.reshape(T, D)
        dact = _matmul(dm, wd[i], "nt", F32, "d_act")
        grads["wd"][i] = _matmul(st["act"], dm, "tn", F32, "d_wd")
        dgu = _swiglu_bwd(st["gu"], dact)
        dh2 = _matmul(dgu, wgu[i], "nt", F32, "d_h2")
        grads["wgu"][i] = _matmul(st["h2"], dgu, "tn", F32, "d_wgu")
        dx1, dsh2, dsc2, grads["norm2_g"][i] = _norm_mod_bwd(
            st["x1"], norm2_g[i:i + 1], st["sc2"], dh2.reshape(NB, S, D), dx)
        dy, dg1 = _gate_res_bwd(dx1, st["y"], st["g1"])
        dy = dy.reshape(T, D)
        if i % 2 == 0:
            do = _matmul(dy, wo_a[j], "nt", BF16, "d_o_a").reshape(NB, S, QA)
            grads["wo_a"][j] = _matmul(st["o"], dy, "tn", F32, "d_wo_a")
            dq, dk, dv, dsink = _attn_a_bwd(st["qkn"], do, sinks_a[j])
            dqkv, dgain = _qk_prep_bwd(st["qkv"], dq.reshape(T, QA), dk.reshape(T, LANES), dv.reshape(T, LANES),
                                       tab_c, tab_s, st["gains"])
            dh = _matmul(dqkv, wqkv_a[j], "nt", F32, "d_h_a")
            grads["wqkv_a"][j] = _matmul(st["h"], dqkv, "tn", F32, "d_wqkv_a")
            grads["q_norm_a"][j] = jnp.sum(dgain[:GROUP_A].reshape(2 * GROUP_A, HEAD_DIM), axis=0)
            grads["k_norm_a"][j] = jnp.sum(dgain[GROUP_A].reshape(2, HEAD_DIM), axis=0)
            grads["sinks_a"][j] = jnp.sum(dsink[..., 0], axis=0)
        else:
            do = _matmul(dy, wo_b[j], "nt", BF16, "d_o_b").reshape(NB, S, -1)
            grads["wo_b"][j] = _matmul(st["o"], dy, "tn", F32, "d_wo_b")
            dq, dk, dv = _attn_b_bwd(st["qkv"], do, rev, fwd)
            dqkv = jnp.concatenate([dq, dk, dv], axis=-1).reshape(T, -1).astype(BF16)
            dh = _matmul(dqkv, wqkv_b[j], "nt", F32, "d_h_b")
            grads["wqkv_b"][j] = _matmul(st["h"], dqkv, "tn", F32, "d_wqkv_b")
        dx, dsh1, dsc1, grads["norm1_g"][i] = _norm_mod_bwd(
            st["x"], norm1_g[i:i + 1], st["sc1"], dh.reshape(NB, S, D), dx1)
        dmod[i] = jnp.concatenate([dsh1, dsc1, dg1, dsh2, dsc2, dg2], axis=-1).reshape(NB, 6 * D)

    matrices = ("wqkv_a", "wo_a", "wqkv_b", "wo_b", "wgu", "wd")
    grads = {name: parts if name in matrices else jnp.stack(parts) for name, parts in grads.items()}
    return loss, dx, grads, jnp.stack(dmod)


def _rows_of(flat, cols=PACK_COLS):
    n = flat.shape[0]
    pad = (-n) % (8 * cols)
    if pad:
        flat = jnp.concatenate([flat, jnp.zeros((pad,), flat.dtype)])
    return flat.reshape(-1, cols)


def kernel(x, c, positions, ada_w, ada_b, norm1_g, norm2_g, wqkv_a, q_norm_a, k_norm_a, sinks_a, wo_a, wqkv_b, wo_b, w_gate, w_up, w_down, loss_target, m_ada_w, m_ada_b, m_norm1_g, m_norm2_g, m_wqkv_a, m_q_norm_a, m_k_norm_a, m_sinks_a, m_wo_a, m_wqkv_b, m_wo_b, m_w_gate, m_w_up, m_w_down, v_ada_w, v_ada_b, v_norm1_g, v_norm2_g, v_wqkv_a, v_q_norm_a, v_k_norm_a, v_sinks_a, v_wo_a, v_wqkv_b, v_wo_b, v_w_gate, v_w_up, v_w_down):
    xi, yi, ci = lax.axis_index("x"), lax.axis_index("y"), lax.axis_index("c")
    dev = 4 * xi + 2 * yi + ci
    chip = 2 * xi + yi
    NB, S, D = x.shape
    B_all = N_DEV * NB
    L = ada_w.shape[0]
    n_mod = ada_w.shape[2] // 2

    c_all = _gather8(_rows_of(c.reshape(-1), LANES), "gather_c").reshape(N_DEV, -1)[:, :NB * D].reshape(B_all, D)
    ada_w_half = lax.dynamic_slice_in_dim(ada_w, ci * n_mod, n_mod, axis=2)
    ada_b_half = lax.dynamic_slice_in_dim(ada_b, dev * n_mod, n_mod, axis=1).reshape(L, 1, n_mod)
    mod_part = _ada_fwd(c_all, ada_w_half, ada_b_half)
    n_part = L * B_all * n_mod
    mod_all = _gather8(_rows_of(mod_part.reshape(-1)), "gather_mod").reshape(N_DEV, -1)[:, :n_part]
    mod_all = mod_all.reshape(N_DEV, L, B_all, n_mod).transpose(1, 2, 0, 3).reshape(L, B_all, N_DEV * n_mod)
    mod = lax.dynamic_slice_in_dim(mod_all, dev * NB, NB, axis=1)

    shards = dict(wqkv_a=wqkv_a, wo_a=wo_a, wqkv_b=wqkv_b, wo_b=wo_b, w_gate=w_gate, w_up=w_up, w_down=w_down)
    halves = []
    for name, _ in _SHARDED:
        w = shards[name]
        half = lax.dynamic_index_in_dim(w.reshape((2, w.shape[0] // 2) + w.shape[1:]), ci, 0, keepdims=False)
        halves.append(half.astype(BF16))
    gathered = _all_gather8(halves, "gather_weights", local_axis=1, local_chunks=8)
    full = {name: _unpack_full(t, axis) for (name, axis), t in zip(_SHARDED, gathered)}
    wgu = [_interleave(gate, up) for gate, up in zip(full["w_gate"], full["w_up"])]

    loss, grad_x, g, dmod = _local_step(
        x, positions, mod, norm1_g, norm2_g, q_norm_a, k_norm_a, sinks_a,
        full["wqkv_a"], full["wo_a"], full["wqkv_b"], full["wo_b"], wgu, full["w_down"], loss_target)

    gate_up = [_deinterleave(t) for t in g["wgu"]]
    g_full = dict(wqkv_a=g["wqkv_a"], wo_a=g["wo_a"], wqkv_b=g["wqkv_b"], wo_b=g["wo_b"],
                  w_gate=[t[0] for t in gate_up], w_up=[t[1] for t in gate_up], w_down=g["wd"])
    packed = [_pack_full(g_full[name], axis) for name, axis in _SHARDED]
    def own(t, index):
        return lax.dynamic_index_in_dim(t, index, 0, keepdims=False)

    from_cores = _exchange(packed, "c", "rs_cores", chunk_axis=0, chunks=4)
    chip_part = [_sum_slabs(own(p, ci), r, "rs_add_cores", with_bf16=True) for p, r in zip(packed, from_cores)]
    from_chips = _exchange([b for _, b in chip_part], "xy", "rs_chips")
    mine = [_sum_slabs(own(p, chip), r, "rs_add_chips") for (p, _), r in zip(chip_part, from_chips)]
    theirs = _sibling_send(mine, "rs_halves")
    grad = {}
    for (name, _), m, t in zip(_SHARDED, mine, theirs):
        first, second = jnp.where(ci == 0, m, t), jnp.where(ci == 0, t, m)
        grad[name] = jnp.stack([first, second]).reshape(shards[name].shape)

    small_names = ("norm1_g", "norm2_g", "q_norm_a", "k_norm_a", "sinks_a")
    small = [dmod.reshape(-1)] + [g[name].reshape(-1) for name in small_names] + [loss.reshape(-1)]
    small_sizes = [t.shape[0] for t in small]
    small_rows = _rows_of(jnp.concatenate(small))
    small_all = _gather8(small_rows, "gather_small")
    small_sum = _sum_leading(small_all, "sum_small").reshape(-1)
    n_dmod = small_sizes[0]
    dmod_all = small_all.reshape(N_DEV, -1)[:, :n_dmod].reshape(N_DEV, L, NB, 6 * D)
    dmod_all = dmod_all.transpose(1, 0, 2, 3).reshape(L, B_all, 6 * D)
    off = n_dmod
    for name, sz in zip(small_names + ("loss",), small_sizes[1:]):
        grad[name] = small_sum[off:off + sz]
        off += sz
    loss_total = grad.pop("loss").reshape(())
    for name, ref in (("norm1_g", norm1_g), ("norm2_g", norm2_g), ("q_norm_a", q_norm_a),
                      ("k_norm_a", k_norm_a), ("sinks_a", sinks_a)):
        grad[name] = grad[name].reshape(ref.shape)

    n_shard = ada_w.shape[2]
    dmod_shard = lax.dynamic_slice_in_dim(dmod_all, chip * n_shard, n_shard, axis=2)
    grad["ada_w"], gb = _ada_bwd(c_all, dmod_all, dmod_shard)
    grad["ada_b"] = gb.reshape(ada_b.shape)

    weights = dict(ada_w=ada_w, ada_b=ada_b, norm1_g=norm1_g, norm2_g=norm2_g, wqkv_a=wqkv_a, q_norm_a=q_norm_a,
                   k_norm_a=k_norm_a, sinks_a=sinks_a, wo_a=wo_a, wqkv_b=wqkv_b, wo_b=wo_b, w_gate=w_gate,
                   w_up=w_up, w_down=w_down)
    m_in = dict(ada_w=m_ada_w, ada_b=m_ada_b, norm1_g=m_norm1_g, norm2_g=m_norm2_g, wqkv_a=m_wqkv_a,
                q_norm_a=m_q_norm_a, k_norm_a=m_k_norm_a, sinks_a=m_sinks_a, wo_a=m_wo_a, wqkv_b=m_wqkv_b,
                wo_b=m_wo_b, w_gate=m_w_gate, w_up=m_w_up, w_down=m_w_down)
    v_in = dict(ada_w=v_ada_w, ada_b=v_ada_b, norm1_g=v_norm1_g, norm2_g=v_norm2_g, wqkv_a=v_wqkv_a,
                q_norm_a=v_q_norm_a, k_norm_a=v_k_norm_a, sinks_a=v_sinks_a, wo_a=v_wo_a, wqkv_b=v_wqkv_b,
                wo_b=v_wo_b, w_gate=v_w_gate, w_up=v_w_up, w_down=v_w_down)
    names = list(weights)
    delta, new_m, new_v = {}, {}, {}
    for name in names:
        delta[name], new_m[name], new_v[name] = _adamw(weights[name], grad[name], m_in[name], v_in[name],
                                                       "adamw_" + name)
    return (loss_total, grad_x, *[grad[k] for k in names], *[delta[k] for k in names],
            *[new_m[k] for k in names], *[new_v[k] for k in names])
```

```python
import jax
import jax.numpy as jnp
from jax import lax
from jax.experimental import pallas as pl
from jax.experimental.pallas import tpu as pltpu

F32 = jnp.float32
BF16 = jnp.bfloat16

DEPTH = 4
HEAD_DIM = 64
N_Q_A = 16
N_KV_A = 2
GROUP_A = N_Q_A // N_KV_A
N_H_B = 16
BLOCK = 128
ROT_DIM = HEAD_DIM // 4
ROPE_THETA = 500000.0
EPS = 1e-6
ATTN_SCALE = HEAD_DIM ** -0.5
NEG_BIG = -1e30

ADAM_LR = 0.001
ADAM_B1 = 0.9
ADAM_B2 = 0.999
ADAM_EPS = 1e-08
ADAM_WD = 0.01
ADAM_STEP = 10

N_DEV = 8
LANES = 128
PACK_COLS = 1024
VMEM_LIMIT_BYTES = 48 * 1024 * 1024
MESH = pl.DeviceIdType.MESH

_NT = (((1,), (1,)), ((), ()))
_TN = (((0,), (0,)), ((), ()))
_NN = (((1,), (0,)), ((), ()))


def _params(sem=None):
    return pltpu.CompilerParams(vmem_limit_bytes=VMEM_LIMIT_BYTES, dimension_semantics=sem)


def _pick(n, cap, mult):
    best = None
    for t in range(mult, min(n, cap) + 1, mult):
        if n % t == 0:
            best = t
    return n if best is None else best


_ANY = pl.BlockSpec(memory_space=pl.ANY)


def _window(index, axis, q, n, shape):
    rest = [slice(None)] * len(shape)
    size = shape[axis] // n
    rest[axis] = pl.ds(q * size, size)
    return tuple(index) + tuple(rest)


def _all_gather8(xs, name, local_axis=0, local_chunks=1):
    n = len(xs)

    def body(*refs):
        x_refs, out_refs = refs[:n], refs[n:2 * n]
        send_sems, recv_sems, local_sems = refs[2 * n:]
        xi, yi, ci = lax.axis_index("x"), lax.axis_index("y"), lax.axis_index("c")
        me, sibling = (xi, yi, ci), (xi, yi, 1 - ci)
        chips = [(1 - xi, yi), (xi, 1 - yi), (1 - xi, 1 - yi)]

        def slab(w, px, py, pc):
            return out_refs[w].at[4 * px + 2 * py + pc]

        def copy(w, k, block, to, src=None):
            return pltpu.make_async_remote_copy(
                src_ref=slab(w, *block) if src is None else src, dst_ref=slab(w, *block),
                send_sem=send_sems.at[k, w], recv_sem=recv_sems.at[k, w], device_id=to, device_id_type=MESH)

        mine = []
        for w in range(n):
            for q in range(local_chunks):
                part = _window((), local_axis, q, local_chunks, xs[w].shape)
                mine.append(pltpu.make_async_copy(x_refs[w].at[part], slab(w, *me).at[part], local_sems.at[w, q]))
                mine[-1].start()
        first = [copy(w, 0, me, sibling, src=x_refs[w]) for w in range(n)]
        first += [copy(w, 1 + j, me, (*chip, ci), src=x_refs[w]) for j, chip in enumerate(chips) for w in range(n)]
        for cp in first:
            cp.start()
        passed = []
        for j, chip in enumerate(chips):
            for w in range(n):
                copy(w, 1 + j, (*chip, ci), me).wait_recv()
                passed.append(copy(w, 4 + j, (*chip, ci), sibling))
                passed[-1].start()
        for w in range(n):
            copy(w, 0, sibling, me).wait_recv()
        for j, chip in enumerate(chips):
            for w in range(n):
                copy(w, 4 + j, (*chip, 1 - ci), me).wait_recv()
        for cp in first + passed:
            cp.wait_send()
        for cp in mine:
            cp.wait()

    return pl.pallas_call(
        body, name=name,
        out_shape=[jax.ShapeDtypeStruct((N_DEV,) + x.shape, x.dtype) for x in xs],
        in_specs=[_ANY] * n, out_specs=[_ANY] * n,
        scratch_shapes=[pltpu.SemaphoreType.DMA((7, n)), pltpu.SemaphoreType.DMA((7, n)),
                        pltpu.SemaphoreType.DMA((n, local_chunks))],
    )(*xs)


def _exchange(xs, group, name, chunk_axis=0, chunks=1):
    n = len(xs)
    n_peers = 1 if group == "c" else 3

    def body(*refs):
        x_refs, out_refs = refs[:n], refs[n:2 * n]
        send_sems, recv_sems = refs[2 * n:]
        xi, yi, ci = lax.axis_index("x"), lax.axis_index("y"), lax.axis_index("c")
        if group == "c":
            peers = [(1 - ci, (xi, yi, 1 - ci))]
        else:
            peers = [(2 * (1 - xi) + yi, (1 - xi, yi, ci)),
                     (2 * xi + (1 - yi), (xi, 1 - yi, ci)),
                     (2 * (1 - xi) + (1 - yi), (1 - xi, 1 - yi, ci))]
        copies = []
        for k, (p, dev) in enumerate(peers):
            for w in range(n):
                slab_shape = xs[w].shape[1:]
                for q in range(chunks):
                    copies.append(pltpu.make_async_remote_copy(
                        src_ref=x_refs[w].at[_window((p,), chunk_axis, q, chunks, slab_shape)],
                        dst_ref=out_refs[w].at[_window((k,), chunk_axis, q, chunks, slab_shape)],
                        send_sem=send_sems.at[k, w, q], recv_sem=recv_sems.at[k, w, q],
                        device_id=dev, device_id_type=MESH))
                    copies[-1].start()
        for cp in copies:
            cp.wait()

    return pl.pallas_call(
        body, name=name,
        out_shape=[jax.ShapeDtypeStruct((n_peers,) + x.shape[1:], x.dtype) for x in xs],
        in_specs=[_ANY] * n, out_specs=[_ANY] * n,
        scratch_shapes=[pltpu.SemaphoreType.DMA((n_peers, n, chunks)), pltpu.SemaphoreType.DMA((n_peers, n, chunks))],
    )(*xs)


def _sibling_send(xs, name, chunk_axis=1, chunks=4):
    n = len(xs)

    def body(*refs):
        x_refs, out_refs = refs[:n], refs[n:2 * n]
        send_sems, recv_sems = refs[2 * n:]
        xi, yi, ci = lax.axis_index("x"), lax.axis_index("y"), lax.axis_index("c")
        copies = []
        for w in range(n):
            for q in range(chunks):
                part = _window((), chunk_axis, q, chunks, xs[w].shape)
                copies.append(pltpu.make_async_remote_copy(
                    src_ref=x_refs[w].at[part], dst_ref=out_refs[w].at[part],
                    send_sem=send_sems.at[w, q], recv_sem=recv_sems.at[w, q],
                    device_id=(xi, yi, 1 - ci), device_id_type=MESH))
                copies[-1].start()
        for cp in copies:
            cp.wait()

    return pl.pallas_call(
        body, name=name,
        out_shape=[jax.ShapeDtypeStruct(x.shape, x.dtype) for x in xs],
        in_specs=[_ANY] * n, out_specs=[_ANY] * n,
        scratch_shapes=[pltpu.SemaphoreType.DMA((n, chunks)), pltpu.SemaphoreType.DMA((n, chunks))],
    )(*xs)


def _sum_leading(x, name, own=None, with_bf16=False):
    P, R, C = x.shape
    tr = _pick(R, max(16, (1 << 19) // (C * (P + 1))), 16)

    def body(*refs):
        n_in = 1 if own is None else 2
        x_ref = refs[n_in - 1]
        acc = x_ref[0].astype(F32) if own is None else refs[0][...] + x_ref[0].astype(F32)
        for p in range(1, P):
            acc = acc + x_ref[p].astype(F32)
        refs[n_in][...] = acc
        if with_bf16:
            refs[n_in + 1][...] = acc.astype(BF16)

    flat = pl.BlockSpec((tr, C), lambda r: (r, 0))
    slabs = pl.BlockSpec((P, tr, C), lambda r: (0, r, 0))
    out = pl.pallas_call(
        body, name=name, grid=(R // tr,),
        in_specs=[slabs] if own is None else [flat, slabs],
        out_specs=[flat, flat] if with_bf16 else [flat],
        out_shape=[jax.ShapeDtypeStruct((R, C), F32)] + ([jax.ShapeDtypeStruct((R, C), BF16)] if with_bf16 else []),
        compiler_params=_params(("arbitrary",)),
    )(*([x] if own is None else [own, x]))
    return out if with_bf16 else out[0]


def _matmul(a, b, mode, out_dtype, name):
    if mode == "nn":
        (M, K), N = a.shape, b.shape[1]
    elif mode == "nt":
        (M, K), N = a.shape, b.shape[0]
    else:
        (K, M), N = a.shape, b.shape[1]
    tm = _pick(M, 1024 if mode != "tn" else 1536, 128)
    tn = _pick(N, 1536, 128)
    tk = _pick(K, 512, 128)
    nk = K // tk
    dims = {"nn": _NN, "nt": _NT, "tn": _TN}[mode]

    def body(a_ref, b_ref, o_ref, acc_ref):
        k = pl.program_id(2)

        @pl.when(k == 0)
        def _():
            acc_ref[...] = jnp.zeros_like(acc_ref)

        acc_ref[...] += lax.dot_general(a_ref[...].astype(BF16), b_ref[...].astype(BF16), dims,
                                        preferred_element_type=F32)

        @pl.when(k == nk - 1)
        def _():
            o_ref[...] = acc_ref[...].astype(o_ref.dtype)

    if mode == "tn":
        a_spec = pl.BlockSpec((tk, tm), lambda i, j, k: (k, i))
    else:
        a_spec = pl.BlockSpec((tm, tk), lambda i, j, k: (i, k))
    if mode == "nt":
        b_spec = pl.BlockSpec((tn, tk), lambda i, j, k: (j, k))
    else:
        b_spec = pl.BlockSpec((tk, tn), lambda i, j, k: (k, j))
    return pl.pallas_call(
        body, name=name, grid=(M // tm, N // tn, nk),
        in_specs=[a_spec, b_spec],
        out_specs=pl.BlockSpec((tm, tn), lambda i, j, k: (i, j)),
        out_shape=jax.ShapeDtypeStruct((M, N), out_dtype),
        scratch_shapes=[pltpu.VMEM((tm, tn), F32)],
        compiler_params=_params(("parallel", "parallel", "arbitrary")),
    )(a, b)


def _row_tile(S):
    return _pick(S, 512, 8)


def _norm_mod_fwd(x, gain, sc, sh):
    NB, S, D = x.shape
    tr = _row_tile(S)

    def body(x_ref, g_ref, sc_ref, sh_ref, h_ref):
        xv = x_ref[...]
        ms = jnp.mean(xv * xv, axis=-1, keepdims=True)
        n = xv * lax.rsqrt(ms + EPS) * g_ref[...]
        h_ref[...] = (n * (1.0 + sc_ref[...]) + sh_ref[...]).astype(BF16)

    tok = pl.BlockSpec((None, tr, D), lambda b, r: (b, r, 0))
    per_ex = pl.BlockSpec((None, 1, D), lambda b, r: (b, 0, 0))
    return pl.pallas_call(
        body, name="norm_mod_fwd", grid=(NB, S // tr),
        in_specs=[tok, pl.BlockSpec((1, D), lambda b, r: (0, 0)), per_ex, per_ex],
        out_specs=tok, out_shape=jax.ShapeDtypeStruct((NB, S, D), BF16),
        compiler_params=_params(("parallel", "parallel")),
    )(x, gain, sc, sh)


def _norm_mod_bwd(x, gain, sc, dh, dres):
    NB, S, D = x.shape
    tr = _row_tile(S)

    def body(x_ref, g_ref, sc_ref, dh_ref, dres_ref, dx_ref, dsh_ref, dsc_ref, dg_ref):
        b, r = pl.program_id(0), pl.program_id(1)

        @pl.when(r == 0)
        def _():
            dsh_ref[...] = jnp.zeros_like(dsh_ref)
            dsc_ref[...] = jnp.zeros_like(dsc_ref)

        @pl.when((r == 0) & (b == 0))
        def _():
            dg_ref[...] = jnp.zeros_like(dg_ref)

        xv = x_ref[...]
        rstd = lax.rsqrt(jnp.mean(xv * xv, axis=-1, keepdims=True) + EPS)
        xh = xv * rstd
        g = g_ref[...]
        dh = dh_ref[...]
        dsh_ref[...] += jnp.sum(dh, axis=0, keepdims=True)
        dsc_ref[...] += jnp.sum(dh * (xh * g), axis=0, keepdims=True)
        dn = dh * (1.0 + sc_ref[...])
        dg_ref[...] += jnp.sum(dn * xh, axis=0, keepdims=True)
        dxh = dn * g
        proj = jnp.mean(dxh * xh, axis=-1, keepdims=True)
        dx_ref[...] = rstd * (dxh - xh * proj) + dres_ref[...]

    tok = pl.BlockSpec((None, tr, D), lambda b, r: (b, r, 0))
    per_ex = pl.BlockSpec((None, 1, D), lambda b, r: (b, 0, 0))
    row = pl.BlockSpec((1, D), lambda b, r: (0, 0))
    return pl.pallas_call(
        body, name="norm_mod_bwd", grid=(NB, S // tr),
        in_specs=[tok, row, per_ex, tok, tok],
        out_specs=[tok, per_ex, per_ex, row],
        out_shape=[jax.ShapeDtypeStruct((NB, S, D), F32), jax.ShapeDtypeStruct((NB, 1, D), F32),
                   jax.ShapeDtypeStruct((NB, 1, D), F32), jax.ShapeDtypeStruct((1, D), F32)],
        compiler_params=_params(("arbitrary", "arbitrary")),
    )(x, gain, sc, dh, dres)


def _gate_res(x, y, g):
    NB, S, D = x.shape
    tr = _row_tile(S)

    def body(x_ref, y_ref, g_ref, o_ref):
        o_ref[...] = x_ref[...] + g_ref[...] * y_ref[...]

    tok = pl.BlockSpec((None, tr, D), lambda b, r: (b, r, 0))
    per_ex = pl.BlockSpec((None, 1, D), lambda b, r: (b, 0, 0))
    return pl.pallas_call(
        body, name="gate_res", grid=(NB, S // tr), in_specs=[tok, tok, per_ex], out_specs=tok,
        out_shape=jax.ShapeDtypeStruct((NB, S, D), F32),
        compiler_params=_params(("parallel", "parallel")),
    )(x, y, g)


def _gate_res_bwd(dxo, y, g):
    NB, S, D = dxo.shape
    tr = _row_tile(S)

    def body(d_ref, y_ref, g_ref, dy_ref, dg_ref):
        @pl.when(pl.program_id(1) == 0)
        def _():
            dg_ref[...] = jnp.zeros_like(dg_ref)

        d = d_ref[...]
        dy_ref[...] = (d * g_ref[...]).astype(BF16)
        dg_ref[...] += jnp.sum(d * y_ref[...], axis=0, keepdims=True)

    tok = pl.BlockSpec((None, tr, D), lambda b, r: (b, r, 0))
    per_ex = pl.BlockSpec((None, 1, D), lambda b, r: (b, 0, 0))
    return pl.pallas_call(
        body, name="gate_res_bwd", grid=(NB, S // tr), in_specs=[tok, tok, per_ex], out_specs=[tok, per_ex],
        out_shape=[jax.ShapeDtypeStruct((NB, S, D), BF16), jax.ShapeDtypeStruct((NB, 1, D), F32)],
        compiler_params=_params(("arbitrary", "arbitrary")),
    )(dxo, y, g)


def _sigmoid(v):
    return 1.0 / (1.0 + jnp.exp(-v))


def _ff_tile(F):
    return _pick(F, 1536, 128)


def _interleave(gate, up):
    F = gate.shape[-1]
    tf = _ff_tile(F)
    parts = []
    for j in range(F // tf):
        parts += [gate[..., j * tf:(j + 1) * tf], up[..., j * tf:(j + 1) * tf]]
    return jnp.concatenate(parts, axis=-1)


def _deinterleave(gu):
    F = gu.shape[-1] // 2
    tf = _ff_tile(F)
    gate = [gu[..., 2 * j * tf:(2 * j + 1) * tf] for j in range(F // tf)]
    up = [gu[..., (2 * j + 1) * tf:(2 * j + 2) * tf] for j in range(F // tf)]
    return jnp.concatenate(gate, axis=-1), jnp.concatenate(up, axis=-1)


def _swiglu_fwd(gu):
    T, F2 = gu.shape
    F = F2 // 2
    tf = _ff_tile(F)
    tr = _pick(T, 256, 8)

    def body(gu_ref, o_ref):
        g = gu_ref[:, :tf]
        o_ref[...] = (g * _sigmoid(g) * gu_ref[:, tf:]).astype(BF16)

    return pl.pallas_call(
        body, name="swiglu_fwd", grid=(T // tr, F // tf),
        in_specs=[pl.BlockSpec((tr, 2 * tf), lambda i, j: (i, j))],
        out_specs=pl.BlockSpec((tr, tf), lambda i, j: (i, j)),
        out_shape=jax.ShapeDtypeStruct((T, F), BF16),
        compiler_params=_params(("parallel", "parallel")),
    )(gu)


def _swiglu_bwd(gu, dact):
    T, F2 = gu.shape
    F = F2 // 2
    tf = _ff_tile(F)
    tr = _pick(T, 256, 8)

    def body(gu_ref, d_ref, o_ref):
        g, u, d = gu_ref[:, :tf], gu_ref[:, tf:], d_ref[...]
        s = _sigmoid(g)
        o_ref[:, :tf] = (d * u * (s * (1.0 + g * (1.0 - s)))).astype(BF16)
        o_ref[:, tf:] = (d * (g * s)).astype(BF16)

    return pl.pallas_call(
        body, name="swiglu_bwd", grid=(T // tr, F // tf),
        in_specs=[pl.BlockSpec((tr, 2 * tf), lambda i, j: (i, j)), pl.BlockSpec((tr, tf), lambda i, j: (i, j))],
        out_specs=pl.BlockSpec((tr, 2 * tf), lambda i, j: (i, j)),
        out_shape=jax.ShapeDtypeStruct((T, F2), BF16),
        compiler_params=_params(("parallel", "parallel")),
    )(gu, dact)


def _loss_fwd_bwd(y, target):
    NB, S, D = y.shape
    tr = _row_tile(S)

    def body(y_ref, t_ref, l_ref, d_ref):
        @pl.when((pl.program_id(0) == 0) & (pl.program_id(1) == 0))
        def _():
            l_ref[...] = jnp.zeros_like(l_ref)

        e = y_ref[...] - t_ref[...]
        d_ref[...] = e / D
        l_ref[...] += 0.5 * jnp.sum(jnp.mean(e * e, axis=-1, keepdims=True), axis=0, keepdims=True)

    tok = pl.BlockSpec((None, tr, D), lambda b, r: (b, r, 0))
    return pl.pallas_call(
        body, name="loss", grid=(NB, S // tr), in_specs=[tok, tok],
        out_specs=[pl.BlockSpec((1, 1), lambda b, r: (0, 0)), tok],
        out_shape=[jax.ShapeDtypeStruct((1, 1), F32), jax.ShapeDtypeStruct((NB, S, D), F32)],
        compiler_params=_params(("arbitrary", "arbitrary")),
    )(y, target)


def _half_sums(v, lo):
    sa = jnp.sum(jnp.where(lo, v, 0.0), axis=-1, keepdims=True)
    sb = jnp.sum(jnp.where(lo, 0.0, v), axis=-1, keepdims=True)
    return jnp.where(lo, sa, sb)


def _rope_swap(v, lane64):
    up = pltpu.roll(v, LANES - ROT_DIM // 2, 1)
    down = pltpu.roll(v, ROT_DIM // 2, 1)
    return jnp.where(lane64 < ROT_DIM // 2, up, jnp.where(lane64 < ROT_DIM, down, 0.0))


def _qk_prep_fwd(qkv, tab_c, tab_s, gains):
    T, W = qkv.shape
    R = W // LANES
    tt = _pick(T, 256, 8)

    def body(x_ref, c_ref, s_ref, g_ref, o_ref):
        lane = lax.broadcasted_iota(jnp.int32, (tt, LANES), 1)
        lo = lane < HEAD_DIM
        lane64 = lane & (HEAD_DIM - 1)
        c, s = c_ref[...], s_ref[...]
        for j in range(R - 1):
            cols = slice(j * LANES, (j + 1) * LANES)
            xv = x_ref[:, cols]
            rstd = lax.rsqrt(_half_sums(xv * xv, lo) / HEAD_DIM + EPS)
            yn = xv * rstd * g_ref[j:j + 1, :]
            o_ref[:, cols] = (yn * c + _rope_swap(yn, lane64) * s).astype(BF16)
        o_ref[:, (R - 1) * LANES:] = x_ref[:, (R - 1) * LANES:].astype(BF16)

    tok = pl.BlockSpec((tt, W), lambda t: (t, 0))
    tab = pl.BlockSpec((tt, LANES), lambda t: (t, 0))
    return pl.pallas_call(
        body, name="qk_prep_fwd", grid=(T // tt,),
        in_specs=[tok, tab, tab, pl.BlockSpec((R, LANES), lambda t: (0, 0))],
        out_specs=tok, out_shape=jax.ShapeDtypeStruct((T, W), BF16),
        compiler_params=_params(("parallel",)),
    )(qkv, tab_c, tab_s, gains)


def _qk_prep_bwd(qkv, dq, dk, dv, tab_c, tab_s, gains):
    T, W = qkv.shape
    R = W // LANES
    QW = dq.shape[1]
    tt = _pick(T, 256, 8)

    def body(x_ref, dq_ref, dk_ref, dv_ref, c_ref, s_ref, g_ref, o_ref, dg_ref):
        @pl.when(pl.program_id(0) == 0)
        def _():
            dg_ref[...] = jnp.zeros_like(dg_ref)

        lane = lax.broadcasted_iota(jnp.int32, (tt, LANES), 1)
        lo = lane < HEAD_DIM
        lane64 = lane & (HEAD_DIM - 1)
        c, s = c_ref[...], s_ref[...]
        for j in range(R - 1):
            cols = slice(j * LANES, (j + 1) * LANES)
            xv = x_ref[:, cols]
            d = dq_ref[:, cols] if j < R - 2 else dk_ref[...]
            rstd = lax.rsqrt(_half_sums(xv * xv, lo) / HEAD_DIM + EPS)
            xh = xv * rstd
            dyn = d * c + _rope_swap(d * s, lane64)
            dg_ref[j:j + 1, :] += jnp.sum(dyn * xh, axis=0, keepdims=True)
            dxh = dyn * g_ref[j:j + 1, :]
            proj = _half_sums(dxh * xh, lo) / HEAD_DIM
            o_ref[:, cols] = (rstd * (dxh - xh * proj)).astype(BF16)
        o_ref[:, (R - 1) * LANES:] = dv_ref[...].astype(BF16)

    tok = pl.BlockSpec((tt, W), lambda t: (t, 0))
    tab = pl.BlockSpec((tt, LANES), lambda t: (t, 0))
    gsp = pl.BlockSpec((R, LANES), lambda t: (0, 0))
    return pl.pallas_call(
        body, name="qk_prep_bwd", grid=(T // tt,),
        in_specs=[tok, pl.BlockSpec((tt, QW), lambda t: (t, 0)), tab, tab, tab, tab, gsp], out_specs=[tok, gsp],
        out_shape=[jax.ShapeDtypeStruct((T, W), BF16), jax.ShapeDtypeStruct((R, LANES), F32)],
        compiler_params=_params(("arbitrary",)),
    )(qkv, dq, dk, dv, tab_c, tab_s, gains)


def _band_mask(i):
    r = lax.broadcasted_iota(jnp.int32, (2 * BLOCK, 2 * BLOCK), 0) & (BLOCK - 1)
    c = lax.broadcasted_iota(jnp.int32, (2 * BLOCK, 2 * BLOCK), 1)
    rel = r + BLOCK - c
    return (rel >= 0) & (rel < BLOCK) & ((c >= BLOCK) | (i > 0))


def _swa_softmax(s, valid, sink):
    s = jnp.where(valid, s * ATTN_SCALE, NEG_BIG)
    m = jnp.maximum(jnp.max(s, axis=1, keepdims=True), sink)
    p = jnp.exp(s - m)
    ps = jnp.exp(sink - m)
    denom = jnp.sum(p, axis=1, keepdims=True) + ps
    return p / denom, ps / denom


A_GROUP = 2


Q_WIDTH_A = N_Q_A * HEAD_DIM
N_PAIR_A = Q_WIDTH_A // LANES


def _swa_specs():
    qs = pl.BlockSpec((None, BLOCK, Q_WIDTH_A), lambda b, i: (b, i, 0))

    def kv(col, back):
        return pl.BlockSpec((None, BLOCK, LANES), lambda b, i: (b, jnp.maximum(i - back, 0), col))

    return qs, kv(N_PAIR_A, 1), kv(N_PAIR_A, 0), kv(N_PAIR_A + 1, 1), kv(N_PAIR_A + 1, 0)


def _dup_heads(t):
    lo = lax.broadcasted_iota(jnp.int32, t.shape, 1) < HEAD_DIM
    sw = pltpu.roll(t.astype(F32), HEAD_DIM, 1).astype(BF16)
    return jnp.where(lo, t, sw), jnp.where(lo, sw, t)


def _kv_tiles(kp_ref, kc_ref, vp_ref, vc_ref):
    kd = _dup_heads(jnp.concatenate([kp_ref[...], kc_ref[...]], axis=0))
    vd = _dup_heads(jnp.concatenate([vp_ref[...], vc_ref[...]], axis=0))
    return kd, vd


def _attn_a_fwd(qkn, sinks):
    NB, S, _ = qkn.shape
    qs, kp, kc, vp, vc = _swa_specs()

    def body(q_ref, kp_ref, kc_ref, vp_ref, vc_ref, sink_ref, o_ref):
        i = pl.program_id(1)
        kd, vd = _kv_tiles(kp_ref, kc_ref, vp_ref, vc_ref)
        valid = _band_mask(i)
        lo = lax.broadcasted_iota(jnp.int32, (BLOCK, LANES), 1) < HEAD_DIM
        top = lax.broadcasted_iota(jnp.int32, (2 * BLOCK, 1), 0) < BLOCK
        for first in range(0, N_PAIR_A, A_GROUP):
            pairs = range(first, first + A_GROUP)
            qs_ = [jnp.concatenate(_head_halves(q_ref[:, p * LANES:(p + 1) * LANES], lo), axis=0) for p in pairs]
            ss = [lax.dot_general(q, kd[2 * p // GROUP_A], _NT, preferred_element_type=F32) for q, p in zip(qs_, pairs)]
            pns = [_swa_softmax(s, valid, jnp.where(top, sink_ref[2 * p], sink_ref[2 * p + 1]))[0]
                   for s, p in zip(ss, pairs)]
            pvs = [jnp.dot(pn.astype(BF16), vd[2 * p // GROUP_A], preferred_element_type=F32) for pn, p in zip(pns, pairs)]
            for pv, p in zip(pvs, pairs):
                o_ref[:, p * LANES:(p + 1) * LANES] = jnp.where(lo, pv[:BLOCK], pv[BLOCK:]).astype(BF16)

    return pl.pallas_call(
        body, name="attn_a_fwd", grid=(NB, S // BLOCK),
        in_specs=[qs, kp, kc, vp, vc, pl.BlockSpec(memory_space=pltpu.SMEM)],
        out_specs=qs, out_shape=jax.ShapeDtypeStruct((NB, S, Q_WIDTH_A), BF16),
        compiler_params=_params(("parallel", "arbitrary")),
    )(qkn, qkn, qkn, qkn, qkn, sinks)


def _attn_a_bwd(qkn, do, sinks):
    NB, S, _ = qkn.shape
    qs, kp, kc, vp, vc = _swa_specs()
    full = pl.BlockSpec((None, S, LANES), lambda b, i: (b, 0, 0))
    sink_out = pl.BlockSpec((None, N_Q_A, LANES), lambda b, i: (b, 0, 0))

    def body(q_ref, do_ref, kp_ref, kc_ref, vp_ref, vc_ref, sink_ref, dq_ref, dk_ref, dv_ref, ds_ref, dk_s, dv_s):
        i = pl.program_id(1)

        @pl.when(i == 0)
        def _():
            dk_ref[...] = jnp.zeros_like(dk_ref)
            dv_ref[...] = jnp.zeros_like(dv_ref)
            ds_ref[...] = jnp.zeros_like(ds_ref)

        dk_s[...] = jnp.zeros_like(dk_s)
        dv_s[...] = jnp.zeros_like(dv_s)
        kd, vd = _kv_tiles(kp_ref, kc_ref, vp_ref, vc_ref)
        valid = _band_mask(i)
        lo = lax.broadcasted_iota(jnp.int32, (BLOCK, LANES), 1) < HEAD_DIM
        top = lax.broadcasted_iota(jnp.int32, (2 * BLOCK, 1), 0) < BLOCK
        for first in range(0, N_PAIR_A, A_GROUP):
            pairs = range(first, first + A_GROUP)
            kvs = [2 * p // GROUP_A for p in pairs]
            qs_ = [jnp.concatenate(_head_halves(q_ref[:, p * LANES:(p + 1) * LANES], lo), axis=0) for p in pairs]
            dos = [jnp.concatenate(_head_halves(do_ref[:, p * LANES:(p + 1) * LANES], lo), axis=0) for p in pairs]
            ss = [lax.dot_general(q, kd[kv], _NT, preferred_element_type=F32) for q, kv in zip(qs_, kvs)]
            dps = [lax.dot_general(d, vd[kv], _NT, preferred_element_type=F32) for d, kv in zip(dos, kvs)]
            sm = [_swa_softmax(s, valid, jnp.where(top, sink_ref[2 * p], sink_ref[2 * p + 1])) for s, p in zip(ss, pairs)]
            deltas = [jnp.sum(pn * dp, axis=1, keepdims=True) for (pn, _), dp in zip(sm, dps)]
            dsbs = [(pn * (dp - delta) * ATTN_SCALE).astype(BF16) for (pn, _), dp, delta in zip(sm, dps, deltas)]
            for n, p in enumerate(pairs):
                dq2 = jnp.dot(dsbs[n], kd[kvs[n]], preferred_element_type=F32)
                dq_ref[:, p * LANES:(p + 1) * LANES] = jnp.where(lo, dq2[:BLOCK], dq2[BLOCK:])
                dk_s[kvs[n]] += lax.dot_general(dsbs[n], qs_[n], _TN, preferred_element_type=F32)
                dv_s[kvs[n]] += lax.dot_general(sm[n][0].astype(BF16), dos[n], _TN, preferred_element_type=F32)
                t = sm[n][1] * deltas[n]
                for hh in range(2):
                    dsink = -jnp.sum(t[hh * BLOCK:(hh + 1) * BLOCK], axis=0, keepdims=True)
                    ds_ref[2 * p + hh:2 * p + hh + 1, :] += jnp.broadcast_to(dsink, (1, LANES))

        lo2 = lax.broadcasted_iota(jnp.int32, (2 * BLOCK, LANES), 1) < HEAD_DIM

        def fold(acc):
            halves = [acc[kv] + pltpu.roll(acc[kv], HEAD_DIM, 1) for kv in range(N_KV_A)]
            return jnp.where(lo2, halves[0], halves[1])

        dk2, dv2 = fold(dk_s), fold(dv_s)

        @pl.when(i > 0)
        def _():
            start = pl.multiple_of((i - 1) * BLOCK, BLOCK)
            dk_ref[pl.ds(start, 2 * BLOCK), :] += dk2
            dv_ref[pl.ds(start, 2 * BLOCK), :] += dv2

        @pl.when(i == 0)
        def _():
            dk_ref[0:BLOCK, :] += dk2[BLOCK:, :]
            dv_ref[0:BLOCK, :] += dv2[BLOCK:, :]

    slots = pltpu.VMEM((N_KV_A, 2 * BLOCK, LANES), F32)
    return pl.pallas_call(
        body, name="attn_a_bwd", grid=(NB, S // BLOCK),
        in_specs=[qs, qs, kp, kc, vp, vc, pl.BlockSpec(memory_space=pltpu.SMEM)],
        out_specs=[qs, full, full, sink_out],
        out_shape=[jax.ShapeDtypeStruct((NB, S, Q_WIDTH_A), F32), jax.ShapeDtypeStruct((NB, S, LANES), F32),
                   jax.ShapeDtypeStruct((NB, S, LANES), F32), jax.ShapeDtypeStruct((NB, N_Q_A, LANES), F32)],
        scratch_shapes=[slots, slots],
        compiler_params=_params(("parallel", "arbitrary")),
    )(qkn, do, qkn, qkn, qkn, qkn, sinks)


def _cumsum_mats():
    src = lax.broadcasted_iota(jnp.int32, (2 * BLOCK, 2 * BLOCK), 0) % BLOCK
    dst = lax.broadcasted_iota(jnp.int32, (2 * BLOCK, 2 * BLOCK), 1)
    ones = dst >= BLOCK
    rev = ((src > dst) | ones).astype(BF16)
    fwd = ((src < dst) | ones).astype(BF16)
    return rev, fwd


def _log_sigmoids(z):
    sp = jnp.log(1.0 + jnp.exp(-jnp.abs(z)))
    return jnp.minimum(z, 0.0) - sp, -(jnp.maximum(z, 0.0) + sp)


def _cumsum_mxu_many(vs, mat):
    parts = []
    for v in vs:
        hi = v.astype(BF16)
        parts.append(jnp.concatenate([hi, (v - hi.astype(F32)).astype(BF16)], axis=1))
    r = jnp.dot(jnp.concatenate(parts, axis=0), mat, preferred_element_type=F32)
    return [(r[n * BLOCK:(n + 1) * BLOCK, :BLOCK], r[n * BLOCK:(n + 1) * BLOCK, BLOCK:]) for n in range(len(vs))]


def _strict_mask():
    r = lax.broadcasted_iota(jnp.int32, (BLOCK, BLOCK), 0)
    c = lax.broadcasted_iota(jnp.int32, (BLOCK, BLOCK), 1)
    return c < r


def _tile(ref, j):
    return ref[pl.ds(pl.multiple_of(j * BLOCK, BLOCK), BLOCK), :]


SWEEP_EXIT = -88.0


def _head_halves(t, lo):
    zero = jnp.zeros_like(t)
    return jnp.where(lo, t, zero), jnp.where(lo, zero, t)


def _sb_specs(S, HD, width):
    n = HD // width
    blk = pl.BlockSpec((None, BLOCK, width), lambda b, p, i: (b, i, p))
    k_full = pl.BlockSpec((None, S, width), lambda b, p, i: (b, 0, n + p))
    v_full = pl.BlockSpec((None, S, width), lambda b, p, i: (b, 0, 2 * n + p))
    mat = pl.BlockSpec((2 * BLOCK, 2 * BLOCK), lambda b, p, i: (0, 0))
    return blk, k_full, v_full, mat


SB_FWD_PAIRS = 2


def _attn_b_fwd(qkv, rev):
    NB, S, W = qkv.shape
    HD = W // 3
    width = SB_FWD_PAIRS * LANES
    n_heads = 2 * SB_FWD_PAIRS
    blk, k_full, v_full, mat = _sb_specs(S, HD, width)

    def body(q_ref, k_ref, v_ref, rev_ref, o_ref):
        i = pl.program_id(2)
        rv = rev_ref[...]
        mask = _strict_mask()
        lo = lax.broadcasted_iota(jnp.int32, (BLOCK, LANES), 1) < HEAD_DIM
        q_all = q_ref[...]
        q_stack = [jnp.concatenate(_head_halves(q_all[:, p * LANES:(p + 1) * LANES] * ATTN_SCALE, lo), axis=0)
                   for p in range(SB_FWD_PAIRS)]

        def pair_tiles(ref, j):
            t = _tile(ref, j)
            return [t[:, p * LANES:(p + 1) * LANES] for p in range(SB_FWD_PAIRS)]

        def tile_pass(j, carries, diagonal):
            ks, vs = pair_tiles(k_ref, j), pair_tiles(v_ref, j)
            zs = []
            for p in range(SB_FWD_PAIRS):
                z2 = lax.dot_general(q_stack[p], ks[p], _NT, preferred_element_type=F32)
                zs += [z2[:BLOCK], z2[BLOCK:]]
            logs = [_log_sigmoids(z) for z in zs]
            cums = _cumsum_mxu_many([jnp.where(mask, lm, 0.0) if diagonal else lm for _, lm in logs], rv)
            probs, new_c = [], []
            for h in range(n_heads):
                after, rs = cums[h]
                if diagonal:
                    a = jnp.where(mask, jnp.exp(logs[h][0] + after), 0.0)
                    new_c.append(rs)
                else:
                    a = jnp.exp(logs[h][0] + after + carries[h])
                    new_c.append(carries[h] + rs)
                probs.append(a.astype(BF16))
            outs = []
            for p in range(SB_FWD_PAIRS):
                pv = jnp.dot(jnp.concatenate(probs[2 * p:2 * p + 2], axis=0), vs[p], preferred_element_type=F32)
                outs.append(jnp.where(lo, pv[:BLOCK], pv[BLOCK:]))
            return new_c, outs

        carries, accs = tile_pass(i, None, True)

        def live(cs):
            top = cs[0]
            for c in cs[1:]:
                top = jnp.maximum(top, c)
            return jnp.max(top) > SWEEP_EXIT

        def cond(st):
            return (st[0] < i) & st[1]

        def step(st):
            jj, _, cs, accs = st
            new_c, outs = tile_pass(i - 1 - jj, cs, False)
            return jj + 1, live(new_c), new_c, [acc + o for acc, o in zip(accs, outs)]

        st = lax.while_loop(cond, step, (jnp.int32(0), live(carries), carries, accs))
        for p in range(SB_FWD_PAIRS):
            o_ref[:, p * LANES:(p + 1) * LANES] = st[3][p].astype(BF16)

    return pl.pallas_call(
        body, name="attn_b_fwd", grid=(NB, HD // width, S // BLOCK),
        in_specs=[blk, k_full, v_full, mat], out_specs=blk,
        out_shape=jax.ShapeDtypeStruct((NB, S, HD), BF16),
        compiler_params=_params(("parallel", "parallel", "arbitrary")),
    )(qkv, qkv, qkv, rev)


def _attn_b_bwd(qkv, do, rev, fwd):
    NB, S, W = qkv.shape
    HD = W // 3
    n_pair = HD // LANES
    nj = S // BLOCK
    blk, k_full, v_full, mat = _sb_specs(S, HD, LANES)
    acc_full = pl.BlockSpec((None, S, LANES), lambda b, p, i: (b, 0, p))

    def body(q_ref, do_ref, k_ref, v_ref, rev_ref, fwd_ref, dq_ref, dk_ref, dv_ref, sig_s, a_s, e_s):
        i = pl.program_id(2)

        @pl.when(i == 0)
        def _():
            dk_ref[...] = jnp.zeros_like(dk_ref)
            dv_ref[...] = jnp.zeros_like(dv_ref)

        q2, do2 = q_ref[...], do_ref[...]
        rv, fw = rev_ref[...], fwd_ref[...]
        mask = _strict_mask()
        lo = lax.broadcasted_iota(jnp.int32, (BLOCK, LANES), 1) < HEAD_DIM
        q_stack = jnp.concatenate(_head_halves(q2, lo), axis=0)
        qs_stack = q_stack * ATTN_SCALE
        do_stack = jnp.concatenate(_head_halves(do2, lo), axis=0)

        def sweep1_tile(j, carries, diagonal):
            kj, vj = _tile(k_ref, j), _tile(v_ref, j)
            z2 = lax.dot_general(qs_stack, kj, _NT, preferred_element_type=F32)
            da2 = lax.dot_general(do_stack, vj, _NT, preferred_element_type=F32)
            logs = [_log_sigmoids(z2[:BLOCK]), _log_sigmoids(z2[BLOCK:])]
            cums = _cumsum_mxu_many([jnp.where(mask, lm, 0.0) if diagonal else lm for _, lm in logs], rv)
            new_c = []
            for h in range(2):
                lb, (after, rs) = logs[h][0], cums[h]
                if diagonal:
                    a = jnp.where(mask, jnp.exp(lb + after), 0.0)
                    new_c.append(rs)
                else:
                    a = jnp.exp(lb + after + carries[h])
                    new_c.append(carries[h] + rs)
                sig_s[h, j] = jnp.exp(lb)
                a_s[h, j] = a
                e_s[h, j] = da2[h * BLOCK:(h + 1) * BLOCK] * a
            return new_c

        carries = sweep1_tile(i, None, True)

        def live(c0, c1):
            return jnp.max(jnp.maximum(c0, c1)) > SWEEP_EXIT

        def cond(st):
            return (st[0] < i) & st[1]

        def sweep1(st):
            jj, _, c0, c1 = st
            new_c = sweep1_tile(i - 1 - jj, (c0, c1), False)
            return jj + 1, live(*new_c), new_c[0], new_c[1]

        visited = lax.while_loop(cond, sweep1, (jnp.int32(0), live(*carries), carries[0], carries[1]))[0]

        def grads(j, st, diagonal):
            p0, p1, dq = st
            kj = _tile(k_ref, j)
            es = [e_s[0, j], e_s[1, j]]
            cums = _cumsum_mxu_many(es, fw)
            dzs, new_p = [], []
            for h, prefix in enumerate((p0, p1)):
                sg = sig_s[h, j]
                e_before, rs = cums[h]
                dz = (es[h] * (1.0 - sg) - (e_before + prefix) * sg) * ATTN_SCALE
                if diagonal:
                    dz = jnp.where(mask, dz, 0.0)
                dzs.append(dz.astype(BF16))
                new_p.append(prefix + rs)
            dz_stack = jnp.concatenate(dzs, axis=0)
            a_stack = jnp.concatenate([a_s[0, j].astype(BF16), a_s[1, j].astype(BF16)], axis=0)
            dq2 = jnp.dot(dz_stack, kj, preferred_element_type=F32)
            rows = pl.ds(pl.multiple_of(j * BLOCK, BLOCK), BLOCK)
            dk_ref[rows, :] += lax.dot_general(dz_stack, q_stack, _TN, preferred_element_type=F32)
            dv_ref[rows, :] += lax.dot_general(a_stack, do_stack, _TN, preferred_element_type=F32)
            return new_p[0], new_p[1], dq + jnp.where(lo, dq2[:BLOCK], dq2[BLOCK:])

        zeros = jnp.zeros((BLOCK, BLOCK), F32)
        st = lax.fori_loop(i - visited, i, lambda j, st: grads(j, st, False), (zeros, zeros, zeros))
        dq_ref[...] = grads(i, st, True)[2]

    tile_stash = pltpu.VMEM((2, nj, BLOCK, BLOCK), F32)
    return pl.pallas_call(
        body, name="attn_b_bwd", grid=(NB, n_pair, nj),
        in_specs=[blk, blk, k_full, v_full, mat, mat], out_specs=[blk, acc_full, acc_full],
        out_shape=[jax.ShapeDtypeStruct((NB, S, HD), F32)] * 3,
        scratch_shapes=[tile_stash, tile_stash, tile_stash],
        compiler_params=_params(("parallel", "parallel", "arbitrary")),
    )(qkv, do, qkv, qkv, rev, fwd)


def _ada_fwd(c_all, w, b):
    L, D, N = w.shape
    B = c_all.shape[0]

    def body(c_ref, w_ref, b_ref, o_ref):
        cv = c_ref[...]
        cond = (cv * _sigmoid(cv)).astype(BF16)
        o_ref[...] = jnp.dot(cond, w_ref[...].astype(BF16), preferred_element_type=F32) + b_ref[...]

    return pl.pallas_call(
        body, name="ada_fwd", grid=(L,),
        in_specs=[pl.BlockSpec((B, D), lambda l: (0, 0)), pl.BlockSpec((None, D, N), lambda l: (l, 0, 0)),
                  pl.BlockSpec((None, 1, N), lambda l: (l, 0, 0))],
        out_specs=pl.BlockSpec((None, B, N), lambda l: (l, 0, 0)),
        out_shape=jax.ShapeDtypeStruct((L, B, N), F32),
        compiler_params=_params(("parallel",)),
    )(c_all, w, b)


def _ada_bwd(c_all, dmod_all, dmod_shard):
    L, B, N = dmod_shard.shape
    D = c_all.shape[1]
    N_all = dmod_all.shape[2]

    def body(c_ref, da_ref, ds_ref, gw_ref, gb_ref):
        cv = c_ref[...]
        cond = (cv * _sigmoid(cv)).astype(BF16)
        gw_ref[...] = lax.dot_general(cond, ds_ref[...].astype(BF16), _TN, preferred_element_type=F32)
        gb_ref[...] = jnp.sum(da_ref[...], axis=0, keepdims=True)

    return pl.pallas_call(
        body, name="ada_bwd", grid=(L,),
        in_specs=[pl.BlockSpec((B, D), lambda l: (0, 0)), pl.BlockSpec((None, B, N_all), lambda l: (l, 0, 0)),
                  pl.BlockSpec((None, B, N), lambda l: (l, 0, 0))],
        out_specs=[pl.BlockSpec((None, D, N), lambda l: (l, 0, 0)), pl.BlockSpec((None, 1, N_all), lambda l: (l, 0, 0))],
        out_shape=[jax.ShapeDtypeStruct((L, D, N), F32), jax.ShapeDtypeStruct((L, 1, N_all), F32)],
        compiler_params=_params(("parallel",)),
    )(c_all, dmod_all, dmod_shard)


def _adamw(w, g, m, v, name):
    shape = w.shape
    C = shape[-1]
    R = w.size // C
    tr = _pick(R, max(8, (1 << 18) // C), 8)
    c1 = 1.0 - ADAM_B1 ** ADAM_STEP
    c2 = 1.0 - ADAM_B2 ** ADAM_STEP

    def body(w_ref, g_ref, m_ref, v_ref, d_ref, nm_ref, nv_ref):
        gv = g_ref[...]
        nm = ADAM_B1 * m_ref[...] + (1.0 - ADAM_B1) * gv
        nv = ADAM_B2 * v_ref[...] + (1.0 - ADAM_B2) * (gv * gv)
        d_ref[...] = -ADAM_LR * ((nm / c1) / (jnp.sqrt(nv / c2) + ADAM_EPS) + ADAM_WD * w_ref[...])
        nm_ref[...] = nm
        nv_ref[...] = nv

    spec = pl.BlockSpec((tr, C), lambda r: (r, 0))
    out = pl.pallas_call(
        body, name=name, grid=(R // tr,), in_specs=[spec] * 4, out_specs=[spec] * 3,
        out_shape=[jax.ShapeDtypeStruct((R, C), F32)] * 3,
        compiler_params=_params(("parallel",)),
    )(*[t.reshape(R, C) for t in (w, g, m, v)])
    return [t.reshape(shape) for t in out]


_SHARDED = (("wqkv_a", 2), ("wo_a", 1), ("wqkv_b", 2), ("wo_b", 1), ("w_gate", 2), ("w_up", 2), ("w_down", 1))


def _pack_full(layers, axis):
    L = len(layers)
    R, C = layers[0].shape

    def shards(m):
        if axis == 2:
            return m.reshape(R, 4, C // 4).transpose(1, 0, 2)
        return m.reshape(4, R // 4, C)

    halves = [jnp.stack([shards(m) for m in layers[h * (L // 2):(h + 1) * (L // 2)]], axis=1) for h in range(2)]
    return jnp.stack(halves)


def _unpack_full(gathered, axis):
    _, Lh, Rs, Cs = gathered.shape
    t = gathered.reshape(4, 2, Lh, Rs, Cs)
    layers = []
    for h in range(2):
        for l in range(Lh):
            piece = t[:, h, l]
            if axis == 2:
                layers.append(piece.transpose(1, 0, 2).reshape(Rs, 4 * Cs))
            else:
                layers.append(piece.reshape(4 * Rs, Cs))
    return layers


def _sum_slabs(own, recv, name, with_bf16=False):
    C = own.shape[-1]
    out = _sum_leading(recv.reshape(recv.shape[0], -1, C), name, own=own.reshape(-1, C), with_bf16=with_bf16)
    if with_bf16:
        return out[0].reshape(own.shape), out[1].reshape(own.shape)
    return out.reshape(own.shape)


def _gather8(x, name):
    return _all_gather8([x], name)[0]


def _rope_tables(positions):
    half = ROT_DIM // 2
    inv_freq = jnp.power(jnp.float32(ROPE_THETA), -jnp.arange(half, dtype=F32) * 2.0 / ROT_DIM)
    ang = positions.astype(F32).reshape(-1, 1) * inv_freq
    cos, sin = jnp.cos(ang), jnp.sin(ang)
    T = ang.shape[0]
    rest = HEAD_DIM - ROT_DIM
    c64 = jnp.concatenate([cos, cos, jnp.ones((T, rest), F32)], axis=1)
    s64 = jnp.concatenate([-sin, sin, jnp.zeros((T, rest), F32)], axis=1)
    return jnp.tile(c64, (1, 2)), jnp.tile(s64, (1, 2))


def _gain_rows(q_gain, k_gain):
    q2 = jnp.tile(q_gain.reshape(1, HEAD_DIM), (GROUP_A, 2))
    k2 = jnp.tile(k_gain.reshape(1, HEAD_DIM), (1, 2))
    return jnp.concatenate([q2, k2, jnp.ones((1, LANES), F32)], axis=0)


def _local_step(x, positions, mod, norm1_g, norm2_g, q_norm_a, k_norm_a, sinks_a,
                wqkv_a, wo_a, wqkv_b, wo_b, wgu, wd, loss_target):
    NB, S, D = x.shape
    T = NB * S
    QA = N_Q_A * HEAD_DIM
    tab_c, tab_s = _rope_tables(positions)
    rev, fwd = _cumsum_mats()

    saved = []
    xc = x
    for i in range(DEPTH):
        j = i // 2
        sh1, sc1, g1, sh2, sc2, g2 = [mod[i][:, k * D:(k + 1) * D].reshape(NB, 1, D) for k in range(6)]
        st = dict(x=xc, sc1=sc1, g1=g1, sc2=sc2, g2=g2)
        h = _norm_mod_fwd(xc, norm1_g[i:i + 1], sc1, sh1)
        st["h"] = h.reshape(T, D)
        if i % 2 == 0:
            st["qkv"] = _matmul(st["h"], wqkv_a[j], "nn", F32, "qkv_a")
            st["gains"] = _gain_rows(q_norm_a[j], k_norm_a[j])
            st["qkn"] = _qk_prep_fwd(st["qkv"], tab_c, tab_s, st["gains"]).reshape(NB, S, -1)
            st["o"] = _attn_a_fwd(st["qkn"], sinks_a[j]).reshape(T, QA)
            y = _matmul(st["o"], wo_a[j], "nn", F32, "wo_a")
        else:
            st["qkv"] = _matmul(st["h"], wqkv_b[j], "nn", BF16, "qkv_b").reshape(NB, S, -1)
            st["o"] = _attn_b_fwd(st["qkv"], rev).reshape(T, N_H_B * HEAD_DIM)
            y = _matmul(st["o"], wo_b[j], "nn", F32, "wo_b")
        st["y"] = y.reshape(NB, S, D)
        x1 = _gate_res(xc, st["y"], g1)
        st["x1"] = x1
        h2 = _norm_mod_fwd(x1, norm2_g[i:i + 1], sc2, sh2)
        st["h2"] = h2.reshape(T, D)
        st["gu"] = _matmul(st["h2"], wgu[i], "nn", F32, "gate_up")
        st["act"] = _swiglu_fwd(st["gu"])
        st["m"] = _matmul(st["act"], wd[i], "nn", F32, "down").reshape(NB, S, D)
        xc = _gate_res(x1, st["m"], g2)
        saved.append(st)

    loss, dx = _loss_fwd_bwd(xc, loss_target)

    grads = {name: [None] * n for name, n in
             (("wqkv_a", 2), ("wo_a", 2), ("wqkv_b", 2), ("wo_b", 2), ("wgu", DEPTH), ("wd", DEPTH),
              ("norm1_g", DEPTH), ("norm2_g", DEPTH), ("q_norm_a", 2), ("k_norm_a", 2), ("sinks_a", 2))}
    dmod = [None] * DEPTH
    for i in reversed(range(DEPTH)):
        j = i // 2
        st = saved[i]
        dm, dg2 = _gate_res_bwd(dx, st["m"], st["g2"])
        dm = dm.reshape(T, D)
        dact = _matmul(dm, wd[i], "nt", F32, "d_act")
        grads["wd"][i] = _matmul(st["act"], dm, "tn", F32, "d_wd")
        dgu = _swiglu_bwd(st["gu"], dact)
        dh2 = _matmul(dgu, wgu[i], "nt", F32, "d_h2")
        grads["wgu"][i] = _matmul(st["h2"], dgu, "tn", F32, "d_wgu")
        dx1, dsh2, dsc2, grads["norm2_g"][i] = _norm_mod_bwd(
            st["x1"], norm2_g[i:i + 1], st["sc2"], dh2.reshape(NB, S, D), dx)
        dy, dg1 = _gate_res_bwd(dx1, st["y"], st["g1"])
        dy = dy.reshape(T, D)
        if i % 2 == 0:
            do = _matmul(dy, wo_a[j], "nt", BF16, "d_o_a").reshape(NB, S, QA)
            grads["wo_a"][j] = _matmul(st["o"], dy, "tn", F32, "d_wo_a")
            dq, dk, dv, dsink = _attn_a_bwd(st["qkn"], do, sinks_a[j])
            dqkv, dgain = _qk_prep_bwd(st["qkv"], dq.reshape(T, QA), dk.reshape(T, LANES), dv.reshape(T, LANES),
                                       tab_c, tab_s, st["gains"])
            dh = _matmul(dqkv, wqkv_a[j], "nt", F32, "d_h_a")
            grads["wqkv_a"][j] = _matmul(st["h"], dqkv, "tn", F32, "d_wqkv_a")
            grads["q_norm_a"][j] = jnp.sum(dgain[:GROUP_A].reshape(2 * GROUP_A, HEAD_DIM), axis=0)
            grads["k_norm_a"][j] = jnp.sum(dgain[GROUP_A].reshape(2, HEAD_DIM), axis=0)
            grads["sinks_a"][j] = jnp.sum(dsink[..., 0], axis=0)
        else:
            do = _matmul(dy, wo_b[j], "nt", BF16, "d_o_b").reshape(NB, S, -1)
            grads["wo_b"][j] = _matmul(st["o"], dy, "tn", F32, "d_wo_b")
            dq, dk, dv = _attn_b_bwd(st["qkv"], do, rev, fwd)
            dqkv = jnp.concatenate([dq, dk, dv], axis=-1).reshape(T, -1).astype(BF16)
            dh = _matmul(dqkv, wqkv_b[j], "nt", F32, "d_h_b")
            grads["wqkv_b"][j] = _matmul(st["h"], dqkv, "tn", F32, "d_wqkv_b")
        dx, dsh1, dsc1, grads["norm1_g"][i] = _norm_mod_bwd(
            st["x"], norm1_g[i:i + 1], st["sc1"], dh.reshape(NB, S, D), dx1)
        dmod[i] = jnp.concatenate([dsh1, dsc1, dg1, dsh2, dsc2, dg2], axis=-1).reshape(NB, 6 * D)

    matrices = ("wqkv_a", "wo_a", "wqkv_b", "wo_b", "wgu", "wd")
    grads = {name: parts if name in matrices else jnp.stack(parts) for name, parts in grads.items()}
    return loss, dx, grads, jnp.stack(dmod)


def _rows_of(flat, cols=PACK_COLS):
    n = flat.shape[0]
    pad = (-n) % (8 * cols)
    if pad:
        flat = jnp.concatenate([flat, jnp.zeros((pad,), flat.dtype)])
    return flat.reshape(-1, cols)


def kernel(x, c, positions, ada_w, ada_b, norm1_g, norm2_g, wqkv_a, q_norm_a, k_norm_a, sinks_a, wo_a, wqkv_b, wo_b, w_gate, w_up, w_down, loss_target, m_ada_w, m_ada_b, m_norm1_g, m_norm2_g, m_wqkv_a, m_q_norm_a, m_k_norm_a, m_sinks_a, m_wo_a, m_wqkv_b, m_wo_b, m_w_gate, m_w_up, m_w_down, v_ada_w, v_ada_b, v_norm1_g, v_norm2_g, v_wqkv_a, v_q_norm_a, v_k_norm_a, v_sinks_a, v_wo_a, v_wqkv_b, v_wo_b, v_w_gate, v_w_up, v_w_down):
    xi, yi, ci = lax.axis_index("x"), lax.axis_index("y"), lax.axis_index("c")
    dev = 4 * xi + 2 * yi + ci
    chip = 2 * xi + yi
    NB, S, D = x.shape
    B_all = N_DEV * NB
    L = ada_w.shape[0]
    n_mod = ada_w.shape[2] // 2

    c_all = _gather8(_rows_of(c.reshape(-1), LANES), "gather_c").reshape(N_DEV, -1)[:, :NB * D].reshape(B_all, D)
    ada_w_half = lax.dynamic_slice_in_dim(ada_w, ci * n_mod, n_mod, axis=2)
    ada_b_half = lax.dynamic_slice_in_dim(ada_b, dev * n_mod, n_mod, axis=1).reshape(L, 1, n_mod)
    mod_part = _ada_fwd(c_all, ada_w_half, ada_b_half)
    n_part = L * B_all * n_mod
    mod_all = _gather8(_rows_of(mod_part.reshape(-1)), "gather_mod").reshape(N_DEV, -1)[:, :n_part]
    mod_all = mod_all.reshape(N_DEV, L, B_all, n_mod).transpose(1, 2, 0, 3).reshape(L, B_all, N_DEV * n_mod)
    mod = lax.dynamic_slice_in_dim(mod_all, dev * NB, NB, axis=1)

    shards = dict(wqkv_a=wqkv_a, wo_a=wo_a, wqkv_b=wqkv_b, wo_b=wo_b, w_gate=w_gate, w_up=w_up, w_down=w_down)
    halves = []
    for name, _ in _SHARDED:
        w = shards[name]
        half = lax.dynamic_index_in_dim(w.reshape((2, w.shape[0] // 2) + w.shape[1:]), ci, 0, keepdims=False)
        halves.append(half.astype(BF16))
    gathered = _all_gather8(halves, "gather_weights", local_axis=1, local_chunks=8)
    full = {name: _unpack_full(t, axis) for (name, axis), t in zip(_SHARDED, gathered)}
    wgu = [_interleave(gate, up) for gate, up in zip(full["w_gate"], full["w_up"])]

    loss, grad_x, g, dmod = _local_step(
        x, positions, mod, norm1_g, norm2_g, q_norm_a, k_norm_a, sinks_a,
        full["wqkv_a"], full["wo_a"], full["wqkv_b"], full["wo_b"], wgu, full["w_down"], loss_target)

    gate_up = [_deinterleave(t) for t in g["wgu"]]
    g_full = dict(wqkv_a=g["wqkv_a"], wo_a=g["wo_a"], wqkv_b=g["wqkv_b"], wo_b=g["wo_b"],
                  w_gate=[t[0] for t in gate_up], w_up=[t[1] for t in gate_up], w_down=g["wd"])
    packed = [_pack_full(g_full[name], axis) for name, axis in _SHARDED]
    def own(t, index):
        return lax.dynamic_index_in_dim(t, index, 0, keepdims=False)

    from_cores = _exchange(packed, "c", "rs_cores", chunk_axis=0, chunks=4)
    chip_part = [_sum_slabs(own(p, ci), r, "rs_add_cores", with_bf16=True) for p, r in zip(packed, from_cores)]
    from_chips = _exchange([b for _, b in chip_part], "xy", "rs_chips")
    mine = [_sum_slabs(own(p, chip), r, "rs_add_chips") for (p, _), r in zip(chip_part, from_chips)]
    theirs = _sibling_send(mine, "rs_halves")
    grad = {}
    for (name, _), m, t in zip(_SHARDED, mine, theirs):
        first, second = jnp.where(ci == 0, m, t), jnp.where(ci == 0, t, m)
        grad[name] = jnp.stack([first, second]).reshape(shards[name].shape)

    small_names = ("norm1_g", "norm2_g", "q_norm_a", "k_norm_a", "sinks_a")
    small = [dmod.reshape(-1)] + [g[name].reshape(-1) for name in small_names] + [loss.reshape(-1)]
    small_sizes = [t.shape[0] for t in small]
    small_rows = _rows_of(jnp.concatenate(small))
    small_all = _gather8(small_rows, "gather_small")
    small_sum = _sum_leading(small_all, "sum_small").reshape(-1)
    n_dmod = small_sizes[0]
    dmod_all = small_all.reshape(N_DEV, -1)[:, :n_dmod].reshape(N_DEV, L, NB, 6 * D)
    dmod_all = dmod_all.transpose(1, 0, 2, 3).reshape(L, B_all, 6 * D)
    off = n_dmod
    for name, sz in zip(small_names + ("loss",), small_sizes[1:]):
        grad[name] = small_sum[off:off + sz]
        off += sz
    loss_total = grad.pop("loss").reshape(())
    for name, ref in (("norm1_g", norm1_g), ("norm2_g", norm2_g), ("q_norm_a", q_norm_a),
                      ("k_norm_a", k_norm_a), ("sinks_a", sinks_a)):
        grad[name] = grad[name].reshape(ref.shape)

    n_shard = ada_w.shape[2]
    dmod_shard = lax.dynamic_slice_in_dim(dmod_all, chip * n_shard, n_shard, axis=2)
    grad["ada_w"], gb = _ada_bwd(c_all, dmod_all, dmod_shard)
    grad["ada_b"] = gb.reshape(ada_b.shape)

    weights = dict(ada_w=ada_w, ada_b=ada_b, norm1_g=norm1_g, norm2_g=norm2_g, wqkv_a=wqkv_a, q_norm_a=q_norm_a,
                   k_norm_a=k_norm_a, sinks_a=sinks_a, wo_a=wo_a, wqkv_b=wqkv_b, wo_b=wo_b, w_gate=w_gate,
                   w_up=w_up, w_down=w_down)
    m_in = dict(ada_w=m_ada_w, ada_b=m_ada_b, norm1_g=m_norm1_g, norm2_g=m_norm2_g, wqkv_a=m_wqkv_a,
                q_norm_a=m_q_norm_a, k_norm_a=m_k_norm_a, sinks_a=m_sinks_a, wo_a=m_wo_a, wqkv_b=m_wqkv_b,
                wo_b=m_wo_b, w_gate=m_w_gate, w_up=m_w_up, w_down=m_w_down)
    v_in = dict(ada_w=v_ada_w, ada_b=v_ada_b, norm1_g=v_norm1_g, norm2_g=v_norm2_g, wqkv_a=v_wqkv_a,
                q_norm_a=v_q_norm_a, k_norm_a=v_k_norm_a, sinks_a=v_sinks_a, wo_a=v_wo_a, wqkv_b=v_wqkv_b,
                wo_b=v_wo_b, w_gate=v_w_gate, w_up=v_w_up, w_down=v_w_down)
    names = list(weights)
    delta, new_m, new_v = {}, {}, {}
    for name in names:
        delta[name], new_m[name], new_v[name] = _adamw(weights[name], grad[name], m_in[name], v_in[name],
                                                       "adamw_" + name)
    return (loss_total, grad_x, *[grad[k] for k in names], *[delta[k] for k in names],
            *[new_m[k] for k in names], *[new_v[k] for k in names])
```

```python
import jax
import jax.numpy as jnp
from jax import lax
from jax.experimental import pallas as pl
from jax.experimental.pallas import tpu as pltpu

F32 = jnp.float32
BF16 = jnp.bfloat16

DEPTH = 4
HEAD_DIM = 64
N_Q_A = 16
N_KV_A = 2
GROUP_A = N_Q_A // N_KV_A
N_H_B = 16
BLOCK = 128
ROT_DIM = HEAD_DIM // 4
ROPE_THETA = 500000.0
EPS = 1e-6
ATTN_SCALE = HEAD_DIM ** -0.5
NEG_BIG = -1e30

ADAM_LR = 0.001
ADAM_B1 = 0.9
ADAM_B2 = 0.999
ADAM_EPS = 1e-08
ADAM_WD = 0.01
ADAM_STEP = 10

N_DEV = 8
LANES = 128
PACK_COLS = 1024
VMEM_LIMIT_BYTES = 48 * 1024 * 1024
MESH = pl.DeviceIdType.MESH

_NT = (((1,), (1,)), ((), ()))
_TN = (((0,), (0,)), ((), ()))
_NN = (((1,), (0,)), ((), ()))


def _params(sem=None):
    return pltpu.CompilerParams(vmem_limit_bytes=VMEM_LIMIT_BYTES, dimension_semantics=sem)


def _pick(n, cap, mult):
    best = None
    for t in range(mult, min(n, cap) + 1, mult):
        if n % t == 0:
            best = t
    return n if best is None else best


_ANY = pl.BlockSpec(memory_space=pl.ANY)


def _window(index, axis, q, n, shape):
    rest = [slice(None)] * len(shape)
    size = shape[axis] // n
    rest[axis] = pl.ds(q * size, size)
    return tuple(index) + tuple(rest)


def _all_gather8(xs, name, local_axis=0, local_chunks=1):
    n = len(xs)

    def body(*refs):
        x_refs, out_refs = refs[:n], refs[n:2 * n]
        send_sems, recv_sems, local_sems = refs[2 * n:]
        xi, yi, ci = lax.axis_index("x"), lax.axis_index("y"), lax.axis_index("c")
        me, sibling = (xi, yi, ci), (xi, yi, 1 - ci)
        chips = [(1 - xi, yi), (xi, 1 - yi), (1 - xi, 1 - yi)]

        def slab(w, px, py, pc):
            return out_refs[w].at[4 * px + 2 * py + pc]

        def copy(w, k, block, to, src=None):
            return pltpu.make_async_remote_copy(
                src_ref=slab(w, *block) if src is None else src, dst_ref=slab(w, *block),
                send_sem=send_sems.at[k, w], recv_sem=recv_sems.at[k, w], device_id=to, device_id_type=MESH)

        mine = []
        for w in range(n):
            for q in range(local_chunks):
                part = _window((), local_axis, q, local_chunks, xs[w].shape)
                mine.append(pltpu.make_async_copy(x_refs[w].at[part], slab(w, *me).at[part], local_sems.at[w, q]))
                mine[-1].start()
        first = [copy(w, 0, me, sibling, src=x_refs[w]) for w in range(n)]
        first += [copy(w, 1 + j, me, (*chip, ci), src=x_refs[w]) for j, chip in enumerate(chips) for w in range(n)]
        for cp in first:
            cp.start()
        passed = []
        for j, chip in enumerate(chips):
            for w in range(n):
                copy(w, 1 + j, (*chip, ci), me).wait_recv()
                passed.append(copy(w, 4 + j, (*chip, ci), sibling))
                passed[-1].start()
        for w in range(n):
            copy(w, 0, sibling, me).wait_recv()
        for j, chip in enumerate(chips):
            for w in range(n):
                copy(w, 4 + j, (*chip, 1 - ci), me).wait_recv()
        for cp in first + passed:
            cp.wait_send()
        for cp in mine:
            cp.wait()

    return pl.pallas_call(
        body, name=name,
        out_shape=[jax.ShapeDtypeStruct((N_DEV,) + x.shape, x.dtype) for x in xs],
        in_specs=[_ANY] * n, out_specs=[_ANY] * n,
        scratch_shapes=[pltpu.SemaphoreType.DMA((7, n)), pltpu.SemaphoreType.DMA((7, n)),
                        pltpu.SemaphoreType.DMA((n, local_chunks))],
    )(*xs)


def _exchange(xs, group, name, chunk_axis=0, chunks=1):
    n = len(xs)
    n_peers = 1 if group == "c" else 3

    def body(*refs):
        x_refs, out_refs = refs[:n], refs[n:2 * n]
        send_sems, recv_sems = refs[2 * n:]
        xi, yi, ci = lax.axis_index("x"), lax.axis_index("y"), lax.axis_index("c")
        if group == "c":
            peers = [(1 - ci, (xi, yi, 1 - ci))]
        else:
            peers = [(2 * (1 - xi) + yi, (1 - xi, yi, ci)),
                     (2 * xi + (1 - yi), (xi, 1 - yi, ci)),
                     (2 * (1 - xi) + (1 - yi), (1 - xi, 1 - yi, ci))]
        copies = []
        for k, (p, dev) in enumerate(peers):
            for w in range(n):
                slab_shape = xs[w].shape[1:]
                for q in range(chunks):
                    copies.append(pltpu.make_async_remote_copy(
                        src_ref=x_refs[w].at[_window((p,), chunk_axis, q, chunks, slab_shape)],
                        dst_ref=out_refs[w].at[_window((k,), chunk_axis, q, chunks, slab_shape)],
                        send_sem=send_sems.at[k, w, q], recv_sem=recv_sems.at[k, w, q],
                        device_id=dev, device_id_type=MESH))
                    copies[-1].start()
        for cp in copies:
            cp.wait()

    return pl.pallas_call(
        body, name=name,
        out_shape=[jax.ShapeDtypeStruct((n_peers,) + x.shape[1:], x.dtype) for x in xs],
        in_specs=[_ANY] * n, out_specs=[_ANY] * n,
        scratch_shapes=[pltpu.SemaphoreType.DMA((n_peers, n, chunks)), pltpu.SemaphoreType.DMA((n_peers, n, chunks))],
    )(*xs)


def _sibling_send(xs, name, chunk_axis=1, chunks=4):
    n = len(xs)

    def body(*refs):
        x_refs, out_refs = refs[:n], refs[n:2 * n]
        send_sems, recv_sems = refs[2 * n:]
        xi, yi, ci = lax.axis_index("x"), lax.axis_index("y"), lax.axis_index("c")
        copies = []
        for w in range(n):
            for q in range(chunks):
                part = _window((), chunk_axis, q, chunks, xs[w].shape)
                copies.append(pltpu.make_async_remote_copy(
                    src_ref=x_refs[w].at[part], dst_ref=out_refs[w].at[part],
                    send_sem=send_sems.at[w, q], recv_sem=recv_sems.at[w, q],
                    device_id=(xi, yi, 1 - ci), device_id_type=MESH))
                copies[-1].start()
        for cp in copies:
            cp.wait()

    return pl.pallas_call(
        body, name=name,
        out_shape=[jax.ShapeDtypeStruct(x.shape, x.dtype) for x in xs],
        in_specs=[_ANY] * n, out_specs=[_ANY] * n,
        scratch_shapes=[pltpu.SemaphoreType.DMA((n, chunks)), pltpu.SemaphoreType.DMA((n, chunks))],
    )(*xs)


def _sum_leading(x, name, own=None, with_bf16=False):
    P, R, C = x.shape
    tr = _pick(R, max(16, (1 << 19) // (C * (P + 1))), 16)

    def body(*refs):
        n_in = 1 if own is None else 2
        x_ref = refs[n_in - 1]
        acc = x_ref[0].astype(F32) if own is None else refs[0][...] + x_ref[0].astype(F32)
        for p in range(1, P):
            acc = acc + x_ref[p].astype(F32)
        refs[n_in][...] = acc
        if with_bf16:
            refs[n_in + 1][...] = acc.astype(BF16)

    flat = pl.BlockSpec((tr, C), lambda r: (r, 0))
    slabs = pl.BlockSpec((P, tr, C), lambda r: (0, r, 0))
    out = pl.pallas_call(
        body, name=name, grid=(R // tr,),
        in_specs=[slabs] if own is None else [flat, slabs],
        out_specs=[flat, flat] if with_bf16 else [flat],
        out_shape=[jax.ShapeDtypeStruct((R, C), F32)] + ([jax.ShapeDtypeStruct((R, C), BF16)] if with_bf16 else []),
        compiler_params=_params(("arbitrary",)),
    )(*([x] if own is None else [own, x]))
    return out if with_bf16 else out[0]


def _matmul(a, b, mode, out_dtype, name):
    if mode == "nn":
        (M, K), N = a.shape, b.shape[1]
    elif mode == "nt":
        (M, K), N = a.shape, b.shape[0]
    else:
        (K, M), N = a.shape, b.shape[1]
    tm = _pick(M, 1024 if mode != "tn" else 1536, 128)
    tn = _pick(N, 1536, 128)
    tk = _pick(K, 512, 128)
    nk = K // tk
    dims = {"nn": _NN, "nt": _NT, "tn": _TN}[mode]

    def body(a_ref, b_ref, o_ref, acc_ref):
        k = pl.program_id(2)

        @pl.when(k == 0)
        def _():
            acc_ref[...] = jnp.zeros_like(acc_ref)

        acc_ref[...] += lax.dot_general(a_ref[...].astype(BF16), b_ref[...].astype(BF16), dims,
                                        preferred_element_type=F32)

        @pl.when(k == nk - 1)
        def _():
            o_ref[...] = acc_ref[...].astype(o_ref.dtype)

    if mode == "tn":
        a_spec = pl.BlockSpec((tk, tm), lambda i, j, k: (k, i))
    else:
        a_spec = pl.BlockSpec((tm, tk), lambda i, j, k: (i, k))
    if mode == "nt":
        b_spec = pl.BlockSpec((tn, tk), lambda i, j, k: (j, k))
    else:
        b_spec = pl.BlockSpec((tk, tn), lambda i, j, k: (k, j))
    return pl.pallas_call(
        body, name=name, grid=(M // tm, N // tn, nk),
        in_specs=[a_spec, b_spec],
        out_specs=pl.BlockSpec((tm, tn), lambda i, j, k: (i, j)),
        out_shape=jax.ShapeDtypeStruct((M, N), out_dtype),
        scratch_shapes=[pltpu.VMEM((tm, tn), F32)],
        compiler_params=_params(("parallel", "parallel", "arbitrary")),
    )(a, b)


def _row_tile(S):
    return _pick(S, 512, 8)


def _norm_mod_fwd(x, gain, sc, sh):
    NB, S, D = x.shape
    tr = _row_tile(S)

    def body(x_ref, g_ref, sc_ref, sh_ref, h_ref):
        xv = x_ref[...]
        ms = jnp.mean(xv * xv, axis=-1, keepdims=True)
        n = xv * lax.rsqrt(ms + EPS) * g_ref[...]
        h_ref[...] = (n * (1.0 + sc_ref[...]) + sh_ref[...]).astype(BF16)

    tok = pl.BlockSpec((None, tr, D), lambda b, r: (b, r, 0))
    per_ex = pl.BlockSpec((None, 1, D), lambda b, r: (b, 0, 0))
    return pl.pallas_call(
        body, name="norm_mod_fwd", grid=(NB, S // tr),
        in_specs=[tok, pl.BlockSpec((1, D), lambda b, r: (0, 0)), per_ex, per_ex],
        out_specs=tok, out_shape=jax.ShapeDtypeStruct((NB, S, D), BF16),
        compiler_params=_params(("parallel", "parallel")),
    )(x, gain, sc, sh)


def _norm_mod_bwd(x, gain, sc, dh, dres):
    NB, S, D = x.shape
    tr = _row_tile(S)

    def body(x_ref, g_ref, sc_ref, dh_ref, dres_ref, dx_ref, dsh_ref, dsc_ref, dg_ref):
        b, r = pl.program_id(0), pl.program_id(1)

        @pl.when(r == 0)
        def _():
            dsh_ref[...] = jnp.zeros_like(dsh_ref)
            dsc_ref[...] = jnp.zeros_like(dsc_ref)

        @pl.when((r == 0) & (b == 0))
        def _():
            dg_ref[...] = jnp.zeros_like(dg_ref)

        xv = x_ref[...]
        rstd = lax.rsqrt(jnp.mean(xv * xv, axis=-1, keepdims=True) + EPS)
        xh = xv * rstd
        g = g_ref[...]
        dh = dh_ref[...]
        dsh_ref[...] += jnp.sum(dh, axis=0, keepdims=True)
        dsc_ref[...] += jnp.sum(dh * (xh * g), axis=0, keepdims=True)
        dn = dh * (1.0 + sc_ref[...])
        dg_ref[...] += jnp.sum(dn * xh, axis=0, keepdims=True)
        dxh = dn * g
        proj = jnp.mean(dxh * xh, axis=-1, keepdims=True)
        dx_ref[...] = rstd * (dxh - xh * proj) + dres_ref[...]

    tok = pl.BlockSpec((None, tr, D), lambda b, r: (b, r, 0))
    per_ex = pl.BlockSpec((None, 1, D), lambda b, r: (b, 0, 0))
    row = pl.BlockSpec((1, D), lambda b, r: (0, 0))
    return pl.pallas_call(
        body, name="norm_mod_bwd", grid=(NB, S // tr),
        in_specs=[tok, row, per_ex, tok, tok],
        out_specs=[tok, per_ex, per_ex, row],
        out_shape=[jax.ShapeDtypeStruct((NB, S, D), F32), jax.ShapeDtypeStruct((NB, 1, D), F32),
                   jax.ShapeDtypeStruct((NB, 1, D), F32), jax.ShapeDtypeStruct((1, D), F32)],
        compiler_params=_params(("arbitrary", "arbitrary")),
    )(x, gain, sc, dh, dres)


def _gate_res(x, y, g):
    NB, S, D = x.shape
    tr = _row_tile(S)

    def body(x_ref, y_ref, g_ref, o_ref):
        o_ref[...] = x_ref[...] + g_ref[...] * y_ref[...]

    tok = pl.BlockSpec((None, tr, D), lambda b, r: (b, r, 0))
    per_ex = pl.BlockSpec((None, 1, D), lambda b, r: (b, 0, 0))
    return pl.pallas_call(
        body, name="gate_res", grid=(NB, S // tr), in_specs=[tok, tok, per_ex], out_specs=tok,
        out_shape=jax.ShapeDtypeStruct((NB, S, D), F32),
        compiler_params=_params(("parallel", "parallel")),
    )(x, y, g)


def _gate_res_bwd(dxo, y, g):
    NB, S, D = dxo.shape
    tr = _row_tile(S)

    def body(d_ref, y_ref, g_ref, dy_ref, dg_ref):
        @pl.when(pl.program_id(1) == 0)
        def _():
            dg_ref[...] = jnp.zeros_like(dg_ref)

        d = d_ref[...]
        dy_ref[...] = (d * g_ref[...]).astype(BF16)
        dg_ref[...] += jnp.sum(d * y_ref[...], axis=0, keepdims=True)

    tok = pl.BlockSpec((None, tr, D), lambda b, r: (b, r, 0))
    per_ex = pl.BlockSpec((None, 1, D), lambda b, r: (b, 0, 0))
    return pl.pallas_call(
        body, name="gate_res_bwd", grid=(NB, S // tr), in_specs=[tok, tok, per_ex], out_specs=[tok, per_ex],
        out_shape=[jax.ShapeDtypeStruct((NB, S, D), BF16), jax.ShapeDtypeStruct((NB, 1, D), F32)],
        compiler_params=_params(("arbitrary", "arbitrary")),
    )(dxo, y, g)


def _sigmoid(v):
    return 1.0 / (1.0 + jnp.exp(-v))


def _ff_tile(F):
    return _pick(F, 1536, 128)


def _interleave(gate, up):
    F = gate.shape[-1]
    tf = _ff_tile(F)
    parts = []
    for j in range(F // tf):
        parts += [gate[..., j * tf:(j + 1) * tf], up[..., j * tf:(j + 1) * tf]]
    return jnp.concatenate(parts, axis=-1)


def _deinterleave(gu):
    F = gu.shape[-1] // 2
    tf = _ff_tile(F)
    gate = [gu[..., 2 * j * tf:(2 * j + 1) * tf] for j in range(F // tf)]
    up = [gu[..., (2 * j + 1) * tf:(2 * j + 2) * tf] for j in range(F // tf)]
    return jnp.concatenate(gate, axis=-1), jnp.concatenate(up, axis=-1)


def _swiglu_fwd(gu):
    T, F2 = gu.shape
    F = F2 // 2
    tf = _ff_tile(F)
    tr = _pick(T, 256, 8)

    def body(gu_ref, o_ref):
        g = gu_ref[:, :tf]
        o_ref[...] = (g * _sigmoid(g) * gu_ref[:, tf:]).astype(BF16)

    return pl.pallas_call(
        body, name="swiglu_fwd", grid=(T // tr, F // tf),
        in_specs=[pl.BlockSpec((tr, 2 * tf), lambda i, j: (i, j))],
        out_specs=pl.BlockSpec((tr, tf), lambda i, j: (i, j)),
        out_shape=jax.ShapeDtypeStruct((T, F), BF16),
        compiler_params=_params(("parallel", "parallel")),
    )(gu)


def _swiglu_bwd(gu, dact):
    T, F2 = gu.shape
    F = F2 // 2
    tf = _ff_tile(F)
    tr = _pick(T, 256, 8)

    def body(gu_ref, d_ref, o_ref):
        g, u, d = gu_ref[:, :tf], gu_ref[:, tf:], d_ref[...]
        s = _sigmoid(g)
        o_ref[:, :tf] = (d * u * (s * (1.0 + g * (1.0 - s)))).astype(BF16)
        o_ref[:, tf:] = (d * (g * s)).astype(BF16)

    return pl.pallas_call(
        body, name="swiglu_bwd", grid=(T // tr, F // tf),
        in_specs=[pl.BlockSpec((tr, 2 * tf), lambda i, j: (i, j)), pl.BlockSpec((tr, tf), lambda i, j: (i, j))],
        out_specs=pl.BlockSpec((tr, 2 * tf), lambda i, j: (i, j)),
        out_shape=jax.ShapeDtypeStruct((T, F2), BF16),
        compiler_params=_params(("parallel", "parallel")),
    )(gu, dact)


def _loss_fwd_bwd(y, target):
    NB, S, D = y.shape
    tr = _row_tile(S)

    def body(y_ref, t_ref, l_ref, d_ref):
        @pl.when((pl.program_id(0) == 0) & (pl.program_id(1) == 0))
        def _():
            l_ref[...] = jnp.zeros_like(l_ref)

        e = y_ref[...] - t_ref[...]
        d_ref[...] = e / D
        l_ref[...] += 0.5 * jnp.sum(jnp.mean(e * e, axis=-1, keepdims=True), axis=0, keepdims=True)

    tok = pl.BlockSpec((None, tr, D), lambda b, r: (b, r, 0))
    return pl.pallas_call(
        body, name="loss", grid=(NB, S // tr), in_specs=[tok, tok],
        out_specs=[pl.BlockSpec((1, 1), lambda b, r: (0, 0)), tok],
        out_shape=[jax.ShapeDtypeStruct((1, 1), F32), jax.ShapeDtypeStruct((NB, S, D), F32)],
        compiler_params=_params(("arbitrary", "arbitrary")),
    )(y, target)


def _half_sums(v, lo):
    sa = jnp.sum(jnp.where(lo, v, 0.0), axis=-1, keepdims=True)
    sb = jnp.sum(jnp.where(lo, 0.0, v), axis=-1, keepdims=True)
    return jnp.where(lo, sa, sb)


def _rope_swap(v, lane64):
    up = pltpu.roll(v, LANES - ROT_DIM // 2, 1)
    down = pltpu.roll(v, ROT_DIM // 2, 1)
    return jnp.where(lane64 < ROT_DIM // 2, up, jnp.where(lane64 < ROT_DIM, down, 0.0))


def _qk_prep_fwd(qkv, tab_c, tab_s, gains):
    T, W = qkv.shape
    R = W // LANES
    tt = _pick(T, 256, 8)

    def body(x_ref, c_ref, s_ref, g_ref, o_ref):
        lane = lax.broadcasted_iota(jnp.int32, (tt, LANES), 1)
        lo = lane < HEAD_DIM
        lane64 = lane & (HEAD_DIM - 1)
        c, s = c_ref[...], s_ref[...]
        for j in range(R - 1):
            cols = slice(j * LANES, (j + 1) * LANES)
            xv = x_ref[:, cols]
            rstd = lax.rsqrt(_half_sums(xv * xv, lo) / HEAD_DIM + EPS)
            yn = xv * rstd * g_ref[j:j + 1, :]
            o_ref[:, cols] = (yn * c + _rope_swap(yn, lane64) * s).astype(BF16)
        o_ref[:, (R - 1) * LANES:] = x_ref[:, (R - 1) * LANES:].astype(BF16)

    tok = pl.BlockSpec((tt, W), lambda t: (t, 0))
    tab = pl.BlockSpec((tt, LANES), lambda t: (t, 0))
    return pl.pallas_call(
        body, name="qk_prep_fwd", grid=(T // tt,),
        in_specs=[tok, tab, tab, pl.BlockSpec((R, LANES), lambda t: (0, 0))],
        out_specs=tok, out_shape=jax.ShapeDtypeStruct((T, W), BF16),
        compiler_params=_params(("parallel",)),
    )(qkv, tab_c, tab_s, gains)


def _qk_prep_bwd(qkv, dq, dk, dv, tab_c, tab_s, gains):
    T, W = qkv.shape
    R = W // LANES
    QW = dq.shape[1]
    tt = _pick(T, 256, 8)

    def body(x_ref, dq_ref, dk_ref, dv_ref, c_ref, s_ref, g_ref, o_ref, dg_ref):
        @pl.when(pl.program_id(0) == 0)
        def _():
            dg_ref[...] = jnp.zeros_like(dg_ref)

        lane = lax.broadcasted_iota(jnp.int32, (tt, LANES), 1)
        lo = lane < HEAD_DIM
        lane64 = lane & (HEAD_DIM - 1)
        c, s = c_ref[...], s_ref[...]
        for j in range(R - 1):
            cols = slice(j * LANES, (j + 1) * LANES)
            xv = x_ref[:, cols]
            d = dq_ref[:, cols] if j < R - 2 else dk_ref[...]
            rstd = lax.rsqrt(_half_sums(xv * xv, lo) / HEAD_DIM + EPS)
            xh = xv * rstd
            dyn = d * c + _rope_swap(d * s, lane64)
            dg_ref[j:j + 1, :] += jnp.sum(dyn * xh, axis=0, keepdims=True)
            dxh = dyn * g_ref[j:j + 1, :]
            proj = _half_sums(dxh * xh, lo) / HEAD_DIM
            o_ref[:, cols] = (rstd * (dxh - xh * proj)).astype(BF16)
        o_ref[:, (R - 1) * LANES:] = dv_ref[...].astype(BF16)

    tok = pl.BlockSpec((tt, W), lambda t: (t, 0))
    tab = pl.BlockSpec((tt, LANES), lambda t: (t, 0))
    gsp = pl.BlockSpec((R, LANES), lambda t: (0, 0))
    return pl.pallas_call(
        body, name="qk_prep_bwd", grid=(T // tt,),
        in_specs=[tok, pl.BlockSpec((tt, QW), lambda t: (t, 0)), tab, tab, tab, tab, gsp], out_specs=[tok, gsp],
        out_shape=[jax.ShapeDtypeStruct((T, W), BF16), jax.ShapeDtypeStruct((R, LANES), F32)],
        compiler_params=_params(("arbitrary",)),
    )(qkv, dq, dk, dv, tab_c, tab_s, gains)


def _band_mask(i):
    r = lax.broadcasted_iota(jnp.int32, (2 * BLOCK, 2 * BLOCK), 0) & (BLOCK - 1)
    c = lax.broadcasted_iota(jnp.int32, (2 * BLOCK, 2 * BLOCK), 1)
    rel = r + BLOCK - c
    return (rel >= 0) & (rel < BLOCK) & ((c >= BLOCK) | (i > 0))


def _swa_softmax(s, valid, sink):
    s = jnp.where(valid, s * ATTN_SCALE, NEG_BIG)
    m = jnp.maximum(jnp.max(s, axis=1, keepdims=True), sink)
    p = jnp.exp(s - m)
    ps = jnp.exp(sink - m)
    denom = jnp.sum(p, axis=1, keepdims=True) + ps
    return p / denom, ps / denom


A_GROUP = 2


Q_WIDTH_A = N_Q_A * HEAD_DIM
N_PAIR_A = Q_WIDTH_A // LANES


def _swa_specs():
    qs = pl.BlockSpec((None, BLOCK, Q_WIDTH_A), lambda b, i: (b, i, 0))

    def kv(col, back):
        return pl.BlockSpec((None, BLOCK, LANES), lambda b, i: (b, jnp.maximum(i - back, 0), col))

    return qs, kv(N_PAIR_A, 1), kv(N_PAIR_A, 0), kv(N_PAIR_A + 1, 1), kv(N_PAIR_A + 1, 0)


def _dup_heads(t):
    lo = lax.broadcasted_iota(jnp.int32, t.shape, 1) < HEAD_DIM
    sw = pltpu.roll(t.astype(F32), HEAD_DIM, 1).astype(BF16)
    return jnp.where(lo, t, sw), jnp.where(lo, sw, t)


def _kv_tiles(kp_ref, kc_ref, vp_ref, vc_ref):
    kd = _dup_heads(jnp.concatenate([kp_ref[...], kc_ref[...]], axis=0))
    vd = _dup_heads(jnp.concatenate([vp_ref[...], vc_ref[...]], axis=0))
    return kd, vd


def _attn_a_fwd(qkn, sinks):
    NB, S, _ = qkn.shape
    qs, kp, kc, vp, vc = _swa_specs()

    def body(q_ref, kp_ref, kc_ref, vp_ref, vc_ref, sink_ref, o_ref):
        i = pl.program_id(1)
        kd, vd = _kv_tiles(kp_ref, kc_ref, vp_ref, vc_ref)
        valid = _band_mask(i)
        lo = lax.broadcasted_iota(jnp.int32, (BLOCK, LANES), 1) < HEAD_DIM
        top = lax.broadcasted_iota(jnp.int32, (2 * BLOCK, 1), 0) < BLOCK
        for first in range(0, N_PAIR_A, A_GROUP):
            pairs = range(first, first + A_GROUP)
            qs_ = [jnp.concatenate(_head_halves(q_ref[:, p * LANES:(p + 1) * LANES], lo), axis=0) for p in pairs]
            ss = [lax.dot_general(q, kd[2 * p // GROUP_A], _NT, preferred_element_type=F32) for q, p in zip(qs_, pairs)]
            pns = [_swa_softmax(s, valid, jnp.where(top, sink_ref[2 * p], sink_ref[2 * p + 1]))[0]
                   for s, p in zip(ss, pairs)]
            pvs = [jnp.dot(pn.astype(BF16), vd[2 * p // GROUP_A], preferred_element_type=F32) for pn, p in zip(pns, pairs)]
            for pv, p in zip(pvs, pairs):
                o_ref[:, p * LANES:(p + 1) * LANES] = jnp.where(lo, pv[:BLOCK], pv[BLOCK:]).astype(BF16)

    return pl.pallas_call(
        body, name="attn_a_fwd", grid=(NB, S // BLOCK),
        in_specs=[qs, kp, kc, vp, vc, pl.BlockSpec(memory_space=pltpu.SMEM)],
        out_specs=qs, out_shape=jax.ShapeDtypeStruct((NB, S, Q_WIDTH_A), BF16),
        compiler_params=_params(("parallel", "arbitrary")),
    )(qkn, qkn, qkn, qkn, qkn, sinks)


def _attn_a_bwd(qkn, do, sinks):
    NB, S, _ = qkn.shape
    qs, kp, kc, vp, vc = _swa_specs()
    full = pl.BlockSpec((None, S, LANES), lambda b, i: (b, 0, 0))
    sink_out = pl.BlockSpec((None, N_Q_A, LANES), lambda b, i: (b, 0, 0))

    def body(q_ref, do_ref, kp_ref, kc_ref, vp_ref, vc_ref, sink_ref, dq_ref, dk_ref, dv_ref, ds_ref, dk_s, dv_s):
        i = pl.program_id(1)

        @pl.when(i == 0)
        def _():
            dk_ref[...] = jnp.zeros_like(dk_ref)
            dv_ref[...] = jnp.zeros_like(dv_ref)
            ds_ref[...] = jnp.zeros_like(ds_ref)

        dk_s[...] = jnp.zeros_like(dk_s)
        dv_s[...] = jnp.zeros_like(dv_s)
        kd, vd = _kv_tiles(kp_ref, kc_ref, vp_ref, vc_ref)
        valid = _band_mask(i)
        lo = lax.broadcasted_iota(jnp.int32, (BLOCK, LANES), 1) < HEAD_DIM
        top = lax.broadcasted_iota(jnp.int32, (2 * BLOCK, 1), 0) < BLOCK
        for first in range(0, N_PAIR_A, A_GROUP):
            pairs = range(first, first + A_GROUP)
            kvs = [2 * p // GROUP_A for p in pairs]
            qs_ = [jnp.concatenate(_head_halves(q_ref[:, p * LANES:(p + 1) * LANES], lo), axis=0) for p in pairs]
            dos = [jnp.concatenate(_head_halves(do_ref[:, p * LANES:(p + 1) * LANES], lo), axis=0) for p in pairs]
            ss = [lax.dot_general(q, kd[kv], _NT, preferred_element_type=F32) for q, kv in zip(qs_, kvs)]
            dps = [lax.dot_general(d, vd[kv], _NT, preferred_element_type=F32) for d, kv in zip(dos, kvs)]
            sm = [_swa_softmax(s, valid, jnp.where(top, sink_ref[2 * p], sink_ref[2 * p + 1])) for s, p in zip(ss, pairs)]
            deltas = [jnp.sum(pn * dp, axis=1, keepdims=True) for (pn, _), dp in zip(sm, dps)]
            dsbs = [(pn * (dp - delta) * ATTN_SCALE).astype(BF16) for (pn, _), dp, delta in zip(sm, dps, deltas)]
            for n, p in enumerate(pairs):
                dq2 = jnp.dot(dsbs[n], kd[kvs[n]], preferred_element_type=F32)
                dq_ref[:, p * LANES:(p + 1) * LANES] = jnp.where(lo, dq2[:BLOCK], dq2[BLOCK:])
                dk_s[kvs[n]] += lax.dot_general(dsbs[n], qs_[n], _TN, preferred_element_type=F32)
                dv_s[kvs[n]] += lax.dot_general(sm[n][0].astype(BF16), dos[n], _TN, preferred_element_type=F32)
                t = sm[n][1] * deltas[n]
                for hh in range(2):
                    dsink = -jnp.sum(t[hh * BLOCK:(hh + 1) * BLOCK], axis=0, keepdims=True)
                    ds_ref[2 * p + hh:2 * p + hh + 1, :] += jnp.broadcast_to(dsink, (1, LANES))

        lo2 = lax.broadcasted_iota(jnp.int32, (2 * BLOCK, LANES), 1) < HEAD_DIM

        def fold(acc):
            halves = [acc[kv] + pltpu.roll(acc[kv], HEAD_DIM, 1) for kv in range(N_KV_A)]
            return jnp.where(lo2, halves[0], halves[1])

        dk2, dv2 = fold(dk_s), fold(dv_s)

        @pl.when(i > 0)
        def _():
            start = pl.multiple_of((i - 1) * BLOCK, BLOCK)
            dk_ref[pl.ds(start, 2 * BLOCK), :] += dk2
            dv_ref[pl.ds(start, 2 * BLOCK), :] += dv2

        @pl.when(i == 0)
        def _():
            dk_ref[0:BLOCK, :] += dk2[BLOCK:, :]
            dv_ref[0:BLOCK, :] += dv2[BLOCK:, :]

    slots = pltpu.VMEM((N_KV_A, 2 * BLOCK, LANES), F32)
    return pl.pallas_call(
        body, name="attn_a_bwd", grid=(NB, S // BLOCK),
        in_specs=[qs, qs, kp, kc, vp, vc, pl.BlockSpec(memory_space=pltpu.SMEM)],
        out_specs=[qs, full, full, sink_out],
        out_shape=[jax.ShapeDtypeStruct((NB, S, Q_WIDTH_A), F32), jax.ShapeDtypeStruct((NB, S, LANES), F32),
                   jax.ShapeDtypeStruct((NB, S, LANES), F32), jax.ShapeDtypeStruct((NB, N_Q_A, LANES), F32)],
        scratch_shapes=[slots, slots],
        compiler_params=_params(("parallel", "arbitrary")),
    )(qkn, do, qkn, qkn, qkn, qkn, sinks)


def _cumsum_mats():
    src = lax.broadcasted_iota(jnp.int32, (2 * BLOCK, 2 * BLOCK), 0) % BLOCK
    dst = lax.broadcasted_iota(jnp.int32, (2 * BLOCK, 2 * BLOCK), 1)
    ones = dst >= BLOCK
    rev = ((src > dst) | ones).astype(BF16)
    fwd = ((src < dst) | ones).astype(BF16)
    return rev, fwd


def _log_sigmoids(z):
    sp = jnp.log(1.0 + jnp.exp(-jnp.abs(z)))
    return jnp.minimum(z, 0.0) - sp, -(jnp.maximum(z, 0.0) + sp)


def _cumsum_mxu_many(vs, mat):
    parts = []
    for v in vs:
        hi = v.astype(BF16)
        parts.append(jnp.concatenate([hi, (v - hi.astype(F32)).astype(BF16)], axis=1))
    r = jnp.dot(jnp.concatenate(parts, axis=0), mat, preferred_element_type=F32)
    return [(r[n * BLOCK:(n + 1) * BLOCK, :BLOCK], r[n * BLOCK:(n + 1) * BLOCK, BLOCK:]) for n in range(len(vs))]


def _strict_mask():
    r = lax.broadcasted_iota(jnp.int32, (BLOCK, BLOCK), 0)
    c = lax.broadcasted_iota(jnp.int32, (BLOCK, BLOCK), 1)
    return c < r


def _tile(ref, j):
    return ref[pl.ds(pl.multiple_of(j * BLOCK, BLOCK), BLOCK), :]


SWEEP_EXIT = -88.0


def _head_halves(t, lo):
    zero = jnp.zeros_like(t)
    return jnp.where(lo, t, zero), jnp.where(lo, zero, t)


def _sb_specs(S, HD, width):
    n = HD // width
    blk = pl.BlockSpec((None, BLOCK, width), lambda b, p, i: (b, i, p))
    k_full = pl.BlockSpec((None, S, width), lambda b, p, i: (b, 0, n + p))
    v_full = pl.BlockSpec((None, S, width), lambda b, p, i: (b, 0, 2 * n + p))
    mat = pl.BlockSpec((2 * BLOCK, 2 * BLOCK), lambda b, p, i: (0, 0))
    return blk, k_full, v_full, mat


SB_FWD_PAIRS = 2
SB_BWD_PAIRS = 2


def _attn_b_fwd(qkv, rev):
    NB, S, W = qkv.shape
    HD = W // 3
    width = SB_FWD_PAIRS * LANES
    n_heads = 2 * SB_FWD_PAIRS
    blk, k_full, v_full, mat = _sb_specs(S, HD, width)

    def body(q_ref, k_ref, v_ref, rev_ref, o_ref):
        i = pl.program_id(2)
        rv = rev_ref[...]
        mask = _strict_mask()
        lo = lax.broadcasted_iota(jnp.int32, (BLOCK, LANES), 1) < HEAD_DIM
        q_all = q_ref[...]
        q_stack = [jnp.concatenate(_head_halves(q_all[:, p * LANES:(p + 1) * LANES] * ATTN_SCALE, lo), axis=0)
                   for p in range(SB_FWD_PAIRS)]

        def pair_tiles(ref, j):
            t = _tile(ref, j)
            return [t[:, p * LANES:(p + 1) * LANES] for p in range(SB_FWD_PAIRS)]

        def tile_pass(j, carries, diagonal):
            ks, vs = pair_tiles(k_ref, j), pair_tiles(v_ref, j)
            zs = []
            for p in range(SB_FWD_PAIRS):
                z2 = lax.dot_general(q_stack[p], ks[p], _NT, preferred_element_type=F32)
                zs += [z2[:BLOCK], z2[BLOCK:]]
            logs = [_log_sigmoids(z) for z in zs]
            cums = _cumsum_mxu_many([jnp.where(mask, lm, 0.0) if diagonal else lm for _, lm in logs], rv)
            probs, new_c = [], []
            for h in range(n_heads):
                after, rs = cums[h]
                if diagonal:
                    a = jnp.where(mask, jnp.exp(logs[h][0] + after), 0.0)
                    new_c.append(rs)
                else:
                    a = jnp.exp(logs[h][0] + after + carries[h])
                    new_c.append(carries[h] + rs)
                probs.append(a.astype(BF16))
            outs = []
            for p in range(SB_FWD_PAIRS):
                pv = jnp.dot(jnp.concatenate(probs[2 * p:2 * p + 2], axis=0), vs[p], preferred_element_type=F32)
                outs.append(jnp.where(lo, pv[:BLOCK], pv[BLOCK:]))
            return new_c, outs

        carries, accs = tile_pass(i, None, True)

        def live(cs):
            top = cs[0]
            for c in cs[1:]:
                top = jnp.maximum(top, c)
            return jnp.max(top) > SWEEP_EXIT

        def cond(st):
            return (st[0] < i) & st[1]

        def step(st):
            jj, _, cs, accs = st
            new_c, outs = tile_pass(i - 1 - jj, cs, False)
            return jj + 1, live(new_c), new_c, [acc + o for acc, o in zip(accs, outs)]

        st = lax.while_loop(cond, step, (jnp.int32(0), live(carries), carries, accs))
        for p in range(SB_FWD_PAIRS):
            o_ref[:, p * LANES:(p + 1) * LANES] = st[3][p].astype(BF16)

    return pl.pallas_call(
        body, name="attn_b_fwd", grid=(NB, HD // width, S // BLOCK),
        in_specs=[blk, k_full, v_full, mat], out_specs=blk,
        out_shape=jax.ShapeDtypeStruct((NB, S, HD), BF16),
        compiler_params=_params(("parallel", "parallel", "arbitrary")),
    )(qkv, qkv, qkv, rev)


def _attn_b_bwd(qkv, do, rev, fwd):
    NB, S, W = qkv.shape
    HD = W // 3
    width = SB_BWD_PAIRS * LANES
    n_heads = 2 * SB_BWD_PAIRS
    nj = S // BLOCK
    blk, k_full, v_full, mat = _sb_specs(S, HD, width)
    acc_full = pl.BlockSpec((None, S, width), lambda b, p, i: (b, 0, p))

    def body(q_ref, do_ref, k_ref, v_ref, rev_ref, fwd_ref, dq_ref, dk_ref, dv_ref, sig_s, a_s, e_s):
        i = pl.program_id(2)

        @pl.when(i == 0)
        def _():
            dk_ref[...] = jnp.zeros_like(dk_ref)
            dv_ref[...] = jnp.zeros_like(dv_ref)

        rv, fw = rev_ref[...], fwd_ref[...]
        mask = _strict_mask()
        lo = lax.broadcasted_iota(jnp.int32, (BLOCK, LANES), 1) < HEAD_DIM
        pairs = range(SB_BWD_PAIRS)

        def cols(p):
            return slice(p * LANES, (p + 1) * LANES)

        q_stack = [jnp.concatenate(_head_halves(q_ref[:, cols(p)], lo), axis=0) for p in pairs]
        qs_stack = [q * ATTN_SCALE for q in q_stack]
        do_stack = [jnp.concatenate(_head_halves(do_ref[:, cols(p)], lo), axis=0) for p in pairs]

        def sweep1_tile(j, carries, diagonal):
            kj, vj = _tile(k_ref, j), _tile(v_ref, j)
            zs, das = [], []
            for p in pairs:
                z2 = lax.dot_general(qs_stack[p], kj[:, cols(p)], _NT, preferred_element_type=F32)
                da2 = lax.dot_general(do_stack[p], vj[:, cols(p)], _NT, preferred_element_type=F32)
                zs += [z2[:BLOCK], z2[BLOCK:]]
                das += [da2[:BLOCK], da2[BLOCK:]]
            logs = [_log_sigmoids(z) for z in zs]
            cums = _cumsum_mxu_many([jnp.where(mask, lm, 0.0) if diagonal else lm for _, lm in logs], rv)
            new_c = []
            for h in range(n_heads):
                lb, (after, rs) = logs[h][0], cums[h]
                if diagonal:
                    a = jnp.where(mask, jnp.exp(lb + after), 0.0)
                    new_c.append(rs)
                else:
                    a = jnp.exp(lb + after + carries[h])
                    new_c.append(carries[h] + rs)
                sig_s[h, j] = jnp.exp(lb)
                a_s[h, j] = a.astype(BF16)
                e_s[h, j] = das[h] * a
            return new_c

        carries = sweep1_tile(i, None, True)

        def live(cs):
            top = cs[0]
            for c in cs[1:]:
                top = jnp.maximum(top, c)
            return jnp.max(top) > SWEEP_EXIT

        def cond(st):
            return (st[0] < i) & st[1]

        def sweep1(st):
            new_c = sweep1_tile(i - 1 - st[0], st[2], False)
            return st[0] + 1, live(new_c), new_c

        visited = lax.while_loop(cond, sweep1, (jnp.int32(0), live(carries), carries))[0]

        def grads(j, st, diagonal):
            prefixes, dqs = st
            kj = _tile(k_ref, j)
            es = [e_s[h, j] for h in range(n_heads)]
            cums = _cumsum_mxu_many(es, fw)
            dzs, new_p = [], []
            for h in range(n_heads):
                sg = sig_s[h, j]
                e_before, rs = cums[h]
                dz = (es[h] * (1.0 - sg) - (e_before + prefixes[h]) * sg) * ATTN_SCALE
                if diagonal:
                    dz = jnp.where(mask, dz, 0.0)
                dzs.append(dz.astype(BF16))
                new_p.append(prefixes[h] + rs)
            rows = pl.ds(pl.multiple_of(j * BLOCK, BLOCK), BLOCK)
            new_dq = []
            for p in pairs:
                dz_stack = jnp.concatenate(dzs[2 * p:2 * p + 2], axis=0)
                a_stack = jnp.concatenate([a_s[2 * p, j], a_s[2 * p + 1, j]], axis=0)
                dq2 = jnp.dot(dz_stack, kj[:, cols(p)], preferred_element_type=F32)
                new_dq.append(dqs[p] + jnp.where(lo, dq2[:BLOCK], dq2[BLOCK:]))
                dk_ref[rows, cols(p)] += lax.dot_general(dz_stack, q_stack[p], _TN, preferred_element_type=F32)
                dv_ref[rows, cols(p)] += lax.dot_general(a_stack, do_stack[p], _TN, preferred_element_type=F32)
            return new_p, new_dq

        zeros = jnp.zeros((BLOCK, BLOCK), F32)
        st = lax.fori_loop(i - visited, i, lambda j, st: grads(j, st, False),
                           ([zeros] * n_heads, [zeros] * SB_BWD_PAIRS))
        dqs = grads(i, st, True)[1]
        for p in pairs:
            dq_ref[:, cols(p)] = dqs[p]

    f32_stash = pltpu.VMEM((n_heads, nj, BLOCK, BLOCK), F32)
    bf16_stash = pltpu.VMEM((n_heads, nj, BLOCK, BLOCK), BF16)
    return pl.pallas_call(
        body, name="attn_b_bwd", grid=(NB, HD // width, nj),
        in_specs=[blk, blk, k_full, v_full, mat, mat], out_specs=[blk, acc_full, acc_full],
        out_shape=[jax.ShapeDtypeStruct((NB, S, HD), F32)] * 3,
        scratch_shapes=[f32_stash, bf16_stash, f32_stash],
        compiler_params=_params(("parallel", "parallel", "arbitrary")),
    )(qkv, do, qkv, qkv, rev, fwd)


def _ada_fwd(c_all, w, b):
    L, D, N = w.shape
    B = c_all.shape[0]

    def body(c_ref, w_ref, b_ref, o_ref):
        cv = c_ref[...]
        cond = (cv * _sigmoid(cv)).astype(BF16)
        o_ref[...] = jnp.dot(cond, w_ref[...].astype(BF16), preferred_element_type=F32) + b_ref[...]

    return pl.pallas_call(
        body, name="ada_fwd", grid=(L,),
        in_specs=[pl.BlockSpec((B, D), lambda l: (0, 0)), pl.BlockSpec((None, D, N), lambda l: (l, 0, 0)),
                  pl.BlockSpec((None, 1, N), lambda l: (l, 0, 0))],
        out_specs=pl.BlockSpec((None, B, N), lambda l: (l, 0, 0)),
        out_shape=jax.ShapeDtypeStruct((L, B, N), F32),
        compiler_params=_params(("parallel",)),
    )(c_all, w, b)


def _ada_bwd(c_all, dmod_all, dmod_shard):
    L, B, N = dmod_shard.shape
    D = c_all.shape[1]
    N_all = dmod_all.shape[2]

    def body(c_ref, da_ref, ds_ref, gw_ref, gb_ref):
        cv = c_ref[...]
        cond = (cv * _sigmoid(cv)).astype(BF16)
        gw_ref[...] = lax.dot_general(cond, ds_ref[...].astype(BF16), _TN, preferred_element_type=F32)
        gb_ref[...] = jnp.sum(da_ref[...], axis=0, keepdims=True)

    return pl.pallas_call(
        body, name="ada_bwd", grid=(L,),
        in_specs=[pl.BlockSpec((B, D), lambda l: (0, 0)), pl.BlockSpec((None, B, N_all), lambda l: (l, 0, 0)),
                  pl.BlockSpec((None, B, N), lambda l: (l, 0, 0))],
        out_specs=[pl.BlockSpec((None, D, N), lambda l: (l, 0, 0)), pl.BlockSpec((None, 1, N_all), lambda l: (l, 0, 0))],
        out_shape=[jax.ShapeDtypeStruct((L, D, N), F32), jax.ShapeDtypeStruct((L, 1, N_all), F32)],
        compiler_params=_params(("parallel",)),
    )(c_all, dmod_all, dmod_shard)


def _adamw(w, g, m, v, name):
    shape = w.shape
    C = shape[-1]
    R = w.size // C
    tr = _pick(R, max(8, (1 << 18) // C), 8)
    c1 = 1.0 - ADAM_B1 ** ADAM_STEP
    c2 = 1.0 - ADAM_B2 ** ADAM_STEP

    def body(w_ref, g_ref, m_ref, v_ref, d_ref, nm_ref, nv_ref):
        gv = g_ref[...]
        nm = ADAM_B1 * m_ref[...] + (1.0 - ADAM_B1) * gv
        nv = ADAM_B2 * v_ref[...] + (1.0 - ADAM_B2) * (gv * gv)
        d_ref[...] = -ADAM_LR * ((nm / c1) / (jnp.sqrt(nv / c2) + ADAM_EPS) + ADAM_WD * w_ref[...])
        nm_ref[...] = nm
        nv_ref[...] = nv

    spec = pl.BlockSpec((tr, C), lambda r: (r, 0))
    out = pl.pallas_call(
        body, name=name, grid=(R // tr,), in_specs=[spec] * 4, out_specs=[spec] * 3,
        out_shape=[jax.ShapeDtypeStruct((R, C), F32)] * 3,
        compiler_params=_params(("parallel",)),
    )(*[t.reshape(R, C) for t in (w, g, m, v)])
    return [t.reshape(shape) for t in out]


_SHARDED = (("wqkv_a", 2), ("wo_a", 1), ("wqkv_b", 2), ("wo_b", 1), ("w_gate", 2), ("w_up", 2), ("w_down", 1))


def _pack_full(layers, axis):
    L = len(layers)
    R, C = layers[0].shape

    def shards(m):
        if axis == 2:
            return m.reshape(R, 4, C // 4).transpose(1, 0, 2)
        return m.reshape(4, R // 4, C)

    halves = [jnp.stack([shards(m) for m in layers[h * (L // 2):(h + 1) * (L // 2)]], axis=1) for h in range(2)]
    return jnp.stack(halves)


def _unpack_full(gathered, axis):
    _, Lh, Rs, Cs = gathered.shape
    t = gathered.reshape(4, 2, Lh, Rs, Cs)
    layers = []
    for h in range(2):
        for l in range(Lh):
            piece = t[:, h, l]
            if axis == 2:
                layers.append(piece.transpose(1, 0, 2).reshape(Rs, 4 * Cs))
            else:
                layers.append(piece.reshape(4 * Rs, Cs))
    return layers


def _sum_slabs(own, recv, name, with_bf16=False):
    C = own.shape[-1]
    out = _sum_leading(recv.reshape(recv.shape[0], -1, C), name, own=own.reshape(-1, C), with_bf16=with_bf16)
    if with_bf16:
        return out[0].reshape(own.shape), out[1].reshape(own.shape)
    return out.reshape(own.shape)


def _gather8(x, name):
    return _all_gather8([x], name)[0]


def _rope_tables(positions):
    half = ROT_DIM // 2
    inv_freq = jnp.power(jnp.float32(ROPE_THETA), -jnp.arange(half, dtype=F32) * 2.0 / ROT_DIM)
    ang = positions.astype(F32).reshape(-1, 1) * inv_freq
    cos, sin = jnp.cos(ang), jnp.sin(ang)
    T = ang.shape[0]
    rest = HEAD_DIM - ROT_DIM
    c64 = jnp.concatenate([cos, cos, jnp.ones((T, rest), F32)], axis=1)
    s64 = jnp.concatenate([-sin, sin, jnp.zeros((T, rest), F32)], axis=1)
    return jnp.tile(c64, (1, 2)), jnp.tile(s64, (1, 2))


def _gain_rows(q_gain, k_gain):
    q2 = jnp.tile(q_gain.reshape(1, HEAD_DIM), (GROUP_A, 2))
    k2 = jnp.tile(k_gain.reshape(1, HEAD_DIM), (1, 2))
    return jnp.concatenate([q2, k2, jnp.ones((1, LANES), F32)], axis=0)


def _local_step(x, positions, mod, norm1_g, norm2_g, q_norm_a, k_norm_a, sinks_a,
                wqkv_a, wo_a, wqkv_b, wo_b, wgu, wd, loss_target):
    NB, S, D = x.shape
    T = NB * S
    QA = N_Q_A * HEAD_DIM
    tab_c, tab_s = _rope_tables(positions)
    rev, fwd = _cumsum_mats()

    saved = []
    xc = x
    for i in range(DEPTH):
        j = i // 2
        sh1, sc1, g1, sh2, sc2, g2 = [mod[i][:, k * D:(k + 1) * D].reshape(NB, 1, D) for k in range(6)]
        st = dict(x=xc, sc1=sc1, g1=g1, sc2=sc2, g2=g2)
        h = _norm_mod_fwd(xc, norm1_g[i:i + 1], sc1, sh1)
        st["h"] = h.reshape(T, D)
        if i % 2 == 0:
            st["qkv"] = _matmul(st["h"], wqkv_a[j], "nn", F32, "qkv_a")
            st["gains"] = _gain_rows(q_norm_a[j], k_norm_a[j])
            st["qkn"] = _qk_prep_fwd(st["qkv"], tab_c, tab_s, st["gains"]).reshape(NB, S, -1)
            st["o"] = _attn_a_fwd(st["qkn"], sinks_a[j]).reshape(T, QA)
            y = _matmul(st["o"], wo_a[j], "nn", F32, "wo_a")
        else:
            st["qkv"] = _matmul(st["h"], wqkv_b[j], "nn", BF16, "qkv_b").reshape(NB, S, -1)
            st["o"] = _attn_b_fwd(st["qkv"], rev).reshape(T, N_H_B * HEAD_DIM)
            y = _matmul(st["o"], wo_b[j], "nn", F32, "wo_b")
        st["y"] = y.reshape(NB, S, D)
        x1 = _gate_res(xc, st["y"], g1)
        st["x1"] = x1
        h2 = _norm_mod_fwd(x1, norm2_g[i:i + 1], sc2, sh2)
        st["h2"] = h2.reshape(T, D)
        st["gu"] = _matmul(st["h2"], wgu[i], "nn", F32, "gate_up")
        st["act"] = _swiglu_fwd(st["gu"])
        st["m"] = _matmul(st["act"], wd[i], "nn", F32, "down").reshape(NB, S, D)
        xc = _gate_res(x1, st["m"], g2)
        saved.append(st)

    loss, dx = _loss_fwd_bwd(xc, loss_target)

    grads = {name: [None] * n for name, n in
             (("wqkv_a", 2), ("wo_a", 2), ("wqkv_b", 2), ("wo_b", 2), ("wgu", DEPTH), ("wd", DEPTH),
              ("norm1_g", DEPTH), ("norm2_g", DEPTH), ("q_norm_a", 2), ("k_norm_a", 2), ("sinks_a", 2))}
    dmod = [None] * DEPTH
    for i in reversed(range(DEPTH)):
        j = i // 2
        st = saved[i]
        dm, dg2 = _gate_res_bwd(dx, st["m"], st["g2"])
        dm = dm.reshape(T, D)
        dact = _matmul(dm, wd[i], "nt", F32, "d_act")
        grads["wd"][i] = _matmul(st["act"], dm, "tn", F32, "d_wd")
        dgu = _swiglu_bwd(st["gu"], dact)
        dh2 = _matmul(dgu, wgu[i], "nt", F32, "d_h2")
        grads["wgu"][i] = _matmul(st["h2"], dgu, "tn", F32, "d_wgu")
        dx1, dsh2, dsc2, grads["norm2_g"][i] = _norm_mod_bwd(
            st["x1"], norm2_g[i:i + 1], st["sc2"], dh2.reshape(NB, S, D), dx)
        dy, dg1 = _gate_res_bwd(dx1, st["y"], st["g1"])
        dy = dy.reshape(T, D)
        if i % 2 == 0:
            do = _matmul(dy, wo_a[j], "nt", BF16, "d_o_a").reshape(NB, S, QA)
            grads["wo_a"][j] = _matmul(st["o"], dy, "tn", F32, "d_wo_a")
            dq, dk, dv, dsink = _attn_a_bwd(st["qkn"], do, sinks_a[j])
            dqkv, dgain = _qk_prep_bwd(st["qkv"], dq.reshape(T, QA), dk.reshape(T, LANES), dv.reshape(T, LANES),
                                       tab_c, tab_s, st["gains"])
            dh = _matmul(dqkv, wqkv_a[j], "nt", F32, "d_h_a")
            grads["wqkv_a"][j] = _matmul(st["h"], dqkv, "tn", F32, "d_wqkv_a")
            grads["q_norm_a"][j] = jnp.sum(dgain[:GROUP_A].reshape(2 * GROUP_A, HEAD_DIM), axis=0)
            grads["k_norm_a"][j] = jnp.sum(dgain[GROUP_A].reshape(2, HEAD_DIM), axis=0)
            grads["sinks_a"][j] = jnp.sum(dsink[..., 0], axis=0)
        else:
            do = _matmul(dy, wo_b[j], "nt", BF16, "d_o_b").reshape(NB, S, -1)
            grads["wo_b"][j] = _matmul(st["o"], dy, "tn", F32, "d_wo_b")
            dq, dk, dv = _attn_b_bwd(st["qkv"], do, rev, fwd)
            dqkv = jnp.concatenate([dq, dk, dv], axis=-1).reshape(T, -1).astype(BF16)
            dh = _matmul(dqkv, wqkv_b[j], "nt", F32, "d_h_b")
            grads["wqkv_b"][j] = _matmul(st["h"], dqkv, "tn", F32, "d_wqkv_b")
        dx, dsh1, dsc1, grads["norm1_g"][i] = _norm_mod_bwd(
            st["x"], norm1_g[i:i + 1], st["sc1"], dh.reshape(NB, S, D), dx1)
        dmod[i] = jnp.concatenate([dsh1, dsc1, dg1, dsh2, dsc2, dg2], axis=-1).reshape(NB, 6 * D)

    matrices = ("wqkv_a", "wo_a", "wqkv_b", "wo_b", "wgu", "wd")
    grads = {name: parts if name in matrices else jnp.stack(parts) for name, parts in grads.items()}
    return loss, dx, grads, jnp.stack(dmod)


def _rows_of(flat, cols=PACK_COLS):
    n = flat.shape[0]
    pad = (-n) % (8 * cols)
    if pad:
        flat = jnp.concatenate([flat, jnp.zeros((pad,), flat.dtype)])
    return flat.reshape(-1, cols)


def kernel(x, c, positions, ada_w, ada_b, norm1_g, norm2_g, wqkv_a, q_norm_a, k_norm_a, sinks_a, wo_a, wqkv_b, wo_b, w_gate, w_up, w_down, loss_target, m_ada_w, m_ada_b, m_norm1_g, m_norm2_g, m_wqkv_a, m_q_norm_a, m_k_norm_a, m_sinks_a, m_wo_a, m_wqkv_b, m_wo_b, m_w_gate, m_w_up, m_w_down, v_ada_w, v_ada_b, v_norm1_g, v_norm2_g, v_wqkv_a, v_q_norm_a, v_k_norm_a, v_sinks_a, v_wo_a, v_wqkv_b, v_wo_b, v_w_gate, v_w_up, v_w_down):
    xi, yi, ci = lax.axis_index("x"), lax.axis_index("y"), lax.axis_index("c")
    dev = 4 * xi + 2 * yi + ci
    chip = 2 * xi + yi
    NB, S, D = x.shape
    B_all = N_DEV * NB
    L = ada_w.shape[0]
    n_mod = ada_w.shape[2] // 2

    c_all = _gather8(_rows_of(c.reshape(-1), LANES), "gather_c").reshape(N_DEV, -1)[:, :NB * D].reshape(B_all, D)
    ada_w_half = lax.dynamic_slice_in_dim(ada_w, ci * n_mod, n_mod, axis=2)
    ada_b_half = lax.dynamic_slice_in_dim(ada_b, dev * n_mod, n_mod, axis=1).reshape(L, 1, n_mod)
    mod_part = _ada_fwd(c_all, ada_w_half, ada_b_half)
    n_part = L * B_all * n_mod
    mod_all = _gather8(_rows_of(mod_part.reshape(-1)), "gather_mod").reshape(N_DEV, -1)[:, :n_part]
    mod_all = mod_all.reshape(N_DEV, L, B_all, n_mod).transpose(1, 2, 0, 3).reshape(L, B_all, N_DEV * n_mod)
    mod = lax.dynamic_slice_in_dim(mod_all, dev * NB, NB, axis=1)

    shards = dict(wqkv_a=wqkv_a, wo_a=wo_a, wqkv_b=wqkv_b, wo_b=wo_b, w_gate=w_gate, w_up=w_up, w_down=w_down)
    halves = []
    for name, _ in _SHARDED:
        w = shards[name]
        half = lax.dynamic_index_in_dim(w.reshape((2, w.shape[0] // 2) + w.shape[1:]), ci, 0, keepdims=False)
        halves.append(half.astype(BF16))
    gathered = _all_gather8(halves, "gather_weights", local_axis=1, local_chunks=8)
    full = {name: _unpack_full(t, axis) for (name, axis), t in zip(_SHARDED, gathered)}
    wgu = [_interleave(gate, up) for gate, up in zip(full["w_gate"], full["w_up"])]

    loss, grad_x, g, dmod = _local_step(
        x, positions, mod, norm1_g, norm2_g, q_norm_a, k_norm_a, sinks_a,
        full["wqkv_a"], full["wo_a"], full["wqkv_b"], full["wo_b"], wgu, full["w_down"], loss_target)

    gate_up = [_deinterleave(t) for t in g["wgu"]]
    g_full = dict(wqkv_a=g["wqkv_a"], wo_a=g["wo_a"], wqkv_b=g["wqkv_b"], wo_b=g["wo_b"],
                  w_gate=[t[0] for t in gate_up], w_up=[t[1] for t in gate_up], w_down=g["wd"])
    packed = [_pack_full(g_full[name], axis) for name, axis in _SHARDED]
    def own(t, index):
        return lax.dynamic_index_in_dim(t, index, 0, keepdims=False)

    from_cores = _exchange(packed, "c", "rs_cores", chunk_axis=0, chunks=4)
    chip_part = [_sum_slabs(own(p, ci), r, "rs_add_cores", with_bf16=True) for p, r in zip(packed, from_cores)]
    from_chips = _exchange([b for _, b in chip_part], "xy", "rs_chips")
    mine = [_sum_slabs(own(p, chip), r, "rs_add_chips") for (p, _), r in zip(chip_part, from_chips)]
    theirs = _sibling_send(mine, "rs_halves")
    grad = {}
    for (name, _), m, t in zip(_SHARDED, mine, theirs):
        first, second = jnp.where(ci == 0, m, t), jnp.where(ci == 0, t, m)
        grad[name] = jnp.stack([first, second]).reshape(shards[name].shape)

    small_names = ("norm1_g", "norm2_g", "q_norm_a", "k_norm_a", "sinks_a")
    small = [dmod.reshape(-1)] + [g[name].reshape(-1) for name in small_names] + [loss.reshape(-1)]
    small_sizes = [t.shape[0] for t in small]
    small_rows = _rows_of(jnp.concatenate(small))
    small_all = _gather8(small_rows, "gather_small")
    small_sum = _sum_leading(small_all, "sum_small").reshape(-1)
    n_dmod = small_sizes[0]
    dmod_all = small_all.reshape(N_DEV, -1)[:, :n_dmod].reshape(N_DEV, L, NB, 6 * D)
    dmod_all = dmod_all.transpose(1, 0, 2, 3).reshape(L, B_all, 6 * D)
    off = n_dmod
    for name, sz in zip(small_names + ("loss",), small_sizes[1:]):
        grad[name] = small_sum[off:off + sz]
        off += sz
    loss_total = grad.pop("loss").reshape(())
    for name, ref in (("norm1_g", norm1_g), ("norm2_g", norm2_g), ("q_norm_a", q_norm_a),
                      ("k_norm_a", k_norm_a), ("sinks_a", sinks_a)):
        grad[name] = grad[name].reshape(ref.shape)

    n_shard = ada_w.shape[2]
    dmod_shard = lax.dynamic_slice_in_dim(dmod_all, chip * n_shard, n_shard, axis=2)
    grad["ada_w"], gb = _ada_bwd(c_all, dmod_all, dmod_shard)
    grad["ada_b"] = gb.reshape(ada_b.shape)

    weights = dict(ada_w=ada_w, ada_b=ada_b, norm1_g=norm1_g, norm2_g=norm2_g, wqkv_a=wqkv_a, q_norm_a=q_norm_a,
                   k_norm_a=k_norm_a, sinks_a=sinks_a, wo_a=wo_a, wqkv_b=wqkv_b, wo_b=wo_b, w_gate=w_gate,
                   w_up=w_up, w_down=w_down)
    m_in = dict(ada_w=m_ada_w, ada_b=m_ada_b, norm1_g=m_norm1_g, norm2_g=m_norm2_g, wqkv_a=m_wqkv_a,
                q_norm_a=m_q_norm_a, k_norm_a=m_k_norm_a, sinks_a=m_sinks_a, wo_a=m_wo_a, wqkv_b=m_wqkv_b,
                wo_b=m_wo_b, w_gate=m_w_gate, w_up=m_w_up, w_down=m_w_down)
    v_in = dict(ada_w=v_ada_w, ada_b=v_ada_b, norm1_g=v_norm1_g, norm2_g=v_norm2_g, wqkv_a=v_wqkv_a,
                q_norm_a=v_q_norm_a, k_norm_a=v_k_norm_a, sinks_a=v_sinks_a, wo_a=v_wo_a, wqkv_b=v_wqkv_b,
                wo_b=v_wo_b, w_gate=v_w_gate, w_up=v_w_up, w_down=v_w_down)
    names = list(weights)
    delta, new_m, new_v = {}, {}, {}
    for name in names:
        delta[name], new_m[name], new_v[name] = _adamw(weights[name], grad[name], m_in[name], v_in[name],
                                                       "adamw_" + name)
    return (loss_total, grad_x, *[grad[k] for k in names], *[delta[k] for k in names],
            *[new_m[k] for k in names], *[new_v[k] for k in names])
```

```python
import jax
import jax.numpy as jnp
from jax import lax
from jax.experimental import pallas as pl
from jax.experimental.pallas import tpu as pltpu

F32 = jnp.float32
BF16 = jnp.bfloat16

DEPTH = 4
HEAD_DIM = 64
N_Q_A = 16
N_KV_A = 2
GROUP_A = N_Q_A // N_KV_A
N_H_B = 16
BLOCK = 128
ROT_DIM = HEAD_DIM // 4
ROPE_THETA = 500000.0
EPS = 1e-6
ATTN_SCALE = HEAD_DIM ** -0.5
NEG_BIG = -1e30

ADAM_LR = 0.001
ADAM_B1 = 0.9
ADAM_B2 = 0.999
ADAM_EPS = 1e-08
ADAM_WD = 0.01
ADAM_STEP = 10

N_DEV = 8
LANES = 128
PACK_COLS = 1024
VMEM_LIMIT_BYTES = 48 * 1024 * 1024
MESH = pl.DeviceIdType.MESH

_NT = (((1,), (1,)), ((), ()))
_TN = (((0,), (0,)), ((), ()))
_NN = (((1,), (0,)), ((), ()))


def _params(sem=None):
    return pltpu.CompilerParams(vmem_limit_bytes=VMEM_LIMIT_BYTES, dimension_semantics=sem)


def _pick(n, cap, mult):
    best = None
    for t in range(mult, min(n, cap) + 1, mult):
        if n % t == 0:
            best = t
    return n if best is None else best


_ANY = pl.BlockSpec(memory_space=pl.ANY)


def _window(index, axis, q, n, shape):
    rest = [slice(None)] * len(shape)
    size = shape[axis] // n
    rest[axis] = pl.ds(q * size, size)
    return tuple(index) + tuple(rest)


def _all_gather8(xs, name, local_axis=0, local_chunks=1):
    n = len(xs)

    def body(*refs):
        x_refs, out_refs = refs[:n], refs[n:2 * n]
        send_sems, recv_sems, local_sems = refs[2 * n:]
        xi, yi, ci = lax.axis_index("x"), lax.axis_index("y"), lax.axis_index("c")
        me, sibling = (xi, yi, ci), (xi, yi, 1 - ci)
        chips = [(1 - xi, yi), (xi, 1 - yi), (1 - xi, 1 - yi)]

        def slab(w, px, py, pc):
            return out_refs[w].at[4 * px + 2 * py + pc]

        def copy(w, k, block, to, src=None):
            return pltpu.make_async_remote_copy(
                src_ref=slab(w, *block) if src is None else src, dst_ref=slab(w, *block),
                send_sem=send_sems.at[k, w], recv_sem=recv_sems.at[k, w], device_id=to, device_id_type=MESH)

        mine = []
        for w in range(n):
            for q in range(local_chunks):
                part = _window((), local_axis, q, local_chunks, xs[w].shape)
                mine.append(pltpu.make_async_copy(x_refs[w].at[part], slab(w, *me).at[part], local_sems.at[w, q]))
                mine[-1].start()
        first = [copy(w, 0, me, sibling, src=x_refs[w]) for w in range(n)]
        first += [copy(w, 1 + j, me, (*chip, ci), src=x_refs[w]) for j, chip in enumerate(chips) for w in range(n)]
        for cp in first:
            cp.start()
        passed = []
        for j, chip in enumerate(chips):
            for w in range(n):
                copy(w, 1 + j, (*chip, ci), me).wait_recv()
                passed.append(copy(w, 4 + j, (*chip, ci), sibling))
                passed[-1].start()
        for w in range(n):
            copy(w, 0, sibling, me).wait_recv()
        for j, chip in enumerate(chips):
            for w in range(n):
                copy(w, 4 + j, (*chip, 1 - ci), me).wait_recv()
        for cp in first + passed:
            cp.wait_send()
        for cp in mine:
            cp.wait()

    return pl.pallas_call(
        body, name=name,
        out_shape=[jax.ShapeDtypeStruct((N_DEV,) + x.shape, x.dtype) for x in xs],
        in_specs=[_ANY] * n, out_specs=[_ANY] * n,
        scratch_shapes=[pltpu.SemaphoreType.DMA((7, n)), pltpu.SemaphoreType.DMA((7, n)),
                        pltpu.SemaphoreType.DMA((n, local_chunks))],
    )(*xs)


def _exchange(xs, group, name, chunk_axis=0, chunks=1):
    n = len(xs)
    n_peers = 1 if group == "c" else 3

    def body(*refs):
        x_refs, out_refs = refs[:n], refs[n:2 * n]
        send_sems, recv_sems = refs[2 * n:]
        xi, yi, ci = lax.axis_index("x"), lax.axis_index("y"), lax.axis_index("c")
        if group == "c":
            peers = [(1 - ci, (xi, yi, 1 - ci))]
        else:
            peers = [(2 * (1 - xi) + yi, (1 - xi, yi, ci)),
                     (2 * xi + (1 - yi), (xi, 1 - yi, ci)),
                     (2 * (1 - xi) + (1 - yi), (1 - xi, 1 - yi, ci))]
        copies = []
        for k, (p, dev) in enumerate(peers):
            for w in range(n):
                slab_shape = xs[w].shape[1:]
                for q in range(chunks):
                    copies.append(pltpu.make_async_remote_copy(
                        src_ref=x_refs[w].at[_window((p,), chunk_axis, q, chunks, slab_shape)],
                        dst_ref=out_refs[w].at[_window((k,), chunk_axis, q, chunks, slab_shape)],
                        send_sem=send_sems.at[k, w, q], recv_sem=recv_sems.at[k, w, q],
                        device_id=dev, device_id_type=MESH))
                    copies[-1].start()
        for cp in copies:
            cp.wait()

    return pl.pallas_call(
        body, name=name,
        out_shape=[jax.ShapeDtypeStruct((n_peers,) + x.shape[1:], x.dtype) for x in xs],
        in_specs=[_ANY] * n, out_specs=[_ANY] * n,
        scratch_shapes=[pltpu.SemaphoreType.DMA((n_peers, n, chunks)), pltpu.SemaphoreType.DMA((n_peers, n, chunks))],
    )(*xs)


def _sibling_send(xs, name, chunk_axis=1, chunks=4):
    n = len(xs)

    def body(*refs):
        x_refs, out_refs = refs[:n], refs[n:2 * n]
        send_sems, recv_sems = refs[2 * n:]
        xi, yi, ci = lax.axis_index("x"), lax.axis_index("y"), lax.axis_index("c")
        copies = []
        for w in range(n):
            for q in range(chunks):
                part = _window((), chunk_axis, q, chunks, xs[w].shape)
                copies.append(pltpu.make_async_remote_copy(
                    src_ref=x_refs[w].at[part], dst_ref=out_refs[w].at[part],
                    send_sem=send_sems.at[w, q], recv_sem=recv_sems.at[w, q],
                    device_id=(xi, yi, 1 - ci), device_id_type=MESH))
                copies[-1].start()
        for cp in copies:
            cp.wait()

    return pl.pallas_call(
        body, name=name,
        out_shape=[jax.ShapeDtypeStruct(x.shape, x.dtype) for x in xs],
        in_specs=[_ANY] * n, out_specs=[_ANY] * n,
        scratch_shapes=[pltpu.SemaphoreType.DMA((n, chunks)), pltpu.SemaphoreType.DMA((n, chunks))],
    )(*xs)


def _sum_leading(x, name, own=None, with_bf16=False):
    P, R, C = x.shape
    tr = _pick(R, max(16, (1 << 19) // (C * (P + 1))), 16)

    def body(*refs):
        n_in = 1 if own is None else 2
        x_ref = refs[n_in - 1]
        acc = x_ref[0].astype(F32) if own is None else refs[0][...] + x_ref[0].astype(F32)
        for p in range(1, P):
            acc = acc + x_ref[p].astype(F32)
        refs[n_in][...] = acc
        if with_bf16:
            refs[n_in + 1][...] = acc.astype(BF16)

    flat = pl.BlockSpec((tr, C), lambda r: (r, 0))
    slabs = pl.BlockSpec((P, tr, C), lambda r: (0, r, 0))
    out = pl.pallas_call(
        body, name=name, grid=(R // tr,),
        in_specs=[slabs] if own is None else [flat, slabs],
        out_specs=[flat, flat] if with_bf16 else [flat],
        out_shape=[jax.ShapeDtypeStruct((R, C), F32)] + ([jax.ShapeDtypeStruct((R, C), BF16)] if with_bf16 else []),
        compiler_params=_params(("arbitrary",)),
    )(*([x] if own is None else [own, x]))
    return out if with_bf16 else out[0]


MATMUL_SINGLE_K = 1280


def _matmul(a, b, mode, out_dtype, name, swiglu=False):
    if mode == "nn":
        (M, K), N = a.shape, b.shape[1]
    elif mode == "nt":
        (M, K), N = a.shape, b.shape[0]
    else:
        (K, M), N = a.shape, b.shape[1]
    tm = _pick(M, 1024 if mode != "tn" else 1536, 128)
    tn = _pick(N, 1536, 128)
    if swiglu:
        tm, tn = _pick(M, 512, 128), 2 * _ff_tile(N // 2)
    tk = K if K <= MATMUL_SINGLE_K else _pick(K, 512, 128)
    nk = K // tk
    dims = {"nn": _NN, "nt": _NT, "tn": _TN}[mode]
    use_scratch = nk > 1 and out_dtype != F32

    def body(a_ref, b_ref, *refs):
        o_ref = refs[0]
        part = lax.dot_general(a_ref[...].astype(BF16), b_ref[...].astype(BF16), dims, preferred_element_type=F32)
        if nk == 1:
            o_ref[...] = part.astype(o_ref.dtype)
            if swiglu:
                g = part[:, :tn // 2]
                refs[1][...] = (g * _sigmoid(g) * part[:, tn // 2:]).astype(BF16)
            return
        k = pl.program_id(2)
        acc_ref = refs[-1] if use_scratch else o_ref

        @pl.when(k == 0)
        def _():
            acc_ref[...] = part

        @pl.when(k > 0)
        def _():
            acc_ref[...] += part

        if use_scratch:
            @pl.when(k == nk - 1)
            def _():
                o_ref[...] = acc_ref[...].astype(o_ref.dtype)

    if mode == "tn":
        a_spec = pl.BlockSpec((tk, tm), lambda i, j, k: (k, i))
    else:
        a_spec = pl.BlockSpec((tm, tk), lambda i, j, k: (i, k))
    if mode == "nt":
        b_spec = pl.BlockSpec((tn, tk), lambda i, j, k: (j, k))
    else:
        b_spec = pl.BlockSpec((tk, tn), lambda i, j, k: (k, j))
    out_specs = [pl.BlockSpec((tm, tn), lambda i, j, k: (i, j))]
    out_shape = [jax.ShapeDtypeStruct((M, N), out_dtype)]
    if swiglu:
        assert nk == 1 and mode == "nn"
        out_specs.append(pl.BlockSpec((tm, tn // 2), lambda i, j, k: (i, j)))
        out_shape.append(jax.ShapeDtypeStruct((M, N // 2), BF16))
    out = pl.pallas_call(
        body, name=name, grid=(M // tm, N // tn, nk),
        in_specs=[a_spec, b_spec], out_specs=out_specs, out_shape=out_shape,
        scratch_shapes=[pltpu.VMEM((tm, tn), F32)] if use_scratch else [],
        compiler_params=_params(("parallel", "parallel", "arbitrary")),
    )(a, b)
    return out if swiglu else out[0]


def _row_tile(S):
    return _pick(S, 512, 8)


def _norm_mod_fwd(x, gain, sc, sh):
    NB, S, D = x.shape
    tr = _row_tile(S)

    def body(x_ref, g_ref, sc_ref, sh_ref, h_ref):
        xv = x_ref[...]
        ms = jnp.mean(xv * xv, axis=-1, keepdims=True)
        n = xv * lax.rsqrt(ms + EPS) * g_ref[...]
        h_ref[...] = (n * (1.0 + sc_ref[...]) + sh_ref[...]).astype(BF16)

    tok = pl.BlockSpec((None, tr, D), lambda b, r: (b, r, 0))
    per_ex = pl.BlockSpec((None, 1, D), lambda b, r: (b, 0, 0))
    return pl.pallas_call(
        body, name="norm_mod_fwd", grid=(NB, S // tr),
        in_specs=[tok, pl.BlockSpec((1, D), lambda b, r: (0, 0)), per_ex, per_ex],
        out_specs=tok, out_shape=jax.ShapeDtypeStruct((NB, S, D), BF16),
        compiler_params=_params(("parallel", "parallel")),
    )(x, gain, sc, sh)


def _norm_mod_bwd(x, gain, sc, dh, dres):
    NB, S, D = x.shape
    tr = _row_tile(S)

    def body(x_ref, g_ref, sc_ref, dh_ref, dres_ref, dx_ref, dsh_ref, dsc_ref, dg_ref):
        b, r = pl.program_id(0), pl.program_id(1)

        @pl.when(r == 0)
        def _():
            dsh_ref[...] = jnp.zeros_like(dsh_ref)
            dsc_ref[...] = jnp.zeros_like(dsc_ref)

        @pl.when((r == 0) & (b == 0))
        def _():
            dg_ref[...] = jnp.zeros_like(dg_ref)

        xv = x_ref[...]
        rstd = lax.rsqrt(jnp.mean(xv * xv, axis=-1, keepdims=True) + EPS)
        xh = xv * rstd
        g = g_ref[...]
        dh = dh_ref[...]
        dsh_ref[...] += jnp.sum(dh, axis=0, keepdims=True)
        dsc_ref[...] += jnp.sum(dh * (xh * g), axis=0, keepdims=True)
        dn = dh * (1.0 + sc_ref[...])
        dg_ref[...] += jnp.sum(dn * xh, axis=0, keepdims=True)
        dxh = dn * g
        proj = jnp.mean(dxh * xh, axis=-1, keepdims=True)
        dx_ref[...] = rstd * (dxh - xh * proj) + dres_ref[...]

    tok = pl.BlockSpec((None, tr, D), lambda b, r: (b, r, 0))
    per_ex = pl.BlockSpec((None, 1, D), lambda b, r: (b, 0, 0))
    row = pl.BlockSpec((1, D), lambda b, r: (0, 0))
    return pl.pallas_call(
        body, name="norm_mod_bwd", grid=(NB, S // tr),
        in_specs=[tok, row, per_ex, tok, tok],
        out_specs=[tok, per_ex, per_ex, row],
        out_shape=[jax.ShapeDtypeStruct((NB, S, D), F32), jax.ShapeDtypeStruct((NB, 1, D), F32),
                   jax.ShapeDtypeStruct((NB, 1, D), F32), jax.ShapeDtypeStruct((1, D), F32)],
        compiler_params=_params(("arbitrary", "arbitrary")),
    )(x, gain, sc, dh, dres)


def _gate_res(x, y, g):
    NB, S, D = x.shape
    tr = _row_tile(S)

    def body(x_ref, y_ref, g_ref, o_ref):
        o_ref[...] = x_ref[...] + g_ref[...] * y_ref[...]

    tok = pl.BlockSpec((None, tr, D), lambda b, r: (b, r, 0))
    per_ex = pl.BlockSpec((None, 1, D), lambda b, r: (b, 0, 0))
    return pl.pallas_call(
        body, name="gate_res", grid=(NB, S // tr), in_specs=[tok, tok, per_ex], out_specs=tok,
        out_shape=jax.ShapeDtypeStruct((NB, S, D), F32),
        compiler_params=_params(("parallel", "parallel")),
    )(x, y, g)


def _gate_res_bwd(dxo, y, g):
    NB, S, D = dxo.shape
    tr = _row_tile(S)

    def body(d_ref, y_ref, g_ref, dy_ref, dg_ref):
        @pl.when(pl.program_id(1) == 0)
        def _():
            dg_ref[...] = jnp.zeros_like(dg_ref)

        d = d_ref[...]
        dy_ref[...] = (d * g_ref[...]).astype(BF16)
        dg_ref[...] += jnp.sum(d * y_ref[...], axis=0, keepdims=True)

    tok = pl.BlockSpec((None, tr, D), lambda b, r: (b, r, 0))
    per_ex = pl.BlockSpec((None, 1, D), lambda b, r: (b, 0, 0))
    return pl.pallas_call(
        body, name="gate_res_bwd", grid=(NB, S // tr), in_specs=[tok, tok, per_ex], out_specs=[tok, per_ex],
        out_shape=[jax.ShapeDtypeStruct((NB, S, D), BF16), jax.ShapeDtypeStruct((NB, 1, D), F32)],
        compiler_params=_params(("arbitrary", "arbitrary")),
    )(dxo, y, g)


def _sigmoid(v):
    return 1.0 / (1.0 + jnp.exp(-v))


def _ff_tile(F):
    return _pick(F, 1536, 128)


def _interleave(gate, up):
    F = gate.shape[-1]
    tf = _ff_tile(F)
    parts = []
    for j in range(F // tf):
        parts += [gate[..., j * tf:(j + 1) * tf], up[..., j * tf:(j + 1) * tf]]
    return jnp.concatenate(parts, axis=-1)


def _deinterleave(gu):
    F = gu.shape[-1] // 2
    tf = _ff_tile(F)
    gate = [gu[..., 2 * j * tf:(2 * j + 1) * tf] for j in range(F // tf)]
    up = [gu[..., (2 * j + 1) * tf:(2 * j + 2) * tf] for j in range(F // tf)]
    return jnp.concatenate(gate, axis=-1), jnp.concatenate(up, axis=-1)


def _swiglu_bwd(gu, dact):
    T, F2 = gu.shape
    F = F2 // 2
    tf = _ff_tile(F)
    tr = _pick(T, 256, 8)

    def body(gu_ref, d_ref, o_ref):
        g, u, d = gu_ref[:, :tf], gu_ref[:, tf:], d_ref[...]
        s = _sigmoid(g)
        o_ref[:, :tf] = (d * u * (s * (1.0 + g * (1.0 - s)))).astype(BF16)
        o_ref[:, tf:] = (d * (g * s)).astype(BF16)

    return pl.pallas_call(
        body, name="swiglu_bwd", grid=(T // tr, F // tf),
        in_specs=[pl.BlockSpec((tr, 2 * tf), lambda i, j: (i, j)), pl.BlockSpec((tr, tf), lambda i, j: (i, j))],
        out_specs=pl.BlockSpec((tr, 2 * tf), lambda i, j: (i, j)),
        out_shape=jax.ShapeDtypeStruct((T, F2), BF16),
        compiler_params=_params(("parallel", "parallel")),
    )(gu, dact)


def _loss_fwd_bwd(y, target):
    NB, S, D = y.shape
    tr = _row_tile(S)

    def body(y_ref, t_ref, l_ref, d_ref):
        @pl.when((pl.program_id(0) == 0) & (pl.program_id(1) == 0))
        def _():
            l_ref[...] = jnp.zeros_like(l_ref)

        e = y_ref[...] - t_ref[...]
        d_ref[...] = e / D
        l_ref[...] += 0.5 * jnp.sum(jnp.mean(e * e, axis=-1, keepdims=True), axis=0, keepdims=True)

    tok = pl.BlockSpec((None, tr, D), lambda b, r: (b, r, 0))
    return pl.pallas_call(
        body, name="loss", grid=(NB, S // tr), in_specs=[tok, tok],
        out_specs=[pl.BlockSpec((1, 1), lambda b, r: (0, 0)), tok],
        out_shape=[jax.ShapeDtypeStruct((1, 1), F32), jax.ShapeDtypeStruct((NB, S, D), F32)],
        compiler_params=_params(("arbitrary", "arbitrary")),
    )(y, target)


def _half_sums(v, lo):
    sa = jnp.sum(jnp.where(lo, v, 0.0), axis=-1, keepdims=True)
    sb = jnp.sum(jnp.where(lo, 0.0, v), axis=-1, keepdims=True)
    return jnp.where(lo, sa, sb)


def _rope_swap(v, lane64):
    up = pltpu.roll(v, LANES - ROT_DIM // 2, 1)
    down = pltpu.roll(v, ROT_DIM // 2, 1)
    return jnp.where(lane64 < ROT_DIM // 2, up, jnp.where(lane64 < ROT_DIM, down, 0.0))


def _qk_prep_fwd(qkv, tab_c, tab_s, gains):
    T, W = qkv.shape
    R = W // LANES
    tt = _pick(T, 256, 8)

    def body(x_ref, c_ref, s_ref, g_ref, o_ref):
        lane = lax.broadcasted_iota(jnp.int32, (tt, LANES), 1)
        lo = lane < HEAD_DIM
        lane64 = lane & (HEAD_DIM - 1)
        c, s = c_ref[...], s_ref[...]
        for j in range(R - 1):
            cols = slice(j * LANES, (j + 1) * LANES)
            xv = x_ref[:, cols]
            rstd = lax.rsqrt(_half_sums(xv * xv, lo) / HEAD_DIM + EPS)
            yn = xv * rstd * g_ref[j:j + 1, :]
            o_ref[:, cols] = (yn * c + _rope_swap(yn, lane64) * s).astype(BF16)
        o_ref[:, (R - 1) * LANES:] = x_ref[:, (R - 1) * LANES:].astype(BF16)

    tok = pl.BlockSpec((tt, W), lambda t: (t, 0))
    tab = pl.BlockSpec((tt, LANES), lambda t: (t, 0))
    return pl.pallas_call(
        body, name="qk_prep_fwd", grid=(T // tt,),
        in_specs=[tok, tab, tab, pl.BlockSpec((R, LANES), lambda t: (0, 0))],
        out_specs=tok, out_shape=jax.ShapeDtypeStruct((T, W), BF16),
        compiler_params=_params(("parallel",)),
    )(qkv, tab_c, tab_s, gains)


def _qk_prep_bwd(qkv, dq, dk, dv, tab_c, tab_s, gains):
    T, W = qkv.shape
    R = W // LANES
    QW = dq.shape[1]
    tt = _pick(T, 256, 8)

    def body(x_ref, dq_ref, dk_ref, dv_ref, c_ref, s_ref, g_ref, o_ref, dg_ref):
        @pl.when(pl.program_id(0) == 0)
        def _():
            dg_ref[...] = jnp.zeros_like(dg_ref)

        lane = lax.broadcasted_iota(jnp.int32, (tt, LANES), 1)
        lo = lane < HEAD_DIM
        lane64 = lane & (HEAD_DIM - 1)
        c, s = c_ref[...], s_ref[...]
        for j in range(R - 1):
            cols = slice(j * LANES, (j + 1) * LANES)
            xv = x_ref[:, cols]
            d = dq_ref[:, cols] if j < R - 2 else dk_ref[...]
            rstd = lax.rsqrt(_half_sums(xv * xv, lo) / HEAD_DIM + EPS)
            xh = xv * rstd
            dyn = d * c + _rope_swap(d * s, lane64)
            dg_ref[j:j + 1, :] += jnp.sum(dyn * xh, axis=0, keepdims=True)
            dxh = dyn * g_ref[j:j + 1, :]
            proj = _half_sums(dxh * xh, lo) / HEAD_DIM
            o_ref[:, cols] = (rstd * (dxh - xh * proj)).astype(BF16)
        o_ref[:, (R - 1) * LANES:] = dv_ref[...].astype(BF16)

    tok = pl.BlockSpec((tt, W), lambda t: (t, 0))
    tab = pl.BlockSpec((tt, LANES), lambda t: (t, 0))
    gsp = pl.BlockSpec((R, LANES), lambda t: (0, 0))
    return pl.pallas_call(
        body, name="qk_prep_bwd", grid=(T // tt,),
        in_specs=[tok, pl.BlockSpec((tt, QW), lambda t: (t, 0)), tab, tab, tab, tab, gsp], out_specs=[tok, gsp],
        out_shape=[jax.ShapeDtypeStruct((T, W), BF16), jax.ShapeDtypeStruct((R, LANES), F32)],
        compiler_params=_params(("arbitrary",)),
    )(qkv, dq, dk, dv, tab_c, tab_s, gains)


def _band_mask(i):
    r = lax.broadcasted_iota(jnp.int32, (2 * BLOCK, 2 * BLOCK), 0) & (BLOCK - 1)
    c = lax.broadcasted_iota(jnp.int32, (2 * BLOCK, 2 * BLOCK), 1)
    rel = r + BLOCK - c
    return (rel >= 0) & (rel < BLOCK) & ((c >= BLOCK) | (i > 0))


def _swa_softmax(s, valid, sink):
    s = jnp.where(valid, s * ATTN_SCALE, NEG_BIG)
    m = jnp.maximum(jnp.max(s, axis=1, keepdims=True), sink)
    p = jnp.exp(s - m)
    ps = jnp.exp(sink - m)
    denom = jnp.sum(p, axis=1, keepdims=True) + ps
    return p / denom, ps / denom


A_GROUP = 2


Q_WIDTH_A = N_Q_A * HEAD_DIM
N_PAIR_A = Q_WIDTH_A // LANES


def _swa_specs():
    qs = pl.BlockSpec((None, BLOCK, Q_WIDTH_A), lambda b, i: (b, i, 0))

    def kv(col, back):
        return pl.BlockSpec((None, BLOCK, LANES), lambda b, i: (b, jnp.maximum(i - back, 0), col))

    return qs, kv(N_PAIR_A, 1), kv(N_PAIR_A, 0), kv(N_PAIR_A + 1, 1), kv(N_PAIR_A + 1, 0)


def _dup_heads(t):
    lo = lax.broadcasted_iota(jnp.int32, t.shape, 1) < HEAD_DIM
    sw = pltpu.roll(t.astype(F32), HEAD_DIM, 1).astype(BF16)
    return jnp.where(lo, t, sw), jnp.where(lo, sw, t)


def _kv_tiles(kp_ref, kc_ref, vp_ref, vc_ref):
    kd = _dup_heads(jnp.concatenate([kp_ref[...], kc_ref[...]], axis=0))
    vd = _dup_heads(jnp.concatenate([vp_ref[...], vc_ref[...]], axis=0))
    return kd, vd


def _attn_a_fwd(qkn, sinks):
    NB, S, _ = qkn.shape
    qs, kp, kc, vp, vc = _swa_specs()

    def body(q_ref, kp_ref, kc_ref, vp_ref, vc_ref, sink_ref, o_ref):
        i = pl.program_id(1)
        kd, vd = _kv_tiles(kp_ref, kc_ref, vp_ref, vc_ref)
        valid = _band_mask(i)
        lo = lax.broadcasted_iota(jnp.int32, (BLOCK, LANES), 1) < HEAD_DIM
        top = lax.broadcasted_iota(jnp.int32, (2 * BLOCK, 1), 0) < BLOCK
        for first in range(0, N_PAIR_A, A_GROUP):
            pairs = range(first, first + A_GROUP)
            qs_ = [jnp.concatenate(_head_halves(q_ref[:, p * LANES:(p + 1) * LANES], lo), axis=0) for p in pairs]
            ss = [lax.dot_general(q, kd[2 * p // GROUP_A], _NT, preferred_element_type=F32) for q, p in zip(qs_, pairs)]
            pns = [_swa_softmax(s, valid, jnp.where(top, sink_ref[2 * p], sink_ref[2 * p + 1]))[0]
                   for s, p in zip(ss, pairs)]
            pvs = [jnp.dot(pn.astype(BF16), vd[2 * p // GROUP_A], preferred_element_type=F32) for pn, p in zip(pns, pairs)]
            for pv, p in zip(pvs, pairs):
                o_ref[:, p * LANES:(p + 1) * LANES] = jnp.where(lo, pv[:BLOCK], pv[BLOCK:]).astype(BF16)

    return pl.pallas_call(
        body, name="attn_a_fwd", grid=(NB, S // BLOCK),
        in_specs=[qs, kp, kc, vp, vc, pl.BlockSpec(memory_space=pltpu.SMEM)],
        out_specs=qs, out_shape=jax.ShapeDtypeStruct((NB, S, Q_WIDTH_A), BF16),
        compiler_params=_params(("parallel", "arbitrary")),
    )(qkn, qkn, qkn, qkn, qkn, sinks)


def _attn_a_bwd(qkn, do, sinks):
    NB, S, _ = qkn.shape
    qs, kp, kc, vp, vc = _swa_specs()
    full = pl.BlockSpec((None, S, LANES), lambda b, i: (b, 0, 0))
    sink_out = pl.BlockSpec((None, N_Q_A, LANES), lambda b, i: (b, 0, 0))

    def body(q_ref, do_ref, kp_ref, kc_ref, vp_ref, vc_ref, sink_ref, dq_ref, dk_ref, dv_ref, ds_ref, dk_s, dv_s):
        i = pl.program_id(1)

        @pl.when(i == 0)
        def _():
            dk_ref[...] = jnp.zeros_like(dk_ref)
            dv_ref[...] = jnp.zeros_like(dv_ref)
            ds_ref[...] = jnp.zeros_like(ds_ref)

        dk_s[...] = jnp.zeros_like(dk_s)
        dv_s[...] = jnp.zeros_like(dv_s)
        kd, vd = _kv_tiles(kp_ref, kc_ref, vp_ref, vc_ref)
        valid = _band_mask(i)
        lo = lax.broadcasted_iota(jnp.int32, (BLOCK, LANES), 1) < HEAD_DIM
        top = lax.broadcasted_iota(jnp.int32, (2 * BLOCK, 1), 0) < BLOCK
        for first in range(0, N_PAIR_A, A_GROUP):
            pairs = range(first, first + A_GROUP)
            kvs = [2 * p // GROUP_A for p in pairs]
            qs_ = [jnp.concatenate(_head_halves(q_ref[:, p * LANES:(p + 1) * LANES], lo), axis=0) for p in pairs]
            dos = [jnp.concatenate(_head_halves(do_ref[:, p * LANES:(p + 1) * LANES], lo), axis=0) for p in pairs]
            ss = [lax.dot_general(q, kd[kv], _NT, preferred_element_type=F32) for q, kv in zip(qs_, kvs)]
            dps = [lax.dot_general(d, vd[kv], _NT, preferred_element_type=F32) for d, kv in zip(dos, kvs)]
            sm = [_swa_softmax(s, valid, jnp.where(top, sink_ref[2 * p], sink_ref[2 * p + 1])) for s, p in zip(ss, pairs)]
            deltas = [jnp.sum(pn * dp, axis=1, keepdims=True) for (pn, _), dp in zip(sm, dps)]
            dsbs = [(pn * (dp - delta) * ATTN_SCALE).astype(BF16) for (pn, _), dp, delta in zip(sm, dps, deltas)]
            for n, p in enumerate(pairs):
                dq2 = jnp.dot(dsbs[n], kd[kvs[n]], preferred_element_type=F32)
                dq_ref[:, p * LANES:(p + 1) * LANES] = jnp.where(lo, dq2[:BLOCK], dq2[BLOCK:])
                dk_s[kvs[n]] += lax.dot_general(dsbs[n], qs_[n], _TN, preferred_element_type=F32)
                dv_s[kvs[n]] += lax.dot_general(sm[n][0].astype(BF16), dos[n], _TN, preferred_element_type=F32)
                t = sm[n][1] * deltas[n]
                for hh in range(2):
                    dsink = -jnp.sum(t[hh * BLOCK:(hh + 1) * BLOCK], axis=0, keepdims=True)
                    ds_ref[2 * p + hh:2 * p + hh + 1, :] += jnp.broadcast_to(dsink, (1, LANES))

        lo2 = lax.broadcasted_iota(jnp.int32, (2 * BLOCK, LANES), 1) < HEAD_DIM

        def fold(acc):
            halves = [acc[kv] + pltpu.roll(acc[kv], HEAD_DIM, 1) for kv in range(N_KV_A)]
            return jnp.where(lo2, halves[0], halves[1])

        dk2, dv2 = fold(dk_s), fold(dv_s)

        @pl.when(i > 0)
        def _():
            start = pl.multiple_of((i - 1) * BLOCK, BLOCK)
            dk_ref[pl.ds(start, 2 * BLOCK), :] += dk2
            dv_ref[pl.ds(start, 2 * BLOCK), :] += dv2

        @pl.when(i == 0)
        def _():
            dk_ref[0:BLOCK, :] += dk2[BLOCK:, :]
            dv_ref[0:BLOCK, :] += dv2[BLOCK:, :]

    slots = pltpu.VMEM((N_KV_A, 2 * BLOCK, LANES), F32)
    return pl.pallas_call(
        body, name="attn_a_bwd", grid=(NB, S // BLOCK),
        in_specs=[qs, qs, kp, kc, vp, vc, pl.BlockSpec(memory_space=pltpu.SMEM)],
        out_specs=[qs, full, full, sink_out],
        out_shape=[jax.ShapeDtypeStruct((NB, S, Q_WIDTH_A), F32), jax.ShapeDtypeStruct((NB, S, LANES), F32),
                   jax.ShapeDtypeStruct((NB, S, LANES), F32), jax.ShapeDtypeStruct((NB, N_Q_A, LANES), F32)],
        scratch_shapes=[slots, slots],
        compiler_params=_params(("parallel", "arbitrary")),
    )(qkn, do, qkn, qkn, qkn, qkn, sinks)


def _cumsum_mats():
    src = lax.broadcasted_iota(jnp.int32, (2 * BLOCK, 2 * BLOCK), 0) % BLOCK
    dst = lax.broadcasted_iota(jnp.int32, (2 * BLOCK, 2 * BLOCK), 1)
    ones = dst >= BLOCK
    rev = ((src > dst) | ones).astype(BF16)
    fwd = ((src < dst) | ones).astype(BF16)
    return rev, fwd


def _log_sigmoids(z):
    sp = jnp.log(1.0 + jnp.exp(-jnp.abs(z)))
    return jnp.minimum(z, 0.0) - sp, -(jnp.maximum(z, 0.0) + sp)


def _cumsum_mxu_many(vs, mat):
    parts = []
    for v in vs:
        hi = v.astype(BF16)
        parts.append(jnp.concatenate([hi, (v - hi.astype(F32)).astype(BF16)], axis=1))
    r = jnp.dot(jnp.concatenate(parts, axis=0), mat, preferred_element_type=F32)
    return [(r[n * BLOCK:(n + 1) * BLOCK, :BLOCK], r[n * BLOCK:(n + 1) * BLOCK, BLOCK:]) for n in range(len(vs))]


def _strict_mask():
    r = lax.broadcasted_iota(jnp.int32, (BLOCK, BLOCK), 0)
    c = lax.broadcasted_iota(jnp.int32, (BLOCK, BLOCK), 1)
    return c < r


def _tile(ref, j):
    return ref[pl.ds(pl.multiple_of(j * BLOCK, BLOCK), BLOCK), :]


SWEEP_EXIT = -88.0


def _head_halves(t, lo):
    zero = jnp.zeros_like(t)
    return jnp.where(lo, t, zero), jnp.where(lo, zero, t)


def _sb_specs(S, HD, width):
    n = HD // width
    blk = pl.BlockSpec((None, BLOCK, width), lambda b, p, i: (b, i, p))
    k_full = pl.BlockSpec((None, S, width), lambda b, p, i: (b, 0, n + p))
    v_full = pl.BlockSpec((None, S, width), lambda b, p, i: (b, 0, 2 * n + p))
    mat = pl.BlockSpec((2 * BLOCK, 2 * BLOCK), lambda b, p, i: (0, 0))
    return blk, k_full, v_full, mat


SB_FWD_PAIRS = 2
SB_BWD_PAIRS = 2


def _attn_b_fwd(qkv, rev):
    NB, S, W = qkv.shape
    HD = W // 3
    width = SB_FWD_PAIRS * LANES
    n_heads = 2 * SB_FWD_PAIRS
    blk, k_full, v_full, mat = _sb_specs(S, HD, width)

    def body(q_ref, k_ref, v_ref, rev_ref, o_ref):
        i = pl.program_id(2)
        rv = rev_ref[...]
        mask = _strict_mask()
        lo = lax.broadcasted_iota(jnp.int32, (BLOCK, LANES), 1) < HEAD_DIM
        q_all = q_ref[...]
        q_stack = [jnp.concatenate(_head_halves(q_all[:, p * LANES:(p + 1) * LANES] * ATTN_SCALE, lo), axis=0)
                   for p in range(SB_FWD_PAIRS)]

        def pair_tiles(ref, j):
            t = _tile(ref, j)
            return [t[:, p * LANES:(p + 1) * LANES] for p in range(SB_FWD_PAIRS)]

        def tile_pass(j, carries, diagonal):
            ks, vs = pair_tiles(k_ref, j), pair_tiles(v_ref, j)
            zs = []
            for p in range(SB_FWD_PAIRS):
                z2 = lax.dot_general(q_stack[p], ks[p], _NT, preferred_element_type=F32)
                zs += [z2[:BLOCK], z2[BLOCK:]]
            logs = [_log_sigmoids(z) for z in zs]
            cums = _cumsum_mxu_many([jnp.where(mask, lm, 0.0) if diagonal else lm for _, lm in logs], rv)
            probs, new_c = [], []
            for h in range(n_heads):
                after, rs = cums[h]
                if diagonal:
                    a = jnp.where(mask, jnp.exp(logs[h][0] + after), 0.0)
                    new_c.append(rs)
                else:
                    a = jnp.exp(logs[h][0] + after + carries[h])
                    new_c.append(carries[h] + rs)
                probs.append(a.astype(BF16))
            outs = []
            for p in range(SB_FWD_PAIRS):
                pv = jnp.dot(jnp.concatenate(probs[2 * p:2 * p + 2], axis=0), vs[p], preferred_element_type=F32)
                outs.append(jnp.where(lo, pv[:BLOCK], pv[BLOCK:]))
            return new_c, outs

        carries, accs = tile_pass(i, None, True)

        def live(cs):
            top = cs[0]
            for c in cs[1:]:
                top = jnp.maximum(top, c)
            return jnp.max(top) > SWEEP_EXIT

        def cond(st):
            return (st[0] < i) & st[1]

        def step(st):
            jj, _, cs, accs = st
            new_c, outs = tile_pass(i - 1 - jj, cs, False)
            return jj + 1, live(new_c), new_c, [acc + o for acc, o in zip(accs, outs)]

        st = lax.while_loop(cond, step, (jnp.int32(0), live(carries), carries, accs))
        for p in range(SB_FWD_PAIRS):
            o_ref[:, p * LANES:(p + 1) * LANES] = st[3][p].astype(BF16)

    return pl.pallas_call(
        body, name="attn_b_fwd", grid=(NB, HD // width, S // BLOCK),
        in_specs=[blk, k_full, v_full, mat], out_specs=blk,
        out_shape=jax.ShapeDtypeStruct((NB, S, HD), BF16),
        compiler_params=_params(("parallel", "parallel", "arbitrary")),
    )(qkv, qkv, qkv, rev)


def _attn_b_bwd(qkv, do, rev, fwd):
    NB, S, W = qkv.shape
    HD = W // 3
    width = SB_BWD_PAIRS * LANES
    n_heads = 2 * SB_BWD_PAIRS
    nj = S // BLOCK
    blk, k_full, v_full, mat = _sb_specs(S, HD, width)
    acc_full = pl.BlockSpec((None, S, width), lambda b, p, i: (b, 0, p))

    def body(q_ref, do_ref, k_ref, v_ref, rev_ref, fwd_ref, dq_ref, dk_ref, dv_ref, sig_s, a_s, e_s):
        i = pl.program_id(2)

        @pl.when(i == 0)
        def _():
            dk_ref[...] = jnp.zeros_like(dk_ref)
            dv_ref[...] = jnp.zeros_like(dv_ref)

        rv, fw = rev_ref[...], fwd_ref[...]
        mask = _strict_mask()
        lo = lax.broadcasted_iota(jnp.int32, (BLOCK, LANES), 1) < HEAD_DIM
        pairs = range(SB_BWD_PAIRS)

        def cols(p):
            return slice(p * LANES, (p + 1) * LANES)

        q_stack = [jnp.concatenate(_head_halves(q_ref[:, cols(p)], lo), axis=0) for p in pairs]
        qs_stack = [q * ATTN_SCALE for q in q_stack]
        do_stack = [jnp.concatenate(_head_halves(do_ref[:, cols(p)], lo), axis=0) for p in pairs]

        def sweep1_tile(j, carries, diagonal):
            kj, vj = _tile(k_ref, j), _tile(v_ref, j)
            zs, das = [], []
            for p in pairs:
                z2 = lax.dot_general(qs_stack[p], kj[:, cols(p)], _NT, preferred_element_type=F32)
                da2 = lax.dot_general(do_stack[p], vj[:, cols(p)], _NT, preferred_element_type=F32)
                zs += [z2[:BLOCK], z2[BLOCK:]]
                das += [da2[:BLOCK], da2[BLOCK:]]
            logs = [_log_sigmoids(z) for z in zs]
            cums = _cumsum_mxu_many([jnp.where(mask, lm, 0.0) if diagonal else lm for _, lm in logs], rv)
            new_c = []
            for h in range(n_heads):
                lb, (after, rs) = logs[h][0], cums[h]
                if diagonal:
                    a = jnp.where(mask, jnp.exp(lb + after), 0.0)
                    new_c.append(rs)
                else:
                    a = jnp.exp(lb + after + carries[h])
                    new_c.append(carries[h] + rs)
                sig_s[h, j] = jnp.exp(lb)
                a_s[h, j] = a.astype(BF16)
                e_s[h, j] = das[h] * a
            return new_c

        carries = sweep1_tile(i, None, True)

        def live(cs):
            top = cs[0]
            for c in cs[1:]:
                top = jnp.maximum(top, c)
            return jnp.max(top) > SWEEP_EXIT

        def cond(st):
            return (st[0] < i) & st[1]

        def sweep1(st):
            new_c = sweep1_tile(i - 1 - st[0], st[2], False)
            return st[0] + 1, live(new_c), new_c

        visited = lax.while_loop(cond, sweep1, (jnp.int32(0), live(carries), carries))[0]

        def grads(j, st, diagonal):
            prefixes, dqs = st
            kj = _tile(k_ref, j)
            es = [e_s[h, j] for h in range(n_heads)]
            cums = _cumsum_mxu_many(es, fw)
            dzs, new_p = [], []
            for h in range(n_heads):
                sg = sig_s[h, j]
                e_before, rs = cums[h]
                dz = (es[h] * (1.0 - sg) - (e_before + prefixes[h]) * sg) * ATTN_SCALE
                if diagonal:
                    dz = jnp.where(mask, dz, 0.0)
                dzs.append(dz.astype(BF16))
                new_p.append(prefixes[h] + rs)
            rows = pl.ds(pl.multiple_of(j * BLOCK, BLOCK), BLOCK)
            new_dq = []
            for p in pairs:
                dz_stack = jnp.concatenate(dzs[2 * p:2 * p + 2], axis=0)
                a_stack = jnp.concatenate([a_s[2 * p, j], a_s[2 * p + 1, j]], axis=0)
                dq2 = jnp.dot(dz_stack, kj[:, cols(p)], preferred_element_type=F32)
                new_dq.append(dqs[p] + jnp.where(lo, dq2[:BLOCK], dq2[BLOCK:]))
                dk_ref[rows, cols(p)] += lax.dot_general(dz_stack, q_stack[p], _TN, preferred_element_type=F32)
                dv_ref[rows, cols(p)] += lax.dot_general(a_stack, do_stack[p], _TN, preferred_element_type=F32)
            return new_p, new_dq

        zeros = jnp.zeros((BLOCK, BLOCK), F32)
        st = lax.fori_loop(i - visited, i, lambda j, st: grads(j, st, False),
                           ([zeros] * n_heads, [zeros] * SB_BWD_PAIRS))
        dqs = grads(i, st, True)[1]
        for p in pairs:
            dq_ref[:, cols(p)] = dqs[p]

    f32_stash = pltpu.VMEM((n_heads, nj, BLOCK, BLOCK), F32)
    bf16_stash = pltpu.VMEM((n_heads, nj, BLOCK, BLOCK), BF16)
    return pl.pallas_call(
        body, name="attn_b_bwd", grid=(NB, HD // width, nj),
        in_specs=[blk, blk, k_full, v_full, mat, mat], out_specs=[blk, acc_full, acc_full],
        out_shape=[jax.ShapeDtypeStruct((NB, S, HD), F32)] * 3,
        scratch_shapes=[f32_stash, bf16_stash, f32_stash],
        compiler_params=_params(("parallel", "parallel", "arbitrary")),
    )(qkv, do, qkv, qkv, rev, fwd)


def _ada_fwd(c_all, w, b):
    L, D, N = w.shape
    B = c_all.shape[0]

    def body(c_ref, w_ref, b_ref, o_ref):
        cv = c_ref[...]
        cond = (cv * _sigmoid(cv)).astype(BF16)
        o_ref[...] = jnp.dot(cond, w_ref[...].astype(BF16), preferred_element_type=F32) + b_ref[...]

    return pl.pallas_call(
        body, name="ada_fwd", grid=(L,),
        in_specs=[pl.BlockSpec((B, D), lambda l: (0, 0)), pl.BlockSpec((None, D, N), lambda l: (l, 0, 0)),
                  pl.BlockSpec((None, 1, N), lambda l: (l, 0, 0))],
        out_specs=pl.BlockSpec((None, B, N), lambda l: (l, 0, 0)),
        out_shape=jax.ShapeDtypeStruct((L, B, N), F32),
        compiler_params=_params(("parallel",)),
    )(c_all, w, b)


def _ada_bwd(c_all, dmod_all, dmod_shard):
    L, B, N = dmod_shard.shape
    D = c_all.shape[1]
    N_all = dmod_all.shape[2]

    def body(c_ref, da_ref, ds_ref, gw_ref, gb_ref):
        cv = c_ref[...]
        cond = (cv * _sigmoid(cv)).astype(BF16)
        gw_ref[...] = lax.dot_general(cond, ds_ref[...].astype(BF16), _TN, preferred_element_type=F32)
        gb_ref[...] = jnp.sum(da_ref[...], axis=0, keepdims=True)

    return pl.pallas_call(
        body, name="ada_bwd", grid=(L,),
        in_specs=[pl.BlockSpec((B, D), lambda l: (0, 0)), pl.BlockSpec((None, B, N_all), lambda l: (l, 0, 0)),
                  pl.BlockSpec((None, B, N), lambda l: (l, 0, 0))],
        out_specs=[pl.BlockSpec((None, D, N), lambda l: (l, 0, 0)), pl.BlockSpec((None, 1, N_all), lambda l: (l, 0, 0))],
        out_shape=[jax.ShapeDtypeStruct((L, D, N), F32), jax.ShapeDtypeStruct((L, 1, N_all), F32)],
        compiler_params=_params(("parallel",)),
    )(c_all, dmod_all, dmod_shard)


def _adamw(w, g, m, v, name):
    shape = w.shape
    C = shape[-1]
    R = w.size // C
    tr = _pick(R, max(8, (1 << 18) // C), 8)
    c1 = 1.0 - ADAM_B1 ** ADAM_STEP
    c2 = 1.0 - ADAM_B2 ** ADAM_STEP

    def body(w_ref, g_ref, m_ref, v_ref, d_ref, nm_ref, nv_ref):
        gv = g_ref[...]
        nm = ADAM_B1 * m_ref[...] + (1.0 - ADAM_B1) * gv
        nv = ADAM_B2 * v_ref[...] + (1.0 - ADAM_B2) * (gv * gv)
        d_ref[...] = -ADAM_LR * ((nm / c1) / (jnp.sqrt(nv / c2) + ADAM_EPS) + ADAM_WD * w_ref[...])
        nm_ref[...] = nm
        nv_ref[...] = nv

    spec = pl.BlockSpec((tr, C), lambda r: (r, 0))
    out = pl.pallas_call(
        body, name=name, grid=(R // tr,), in_specs=[spec] * 4, out_specs=[spec] * 3,
        out_shape=[jax.ShapeDtypeStruct((R, C), F32)] * 3,
        compiler_params=_params(("parallel",)),
    )(*[t.reshape(R, C) for t in (w, g, m, v)])
    return [t.reshape(shape) for t in out]


_SHARDED = (("wqkv_a", 2), ("wo_a", 1), ("wqkv_b", 2), ("wo_b", 1), ("w_gate", 2), ("w_up", 2), ("w_down", 1))


def _pack_full(layers, axis):
    L = len(layers)
    R, C = layers[0].shape

    def shards(m):
        if axis == 2:
            return m.reshape(R, 4, C // 4).transpose(1, 0, 2)
        return m.reshape(4, R // 4, C)

    halves = [jnp.stack([shards(m) for m in layers[h * (L // 2):(h + 1) * (L // 2)]], axis=1) for h in range(2)]
    return jnp.stack(halves)


def _unpack_full(gathered, axis):
    _, Lh, Rs, Cs = gathered.shape
    t = gathered.reshape(4, 2, Lh, Rs, Cs)
    layers = []
    for h in range(2):
        for l in range(Lh):
            piece = t[:, h, l]
            if axis == 2:
                layers.append(piece.transpose(1, 0, 2).reshape(Rs, 4 * Cs))
            else:
                layers.append(piece.reshape(4 * Rs, Cs))
    return layers


def _sum_slabs(own, recv, name, with_bf16=False):
    C = own.shape[-1]
    out = _sum_leading(recv.reshape(recv.shape[0], -1, C), name, own=own.reshape(-1, C), with_bf16=with_bf16)
    if with_bf16:
        return out[0].reshape(own.shape), out[1].reshape(own.shape)
    return out.reshape(own.shape)


def _gather8(x, name):
    return _all_gather8([x], name)[0]


def _rope_tables(positions):
    half = ROT_DIM // 2
    inv_freq = jnp.power(jnp.float32(ROPE_THETA), -jnp.arange(half, dtype=F32) * 2.0 / ROT_DIM)
    ang = positions.astype(F32).reshape(-1, 1) * inv_freq
    cos, sin = jnp.cos(ang), jnp.sin(ang)
    T = ang.shape[0]
    rest = HEAD_DIM - ROT_DIM
    c64 = jnp.concatenate([cos, cos, jnp.ones((T, rest), F32)], axis=1)
    s64 = jnp.concatenate([-sin, sin, jnp.zeros((T, rest), F32)], axis=1)
    return jnp.tile(c64, (1, 2)), jnp.tile(s64, (1, 2))


def _gain_rows(q_gain, k_gain):
    q2 = jnp.tile(q_gain.reshape(1, HEAD_DIM), (GROUP_A, 2))
    k2 = jnp.tile(k_gain.reshape(1, HEAD_DIM), (1, 2))
    return jnp.concatenate([q2, k2, jnp.ones((1, LANES), F32)], axis=0)


def _local_step(x, positions, mod, norm1_g, norm2_g, q_norm_a, k_norm_a, sinks_a,
                wqkv_a, wo_a, wqkv_b, wo_b, wgu, wd, loss_target):
    NB, S, D = x.shape
    T = NB * S
    QA = N_Q_A * HEAD_DIM
    tab_c, tab_s = _rope_tables(positions)
    rev, fwd = _cumsum_mats()

    saved = []
    xc = x
    for i in range(DEPTH):
        j = i // 2
        sh1, sc1, g1, sh2, sc2, g2 = [mod[i][:, k * D:(k + 1) * D].reshape(NB, 1, D) for k in range(6)]
        st = dict(x=xc, sc1=sc1, g1=g1, sc2=sc2, g2=g2)
        h = _norm_mod_fwd(xc, norm1_g[i:i + 1], sc1, sh1)
        st["h"] = h.reshape(T, D)
        if i % 2 == 0:
            st["qkv"] = _matmul(st["h"], wqkv_a[j], "nn", F32, "qkv_a")
            st["gains"] = _gain_rows(q_norm_a[j], k_norm_a[j])
            st["qkn"] = _qk_prep_fwd(st["qkv"], tab_c, tab_s, st["gains"]).reshape(NB, S, -1)
            st["o"] = _attn_a_fwd(st["qkn"], sinks_a[j]).reshape(T, QA)
            y = _matmul(st["o"], wo_a[j], "nn", F32, "wo_a")
        else:
            st["qkv"] = _matmul(st["h"], wqkv_b[j], "nn", BF16, "qkv_b").reshape(NB, S, -1)
            st["o"] = _attn_b_fwd(st["qkv"], rev).reshape(T, N_H_B * HEAD_DIM)
            y = _matmul(st["o"], wo_b[j], "nn", F32, "wo_b")
        st["y"] = y.reshape(NB, S, D)
        x1 = _gate_res(xc, st["y"], g1)
        st["x1"] = x1
        h2 = _norm_mod_fwd(x1, norm2_g[i:i + 1], sc2, sh2)
        st["h2"] = h2.reshape(T, D)
        st["gu"], st["act"] = _matmul(st["h2"], wgu[i], "nn", F32, "gate_up", swiglu=True)
        st["m"] = _matmul(st["act"], wd[i], "nn", F32, "down").reshape(NB, S, D)
        xc = _gate_res(x1, st["m"], g2)
        saved.append(st)

    loss, dx = _loss_fwd_bwd(xc, loss_target)

    grads = {name: [None] * n for name, n in
             (("wqkv_a", 2), ("wo_a", 2), ("wqkv_b", 2), ("wo_b", 2), ("wgu", DEPTH), ("wd", DEPTH),
              ("norm1_g", DEPTH), ("norm2_g", DEPTH), ("q_norm_a", 2), ("k_norm_a", 2), ("sinks_a", 2))}
    dmod = [None] * DEPTH
    for i in reversed(range(DEPTH)):
        j = i // 2
        st = saved[i]
        dm, dg2 = _gate_res_bwd(dx, st["m"], st["g2"])
        dm = dm.reshape(T, D)
        dact = _matmul(dm, wd[i], "nt", F32, "d_act")
        grads["wd"][i] = _matmul(st["act"], dm, "tn", F32, "d_wd")
        dgu = _swiglu_bwd(st["gu"], dact)
        dh2 = _matmul(dgu, wgu[i], "nt", F32, "d_h2")
        grads["wgu"][i] = _matmul(st["h2"], dgu, "tn", F32, "d_wgu")
        dx1, dsh2, dsc2, grads["norm2_g"][i] = _norm_mod_bwd(
            st["x1"], norm2_g[i:i + 1], st["sc2"], dh2.reshape(NB, S, D), dx)
        dy, dg1 = _gate_res_bwd(dx1, st["y"], st["g1"])
        dy = dy.reshape(T, D)
        if i % 2 == 0:
            do = _matmul(dy, wo_a[j], "nt", BF16, "d_o_a").reshape(NB, S, QA)
            grads["wo_a"][j] = _matmul(st["o"], dy, "tn", F32, "d_wo_a")
            dq, dk, dv, dsink = _attn_a_bwd(st["qkn"], do, sinks_a[j])
            dqkv, dgain = _qk_prep_bwd(st["qkv"], dq.reshape(T, QA), dk.reshape(T, LANES), dv.reshape(T, LANES),
                                       tab_c, tab_s, st["gains"])
            dh = _matmul(dqkv, wqkv_a[j], "nt", F32, "d_h_a")
            grads["wqkv_a"][j] = _matmul(st["h"], dqkv, "tn", F32, "d_wqkv_a")
            grads["q_norm_a"][j] = jnp.sum(dgain[:GROUP_A].reshape(2 * GROUP_A, HEAD_DIM), axis=0)
            grads["k_norm_a"][j] = jnp.sum(dgain[GROUP_A].reshape(2, HEAD_DIM), axis=0)
            grads["sinks_a"][j] = jnp.sum(dsink[..., 0], axis=0)
        else:
            do = _matmul(dy, wo_b[j], "nt", BF16, "d_o_b").reshape(NB, S, -1)
            grads["wo_b"][j] = _matmul(st["o"], dy, "tn", F32, "d_wo_b")
            dq, dk, dv = _attn_b_bwd(st["qkv"], do, rev, fwd)
            dqkv = jnp.concatenate([dq, dk, dv], axis=-1).reshape(T, -1).astype(BF16)
            dh = _matmul(dqkv, wqkv_b[j], "nt", F32, "d_h_b")
            grads["wqkv_b"][j] = _matmul(st["h"], dqkv, "tn", F32, "d_wqkv_b")
        dx, dsh1, dsc1, grads["norm1_g"][i] = _norm_mod_bwd(
            st["x"], norm1_g[i:i + 1], st["sc1"], dh.reshape(NB, S, D), dx1)
        dmod[i] = jnp.concatenate([dsh1, dsc1, dg1, dsh2, dsc2, dg2], axis=-1).reshape(NB, 6 * D)

    matrices = ("wqkv_a", "wo_a", "wqkv_b", "wo_b", "wgu", "wd")
    grads = {name: parts if name in matrices else jnp.stack(parts) for name, parts in grads.items()}
    return loss, dx, grads, jnp.stack(dmod)


def _rows_of(flat, cols=PACK_COLS):
    n = flat.shape[0]
    pad = (-n) % (8 * cols)
    if pad:
        flat = jnp.concatenate([flat, jnp.zeros((pad,), flat.dtype)])
    return flat.reshape(-1, cols)


def kernel(x, c, positions, ada_w, ada_b, norm1_g, norm2_g, wqkv_a, q_norm_a, k_norm_a, sinks_a, wo_a, wqkv_b, wo_b, w_gate, w_up, w_down, loss_target, m_ada_w, m_ada_b, m_norm1_g, m_norm2_g, m_wqkv_a, m_q_norm_a, m_k_norm_a, m_sinks_a, m_wo_a, m_wqkv_b, m_wo_b, m_w_gate, m_w_up, m_w_down, v_ada_w, v_ada_b, v_norm1_g, v_norm2_g, v_wqkv_a, v_q_norm_a, v_k_norm_a, v_sinks_a, v_wo_a, v_wqkv_b, v_wo_b, v_w_gate, v_w_up, v_w_down):
    xi, yi, ci = lax.axis_index("x"), lax.axis_index("y"), lax.axis_index("c")
    dev = 4 * xi + 2 * yi + ci
    chip = 2 * xi + yi
    NB, S, D = x.shape
    B_all = N_DEV * NB
    L = ada_w.shape[0]
    n_mod = ada_w.shape[2] // 2

    c_all = _gather8(_rows_of(c.reshape(-1), LANES), "gather_c").reshape(N_DEV, -1)[:, :NB * D].reshape(B_all, D)
    ada_w_half = lax.dynamic_slice_in_dim(ada_w, ci * n_mod, n_mod, axis=2)
    ada_b_half = lax.dynamic_slice_in_dim(ada_b, dev * n_mod, n_mod, axis=1).reshape(L, 1, n_mod)
    mod_part = _ada_fwd(c_all, ada_w_half, ada_b_half)
    n_part = L * B_all * n_mod
    mod_all = _gather8(_rows_of(mod_part.reshape(-1)), "gather_mod").reshape(N_DEV, -1)[:, :n_part]
    mod_all = mod_all.reshape(N_DEV, L, B_all, n_mod).transpose(1, 2, 0, 3).reshape(L, B_all, N_DEV * n_mod)
    mod = lax.dynamic_slice_in_dim(mod_all, dev * NB, NB, axis=1)

    shards = dict(wqkv_a=wqkv_a, wo_a=wo_a, wqkv_b=wqkv_b, wo_b=wo_b, w_gate=w_gate, w_up=w_up, w_down=w_down)
    halves = []
    for name, _ in _SHARDED:
        w = shards[name]
        half = lax.dynamic_index_in_dim(w.reshape((2, w.shape[0] // 2) + w.shape[1:]), ci, 0, keepdims=False)
        halves.append(half.astype(BF16))
    gathered = _all_gather8(halves, "gather_weights", local_axis=1, local_chunks=8)
    full = {name: _unpack_full(t, axis) for (name, axis), t in zip(_SHARDED, gathered)}
    wgu = [_interleave(gate, up) for gate, up in zip(full["w_gate"], full["w_up"])]

    loss, grad_x, g, dmod = _local_step(
        x, positions, mod, norm1_g, norm2_g, q_norm_a, k_norm_a, sinks_a,
        full["wqkv_a"], full["wo_a"], full["wqkv_b"], full["wo_b"], wgu, full["w_down"], loss_target)

    gate_up = [_deinterleave(t) for t in g["wgu"]]
    g_full = dict(wqkv_a=g["wqkv_a"], wo_a=g["wo_a"], wqkv_b=g["wqkv_b"], wo_b=g["wo_b"],
                  w_gate=[t[0] for t in gate_up], w_up=[t[1] for t in gate_up], w_down=g["wd"])
    packed = [_pack_full(g_full[name], axis) for name, axis in _SHARDED]
    def own(t, index):
        return lax.dynamic_index_in_dim(t, index, 0, keepdims=False)

    from_cores = _exchange(packed, "c", "rs_cores", chunk_axis=0, chunks=4)
    chip_part = [_sum_slabs(own(p, ci), r, "rs_add_cores", with_bf16=True) for p, r in zip(packed, from_cores)]
    from_chips = _exchange([b for _, b in chip_part], "xy", "rs_chips")
    mine = [_sum_slabs(own(p, chip), r, "rs_add_chips") for (p, _), r in zip(chip_part, from_chips)]
    theirs = _sibling_send(mine, "rs_halves")
    grad = {}
    for (name, _), m, t in zip(_SHARDED, mine, theirs):
        first, second = jnp.where(ci == 0, m, t), jnp.where(ci == 0, t, m)
        grad[name] = jnp.stack([first, second]).reshape(shards[name].shape)

    small_names = ("norm1_g", "norm2_g", "q_norm_a", "k_norm_a", "sinks_a")
    small = [dmod.reshape(-1)] + [g[name].reshape(-1) for name in small_names] + [loss.reshape(-1)]
    small_sizes = [t.shape[0] for t in small]
    small_rows = _rows_of(jnp.concatenate(small))
    small_all = _gather8(small_rows, "gather_small")
    small_sum = _sum_leading(small_all, "sum_small").reshape(-1)
    n_dmod = small_sizes[0]
    dmod_all = small_all.reshape(N_DEV, -1)[:, :n_dmod].reshape(N_DEV, L, NB, 6 * D)
    dmod_all = dmod_all.transpose(1, 0, 2, 3).reshape(L, B_all, 6 * D)
    off = n_dmod
    for name, sz in zip(small_names + ("loss",), small_sizes[1:]):
        grad[name] = small_sum[off:off + sz]
        off += sz
    loss_total = grad.pop("loss").reshape(())
    for name, ref in (("norm1_g", norm1_g), ("norm2_g", norm2_g), ("q_norm_a", q_norm_a),
                      ("k_norm_a", k_norm_a), ("sinks_a", sinks_a)):
        grad[name] = grad[name].reshape(ref.shape)

    n_shard = ada_w.shape[2]
    dmod_shard = lax.dynamic_slice_in_dim(dmod_all, chip * n_shard, n_shard, axis=2)
    grad["ada_w"], gb = _ada_bwd(c_all, dmod_all, dmod_shard)
    grad["ada_b"] = gb.reshape(ada_b.shape)

    weights = dict(ada_w=ada_w, ada_b=ada_b, norm1_g=norm1_g, norm2_g=norm2_g, wqkv_a=wqkv_a, q_norm_a=q_norm_a,
                   k_norm_a=k_norm_a, sinks_a=sinks_a, wo_a=wo_a, wqkv_b=wqkv_b, wo_b=wo_b, w_gate=w_gate,
                   w_up=w_up, w_down=w_down)
    m_in = dict(ada_w=m_ada_w, ada_b=m_ada_b, norm1_g=m_norm1_g, norm2_g=m_norm2_g, wqkv_a=m_wqkv_a,
                q_norm_a=m_q_norm_a, k_norm_a=m_k_norm_a, sinks_a=m_sinks_a, wo_a=m_wo_a, wqkv_b=m_wqkv_b,
                wo_b=m_wo_b, w_gate=m_w_gate, w_up=m_w_up, w_down=m_w_down)
    v_in = dict(ada_w=v_ada_w, ada_b=v_ada_b, norm1_g=v_norm1_g, norm2_g=v_norm2_g, wqkv_a=v_wqkv_a,
                q_norm_a=v_q_norm_a, k_norm_a=v_k_norm_a, sinks_a=v_sinks_a, wo_a=v_wo_a, wqkv_b=v_wqkv_b,
                wo_b=v_wo_b, w_gate=v_w_gate, w_up=v_w_up, w_down=v_w_down)
    names = list(weights)
    delta, new_m, new_v = {}, {}, {}
    for name in names:
        delta[name], new_m[name], new_v[name] = _adamw(weights[name], grad[name], m_in[name], v_in[name],
                                                       "adamw_" + name)
    return (loss_total, grad_x, *[grad[k] for k in names], *[delta[k] for k in names],
            *[new_m[k] for k in names], *[new_v[k] for k in names])
```

```python
import jax
import jax.numpy as jnp
from jax import lax
from jax.experimental import pallas as pl
from jax.experimental.pallas import tpu as pltpu

F32 = jnp.float32
BF16 = jnp.bfloat16

DEPTH = 4
HEAD_DIM = 64
N_Q_A = 16
N_KV_A = 2
GROUP_A = N_Q_A // N_KV_A
N_H_B = 16
BLOCK = 128
ROT_DIM = HEAD_DIM // 4
ROPE_THETA = 500000.0
EPS = 1e-6
ATTN_SCALE = HEAD_DIM ** -0.5
NEG_BIG = -1e30

ADAM_LR = 0.001
ADAM_B1 = 0.9
ADAM_B2 = 0.999
ADAM_EPS = 1e-08
ADAM_WD = 0.01
ADAM_STEP = 10

N_DEV = 8
LANES = 128
PACK_COLS = 1024
VMEM_LIMIT_BYTES = 48 * 1024 * 1024
MESH = pl.DeviceIdType.MESH

_NT = (((1,), (1,)), ((), ()))
_TN = (((0,), (0,)), ((), ()))
_NN = (((1,), (0,)), ((), ()))


def _params(sem=None):
    return pltpu.CompilerParams(vmem_limit_bytes=VMEM_LIMIT_BYTES, dimension_semantics=sem)


def _pick(n, cap, mult):
    best = None
    for t in range(mult, min(n, cap) + 1, mult):
        if n % t == 0:
            best = t
    return n if best is None else best


_ANY = pl.BlockSpec(memory_space=pl.ANY)


def _window(index, axis, q, n, shape):
    rest = [slice(None)] * len(shape)
    size = shape[axis] // n
    rest[axis] = pl.ds(q * size, size)
    return tuple(index) + tuple(rest)


def _all_gather8(xs, name, local_axis=0, local_chunks=1):
    n = len(xs)

    def body(*refs):
        x_refs, out_refs = refs[:n], refs[n:2 * n]
        send_sems, recv_sems, local_sems = refs[2 * n:]
        xi, yi, ci = lax.axis_index("x"), lax.axis_index("y"), lax.axis_index("c")
        me, sibling = (xi, yi, ci), (xi, yi, 1 - ci)
        chips = [(1 - xi, yi), (xi, 1 - yi), (1 - xi, 1 - yi)]

        def slab(w, px, py, pc):
            return out_refs[w].at[4 * px + 2 * py + pc]

        def copy(w, k, block, to, src=None):
            return pltpu.make_async_remote_copy(
                src_ref=slab(w, *block) if src is None else src, dst_ref=slab(w, *block),
                send_sem=send_sems.at[k, w], recv_sem=recv_sems.at[k, w], device_id=to, device_id_type=MESH)

        mine = []
        for w in range(n):
            for q in range(local_chunks):
                part = _window((), local_axis, q, local_chunks, xs[w].shape)
                mine.append(pltpu.make_async_copy(x_refs[w].at[part], slab(w, *me).at[part], local_sems.at[w, q]))
                mine[-1].start()
        first = [copy(w, 0, me, sibling, src=x_refs[w]) for w in range(n)]
        first += [copy(w, 1 + j, me, (*chip, ci), src=x_refs[w]) for j, chip in enumerate(chips) for w in range(n)]
        for cp in first:
            cp.start()
        passed = []
        for j, chip in enumerate(chips):
            for w in range(n):
                copy(w, 1 + j, (*chip, ci), me).wait_recv()
                passed.append(copy(w, 4 + j, (*chip, ci), sibling))
                passed[-1].start()
        for w in range(n):
            copy(w, 0, sibling, me).wait_recv()
        for j, chip in enumerate(chips):
            for w in range(n):
                copy(w, 4 + j, (*chip, 1 - ci), me).wait_recv()
        for cp in first + passed:
            cp.wait_send()
        for cp in mine:
            cp.wait()

    return pl.pallas_call(
        body, name=name,
        out_shape=[jax.ShapeDtypeStruct((N_DEV,) + x.shape, x.dtype) for x in xs],
        in_specs=[_ANY] * n, out_specs=[_ANY] * n,
        scratch_shapes=[pltpu.SemaphoreType.DMA((7, n)), pltpu.SemaphoreType.DMA((7, n)),
                        pltpu.SemaphoreType.DMA((n, local_chunks))],
    )(*xs)


def _exchange(xs, group, name, chunk_axis=0, chunks=1):
    n = len(xs)
    n_peers = 1 if group == "c" else 3

    def body(*refs):
        x_refs, out_refs = refs[:n], refs[n:2 * n]
        send_sems, recv_sems = refs[2 * n:]
        xi, yi, ci = lax.axis_index("x"), lax.axis_index("y"), lax.axis_index("c")
        if group == "c":
            peers = [(1 - ci, (xi, yi, 1 - ci))]
        else:
            peers = [(2 * (1 - xi) + yi, (1 - xi, yi, ci)),
                     (2 * xi + (1 - yi), (xi, 1 - yi, ci)),
                     (2 * (1 - xi) + (1 - yi), (1 - xi, 1 - yi, ci))]
        copies = []
        for k, (p, dev) in enumerate(peers):
            for w in range(n):
                slab_shape = xs[w].shape[1:]
                for q in range(chunks):
                    copies.append(pltpu.make_async_remote_copy(
                        src_ref=x_refs[w].at[_window((p,), chunk_axis, q, chunks, slab_shape)],
                        dst_ref=out_refs[w].at[_window((k,), chunk_axis, q, chunks, slab_shape)],
                        send_sem=send_sems.at[k, w, q], recv_sem=recv_sems.at[k, w, q],
                        device_id=dev, device_id_type=MESH))
                    copies[-1].start()
        for cp in copies:
            cp.wait()

    return pl.pallas_call(
        body, name=name,
        out_shape=[jax.ShapeDtypeStruct((n_peers,) + x.shape[1:], x.dtype) for x in xs],
        in_specs=[_ANY] * n, out_specs=[_ANY] * n,
        scratch_shapes=[pltpu.SemaphoreType.DMA((n_peers, n, chunks)), pltpu.SemaphoreType.DMA((n_peers, n, chunks))],
    )(*xs)


def _sibling_send(xs, name, chunk_axis=1, chunks=4):
    n = len(xs)

    def body(*refs):
        x_refs, out_refs = refs[:n], refs[n:2 * n]
        send_sems, recv_sems = refs[2 * n:]
        xi, yi, ci = lax.axis_index("x"), lax.axis_index("y"), lax.axis_index("c")
        copies = []
        for w in range(n):
            for q in range(chunks):
                part = _window((), chunk_axis, q, chunks, xs[w].shape)
                copies.append(pltpu.make_async_remote_copy(
                    src_ref=x_refs[w].at[part], dst_ref=out_refs[w].at[part],
                    send_sem=send_sems.at[w, q], recv_sem=recv_sems.at[w, q],
                    device_id=(xi, yi, 1 - ci), device_id_type=MESH))
                copies[-1].start()
        for cp in copies:
            cp.wait()

    return pl.pallas_call(
        body, name=name,
        out_shape=[jax.ShapeDtypeStruct(x.shape, x.dtype) for x in xs],
        in_specs=[_ANY] * n, out_specs=[_ANY] * n,
        scratch_shapes=[pltpu.SemaphoreType.DMA((n, chunks)), pltpu.SemaphoreType.DMA((n, chunks))],
    )(*xs)


def _sum_leading(x, name, own=None, with_bf16=False):
    P, R, C = x.shape
    tr = _pick(R, max(16, (1 << 19) // (C * (P + 1))), 16)

    def body(*refs):
        n_in = 1 if own is None else 2
        x_ref = refs[n_in - 1]
        acc = x_ref[0].astype(F32) if own is None else refs[0][...] + x_ref[0].astype(F32)
        for p in range(1, P):
            acc = acc + x_ref[p].astype(F32)
        refs[n_in][...] = acc
        if with_bf16:
            refs[n_in + 1][...] = acc.astype(BF16)

    flat = pl.BlockSpec((tr, C), lambda r: (r, 0))
    slabs = pl.BlockSpec((P, tr, C), lambda r: (0, r, 0))
    out = pl.pallas_call(
        body, name=name, grid=(R // tr,),
        in_specs=[slabs] if own is None else [flat, slabs],
        out_specs=[flat, flat] if with_bf16 else [flat],
        out_shape=[jax.ShapeDtypeStruct((R, C), F32)] + ([jax.ShapeDtypeStruct((R, C), BF16)] if with_bf16 else []),
        compiler_params=_params(("arbitrary",)),
    )(*([x] if own is None else [own, x]))
    return out if with_bf16 else out[0]


MATMUL_SINGLE_K = 1280
MATMUL_VMEM_BUDGET = 36 * 1024 * 1024


def _matmul(a, b, mode, out_dtype, name, swiglu=False):
    if mode == "nn":
        (M, K), N = a.shape, b.shape[1]
    elif mode == "nt":
        (M, K), N = a.shape, b.shape[0]
    else:
        (K, M), N = a.shape, b.shape[1]
    tm = _pick(M, 1024 if mode != "tn" else 1536, 128)
    tn = _pick(N, 1536, 128)
    if swiglu:
        tm, tn = _pick(M, 512, 128), 2 * _ff_tile(N // 2)
    out_bytes = jnp.dtype(out_dtype).itemsize
    tk = K
    if K > MATMUL_SINGLE_K:
        for cap in (2048, 1024, 512):
            tk = _pick(K, cap, 128)
            blocks = 2 * 2 * tk * (tm + tn) + tm * tn * (2 * out_bytes + (4 if out_dtype != F32 else 0))
            if blocks <= MATMUL_VMEM_BUDGET:
                break
    nk = K // tk
    dims = {"nn": _NN, "nt": _NT, "tn": _TN}[mode]
    use_scratch = nk > 1 and out_dtype != F32

    def body(a_ref, b_ref, *refs):
        o_ref = refs[0]

        def product():
            return lax.dot_general(a_ref[...].astype(BF16), b_ref[...].astype(BF16), dims,
                                   preferred_element_type=F32)

        if nk == 1:
            part = product()
            o_ref[...] = part.astype(o_ref.dtype)
            if swiglu:
                g = part[:, :tn // 2]
                refs[1][...] = (g * _sigmoid(g) * part[:, tn // 2:]).astype(BF16)
            return
        k = pl.program_id(2)
        acc_ref = refs[-1] if use_scratch else o_ref

        @pl.when(k == 0)
        def _():
            acc_ref[...] = jnp.zeros_like(acc_ref)

        acc_ref[...] += product()

        if use_scratch:
            @pl.when(k == nk - 1)
            def _():
                o_ref[...] = acc_ref[...].astype(o_ref.dtype)

    if mode == "tn":
        a_spec = pl.BlockSpec((tk, tm), lambda i, j, k: (k, i))
    else:
        a_spec = pl.BlockSpec((tm, tk), lambda i, j, k: (i, k))
    if mode == "nt":
        b_spec = pl.BlockSpec((tn, tk), lambda i, j, k: (j, k))
    else:
        b_spec = pl.BlockSpec((tk, tn), lambda i, j, k: (k, j))
    out_specs = [pl.BlockSpec((tm, tn), lambda i, j, k: (i, j))]
    out_shape = [jax.ShapeDtypeStruct((M, N), out_dtype)]
    if swiglu:
        assert nk == 1 and mode == "nn"
        out_specs.append(pl.BlockSpec((tm, tn // 2), lambda i, j, k: (i, j)))
        out_shape.append(jax.ShapeDtypeStruct((M, N // 2), BF16))
    out = pl.pallas_call(
        body, name=name, grid=(M // tm, N // tn, nk),
        in_specs=[a_spec, b_spec], out_specs=out_specs, out_shape=out_shape,
        scratch_shapes=[pltpu.VMEM((tm, tn), F32)] if use_scratch else [],
        compiler_params=_params(("parallel", "parallel", "arbitrary")),
    )(a, b)
    return out if swiglu else out[0]


def _row_tile(S):
    return _pick(S, 512, 8)


def _norm_mod_fwd(x, gain, sc, sh):
    NB, S, D = x.shape
    tr = _row_tile(S)

    def body(x_ref, g_ref, sc_ref, sh_ref, h_ref):
        xv = x_ref[...]
        ms = jnp.mean(xv * xv, axis=-1, keepdims=True)
        n = xv * lax.rsqrt(ms + EPS) * g_ref[...]
        h_ref[...] = (n * (1.0 + sc_ref[...]) + sh_ref[...]).astype(BF16)

    tok = pl.BlockSpec((None, tr, D), lambda b, r: (b, r, 0))
    per_ex = pl.BlockSpec((None, 1, D), lambda b, r: (b, 0, 0))
    return pl.pallas_call(
        body, name="norm_mod_fwd", grid=(NB, S // tr),
        in_specs=[tok, pl.BlockSpec((1, D), lambda b, r: (0, 0)), per_ex, per_ex],
        out_specs=tok, out_shape=jax.ShapeDtypeStruct((NB, S, D), BF16),
        compiler_params=_params(("parallel", "parallel")),
    )(x, gain, sc, sh)


def _norm_mod_bwd(x, gain, sc, dh, dres):
    NB, S, D = x.shape
    tr = _row_tile(S)

    def body(x_ref, g_ref, sc_ref, dh_ref, dres_ref, dx_ref, dsh_ref, dsc_ref, dg_ref):
        b, r = pl.program_id(0), pl.program_id(1)

        @pl.when(r == 0)
        def _():
            dsh_ref[...] = jnp.zeros_like(dsh_ref)
            dsc_ref[...] = jnp.zeros_like(dsc_ref)

        @pl.when((r == 0) & (b == 0))
        def _():
            dg_ref[...] = jnp.zeros_like(dg_ref)

        xv = x_ref[...]
        rstd = lax.rsqrt(jnp.mean(xv * xv, axis=-1, keepdims=True) + EPS)
        xh = xv * rstd
        g = g_ref[...]
        dh = dh_ref[...]
        dsh_ref[...] += jnp.sum(dh, axis=0, keepdims=True)
        dsc_ref[...] += jnp.sum(dh * (xh * g), axis=0, keepdims=True)
        dn = dh * (1.0 + sc_ref[...])
        dg_ref[...] += jnp.sum(dn * xh, axis=0, keepdims=True)
        dxh = dn * g
        proj = jnp.mean(dxh * xh, axis=-1, keepdims=True)
        dx_ref[...] = rstd * (dxh - xh * proj) + dres_ref[...]

    tok = pl.BlockSpec((None, tr, D), lambda b, r: (b, r, 0))
    per_ex = pl.BlockSpec((None, 1, D), lambda b, r: (b, 0, 0))
    row = pl.BlockSpec((1, D), lambda b, r: (0, 0))
    return pl.pallas_call(
        body, name="norm_mod_bwd", grid=(NB, S // tr),
        in_specs=[tok, row, per_ex, tok, tok],
        out_specs=[tok, per_ex, per_ex, row],
        out_shape=[jax.ShapeDtypeStruct((NB, S, D), F32), jax.ShapeDtypeStruct((NB, 1, D), F32),
                   jax.ShapeDtypeStruct((NB, 1, D), F32), jax.ShapeDtypeStruct((1, D), F32)],
        compiler_params=_params(("arbitrary", "arbitrary")),
    )(x, gain, sc, dh, dres)


def _gate_res(x, y, g):
    NB, S, D = x.shape
    tr = _row_tile(S)

    def body(x_ref, y_ref, g_ref, o_ref):
        o_ref[...] = x_ref[...] + g_ref[...] * y_ref[...]

    tok = pl.BlockSpec((None, tr, D), lambda b, r: (b, r, 0))
    per_ex = pl.BlockSpec((None, 1, D), lambda b, r: (b, 0, 0))
    return pl.pallas_call(
        body, name="gate_res", grid=(NB, S // tr), in_specs=[tok, tok, per_ex], out_specs=tok,
        out_shape=jax.ShapeDtypeStruct((NB, S, D), F32),
        compiler_params=_params(("parallel", "parallel")),
    )(x, y, g)


def _gate_res_bwd(dxo, y, g):
    NB, S, D = dxo.shape
    tr = _row_tile(S)

    def body(d_ref, y_ref, g_ref, dy_ref, dg_ref):
        @pl.when(pl.program_id(1) == 0)
        def _():
            dg_ref[...] = jnp.zeros_like(dg_ref)

        d = d_ref[...]
        dy_ref[...] = (d * g_ref[...]).astype(BF16)
        dg_ref[...] += jnp.sum(d * y_ref[...], axis=0, keepdims=True)

    tok = pl.BlockSpec((None, tr, D), lambda b, r: (b, r, 0))
    per_ex = pl.BlockSpec((None, 1, D), lambda b, r: (b, 0, 0))
    return pl.pallas_call(
        body, name="gate_res_bwd", grid=(NB, S // tr), in_specs=[tok, tok, per_ex], out_specs=[tok, per_ex],
        out_shape=[jax.ShapeDtypeStruct((NB, S, D), BF16), jax.ShapeDtypeStruct((NB, 1, D), F32)],
        compiler_params=_params(("arbitrary", "arbitrary")),
    )(dxo, y, g)


def _sigmoid(v):
    return 1.0 / (1.0 + jnp.exp(-v))


def _ff_tile(F):
    return _pick(F, 1536, 128)


def _interleave(gate, up):
    F = gate.shape[-1]
    tf = _ff_tile(F)
    parts = []
    for j in range(F // tf):
        parts += [gate[..., j * tf:(j + 1) * tf], up[..., j * tf:(j + 1) * tf]]
    return jnp.concatenate(parts, axis=-1)


def _deinterleave(gu):
    F = gu.shape[-1] // 2
    tf = _ff_tile(F)
    gate = [gu[..., 2 * j * tf:(2 * j + 1) * tf] for j in range(F // tf)]
    up = [gu[..., (2 * j + 1) * tf:(2 * j + 2) * tf] for j in range(F // tf)]
    return jnp.concatenate(gate, axis=-1), jnp.concatenate(up, axis=-1)


def _swiglu_bwd(gu, dact):
    T, F2 = gu.shape
    F = F2 // 2
    tf = _ff_tile(F)
    tr = _pick(T, 256, 8)

    def body(gu_ref, d_ref, o_ref):
        g, u, d = gu_ref[:, :tf], gu_ref[:, tf:], d_ref[...]
        s = _sigmoid(g)
        o_ref[:, :tf] = (d * u * (s * (1.0 + g * (1.0 - s)))).astype(BF16)
        o_ref[:, tf:] = (d * (g * s)).astype(BF16)

    return pl.pallas_call(
        body, name="swiglu_bwd", grid=(T // tr, F // tf),
        in_specs=[pl.BlockSpec((tr, 2 * tf), lambda i, j: (i, j)), pl.BlockSpec((tr, tf), lambda i, j: (i, j))],
        out_specs=pl.BlockSpec((tr, 2 * tf), lambda i, j: (i, j)),
        out_shape=jax.ShapeDtypeStruct((T, F2), BF16),
        compiler_params=_params(("parallel", "parallel")),
    )(gu, dact)


def _loss_fwd_bwd(y, target):
    NB, S, D = y.shape
    tr = _row_tile(S)

    def body(y_ref, t_ref, l_ref, d_ref):
        @pl.when((pl.program_id(0) == 0) & (pl.program_id(1) == 0))
        def _():
            l_ref[...] = jnp.zeros_like(l_ref)

        e = y_ref[...] - t_ref[...]
        d_ref[...] = e / D
        l_ref[...] += 0.5 * jnp.sum(jnp.mean(e * e, axis=-1, keepdims=True), axis=0, keepdims=True)

    tok = pl.BlockSpec((None, tr, D), lambda b, r: (b, r, 0))
    return pl.pallas_call(
        body, name="loss", grid=(NB, S // tr), in_specs=[tok, tok],
        out_specs=[pl.BlockSpec((1, 1), lambda b, r: (0, 0)), tok],
        out_shape=[jax.ShapeDtypeStruct((1, 1), F32), jax.ShapeDtypeStruct((NB, S, D), F32)],
        compiler_params=_params(("arbitrary", "arbitrary")),
    )(y, target)


def _half_sums(v, lo):
    sa = jnp.sum(jnp.where(lo, v, 0.0), axis=-1, keepdims=True)
    sb = jnp.sum(jnp.where(lo, 0.0, v), axis=-1, keepdims=True)
    return jnp.where(lo, sa, sb)


def _rope_swap(v, lane64):
    up = pltpu.roll(v, LANES - ROT_DIM // 2, 1)
    down = pltpu.roll(v, ROT_DIM // 2, 1)
    return jnp.where(lane64 < ROT_DIM // 2, up, jnp.where(lane64 < ROT_DIM, down, 0.0))


def _qk_prep_fwd(qkv, tab_c, tab_s, gains):
    T, W = qkv.shape
    R = W // LANES
    tt = _pick(T, 256, 8)

    def body(x_ref, c_ref, s_ref, g_ref, o_ref):
        lane = lax.broadcasted_iota(jnp.int32, (tt, LANES), 1)
        lo = lane < HEAD_DIM
        lane64 = lane & (HEAD_DIM - 1)
        c, s = c_ref[...], s_ref[...]
        for j in range(R - 1):
            cols = slice(j * LANES, (j + 1) * LANES)
            xv = x_ref[:, cols]
            rstd = lax.rsqrt(_half_sums(xv * xv, lo) / HEAD_DIM + EPS)
            yn = xv * rstd * g_ref[j:j + 1, :]
            o_ref[:, cols] = (yn * c + _rope_swap(yn, lane64) * s).astype(BF16)
        o_ref[:, (R - 1) * LANES:] = x_ref[:, (R - 1) * LANES:].astype(BF16)

    tok = pl.BlockSpec((tt, W), lambda t: (t, 0))
    tab = pl.BlockSpec((tt, LANES), lambda t: (t, 0))
    return pl.pallas_call(
        body, name="qk_prep_fwd", grid=(T // tt,),
        in_specs=[tok, tab, tab, pl.BlockSpec((R, LANES), lambda t: (0, 0))],
        out_specs=tok, out_shape=jax.ShapeDtypeStruct((T, W), BF16),
        compiler_params=_params(("parallel",)),
    )(qkv, tab_c, tab_s, gains)


def _qk_prep_bwd(qkv, dq, dk, dv, tab_c, tab_s, gains):
    T, W = qkv.shape
    R = W // LANES
    QW = dq.shape[1]
    tt = _pick(T, 256, 8)

    def body(x_ref, dq_ref, dk_ref, dv_ref, c_ref, s_ref, g_ref, o_ref, dg_ref):
        @pl.when(pl.program_id(0) == 0)
        def _():
            dg_ref[...] = jnp.zeros_like(dg_ref)

        lane = lax.broadcasted_iota(jnp.int32, (tt, LANES), 1)
        lo = lane < HEAD_DIM
        lane64 = lane & (HEAD_DIM - 1)
        c, s = c_ref[...], s_ref[...]
        for j in range(R - 1):
            cols = slice(j * LANES, (j + 1) * LANES)
            xv = x_ref[:, cols]
            d = dq_ref[:, cols] if j < R - 2 else dk_ref[...]
            rstd = lax.rsqrt(_half_sums(xv * xv, lo) / HEAD_DIM + EPS)
            xh = xv * rstd
            dyn = d * c + _rope_swap(d * s, lane64)
            dg_ref[j:j + 1, :] += jnp.sum(dyn * xh, axis=0, keepdims=True)
            dxh = dyn * g_ref[j:j + 1, :]
            proj = _half_sums(dxh * xh, lo) / HEAD_DIM
            o_ref[:, cols] = (rstd * (dxh - xh * proj)).astype(BF16)
        o_ref[:, (R - 1) * LANES:] = dv_ref[...].astype(BF16)

    tok = pl.BlockSpec((tt, W), lambda t: (t, 0))
    tab = pl.BlockSpec((tt, LANES), lambda t: (t, 0))
    gsp = pl.BlockSpec((R, LANES), lambda t: (0, 0))
    return pl.pallas_call(
        body, name="qk_prep_bwd", grid=(T // tt,),
        in_specs=[tok, pl.BlockSpec((tt, QW), lambda t: (t, 0)), tab, tab, tab, tab, gsp], out_specs=[tok, gsp],
        out_shape=[jax.ShapeDtypeStruct((T, W), BF16), jax.ShapeDtypeStruct((R, LANES), F32)],
        compiler_params=_params(("arbitrary",)),
    )(qkv, dq, dk, dv, tab_c, tab_s, gains)


def _band_mask(i):
    r = lax.broadcasted_iota(jnp.int32, (2 * BLOCK, 2 * BLOCK), 0) & (BLOCK - 1)
    c = lax.broadcasted_iota(jnp.int32, (2 * BLOCK, 2 * BLOCK), 1)
    rel = r + BLOCK - c
    return (rel >= 0) & (rel < BLOCK) & ((c >= BLOCK) | (i > 0))


def _swa_softmax(s, valid, sink):
    s = jnp.where(valid, s * ATTN_SCALE, NEG_BIG)
    m = jnp.maximum(jnp.max(s, axis=1, keepdims=True), sink)
    p = jnp.exp(s - m)
    ps = jnp.exp(sink - m)
    denom = jnp.sum(p, axis=1, keepdims=True) + ps
    return p / denom, ps / denom


A_GROUP = 2


Q_WIDTH_A = N_Q_A * HEAD_DIM
N_PAIR_A = Q_WIDTH_A // LANES


def _swa_specs():
    qs = pl.BlockSpec((None, BLOCK, Q_WIDTH_A), lambda b, i: (b, i, 0))

    def kv(col, back):
        return pl.BlockSpec((None, BLOCK, LANES), lambda b, i: (b, jnp.maximum(i - back, 0), col))

    return qs, kv(N_PAIR_A, 1), kv(N_PAIR_A, 0), kv(N_PAIR_A + 1, 1), kv(N_PAIR_A + 1, 0)


def _dup_heads(t):
    lo = lax.broadcasted_iota(jnp.int32, t.shape, 1) < HEAD_DIM
    sw = pltpu.roll(t.astype(F32), HEAD_DIM, 1).astype(BF16)
    return jnp.where(lo, t, sw), jnp.where(lo, sw, t)


def _kv_tiles(kp_ref, kc_ref, vp_ref, vc_ref):
    kd = _dup_heads(jnp.concatenate([kp_ref[...], kc_ref[...]], axis=0))
    vd = _dup_heads(jnp.concatenate([vp_ref[...], vc_ref[...]], axis=0))
    return kd, vd


def _attn_a_fwd(qkn, sinks):
    NB, S, _ = qkn.shape
    qs, kp, kc, vp, vc = _swa_specs()

    def body(q_ref, kp_ref, kc_ref, vp_ref, vc_ref, sink_ref, o_ref):
        i = pl.program_id(1)
        kd, vd = _kv_tiles(kp_ref, kc_ref, vp_ref, vc_ref)
        valid = _band_mask(i)
        lo = lax.broadcasted_iota(jnp.int32, (BLOCK, LANES), 1) < HEAD_DIM
        top = lax.broadcasted_iota(jnp.int32, (2 * BLOCK, 1), 0) < BLOCK
        for first in range(0, N_PAIR_A, A_GROUP):
            pairs = range(first, first + A_GROUP)
            qs_ = [jnp.concatenate(_head_halves(q_ref[:, p * LANES:(p + 1) * LANES], lo), axis=0) for p in pairs]
            ss = [lax.dot_general(q, kd[2 * p // GROUP_A], _NT, preferred_element_type=F32) for q, p in zip(qs_, pairs)]
            pns = [_swa_softmax(s, valid, jnp.where(top, sink_ref[2 * p], sink_ref[2 * p + 1]))[0]
                   for s, p in zip(ss, pairs)]
            pvs = [jnp.dot(pn.astype(BF16), vd[2 * p // GROUP_A], preferred_element_type=F32) for pn, p in zip(pns, pairs)]
            for pv, p in zip(pvs, pairs):
                o_ref[:, p * LANES:(p + 1) * LANES] = jnp.where(lo, pv[:BLOCK], pv[BLOCK:]).astype(BF16)

    return pl.pallas_call(
        body, name="attn_a_fwd", grid=(NB, S // BLOCK),
        in_specs=[qs, kp, kc, vp, vc, pl.BlockSpec(memory_space=pltpu.SMEM)],
        out_specs=qs, out_shape=jax.ShapeDtypeStruct((NB, S, Q_WIDTH_A), BF16),
        compiler_params=_params(("parallel", "arbitrary")),
    )(qkn, qkn, qkn, qkn, qkn, sinks)


def _attn_a_bwd(qkn, do, sinks):
    NB, S, _ = qkn.shape
    qs, kp, kc, vp, vc = _swa_specs()
    full = pl.BlockSpec((None, S, LANES), lambda b, i: (b, 0, 0))
    sink_out = pl.BlockSpec((None, N_Q_A, LANES), lambda b, i: (b, 0, 0))

    def body(q_ref, do_ref, kp_ref, kc_ref, vp_ref, vc_ref, sink_ref, dq_ref, dk_ref, dv_ref, ds_ref, dk_s, dv_s):
        i = pl.program_id(1)

        @pl.when(i == 0)
        def _():
            dk_ref[...] = jnp.zeros_like(dk_ref)
            dv_ref[...] = jnp.zeros_like(dv_ref)
            ds_ref[...] = jnp.zeros_like(ds_ref)

        dk_s[...] = jnp.zeros_like(dk_s)
        dv_s[...] = jnp.zeros_like(dv_s)
        kd, vd = _kv_tiles(kp_ref, kc_ref, vp_ref, vc_ref)
        valid = _band_mask(i)
        lo = lax.broadcasted_iota(jnp.int32, (BLOCK, LANES), 1) < HEAD_DIM
        top = lax.broadcasted_iota(jnp.int32, (2 * BLOCK, 1), 0) < BLOCK
        for first in range(0, N_PAIR_A, A_GROUP):
            pairs = range(first, first + A_GROUP)
            kvs = [2 * p // GROUP_A for p in pairs]
            qs_ = [jnp.concatenate(_head_halves(q_ref[:, p * LANES:(p + 1) * LANES], lo), axis=0) for p in pairs]
            dos = [jnp.concatenate(_head_halves(do_ref[:, p * LANES:(p + 1) * LANES], lo), axis=0) for p in pairs]
            ss = [lax.dot_general(q, kd[kv], _NT, preferred_element_type=F32) for q, kv in zip(qs_, kvs)]
            dps = [lax.dot_general(d, vd[kv], _NT, preferred_element_type=F32) for d, kv in zip(dos, kvs)]
            sm = [_swa_softmax(s, valid, jnp.where(top, sink_ref[2 * p], sink_ref[2 * p + 1])) for s, p in zip(ss, pairs)]
            deltas = [jnp.sum(pn * dp, axis=1, keepdims=True) for (pn, _), dp in zip(sm, dps)]
            dsbs = [(pn * (dp - delta) * ATTN_SCALE).astype(BF16) for (pn, _), dp, delta in zip(sm, dps, deltas)]
            for n, p in enumerate(pairs):
                dq2 = jnp.dot(dsbs[n], kd[kvs[n]], preferred_element_type=F32)
                dq_ref[:, p * LANES:(p + 1) * LANES] = jnp.where(lo, dq2[:BLOCK], dq2[BLOCK:])
                dk_s[kvs[n]] += lax.dot_general(dsbs[n], qs_[n], _TN, preferred_element_type=F32)
                dv_s[kvs[n]] += lax.dot_general(sm[n][0].astype(BF16), dos[n], _TN, preferred_element_type=F32)
                t = sm[n][1] * deltas[n]
                for hh in range(2):
                    dsink = -jnp.sum(t[hh * BLOCK:(hh + 1) * BLOCK], axis=0, keepdims=True)
                    ds_ref[2 * p + hh:2 * p + hh + 1, :] += jnp.broadcast_to(dsink, (1, LANES))

        lo2 = lax.broadcasted_iota(jnp.int32, (2 * BLOCK, LANES), 1) < HEAD_DIM

        def fold(acc):
            halves = [acc[kv] + pltpu.roll(acc[kv], HEAD_DIM, 1) for kv in range(N_KV_A)]
            return jnp.where(lo2, halves[0], halves[1])

        dk2, dv2 = fold(dk_s), fold(dv_s)

        @pl.when(i > 0)
        def _():
            start = pl.multiple_of((i - 1) * BLOCK, BLOCK)
            dk_ref[pl.ds(start, 2 * BLOCK), :] += dk2
            dv_ref[pl.ds(start, 2 * BLOCK), :] += dv2

        @pl.when(i == 0)
        def _():
            dk_ref[0:BLOCK, :] += dk2[BLOCK:, :]
            dv_ref[0:BLOCK, :] += dv2[BLOCK:, :]

    slots = pltpu.VMEM((N_KV_A, 2 * BLOCK, LANES), F32)
    return pl.pallas_call(
        body, name="attn_a_bwd", grid=(NB, S // BLOCK),
        in_specs=[qs, qs, kp, kc, vp, vc, pl.BlockSpec(memory_space=pltpu.SMEM)],
        out_specs=[qs, full, full, sink_out],
        out_shape=[jax.ShapeDtypeStruct((NB, S, Q_WIDTH_A), F32), jax.ShapeDtypeStruct((NB, S, LANES), F32),
                   jax.ShapeDtypeStruct((NB, S, LANES), F32), jax.ShapeDtypeStruct((NB, N_Q_A, LANES), F32)],
        scratch_shapes=[slots, slots],
        compiler_params=_params(("parallel", "arbitrary")),
    )(qkn, do, qkn, qkn, qkn, qkn, sinks)


def _cumsum_mats():
    src = lax.broadcasted_iota(jnp.int32, (2 * BLOCK, 2 * BLOCK), 0) % BLOCK
    dst = lax.broadcasted_iota(jnp.int32, (2 * BLOCK, 2 * BLOCK), 1)
    ones = dst >= BLOCK
    rev = ((src > dst) | ones).astype(BF16)
    fwd = ((src < dst) | ones).astype(BF16)
    return rev, fwd


def _log_sigmoids(z):
    sp = jnp.log(1.0 + jnp.exp(-jnp.abs(z)))
    return jnp.minimum(z, 0.0) - sp, -(jnp.maximum(z, 0.0) + sp)


def _cumsum_mxu_many(vs, mat):
    parts = []
    for v in vs:
        hi = v.astype(BF16)
        parts.append(jnp.concatenate([hi, (v - hi.astype(F32)).astype(BF16)], axis=1))
    r = jnp.dot(jnp.concatenate(parts, axis=0), mat, preferred_element_type=F32)
    return [(r[n * BLOCK:(n + 1) * BLOCK, :BLOCK], r[n * BLOCK:(n + 1) * BLOCK, BLOCK:]) for n in range(len(vs))]


def _strict_mask():
    r = lax.broadcasted_iota(jnp.int32, (BLOCK, BLOCK), 0)
    c = lax.broadcasted_iota(jnp.int32, (BLOCK, BLOCK), 1)
    return c < r


def _tile(ref, j):
    return ref[pl.ds(pl.multiple_of(j * BLOCK, BLOCK), BLOCK), :]


SWEEP_EXIT = -88.0


def _head_halves(t, lo):
    zero = jnp.zeros_like(t)
    return jnp.where(lo, t, zero), jnp.where(lo, zero, t)


def _sb_specs(S, HD, width):
    n = HD // width
    blk = pl.BlockSpec((None, BLOCK, width), lambda b, p, i: (b, i, p))
    k_full = pl.BlockSpec((None, S, width), lambda b, p, i: (b, 0, n + p))
    v_full = pl.BlockSpec((None, S, width), lambda b, p, i: (b, 0, 2 * n + p))
    mat = pl.BlockSpec((2 * BLOCK, 2 * BLOCK), lambda b, p, i: (0, 0))
    return blk, k_full, v_full, mat


SB_FWD_PAIRS = 2
SB_BWD_PAIRS = 2


def _attn_b_fwd(qkv, rev):
    NB, S, W = qkv.shape
    HD = W // 3
    width = SB_FWD_PAIRS * LANES
    n_heads = 2 * SB_FWD_PAIRS
    blk, k_full, v_full, mat = _sb_specs(S, HD, width)

    def body(q_ref, k_ref, v_ref, rev_ref, o_ref):
        i = pl.program_id(2)
        rv = rev_ref[...]
        mask = _strict_mask()
        lo = lax.broadcasted_iota(jnp.int32, (BLOCK, LANES), 1) < HEAD_DIM
        q_all = q_ref[...]
        q_stack = [jnp.concatenate(_head_halves(q_all[:, p * LANES:(p + 1) * LANES] * ATTN_SCALE, lo), axis=0)
                   for p in range(SB_FWD_PAIRS)]

        def pair_tiles(ref, j):
            t = _tile(ref, j)
            return [t[:, p * LANES:(p + 1) * LANES] for p in range(SB_FWD_PAIRS)]

        def tile_pass(j, carries, diagonal):
            ks, vs = pair_tiles(k_ref, j), pair_tiles(v_ref, j)
            zs = []
            for p in range(SB_FWD_PAIRS):
                z2 = lax.dot_general(q_stack[p], ks[p], _NT, preferred_element_type=F32)
                zs += [z2[:BLOCK], z2[BLOCK:]]
            logs = [_log_sigmoids(z) for z in zs]
            cums = _cumsum_mxu_many([jnp.where(mask, lm, 0.0) if diagonal else lm for _, lm in logs], rv)
            probs, new_c = [], []
            for h in range(n_heads):
                after, rs = cums[h]
                if diagonal:
                    a = jnp.where(mask, jnp.exp(logs[h][0] + after), 0.0)
                    new_c.append(rs)
                else:
                    a = jnp.exp(logs[h][0] + after + carries[h])
                    new_c.append(carries[h] + rs)
                probs.append(a.astype(BF16))
            outs = []
            for p in range(SB_FWD_PAIRS):
                pv = jnp.dot(jnp.concatenate(probs[2 * p:2 * p + 2], axis=0), vs[p], preferred_element_type=F32)
                outs.append(jnp.where(lo, pv[:BLOCK], pv[BLOCK:]))
            return new_c, outs

        carries, accs = tile_pass(i, None, True)

        def live(cs):
            top = cs[0]
            for c in cs[1:]:
                top = jnp.maximum(top, c)
            return jnp.max(top) > SWEEP_EXIT

        def cond(st):
            return (st[0] < i) & st[1]

        def step(st):
            jj, _, cs, accs = st
            new_c, outs = tile_pass(i - 1 - jj, cs, False)
            return jj + 1, live(new_c), new_c, [acc + o for acc, o in zip(accs, outs)]

        st = lax.while_loop(cond, step, (jnp.int32(0), live(carries), carries, accs))
        for p in range(SB_FWD_PAIRS):
            o_ref[:, p * LANES:(p + 1) * LANES] = st[3][p].astype(BF16)

    return pl.pallas_call(
        body, name="attn_b_fwd", grid=(NB, HD // width, S // BLOCK),
        in_specs=[blk, k_full, v_full, mat], out_specs=blk,
        out_shape=jax.ShapeDtypeStruct((NB, S, HD), BF16),
        compiler_params=_params(("parallel", "parallel", "arbitrary")),
    )(qkv, qkv, qkv, rev)


def _attn_b_bwd(qkv, do, rev, fwd):
    NB, S, W = qkv.shape
    HD = W // 3
    width = SB_BWD_PAIRS * LANES
    n_heads = 2 * SB_BWD_PAIRS
    nj = S // BLOCK
    blk, k_full, v_full, mat = _sb_specs(S, HD, width)
    acc_full = pl.BlockSpec((None, S, width), lambda b, p, i: (b, 0, p))

    def body(q_ref, do_ref, k_ref, v_ref, rev_ref, fwd_ref, dq_ref, dk_ref, dv_ref, sig_s, a_s, e_s):
        i = pl.program_id(2)

        @pl.when(i == 0)
        def _():
            dk_ref[...] = jnp.zeros_like(dk_ref)
            dv_ref[...] = jnp.zeros_like(dv_ref)

        rv, fw = rev_ref[...], fwd_ref[...]
        mask = _strict_mask()
        lo = lax.broadcasted_iota(jnp.int32, (BLOCK, LANES), 1) < HEAD_DIM
        pairs = range(SB_BWD_PAIRS)

        def cols(p):
            return slice(p * LANES, (p + 1) * LANES)

        q_stack = [jnp.concatenate(_head_halves(q_ref[:, cols(p)], lo), axis=0) for p in pairs]
        qs_stack = [q * ATTN_SCALE for q in q_stack]
        do_stack = [jnp.concatenate(_head_halves(do_ref[:, cols(p)], lo), axis=0) for p in pairs]

        def sweep1_tile(j, carries, diagonal):
            kj, vj = _tile(k_ref, j), _tile(v_ref, j)
            zs, das = [], []
            for p in pairs:
                z2 = lax.dot_general(qs_stack[p], kj[:, cols(p)], _NT, preferred_element_type=F32)
                da2 = lax.dot_general(do_stack[p], vj[:, cols(p)], _NT, preferred_element_type=F32)
                zs += [z2[:BLOCK], z2[BLOCK:]]
                das += [da2[:BLOCK], da2[BLOCK:]]
            logs = [_log_sigmoids(z) for z in zs]
            cums = _cumsum_mxu_many([jnp.where(mask, lm, 0.0) if diagonal else lm for _, lm in logs], rv)
            new_c = []
            for h in range(n_heads):
                lb, (after, rs) = logs[h][0], cums[h]
                if diagonal:
                    a = jnp.where(mask, jnp.exp(lb + after), 0.0)
                    new_c.append(rs)
                else:
                    a = jnp.exp(lb + after + carries[h])
                    new_c.append(carries[h] + rs)
                sig_s[h, j] = jnp.exp(lb)
                a_s[h, j] = a.astype(BF16)
                e_s[h, j] = das[h] * a
            return new_c

        carries = sweep1_tile(i, None, True)

        def live(cs):
            top = cs[0]
            for c in cs[1:]:
                top = jnp.maximum(top, c)
            return jnp.max(top) > SWEEP_EXIT

        def cond(st):
            return (st[0] < i) & st[1]

        def sweep1(st):
            new_c = sweep1_tile(i - 1 - st[0], st[2], False)
            return st[0] + 1, live(new_c), new_c

        visited = lax.while_loop(cond, sweep1, (jnp.int32(0), live(carries), carries))[0]

        def grads(j, st, diagonal):
            prefixes, dqs = st
            kj = _tile(k_ref, j)
            es = [e_s[h, j] for h in range(n_heads)]
            cums = _cumsum_mxu_many(es, fw)
            dzs, new_p = [], []
            for h in range(n_heads):
                sg = sig_s[h, j]
                e_before, rs = cums[h]
                dz = (es[h] * (1.0 - sg) - (e_before + prefixes[h]) * sg) * ATTN_SCALE
                if diagonal:
                    dz = jnp.where(mask, dz, 0.0)
                dzs.append(dz.astype(BF16))
                new_p.append(prefixes[h] + rs)
            rows = pl.ds(pl.multiple_of(j * BLOCK, BLOCK), BLOCK)
            new_dq = []
            for p in pairs:
                dz_stack = jnp.concatenate(dzs[2 * p:2 * p + 2], axis=0)
                a_stack = jnp.concatenate([a_s[2 * p, j], a_s[2 * p + 1, j]], axis=0)
                dq2 = jnp.dot(dz_stack, kj[:, cols(p)], preferred_element_type=F32)
                new_dq.append(dqs[p] + jnp.where(lo, dq2[:BLOCK], dq2[BLOCK:]))
                dk_ref[rows, cols(p)] += lax.dot_general(dz_stack, q_stack[p], _TN, preferred_element_type=F32)
                dv_ref[rows, cols(p)] += lax.dot_general(a_stack, do_stack[p], _TN, preferred_element_type=F32)
            return new_p, new_dq

        zeros = jnp.zeros((BLOCK, BLOCK), F32)
        st = lax.fori_loop(i - visited, i, lambda j, st: grads(j, st, False),
                           ([zeros] * n_heads, [zeros] * SB_BWD_PAIRS))
        dqs = grads(i, st, True)[1]
        for p in pairs:
            dq_ref[:, cols(p)] = dqs[p]

    f32_stash = pltpu.VMEM((n_heads, nj, BLOCK, BLOCK), F32)
    bf16_stash = pltpu.VMEM((n_heads, nj, BLOCK, BLOCK), BF16)
    return pl.pallas_call(
        body, name="attn_b_bwd", grid=(NB, HD // width, nj),
        in_specs=[blk, blk, k_full, v_full, mat, mat], out_specs=[blk, acc_full, acc_full],
        out_shape=[jax.ShapeDtypeStruct((NB, S, HD), F32)] * 3,
        scratch_shapes=[f32_stash, bf16_stash, f32_stash],
        compiler_params=_params(("parallel", "parallel", "arbitrary")),
    )(qkv, do, qkv, qkv, rev, fwd)


def _ada_fwd(c_all, w, b):
    L, D, N = w.shape
    B = c_all.shape[0]

    def body(c_ref, w_ref, b_ref, o_ref):
        cv = c_ref[...]
        cond = (cv * _sigmoid(cv)).astype(BF16)
        o_ref[...] = jnp.dot(cond, w_ref[...].astype(BF16), preferred_element_type=F32) + b_ref[...]

    return pl.pallas_call(
        body, name="ada_fwd", grid=(L,),
        in_specs=[pl.BlockSpec((B, D), lambda l: (0, 0)), pl.BlockSpec((None, D, N), lambda l: (l, 0, 0)),
                  pl.BlockSpec((None, 1, N), lambda l: (l, 0, 0))],
        out_specs=pl.BlockSpec((None, B, N), lambda l: (l, 0, 0)),
        out_shape=jax.ShapeDtypeStruct((L, B, N), F32),
        compiler_params=_params(("parallel",)),
    )(c_all, w, b)


def _ada_bwd(c_all, dmod_all, dmod_shard):
    L, B, N = dmod_shard.shape
    D = c_all.shape[1]
    N_all = dmod_all.shape[2]

    def body(c_ref, da_ref, ds_ref, gw_ref, gb_ref):
        cv = c_ref[...]
        cond = (cv * _sigmoid(cv)).astype(BF16)
        gw_ref[...] = lax.dot_general(cond, ds_ref[...].astype(BF16), _TN, preferred_element_type=F32)
        gb_ref[...] = jnp.sum(da_ref[...], axis=0, keepdims=True)

    return pl.pallas_call(
        body, name="ada_bwd", grid=(L,),
        in_specs=[pl.BlockSpec((B, D), lambda l: (0, 0)), pl.BlockSpec((None, B, N_all), lambda l: (l, 0, 0)),
                  pl.BlockSpec((None, B, N), lambda l: (l, 0, 0))],
        out_specs=[pl.BlockSpec((None, D, N), lambda l: (l, 0, 0)), pl.BlockSpec((None, 1, N_all), lambda l: (l, 0, 0))],
        out_shape=[jax.ShapeDtypeStruct((L, D, N), F32), jax.ShapeDtypeStruct((L, 1, N_all), F32)],
        compiler_params=_params(("parallel",)),
    )(c_all, dmod_all, dmod_shard)


def _adamw(w, g, m, v, name):
    shape = w.shape
    C = shape[-1]
    R = w.size // C
    tr = _pick(R, max(8, (1 << 18) // C), 8)
    c1 = 1.0 - ADAM_B1 ** ADAM_STEP
    c2 = 1.0 - ADAM_B2 ** ADAM_STEP

    def body(w_ref, g_ref, m_ref, v_ref, d_ref, nm_ref, nv_ref):
        gv = g_ref[...]
        nm = ADAM_B1 * m_ref[...] + (1.0 - ADAM_B1) * gv
        nv = ADAM_B2 * v_ref[...] + (1.0 - ADAM_B2) * (gv * gv)
        d_ref[...] = -ADAM_LR * ((nm / c1) / (jnp.sqrt(nv / c2) + ADAM_EPS) + ADAM_WD * w_ref[...])
        nm_ref[...] = nm
        nv_ref[...] = nv

    spec = pl.BlockSpec((tr, C), lambda r: (r, 0))
    out = pl.pallas_call(
        body, name=name, grid=(R // tr,), in_specs=[spec] * 4, out_specs=[spec] * 3,
        out_shape=[jax.ShapeDtypeStruct((R, C), F32)] * 3,
        compiler_params=_params(("parallel",)),
    )(*[t.reshape(R, C) for t in (w, g, m, v)])
    return [t.reshape(shape) for t in out]


_SHARDED = (("wqkv_a", 2), ("wo_a", 1), ("wqkv_b", 2), ("wo_b", 1), ("w_gate", 2), ("w_up", 2), ("w_down", 1))


def _pack_full(layers, axis, gate_up=None):
    L = len(layers)
    R, C = layers[0].shape

    def shards(m):
        if gate_up is not None:
            F = C // 2
            tf, Cs = _ff_tile(F), F // 4
            assert tf % Cs == 0
            starts = [(2 * (s * Cs // tf) + gate_up) * tf + s * Cs % tf for s in range(4)]
            return jnp.stack([m[:, st:st + Cs] for st in starts])
        if axis == 2:
            return m.reshape(R, 4, C // 4).transpose(1, 0, 2)
        return m.reshape(4, R // 4, C)

    halves = [jnp.stack([shards(m) for m in layers[h * (L // 2):(h + 1) * (L // 2)]], axis=1) for h in range(2)]
    return jnp.stack(halves)


def _unpack_full(gathered, axis):
    _, Lh, Rs, Cs = gathered.shape
    t = gathered.reshape(4, 2, Lh, Rs, Cs)
    layers = []
    for h in range(2):
        for l in range(Lh):
            piece = t[:, h, l]
            if axis == 2:
                layers.append(piece.transpose(1, 0, 2).reshape(Rs, 4 * Cs))
            else:
                layers.append(piece.reshape(4 * Rs, Cs))
    return layers


def _sum_slabs(own, recv, name, with_bf16=False):
    C = own.shape[-1]
    out = _sum_leading(recv.reshape(recv.shape[0], -1, C), name, own=own.reshape(-1, C), with_bf16=with_bf16)
    if with_bf16:
        return out[0].reshape(own.shape), out[1].reshape(own.shape)
    return out.reshape(own.shape)


def _gather8(x, name):
    return _all_gather8([x], name)[0]


def _rope_tables(positions):
    half = ROT_DIM // 2
    inv_freq = jnp.power(jnp.float32(ROPE_THETA), -jnp.arange(half, dtype=F32) * 2.0 / ROT_DIM)
    ang = positions.astype(F32).reshape(-1, 1) * inv_freq
    cos, sin = jnp.cos(ang), jnp.sin(ang)
    T = ang.shape[0]
    rest = HEAD_DIM - ROT_DIM
    c64 = jnp.concatenate([cos, cos, jnp.ones((T, rest), F32)], axis=1)
    s64 = jnp.concatenate([-sin, sin, jnp.zeros((T, rest), F32)], axis=1)
    return jnp.tile(c64, (1, 2)), jnp.tile(s64, (1, 2))


def _gain_rows(q_gain, k_gain):
    q2 = jnp.tile(q_gain.reshape(1, HEAD_DIM), (GROUP_A, 2))
    k2 = jnp.tile(k_gain.reshape(1, HEAD_DIM), (1, 2))
    return jnp.concatenate([q2, k2, jnp.ones((1, LANES), F32)], axis=0)


def _local_step(x, positions, mod, norm1_g, norm2_g, q_norm_a, k_norm_a, sinks_a,
                wqkv_a, wo_a, wqkv_b, wo_b, wgu, wd, loss_target):
    NB, S, D = x.shape
    T = NB * S
    QA = N_Q_A * HEAD_DIM
    tab_c, tab_s = _rope_tables(positions)
    rev, fwd = _cumsum_mats()

    saved = []
    xc = x
    for i in range(DEPTH):
        j = i // 2
        sh1, sc1, g1, sh2, sc2, g2 = [mod[i][:, k * D:(k + 1) * D].reshape(NB, 1, D) for k in range(6)]
        st = dict(x=xc, sc1=sc1, g1=g1, sc2=sc2, g2=g2)
        h = _norm_mod_fwd(xc, norm1_g[i:i + 1], sc1, sh1)
        st["h"] = h.reshape(T, D)
        if i % 2 == 0:
            st["qkv"] = _matmul(st["h"], wqkv_a[j], "nn", F32, "qkv_a")
            st["gains"] = _gain_rows(q_norm_a[j], k_norm_a[j])
            st["qkn"] = _qk_prep_fwd(st["qkv"], tab_c, tab_s, st["gains"]).reshape(NB, S, -1)
            st["o"] = _attn_a_fwd(st["qkn"], sinks_a[j]).reshape(T, QA)
            y = _matmul(st["o"], wo_a[j], "nn", F32, "wo_a")
        else:
            st["qkv"] = _matmul(st["h"], wqkv_b[j], "nn", BF16, "qkv_b").reshape(NB, S, -1)
            st["o"] = _attn_b_fwd(st["qkv"], rev).reshape(T, N_H_B * HEAD_DIM)
            y = _matmul(st["o"], wo_b[j], "nn", F32, "wo_b")
        st["y"] = y.reshape(NB, S, D)
        x1 = _gate_res(xc, st["y"], g1)
        st["x1"] = x1
        h2 = _norm_mod_fwd(x1, norm2_g[i:i + 1], sc2, sh2)
        st["h2"] = h2.reshape(T, D)
        st["gu"], st["act"] = _matmul(st["h2"], wgu[i], "nn", F32, "gate_up", swiglu=True)
        st["m"] = _matmul(st["act"], wd[i], "nn", F32, "down").reshape(NB, S, D)
        xc = _gate_res(x1, st["m"], g2)
        saved.append(st)

    loss, dx = _loss_fwd_bwd(xc, loss_target)

    grads = {name: [None] * n for name, n in
             (("wqkv_a", 2), ("wo_a", 2), ("wqkv_b", 2), ("wo_b", 2), ("wgu", DEPTH), ("wd", DEPTH),
              ("norm1_g", DEPTH), ("norm2_g", DEPTH), ("q_norm_a", 2), ("k_norm_a", 2), ("sinks_a", 2))}
    dmod = [None] * DEPTH
    for i in reversed(range(DEPTH)):
        j = i // 2
        st = saved[i]
        dm, dg2 = _gate_res_bwd(dx, st["m"], st["g2"])
        dm = dm.reshape(T, D)
        dact = _matmul(dm, wd[i], "nt", F32, "d_act")
        grads["wd"][i] = _matmul(st["act"], dm, "tn", F32, "d_wd")
        dgu = _swiglu_bwd(st["gu"], dact)
        dh2 = _matmul(dgu, wgu[i], "nt", F32, "d_h2")
        grads["wgu"][i] = _matmul(st["h2"], dgu, "tn", F32, "d_wgu")
        dx1, dsh2, dsc2, grads["norm2_g"][i] = _norm_mod_bwd(
            st["x1"], norm2_g[i:i + 1], st["sc2"], dh2.reshape(NB, S, D), dx)
        dy, dg1 = _gate_res_bwd(dx1, st["y"], st["g1"])
        dy = dy.reshape(T, D)
        if i % 2 == 0:
            do = _matmul(dy, wo_a[j], "nt", BF16, "d_o_a").reshape(NB, S, QA)
            grads["wo_a"][j] = _matmul(st["o"], dy, "tn", F32, "d_wo_a")
            dq, dk, dv, dsink = _attn_a_bwd(st["qkn"], do, sinks_a[j])
            dqkv, dgain = _qk_prep_bwd(st["qkv"], dq.reshape(T, QA), dk.reshape(T, LANES), dv.reshape(T, LANES),
                                       tab_c, tab_s, st["gains"])
            dh = _matmul(dqkv, wqkv_a[j], "nt", F32, "d_h_a")
            grads["wqkv_a"][j] = _matmul(st["h"], dqkv, "tn", F32, "d_wqkv_a")
            grads["q_norm_a"][j] = jnp.sum(dgain[:GROUP_A].reshape(2 * GROUP_A, HEAD_DIM), axis=0)
            grads["k_norm_a"][j] = jnp.sum(dgain[GROUP_A].reshape(2, HEAD_DIM), axis=0)
            grads["sinks_a"][j] = jnp.sum(dsink[..., 0], axis=0)
        else:
            do = _matmul(dy, wo_b[j], "nt", BF16, "d_o_b").reshape(NB, S, -1)
            grads["wo_b"][j] = _matmul(st["o"], dy, "tn", F32, "d_wo_b")
            dq, dk, dv = _attn_b_bwd(st["qkv"], do, rev, fwd)
            dqkv = jnp.concatenate([dq, dk, dv], axis=-1).reshape(T, -1).astype(BF16)
            dh = _matmul(dqkv, wqkv_b[j], "nt", F32, "d_h_b")
            grads["wqkv_b"][j] = _matmul(st["h"], dqkv, "tn", F32, "d_wqkv_b")
        dx, dsh1, dsc1, grads["norm1_g"][i] = _norm_mod_bwd(
            st["x"], norm1_g[i:i + 1], st["sc1"], dh.reshape(NB, S, D), dx1)
        dmod[i] = jnp.concatenate([dsh1, dsc1, dg1, dsh2, dsc2, dg2], axis=-1).reshape(NB, 6 * D)

    matrices = ("wqkv_a", "wo_a", "wqkv_b", "wo_b", "wgu", "wd")
    grads = {name: parts if name in matrices else jnp.stack(parts) for name, parts in grads.items()}
    return loss, dx, grads, jnp.stack(dmod)


def _rows_of(flat, cols=PACK_COLS):
    n = flat.shape[0]
    pad = (-n) % (8 * cols)
    if pad:
        flat = jnp.concatenate([flat, jnp.zeros((pad,), flat.dtype)])
    return flat.reshape(-1, cols)


def kernel(x, c, positions, ada_w, ada_b, norm1_g, norm2_g, wqkv_a, q_norm_a, k_norm_a, sinks_a, wo_a, wqkv_b, wo_b, w_gate, w_up, w_down, loss_target, m_ada_w, m_ada_b, m_norm1_g, m_norm2_g, m_wqkv_a, m_q_norm_a, m_k_norm_a, m_sinks_a, m_wo_a, m_wqkv_b, m_wo_b, m_w_gate, m_w_up, m_w_down, v_ada_w, v_ada_b, v_norm1_g, v_norm2_g, v_wqkv_a, v_q_norm_a, v_k_norm_a, v_sinks_a, v_wo_a, v_wqkv_b, v_wo_b, v_w_gate, v_w_up, v_w_down):
    xi, yi, ci = lax.axis_index("x"), lax.axis_index("y"), lax.axis_index("c")
    dev = 4 * xi + 2 * yi + ci
    chip = 2 * xi + yi
    NB, S, D = x.shape
    B_all = N_DEV * NB
    L = ada_w.shape[0]
    n_mod = ada_w.shape[2] // 2

    c_all = _gather8(_rows_of(c.reshape(-1), LANES), "gather_c").reshape(N_DEV, -1)[:, :NB * D].reshape(B_all, D)
    ada_w_half = lax.dynamic_slice_in_dim(ada_w, ci * n_mod, n_mod, axis=2)
    ada_b_half = lax.dynamic_slice_in_dim(ada_b, dev * n_mod, n_mod, axis=1).reshape(L, 1, n_mod)
    mod_part = _ada_fwd(c_all, ada_w_half, ada_b_half)
    n_part = L * B_all * n_mod
    mod_all = _gather8(_rows_of(mod_part.reshape(-1)), "gather_mod").reshape(N_DEV, -1)[:, :n_part]
    mod_all = mod_all.reshape(N_DEV, L, B_all, n_mod).transpose(1, 2, 0, 3).reshape(L, B_all, N_DEV * n_mod)
    mod = lax.dynamic_slice_in_dim(mod_all, dev * NB, NB, axis=1)

    shards = dict(wqkv_a=wqkv_a, wo_a=wo_a, wqkv_b=wqkv_b, wo_b=wo_b, w_gate=w_gate, w_up=w_up, w_down=w_down)
    halves = []
    for name, _ in _SHARDED:
        w = shards[name]
        half = lax.dynamic_index_in_dim(w.reshape((2, w.shape[0] // 2) + w.shape[1:]), ci, 0, keepdims=False)
        halves.append(half.astype(BF16))
    gathered = _all_gather8(halves, "gather_weights", local_axis=1, local_chunks=8)
    full = {name: _unpack_full(t, axis) for (name, axis), t in zip(_SHARDED, gathered)}
    wgu = [_interleave(gate, up) for gate, up in zip(full["w_gate"], full["w_up"])]

    loss, grad_x, g, dmod = _local_step(
        x, positions, mod, norm1_g, norm2_g, q_norm_a, k_norm_a, sinks_a,
        full["wqkv_a"], full["wo_a"], full["wqkv_b"], full["wo_b"], wgu, full["w_down"], loss_target)

    g_full = dict(wqkv_a=g["wqkv_a"], wo_a=g["wo_a"], wqkv_b=g["wqkv_b"], wo_b=g["wo_b"],
                  w_gate=g["wgu"], w_up=g["wgu"], w_down=g["wd"])
    which = dict(w_gate=0, w_up=1)
    packed = [_pack_full(g_full[name], axis, which.get(name)) for name, axis in _SHARDED]
    def own(t, index):
        return lax.dynamic_index_in_dim(t, index, 0, keepdims=False)

    from_cores = _exchange(packed, "c", "rs_cores", chunk_axis=0, chunks=4)
    chip_part = [_sum_slabs(own(p, ci), r, "rs_add_cores", with_bf16=True) for p, r in zip(packed, from_cores)]
    from_chips = _exchange([b for _, b in chip_part], "xy", "rs_chips")
    mine = [_sum_slabs(own(p, chip), r, "rs_add_chips") for (p, _), r in zip(chip_part, from_chips)]
    theirs = _sibling_send(mine, "rs_halves")
    grad = {}
    for (name, _), m, t in zip(_SHARDED, mine, theirs):
        first, second = jnp.where(ci == 0, m, t), jnp.where(ci == 0, t, m)
        grad[name] = jnp.stack([first, second]).reshape(shards[name].shape)

    small_names = ("norm1_g", "norm2_g", "q_norm_a", "k_norm_a", "sinks_a")
    small = [dmod.reshape(-1)] + [g[name].reshape(-1) for name in small_names] + [loss.reshape(-1)]
    small_sizes = [t.shape[0] for t in small]
    small_rows = _rows_of(jnp.concatenate(small))
    small_all = _gather8(small_rows, "gather_small")
    small_sum = _sum_leading(small_all, "sum_small").reshape(-1)
    n_dmod = small_sizes[0]
    dmod_all = small_all.reshape(N_DEV, -1)[:, :n_dmod].reshape(N_DEV, L, NB, 6 * D)
    dmod_all = dmod_all.transpose(1, 0, 2, 3).reshape(L, B_all, 6 * D)
    off = n_dmod
    for name, sz in zip(small_names + ("loss",), small_sizes[1:]):
        grad[name] = small_sum[off:off + sz]
        off += sz
    loss_total = grad.pop("loss").reshape(())
    for name, ref in (("norm1_g", norm1_g), ("norm2_g", norm2_g), ("q_norm_a", q_norm_a),
                      ("k_norm_a", k_norm_a), ("sinks_a", sinks_a)):
        grad[name] = grad[name].reshape(ref.shape)

    n_shard = ada_w.shape[2]
    dmod_shard = lax.dynamic_slice_in_dim(dmod_all, chip * n_shard, n_shard, axis=2)
    grad["ada_w"], gb = _ada_bwd(c_all, dmod_all, dmod_shard)
    grad["ada_b"] = gb.reshape(ada_b.shape)

    weights = dict(ada_w=ada_w, ada_b=ada_b, norm1_g=norm1_g, norm2_g=norm2_g, wqkv_a=wqkv_a, q_norm_a=q_norm_a,
                   k_norm_a=k_norm_a, sinks_a=sinks_a, wo_a=wo_a, wqkv_b=wqkv_b, wo_b=wo_b, w_gate=w_gate,
                   w_up=w_up, w_down=w_down)
    m_in = dict(ada_w=m_ada_w, ada_b=m_ada_b, norm1_g=m_norm1_g, norm2_g=m_norm2_g, wqkv_a=m_wqkv_a,
                q_norm_a=m_q_norm_a, k_norm_a=m_k_norm_a, sinks_a=m_sinks_a, wo_a=m_wo_a, wqkv_b=m_wqkv_b,
                wo_b=m_wo_b, w_gate=m_w_gate, w_up=m_w_up, w_down=m_w_down)
    v_in = dict(ada_w=v_ada_w, ada_b=v_ada_b, norm1_g=v_norm1_g, norm2_g=v_norm2_g, wqkv_a=v_wqkv_a,
                q_norm_a=v_q_norm_a, k_norm_a=v_k_norm_a, sinks_a=v_sinks_a, wo_a=v_wo_a, wqkv_b=v_wqkv_b,
                wo_b=v_wo_b, w_gate=v_w_gate, w_up=v_w_up, w_down=v_w_down)
    names = list(weights)
    delta, new_m, new_v = {}, {}, {}
    for name in names:
        delta[name], new_m[name], new_v[name] = _adamw(weights[name], grad[name], m_in[name], v_in[name],
                                                       "adamw_" + name)
    return (loss_total, grad_x, *[grad[k] for k in names], *[delta[k] for k in names],
            *[new_m[k] for k in names], *[new_v[k] for k in names])
```

```python
import jax
import jax.numpy as jnp
from jax import lax
from jax.experimental import pallas as pl
from jax.experimental.pallas import tpu as pltpu

F32 = jnp.float32
BF16 = jnp.bfloat16

DEPTH = 4
HEAD_DIM = 64
N_Q_A = 16
N_KV_A = 2
GROUP_A = N_Q_A // N_KV_A
N_H_B = 16
BLOCK = 128
ROT_DIM = HEAD_DIM // 4
ROPE_THETA = 500000.0
EPS = 1e-6
ATTN_SCALE = HEAD_DIM ** -0.5
NEG_BIG = -1e30

ADAM_LR = 0.001
ADAM_B1 = 0.9
ADAM_B2 = 0.999
ADAM_EPS = 1e-08
ADAM_WD = 0.01
ADAM_STEP = 10

N_DEV = 8
LANES = 128
PACK_COLS = 1024
VMEM_LIMIT_BYTES = 48 * 1024 * 1024
MESH = pl.DeviceIdType.MESH

_NT = (((1,), (1,)), ((), ()))
_TN = (((0,), (0,)), ((), ()))
_NN = (((1,), (0,)), ((), ()))


def _params(sem=None):
    return pltpu.CompilerParams(vmem_limit_bytes=VMEM_LIMIT_BYTES, dimension_semantics=sem)


def _pick(n, cap, mult):
    best = None
    for t in range(mult, min(n, cap) + 1, mult):
        if n % t == 0:
            best = t
    return n if best is None else best


_ANY = pl.BlockSpec(memory_space=pl.ANY)


def _window(index, axis, q, n, shape):
    rest = [slice(None)] * len(shape)
    size = shape[axis] // n
    rest[axis] = pl.ds(q * size, size)
    return tuple(index) + tuple(rest)


def _all_gather8(xs, name, local_axis=0, local_chunks=1):
    n = len(xs)

    def body(*refs):
        x_refs, out_refs = refs[:n], refs[n:2 * n]
        send_sems, recv_sems, local_sems = refs[2 * n:]
        xi, yi, ci = lax.axis_index("x"), lax.axis_index("y"), lax.axis_index("c")
        me, sibling = (xi, yi, ci), (xi, yi, 1 - ci)
        chips = [(1 - xi, yi), (xi, 1 - yi), (1 - xi, 1 - yi)]

        def slab(w, px, py, pc):
            return out_refs[w].at[4 * px + 2 * py + pc]

        def copy(w, k, block, to, src=None):
            return pltpu.make_async_remote_copy(
                src_ref=slab(w, *block) if src is None else src, dst_ref=slab(w, *block),
                send_sem=send_sems.at[k, w], recv_sem=recv_sems.at[k, w], device_id=to, device_id_type=MESH)

        mine = []
        for w in range(n):
            for q in range(local_chunks):
                part = _window((), local_axis, q, local_chunks, xs[w].shape)
                mine.append(pltpu.make_async_copy(x_refs[w].at[part], slab(w, *me).at[part], local_sems.at[w, q]))
                mine[-1].start()
        first = [copy(w, 0, me, sibling, src=x_refs[w]) for w in range(n)]
        first += [copy(w, 1 + j, me, (*chip, ci), src=x_refs[w]) for j, chip in enumerate(chips) for w in range(n)]
        for cp in first:
            cp.start()
        passed = []
        for j, chip in enumerate(chips):
            for w in range(n):
                copy(w, 1 + j, (*chip, ci), me).wait_recv()
                passed.append(copy(w, 4 + j, (*chip, ci), sibling))
                passed[-1].start()
        for w in range(n):
            copy(w, 0, sibling, me).wait_recv()
        for j, chip in enumerate(chips):
            for w in range(n):
                copy(w, 4 + j, (*chip, 1 - ci), me).wait_recv()
        for cp in first + passed:
            cp.wait_send()
        for cp in mine:
            cp.wait()

    return pl.pallas_call(
        body, name=name,
        out_shape=[jax.ShapeDtypeStruct((N_DEV,) + x.shape, x.dtype) for x in xs],
        in_specs=[_ANY] * n, out_specs=[_ANY] * n,
        scratch_shapes=[pltpu.SemaphoreType.DMA((7, n)), pltpu.SemaphoreType.DMA((7, n)),
                        pltpu.SemaphoreType.DMA((n, local_chunks))],
    )(*xs)


def _exchange(xs, group, name, chunk_axis=0, chunks=1):
    n = len(xs)
    n_peers = 1 if group == "c" else 3

    def body(*refs):
        x_refs, out_refs = refs[:n], refs[n:2 * n]
        send_sems, recv_sems = refs[2 * n:]
        xi, yi, ci = lax.axis_index("x"), lax.axis_index("y"), lax.axis_index("c")
        if group == "c":
            peers = [(1 - ci, (xi, yi, 1 - ci))]
        else:
            peers = [(2 * (1 - xi) + yi, (1 - xi, yi, ci)),
                     (2 * xi + (1 - yi), (xi, 1 - yi, ci)),
                     (2 * (1 - xi) + (1 - yi), (1 - xi, 1 - yi, ci))]
        copies = []
        for k, (p, dev) in enumerate(peers):
            for w in range(n):
                slab_shape = xs[w].shape[1:]
                for q in range(chunks):
                    copies.append(pltpu.make_async_remote_copy(
                        src_ref=x_refs[w].at[_window((p,), chunk_axis, q, chunks, slab_shape)],
                        dst_ref=out_refs[w].at[_window((k,), chunk_axis, q, chunks, slab_shape)],
                        send_sem=send_sems.at[k, w, q], recv_sem=recv_sems.at[k, w, q],
                        device_id=dev, device_id_type=MESH))
                    copies[-1].start()
        for cp in copies:
            cp.wait()

    return pl.pallas_call(
        body, name=name,
        out_shape=[jax.ShapeDtypeStruct((n_peers,) + x.shape[1:], x.dtype) for x in xs],
        in_specs=[_ANY] * n, out_specs=[_ANY] * n,
        scratch_shapes=[pltpu.SemaphoreType.DMA((n_peers, n, chunks)), pltpu.SemaphoreType.DMA((n_peers, n, chunks))],
    )(*xs)


def _sibling_send(xs, name, chunk_axis=1, chunks=4):
    n = len(xs)

    def body(*refs):
        x_refs, out_refs = refs[:n], refs[n:2 * n]
        send_sems, recv_sems = refs[2 * n:]
        xi, yi, ci = lax.axis_index("x"), lax.axis_index("y"), lax.axis_index("c")
        copies = []
        for w in range(n):
            for q in range(chunks):
                part = _window((), chunk_axis, q, chunks, xs[w].shape)
                copies.append(pltpu.make_async_remote_copy(
                    src_ref=x_refs[w].at[part], dst_ref=out_refs[w].at[part],
                    send_sem=send_sems.at[w, q], recv_sem=recv_sems.at[w, q],
                    device_id=(xi, yi, 1 - ci), device_id_type=MESH))
                copies[-1].start()
        for cp in copies:
            cp.wait()

    return pl.pallas_call(
        body, name=name,
        out_shape=[jax.ShapeDtypeStruct(x.shape, x.dtype) for x in xs],
        in_specs=[_ANY] * n, out_specs=[_ANY] * n,
        scratch_shapes=[pltpu.SemaphoreType.DMA((n, chunks)), pltpu.SemaphoreType.DMA((n, chunks))],
    )(*xs)


def _sum_leading(x, name, own=None, with_bf16=False):
    P, R, C = x.shape
    tr = _pick(R, max(16, (1 << 19) // (C * (P + 1))), 16)

    def body(*refs):
        n_in = 1 if own is None else 2
        x_ref = refs[n_in - 1]
        acc = x_ref[0].astype(F32) if own is None else refs[0][...] + x_ref[0].astype(F32)
        for p in range(1, P):
            acc = acc + x_ref[p].astype(F32)
        refs[n_in][...] = acc
        if with_bf16:
            refs[n_in + 1][...] = acc.astype(BF16)

    flat = pl.BlockSpec((tr, C), lambda r: (r, 0))
    slabs = pl.BlockSpec((P, tr, C), lambda r: (0, r, 0))
    out = pl.pallas_call(
        body, name=name, grid=(R // tr,),
        in_specs=[slabs] if own is None else [flat, slabs],
        out_specs=[flat, flat] if with_bf16 else [flat],
        out_shape=[jax.ShapeDtypeStruct((R, C), F32)] + ([jax.ShapeDtypeStruct((R, C), BF16)] if with_bf16 else []),
        compiler_params=_params(("arbitrary",)),
    )(*([x] if own is None else [own, x]))
    return out if with_bf16 else out[0]


MATMUL_SINGLE_K = 1280
MATMUL_VMEM_BUDGET = 36 * 1024 * 1024


def _matmul(a, b, mode, out_dtype, name, swiglu=False):
    if mode == "nn":
        (M, K), N = a.shape, b.shape[1]
    elif mode == "nt":
        (M, K), N = a.shape, b.shape[0]
    else:
        (K, M), N = a.shape, b.shape[1]
    tm = _pick(M, 1024 if mode != "tn" else 1536, 128)
    tn = _pick(N, 1536, 128)
    if swiglu:
        tm, tn = _pick(M, 512, 128), 2 * _ff_tile(N // 2)
    out_bytes = jnp.dtype(out_dtype).itemsize
    tk = K
    if K > MATMUL_SINGLE_K:
        for cap in (2048, 1024, 512):
            tk = _pick(K, cap, 128)
            blocks = 2 * 2 * tk * (tm + tn) + tm * tn * (2 * out_bytes + (4 if out_dtype != F32 else 0))
            if blocks <= MATMUL_VMEM_BUDGET:
                break
    nk = K // tk
    dims = {"nn": _NN, "nt": _NT, "tn": _TN}[mode]
    use_scratch = nk > 1 and out_dtype != F32

    def body(a_ref, b_ref, *refs):
        o_ref = refs[0]

        def product():
            return lax.dot_general(a_ref[...].astype(BF16), b_ref[...].astype(BF16), dims,
                                   preferred_element_type=F32)

        if nk == 1:
            part = product()
            o_ref[...] = part.astype(o_ref.dtype)
            if swiglu:
                g = part[:, :tn // 2]
                refs[1][...] = (g * _sigmoid(g) * part[:, tn // 2:]).astype(BF16)
            return
        k = pl.program_id(2)
        acc_ref = refs[-1] if use_scratch else o_ref

        @pl.when(k == 0)
        def _():
            acc_ref[...] = jnp.zeros_like(acc_ref)

        acc_ref[...] += product()

        if use_scratch:
            @pl.when(k == nk - 1)
            def _():
                o_ref[...] = acc_ref[...].astype(o_ref.dtype)

    if mode == "tn":
        a_spec = pl.BlockSpec((tk, tm), lambda i, j, k: (k, i))
    else:
        a_spec = pl.BlockSpec((tm, tk), lambda i, j, k: (i, k))
    if mode == "nt":
        b_spec = pl.BlockSpec((tn, tk), lambda i, j, k: (j, k))
    else:
        b_spec = pl.BlockSpec((tk, tn), lambda i, j, k: (k, j))
    out_specs = [pl.BlockSpec((tm, tn), lambda i, j, k: (i, j))]
    out_shape = [jax.ShapeDtypeStruct((M, N), out_dtype)]
    if swiglu:
        assert nk == 1 and mode == "nn"
        out_specs.append(pl.BlockSpec((tm, tn // 2), lambda i, j, k: (i, j)))
        out_shape.append(jax.ShapeDtypeStruct((M, N // 2), BF16))
    out = pl.pallas_call(
        body, name=name, grid=(M // tm, N // tn, nk),
        in_specs=[a_spec, b_spec], out_specs=out_specs, out_shape=out_shape,
        scratch_shapes=[pltpu.VMEM((tm, tn), F32)] if use_scratch else [],
        compiler_params=_params(("parallel", "parallel", "arbitrary")),
    )(a, b)
    return out if swiglu else out[0]


def _row_tile(S):
    return _pick(S, 512, 8)


def _norm_mod_fwd(x, gain, sc, sh):
    NB, S, D = x.shape
    tr = _row_tile(S)

    def body(x_ref, g_ref, sc_ref, sh_ref, h_ref):
        xv = x_ref[...]
        ms = jnp.mean(xv * xv, axis=-1, keepdims=True)
        n = xv * lax.rsqrt(ms + EPS) * g_ref[...]
        h_ref[...] = (n * (1.0 + sc_ref[...]) + sh_ref[...]).astype(BF16)

    tok = pl.BlockSpec((None, tr, D), lambda b, r: (b, r, 0))
    per_ex = pl.BlockSpec((None, 1, D), lambda b, r: (b, 0, 0))
    return pl.pallas_call(
        body, name="norm_mod_fwd", grid=(NB, S // tr),
        in_specs=[tok, pl.BlockSpec((1, D), lambda b, r: (0, 0)), per_ex, per_ex],
        out_specs=tok, out_shape=jax.ShapeDtypeStruct((NB, S, D), BF16),
        compiler_params=_params(("parallel", "parallel")),
    )(x, gain, sc, sh)


def _norm_mod_bwd(x, gain, sc, dh, dres):
    NB, S, D = x.shape
    tr = _row_tile(S)

    def body(x_ref, g_ref, sc_ref, dh_ref, dres_ref, dx_ref, dsh_ref, dsc_ref, dg_ref):
        b, r = pl.program_id(0), pl.program_id(1)

        @pl.when(r == 0)
        def _():
            dsh_ref[...] = jnp.zeros_like(dsh_ref)
            dsc_ref[...] = jnp.zeros_like(dsc_ref)

        @pl.when((r == 0) & (b == 0))
        def _():
            dg_ref[...] = jnp.zeros_like(dg_ref)

        xv = x_ref[...]
        rstd = lax.rsqrt(jnp.mean(xv * xv, axis=-1, keepdims=True) + EPS)
        xh = xv * rstd
        g = g_ref[...]
        dh = dh_ref[...]
        dsh_ref[...] += jnp.sum(dh, axis=0, keepdims=True)
        dsc_ref[...] += jnp.sum(dh * (xh * g), axis=0, keepdims=True)
        dn = dh * (1.0 + sc_ref[...])
        dg_ref[...] += jnp.sum(dn * xh, axis=0, keepdims=True)
        dxh = dn * g
        proj = jnp.mean(dxh * xh, axis=-1, keepdims=True)
        dx_ref[...] = rstd * (dxh - xh * proj) + dres_ref[...]

    tok = pl.BlockSpec((None, tr, D), lambda b, r: (b, r, 0))
    per_ex = pl.BlockSpec((None, 1, D), lambda b, r: (b, 0, 0))
    row = pl.BlockSpec((1, D), lambda b, r: (0, 0))
    return pl.pallas_call(
        body, name="norm_mod_bwd", grid=(NB, S // tr),
        in_specs=[tok, row, per_ex, tok, tok],
        out_specs=[tok, per_ex, per_ex, row],
        out_shape=[jax.ShapeDtypeStruct((NB, S, D), F32), jax.ShapeDtypeStruct((NB, 1, D), F32),
                   jax.ShapeDtypeStruct((NB, 1, D), F32), jax.ShapeDtypeStruct((1, D), F32)],
        compiler_params=_params(("arbitrary", "arbitrary")),
    )(x, gain, sc, dh, dres)


def _gate_res(x, y, g):
    NB, S, D = x.shape
    tr = _row_tile(S)

    def body(x_ref, y_ref, g_ref, o_ref):
        o_ref[...] = x_ref[...] + g_ref[...] * y_ref[...]

    tok = pl.BlockSpec((None, tr, D), lambda b, r: (b, r, 0))
    per_ex = pl.BlockSpec((None, 1, D), lambda b, r: (b, 0, 0))
    return pl.pallas_call(
        body, name="gate_res", grid=(NB, S // tr), in_specs=[tok, tok, per_ex], out_specs=tok,
        out_shape=jax.ShapeDtypeStruct((NB, S, D), F32),
        compiler_params=_params(("parallel", "parallel")),
    )(x, y, g)


def _gate_res_bwd(dxo, y, g):
    NB, S, D = dxo.shape
    tr = _row_tile(S)

    def body(d_ref, y_ref, g_ref, dy_ref, dg_ref):
        @pl.when(pl.program_id(1) == 0)
        def _():
            dg_ref[...] = jnp.zeros_like(dg_ref)

        d = d_ref[...]
        dy_ref[...] = (d * g_ref[...]).astype(BF16)
        dg_ref[...] += jnp.sum(d * y_ref[...], axis=0, keepdims=True)

    tok = pl.BlockSpec((None, tr, D), lambda b, r: (b, r, 0))
    per_ex = pl.BlockSpec((None, 1, D), lambda b, r: (b, 0, 0))
    return pl.pallas_call(
        body, name="gate_res_bwd", grid=(NB, S // tr), in_specs=[tok, tok, per_ex], out_specs=[tok, per_ex],
        out_shape=[jax.ShapeDtypeStruct((NB, S, D), BF16), jax.ShapeDtypeStruct((NB, 1, D), F32)],
        compiler_params=_params(("arbitrary", "arbitrary")),
    )(dxo, y, g)


def _sigmoid(v):
    return 1.0 / (1.0 + jnp.exp(-v))


def _ff_tile(F):
    return _pick(F, 1536, 128)


def _interleave(gate, up):
    F = gate.shape[-1]
    tf = _ff_tile(F)
    parts = []
    for j in range(F // tf):
        parts += [gate[..., j * tf:(j + 1) * tf], up[..., j * tf:(j + 1) * tf]]
    return jnp.concatenate(parts, axis=-1)


def _deinterleave(gu):
    F = gu.shape[-1] // 2
    tf = _ff_tile(F)
    gate = [gu[..., 2 * j * tf:(2 * j + 1) * tf] for j in range(F // tf)]
    up = [gu[..., (2 * j + 1) * tf:(2 * j + 2) * tf] for j in range(F // tf)]
    return jnp.concatenate(gate, axis=-1), jnp.concatenate(up, axis=-1)


def _swiglu_bwd(dm, wd, gu):
    T, D = dm.shape
    F = wd.shape[0]
    tf = _ff_tile(F)
    tm = _pick(T, 512, 128)
    assert D <= MATMUL_SINGLE_K

    def body(a_ref, b_ref, gu_ref, o_ref):
        d = lax.dot_general(a_ref[...], b_ref[...], _NT, preferred_element_type=F32)
        g, u = gu_ref[:, :tf], gu_ref[:, tf:]
        s = _sigmoid(g)
        o_ref[:, :tf] = (d * u * (s * (1.0 + g * (1.0 - s)))).astype(BF16)
        o_ref[:, tf:] = (d * (g * s)).astype(BF16)

    return pl.pallas_call(
        body, name="swiglu_bwd", grid=(T // tm, F // tf),
        in_specs=[pl.BlockSpec((tm, D), lambda i, j: (i, 0)), pl.BlockSpec((tf, D), lambda i, j: (j, 0)),
                  pl.BlockSpec((tm, 2 * tf), lambda i, j: (i, j))],
        out_specs=pl.BlockSpec((tm, 2 * tf), lambda i, j: (i, j)),
        out_shape=jax.ShapeDtypeStruct((T, 2 * F), BF16),
        compiler_params=_params(("parallel", "parallel")),
    )(dm, wd, gu)


def _loss_fwd_bwd(y, target):
    NB, S, D = y.shape
    tr = _row_tile(S)

    def body(y_ref, t_ref, l_ref, d_ref):
        @pl.when((pl.program_id(0) == 0) & (pl.program_id(1) == 0))
        def _():
            l_ref[...] = jnp.zeros_like(l_ref)

        e = y_ref[...] - t_ref[...]
        d_ref[...] = e / D
        l_ref[...] += 0.5 * jnp.sum(jnp.mean(e * e, axis=-1, keepdims=True), axis=0, keepdims=True)

    tok = pl.BlockSpec((None, tr, D), lambda b, r: (b, r, 0))
    return pl.pallas_call(
        body, name="loss", grid=(NB, S // tr), in_specs=[tok, tok],
        out_specs=[pl.BlockSpec((1, 1), lambda b, r: (0, 0)), tok],
        out_shape=[jax.ShapeDtypeStruct((1, 1), F32), jax.ShapeDtypeStruct((NB, S, D), F32)],
        compiler_params=_params(("arbitrary", "arbitrary")),
    )(y, target)


def _half_sums(v, lo):
    sa = jnp.sum(jnp.where(lo, v, 0.0), axis=-1, keepdims=True)
    sb = jnp.sum(jnp.where(lo, 0.0, v), axis=-1, keepdims=True)
    return jnp.where(lo, sa, sb)


def _rope_swap(v, lane64):
    up = pltpu.roll(v, LANES - ROT_DIM // 2, 1)
    down = pltpu.roll(v, ROT_DIM // 2, 1)
    return jnp.where(lane64 < ROT_DIM // 2, up, jnp.where(lane64 < ROT_DIM, down, 0.0))


def _qk_prep_fwd(qkv, tab_c, tab_s, gains):
    T, W = qkv.shape
    R = W // LANES
    tt = _pick(T, 256, 8)

    def body(x_ref, c_ref, s_ref, g_ref, o_ref):
        lane = lax.broadcasted_iota(jnp.int32, (tt, LANES), 1)
        lo = lane < HEAD_DIM
        lane64 = lane & (HEAD_DIM - 1)
        c, s = c_ref[...], s_ref[...]
        for j in range(R - 1):
            cols = slice(j * LANES, (j + 1) * LANES)
            xv = x_ref[:, cols]
            rstd = lax.rsqrt(_half_sums(xv * xv, lo) / HEAD_DIM + EPS)
            yn = xv * rstd * g_ref[j:j + 1, :]
            o_ref[:, cols] = (yn * c + _rope_swap(yn, lane64) * s).astype(BF16)
        o_ref[:, (R - 1) * LANES:] = x_ref[:, (R - 1) * LANES:].astype(BF16)

    tok = pl.BlockSpec((tt, W), lambda t: (t, 0))
    tab = pl.BlockSpec((tt, LANES), lambda t: (t, 0))
    return pl.pallas_call(
        body, name="qk_prep_fwd", grid=(T // tt,),
        in_specs=[tok, tab, tab, pl.BlockSpec((R, LANES), lambda t: (0, 0))],
        out_specs=tok, out_shape=jax.ShapeDtypeStruct((T, W), BF16),
        compiler_params=_params(("parallel",)),
    )(qkv, tab_c, tab_s, gains)


def _qk_prep_bwd(qkv, dq, dk, dv, tab_c, tab_s, gains):
    T, W = qkv.shape
    R = W // LANES
    QW = dq.shape[1]
    tt = _pick(T, 256, 8)

    def body(x_ref, dq_ref, dk_ref, dv_ref, c_ref, s_ref, g_ref, o_ref, dg_ref):
        @pl.when(pl.program_id(0) == 0)
        def _():
            dg_ref[...] = jnp.zeros_like(dg_ref)

        lane = lax.broadcasted_iota(jnp.int32, (tt, LANES), 1)
        lo = lane < HEAD_DIM
        lane64 = lane & (HEAD_DIM - 1)
        c, s = c_ref[...], s_ref[...]
        for j in range(R - 1):
            cols = slice(j * LANES, (j + 1) * LANES)
            xv = x_ref[:, cols]
            d = dq_ref[:, cols] if j < R - 2 else dk_ref[...]
            rstd = lax.rsqrt(_half_sums(xv * xv, lo) / HEAD_DIM + EPS)
            xh = xv * rstd
            dyn = d * c + _rope_swap(d * s, lane64)
            dg_ref[j:j + 1, :] += jnp.sum(dyn * xh, axis=0, keepdims=True)
            dxh = dyn * g_ref[j:j + 1, :]
            proj = _half_sums(dxh * xh, lo) / HEAD_DIM
            o_ref[:, cols] = (rstd * (dxh - xh * proj)).astype(BF16)
        o_ref[:, (R - 1) * LANES:] = dv_ref[...].astype(BF16)

    tok = pl.BlockSpec((tt, W), lambda t: (t, 0))
    tab = pl.BlockSpec((tt, LANES), lambda t: (t, 0))
    gsp = pl.BlockSpec((R, LANES), lambda t: (0, 0))
    return pl.pallas_call(
        body, name="qk_prep_bwd", grid=(T // tt,),
        in_specs=[tok, pl.BlockSpec((tt, QW), lambda t: (t, 0)), tab, tab, tab, tab, gsp], out_specs=[tok, gsp],
        out_shape=[jax.ShapeDtypeStruct((T, W), BF16), jax.ShapeDtypeStruct((R, LANES), F32)],
        compiler_params=_params(("arbitrary",)),
    )(qkv, dq, dk, dv, tab_c, tab_s, gains)


def _band_mask(i):
    r = lax.broadcasted_iota(jnp.int32, (2 * BLOCK, 2 * BLOCK), 0) & (BLOCK - 1)
    c = lax.broadcasted_iota(jnp.int32, (2 * BLOCK, 2 * BLOCK), 1)
    rel = r + BLOCK - c
    return (rel >= 0) & (rel < BLOCK) & ((c >= BLOCK) | (i > 0))


def _swa_softmax(s, valid, sink):
    s = jnp.where(valid, s * ATTN_SCALE, NEG_BIG)
    m = jnp.maximum(jnp.max(s, axis=1, keepdims=True), sink)
    p = jnp.exp(s - m)
    ps = jnp.exp(sink - m)
    denom = jnp.sum(p, axis=1, keepdims=True) + ps
    return p / denom, ps / denom


A_GROUP = 2


Q_WIDTH_A = N_Q_A * HEAD_DIM
N_PAIR_A = Q_WIDTH_A // LANES


def _swa_specs():
    qs = pl.BlockSpec((None, BLOCK, Q_WIDTH_A), lambda b, i: (b, i, 0))

    def kv(col, back):
        return pl.BlockSpec((None, BLOCK, LANES), lambda b, i: (b, jnp.maximum(i - back, 0), col))

    return qs, kv(N_PAIR_A, 1), kv(N_PAIR_A, 0), kv(N_PAIR_A + 1, 1), kv(N_PAIR_A + 1, 0)


def _dup_heads(t):
    lo = lax.broadcasted_iota(jnp.int32, t.shape, 1) < HEAD_DIM
    sw = pltpu.roll(t.astype(F32), HEAD_DIM, 1).astype(BF16)
    return jnp.where(lo, t, sw), jnp.where(lo, sw, t)


def _kv_tiles(kp_ref, kc_ref, vp_ref, vc_ref):
    kd = _dup_heads(jnp.concatenate([kp_ref[...], kc_ref[...]], axis=0))
    vd = _dup_heads(jnp.concatenate([vp_ref[...], vc_ref[...]], axis=0))
    return kd, vd


def _attn_a_fwd(qkn, sinks):
    NB, S, _ = qkn.shape
    qs, kp, kc, vp, vc = _swa_specs()

    def body(q_ref, kp_ref, kc_ref, vp_ref, vc_ref, sink_ref, o_ref):
        i = pl.program_id(1)
        kd, vd = _kv_tiles(kp_ref, kc_ref, vp_ref, vc_ref)
        valid = _band_mask(i)
        lo = lax.broadcasted_iota(jnp.int32, (BLOCK, LANES), 1) < HEAD_DIM
        top = lax.broadcasted_iota(jnp.int32, (2 * BLOCK, 1), 0) < BLOCK
        for first in range(0, N_PAIR_A, A_GROUP):
            pairs = range(first, first + A_GROUP)
            qs_ = [jnp.concatenate(_head_halves(q_ref[:, p * LANES:(p + 1) * LANES], lo), axis=0) for p in pairs]
            ss = [lax.dot_general(q, kd[2 * p // GROUP_A], _NT, preferred_element_type=F32) for q, p in zip(qs_, pairs)]
            pns = [_swa_softmax(s, valid, jnp.where(top, sink_ref[2 * p], sink_ref[2 * p + 1]))[0]
                   for s, p in zip(ss, pairs)]
            pvs = [jnp.dot(pn.astype(BF16), vd[2 * p // GROUP_A], preferred_element_type=F32) for pn, p in zip(pns, pairs)]
            for pv, p in zip(pvs, pairs):
                o_ref[:, p * LANES:(p + 1) * LANES] = jnp.where(lo, pv[:BLOCK], pv[BLOCK:]).astype(BF16)

    return pl.pallas_call(
        body, name="attn_a_fwd", grid=(NB, S // BLOCK),
        in_specs=[qs, kp, kc, vp, vc, pl.BlockSpec(memory_space=pltpu.SMEM)],
        out_specs=qs, out_shape=jax.ShapeDtypeStruct((NB, S, Q_WIDTH_A), BF16),
        compiler_params=_params(("parallel", "arbitrary")),
    )(qkn, qkn, qkn, qkn, qkn, sinks)


def _attn_a_bwd(qkn, do, sinks):
    NB, S, _ = qkn.shape
    qs, kp, kc, vp, vc = _swa_specs()
    full = pl.BlockSpec((None, S, LANES), lambda b, i: (b, 0, 0))
    sink_out = pl.BlockSpec((None, N_Q_A, LANES), lambda b, i: (b, 0, 0))

    def body(q_ref, do_ref, kp_ref, kc_ref, vp_ref, vc_ref, sink_ref, dq_ref, dk_ref, dv_ref, ds_ref, dk_s, dv_s):
        i = pl.program_id(1)

        @pl.when(i == 0)
        def _():
            dk_ref[...] = jnp.zeros_like(dk_ref)
            dv_ref[...] = jnp.zeros_like(dv_ref)
            ds_ref[...] = jnp.zeros_like(ds_ref)

        dk_s[...] = jnp.zeros_like(dk_s)
        dv_s[...] = jnp.zeros_like(dv_s)
        kd, vd = _kv_tiles(kp_ref, kc_ref, vp_ref, vc_ref)
        valid = _band_mask(i)
        lo = lax.broadcasted_iota(jnp.int32, (BLOCK, LANES), 1) < HEAD_DIM
        top = lax.broadcasted_iota(jnp.int32, (2 * BLOCK, 1), 0) < BLOCK
        for first in range(0, N_PAIR_A, A_GROUP):
            pairs = range(first, first + A_GROUP)
            kvs = [2 * p // GROUP_A for p in pairs]
            qs_ = [jnp.concatenate(_head_halves(q_ref[:, p * LANES:(p + 1) * LANES], lo), axis=0) for p in pairs]
            dos = [jnp.concatenate(_head_halves(do_ref[:, p * LANES:(p + 1) * LANES], lo), axis=0) for p in pairs]
            ss = [lax.dot_general(q, kd[kv], _NT, preferred_element_type=F32) for q, kv in zip(qs_, kvs)]
            dps = [lax.dot_general(d, vd[kv], _NT, preferred_element_type=F32) for d, kv in zip(dos, kvs)]
            sm = [_swa_softmax(s, valid, jnp.where(top, sink_ref[2 * p], sink_ref[2 * p + 1])) for s, p in zip(ss, pairs)]
            deltas = [jnp.sum(pn * dp, axis=1, keepdims=True) for (pn, _), dp in zip(sm, dps)]
            dsbs = [(pn * (dp - delta) * ATTN_SCALE).astype(BF16) for (pn, _), dp, delta in zip(sm, dps, deltas)]
            for n, p in enumerate(pairs):
                dq2 = jnp.dot(dsbs[n], kd[kvs[n]], preferred_element_type=F32)
                dq_ref[:, p * LANES:(p + 1) * LANES] = jnp.where(lo, dq2[:BLOCK], dq2[BLOCK:])
                dk_s[kvs[n]] += lax.dot_general(dsbs[n], qs_[n], _TN, preferred_element_type=F32)
                dv_s[kvs[n]] += lax.dot_general(sm[n][0].astype(BF16), dos[n], _TN, preferred_element_type=F32)
                t = sm[n][1] * deltas[n]
                for hh in range(2):
                    dsink = -jnp.sum(t[hh * BLOCK:(hh + 1) * BLOCK], axis=0, keepdims=True)
                    ds_ref[2 * p + hh:2 * p + hh + 1, :] += jnp.broadcast_to(dsink, (1, LANES))

        lo2 = lax.broadcasted_iota(jnp.int32, (2 * BLOCK, LANES), 1) < HEAD_DIM

        def fold(acc):
            halves = [acc[kv] + pltpu.roll(acc[kv], HEAD_DIM, 1) for kv in range(N_KV_A)]
            return jnp.where(lo2, halves[0], halves[1])

        dk2, dv2 = fold(dk_s), fold(dv_s)

        @pl.when(i > 0)
        def _():
            start = pl.multiple_of((i - 1) * BLOCK, BLOCK)
            dk_ref[pl.ds(start, 2 * BLOCK), :] += dk2
            dv_ref[pl.ds(start, 2 * BLOCK), :] += dv2

        @pl.when(i == 0)
        def _():
            dk_ref[0:BLOCK, :] += dk2[BLOCK:, :]
            dv_ref[0:BLOCK, :] += dv2[BLOCK:, :]

    slots = pltpu.VMEM((N_KV_A, 2 * BLOCK, LANES), F32)
    return pl.pallas_call(
        body, name="attn_a_bwd", grid=(NB, S // BLOCK),
        in_specs=[qs, qs, kp, kc, vp, vc, pl.BlockSpec(memory_space=pltpu.SMEM)],
        out_specs=[qs, full, full, sink_out],
        out_shape=[jax.ShapeDtypeStruct((NB, S, Q_WIDTH_A), F32), jax.ShapeDtypeStruct((NB, S, LANES), F32),
                   jax.ShapeDtypeStruct((NB, S, LANES), F32), jax.ShapeDtypeStruct((NB, N_Q_A, LANES), F32)],
        scratch_shapes=[slots, slots],
        compiler_params=_params(("parallel", "arbitrary")),
    )(qkn, do, qkn, qkn, qkn, qkn, sinks)


def _cumsum_mats():
    src = lax.broadcasted_iota(jnp.int32, (2 * BLOCK, 2 * BLOCK), 0) % BLOCK
    dst = lax.broadcasted_iota(jnp.int32, (2 * BLOCK, 2 * BLOCK), 1)
    ones = dst >= BLOCK
    rev = ((src > dst) | ones).astype(BF16)
    fwd = ((src < dst) | ones).astype(BF16)
    return rev, fwd


def _log_sigmoids(z):
    sp = jnp.log(1.0 + jnp.exp(-jnp.abs(z)))
    return jnp.minimum(z, 0.0) - sp, -(jnp.maximum(z, 0.0) + sp)


def _cumsum_mxu_many(vs, mat):
    parts = []
    for v in vs:
        hi = v.astype(BF16)
        parts.append(jnp.concatenate([hi, (v - hi.astype(F32)).astype(BF16)], axis=1))
    r = jnp.dot(jnp.concatenate(parts, axis=0), mat, preferred_element_type=F32)
    return [(r[n * BLOCK:(n + 1) * BLOCK, :BLOCK], r[n * BLOCK:(n + 1) * BLOCK, BLOCK:]) for n in range(len(vs))]


def _strict_mask():
    r = lax.broadcasted_iota(jnp.int32, (BLOCK, BLOCK), 0)
    c = lax.broadcasted_iota(jnp.int32, (BLOCK, BLOCK), 1)
    return c < r


def _tile(ref, j):
    return ref[pl.ds(pl.multiple_of(j * BLOCK, BLOCK), BLOCK), :]


SWEEP_EXIT = -88.0


def _head_halves(t, lo):
    zero = jnp.zeros_like(t)
    return jnp.where(lo, t, zero), jnp.where(lo, zero, t)


def _sb_specs(S, HD, width):
    n = HD // width
    blk = pl.BlockSpec((None, BLOCK, width), lambda b, p, i: (b, i, p))
    k_full = pl.BlockSpec((None, S, width), lambda b, p, i: (b, 0, n + p))
    v_full = pl.BlockSpec((None, S, width), lambda b, p, i: (b, 0, 2 * n + p))
    mat = pl.BlockSpec((2 * BLOCK, 2 * BLOCK), lambda b, p, i: (0, 0))
    return blk, k_full, v_full, mat


SB_FWD_PAIRS = 4
SB_BWD_PAIRS = 2


def _attn_b_fwd(qkv, rev):
    NB, S, W = qkv.shape
    HD = W // 3
    width = SB_FWD_PAIRS * LANES
    n_heads = 2 * SB_FWD_PAIRS
    blk, k_full, v_full, mat = _sb_specs(S, HD, width)

    def body(q_ref, k_ref, v_ref, rev_ref, o_ref):
        i = pl.program_id(2)
        rv = rev_ref[...]
        mask = _strict_mask()
        lo = lax.broadcasted_iota(jnp.int32, (BLOCK, LANES), 1) < HEAD_DIM
        q_all = q_ref[...]
        q_stack = [jnp.concatenate(_head_halves(q_all[:, p * LANES:(p + 1) * LANES] * ATTN_SCALE, lo), axis=0)
                   for p in range(SB_FWD_PAIRS)]

        def pair_tiles(ref, j):
            t = _tile(ref, j)
            return [t[:, p * LANES:(p + 1) * LANES] for p in range(SB_FWD_PAIRS)]

        def tile_pass(j, carries, diagonal):
            ks, vs = pair_tiles(k_ref, j), pair_tiles(v_ref, j)
            zs = []
            for p in range(SB_FWD_PAIRS):
                z2 = lax.dot_general(q_stack[p], ks[p], _NT, preferred_element_type=F32)
                zs += [z2[:BLOCK], z2[BLOCK:]]
            logs = [_log_sigmoids(z) for z in zs]
            cums = _cumsum_mxu_many([jnp.where(mask, lm, 0.0) if diagonal else lm for _, lm in logs], rv)
            probs, new_c = [], []
            for h in range(n_heads):
                after, rs = cums[h]
                if diagonal:
                    a = jnp.where(mask, jnp.exp(logs[h][0] + after), 0.0)
                    new_c.append(rs)
                else:
                    a = jnp.exp(logs[h][0] + after + carries[h])
                    new_c.append(carries[h] + rs)
                probs.append(a.astype(BF16))
            outs = []
            for p in range(SB_FWD_PAIRS):
                pv = jnp.dot(jnp.concatenate(probs[2 * p:2 * p + 2], axis=0), vs[p], preferred_element_type=F32)
                outs.append(jnp.where(lo, pv[:BLOCK], pv[BLOCK:]))
            return new_c, outs

        carries, accs = tile_pass(i, None, True)

        def live(cs):
            top = cs[0]
            for c in cs[1:]:
                top = jnp.maximum(top, c)
            return jnp.max(top) > SWEEP_EXIT

        def cond(st):
            return (st[0] < i) & st[1]

        def step(st):
            jj, _, cs, accs = st
            new_c, outs = tile_pass(i - 1 - jj, cs, False)
            return jj + 1, live(new_c), new_c, [acc + o for acc, o in zip(accs, outs)]

        st = lax.while_loop(cond, step, (jnp.int32(0), live(carries), carries, accs))
        for p in range(SB_FWD_PAIRS):
            o_ref[:, p * LANES:(p + 1) * LANES] = st[3][p].astype(BF16)

    return pl.pallas_call(
        body, name="attn_b_fwd", grid=(NB, HD // width, S // BLOCK),
        in_specs=[blk, k_full, v_full, mat], out_specs=blk,
        out_shape=jax.ShapeDtypeStruct((NB, S, HD), BF16),
        compiler_params=_params(("parallel", "parallel", "arbitrary")),
    )(qkv, qkv, qkv, rev)


def _attn_b_bwd(qkv, do, rev, fwd):
    NB, S, W = qkv.shape
    HD = W // 3
    width = SB_BWD_PAIRS * LANES
    n_heads = 2 * SB_BWD_PAIRS
    nj = S // BLOCK
    blk, k_full, v_full, mat = _sb_specs(S, HD, width)
    acc_full = pl.BlockSpec((None, S, width), lambda b, p, i: (b, 0, p))

    def body(q_ref, do_ref, k_ref, v_ref, rev_ref, fwd_ref, dq_ref, dk_ref, dv_ref, sig_s, a_s, e_s):
        i = pl.program_id(2)

        @pl.when(i == 0)
        def _():
            dk_ref[...] = jnp.zeros_like(dk_ref)
            dv_ref[...] = jnp.zeros_like(dv_ref)

        rv, fw = rev_ref[...], fwd_ref[...]
        mask = _strict_mask()
        lo = lax.broadcasted_iota(jnp.int32, (BLOCK, LANES), 1) < HEAD_DIM
        pairs = range(SB_BWD_PAIRS)

        def cols(p):
            return slice(p * LANES, (p + 1) * LANES)

        q_stack = [jnp.concatenate(_head_halves(q_ref[:, cols(p)], lo), axis=0) for p in pairs]
        qs_stack = [q * ATTN_SCALE for q in q_stack]
        do_stack = [jnp.concatenate(_head_halves(do_ref[:, cols(p)], lo), axis=0) for p in pairs]

        def sweep1_tile(j, carries, diagonal):
            kj, vj = _tile(k_ref, j), _tile(v_ref, j)
            zs, das = [], []
            for p in pairs:
                z2 = lax.dot_general(qs_stack[p], kj[:, cols(p)], _NT, preferred_element_type=F32)
                da2 = lax.dot_general(do_stack[p], vj[:, cols(p)], _NT, preferred_element_type=F32)
                zs += [z2[:BLOCK], z2[BLOCK:]]
                das += [da2[:BLOCK], da2[BLOCK:]]
            logs = [_log_sigmoids(z) for z in zs]
            cums = _cumsum_mxu_many([jnp.where(mask, lm, 0.0) if diagonal else lm for _, lm in logs], rv)
            new_c = []
            for h in range(n_heads):
                lb, (after, rs) = logs[h][0], cums[h]
                if diagonal:
                    a = jnp.where(mask, jnp.exp(lb + after), 0.0)
                    new_c.append(rs)
                else:
                    a = jnp.exp(lb + after + carries[h])
                    new_c.append(carries[h] + rs)
                sig_s[h, j] = jnp.exp(lb)
                a_s[h, j] = a.astype(BF16)
                e_s[h, j] = das[h] * a
            return new_c

        carries = sweep1_tile(i, None, True)

        def live(cs):
            top = cs[0]
            for c in cs[1:]:
                top = jnp.maximum(top, c)
            return jnp.max(top) > SWEEP_EXIT

        def cond(st):
            return (st[0] < i) & st[1]

        def sweep1(st):
            new_c = sweep1_tile(i - 1 - st[0], st[2], False)
            return st[0] + 1, live(new_c), new_c

        visited = lax.while_loop(cond, sweep1, (jnp.int32(0), live(carries), carries))[0]

        def grads(j, st, diagonal):
            prefixes, dqs = st
            kj = _tile(k_ref, j)
            es = [e_s[h, j] for h in range(n_heads)]
            cums = _cumsum_mxu_many(es, fw)
            dzs, new_p = [], []
            for h in range(n_heads):
                sg = sig_s[h, j]
                e_before, rs = cums[h]
                dz = (es[h] * (1.0 - sg) - (e_before + prefixes[h]) * sg) * ATTN_SCALE
                if diagonal:
                    dz = jnp.where(mask, dz, 0.0)
                dzs.append(dz.astype(BF16))
                new_p.append(prefixes[h] + rs)
            rows = pl.ds(pl.multiple_of(j * BLOCK, BLOCK), BLOCK)
            new_dq = []
            for p in pairs:
                dz_stack = jnp.concatenate(dzs[2 * p:2 * p + 2], axis=0)
                a_stack = jnp.concatenate([a_s[2 * p, j], a_s[2 * p + 1, j]], axis=0)
                dq2 = jnp.dot(dz_stack, kj[:, cols(p)], preferred_element_type=F32)
                new_dq.append(dqs[p] + jnp.where(lo, dq2[:BLOCK], dq2[BLOCK:]))
                dk_ref[rows, cols(p)] += lax.dot_general(dz_stack, q_stack[p], _TN, preferred_element_type=F32)
                dv_ref[rows, cols(p)] += lax.dot_general(a_stack, do_stack[p], _TN, preferred_element_type=F32)
            return new_p, new_dq

        zeros = jnp.zeros((BLOCK, BLOCK), F32)
        st = lax.fori_loop(i - visited, i, lambda j, st: grads(j, st, False),
                           ([zeros] * n_heads, [zeros] * SB_BWD_PAIRS))
        dqs = grads(i, st, True)[1]
        for p in pairs:
            dq_ref[:, cols(p)] = dqs[p]

    f32_stash = pltpu.VMEM((n_heads, nj, BLOCK, BLOCK), F32)
    bf16_stash = pltpu.VMEM((n_heads, nj, BLOCK, BLOCK), BF16)
    return pl.pallas_call(
        body, name="attn_b_bwd", grid=(NB, HD // width, nj),
        in_specs=[blk, blk, k_full, v_full, mat, mat], out_specs=[blk, acc_full, acc_full],
        out_shape=[jax.ShapeDtypeStruct((NB, S, HD), F32)] * 3,
        scratch_shapes=[f32_stash, bf16_stash, f32_stash],
        compiler_params=_params(("parallel", "parallel", "arbitrary")),
    )(qkv, do, qkv, qkv, rev, fwd)


def _ada_fwd(c_all, w, b):
    L, D, N = w.shape
    B = c_all.shape[0]

    def body(c_ref, w_ref, b_ref, o_ref):
        cv = c_ref[...]
        cond = (cv * _sigmoid(cv)).astype(BF16)
        o_ref[...] = jnp.dot(cond, w_ref[...].astype(BF16), preferred_element_type=F32) + b_ref[...]

    return pl.pallas_call(
        body, name="ada_fwd", grid=(L,),
        in_specs=[pl.BlockSpec((B, D), lambda l: (0, 0)), pl.BlockSpec((None, D, N), lambda l: (l, 0, 0)),
                  pl.BlockSpec((None, 1, N), lambda l: (l, 0, 0))],
        out_specs=pl.BlockSpec((None, B, N), lambda l: (l, 0, 0)),
        out_shape=jax.ShapeDtypeStruct((L, B, N), F32),
        compiler_params=_params(("parallel",)),
    )(c_all, w, b)


def _ada_bwd(c_all, dmod_all, dmod_shard):
    L, B, N = dmod_shard.shape
    D = c_all.shape[1]
    N_all = dmod_all.shape[2]

    def body(c_ref, da_ref, ds_ref, gw_ref, gb_ref):
        cv = c_ref[...]
        cond = (cv * _sigmoid(cv)).astype(BF16)
        gw_ref[...] = lax.dot_general(cond, ds_ref[...].astype(BF16), _TN, preferred_element_type=F32)
        gb_ref[...] = jnp.sum(da_ref[...], axis=0, keepdims=True)

    return pl.pallas_call(
        body, name="ada_bwd", grid=(L,),
        in_specs=[pl.BlockSpec((B, D), lambda l: (0, 0)), pl.BlockSpec((None, B, N_all), lambda l: (l, 0, 0)),
                  pl.BlockSpec((None, B, N), lambda l: (l, 0, 0))],
        out_specs=[pl.BlockSpec((None, D, N), lambda l: (l, 0, 0)), pl.BlockSpec((None, 1, N_all), lambda l: (l, 0, 0))],
        out_shape=[jax.ShapeDtypeStruct((L, D, N), F32), jax.ShapeDtypeStruct((L, 1, N_all), F32)],
        compiler_params=_params(("parallel",)),
    )(c_all, dmod_all, dmod_shard)


def _adamw(w, g, m, v, name):
    shape = w.shape
    C = shape[-1]
    R = w.size // C
    tr = _pick(R, max(8, (1 << 18) // C), 8)
    c1 = 1.0 - ADAM_B1 ** ADAM_STEP
    c2 = 1.0 - ADAM_B2 ** ADAM_STEP

    def body(w_ref, g_ref, m_ref, v_ref, d_ref, nm_ref, nv_ref):
        gv = g_ref[...]
        nm = ADAM_B1 * m_ref[...] + (1.0 - ADAM_B1) * gv
        nv = ADAM_B2 * v_ref[...] + (1.0 - ADAM_B2) * (gv * gv)
        d_ref[...] = -ADAM_LR * ((nm / c1) / (jnp.sqrt(nv / c2) + ADAM_EPS) + ADAM_WD * w_ref[...])
        nm_ref[...] = nm
        nv_ref[...] = nv

    spec = pl.BlockSpec((tr, C), lambda r: (r, 0))
    out = pl.pallas_call(
        body, name=name, grid=(R // tr,), in_specs=[spec] * 4, out_specs=[spec] * 3,
        out_shape=[jax.ShapeDtypeStruct((R, C), F32)] * 3,
        compiler_params=_params(("parallel",)),
    )(*[t.reshape(R, C) for t in (w, g, m, v)])
    return [t.reshape(shape) for t in out]


_SHARDED = (("wqkv_a", 2), ("wo_a", 1), ("wqkv_b", 2), ("wo_b", 1), ("w_gate", 2), ("w_up", 2), ("w_down", 1))


def _pack_full(layers, axis, gate_up=None):
    L = len(layers)
    R, C = layers[0].shape

    def shards(m):
        if gate_up is not None:
            F = C // 2
            tf, Cs = _ff_tile(F), F // 4
            assert tf % Cs == 0
            starts = [(2 * (s * Cs // tf) + gate_up) * tf + s * Cs % tf for s in range(4)]
            return jnp.stack([m[:, st:st + Cs] for st in starts])
        if axis == 2:
            return m.reshape(R, 4, C // 4).transpose(1, 0, 2)
        return m.reshape(4, R // 4, C)

    halves = [jnp.stack([shards(m) for m in layers[h * (L // 2):(h + 1) * (L // 2)]], axis=1) for h in range(2)]
    return jnp.stack(halves)


def _unpack_full(gathered, axis):
    _, Lh, Rs, Cs = gathered.shape
    t = gathered.reshape(4, 2, Lh, Rs, Cs)
    layers = []
    for h in range(2):
        for l in range(Lh):
            piece = t[:, h, l]
            if axis == 2:
                layers.append(piece.transpose(1, 0, 2).reshape(Rs, 4 * Cs))
            else:
                layers.append(piece.reshape(4 * Rs, Cs))
    return layers


def _sum_slabs(own, recv, name, with_bf16=False):
    C = own.shape[-1]
    out = _sum_leading(recv.reshape(recv.shape[0], -1, C), name, own=own.reshape(-1, C), with_bf16=with_bf16)
    if with_bf16:
        return out[0].reshape(own.shape), out[1].reshape(own.shape)
    return out.reshape(own.shape)


def _gather8(x, name):
    return _all_gather8([x], name)[0]


def _rope_tables(positions):
    half = ROT_DIM // 2
    inv_freq = jnp.power(jnp.float32(ROPE_THETA), -jnp.arange(half, dtype=F32) * 2.0 / ROT_DIM)
    ang = positions.astype(F32).reshape(-1, 1) * inv_freq
    cos, sin = jnp.cos(ang), jnp.sin(ang)
    T = ang.shape[0]
    rest = HEAD_DIM - ROT_DIM
    c64 = jnp.concatenate([cos, cos, jnp.ones((T, rest), F32)], axis=1)
    s64 = jnp.concatenate([-sin, sin, jnp.zeros((T, rest), F32)], axis=1)
    return jnp.tile(c64, (1, 2)), jnp.tile(s64, (1, 2))


def _gain_rows(q_gain, k_gain):
    q2 = jnp.tile(q_gain.reshape(1, HEAD_DIM), (GROUP_A, 2))
    k2 = jnp.tile(k_gain.reshape(1, HEAD_DIM), (1, 2))
    return jnp.concatenate([q2, k2, jnp.ones((1, LANES), F32)], axis=0)


def _local_step(x, positions, mod, norm1_g, norm2_g, q_norm_a, k_norm_a, sinks_a,
                wqkv_a, wo_a, wqkv_b, wo_b, wgu, wd, loss_target):
    NB, S, D = x.shape
    T = NB * S
    QA = N_Q_A * HEAD_DIM
    tab_c, tab_s = _rope_tables(positions)
    rev, fwd = _cumsum_mats()

    saved = []
    xc = x
    for i in range(DEPTH):
        j = i // 2
        sh1, sc1, g1, sh2, sc2, g2 = [mod[i][:, k * D:(k + 1) * D].reshape(NB, 1, D) for k in range(6)]
        st = dict(x=xc, sc1=sc1, g1=g1, sc2=sc2, g2=g2)
        h = _norm_mod_fwd(xc, norm1_g[i:i + 1], sc1, sh1)
        st["h"] = h.reshape(T, D)
        if i % 2 == 0:
            st["qkv"] = _matmul(st["h"], wqkv_a[j], "nn", F32, "qkv_a")
            st["gains"] = _gain_rows(q_norm_a[j], k_norm_a[j])
            st["qkn"] = _qk_prep_fwd(st["qkv"], tab_c, tab_s, st["gains"]).reshape(NB, S, -1)
            st["o"] = _attn_a_fwd(st["qkn"], sinks_a[j]).reshape(T, QA)
            y = _matmul(st["o"], wo_a[j], "nn", F32, "wo_a")
        else:
            st["qkv"] = _matmul(st["h"], wqkv_b[j], "nn", BF16, "qkv_b").reshape(NB, S, -1)
            st["o"] = _attn_b_fwd(st["qkv"], rev).reshape(T, N_H_B * HEAD_DIM)
            y = _matmul(st["o"], wo_b[j], "nn", F32, "wo_b")
        st["y"] = y.reshape(NB, S, D)
        x1 = _gate_res(xc, st["y"], g1)
        st["x1"] = x1
        h2 = _norm_mod_fwd(x1, norm2_g[i:i + 1], sc2, sh2)
        st["h2"] = h2.reshape(T, D)
        st["gu"], st["act"] = _matmul(st["h2"], wgu[i], "nn", F32, "gate_up", swiglu=True)
        st["m"] = _matmul(st["act"], wd[i], "nn", F32, "down").reshape(NB, S, D)
        xc = _gate_res(x1, st["m"], g2)
        saved.append(st)

    loss, dx = _loss_fwd_bwd(xc, loss_target)

    grads = {name: [None] * n for name, n in
             (("wqkv_a", 2), ("wo_a", 2), ("wqkv_b", 2), ("wo_b", 2), ("wgu", DEPTH), ("wd", DEPTH),
              ("norm1_g", DEPTH), ("norm2_g", DEPTH), ("q_norm_a", 2), ("k_norm_a", 2), ("sinks_a", 2))}
    dmod = [None] * DEPTH
    for i in reversed(range(DEPTH)):
        j = i // 2
        st = saved[i]
        dm, dg2 = _gate_res_bwd(dx, st["m"], st["g2"])
        dm = dm.reshape(T, D)
        grads["wd"][i] = _matmul(st["act"], dm, "tn", F32, "d_wd")
        dgu = _swiglu_bwd(dm, wd[i], st["gu"])
        dh2 = _matmul(dgu, wgu[i], "nt", F32, "d_h2")
        grads["wgu"][i] = _matmul(st["h2"], dgu, "tn", F32, "d_wgu")
        dx1, dsh2, dsc2, grads["norm2_g"][i] = _norm_mod_bwd(
            st["x1"], norm2_g[i:i + 1], st["sc2"], dh2.reshape(NB, S, D), dx)
        dy, dg1 = _gate_res_bwd(dx1, st["y"], st["g1"])
        dy = dy.reshape(T, D)
        if i % 2 == 0:
            do = _matmul(dy, wo_a[j], "nt", BF16, "d_o_a").reshape(NB, S, QA)
            grads["wo_a"][j] = _matmul(st["o"], dy, "tn", F32, "d_wo_a")
            dq, dk, dv, dsink = _attn_a_bwd(st["qkn"], do, sinks_a[j])
            dqkv, dgain = _qk_prep_bwd(st["qkv"], dq.reshape(T, QA), dk.reshape(T, LANES), dv.reshape(T, LANES),
                                       tab_c, tab_s, st["gains"])
            dh = _matmul(dqkv, wqkv_a[j], "nt", F32, "d_h_a")
            grads["wqkv_a"][j] = _matmul(st["h"], dqkv, "tn", F32, "d_wqkv_a")
            grads["q_norm_a"][j] = jnp.sum(dgain[:GROUP_A].reshape(2 * GROUP_A, HEAD_DIM), axis=0)
            grads["k_norm_a"][j] = jnp.sum(dgain[GROUP_A].reshape(2, HEAD_DIM), axis=0)
            grads["sinks_a"][j] = jnp.sum(dsink[..., 0], axis=0)
        else:
            do = _matmul(dy, wo_b[j], "nt", BF16, "d_o_b").reshape(NB, S, -1)
            grads["wo_b"][j] = _matmul(st["o"], dy, "tn", F32, "d_wo_b")
            dq, dk, dv = _attn_b_bwd(st["qkv"], do, rev, fwd)
            dqkv = jnp.concatenate([dq, dk, dv], axis=-1).reshape(T, -1).astype(BF16)
            dh = _matmul(dqkv, wqkv_b[j], "nt", F32, "d_h_b")
            grads["wqkv_b"][j] = _matmul(st["h"], dqkv, "tn", F32, "d_wqkv_b")
        dx, dsh1, dsc1, grads["norm1_g"][i] = _norm_mod_bwd(
            st["x"], norm1_g[i:i + 1], st["sc1"], dh.reshape(NB, S, D), dx1)
        dmod[i] = jnp.concatenate([dsh1, dsc1, dg1, dsh2, dsc2, dg2], axis=-1).reshape(NB, 6 * D)

    matrices = ("wqkv_a", "wo_a", "wqkv_b", "wo_b", "wgu", "wd")
    grads = {name: parts if name in matrices else jnp.stack(parts) for name, parts in grads.items()}
    return loss, dx, grads, jnp.stack(dmod)


def _rows_of(flat, cols=PACK_COLS):
    n = flat.shape[0]
    pad = (-n) % (8 * cols)
    if pad:
        flat = jnp.concatenate([flat, jnp.zeros((pad,), flat.dtype)])
    return flat.reshape(-1, cols)


def kernel(x, c, positions, ada_w, ada_b, norm1_g, norm2_g, wqkv_a, q_norm_a, k_norm_a, sinks_a, wo_a, wqkv_b, wo_b, w_gate, w_up, w_down, loss_target, m_ada_w, m_ada_b, m_norm1_g, m_norm2_g, m_wqkv_a, m_q_norm_a, m_k_norm_a, m_sinks_a, m_wo_a, m_wqkv_b, m_wo_b, m_w_gate, m_w_up, m_w_down, v_ada_w, v_ada_b, v_norm1_g, v_norm2_g, v_wqkv_a, v_q_norm_a, v_k_norm_a, v_sinks_a, v_wo_a, v_wqkv_b, v_wo_b, v_w_gate, v_w_up, v_w_down):
    xi, yi, ci = lax.axis_index("x"), lax.axis_index("y"), lax.axis_index("c")
    dev = 4 * xi + 2 * yi + ci
    chip = 2 * xi + yi
    NB, S, D = x.shape
    B_all = N_DEV * NB
    L = ada_w.shape[0]
    n_mod = ada_w.shape[2] // 2

    c_all = _gather8(_rows_of(c.reshape(-1), LANES), "gather_c").reshape(N_DEV, -1)[:, :NB * D].reshape(B_all, D)
    ada_w_half = lax.dynamic_slice_in_dim(ada_w, ci * n_mod, n_mod, axis=2)
    ada_b_half = lax.dynamic_slice_in_dim(ada_b, dev * n_mod, n_mod, axis=1).reshape(L, 1, n_mod)
    mod_part = _ada_fwd(c_all, ada_w_half, ada_b_half)
    n_part = L * B_all * n_mod
    mod_all = _gather8(_rows_of(mod_part.reshape(-1)), "gather_mod").reshape(N_DEV, -1)[:, :n_part]
    mod_all = mod_all.reshape(N_DEV, L, B_all, n_mod).transpose(1, 2, 0, 3).reshape(L, B_all, N_DEV * n_mod)
    mod = lax.dynamic_slice_in_dim(mod_all, dev * NB, NB, axis=1)

    shards = dict(wqkv_a=wqkv_a, wo_a=wo_a, wqkv_b=wqkv_b, wo_b=wo_b, w_gate=w_gate, w_up=w_up, w_down=w_down)
    halves = []
    for name, _ in _SHARDED:
        w = shards[name]
        half = lax.dynamic_index_in_dim(w.reshape((2, w.shape[0] // 2) + w.shape[1:]), ci, 0, keepdims=False)
        halves.append(half.astype(BF16))
    gathered = _all_gather8(halves, "gather_weights", local_axis=1, local_chunks=8)
    full = {name: _unpack_full(t, axis) for (name, axis), t in zip(_SHARDED, gathered)}
    wgu = [_interleave(gate, up) for gate, up in zip(full["w_gate"], full["w_up"])]

    loss, grad_x, g, dmod = _local_step(
        x, positions, mod, norm1_g, norm2_g, q_norm_a, k_norm_a, sinks_a,
        full["wqkv_a"], full["wo_a"], full["wqkv_b"], full["wo_b"], wgu, full["w_down"], loss_target)

    g_full = dict(wqkv_a=g["wqkv_a"], wo_a=g["wo_a"], wqkv_b=g["wqkv_b"], wo_b=g["wo_b"],
                  w_gate=g["wgu"], w_up=g["wgu"], w_down=g["wd"])
    which = dict(w_gate=0, w_up=1)
    packed = [_pack_full(g_full[name], axis, which.get(name)) for name, axis in _SHARDED]
    def own(t, index):
        return lax.dynamic_index_in_dim(t, index, 0, keepdims=False)

    from_cores = _exchange(packed, "c", "rs_cores", chunk_axis=0, chunks=4)
    chip_part = [_sum_slabs(own(p, ci), r, "rs_add_cores", with_bf16=True) for p, r in zip(packed, from_cores)]
    from_chips = _exchange([b for _, b in chip_part], "xy", "rs_chips")
    mine = [_sum_slabs(own(p, chip), r, "rs_add_chips") for (p, _), r in zip(chip_part, from_chips)]
    theirs = _sibling_send(mine, "rs_halves")
    grad = {}
    for (name, _), m, t in zip(_SHARDED, mine, theirs):
        first, second = jnp.where(ci == 0, m, t), jnp.where(ci == 0, t, m)
        grad[name] = jnp.stack([first, second]).reshape(shards[name].shape)

    small_names = ("norm1_g", "norm2_g", "q_norm_a", "k_norm_a", "sinks_a")
    small = [dmod.reshape(-1)] + [g[name].reshape(-1) for name in small_names] + [loss.reshape(-1)]
    small_sizes = [t.shape[0] for t in small]
    small_rows = _rows_of(jnp.concatenate(small))
    small_all = _gather8(small_rows, "gather_small")
    small_sum = _sum_leading(small_all, "sum_small").reshape(-1)
    n_dmod = small_sizes[0]
    dmod_all = small_all.reshape(N_DEV, -1)[:, :n_dmod].reshape(N_DEV, L, NB, 6 * D)
    dmod_all = dmod_all.transpose(1, 0, 2, 3).reshape(L, B_all, 6 * D)
    off = n_dmod
    for name, sz in zip(small_names + ("loss",), small_sizes[1:]):
        grad[name] = small_sum[off:off + sz]
        off += sz
    loss_total = grad.pop("loss").reshape(())
    for name, ref in (("norm1_g", norm1_g), ("norm2_g", norm2_g), ("q_norm_a", q_norm_a),
                      ("k_norm_a", k_norm_a), ("sinks_a", sinks_a)):
        grad[name] = grad[name].reshape(ref.shape)

    n_shard = ada_w.shape[2]
    dmod_shard = lax.dynamic_slice_in_dim(dmod_all, chip * n_shard, n_shard, axis=2)
    grad["ada_w"], gb = _ada_bwd(c_all, dmod_all, dmod_shard)
    grad["ada_b"] = gb.reshape(ada_b.shape)

    weights = dict(ada_w=ada_w, ada_b=ada_b, norm1_g=norm1_g, norm2_g=norm2_g, wqkv_a=wqkv_a, q_norm_a=q_norm_a,
                   k_norm_a=k_norm_a, sinks_a=sinks_a, wo_a=wo_a, wqkv_b=wqkv_b, wo_b=wo_b, w_gate=w_gate,
                   w_up=w_up, w_down=w_down)
    m_in = dict(ada_w=m_ada_w, ada_b=m_ada_b, norm1_g=m_norm1_g, norm2_g=m_norm2_g, wqkv_a=m_wqkv_a,
                q_norm_a=m_q_norm_a, k_norm_a=m_k_norm_a, sinks_a=m_sinks_a, wo_a=m_wo_a, wqkv_b=m_wqkv_b,
                wo_b=m_wo_b, w_gate=m_w_gate, w_up=m_w_up, w_down=m_w_down)
    v_in = dict(ada_w=v_ada_w, ada_b=v_ada_b, norm1_g=v_norm1_g, norm2_g=v_norm2_g, wqkv_a=v_wqkv_a,
                q_norm_a=v_q_norm_a, k_norm_a=v_k_norm_a, sinks_a=v_sinks_a, wo_a=v_wo_a, wqkv_b=v_wqkv_b,
                wo_b=v_wo_b, w_gate=v_w_gate, w_up=v_w_up, w_down=v_w_down)
    names = list(weights)
    delta, new_m, new_v = {}, {}, {}
    for name in names:
        delta[name], new_m[name], new_v[name] = _adamw(weights[name], grad[name], m_in[name], v_in[name],
                                                       "adamw_" + name)
    return (loss_total, grad_x, *[grad[k] for k in names], *[delta[k] for k in names],
            *[new_m[k] for k in names], *[new_v[k] for k in names])
```

```python
import jax
import jax.numpy as jnp
from jax import lax
from jax.experimental import pallas as pl
from jax.experimental.pallas import tpu as pltpu

F32 = jnp.float32
BF16 = jnp.bfloat16

DEPTH = 4
HEAD_DIM = 64
N_Q_A = 16
N_KV_A = 2
GROUP_A = N_Q_A // N_KV_A
N_H_B = 16
BLOCK = 128
ROT_DIM = HEAD_DIM // 4
ROPE_THETA = 500000.0
EPS = 1e-6
ATTN_SCALE = HEAD_DIM ** -0.5
NEG_BIG = -1e30

ADAM_LR = 0.001
ADAM_B1 = 0.9
ADAM_B2 = 0.999
ADAM_EPS = 1e-08
ADAM_WD = 0.01
ADAM_STEP = 10

N_DEV = 8
LANES = 128
PACK_COLS = 1024
VMEM_LIMIT_BYTES = 48 * 1024 * 1024
MESH = pl.DeviceIdType.MESH

_NT = (((1,), (1,)), ((), ()))
_TN = (((0,), (0,)), ((), ()))
_NN = (((1,), (0,)), ((), ()))


def _params(sem=None, vmem_limit_bytes=VMEM_LIMIT_BYTES):
    return pltpu.CompilerParams(vmem_limit_bytes=vmem_limit_bytes, dimension_semantics=sem)


def _pick(n, cap, mult):
    best = None
    for t in range(mult, min(n, cap) + 1, mult):
        if n % t == 0:
            best = t
    return n if best is None else best


_ANY = pl.BlockSpec(memory_space=pl.ANY)


def _window(index, axis, q, n, shape):
    rest = [slice(None)] * len(shape)
    size = shape[axis] // n
    rest[axis] = pl.ds(q * size, size)
    return tuple(index) + tuple(rest)


def _all_gather8(xs, name, local_axis=0, local_chunks=1):
    n = len(xs)

    def body(*refs):
        x_refs, out_refs = refs[:n], refs[n:2 * n]
        send_sems, recv_sems, local_sems = refs[2 * n:]
        xi, yi, ci = lax.axis_index("x"), lax.axis_index("y"), lax.axis_index("c")
        me, sibling = (xi, yi, ci), (xi, yi, 1 - ci)
        chips = [(1 - xi, yi), (xi, 1 - yi), (1 - xi, 1 - yi)]

        def slab(w, px, py, pc):
            return out_refs[w].at[4 * px + 2 * py + pc]

        def copy(w, k, block, to, src=None):
            return pltpu.make_async_remote_copy(
                src_ref=slab(w, *block) if src is None else src, dst_ref=slab(w, *block),
                send_sem=send_sems.at[k, w], recv_sem=recv_sems.at[k, w], device_id=to, device_id_type=MESH)

        mine = []
        for w in range(n):
            for q in range(local_chunks):
                part = _window((), local_axis, q, local_chunks, xs[w].shape)
                mine.append(pltpu.make_async_copy(x_refs[w].at[part], slab(w, *me).at[part], local_sems.at[w, q]))
                mine[-1].start()
        first = [copy(w, 0, me, sibling, src=x_refs[w]) for w in range(n)]
        first += [copy(w, 1 + j, me, (*chip, ci), src=x_refs[w]) for j, chip in enumerate(chips) for w in range(n)]
        for cp in first:
            cp.start()
        passed = []
        for j, chip in enumerate(chips):
            for w in range(n):
                copy(w, 1 + j, (*chip, ci), me).wait_recv()
                passed.append(copy(w, 4 + j, (*chip, ci), sibling))
                passed[-1].start()
        for w in range(n):
            copy(w, 0, sibling, me).wait_recv()
        for j, chip in enumerate(chips):
            for w in range(n):
                copy(w, 4 + j, (*chip, 1 - ci), me).wait_recv()
        for cp in first + passed:
            cp.wait_send()
        for cp in mine:
            cp.wait()

    return pl.pallas_call(
        body, name=name,
        out_shape=[jax.ShapeDtypeStruct((N_DEV,) + x.shape, x.dtype) for x in xs],
        in_specs=[_ANY] * n, out_specs=[_ANY] * n,
        scratch_shapes=[pltpu.SemaphoreType.DMA((7, n)), pltpu.SemaphoreType.DMA((7, n)),
                        pltpu.SemaphoreType.DMA((n, local_chunks))],
    )(*xs)


def _exchange(xs, group, name, chunk_axis=0, chunks=1):
    n = len(xs)
    n_peers = 1 if group == "c" else 3

    def body(*refs):
        x_refs, out_refs = refs[:n], refs[n:2 * n]
        send_sems, recv_sems = refs[2 * n:]
        xi, yi, ci = lax.axis_index("x"), lax.axis_index("y"), lax.axis_index("c")
        if group == "c":
            peers = [(1 - ci, (xi, yi, 1 - ci))]
        else:
            peers = [(2 * (1 - xi) + yi, (1 - xi, yi, ci)),
                     (2 * xi + (1 - yi), (xi, 1 - yi, ci)),
                     (2 * (1 - xi) + (1 - yi), (1 - xi, 1 - yi, ci))]
        copies = []
        for k, (p, dev) in enumerate(peers):
            for w in range(n):
                slab_shape = xs[w].shape[1:]
                for q in range(chunks):
                    copies.append(pltpu.make_async_remote_copy(
                        src_ref=x_refs[w].at[_window((p,), chunk_axis, q, chunks, slab_shape)],
                        dst_ref=out_refs[w].at[_window((k,), chunk_axis, q, chunks, slab_shape)],
                        send_sem=send_sems.at[k, w, q], recv_sem=recv_sems.at[k, w, q],
                        device_id=dev, device_id_type=MESH))
                    copies[-1].start()
        for cp in copies:
            cp.wait()

    return pl.pallas_call(
        body, name=name,
        out_shape=[jax.ShapeDtypeStruct((n_peers,) + x.shape[1:], x.dtype) for x in xs],
        in_specs=[_ANY] * n, out_specs=[_ANY] * n,
        scratch_shapes=[pltpu.SemaphoreType.DMA((n_peers, n, chunks)), pltpu.SemaphoreType.DMA((n_peers, n, chunks))],
    )(*xs)


def _sibling_send(xs, name, chunk_axis=1, chunks=4):
    n = len(xs)

    def body(*refs):
        x_refs, out_refs = refs[:n], refs[n:2 * n]
        send_sems, recv_sems = refs[2 * n:]
        xi, yi, ci = lax.axis_index("x"), lax.axis_index("y"), lax.axis_index("c")
        copies = []
        for w in range(n):
            for q in range(chunks):
                part = _window((), chunk_axis, q, chunks, xs[w].shape)
                copies.append(pltpu.make_async_remote_copy(
                    src_ref=x_refs[w].at[part], dst_ref=out_refs[w].at[part],
                    send_sem=send_sems.at[w, q], recv_sem=recv_sems.at[w, q],
                    device_id=(xi, yi, 1 - ci), device_id_type=MESH))
                copies[-1].start()
        for cp in copies:
            cp.wait()

    return pl.pallas_call(
        body, name=name,
        out_shape=[jax.ShapeDtypeStruct(x.shape, x.dtype) for x in xs],
        in_specs=[_ANY] * n, out_specs=[_ANY] * n,
        scratch_shapes=[pltpu.SemaphoreType.DMA((n, chunks)), pltpu.SemaphoreType.DMA((n, chunks))],
    )(*xs)


def _sum_leading(x, name, own=None, with_bf16=False):
    P, R, C = x.shape
    tr = _pick(R, max(16, (1 << 19) // (C * (P + 1))), 16)

    def body(*refs):
        n_in = 1 if own is None else 2
        x_ref = refs[n_in - 1]
        acc = x_ref[0].astype(F32) if own is None else refs[0][...] + x_ref[0].astype(F32)
        for p in range(1, P):
            acc = acc + x_ref[p].astype(F32)
        refs[n_in][...] = acc
        if with_bf16:
            refs[n_in + 1][...] = acc.astype(BF16)

    flat = pl.BlockSpec((tr, C), lambda r: (r, 0))
    slabs = pl.BlockSpec((P, tr, C), lambda r: (0, r, 0))
    out = pl.pallas_call(
        body, name=name, grid=(R // tr,),
        in_specs=[slabs] if own is None else [flat, slabs],
        out_specs=[flat, flat] if with_bf16 else [flat],
        out_shape=[jax.ShapeDtypeStruct((R, C), F32)] + ([jax.ShapeDtypeStruct((R, C), BF16)] if with_bf16 else []),
        compiler_params=_params(("arbitrary",)),
    )(*([x] if own is None else [own, x]))
    return out if with_bf16 else out[0]


MATMUL_SINGLE_K = 1280
MATMUL_VMEM_BUDGET = 36 * 1024 * 1024


def _matmul(a, b, mode, out_dtype, name, swiglu=False):
    if mode == "nn":
        (M, K), N = a.shape, b.shape[1]
    elif mode == "nt":
        (M, K), N = a.shape, b.shape[0]
    else:
        (K, M), N = a.shape, b.shape[1]
    tm = _pick(M, 1024 if mode != "tn" else 1536, 128)
    tn = _pick(N, 1536, 128)
    if swiglu:
        tm, tn = _pick(M, 512, 128), 2 * _ff_tile(N // 2)
    out_bytes = jnp.dtype(out_dtype).itemsize
    tk = K
    if K > MATMUL_SINGLE_K:
        for cap in (2048, 1024, 512):
            tk = _pick(K, cap, 128)
            blocks = 2 * 2 * tk * (tm + tn) + tm * tn * (2 * out_bytes + (4 if out_dtype != F32 else 0))
            if blocks <= MATMUL_VMEM_BUDGET:
                break
    nk = K // tk
    dims = {"nn": _NN, "nt": _NT, "tn": _TN}[mode]
    use_scratch = nk > 1 and out_dtype != F32

    def body(a_ref, b_ref, *refs):
        o_ref = refs[0]

        def product():
            return lax.dot_general(a_ref[...].astype(BF16), b_ref[...].astype(BF16), dims,
                                   preferred_element_type=F32)

        if nk == 1:
            part = product()
            o_ref[...] = part.astype(o_ref.dtype)
            if swiglu:
                g = part[:, :tn // 2]
                refs[1][...] = (g * _sigmoid(g) * part[:, tn // 2:]).astype(BF16)
            return
        k = pl.program_id(2)
        acc_ref = refs[-1] if use_scratch else o_ref

        @pl.when(k == 0)
        def _():
            acc_ref[...] = jnp.zeros_like(acc_ref)

        acc_ref[...] += product()

        if use_scratch:
            @pl.when(k == nk - 1)
            def _():
                o_ref[...] = acc_ref[...].astype(o_ref.dtype)

    if mode == "tn":
        a_spec = pl.BlockSpec((tk, tm), lambda i, j, k: (k, i))
    else:
        a_spec = pl.BlockSpec((tm, tk), lambda i, j, k: (i, k))
    if mode == "nt":
        b_spec = pl.BlockSpec((tn, tk), lambda i, j, k: (j, k))
    else:
        b_spec = pl.BlockSpec((tk, tn), lambda i, j, k: (k, j))
    out_specs = [pl.BlockSpec((tm, tn), lambda i, j, k: (i, j))]
    out_shape = [jax.ShapeDtypeStruct((M, N), out_dtype)]
    if swiglu:
        assert nk == 1 and mode == "nn"
        out_specs.append(pl.BlockSpec((tm, tn // 2), lambda i, j, k: (i, j)))
        out_shape.append(jax.ShapeDtypeStruct((M, N // 2), BF16))
    out = pl.pallas_call(
        body, name=name, grid=(M // tm, N // tn, nk),
        in_specs=[a_spec, b_spec], out_specs=out_specs, out_shape=out_shape,
        scratch_shapes=[pltpu.VMEM((tm, tn), F32)] if use_scratch else [],
        compiler_params=_params(("parallel", "parallel", "arbitrary")),
    )(a, b)
    return out if swiglu else out[0]


def _row_tile(S):
    return _pick(S, 512, 8)


def _norm_mod_fwd(x, gain, sc, sh):
    NB, S, D = x.shape
    tr = _row_tile(S)

    def body(x_ref, g_ref, sc_ref, sh_ref, h_ref):
        xv = x_ref[...]
        ms = jnp.mean(xv * xv, axis=-1, keepdims=True)
        n = xv * lax.rsqrt(ms + EPS) * g_ref[...]
        h_ref[...] = (n * (1.0 + sc_ref[...]) + sh_ref[...]).astype(BF16)

    tok = pl.BlockSpec((None, tr, D), lambda b, r: (b, r, 0))
    per_ex = pl.BlockSpec((None, 1, D), lambda b, r: (b, 0, 0))
    return pl.pallas_call(
        body, name="norm_mod_fwd", grid=(NB, S // tr),
        in_specs=[tok, pl.BlockSpec((1, D), lambda b, r: (0, 0)), per_ex, per_ex],
        out_specs=tok, out_shape=jax.ShapeDtypeStruct((NB, S, D), BF16),
        compiler_params=_params(("parallel", "parallel")),
    )(x, gain, sc, sh)


def _norm_mod_bwd(x, gain, sc, dh, dres):
    NB, S, D = x.shape
    tr = _row_tile(S)

    def body(x_ref, g_ref, sc_ref, dh_ref, dres_ref, dx_ref, dsh_ref, dsc_ref, dg_ref):
        b, r = pl.program_id(0), pl.program_id(1)

        @pl.when(r == 0)
        def _():
            dsh_ref[...] = jnp.zeros_like(dsh_ref)
            dsc_ref[...] = jnp.zeros_like(dsc_ref)

        @pl.when((r == 0) & (b == 0))
        def _():
            dg_ref[...] = jnp.zeros_like(dg_ref)

        xv = x_ref[...]
        rstd = lax.rsqrt(jnp.mean(xv * xv, axis=-1, keepdims=True) + EPS)
        xh = xv * rstd
        g = g_ref[...]
        dh = dh_ref[...]
        dsh_ref[...] += jnp.sum(dh, axis=0, keepdims=True)
        dsc_ref[...] += jnp.sum(dh * (xh * g), axis=0, keepdims=True)
        dn = dh * (1.0 + sc_ref[...])
        dg_ref[...] += jnp.sum(dn * xh, axis=0, keepdims=True)
        dxh = dn * g
        proj = jnp.mean(dxh * xh, axis=-1, keepdims=True)
        dx_ref[...] = rstd * (dxh - xh * proj) + dres_ref[...]

    tok = pl.BlockSpec((None, tr, D), lambda b, r: (b, r, 0))
    per_ex = pl.BlockSpec((None, 1, D), lambda b, r: (b, 0, 0))
    row = pl.BlockSpec((1, D), lambda b, r: (0, 0))
    return pl.pallas_call(
        body, name="norm_mod_bwd", grid=(NB, S // tr),
        in_specs=[tok, row, per_ex, tok, tok],
        out_specs=[tok, per_ex, per_ex, row],
        out_shape=[jax.ShapeDtypeStruct((NB, S, D), F32), jax.ShapeDtypeStruct((NB, 1, D), F32),
                   jax.ShapeDtypeStruct((NB, 1, D), F32), jax.ShapeDtypeStruct((1, D), F32)],
        compiler_params=_params(("arbitrary", "arbitrary")),
    )(x, gain, sc, dh, dres)


def _gate_res(x, y, g):
    NB, S, D = x.shape
    tr = _row_tile(S)

    def body(x_ref, y_ref, g_ref, o_ref):
        o_ref[...] = x_ref[...] + g_ref[...] * y_ref[...]

    tok = pl.BlockSpec((None, tr, D), lambda b, r: (b, r, 0))
    per_ex = pl.BlockSpec((None, 1, D), lambda b, r: (b, 0, 0))
    return pl.pallas_call(
        body, name="gate_res", grid=(NB, S // tr), in_specs=[tok, tok, per_ex], out_specs=tok,
        out_shape=jax.ShapeDtypeStruct((NB, S, D), F32),
        compiler_params=_params(("parallel", "parallel")),
    )(x, y, g)


def _gate_res_bwd(dxo, y, g):
    NB, S, D = dxo.shape
    tr = _row_tile(S)

    def body(d_ref, y_ref, g_ref, dy_ref, dg_ref):
        @pl.when(pl.program_id(1) == 0)
        def _():
            dg_ref[...] = jnp.zeros_like(dg_ref)

        d = d_ref[...]
        dy_ref[...] = (d * g_ref[...]).astype(BF16)
        dg_ref[...] += jnp.sum(d * y_ref[...], axis=0, keepdims=True)

    tok = pl.BlockSpec((None, tr, D), lambda b, r: (b, r, 0))
    per_ex = pl.BlockSpec((None, 1, D), lambda b, r: (b, 0, 0))
    return pl.pallas_call(
        body, name="gate_res_bwd", grid=(NB, S // tr), in_specs=[tok, tok, per_ex], out_specs=[tok, per_ex],
        out_shape=[jax.ShapeDtypeStruct((NB, S, D), BF16), jax.ShapeDtypeStruct((NB, 1, D), F32)],
        compiler_params=_params(("arbitrary", "arbitrary")),
    )(dxo, y, g)


def _sigmoid(v):
    return 1.0 / (1.0 + jnp.exp(-v))


def _ff_tile(F):
    return _pick(F, 1536, 128)


def _interleave(gate, up):
    F = gate.shape[-1]
    tf = _ff_tile(F)
    parts = []
    for j in range(F // tf):
        parts += [gate[..., j * tf:(j + 1) * tf], up[..., j * tf:(j + 1) * tf]]
    return jnp.concatenate(parts, axis=-1)


def _deinterleave(gu):
    F = gu.shape[-1] // 2
    tf = _ff_tile(F)
    gate = [gu[..., 2 * j * tf:(2 * j + 1) * tf] for j in range(F // tf)]
    up = [gu[..., (2 * j + 1) * tf:(2 * j + 2) * tf] for j in range(F // tf)]
    return jnp.concatenate(gate, axis=-1), jnp.concatenate(up, axis=-1)


def _swiglu_bwd(dm, wd, gu):
    T, D = dm.shape
    F = wd.shape[0]
    tf = _ff_tile(F)
    tm = _pick(T, 512, 128)
    assert D <= MATMUL_SINGLE_K

    def body(a_ref, b_ref, gu_ref, o_ref):
        d = lax.dot_general(a_ref[...], b_ref[...], _NT, preferred_element_type=F32)
        g, u = gu_ref[:, :tf], gu_ref[:, tf:]
        s = _sigmoid(g)
        o_ref[:, :tf] = (d * u * (s * (1.0 + g * (1.0 - s)))).astype(BF16)
        o_ref[:, tf:] = (d * (g * s)).astype(BF16)

    return pl.pallas_call(
        body, name="swiglu_bwd", grid=(T // tm, F // tf),
        in_specs=[pl.BlockSpec((tm, D), lambda i, j: (i, 0)), pl.BlockSpec((tf, D), lambda i, j: (j, 0)),
                  pl.BlockSpec((tm, 2 * tf), lambda i, j: (i, j))],
        out_specs=pl.BlockSpec((tm, 2 * tf), lambda i, j: (i, j)),
        out_shape=jax.ShapeDtypeStruct((T, 2 * F), BF16),
        compiler_params=_params(("parallel", "parallel")),
    )(dm, wd, gu)


def _loss_fwd_bwd(y, target):
    NB, S, D = y.shape
    tr = _row_tile(S)

    def body(y_ref, t_ref, l_ref, d_ref):
        @pl.when((pl.program_id(0) == 0) & (pl.program_id(1) == 0))
        def _():
            l_ref[...] = jnp.zeros_like(l_ref)

        e = y_ref[...] - t_ref[...]
        d_ref[...] = e / D
        l_ref[...] += 0.5 * jnp.sum(jnp.mean(e * e, axis=-1, keepdims=True), axis=0, keepdims=True)

    tok = pl.BlockSpec((None, tr, D), lambda b, r: (b, r, 0))
    return pl.pallas_call(
        body, name="loss", grid=(NB, S // tr), in_specs=[tok, tok],
        out_specs=[pl.BlockSpec((1, 1), lambda b, r: (0, 0)), tok],
        out_shape=[jax.ShapeDtypeStruct((1, 1), F32), jax.ShapeDtypeStruct((NB, S, D), F32)],
        compiler_params=_params(("arbitrary", "arbitrary")),
    )(y, target)


def _half_sums(v, lo):
    sa = jnp.sum(jnp.where(lo, v, 0.0), axis=-1, keepdims=True)
    sb = jnp.sum(jnp.where(lo, 0.0, v), axis=-1, keepdims=True)
    return jnp.where(lo, sa, sb)


def _rope_swap(v, lane64):
    up = pltpu.roll(v, LANES - ROT_DIM // 2, 1)
    down = pltpu.roll(v, ROT_DIM // 2, 1)
    return jnp.where(lane64 < ROT_DIM // 2, up, jnp.where(lane64 < ROT_DIM, down, 0.0))


def _qk_prep_fwd(qkv, tab_c, tab_s, gains):
    T, W = qkv.shape
    R = W // LANES
    tt = _pick(T, 256, 8)

    def body(x_ref, c_ref, s_ref, g_ref, o_ref):
        lane = lax.broadcasted_iota(jnp.int32, (tt, LANES), 1)
        lo = lane < HEAD_DIM
        lane64 = lane & (HEAD_DIM - 1)
        c, s = c_ref[...], s_ref[...]
        for j in range(R - 1):
            cols = slice(j * LANES, (j + 1) * LANES)
            xv = x_ref[:, cols]
            rstd = lax.rsqrt(_half_sums(xv * xv, lo) / HEAD_DIM + EPS)
            yn = xv * rstd * g_ref[j:j + 1, :]
            o_ref[:, cols] = (yn * c + _rope_swap(yn, lane64) * s).astype(BF16)
        o_ref[:, (R - 1) * LANES:] = x_ref[:, (R - 1) * LANES:].astype(BF16)

    tok = pl.BlockSpec((tt, W), lambda t: (t, 0))
    tab = pl.BlockSpec((tt, LANES), lambda t: (t, 0))
    return pl.pallas_call(
        body, name="qk_prep_fwd", grid=(T // tt,),
        in_specs=[tok, tab, tab, pl.BlockSpec((R, LANES), lambda t: (0, 0))],
        out_specs=tok, out_shape=jax.ShapeDtypeStruct((T, W), BF16),
        compiler_params=_params(("parallel",)),
    )(qkv, tab_c, tab_s, gains)


def _qk_prep_bwd(qkv, dq, dk, dv, tab_c, tab_s, gains):
    T, W = qkv.shape
    R = W // LANES
    QW = dq.shape[1]
    tt = _pick(T, 256, 8)

    def body(x_ref, dq_ref, dk_ref, dv_ref, c_ref, s_ref, g_ref, o_ref, dg_ref):
        @pl.when(pl.program_id(0) == 0)
        def _():
            dg_ref[...] = jnp.zeros_like(dg_ref)

        lane = lax.broadcasted_iota(jnp.int32, (tt, LANES), 1)
        lo = lane < HEAD_DIM
        lane64 = lane & (HEAD_DIM - 1)
        c, s = c_ref[...], s_ref[...]
        for j in range(R - 1):
            cols = slice(j * LANES, (j + 1) * LANES)
            xv = x_ref[:, cols]
            d = dq_ref[:, cols] if j < R - 2 else dk_ref[...]
            rstd = lax.rsqrt(_half_sums(xv * xv, lo) / HEAD_DIM + EPS)
            xh = xv * rstd
            dyn = d * c + _rope_swap(d * s, lane64)
            dg_ref[j:j + 1, :] += jnp.sum(dyn * xh, axis=0, keepdims=True)
            dxh = dyn * g_ref[j:j + 1, :]
            proj = _half_sums(dxh * xh, lo) / HEAD_DIM
            o_ref[:, cols] = (rstd * (dxh - xh * proj)).astype(BF16)
        o_ref[:, (R - 1) * LANES:] = dv_ref[...].astype(BF16)

    tok = pl.BlockSpec((tt, W), lambda t: (t, 0))
    tab = pl.BlockSpec((tt, LANES), lambda t: (t, 0))
    gsp = pl.BlockSpec((R, LANES), lambda t: (0, 0))
    return pl.pallas_call(
        body, name="qk_prep_bwd", grid=(T // tt,),
        in_specs=[tok, pl.BlockSpec((tt, QW), lambda t: (t, 0)), tab, tab, tab, tab, gsp], out_specs=[tok, gsp],
        out_shape=[jax.ShapeDtypeStruct((T, W), BF16), jax.ShapeDtypeStruct((R, LANES), F32)],
        compiler_params=_params(("arbitrary",)),
    )(qkv, dq, dk, dv, tab_c, tab_s, gains)


def _band_mask(i):
    r = lax.broadcasted_iota(jnp.int32, (2 * BLOCK, 2 * BLOCK), 0) & (BLOCK - 1)
    c = lax.broadcasted_iota(jnp.int32, (2 * BLOCK, 2 * BLOCK), 1)
    rel = r + BLOCK - c
    return (rel >= 0) & (rel < BLOCK) & ((c >= BLOCK) | (i > 0))


def _swa_softmax(s, valid, sink):
    s = jnp.where(valid, s * ATTN_SCALE, NEG_BIG)
    m = jnp.maximum(jnp.max(s, axis=1, keepdims=True), sink)
    p = jnp.exp(s - m)
    ps = jnp.exp(sink - m)
    denom = jnp.sum(p, axis=1, keepdims=True) + ps
    return p / denom, ps / denom


A_GROUP = 2


Q_WIDTH_A = N_Q_A * HEAD_DIM
N_PAIR_A = Q_WIDTH_A // LANES


def _swa_specs():
    qs = pl.BlockSpec((None, BLOCK, Q_WIDTH_A), lambda b, i: (b, i, 0))

    def kv(col, back):
        return pl.BlockSpec((None, BLOCK, LANES), lambda b, i: (b, jnp.maximum(i - back, 0), col))

    return qs, kv(N_PAIR_A, 1), kv(N_PAIR_A, 0), kv(N_PAIR_A + 1, 1), kv(N_PAIR_A + 1, 0)


def _dup_heads(t):
    lo = lax.broadcasted_iota(jnp.int32, t.shape, 1) < HEAD_DIM
    sw = pltpu.roll(t.astype(F32), HEAD_DIM, 1).astype(BF16)
    return jnp.where(lo, t, sw), jnp.where(lo, sw, t)


def _kv_tiles(kp_ref, kc_ref, vp_ref, vc_ref):
    kd = _dup_heads(jnp.concatenate([kp_ref[...], kc_ref[...]], axis=0))
    vd = _dup_heads(jnp.concatenate([vp_ref[...], vc_ref[...]], axis=0))
    return kd, vd


def _attn_a_fwd(qkn, sinks):
    NB, S, _ = qkn.shape
    qs, kp, kc, vp, vc = _swa_specs()

    def body(q_ref, kp_ref, kc_ref, vp_ref, vc_ref, sink_ref, o_ref):
        i = pl.program_id(1)
        kd, vd = _kv_tiles(kp_ref, kc_ref, vp_ref, vc_ref)
        valid = _band_mask(i)
        lo = lax.broadcasted_iota(jnp.int32, (BLOCK, LANES), 1) < HEAD_DIM
        top = lax.broadcasted_iota(jnp.int32, (2 * BLOCK, 1), 0) < BLOCK
        for first in range(0, N_PAIR_A, A_GROUP):
            pairs = range(first, first + A_GROUP)
            qs_ = [jnp.concatenate(_head_halves(q_ref[:, p * LANES:(p + 1) * LANES], lo), axis=0) for p in pairs]
            ss = [lax.dot_general(q, kd[2 * p // GROUP_A], _NT, preferred_element_type=F32) for q, p in zip(qs_, pairs)]
            pns = [_swa_softmax(s, valid, jnp.where(top, sink_ref[2 * p], sink_ref[2 * p + 1]))[0]
                   for s, p in zip(ss, pairs)]
            pvs = [jnp.dot(pn.astype(BF16), vd[2 * p // GROUP_A], preferred_element_type=F32) for pn, p in zip(pns, pairs)]
            for pv, p in zip(pvs, pairs):
                o_ref[:, p * LANES:(p + 1) * LANES] = jnp.where(lo, pv[:BLOCK], pv[BLOCK:]).astype(BF16)

    return pl.pallas_call(
        body, name="attn_a_fwd", grid=(NB, S // BLOCK),
        in_specs=[qs, kp, kc, vp, vc, pl.BlockSpec(memory_space=pltpu.SMEM)],
        out_specs=qs, out_shape=jax.ShapeDtypeStruct((NB, S, Q_WIDTH_A), BF16),
        compiler_params=_params(("parallel", "arbitrary")),
    )(qkn, qkn, qkn, qkn, qkn, sinks)


def _attn_a_bwd(qkn, do, sinks):
    NB, S, _ = qkn.shape
    qs, kp, kc, vp, vc = _swa_specs()
    full = pl.BlockSpec((None, S, LANES), lambda b, i: (b, 0, 0))
    sink_out = pl.BlockSpec((None, N_Q_A, LANES), lambda b, i: (b, 0, 0))

    def body(q_ref, do_ref, kp_ref, kc_ref, vp_ref, vc_ref, sink_ref, dq_ref, dk_ref, dv_ref, ds_ref, dk_s, dv_s):
        i = pl.program_id(1)

        @pl.when(i == 0)
        def _():
            dk_ref[...] = jnp.zeros_like(dk_ref)
            dv_ref[...] = jnp.zeros_like(dv_ref)
            ds_ref[...] = jnp.zeros_like(ds_ref)

        dk_s[...] = jnp.zeros_like(dk_s)
        dv_s[...] = jnp.zeros_like(dv_s)
        kd, vd = _kv_tiles(kp_ref, kc_ref, vp_ref, vc_ref)
        valid = _band_mask(i)
        lo = lax.broadcasted_iota(jnp.int32, (BLOCK, LANES), 1) < HEAD_DIM
        top = lax.broadcasted_iota(jnp.int32, (2 * BLOCK, 1), 0) < BLOCK
        for first in range(0, N_PAIR_A, A_GROUP):
            pairs = range(first, first + A_GROUP)
            kvs = [2 * p // GROUP_A for p in pairs]
            qs_ = [jnp.concatenate(_head_halves(q_ref[:, p * LANES:(p + 1) * LANES], lo), axis=0) for p in pairs]
            dos = [jnp.concatenate(_head_halves(do_ref[:, p * LANES:(p + 1) * LANES], lo), axis=0) for p in pairs]
            ss = [lax.dot_general(q, kd[kv], _NT, preferred_element_type=F32) for q, kv in zip(qs_, kvs)]
            dps = [lax.dot_general(d, vd[kv], _NT, preferred_element_type=F32) for d, kv in zip(dos, kvs)]
            sm = [_swa_softmax(s, valid, jnp.where(top, sink_ref[2 * p], sink_ref[2 * p + 1])) for s, p in zip(ss, pairs)]
            deltas = [jnp.sum(pn * dp, axis=1, keepdims=True) for (pn, _), dp in zip(sm, dps)]
            dsbs = [(pn * (dp - delta) * ATTN_SCALE).astype(BF16) for (pn, _), dp, delta in zip(sm, dps, deltas)]
            for n, p in enumerate(pairs):
                dq2 = jnp.dot(dsbs[n], kd[kvs[n]], preferred_element_type=F32)
                dq_ref[:, p * LANES:(p + 1) * LANES] = jnp.where(lo, dq2[:BLOCK], dq2[BLOCK:])
                dk_s[kvs[n]] += lax.dot_general(dsbs[n], qs_[n], _TN, preferred_element_type=F32)
                dv_s[kvs[n]] += lax.dot_general(sm[n][0].astype(BF16), dos[n], _TN, preferred_element_type=F32)
                t = sm[n][1] * deltas[n]
                for hh in range(2):
                    dsink = -jnp.sum(t[hh * BLOCK:(hh + 1) * BLOCK], axis=0, keepdims=True)
                    ds_ref[2 * p + hh:2 * p + hh + 1, :] += jnp.broadcast_to(dsink, (1, LANES))

        lo2 = lax.broadcasted_iota(jnp.int32, (2 * BLOCK, LANES), 1) < HEAD_DIM

        def fold(acc):
            halves = [acc[kv] + pltpu.roll(acc[kv], HEAD_DIM, 1) for kv in range(N_KV_A)]
            return jnp.where(lo2, halves[0], halves[1])

        dk2, dv2 = fold(dk_s), fold(dv_s)

        @pl.when(i > 0)
        def _():
            start = pl.multiple_of((i - 1) * BLOCK, BLOCK)
            dk_ref[pl.ds(start, 2 * BLOCK), :] += dk2
            dv_ref[pl.ds(start, 2 * BLOCK), :] += dv2

        @pl.when(i == 0)
        def _():
            dk_ref[0:BLOCK, :] += dk2[BLOCK:, :]
            dv_ref[0:BLOCK, :] += dv2[BLOCK:, :]

    slots = pltpu.VMEM((N_KV_A, 2 * BLOCK, LANES), F32)
    return pl.pallas_call(
        body, name="attn_a_bwd", grid=(NB, S // BLOCK),
        in_specs=[qs, qs, kp, kc, vp, vc, pl.BlockSpec(memory_space=pltpu.SMEM)],
        out_specs=[qs, full, full, sink_out],
        out_shape=[jax.ShapeDtypeStruct((NB, S, Q_WIDTH_A), F32), jax.ShapeDtypeStruct((NB, S, LANES), F32),
                   jax.ShapeDtypeStruct((NB, S, LANES), F32), jax.ShapeDtypeStruct((NB, N_Q_A, LANES), F32)],
        scratch_shapes=[slots, slots],
        compiler_params=_params(("parallel", "arbitrary")),
    )(qkn, do, qkn, qkn, qkn, qkn, sinks)


def _cumsum_mats():
    src = lax.broadcasted_iota(jnp.int32, (2 * BLOCK, 2 * BLOCK), 0) % BLOCK
    dst = lax.broadcasted_iota(jnp.int32, (2 * BLOCK, 2 * BLOCK), 1)
    ones = dst >= BLOCK
    rev = ((src > dst) | ones).astype(BF16)
    fwd = ((src < dst) | ones).astype(BF16)
    return rev, fwd


def _log_sigmoids(z):
    sp = jnp.log(1.0 + jnp.exp(-jnp.abs(z)))
    return jnp.minimum(z, 0.0) - sp, -(jnp.maximum(z, 0.0) + sp)


def _cumsum_mxu_many(vs, mat):
    parts = []
    for v in vs:
        hi = v.astype(BF16)
        parts.append(jnp.concatenate([hi, (v - hi.astype(F32)).astype(BF16)], axis=1))
    r = jnp.dot(jnp.concatenate(parts, axis=0), mat, preferred_element_type=F32)
    return [(r[n * BLOCK:(n + 1) * BLOCK, :BLOCK], r[n * BLOCK:(n + 1) * BLOCK, BLOCK:]) for n in range(len(vs))]


def _strict_mask():
    r = lax.broadcasted_iota(jnp.int32, (BLOCK, BLOCK), 0)
    c = lax.broadcasted_iota(jnp.int32, (BLOCK, BLOCK), 1)
    return c < r


def _tile(ref, j):
    return ref[pl.ds(pl.multiple_of(j * BLOCK, BLOCK), BLOCK), :]


SWEEP_EXIT = -88.0


def _head_halves(t, lo):
    zero = jnp.zeros_like(t)
    return jnp.where(lo, t, zero), jnp.where(lo, zero, t)


def _sb_specs(S, HD, width):
    n = HD // width
    blk = pl.BlockSpec((None, BLOCK, width), lambda b, p, i: (b, i, p))
    k_full = pl.BlockSpec((None, S, width), lambda b, p, i: (b, 0, n + p))
    v_full = pl.BlockSpec((None, S, width), lambda b, p, i: (b, 0, 2 * n + p))
    mat = pl.BlockSpec((2 * BLOCK, 2 * BLOCK), lambda b, p, i: (0, 0))
    return blk, k_full, v_full, mat


SB_FWD_PAIRS = 4
SB_BWD_PAIRS = 2
SB_BWD_TILES = 2
SB_BWD_VMEM_LIMIT_BYTES = 58 * 1024 * 1024


def _attn_b_fwd(qkv, rev):
    NB, S, W = qkv.shape
    HD = W // 3
    width = SB_FWD_PAIRS * LANES
    n_heads = 2 * SB_FWD_PAIRS
    blk, k_full, v_full, mat = _sb_specs(S, HD, width)

    def body(q_ref, k_ref, v_ref, rev_ref, o_ref):
        i = pl.program_id(2)
        rv = rev_ref[...]
        mask = _strict_mask()
        lo = lax.broadcasted_iota(jnp.int32, (BLOCK, LANES), 1) < HEAD_DIM
        q_all = q_ref[...]
        q_stack = [jnp.concatenate(_head_halves(q_all[:, p * LANES:(p + 1) * LANES] * ATTN_SCALE, lo), axis=0)
                   for p in range(SB_FWD_PAIRS)]

        def pair_tiles(ref, j):
            t = _tile(ref, j)
            return [t[:, p * LANES:(p + 1) * LANES] for p in range(SB_FWD_PAIRS)]

        def tile_pass(j, carries, diagonal):
            ks, vs = pair_tiles(k_ref, j), pair_tiles(v_ref, j)
            zs = []
            for p in range(SB_FWD_PAIRS):
                z2 = lax.dot_general(q_stack[p], ks[p], _NT, preferred_element_type=F32)
                zs += [z2[:BLOCK], z2[BLOCK:]]
            logs = [_log_sigmoids(z) for z in zs]
            cums = _cumsum_mxu_many([jnp.where(mask, lm, 0.0) if diagonal else lm for _, lm in logs], rv)
            probs, new_c = [], []
            for h in range(n_heads):
                after, rs = cums[h]
                if diagonal:
                    a = jnp.where(mask, jnp.exp(logs[h][0] + after), 0.0)
                    new_c.append(rs)
                else:
                    a = jnp.exp(logs[h][0] + after + carries[h])
                    new_c.append(carries[h] + rs)
                probs.append(a.astype(BF16))
            outs = []
            for p in range(SB_FWD_PAIRS):
                pv = jnp.dot(jnp.concatenate(probs[2 * p:2 * p + 2], axis=0), vs[p], preferred_element_type=F32)
                outs.append(jnp.where(lo, pv[:BLOCK], pv[BLOCK:]))
            return new_c, outs

        carries, accs = tile_pass(i, None, True)

        def live(cs):
            top = cs[0]
            for c in cs[1:]:
                top = jnp.maximum(top, c)
            return jnp.max(top) > SWEEP_EXIT

        def cond(st):
            return (st[0] < i) & st[1]

        def step(st):
            jj, _, cs, accs = st
            new_c, outs = tile_pass(i - 1 - jj, cs, False)
            return jj + 1, live(new_c), new_c, [acc + o for acc, o in zip(accs, outs)]

        st = lax.while_loop(cond, step, (jnp.int32(0), live(carries), carries, accs))
        for p in range(SB_FWD_PAIRS):
            o_ref[:, p * LANES:(p + 1) * LANES] = st[3][p].astype(BF16)

    return pl.pallas_call(
        body, name="attn_b_fwd", grid=(NB, HD // width, S // BLOCK),
        in_specs=[blk, k_full, v_full, mat], out_specs=blk,
        out_shape=jax.ShapeDtypeStruct((NB, S, HD), BF16),
        compiler_params=_params(("parallel", "parallel", "arbitrary")),
    )(qkv, qkv, qkv, rev)


def _attn_b_bwd(qkv, do, rev, fwd):
    NB, S, W = qkv.shape
    HD = W // 3
    width = SB_BWD_PAIRS * LANES
    n_heads = 2 * SB_BWD_PAIRS
    nj = S // BLOCK
    blk, k_full, v_full, mat = _sb_specs(S, HD, width)
    acc_full = pl.BlockSpec((None, S, width), lambda b, p, i: (b, 0, p))

    def body(q_ref, do_ref, k_ref, v_ref, rev_ref, fwd_ref, dq_ref, dk_ref, dv_ref, sig_s, a_s, e_s):
        i = pl.program_id(2)

        @pl.when(i == 0)
        def _():
            dk_ref[...] = jnp.zeros_like(dk_ref)
            dv_ref[...] = jnp.zeros_like(dv_ref)

        rv, fw = rev_ref[...], fwd_ref[...]
        mask = _strict_mask()
        lo = lax.broadcasted_iota(jnp.int32, (BLOCK, LANES), 1) < HEAD_DIM
        pairs = range(SB_BWD_PAIRS)

        def cols(p):
            return slice(p * LANES, (p + 1) * LANES)

        q_stack = [jnp.concatenate(_head_halves(q_ref[:, cols(p)], lo), axis=0) for p in pairs]
        qs_stack = [q * ATTN_SCALE for q in q_stack]
        do_stack = [jnp.concatenate(_head_halves(do_ref[:, cols(p)], lo), axis=0) for p in pairs]

        def sweep1_tiles(js, carries, diagonal):
            zs, das = [], []
            for j in js:
                kj, vj = _tile(k_ref, j), _tile(v_ref, j)
                for p in pairs:
                    z2 = lax.dot_general(qs_stack[p], kj[:, cols(p)], _NT, preferred_element_type=F32)
                    da2 = lax.dot_general(do_stack[p], vj[:, cols(p)], _NT, preferred_element_type=F32)
                    zs += [z2[:BLOCK], z2[BLOCK:]]
                    das += [da2[:BLOCK], da2[BLOCK:]]
            logs = [_log_sigmoids(z) for z in zs]
            cums = _cumsum_mxu_many([jnp.where(mask, lm, 0.0) if diagonal else lm for _, lm in logs], rv)
            new_c, stores = [], []
            for h in range(n_heads):
                carry = None if diagonal else carries[h]
                for t, j in enumerate(js):
                    n = t * n_heads + h
                    lb, (after, rs) = logs[n][0], cums[n]
                    if diagonal:
                        a = jnp.where(mask, jnp.exp(lb + after), 0.0)
                        carry = rs
                    else:
                        a = jnp.exp(lb + after + carry)
                        carry = carry + rs
                    stores.append((t, h, j, jnp.exp(lb), a.astype(BF16), das[n] * a))
                new_c.append(carry)
            for t, h, j, sg, ab, e in sorted(stores, key=lambda s: -s[0]):
                sig_s[h, j] = sg
                a_s[h, j] = ab
                e_s[h, j] = e
            return new_c

        carries = sweep1_tiles([i], None, True)

        def live(cs):
            top = cs[0]
            for c in cs[1:]:
                top = jnp.maximum(top, c)
            return jnp.max(top) > SWEEP_EXIT

        def cond(st):
            return (SB_BWD_TILES * st[0] < i) & st[1]

        def sweep1(st):
            first = i - 1 - SB_BWD_TILES * st[0]
            new_c = sweep1_tiles([jnp.maximum(first - t, 0) for t in range(SB_BWD_TILES)], st[2], False)
            return st[0] + 1, live(new_c), new_c

        trips = lax.while_loop(cond, sweep1, (jnp.int32(0), live(carries), carries))[0]
        lowest = jnp.maximum(i - SB_BWD_TILES * trips, 0)

        def grads(js, st, diagonal):
            prefixes, dqs = st
            es = [e_s[h, j] for j in js for h in range(n_heads)]
            cums = _cumsum_mxu_many(es, fw)
            dzs, new_p = [], []
            for h in range(n_heads):
                prefix = prefixes[h]
                for t, j in enumerate(js):
                    n = t * n_heads + h
                    sg = sig_s[h, j]
                    e_before, rs = cums[n]
                    dz = (es[n] * (1.0 - sg) - (e_before + prefix) * sg) * ATTN_SCALE
                    if diagonal:
                        dz = jnp.where(mask, dz, 0.0)
                    dzs.append((t, h, dz.astype(BF16)))
                    prefix = prefix + rs
                new_p.append(prefix)
            dz_of = {(t, h): dz for t, h, dz in dzs}
            new_dq = list(dqs)
            for t, j in enumerate(js):
                kj = _tile(k_ref, j)
                rows = pl.ds(pl.multiple_of(j * BLOCK, BLOCK), BLOCK)
                for p in pairs:
                    dz_stack = jnp.concatenate([dz_of[t, 2 * p], dz_of[t, 2 * p + 1]], axis=0)
                    a_stack = jnp.concatenate([a_s[2 * p, j], a_s[2 * p + 1, j]], axis=0)
                    dq2 = jnp.dot(dz_stack, kj[:, cols(p)], preferred_element_type=F32)
                    new_dq[p] = new_dq[p] + jnp.where(lo, dq2[:BLOCK], dq2[BLOCK:])
                    dk_ref[rows, cols(p)] += lax.dot_general(dz_stack, q_stack[p], _TN, preferred_element_type=F32)
                    dv_ref[rows, cols(p)] += lax.dot_general(a_stack, do_stack[p], _TN, preferred_element_type=F32)
            return new_p, new_dq

        zeros = jnp.zeros((BLOCK, BLOCK), F32)
        st = ([zeros] * n_heads, [zeros] * SB_BWD_PAIRS)
        count = i - lowest
        st = lax.fori_loop(0, count % SB_BWD_TILES, lambda t, st: grads([lowest + t], st, False), st)
        start = lowest + count % SB_BWD_TILES
        st = lax.fori_loop(0, count // SB_BWD_TILES,
                           lambda t, st: grads([start + SB_BWD_TILES * t + u for u in range(SB_BWD_TILES)], st, False), st)
        dqs = grads([i], st, True)[1]
        for p in pairs:
            dq_ref[:, cols(p)] = dqs[p]

    f32_stash = pltpu.VMEM((n_heads, nj, BLOCK, BLOCK), F32)
    bf16_stash = pltpu.VMEM((n_heads, nj, BLOCK, BLOCK), BF16)
    return pl.pallas_call(
        body, name="attn_b_bwd", grid=(NB, HD // width, nj),
        in_specs=[blk, blk, k_full, v_full, mat, mat], out_specs=[blk, acc_full, acc_full],
        out_shape=[jax.ShapeDtypeStruct((NB, S, HD), F32)] * 3,
        scratch_shapes=[f32_stash, bf16_stash, f32_stash],
        compiler_params=_params(("parallel", "parallel", "arbitrary"), SB_BWD_VMEM_LIMIT_BYTES),
    )(qkv, do, qkv, qkv, rev, fwd)


def _ada_fwd(c_all, w, b):
    L, D, N = w.shape
    B = c_all.shape[0]

    def body(c_ref, w_ref, b_ref, o_ref):
        cv = c_ref[...]
        cond = (cv * _sigmoid(cv)).astype(BF16)
        o_ref[...] = jnp.dot(cond, w_ref[...].astype(BF16), preferred_element_type=F32) + b_ref[...]

    return pl.pallas_call(
        body, name="ada_fwd", grid=(L,),
        in_specs=[pl.BlockSpec((B, D), lambda l: (0, 0)), pl.BlockSpec((None, D, N), lambda l: (l, 0, 0)),
                  pl.BlockSpec((None, 1, N), lambda l: (l, 0, 0))],
        out_specs=pl.BlockSpec((None, B, N), lambda l: (l, 0, 0)),
        out_shape=jax.ShapeDtypeStruct((L, B, N), F32),
        compiler_params=_params(("parallel",)),
    )(c_all, w, b)


def _ada_bwd(c_all, dmod_all, dmod_shard):
    L, B, N = dmod_shard.shape
    D = c_all.shape[1]
    N_all = dmod_all.shape[2]

    def body(c_ref, da_ref, ds_ref, gw_ref, gb_ref):
        cv = c_ref[...]
        cond = (cv * _sigmoid(cv)).astype(BF16)
        gw_ref[...] = lax.dot_general(cond, ds_ref[...].astype(BF16), _TN, preferred_element_type=F32)
        gb_ref[...] = jnp.sum(da_ref[...], axis=0, keepdims=True)

    return pl.pallas_call(
        body, name="ada_bwd", grid=(L,),
        in_specs=[pl.BlockSpec((B, D), lambda l: (0, 0)), pl.BlockSpec((None, B, N_all), lambda l: (l, 0, 0)),
                  pl.BlockSpec((None, B, N), lambda l: (l, 0, 0))],
        out_specs=[pl.BlockSpec((None, D, N), lambda l: (l, 0, 0)), pl.BlockSpec((None, 1, N_all), lambda l: (l, 0, 0))],
        out_shape=[jax.ShapeDtypeStruct((L, D, N), F32), jax.ShapeDtypeStruct((L, 1, N_all), F32)],
        compiler_params=_params(("parallel",)),
    )(c_all, dmod_all, dmod_shard)


def _adamw(w, g, m, v, name):
    shape = w.shape
    C = shape[-1]
    R = w.size // C
    tr = _pick(R, max(8, (1 << 18) // C), 8)
    c1 = 1.0 - ADAM_B1 ** ADAM_STEP
    c2 = 1.0 - ADAM_B2 ** ADAM_STEP

    def body(w_ref, g_ref, m_ref, v_ref, d_ref, nm_ref, nv_ref):
        gv = g_ref[...]
        nm = ADAM_B1 * m_ref[...] + (1.0 - ADAM_B1) * gv
        nv = ADAM_B2 * v_ref[...] + (1.0 - ADAM_B2) * (gv * gv)
        d_ref[...] = -ADAM_LR * ((nm / c1) / (jnp.sqrt(nv / c2) + ADAM_EPS) + ADAM_WD * w_ref[...])
        nm_ref[...] = nm
        nv_ref[...] = nv

    spec = pl.BlockSpec((tr, C), lambda r: (r, 0))
    out = pl.pallas_call(
        body, name=name, grid=(R // tr,), in_specs=[spec] * 4, out_specs=[spec] * 3,
        out_shape=[jax.ShapeDtypeStruct((R, C), F32)] * 3,
        compiler_params=_params(("parallel",)),
    )(*[t.reshape(R, C) for t in (w, g, m, v)])
    return [t.reshape(shape) for t in out]


_SHARDED = (("wqkv_a", 2), ("wo_a", 1), ("wqkv_b", 2), ("wo_b", 1), ("w_gate", 2), ("w_up", 2), ("w_down", 1))


def _pack_full(layers, axis, gate_up=None):
    L = len(layers)
    R, C = layers[0].shape

    def shards(m):
        if gate_up is not None:
            F = C // 2
            tf, Cs = _ff_tile(F), F // 4
            assert tf % Cs == 0
            starts = [(2 * (s * Cs // tf) + gate_up) * tf + s * Cs % tf for s in range(4)]
            return jnp.stack([m[:, st:st + Cs] for st in starts])
        if axis == 2:
            return m.reshape(R, 4, C // 4).transpose(1, 0, 2)
        return m.reshape(4, R // 4, C)

    halves = [jnp.stack([shards(m) for m in layers[h * (L // 2):(h + 1) * (L // 2)]], axis=1) for h in range(2)]
    return jnp.stack(halves)


def _unpack_full(gathered, axis):
    _, Lh, Rs, Cs = gathered.shape
    t = gathered.reshape(4, 2, Lh, Rs, Cs)
    layers = []
    for h in range(2):
        for l in range(Lh):
            piece = t[:, h, l]
            if axis == 2:
                layers.append(piece.transpose(1, 0, 2).reshape(Rs, 4 * Cs))
            else:
                layers.append(piece.reshape(4 * Rs, Cs))
    return layers


def _sum_slabs(own, recv, name, with_bf16=False):
    C = own.shape[-1]
    out = _sum_leading(recv.reshape(recv.shape[0], -1, C), name, own=own.reshape(-1, C), with_bf16=with_bf16)
    if with_bf16:
        return out[0].reshape(own.shape), out[1].reshape(own.shape)
    return out.reshape(own.shape)


def _gather8(x, name):
    return _all_gather8([x], name)[0]


def _rope_tables(positions):
    half = ROT_DIM // 2
    inv_freq = jnp.power(jnp.float32(ROPE_THETA), -jnp.arange(half, dtype=F32) * 2.0 / ROT_DIM)
    ang = positions.astype(F32).reshape(-1, 1) * inv_freq
    cos, sin = jnp.cos(ang), jnp.sin(ang)
    T = ang.shape[0]
    rest = HEAD_DIM - ROT_DIM
    c64 = jnp.concatenate([cos, cos, jnp.ones((T, rest), F32)], axis=1)
    s64 = jnp.concatenate([-sin, sin, jnp.zeros((T, rest), F32)], axis=1)
    return jnp.tile(c64, (1, 2)), jnp.tile(s64, (1, 2))


def _gain_rows(q_gain, k_gain):
    q2 = jnp.tile(q_gain.reshape(1, HEAD_DIM), (GROUP_A, 2))
    k2 = jnp.tile(k_gain.reshape(1, HEAD_DIM), (1, 2))
    return jnp.concatenate([q2, k2, jnp.ones((1, LANES), F32)], axis=0)


def _local_step(x, positions, mod, norm1_g, norm2_g, q_norm_a, k_norm_a, sinks_a,
                wqkv_a, wo_a, wqkv_b, wo_b, wgu, wd, loss_target):
    NB, S, D = x.shape
    T = NB * S
    QA = N_Q_A * HEAD_DIM
    tab_c, tab_s = _rope_tables(positions)
    rev, fwd = _cumsum_mats()

    saved = []
    xc = x
    for i in range(DEPTH):
        j = i // 2
        sh1, sc1, g1, sh2, sc2, g2 = [mod[i][:, k * D:(k + 1) * D].reshape(NB, 1, D) for k in range(6)]
        st = dict(x=xc, sc1=sc1, g1=g1, sc2=sc2, g2=g2)
        h = _norm_mod_fwd(xc, norm1_g[i:i + 1], sc1, sh1)
        st["h"] = h.reshape(T, D)
        if i % 2 == 0:
            st["qkv"] = _matmul(st["h"], wqkv_a[j], "nn", F32, "qkv_a")
            st["gains"] = _gain_rows(q_norm_a[j], k_norm_a[j])
            st["qkn"] = _qk_prep_fwd(st["qkv"], tab_c, tab_s, st["gains"]).reshape(NB, S, -1)
            st["o"] = _attn_a_fwd(st["qkn"], sinks_a[j]).reshape(T, QA)
            y = _matmul(st["o"], wo_a[j], "nn", F32, "wo_a")
        else:
            st["qkv"] = _matmul(st["h"], wqkv_b[j], "nn", BF16, "qkv_b").reshape(NB, S, -1)
            st["o"] = _attn_b_fwd(st["qkv"], rev).reshape(T, N_H_B * HEAD_DIM)
            y = _matmul(st["o"], wo_b[j], "nn", F32, "wo_b")
        st["y"] = y.reshape(NB, S, D)
        x1 = _gate_res(xc, st["y"], g1)
        st["x1"] = x1
        h2 = _norm_mod_fwd(x1, norm2_g[i:i + 1], sc2, sh2)
        st["h2"] = h2.reshape(T, D)
        st["gu"], st["act"] = _matmul(st["h2"], wgu[i], "nn", F32, "gate_up", swiglu=True)
        st["m"] = _matmul(st["act"], wd[i], "nn", F32, "down").reshape(NB, S, D)
        xc = _gate_res(x1, st["m"], g2)
        saved.append(st)

    loss, dx = _loss_fwd_bwd(xc, loss_target)

    grads = {name: [None] * n for name, n in
             (("wqkv_a", 2), ("wo_a", 2), ("wqkv_b", 2), ("wo_b", 2), ("wgu", DEPTH), ("wd", DEPTH),
              ("norm1_g", DEPTH), ("norm2_g", DEPTH), ("q_norm_a", 2), ("k_norm_a", 2), ("sinks_a", 2))}
    dmod = [None] * DEPTH
    for i in reversed(range(DEPTH)):
        j = i // 2
        st = saved[i]
        dm, dg2 = _gate_res_bwd(dx, st["m"], st["g2"])
        dm = dm.reshape(T, D)
        grads["wd"][i] = _matmul(st["act"], dm, "tn", F32, "d_wd")
        dgu = _swiglu_bwd(dm, wd[i], st["gu"])
        dh2 = _matmul(dgu, wgu[i], "nt", F32, "d_h2")
        grads["wgu"][i] = _matmul(st["h2"], dgu, "tn", F32, "d_wgu")
        dx1, dsh2, dsc2, grads["norm2_g"][i] = _norm_mod_bwd(
            st["x1"], norm2_g[i:i + 1], st["sc2"], dh2.reshape(NB, S, D), dx)
        dy, dg1 = _gate_res_bwd(dx1, st["y"], st["g1"])
        dy = dy.reshape(T, D)
        if i % 2 == 0:
            do = _matmul(dy, wo_a[j], "nt", BF16, "d_o_a").reshape(NB, S, QA)
            grads["wo_a"][j] = _matmul(st["o"], dy, "tn", F32, "d_wo_a")
            dq, dk, dv, dsink = _attn_a_bwd(st["qkn"], do, sinks_a[j])
            dqkv, dgain = _qk_prep_bwd(st["qkv"], dq.reshape(T, QA), dk.reshape(T, LANES), dv.reshape(T, LANES),
                                       tab_c, tab_s, st["gains"])
            dh = _matmul(dqkv, wqkv_a[j], "nt", F32, "d_h_a")
            grads["wqkv_a"][j] = _matmul(st["h"], dqkv, "tn", F32, "d_wqkv_a")
            grads["q_norm_a"][j] = jnp.sum(dgain[:GROUP_A].reshape(2 * GROUP_A, HEAD_DIM), axis=0)
            grads["k_norm_a"][j] = jnp.sum(dgain[GROUP_A].reshape(2, HEAD_DIM), axis=0)
            grads["sinks_a"][j] = jnp.sum(dsink[..., 0], axis=0)
        else:
            do = _matmul(dy, wo_b[j], "nt", BF16, "d_o_b").reshape(NB, S, -1)
            grads["wo_b"][j] = _matmul(st["o"], dy, "tn", F32, "d_wo_b")
            dq, dk, dv = _attn_b_bwd(st["qkv"], do, rev, fwd)
            dqkv = jnp.concatenate([dq, dk, dv], axis=-1).reshape(T, -1).astype(BF16)
            dh = _matmul(dqkv, wqkv_b[j], "nt", F32, "d_h_b")
            grads["wqkv_b"][j] = _matmul(st["h"], dqkv, "tn", F32, "d_wqkv_b")
        dx, dsh1, dsc1, grads["norm1_g"][i] = _norm_mod_bwd(
            st["x"], norm1_g[i:i + 1], st["sc1"], dh.reshape(NB, S, D), dx1)
        dmod[i] = jnp.concatenate([dsh1, dsc1, dg1, dsh2, dsc2, dg2], axis=-1).reshape(NB, 6 * D)

    matrices = ("wqkv_a", "wo_a", "wqkv_b", "wo_b", "wgu", "wd")
    grads = {name: parts if name in matrices else jnp.stack(parts) for name, parts in grads.items()}
    return loss, dx, grads, jnp.stack(dmod)


def _rows_of(flat, cols=PACK_COLS):
    n = flat.shape[0]
    pad = (-n) % (8 * cols)
    if pad:
        flat = jnp.concatenate([flat, jnp.zeros((pad,), flat.dtype)])
    return flat.reshape(-1, cols)


def kernel(x, c, positions, ada_w, ada_b, norm1_g, norm2_g, wqkv_a, q_norm_a, k_norm_a, sinks_a, wo_a, wqkv_b, wo_b, w_gate, w_up, w_down, loss_target, m_ada_w, m_ada_b, m_norm1_g, m_norm2_g, m_wqkv_a, m_q_norm_a, m_k_norm_a, m_sinks_a, m_wo_a, m_wqkv_b, m_wo_b, m_w_gate, m_w_up, m_w_down, v_ada_w, v_ada_b, v_norm1_g, v_norm2_g, v_wqkv_a, v_q_norm_a, v_k_norm_a, v_sinks_a, v_wo_a, v_wqkv_b, v_wo_b, v_w_gate, v_w_up, v_w_down):
    xi, yi, ci = lax.axis_index("x"), lax.axis_index("y"), lax.axis_index("c")
    dev = 4 * xi + 2 * yi + ci
    chip = 2 * xi + yi
    NB, S, D = x.shape
    B_all = N_DEV * NB
    L = ada_w.shape[0]
    n_mod = ada_w.shape[2] // 2

    c_all = _gather8(_rows_of(c.reshape(-1), LANES), "gather_c").reshape(N_DEV, -1)[:, :NB * D].reshape(B_all, D)
    ada_w_half = lax.dynamic_slice_in_dim(ada_w, ci * n_mod, n_mod, axis=2)
    ada_b_half = lax.dynamic_slice_in_dim(ada_b, dev * n_mod, n_mod, axis=1).reshape(L, 1, n_mod)
    mod_part = _ada_fwd(c_all, ada_w_half, ada_b_half)
    n_part = L * B_all * n_mod
    mod_all = _gather8(_rows_of(mod_part.reshape(-1)), "gather_mod").reshape(N_DEV, -1)[:, :n_part]
    mod_all = mod_all.reshape(N_DEV, L, B_all, n_mod).transpose(1, 2, 0, 3).reshape(L, B_all, N_DEV * n_mod)
    mod = lax.dynamic_slice_in_dim(mod_all, dev * NB, NB, axis=1)

    shards = dict(wqkv_a=wqkv_a, wo_a=wo_a, wqkv_b=wqkv_b, wo_b=wo_b, w_gate=w_gate, w_up=w_up, w_down=w_down)
    halves = []
    for name, _ in _SHARDED:
        w = shards[name]
        half = lax.dynamic_index_in_dim(w.reshape((2, w.shape[0] // 2) + w.shape[1:]), ci, 0, keepdims=False)
        halves.append(half.astype(BF16))
    gathered = _all_gather8(halves, "gather_weights", local_axis=1, local_chunks=8)
    full = {name: _unpack_full(t, axis) for (name, axis), t in zip(_SHARDED, gathered)}
    wgu = [_interleave(gate, up) for gate, up in zip(full["w_gate"], full["w_up"])]

    loss, grad_x, g, dmod = _local_step(
        x, positions, mod, norm1_g, norm2_g, q_norm_a, k_norm_a, sinks_a,
        full["wqkv_a"], full["wo_a"], full["wqkv_b"], full["wo_b"], wgu, full["w_down"], loss_target)

    g_full = dict(wqkv_a=g["wqkv_a"], wo_a=g["wo_a"], wqkv_b=g["wqkv_b"], wo_b=g["wo_b"],
                  w_gate=g["wgu"], w_up=g["wgu"], w_down=g["wd"])
    which = dict(w_gate=0, w_up=1)
    packed = [_pack_full(g_full[name], axis, which.get(name)) for name, axis in _SHARDED]
    def own(t, index):
        return lax.dynamic_index_in_dim(t, index, 0, keepdims=False)

    from_cores = _exchange(packed, "c", "rs_cores", chunk_axis=0, chunks=4)
    chip_part = [_sum_slabs(own(p, ci), r, "rs_add_cores", with_bf16=True) for p, r in zip(packed, from_cores)]
    from_chips = _exchange([b for _, b in chip_part], "xy", "rs_chips")
    mine = [_sum_slabs(own(p, chip), r, "rs_add_chips") for (p, _), r in zip(chip_part, from_chips)]
    theirs = _sibling_send(mine, "rs_halves")
    grad = {}
    for (name, _), m, t in zip(_SHARDED, mine, theirs):
        first, second = jnp.where(ci == 0, m, t), jnp.where(ci == 0, t, m)
        grad[name] = jnp.stack([first, second]).reshape(shards[name].shape)

    small_names = ("norm1_g", "norm2_g", "q_norm_a", "k_norm_a", "sinks_a")
    small = [dmod.reshape(-1)] + [g[name].reshape(-1) for name in small_names] + [loss.reshape(-1)]
    small_sizes = [t.shape[0] for t in small]
    small_rows = _rows_of(jnp.concatenate(small))
    small_all = _gather8(small_rows, "gather_small")
    small_sum = _sum_leading(small_all, "sum_small").reshape(-1)
    n_dmod = small_sizes[0]
    dmod_all = small_all.reshape(N_DEV, -1)[:, :n_dmod].reshape(N_DEV, L, NB, 6 * D)
    dmod_all = dmod_all.transpose(1, 0, 2, 3).reshape(L, B_all, 6 * D)
    off = n_dmod
    for name, sz in zip(small_names + ("loss",), small_sizes[1:]):
        grad[name] = small_sum[off:off + sz]
        off += sz
    loss_total = grad.pop("loss").reshape(())
    for name, ref in (("norm1_g", norm1_g), ("norm2_g", norm2_g), ("q_norm_a", q_norm_a),
                      ("k_norm_a", k_norm_a), ("sinks_a", sinks_a)):
        grad[name] = grad[name].reshape(ref.shape)

    n_shard = ada_w.shape[2]
    dmod_shard = lax.dynamic_slice_in_dim(dmod_all, chip * n_shard, n_shard, axis=2)
    grad["ada_w"], gb = _ada_bwd(c_all, dmod_all, dmod_shard)
    grad["ada_b"] = gb.reshape(ada_b.shape)

    weights = dict(ada_w=ada_w, ada_b=ada_b, norm1_g=norm1_g, norm2_g=norm2_g, wqkv_a=wqkv_a, q_norm_a=q_norm_a,
                   k_norm_a=k_norm_a, sinks_a=sinks_a, wo_a=wo_a, wqkv_b=wqkv_b, wo_b=wo_b, w_gate=w_gate,
                   w_up=w_up, w_down=w_down)
    m_in = dict(ada_w=m_ada_w, ada_b=m_ada_b, norm1_g=m_norm1_g, norm2_g=m_norm2_g, wqkv_a=m_wqkv_a,
                q_norm_a=m_q_norm_a, k_norm_a=m_k_norm_a, sinks_a=m_sinks_a, wo_a=m_wo_a, wqkv_b=m_wqkv_b,
                wo_b=m_wo_b, w_gate=m_w_gate, w_up=m_w_up, w_down=m_w_down)
    v_in = dict(ada_w=v_ada_w, ada_b=v_ada_b, norm1_g=v_norm1_g, norm2_g=v_norm2_g, wqkv_a=v_wqkv_a,
                q_norm_a=v_q_norm_a, k_norm_a=v_k_norm_a, sinks_a=v_sinks_a, wo_a=v_wo_a, wqkv_b=v_wqkv_b,
                wo_b=v_wo_b, w_gate=v_w_gate, w_up=v_w_up, w_down=v_w_down)
    names = list(weights)
    delta, new_m, new_v = {}, {}, {}
    for name in names:
        delta[name], new_m[name], new_v[name] = _adamw(weights[name], grad[name], m_in[name], v_in[name],
                                                       "adamw_" + name)
    return (loss_total, grad_x, *[grad[k] for k in names], *[delta[k] for k in names],
            *[new_m[k] for k in names], *[new_v[k] for k in names])
```

```python
import jax
import jax.numpy as jnp
from jax import lax
from jax.experimental import pallas as pl
from jax.experimental.pallas import tpu as pltpu

F32 = jnp.float32
BF16 = jnp.bfloat16

DEPTH = 4
HEAD_DIM = 64
N_Q_A = 16
N_KV_A = 2
GROUP_A = N_Q_A // N_KV_A
N_H_B = 16
BLOCK = 128
ROT_DIM = HEAD_DIM // 4
ROPE_THETA = 500000.0
EPS = 1e-6
ATTN_SCALE = HEAD_DIM ** -0.5
NEG_BIG = -1e30

ADAM_LR = 0.001
ADAM_B1 = 0.9
ADAM_B2 = 0.999
ADAM_EPS = 1e-08
ADAM_WD = 0.01
ADAM_STEP = 10

N_DEV = 8
LANES = 128
PACK_COLS = 1024
VMEM_LIMIT_BYTES = 48 * 1024 * 1024
MESH = pl.DeviceIdType.MESH

_NT = (((1,), (1,)), ((), ()))
_TN = (((0,), (0,)), ((), ()))
_NN = (((1,), (0,)), ((), ()))


def _params(sem=None, vmem_limit_bytes=VMEM_LIMIT_BYTES):
    return pltpu.CompilerParams(vmem_limit_bytes=vmem_limit_bytes, dimension_semantics=sem)


def _pick(n, cap, mult):
    best = None
    for t in range(mult, min(n, cap) + 1, mult):
        if n % t == 0:
            best = t
    return n if best is None else best


_ANY = pl.BlockSpec(memory_space=pl.ANY)


def _window(index, axis, q, n, shape):
    rest = [slice(None)] * len(shape)
    size = shape[axis] // n
    rest[axis] = pl.ds(q * size, size)
    return tuple(index) + tuple(rest)


def _all_gather8(xs, name, local_axis=0, local_chunks=1):
    n = len(xs)

    def body(*refs):
        x_refs, out_refs = refs[:n], refs[n:2 * n]
        send_sems, recv_sems, local_sems = refs[2 * n:]
        xi, yi, ci = lax.axis_index("x"), lax.axis_index("y"), lax.axis_index("c")
        me, sibling = (xi, yi, ci), (xi, yi, 1 - ci)
        chips = [(1 - xi, yi), (xi, 1 - yi), (1 - xi, 1 - yi)]

        def slab(w, px, py, pc):
            return out_refs[w].at[4 * px + 2 * py + pc]

        def copy(w, k, block, to, src=None):
            return pltpu.make_async_remote_copy(
                src_ref=slab(w, *block) if src is None else src, dst_ref=slab(w, *block),
                send_sem=send_sems.at[k, w], recv_sem=recv_sems.at[k, w], device_id=to, device_id_type=MESH)

        mine = []
        for w in range(n):
            for q in range(local_chunks):
                part = _window((), local_axis, q, local_chunks, xs[w].shape)
                mine.append(pltpu.make_async_copy(x_refs[w].at[part], slab(w, *me).at[part], local_sems.at[w, q]))
                mine[-1].start()
        first = [copy(w, 0, me, sibling, src=x_refs[w]) for w in range(n)]
        first += [copy(w, 1 + j, me, (*chip, ci), src=x_refs[w]) for j, chip in enumerate(chips) for w in range(n)]
        for cp in first:
            cp.start()
        passed = []
        for j, chip in enumerate(chips):
            for w in range(n):
                copy(w, 1 + j, (*chip, ci), me).wait_recv()
                passed.append(copy(w, 4 + j, (*chip, ci), sibling))
                passed[-1].start()
        for w in range(n):
            copy(w, 0, sibling, me).wait_recv()
        for j, chip in enumerate(chips):
            for w in range(n):
                copy(w, 4 + j, (*chip, 1 - ci), me).wait_recv()
        for cp in first + passed:
            cp.wait_send()
        for cp in mine:
            cp.wait()

    return pl.pallas_call(
        body, name=name,
        out_shape=[jax.ShapeDtypeStruct((N_DEV,) + x.shape, x.dtype) for x in xs],
        in_specs=[_ANY] * n, out_specs=[_ANY] * n,
        scratch_shapes=[pltpu.SemaphoreType.DMA((7, n)), pltpu.SemaphoreType.DMA((7, n)),
                        pltpu.SemaphoreType.DMA((n, local_chunks))],
    )(*xs)


def _exchange(xs, group, name, chunk_axis=0, chunks=1):
    n = len(xs)
    n_peers = 1 if group == "c" else 3

    def body(*refs):
        x_refs, out_refs = refs[:n], refs[n:2 * n]
        send_sems, recv_sems = refs[2 * n:]
        xi, yi, ci = lax.axis_index("x"), lax.axis_index("y"), lax.axis_index("c")
        if group == "c":
            peers = [(1 - ci, (xi, yi, 1 - ci))]
        else:
            peers = [(2 * (1 - xi) + yi, (1 - xi, yi, ci)),
                     (2 * xi + (1 - yi), (xi, 1 - yi, ci)),
                     (2 * (1 - xi) + (1 - yi), (1 - xi, 1 - yi, ci))]
        copies = []
        for k, (p, dev) in enumerate(peers):
            for w in range(n):
                slab_shape = xs[w].shape[1:]
                for q in range(chunks):
                    copies.append(pltpu.make_async_remote_copy(
                        src_ref=x_refs[w].at[_window((p,), chunk_axis, q, chunks, slab_shape)],
                        dst_ref=out_refs[w].at[_window((k,), chunk_axis, q, chunks, slab_shape)],
                        send_sem=send_sems.at[k, w, q], recv_sem=recv_sems.at[k, w, q],
                        device_id=dev, device_id_type=MESH))
                    copies[-1].start()
        for cp in copies:
            cp.wait()

    return pl.pallas_call(
        body, name=name,
        out_shape=[jax.ShapeDtypeStruct((n_peers,) + x.shape[1:], x.dtype) for x in xs],
        in_specs=[_ANY] * n, out_specs=[_ANY] * n,
        scratch_shapes=[pltpu.SemaphoreType.DMA((n_peers, n, chunks)), pltpu.SemaphoreType.DMA((n_peers, n, chunks))],
    )(*xs)


def _sibling_send(xs, name, chunk_axis=1, chunks=4):
    n = len(xs)

    def body(*refs):
        x_refs, out_refs = refs[:n], refs[n:2 * n]
        send_sems, recv_sems = refs[2 * n:]
        xi, yi, ci = lax.axis_index("x"), lax.axis_index("y"), lax.axis_index("c")
        copies = []
        for w in range(n):
            for q in range(chunks):
                part = _window((), chunk_axis, q, chunks, xs[w].shape)
                copies.append(pltpu.make_async_remote_copy(
                    src_ref=x_refs[w].at[part], dst_ref=out_refs[w].at[part],
                    send_sem=send_sems.at[w, q], recv_sem=recv_sems.at[w, q],
                    device_id=(xi, yi, 1 - ci), device_id_type=MESH))
                copies[-1].start()
        for cp in copies:
            cp.wait()

    return pl.pallas_call(
        body, name=name,
        out_shape=[jax.ShapeDtypeStruct(x.shape, x.dtype) for x in xs],
        in_specs=[_ANY] * n, out_specs=[_ANY] * n,
        scratch_shapes=[pltpu.SemaphoreType.DMA((n, chunks)), pltpu.SemaphoreType.DMA((n, chunks))],
    )(*xs)


def _sum_leading(x, name, own=None, with_bf16=False):
    P, R, C = x.shape
    tr = _pick(R, max(16, (1 << 19) // (C * (P + 1))), 16)

    def body(*refs):
        n_in = 1 if own is None else 2
        x_ref = refs[n_in - 1]
        acc = x_ref[0].astype(F32) if own is None else refs[0][...] + x_ref[0].astype(F32)
        for p in range(1, P):
            acc = acc + x_ref[p].astype(F32)
        refs[n_in][...] = acc
        if with_bf16:
            refs[n_in + 1][...] = acc.astype(BF16)

    flat = pl.BlockSpec((tr, C), lambda r: (r, 0))
    slabs = pl.BlockSpec((P, tr, C), lambda r: (0, r, 0))
    out = pl.pallas_call(
        body, name=name, grid=(R // tr,),
        in_specs=[slabs] if own is None else [flat, slabs],
        out_specs=[flat, flat] if with_bf16 else [flat],
        out_shape=[jax.ShapeDtypeStruct((R, C), F32)] + ([jax.ShapeDtypeStruct((R, C), BF16)] if with_bf16 else []),
        compiler_params=_params(("arbitrary",)),
    )(*([x] if own is None else [own, x]))
    return out if with_bf16 else out[0]


MATMUL_SINGLE_K = 1280
MATMUL_VMEM_BUDGET = 36 * 1024 * 1024


def _matmul(a, b, mode, out_dtype, name, swiglu=False):
    if mode == "nn":
        (M, K), N = a.shape, b.shape[1]
    elif mode == "nt":
        (M, K), N = a.shape, b.shape[0]
    else:
        (K, M), N = a.shape, b.shape[1]
    tm = _pick(M, 1024 if mode != "tn" else 1536, 128)
    tn = _pick(N, 1536, 128)
    if swiglu:
        tm, tn = _pick(M, 1024 if out_dtype == BF16 else 512, 128), 2 * _ff_tile(N // 2)
    out_bytes = jnp.dtype(out_dtype).itemsize
    tk = K
    if K > MATMUL_SINGLE_K:
        for cap in (2048, 1024, 512):
            tk = _pick(K, cap, 128)
            blocks = 2 * 2 * tk * (tm + tn) + tm * tn * (2 * out_bytes + (4 if out_dtype != F32 else 0))
            if blocks <= MATMUL_VMEM_BUDGET:
                break
    nk = K // tk
    dims = {"nn": _NN, "nt": _NT, "tn": _TN}[mode]
    use_scratch = nk > 1 and out_dtype != F32

    def body(a_ref, b_ref, *refs):
        o_ref = refs[0]

        def product():
            return lax.dot_general(a_ref[...].astype(BF16), b_ref[...].astype(BF16), dims,
                                   preferred_element_type=F32)

        if nk == 1:
            part = product()
            o_ref[...] = part.astype(o_ref.dtype)
            if swiglu:
                g = part[:, :tn // 2]
                refs[1][...] = (g * _sigmoid(g) * part[:, tn // 2:]).astype(BF16)
            return
        k = pl.program_id(2)
        acc_ref = refs[-1] if use_scratch else o_ref

        @pl.when(k == 0)
        def _():
            acc_ref[...] = jnp.zeros_like(acc_ref)

        acc_ref[...] += product()

        if use_scratch:
            @pl.when(k == nk - 1)
            def _():
                o_ref[...] = acc_ref[...].astype(o_ref.dtype)

    if mode == "tn":
        a_spec = pl.BlockSpec((tk, tm), lambda i, j, k: (k, i))
    else:
        a_spec = pl.BlockSpec((tm, tk), lambda i, j, k: (i, k))
    if mode == "nt":
        b_spec = pl.BlockSpec((tn, tk), lambda i, j, k: (j, k))
    else:
        b_spec = pl.BlockSpec((tk, tn), lambda i, j, k: (k, j))
    out_specs = [pl.BlockSpec((tm, tn), lambda i, j, k: (i, j))]
    out_shape = [jax.ShapeDtypeStruct((M, N), out_dtype)]
    if swiglu:
        assert nk == 1 and mode == "nn"
        out_specs.append(pl.BlockSpec((tm, tn // 2), lambda i, j, k: (i, j)))
        out_shape.append(jax.ShapeDtypeStruct((M, N // 2), BF16))
    out = pl.pallas_call(
        body, name=name, grid=(M // tm, N // tn, nk),
        in_specs=[a_spec, b_spec], out_specs=out_specs, out_shape=out_shape,
        scratch_shapes=[pltpu.VMEM((tm, tn), F32)] if use_scratch else [],
        compiler_params=_params(("parallel", "parallel", "arbitrary")),
    )(a, b)
    return out if swiglu else out[0]


def _row_tile(S):
    return _pick(S, 512, 8)


def _norm_mod_fwd(x, gain, sc, sh):
    NB, S, D = x.shape
    tr = _row_tile(S)

    def body(x_ref, g_ref, sc_ref, sh_ref, h_ref):
        xv = x_ref[...]
        ms = jnp.mean(xv * xv, axis=-1, keepdims=True)
        n = xv * lax.rsqrt(ms + EPS) * g_ref[...]
        h_ref[...] = (n * (1.0 + sc_ref[...]) + sh_ref[...]).astype(BF16)

    tok = pl.BlockSpec((None, tr, D), lambda b, r: (b, r, 0))
    per_ex = pl.BlockSpec((None, 1, D), lambda b, r: (b, 0, 0))
    return pl.pallas_call(
        body, name="norm_mod_fwd", grid=(NB, S // tr),
        in_specs=[tok, pl.BlockSpec((1, D), lambda b, r: (0, 0)), per_ex, per_ex],
        out_specs=tok, out_shape=jax.ShapeDtypeStruct((NB, S, D), BF16),
        compiler_params=_params(("parallel", "parallel")),
    )(x, gain, sc, sh)


def _norm_mod_bwd(x, gain, sc, dh, dres):
    NB, S, D = x.shape
    tr = _row_tile(S)

    def body(x_ref, g_ref, sc_ref, dh_ref, dres_ref, dx_ref, dsh_ref, dsc_ref, dg_ref):
        b, r = pl.program_id(0), pl.program_id(1)

        @pl.when(r == 0)
        def _():
            dsh_ref[...] = jnp.zeros_like(dsh_ref)
            dsc_ref[...] = jnp.zeros_like(dsc_ref)

        @pl.when((r == 0) & (b == 0))
        def _():
            dg_ref[...] = jnp.zeros_like(dg_ref)

        xv = x_ref[...]
        rstd = lax.rsqrt(jnp.mean(xv * xv, axis=-1, keepdims=True) + EPS)
        xh = xv * rstd
        g = g_ref[...]
        dh = dh_ref[...]
        dsh_ref[...] += jnp.sum(dh, axis=0, keepdims=True)
        dsc_ref[...] += jnp.sum(dh * (xh * g), axis=0, keepdims=True)
        dn = dh * (1.0 + sc_ref[...])
        dg_ref[...] += jnp.sum(dn * xh, axis=0, keepdims=True)
        dxh = dn * g
        proj = jnp.mean(dxh * xh, axis=-1, keepdims=True)
        dx_ref[...] = rstd * (dxh - xh * proj) + dres_ref[...]

    tok = pl.BlockSpec((None, tr, D), lambda b, r: (b, r, 0))
    per_ex = pl.BlockSpec((None, 1, D), lambda b, r: (b, 0, 0))
    row = pl.BlockSpec((1, D), lambda b, r: (0, 0))
    return pl.pallas_call(
        body, name="norm_mod_bwd", grid=(NB, S // tr),
        in_specs=[tok, row, per_ex, tok, tok],
        out_specs=[tok, per_ex, per_ex, row],
        out_shape=[jax.ShapeDtypeStruct((NB, S, D), F32), jax.ShapeDtypeStruct((NB, 1, D), F32),
                   jax.ShapeDtypeStruct((NB, 1, D), F32), jax.ShapeDtypeStruct((1, D), F32)],
        compiler_params=_params(("arbitrary", "arbitrary")),
    )(x, gain, sc, dh, dres)


def _gate_res(x, y, g):
    NB, S, D = x.shape
    tr = _row_tile(S)

    def body(x_ref, y_ref, g_ref, o_ref):
        o_ref[...] = x_ref[...] + g_ref[...] * y_ref[...]

    tok = pl.BlockSpec((None, tr, D), lambda b, r: (b, r, 0))
    per_ex = pl.BlockSpec((None, 1, D), lambda b, r: (b, 0, 0))
    return pl.pallas_call(
        body, name="gate_res", grid=(NB, S // tr), in_specs=[tok, tok, per_ex], out_specs=tok,
        out_shape=jax.ShapeDtypeStruct((NB, S, D), F32),
        compiler_params=_params(("parallel", "parallel")),
    )(x, y, g)


def _gate_res_bwd(dxo, y, g):
    NB, S, D = dxo.shape
    tr = _row_tile(S)

    def body(d_ref, y_ref, g_ref, dy_ref, dg_ref):
        @pl.when(pl.program_id(1) == 0)
        def _():
            dg_ref[...] = jnp.zeros_like(dg_ref)

        d = d_ref[...]
        dy_ref[...] = (d * g_ref[...]).astype(BF16)
        dg_ref[...] += jnp.sum(d * y_ref[...], axis=0, keepdims=True)

    tok = pl.BlockSpec((None, tr, D), lambda b, r: (b, r, 0))
    per_ex = pl.BlockSpec((None, 1, D), lambda b, r: (b, 0, 0))
    return pl.pallas_call(
        body, name="gate_res_bwd", grid=(NB, S // tr), in_specs=[tok, tok, per_ex], out_specs=[tok, per_ex],
        out_shape=[jax.ShapeDtypeStruct((NB, S, D), BF16), jax.ShapeDtypeStruct((NB, 1, D), F32)],
        compiler_params=_params(("arbitrary", "arbitrary")),
    )(dxo, y, g)


def _sigmoid(v):
    return 1.0 / (1.0 + jnp.exp(-v))


def _ff_tile(F):
    return _pick(F, 1536, 128)


def _interleave(gate, up):
    F = gate.shape[-1]
    tf = _ff_tile(F)
    parts = []
    for j in range(F // tf):
        parts += [gate[..., j * tf:(j + 1) * tf], up[..., j * tf:(j + 1) * tf]]
    return jnp.concatenate(parts, axis=-1)


def _deinterleave(gu):
    F = gu.shape[-1] // 2
    tf = _ff_tile(F)
    gate = [gu[..., 2 * j * tf:(2 * j + 1) * tf] for j in range(F // tf)]
    up = [gu[..., (2 * j + 1) * tf:(2 * j + 2) * tf] for j in range(F // tf)]
    return jnp.concatenate(gate, axis=-1), jnp.concatenate(up, axis=-1)


def _swiglu_bwd(dm, wd, gu):
    T, D = dm.shape
    F = wd.shape[0]
    tf = _ff_tile(F)
    tm = _pick(T, 512, 128)
    assert D <= MATMUL_SINGLE_K

    def body(a_ref, b_ref, gu_ref, o_ref):
        d = lax.dot_general(a_ref[...], b_ref[...], _NT, preferred_element_type=F32)
        g, u = gu_ref[:, :tf].astype(F32), gu_ref[:, tf:].astype(F32)
        s = _sigmoid(g)
        o_ref[:, :tf] = (d * u * (s * (1.0 + g * (1.0 - s)))).astype(BF16)
        o_ref[:, tf:] = (d * (g * s)).astype(BF16)

    return pl.pallas_call(
        body, name="swiglu_bwd", grid=(T // tm, F // tf),
        in_specs=[pl.BlockSpec((tm, D), lambda i, j: (i, 0)), pl.BlockSpec((tf, D), lambda i, j: (j, 0)),
                  pl.BlockSpec((tm, 2 * tf), lambda i, j: (i, j))],
        out_specs=pl.BlockSpec((tm, 2 * tf), lambda i, j: (i, j)),
        out_shape=jax.ShapeDtypeStruct((T, 2 * F), BF16),
        compiler_params=_params(("parallel", "parallel")),
    )(dm, wd, gu)


def _loss_fwd_bwd(y, target):
    NB, S, D = y.shape
    tr = _row_tile(S)

    def body(y_ref, t_ref, l_ref, d_ref):
        @pl.when((pl.program_id(0) == 0) & (pl.program_id(1) == 0))
        def _():
            l_ref[...] = jnp.zeros_like(l_ref)

        e = y_ref[...] - t_ref[...]
        d_ref[...] = e / D
        l_ref[...] += 0.5 * jnp.sum(jnp.mean(e * e, axis=-1, keepdims=True), axis=0, keepdims=True)

    tok = pl.BlockSpec((None, tr, D), lambda b, r: (b, r, 0))
    return pl.pallas_call(
        body, name="loss", grid=(NB, S // tr), in_specs=[tok, tok],
        out_specs=[pl.BlockSpec((1, 1), lambda b, r: (0, 0)), tok],
        out_shape=[jax.ShapeDtypeStruct((1, 1), F32), jax.ShapeDtypeStruct((NB, S, D), F32)],
        compiler_params=_params(("arbitrary", "arbitrary")),
    )(y, target)


def _half_sums(v, lo):
    sa = jnp.sum(jnp.where(lo, v, 0.0), axis=-1, keepdims=True)
    sb = jnp.sum(jnp.where(lo, 0.0, v), axis=-1, keepdims=True)
    return jnp.where(lo, sa, sb)


def _rope_swap(v, lane64):
    up = pltpu.roll(v, LANES - ROT_DIM // 2, 1)
    down = pltpu.roll(v, ROT_DIM // 2, 1)
    return jnp.where(lane64 < ROT_DIM // 2, up, jnp.where(lane64 < ROT_DIM, down, 0.0))


def _qk_prep_fwd(qkv, tab_c, tab_s, gains):
    T, W = qkv.shape
    R = W // LANES
    tt = _pick(T, 256, 8)

    def body(x_ref, c_ref, s_ref, g_ref, o_ref):
        lane = lax.broadcasted_iota(jnp.int32, (tt, LANES), 1)
        lo = lane < HEAD_DIM
        lane64 = lane & (HEAD_DIM - 1)
        c, s = c_ref[...], s_ref[...]
        for j in range(R - 1):
            cols = slice(j * LANES, (j + 1) * LANES)
            xv = x_ref[:, cols]
            rstd = lax.rsqrt(_half_sums(xv * xv, lo) / HEAD_DIM + EPS)
            yn = xv * rstd * g_ref[j:j + 1, :]
            o_ref[:, cols] = (yn * c + _rope_swap(yn, lane64) * s).astype(BF16)
        o_ref[:, (R - 1) * LANES:] = x_ref[:, (R - 1) * LANES:].astype(BF16)

    tok = pl.BlockSpec((tt, W), lambda t: (t, 0))
    tab = pl.BlockSpec((tt, LANES), lambda t: (t, 0))
    return pl.pallas_call(
        body, name="qk_prep_fwd", grid=(T // tt,),
        in_specs=[tok, tab, tab, pl.BlockSpec((R, LANES), lambda t: (0, 0))],
        out_specs=tok, out_shape=jax.ShapeDtypeStruct((T, W), BF16),
        compiler_params=_params(("parallel",)),
    )(qkv, tab_c, tab_s, gains)


def _qk_prep_bwd(qkv, dq, dk, dv, tab_c, tab_s, gains):
    T, W = qkv.shape
    R = W // LANES
    QW = dq.shape[1]
    tt = _pick(T, 256, 8)

    def body(x_ref, dq_ref, dk_ref, dv_ref, c_ref, s_ref, g_ref, o_ref, dg_ref):
        @pl.when(pl.program_id(0) == 0)
        def _():
            dg_ref[...] = jnp.zeros_like(dg_ref)

        lane = lax.broadcasted_iota(jnp.int32, (tt, LANES), 1)
        lo = lane < HEAD_DIM
        lane64 = lane & (HEAD_DIM - 1)
        c, s = c_ref[...], s_ref[...]
        for j in range(R - 1):
            cols = slice(j * LANES, (j + 1) * LANES)
            xv = x_ref[:, cols]
            d = dq_ref[:, cols] if j < R - 2 else dk_ref[...]
            rstd = lax.rsqrt(_half_sums(xv * xv, lo) / HEAD_DIM + EPS)
            xh = xv * rstd
            dyn = d * c + _rope_swap(d * s, lane64)
            dg_ref[j:j + 1, :] += jnp.sum(dyn * xh, axis=0, keepdims=True)
            dxh = dyn * g_ref[j:j + 1, :]
            proj = _half_sums(dxh * xh, lo) / HEAD_DIM
            o_ref[:, cols] = (rstd * (dxh - xh * proj)).astype(BF16)
        o_ref[:, (R - 1) * LANES:] = dv_ref[...].astype(BF16)

    tok = pl.BlockSpec((tt, W), lambda t: (t, 0))
    tab = pl.BlockSpec((tt, LANES), lambda t: (t, 0))
    gsp = pl.BlockSpec((R, LANES), lambda t: (0, 0))
    return pl.pallas_call(
        body, name="qk_prep_bwd", grid=(T // tt,),
        in_specs=[tok, pl.BlockSpec((tt, QW), lambda t: (t, 0)), tab, tab, tab, tab, gsp], out_specs=[tok, gsp],
        out_shape=[jax.ShapeDtypeStruct((T, W), BF16), jax.ShapeDtypeStruct((R, LANES), F32)],
        compiler_params=_params(("arbitrary",)),
    )(qkv, dq, dk, dv, tab_c, tab_s, gains)


def _band_mask(i):
    r = lax.broadcasted_iota(jnp.int32, (2 * BLOCK, 2 * BLOCK), 0) & (BLOCK - 1)
    c = lax.broadcasted_iota(jnp.int32, (2 * BLOCK, 2 * BLOCK), 1)
    rel = r + BLOCK - c
    return (rel >= 0) & (rel < BLOCK) & ((c >= BLOCK) | (i > 0))


def _swa_softmax(s, valid, sink):
    s = jnp.where(valid, s * ATTN_SCALE, NEG_BIG)
    m = jnp.maximum(jnp.max(s, axis=1, keepdims=True), sink)
    p = jnp.exp(s - m)
    ps = jnp.exp(sink - m)
    denom = jnp.sum(p, axis=1, keepdims=True) + ps
    return p / denom, ps / denom


A_GROUP = 4


Q_WIDTH_A = N_Q_A * HEAD_DIM
N_PAIR_A = Q_WIDTH_A // LANES


def _swa_specs():
    qs = pl.BlockSpec((None, BLOCK, Q_WIDTH_A), lambda b, i: (b, i, 0))

    def kv(col, back):
        return pl.BlockSpec((None, BLOCK, LANES), lambda b, i: (b, jnp.maximum(i - back, 0), col))

    return qs, kv(N_PAIR_A, 1), kv(N_PAIR_A, 0), kv(N_PAIR_A + 1, 1), kv(N_PAIR_A + 1, 0)


def _dup_heads(t):
    lo = lax.broadcasted_iota(jnp.int32, t.shape, 1) < HEAD_DIM
    sw = pltpu.roll(t.astype(F32), HEAD_DIM, 1).astype(BF16)
    return jnp.where(lo, t, sw), jnp.where(lo, sw, t)


def _kv_tiles(kp_ref, kc_ref, vp_ref, vc_ref):
    kd = _dup_heads(jnp.concatenate([kp_ref[...], kc_ref[...]], axis=0))
    vd = _dup_heads(jnp.concatenate([vp_ref[...], vc_ref[...]], axis=0))
    return kd, vd


def _attn_a_fwd(qkn, sinks):
    NB, S, _ = qkn.shape
    qs, kp, kc, vp, vc = _swa_specs()

    def body(q_ref, kp_ref, kc_ref, vp_ref, vc_ref, sink_ref, o_ref):
        i = pl.program_id(1)
        kd, vd = _kv_tiles(kp_ref, kc_ref, vp_ref, vc_ref)
        valid = _band_mask(i)
        lo = lax.broadcasted_iota(jnp.int32, (BLOCK, LANES), 1) < HEAD_DIM
        top = lax.broadcasted_iota(jnp.int32, (2 * BLOCK, 1), 0) < BLOCK
        for first in range(0, N_PAIR_A, A_GROUP):
            pairs = range(first, first + A_GROUP)
            qs_ = [jnp.concatenate(_head_halves(q_ref[:, p * LANES:(p + 1) * LANES], lo), axis=0) for p in pairs]
            ss = [lax.dot_general(q, kd[2 * p // GROUP_A], _NT, preferred_element_type=F32) for q, p in zip(qs_, pairs)]
            pns = [_swa_softmax(s, valid, jnp.where(top, sink_ref[2 * p], sink_ref[2 * p + 1]))[0]
                   for s, p in zip(ss, pairs)]
            pvs = [jnp.dot(pn.astype(BF16), vd[2 * p // GROUP_A], preferred_element_type=F32) for pn, p in zip(pns, pairs)]
            for pv, p in zip(pvs, pairs):
                o_ref[:, p * LANES:(p + 1) * LANES] = jnp.where(lo, pv[:BLOCK], pv[BLOCK:]).astype(BF16)

    return pl.pallas_call(
        body, name="attn_a_fwd", grid=(NB, S // BLOCK),
        in_specs=[qs, kp, kc, vp, vc, pl.BlockSpec(memory_space=pltpu.SMEM)],
        out_specs=qs, out_shape=jax.ShapeDtypeStruct((NB, S, Q_WIDTH_A), BF16),
        compiler_params=_params(("parallel", "arbitrary")),
    )(qkn, qkn, qkn, qkn, qkn, sinks)


def _attn_a_bwd(qkn, do, sinks):
    NB, S, _ = qkn.shape
    qs, kp, kc, vp, vc = _swa_specs()
    full = pl.BlockSpec((None, S, LANES), lambda b, i: (b, 0, 0))
    sink_out = pl.BlockSpec((None, N_Q_A, LANES), lambda b, i: (b, 0, 0))

    def body(q_ref, do_ref, kp_ref, kc_ref, vp_ref, vc_ref, sink_ref, dq_ref, dk_ref, dv_ref, ds_ref, dk_s, dv_s):
        i = pl.program_id(1)

        @pl.when(i == 0)
        def _():
            dk_ref[...] = jnp.zeros_like(dk_ref)
            dv_ref[...] = jnp.zeros_like(dv_ref)
            ds_ref[...] = jnp.zeros_like(ds_ref)

        dk_s[...] = jnp.zeros_like(dk_s)
        dv_s[...] = jnp.zeros_like(dv_s)
        kd, vd = _kv_tiles(kp_ref, kc_ref, vp_ref, vc_ref)
        valid = _band_mask(i)
        lo = lax.broadcasted_iota(jnp.int32, (BLOCK, LANES), 1) < HEAD_DIM
        top = lax.broadcasted_iota(jnp.int32, (2 * BLOCK, 1), 0) < BLOCK
        for first in range(0, N_PAIR_A, A_GROUP):
            pairs = range(first, first + A_GROUP)
            kvs = [2 * p // GROUP_A for p in pairs]
            qs_ = [jnp.concatenate(_head_halves(q_ref[:, p * LANES:(p + 1) * LANES], lo), axis=0) for p in pairs]
            dos = [jnp.concatenate(_head_halves(do_ref[:, p * LANES:(p + 1) * LANES], lo), axis=0) for p in pairs]
            ss = [lax.dot_general(q, kd[kv], _NT, preferred_element_type=F32) for q, kv in zip(qs_, kvs)]
            dps = [lax.dot_general(d, vd[kv], _NT, preferred_element_type=F32) for d, kv in zip(dos, kvs)]
            sm = [_swa_softmax(s, valid, jnp.where(top, sink_ref[2 * p], sink_ref[2 * p + 1])) for s, p in zip(ss, pairs)]
            deltas = [jnp.sum(pn * dp, axis=1, keepdims=True) for (pn, _), dp in zip(sm, dps)]
            dsbs = [(pn * (dp - delta) * ATTN_SCALE).astype(BF16) for (pn, _), dp, delta in zip(sm, dps, deltas)]
            for n, p in enumerate(pairs):
                dq2 = jnp.dot(dsbs[n], kd[kvs[n]], preferred_element_type=F32)
                dq_ref[:, p * LANES:(p + 1) * LANES] = jnp.where(lo, dq2[:BLOCK], dq2[BLOCK:])
                dk_s[kvs[n]] += lax.dot_general(dsbs[n], qs_[n], _TN, preferred_element_type=F32)
                dv_s[kvs[n]] += lax.dot_general(sm[n][0].astype(BF16), dos[n], _TN, preferred_element_type=F32)
                t = sm[n][1] * deltas[n]
                for hh in range(2):
                    dsink = -jnp.sum(t[hh * BLOCK:(hh + 1) * BLOCK], axis=0, keepdims=True)
                    ds_ref[2 * p + hh:2 * p + hh + 1, :] += jnp.broadcast_to(dsink, (1, LANES))

        lo2 = lax.broadcasted_iota(jnp.int32, (2 * BLOCK, LANES), 1) < HEAD_DIM

        def fold(acc):
            halves = [acc[kv] + pltpu.roll(acc[kv], HEAD_DIM, 1) for kv in range(N_KV_A)]
            return jnp.where(lo2, halves[0], halves[1])

        dk2, dv2 = fold(dk_s), fold(dv_s)

        @pl.when(i > 0)
        def _():
            start = pl.multiple_of((i - 1) * BLOCK, BLOCK)
            dk_ref[pl.ds(start, 2 * BLOCK), :] += dk2
            dv_ref[pl.ds(start, 2 * BLOCK), :] += dv2

        @pl.when(i == 0)
        def _():
            dk_ref[0:BLOCK, :] += dk2[BLOCK:, :]
            dv_ref[0:BLOCK, :] += dv2[BLOCK:, :]

    slots = pltpu.VMEM((N_KV_A, 2 * BLOCK, LANES), F32)
    return pl.pallas_call(
        body, name="attn_a_bwd", grid=(NB, S // BLOCK),
        in_specs=[qs, qs, kp, kc, vp, vc, pl.BlockSpec(memory_space=pltpu.SMEM)],
        out_specs=[qs, full, full, sink_out],
        out_shape=[jax.ShapeDtypeStruct((NB, S, Q_WIDTH_A), F32), jax.ShapeDtypeStruct((NB, S, LANES), F32),
                   jax.ShapeDtypeStruct((NB, S, LANES), F32), jax.ShapeDtypeStruct((NB, N_Q_A, LANES), F32)],
        scratch_shapes=[slots, slots],
        compiler_params=_params(("parallel", "arbitrary")),
    )(qkn, do, qkn, qkn, qkn, qkn, sinks)


def _cumsum_mats():
    src = lax.broadcasted_iota(jnp.int32, (2 * BLOCK, 2 * BLOCK), 0) % BLOCK
    dst = lax.broadcasted_iota(jnp.int32, (2 * BLOCK, 2 * BLOCK), 1)
    ones = dst >= BLOCK
    rev = ((src > dst) | ones).astype(BF16)
    fwd = ((src < dst) | ones).astype(BF16)
    return rev, fwd


def _log_sigmoids(z):
    sp = jnp.log(1.0 + jnp.exp(-jnp.abs(z)))
    return jnp.minimum(z, 0.0) - sp, -(jnp.maximum(z, 0.0) + sp)


def _cumsum_mxu_many(vs, mat):
    parts = []
    for v in vs:
        hi = v.astype(BF16)
        parts.append(jnp.concatenate([hi, (v - hi.astype(F32)).astype(BF16)], axis=1))
    r = jnp.dot(jnp.concatenate(parts, axis=0), mat, preferred_element_type=F32)
    return [(r[n * BLOCK:(n + 1) * BLOCK, :BLOCK], r[n * BLOCK:(n + 1) * BLOCK, BLOCK:]) for n in range(len(vs))]


def _strict_mask():
    r = lax.broadcasted_iota(jnp.int32, (BLOCK, BLOCK), 0)
    c = lax.broadcasted_iota(jnp.int32, (BLOCK, BLOCK), 1)
    return c < r


def _tile(ref, j):
    return ref[pl.ds(pl.multiple_of(j * BLOCK, BLOCK), BLOCK), :]


SWEEP_EXIT = -88.0


def _head_halves(t, lo):
    zero = jnp.zeros_like(t)
    return jnp.where(lo, t, zero), jnp.where(lo, zero, t)


def _sb_specs(S, HD, width):
    n = HD // width
    blk = pl.BlockSpec((None, BLOCK, width), lambda b, p, i: (b, i, p))
    k_full = pl.BlockSpec((None, S, width), lambda b, p, i: (b, 0, n + p))
    v_full = pl.BlockSpec((None, S, width), lambda b, p, i: (b, 0, 2 * n + p))
    mat = pl.BlockSpec((2 * BLOCK, 2 * BLOCK), lambda b, p, i: (0, 0))
    return blk, k_full, v_full, mat


SB_FWD_PAIRS = 4
SB_BWD_PAIRS = 2
SB_BWD_TILES = 2
SB_BWD_VMEM_LIMIT_BYTES = 58 * 1024 * 1024


def _attn_b_fwd(qkv, rev):
    NB, S, W = qkv.shape
    HD = W // 3
    width = SB_FWD_PAIRS * LANES
    n_heads = 2 * SB_FWD_PAIRS
    blk, k_full, v_full, mat = _sb_specs(S, HD, width)

    def body(q_ref, k_ref, v_ref, rev_ref, o_ref):
        i = pl.program_id(2)
        rv = rev_ref[...]
        mask = _strict_mask()
        lo = lax.broadcasted_iota(jnp.int32, (BLOCK, LANES), 1) < HEAD_DIM
        q_all = q_ref[...]
        q_stack = [jnp.concatenate(_head_halves(q_all[:, p * LANES:(p + 1) * LANES] * ATTN_SCALE, lo), axis=0)
                   for p in range(SB_FWD_PAIRS)]

        def pair_tiles(ref, j):
            t = _tile(ref, j)
            return [t[:, p * LANES:(p + 1) * LANES] for p in range(SB_FWD_PAIRS)]

        def tile_pass(j, carries, diagonal):
            ks, vs = pair_tiles(k_ref, j), pair_tiles(v_ref, j)
            zs = []
            for p in range(SB_FWD_PAIRS):
                z2 = lax.dot_general(q_stack[p], ks[p], _NT, preferred_element_type=F32)
                zs += [z2[:BLOCK], z2[BLOCK:]]
            logs = [_log_sigmoids(z) for z in zs]
            cums = _cumsum_mxu_many([jnp.where(mask, lm, 0.0) if diagonal else lm for _, lm in logs], rv)
            probs, new_c = [], []
            for h in range(n_heads):
                after, rs = cums[h]
                if diagonal:
                    a = jnp.where(mask, jnp.exp(logs[h][0] + after), 0.0)
                    new_c.append(rs)
                else:
                    a = jnp.exp(logs[h][0] + after + carries[h])
                    new_c.append(carries[h] + rs)
                probs.append(a.astype(BF16))
            outs = []
            for p in range(SB_FWD_PAIRS):
                pv = jnp.dot(jnp.concatenate(probs[2 * p:2 * p + 2], axis=0), vs[p], preferred_element_type=F32)
                outs.append(jnp.where(lo, pv[:BLOCK], pv[BLOCK:]))
            return new_c, outs

        carries, accs = tile_pass(i, None, True)

        def live(cs):
            top = cs[0]
            for c in cs[1:]:
                top = jnp.maximum(top, c)
            return jnp.max(top) > SWEEP_EXIT

        def cond(st):
            return (st[0] < i) & st[1]

        def step(st):
            jj, _, cs, accs = st
            new_c, outs = tile_pass(i - 1 - jj, cs, False)
            return jj + 1, live(new_c), new_c, [acc + o for acc, o in zip(accs, outs)]

        st = lax.while_loop(cond, step, (jnp.int32(0), live(carries), carries, accs))
        for p in range(SB_FWD_PAIRS):
            o_ref[:, p * LANES:(p + 1) * LANES] = st[3][p].astype(BF16)

    return pl.pallas_call(
        body, name="attn_b_fwd", grid=(NB, HD // width, S // BLOCK),
        in_specs=[blk, k_full, v_full, mat], out_specs=blk,
        out_shape=jax.ShapeDtypeStruct((NB, S, HD), BF16),
        compiler_params=_params(("parallel", "parallel", "arbitrary")),
    )(qkv, qkv, qkv, rev)


def _attn_b_bwd(qkv, do, rev, fwd):
    NB, S, W = qkv.shape
    HD = W // 3
    width = SB_BWD_PAIRS * LANES
    n_heads = 2 * SB_BWD_PAIRS
    nj = S // BLOCK
    blk, k_full, v_full, mat = _sb_specs(S, HD, width)
    acc_full = pl.BlockSpec((None, S, width), lambda b, p, i: (b, 0, p))

    def body(q_ref, do_ref, k_ref, v_ref, rev_ref, fwd_ref, dq_ref, dk_ref, dv_ref, sig_s, a_s, e_s):
        i = pl.program_id(2)

        @pl.when(i == 0)
        def _():
            dk_ref[...] = jnp.zeros_like(dk_ref)
            dv_ref[...] = jnp.zeros_like(dv_ref)

        rv, fw = rev_ref[...], fwd_ref[...]
        mask = _strict_mask()
        lo = lax.broadcasted_iota(jnp.int32, (BLOCK, LANES), 1) < HEAD_DIM
        pairs = range(SB_BWD_PAIRS)

        def cols(p):
            return slice(p * LANES, (p + 1) * LANES)

        q_stack = [jnp.concatenate(_head_halves(q_ref[:, cols(p)], lo), axis=0) for p in pairs]
        qs_stack = [q * ATTN_SCALE for q in q_stack]
        do_stack = [jnp.concatenate(_head_halves(do_ref[:, cols(p)], lo), axis=0) for p in pairs]

        def sweep1_tiles(js, carries, diagonal):
            zs, das = [], []
            for j in js:
                kj, vj = _tile(k_ref, j), _tile(v_ref, j)
                for p in pairs:
                    z2 = lax.dot_general(qs_stack[p], kj[:, cols(p)], _NT, preferred_element_type=F32)
                    da2 = lax.dot_general(do_stack[p], vj[:, cols(p)], _NT, preferred_element_type=F32)
                    zs += [z2[:BLOCK], z2[BLOCK:]]
                    das += [da2[:BLOCK], da2[BLOCK:]]
            logs = [_log_sigmoids(z) for z in zs]
            cums = _cumsum_mxu_many([jnp.where(mask, lm, 0.0) if diagonal else lm for _, lm in logs], rv)
            new_c, stores = [], []
            for h in range(n_heads):
                carry = None if diagonal else carries[h]
                for t, j in enumerate(js):
                    n = t * n_heads + h
                    lb, (after, rs) = logs[n][0], cums[n]
                    if diagonal:
                        a = jnp.where(mask, jnp.exp(lb + after), 0.0)
                        carry = rs
                    else:
                        a = jnp.exp(lb + after + carry)
                        carry = carry + rs
                    stores.append((t, h, j, jnp.exp(lb), a.astype(BF16), das[n] * a))
                new_c.append(carry)
            for t, h, j, sg, ab, e in sorted(stores, key=lambda s: -s[0]):
                sig_s[h, j] = sg
                a_s[h, j] = ab
                e_s[h, j] = e
            return new_c

        carries = sweep1_tiles([i], None, True)

        def live(cs):
            top = cs[0]
            for c in cs[1:]:
                top = jnp.maximum(top, c)
            return jnp.max(top) > SWEEP_EXIT

        def cond(st):
            return (SB_BWD_TILES * st[0] < i) & st[1]

        def sweep1(st):
            first = i - 1 - SB_BWD_TILES * st[0]
            new_c = sweep1_tiles([jnp.maximum(first - t, 0) for t in range(SB_BWD_TILES)], st[2], False)
            return st[0] + 1, live(new_c), new_c

        trips = lax.while_loop(cond, sweep1, (jnp.int32(0), live(carries), carries))[0]
        lowest = jnp.maximum(i - SB_BWD_TILES * trips, 0)

        def grads(js, st, diagonal):
            prefixes, dqs = st
            es = [e_s[h, j] for j in js for h in range(n_heads)]
            cums = _cumsum_mxu_many(es, fw)
            dzs, new_p = [], []
            for h in range(n_heads):
                prefix = prefixes[h]
                for t, j in enumerate(js):
                    n = t * n_heads + h
                    sg = sig_s[h, j]
                    e_before, rs = cums[n]
                    dz = (es[n] * (1.0 - sg) - (e_before + prefix) * sg) * ATTN_SCALE
                    if diagonal:
                        dz = jnp.where(mask, dz, 0.0)
                    dzs.append((t, h, dz.astype(BF16)))
                    prefix = prefix + rs
                new_p.append(prefix)
            dz_of = {(t, h): dz for t, h, dz in dzs}
            new_dq = list(dqs)
            for t, j in enumerate(js):
                kj = _tile(k_ref, j)
                rows = pl.ds(pl.multiple_of(j * BLOCK, BLOCK), BLOCK)
                for p in pairs:
                    dz_stack = jnp.concatenate([dz_of[t, 2 * p], dz_of[t, 2 * p + 1]], axis=0)
                    a_stack = jnp.concatenate([a_s[2 * p, j], a_s[2 * p + 1, j]], axis=0)
                    dq2 = jnp.dot(dz_stack, kj[:, cols(p)], preferred_element_type=F32)
                    new_dq[p] = new_dq[p] + jnp.where(lo, dq2[:BLOCK], dq2[BLOCK:])
                    dk_ref[rows, cols(p)] += lax.dot_general(dz_stack, q_stack[p], _TN, preferred_element_type=F32)
                    dv_ref[rows, cols(p)] += lax.dot_general(a_stack, do_stack[p], _TN, preferred_element_type=F32)
            return new_p, new_dq

        zeros = jnp.zeros((BLOCK, BLOCK), F32)
        st = ([zeros] * n_heads, [zeros] * SB_BWD_PAIRS)
        count = i - lowest
        st = lax.fori_loop(0, count % SB_BWD_TILES, lambda t, st: grads([lowest + t], st, False), st)
        start = lowest + count % SB_BWD_TILES
        st = lax.fori_loop(0, count // SB_BWD_TILES,
                           lambda t, st: grads([start + SB_BWD_TILES * t + u for u in range(SB_BWD_TILES)], st, False), st)
        dqs = grads([i], st, True)[1]
        for p in pairs:
            dq_ref[:, cols(p)] = dqs[p]

    f32_stash = pltpu.VMEM((n_heads, nj, BLOCK, BLOCK), F32)
    bf16_stash = pltpu.VMEM((n_heads, nj, BLOCK, BLOCK), BF16)
    return pl.pallas_call(
        body, name="attn_b_bwd", grid=(NB, HD // width, nj),
        in_specs=[blk, blk, k_full, v_full, mat, mat], out_specs=[blk, acc_full, acc_full],
        out_shape=[jax.ShapeDtypeStruct((NB, S, HD), F32)] * 3,
        scratch_shapes=[f32_stash, bf16_stash, f32_stash],
        compiler_params=_params(("parallel", "parallel", "arbitrary"), SB_BWD_VMEM_LIMIT_BYTES),
    )(qkv, do, qkv, qkv, rev, fwd)


def _ada_fwd(c_all, w, b):
    L, D, N = w.shape
    B = c_all.shape[0]

    def body(c_ref, w_ref, b_ref, o_ref):
        cv = c_ref[...]
        cond = (cv * _sigmoid(cv)).astype(BF16)
        o_ref[...] = jnp.dot(cond, w_ref[...].astype(BF16), preferred_element_type=F32) + b_ref[...]

    return pl.pallas_call(
        body, name="ada_fwd", grid=(L,),
        in_specs=[pl.BlockSpec((B, D), lambda l: (0, 0)), pl.BlockSpec((None, D, N), lambda l: (l, 0, 0)),
                  pl.BlockSpec((None, 1, N), lambda l: (l, 0, 0))],
        out_specs=pl.BlockSpec((None, B, N), lambda l: (l, 0, 0)),
        out_shape=jax.ShapeDtypeStruct((L, B, N), F32),
        compiler_params=_params(("parallel",)),
    )(c_all, w, b)


def _ada_bwd(c_all, dmod_all, dmod_shard):
    L, B, N = dmod_shard.shape
    D = c_all.shape[1]
    N_all = dmod_all.shape[2]

    def body(c_ref, da_ref, ds_ref, gw_ref, gb_ref):
        cv = c_ref[...]
        cond = (cv * _sigmoid(cv)).astype(BF16)
        gw_ref[...] = lax.dot_general(cond, ds_ref[...].astype(BF16), _TN, preferred_element_type=F32)
        gb_ref[...] = jnp.sum(da_ref[...], axis=0, keepdims=True)

    return pl.pallas_call(
        body, name="ada_bwd", grid=(L,),
        in_specs=[pl.BlockSpec((B, D), lambda l: (0, 0)), pl.BlockSpec((None, B, N_all), lambda l: (l, 0, 0)),
                  pl.BlockSpec((None, B, N), lambda l: (l, 0, 0))],
        out_specs=[pl.BlockSpec((None, D, N), lambda l: (l, 0, 0)), pl.BlockSpec((None, 1, N_all), lambda l: (l, 0, 0))],
        out_shape=[jax.ShapeDtypeStruct((L, D, N), F32), jax.ShapeDtypeStruct((L, 1, N_all), F32)],
        compiler_params=_params(("parallel",)),
    )(c_all, dmod_all, dmod_shard)


def _adamw(w, g, m, v, name):
    shape = w.shape
    C = shape[-1]
    R = w.size // C
    tr = _pick(R, max(8, (1 << 18) // C), 8)
    c1 = 1.0 - ADAM_B1 ** ADAM_STEP
    c2 = 1.0 - ADAM_B2 ** ADAM_STEP

    def body(w_ref, g_ref, m_ref, v_ref, d_ref, nm_ref, nv_ref):
        gv = g_ref[...]
        nm = ADAM_B1 * m_ref[...] + (1.0 - ADAM_B1) * gv
        nv = ADAM_B2 * v_ref[...] + (1.0 - ADAM_B2) * (gv * gv)
        d_ref[...] = -ADAM_LR * ((nm / c1) / (jnp.sqrt(nv / c2) + ADAM_EPS) + ADAM_WD * w_ref[...])
        nm_ref[...] = nm
        nv_ref[...] = nv

    spec = pl.BlockSpec((tr, C), lambda r: (r, 0))
    out = pl.pallas_call(
        body, name=name, grid=(R // tr,), in_specs=[spec] * 4, out_specs=[spec] * 3,
        out_shape=[jax.ShapeDtypeStruct((R, C), F32)] * 3,
        compiler_params=_params(("parallel",)),
    )(*[t.reshape(R, C) for t in (w, g, m, v)])
    return [t.reshape(shape) for t in out]


_SHARDED = (("wqkv_a", 2), ("wo_a", 1), ("wqkv_b", 2), ("wo_b", 1), ("w_gate", 2), ("w_up", 2), ("w_down", 1))


def _pack_full(layers, axis, gate_up=None):
    L = len(layers)
    R, C = layers[0].shape

    def shards(m):
        if gate_up is not None:
            F = C // 2
            tf, Cs = _ff_tile(F), F // 4
            assert tf % Cs == 0
            starts = [(2 * (s * Cs // tf) + gate_up) * tf + s * Cs % tf for s in range(4)]
            return jnp.stack([m[:, st:st + Cs] for st in starts])
        if axis == 2:
            return m.reshape(R, 4, C // 4).transpose(1, 0, 2)
        return m.reshape(4, R // 4, C)

    halves = [jnp.stack([shards(m) for m in layers[h * (L // 2):(h + 1) * (L // 2)]], axis=1) for h in range(2)]
    return jnp.stack(halves)


def _unpack_full(gathered, axis):
    _, Lh, Rs, Cs = gathered.shape
    t = gathered.reshape(4, 2, Lh, Rs, Cs)
    layers = []
    for h in range(2):
        for l in range(Lh):
            piece = t[:, h, l]
            if axis == 2:
                layers.append(piece.transpose(1, 0, 2).reshape(Rs, 4 * Cs))
            else:
                layers.append(piece.reshape(4 * Rs, Cs))
    return layers


def _sum_slabs(own, recv, name, with_bf16=False):
    C = own.shape[-1]
    out = _sum_leading(recv.reshape(recv.shape[0], -1, C), name, own=own.reshape(-1, C), with_bf16=with_bf16)
    if with_bf16:
        return out[0].reshape(own.shape), out[1].reshape(own.shape)
    return out.reshape(own.shape)


def _gather8(x, name):
    return _all_gather8([x], name)[0]


def _rope_tables(positions):
    half = ROT_DIM // 2
    inv_freq = jnp.power(jnp.float32(ROPE_THETA), -jnp.arange(half, dtype=F32) * 2.0 / ROT_DIM)
    ang = positions.astype(F32).reshape(-1, 1) * inv_freq
    cos, sin = jnp.cos(ang), jnp.sin(ang)
    T = ang.shape[0]
    rest = HEAD_DIM - ROT_DIM
    c64 = jnp.concatenate([cos, cos, jnp.ones((T, rest), F32)], axis=1)
    s64 = jnp.concatenate([-sin, sin, jnp.zeros((T, rest), F32)], axis=1)
    return jnp.tile(c64, (1, 2)), jnp.tile(s64, (1, 2))


def _gain_rows(q_gain, k_gain):
    q2 = jnp.tile(q_gain.reshape(1, HEAD_DIM), (GROUP_A, 2))
    k2 = jnp.tile(k_gain.reshape(1, HEAD_DIM), (1, 2))
    return jnp.concatenate([q2, k2, jnp.ones((1, LANES), F32)], axis=0)


def _local_step(x, positions, mod, norm1_g, norm2_g, q_norm_a, k_norm_a, sinks_a,
                wqkv_a, wo_a, wqkv_b, wo_b, wgu, wd, loss_target):
    NB, S, D = x.shape
    T = NB * S
    QA = N_Q_A * HEAD_DIM
    tab_c, tab_s = _rope_tables(positions)
    rev, fwd = _cumsum_mats()

    saved = []
    xc = x
    for i in range(DEPTH):
        j = i // 2
        sh1, sc1, g1, sh2, sc2, g2 = [mod[i][:, k * D:(k + 1) * D].reshape(NB, 1, D) for k in range(6)]
        st = dict(x=xc, sc1=sc1, g1=g1, sc2=sc2, g2=g2)
        h = _norm_mod_fwd(xc, norm1_g[i:i + 1], sc1, sh1)
        st["h"] = h.reshape(T, D)
        if i % 2 == 0:
            st["qkv"] = _matmul(st["h"], wqkv_a[j], "nn", F32, "qkv_a")
            st["gains"] = _gain_rows(q_norm_a[j], k_norm_a[j])
            st["qkn"] = _qk_prep_fwd(st["qkv"], tab_c, tab_s, st["gains"]).reshape(NB, S, -1)
            st["o"] = _attn_a_fwd(st["qkn"], sinks_a[j]).reshape(T, QA)
            y = _matmul(st["o"], wo_a[j], "nn", F32, "wo_a")
        else:
            st["qkv"] = _matmul(st["h"], wqkv_b[j], "nn", BF16, "qkv_b").reshape(NB, S, -1)
            st["o"] = _attn_b_fwd(st["qkv"], rev).reshape(T, N_H_B * HEAD_DIM)
            y = _matmul(st["o"], wo_b[j], "nn", F32, "wo_b")
        st["y"] = y.reshape(NB, S, D)
        x1 = _gate_res(xc, st["y"], g1)
        st["x1"] = x1
        h2 = _norm_mod_fwd(x1, norm2_g[i:i + 1], sc2, sh2)
        st["h2"] = h2.reshape(T, D)
        st["gu"], st["act"] = _matmul(st["h2"], wgu[i], "nn", BF16, "gate_up", swiglu=True)
        st["m"] = _matmul(st["act"], wd[i], "nn", F32, "down").reshape(NB, S, D)
        xc = _gate_res(x1, st["m"], g2)
        saved.append(st)

    loss, dx = _loss_fwd_bwd(xc, loss_target)

    grads = {name: [None] * n for name, n in
             (("wqkv_a", 2), ("wo_a", 2), ("wqkv_b", 2), ("wo_b", 2), ("wgu", DEPTH), ("wd", DEPTH),
              ("norm1_g", DEPTH), ("norm2_g", DEPTH), ("q_norm_a", 2), ("k_norm_a", 2), ("sinks_a", 2))}
    dmod = [None] * DEPTH
    for i in reversed(range(DEPTH)):
        j = i // 2
        st = saved[i]
        dm, dg2 = _gate_res_bwd(dx, st["m"], st["g2"])
        dm = dm.reshape(T, D)
        grads["wd"][i] = _matmul(st["act"], dm, "tn", F32, "d_wd")
        dgu = _swiglu_bwd(dm, wd[i], st["gu"])
        dh2 = _matmul(dgu, wgu[i], "nt", F32, "d_h2")
        grads["wgu"][i] = _matmul(st["h2"], dgu, "tn", F32, "d_wgu")
        dx1, dsh2, dsc2, grads["norm2_g"][i] = _norm_mod_bwd(
            st["x1"], norm2_g[i:i + 1], st["sc2"], dh2.reshape(NB, S, D), dx)
        dy, dg1 = _gate_res_bwd(dx1, st["y"], st["g1"])
        dy = dy.reshape(T, D)
        if i % 2 == 0:
            do = _matmul(dy, wo_a[j], "nt", BF16, "d_o_a").reshape(NB, S, QA)
            grads["wo_a"][j] = _matmul(st["o"], dy, "tn", F32, "d_wo_a")
            dq, dk, dv, dsink = _attn_a_bwd(st["qkn"], do, sinks_a[j])
            dqkv, dgain = _qk_prep_bwd(st["qkv"], dq.reshape(T, QA), dk.reshape(T, LANES), dv.reshape(T, LANES),
                                       tab_c, tab_s, st["gains"])
            dh = _matmul(dqkv, wqkv_a[j], "nt", F32, "d_h_a")
            grads["wqkv_a"][j] = _matmul(st["h"], dqkv, "tn", F32, "d_wqkv_a")
            grads["q_norm_a"][j] = jnp.sum(dgain[:GROUP_A].reshape(2 * GROUP_A, HEAD_DIM), axis=0)
            grads["k_norm_a"][j] = jnp.sum(dgain[GROUP_A].reshape(2, HEAD_DIM), axis=0)
            grads["sinks_a"][j] = jnp.sum(dsink[..., 0], axis=0)
        else:
            do = _matmul(dy, wo_b[j], "nt", BF16, "d_o_b").reshape(NB, S, -1)
            grads["wo_b"][j] = _matmul(st["o"], dy, "tn", F32, "d_wo_b")
            dq, dk, dv = _attn_b_bwd(st["qkv"], do, rev, fwd)
            dqkv = jnp.concatenate([dq, dk, dv], axis=-1).reshape(T, -1).astype(BF16)
            dh = _matmul(dqkv, wqkv_b[j], "nt", F32, "d_h_b")
            grads["wqkv_b"][j] = _matmul(st["h"], dqkv, "tn", F32, "d_wqkv_b")
        dx, dsh1, dsc1, grads["norm1_g"][i] = _norm_mod_bwd(
            st["x"], norm1_g[i:i + 1], st["sc1"], dh.reshape(NB, S, D), dx1)
        dmod[i] = jnp.concatenate([dsh1, dsc1, dg1, dsh2, dsc2, dg2], axis=-1).reshape(NB, 6 * D)

    matrices = ("wqkv_a", "wo_a", "wqkv_b", "wo_b", "wgu", "wd")
    grads = {name: parts if name in matrices else jnp.stack(parts) for name, parts in grads.items()}
    return loss, dx, grads, jnp.stack(dmod)


def _rows_of(flat, cols=PACK_COLS):
    n = flat.shape[0]
    pad = (-n) % (8 * cols)
    if pad:
        flat = jnp.concatenate([flat, jnp.zeros((pad,), flat.dtype)])
    return flat.reshape(-1, cols)


def kernel(x, c, positions, ada_w, ada_b, norm1_g, norm2_g, wqkv_a, q_norm_a, k_norm_a, sinks_a, wo_a, wqkv_b, wo_b, w_gate, w_up, w_down, loss_target, m_ada_w, m_ada_b, m_norm1_g, m_norm2_g, m_wqkv_a, m_q_norm_a, m_k_norm_a, m_sinks_a, m_wo_a, m_wqkv_b, m_wo_b, m_w_gate, m_w_up, m_w_down, v_ada_w, v_ada_b, v_norm1_g, v_norm2_g, v_wqkv_a, v_q_norm_a, v_k_norm_a, v_sinks_a, v_wo_a, v_wqkv_b, v_wo_b, v_w_gate, v_w_up, v_w_down):
    xi, yi, ci = lax.axis_index("x"), lax.axis_index("y"), lax.axis_index("c")
    dev = 4 * xi + 2 * yi + ci
    chip = 2 * xi + yi
    NB, S, D = x.shape
    B_all = N_DEV * NB
    L = ada_w.shape[0]
    n_mod = ada_w.shape[2] // 2

    c_all = _gather8(_rows_of(c.reshape(-1), LANES), "gather_c").reshape(N_DEV, -1)[:, :NB * D].reshape(B_all, D)
    ada_w_half = lax.dynamic_slice_in_dim(ada_w, ci * n_mod, n_mod, axis=2)
    ada_b_half = lax.dynamic_slice_in_dim(ada_b, dev * n_mod, n_mod, axis=1).reshape(L, 1, n_mod)
    mod_part = _ada_fwd(c_all, ada_w_half, ada_b_half)
    n_part = L * B_all * n_mod
    mod_all = _gather8(_rows_of(mod_part.reshape(-1)), "gather_mod").reshape(N_DEV, -1)[:, :n_part]
    mod_all = mod_all.reshape(N_DEV, L, B_all, n_mod).transpose(1, 2, 0, 3).reshape(L, B_all, N_DEV * n_mod)
    mod = lax.dynamic_slice_in_dim(mod_all, dev * NB, NB, axis=1)

    shards = dict(wqkv_a=wqkv_a, wo_a=wo_a, wqkv_b=wqkv_b, wo_b=wo_b, w_gate=w_gate, w_up=w_up, w_down=w_down)
    halves = []
    for name, _ in _SHARDED:
        w = shards[name]
        half = lax.dynamic_index_in_dim(w.reshape((2, w.shape[0] // 2) + w.shape[1:]), ci, 0, keepdims=False)
        halves.append(half.astype(BF16))
    gathered = _all_gather8(halves, "gather_weights", local_axis=1, local_chunks=8)
    full = {name: _unpack_full(t, axis) for (name, axis), t in zip(_SHARDED, gathered)}
    wgu = [_interleave(gate, up) for gate, up in zip(full["w_gate"], full["w_up"])]

    loss, grad_x, g, dmod = _local_step(
        x, positions, mod, norm1_g, norm2_g, q_norm_a, k_norm_a, sinks_a,
        full["wqkv_a"], full["wo_a"], full["wqkv_b"], full["wo_b"], wgu, full["w_down"], loss_target)

    g_full = dict(wqkv_a=g["wqkv_a"], wo_a=g["wo_a"], wqkv_b=g["wqkv_b"], wo_b=g["wo_b"],
                  w_gate=g["wgu"], w_up=g["wgu"], w_down=g["wd"])
    which = dict(w_gate=0, w_up=1)
    packed = [_pack_full(g_full[name], axis, which.get(name)) for name, axis in _SHARDED]
    def own(t, index):
        return lax.dynamic_index_in_dim(t, index, 0, keepdims=False)

    from_cores = _exchange(packed, "c", "rs_cores", chunk_axis=0, chunks=4)
    chip_part = [_sum_slabs(own(p, ci), r, "rs_add_cores", with_bf16=True) for p, r in zip(packed, from_cores)]
    from_chips = _exchange([b for _, b in chip_part], "xy", "rs_chips")
    mine = [_sum_slabs(own(p, chip), r, "rs_add_chips") for (p, _), r in zip(chip_part, from_chips)]
    theirs = _sibling_send(mine, "rs_halves")
    grad = {}
    for (name, _), m, t in zip(_SHARDED, mine, theirs):
        first, second = jnp.where(ci == 0, m, t), jnp.where(ci == 0, t, m)
        grad[name] = jnp.stack([first, second]).reshape(shards[name].shape)

    small_names = ("norm1_g", "norm2_g", "q_norm_a", "k_norm_a", "sinks_a")
    small = [dmod.reshape(-1)] + [g[name].reshape(-1) for name in small_names] + [loss.reshape(-1)]
    small_sizes = [t.shape[0] for t in small]
    small_rows = _rows_of(jnp.concatenate(small))
    small_all = _gather8(small_rows, "gather_small")
    small_sum = _sum_leading(small_all, "sum_small").reshape(-1)
    n_dmod = small_sizes[0]
    dmod_all = small_all.reshape(N_DEV, -1)[:, :n_dmod].reshape(N_DEV, L, NB, 6 * D)
    dmod_all = dmod_all.transpose(1, 0, 2, 3).reshape(L, B_all, 6 * D)
    off = n_dmod
    for name, sz in zip(small_names + ("loss",), small_sizes[1:]):
        grad[name] = small_sum[off:off + sz]
        off += sz
    loss_total = grad.pop("loss").reshape(())
    for name, ref in (("norm1_g", norm1_g), ("norm2_g", norm2_g), ("q_norm_a", q_norm_a),
                      ("k_norm_a", k_norm_a), ("sinks_a", sinks_a)):
        grad[name] = grad[name].reshape(ref.shape)

    n_shard = ada_w.shape[2]
    dmod_shard = lax.dynamic_slice_in_dim(dmod_all, chip * n_shard, n_shard, axis=2)
    grad["ada_w"], gb = _ada_bwd(c_all, dmod_all, dmod_shard)
    grad["ada_b"] = gb.reshape(ada_b.shape)

    weights = dict(ada_w=ada_w, ada_b=ada_b, norm1_g=norm1_g, norm2_g=norm2_g, wqkv_a=wqkv_a, q_norm_a=q_norm_a,
                   k_norm_a=k_norm_a, sinks_a=sinks_a, wo_a=wo_a, wqkv_b=wqkv_b, wo_b=wo_b, w_gate=w_gate,
                   w_up=w_up, w_down=w_down)
    m_in = dict(ada_w=m_ada_w, ada_b=m_ada_b, norm1_g=m_norm1_g, norm2_g=m_norm2_g, wqkv_a=m_wqkv_a,
                q_norm_a=m_q_norm_a, k_norm_a=m_k_norm_a, sinks_a=m_sinks_a, wo_a=m_wo_a, wqkv_b=m_wqkv_b,
                wo_b=m_wo_b, w_gate=m_w_gate, w_up=m_w_up, w_down=m_w_down)
    v_in = dict(ada_w=v_ada_w, ada_b=v_ada_b, norm1_g=v_norm1_g, norm2_g=v_norm2_g, wqkv_a=v_wqkv_a,
                q_norm_a=v_q_norm_a, k_norm_a=v_k_norm_a, sinks_a=v_sinks_a, wo_a=v_wo_a, wqkv_b=v_wqkv_b,
                wo_b=v_wo_b, w_gate=v_w_gate, w_up=v_w_up, w_down=v_w_down)
    names = list(weights)
    delta, new_m, new_v = {}, {}, {}
    for name in names:
        delta[name], new_m[name], new_v[name] = _adamw(weights[name], grad[name], m_in[name], v_in[name],
                                                       "adamw_" + name)
    return (loss_total, grad_x, *[grad[k] for k in names], *[delta[k] for k in names],
            *[new_m[k] for k in names], *[new_v[k] for k in names])
```

```python
import jax
import jax.numpy as jnp
from jax import lax
from jax.experimental import pallas as pl
from jax.experimental.pallas import tpu as pltpu

F32 = jnp.float32
BF16 = jnp.bfloat16

DEPTH = 4
HEAD_DIM = 64
N_Q_A = 16
N_KV_A = 2
GROUP_A = N_Q_A // N_KV_A
N_H_B = 16
BLOCK = 128
ROT_DIM = HEAD_DIM // 4
ROPE_THETA = 500000.0
EPS = 1e-6
ATTN_SCALE = HEAD_DIM ** -0.5
NEG_BIG = -1e30

ADAM_LR = 0.001
ADAM_B1 = 0.9
ADAM_B2 = 0.999
ADAM_EPS = 1e-08
ADAM_WD = 0.01
ADAM_STEP = 10

N_DEV = 8
LANES = 128
PACK_COLS = 1024
VMEM_LIMIT_BYTES = 48 * 1024 * 1024
MESH = pl.DeviceIdType.MESH

_NT = (((1,), (1,)), ((), ()))
_TN = (((0,), (0,)), ((), ()))
_NN = (((1,), (0,)), ((), ()))


def _params(sem=None, vmem_limit_bytes=VMEM_LIMIT_BYTES):
    return pltpu.CompilerParams(vmem_limit_bytes=vmem_limit_bytes, dimension_semantics=sem)


def _pick(n, cap, mult):
    best = None
    for t in range(mult, min(n, cap) + 1, mult):
        if n % t == 0:
            best = t
    return n if best is None else best


_ANY = pl.BlockSpec(memory_space=pl.ANY)


def _window(index, axis, q, n, shape):
    rest = [slice(None)] * len(shape)
    size = shape[axis] // n
    rest[axis] = pl.ds(q * size, size)
    return tuple(index) + tuple(rest)


def _all_gather8(xs, name, local_axis=0, local_chunks=1, relay_axis=None):
    n = len(xs)
    n_sems = 7 if relay_axis is None else 9

    def body(*refs):
        x_refs, out_refs = refs[:n], refs[n:2 * n]
        send_sems, recv_sems, local_sems = refs[2 * n:]
        xi, yi, ci = lax.axis_index("x"), lax.axis_index("y"), lax.axis_index("c")
        me, sibling = (xi, yi, ci), (xi, yi, 1 - ci)
        chips = [(1 - xi, yi), (xi, 1 - yi), (1 - xi, 1 - yi)]

        def slab(w, px, py, pc):
            return out_refs[w].at[4 * px + 2 * py + pc]

        def copy(w, k, block, to, src=None):
            return pltpu.make_async_remote_copy(
                src_ref=slab(w, *block) if src is None else src, dst_ref=slab(w, *block),
                send_sem=send_sems.at[k, w], recv_sem=recv_sems.at[k, w], device_id=to, device_id_type=MESH)

        mine = []
        for w in range(n):
            for q in range(local_chunks):
                part = _window((), local_axis, q, local_chunks, xs[w].shape)
                mine.append(pltpu.make_async_copy(x_refs[w].at[part], slab(w, *me).at[part], local_sems.at[w, q]))
                mine[-1].start()
        direct = chips if relay_axis is None else chips[:2]
        first = [copy(w, 0, me, sibling, src=x_refs[w]) for w in range(n)]
        first += [copy(w, 1 + j, me, (*chip, ci), src=x_refs[w]) for j, chip in enumerate(direct) for w in range(n)]
        for cp in first:
            cp.start()

        def relay(w, part, block, to):
            piece = _window((), relay_axis, part, 2, xs[w].shape)
            return pltpu.make_async_remote_copy(
                src_ref=slab(w, *block).at[piece], dst_ref=slab(w, *block).at[piece],
                send_sem=send_sems.at[7 + part, w], recv_sem=recv_sems.at[7 + part, w],
                device_id=to, device_id_type=MESH)

        passed = []
        for j, chip in enumerate(direct):
            for w in range(n):
                copy(w, 1 + j, (*chip, ci), me).wait_recv()
                passed.append(copy(w, 4 + j, (*chip, ci), sibling))
                passed[-1].start()
                if relay_axis is not None:
                    passed.append(relay(w, j, (*chip, ci), (*chips[1 - j], ci)))
                    passed[-1].start()
        if relay_axis is not None:
            for w in range(n):
                for part in range(2):
                    relay(w, part, (*chips[2], ci), me).wait_recv()
                passed.append(copy(w, 6, (*chips[2], ci), sibling))
                passed[-1].start()
        for w in range(n):
            copy(w, 0, sibling, me).wait_recv()
        for j, chip in enumerate(chips):
            for w in range(n):
                copy(w, 4 + j, (*chip, 1 - ci), me).wait_recv()
        for cp in first + passed:
            cp.wait_send()
        for cp in mine:
            cp.wait()

    return pl.pallas_call(
        body, name=name,
        out_shape=[jax.ShapeDtypeStruct((N_DEV,) + x.shape, x.dtype) for x in xs],
        in_specs=[_ANY] * n, out_specs=[_ANY] * n,
        scratch_shapes=[pltpu.SemaphoreType.DMA((n_sems, n)), pltpu.SemaphoreType.DMA((n_sems, n)),
                        pltpu.SemaphoreType.DMA((n, local_chunks))],
    )(*xs)


def _exchange_cores(xs, name, chunk_axis=0, chunks=1):
    n = len(xs)
    n_peers = 1

    def body(*refs):
        x_refs, out_refs = refs[:n], refs[n:2 * n]
        send_sems, recv_sems = refs[2 * n:]
        xi, yi, ci = lax.axis_index("x"), lax.axis_index("y"), lax.axis_index("c")
        peers = [(1 - ci, (xi, yi, 1 - ci))]
        copies = []
        for k, (p, dev) in enumerate(peers):
            for w in range(n):
                slab_shape = xs[w].shape[1:]
                for q in range(chunks):
                    copies.append(pltpu.make_async_remote_copy(
                        src_ref=x_refs[w].at[_window((p,), chunk_axis, q, chunks, slab_shape)],
                        dst_ref=out_refs[w].at[_window((k,), chunk_axis, q, chunks, slab_shape)],
                        send_sem=send_sems.at[k, w, q], recv_sem=recv_sems.at[k, w, q],
                        device_id=dev, device_id_type=MESH))
                    copies[-1].start()
        for cp in copies:
            cp.wait()

    return pl.pallas_call(
        body, name=name,
        out_shape=[jax.ShapeDtypeStruct((n_peers,) + x.shape[1:], x.dtype) for x in xs],
        in_specs=[_ANY] * n, out_specs=[_ANY] * n,
        scratch_shapes=[pltpu.SemaphoreType.DMA((n_peers, n, chunks)), pltpu.SemaphoreType.DMA((n_peers, n, chunks))],
    )(*xs)


def _exchange_chips(xs, name, relay_axis):
    n = len(xs)

    def half_shape(x):
        shape = list(x.shape[1:])
        shape[relay_axis] //= 2
        return tuple(shape)

    def body(*refs):
        x_refs, out_refs, hop_refs = refs[:n], refs[n:2 * n], refs[2 * n:3 * n]
        send_sems, recv_sems = refs[3 * n:]
        xi, yi, ci = lax.axis_index("x"), lax.axis_index("y"), lax.axis_index("c")
        nbr = [(1 - xi, yi, ci), (xi, 1 - yi, ci)]
        slab_of_nbr = [2 * (1 - xi) + yi, 2 * xi + (1 - yi)]
        slab_of_diag = 2 * (1 - xi) + (1 - yi)

        def copy(k, w, src, dst, to):
            return pltpu.make_async_remote_copy(src_ref=src, dst_ref=dst, send_sem=send_sems.at[k, w],
                                                recv_sem=recv_sems.at[k, w], device_id=to, device_id_type=MESH)

        def piece(w, part):
            return _window((), relay_axis, part, 2, xs[w].shape[1:])

        sent = []
        for w in range(n):
            for j in range(2):
                sent.append(copy(j, w, x_refs[w].at[slab_of_nbr[j]], out_refs[w].at[j], nbr[j]))
                sent.append(copy(2 + j, w, x_refs[w].at[(slab_of_diag,) + piece(w, j)], hop_refs[w].at[j], nbr[j]))
        for cp in sent:
            cp.start()
        for w in range(n):
            for j in range(2):
                copy(2 + j, w, hop_refs[w].at[j], hop_refs[w].at[j], nbr[j]).wait_recv()
                sent.append(copy(4 + j, w, hop_refs[w].at[j], out_refs[w].at[(2,) + piece(w, j)], nbr[1 - j]))
                sent[-1].start()
        for w in range(n):
            for j in range(2):
                copy(j, w, out_refs[w].at[j], out_refs[w].at[j], nbr[j]).wait_recv()
                half = out_refs[w].at[(2,) + piece(w, j)]
                copy(4 + j, w, half, half, nbr[1 - j]).wait_recv()
        for cp in sent:
            cp.wait_send()

    out = pl.pallas_call(
        body, name=name,
        out_shape=[jax.ShapeDtypeStruct((3,) + x.shape[1:], x.dtype) for x in xs]
        + [jax.ShapeDtypeStruct((2,) + half_shape(x), x.dtype) for x in xs],
        in_specs=[_ANY] * n, out_specs=[_ANY] * (2 * n),
        scratch_shapes=[pltpu.SemaphoreType.DMA((6, n)), pltpu.SemaphoreType.DMA((6, n))],
    )(*xs)
    return out[:n]


def _sibling_send(xs, name, chunk_axis=1, chunks=4):
    n = len(xs)

    def body(*refs):
        x_refs, out_refs = refs[:n], refs[n:2 * n]
        send_sems, recv_sems = refs[2 * n:]
        xi, yi, ci = lax.axis_index("x"), lax.axis_index("y"), lax.axis_index("c")
        copies = []
        for w in range(n):
            for q in range(chunks):
                part = _window((), chunk_axis, q, chunks, xs[w].shape)
                copies.append(pltpu.make_async_remote_copy(
                    src_ref=x_refs[w].at[part], dst_ref=out_refs[w].at[part],
                    send_sem=send_sems.at[w, q], recv_sem=recv_sems.at[w, q],
                    device_id=(xi, yi, 1 - ci), device_id_type=MESH))
                copies[-1].start()
        for cp in copies:
            cp.wait()

    return pl.pallas_call(
        body, name=name,
        out_shape=[jax.ShapeDtypeStruct(x.shape, x.dtype) for x in xs],
        in_specs=[_ANY] * n, out_specs=[_ANY] * n,
        scratch_shapes=[pltpu.SemaphoreType.DMA((n, chunks)), pltpu.SemaphoreType.DMA((n, chunks))],
    )(*xs)


def _sum_leading(x, name, own=None, with_bf16=False):
    P, R, C = x.shape
    tr = _pick(R, max(16, (1 << 19) // (C * (P + 1))), 16)

    def body(*refs):
        n_in = 1 if own is None else 2
        x_ref = refs[n_in - 1]
        acc = x_ref[0].astype(F32) if own is None else refs[0][...] + x_ref[0].astype(F32)
        for p in range(1, P):
            acc = acc + x_ref[p].astype(F32)
        refs[n_in][...] = acc
        if with_bf16:
            refs[n_in + 1][...] = acc.astype(BF16)

    flat = pl.BlockSpec((tr, C), lambda r: (r, 0))
    slabs = pl.BlockSpec((P, tr, C), lambda r: (0, r, 0))
    out = pl.pallas_call(
        body, name=name, grid=(R // tr,),
        in_specs=[slabs] if own is None else [flat, slabs],
        out_specs=[flat, flat] if with_bf16 else [flat],
        out_shape=[jax.ShapeDtypeStruct((R, C), F32)] + ([jax.ShapeDtypeStruct((R, C), BF16)] if with_bf16 else []),
        compiler_params=_params(("arbitrary",)),
    )(*([x] if own is None else [own, x]))
    return out if with_bf16 else out[0]


MATMUL_SINGLE_K = 1280
MATMUL_VMEM_BUDGET = 36 * 1024 * 1024


def _matmul(a, b, mode, out_dtype, name, swiglu=False):
    if mode == "nn":
        (M, K), N = a.shape, b.shape[1]
    elif mode == "nt":
        (M, K), N = a.shape, b.shape[0]
    else:
        (K, M), N = a.shape, b.shape[1]
    tm = _pick(M, 1024 if mode != "tn" else 1536, 128)
    tn = _pick(N, 1536, 128)
    if swiglu:
        tm, tn = _pick(M, 1024 if out_dtype == BF16 else 512, 128), 2 * _ff_tile(N // 2)
    out_bytes = jnp.dtype(out_dtype).itemsize
    tk = K
    if K > MATMUL_SINGLE_K:
        for cap in (2048, 1024, 512):
            tk = _pick(K, cap, 128)
            blocks = 2 * 2 * tk * (tm + tn) + tm * tn * (2 * out_bytes + (4 if out_dtype != F32 else 0))
            if blocks <= MATMUL_VMEM_BUDGET:
                break
    nk = K // tk
    dims = {"nn": _NN, "nt": _NT, "tn": _TN}[mode]
    use_scratch = nk > 1 and out_dtype != F32

    def body(a_ref, b_ref, *refs):
        o_ref = refs[0]

        def product():
            return lax.dot_general(a_ref[...].astype(BF16), b_ref[...].astype(BF16), dims,
                                   preferred_element_type=F32)

        if nk == 1:
            part = product()
            o_ref[...] = part.astype(o_ref.dtype)
            if swiglu:
                g = part[:, :tn // 2]
                refs[1][...] = (g * _sigmoid(g) * part[:, tn // 2:]).astype(BF16)
            return
        k = pl.program_id(2)
        acc_ref = refs[-1] if use_scratch else o_ref

        @pl.when(k == 0)
        def _():
            acc_ref[...] = jnp.zeros_like(acc_ref)

        acc_ref[...] += product()

        if use_scratch:
            @pl.when(k == nk - 1)
            def _():
                o_ref[...] = acc_ref[...].astype(o_ref.dtype)

    if mode == "tn":
        a_spec = pl.BlockSpec((tk, tm), lambda i, j, k: (k, i))
    else:
        a_spec = pl.BlockSpec((tm, tk), lambda i, j, k: (i, k))
    if mode == "nt":
        b_spec = pl.BlockSpec((tn, tk), lambda i, j, k: (j, k))
    else:
        b_spec = pl.BlockSpec((tk, tn), lambda i, j, k: (k, j))
    out_specs = [pl.BlockSpec((tm, tn), lambda i, j, k: (i, j))]
    out_shape = [jax.ShapeDtypeStruct((M, N), out_dtype)]
    if swiglu:
        assert nk == 1 and mode == "nn"
        out_specs.append(pl.BlockSpec((tm, tn // 2), lambda i, j, k: (i, j)))
        out_shape.append(jax.ShapeDtypeStruct((M, N // 2), BF16))
    out = pl.pallas_call(
        body, name=name, grid=(M // tm, N // tn, nk),
        in_specs=[a_spec, b_spec], out_specs=out_specs, out_shape=out_shape,
        scratch_shapes=[pltpu.VMEM((tm, tn), F32)] if use_scratch else [],
        compiler_params=_params(("parallel", "parallel", "arbitrary")),
    )(a, b)
    return out if swiglu else out[0]


def _row_tile(S):
    return _pick(S, 512, 8)


def _norm_mod_fwd(x, gain, sc, sh):
    NB, S, D = x.shape
    tr = _row_tile(S)

    def body(x_ref, g_ref, sc_ref, sh_ref, h_ref):
        xv = x_ref[...]
        ms = jnp.mean(xv * xv, axis=-1, keepdims=True)
        n = xv * lax.rsqrt(ms + EPS) * g_ref[...]
        h_ref[...] = (n * (1.0 + sc_ref[...]) + sh_ref[...]).astype(BF16)

    tok = pl.BlockSpec((None, tr, D), lambda b, r: (b, r, 0))
    per_ex = pl.BlockSpec((None, 1, D), lambda b, r: (b, 0, 0))
    return pl.pallas_call(
        body, name="norm_mod_fwd", grid=(NB, S // tr),
        in_specs=[tok, pl.BlockSpec((1, D), lambda b, r: (0, 0)), per_ex, per_ex],
        out_specs=tok, out_shape=jax.ShapeDtypeStruct((NB, S, D), BF16),
        compiler_params=_params(("parallel", "parallel")),
    )(x, gain, sc, sh)


def _norm_mod_bwd(x, gain, sc, dh, dres):
    NB, S, D = x.shape
    tr = _row_tile(S)

    def body(x_ref, g_ref, sc_ref, dh_ref, dres_ref, dx_ref, dsh_ref, dsc_ref, dg_ref):
        b, r = pl.program_id(0), pl.program_id(1)

        @pl.when(r == 0)
        def _():
            dsh_ref[...] = jnp.zeros_like(dsh_ref)
            dsc_ref[...] = jnp.zeros_like(dsc_ref)

        @pl.when((r == 0) & (b == 0))
        def _():
            dg_ref[...] = jnp.zeros_like(dg_ref)

        xv = x_ref[...]
        rstd = lax.rsqrt(jnp.mean(xv * xv, axis=-1, keepdims=True) + EPS)
        xh = xv * rstd
        g = g_ref[...]
        dh = dh_ref[...]
        dsh_ref[...] += jnp.sum(dh, axis=0, keepdims=True)
        dsc_ref[...] += jnp.sum(dh * (xh * g), axis=0, keepdims=True)
        dn = dh * (1.0 + sc_ref[...])
        dg_ref[...] += jnp.sum(dn * xh, axis=0, keepdims=True)
        dxh = dn * g
        proj = jnp.mean(dxh * xh, axis=-1, keepdims=True)
        dx_ref[...] = rstd * (dxh - xh * proj) + dres_ref[...]

    tok = pl.BlockSpec((None, tr, D), lambda b, r: (b, r, 0))
    per_ex = pl.BlockSpec((None, 1, D), lambda b, r: (b, 0, 0))
    row = pl.BlockSpec((1, D), lambda b, r: (0, 0))
    return pl.pallas_call(
        body, name="norm_mod_bwd", grid=(NB, S // tr),
        in_specs=[tok, row, per_ex, tok, tok],
        out_specs=[tok, per_ex, per_ex, row],
        out_shape=[jax.ShapeDtypeStruct((NB, S, D), F32), jax.ShapeDtypeStruct((NB, 1, D), F32),
                   jax.ShapeDtypeStruct((NB, 1, D), F32), jax.ShapeDtypeStruct((1, D), F32)],
        compiler_params=_params(("arbitrary", "arbitrary")),
    )(x, gain, sc, dh, dres)


def _gate_res(x, y, g):
    NB, S, D = x.shape
    tr = _row_tile(S)

    def body(x_ref, y_ref, g_ref, o_ref):
        o_ref[...] = x_ref[...] + g_ref[...] * y_ref[...]

    tok = pl.BlockSpec((None, tr, D), lambda b, r: (b, r, 0))
    per_ex = pl.BlockSpec((None, 1, D), lambda b, r: (b, 0, 0))
    return pl.pallas_call(
        body, name="gate_res", grid=(NB, S // tr), in_specs=[tok, tok, per_ex], out_specs=tok,
        out_shape=jax.ShapeDtypeStruct((NB, S, D), F32),
        compiler_params=_params(("parallel", "parallel")),
    )(x, y, g)


def _gate_res_bwd(dxo, y, g):
    NB, S, D = dxo.shape
    tr = _row_tile(S)

    def body(d_ref, y_ref, g_ref, dy_ref, dg_ref):
        @pl.when(pl.program_id(1) == 0)
        def _():
            dg_ref[...] = jnp.zeros_like(dg_ref)

        d = d_ref[...]
        dy_ref[...] = (d * g_ref[...]).astype(BF16)
        dg_ref[...] += jnp.sum(d * y_ref[...], axis=0, keepdims=True)

    tok = pl.BlockSpec((None, tr, D), lambda b, r: (b, r, 0))
    per_ex = pl.BlockSpec((None, 1, D), lambda b, r: (b, 0, 0))
    return pl.pallas_call(
        body, name="gate_res_bwd", grid=(NB, S // tr), in_specs=[tok, tok, per_ex], out_specs=[tok, per_ex],
        out_shape=[jax.ShapeDtypeStruct((NB, S, D), BF16), jax.ShapeDtypeStruct((NB, 1, D), F32)],
        compiler_params=_params(("arbitrary", "arbitrary")),
    )(dxo, y, g)


def _sigmoid(v):
    return 1.0 / (1.0 + jnp.exp(-v))


def _ff_tile(F):
    return _pick(F, 1536, 128)


def _interleave(gate, up):
    F = gate.shape[-1]
    tf = _ff_tile(F)
    parts = []
    for j in range(F // tf):
        parts += [gate[..., j * tf:(j + 1) * tf], up[..., j * tf:(j + 1) * tf]]
    return jnp.concatenate(parts, axis=-1)


def _deinterleave(gu):
    F = gu.shape[-1] // 2
    tf = _ff_tile(F)
    gate = [gu[..., 2 * j * tf:(2 * j + 1) * tf] for j in range(F // tf)]
    up = [gu[..., (2 * j + 1) * tf:(2 * j + 2) * tf] for j in range(F // tf)]
    return jnp.concatenate(gate, axis=-1), jnp.concatenate(up, axis=-1)


def _swiglu_bwd(dm, wd, gu):
    T, D = dm.shape
    F = wd.shape[0]
    tf = _ff_tile(F)
    tm = _pick(T, 512, 128)
    assert D <= MATMUL_SINGLE_K

    def body(a_ref, b_ref, gu_ref, o_ref):
        d = lax.dot_general(a_ref[...], b_ref[...], _NT, preferred_element_type=F32)
        g, u = gu_ref[:, :tf].astype(F32), gu_ref[:, tf:].astype(F32)
        s = _sigmoid(g)
        o_ref[:, :tf] = (d * u * (s * (1.0 + g * (1.0 - s)))).astype(BF16)
        o_ref[:, tf:] = (d * (g * s)).astype(BF16)

    return pl.pallas_call(
        body, name="swiglu_bwd", grid=(T // tm, F // tf),
        in_specs=[pl.BlockSpec((tm, D), lambda i, j: (i, 0)), pl.BlockSpec((tf, D), lambda i, j: (j, 0)),
                  pl.BlockSpec((tm, 2 * tf), lambda i, j: (i, j))],
        out_specs=pl.BlockSpec((tm, 2 * tf), lambda i, j: (i, j)),
        out_shape=jax.ShapeDtypeStruct((T, 2 * F), BF16),
        compiler_params=_params(("parallel", "parallel")),
    )(dm, wd, gu)


def _loss_fwd_bwd(y, target):
    NB, S, D = y.shape
    tr = _row_tile(S)

    def body(y_ref, t_ref, l_ref, d_ref):
        @pl.when((pl.program_id(0) == 0) & (pl.program_id(1) == 0))
        def _():
            l_ref[...] = jnp.zeros_like(l_ref)

        e = y_ref[...] - t_ref[...]
        d_ref[...] = e / D
        l_ref[...] += 0.5 * jnp.sum(jnp.mean(e * e, axis=-1, keepdims=True), axis=0, keepdims=True)

    tok = pl.BlockSpec((None, tr, D), lambda b, r: (b, r, 0))
    return pl.pallas_call(
        body, name="loss", grid=(NB, S // tr), in_specs=[tok, tok],
        out_specs=[pl.BlockSpec((1, 1), lambda b, r: (0, 0)), tok],
        out_shape=[jax.ShapeDtypeStruct((1, 1), F32), jax.ShapeDtypeStruct((NB, S, D), F32)],
        compiler_params=_params(("arbitrary", "arbitrary")),
    )(y, target)


def _half_sums(v, lo):
    sa = jnp.sum(jnp.where(lo, v, 0.0), axis=-1, keepdims=True)
    sb = jnp.sum(jnp.where(lo, 0.0, v), axis=-1, keepdims=True)
    return jnp.where(lo, sa, sb)


def _rope_swap(v, lane64):
    up = pltpu.roll(v, LANES - ROT_DIM // 2, 1)
    down = pltpu.roll(v, ROT_DIM // 2, 1)
    return jnp.where(lane64 < ROT_DIM // 2, up, jnp.where(lane64 < ROT_DIM, down, 0.0))


def _qk_prep_fwd(qkv, tab_c, tab_s, gains):
    T, W = qkv.shape
    R = W // LANES
    tt = _pick(T, 256, 8)

    def body(x_ref, c_ref, s_ref, g_ref, o_ref):
        lane = lax.broadcasted_iota(jnp.int32, (tt, LANES), 1)
        lo = lane < HEAD_DIM
        lane64 = lane & (HEAD_DIM - 1)
        c, s = c_ref[...], s_ref[...]
        for j in range(R - 1):
            cols = slice(j * LANES, (j + 1) * LANES)
            xv = x_ref[:, cols]
            rstd = lax.rsqrt(_half_sums(xv * xv, lo) / HEAD_DIM + EPS)
            yn = xv * rstd * g_ref[j:j + 1, :]
            o_ref[:, cols] = (yn * c + _rope_swap(yn, lane64) * s).astype(BF16)
        o_ref[:, (R - 1) * LANES:] = x_ref[:, (R - 1) * LANES:].astype(BF16)

    tok = pl.BlockSpec((tt, W), lambda t: (t, 0))
    tab = pl.BlockSpec((tt, LANES), lambda t: (t, 0))
    return pl.pallas_call(
        body, name="qk_prep_fwd", grid=(T // tt,),
        in_specs=[tok, tab, tab, pl.BlockSpec((R, LANES), lambda t: (0, 0))],
        out_specs=tok, out_shape=jax.ShapeDtypeStruct((T, W), BF16),
        compiler_params=_params(("parallel",)),
    )(qkv, tab_c, tab_s, gains)


def _qk_prep_bwd(qkv, dq, dk, dv, tab_c, tab_s, gains):
    T, W = qkv.shape
    R = W // LANES
    QW = dq.shape[1]
    tt = _pick(T, 256, 8)

    def body(x_ref, dq_ref, dk_ref, dv_ref, c_ref, s_ref, g_ref, o_ref, dg_ref):
        @pl.when(pl.program_id(0) == 0)
        def _():
            dg_ref[...] = jnp.zeros_like(dg_ref)

        lane = lax.broadcasted_iota(jnp.int32, (tt, LANES), 1)
        lo = lane < HEAD_DIM
        lane64 = lane & (HEAD_DIM - 1)
        c, s = c_ref[...], s_ref[...]
        for j in range(R - 1):
            cols = slice(j * LANES, (j + 1) * LANES)
            xv = x_ref[:, cols]
            d = dq_ref[:, cols] if j < R - 2 else dk_ref[...]
            rstd = lax.rsqrt(_half_sums(xv * xv, lo) / HEAD_DIM + EPS)
            xh = xv * rstd
            dyn = d * c + _rope_swap(d * s, lane64)
            dg_ref[j:j + 1, :] += jnp.sum(dyn * xh, axis=0, keepdims=True)
            dxh = dyn * g_ref[j:j + 1, :]
            proj = _half_sums(dxh * xh, lo) / HEAD_DIM
            o_ref[:, cols] = (rstd * (dxh - xh * proj)).astype(BF16)
        o_ref[:, (R - 1) * LANES:] = dv_ref[...].astype(BF16)

    tok = pl.BlockSpec((tt, W), lambda t: (t, 0))
    tab = pl.BlockSpec((tt, LANES), lambda t: (t, 0))
    gsp = pl.BlockSpec((R, LANES), lambda t: (0, 0))
    return pl.pallas_call(
        body, name="qk_prep_bwd", grid=(T // tt,),
        in_specs=[tok, pl.BlockSpec((tt, QW), lambda t: (t, 0)), tab, tab, tab, tab, gsp], out_specs=[tok, gsp],
        out_shape=[jax.ShapeDtypeStruct((T, W), BF16), jax.ShapeDtypeStruct((R, LANES), F32)],
        compiler_params=_params(("arbitrary",)),
    )(qkv, dq, dk, dv, tab_c, tab_s, gains)


def _band_mask(i):
    r = lax.broadcasted_iota(jnp.int32, (2 * BLOCK, 2 * BLOCK), 0) & (BLOCK - 1)
    c = lax.broadcasted_iota(jnp.int32, (2 * BLOCK, 2 * BLOCK), 1)
    rel = r + BLOCK - c
    return (rel >= 0) & (rel < BLOCK) & ((c >= BLOCK) | (i > 0))


def _swa_softmax(s, valid, sink):
    s = jnp.where(valid, s * ATTN_SCALE, NEG_BIG)
    m = jnp.maximum(jnp.max(s, axis=1, keepdims=True), sink)
    p = jnp.exp(s - m)
    ps = jnp.exp(sink - m)
    denom = jnp.sum(p, axis=1, keepdims=True) + ps
    return p / denom, ps / denom


A_GROUP = 4


Q_WIDTH_A = N_Q_A * HEAD_DIM
N_PAIR_A = Q_WIDTH_A // LANES


def _swa_specs():
    qs = pl.BlockSpec((None, BLOCK, Q_WIDTH_A), lambda b, i: (b, i, 0))

    def kv(col, back):
        return pl.BlockSpec((None, BLOCK, LANES), lambda b, i: (b, jnp.maximum(i - back, 0), col))

    return qs, kv(N_PAIR_A, 1), kv(N_PAIR_A, 0), kv(N_PAIR_A + 1, 1), kv(N_PAIR_A + 1, 0)


def _dup_heads(t):
    lo = lax.broadcasted_iota(jnp.int32, t.shape, 1) < HEAD_DIM
    sw = pltpu.roll(t.astype(F32), HEAD_DIM, 1).astype(BF16)
    return jnp.where(lo, t, sw), jnp.where(lo, sw, t)


def _kv_tiles(kp_ref, kc_ref, vp_ref, vc_ref):
    kd = _dup_heads(jnp.concatenate([kp_ref[...], kc_ref[...]], axis=0))
    vd = _dup_heads(jnp.concatenate([vp_ref[...], vc_ref[...]], axis=0))
    return kd, vd


def _attn_a_fwd(qkn, sinks):
    NB, S, _ = qkn.shape
    qs, kp, kc, vp, vc = _swa_specs()

    def body(q_ref, kp_ref, kc_ref, vp_ref, vc_ref, sink_ref, o_ref):
        i = pl.program_id(1)
        kd, vd = _kv_tiles(kp_ref, kc_ref, vp_ref, vc_ref)
        valid = _band_mask(i)
        lo = lax.broadcasted_iota(jnp.int32, (BLOCK, LANES), 1) < HEAD_DIM
        top = lax.broadcasted_iota(jnp.int32, (2 * BLOCK, 1), 0) < BLOCK
        for first in range(0, N_PAIR_A, A_GROUP):
            pairs = range(first, first + A_GROUP)
            qs_ = [jnp.concatenate(_head_halves(q_ref[:, p * LANES:(p + 1) * LANES], lo), axis=0) for p in pairs]
            ss = [lax.dot_general(q, kd[2 * p // GROUP_A], _NT, preferred_element_type=F32) for q, p in zip(qs_, pairs)]
            pns = [_swa_softmax(s, valid, jnp.where(top, sink_ref[2 * p], sink_ref[2 * p + 1]))[0]
                   for s, p in zip(ss, pairs)]
            pvs = [jnp.dot(pn.astype(BF16), vd[2 * p // GROUP_A], preferred_element_type=F32) for pn, p in zip(pns, pairs)]
            for pv, p in zip(pvs, pairs):
                o_ref[:, p * LANES:(p + 1) * LANES] = jnp.where(lo, pv[:BLOCK], pv[BLOCK:]).astype(BF16)

    return pl.pallas_call(
        body, name="attn_a_fwd", grid=(NB, S // BLOCK),
        in_specs=[qs, kp, kc, vp, vc, pl.BlockSpec(memory_space=pltpu.SMEM)],
        out_specs=qs, out_shape=jax.ShapeDtypeStruct((NB, S, Q_WIDTH_A), BF16),
        compiler_params=_params(("parallel", "arbitrary")),
    )(qkn, qkn, qkn, qkn, qkn, sinks)


def _attn_a_bwd(qkn, do, sinks):
    NB, S, _ = qkn.shape
    qs, kp, kc, vp, vc = _swa_specs()
    full = pl.BlockSpec((None, S, LANES), lambda b, i: (b, 0, 0))
    sink_out = pl.BlockSpec((None, N_Q_A, LANES), lambda b, i: (b, 0, 0))

    def body(q_ref, do_ref, kp_ref, kc_ref, vp_ref, vc_ref, sink_ref, dq_ref, dk_ref, dv_ref, ds_ref, dk_s, dv_s):
        i = pl.program_id(1)

        @pl.when(i == 0)
        def _():
            dk_ref[...] = jnp.zeros_like(dk_ref)
            dv_ref[...] = jnp.zeros_like(dv_ref)
            ds_ref[...] = jnp.zeros_like(ds_ref)

        dk_s[...] = jnp.zeros_like(dk_s)
        dv_s[...] = jnp.zeros_like(dv_s)
        kd, vd = _kv_tiles(kp_ref, kc_ref, vp_ref, vc_ref)
        valid = _band_mask(i)
        lo = lax.broadcasted_iota(jnp.int32, (BLOCK, LANES), 1) < HEAD_DIM
        top = lax.broadcasted_iota(jnp.int32, (2 * BLOCK, 1), 0) < BLOCK
        for first in range(0, N_PAIR_A, A_GROUP):
            pairs = range(first, first + A_GROUP)
            kvs = [2 * p // GROUP_A for p in pairs]
            qs_ = [jnp.concatenate(_head_halves(q_ref[:, p * LANES:(p + 1) * LANES], lo), axis=0) for p in pairs]
            dos = [jnp.concatenate(_head_halves(do_ref[:, p * LANES:(p + 1) * LANES], lo), axis=0) for p in pairs]
            ss = [lax.dot_general(q, kd[kv], _NT, preferred_element_type=F32) for q, kv in zip(qs_, kvs)]
            dps = [lax.dot_general(d, vd[kv], _NT, preferred_element_type=F32) for d, kv in zip(dos, kvs)]
            sm = [_swa_softmax(s, valid, jnp.where(top, sink_ref[2 * p], sink_ref[2 * p + 1])) for s, p in zip(ss, pairs)]
            deltas = [jnp.sum(pn * dp, axis=1, keepdims=True) for (pn, _), dp in zip(sm, dps)]
            dsbs = [(pn * (dp - delta) * ATTN_SCALE).astype(BF16) for (pn, _), dp, delta in zip(sm, dps, deltas)]
            for n, p in enumerate(pairs):
                dq2 = jnp.dot(dsbs[n], kd[kvs[n]], preferred_element_type=F32)
                dq_ref[:, p * LANES:(p + 1) * LANES] = jnp.where(lo, dq2[:BLOCK], dq2[BLOCK:])
                dk_s[kvs[n]] += lax.dot_general(dsbs[n], qs_[n], _TN, preferred_element_type=F32)
                dv_s[kvs[n]] += lax.dot_general(sm[n][0].astype(BF16), dos[n], _TN, preferred_element_type=F32)
                t = sm[n][1] * deltas[n]
                for hh in range(2):
                    dsink = -jnp.sum(t[hh * BLOCK:(hh + 1) * BLOCK], axis=0, keepdims=True)
                    ds_ref[2 * p + hh:2 * p + hh + 1, :] += jnp.broadcast_to(dsink, (1, LANES))

        lo2 = lax.broadcasted_iota(jnp.int32, (2 * BLOCK, LANES), 1) < HEAD_DIM

        def fold(acc):
            halves = [acc[kv] + pltpu.roll(acc[kv], HEAD_DIM, 1) for kv in range(N_KV_A)]
            return jnp.where(lo2, halves[0], halves[1])

        dk2, dv2 = fold(dk_s), fold(dv_s)

        @pl.when(i > 0)
        def _():
            start = pl.multiple_of((i - 1) * BLOCK, BLOCK)
            dk_ref[pl.ds(start, 2 * BLOCK), :] += dk2
            dv_ref[pl.ds(start, 2 * BLOCK), :] += dv2

        @pl.when(i == 0)
        def _():
            dk_ref[0:BLOCK, :] += dk2[BLOCK:, :]
            dv_ref[0:BLOCK, :] += dv2[BLOCK:, :]

    slots = pltpu.VMEM((N_KV_A, 2 * BLOCK, LANES), F32)
    return pl.pallas_call(
        body, name="attn_a_bwd", grid=(NB, S // BLOCK),
        in_specs=[qs, qs, kp, kc, vp, vc, pl.BlockSpec(memory_space=pltpu.SMEM)],
        out_specs=[qs, full, full, sink_out],
        out_shape=[jax.ShapeDtypeStruct((NB, S, Q_WIDTH_A), F32), jax.ShapeDtypeStruct((NB, S, LANES), F32),
                   jax.ShapeDtypeStruct((NB, S, LANES), F32), jax.ShapeDtypeStruct((NB, N_Q_A, LANES), F32)],
        scratch_shapes=[slots, slots],
        compiler_params=_params(("parallel", "arbitrary")),
    )(qkn, do, qkn, qkn, qkn, qkn, sinks)


def _cumsum_mats():
    src = lax.broadcasted_iota(jnp.int32, (2 * BLOCK, 2 * BLOCK), 0) % BLOCK
    dst = lax.broadcasted_iota(jnp.int32, (2 * BLOCK, 2 * BLOCK), 1)
    ones = dst >= BLOCK
    rev = ((src > dst) | ones).astype(BF16)
    fwd = ((src < dst) | ones).astype(BF16)
    return rev, fwd


def _log_sigmoids(z):
    sp = jnp.log(1.0 + jnp.exp(-jnp.abs(z)))
    return jnp.minimum(z, 0.0) - sp, -(jnp.maximum(z, 0.0) + sp)


def _cumsum_mxu_many(vs, mat):
    parts = []
    for v in vs:
        hi = v.astype(BF16)
        parts.append(jnp.concatenate([hi, (v - hi.astype(F32)).astype(BF16)], axis=1))
    r = jnp.dot(jnp.concatenate(parts, axis=0), mat, preferred_element_type=F32)
    return [(r[n * BLOCK:(n + 1) * BLOCK, :BLOCK], r[n * BLOCK:(n + 1) * BLOCK, BLOCK:]) for n in range(len(vs))]


def _strict_mask():
    r = lax.broadcasted_iota(jnp.int32, (BLOCK, BLOCK), 0)
    c = lax.broadcasted_iota(jnp.int32, (BLOCK, BLOCK), 1)
    return c < r


def _tile(ref, j):
    return ref[pl.ds(pl.multiple_of(j * BLOCK, BLOCK), BLOCK), :]


SWEEP_EXIT = -88.0


def _head_halves(t, lo):
    zero = jnp.zeros_like(t)
    return jnp.where(lo, t, zero), jnp.where(lo, zero, t)


def _sb_specs(S, HD, width):
    n = HD // width
    blk = pl.BlockSpec((None, BLOCK, width), lambda b, p, i: (b, i, p))
    k_full = pl.BlockSpec((None, S, width), lambda b, p, i: (b, 0, n + p))
    v_full = pl.BlockSpec((None, S, width), lambda b, p, i: (b, 0, 2 * n + p))
    mat = pl.BlockSpec((2 * BLOCK, 2 * BLOCK), lambda b, p, i: (0, 0))
    return blk, k_full, v_full, mat


SB_FWD_PAIRS = 4
SB_BWD_PAIRS = 2
SB_BWD_TILES = 2
SB_BWD_VMEM_LIMIT_BYTES = 58 * 1024 * 1024


def _attn_b_fwd(qkv, rev):
    NB, S, W = qkv.shape
    HD = W // 3
    width = SB_FWD_PAIRS * LANES
    n_heads = 2 * SB_FWD_PAIRS
    blk, k_full, v_full, mat = _sb_specs(S, HD, width)

    def body(q_ref, k_ref, v_ref, rev_ref, o_ref):
        i = pl.program_id(2)
        rv = rev_ref[...]
        mask = _strict_mask()
        lo = lax.broadcasted_iota(jnp.int32, (BLOCK, LANES), 1) < HEAD_DIM
        q_all = q_ref[...]
        q_stack = [jnp.concatenate(_head_halves(q_all[:, p * LANES:(p + 1) * LANES] * ATTN_SCALE, lo), axis=0)
                   for p in range(SB_FWD_PAIRS)]

        def pair_tiles(ref, j):
            t = _tile(ref, j)
            return [t[:, p * LANES:(p + 1) * LANES] for p in range(SB_FWD_PAIRS)]

        def tile_pass(j, carries, diagonal):
            ks, vs = pair_tiles(k_ref, j), pair_tiles(v_ref, j)
            zs = []
            for p in range(SB_FWD_PAIRS):
                z2 = lax.dot_general(q_stack[p], ks[p], _NT, preferred_element_type=F32)
                zs += [z2[:BLOCK], z2[BLOCK:]]
            logs = [_log_sigmoids(z) for z in zs]
            cums = _cumsum_mxu_many([jnp.where(mask, lm, 0.0) if diagonal else lm for _, lm in logs], rv)
            probs, new_c = [], []
            for h in range(n_heads):
                after, rs = cums[h]
                if diagonal:
                    a = jnp.where(mask, jnp.exp(logs[h][0] + after), 0.0)
                    new_c.append(rs)
                else:
                    a = jnp.exp(logs[h][0] + after + carries[h])
                    new_c.append(carries[h] + rs)
                probs.append(a.astype(BF16))
            outs = []
            for p in range(SB_FWD_PAIRS):
                pv = jnp.dot(jnp.concatenate(probs[2 * p:2 * p + 2], axis=0), vs[p], preferred_element_type=F32)
                outs.append(jnp.where(lo, pv[:BLOCK], pv[BLOCK:]))
            return new_c, outs

        carries, accs = tile_pass(i, None, True)

        def live(cs):
            top = cs[0]
            for c in cs[1:]:
                top = jnp.maximum(top, c)
            return jnp.max(top) > SWEEP_EXIT

        def cond(st):
            return (st[0] < i) & st[1]

        def step(st):
            jj, _, cs, accs = st
            new_c, outs = tile_pass(i - 1 - jj, cs, False)
            return jj + 1, live(new_c), new_c, [acc + o for acc, o in zip(accs, outs)]

        st = lax.while_loop(cond, step, (jnp.int32(0), live(carries), carries, accs))
        for p in range(SB_FWD_PAIRS):
            o_ref[:, p * LANES:(p + 1) * LANES] = st[3][p].astype(BF16)

    return pl.pallas_call(
        body, name="attn_b_fwd", grid=(NB, HD // width, S // BLOCK),
        in_specs=[blk, k_full, v_full, mat], out_specs=blk,
        out_shape=jax.ShapeDtypeStruct((NB, S, HD), BF16),
        compiler_params=_params(("parallel", "parallel", "arbitrary")),
    )(qkv, qkv, qkv, rev)


def _attn_b_bwd(qkv, do, rev, fwd):
    NB, S, W = qkv.shape
    HD = W // 3
    width = SB_BWD_PAIRS * LANES
    n_heads = 2 * SB_BWD_PAIRS
    nj = S // BLOCK
    blk, k_full, v_full, mat = _sb_specs(S, HD, width)
    acc_full = pl.BlockSpec((None, S, width), lambda b, p, i: (b, 0, p))

    def body(q_ref, do_ref, k_ref, v_ref, rev_ref, fwd_ref, dq_ref, dk_ref, dv_ref, sig_s, a_s, e_s):
        i = pl.program_id(2)

        @pl.when(i == 0)
        def _():
            dk_ref[...] = jnp.zeros_like(dk_ref)
            dv_ref[...] = jnp.zeros_like(dv_ref)

        rv, fw = rev_ref[...], fwd_ref[...]
        mask = _strict_mask()
        lo = lax.broadcasted_iota(jnp.int32, (BLOCK, LANES), 1) < HEAD_DIM
        pairs = range(SB_BWD_PAIRS)

        def cols(p):
            return slice(p * LANES, (p + 1) * LANES)

        q_stack = [jnp.concatenate(_head_halves(q_ref[:, cols(p)], lo), axis=0) for p in pairs]
        qs_stack = [q * ATTN_SCALE for q in q_stack]
        do_stack = [jnp.concatenate(_head_halves(do_ref[:, cols(p)], lo), axis=0) for p in pairs]

        def sweep1_tiles(js, carries, diagonal):
            zs, das = [], []
            for j in js:
                kj, vj = _tile(k_ref, j), _tile(v_ref, j)
                for p in pairs:
                    z2 = lax.dot_general(qs_stack[p], kj[:, cols(p)], _NT, preferred_element_type=F32)
                    da2 = lax.dot_general(do_stack[p], vj[:, cols(p)], _NT, preferred_element_type=F32)
                    zs += [z2[:BLOCK], z2[BLOCK:]]
                    das += [da2[:BLOCK], da2[BLOCK:]]
            logs = [_log_sigmoids(z) for z in zs]
            cums = _cumsum_mxu_many([jnp.where(mask, lm, 0.0) if diagonal else lm for _, lm in logs], rv)
            new_c, stores = [], []
            for h in range(n_heads):
                carry = None if diagonal else carries[h]
                for t, j in enumerate(js):
                    n = t * n_heads + h
                    lb, (after, rs) = logs[n][0], cums[n]
                    if diagonal:
                        a = jnp.where(mask, jnp.exp(lb + after), 0.0)
                        carry = rs
                    else:
                        a = jnp.exp(lb + after + carry)
                        carry = carry + rs
                    stores.append((t, h, j, jnp.exp(lb), a.astype(BF16), das[n] * a))
                new_c.append(carry)
            for t, h, j, sg, ab, e in sorted(stores, key=lambda s: -s[0]):
                sig_s[h, j] = sg
                a_s[h, j] = ab
                e_s[h, j] = e
            return new_c

        carries = sweep1_tiles([i], None, True)

        def live(cs):
            top = cs[0]
            for c in cs[1:]:
                top = jnp.maximum(top, c)
            return jnp.max(top) > SWEEP_EXIT

        def cond(st):
            return (SB_BWD_TILES * st[0] < i) & st[1]

        def sweep1(st):
            first = i - 1 - SB_BWD_TILES * st[0]
            new_c = sweep1_tiles([jnp.maximum(first - t, 0) for t in range(SB_BWD_TILES)], st[2], False)
            return st[0] + 1, live(new_c), new_c

        trips = lax.while_loop(cond, sweep1, (jnp.int32(0), live(carries), carries))[0]
        lowest = jnp.maximum(i - SB_BWD_TILES * trips, 0)

        def grads(js, st, diagonal):
            prefixes, dqs = st
            es = [e_s[h, j] for j in js for h in range(n_heads)]
            cums = _cumsum_mxu_many(es, fw)
            dzs, new_p = [], []
            for h in range(n_heads):
                prefix = prefixes[h]
                for t, j in enumerate(js):
                    n = t * n_heads + h
                    sg = sig_s[h, j]
                    e_before, rs = cums[n]
                    dz = (es[n] * (1.0 - sg) - (e_before + prefix) * sg) * ATTN_SCALE
                    if diagonal:
                        dz = jnp.where(mask, dz, 0.0)
                    dzs.append((t, h, dz.astype(BF16)))
                    prefix = prefix + rs
                new_p.append(prefix)
            dz_of = {(t, h): dz for t, h, dz in dzs}
            new_dq = list(dqs)
            for t, j in enumerate(js):
                kj = _tile(k_ref, j)
                rows = pl.ds(pl.multiple_of(j * BLOCK, BLOCK), BLOCK)
                for p in pairs:
                    dz_stack = jnp.concatenate([dz_of[t, 2 * p], dz_of[t, 2 * p + 1]], axis=0)
                    a_stack = jnp.concatenate([a_s[2 * p, j], a_s[2 * p + 1, j]], axis=0)
                    dq2 = jnp.dot(dz_stack, kj[:, cols(p)], preferred_element_type=F32)
                    new_dq[p] = new_dq[p] + jnp.where(lo, dq2[:BLOCK], dq2[BLOCK:])
                    dk_ref[rows, cols(p)] += lax.dot_general(dz_stack, q_stack[p], _TN, preferred_element_type=F32)
                    dv_ref[rows, cols(p)] += lax.dot_general(a_stack, do_stack[p], _TN, preferred_element_type=F32)
            return new_p, new_dq

        zeros = jnp.zeros((BLOCK, BLOCK), F32)
        st = ([zeros] * n_heads, [zeros] * SB_BWD_PAIRS)
        count = i - lowest
        st = lax.fori_loop(0, count % SB_BWD_TILES, lambda t, st: grads([lowest + t], st, False), st)
        start = lowest + count % SB_BWD_TILES
        st = lax.fori_loop(0, count // SB_BWD_TILES,
                           lambda t, st: grads([start + SB_BWD_TILES * t + u for u in range(SB_BWD_TILES)], st, False), st)
        dqs = grads([i], st, True)[1]
        for p in pairs:
            dq_ref[:, cols(p)] = dqs[p]

    f32_stash = pltpu.VMEM((n_heads, nj, BLOCK, BLOCK), F32)
    bf16_stash = pltpu.VMEM((n_heads, nj, BLOCK, BLOCK), BF16)
    return pl.pallas_call(
        body, name="attn_b_bwd", grid=(NB, HD // width, nj),
        in_specs=[blk, blk, k_full, v_full, mat, mat], out_specs=[blk, acc_full, acc_full],
        out_shape=[jax.ShapeDtypeStruct((NB, S, HD), F32)] * 3,
        scratch_shapes=[f32_stash, bf16_stash, f32_stash],
        compiler_params=_params(("parallel", "parallel", "arbitrary"), SB_BWD_VMEM_LIMIT_BYTES),
    )(qkv, do, qkv, qkv, rev, fwd)


def _ada_fwd(c_all, w, b):
    L, D, N = w.shape
    B = c_all.shape[0]

    def body(c_ref, w_ref, b_ref, o_ref):
        cv = c_ref[...]
        cond = (cv * _sigmoid(cv)).astype(BF16)
        o_ref[...] = jnp.dot(cond, w_ref[...].astype(BF16), preferred_element_type=F32) + b_ref[...]

    return pl.pallas_call(
        body, name="ada_fwd", grid=(L,),
        in_specs=[pl.BlockSpec((B, D), lambda l: (0, 0)), pl.BlockSpec((None, D, N), lambda l: (l, 0, 0)),
                  pl.BlockSpec((None, 1, N), lambda l: (l, 0, 0))],
        out_specs=pl.BlockSpec((None, B, N), lambda l: (l, 0, 0)),
        out_shape=jax.ShapeDtypeStruct((L, B, N), F32),
        compiler_params=_params(("parallel",)),
    )(c_all, w, b)


def _ada_bwd(c_all, dmod_all, dmod_shard):
    L, B, N = dmod_shard.shape
    D = c_all.shape[1]
    N_all = dmod_all.shape[2]

    def body(c_ref, da_ref, ds_ref, gw_ref, gb_ref):
        cv = c_ref[...]
        cond = (cv * _sigmoid(cv)).astype(BF16)
        gw_ref[...] = lax.dot_general(cond, ds_ref[...].astype(BF16), _TN, preferred_element_type=F32)
        gb_ref[...] = jnp.sum(da_ref[...], axis=0, keepdims=True)

    return pl.pallas_call(
        body, name="ada_bwd", grid=(L,),
        in_specs=[pl.BlockSpec((B, D), lambda l: (0, 0)), pl.BlockSpec((None, B, N_all), lambda l: (l, 0, 0)),
                  pl.BlockSpec((None, B, N), lambda l: (l, 0, 0))],
        out_specs=[pl.BlockSpec((None, D, N), lambda l: (l, 0, 0)), pl.BlockSpec((None, 1, N_all), lambda l: (l, 0, 0))],
        out_shape=[jax.ShapeDtypeStruct((L, D, N), F32), jax.ShapeDtypeStruct((L, 1, N_all), F32)],
        compiler_params=_params(("parallel",)),
    )(c_all, dmod_all, dmod_shard)


def _adamw(w, g, m, v, name):
    shape = w.shape
    C = shape[-1]
    R = w.size // C
    tr = _pick(R, max(8, (1 << 18) // C), 8)
    c1 = 1.0 - ADAM_B1 ** ADAM_STEP
    c2 = 1.0 - ADAM_B2 ** ADAM_STEP

    def body(w_ref, g_ref, m_ref, v_ref, d_ref, nm_ref, nv_ref):
        gv = g_ref[...]
        nm = ADAM_B1 * m_ref[...] + (1.0 - ADAM_B1) * gv
        nv = ADAM_B2 * v_ref[...] + (1.0 - ADAM_B2) * (gv * gv)
        d_ref[...] = -ADAM_LR * ((nm / c1) / (jnp.sqrt(nv / c2) + ADAM_EPS) + ADAM_WD * w_ref[...])
        nm_ref[...] = nm
        nv_ref[...] = nv

    spec = pl.BlockSpec((tr, C), lambda r: (r, 0))
    out = pl.pallas_call(
        body, name=name, grid=(R // tr,), in_specs=[spec] * 4, out_specs=[spec] * 3,
        out_shape=[jax.ShapeDtypeStruct((R, C), F32)] * 3,
        compiler_params=_params(("parallel",)),
    )(*[t.reshape(R, C) for t in (w, g, m, v)])
    return [t.reshape(shape) for t in out]


_SHARDED = (("wqkv_a", 2), ("wo_a", 1), ("wqkv_b", 2), ("wo_b", 1), ("w_gate", 2), ("w_up", 2), ("w_down", 1))


def _pack_full(layers, axis, gate_up=None):
    L = len(layers)
    R, C = layers[0].shape

    def shards(m):
        if gate_up is not None:
            F = C // 2
            tf, Cs = _ff_tile(F), F // 4
            assert tf % Cs == 0
            starts = [(2 * (s * Cs // tf) + gate_up) * tf + s * Cs % tf for s in range(4)]
            return jnp.stack([m[:, st:st + Cs] for st in starts])
        if axis == 2:
            return m.reshape(R, 4, C // 4).transpose(1, 0, 2)
        return m.reshape(4, R // 4, C)

    halves = [jnp.stack([shards(m) for m in layers[h * (L // 2):(h + 1) * (L // 2)]], axis=1) for h in range(2)]
    return jnp.stack(halves)


def _unpack_full(gathered, axis):
    _, Lh, Rs, Cs = gathered.shape
    t = gathered.reshape(4, 2, Lh, Rs, Cs)
    layers = []
    for h in range(2):
        for l in range(Lh):
            piece = t[:, h, l]
            if axis == 2:
                layers.append(piece.transpose(1, 0, 2).reshape(Rs, 4 * Cs))
            else:
                layers.append(piece.reshape(4 * Rs, Cs))
    return layers


def _sum_slabs(own, recv, name, with_bf16=False):
    C = own.shape[-1]
    out = _sum_leading(recv.reshape(recv.shape[0], -1, C), name, own=own.reshape(-1, C), with_bf16=with_bf16)
    if with_bf16:
        return out[0].reshape(own.shape), out[1].reshape(own.shape)
    return out.reshape(own.shape)


def _gather8(x, name):
    return _all_gather8([x], name)[0]


def _rope_tables(positions):
    half = ROT_DIM // 2
    inv_freq = jnp.power(jnp.float32(ROPE_THETA), -jnp.arange(half, dtype=F32) * 2.0 / ROT_DIM)
    ang = positions.astype(F32).reshape(-1, 1) * inv_freq
    cos, sin = jnp.cos(ang), jnp.sin(ang)
    T = ang.shape[0]
    rest = HEAD_DIM - ROT_DIM
    c64 = jnp.concatenate([cos, cos, jnp.ones((T, rest), F32)], axis=1)
    s64 = jnp.concatenate([-sin, sin, jnp.zeros((T, rest), F32)], axis=1)
    return jnp.tile(c64, (1, 2)), jnp.tile(s64, (1, 2))


def _gain_rows(q_gain, k_gain):
    q2 = jnp.tile(q_gain.reshape(1, HEAD_DIM), (GROUP_A, 2))
    k2 = jnp.tile(k_gain.reshape(1, HEAD_DIM), (1, 2))
    return jnp.concatenate([q2, k2, jnp.ones((1, LANES), F32)], axis=0)


def _local_step(x, positions, mod, norm1_g, norm2_g, q_norm_a, k_norm_a, sinks_a,
                wqkv_a, wo_a, wqkv_b, wo_b, wgu, wd, loss_target):
    NB, S, D = x.shape
    T = NB * S
    QA = N_Q_A * HEAD_DIM
    tab_c, tab_s = _rope_tables(positions)
    rev, fwd = _cumsum_mats()

    saved = []
    xc = x
    for i in range(DEPTH):
        j = i // 2
        sh1, sc1, g1, sh2, sc2, g2 = [mod[i][:, k * D:(k + 1) * D].reshape(NB, 1, D) for k in range(6)]
        st = dict(x=xc, sc1=sc1, g1=g1, sc2=sc2, g2=g2)
        h = _norm_mod_fwd(xc, norm1_g[i:i + 1], sc1, sh1)
        st["h"] = h.reshape(T, D)
        if i % 2 == 0:
            st["qkv"] = _matmul(st["h"], wqkv_a[j], "nn", F32, "qkv_a")
            st["gains"] = _gain_rows(q_norm_a[j], k_norm_a[j])
            st["qkn"] = _qk_prep_fwd(st["qkv"], tab_c, tab_s, st["gains"]).reshape(NB, S, -1)
            st["o"] = _attn_a_fwd(st["qkn"], sinks_a[j]).reshape(T, QA)
            y = _matmul(st["o"], wo_a[j], "nn", F32, "wo_a")
        else:
            st["qkv"] = _matmul(st["h"], wqkv_b[j], "nn", BF16, "qkv_b").reshape(NB, S, -1)
            st["o"] = _attn_b_fwd(st["qkv"], rev).reshape(T, N_H_B * HEAD_DIM)
            y = _matmul(st["o"], wo_b[j], "nn", F32, "wo_b")
        st["y"] = y.reshape(NB, S, D)
        x1 = _gate_res(xc, st["y"], g1)
        st["x1"] = x1
        h2 = _norm_mod_fwd(x1, norm2_g[i:i + 1], sc2, sh2)
        st["h2"] = h2.reshape(T, D)
        st["gu"], st["act"] = _matmul(st["h2"], wgu[i], "nn", BF16, "gate_up", swiglu=True)
        st["m"] = _matmul(st["act"], wd[i], "nn", F32, "down").reshape(NB, S, D)
        xc = _gate_res(x1, st["m"], g2)
        saved.append(st)

    loss, dx = _loss_fwd_bwd(xc, loss_target)

    grads = {name: [None] * n for name, n in
             (("wqkv_a", 2), ("wo_a", 2), ("wqkv_b", 2), ("wo_b", 2), ("wgu", DEPTH), ("wd", DEPTH),
              ("norm1_g", DEPTH), ("norm2_g", DEPTH), ("q_norm_a", 2), ("k_norm_a", 2), ("sinks_a", 2))}
    dmod = [None] * DEPTH
    for i in reversed(range(DEPTH)):
        j = i // 2
        st = saved[i]
        dm, dg2 = _gate_res_bwd(dx, st["m"], st["g2"])
        dm = dm.reshape(T, D)
        grads["wd"][i] = _matmul(st["act"], dm, "tn", F32, "d_wd")
        dgu = _swiglu_bwd(dm, wd[i], st["gu"])
        dh2 = _matmul(dgu, wgu[i], "nt", F32, "d_h2")
        grads["wgu"][i] = _matmul(st["h2"], dgu, "tn", F32, "d_wgu")
        dx1, dsh2, dsc2, grads["norm2_g"][i] = _norm_mod_bwd(
            st["x1"], norm2_g[i:i + 1], st["sc2"], dh2.reshape(NB, S, D), dx)
        dy, dg1 = _gate_res_bwd(dx1, st["y"], st["g1"])
        dy = dy.reshape(T, D)
        if i % 2 == 0:
            do = _matmul(dy, wo_a[j], "nt", BF16, "d_o_a").reshape(NB, S, QA)
            grads["wo_a"][j] = _matmul(st["o"], dy, "tn", F32, "d_wo_a")
            dq, dk, dv, dsink = _attn_a_bwd(st["qkn"], do, sinks_a[j])
            dqkv, dgain = _qk_prep_bwd(st["qkv"], dq.reshape(T, QA), dk.reshape(T, LANES), dv.reshape(T, LANES),
                                       tab_c, tab_s, st["gains"])
            dh = _matmul(dqkv, wqkv_a[j], "nt", F32, "d_h_a")
            grads["wqkv_a"][j] = _matmul(st["h"], dqkv, "tn", F32, "d_wqkv_a")
            grads["q_norm_a"][j] = jnp.sum(dgain[:GROUP_A].reshape(2 * GROUP_A, HEAD_DIM), axis=0)
            grads["k_norm_a"][j] = jnp.sum(dgain[GROUP_A].reshape(2, HEAD_DIM), axis=0)
            grads["sinks_a"][j] = jnp.sum(dsink[..., 0], axis=0)
        else:
            do = _matmul(dy, wo_b[j], "nt", BF16, "d_o_b").reshape(NB, S, -1)
            grads["wo_b"][j] = _matmul(st["o"], dy, "tn", F32, "d_wo_b")
            dq, dk, dv = _attn_b_bwd(st["qkv"], do, rev, fwd)
            dqkv = jnp.concatenate([dq, dk, dv], axis=-1).reshape(T, -1).astype(BF16)
            dh = _matmul(dqkv, wqkv_b[j], "nt", F32, "d_h_b")
            grads["wqkv_b"][j] = _matmul(st["h"], dqkv, "tn", F32, "d_wqkv_b")
        dx, dsh1, dsc1, grads["norm1_g"][i] = _norm_mod_bwd(
            st["x"], norm1_g[i:i + 1], st["sc1"], dh.reshape(NB, S, D), dx1)
        dmod[i] = jnp.concatenate([dsh1, dsc1, dg1, dsh2, dsc2, dg2], axis=-1).reshape(NB, 6 * D)

    matrices = ("wqkv_a", "wo_a", "wqkv_b", "wo_b", "wgu", "wd")
    grads = {name: parts if name in matrices else jnp.stack(parts) for name, parts in grads.items()}
    return loss, dx, grads, jnp.stack(dmod)


def _rows_of(flat, cols=PACK_COLS):
    n = flat.shape[0]
    pad = (-n) % (8 * cols)
    if pad:
        flat = jnp.concatenate([flat, jnp.zeros((pad,), flat.dtype)])
    return flat.reshape(-1, cols)


def kernel(x, c, positions, ada_w, ada_b, norm1_g, norm2_g, wqkv_a, q_norm_a, k_norm_a, sinks_a, wo_a, wqkv_b, wo_b, w_gate, w_up, w_down, loss_target, m_ada_w, m_ada_b, m_norm1_g, m_norm2_g, m_wqkv_a, m_q_norm_a, m_k_norm_a, m_sinks_a, m_wo_a, m_wqkv_b, m_wo_b, m_w_gate, m_w_up, m_w_down, v_ada_w, v_ada_b, v_norm1_g, v_norm2_g, v_wqkv_a, v_q_norm_a, v_k_norm_a, v_sinks_a, v_wo_a, v_wqkv_b, v_wo_b, v_w_gate, v_w_up, v_w_down):
    xi, yi, ci = lax.axis_index("x"), lax.axis_index("y"), lax.axis_index("c")
    dev = 4 * xi + 2 * yi + ci
    chip = 2 * xi + yi
    NB, S, D = x.shape
    B_all = N_DEV * NB
    L = ada_w.shape[0]
    n_mod = ada_w.shape[2] // 2

    c_all = _gather8(_rows_of(c.reshape(-1), LANES), "gather_c").reshape(N_DEV, -1)[:, :NB * D].reshape(B_all, D)
    ada_w_half = lax.dynamic_slice_in_dim(ada_w, ci * n_mod, n_mod, axis=2)
    ada_b_half = lax.dynamic_slice_in_dim(ada_b, dev * n_mod, n_mod, axis=1).reshape(L, 1, n_mod)
    mod_part = _ada_fwd(c_all, ada_w_half, ada_b_half)
    n_part = L * B_all * n_mod
    mod_all = _gather8(_rows_of(mod_part.reshape(-1)), "gather_mod").reshape(N_DEV, -1)[:, :n_part]
    mod_all = mod_all.reshape(N_DEV, L, B_all, n_mod).transpose(1, 2, 0, 3).reshape(L, B_all, N_DEV * n_mod)
    mod = lax.dynamic_slice_in_dim(mod_all, dev * NB, NB, axis=1)

    shards = dict(wqkv_a=wqkv_a, wo_a=wo_a, wqkv_b=wqkv_b, wo_b=wo_b, w_gate=w_gate, w_up=w_up, w_down=w_down)
    halves = []
    for name, _ in _SHARDED:
        w = shards[name]
        half = lax.dynamic_index_in_dim(w.reshape((2, w.shape[0] // 2) + w.shape[1:]), ci, 0, keepdims=False)
        halves.append(half.astype(BF16))
    gathered = _all_gather8(halves, "gather_weights", local_axis=1, local_chunks=8, relay_axis=1)
    full = {name: _unpack_full(t, axis) for (name, axis), t in zip(_SHARDED, gathered)}
    wgu = [_interleave(gate, up) for gate, up in zip(full["w_gate"], full["w_up"])]

    loss, grad_x, g, dmod = _local_step(
        x, positions, mod, norm1_g, norm2_g, q_norm_a, k_norm_a, sinks_a,
        full["wqkv_a"], full["wo_a"], full["wqkv_b"], full["wo_b"], wgu, full["w_down"], loss_target)

    g_full = dict(wqkv_a=g["wqkv_a"], wo_a=g["wo_a"], wqkv_b=g["wqkv_b"], wo_b=g["wo_b"],
                  w_gate=g["wgu"], w_up=g["wgu"], w_down=g["wd"])
    which = dict(w_gate=0, w_up=1)
    packed = [_pack_full(g_full[name], axis, which.get(name)) for name, axis in _SHARDED]
    def own(t, index):
        return lax.dynamic_index_in_dim(t, index, 0, keepdims=False)

    from_cores = _exchange_cores(packed, "rs_cores", chunk_axis=0, chunks=4)
    chip_part = [_sum_slabs(own(p, ci), r, "rs_add_cores", with_bf16=True) for p, r in zip(packed, from_cores)]
    from_chips = _exchange_chips([b for _, b in chip_part], "rs_chips", relay_axis=1)
    mine = [_sum_slabs(own(p, chip), r, "rs_add_chips") for (p, _), r in zip(chip_part, from_chips)]
    theirs = _sibling_send(mine, "rs_halves")
    grad = {}
    for (name, _), m, t in zip(_SHARDED, mine, theirs):
        first, second = jnp.where(ci == 0, m, t), jnp.where(ci == 0, t, m)
        grad[name] = jnp.stack([first, second]).reshape(shards[name].shape)

    small_names = ("norm1_g", "norm2_g", "q_norm_a", "k_norm_a", "sinks_a")
    small = [dmod.reshape(-1)] + [g[name].reshape(-1) for name in small_names] + [loss.reshape(-1)]
    small_sizes = [t.shape[0] for t in small]
    small_rows = _rows_of(jnp.concatenate(small))
    small_all = _gather8(small_rows, "gather_small")
    small_sum = _sum_leading(small_all, "sum_small").reshape(-1)
    n_dmod = small_sizes[0]
    dmod_all = small_all.reshape(N_DEV, -1)[:, :n_dmod].reshape(N_DEV, L, NB, 6 * D)
    dmod_all = dmod_all.transpose(1, 0, 2, 3).reshape(L, B_all, 6 * D)
    off = n_dmod
    for name, sz in zip(small_names + ("loss",), small_sizes[1:]):
        grad[name] = small_sum[off:off + sz]
        off += sz
    loss_total = grad.pop("loss").reshape(())
    for name, ref in (("norm1_g", norm1_g), ("norm2_g", norm2_g), ("q_norm_a", q_norm_a),
                      ("k_norm_a", k_norm_a), ("sinks_a", sinks_a)):
        grad[name] = grad[name].reshape(ref.shape)

    n_shard = ada_w.shape[2]
    dmod_shard = lax.dynamic_slice_in_dim(dmod_all, chip * n_shard, n_shard, axis=2)
    grad["ada_w"], gb = _ada_bwd(c_all, dmod_all, dmod_shard)
    grad["ada_b"] = gb.reshape(ada_b.shape)

    weights = dict(ada_w=ada_w, ada_b=ada_b, norm1_g=norm1_g, norm2_g=norm2_g, wqkv_a=wqkv_a, q_norm_a=q_norm_a,
                   k_norm_a=k_norm_a, sinks_a=sinks_a, wo_a=wo_a, wqkv_b=wqkv_b, wo_b=wo_b, w_gate=w_gate,
                   w_up=w_up, w_down=w_down)
    m_in = dict(ada_w=m_ada_w, ada_b=m_ada_b, norm1_g=m_norm1_g, norm2_g=m_norm2_g, wqkv_a=m_wqkv_a,
                q_norm_a=m_q_norm_a, k_norm_a=m_k_norm_a, sinks_a=m_sinks_a, wo_a=m_wo_a, wqkv_b=m_wqkv_b,
                wo_b=m_wo_b, w_gate=m_w_gate, w_up=m_w_up, w_down=m_w_down)
    v_in = dict(ada_w=v_ada_w, ada_b=v_ada_b, norm1_g=v_norm1_g, norm2_g=v_norm2_g, wqkv_a=v_wqkv_a,
                q_norm_a=v_q_norm_a, k_norm_a=v_k_norm_a, sinks_a=v_sinks_a, wo_a=v_wo_a, wqkv_b=v_wqkv_b,
                wo_b=v_wo_b, w_gate=v_w_gate, w_up=v_w_up, w_down=v_w_down)
    names = list(weights)
    delta, new_m, new_v = {}, {}, {}
    for name in names:
        delta[name], new_m[name], new_v[name] = _adamw(weights[name], grad[name], m_in[name], v_in[name],
                                                       "adamw_" + name)
    return (loss_total, grad_x, *[grad[k] for k in names], *[delta[k] for k in names],
            *[new_m[k] for k in names], *[new_v[k] for k in names])
```

```python
import jax
import jax.numpy as jnp
from jax import lax
from jax.experimental import pallas as pl
from jax.experimental.pallas import tpu as pltpu

F32 = jnp.float32
BF16 = jnp.bfloat16

DEPTH = 4
HEAD_DIM = 64
N_Q_A = 16
N_KV_A = 2
GROUP_A = N_Q_A // N_KV_A
N_H_B = 16
BLOCK = 128
ROT_DIM = HEAD_DIM // 4
ROPE_THETA = 500000.0
EPS = 1e-6
ATTN_SCALE = HEAD_DIM ** -0.5
NEG_BIG = -1e30

ADAM_LR = 0.001
ADAM_B1 = 0.9
ADAM_B2 = 0.999
ADAM_EPS = 1e-08
ADAM_WD = 0.01
ADAM_STEP = 10

N_DEV = 8
LANES = 128
PACK_COLS = 1024
VMEM_LIMIT_BYTES = 48 * 1024 * 1024
MESH = pl.DeviceIdType.MESH

_NT = (((1,), (1,)), ((), ()))
_TN = (((0,), (0,)), ((), ()))
_NN = (((1,), (0,)), ((), ()))


def _params(sem=None, vmem_limit_bytes=VMEM_LIMIT_BYTES):
    return pltpu.CompilerParams(vmem_limit_bytes=vmem_limit_bytes, dimension_semantics=sem)


def _pick(n, cap, mult):
    best = None
    for t in range(mult, min(n, cap) + 1, mult):
        if n % t == 0:
            best = t
    return n if best is None else best


_ANY = pl.BlockSpec(memory_space=pl.ANY)


def _window(index, axis, q, n, shape):
    rest = [slice(None)] * len(shape)
    size = shape[axis] // n
    rest[axis] = pl.ds(q * size, size)
    return tuple(index) + tuple(rest)


def _all_gather8(xs, name, local_axis=0, local_chunks=1, relay_axis=None):
    n = len(xs)
    n_sems = 7 if relay_axis is None else 9

    def body(*refs):
        x_refs, out_refs = refs[:n], refs[n:2 * n]
        send_sems, recv_sems, local_sems = refs[2 * n:]
        xi, yi, ci = lax.axis_index("x"), lax.axis_index("y"), lax.axis_index("c")
        me, sibling = (xi, yi, ci), (xi, yi, 1 - ci)
        chips = [(1 - xi, yi), (xi, 1 - yi), (1 - xi, 1 - yi)]

        def slab(w, px, py, pc):
            return out_refs[w].at[4 * px + 2 * py + pc]

        def copy(w, k, block, to, src=None):
            return pltpu.make_async_remote_copy(
                src_ref=slab(w, *block) if src is None else src, dst_ref=slab(w, *block),
                send_sem=send_sems.at[k, w], recv_sem=recv_sems.at[k, w], device_id=to, device_id_type=MESH)

        mine = []
        for w in range(n):
            for q in range(local_chunks):
                part = _window((), local_axis, q, local_chunks, xs[w].shape)
                mine.append(pltpu.make_async_copy(x_refs[w].at[part], slab(w, *me).at[part], local_sems.at[w, q]))
                mine[-1].start()
        direct = chips if relay_axis is None else chips[:2]
        first = [copy(w, 0, me, sibling, src=x_refs[w]) for w in range(n)]
        first += [copy(w, 1 + j, me, (*chip, ci), src=x_refs[w]) for j, chip in enumerate(direct) for w in range(n)]
        for cp in first:
            cp.start()

        def relay(w, part, block, to):
            piece = _window((), relay_axis, part, 2, xs[w].shape)
            return pltpu.make_async_remote_copy(
                src_ref=slab(w, *block).at[piece], dst_ref=slab(w, *block).at[piece],
                send_sem=send_sems.at[7 + part, w], recv_sem=recv_sems.at[7 + part, w],
                device_id=to, device_id_type=MESH)

        passed = []
        for j, chip in enumerate(direct):
            for w in range(n):
                copy(w, 1 + j, (*chip, ci), me).wait_recv()
                passed.append(copy(w, 4 + j, (*chip, ci), sibling))
                passed[-1].start()
                if relay_axis is not None:
                    passed.append(relay(w, j, (*chip, ci), (*chips[1 - j], ci)))
                    passed[-1].start()
        if relay_axis is not None:
            for w in range(n):
                for part in range(2):
                    relay(w, part, (*chips[2], ci), me).wait_recv()
                passed.append(copy(w, 6, (*chips[2], ci), sibling))
                passed[-1].start()
        for w in range(n):
            copy(w, 0, sibling, me).wait_recv()
        for j, chip in enumerate(chips):
            for w in range(n):
                copy(w, 4 + j, (*chip, 1 - ci), me).wait_recv()
        for cp in first + passed:
            cp.wait_send()
        for cp in mine:
            cp.wait()

    return pl.pallas_call(
        body, name=name,
        out_shape=[jax.ShapeDtypeStruct((N_DEV,) + x.shape, x.dtype) for x in xs],
        in_specs=[_ANY] * n, out_specs=[_ANY] * n,
        scratch_shapes=[pltpu.SemaphoreType.DMA((n_sems, n)), pltpu.SemaphoreType.DMA((n_sems, n)),
                        pltpu.SemaphoreType.DMA((n, local_chunks))],
    )(*xs)


def _exchange_cores(xs, name, chunk_axis=0, chunks=1):
    n = len(xs)
    n_peers = 1

    def body(*refs):
        x_refs, out_refs = refs[:n], refs[n:2 * n]
        send_sems, recv_sems = refs[2 * n:]
        xi, yi, ci = lax.axis_index("x"), lax.axis_index("y"), lax.axis_index("c")
        peers = [(1 - ci, (xi, yi, 1 - ci))]
        copies = []
        for k, (p, dev) in enumerate(peers):
            for w in range(n):
                slab_shape = xs[w].shape[1:]
                for q in range(chunks):
                    copies.append(pltpu.make_async_remote_copy(
                        src_ref=x_refs[w].at[_window((p,), chunk_axis, q, chunks, slab_shape)],
                        dst_ref=out_refs[w].at[_window((k,), chunk_axis, q, chunks, slab_shape)],
                        send_sem=send_sems.at[k, w, q], recv_sem=recv_sems.at[k, w, q],
                        device_id=dev, device_id_type=MESH))
                    copies[-1].start()
        for cp in copies:
            cp.wait()

    return pl.pallas_call(
        body, name=name,
        out_shape=[jax.ShapeDtypeStruct((n_peers,) + x.shape[1:], x.dtype) for x in xs],
        in_specs=[_ANY] * n, out_specs=[_ANY] * n,
        scratch_shapes=[pltpu.SemaphoreType.DMA((n_peers, n, chunks)), pltpu.SemaphoreType.DMA((n_peers, n, chunks))],
    )(*xs)


def _exchange_chips(xs, name, relay_axis):
    n = len(xs)

    def half_shape(x):
        shape = list(x.shape[1:])
        shape[relay_axis] //= 2
        return tuple(shape)

    def body(*refs):
        x_refs, out_refs, hop_refs = refs[:n], refs[n:2 * n], refs[2 * n:3 * n]
        send_sems, recv_sems = refs[3 * n:]
        xi, yi, ci = lax.axis_index("x"), lax.axis_index("y"), lax.axis_index("c")
        nbr = [(1 - xi, yi, ci), (xi, 1 - yi, ci)]
        slab_of_nbr = [2 * (1 - xi) + yi, 2 * xi + (1 - yi)]
        slab_of_diag = 2 * (1 - xi) + (1 - yi)

        def copy(k, w, src, dst, to):
            return pltpu.make_async_remote_copy(src_ref=src, dst_ref=dst, send_sem=send_sems.at[k, w],
                                                recv_sem=recv_sems.at[k, w], device_id=to, device_id_type=MESH)

        def piece(w, part):
            return _window((), relay_axis, part, 2, xs[w].shape[1:])

        sent = []
        for w in range(n):
            for j in range(2):
                sent.append(copy(j, w, x_refs[w].at[slab_of_nbr[j]], out_refs[w].at[j], nbr[j]))
                sent.append(copy(2 + j, w, x_refs[w].at[(slab_of_diag,) + piece(w, j)], hop_refs[w].at[j], nbr[j]))
        for cp in sent:
            cp.start()
        for w in range(n):
            for j in range(2):
                copy(2 + j, w, hop_refs[w].at[j], hop_refs[w].at[j], nbr[j]).wait_recv()
                sent.append(copy(4 + j, w, hop_refs[w].at[j], out_refs[w].at[(2,) + piece(w, j)], nbr[1 - j]))
                sent[-1].start()
        for w in range(n):
            for j in range(2):
                copy(j, w, out_refs[w].at[j], out_refs[w].at[j], nbr[j]).wait_recv()
                half = out_refs[w].at[(2,) + piece(w, j)]
                copy(4 + j, w, half, half, nbr[1 - j]).wait_recv()
        for cp in sent:
            cp.wait_send()

    out = pl.pallas_call(
        body, name=name,
        out_shape=[jax.ShapeDtypeStruct((3,) + x.shape[1:], x.dtype) for x in xs]
        + [jax.ShapeDtypeStruct((2,) + half_shape(x), x.dtype) for x in xs],
        in_specs=[_ANY] * n, out_specs=[_ANY] * (2 * n),
        scratch_shapes=[pltpu.SemaphoreType.DMA((6, n)), pltpu.SemaphoreType.DMA((6, n))],
    )(*xs)
    return out[:n]


def _sibling_send(xs, name, chunk_axis=1, chunks=4):
    n = len(xs)

    def body(*refs):
        x_refs, out_refs = refs[:n], refs[n:2 * n]
        send_sems, recv_sems = refs[2 * n:]
        xi, yi, ci = lax.axis_index("x"), lax.axis_index("y"), lax.axis_index("c")
        copies = []
        for w in range(n):
            for q in range(chunks):
                part = _window((), chunk_axis, q, chunks, xs[w].shape)
                copies.append(pltpu.make_async_remote_copy(
                    src_ref=x_refs[w].at[part], dst_ref=out_refs[w].at[part],
                    send_sem=send_sems.at[w, q], recv_sem=recv_sems.at[w, q],
                    device_id=(xi, yi, 1 - ci), device_id_type=MESH))
                copies[-1].start()
        for cp in copies:
            cp.wait()

    return pl.pallas_call(
        body, name=name,
        out_shape=[jax.ShapeDtypeStruct(x.shape, x.dtype) for x in xs],
        in_specs=[_ANY] * n, out_specs=[_ANY] * n,
        scratch_shapes=[pltpu.SemaphoreType.DMA((n, chunks)), pltpu.SemaphoreType.DMA((n, chunks))],
    )(*xs)


def _sum_leading(x, name, own=None, with_bf16=False):
    P, R, C = x.shape
    tr = _pick(R, max(16, (1 << 19) // (C * (P + 1))), 16)

    def body(*refs):
        n_in = 1 if own is None else 2
        x_ref = refs[n_in - 1]
        acc = x_ref[0].astype(F32) if own is None else refs[0][...] + x_ref[0].astype(F32)
        for p in range(1, P):
            acc = acc + x_ref[p].astype(F32)
        refs[n_in][...] = acc
        if with_bf16:
            refs[n_in + 1][...] = acc.astype(BF16)

    flat = pl.BlockSpec((tr, C), lambda r: (r, 0))
    slabs = pl.BlockSpec((P, tr, C), lambda r: (0, r, 0))
    out = pl.pallas_call(
        body, name=name, grid=(R // tr,),
        in_specs=[slabs] if own is None else [flat, slabs],
        out_specs=[flat, flat] if with_bf16 else [flat],
        out_shape=[jax.ShapeDtypeStruct((R, C), F32)] + ([jax.ShapeDtypeStruct((R, C), BF16)] if with_bf16 else []),
        compiler_params=_params(("arbitrary",)),
    )(*([x] if own is None else [own, x]))
    return out if with_bf16 else out[0]


MATMUL_SINGLE_K = 1280
MATMUL_VMEM_BUDGET = 36 * 1024 * 1024


def _matmul(a, b, mode, out_dtype, name, swiglu=False):
    if mode == "nn":
        (M, K), N = a.shape, b.shape[1]
    elif mode == "nt":
        (M, K), N = a.shape, b.shape[0]
    else:
        (K, M), N = a.shape, b.shape[1]
    tm = _pick(M, 1024 if mode != "tn" else 1536, 128)
    tn = _pick(N, 1536, 128)
    if swiglu:
        tm, tn = _pick(M, 1024 if out_dtype == BF16 else 512, 128), 2 * _ff_tile(N // 2)
    out_bytes = jnp.dtype(out_dtype).itemsize
    tk = K
    if K > MATMUL_SINGLE_K:
        for cap in (2048, 1024, 512):
            tk = _pick(K, cap, 128)
            blocks = 2 * 2 * tk * (tm + tn) + tm * tn * (2 * out_bytes + (4 if out_dtype != F32 else 0))
            if blocks <= MATMUL_VMEM_BUDGET:
                break
    nk = K // tk
    dims = {"nn": _NN, "nt": _NT, "tn": _TN}[mode]
    use_scratch = nk > 1 and out_dtype != F32

    def body(a_ref, b_ref, *refs):
        o_ref = refs[0]

        def product():
            return lax.dot_general(a_ref[...].astype(BF16), b_ref[...].astype(BF16), dims,
                                   preferred_element_type=F32)

        if nk == 1:
            part = product()
            o_ref[...] = part.astype(o_ref.dtype)
            if swiglu:
                g = part[:, :tn // 2]
                refs[1][...] = (g * _sigmoid(g) * part[:, tn // 2:]).astype(BF16)
            return
        k = pl.program_id(2)
        acc_ref = refs[-1] if use_scratch else o_ref

        @pl.when(k == 0)
        def _():
            acc_ref[...] = jnp.zeros_like(acc_ref)

        acc_ref[...] += product()

        if use_scratch:
            @pl.when(k == nk - 1)
            def _():
                o_ref[...] = acc_ref[...].astype(o_ref.dtype)

    if mode == "tn":
        a_spec = pl.BlockSpec((tk, tm), lambda i, j, k: (k, i))
    else:
        a_spec = pl.BlockSpec((tm, tk), lambda i, j, k: (i, k))
    if mode == "nt":
        b_spec = pl.BlockSpec((tn, tk), lambda i, j, k: (j, k))
    else:
        b_spec = pl.BlockSpec((tk, tn), lambda i, j, k: (k, j))
    out_specs = [pl.BlockSpec((tm, tn), lambda i, j, k: (i, j))]
    out_shape = [jax.ShapeDtypeStruct((M, N), out_dtype)]
    if swiglu:
        assert nk == 1 and mode == "nn"
        out_specs.append(pl.BlockSpec((tm, tn // 2), lambda i, j, k: (i, j)))
        out_shape.append(jax.ShapeDtypeStruct((M, N // 2), BF16))
    out = pl.pallas_call(
        body, name=name, grid=(M // tm, N // tn, nk),
        in_specs=[a_spec, b_spec], out_specs=out_specs, out_shape=out_shape,
        scratch_shapes=[pltpu.VMEM((tm, tn), F32)] if use_scratch else [],
        compiler_params=_params(("parallel", "parallel", "arbitrary")),
    )(a, b)
    return out if swiglu else out[0]


def _row_tile(S):
    return _pick(S, 512, 8)


def _norm_mod_fwd(x, gain, sc, sh):
    NB, S, D = x.shape
    tr = _row_tile(S)

    def body(x_ref, g_ref, sc_ref, sh_ref, h_ref):
        xv = x_ref[...]
        ms = jnp.mean(xv * xv, axis=-1, keepdims=True)
        n = xv * lax.rsqrt(ms + EPS) * g_ref[...]
        h_ref[...] = (n * (1.0 + sc_ref[...]) + sh_ref[...]).astype(BF16)

    tok = pl.BlockSpec((None, tr, D), lambda b, r: (b, r, 0))
    per_ex = pl.BlockSpec((None, 1, D), lambda b, r: (b, 0, 0))
    return pl.pallas_call(
        body, name="norm_mod_fwd", grid=(NB, S // tr),
        in_specs=[tok, pl.BlockSpec((1, D), lambda b, r: (0, 0)), per_ex, per_ex],
        out_specs=tok, out_shape=jax.ShapeDtypeStruct((NB, S, D), BF16),
        compiler_params=_params(("parallel", "parallel")),
    )(x, gain, sc, sh)


def _norm_mod_bwd(a, w, x, gain, sc, dres, name, y=None, g=None):
    NB, S, D = x.shape
    T, K = a.shape
    tm = _pick(S, 512, 128)
    per_ex_tiles = S // tm
    tk = K if K <= MATMUL_SINGLE_K else _pick(K, 1536, 128)
    nk = K // tk
    gated = y is not None

    def body(*refs):
        a_ref, w_ref, x_ref, g_ref, sc_ref, dres_ref = refs[:6]
        refs = refs[6:]
        if gated:
            y_ref, gate_ref = refs[:2]
            refs = refs[2:]
        dx_ref, dsh_ref, dsc_ref, dgain_ref = refs[:4]
        acc_ref = refs[-1]
        i, k = pl.program_id(0), pl.program_id(1)

        @pl.when(k == 0)
        def _():
            acc_ref[...] = jnp.zeros_like(acc_ref)

        acc_ref[...] += lax.dot_general(a_ref[...], w_ref[...], _NT, preferred_element_type=F32)

        @pl.when(k == nk - 1)
        def _():
            first_of_example = i % per_ex_tiles == 0

            @pl.when(first_of_example)
            def _():
                dsh_ref[...] = jnp.zeros_like(dsh_ref)
                dsc_ref[...] = jnp.zeros_like(dsc_ref)
                if gated:
                    refs[5][...] = jnp.zeros_like(refs[5])

            @pl.when(i == 0)
            def _():
                dgain_ref[...] = jnp.zeros_like(dgain_ref)

            xv = x_ref[...]
            rstd = lax.rsqrt(jnp.mean(xv * xv, axis=-1, keepdims=True) + EPS)
            xh = xv * rstd
            gn = g_ref[...]
            dh = acc_ref[...]
            dsh_ref[...] += jnp.sum(dh, axis=0, keepdims=True)
            dsc_ref[...] += jnp.sum(dh * (xh * gn), axis=0, keepdims=True)
            dn = dh * (1.0 + sc_ref[...])
            dgain_ref[...] += jnp.sum(dn * xh, axis=0, keepdims=True)
            dxh = dn * gn
            proj = jnp.mean(dxh * xh, axis=-1, keepdims=True)
            dx = rstd * (dxh - xh * proj) + dres_ref[...]
            dx_ref[...] = dx
            if gated:
                refs[4][...] = (dx * gate_ref[...]).astype(BF16)
                refs[5][...] += jnp.sum(dx * y_ref[...], axis=0, keepdims=True)

    tok = pl.BlockSpec((None, tm, D), lambda i, k: (i // per_ex_tiles, i % per_ex_tiles, 0))
    per_ex = pl.BlockSpec((None, 1, D), lambda i, k: (i // per_ex_tiles, 0, 0))
    row = pl.BlockSpec((1, D), lambda i, k: (0, 0))
    in_specs = [pl.BlockSpec((tm, tk), lambda i, k: (i, k)), pl.BlockSpec((D, tk), lambda i, k: (0, k)),
                tok, row, per_ex, tok]
    out_specs = [tok, per_ex, per_ex, row]
    out_shape = [jax.ShapeDtypeStruct((NB, S, D), F32), jax.ShapeDtypeStruct((NB, 1, D), F32),
                 jax.ShapeDtypeStruct((NB, 1, D), F32), jax.ShapeDtypeStruct((1, D), F32)]
    operands = [a, w, x, gain, sc, dres]
    if gated:
        in_specs += [tok, per_ex]
        out_specs += [tok, per_ex]
        out_shape += [jax.ShapeDtypeStruct((NB, S, D), BF16), jax.ShapeDtypeStruct((NB, 1, D), F32)]
        operands += [y, g]
    return pl.pallas_call(
        body, name=name, grid=(T // tm, nk), in_specs=in_specs, out_specs=out_specs, out_shape=out_shape,
        scratch_shapes=[pltpu.VMEM((tm, D), F32)],
        compiler_params=_params(("arbitrary", "arbitrary")),
    )(*operands)


def _gate_res(x, y, g):
    NB, S, D = x.shape
    tr = _row_tile(S)

    def body(x_ref, y_ref, g_ref, o_ref):
        o_ref[...] = x_ref[...] + g_ref[...] * y_ref[...]

    tok = pl.BlockSpec((None, tr, D), lambda b, r: (b, r, 0))
    per_ex = pl.BlockSpec((None, 1, D), lambda b, r: (b, 0, 0))
    return pl.pallas_call(
        body, name="gate_res", grid=(NB, S // tr), in_specs=[tok, tok, per_ex], out_specs=tok,
        out_shape=jax.ShapeDtypeStruct((NB, S, D), F32),
        compiler_params=_params(("parallel", "parallel")),
    )(x, y, g)


def _gate_res_bwd(dxo, y, g):
    NB, S, D = dxo.shape
    tr = _row_tile(S)

    def body(d_ref, y_ref, g_ref, dy_ref, dg_ref):
        @pl.when(pl.program_id(1) == 0)
        def _():
            dg_ref[...] = jnp.zeros_like(dg_ref)

        d = d_ref[...]
        dy_ref[...] = (d * g_ref[...]).astype(BF16)
        dg_ref[...] += jnp.sum(d * y_ref[...], axis=0, keepdims=True)

    tok = pl.BlockSpec((None, tr, D), lambda b, r: (b, r, 0))
    per_ex = pl.BlockSpec((None, 1, D), lambda b, r: (b, 0, 0))
    return pl.pallas_call(
        body, name="gate_res_bwd", grid=(NB, S // tr), in_specs=[tok, tok, per_ex], out_specs=[tok, per_ex],
        out_shape=[jax.ShapeDtypeStruct((NB, S, D), BF16), jax.ShapeDtypeStruct((NB, 1, D), F32)],
        compiler_params=_params(("arbitrary", "arbitrary")),
    )(dxo, y, g)


def _sigmoid(v):
    return 1.0 / (1.0 + jnp.exp(-v))


def _ff_tile(F):
    return _pick(F, 1536, 128)


def _interleave(gate, up):
    F = gate.shape[-1]
    tf = _ff_tile(F)
    parts = []
    for j in range(F // tf):
        parts += [gate[..., j * tf:(j + 1) * tf], up[..., j * tf:(j + 1) * tf]]
    return jnp.concatenate(parts, axis=-1)


def _deinterleave(gu):
    F = gu.shape[-1] // 2
    tf = _ff_tile(F)
    gate = [gu[..., 2 * j * tf:(2 * j + 1) * tf] for j in range(F // tf)]
    up = [gu[..., (2 * j + 1) * tf:(2 * j + 2) * tf] for j in range(F // tf)]
    return jnp.concatenate(gate, axis=-1), jnp.concatenate(up, axis=-1)


def _swiglu_bwd(dm, wd, gu):
    T, D = dm.shape
    F = wd.shape[0]
    tf = _ff_tile(F)
    tm = _pick(T, 512, 128)
    assert D <= MATMUL_SINGLE_K

    def body(a_ref, b_ref, gu_ref, o_ref):
        d = lax.dot_general(a_ref[...], b_ref[...], _NT, preferred_element_type=F32)
        g, u = gu_ref[:, :tf].astype(F32), gu_ref[:, tf:].astype(F32)
        s = _sigmoid(g)
        o_ref[:, :tf] = (d * u * (s * (1.0 + g * (1.0 - s)))).astype(BF16)
        o_ref[:, tf:] = (d * (g * s)).astype(BF16)

    return pl.pallas_call(
        body, name="swiglu_bwd", grid=(T // tm, F // tf),
        in_specs=[pl.BlockSpec((tm, D), lambda i, j: (i, 0)), pl.BlockSpec((tf, D), lambda i, j: (j, 0)),
                  pl.BlockSpec((tm, 2 * tf), lambda i, j: (i, j))],
        out_specs=pl.BlockSpec((tm, 2 * tf), lambda i, j: (i, j)),
        out_shape=jax.ShapeDtypeStruct((T, 2 * F), BF16),
        compiler_params=_params(("parallel", "parallel")),
    )(dm, wd, gu)


def _loss_fwd_bwd(y, target):
    NB, S, D = y.shape
    tr = _row_tile(S)

    def body(y_ref, t_ref, l_ref, d_ref):
        @pl.when((pl.program_id(0) == 0) & (pl.program_id(1) == 0))
        def _():
            l_ref[...] = jnp.zeros_like(l_ref)

        e = y_ref[...] - t_ref[...]
        d_ref[...] = e / D
        l_ref[...] += 0.5 * jnp.sum(jnp.mean(e * e, axis=-1, keepdims=True), axis=0, keepdims=True)

    tok = pl.BlockSpec((None, tr, D), lambda b, r: (b, r, 0))
    return pl.pallas_call(
        body, name="loss", grid=(NB, S // tr), in_specs=[tok, tok],
        out_specs=[pl.BlockSpec((1, 1), lambda b, r: (0, 0)), tok],
        out_shape=[jax.ShapeDtypeStruct((1, 1), F32), jax.ShapeDtypeStruct((NB, S, D), F32)],
        compiler_params=_params(("arbitrary", "arbitrary")),
    )(y, target)


def _half_sums(v, lo):
    sa = jnp.sum(jnp.where(lo, v, 0.0), axis=-1, keepdims=True)
    sb = jnp.sum(jnp.where(lo, 0.0, v), axis=-1, keepdims=True)
    return jnp.where(lo, sa, sb)


def _rope_swap(v, lane64):
    up = pltpu.roll(v, LANES - ROT_DIM // 2, 1)
    down = pltpu.roll(v, ROT_DIM // 2, 1)
    return jnp.where(lane64 < ROT_DIM // 2, up, jnp.where(lane64 < ROT_DIM, down, 0.0))


def _qk_prep_fwd(qkv, tab_c, tab_s, gains):
    T, W = qkv.shape
    R = W // LANES
    tt = _pick(T, 256, 8)

    def body(x_ref, c_ref, s_ref, g_ref, o_ref):
        lane = lax.broadcasted_iota(jnp.int32, (tt, LANES), 1)
        lo = lane < HEAD_DIM
        lane64 = lane & (HEAD_DIM - 1)
        c, s = c_ref[...], s_ref[...]
        for j in range(R - 1):
            cols = slice(j * LANES, (j + 1) * LANES)
            xv = x_ref[:, cols]
            rstd = lax.rsqrt(_half_sums(xv * xv, lo) / HEAD_DIM + EPS)
            yn = xv * rstd * g_ref[j:j + 1, :]
            o_ref[:, cols] = (yn * c + _rope_swap(yn, lane64) * s).astype(BF16)
        o_ref[:, (R - 1) * LANES:] = x_ref[:, (R - 1) * LANES:].astype(BF16)

    tok = pl.BlockSpec((tt, W), lambda t: (t, 0))
    tab = pl.BlockSpec((tt, LANES), lambda t: (t, 0))
    return pl.pallas_call(
        body, name="qk_prep_fwd", grid=(T // tt,),
        in_specs=[tok, tab, tab, pl.BlockSpec((R, LANES), lambda t: (0, 0))],
        out_specs=tok, out_shape=jax.ShapeDtypeStruct((T, W), BF16),
        compiler_params=_params(("parallel",)),
    )(qkv, tab_c, tab_s, gains)


def _qk_prep_bwd(qkv, dq, dk, dv, tab_c, tab_s, gains):
    T, W = qkv.shape
    R = W // LANES
    QW = dq.shape[1]
    tt = _pick(T, 256, 8)

    def body(x_ref, dq_ref, dk_ref, dv_ref, c_ref, s_ref, g_ref, o_ref, dg_ref):
        @pl.when(pl.program_id(0) == 0)
        def _():
            dg_ref[...] = jnp.zeros_like(dg_ref)

        lane = lax.broadcasted_iota(jnp.int32, (tt, LANES), 1)
        lo = lane < HEAD_DIM
        lane64 = lane & (HEAD_DIM - 1)
        c, s = c_ref[...], s_ref[...]
        for j in range(R - 1):
            cols = slice(j * LANES, (j + 1) * LANES)
            xv = x_ref[:, cols]
            d = dq_ref[:, cols] if j < R - 2 else dk_ref[...]
            rstd = lax.rsqrt(_half_sums(xv * xv, lo) / HEAD_DIM + EPS)
            xh = xv * rstd
            dyn = d * c + _rope_swap(d * s, lane64)
            dg_ref[j:j + 1, :] += jnp.sum(dyn * xh, axis=0, keepdims=True)
            dxh = dyn * g_ref[j:j + 1, :]
            proj = _half_sums(dxh * xh, lo) / HEAD_DIM
            o_ref[:, cols] = (rstd * (dxh - xh * proj)).astype(BF16)
        o_ref[:, (R - 1) * LANES:] = dv_ref[...].astype(BF16)

    tok = pl.BlockSpec((tt, W), lambda t: (t, 0))
    tab = pl.BlockSpec((tt, LANES), lambda t: (t, 0))
    gsp = pl.BlockSpec((R, LANES), lambda t: (0, 0))
    return pl.pallas_call(
        body, name="qk_prep_bwd", grid=(T // tt,),
        in_specs=[tok, pl.BlockSpec((tt, QW), lambda t: (t, 0)), tab, tab, tab, tab, gsp], out_specs=[tok, gsp],
        out_shape=[jax.ShapeDtypeStruct((T, W), BF16), jax.ShapeDtypeStruct((R, LANES), F32)],
        compiler_params=_params(("arbitrary",)),
    )(qkv, dq, dk, dv, tab_c, tab_s, gains)


def _band_mask(i):
    r = lax.broadcasted_iota(jnp.int32, (2 * BLOCK, 2 * BLOCK), 0) & (BLOCK - 1)
    c = lax.broadcasted_iota(jnp.int32, (2 * BLOCK, 2 * BLOCK), 1)
    rel = r + BLOCK - c
    return (rel >= 0) & (rel < BLOCK) & ((c >= BLOCK) | (i > 0))


def _swa_softmax(s, valid, sink):
    s = jnp.where(valid, s * ATTN_SCALE, NEG_BIG)
    m = jnp.maximum(jnp.max(s, axis=1, keepdims=True), sink)
    p = jnp.exp(s - m)
    ps = jnp.exp(sink - m)
    denom = jnp.sum(p, axis=1, keepdims=True) + ps
    return p / denom, ps / denom


A_GROUP = 4


Q_WIDTH_A = N_Q_A * HEAD_DIM
N_PAIR_A = Q_WIDTH_A // LANES


def _swa_specs():
    qs = pl.BlockSpec((None, BLOCK, Q_WIDTH_A), lambda b, i: (b, i, 0))

    def kv(col, back):
        return pl.BlockSpec((None, BLOCK, LANES), lambda b, i: (b, jnp.maximum(i - back, 0), col))

    return qs, kv(N_PAIR_A, 1), kv(N_PAIR_A, 0), kv(N_PAIR_A + 1, 1), kv(N_PAIR_A + 1, 0)


def _dup_heads(t):
    lo = lax.broadcasted_iota(jnp.int32, t.shape, 1) < HEAD_DIM
    sw = pltpu.roll(t.astype(F32), HEAD_DIM, 1).astype(BF16)
    return jnp.where(lo, t, sw), jnp.where(lo, sw, t)


def _kv_tiles(kp_ref, kc_ref, vp_ref, vc_ref):
    kd = _dup_heads(jnp.concatenate([kp_ref[...], kc_ref[...]], axis=0))
    vd = _dup_heads(jnp.concatenate([vp_ref[...], vc_ref[...]], axis=0))
    return kd, vd


def _attn_a_fwd(qkn, sinks):
    NB, S, _ = qkn.shape
    qs, kp, kc, vp, vc = _swa_specs()

    def body(q_ref, kp_ref, kc_ref, vp_ref, vc_ref, sink_ref, o_ref):
        i = pl.program_id(1)
        kd, vd = _kv_tiles(kp_ref, kc_ref, vp_ref, vc_ref)
        valid = _band_mask(i)
        lo = lax.broadcasted_iota(jnp.int32, (BLOCK, LANES), 1) < HEAD_DIM
        top = lax.broadcasted_iota(jnp.int32, (2 * BLOCK, 1), 0) < BLOCK
        for first in range(0, N_PAIR_A, A_GROUP):
            pairs = range(first, first + A_GROUP)
            qs_ = [jnp.concatenate(_head_halves(q_ref[:, p * LANES:(p + 1) * LANES], lo), axis=0) for p in pairs]
            ss = [lax.dot_general(q, kd[2 * p // GROUP_A], _NT, preferred_element_type=F32) for q, p in zip(qs_, pairs)]
            pns = [_swa_softmax(s, valid, jnp.where(top, sink_ref[2 * p], sink_ref[2 * p + 1]))[0]
                   for s, p in zip(ss, pairs)]
            pvs = [jnp.dot(pn.astype(BF16), vd[2 * p // GROUP_A], preferred_element_type=F32) for pn, p in zip(pns, pairs)]
            for pv, p in zip(pvs, pairs):
                o_ref[:, p * LANES:(p + 1) * LANES] = jnp.where(lo, pv[:BLOCK], pv[BLOCK:]).astype(BF16)

    return pl.pallas_call(
        body, name="attn_a_fwd", grid=(NB, S // BLOCK),
        in_specs=[qs, kp, kc, vp, vc, pl.BlockSpec(memory_space=pltpu.SMEM)],
        out_specs=qs, out_shape=jax.ShapeDtypeStruct((NB, S, Q_WIDTH_A), BF16),
        compiler_params=_params(("parallel", "arbitrary")),
    )(qkn, qkn, qkn, qkn, qkn, sinks)


def _attn_a_bwd(qkn, do, sinks):
    NB, S, _ = qkn.shape
    qs, kp, kc, vp, vc = _swa_specs()
    full = pl.BlockSpec((None, S, LANES), lambda b, i: (b, 0, 0))
    sink_out = pl.BlockSpec((None, N_Q_A, LANES), lambda b, i: (b, 0, 0))

    def body(q_ref, do_ref, kp_ref, kc_ref, vp_ref, vc_ref, sink_ref, dq_ref, dk_ref, dv_ref, ds_ref, dk_s, dv_s):
        i = pl.program_id(1)

        @pl.when(i == 0)
        def _():
            dk_ref[...] = jnp.zeros_like(dk_ref)
            dv_ref[...] = jnp.zeros_like(dv_ref)
            ds_ref[...] = jnp.zeros_like(ds_ref)

        dk_s[...] = jnp.zeros_like(dk_s)
        dv_s[...] = jnp.zeros_like(dv_s)
        kd, vd = _kv_tiles(kp_ref, kc_ref, vp_ref, vc_ref)
        valid = _band_mask(i)
        lo = lax.broadcasted_iota(jnp.int32, (BLOCK, LANES), 1) < HEAD_DIM
        top = lax.broadcasted_iota(jnp.int32, (2 * BLOCK, 1), 0) < BLOCK
        for first in range(0, N_PAIR_A, A_GROUP):
            pairs = range(first, first + A_GROUP)
            kvs = [2 * p // GROUP_A for p in pairs]
            qs_ = [jnp.concatenate(_head_halves(q_ref[:, p * LANES:(p + 1) * LANES], lo), axis=0) for p in pairs]
            dos = [jnp.concatenate(_head_halves(do_ref[:, p * LANES:(p + 1) * LANES], lo), axis=0) for p in pairs]
            ss = [lax.dot_general(q, kd[kv], _NT, preferred_element_type=F32) for q, kv in zip(qs_, kvs)]
            dps = [lax.dot_general(d, vd[kv], _NT, preferred_element_type=F32) for d, kv in zip(dos, kvs)]
            sm = [_swa_softmax(s, valid, jnp.where(top, sink_ref[2 * p], sink_ref[2 * p + 1])) for s, p in zip(ss, pairs)]
            deltas = [jnp.sum(pn * dp, axis=1, keepdims=True) for (pn, _), dp in zip(sm, dps)]
            dsbs = [(pn * (dp - delta) * ATTN_SCALE).astype(BF16) for (pn, _), dp, delta in zip(sm, dps, deltas)]
            for n, p in enumerate(pairs):
                dq2 = jnp.dot(dsbs[n], kd[kvs[n]], preferred_element_type=F32)
                dq_ref[:, p * LANES:(p + 1) * LANES] = jnp.where(lo, dq2[:BLOCK], dq2[BLOCK:])
                dk_s[kvs[n]] += lax.dot_general(dsbs[n], qs_[n], _TN, preferred_element_type=F32)
                dv_s[kvs[n]] += lax.dot_general(sm[n][0].astype(BF16), dos[n], _TN, preferred_element_type=F32)
                t = sm[n][1] * deltas[n]
                for hh in range(2):
                    dsink = -jnp.sum(t[hh * BLOCK:(hh + 1) * BLOCK], axis=0, keepdims=True)
                    ds_ref[2 * p + hh:2 * p + hh + 1, :] += jnp.broadcast_to(dsink, (1, LANES))

        lo2 = lax.broadcasted_iota(jnp.int32, (2 * BLOCK, LANES), 1) < HEAD_DIM

        def fold(acc):
            halves = [acc[kv] + pltpu.roll(acc[kv], HEAD_DIM, 1) for kv in range(N_KV_A)]
            return jnp.where(lo2, halves[0], halves[1])

        dk2, dv2 = fold(dk_s), fold(dv_s)

        @pl.when(i > 0)
        def _():
            start = pl.multiple_of((i - 1) * BLOCK, BLOCK)
            dk_ref[pl.ds(start, 2 * BLOCK), :] += dk2
            dv_ref[pl.ds(start, 2 * BLOCK), :] += dv2

        @pl.when(i == 0)
        def _():
            dk_ref[0:BLOCK, :] += dk2[BLOCK:, :]
            dv_ref[0:BLOCK, :] += dv2[BLOCK:, :]

    slots = pltpu.VMEM((N_KV_A, 2 * BLOCK, LANES), F32)
    return pl.pallas_call(
        body, name="attn_a_bwd", grid=(NB, S // BLOCK),
        in_specs=[qs, qs, kp, kc, vp, vc, pl.BlockSpec(memory_space=pltpu.SMEM)],
        out_specs=[qs, full, full, sink_out],
        out_shape=[jax.ShapeDtypeStruct((NB, S, Q_WIDTH_A), F32), jax.ShapeDtypeStruct((NB, S, LANES), F32),
                   jax.ShapeDtypeStruct((NB, S, LANES), F32), jax.ShapeDtypeStruct((NB, N_Q_A, LANES), F32)],
        scratch_shapes=[slots, slots],
        compiler_params=_params(("parallel", "arbitrary")),
    )(qkn, do, qkn, qkn, qkn, qkn, sinks)


def _cumsum_mats():
    src = lax.broadcasted_iota(jnp.int32, (2 * BLOCK, 2 * BLOCK), 0) % BLOCK
    dst = lax.broadcasted_iota(jnp.int32, (2 * BLOCK, 2 * BLOCK), 1)
    ones = dst >= BLOCK
    rev = ((src > dst) | ones).astype(BF16)
    fwd = ((src < dst) | ones).astype(BF16)
    return rev, fwd


def _log_sigmoids(z):
    sp = jnp.log(1.0 + jnp.exp(-jnp.abs(z)))
    return jnp.minimum(z, 0.0) - sp, -(jnp.maximum(z, 0.0) + sp)


def _cumsum_mxu_many(vs, mat):
    parts = []
    for v in vs:
        hi = v.astype(BF16)
        parts.append(jnp.concatenate([hi, (v - hi.astype(F32)).astype(BF16)], axis=1))
    r = jnp.dot(jnp.concatenate(parts, axis=0), mat, preferred_element_type=F32)
    return [(r[n * BLOCK:(n + 1) * BLOCK, :BLOCK], r[n * BLOCK:(n + 1) * BLOCK, BLOCK:]) for n in range(len(vs))]


def _strict_mask():
    r = lax.broadcasted_iota(jnp.int32, (BLOCK, BLOCK), 0)
    c = lax.broadcasted_iota(jnp.int32, (BLOCK, BLOCK), 1)
    return c < r


def _tile(ref, j):
    return ref[pl.ds(pl.multiple_of(j * BLOCK, BLOCK), BLOCK), :]


SWEEP_EXIT = -88.0


def _head_halves(t, lo):
    zero = jnp.zeros_like(t)
    return jnp.where(lo, t, zero), jnp.where(lo, zero, t)


def _sb_specs(S, HD, width):
    n = HD // width
    blk = pl.BlockSpec((None, BLOCK, width), lambda b, p, i: (b, i, p))
    k_full = pl.BlockSpec((None, S, width), lambda b, p, i: (b, 0, n + p))
    v_full = pl.BlockSpec((None, S, width), lambda b, p, i: (b, 0, 2 * n + p))
    mat = pl.BlockSpec((2 * BLOCK, 2 * BLOCK), lambda b, p, i: (0, 0))
    return blk, k_full, v_full, mat


SB_FWD_PAIRS = 4
SB_BWD_PAIRS = 2
SB_BWD_TILES = 2
SB_BWD_VMEM_LIMIT_BYTES = 58 * 1024 * 1024


def _attn_b_fwd(qkv, rev):
    NB, S, W = qkv.shape
    HD = W // 3
    width = SB_FWD_PAIRS * LANES
    n_heads = 2 * SB_FWD_PAIRS
    blk, k_full, v_full, mat = _sb_specs(S, HD, width)

    def body(q_ref, k_ref, v_ref, rev_ref, o_ref):
        i = pl.program_id(2)
        rv = rev_ref[...]
        mask = _strict_mask()
        lo = lax.broadcasted_iota(jnp.int32, (BLOCK, LANES), 1) < HEAD_DIM
        q_all = q_ref[...]
        q_stack = [jnp.concatenate(_head_halves(q_all[:, p * LANES:(p + 1) * LANES] * ATTN_SCALE, lo), axis=0)
                   for p in range(SB_FWD_PAIRS)]

        def pair_tiles(ref, j):
            t = _tile(ref, j)
            return [t[:, p * LANES:(p + 1) * LANES] for p in range(SB_FWD_PAIRS)]

        def tile_pass(j, carries, diagonal):
            ks, vs = pair_tiles(k_ref, j), pair_tiles(v_ref, j)
            zs = []
            for p in range(SB_FWD_PAIRS):
                z2 = lax.dot_general(q_stack[p], ks[p], _NT, preferred_element_type=F32)
                zs += [z2[:BLOCK], z2[BLOCK:]]
            logs = [_log_sigmoids(z) for z in zs]
            cums = _cumsum_mxu_many([jnp.where(mask, lm, 0.0) if diagonal else lm for _, lm in logs], rv)
            probs, new_c = [], []
            for h in range(n_heads):
                after, rs = cums[h]
                if diagonal:
                    a = jnp.where(mask, jnp.exp(logs[h][0] + after), 0.0)
                    new_c.append(rs)
                else:
                    a = jnp.exp(logs[h][0] + after + carries[h])
                    new_c.append(carries[h] + rs)
                probs.append(a.astype(BF16))
            outs = []
            for p in range(SB_FWD_PAIRS):
                pv = jnp.dot(jnp.concatenate(probs[2 * p:2 * p + 2], axis=0), vs[p], preferred_element_type=F32)
                outs.append(jnp.where(lo, pv[:BLOCK], pv[BLOCK:]))
            return new_c, outs

        carries, accs = tile_pass(i, None, True)

        def live(cs):
            top = cs[0]
            for c in cs[1:]:
                top = jnp.maximum(top, c)
            return jnp.max(top) > SWEEP_EXIT

        def cond(st):
            return (st[0] < i) & st[1]

        def step(st):
            jj, _, cs, accs = st
            new_c, outs = tile_pass(i - 1 - jj, cs, False)
            return jj + 1, live(new_c), new_c, [acc + o for acc, o in zip(accs, outs)]

        st = lax.while_loop(cond, step, (jnp.int32(0), live(carries), carries, accs))
        for p in range(SB_FWD_PAIRS):
            o_ref[:, p * LANES:(p + 1) * LANES] = st[3][p].astype(BF16)

    return pl.pallas_call(
        body, name="attn_b_fwd", grid=(NB, HD // width, S // BLOCK),
        in_specs=[blk, k_full, v_full, mat], out_specs=blk,
        out_shape=jax.ShapeDtypeStruct((NB, S, HD), BF16),
        compiler_params=_params(("parallel", "parallel", "arbitrary")),
    )(qkv, qkv, qkv, rev)


def _attn_b_bwd(qkv, do, rev, fwd):
    NB, S, W = qkv.shape
    HD = W // 3
    width = SB_BWD_PAIRS * LANES
    n_heads = 2 * SB_BWD_PAIRS
    nj = S // BLOCK
    blk, k_full, v_full, mat = _sb_specs(S, HD, width)
    acc_full = pl.BlockSpec((None, S, width), lambda b, p, i: (b, 0, p))

    def body(q_ref, do_ref, k_ref, v_ref, rev_ref, fwd_ref, dq_ref, dk_ref, dv_ref, sig_s, a_s, e_s):
        i = pl.program_id(2)

        @pl.when(i == 0)
        def _():
            dk_ref[...] = jnp.zeros_like(dk_ref)
            dv_ref[...] = jnp.zeros_like(dv_ref)

        rv, fw = rev_ref[...], fwd_ref[...]
        mask = _strict_mask()
        lo = lax.broadcasted_iota(jnp.int32, (BLOCK, LANES), 1) < HEAD_DIM
        pairs = range(SB_BWD_PAIRS)

        def cols(p):
            return slice(p * LANES, (p + 1) * LANES)

        q_stack = [jnp.concatenate(_head_halves(q_ref[:, cols(p)], lo), axis=0) for p in pairs]
        qs_stack = [q * ATTN_SCALE for q in q_stack]
        do_stack = [jnp.concatenate(_head_halves(do_ref[:, cols(p)], lo), axis=0) for p in pairs]

        def sweep1_tiles(js, carries, diagonal):
            zs, das = [], []
            for j in js:
                kj, vj = _tile(k_ref, j), _tile(v_ref, j)
                for p in pairs:
                    z2 = lax.dot_general(qs_stack[p], kj[:, cols(p)], _NT, preferred_element_type=F32)
                    da2 = lax.dot_general(do_stack[p], vj[:, cols(p)], _NT, preferred_element_type=F32)
                    zs += [z2[:BLOCK], z2[BLOCK:]]
                    das += [da2[:BLOCK], da2[BLOCK:]]
            logs = [_log_sigmoids(z) for z in zs]
            cums = _cumsum_mxu_many([jnp.where(mask, lm, 0.0) if diagonal else lm for _, lm in logs], rv)
            new_c, stores = [], []
            for h in range(n_heads):
                carry = None if diagonal else carries[h]
                for t, j in enumerate(js):
                    n = t * n_heads + h
                    lb, (after, rs) = logs[n][0], cums[n]
                    if diagonal:
                        a = jnp.where(mask, jnp.exp(lb + after), 0.0)
                        carry = rs
                    else:
                        a = jnp.exp(lb + after + carry)
                        carry = carry + rs
                    stores.append((t, h, j, jnp.exp(lb), a.astype(BF16), das[n] * a))
                new_c.append(carry)
            for t, h, j, sg, ab, e in sorted(stores, key=lambda s: -s[0]):
                sig_s[h, j] = sg
                a_s[h, j] = ab
                e_s[h, j] = e
            return new_c

        carries = sweep1_tiles([i], None, True)

        def live(cs):
            top = cs[0]
            for c in cs[1:]:
                top = jnp.maximum(top, c)
            return jnp.max(top) > SWEEP_EXIT

        def cond(st):
            return (SB_BWD_TILES * st[0] < i) & st[1]

        def sweep1(st):
            first = i - 1 - SB_BWD_TILES * st[0]
            new_c = sweep1_tiles([jnp.maximum(first - t, 0) for t in range(SB_BWD_TILES)], st[2], False)
            return st[0] + 1, live(new_c), new_c

        trips = lax.while_loop(cond, sweep1, (jnp.int32(0), live(carries), carries))[0]
        lowest = jnp.maximum(i - SB_BWD_TILES * trips, 0)

        def grads(js, st, diagonal):
            prefixes, dqs = st
            es = [e_s[h, j] for j in js for h in range(n_heads)]
            cums = _cumsum_mxu_many(es, fw)
            dzs, new_p = [], []
            for h in range(n_heads):
                prefix = prefixes[h]
                for t, j in enumerate(js):
                    n = t * n_heads + h
                    sg = sig_s[h, j]
                    e_before, rs = cums[n]
                    dz = (es[n] * (1.0 - sg) - (e_before + prefix) * sg) * ATTN_SCALE
                    if diagonal:
                        dz = jnp.where(mask, dz, 0.0)
                    dzs.append((t, h, dz.astype(BF16)))
                    prefix = prefix + rs
                new_p.append(prefix)
            dz_of = {(t, h): dz for t, h, dz in dzs}
            new_dq = list(dqs)
            for t, j in enumerate(js):
                kj = _tile(k_ref, j)
                rows = pl.ds(pl.multiple_of(j * BLOCK, BLOCK), BLOCK)
                for p in pairs:
                    dz_stack = jnp.concatenate([dz_of[t, 2 * p], dz_of[t, 2 * p + 1]], axis=0)
                    a_stack = jnp.concatenate([a_s[2 * p, j], a_s[2 * p + 1, j]], axis=0)
                    dq2 = jnp.dot(dz_stack, kj[:, cols(p)], preferred_element_type=F32)
                    new_dq[p] = new_dq[p] + jnp.where(lo, dq2[:BLOCK], dq2[BLOCK:])
                    dk_ref[rows, cols(p)] += lax.dot_general(dz_stack, q_stack[p], _TN, preferred_element_type=F32)
                    dv_ref[rows, cols(p)] += lax.dot_general(a_stack, do_stack[p], _TN, preferred_element_type=F32)
            return new_p, new_dq

        zeros = jnp.zeros((BLOCK, BLOCK), F32)
        st = ([zeros] * n_heads, [zeros] * SB_BWD_PAIRS)
        count = i - lowest
        st = lax.fori_loop(0, count % SB_BWD_TILES, lambda t, st: grads([lowest + t], st, False), st)
        start = lowest + count % SB_BWD_TILES
        st = lax.fori_loop(0, count // SB_BWD_TILES,
                           lambda t, st: grads([start + SB_BWD_TILES * t + u for u in range(SB_BWD_TILES)], st, False), st)
        dqs = grads([i], st, True)[1]
        for p in pairs:
            dq_ref[:, cols(p)] = dqs[p]

    f32_stash = pltpu.VMEM((n_heads, nj, BLOCK, BLOCK), F32)
    bf16_stash = pltpu.VMEM((n_heads, nj, BLOCK, BLOCK), BF16)
    return pl.pallas_call(
        body, name="attn_b_bwd", grid=(NB, HD // width, nj),
        in_specs=[blk, blk, k_full, v_full, mat, mat], out_specs=[blk, acc_full, acc_full],
        out_shape=[jax.ShapeDtypeStruct((NB, S, HD), F32)] * 3,
        scratch_shapes=[f32_stash, bf16_stash, f32_stash],
        compiler_params=_params(("parallel", "parallel", "arbitrary"), SB_BWD_VMEM_LIMIT_BYTES),
    )(qkv, do, qkv, qkv, rev, fwd)


def _ada_fwd(c_all, w, b):
    L, D, N = w.shape
    B = c_all.shape[0]

    def body(c_ref, w_ref, b_ref, o_ref):
        cv = c_ref[...]
        cond = (cv * _sigmoid(cv)).astype(BF16)
        o_ref[...] = jnp.dot(cond, w_ref[...].astype(BF16), preferred_element_type=F32) + b_ref[...]

    return pl.pallas_call(
        body, name="ada_fwd", grid=(L,),
        in_specs=[pl.BlockSpec((B, D), lambda l: (0, 0)), pl.BlockSpec((None, D, N), lambda l: (l, 0, 0)),
                  pl.BlockSpec((None, 1, N), lambda l: (l, 0, 0))],
        out_specs=pl.BlockSpec((None, B, N), lambda l: (l, 0, 0)),
        out_shape=jax.ShapeDtypeStruct((L, B, N), F32),
        compiler_params=_params(("parallel",)),
    )(c_all, w, b)


def _ada_bwd(c_all, dmod_all, dmod_shard):
    L, B, N = dmod_shard.shape
    D = c_all.shape[1]
    N_all = dmod_all.shape[2]

    def body(c_ref, da_ref, ds_ref, gw_ref, gb_ref):
        cv = c_ref[...]
        cond = (cv * _sigmoid(cv)).astype(BF16)
        gw_ref[...] = lax.dot_general(cond, ds_ref[...].astype(BF16), _TN, preferred_element_type=F32)
        gb_ref[...] = jnp.sum(da_ref[...], axis=0, keepdims=True)

    return pl.pallas_call(
        body, name="ada_bwd", grid=(L,),
        in_specs=[pl.BlockSpec((B, D), lambda l: (0, 0)), pl.BlockSpec((None, B, N_all), lambda l: (l, 0, 0)),
                  pl.BlockSpec((None, B, N), lambda l: (l, 0, 0))],
        out_specs=[pl.BlockSpec((None, D, N), lambda l: (l, 0, 0)), pl.BlockSpec((None, 1, N_all), lambda l: (l, 0, 0))],
        out_shape=[jax.ShapeDtypeStruct((L, D, N), F32), jax.ShapeDtypeStruct((L, 1, N_all), F32)],
        compiler_params=_params(("parallel",)),
    )(c_all, dmod_all, dmod_shard)


def _adamw(w, g, m, v, name):
    shape = w.shape
    C = shape[-1]
    R = w.size // C
    tr = _pick(R, max(8, (1 << 18) // C), 8)
    c1 = 1.0 - ADAM_B1 ** ADAM_STEP
    c2 = 1.0 - ADAM_B2 ** ADAM_STEP

    def body(w_ref, g_ref, m_ref, v_ref, d_ref, nm_ref, nv_ref):
        gv = g_ref[...]
        nm = ADAM_B1 * m_ref[...] + (1.0 - ADAM_B1) * gv
        nv = ADAM_B2 * v_ref[...] + (1.0 - ADAM_B2) * (gv * gv)
        d_ref[...] = -ADAM_LR * ((nm / c1) / (jnp.sqrt(nv / c2) + ADAM_EPS) + ADAM_WD * w_ref[...])
        nm_ref[...] = nm
        nv_ref[...] = nv

    spec = pl.BlockSpec((tr, C), lambda r: (r, 0))
    out = pl.pallas_call(
        body, name=name, grid=(R // tr,), in_specs=[spec] * 4, out_specs=[spec] * 3,
        out_shape=[jax.ShapeDtypeStruct((R, C), F32)] * 3,
        compiler_params=_params(("parallel",)),
    )(*[t.reshape(R, C) for t in (w, g, m, v)])
    return [t.reshape(shape) for t in out]


_SHARDED = (("wqkv_a", 2), ("wo_a", 1), ("wqkv_b", 2), ("wo_b", 1), ("w_gate", 2), ("w_up", 2), ("w_down", 1))


def _pack_full(layers, axis, gate_up=None):
    L = len(layers)
    R, C = layers[0].shape

    def shards(m):
        if gate_up is not None:
            F = C // 2
            tf, Cs = _ff_tile(F), F // 4
            assert tf % Cs == 0
            starts = [(2 * (s * Cs // tf) + gate_up) * tf + s * Cs % tf for s in range(4)]
            return jnp.stack([m[:, st:st + Cs] for st in starts])
        if axis == 2:
            return m.reshape(R, 4, C // 4).transpose(1, 0, 2)
        return m.reshape(4, R // 4, C)

    halves = [jnp.stack([shards(m) for m in layers[h * (L // 2):(h + 1) * (L // 2)]], axis=1) for h in range(2)]
    return jnp.stack(halves)


def _unpack_full(gathered, axis):
    _, Lh, Rs, Cs = gathered.shape
    t = gathered.reshape(4, 2, Lh, Rs, Cs)
    layers = []
    for h in range(2):
        for l in range(Lh):
            piece = t[:, h, l]
            if axis == 2:
                layers.append(piece.transpose(1, 0, 2).reshape(Rs, 4 * Cs))
            else:
                layers.append(piece.reshape(4 * Rs, Cs))
    return layers


def _sum_slabs(own, recv, name, with_bf16=False):
    C = own.shape[-1]
    out = _sum_leading(recv.reshape(recv.shape[0], -1, C), name, own=own.reshape(-1, C), with_bf16=with_bf16)
    if with_bf16:
        return out[0].reshape(own.shape), out[1].reshape(own.shape)
    return out.reshape(own.shape)


def _gather8(x, name):
    return _all_gather8([x], name)[0]


def _rope_tables(positions):
    half = ROT_DIM // 2
    inv_freq = jnp.power(jnp.float32(ROPE_THETA), -jnp.arange(half, dtype=F32) * 2.0 / ROT_DIM)
    ang = positions.astype(F32).reshape(-1, 1) * inv_freq
    cos, sin = jnp.cos(ang), jnp.sin(ang)
    T = ang.shape[0]
    rest = HEAD_DIM - ROT_DIM
    c64 = jnp.concatenate([cos, cos, jnp.ones((T, rest), F32)], axis=1)
    s64 = jnp.concatenate([-sin, sin, jnp.zeros((T, rest), F32)], axis=1)
    return jnp.tile(c64, (1, 2)), jnp.tile(s64, (1, 2))


def _gain_rows(q_gain, k_gain):
    q2 = jnp.tile(q_gain.reshape(1, HEAD_DIM), (GROUP_A, 2))
    k2 = jnp.tile(k_gain.reshape(1, HEAD_DIM), (1, 2))
    return jnp.concatenate([q2, k2, jnp.ones((1, LANES), F32)], axis=0)


def _local_step(x, positions, mod, norm1_g, norm2_g, q_norm_a, k_norm_a, sinks_a,
                wqkv_a, wo_a, wqkv_b, wo_b, wgu, wd, loss_target):
    NB, S, D = x.shape
    T = NB * S
    QA = N_Q_A * HEAD_DIM
    tab_c, tab_s = _rope_tables(positions)
    rev, fwd = _cumsum_mats()

    saved = []
    xc = x
    for i in range(DEPTH):
        j = i // 2
        sh1, sc1, g1, sh2, sc2, g2 = [mod[i][:, k * D:(k + 1) * D].reshape(NB, 1, D) for k in range(6)]
        st = dict(x=xc, sc1=sc1, g1=g1, sc2=sc2, g2=g2)
        h = _norm_mod_fwd(xc, norm1_g[i:i + 1], sc1, sh1)
        st["h"] = h.reshape(T, D)
        if i % 2 == 0:
            st["qkv"] = _matmul(st["h"], wqkv_a[j], "nn", F32, "qkv_a")
            st["gains"] = _gain_rows(q_norm_a[j], k_norm_a[j])
            st["qkn"] = _qk_prep_fwd(st["qkv"], tab_c, tab_s, st["gains"]).reshape(NB, S, -1)
            st["o"] = _attn_a_fwd(st["qkn"], sinks_a[j]).reshape(T, QA)
            y = _matmul(st["o"], wo_a[j], "nn", F32, "wo_a")
        else:
            st["qkv"] = _matmul(st["h"], wqkv_b[j], "nn", BF16, "qkv_b").reshape(NB, S, -1)
            st["o"] = _attn_b_fwd(st["qkv"], rev).reshape(T, N_H_B * HEAD_DIM)
            y = _matmul(st["o"], wo_b[j], "nn", F32, "wo_b")
        st["y"] = y.reshape(NB, S, D)
        x1 = _gate_res(xc, st["y"], g1)
        st["x1"] = x1
        h2 = _norm_mod_fwd(x1, norm2_g[i:i + 1], sc2, sh2)
        st["h2"] = h2.reshape(T, D)
        st["gu"], st["act"] = _matmul(st["h2"], wgu[i], "nn", BF16, "gate_up", swiglu=True)
        st["m"] = _matmul(st["act"], wd[i], "nn", F32, "down").reshape(NB, S, D)
        xc = _gate_res(x1, st["m"], g2)
        saved.append(st)

    loss, dx = _loss_fwd_bwd(xc, loss_target)

    grads = {name: [None] * n for name, n in
             (("wqkv_a", 2), ("wo_a", 2), ("wqkv_b", 2), ("wo_b", 2), ("wgu", DEPTH), ("wd", DEPTH),
              ("norm1_g", DEPTH), ("norm2_g", DEPTH), ("q_norm_a", 2), ("k_norm_a", 2), ("sinks_a", 2))}
    dmod = [None] * DEPTH
    dm, dg2 = _gate_res_bwd(dx, saved[-1]["m"], saved[-1]["g2"])
    for i in reversed(range(DEPTH)):
        j = i // 2
        st = saved[i]
        dm = dm.reshape(T, D)
        grads["wd"][i] = _matmul(st["act"], dm, "tn", F32, "d_wd")
        dgu = _swiglu_bwd(dm, wd[i], st["gu"])
        grads["wgu"][i] = _matmul(st["h2"], dgu, "tn", F32, "d_wgu")
        dx1, dsh2, dsc2, grads["norm2_g"][i], dy, dg1 = _norm_mod_bwd(
            dgu, wgu[i], st["x1"], norm2_g[i:i + 1], st["sc2"], dx, "d_h2", y=st["y"], g=st["g1"])
        dy = dy.reshape(T, D)
        if i % 2 == 0:
            do = _matmul(dy, wo_a[j], "nt", BF16, "d_o_a").reshape(NB, S, QA)
            grads["wo_a"][j] = _matmul(st["o"], dy, "tn", F32, "d_wo_a")
            dq, dk, dv, dsink = _attn_a_bwd(st["qkn"], do, sinks_a[j])
            dqkv, dgain = _qk_prep_bwd(st["qkv"], dq.reshape(T, QA), dk.reshape(T, LANES), dv.reshape(T, LANES),
                                       tab_c, tab_s, st["gains"])
            w_in = wqkv_a[j]
            grads["wqkv_a"][j] = _matmul(st["h"], dqkv, "tn", F32, "d_wqkv_a")
            grads["q_norm_a"][j] = jnp.sum(dgain[:GROUP_A].reshape(2 * GROUP_A, HEAD_DIM), axis=0)
            grads["k_norm_a"][j] = jnp.sum(dgain[GROUP_A].reshape(2, HEAD_DIM), axis=0)
            grads["sinks_a"][j] = jnp.sum(dsink[..., 0], axis=0)
        else:
            do = _matmul(dy, wo_b[j], "nt", BF16, "d_o_b").reshape(NB, S, -1)
            grads["wo_b"][j] = _matmul(st["o"], dy, "tn", F32, "d_wo_b")
            dq, dk, dv = _attn_b_bwd(st["qkv"], do, rev, fwd)
            dqkv = jnp.concatenate([dq, dk, dv], axis=-1).reshape(T, -1).astype(BF16)
            w_in = wqkv_b[j]
            grads["wqkv_b"][j] = _matmul(st["h"], dqkv, "tn", F32, "d_wqkv_b")
        this_dg2 = dg2
        if i > 0:
            dx, dsh1, dsc1, grads["norm1_g"][i], dm, dg2 = _norm_mod_bwd(
                dqkv, w_in, st["x"], norm1_g[i:i + 1], st["sc1"], dx1, "d_h", y=saved[i - 1]["m"], g=saved[i - 1]["g2"])
        else:
            dx, dsh1, dsc1, grads["norm1_g"][i] = _norm_mod_bwd(
                dqkv, w_in, st["x"], norm1_g[i:i + 1], st["sc1"], dx1, "d_h")
        dmod[i] = jnp.concatenate([dsh1, dsc1, dg1, dsh2, dsc2, this_dg2], axis=-1).reshape(NB, 6 * D)

    matrices = ("wqkv_a", "wo_a", "wqkv_b", "wo_b", "wgu", "wd")
    grads = {name: parts if name in matrices else jnp.stack(parts) for name, parts in grads.items()}
    return loss, dx, grads, jnp.stack(dmod)


def _rows_of(flat, cols=PACK_COLS):
    n = flat.shape[0]
    pad = (-n) % (8 * cols)
    if pad:
        flat = jnp.concatenate([flat, jnp.zeros((pad,), flat.dtype)])
    return flat.reshape(-1, cols)


def kernel(x, c, positions, ada_w, ada_b, norm1_g, norm2_g, wqkv_a, q_norm_a, k_norm_a, sinks_a, wo_a, wqkv_b, wo_b, w_gate, w_up, w_down, loss_target, m_ada_w, m_ada_b, m_norm1_g, m_norm2_g, m_wqkv_a, m_q_norm_a, m_k_norm_a, m_sinks_a, m_wo_a, m_wqkv_b, m_wo_b, m_w_gate, m_w_up, m_w_down, v_ada_w, v_ada_b, v_norm1_g, v_norm2_g, v_wqkv_a, v_q_norm_a, v_k_norm_a, v_sinks_a, v_wo_a, v_wqkv_b, v_wo_b, v_w_gate, v_w_up, v_w_down):
    xi, yi, ci = lax.axis_index("x"), lax.axis_index("y"), lax.axis_index("c")
    dev = 4 * xi + 2 * yi + ci
    chip = 2 * xi + yi
    NB, S, D = x.shape
    B_all = N_DEV * NB
    L = ada_w.shape[0]
    n_mod = ada_w.shape[2] // 2

    c_all = _gather8(_rows_of(c.reshape(-1), LANES), "gather_c").reshape(N_DEV, -1)[:, :NB * D].reshape(B_all, D)
    ada_w_half = lax.dynamic_slice_in_dim(ada_w, ci * n_mod, n_mod, axis=2)
    ada_b_half = lax.dynamic_slice_in_dim(ada_b, dev * n_mod, n_mod, axis=1).reshape(L, 1, n_mod)
    mod_part = _ada_fwd(c_all, ada_w_half, ada_b_half)
    n_part = L * B_all * n_mod
    mod_all = _gather8(_rows_of(mod_part.reshape(-1)), "gather_mod").reshape(N_DEV, -1)[:, :n_part]
    mod_all = mod_all.reshape(N_DEV, L, B_all, n_mod).transpose(1, 2, 0, 3).reshape(L, B_all, N_DEV * n_mod)
    mod = lax.dynamic_slice_in_dim(mod_all, dev * NB, NB, axis=1)

    shards = dict(wqkv_a=wqkv_a, wo_a=wo_a, wqkv_b=wqkv_b, wo_b=wo_b, w_gate=w_gate, w_up=w_up, w_down=w_down)
    halves = []
    for name, _ in _SHARDED:
        w = shards[name]
        half = lax.dynamic_index_in_dim(w.reshape((2, w.shape[0] // 2) + w.shape[1:]), ci, 0, keepdims=False)
        halves.append(half.astype(BF16))
    gathered = _all_gather8(halves, "gather_weights", local_axis=1, local_chunks=8, relay_axis=1)
    full = {name: _unpack_full(t, axis) for (name, axis), t in zip(_SHARDED, gathered)}
    wgu = [_interleave(gate, up) for gate, up in zip(full["w_gate"], full["w_up"])]

    loss, grad_x, g, dmod = _local_step(
        x, positions, mod, norm1_g, norm2_g, q_norm_a, k_norm_a, sinks_a,
        full["wqkv_a"], full["wo_a"], full["wqkv_b"], full["wo_b"], wgu, full["w_down"], loss_target)

    g_full = dict(wqkv_a=g["wqkv_a"], wo_a=g["wo_a"], wqkv_b=g["wqkv_b"], wo_b=g["wo_b"],
                  w_gate=g["wgu"], w_up=g["wgu"], w_down=g["wd"])
    which = dict(w_gate=0, w_up=1)
    packed = [_pack_full(g_full[name], axis, which.get(name)) for name, axis in _SHARDED]
    def own(t, index):
        return lax.dynamic_index_in_dim(t, index, 0, keepdims=False)

    from_cores = _exchange_cores(packed, "rs_cores", chunk_axis=0, chunks=4)
    chip_part = [_sum_slabs(own(p, ci), r, "rs_add_cores", with_bf16=True) for p, r in zip(packed, from_cores)]
    from_chips = _exchange_chips([b for _, b in chip_part], "rs_chips", relay_axis=1)
    mine = [_sum_slabs(own(p, chip), r, "rs_add_chips") for (p, _), r in zip(chip_part, from_chips)]
    theirs = _sibling_send(mine, "rs_halves")
    grad = {}
    for (name, _), m, t in zip(_SHARDED, mine, theirs):
        first, second = jnp.where(ci == 0, m, t), jnp.where(ci == 0, t, m)
        grad[name] = jnp.stack([first, second]).reshape(shards[name].shape)

    small_names = ("norm1_g", "norm2_g", "q_norm_a", "k_norm_a", "sinks_a")
    small = [dmod.reshape(-1)] + [g[name].reshape(-1) for name in small_names] + [loss.reshape(-1)]
    small_sizes = [t.shape[0] for t in small]
    small_rows = _rows_of(jnp.concatenate(small))
    small_all = _gather8(small_rows, "gather_small")
    small_sum = _sum_leading(small_all, "sum_small").reshape(-1)
    n_dmod = small_sizes[0]
    dmod_all = small_all.reshape(N_DEV, -1)[:, :n_dmod].reshape(N_DEV, L, NB, 6 * D)
    dmod_all = dmod_all.transpose(1, 0, 2, 3).reshape(L, B_all, 6 * D)
    off = n_dmod
    for name, sz in zip(small_names + ("loss",), small_sizes[1:]):
        grad[name] = small_sum[off:off + sz]
        off += sz
    loss_total = grad.pop("loss").reshape(())
    for name, ref in (("norm1_g", norm1_g), ("norm2_g", norm2_g), ("q_norm_a", q_norm_a),
                      ("k_norm_a", k_norm_a), ("sinks_a", sinks_a)):
        grad[name] = grad[name].reshape(ref.shape)

    n_shard = ada_w.shape[2]
    dmod_shard = lax.dynamic_slice_in_dim(dmod_all, chip * n_shard, n_shard, axis=2)
    grad["ada_w"], gb = _ada_bwd(c_all, dmod_all, dmod_shard)
    grad["ada_b"] = gb.reshape(ada_b.shape)

    weights = dict(ada_w=ada_w, ada_b=ada_b, norm1_g=norm1_g, norm2_g=norm2_g, wqkv_a=wqkv_a, q_norm_a=q_norm_a,
                   k_norm_a=k_norm_a, sinks_a=sinks_a, wo_a=wo_a, wqkv_b=wqkv_b, wo_b=wo_b, w_gate=w_gate,
                   w_up=w_up, w_down=w_down)
    m_in = dict(ada_w=m_ada_w, ada_b=m_ada_b, norm1_g=m_norm1_g, norm2_g=m_norm2_g, wqkv_a=m_wqkv_a,
                q_norm_a=m_q_norm_a, k_norm_a=m_k_norm_a, sinks_a=m_sinks_a, wo_a=m_wo_a, wqkv_b=m_wqkv_b,
                wo_b=m_wo_b, w_gate=m_w_gate, w_up=m_w_up, w_down=m_w_down)
    v_in = dict(ada_w=v_ada_w, ada_b=v_ada_b, norm1_g=v_norm1_g, norm2_g=v_norm2_g, wqkv_a=v_wqkv_a,
                q_norm_a=v_q_norm_a, k_norm_a=v_k_norm_a, sinks_a=v_sinks_a, wo_a=v_wo_a, wqkv_b=v_wqkv_b,
                wo_b=v_wo_b, w_gate=v_w_gate, w_up=v_w_up, w_down=v_w_down)
    names = list(weights)
    delta, new_m, new_v = {}, {}, {}
    for name in names:
        delta[name], new_m[name], new_v[name] = _adamw(weights[name], grad[name], m_in[name], v_in[name],
                                                       "adamw_" + name)
    return (loss_total, grad_x, *[grad[k] for k in names], *[delta[k] for k in names],
            *[new_m[k] for k in names], *[new_v[k] for k in names])
```

```python
import jax
import jax.numpy as jnp
from jax import lax
from jax.experimental import pallas as pl
from jax.experimental.pallas import tpu as pltpu

F32 = jnp.float32
BF16 = jnp.bfloat16

DEPTH = 4
HEAD_DIM = 64
N_Q_A = 16
N_KV_A = 2
GROUP_A = N_Q_A // N_KV_A
N_H_B = 16
BLOCK = 128
ROT_DIM = HEAD_DIM // 4
ROPE_THETA = 500000.0
EPS = 1e-6
ATTN_SCALE = HEAD_DIM ** -0.5
NEG_BIG = -1e30

ADAM_LR = 0.001
ADAM_B1 = 0.9
ADAM_B2 = 0.999
ADAM_EPS = 1e-08
ADAM_WD = 0.01
ADAM_STEP = 10

N_DEV = 8
LANES = 128
PACK_COLS = 1024
VMEM_LIMIT_BYTES = 48 * 1024 * 1024
MESH = pl.DeviceIdType.MESH

_NT = (((1,), (1,)), ((), ()))
_TN = (((0,), (0,)), ((), ()))
_NN = (((1,), (0,)), ((), ()))


def _params(sem=None, vmem_limit_bytes=VMEM_LIMIT_BYTES):
    return pltpu.CompilerParams(vmem_limit_bytes=vmem_limit_bytes, dimension_semantics=sem)


def _pick(n, cap, mult):
    best = None
    for t in range(mult, min(n, cap) + 1, mult):
        if n % t == 0:
            best = t
    return n if best is None else best


_ANY = pl.BlockSpec(memory_space=pl.ANY)


def _window(index, axis, q, n, shape):
    rest = [slice(None)] * len(shape)
    size = shape[axis] // n
    rest[axis] = pl.ds(q * size, size)
    return tuple(index) + tuple(rest)


def _all_gather8(xs, name, local_axis=0, local_chunks=1, relay_axis=None):
    n = len(xs)
    n_sems = 7 if relay_axis is None else 9

    def body(*refs):
        x_refs, out_refs = refs[:n], refs[n:2 * n]
        send_sems, recv_sems, local_sems = refs[2 * n:]
        xi, yi, ci = lax.axis_index("x"), lax.axis_index("y"), lax.axis_index("c")
        me, sibling = (xi, yi, ci), (xi, yi, 1 - ci)
        chips = [(1 - xi, yi), (xi, 1 - yi), (1 - xi, 1 - yi)]

        def slab(w, px, py, pc):
            return out_refs[w].at[4 * px + 2 * py + pc]

        def copy(w, k, block, to, src=None):
            return pltpu.make_async_remote_copy(
                src_ref=slab(w, *block) if src is None else src, dst_ref=slab(w, *block),
                send_sem=send_sems.at[k, w], recv_sem=recv_sems.at[k, w], device_id=to, device_id_type=MESH)

        mine = []
        for w in range(n):
            for q in range(local_chunks):
                part = _window((), local_axis, q, local_chunks, xs[w].shape)
                mine.append(pltpu.make_async_copy(x_refs[w].at[part], slab(w, *me).at[part], local_sems.at[w, q]))
                mine[-1].start()
        direct = chips if relay_axis is None else chips[:2]
        first = [copy(w, 0, me, sibling, src=x_refs[w]) for w in range(n)]
        first += [copy(w, 1 + j, me, (*chip, ci), src=x_refs[w]) for j, chip in enumerate(direct) for w in range(n)]
        for cp in first:
            cp.start()

        def relay(w, part, block, to):
            piece = _window((), relay_axis, part, 2, xs[w].shape)
            return pltpu.make_async_remote_copy(
                src_ref=slab(w, *block).at[piece], dst_ref=slab(w, *block).at[piece],
                send_sem=send_sems.at[7 + part, w], recv_sem=recv_sems.at[7 + part, w],
                device_id=to, device_id_type=MESH)

        passed = []
        for j, chip in enumerate(direct):
            for w in range(n):
                copy(w, 1 + j, (*chip, ci), me).wait_recv()
                passed.append(copy(w, 4 + j, (*chip, ci), sibling))
                passed[-1].start()
                if relay_axis is not None:
                    passed.append(relay(w, j, (*chip, ci), (*chips[1 - j], ci)))
                    passed[-1].start()
        if relay_axis is not None:
            for w in range(n):
                for part in range(2):
                    relay(w, part, (*chips[2], ci), me).wait_recv()
                passed.append(copy(w, 6, (*chips[2], ci), sibling))
                passed[-1].start()
        for w in range(n):
            copy(w, 0, sibling, me).wait_recv()
        for j, chip in enumerate(chips):
            for w in range(n):
                copy(w, 4 + j, (*chip, 1 - ci), me).wait_recv()
        for cp in first + passed:
            cp.wait_send()
        for cp in mine:
            cp.wait()

    return pl.pallas_call(
        body, name=name,
        out_shape=[jax.ShapeDtypeStruct((N_DEV,) + x.shape, x.dtype) for x in xs],
        in_specs=[_ANY] * n, out_specs=[_ANY] * n,
        scratch_shapes=[pltpu.SemaphoreType.DMA((n_sems, n)), pltpu.SemaphoreType.DMA((n_sems, n)),
                        pltpu.SemaphoreType.DMA((n, local_chunks))],
    )(*xs)


def _exchange_cores(xs, name, chunk_axis=0, chunks=1):
    n = len(xs)
    n_peers = 1

    def body(*refs):
        x_refs, out_refs = refs[:n], refs[n:2 * n]
        send_sems, recv_sems = refs[2 * n:]
        xi, yi, ci = lax.axis_index("x"), lax.axis_index("y"), lax.axis_index("c")
        peers = [(1 - ci, (xi, yi, 1 - ci))]
        copies = []
        for k, (p, dev) in enumerate(peers):
            for w in range(n):
                slab_shape = xs[w].shape[1:]
                for q in range(chunks):
                    copies.append(pltpu.make_async_remote_copy(
                        src_ref=x_refs[w].at[_window((p,), chunk_axis, q, chunks, slab_shape)],
                        dst_ref=out_refs[w].at[_window((k,), chunk_axis, q, chunks, slab_shape)],
                        send_sem=send_sems.at[k, w, q], recv_sem=recv_sems.at[k, w, q],
                        device_id=dev, device_id_type=MESH))
                    copies[-1].start()
        for cp in copies:
            cp.wait()

    return pl.pallas_call(
        body, name=name,
        out_shape=[jax.ShapeDtypeStruct((n_peers,) + x.shape[1:], x.dtype) for x in xs],
        in_specs=[_ANY] * n, out_specs=[_ANY] * n,
        scratch_shapes=[pltpu.SemaphoreType.DMA((n_peers, n, chunks)), pltpu.SemaphoreType.DMA((n_peers, n, chunks))],
    )(*xs)


def _exchange_chips(xs, name, relay_axis):
    n = len(xs)

    def half_shape(x):
        shape = list(x.shape[1:])
        shape[relay_axis] //= 2
        return tuple(shape)

    def body(*refs):
        x_refs, out_refs, hop_refs = refs[:n], refs[n:2 * n], refs[2 * n:3 * n]
        send_sems, recv_sems = refs[3 * n:]
        xi, yi, ci = lax.axis_index("x"), lax.axis_index("y"), lax.axis_index("c")
        nbr = [(1 - xi, yi, ci), (xi, 1 - yi, ci)]
        slab_of_nbr = [2 * (1 - xi) + yi, 2 * xi + (1 - yi)]
        slab_of_diag = 2 * (1 - xi) + (1 - yi)

        def copy(k, w, src, dst, to):
            return pltpu.make_async_remote_copy(src_ref=src, dst_ref=dst, send_sem=send_sems.at[k, w],
                                                recv_sem=recv_sems.at[k, w], device_id=to, device_id_type=MESH)

        def piece(w, part):
            return _window((), relay_axis, part, 2, xs[w].shape[1:])

        sent = []
        for w in range(n):
            for j in range(2):
                sent.append(copy(j, w, x_refs[w].at[slab_of_nbr[j]], out_refs[w].at[j], nbr[j]))
                sent.append(copy(2 + j, w, x_refs[w].at[(slab_of_diag,) + piece(w, j)], hop_refs[w].at[j], nbr[j]))
        for cp in sent:
            cp.start()
        for w in range(n):
            for j in range(2):
                copy(2 + j, w, hop_refs[w].at[j], hop_refs[w].at[j], nbr[j]).wait_recv()
                sent.append(copy(4 + j, w, hop_refs[w].at[j], out_refs[w].at[(2,) + piece(w, j)], nbr[1 - j]))
                sent[-1].start()
        for w in range(n):
            for j in range(2):
                copy(j, w, out_refs[w].at[j], out_refs[w].at[j], nbr[j]).wait_recv()
                half = out_refs[w].at[(2,) + piece(w, j)]
                copy(4 + j, w, half, half, nbr[1 - j]).wait_recv()
        for cp in sent:
            cp.wait_send()

    out = pl.pallas_call(
        body, name=name,
        out_shape=[jax.ShapeDtypeStruct((3,) + x.shape[1:], x.dtype) for x in xs]
        + [jax.ShapeDtypeStruct((2,) + half_shape(x), x.dtype) for x in xs],
        in_specs=[_ANY] * n, out_specs=[_ANY] * (2 * n),
        scratch_shapes=[pltpu.SemaphoreType.DMA((6, n)), pltpu.SemaphoreType.DMA((6, n))],
    )(*xs)
    return out[:n]


def _sibling_send(xs, name, chunk_axis=1, chunks=4):
    n = len(xs)

    def body(*refs):
        x_refs, out_refs = refs[:n], refs[n:2 * n]
        send_sems, recv_sems = refs[2 * n:]
        xi, yi, ci = lax.axis_index("x"), lax.axis_index("y"), lax.axis_index("c")
        copies = []
        for w in range(n):
            for q in range(chunks):
                part = _window((), chunk_axis, q, chunks, xs[w].shape)
                copies.append(pltpu.make_async_remote_copy(
                    src_ref=x_refs[w].at[part], dst_ref=out_refs[w].at[part],
                    send_sem=send_sems.at[w, q], recv_sem=recv_sems.at[w, q],
                    device_id=(xi, yi, 1 - ci), device_id_type=MESH))
                copies[-1].start()
        for cp in copies:
            cp.wait()

    return pl.pallas_call(
        body, name=name,
        out_shape=[jax.ShapeDtypeStruct(x.shape, x.dtype) for x in xs],
        in_specs=[_ANY] * n, out_specs=[_ANY] * n,
        scratch_shapes=[pltpu.SemaphoreType.DMA((n, chunks)), pltpu.SemaphoreType.DMA((n, chunks))],
    )(*xs)


def _sum_leading(x, name, own=None, with_bf16=False):
    P, R, C = x.shape
    tr = _pick(R, max(16, (1 << 19) // (C * (P + 1))), 16)

    def body(*refs):
        n_in = 1 if own is None else 2
        x_ref = refs[n_in - 1]
        acc = x_ref[0].astype(F32) if own is None else refs[0][...] + x_ref[0].astype(F32)
        for p in range(1, P):
            acc = acc + x_ref[p].astype(F32)
        refs[n_in][...] = acc
        if with_bf16:
            refs[n_in + 1][...] = acc.astype(BF16)

    flat = pl.BlockSpec((tr, C), lambda r: (r, 0))
    slabs = pl.BlockSpec((P, tr, C), lambda r: (0, r, 0))
    out = pl.pallas_call(
        body, name=name, grid=(R // tr,),
        in_specs=[slabs] if own is None else [flat, slabs],
        out_specs=[flat, flat] if with_bf16 else [flat],
        out_shape=[jax.ShapeDtypeStruct((R, C), F32)] + ([jax.ShapeDtypeStruct((R, C), BF16)] if with_bf16 else []),
        compiler_params=_params(("arbitrary",)),
    )(*([x] if own is None else [own, x]))
    return out if with_bf16 else out[0]


MATMUL_SINGLE_K = 1280
MATMUL_VMEM_BUDGET = 36 * 1024 * 1024


def _matmul(a, b, mode, out_dtype, name, swiglu=False):
    if mode == "nn":
        (M, K), N = a.shape, b.shape[1]
    elif mode == "nt":
        (M, K), N = a.shape, b.shape[0]
    else:
        (K, M), N = a.shape, b.shape[1]
    tm = _pick(M, 1024 if mode != "tn" else 1536, 128)
    tn = _pick(N, 1536, 128)
    if swiglu:
        tm, tn = _pick(M, 1024 if out_dtype == BF16 else 512, 128), 2 * _ff_tile(N // 2)
    out_bytes = jnp.dtype(out_dtype).itemsize
    tk = K
    if K > MATMUL_SINGLE_K:
        for cap in (2048, 1024, 512):
            tk = _pick(K, cap, 128)
            blocks = 2 * 2 * tk * (tm + tn) + tm * tn * (2 * out_bytes + (4 if out_dtype != F32 else 0))
            if blocks <= MATMUL_VMEM_BUDGET:
                break
    nk = K // tk
    dims = {"nn": _NN, "nt": _NT, "tn": _TN}[mode]
    use_scratch = nk > 1 and out_dtype != F32

    def body(a_ref, b_ref, *refs):
        o_ref = refs[0]

        def product():
            return lax.dot_general(a_ref[...].astype(BF16), b_ref[...].astype(BF16), dims,
                                   preferred_element_type=F32)

        if nk == 1:
            part = product()
            o_ref[...] = part.astype(o_ref.dtype)
            if swiglu:
                g = part[:, :tn // 2]
                refs[1][...] = (g * _sigmoid(g) * part[:, tn // 2:]).astype(BF16)
            return
        k = pl.program_id(2)
        acc_ref = refs[-1] if use_scratch else o_ref

        @pl.when(k == 0)
        def _():
            acc_ref[...] = jnp.zeros_like(acc_ref)

        acc_ref[...] += product()

        if use_scratch:
            @pl.when(k == nk - 1)
            def _():
                o_ref[...] = acc_ref[...].astype(o_ref.dtype)

    if mode == "tn":
        a_spec = pl.BlockSpec((tk, tm), lambda i, j, k: (k, i))
    else:
        a_spec = pl.BlockSpec((tm, tk), lambda i, j, k: (i, k))
    if mode == "nt":
        b_spec = pl.BlockSpec((tn, tk), lambda i, j, k: (j, k))
    else:
        b_spec = pl.BlockSpec((tk, tn), lambda i, j, k: (k, j))
    out_specs = [pl.BlockSpec((tm, tn), lambda i, j, k: (i, j))]
    out_shape = [jax.ShapeDtypeStruct((M, N), out_dtype)]
    if swiglu:
        assert nk == 1 and mode == "nn"
        out_specs.append(pl.BlockSpec((tm, tn // 2), lambda i, j, k: (i, j)))
        out_shape.append(jax.ShapeDtypeStruct((M, N // 2), BF16))
    out = pl.pallas_call(
        body, name=name, grid=(M // tm, N // tn, nk),
        in_specs=[a_spec, b_spec], out_specs=out_specs, out_shape=out_shape,
        scratch_shapes=[pltpu.VMEM((tm, tn), F32)] if use_scratch else [],
        compiler_params=_params(("parallel", "parallel", "arbitrary")),
    )(a, b)
    return out if swiglu else out[0]


def _row_tile(S):
    return _pick(S, 512, 8)


def _norm_mod_fwd(x, gain, sc, sh):
    NB, S, D = x.shape
    tr = _row_tile(S)

    def body(x_ref, g_ref, sc_ref, sh_ref, h_ref):
        xv = x_ref[...]
        ms = jnp.mean(xv * xv, axis=-1, keepdims=True)
        n = xv * lax.rsqrt(ms + EPS) * g_ref[...]
        h_ref[...] = (n * (1.0 + sc_ref[...]) + sh_ref[...]).astype(BF16)

    tok = pl.BlockSpec((None, tr, D), lambda b, r: (b, r, 0))
    per_ex = pl.BlockSpec((None, 1, D), lambda b, r: (b, 0, 0))
    return pl.pallas_call(
        body, name="norm_mod_fwd", grid=(NB, S // tr),
        in_specs=[tok, pl.BlockSpec((1, D), lambda b, r: (0, 0)), per_ex, per_ex],
        out_specs=tok, out_shape=jax.ShapeDtypeStruct((NB, S, D), BF16),
        compiler_params=_params(("parallel", "parallel")),
    )(x, gain, sc, sh)


def _norm_mod_bwd(a, w, x, gain, sc, dres, name, y=None, g=None):
    NB, S, D = x.shape
    T, K = a.shape
    tm = _pick(S, 512, 128)
    per_ex_tiles = S // tm
    tk = K if K <= MATMUL_SINGLE_K else _pick(K, 1536, 128)
    nk = K // tk
    gated = y is not None

    def body(*refs):
        a_ref, w_ref, x_ref, g_ref, sc_ref, dres_ref = refs[:6]
        refs = refs[6:]
        if gated:
            y_ref, gate_ref = refs[:2]
            refs = refs[2:]
        dx_ref, dsh_ref, dsc_ref, dgain_ref = refs[:4]
        acc_ref = refs[-1]
        i, k = pl.program_id(0), pl.program_id(1)

        @pl.when(k == 0)
        def _():
            acc_ref[...] = jnp.zeros_like(acc_ref)

        acc_ref[...] += lax.dot_general(a_ref[...], w_ref[...], _NT, preferred_element_type=F32)

        @pl.when(k == nk - 1)
        def _():
            first_of_example = i % per_ex_tiles == 0

            @pl.when(first_of_example)
            def _():
                dsh_ref[...] = jnp.zeros_like(dsh_ref)
                dsc_ref[...] = jnp.zeros_like(dsc_ref)
                if gated:
                    refs[5][...] = jnp.zeros_like(refs[5])

            @pl.when(i == 0)
            def _():
                dgain_ref[...] = jnp.zeros_like(dgain_ref)

            xv = x_ref[...]
            rstd = lax.rsqrt(jnp.mean(xv * xv, axis=-1, keepdims=True) + EPS)
            xh = xv * rstd
            gn = g_ref[...]
            dh = acc_ref[...]
            dsh_ref[...] += jnp.sum(dh, axis=0, keepdims=True)
            dsc_ref[...] += jnp.sum(dh * (xh * gn), axis=0, keepdims=True)
            dn = dh * (1.0 + sc_ref[...])
            dgain_ref[...] += jnp.sum(dn * xh, axis=0, keepdims=True)
            dxh = dn * gn
            proj = jnp.mean(dxh * xh, axis=-1, keepdims=True)
            dx = rstd * (dxh - xh * proj) + dres_ref[...]
            dx_ref[...] = dx
            if gated:
                refs[4][...] = (dx * gate_ref[...]).astype(BF16)
                refs[5][...] += jnp.sum(dx * y_ref[...], axis=0, keepdims=True)

    tok = pl.BlockSpec((None, tm, D), lambda i, k: (i // per_ex_tiles, i % per_ex_tiles, 0))
    per_ex = pl.BlockSpec((None, 1, D), lambda i, k: (i // per_ex_tiles, 0, 0))
    row = pl.BlockSpec((1, D), lambda i, k: (0, 0))
    in_specs = [pl.BlockSpec((tm, tk), lambda i, k: (i, k)), pl.BlockSpec((D, tk), lambda i, k: (0, k)),
                tok, row, per_ex, tok]
    out_specs = [tok, per_ex, per_ex, row]
    out_shape = [jax.ShapeDtypeStruct((NB, S, D), F32), jax.ShapeDtypeStruct((NB, 1, D), F32),
                 jax.ShapeDtypeStruct((NB, 1, D), F32), jax.ShapeDtypeStruct((1, D), F32)]
    operands = [a, w, x, gain, sc, dres]
    if gated:
        in_specs += [tok, per_ex]
        out_specs += [tok, per_ex]
        out_shape += [jax.ShapeDtypeStruct((NB, S, D), BF16), jax.ShapeDtypeStruct((NB, 1, D), F32)]
        operands += [y, g]
    return pl.pallas_call(
        body, name=name, grid=(T // tm, nk), in_specs=in_specs, out_specs=out_specs, out_shape=out_shape,
        scratch_shapes=[pltpu.VMEM((tm, D), F32)],
        compiler_params=_params(("arbitrary", "arbitrary")),
    )(*operands)


def _gate_res(x, y, g, norm=None):
    NB, S, D = x.shape
    tr = _row_tile(S)

    def body(x_ref, y_ref, g_ref, *refs):
        xo = x_ref[...] + g_ref[...] * y_ref[...]
        refs[-1 if norm is None else -2][...] = xo
        if norm is not None:
            gain_ref, sc_ref, sh_ref, _, h_ref = refs
            n = xo * lax.rsqrt(jnp.mean(xo * xo, axis=-1, keepdims=True) + EPS) * gain_ref[...]
            h_ref[...] = (n * (1.0 + sc_ref[...]) + sh_ref[...]).astype(BF16)

    tok = pl.BlockSpec((None, tr, D), lambda b, r: (b, r, 0))
    per_ex = pl.BlockSpec((None, 1, D), lambda b, r: (b, 0, 0))
    in_specs, out_specs, operands = [tok, tok, per_ex], [tok], [x, y, g]
    out_shape = [jax.ShapeDtypeStruct((NB, S, D), F32)]
    if norm is not None:
        in_specs += [pl.BlockSpec((1, D), lambda b, r: (0, 0)), per_ex, per_ex]
        out_specs.append(tok)
        out_shape.append(jax.ShapeDtypeStruct((NB, S, D), BF16))
        operands += list(norm)
    out = pl.pallas_call(
        body, name="gate_res", grid=(NB, S // tr), in_specs=in_specs, out_specs=out_specs, out_shape=out_shape,
        compiler_params=_params(("parallel", "parallel")),
    )(*operands)
    return out[0] if norm is None else out


def _gate_res_bwd(dxo, y, g):
    NB, S, D = dxo.shape
    tr = _row_tile(S)

    def body(d_ref, y_ref, g_ref, dy_ref, dg_ref):
        @pl.when(pl.program_id(1) == 0)
        def _():
            dg_ref[...] = jnp.zeros_like(dg_ref)

        d = d_ref[...]
        dy_ref[...] = (d * g_ref[...]).astype(BF16)
        dg_ref[...] += jnp.sum(d * y_ref[...], axis=0, keepdims=True)

    tok = pl.BlockSpec((None, tr, D), lambda b, r: (b, r, 0))
    per_ex = pl.BlockSpec((None, 1, D), lambda b, r: (b, 0, 0))
    return pl.pallas_call(
        body, name="gate_res_bwd", grid=(NB, S // tr), in_specs=[tok, tok, per_ex], out_specs=[tok, per_ex],
        out_shape=[jax.ShapeDtypeStruct((NB, S, D), BF16), jax.ShapeDtypeStruct((NB, 1, D), F32)],
        compiler_params=_params(("arbitrary", "arbitrary")),
    )(dxo, y, g)


def _sigmoid(v):
    return 1.0 / (1.0 + jnp.exp(-v))


def _ff_tile(F):
    return _pick(F, 1536, 128)


def _interleave(gate, up):
    F = gate.shape[-1]
    tf = _ff_tile(F)
    parts = []
    for j in range(F // tf):
        parts += [gate[..., j * tf:(j + 1) * tf], up[..., j * tf:(j + 1) * tf]]
    return jnp.concatenate(parts, axis=-1)


def _deinterleave(gu):
    F = gu.shape[-1] // 2
    tf = _ff_tile(F)
    gate = [gu[..., 2 * j * tf:(2 * j + 1) * tf] for j in range(F // tf)]
    up = [gu[..., (2 * j + 1) * tf:(2 * j + 2) * tf] for j in range(F // tf)]
    return jnp.concatenate(gate, axis=-1), jnp.concatenate(up, axis=-1)


def _swiglu_bwd(dm, wd, gu):
    T, D = dm.shape
    F = wd.shape[0]
    tf = _ff_tile(F)
    tm = _pick(T, 512, 128)
    assert D <= MATMUL_SINGLE_K

    def body(a_ref, b_ref, gu_ref, o_ref):
        d = lax.dot_general(a_ref[...], b_ref[...], _NT, preferred_element_type=F32)
        g, u = gu_ref[:, :tf].astype(F32), gu_ref[:, tf:].astype(F32)
        s = _sigmoid(g)
        o_ref[:, :tf] = (d * u * (s * (1.0 + g * (1.0 - s)))).astype(BF16)
        o_ref[:, tf:] = (d * (g * s)).astype(BF16)

    return pl.pallas_call(
        body, name="swiglu_bwd", grid=(T // tm, F // tf),
        in_specs=[pl.BlockSpec((tm, D), lambda i, j: (i, 0)), pl.BlockSpec((tf, D), lambda i, j: (j, 0)),
                  pl.BlockSpec((tm, 2 * tf), lambda i, j: (i, j))],
        out_specs=pl.BlockSpec((tm, 2 * tf), lambda i, j: (i, j)),
        out_shape=jax.ShapeDtypeStruct((T, 2 * F), BF16),
        compiler_params=_params(("parallel", "parallel")),
    )(dm, wd, gu)


def _loss_fwd_bwd(y, target):
    NB, S, D = y.shape
    tr = _row_tile(S)

    def body(y_ref, t_ref, l_ref, d_ref):
        @pl.when((pl.program_id(0) == 0) & (pl.program_id(1) == 0))
        def _():
            l_ref[...] = jnp.zeros_like(l_ref)

        e = y_ref[...] - t_ref[...]
        d_ref[...] = e / D
        l_ref[...] += 0.5 * jnp.sum(jnp.mean(e * e, axis=-1, keepdims=True), axis=0, keepdims=True)

    tok = pl.BlockSpec((None, tr, D), lambda b, r: (b, r, 0))
    return pl.pallas_call(
        body, name="loss", grid=(NB, S // tr), in_specs=[tok, tok],
        out_specs=[pl.BlockSpec((1, 1), lambda b, r: (0, 0)), tok],
        out_shape=[jax.ShapeDtypeStruct((1, 1), F32), jax.ShapeDtypeStruct((NB, S, D), F32)],
        compiler_params=_params(("arbitrary", "arbitrary")),
    )(y, target)


def _half_sums(v, lo):
    sa = jnp.sum(jnp.where(lo, v, 0.0), axis=-1, keepdims=True)
    sb = jnp.sum(jnp.where(lo, 0.0, v), axis=-1, keepdims=True)
    return jnp.where(lo, sa, sb)


def _rope_swap(v, lane64):
    up = pltpu.roll(v, LANES - ROT_DIM // 2, 1)
    down = pltpu.roll(v, ROT_DIM // 2, 1)
    return jnp.where(lane64 < ROT_DIM // 2, up, jnp.where(lane64 < ROT_DIM, down, 0.0))


def _qk_prep_fwd(qkv, tab_c, tab_s, gains):
    T, W = qkv.shape
    R = W // LANES
    tt = _pick(T, 256, 8)

    def body(x_ref, c_ref, s_ref, g_ref, o_ref):
        lane = lax.broadcasted_iota(jnp.int32, (tt, LANES), 1)
        lo = lane < HEAD_DIM
        lane64 = lane & (HEAD_DIM - 1)
        c, s = c_ref[...], s_ref[...]
        for j in range(R - 1):
            cols = slice(j * LANES, (j + 1) * LANES)
            xv = x_ref[:, cols]
            rstd = lax.rsqrt(_half_sums(xv * xv, lo) / HEAD_DIM + EPS)
            yn = xv * rstd * g_ref[j:j + 1, :]
            o_ref[:, cols] = (yn * c + _rope_swap(yn, lane64) * s).astype(BF16)
        o_ref[:, (R - 1) * LANES:] = x_ref[:, (R - 1) * LANES:].astype(BF16)

    tok = pl.BlockSpec((tt, W), lambda t: (t, 0))
    tab = pl.BlockSpec((tt, LANES), lambda t: (t, 0))
    return pl.pallas_call(
        body, name="qk_prep_fwd", grid=(T // tt,),
        in_specs=[tok, tab, tab, pl.BlockSpec((R, LANES), lambda t: (0, 0))],
        out_specs=tok, out_shape=jax.ShapeDtypeStruct((T, W), BF16),
        compiler_params=_params(("parallel",)),
    )(qkv, tab_c, tab_s, gains)


def _qk_prep_bwd(qkv, dq, dk, dv, tab_c, tab_s, gains):
    T, W = qkv.shape
    R = W // LANES
    QW = dq.shape[1]
    tt = _pick(T, 256, 8)

    def body(x_ref, dq_ref, dk_ref, dv_ref, c_ref, s_ref, g_ref, o_ref, dg_ref):
        @pl.when(pl.program_id(0) == 0)
        def _():
            dg_ref[...] = jnp.zeros_like(dg_ref)

        lane = lax.broadcasted_iota(jnp.int32, (tt, LANES), 1)
        lo = lane < HEAD_DIM
        lane64 = lane & (HEAD_DIM - 1)
        c, s = c_ref[...], s_ref[...]
        for j in range(R - 1):
            cols = slice(j * LANES, (j + 1) * LANES)
            xv = x_ref[:, cols]
            d = dq_ref[:, cols] if j < R - 2 else dk_ref[...]
            rstd = lax.rsqrt(_half_sums(xv * xv, lo) / HEAD_DIM + EPS)
            xh = xv * rstd
            dyn = d * c + _rope_swap(d * s, lane64)
            dg_ref[j:j + 1, :] += jnp.sum(dyn * xh, axis=0, keepdims=True)
            dxh = dyn * g_ref[j:j + 1, :]
            proj = _half_sums(dxh * xh, lo) / HEAD_DIM
            o_ref[:, cols] = (rstd * (dxh - xh * proj)).astype(BF16)
        o_ref[:, (R - 1) * LANES:] = dv_ref[...].astype(BF16)

    tok = pl.BlockSpec((tt, W), lambda t: (t, 0))
    tab = pl.BlockSpec((tt, LANES), lambda t: (t, 0))
    gsp = pl.BlockSpec((R, LANES), lambda t: (0, 0))
    return pl.pallas_call(
        body, name="qk_prep_bwd", grid=(T // tt,),
        in_specs=[tok, pl.BlockSpec((tt, QW), lambda t: (t, 0)), tab, tab, tab, tab, gsp], out_specs=[tok, gsp],
        out_shape=[jax.ShapeDtypeStruct((T, W), BF16), jax.ShapeDtypeStruct((R, LANES), F32)],
        compiler_params=_params(("arbitrary",)),
    )(qkv, dq, dk, dv, tab_c, tab_s, gains)


def _band_mask(i):
    r = lax.broadcasted_iota(jnp.int32, (2 * BLOCK, 2 * BLOCK), 0) & (BLOCK - 1)
    c = lax.broadcasted_iota(jnp.int32, (2 * BLOCK, 2 * BLOCK), 1)
    rel = r + BLOCK - c
    return (rel >= 0) & (rel < BLOCK) & ((c >= BLOCK) | (i > 0))


def _swa_softmax(s, valid, sink):
    s = jnp.where(valid, s * ATTN_SCALE, NEG_BIG)
    m = jnp.maximum(jnp.max(s, axis=1, keepdims=True), sink)
    p = jnp.exp(s - m)
    ps = jnp.exp(sink - m)
    denom = jnp.sum(p, axis=1, keepdims=True) + ps
    return p / denom, ps / denom


A_GROUP = 4


Q_WIDTH_A = N_Q_A * HEAD_DIM
N_PAIR_A = Q_WIDTH_A // LANES


def _swa_specs():
    qs = pl.BlockSpec((None, BLOCK, Q_WIDTH_A), lambda b, i: (b, i, 0))

    def kv(col, back):
        return pl.BlockSpec((None, BLOCK, LANES), lambda b, i: (b, jnp.maximum(i - back, 0), col))

    return qs, kv(N_PAIR_A, 1), kv(N_PAIR_A, 0), kv(N_PAIR_A + 1, 1), kv(N_PAIR_A + 1, 0)


def _dup_heads(t):
    lo = lax.broadcasted_iota(jnp.int32, t.shape, 1) < HEAD_DIM
    sw = pltpu.roll(t.astype(F32), HEAD_DIM, 1).astype(BF16)
    return jnp.where(lo, t, sw), jnp.where(lo, sw, t)


def _kv_tiles(kp_ref, kc_ref, vp_ref, vc_ref):
    kd = _dup_heads(jnp.concatenate([kp_ref[...], kc_ref[...]], axis=0))
    vd = _dup_heads(jnp.concatenate([vp_ref[...], vc_ref[...]], axis=0))
    return kd, vd


def _attn_a_fwd(qkn, sinks):
    NB, S, _ = qkn.shape
    qs, kp, kc, vp, vc = _swa_specs()

    def body(q_ref, kp_ref, kc_ref, vp_ref, vc_ref, sink_ref, o_ref):
        i = pl.program_id(1)
        kd, vd = _kv_tiles(kp_ref, kc_ref, vp_ref, vc_ref)
        valid = _band_mask(i)
        lo = lax.broadcasted_iota(jnp.int32, (BLOCK, LANES), 1) < HEAD_DIM
        top = lax.broadcasted_iota(jnp.int32, (2 * BLOCK, 1), 0) < BLOCK
        for first in range(0, N_PAIR_A, A_GROUP):
            pairs = range(first, first + A_GROUP)
            qs_ = [jnp.concatenate(_head_halves(q_ref[:, p * LANES:(p + 1) * LANES], lo), axis=0) for p in pairs]
            ss = [lax.dot_general(q, kd[2 * p // GROUP_A], _NT, preferred_element_type=F32) for q, p in zip(qs_, pairs)]
            pns = [_swa_softmax(s, valid, jnp.where(top, sink_ref[2 * p], sink_ref[2 * p + 1]))[0]
                   for s, p in zip(ss, pairs)]
            pvs = [jnp.dot(pn.astype(BF16), vd[2 * p // GROUP_A], preferred_element_type=F32) for pn, p in zip(pns, pairs)]
            for pv, p in zip(pvs, pairs):
                o_ref[:, p * LANES:(p + 1) * LANES] = jnp.where(lo, pv[:BLOCK], pv[BLOCK:]).astype(BF16)

    return pl.pallas_call(
        body, name="attn_a_fwd", grid=(NB, S // BLOCK),
        in_specs=[qs, kp, kc, vp, vc, pl.BlockSpec(memory_space=pltpu.SMEM)],
        out_specs=qs, out_shape=jax.ShapeDtypeStruct((NB, S, Q_WIDTH_A), BF16),
        compiler_params=_params(("parallel", "arbitrary")),
    )(qkn, qkn, qkn, qkn, qkn, sinks)


def _attn_a_bwd(qkn, do, sinks):
    NB, S, _ = qkn.shape
    qs, kp, kc, vp, vc = _swa_specs()
    full = pl.BlockSpec((None, S, LANES), lambda b, i: (b, 0, 0))
    sink_out = pl.BlockSpec((None, N_Q_A, LANES), lambda b, i: (b, 0, 0))

    def body(q_ref, do_ref, kp_ref, kc_ref, vp_ref, vc_ref, sink_ref, dq_ref, dk_ref, dv_ref, ds_ref, dk_s, dv_s):
        i = pl.program_id(1)

        @pl.when(i == 0)
        def _():
            dk_ref[...] = jnp.zeros_like(dk_ref)
            dv_ref[...] = jnp.zeros_like(dv_ref)
            ds_ref[...] = jnp.zeros_like(ds_ref)

        dk_s[...] = jnp.zeros_like(dk_s)
        dv_s[...] = jnp.zeros_like(dv_s)
        kd, vd = _kv_tiles(kp_ref, kc_ref, vp_ref, vc_ref)
        valid = _band_mask(i)
        lo = lax.broadcasted_iota(jnp.int32, (BLOCK, LANES), 1) < HEAD_DIM
        top = lax.broadcasted_iota(jnp.int32, (2 * BLOCK, 1), 0) < BLOCK
        for first in range(0, N_PAIR_A, A_GROUP):
            pairs = range(first, first + A_GROUP)
            kvs = [2 * p // GROUP_A for p in pairs]
            qs_ = [jnp.concatenate(_head_halves(q_ref[:, p * LANES:(p + 1) * LANES], lo), axis=0) for p in pairs]
            dos = [jnp.concatenate(_head_halves(do_ref[:, p * LANES:(p + 1) * LANES], lo), axis=0) for p in pairs]
            ss = [lax.dot_general(q, kd[kv], _NT, preferred_element_type=F32) for q, kv in zip(qs_, kvs)]
            dps = [lax.dot_general(d, vd[kv], _NT, preferred_element_type=F32) for d, kv in zip(dos, kvs)]
            sm = [_swa_softmax(s, valid, jnp.where(top, sink_ref[2 * p], sink_ref[2 * p + 1])) for s, p in zip(ss, pairs)]
            deltas = [jnp.sum(pn * dp, axis=1, keepdims=True) for (pn, _), dp in zip(sm, dps)]
            dsbs = [(pn * (dp - delta) * ATTN_SCALE).astype(BF16) for (pn, _), dp, delta in zip(sm, dps, deltas)]
            for n, p in enumerate(pairs):
                dq2 = jnp.dot(dsbs[n], kd[kvs[n]], preferred_element_type=F32)
                dq_ref[:, p * LANES:(p + 1) * LANES] = jnp.where(lo, dq2[:BLOCK], dq2[BLOCK:])
                dk_s[kvs[n]] += lax.dot_general(dsbs[n], qs_[n], _TN, preferred_element_type=F32)
                dv_s[kvs[n]] += lax.dot_general(sm[n][0].astype(BF16), dos[n], _TN, preferred_element_type=F32)
                t = sm[n][1] * deltas[n]
                for hh in range(2):
                    dsink = -jnp.sum(t[hh * BLOCK:(hh + 1) * BLOCK], axis=0, keepdims=True)
                    ds_ref[2 * p + hh:2 * p + hh + 1, :] += jnp.broadcast_to(dsink, (1, LANES))

        lo2 = lax.broadcasted_iota(jnp.int32, (2 * BLOCK, LANES), 1) < HEAD_DIM

        def fold(acc):
            halves = [acc[kv] + pltpu.roll(acc[kv], HEAD_DIM, 1) for kv in range(N_KV_A)]
            return jnp.where(lo2, halves[0], halves[1])

        dk2, dv2 = fold(dk_s), fold(dv_s)

        @pl.when(i > 0)
        def _():
            start = pl.multiple_of((i - 1) * BLOCK, BLOCK)
            dk_ref[pl.ds(start, 2 * BLOCK), :] += dk2
            dv_ref[pl.ds(start, 2 * BLOCK), :] += dv2

        @pl.when(i == 0)
        def _():
            dk_ref[0:BLOCK, :] += dk2[BLOCK:, :]
            dv_ref[0:BLOCK, :] += dv2[BLOCK:, :]

    slots = pltpu.VMEM((N_KV_A, 2 * BLOCK, LANES), F32)
    return pl.pallas_call(
        body, name="attn_a_bwd", grid=(NB, S // BLOCK),
        in_specs=[qs, qs, kp, kc, vp, vc, pl.BlockSpec(memory_space=pltpu.SMEM)],
        out_specs=[qs, full, full, sink_out],
        out_shape=[jax.ShapeDtypeStruct((NB, S, Q_WIDTH_A), F32), jax.ShapeDtypeStruct((NB, S, LANES), F32),
                   jax.ShapeDtypeStruct((NB, S, LANES), F32), jax.ShapeDtypeStruct((NB, N_Q_A, LANES), F32)],
        scratch_shapes=[slots, slots],
        compiler_params=_params(("parallel", "arbitrary")),
    )(qkn, do, qkn, qkn, qkn, qkn, sinks)


def _cumsum_mats():
    src = lax.broadcasted_iota(jnp.int32, (2 * BLOCK, 2 * BLOCK), 0) % BLOCK
    dst = lax.broadcasted_iota(jnp.int32, (2 * BLOCK, 2 * BLOCK), 1)
    ones = dst >= BLOCK
    rev = ((src > dst) | ones).astype(BF16)
    fwd = ((src < dst) | ones).astype(BF16)
    return rev, fwd


def _log_sigmoids(z):
    sp = jnp.log(1.0 + jnp.exp(-jnp.abs(z)))
    return jnp.minimum(z, 0.0) - sp, -(jnp.maximum(z, 0.0) + sp)


def _cumsum_mxu_many(vs, mat):
    parts = []
    for v in vs:
        hi = v.astype(BF16)
        parts.append(jnp.concatenate([hi, (v - hi.astype(F32)).astype(BF16)], axis=1))
    r = jnp.dot(jnp.concatenate(parts, axis=0), mat, preferred_element_type=F32)
    return [(r[n * BLOCK:(n + 1) * BLOCK, :BLOCK], r[n * BLOCK:(n + 1) * BLOCK, BLOCK:]) for n in range(len(vs))]


def _strict_mask():
    r = lax.broadcasted_iota(jnp.int32, (BLOCK, BLOCK), 0)
    c = lax.broadcasted_iota(jnp.int32, (BLOCK, BLOCK), 1)
    return c < r


def _tile(ref, j):
    return ref[pl.ds(pl.multiple_of(j * BLOCK, BLOCK), BLOCK), :]


SWEEP_EXIT = -88.0


def _head_halves(t, lo):
    zero = jnp.zeros_like(t)
    return jnp.where(lo, t, zero), jnp.where(lo, zero, t)


def _sb_specs(S, HD, width):
    n = HD // width
    blk = pl.BlockSpec((None, BLOCK, width), lambda b, p, i: (b, i, p))
    k_full = pl.BlockSpec((None, S, width), lambda b, p, i: (b, 0, n + p))
    v_full = pl.BlockSpec((None, S, width), lambda b, p, i: (b, 0, 2 * n + p))
    mat = pl.BlockSpec((2 * BLOCK, 2 * BLOCK), lambda b, p, i: (0, 0))
    return blk, k_full, v_full, mat


SB_FWD_PAIRS = 4
SB_BWD_PAIRS = 2
SB_BWD_TILES = 2
SB_BWD_VMEM_LIMIT_BYTES = 58 * 1024 * 1024


def _attn_b_fwd(qkv, rev):
    NB, S, W = qkv.shape
    HD = W // 3
    width = SB_FWD_PAIRS * LANES
    n_heads = 2 * SB_FWD_PAIRS
    blk, k_full, v_full, mat = _sb_specs(S, HD, width)

    def body(q_ref, k_ref, v_ref, rev_ref, o_ref):
        i = pl.program_id(2)
        rv = rev_ref[...]
        mask = _strict_mask()
        lo = lax.broadcasted_iota(jnp.int32, (BLOCK, LANES), 1) < HEAD_DIM
        q_all = q_ref[...]
        q_stack = [jnp.concatenate(_head_halves(q_all[:, p * LANES:(p + 1) * LANES] * ATTN_SCALE, lo), axis=0)
                   for p in range(SB_FWD_PAIRS)]

        def pair_tiles(ref, j):
            t = _tile(ref, j)
            return [t[:, p * LANES:(p + 1) * LANES] for p in range(SB_FWD_PAIRS)]

        def tile_pass(j, carries, diagonal):
            ks, vs = pair_tiles(k_ref, j), pair_tiles(v_ref, j)
            zs = []
            for p in range(SB_FWD_PAIRS):
                z2 = lax.dot_general(q_stack[p], ks[p], _NT, preferred_element_type=F32)
                zs += [z2[:BLOCK], z2[BLOCK:]]
            logs = [_log_sigmoids(z) for z in zs]
            cums = _cumsum_mxu_many([jnp.where(mask, lm, 0.0) if diagonal else lm for _, lm in logs], rv)
            probs, new_c = [], []
            for h in range(n_heads):
                after, rs = cums[h]
                if diagonal:
                    a = jnp.where(mask, jnp.exp(logs[h][0] + after), 0.0)
                    new_c.append(rs)
                else:
                    a = jnp.exp(logs[h][0] + after + carries[h])
                    new_c.append(carries[h] + rs)
                probs.append(a.astype(BF16))
            outs = []
            for p in range(SB_FWD_PAIRS):
                pv = jnp.dot(jnp.concatenate(probs[2 * p:2 * p + 2], axis=0), vs[p], preferred_element_type=F32)
                outs.append(jnp.where(lo, pv[:BLOCK], pv[BLOCK:]))
            return new_c, outs

        carries, accs = tile_pass(i, None, True)

        def live(cs):
            top = cs[0]
            for c in cs[1:]:
                top = jnp.maximum(top, c)
            return jnp.max(top) > SWEEP_EXIT

        def cond(st):
            return (st[0] < i) & st[1]

        def step(st):
            jj, _, cs, accs = st
            new_c, outs = tile_pass(i - 1 - jj, cs, False)
            return jj + 1, live(new_c), new_c, [acc + o for acc, o in zip(accs, outs)]

        st = lax.while_loop(cond, step, (jnp.int32(0), live(carries), carries, accs))
        for p in range(SB_FWD_PAIRS):
            o_ref[:, p * LANES:(p + 1) * LANES] = st[3][p].astype(BF16)

    return pl.pallas_call(
        body, name="attn_b_fwd", grid=(NB, HD // width, S // BLOCK),
        in_specs=[blk, k_full, v_full, mat], out_specs=blk,
        out_shape=jax.ShapeDtypeStruct((NB, S, HD), BF16),
        compiler_params=_params(("parallel", "parallel", "arbitrary")),
    )(qkv, qkv, qkv, rev)


def _attn_b_bwd(qkv, do, rev, fwd):
    NB, S, W = qkv.shape
    HD = W // 3
    width = SB_BWD_PAIRS * LANES
    n_heads = 2 * SB_BWD_PAIRS
    nj = S // BLOCK
    blk, k_full, v_full, mat = _sb_specs(S, HD, width)
    acc_full = pl.BlockSpec((None, S, width), lambda b, p, i: (b, 0, p))

    def body(q_ref, do_ref, k_ref, v_ref, rev_ref, fwd_ref, dq_ref, dk_ref, dv_ref, sig_s, a_s, e_s):
        i = pl.program_id(2)

        @pl.when(i == 0)
        def _():
            dk_ref[...] = jnp.zeros_like(dk_ref)
            dv_ref[...] = jnp.zeros_like(dv_ref)

        rv, fw = rev_ref[...], fwd_ref[...]
        mask = _strict_mask()
        lo = lax.broadcasted_iota(jnp.int32, (BLOCK, LANES), 1) < HEAD_DIM
        pairs = range(SB_BWD_PAIRS)

        def cols(p):
            return slice(p * LANES, (p + 1) * LANES)

        q_stack = [jnp.concatenate(_head_halves(q_ref[:, cols(p)], lo), axis=0) for p in pairs]
        qs_stack = [q * ATTN_SCALE for q in q_stack]
        do_stack = [jnp.concatenate(_head_halves(do_ref[:, cols(p)], lo), axis=0) for p in pairs]

        def sweep1_tiles(js, carries, diagonal):
            zs, das = [], []
            for j in js:
                kj, vj = _tile(k_ref, j), _tile(v_ref, j)
                for p in pairs:
                    z2 = lax.dot_general(qs_stack[p], kj[:, cols(p)], _NT, preferred_element_type=F32)
                    da2 = lax.dot_general(do_stack[p], vj[:, cols(p)], _NT, preferred_element_type=F32)
                    zs += [z2[:BLOCK], z2[BLOCK:]]
                    das += [da2[:BLOCK], da2[BLOCK:]]
            logs = [_log_sigmoids(z) for z in zs]
            cums = _cumsum_mxu_many([jnp.where(mask, lm, 0.0) if diagonal else lm for _, lm in logs], rv)
            new_c, stores = [], []
            for h in range(n_heads):
                carry = None if diagonal else carries[h]
                for t, j in enumerate(js):
                    n = t * n_heads + h
                    lb, (after, rs) = logs[n][0], cums[n]
                    if diagonal:
                        a = jnp.where(mask, jnp.exp(lb + after), 0.0)
                        carry = rs
                    else:
                        a = jnp.exp(lb + after + carry)
                        carry = carry + rs
                    stores.append((t, h, j, jnp.exp(lb), a.astype(BF16), das[n] * a))
                new_c.append(carry)
            for t, h, j, sg, ab, e in sorted(stores, key=lambda s: -s[0]):
                sig_s[h, j] = sg
                a_s[h, j] = ab
                e_s[h, j] = e
            return new_c

        carries = sweep1_tiles([i], None, True)

        def live(cs):
            top = cs[0]
            for c in cs[1:]:
                top = jnp.maximum(top, c)
            return jnp.max(top) > SWEEP_EXIT

        def cond(st):
            return (SB_BWD_TILES * st[0] < i) & st[1]

        def sweep1(st):
            first = i - 1 - SB_BWD_TILES * st[0]
            new_c = sweep1_tiles([jnp.maximum(first - t, 0) for t in range(SB_BWD_TILES)], st[2], False)
            return st[0] + 1, live(new_c), new_c

        trips = lax.while_loop(cond, sweep1, (jnp.int32(0), live(carries), carries))[0]
        lowest = jnp.maximum(i - SB_BWD_TILES * trips, 0)

        def grads(js, st, diagonal):
            prefixes, dqs = st
            es = [e_s[h, j] for j in js for h in range(n_heads)]
            cums = _cumsum_mxu_many(es, fw)
            dzs, new_p = [], []
            for h in range(n_heads):
                prefix = prefixes[h]
                for t, j in enumerate(js):
                    n = t * n_heads + h
                    sg = sig_s[h, j]
                    e_before, rs = cums[n]
                    dz = (es[n] * (1.0 - sg) - (e_before + prefix) * sg) * ATTN_SCALE
                    if diagonal:
                        dz = jnp.where(mask, dz, 0.0)
                    dzs.append((t, h, dz.astype(BF16)))
                    prefix = prefix + rs
                new_p.append(prefix)
            dz_of = {(t, h): dz for t, h, dz in dzs}
            new_dq = list(dqs)
            for t, j in enumerate(js):
                kj = _tile(k_ref, j)
                rows = pl.ds(pl.multiple_of(j * BLOCK, BLOCK), BLOCK)
                for p in pairs:
                    dz_stack = jnp.concatenate([dz_of[t, 2 * p], dz_of[t, 2 * p + 1]], axis=0)
                    a_stack = jnp.concatenate([a_s[2 * p, j], a_s[2 * p + 1, j]], axis=0)
                    dq2 = jnp.dot(dz_stack, kj[:, cols(p)], preferred_element_type=F32)
                    new_dq[p] = new_dq[p] + jnp.where(lo, dq2[:BLOCK], dq2[BLOCK:])
                    dk_ref[rows, cols(p)] += lax.dot_general(dz_stack, q_stack[p], _TN, preferred_element_type=F32)
                    dv_ref[rows, cols(p)] += lax.dot_general(a_stack, do_stack[p], _TN, preferred_element_type=F32)
            return new_p, new_dq

        zeros = jnp.zeros((BLOCK, BLOCK), F32)
        st = ([zeros] * n_heads, [zeros] * SB_BWD_PAIRS)
        count = i - lowest
        st = lax.fori_loop(0, count % SB_BWD_TILES, lambda t, st: grads([lowest + t], st, False), st)
        start = lowest + count % SB_BWD_TILES
        st = lax.fori_loop(0, count // SB_BWD_TILES,
                           lambda t, st: grads([start + SB_BWD_TILES * t + u for u in range(SB_BWD_TILES)], st, False), st)
        dqs = grads([i], st, True)[1]
        for p in pairs:
            dq_ref[:, cols(p)] = dqs[p]

    f32_stash = pltpu.VMEM((n_heads, nj, BLOCK, BLOCK), F32)
    bf16_stash = pltpu.VMEM((n_heads, nj, BLOCK, BLOCK), BF16)
    return pl.pallas_call(
        body, name="attn_b_bwd", grid=(NB, HD // width, nj),
        in_specs=[blk, blk, k_full, v_full, mat, mat], out_specs=[blk, acc_full, acc_full],
        out_shape=[jax.ShapeDtypeStruct((NB, S, HD), F32)] * 3,
        scratch_shapes=[f32_stash, bf16_stash, f32_stash],
        compiler_params=_params(("parallel", "parallel", "arbitrary"), SB_BWD_VMEM_LIMIT_BYTES),
    )(qkv, do, qkv, qkv, rev, fwd)


def _ada_fwd(c_all, w, b):
    L, D, N = w.shape
    B = c_all.shape[0]

    def body(c_ref, w_ref, b_ref, o_ref):
        cv = c_ref[...]
        cond = (cv * _sigmoid(cv)).astype(BF16)
        o_ref[...] = jnp.dot(cond, w_ref[...].astype(BF16), preferred_element_type=F32) + b_ref[...]

    return pl.pallas_call(
        body, name="ada_fwd", grid=(L,),
        in_specs=[pl.BlockSpec((B, D), lambda l: (0, 0)), pl.BlockSpec((None, D, N), lambda l: (l, 0, 0)),
                  pl.BlockSpec((None, 1, N), lambda l: (l, 0, 0))],
        out_specs=pl.BlockSpec((None, B, N), lambda l: (l, 0, 0)),
        out_shape=jax.ShapeDtypeStruct((L, B, N), F32),
        compiler_params=_params(("parallel",)),
    )(c_all, w, b)


def _ada_bwd(c_all, dmod_all, dmod_shard):
    L, B, N = dmod_shard.shape
    D = c_all.shape[1]
    N_all = dmod_all.shape[2]

    def body(c_ref, da_ref, ds_ref, gw_ref, gb_ref):
        cv = c_ref[...]
        cond = (cv * _sigmoid(cv)).astype(BF16)
        gw_ref[...] = lax.dot_general(cond, ds_ref[...].astype(BF16), _TN, preferred_element_type=F32)
        gb_ref[...] = jnp.sum(da_ref[...], axis=0, keepdims=True)

    return pl.pallas_call(
        body, name="ada_bwd", grid=(L,),
        in_specs=[pl.BlockSpec((B, D), lambda l: (0, 0)), pl.BlockSpec((None, B, N_all), lambda l: (l, 0, 0)),
                  pl.BlockSpec((None, B, N), lambda l: (l, 0, 0))],
        out_specs=[pl.BlockSpec((None, D, N), lambda l: (l, 0, 0)), pl.BlockSpec((None, 1, N_all), lambda l: (l, 0, 0))],
        out_shape=[jax.ShapeDtypeStruct((L, D, N), F32), jax.ShapeDtypeStruct((L, 1, N_all), F32)],
        compiler_params=_params(("parallel",)),
    )(c_all, dmod_all, dmod_shard)


def _adamw(w, g, m, v, name):
    shape = w.shape
    C = shape[-1]
    R = w.size // C
    tr = _pick(R, max(8, (1 << 18) // C), 8)
    c1 = 1.0 - ADAM_B1 ** ADAM_STEP
    c2 = 1.0 - ADAM_B2 ** ADAM_STEP

    def body(w_ref, g_ref, m_ref, v_ref, d_ref, nm_ref, nv_ref):
        gv = g_ref[...]
        nm = ADAM_B1 * m_ref[...] + (1.0 - ADAM_B1) * gv
        nv = ADAM_B2 * v_ref[...] + (1.0 - ADAM_B2) * (gv * gv)
        d_ref[...] = -ADAM_LR * ((nm / c1) / (jnp.sqrt(nv / c2) + ADAM_EPS) + ADAM_WD * w_ref[...])
        nm_ref[...] = nm
        nv_ref[...] = nv

    spec = pl.BlockSpec((tr, C), lambda r: (r, 0))
    out = pl.pallas_call(
        body, name=name, grid=(R // tr,), in_specs=[spec] * 4, out_specs=[spec] * 3,
        out_shape=[jax.ShapeDtypeStruct((R, C), F32)] * 3,
        compiler_params=_params(("parallel",)),
    )(*[t.reshape(R, C) for t in (w, g, m, v)])
    return [t.reshape(shape) for t in out]


_SHARDED = (("wqkv_a", 2), ("wo_a", 1), ("wqkv_b", 2), ("wo_b", 1), ("w_gate", 2), ("w_up", 2), ("w_down", 1))


def _pack_full(layers, axis, gate_up=None):
    L = len(layers)
    R, C = layers[0].shape

    def shards(m):
        if gate_up is not None:
            F = C // 2
            tf, Cs = _ff_tile(F), F // 4
            assert tf % Cs == 0
            starts = [(2 * (s * Cs // tf) + gate_up) * tf + s * Cs % tf for s in range(4)]
            return jnp.stack([m[:, st:st + Cs] for st in starts])
        if axis == 2:
            return m.reshape(R, 4, C // 4).transpose(1, 0, 2)
        return m.reshape(4, R // 4, C)

    halves = [jnp.stack([shards(m) for m in layers[h * (L // 2):(h + 1) * (L // 2)]], axis=1) for h in range(2)]
    return jnp.stack(halves)


def _unpack_full(gathered, axis):
    _, Lh, Rs, Cs = gathered.shape
    t = gathered.reshape(4, 2, Lh, Rs, Cs)
    layers = []
    for h in range(2):
        for l in range(Lh):
            piece = t[:, h, l]
            if axis == 2:
                layers.append(piece.transpose(1, 0, 2).reshape(Rs, 4 * Cs))
            else:
                layers.append(piece.reshape(4 * Rs, Cs))
    return layers


def _sum_slabs(own, recv, name, with_bf16=False):
    C = own.shape[-1]
    out = _sum_leading(recv.reshape(recv.shape[0], -1, C), name, own=own.reshape(-1, C), with_bf16=with_bf16)
    if with_bf16:
        return out[0].reshape(own.shape), out[1].reshape(own.shape)
    return out.reshape(own.shape)


def _gather8(x, name):
    return _all_gather8([x], name)[0]


def _rope_tables(positions):
    half = ROT_DIM // 2
    inv_freq = jnp.power(jnp.float32(ROPE_THETA), -jnp.arange(half, dtype=F32) * 2.0 / ROT_DIM)
    ang = positions.astype(F32).reshape(-1, 1) * inv_freq
    cos, sin = jnp.cos(ang), jnp.sin(ang)
    T = ang.shape[0]
    rest = HEAD_DIM - ROT_DIM
    c64 = jnp.concatenate([cos, cos, jnp.ones((T, rest), F32)], axis=1)
    s64 = jnp.concatenate([-sin, sin, jnp.zeros((T, rest), F32)], axis=1)
    return jnp.tile(c64, (1, 2)), jnp.tile(s64, (1, 2))


def _gain_rows(q_gain, k_gain):
    q2 = jnp.tile(q_gain.reshape(1, HEAD_DIM), (GROUP_A, 2))
    k2 = jnp.tile(k_gain.reshape(1, HEAD_DIM), (1, 2))
    return jnp.concatenate([q2, k2, jnp.ones((1, LANES), F32)], axis=0)


def _local_step(x, positions, mod, norm1_g, norm2_g, q_norm_a, k_norm_a, sinks_a,
                wqkv_a, wo_a, wqkv_b, wo_b, wgu, wd, loss_target):
    NB, S, D = x.shape
    T = NB * S
    QA = N_Q_A * HEAD_DIM
    tab_c, tab_s = _rope_tables(positions)
    rev, fwd = _cumsum_mats()

    saved = []
    xc = x
    mods = [[mod[i][:, k * D:(k + 1) * D].reshape(NB, 1, D) for k in range(6)] for i in range(DEPTH)]
    h = _norm_mod_fwd(xc, norm1_g[0:1], mods[0][1], mods[0][0])
    for i in range(DEPTH):
        j = i // 2
        sh1, sc1, g1, sh2, sc2, g2 = mods[i]
        st = dict(x=xc, sc1=sc1, g1=g1, sc2=sc2, g2=g2)
        st["h"] = h.reshape(T, D)
        if i % 2 == 0:
            st["qkv"] = _matmul(st["h"], wqkv_a[j], "nn", F32, "qkv_a")
            st["gains"] = _gain_rows(q_norm_a[j], k_norm_a[j])
            st["qkn"] = _qk_prep_fwd(st["qkv"], tab_c, tab_s, st["gains"]).reshape(NB, S, -1)
            st["o"] = _attn_a_fwd(st["qkn"], sinks_a[j]).reshape(T, QA)
            y = _matmul(st["o"], wo_a[j], "nn", F32, "wo_a")
        else:
            st["qkv"] = _matmul(st["h"], wqkv_b[j], "nn", BF16, "qkv_b").reshape(NB, S, -1)
            st["o"] = _attn_b_fwd(st["qkv"], rev).reshape(T, N_H_B * HEAD_DIM)
            y = _matmul(st["o"], wo_b[j], "nn", F32, "wo_b")
        st["y"] = y.reshape(NB, S, D)
        x1, h2 = _gate_res(xc, st["y"], g1, norm=(norm2_g[i:i + 1], sc2, sh2))
        st["x1"] = x1
        st["h2"] = h2.reshape(T, D)
        st["gu"], st["act"] = _matmul(st["h2"], wgu[i], "nn", BF16, "gate_up", swiglu=True)
        st["m"] = _matmul(st["act"], wd[i], "nn", F32, "down").reshape(NB, S, D)
        if i + 1 < DEPTH:
            xc, h = _gate_res(x1, st["m"], g2, norm=(norm1_g[i + 1:i + 2], mods[i + 1][1], mods[i + 1][0]))
        else:
            xc = _gate_res(x1, st["m"], g2)
        saved.append(st)

    loss, dx = _loss_fwd_bwd(xc, loss_target)

    grads = {name: [None] * n for name, n in
             (("wqkv_a", 2), ("wo_a", 2), ("wqkv_b", 2), ("wo_b", 2), ("wgu", DEPTH), ("wd", DEPTH),
              ("norm1_g", DEPTH), ("norm2_g", DEPTH), ("q_norm_a", 2), ("k_norm_a", 2), ("sinks_a", 2))}
    dmod = [None] * DEPTH
    dm, dg2 = _gate_res_bwd(dx, saved[-1]["m"], saved[-1]["g2"])
    for i in reversed(range(DEPTH)):
        j = i // 2
        st = saved[i]
        dm = dm.reshape(T, D)
        grads["wd"][i] = _matmul(st["act"], dm, "tn", F32, "d_wd")
        dgu = _swiglu_bwd(dm, wd[i], st["gu"])
        grads["wgu"][i] = _matmul(st["h2"], dgu, "tn", F32, "d_wgu")
        dx1, dsh2, dsc2, grads["norm2_g"][i], dy, dg1 = _norm_mod_bwd(
            dgu, wgu[i], st["x1"], norm2_g[i:i + 1], st["sc2"], dx, "d_h2", y=st["y"], g=st["g1"])
        dy = dy.reshape(T, D)
        if i % 2 == 0:
            do = _matmul(dy, wo_a[j], "nt", BF16, "d_o_a").reshape(NB, S, QA)
            grads["wo_a"][j] = _matmul(st["o"], dy, "tn", F32, "d_wo_a")
            dq, dk, dv, dsink = _attn_a_bwd(st["qkn"], do, sinks_a[j])
            dqkv, dgain = _qk_prep_bwd(st["qkv"], dq.reshape(T, QA), dk.reshape(T, LANES), dv.reshape(T, LANES),
                                       tab_c, tab_s, st["gains"])
            w_in = wqkv_a[j]
            grads["wqkv_a"][j] = _matmul(st["h"], dqkv, "tn", F32, "d_wqkv_a")
            grads["q_norm_a"][j] = jnp.sum(dgain[:GROUP_A].reshape(2 * GROUP_A, HEAD_DIM), axis=0)
            grads["k_norm_a"][j] = jnp.sum(dgain[GROUP_A].reshape(2, HEAD_DIM), axis=0)
            grads["sinks_a"][j] = jnp.sum(dsink[..., 0], axis=0)
        else:
            do = _matmul(dy, wo_b[j], "nt", BF16, "d_o_b").reshape(NB, S, -1)
            grads["wo_b"][j] = _matmul(st["o"], dy, "tn", F32, "d_wo_b")
            dq, dk, dv = _attn_b_bwd(st["qkv"], do, rev, fwd)
            dqkv = jnp.concatenate([dq, dk, dv], axis=-1).reshape(T, -1).astype(BF16)
            w_in = wqkv_b[j]
            grads["wqkv_b"][j] = _matmul(st["h"], dqkv, "tn", F32, "d_wqkv_b")
        this_dg2 = dg2
        if i > 0:
            dx, dsh1, dsc1, grads["norm1_g"][i], dm, dg2 = _norm_mod_bwd(
                dqkv, w_in, st["x"], norm1_g[i:i + 1], st["sc1"], dx1, "d_h", y=saved[i - 1]["m"], g=saved[i - 1]["g2"])
        else:
            dx, dsh1, dsc1, grads["norm1_g"][i] = _norm_mod_bwd(
                dqkv, w_in, st["x"], norm1_g[i:i + 1], st["sc1"], dx1, "d_h")
        dmod[i] = jnp.concatenate([dsh1, dsc1, dg1, dsh2, dsc2, this_dg2], axis=-1).reshape(NB, 6 * D)

    matrices = ("wqkv_a", "wo_a", "wqkv_b", "wo_b", "wgu", "wd")
    grads = {name: parts if name in matrices else jnp.stack(parts) for name, parts in grads.items()}
    return loss, dx, grads, jnp.stack(dmod)


def _rows_of(flat, cols=PACK_COLS):
    n = flat.shape[0]
    pad = (-n) % (8 * cols)
    if pad:
        flat = jnp.concatenate([flat, jnp.zeros((pad,), flat.dtype)])
    return flat.reshape(-1, cols)


def kernel(x, c, positions, ada_w, ada_b, norm1_g, norm2_g, wqkv_a, q_norm_a, k_norm_a, sinks_a, wo_a, wqkv_b, wo_b, w_gate, w_up, w_down, loss_target, m_ada_w, m_ada_b, m_norm1_g, m_norm2_g, m_wqkv_a, m_q_norm_a, m_k_norm_a, m_sinks_a, m_wo_a, m_wqkv_b, m_wo_b, m_w_gate, m_w_up, m_w_down, v_ada_w, v_ada_b, v_norm1_g, v_norm2_g, v_wqkv_a, v_q_norm_a, v_k_norm_a, v_sinks_a, v_wo_a, v_wqkv_b, v_wo_b, v_w_gate, v_w_up, v_w_down):
    xi, yi, ci = lax.axis_index("x"), lax.axis_index("y"), lax.axis_index("c")
    dev = 4 * xi + 2 * yi + ci
    chip = 2 * xi + yi
    NB, S, D = x.shape
    B_all = N_DEV * NB
    L = ada_w.shape[0]
    n_mod = ada_w.shape[2] // 2

    c_all = _gather8(_rows_of(c.reshape(-1), LANES), "gather_c").reshape(N_DEV, -1)[:, :NB * D].reshape(B_all, D)
    ada_w_half = lax.dynamic_slice_in_dim(ada_w, ci * n_mod, n_mod, axis=2)
    ada_b_half = lax.dynamic_slice_in_dim(ada_b, dev * n_mod, n_mod, axis=1).reshape(L, 1, n_mod)
    mod_part = _ada_fwd(c_all, ada_w_half, ada_b_half)
    n_part = L * B_all * n_mod
    mod_all = _gather8(_rows_of(mod_part.reshape(-1)), "gather_mod").reshape(N_DEV, -1)[:, :n_part]
    mod_all = mod_all.reshape(N_DEV, L, B_all, n_mod).transpose(1, 2, 0, 3).reshape(L, B_all, N_DEV * n_mod)
    mod = lax.dynamic_slice_in_dim(mod_all, dev * NB, NB, axis=1)

    shards = dict(wqkv_a=wqkv_a, wo_a=wo_a, wqkv_b=wqkv_b, wo_b=wo_b, w_gate=w_gate, w_up=w_up, w_down=w_down)
    halves = []
    for name, _ in _SHARDED:
        w = shards[name]
        half = lax.dynamic_index_in_dim(w.reshape((2, w.shape[0] // 2) + w.shape[1:]), ci, 0, keepdims=False)
        halves.append(half.astype(BF16))
    gathered = _all_gather8(halves, "gather_weights", local_axis=1, local_chunks=8, relay_axis=1)
    full = {name: _unpack_full(t, axis) for (name, axis), t in zip(_SHARDED, gathered)}
    wgu = [_interleave(gate, up) for gate, up in zip(full["w_gate"], full["w_up"])]

    loss, grad_x, g, dmod = _local_step(
        x, positions, mod, norm1_g, norm2_g, q_norm_a, k_norm_a, sinks_a,
        full["wqkv_a"], full["wo_a"], full["wqkv_b"], full["wo_b"], wgu, full["w_down"], loss_target)

    g_full = dict(wqkv_a=g["wqkv_a"], wo_a=g["wo_a"], wqkv_b=g["wqkv_b"], wo_b=g["wo_b"],
                  w_gate=g["wgu"], w_up=g["wgu"], w_down=g["wd"])
    which = dict(w_gate=0, w_up=1)
    packed = [_pack_full(g_full[name], axis, which.get(name)) for name, axis in _SHARDED]
    def own(t, index):
        return lax.dynamic_index_in_dim(t, index, 0, keepdims=False)

    from_cores = _exchange_cores(packed, "rs_cores", chunk_axis=0, chunks=4)
    chip_part = [_sum_slabs(own(p, ci), r, "rs_add_cores", with_bf16=True) for p, r in zip(packed, from_cores)]
    from_chips = _exchange_chips([b for _, b in chip_part], "rs_chips", relay_axis=1)
    mine = [_sum_slabs(own(p, chip), r, "rs_add_chips") for (p, _), r in zip(chip_part, from_chips)]
    theirs = _sibling_send(mine, "rs_halves")
    grad = {}
    for (name, _), m, t in zip(_SHARDED, mine, theirs):
        first, second = jnp.where(ci == 0, m, t), jnp.where(ci == 0, t, m)
        grad[name] = jnp.stack([first, second]).reshape(shards[name].shape)

    small_names = ("norm1_g", "norm2_g", "q_norm_a", "k_norm_a", "sinks_a")
    small = [dmod.reshape(-1)] + [g[name].reshape(-1) for name in small_names] + [loss.reshape(-1)]
    small_sizes = [t.shape[0] for t in small]
    small_rows = _rows_of(jnp.concatenate(small))
    small_all = _gather8(small_rows, "gather_small")
    small_sum = _sum_leading(small_all, "sum_small").reshape(-1)
    n_dmod = small_sizes[0]
    dmod_all = small_all.reshape(N_DEV, -1)[:, :n_dmod].reshape(N_DEV, L, NB, 6 * D)
    dmod_all = dmod_all.transpose(1, 0, 2, 3).reshape(L, B_all, 6 * D)
    off = n_dmod
    for name, sz in zip(small_names + ("loss",), small_sizes[1:]):
        grad[name] = small_sum[off:off + sz]
        off += sz
    loss_total = grad.pop("loss").reshape(())
    for name, ref in (("norm1_g", norm1_g), ("norm2_g", norm2_g), ("q_norm_a", q_norm_a),
                      ("k_norm_a", k_norm_a), ("sinks_a", sinks_a)):
        grad[name] = grad[name].reshape(ref.shape)

    n_shard = ada_w.shape[2]
    dmod_shard = lax.dynamic_slice_in_dim(dmod_all, chip * n_shard, n_shard, axis=2)
    grad["ada_w"], gb = _ada_bwd(c_all, dmod_all, dmod_shard)
    grad["ada_b"] = gb.reshape(ada_b.shape)

    weights = dict(ada_w=ada_w, ada_b=ada_b, norm1_g=norm1_g, norm2_g=norm2_g, wqkv_a=wqkv_a, q_norm_a=q_norm_a,
                   k_norm_a=k_norm_a, sinks_a=sinks_a, wo_a=wo_a, wqkv_b=wqkv_b, wo_b=wo_b, w_gate=w_gate,
                   w_up=w_up, w_down=w_down)
    m_in = dict(ada_w=m_ada_w, ada_b=m_ada_b, norm1_g=m_norm1_g, norm2_g=m_norm2_g, wqkv_a=m_wqkv_a,
                q_norm_a=m_q_norm_a, k_norm_a=m_k_norm_a, sinks_a=m_sinks_a, wo_a=m_wo_a, wqkv_b=m_wqkv_b,
                wo_b=m_wo_b, w_gate=m_w_gate, w_up=m_w_up, w_down=m_w_down)
    v_in = dict(ada_w=v_ada_w, ada_b=v_ada_b, norm1_g=v_norm1_g, norm2_g=v_norm2_g, wqkv_a=v_wqkv_a,
                q_norm_a=v_q_norm_a, k_norm_a=v_k_norm_a, sinks_a=v_sinks_a, wo_a=v_wo_a, wqkv_b=v_wqkv_b,
                wo_b=v_wo_b, w_gate=v_w_gate, w_up=v_w_up, w_down=v_w_down)
    names = list(weights)
    delta, new_m, new_v = {}, {}, {}
    for name in names:
        delta[name], new_m[name], new_v[name] = _adamw(weights[name], grad[name], m_in[name], v_in[name],
                                                       "adamw_" + name)
    return (loss_total, grad_x, *[grad[k] for k in names], *[delta[k] for k in names],
            *[new_m[k] for k in names], *[new_v[k] for k in names])
```

```python
import jax
import jax.numpy as jnp
from jax import lax
from jax.experimental import pallas as pl
from jax.experimental.pallas import tpu as pltpu

F32 = jnp.float32
BF16 = jnp.bfloat16

DEPTH = 4
HEAD_DIM = 64
N_Q_A = 16
N_KV_A = 2
GROUP_A = N_Q_A // N_KV_A
N_H_B = 16
BLOCK = 128
ROT_DIM = HEAD_DIM // 4
ROPE_THETA = 500000.0
EPS = 1e-6
ATTN_SCALE = HEAD_DIM ** -0.5
NEG_BIG = -1e30

ADAM_LR = 0.001
ADAM_B1 = 0.9
ADAM_B2 = 0.999
ADAM_EPS = 1e-08
ADAM_WD = 0.01
ADAM_STEP = 10

N_DEV = 8
LANES = 128
PACK_COLS = 1024
VMEM_LIMIT_BYTES = 48 * 1024 * 1024
MESH = pl.DeviceIdType.MESH

_NT = (((1,), (1,)), ((), ()))
_TN = (((0,), (0,)), ((), ()))
_NN = (((1,), (0,)), ((), ()))


def _params(sem=None, vmem_limit_bytes=VMEM_LIMIT_BYTES):
    return pltpu.CompilerParams(vmem_limit_bytes=vmem_limit_bytes, dimension_semantics=sem)


def _pick(n, cap, mult):
    best = None
    for t in range(mult, min(n, cap) + 1, mult):
        if n % t == 0:
            best = t
    return n if best is None else best


_ANY = pl.BlockSpec(memory_space=pl.ANY)


def _window(index, axis, q, n, shape):
    rest = [slice(None)] * len(shape)
    size = shape[axis] // n
    rest[axis] = pl.ds(q * size, size)
    return tuple(index) + tuple(rest)


def _all_gather8(xs, name, local_axis=0, local_chunks=1, relay_axis=None):
    n = len(xs)
    n_sems = 7 if relay_axis is None else 9

    def body(*refs):
        x_refs, out_refs = refs[:n], refs[n:2 * n]
        send_sems, recv_sems, local_sems = refs[2 * n:]
        xi, yi, ci = lax.axis_index("x"), lax.axis_index("y"), lax.axis_index("c")
        me, sibling = (xi, yi, ci), (xi, yi, 1 - ci)
        chips = [(1 - xi, yi), (xi, 1 - yi), (1 - xi, 1 - yi)]

        def slab(w, px, py, pc):
            return out_refs[w].at[4 * px + 2 * py + pc]

        def copy(w, k, block, to, src=None):
            return pltpu.make_async_remote_copy(
                src_ref=slab(w, *block) if src is None else src, dst_ref=slab(w, *block),
                send_sem=send_sems.at[k, w], recv_sem=recv_sems.at[k, w], device_id=to, device_id_type=MESH)

        mine = []
        for w in range(n):
            for q in range(local_chunks):
                part = _window((), local_axis, q, local_chunks, xs[w].shape)
                mine.append(pltpu.make_async_copy(x_refs[w].at[part], slab(w, *me).at[part], local_sems.at[w, q]))
                mine[-1].start()
        direct = chips if relay_axis is None else chips[:2]
        first = [copy(w, 0, me, sibling, src=x_refs[w]) for w in range(n)]
        first += [copy(w, 1 + j, me, (*chip, ci), src=x_refs[w]) for j, chip in enumerate(direct) for w in range(n)]
        for cp in first:
            cp.start()

        def relay(w, part, block, to):
            piece = _window((), relay_axis, part, 2, xs[w].shape)
            return pltpu.make_async_remote_copy(
                src_ref=slab(w, *block).at[piece], dst_ref=slab(w, *block).at[piece],
                send_sem=send_sems.at[7 + part, w], recv_sem=recv_sems.at[7 + part, w],
                device_id=to, device_id_type=MESH)

        passed = []
        for j, chip in enumerate(direct):
            for w in range(n):
                copy(w, 1 + j, (*chip, ci), me).wait_recv()
                passed.append(copy(w, 4 + j, (*chip, ci), sibling))
                passed[-1].start()
                if relay_axis is not None:
                    passed.append(relay(w, j, (*chip, ci), (*chips[1 - j], ci)))
                    passed[-1].start()
        if relay_axis is not None:
            for w in range(n):
                for part in range(2):
                    relay(w, part, (*chips[2], ci), me).wait_recv()
                passed.append(copy(w, 6, (*chips[2], ci), sibling))
                passed[-1].start()
        for w in range(n):
            copy(w, 0, sibling, me).wait_recv()
        for j, chip in enumerate(chips):
            for w in range(n):
                copy(w, 4 + j, (*chip, 1 - ci), me).wait_recv()
        for cp in first + passed:
            cp.wait_send()
        for cp in mine:
            cp.wait()

    return pl.pallas_call(
        body, name=name,
        out_shape=[jax.ShapeDtypeStruct((N_DEV,) + x.shape, x.dtype) for x in xs],
        in_specs=[_ANY] * n, out_specs=[_ANY] * n,
        scratch_shapes=[pltpu.SemaphoreType.DMA((n_sems, n)), pltpu.SemaphoreType.DMA((n_sems, n)),
                        pltpu.SemaphoreType.DMA((n, local_chunks))],
    )(*xs)


def _exchange_cores(xs, name, chunk_axis=0, chunks=1):
    n = len(xs)
    n_peers = 1

    def body(*refs):
        x_refs, out_refs = refs[:n], refs[n:2 * n]
        send_sems, recv_sems = refs[2 * n:]
        xi, yi, ci = lax.axis_index("x"), lax.axis_index("y"), lax.axis_index("c")
        peers = [(1 - ci, (xi, yi, 1 - ci))]
        copies = []
        for k, (p, dev) in enumerate(peers):
            for w in range(n):
                slab_shape = xs[w].shape[1:]
                for q in range(chunks):
                    copies.append(pltpu.make_async_remote_copy(
                        src_ref=x_refs[w].at[_window((p,), chunk_axis, q, chunks, slab_shape)],
                        dst_ref=out_refs[w].at[_window((k,), chunk_axis, q, chunks, slab_shape)],
                        send_sem=send_sems.at[k, w, q], recv_sem=recv_sems.at[k, w, q],
                        device_id=dev, device_id_type=MESH))
                    copies[-1].start()
        for cp in copies:
            cp.wait()

    return pl.pallas_call(
        body, name=name,
        out_shape=[jax.ShapeDtypeStruct((n_peers,) + x.shape[1:], x.dtype) for x in xs],
        in_specs=[_ANY] * n, out_specs=[_ANY] * n,
        scratch_shapes=[pltpu.SemaphoreType.DMA((n_peers, n, chunks)), pltpu.SemaphoreType.DMA((n_peers, n, chunks))],
    )(*xs)


def _exchange_chips(xs, name, relay_axis):
    n = len(xs)

    def half_shape(x):
        shape = list(x.shape[1:])
        shape[relay_axis] //= 2
        return tuple(shape)

    def body(*refs):
        x_refs, out_refs, hop_refs = refs[:n], refs[n:2 * n], refs[2 * n:3 * n]
        send_sems, recv_sems = refs[3 * n:]
        xi, yi, ci = lax.axis_index("x"), lax.axis_index("y"), lax.axis_index("c")
        nbr = [(1 - xi, yi, ci), (xi, 1 - yi, ci)]
        slab_of_nbr = [2 * (1 - xi) + yi, 2 * xi + (1 - yi)]
        slab_of_diag = 2 * (1 - xi) + (1 - yi)

        def copy(k, w, src, dst, to):
            return pltpu.make_async_remote_copy(src_ref=src, dst_ref=dst, send_sem=send_sems.at[k, w],
                                                recv_sem=recv_sems.at[k, w], device_id=to, device_id_type=MESH)

        def piece(w, part):
            return _window((), relay_axis, part, 2, xs[w].shape[1:])

        sent = []
        for w in range(n):
            for j in range(2):
                sent.append(copy(j, w, x_refs[w].at[slab_of_nbr[j]], out_refs[w].at[j], nbr[j]))
                sent.append(copy(2 + j, w, x_refs[w].at[(slab_of_diag,) + piece(w, j)], hop_refs[w].at[j], nbr[j]))
        for cp in sent:
            cp.start()
        for w in range(n):
            for j in range(2):
                copy(2 + j, w, hop_refs[w].at[j], hop_refs[w].at[j], nbr[j]).wait_recv()
                sent.append(copy(4 + j, w, hop_refs[w].at[j], out_refs[w].at[(2,) + piece(w, j)], nbr[1 - j]))
                sent[-1].start()
        for w in range(n):
            for j in range(2):
                copy(j, w, out_refs[w].at[j], out_refs[w].at[j], nbr[j]).wait_recv()
                half = out_refs[w].at[(2,) + piece(w, j)]
                copy(4 + j, w, half, half, nbr[1 - j]).wait_recv()
        for cp in sent:
            cp.wait_send()

    out = pl.pallas_call(
        body, name=name,
        out_shape=[jax.ShapeDtypeStruct((3,) + x.shape[1:], x.dtype) for x in xs]
        + [jax.ShapeDtypeStruct((2,) + half_shape(x), x.dtype) for x in xs],
        in_specs=[_ANY] * n, out_specs=[_ANY] * (2 * n),
        scratch_shapes=[pltpu.SemaphoreType.DMA((6, n)), pltpu.SemaphoreType.DMA((6, n))],
    )(*xs)
    return out[:n]


def _sibling_send(xs, name, chunk_axis=1, chunks=4):
    n = len(xs)

    def body(*refs):
        x_refs, out_refs = refs[:n], refs[n:2 * n]
        send_sems, recv_sems = refs[2 * n:]
        xi, yi, ci = lax.axis_index("x"), lax.axis_index("y"), lax.axis_index("c")
        copies = []
        for w in range(n):
            for q in range(chunks):
                part = _window((), chunk_axis, q, chunks, xs[w].shape)
                copies.append(pltpu.make_async_remote_copy(
                    src_ref=x_refs[w].at[part], dst_ref=out_refs[w].at[part],
                    send_sem=send_sems.at[w, q], recv_sem=recv_sems.at[w, q],
                    device_id=(xi, yi, 1 - ci), device_id_type=MESH))
                copies[-1].start()
        for cp in copies:
            cp.wait()

    return pl.pallas_call(
        body, name=name,
        out_shape=[jax.ShapeDtypeStruct(x.shape, x.dtype) for x in xs],
        in_specs=[_ANY] * n, out_specs=[_ANY] * n,
        scratch_shapes=[pltpu.SemaphoreType.DMA((n, chunks)), pltpu.SemaphoreType.DMA((n, chunks))],
    )(*xs)


def _sum_leading(x, name, own=None, with_bf16=False):
    P, R, C = x.shape
    tr = _pick(R, max(16, (1 << 19) // (C * (P + 1))), 16)

    def body(*refs):
        n_in = 1 if own is None else 2
        x_ref = refs[n_in - 1]
        acc = x_ref[0].astype(F32) if own is None else refs[0][...] + x_ref[0].astype(F32)
        for p in range(1, P):
            acc = acc + x_ref[p].astype(F32)
        refs[n_in][...] = acc
        if with_bf16:
            refs[n_in + 1][...] = acc.astype(BF16)

    flat = pl.BlockSpec((tr, C), lambda r: (r, 0))
    slabs = pl.BlockSpec((P, tr, C), lambda r: (0, r, 0))
    out = pl.pallas_call(
        body, name=name, grid=(R // tr,),
        in_specs=[slabs] if own is None else [flat, slabs],
        out_specs=[flat, flat] if with_bf16 else [flat],
        out_shape=[jax.ShapeDtypeStruct((R, C), F32)] + ([jax.ShapeDtypeStruct((R, C), BF16)] if with_bf16 else []),
        compiler_params=_params(("arbitrary",)),
    )(*([x] if own is None else [own, x]))
    return out if with_bf16 else out[0]


MATMUL_SINGLE_K = 1280
MATMUL_VMEM_BUDGET = 36 * 1024 * 1024


def _matmul(a, b, mode, out_dtype, name, swiglu=False):
    if mode == "nn":
        (M, K), N = a.shape, b.shape[1]
    elif mode == "nt":
        (M, K), N = a.shape, b.shape[0]
    else:
        (K, M), N = a.shape, b.shape[1]
    tm = _pick(M, 1024 if mode != "tn" else 1536, 128)
    tn = _pick(N, 1536, 128)
    if swiglu:
        tm, tn = _pick(M, 1024 if out_dtype == BF16 else 512, 128), 2 * _ff_tile(N // 2)
    out_bytes = jnp.dtype(out_dtype).itemsize
    tk = K
    if K > MATMUL_SINGLE_K:
        for cap in (2048, 1024, 512):
            tk = _pick(K, cap, 128)
            blocks = 2 * 2 * tk * (tm + tn) + tm * tn * (2 * out_bytes + (4 if out_dtype != F32 else 0))
            if blocks <= MATMUL_VMEM_BUDGET:
                break
    nk = K // tk
    dims = {"nn": _NN, "nt": _NT, "tn": _TN}[mode]
    use_scratch = nk > 1 and out_dtype != F32

    def body(a_ref, b_ref, *refs):
        o_ref = refs[0]

        def product():
            return lax.dot_general(a_ref[...].astype(BF16), b_ref[...].astype(BF16), dims,
                                   preferred_element_type=F32)

        if nk == 1:
            part = product()
            o_ref[...] = part.astype(o_ref.dtype)
            if swiglu:
                g = part[:, :tn // 2]
                refs[1][...] = (g * _sigmoid(g) * part[:, tn // 2:]).astype(BF16)
            return
        k = pl.program_id(2)
        acc_ref = refs[-1] if use_scratch else o_ref

        @pl.when(k == 0)
        def _():
            acc_ref[...] = jnp.zeros_like(acc_ref)

        acc_ref[...] += product()

        if use_scratch:
            @pl.when(k == nk - 1)
            def _():
                o_ref[...] = acc_ref[...].astype(o_ref.dtype)

    if mode == "tn":
        a_spec = pl.BlockSpec((tk, tm), lambda i, j, k: (k, i))
    else:
        a_spec = pl.BlockSpec((tm, tk), lambda i, j, k: (i, k))
    if mode == "nt":
        b_spec = pl.BlockSpec((tn, tk), lambda i, j, k: (j, k))
    else:
        b_spec = pl.BlockSpec((tk, tn), lambda i, j, k: (k, j))
    out_specs = [pl.BlockSpec((tm, tn), lambda i, j, k: (i, j))]
    out_shape = [jax.ShapeDtypeStruct((M, N), out_dtype)]
    if swiglu:
        assert nk == 1 and mode == "nn"
        out_specs.append(pl.BlockSpec((tm, tn // 2), lambda i, j, k: (i, j)))
        out_shape.append(jax.ShapeDtypeStruct((M, N // 2), BF16))
    out = pl.pallas_call(
        body, name=name, grid=(M // tm, N // tn, nk),
        in_specs=[a_spec, b_spec], out_specs=out_specs, out_shape=out_shape,
        scratch_shapes=[pltpu.VMEM((tm, tn), F32)] if use_scratch else [],
        compiler_params=_params(("parallel", "parallel", "arbitrary")),
    )(a, b)
    return out if swiglu else out[0]


def _row_tile(S):
    return _pick(S, 512, 8)


def _norm_mod_fwd(x, gain, sc, sh):
    NB, S, D = x.shape
    tr = _row_tile(S)

    def body(x_ref, g_ref, sc_ref, sh_ref, h_ref):
        xv = x_ref[...]
        ms = jnp.mean(xv * xv, axis=-1, keepdims=True)
        n = xv * lax.rsqrt(ms + EPS) * g_ref[...]
        h_ref[...] = (n * (1.0 + sc_ref[...]) + sh_ref[...]).astype(BF16)

    tok = pl.BlockSpec((None, tr, D), lambda b, r: (b, r, 0))
    per_ex = pl.BlockSpec((None, 1, D), lambda b, r: (b, 0, 0))
    return pl.pallas_call(
        body, name="norm_mod_fwd", grid=(NB, S // tr),
        in_specs=[tok, pl.BlockSpec((1, D), lambda b, r: (0, 0)), per_ex, per_ex],
        out_specs=tok, out_shape=jax.ShapeDtypeStruct((NB, S, D), BF16),
        compiler_params=_params(("parallel", "parallel")),
    )(x, gain, sc, sh)


def _norm_mod_bwd(a, w, x, gain, sc, dres, name, y=None, g=None):
    NB, S, D = x.shape
    T, K = a.shape
    tm = _pick(S, 512, 128)
    per_ex_tiles = S // tm
    tk = K if K <= MATMUL_SINGLE_K else _pick(K, 1536, 128)
    nk = K // tk
    gated = y is not None

    def body(*refs):
        a_ref, w_ref, x_ref, g_ref, sc_ref, dres_ref = refs[:6]
        refs = refs[6:]
        if gated:
            y_ref, gate_ref = refs[:2]
            refs = refs[2:]
        dx_ref, dsh_ref, dsc_ref, dgain_ref = refs[:4]
        acc_ref = refs[-1]
        i, k = pl.program_id(0), pl.program_id(1)

        @pl.when(k == 0)
        def _():
            acc_ref[...] = jnp.zeros_like(acc_ref)

        acc_ref[...] += lax.dot_general(a_ref[...], w_ref[...], _NT, preferred_element_type=F32)

        @pl.when(k == nk - 1)
        def _():
            first_of_example = i % per_ex_tiles == 0

            @pl.when(first_of_example)
            def _():
                dsh_ref[...] = jnp.zeros_like(dsh_ref)
                dsc_ref[...] = jnp.zeros_like(dsc_ref)
                if gated:
                    refs[5][...] = jnp.zeros_like(refs[5])

            @pl.when(i == 0)
            def _():
                dgain_ref[...] = jnp.zeros_like(dgain_ref)

            xv = x_ref[...]
            rstd = lax.rsqrt(jnp.mean(xv * xv, axis=-1, keepdims=True) + EPS)
            xh = xv * rstd
            gn = g_ref[...]
            dh = acc_ref[...]
            dsh_ref[...] += jnp.sum(dh, axis=0, keepdims=True)
            dsc_ref[...] += jnp.sum(dh * (xh * gn), axis=0, keepdims=True)
            dn = dh * (1.0 + sc_ref[...])
            dgain_ref[...] += jnp.sum(dn * xh, axis=0, keepdims=True)
            dxh = dn * gn
            proj = jnp.mean(dxh * xh, axis=-1, keepdims=True)
            dx = rstd * (dxh - xh * proj) + dres_ref[...]
            dx_ref[...] = dx
            if gated:
                refs[4][...] = (dx * gate_ref[...]).astype(BF16)
                refs[5][...] += jnp.sum(dx * y_ref[...], axis=0, keepdims=True)

    tok = pl.BlockSpec((None, tm, D), lambda i, k: (i // per_ex_tiles, i % per_ex_tiles, 0))
    per_ex = pl.BlockSpec((None, 1, D), lambda i, k: (i // per_ex_tiles, 0, 0))
    row = pl.BlockSpec((1, D), lambda i, k: (0, 0))
    in_specs = [pl.BlockSpec((tm, tk), lambda i, k: (i, k)), pl.BlockSpec((D, tk), lambda i, k: (0, k)),
                tok, row, per_ex, tok]
    out_specs = [tok, per_ex, per_ex, row]
    out_shape = [jax.ShapeDtypeStruct((NB, S, D), F32), jax.ShapeDtypeStruct((NB, 1, D), F32),
                 jax.ShapeDtypeStruct((NB, 1, D), F32), jax.ShapeDtypeStruct((1, D), F32)]
    operands = [a, w, x, gain, sc, dres]
    if gated:
        in_specs += [tok, per_ex]
        out_specs += [tok, per_ex]
        out_shape += [jax.ShapeDtypeStruct((NB, S, D), BF16), jax.ShapeDtypeStruct((NB, 1, D), F32)]
        operands += [y, g]
    return pl.pallas_call(
        body, name=name, grid=(T // tm, nk), in_specs=in_specs, out_specs=out_specs, out_shape=out_shape,
        scratch_shapes=[pltpu.VMEM((tm, D), F32)],
        compiler_params=_params(("arbitrary", "arbitrary")),
    )(*operands)


def _gate_res(x, y, g, norm=None):
    NB, S, D = x.shape
    tr = _row_tile(S)

    def body(x_ref, y_ref, g_ref, *refs):
        xo = x_ref[...] + g_ref[...] * y_ref[...]
        refs[-1 if norm is None else -2][...] = xo
        if norm is not None:
            gain_ref, sc_ref, sh_ref, _, h_ref = refs
            n = xo * lax.rsqrt(jnp.mean(xo * xo, axis=-1, keepdims=True) + EPS) * gain_ref[...]
            h_ref[...] = (n * (1.0 + sc_ref[...]) + sh_ref[...]).astype(BF16)

    tok = pl.BlockSpec((None, tr, D), lambda b, r: (b, r, 0))
    per_ex = pl.BlockSpec((None, 1, D), lambda b, r: (b, 0, 0))
    in_specs, out_specs, operands = [tok, tok, per_ex], [tok], [x, y, g]
    out_shape = [jax.ShapeDtypeStruct((NB, S, D), F32)]
    if norm is not None:
        in_specs += [pl.BlockSpec((1, D), lambda b, r: (0, 0)), per_ex, per_ex]
        out_specs.append(tok)
        out_shape.append(jax.ShapeDtypeStruct((NB, S, D), BF16))
        operands += list(norm)
    out = pl.pallas_call(
        body, name="gate_res", grid=(NB, S // tr), in_specs=in_specs, out_specs=out_specs, out_shape=out_shape,
        compiler_params=_params(("parallel", "parallel")),
    )(*operands)
    return out[0] if norm is None else out


def _gate_res_bwd(dxo, y, g):
    NB, S, D = dxo.shape
    tr = _row_tile(S)

    def body(d_ref, y_ref, g_ref, dy_ref, dg_ref):
        @pl.when(pl.program_id(1) == 0)
        def _():
            dg_ref[...] = jnp.zeros_like(dg_ref)

        d = d_ref[...]
        dy_ref[...] = (d * g_ref[...]).astype(BF16)
        dg_ref[...] += jnp.sum(d * y_ref[...], axis=0, keepdims=True)

    tok = pl.BlockSpec((None, tr, D), lambda b, r: (b, r, 0))
    per_ex = pl.BlockSpec((None, 1, D), lambda b, r: (b, 0, 0))
    return pl.pallas_call(
        body, name="gate_res_bwd", grid=(NB, S // tr), in_specs=[tok, tok, per_ex], out_specs=[tok, per_ex],
        out_shape=[jax.ShapeDtypeStruct((NB, S, D), BF16), jax.ShapeDtypeStruct((NB, 1, D), F32)],
        compiler_params=_params(("arbitrary", "arbitrary")),
    )(dxo, y, g)


def _sigmoid(v):
    return 1.0 / (1.0 + jnp.exp(-v))


def _ff_tile(F):
    return _pick(F, 1536, 128)


def _interleave(gate, up):
    F = gate.shape[-1]
    tf = _ff_tile(F)
    parts = []
    for j in range(F // tf):
        parts += [gate[..., j * tf:(j + 1) * tf], up[..., j * tf:(j + 1) * tf]]
    return jnp.concatenate(parts, axis=-1)


def _swiglu_bwd(dm, wd, gu):
    T, D = dm.shape
    F = wd.shape[0]
    tf = _ff_tile(F)
    tm = _pick(T, 1024, 128)
    assert D <= MATMUL_SINGLE_K

    def body(a_ref, b_ref, gu_ref, o_ref):
        d = lax.dot_general(a_ref[...], b_ref[...], _NT, preferred_element_type=F32)
        g, u = gu_ref[:, :tf].astype(F32), gu_ref[:, tf:].astype(F32)
        s = _sigmoid(g)
        o_ref[:, :tf] = (d * u * (s * (1.0 + g * (1.0 - s)))).astype(BF16)
        o_ref[:, tf:] = (d * (g * s)).astype(BF16)

    return pl.pallas_call(
        body, name="swiglu_bwd", grid=(T // tm, F // tf),
        in_specs=[pl.BlockSpec((tm, D), lambda i, j: (i, 0)), pl.BlockSpec((tf, D), lambda i, j: (j, 0)),
                  pl.BlockSpec((tm, 2 * tf), lambda i, j: (i, j))],
        out_specs=pl.BlockSpec((tm, 2 * tf), lambda i, j: (i, j)),
        out_shape=jax.ShapeDtypeStruct((T, 2 * F), BF16),
        compiler_params=_params(("parallel", "parallel")),
    )(dm, wd, gu)


def _loss_fwd_bwd(y, target):
    NB, S, D = y.shape
    tr = _row_tile(S)

    def body(y_ref, t_ref, l_ref, d_ref):
        @pl.when((pl.program_id(0) == 0) & (pl.program_id(1) == 0))
        def _():
            l_ref[...] = jnp.zeros_like(l_ref)

        e = y_ref[...] - t_ref[...]
        d_ref[...] = e / D
        l_ref[...] += 0.5 * jnp.sum(jnp.mean(e * e, axis=-1, keepdims=True), axis=0, keepdims=True)

    tok = pl.BlockSpec((None, tr, D), lambda b, r: (b, r, 0))
    return pl.pallas_call(
        body, name="loss", grid=(NB, S // tr), in_specs=[tok, tok],
        out_specs=[pl.BlockSpec((1, 1), lambda b, r: (0, 0)), tok],
        out_shape=[jax.ShapeDtypeStruct((1, 1), F32), jax.ShapeDtypeStruct((NB, S, D), F32)],
        compiler_params=_params(("arbitrary", "arbitrary")),
    )(y, target)


def _half_sums(v, lo):
    sa = jnp.sum(jnp.where(lo, v, 0.0), axis=-1, keepdims=True)
    sb = jnp.sum(jnp.where(lo, 0.0, v), axis=-1, keepdims=True)
    return jnp.where(lo, sa, sb)


def _rope_swap(v, lane64):
    up = pltpu.roll(v, LANES - ROT_DIM // 2, 1)
    down = pltpu.roll(v, ROT_DIM // 2, 1)
    return jnp.where(lane64 < ROT_DIM // 2, up, jnp.where(lane64 < ROT_DIM, down, 0.0))


def _qk_prep_fwd(qkv, tab_c, tab_s, gains):
    T, W = qkv.shape
    R = W // LANES
    tt = _pick(T, 256, 8)

    def body(x_ref, c_ref, s_ref, g_ref, o_ref):
        lane = lax.broadcasted_iota(jnp.int32, (tt, LANES), 1)
        lo = lane < HEAD_DIM
        lane64 = lane & (HEAD_DIM - 1)
        c, s = c_ref[...], s_ref[...]
        for j in range(R - 1):
            cols = slice(j * LANES, (j + 1) * LANES)
            xv = x_ref[:, cols]
            rstd = lax.rsqrt(_half_sums(xv * xv, lo) / HEAD_DIM + EPS)
            yn = xv * rstd * g_ref[j:j + 1, :]
            o_ref[:, cols] = (yn * c + _rope_swap(yn, lane64) * s).astype(BF16)
        o_ref[:, (R - 1) * LANES:] = x_ref[:, (R - 1) * LANES:].astype(BF16)

    tok = pl.BlockSpec((tt, W), lambda t: (t, 0))
    tab = pl.BlockSpec((tt, LANES), lambda t: (t, 0))
    return pl.pallas_call(
        body, name="qk_prep_fwd", grid=(T // tt,),
        in_specs=[tok, tab, tab, pl.BlockSpec((R, LANES), lambda t: (0, 0))],
        out_specs=tok, out_shape=jax.ShapeDtypeStruct((T, W), BF16),
        compiler_params=_params(("parallel",)),
    )(qkv, tab_c, tab_s, gains)


def _qk_prep_bwd(qkv, dq, dk, dv, tab_c, tab_s, gains):
    T, W = qkv.shape
    R = W // LANES
    QW = dq.shape[1]
    tt = _pick(T, 256, 8)

    def body(x_ref, dq_ref, dk_ref, dv_ref, c_ref, s_ref, g_ref, o_ref, dg_ref):
        @pl.when(pl.program_id(0) == 0)
        def _():
            dg_ref[...] = jnp.zeros_like(dg_ref)

        lane = lax.broadcasted_iota(jnp.int32, (tt, LANES), 1)
        lo = lane < HEAD_DIM
        lane64 = lane & (HEAD_DIM - 1)
        c, s = c_ref[...], s_ref[...]
        for j in range(R - 1):
            cols = slice(j * LANES, (j + 1) * LANES)
            xv = x_ref[:, cols]
            d = dq_ref[:, cols] if j < R - 2 else dk_ref[...]
            rstd = lax.rsqrt(_half_sums(xv * xv, lo) / HEAD_DIM + EPS)
            xh = xv * rstd
            dyn = d * c + _rope_swap(d * s, lane64)
            dg_ref[j:j + 1, :] += jnp.sum(dyn * xh, axis=0, keepdims=True)
            dxh = dyn * g_ref[j:j + 1, :]
            proj = _half_sums(dxh * xh, lo) / HEAD_DIM
            o_ref[:, cols] = (rstd * (dxh - xh * proj)).astype(BF16)
        o_ref[:, (R - 1) * LANES:] = dv_ref[...].astype(BF16)

    tok = pl.BlockSpec((tt, W), lambda t: (t, 0))
    tab = pl.BlockSpec((tt, LANES), lambda t: (t, 0))
    gsp = pl.BlockSpec((R, LANES), lambda t: (0, 0))
    return pl.pallas_call(
        body, name="qk_prep_bwd", grid=(T // tt,),
        in_specs=[tok, pl.BlockSpec((tt, QW), lambda t: (t, 0)), tab, tab, tab, tab, gsp], out_specs=[tok, gsp],
        out_shape=[jax.ShapeDtypeStruct((T, W), BF16), jax.ShapeDtypeStruct((R, LANES), F32)],
        compiler_params=_params(("arbitrary",)),
    )(qkv, dq, dk, dv, tab_c, tab_s, gains)


def _band_mask(i):
    r = lax.broadcasted_iota(jnp.int32, (2 * BLOCK, 2 * BLOCK), 0) & (BLOCK - 1)
    c = lax.broadcasted_iota(jnp.int32, (2 * BLOCK, 2 * BLOCK), 1)
    rel = r + BLOCK - c
    return (rel >= 0) & (rel < BLOCK) & ((c >= BLOCK) | (i > 0))


def _swa_softmax(s, valid, sink):
    s = jnp.where(valid, s * ATTN_SCALE, NEG_BIG)
    m = jnp.maximum(jnp.max(s, axis=1, keepdims=True), sink)
    p = jnp.exp(s - m)
    ps = jnp.exp(sink - m)
    denom = jnp.sum(p, axis=1, keepdims=True) + ps
    return p / denom, ps / denom


A_GROUP = 4


Q_WIDTH_A = N_Q_A * HEAD_DIM
N_PAIR_A = Q_WIDTH_A // LANES


def _swa_specs():
    qs = pl.BlockSpec((None, BLOCK, Q_WIDTH_A), lambda b, i: (b, i, 0))

    def kv(col, back):
        return pl.BlockSpec((None, BLOCK, LANES), lambda b, i: (b, jnp.maximum(i - back, 0), col))

    return qs, kv(N_PAIR_A, 1), kv(N_PAIR_A, 0), kv(N_PAIR_A + 1, 1), kv(N_PAIR_A + 1, 0)


def _dup_heads(t):
    lo = lax.broadcasted_iota(jnp.int32, t.shape, 1) < HEAD_DIM
    sw = pltpu.roll(t.astype(F32), HEAD_DIM, 1).astype(BF16)
    return jnp.where(lo, t, sw), jnp.where(lo, sw, t)


def _kv_tiles(kp_ref, kc_ref, vp_ref, vc_ref):
    kd = _dup_heads(jnp.concatenate([kp_ref[...], kc_ref[...]], axis=0))
    vd = _dup_heads(jnp.concatenate([vp_ref[...], vc_ref[...]], axis=0))
    return kd, vd


def _attn_a_fwd(qkn, sinks):
    NB, S, _ = qkn.shape
    qs, kp, kc, vp, vc = _swa_specs()

    def body(q_ref, kp_ref, kc_ref, vp_ref, vc_ref, sink_ref, o_ref):
        i = pl.program_id(1)
        kd, vd = _kv_tiles(kp_ref, kc_ref, vp_ref, vc_ref)
        valid = _band_mask(i)
        lo = lax.broadcasted_iota(jnp.int32, (BLOCK, LANES), 1) < HEAD_DIM
        top = lax.broadcasted_iota(jnp.int32, (2 * BLOCK, 1), 0) < BLOCK
        for first in range(0, N_PAIR_A, A_GROUP):
            pairs = range(first, first + A_GROUP)
            qs_ = [jnp.concatenate(_head_halves(q_ref[:, p * LANES:(p + 1) * LANES], lo), axis=0) for p in pairs]
            ss = [lax.dot_general(q, kd[2 * p // GROUP_A], _NT, preferred_element_type=F32) for q, p in zip(qs_, pairs)]
            pns = [_swa_softmax(s, valid, jnp.where(top, sink_ref[2 * p], sink_ref[2 * p + 1]))[0]
                   for s, p in zip(ss, pairs)]
            pvs = [jnp.dot(pn.astype(BF16), vd[2 * p // GROUP_A], preferred_element_type=F32) for pn, p in zip(pns, pairs)]
            for pv, p in zip(pvs, pairs):
                o_ref[:, p * LANES:(p + 1) * LANES] = jnp.where(lo, pv[:BLOCK], pv[BLOCK:]).astype(BF16)

    return pl.pallas_call(
        body, name="attn_a_fwd", grid=(NB, S // BLOCK),
        in_specs=[qs, kp, kc, vp, vc, pl.BlockSpec(memory_space=pltpu.SMEM)],
        out_specs=qs, out_shape=jax.ShapeDtypeStruct((NB, S, Q_WIDTH_A), BF16),
        compiler_params=_params(("parallel", "arbitrary")),
    )(qkn, qkn, qkn, qkn, qkn, sinks)


def _attn_a_bwd(qkn, do, sinks):
    NB, S, _ = qkn.shape
    qs, kp, kc, vp, vc = _swa_specs()
    full = pl.BlockSpec((None, S, LANES), lambda b, i: (b, 0, 0))
    sink_out = pl.BlockSpec((None, N_Q_A, LANES), lambda b, i: (b, 0, 0))

    def body(q_ref, do_ref, kp_ref, kc_ref, vp_ref, vc_ref, sink_ref, dq_ref, dk_ref, dv_ref, ds_ref, dk_s, dv_s):
        i = pl.program_id(1)

        @pl.when(i == 0)
        def _():
            dk_ref[...] = jnp.zeros_like(dk_ref)
            dv_ref[...] = jnp.zeros_like(dv_ref)
            ds_ref[...] = jnp.zeros_like(ds_ref)

        dk_s[...] = jnp.zeros_like(dk_s)
        dv_s[...] = jnp.zeros_like(dv_s)
        kd, vd = _kv_tiles(kp_ref, kc_ref, vp_ref, vc_ref)
        valid = _band_mask(i)
        lo = lax.broadcasted_iota(jnp.int32, (BLOCK, LANES), 1) < HEAD_DIM
        top = lax.broadcasted_iota(jnp.int32, (2 * BLOCK, 1), 0) < BLOCK
        for first in range(0, N_PAIR_A, A_GROUP):
            pairs = range(first, first + A_GROUP)
            kvs = [2 * p // GROUP_A for p in pairs]
            qs_ = [jnp.concatenate(_head_halves(q_ref[:, p * LANES:(p + 1) * LANES], lo), axis=0) for p in pairs]
            dos = [jnp.concatenate(_head_halves(do_ref[:, p * LANES:(p + 1) * LANES], lo), axis=0) for p in pairs]
            ss = [lax.dot_general(q, kd[kv], _NT, preferred_element_type=F32) for q, kv in zip(qs_, kvs)]
            dps = [lax.dot_general(d, vd[kv], _NT, preferred_element_type=F32) for d, kv in zip(dos, kvs)]
            sm = [_swa_softmax(s, valid, jnp.where(top, sink_ref[2 * p], sink_ref[2 * p + 1])) for s, p in zip(ss, pairs)]
            deltas = [jnp.sum(pn * dp, axis=1, keepdims=True) for (pn, _), dp in zip(sm, dps)]
            dsbs = [(pn * (dp - delta) * ATTN_SCALE).astype(BF16) for (pn, _), dp, delta in zip(sm, dps, deltas)]
            for n, p in enumerate(pairs):
                dq2 = jnp.dot(dsbs[n], kd[kvs[n]], preferred_element_type=F32)
                dq_ref[:, p * LANES:(p + 1) * LANES] = jnp.where(lo, dq2[:BLOCK], dq2[BLOCK:])
                dk_s[kvs[n]] += lax.dot_general(dsbs[n], qs_[n], _TN, preferred_element_type=F32)
                dv_s[kvs[n]] += lax.dot_general(sm[n][0].astype(BF16), dos[n], _TN, preferred_element_type=F32)
                t = sm[n][1] * deltas[n]
                for hh in range(2):
                    dsink = -jnp.sum(t[hh * BLOCK:(hh + 1) * BLOCK], axis=0, keepdims=True)
                    ds_ref[2 * p + hh:2 * p + hh + 1, :] += jnp.broadcast_to(dsink, (1, LANES))

        lo2 = lax.broadcasted_iota(jnp.int32, (2 * BLOCK, LANES), 1) < HEAD_DIM

        def fold(acc):
            halves = [acc[kv] + pltpu.roll(acc[kv], HEAD_DIM, 1) for kv in range(N_KV_A)]
            return jnp.where(lo2, halves[0], halves[1])

        dk2, dv2 = fold(dk_s), fold(dv_s)

        @pl.when(i > 0)
        def _():
            start = pl.multiple_of((i - 1) * BLOCK, BLOCK)
            dk_ref[pl.ds(start, 2 * BLOCK), :] += dk2
            dv_ref[pl.ds(start, 2 * BLOCK), :] += dv2

        @pl.when(i == 0)
        def _():
            dk_ref[0:BLOCK, :] += dk2[BLOCK:, :]
            dv_ref[0:BLOCK, :] += dv2[BLOCK:, :]

    slots = pltpu.VMEM((N_KV_A, 2 * BLOCK, LANES), F32)
    return pl.pallas_call(
        body, name="attn_a_bwd", grid=(NB, S // BLOCK),
        in_specs=[qs, qs, kp, kc, vp, vc, pl.BlockSpec(memory_space=pltpu.SMEM)],
        out_specs=[qs, full, full, sink_out],
        out_shape=[jax.ShapeDtypeStruct((NB, S, Q_WIDTH_A), F32), jax.ShapeDtypeStruct((NB, S, LANES), F32),
                   jax.ShapeDtypeStruct((NB, S, LANES), F32), jax.ShapeDtypeStruct((NB, N_Q_A, LANES), F32)],
        scratch_shapes=[slots, slots],
        compiler_params=_params(("parallel", "arbitrary")),
    )(qkn, do, qkn, qkn, qkn, qkn, sinks)


def _cumsum_mats():
    src = lax.broadcasted_iota(jnp.int32, (2 * BLOCK, 2 * BLOCK), 0) % BLOCK
    dst = lax.broadcasted_iota(jnp.int32, (2 * BLOCK, 2 * BLOCK), 1)
    ones = dst >= BLOCK
    rev = ((src > dst) | ones).astype(BF16)
    fwd = ((src < dst) | ones).astype(BF16)
    return rev, fwd


def _log_sigmoids(z):
    sp = jnp.log(1.0 + jnp.exp(-jnp.abs(z)))
    return jnp.minimum(z, 0.0) - sp, -(jnp.maximum(z, 0.0) + sp)


def _cumsum_mxu_many(vs, mat):
    parts = []
    for v in vs:
        hi = v.astype(BF16)
        parts.append(jnp.concatenate([hi, (v - hi.astype(F32)).astype(BF16)], axis=1))
    r = jnp.dot(jnp.concatenate(parts, axis=0), mat, preferred_element_type=F32)
    return [(r[n * BLOCK:(n + 1) * BLOCK, :BLOCK], r[n * BLOCK:(n + 1) * BLOCK, BLOCK:]) for n in range(len(vs))]


def _strict_mask():
    r = lax.broadcasted_iota(jnp.int32, (BLOCK, BLOCK), 0)
    c = lax.broadcasted_iota(jnp.int32, (BLOCK, BLOCK), 1)
    return c < r


def _tile(ref, j):
    return ref[pl.ds(pl.multiple_of(j * BLOCK, BLOCK), BLOCK), :]


SWEEP_EXIT = -88.0


def _head_halves(t, lo):
    zero = jnp.zeros_like(t)
    return jnp.where(lo, t, zero), jnp.where(lo, zero, t)


def _sb_specs(S, HD, width):
    n = HD // width
    blk = pl.BlockSpec((None, BLOCK, width), lambda b, p, i: (b, i, p))
    k_full = pl.BlockSpec((None, S, width), lambda b, p, i: (b, 0, n + p))
    v_full = pl.BlockSpec((None, S, width), lambda b, p, i: (b, 0, 2 * n + p))
    mat = pl.BlockSpec((2 * BLOCK, 2 * BLOCK), lambda b, p, i: (0, 0))
    return blk, k_full, v_full, mat


SB_FWD_PAIRS = 4
SB_BWD_PAIRS = 2
SB_BWD_TILES = 2
SB_BWD_VMEM_LIMIT_BYTES = 58 * 1024 * 1024


def _attn_b_fwd(qkv, rev):
    NB, S, W = qkv.shape
    HD = W // 3
    width = SB_FWD_PAIRS * LANES
    n_heads = 2 * SB_FWD_PAIRS
    blk, k_full, v_full, mat = _sb_specs(S, HD, width)

    def body(q_ref, k_ref, v_ref, rev_ref, o_ref):
        i = pl.program_id(2)
        rv = rev_ref[...]
        mask = _strict_mask()
        lo = lax.broadcasted_iota(jnp.int32, (BLOCK, LANES), 1) < HEAD_DIM
        q_all = q_ref[...]
        q_stack = [jnp.concatenate(_head_halves(q_all[:, p * LANES:(p + 1) * LANES] * ATTN_SCALE, lo), axis=0)
                   for p in range(SB_FWD_PAIRS)]

        def pair_tiles(ref, j):
            t = _tile(ref, j)
            return [t[:, p * LANES:(p + 1) * LANES] for p in range(SB_FWD_PAIRS)]

        def tile_pass(j, carries, diagonal):
            ks, vs = pair_tiles(k_ref, j), pair_tiles(v_ref, j)
            zs = []
            for p in range(SB_FWD_PAIRS):
                z2 = lax.dot_general(q_stack[p], ks[p], _NT, preferred_element_type=F32)
                zs += [z2[:BLOCK], z2[BLOCK:]]
            logs = [_log_sigmoids(z) for z in zs]
            cums = _cumsum_mxu_many([jnp.where(mask, lm, 0.0) if diagonal else lm for _, lm in logs], rv)
            probs, new_c = [], []
            for h in range(n_heads):
                after, rs = cums[h]
                if diagonal:
                    a = jnp.where(mask, jnp.exp(logs[h][0] + after), 0.0)
                    new_c.append(rs)
                else:
                    a = jnp.exp(logs[h][0] + after + carries[h])
                    new_c.append(carries[h] + rs)
                probs.append(a.astype(BF16))
            outs = []
            for p in range(SB_FWD_PAIRS):
                pv = jnp.dot(jnp.concatenate(probs[2 * p:2 * p + 2], axis=0), vs[p], preferred_element_type=F32)
                outs.append(jnp.where(lo, pv[:BLOCK], pv[BLOCK:]))
            return new_c, outs

        carries, accs = tile_pass(i, None, True)

        def live(cs):
            top = cs[0]
            for c in cs[1:]:
                top = jnp.maximum(top, c)
            return jnp.max(top) > SWEEP_EXIT

        def cond(st):
            return (st[0] < i) & st[1]

        def step(st):
            jj, _, cs, accs = st
            new_c, outs = tile_pass(i - 1 - jj, cs, False)
            return jj + 1, live(new_c), new_c, [acc + o for acc, o in zip(accs, outs)]

        st = lax.while_loop(cond, step, (jnp.int32(0), live(carries), carries, accs))
        for p in range(SB_FWD_PAIRS):
            o_ref[:, p * LANES:(p + 1) * LANES] = st[3][p].astype(BF16)

    return pl.pallas_call(
        body, name="attn_b_fwd", grid=(NB, HD // width, S // BLOCK),
        in_specs=[blk, k_full, v_full, mat], out_specs=blk,
        out_shape=jax.ShapeDtypeStruct((NB, S, HD), BF16),
        compiler_params=_params(("parallel", "parallel", "arbitrary")),
    )(qkv, qkv, qkv, rev)


def _attn_b_bwd(qkv, do, rev, fwd):
    NB, S, W = qkv.shape
    HD = W // 3
    width = SB_BWD_PAIRS * LANES
    n_heads = 2 * SB_BWD_PAIRS
    nj = S // BLOCK
    blk, k_full, v_full, mat = _sb_specs(S, HD, width)
    acc_full = pl.BlockSpec((None, S, width), lambda b, p, i: (b, 0, p))

    def body(q_ref, do_ref, k_ref, v_ref, rev_ref, fwd_ref, dq_ref, dk_ref, dv_ref, sig_s, a_s, e_s):
        i = pl.program_id(2)

        @pl.when(i == 0)
        def _():
            dk_ref[...] = jnp.zeros_like(dk_ref)
            dv_ref[...] = jnp.zeros_like(dv_ref)

        rv, fw = rev_ref[...], fwd_ref[...]
        mask = _strict_mask()
        lo = lax.broadcasted_iota(jnp.int32, (BLOCK, LANES), 1) < HEAD_DIM
        pairs = range(SB_BWD_PAIRS)

        def cols(p):
            return slice(p * LANES, (p + 1) * LANES)

        q_stack = [jnp.concatenate(_head_halves(q_ref[:, cols(p)], lo), axis=0) for p in pairs]
        qs_stack = [q * ATTN_SCALE for q in q_stack]
        do_stack = [jnp.concatenate(_head_halves(do_ref[:, cols(p)], lo), axis=0) for p in pairs]

        def sweep1_tiles(js, carries, diagonal):
            zs, das = [], []
            for j in js:
                kj, vj = _tile(k_ref, j), _tile(v_ref, j)
                for p in pairs:
                    z2 = lax.dot_general(qs_stack[p], kj[:, cols(p)], _NT, preferred_element_type=F32)
                    da2 = lax.dot_general(do_stack[p], vj[:, cols(p)], _NT, preferred_element_type=F32)
                    zs += [z2[:BLOCK], z2[BLOCK:]]
                    das += [da2[:BLOCK], da2[BLOCK:]]
            logs = [_log_sigmoids(z) for z in zs]
            cums = _cumsum_mxu_many([jnp.where(mask, lm, 0.0) if diagonal else lm for _, lm in logs], rv)
            new_c, stores = [], []
            for h in range(n_heads):
                carry = None if diagonal else carries[h]
                for t, j in enumerate(js):
                    n = t * n_heads + h
                    lb, (after, rs) = logs[n][0], cums[n]
                    if diagonal:
                        a = jnp.where(mask, jnp.exp(lb + after), 0.0)
                        carry = rs
                    else:
                        a = jnp.exp(lb + after + carry)
                        carry = carry + rs
                    stores.append((t, h, j, jnp.exp(lb), a.astype(BF16), das[n] * a))
                new_c.append(carry)
            for t, h, j, sg, ab, e in sorted(stores, key=lambda s: -s[0]):
                sig_s[h, j] = sg
                a_s[h, j] = ab
                e_s[h, j] = e
            return new_c

        carries = sweep1_tiles([i], None, True)

        def live(cs):
            top = cs[0]
            for c in cs[1:]:
                top = jnp.maximum(top, c)
            return jnp.max(top) > SWEEP_EXIT

        def cond(st):
            return (SB_BWD_TILES * st[0] < i) & st[1]

        def sweep1(st):
            first = i - 1 - SB_BWD_TILES * st[0]
            new_c = sweep1_tiles([jnp.maximum(first - t, 0) for t in range(SB_BWD_TILES)], st[2], False)
            return st[0] + 1, live(new_c), new_c

        trips = lax.while_loop(cond, sweep1, (jnp.int32(0), live(carries), carries))[0]
        lowest = jnp.maximum(i - SB_BWD_TILES * trips, 0)

        def grads(js, st, diagonal):
            prefixes, dqs = st
            es = [e_s[h, j] for j in js for h in range(n_heads)]
            cums = _cumsum_mxu_many(es, fw)
            dzs, new_p = [], []
            for h in range(n_heads):
                prefix = prefixes[h]
                for t, j in enumerate(js):
                    n = t * n_heads + h
                    sg = sig_s[h, j]
                    e_before, rs = cums[n]
                    dz = (es[n] * (1.0 - sg) - (e_before + prefix) * sg) * ATTN_SCALE
                    if diagonal:
                        dz = jnp.where(mask, dz, 0.0)
                    dzs.append((t, h, dz.astype(BF16)))
                    prefix = prefix + rs
                new_p.append(prefix)
            dz_of = {(t, h): dz for t, h, dz in dzs}
            new_dq = list(dqs)
            for t, j in enumerate(js):
                kj = _tile(k_ref, j)
                rows = pl.ds(pl.multiple_of(j * BLOCK, BLOCK), BLOCK)
                for p in pairs:
                    dz_stack = jnp.concatenate([dz_of[t, 2 * p], dz_of[t, 2 * p + 1]], axis=0)
                    a_stack = jnp.concatenate([a_s[2 * p, j], a_s[2 * p + 1, j]], axis=0)
                    dq2 = jnp.dot(dz_stack, kj[:, cols(p)], preferred_element_type=F32)
                    new_dq[p] = new_dq[p] + jnp.where(lo, dq2[:BLOCK], dq2[BLOCK:])
                    dk_ref[rows, cols(p)] += lax.dot_general(dz_stack, q_stack[p], _TN, preferred_element_type=F32)
                    dv_ref[rows, cols(p)] += lax.dot_general(a_stack, do_stack[p], _TN, preferred_element_type=F32)
            return new_p, new_dq

        zeros = jnp.zeros((BLOCK, BLOCK), F32)
        st = ([zeros] * n_heads, [zeros] * SB_BWD_PAIRS)
        count = i - lowest
        st = lax.fori_loop(0, count % SB_BWD_TILES, lambda t, st: grads([lowest + t], st, False), st)
        start = lowest + count % SB_BWD_TILES
        st = lax.fori_loop(0, count // SB_BWD_TILES,
                           lambda t, st: grads([start + SB_BWD_TILES * t + u for u in range(SB_BWD_TILES)], st, False), st)
        dqs = grads([i], st, True)[1]
        for p in pairs:
            dq_ref[:, cols(p)] = dqs[p]

    f32_stash = pltpu.VMEM((n_heads, nj, BLOCK, BLOCK), F32)
    bf16_stash = pltpu.VMEM((n_heads, nj, BLOCK, BLOCK), BF16)
    return pl.pallas_call(
        body, name="attn_b_bwd", grid=(NB, HD // width, nj),
        in_specs=[blk, blk, k_full, v_full, mat, mat], out_specs=[blk, acc_full, acc_full],
        out_shape=[jax.ShapeDtypeStruct((NB, S, HD), F32)] * 3,
        scratch_shapes=[f32_stash, bf16_stash, f32_stash],
        compiler_params=_params(("parallel", "parallel", "arbitrary"), SB_BWD_VMEM_LIMIT_BYTES),
    )(qkv, do, qkv, qkv, rev, fwd)


def _ada_fwd(c_all, w, b):
    L, D, N = w.shape
    B = c_all.shape[0]

    def body(c_ref, w_ref, b_ref, o_ref):
        cv = c_ref[...]
        cond = (cv * _sigmoid(cv)).astype(BF16)
        o_ref[...] = jnp.dot(cond, w_ref[...].astype(BF16), preferred_element_type=F32) + b_ref[...]

    return pl.pallas_call(
        body, name="ada_fwd", grid=(L,),
        in_specs=[pl.BlockSpec((B, D), lambda l: (0, 0)), pl.BlockSpec((None, D, N), lambda l: (l, 0, 0)),
                  pl.BlockSpec((None, 1, N), lambda l: (l, 0, 0))],
        out_specs=pl.BlockSpec((None, B, N), lambda l: (l, 0, 0)),
        out_shape=jax.ShapeDtypeStruct((L, B, N), F32),
        compiler_params=_params(("parallel",)),
    )(c_all, w, b)


def _ada_bwd(c_all, dmod_all, dmod_shard):
    L, B, N = dmod_shard.shape
    D = c_all.shape[1]
    N_all = dmod_all.shape[2]

    def body(c_ref, da_ref, ds_ref, gw_ref, gb_ref):
        cv = c_ref[...]
        cond = (cv * _sigmoid(cv)).astype(BF16)
        gw_ref[...] = lax.dot_general(cond, ds_ref[...].astype(BF16), _TN, preferred_element_type=F32)
        gb_ref[...] = jnp.sum(da_ref[...], axis=0, keepdims=True)

    return pl.pallas_call(
        body, name="ada_bwd", grid=(L,),
        in_specs=[pl.BlockSpec((B, D), lambda l: (0, 0)), pl.BlockSpec((None, B, N_all), lambda l: (l, 0, 0)),
                  pl.BlockSpec((None, B, N), lambda l: (l, 0, 0))],
        out_specs=[pl.BlockSpec((None, D, N), lambda l: (l, 0, 0)), pl.BlockSpec((None, 1, N_all), lambda l: (l, 0, 0))],
        out_shape=[jax.ShapeDtypeStruct((L, D, N), F32), jax.ShapeDtypeStruct((L, 1, N_all), F32)],
        compiler_params=_params(("parallel",)),
    )(c_all, dmod_all, dmod_shard)


def _adamw(w, g, m, v, name):
    shape = w.shape
    if w.ndim == 2:
        w, g, m, v = [t.reshape((1,) + shape) for t in (w, g, m, v)]
    L, R, C = w.shape
    tr = _pick(R, max(8, (1 << 18) // C), 8)
    c1 = 1.0 - ADAM_B1 ** ADAM_STEP
    c2 = 1.0 - ADAM_B2 ** ADAM_STEP

    def body(w_ref, g_ref, m_ref, v_ref, d_ref, nm_ref, nv_ref):
        gv = g_ref[...]
        nm = ADAM_B1 * m_ref[...] + (1.0 - ADAM_B1) * gv
        nv = ADAM_B2 * v_ref[...] + (1.0 - ADAM_B2) * (gv * gv)
        d_ref[...] = -ADAM_LR * ((nm / c1) / (jnp.sqrt(nv / c2) + ADAM_EPS) + ADAM_WD * w_ref[...])
        nm_ref[...] = nm
        nv_ref[...] = nv

    spec = pl.BlockSpec((None, tr, C), lambda l, r: (l, r, 0))
    out = pl.pallas_call(
        body, name=name, grid=(L, R // tr), in_specs=[spec] * 4, out_specs=[spec] * 3,
        out_shape=[jax.ShapeDtypeStruct((L, R, C), F32)] * 3,
        compiler_params=_params(("parallel", "parallel")),
    )(w, g, m, v)
    return [t.reshape(shape) for t in out]


_SHARDED = (("wqkv_a", 2), ("wo_a", 1), ("wqkv_b", 2), ("wo_b", 1), ("w_gate", 2), ("w_up", 2), ("w_down", 1))


def _pack_full(layers, axis, gate_up=None):
    L = len(layers)
    R, C = layers[0].shape

    def shards(m):
        if gate_up is not None:
            F = C // 2
            tf, Cs = _ff_tile(F), F // 4
            assert tf % Cs == 0
            starts = [(2 * (s * Cs // tf) + gate_up) * tf + s * Cs % tf for s in range(4)]
            return jnp.stack([m[:, st:st + Cs] for st in starts])
        if axis == 2:
            return m.reshape(R, 4, C // 4).transpose(1, 0, 2)
        return m.reshape(4, R // 4, C)

    halves = [jnp.stack([shards(m) for m in layers[h * (L // 2):(h + 1) * (L // 2)]], axis=1) for h in range(2)]
    return jnp.stack(halves)


def _unpack_full(gathered, axis):
    _, Lh, Rs, Cs = gathered.shape
    t = gathered.reshape(4, 2, Lh, Rs, Cs)
    layers = []
    for h in range(2):
        for l in range(Lh):
            piece = t[:, h, l]
            if axis == 2:
                layers.append(piece.transpose(1, 0, 2).reshape(Rs, 4 * Cs))
            else:
                layers.append(piece.reshape(4 * Rs, Cs))
    return layers


def _sum_slabs(own, recv, name, with_bf16=False):
    C = own.shape[-1]
    out = _sum_leading(recv.reshape(recv.shape[0], -1, C), name, own=own.reshape(-1, C), with_bf16=with_bf16)
    if with_bf16:
        return out[0].reshape(own.shape), out[1].reshape(own.shape)
    return out.reshape(own.shape)


def _gather8(x, name):
    return _all_gather8([x], name)[0]


def _rope_tables(positions):
    half = ROT_DIM // 2
    inv_freq = jnp.power(jnp.float32(ROPE_THETA), -jnp.arange(half, dtype=F32) * 2.0 / ROT_DIM)
    ang = positions.astype(F32).reshape(-1, 1) * inv_freq
    cos, sin = jnp.cos(ang), jnp.sin(ang)
    T = ang.shape[0]
    rest = HEAD_DIM - ROT_DIM
    c64 = jnp.concatenate([cos, cos, jnp.ones((T, rest), F32)], axis=1)
    s64 = jnp.concatenate([-sin, sin, jnp.zeros((T, rest), F32)], axis=1)
    return jnp.tile(c64, (1, 2)), jnp.tile(s64, (1, 2))


def _gain_rows(q_gain, k_gain):
    q2 = jnp.tile(q_gain.reshape(1, HEAD_DIM), (GROUP_A, 2))
    k2 = jnp.tile(k_gain.reshape(1, HEAD_DIM), (1, 2))
    return jnp.concatenate([q2, k2, jnp.ones((1, LANES), F32)], axis=0)


def _local_step(x, positions, mod, norm1_g, norm2_g, q_norm_a, k_norm_a, sinks_a,
                wqkv_a, wo_a, wqkv_b, wo_b, wgu, wd, loss_target):
    NB, S, D = x.shape
    T = NB * S
    QA = N_Q_A * HEAD_DIM
    tab_c, tab_s = _rope_tables(positions)
    rev, fwd = _cumsum_mats()

    saved = []
    xc = x
    mods = [[mod[i][:, k * D:(k + 1) * D].reshape(NB, 1, D) for k in range(6)] for i in range(DEPTH)]
    h = _norm_mod_fwd(xc, norm1_g[0:1], mods[0][1], mods[0][0])
    for i in range(DEPTH):
        j = i // 2
        sh1, sc1, g1, sh2, sc2, g2 = mods[i]
        st = dict(x=xc, sc1=sc1, g1=g1, sc2=sc2, g2=g2)
        st["h"] = h.reshape(T, D)
        if i % 2 == 0:
            st["qkv"] = _matmul(st["h"], wqkv_a[j], "nn", F32, "qkv_a")
            st["gains"] = _gain_rows(q_norm_a[j], k_norm_a[j])
            st["qkn"] = _qk_prep_fwd(st["qkv"], tab_c, tab_s, st["gains"]).reshape(NB, S, -1)
            st["o"] = _attn_a_fwd(st["qkn"], sinks_a[j]).reshape(T, QA)
            y = _matmul(st["o"], wo_a[j], "nn", F32, "wo_a")
        else:
            st["qkv"] = _matmul(st["h"], wqkv_b[j], "nn", BF16, "qkv_b").reshape(NB, S, -1)
            st["o"] = _attn_b_fwd(st["qkv"], rev).reshape(T, N_H_B * HEAD_DIM)
            y = _matmul(st["o"], wo_b[j], "nn", F32, "wo_b")
        st["y"] = y.reshape(NB, S, D)
        x1, h2 = _gate_res(xc, st["y"], g1, norm=(norm2_g[i:i + 1], sc2, sh2))
        st["x1"] = x1
        st["h2"] = h2.reshape(T, D)
        st["gu"], st["act"] = _matmul(st["h2"], wgu[i], "nn", BF16, "gate_up", swiglu=True)
        st["m"] = _matmul(st["act"], wd[i], "nn", F32, "down").reshape(NB, S, D)
        if i + 1 < DEPTH:
            xc, h = _gate_res(x1, st["m"], g2, norm=(norm1_g[i + 1:i + 2], mods[i + 1][1], mods[i + 1][0]))
        else:
            xc = _gate_res(x1, st["m"], g2)
        saved.append(st)

    loss, dx = _loss_fwd_bwd(xc, loss_target)

    grads = {name: [None] * n for name, n in
             (("wqkv_a", 2), ("wo_a", 2), ("wqkv_b", 2), ("wo_b", 2), ("wgu", DEPTH), ("wd", DEPTH),
              ("norm1_g", DEPTH), ("norm2_g", DEPTH), ("q_norm_a", 2), ("k_norm_a", 2), ("sinks_a", 2))}
    dmod = [None] * DEPTH
    dm, dg2 = _gate_res_bwd(dx, saved[-1]["m"], saved[-1]["g2"])
    for i in reversed(range(DEPTH)):
        j = i // 2
        st = saved[i]
        dm = dm.reshape(T, D)
        grads["wd"][i] = _matmul(st["act"], dm, "tn", F32, "d_wd")
        dgu = _swiglu_bwd(dm, wd[i], st["gu"])
        grads["wgu"][i] = _matmul(st["h2"], dgu, "tn", F32, "d_wgu")
        dx1, dsh2, dsc2, grads["norm2_g"][i], dy, dg1 = _norm_mod_bwd(
            dgu, wgu[i], st["x1"], norm2_g[i:i + 1], st["sc2"], dx, "d_h2", y=st["y"], g=st["g1"])
        dy = dy.reshape(T, D)
        if i % 2 == 0:
            do = _matmul(dy, wo_a[j], "nt", BF16, "d_o_a").reshape(NB, S, QA)
            grads["wo_a"][j] = _matmul(st["o"], dy, "tn", F32, "d_wo_a")
            dq, dk, dv, dsink = _attn_a_bwd(st["qkn"], do, sinks_a[j])
            dqkv, dgain = _qk_prep_bwd(st["qkv"], dq.reshape(T, QA), dk.reshape(T, LANES), dv.reshape(T, LANES),
                                       tab_c, tab_s, st["gains"])
            w_in = wqkv_a[j]
            grads["wqkv_a"][j] = _matmul(st["h"], dqkv, "tn", F32, "d_wqkv_a")
            grads["q_norm_a"][j] = jnp.sum(dgain[:GROUP_A].reshape(2 * GROUP_A, HEAD_DIM), axis=0)
            grads["k_norm_a"][j] = jnp.sum(dgain[GROUP_A].reshape(2, HEAD_DIM), axis=0)
            grads["sinks_a"][j] = jnp.sum(dsink[..., 0], axis=0)
        else:
            do = _matmul(dy, wo_b[j], "nt", BF16, "d_o_b").reshape(NB, S, -1)
            grads["wo_b"][j] = _matmul(st["o"], dy, "tn", F32, "d_wo_b")
            dq, dk, dv = _attn_b_bwd(st["qkv"], do, rev, fwd)
            dqkv = jnp.concatenate([dq, dk, dv], axis=-1).reshape(T, -1).astype(BF16)
            w_in = wqkv_b[j]
            grads["wqkv_b"][j] = _matmul(st["h"], dqkv, "tn", F32, "d_wqkv_b")
        this_dg2 = dg2
        if i > 0:
            dx, dsh1, dsc1, grads["norm1_g"][i], dm, dg2 = _norm_mod_bwd(
                dqkv, w_in, st["x"], norm1_g[i:i + 1], st["sc1"], dx1, "d_h", y=saved[i - 1]["m"], g=saved[i - 1]["g2"])
        else:
            dx, dsh1, dsc1, grads["norm1_g"][i] = _norm_mod_bwd(
                dqkv, w_in, st["x"], norm1_g[i:i + 1], st["sc1"], dx1, "d_h")
        dmod[i] = jnp.concatenate([dsh1, dsc1, dg1, dsh2, dsc2, this_dg2], axis=-1).reshape(NB, 6 * D)

    matrices = ("wqkv_a", "wo_a", "wqkv_b", "wo_b", "wgu", "wd")
    grads = {name: parts if name in matrices else jnp.stack(parts) for name, parts in grads.items()}
    return loss, dx, grads, jnp.stack(dmod)


def _rows_of(flat, cols=PACK_COLS):
    n = flat.shape[0]
    pad = (-n) % (8 * cols)
    if pad:
        flat = jnp.concatenate([flat, jnp.zeros((pad,), flat.dtype)])
    return flat.reshape(-1, cols)


def kernel(x, c, positions, ada_w, ada_b, norm1_g, norm2_g, wqkv_a, q_norm_a, k_norm_a, sinks_a, wo_a, wqkv_b, wo_b, w_gate, w_up, w_down, loss_target, m_ada_w, m_ada_b, m_norm1_g, m_norm2_g, m_wqkv_a, m_q_norm_a, m_k_norm_a, m_sinks_a, m_wo_a, m_wqkv_b, m_wo_b, m_w_gate, m_w_up, m_w_down, v_ada_w, v_ada_b, v_norm1_g, v_norm2_g, v_wqkv_a, v_q_norm_a, v_k_norm_a, v_sinks_a, v_wo_a, v_wqkv_b, v_wo_b, v_w_gate, v_w_up, v_w_down):
    xi, yi, ci = lax.axis_index("x"), lax.axis_index("y"), lax.axis_index("c")
    dev = 4 * xi + 2 * yi + ci
    chip = 2 * xi + yi
    NB, S, D = x.shape
    B_all = N_DEV * NB
    L = ada_w.shape[0]
    n_mod = ada_w.shape[2] // 2

    c_all = _gather8(_rows_of(c.reshape(-1), LANES), "gather_c").reshape(N_DEV, -1)[:, :NB * D].reshape(B_all, D)
    ada_w_half = lax.dynamic_slice_in_dim(ada_w, ci * n_mod, n_mod, axis=2)
    ada_b_half = lax.dynamic_slice_in_dim(ada_b, dev * n_mod, n_mod, axis=1).reshape(L, 1, n_mod)
    mod_part = _ada_fwd(c_all, ada_w_half, ada_b_half)
    n_part = L * B_all * n_mod
    mod_all = _gather8(_rows_of(mod_part.reshape(-1)), "gather_mod").reshape(N_DEV, -1)[:, :n_part]
    mod_all = mod_all.reshape(N_DEV, L, B_all, n_mod).transpose(1, 2, 0, 3).reshape(L, B_all, N_DEV * n_mod)
    mod = lax.dynamic_slice_in_dim(mod_all, dev * NB, NB, axis=1)

    shards = dict(wqkv_a=wqkv_a, wo_a=wo_a, wqkv_b=wqkv_b, wo_b=wo_b, w_gate=w_gate, w_up=w_up, w_down=w_down)
    halves = []
    for name, _ in _SHARDED:
        w = shards[name]
        half = lax.dynamic_index_in_dim(w.reshape((2, w.shape[0] // 2) + w.shape[1:]), ci, 0, keepdims=False)
        halves.append(half.astype(BF16))
    gathered = _all_gather8(halves, "gather_weights", local_axis=1, local_chunks=8, relay_axis=1)
    full = {name: _unpack_full(t, axis) for (name, axis), t in zip(_SHARDED, gathered)}
    wgu = [_interleave(gate, up) for gate, up in zip(full["w_gate"], full["w_up"])]

    loss, grad_x, g, dmod = _local_step(
        x, positions, mod, norm1_g, norm2_g, q_norm_a, k_norm_a, sinks_a,
        full["wqkv_a"], full["wo_a"], full["wqkv_b"], full["wo_b"], wgu, full["w_down"], loss_target)

    g_full = dict(wqkv_a=g["wqkv_a"], wo_a=g["wo_a"], wqkv_b=g["wqkv_b"], wo_b=g["wo_b"],
                  w_gate=g["wgu"], w_up=g["wgu"], w_down=g["wd"])
    which = dict(w_gate=0, w_up=1)
    packed = [_pack_full(g_full[name], axis, which.get(name)) for name, axis in _SHARDED]
    def own(t, index):
        return lax.dynamic_index_in_dim(t, index, 0, keepdims=False)

    from_cores = _exchange_cores(packed, "rs_cores", chunk_axis=0, chunks=4)
    chip_part = [_sum_slabs(own(p, ci), r, "rs_add_cores", with_bf16=True) for p, r in zip(packed, from_cores)]
    from_chips = _exchange_chips([b for _, b in chip_part], "rs_chips", relay_axis=1)
    mine = [_sum_slabs(own(p, chip), r, "rs_add_chips") for (p, _), r in zip(chip_part, from_chips)]
    theirs = _sibling_send(mine, "rs_halves")
    grad = {}
    for (name, _), m, t in zip(_SHARDED, mine, theirs):
        first, second = jnp.where(ci == 0, m, t), jnp.where(ci == 0, t, m)
        grad[name] = jnp.stack([first, second]).reshape(shards[name].shape)

    small_names = ("norm1_g", "norm2_g", "q_norm_a", "k_norm_a", "sinks_a")
    small = [dmod.reshape(-1)] + [g[name].reshape(-1) for name in small_names] + [loss.reshape(-1)]
    small_sizes = [t.shape[0] for t in small]
    small_rows = _rows_of(jnp.concatenate(small))
    small_all = _gather8(small_rows, "gather_small")
    small_sum = _sum_leading(small_all, "sum_small").reshape(-1)
    n_dmod = small_sizes[0]
    dmod_all = small_all.reshape(N_DEV, -1)[:, :n_dmod].reshape(N_DEV, L, NB, 6 * D)
    dmod_all = dmod_all.transpose(1, 0, 2, 3).reshape(L, B_all, 6 * D)
    off = n_dmod
    for name, sz in zip(small_names + ("loss",), small_sizes[1:]):
        grad[name] = small_sum[off:off + sz]
        off += sz
    loss_total = grad.pop("loss").reshape(())
    for name, ref in (("norm1_g", norm1_g), ("norm2_g", norm2_g), ("q_norm_a", q_norm_a),
                      ("k_norm_a", k_norm_a), ("sinks_a", sinks_a)):
        grad[name] = grad[name].reshape(ref.shape)

    n_shard = ada_w.shape[2]
    dmod_shard = lax.dynamic_slice_in_dim(dmod_all, chip * n_shard, n_shard, axis=2)
    grad["ada_w"], gb = _ada_bwd(c_all, dmod_all, dmod_shard)
    grad["ada_b"] = gb.reshape(ada_b.shape)

    weights = dict(ada_w=ada_w, ada_b=ada_b, norm1_g=norm1_g, norm2_g=norm2_g, wqkv_a=wqkv_a, q_norm_a=q_norm_a,
                   k_norm_a=k_norm_a, sinks_a=sinks_a, wo_a=wo_a, wqkv_b=wqkv_b, wo_b=wo_b, w_gate=w_gate,
                   w_up=w_up, w_down=w_down)
    m_in = dict(ada_w=m_ada_w, ada_b=m_ada_b, norm1_g=m_norm1_g, norm2_g=m_norm2_g, wqkv_a=m_wqkv_a,
                q_norm_a=m_q_norm_a, k_norm_a=m_k_norm_a, sinks_a=m_sinks_a, wo_a=m_wo_a, wqkv_b=m_wqkv_b,
                wo_b=m_wo_b, w_gate=m_w_gate, w_up=m_w_up, w_down=m_w_down)
    v_in = dict(ada_w=v_ada_w, ada_b=v_ada_b, norm1_g=v_norm1_g, norm2_g=v_norm2_g, wqkv_a=v_wqkv_a,
                q_norm_a=v_q_norm_a, k_norm_a=v_k_norm_a, sinks_a=v_sinks_a, wo_a=v_wo_a, wqkv_b=v_wqkv_b,
                wo_b=v_wo_b, w_gate=v_w_gate, w_up=v_w_up, w_down=v_w_down)
    names = list(weights)
    delta, new_m, new_v = {}, {}, {}
    for name in names:
        delta[name], new_m[name], new_v[name] = _adamw(weights[name], grad[name], m_in[name], v_in[name],
                                                       "adamw_" + name)
    return (loss_total, grad_x, *[grad[k] for k in names], *[delta[k] for k in names],
            *[new_m[k] for k in names], *[new_v[k] for k in names])
```

```python
import jax
import jax.numpy as jnp
from jax import lax
from jax.experimental import pallas as pl
from jax.experimental.pallas import tpu as pltpu

F32 = jnp.float32
BF16 = jnp.bfloat16

DEPTH = 4
HEAD_DIM = 64
N_Q_A = 16
N_KV_A = 2
GROUP_A = N_Q_A // N_KV_A
N_H_B = 16
BLOCK = 128
ROT_DIM = HEAD_DIM // 4
ROPE_THETA = 500000.0
EPS = 1e-6
ATTN_SCALE = HEAD_DIM ** -0.5
NEG_BIG = -1e30

ADAM_LR = 0.001
ADAM_B1 = 0.9
ADAM_B2 = 0.999
ADAM_EPS = 1e-08
ADAM_WD = 0.01
ADAM_STEP = 10

N_DEV = 8
LANES = 128
PACK_COLS = 1024
VMEM_LIMIT_BYTES = 48 * 1024 * 1024
MESH = pl.DeviceIdType.MESH

_NT = (((1,), (1,)), ((), ()))
_TN = (((0,), (0,)), ((), ()))
_NN = (((1,), (0,)), ((), ()))


def _params(sem=None, vmem_limit_bytes=VMEM_LIMIT_BYTES):
    return pltpu.CompilerParams(vmem_limit_bytes=vmem_limit_bytes, dimension_semantics=sem)


def _pick(n, cap, mult):
    best = None
    for t in range(mult, min(n, cap) + 1, mult):
        if n % t == 0:
            best = t
    return n if best is None else best


_ANY = pl.BlockSpec(memory_space=pl.ANY)


def _window(index, axis, q, n, shape):
    rest = [slice(None)] * len(shape)
    size = shape[axis] // n
    rest[axis] = pl.ds(q * size, size)
    return tuple(index) + tuple(rest)


def _all_gather8(xs, name, local_axis=0, local_chunks=1, relay_axis=None):
    n = len(xs)
    n_sems = 7 if relay_axis is None else 9

    def body(*refs):
        x_refs, out_refs = refs[:n], refs[n:2 * n]
        send_sems, recv_sems, local_sems = refs[2 * n:]
        xi, yi, ci = lax.axis_index("x"), lax.axis_index("y"), lax.axis_index("c")
        me, sibling = (xi, yi, ci), (xi, yi, 1 - ci)
        chips = [(1 - xi, yi), (xi, 1 - yi), (1 - xi, 1 - yi)]

        def slab(w, px, py, pc):
            return out_refs[w].at[4 * px + 2 * py + pc]

        def copy(w, k, block, to, src=None):
            return pltpu.make_async_remote_copy(
                src_ref=slab(w, *block) if src is None else src, dst_ref=slab(w, *block),
                send_sem=send_sems.at[k, w], recv_sem=recv_sems.at[k, w], device_id=to, device_id_type=MESH)

        mine = []
        for w in range(n):
            for q in range(local_chunks):
                part = _window((), local_axis, q, local_chunks, xs[w].shape)
                mine.append(pltpu.make_async_copy(x_refs[w].at[part], slab(w, *me).at[part], local_sems.at[w, q]))
                mine[-1].start()
        direct = chips if relay_axis is None else chips[:2]
        first = [copy(w, 0, me, sibling, src=x_refs[w]) for w in range(n)]
        first += [copy(w, 1 + j, me, (*chip, ci), src=x_refs[w]) for j, chip in enumerate(direct) for w in range(n)]
        for cp in first:
            cp.start()

        def relay(w, part, block, to):
            piece = _window((), relay_axis, part, 2, xs[w].shape)
            return pltpu.make_async_remote_copy(
                src_ref=slab(w, *block).at[piece], dst_ref=slab(w, *block).at[piece],
                send_sem=send_sems.at[7 + part, w], recv_sem=recv_sems.at[7 + part, w],
                device_id=to, device_id_type=MESH)

        passed = []
        for j, chip in enumerate(direct):
            for w in range(n):
                copy(w, 1 + j, (*chip, ci), me).wait_recv()
                passed.append(copy(w, 4 + j, (*chip, ci), sibling))
                passed[-1].start()
                if relay_axis is not None:
                    passed.append(relay(w, j, (*chip, ci), (*chips[1 - j], ci)))
                    passed[-1].start()
        if relay_axis is not None:
            for w in range(n):
                for part in range(2):
                    relay(w, part, (*chips[2], ci), me).wait_recv()
                passed.append(copy(w, 6, (*chips[2], ci), sibling))
                passed[-1].start()
        for w in range(n):
            copy(w, 0, sibling, me).wait_recv()
        for j, chip in enumerate(chips):
            for w in range(n):
                copy(w, 4 + j, (*chip, 1 - ci), me).wait_recv()
        for cp in first + passed:
            cp.wait_send()
        for cp in mine:
            cp.wait()

    return pl.pallas_call(
        body, name=name,
        out_shape=[jax.ShapeDtypeStruct((N_DEV,) + x.shape, x.dtype) for x in xs],
        in_specs=[_ANY] * n, out_specs=[_ANY] * n,
        scratch_shapes=[pltpu.SemaphoreType.DMA((n_sems, n)), pltpu.SemaphoreType.DMA((n_sems, n)),
                        pltpu.SemaphoreType.DMA((n, local_chunks))],
    )(*xs)


def _exchange_cores(xs, name, chunk_axis=0, chunks=1):
    n = len(xs)
    n_peers = 1

    def body(*refs):
        x_refs, out_refs = refs[:n], refs[n:2 * n]
        send_sems, recv_sems = refs[2 * n:]
        xi, yi, ci = lax.axis_index("x"), lax.axis_index("y"), lax.axis_index("c")
        peers = [(1 - ci, (xi, yi, 1 - ci))]
        copies = []
        for k, (p, dev) in enumerate(peers):
            for w in range(n):
                slab_shape = xs[w].shape[1:]
                for q in range(chunks):
                    copies.append(pltpu.make_async_remote_copy(
                        src_ref=x_refs[w].at[_window((p,), chunk_axis, q, chunks, slab_shape)],
                        dst_ref=out_refs[w].at[_window((k,), chunk_axis, q, chunks, slab_shape)],
                        send_sem=send_sems.at[k, w, q], recv_sem=recv_sems.at[k, w, q],
                        device_id=dev, device_id_type=MESH))
                    copies[-1].start()
        for cp in copies:
            cp.wait()

    return pl.pallas_call(
        body, name=name,
        out_shape=[jax.ShapeDtypeStruct((n_peers,) + x.shape[1:], x.dtype) for x in xs],
        in_specs=[_ANY] * n, out_specs=[_ANY] * n,
        scratch_shapes=[pltpu.SemaphoreType.DMA((n_peers, n, chunks)), pltpu.SemaphoreType.DMA((n_peers, n, chunks))],
    )(*xs)


def _exchange_chips(xs, name, relay_axis):
    n = len(xs)

    def half_shape(x):
        shape = list(x.shape[1:])
        shape[relay_axis] //= 2
        return tuple(shape)

    def body(*refs):
        x_refs, out_refs, hop_refs = refs[:n], refs[n:2 * n], refs[2 * n:3 * n]
        send_sems, recv_sems = refs[3 * n:]
        xi, yi, ci = lax.axis_index("x"), lax.axis_index("y"), lax.axis_index("c")
        nbr = [(1 - xi, yi, ci), (xi, 1 - yi, ci)]
        slab_of_nbr = [2 * (1 - xi) + yi, 2 * xi + (1 - yi)]
        slab_of_diag = 2 * (1 - xi) + (1 - yi)

        def copy(k, w, src, dst, to):
            return pltpu.make_async_remote_copy(src_ref=src, dst_ref=dst, send_sem=send_sems.at[k, w],
                                                recv_sem=recv_sems.at[k, w], device_id=to, device_id_type=MESH)

        def piece(w, part):
            return _window((), relay_axis, part, 2, xs[w].shape[1:])

        sent = []
        for w in range(n):
            for j in range(2):
                sent.append(copy(j, w, x_refs[w].at[slab_of_nbr[j]], out_refs[w].at[j], nbr[j]))
                sent.append(copy(2 + j, w, x_refs[w].at[(slab_of_diag,) + piece(w, j)], hop_refs[w].at[j], nbr[j]))
        for cp in sent:
            cp.start()
        for w in range(n):
            for j in range(2):
                copy(2 + j, w, hop_refs[w].at[j], hop_refs[w].at[j], nbr[j]).wait_recv()
                sent.append(copy(4 + j, w, hop_refs[w].at[j], out_refs[w].at[(2,) + piece(w, j)], nbr[1 - j]))
                sent[-1].start()
        for w in range(n):
            for j in range(2):
                copy(j, w, out_refs[w].at[j], out_refs[w].at[j], nbr[j]).wait_recv()
                half = out_refs[w].at[(2,) + piece(w, j)]
                copy(4 + j, w, half, half, nbr[1 - j]).wait_recv()
        for cp in sent:
            cp.wait_send()

    out = pl.pallas_call(
        body, name=name,
        out_shape=[jax.ShapeDtypeStruct((3,) + x.shape[1:], x.dtype) for x in xs]
        + [jax.ShapeDtypeStruct((2,) + half_shape(x), x.dtype) for x in xs],
        in_specs=[_ANY] * n, out_specs=[_ANY] * (2 * n),
        scratch_shapes=[pltpu.SemaphoreType.DMA((6, n)), pltpu.SemaphoreType.DMA((6, n))],
    )(*xs)
    return out[:n]


def _sibling_send(xs, name, chunk_axis=1, chunks=4):
    n = len(xs)

    def body(*refs):
        x_refs, out_refs = refs[:n], refs[n:2 * n]
        send_sems, recv_sems = refs[2 * n:]
        xi, yi, ci = lax.axis_index("x"), lax.axis_index("y"), lax.axis_index("c")
        copies = []
        for w in range(n):
            for q in range(chunks):
                part = _window((), chunk_axis, q, chunks, xs[w].shape)
                copies.append(pltpu.make_async_remote_copy(
                    src_ref=x_refs[w].at[part], dst_ref=out_refs[w].at[part],
                    send_sem=send_sems.at[w, q], recv_sem=recv_sems.at[w, q],
                    device_id=(xi, yi, 1 - ci), device_id_type=MESH))
                copies[-1].start()
        for cp in copies:
            cp.wait()

    return pl.pallas_call(
        body, name=name,
        out_shape=[jax.ShapeDtypeStruct(x.shape, x.dtype) for x in xs],
        in_specs=[_ANY] * n, out_specs=[_ANY] * n,
        scratch_shapes=[pltpu.SemaphoreType.DMA((n, chunks)), pltpu.SemaphoreType.DMA((n, chunks))],
    )(*xs)


def _sum_leading(x, name, own=None, own_index=None, with_bf16=False):
    P, R, C = x.shape
    tr = _pick(R, max(16, (1 << 19) // (C * (P + 1))), 16)

    def body(idx_ref, *refs):
        n_in = 1 if own is None else 2
        x_ref = refs[n_in - 1]
        acc = x_ref[0].astype(F32) if own is None else refs[0][...] + x_ref[0].astype(F32)
        for p in range(1, P):
            acc = acc + x_ref[p].astype(F32)
        refs[n_in][...] = acc
        if with_bf16:
            refs[n_in + 1][...] = acc.astype(BF16)

    flat = pl.BlockSpec((tr, C), lambda r, idx: (r, 0))
    slabs = pl.BlockSpec((P, tr, C), lambda r, idx: (0, r, 0))
    own_slab = pl.BlockSpec((None, tr, C), lambda r, idx: (idx[0], r, 0))
    index = jnp.zeros((1,), jnp.int32) if own_index is None else jnp.reshape(own_index, (1,)).astype(jnp.int32)
    out = pl.pallas_call(
        body, name=name,
        grid_spec=pltpu.PrefetchScalarGridSpec(
            num_scalar_prefetch=1, grid=(R // tr,),
            in_specs=[slabs] if own is None else [own_slab, slabs],
            out_specs=[flat, flat] if with_bf16 else [flat]),
        out_shape=[jax.ShapeDtypeStruct((R, C), F32)] + ([jax.ShapeDtypeStruct((R, C), BF16)] if with_bf16 else []),
        compiler_params=_params(("arbitrary",)),
    )(index, *([x] if own is None else [own, x]))
    return out if with_bf16 else out[0]


MATMUL_SINGLE_K = 1280
MATMUL_VMEM_BUDGET = 36 * 1024 * 1024


def _matmul(a, b, mode, out_dtype, name, swiglu=False):
    if mode == "nn":
        (M, K), N = a.shape, b.shape[1]
    elif mode == "nt":
        (M, K), N = a.shape, b.shape[0]
    else:
        (K, M), N = a.shape, b.shape[1]
    tm = _pick(M, 1024 if mode != "tn" else 1536, 128)
    tn = _pick(N, 1536, 128)
    if swiglu:
        tm, tn = _pick(M, 1024 if out_dtype == BF16 else 512, 128), 2 * _ff_tile(N // 2)
    out_bytes = jnp.dtype(out_dtype).itemsize
    tk = K
    if K > MATMUL_SINGLE_K:
        for cap in (2048, 1024, 512):
            tk = _pick(K, cap, 128)
            blocks = 2 * 2 * tk * (tm + tn) + tm * tn * (2 * out_bytes + (4 if out_dtype != F32 else 0))
            if blocks <= MATMUL_VMEM_BUDGET:
                break
    nk = K // tk
    dims = {"nn": _NN, "nt": _NT, "tn": _TN}[mode]
    use_scratch = nk > 1 and out_dtype != F32

    def body(a_ref, b_ref, *refs):
        o_ref = refs[0]

        def product():
            return lax.dot_general(a_ref[...].astype(BF16), b_ref[...].astype(BF16), dims,
                                   preferred_element_type=F32)

        if nk == 1:
            part = product()
            o_ref[...] = part.astype(o_ref.dtype)
            if swiglu:
                g = part[:, :tn // 2]
                refs[1][...] = (g * _sigmoid(g) * part[:, tn // 2:]).astype(BF16)
            return
        k = pl.program_id(2)
        acc_ref = refs[-1] if use_scratch else o_ref

        @pl.when(k == 0)
        def _():
            acc_ref[...] = jnp.zeros_like(acc_ref)

        acc_ref[...] += product()

        if use_scratch:
            @pl.when(k == nk - 1)
            def _():
                o_ref[...] = acc_ref[...].astype(o_ref.dtype)

    if mode == "tn":
        a_spec = pl.BlockSpec((tk, tm), lambda i, j, k: (k, i))
    else:
        a_spec = pl.BlockSpec((tm, tk), lambda i, j, k: (i, k))
    if mode == "nt":
        b_spec = pl.BlockSpec((tn, tk), lambda i, j, k: (j, k))
    else:
        b_spec = pl.BlockSpec((tk, tn), lambda i, j, k: (k, j))
    out_specs = [pl.BlockSpec((tm, tn), lambda i, j, k: (i, j))]
    out_shape = [jax.ShapeDtypeStruct((M, N), out_dtype)]
    if swiglu:
        assert nk == 1 and mode == "nn"
        out_specs.append(pl.BlockSpec((tm, tn // 2), lambda i, j, k: (i, j)))
        out_shape.append(jax.ShapeDtypeStruct((M, N // 2), BF16))
    out = pl.pallas_call(
        body, name=name, grid=(M // tm, N // tn, nk),
        in_specs=[a_spec, b_spec], out_specs=out_specs, out_shape=out_shape,
        scratch_shapes=[pltpu.VMEM((tm, tn), F32)] if use_scratch else [],
        compiler_params=_params(("parallel", "parallel", "arbitrary")),
    )(a, b)
    return out if swiglu else out[0]


def _row_tile(S):
    return _pick(S, 512, 8)


def _norm_mod_fwd(x, gain, sc, sh):
    NB, S, D = x.shape
    tr = _row_tile(S)

    def body(x_ref, g_ref, sc_ref, sh_ref, h_ref):
        xv = x_ref[...]
        ms = jnp.mean(xv * xv, axis=-1, keepdims=True)
        n = xv * lax.rsqrt(ms + EPS) * g_ref[...]
        h_ref[...] = (n * (1.0 + sc_ref[...]) + sh_ref[...]).astype(BF16)

    tok = pl.BlockSpec((None, tr, D), lambda b, r: (b, r, 0))
    per_ex = pl.BlockSpec((None, 1, D), lambda b, r: (b, 0, 0))
    return pl.pallas_call(
        body, name="norm_mod_fwd", grid=(NB, S // tr),
        in_specs=[tok, pl.BlockSpec((1, D), lambda b, r: (0, 0)), per_ex, per_ex],
        out_specs=tok, out_shape=jax.ShapeDtypeStruct((NB, S, D), BF16),
        compiler_params=_params(("parallel", "parallel")),
    )(x, gain, sc, sh)


def _norm_mod_bwd(a, w, x, gain, sc, dres, name, y=None, g=None):
    NB, S, D = x.shape
    T, K = a.shape
    tm = _pick(S, 512, 128)
    per_ex_tiles = S // tm
    tk = K if K <= MATMUL_SINGLE_K else _pick(K, 1536, 128)
    nk = K // tk
    gated = y is not None

    def body(*refs):
        a_ref, w_ref, x_ref, g_ref, sc_ref, dres_ref = refs[:6]
        refs = refs[6:]
        if gated:
            y_ref, gate_ref = refs[:2]
            refs = refs[2:]
        dx_ref, dsh_ref, dsc_ref, dgain_ref = refs[:4]
        acc_ref = refs[-1]
        i, k = pl.program_id(0), pl.program_id(1)

        @pl.when(k == 0)
        def _():
            acc_ref[...] = jnp.zeros_like(acc_ref)

        acc_ref[...] += lax.dot_general(a_ref[...], w_ref[...], _NT, preferred_element_type=F32)

        @pl.when(k == nk - 1)
        def _():
            first_of_example = i % per_ex_tiles == 0

            @pl.when(first_of_example)
            def _():
                dsh_ref[...] = jnp.zeros_like(dsh_ref)
                dsc_ref[...] = jnp.zeros_like(dsc_ref)
                if gated:
                    refs[5][...] = jnp.zeros_like(refs[5])

            @pl.when(i == 0)
            def _():
                dgain_ref[...] = jnp.zeros_like(dgain_ref)

            xv = x_ref[...]
            rstd = lax.rsqrt(jnp.mean(xv * xv, axis=-1, keepdims=True) + EPS)
            xh = xv * rstd
            gn = g_ref[...]
            dh = acc_ref[...]
            dsh_ref[...] += jnp.sum(dh, axis=0, keepdims=True)
            dsc_ref[...] += jnp.sum(dh * (xh * gn), axis=0, keepdims=True)
            dn = dh * (1.0 + sc_ref[...])
            dgain_ref[...] += jnp.sum(dn * xh, axis=0, keepdims=True)
            dxh = dn * gn
            proj = jnp.mean(dxh * xh, axis=-1, keepdims=True)
            dx = rstd * (dxh - xh * proj) + dres_ref[...]
            dx_ref[...] = dx
            if gated:
                refs[4][...] = (dx * gate_ref[...]).astype(BF16)
                refs[5][...] += jnp.sum(dx * y_ref[...], axis=0, keepdims=True)

    tok = pl.BlockSpec((None, tm, D), lambda i, k: (i // per_ex_tiles, i % per_ex_tiles, 0))
    per_ex = pl.BlockSpec((None, 1, D), lambda i, k: (i // per_ex_tiles, 0, 0))
    row = pl.BlockSpec((1, D), lambda i, k: (0, 0))
    in_specs = [pl.BlockSpec((tm, tk), lambda i, k: (i, k)), pl.BlockSpec((D, tk), lambda i, k: (0, k)),
                tok, row, per_ex, tok]
    out_specs = [tok, per_ex, per_ex, row]
    out_shape = [jax.ShapeDtypeStruct((NB, S, D), F32), jax.ShapeDtypeStruct((NB, 1, D), F32),
                 jax.ShapeDtypeStruct((NB, 1, D), F32), jax.ShapeDtypeStruct((1, D), F32)]
    operands = [a, w, x, gain, sc, dres]
    if gated:
        in_specs += [tok, per_ex]
        out_specs += [tok, per_ex]
        out_shape += [jax.ShapeDtypeStruct((NB, S, D), BF16), jax.ShapeDtypeStruct((NB, 1, D), F32)]
        operands += [y, g]
    return pl.pallas_call(
        body, name=name, grid=(T // tm, nk), in_specs=in_specs, out_specs=out_specs, out_shape=out_shape,
        scratch_shapes=[pltpu.VMEM((tm, D), F32)],
        compiler_params=_params(("arbitrary", "arbitrary")),
    )(*operands)


def _gate_res(x, y, g, norm=None):
    NB, S, D = x.shape
    tr = _row_tile(S)

    def body(x_ref, y_ref, g_ref, *refs):
        xo = x_ref[...] + g_ref[...] * y_ref[...]
        refs[-1 if norm is None else -2][...] = xo
        if norm is not None:
            gain_ref, sc_ref, sh_ref, _, h_ref = refs
            n = xo * lax.rsqrt(jnp.mean(xo * xo, axis=-1, keepdims=True) + EPS) * gain_ref[...]
            h_ref[...] = (n * (1.0 + sc_ref[...]) + sh_ref[...]).astype(BF16)

    tok = pl.BlockSpec((None, tr, D), lambda b, r: (b, r, 0))
    per_ex = pl.BlockSpec((None, 1, D), lambda b, r: (b, 0, 0))
    in_specs, out_specs, operands = [tok, tok, per_ex], [tok], [x, y, g]
    out_shape = [jax.ShapeDtypeStruct((NB, S, D), F32)]
    if norm is not None:
        in_specs += [pl.BlockSpec((1, D), lambda b, r: (0, 0)), per_ex, per_ex]
        out_specs.append(tok)
        out_shape.append(jax.ShapeDtypeStruct((NB, S, D), BF16))
        operands += list(norm)
    out = pl.pallas_call(
        body, name="gate_res", grid=(NB, S // tr), in_specs=in_specs, out_specs=out_specs, out_shape=out_shape,
        compiler_params=_params(("parallel", "parallel")),
    )(*operands)
    return out[0] if norm is None else out


def _gate_res_bwd(dxo, y, g):
    NB, S, D = dxo.shape
    tr = _row_tile(S)

    def body(d_ref, y_ref, g_ref, dy_ref, dg_ref):
        @pl.when(pl.program_id(1) == 0)
        def _():
            dg_ref[...] = jnp.zeros_like(dg_ref)

        d = d_ref[...]
        dy_ref[...] = (d * g_ref[...]).astype(BF16)
        dg_ref[...] += jnp.sum(d * y_ref[...], axis=0, keepdims=True)

    tok = pl.BlockSpec((None, tr, D), lambda b, r: (b, r, 0))
    per_ex = pl.BlockSpec((None, 1, D), lambda b, r: (b, 0, 0))
    return pl.pallas_call(
        body, name="gate_res_bwd", grid=(NB, S // tr), in_specs=[tok, tok, per_ex], out_specs=[tok, per_ex],
        out_shape=[jax.ShapeDtypeStruct((NB, S, D), BF16), jax.ShapeDtypeStruct((NB, 1, D), F32)],
        compiler_params=_params(("arbitrary", "arbitrary")),
    )(dxo, y, g)


def _sigmoid(v):
    return 1.0 / (1.0 + jnp.exp(-v))


def _ff_tile(F):
    return _pick(F, 1536, 128)


def _interleave(gate, up):
    F = gate.shape[-1]
    tf = _ff_tile(F)
    parts = []
    for j in range(F // tf):
        parts += [gate[..., j * tf:(j + 1) * tf], up[..., j * tf:(j + 1) * tf]]
    return jnp.concatenate(parts, axis=-1)


def _swiglu_bwd(dm, wd, gu):
    T, D = dm.shape
    F = wd.shape[0]
    tf = _ff_tile(F)
    tm = _pick(T, 1024, 128)
    assert D <= MATMUL_SINGLE_K

    def body(a_ref, b_ref, gu_ref, o_ref):
        d = lax.dot_general(a_ref[...], b_ref[...], _NT, preferred_element_type=F32)
        g, u = gu_ref[:, :tf].astype(F32), gu_ref[:, tf:].astype(F32)
        s = _sigmoid(g)
        o_ref[:, :tf] = (d * u * (s * (1.0 + g * (1.0 - s)))).astype(BF16)
        o_ref[:, tf:] = (d * (g * s)).astype(BF16)

    return pl.pallas_call(
        body, name="swiglu_bwd", grid=(T // tm, F // tf),
        in_specs=[pl.BlockSpec((tm, D), lambda i, j: (i, 0)), pl.BlockSpec((tf, D), lambda i, j: (j, 0)),
                  pl.BlockSpec((tm, 2 * tf), lambda i, j: (i, j))],
        out_specs=pl.BlockSpec((tm, 2 * tf), lambda i, j: (i, j)),
        out_shape=jax.ShapeDtypeStruct((T, 2 * F), BF16),
        compiler_params=_params(("parallel", "parallel")),
    )(dm, wd, gu)


def _loss_fwd_bwd(y, target):
    NB, S, D = y.shape
    tr = _row_tile(S)

    def body(y_ref, t_ref, l_ref, d_ref):
        @pl.when((pl.program_id(0) == 0) & (pl.program_id(1) == 0))
        def _():
            l_ref[...] = jnp.zeros_like(l_ref)

        e = y_ref[...] - t_ref[...]
        d_ref[...] = e / D
        l_ref[...] += 0.5 * jnp.sum(jnp.mean(e * e, axis=-1, keepdims=True), axis=0, keepdims=True)

    tok = pl.BlockSpec((None, tr, D), lambda b, r: (b, r, 0))
    return pl.pallas_call(
        body, name="loss", grid=(NB, S // tr), in_specs=[tok, tok],
        out_specs=[pl.BlockSpec((1, 1), lambda b, r: (0, 0)), tok],
        out_shape=[jax.ShapeDtypeStruct((1, 1), F32), jax.ShapeDtypeStruct((NB, S, D), F32)],
        compiler_params=_params(("arbitrary", "arbitrary")),
    )(y, target)


def _half_sums(v, lo):
    sa = jnp.sum(jnp.where(lo, v, 0.0), axis=-1, keepdims=True)
    sb = jnp.sum(jnp.where(lo, 0.0, v), axis=-1, keepdims=True)
    return jnp.where(lo, sa, sb)


def _rope_swap(v, lane64):
    up = pltpu.roll(v, LANES - ROT_DIM // 2, 1)
    down = pltpu.roll(v, ROT_DIM // 2, 1)
    return jnp.where(lane64 < ROT_DIM // 2, up, jnp.where(lane64 < ROT_DIM, down, 0.0))


def _qk_prep_fwd(qkv, tab_c, tab_s, gains):
    T, W = qkv.shape
    R = W // LANES
    tt = _pick(T, 256, 8)

    def body(x_ref, c_ref, s_ref, g_ref, o_ref):
        lane = lax.broadcasted_iota(jnp.int32, (tt, LANES), 1)
        lo = lane < HEAD_DIM
        lane64 = lane & (HEAD_DIM - 1)
        c, s = c_ref[...], s_ref[...]
        for j in range(R - 1):
            cols = slice(j * LANES, (j + 1) * LANES)
            xv = x_ref[:, cols]
            rstd = lax.rsqrt(_half_sums(xv * xv, lo) / HEAD_DIM + EPS)
            yn = xv * rstd * g_ref[j:j + 1, :]
            o_ref[:, cols] = (yn * c + _rope_swap(yn, lane64) * s).astype(BF16)
        o_ref[:, (R - 1) * LANES:] = x_ref[:, (R - 1) * LANES:].astype(BF16)

    tok = pl.BlockSpec((tt, W), lambda t: (t, 0))
    tab = pl.BlockSpec((tt, LANES), lambda t: (t, 0))
    return pl.pallas_call(
        body, name="qk_prep_fwd", grid=(T // tt,),
        in_specs=[tok, tab, tab, pl.BlockSpec((R, LANES), lambda t: (0, 0))],
        out_specs=tok, out_shape=jax.ShapeDtypeStruct((T, W), BF16),
        compiler_params=_params(("parallel",)),
    )(qkv, tab_c, tab_s, gains)


def _qk_prep_bwd(qkv, dq, dk, dv, tab_c, tab_s, gains):
    T, W = qkv.shape
    R = W // LANES
    QW = dq.shape[1]
    tt = _pick(T, 256, 8)

    def body(x_ref, dq_ref, dk_ref, dv_ref, c_ref, s_ref, g_ref, o_ref, dg_ref):
        @pl.when(pl.program_id(0) == 0)
        def _():
            dg_ref[...] = jnp.zeros_like(dg_ref)

        lane = lax.broadcasted_iota(jnp.int32, (tt, LANES), 1)
        lo = lane < HEAD_DIM
        lane64 = lane & (HEAD_DIM - 1)
        c, s = c_ref[...], s_ref[...]
        for j in range(R - 1):
            cols = slice(j * LANES, (j + 1) * LANES)
            xv = x_ref[:, cols]
            d = dq_ref[:, cols] if j < R - 2 else dk_ref[...]
            rstd = lax.rsqrt(_half_sums(xv * xv, lo) / HEAD_DIM + EPS)
            xh = xv * rstd
            dyn = d * c + _rope_swap(d * s, lane64)
            dg_ref[j:j + 1, :] += jnp.sum(dyn * xh, axis=0, keepdims=True)
            dxh = dyn * g_ref[j:j + 1, :]
            proj = _half_sums(dxh * xh, lo) / HEAD_DIM
            o_ref[:, cols] = (rstd * (dxh - xh * proj)).astype(BF16)
        o_ref[:, (R - 1) * LANES:] = dv_ref[...].astype(BF16)

    tok = pl.BlockSpec((tt, W), lambda t: (t, 0))
    tab = pl.BlockSpec((tt, LANES), lambda t: (t, 0))
    gsp = pl.BlockSpec((R, LANES), lambda t: (0, 0))
    return pl.pallas_call(
        body, name="qk_prep_bwd", grid=(T // tt,),
        in_specs=[tok, pl.BlockSpec((tt, QW), lambda t: (t, 0)), tab, tab, tab, tab, gsp], out_specs=[tok, gsp],
        out_shape=[jax.ShapeDtypeStruct((T, W), BF16), jax.ShapeDtypeStruct((R, LANES), F32)],
        compiler_params=_params(("arbitrary",)),
    )(qkv, dq, dk, dv, tab_c, tab_s, gains)


def _band_mask(i):
    r = lax.broadcasted_iota(jnp.int32, (2 * BLOCK, 2 * BLOCK), 0) & (BLOCK - 1)
    c = lax.broadcasted_iota(jnp.int32, (2 * BLOCK, 2 * BLOCK), 1)
    rel = r + BLOCK - c
    return (rel >= 0) & (rel < BLOCK) & ((c >= BLOCK) | (i > 0))


def _swa_softmax(s, valid, sink):
    s = jnp.where(valid, s * ATTN_SCALE, NEG_BIG)
    m = jnp.maximum(jnp.max(s, axis=1, keepdims=True), sink)
    p = jnp.exp(s - m)
    ps = jnp.exp(sink - m)
    denom = jnp.sum(p, axis=1, keepdims=True) + ps
    return p / denom, ps / denom


A_GROUP = 4


Q_WIDTH_A = N_Q_A * HEAD_DIM
N_PAIR_A = Q_WIDTH_A // LANES


def _swa_specs():
    qs = pl.BlockSpec((None, BLOCK, Q_WIDTH_A), lambda b, i: (b, i, 0))

    def kv(col, back):
        return pl.BlockSpec((None, BLOCK, LANES), lambda b, i: (b, jnp.maximum(i - back, 0), col))

    return qs, kv(N_PAIR_A, 1), kv(N_PAIR_A, 0), kv(N_PAIR_A + 1, 1), kv(N_PAIR_A + 1, 0)


def _dup_heads(t):
    lo = lax.broadcasted_iota(jnp.int32, t.shape, 1) < HEAD_DIM
    sw = pltpu.roll(t.astype(F32), HEAD_DIM, 1).astype(BF16)
    return jnp.where(lo, t, sw), jnp.where(lo, sw, t)


def _kv_tiles(kp_ref, kc_ref, vp_ref, vc_ref):
    kd = _dup_heads(jnp.concatenate([kp_ref[...], kc_ref[...]], axis=0))
    vd = _dup_heads(jnp.concatenate([vp_ref[...], vc_ref[...]], axis=0))
    return kd, vd


def _attn_a_fwd(qkn, sinks):
    NB, S, _ = qkn.shape
    qs, kp, kc, vp, vc = _swa_specs()

    def body(q_ref, kp_ref, kc_ref, vp_ref, vc_ref, sink_ref, o_ref):
        i = pl.program_id(1)
        kd, vd = _kv_tiles(kp_ref, kc_ref, vp_ref, vc_ref)
        valid = _band_mask(i)
        lo = lax.broadcasted_iota(jnp.int32, (BLOCK, LANES), 1) < HEAD_DIM
        top = lax.broadcasted_iota(jnp.int32, (2 * BLOCK, 1), 0) < BLOCK
        for first in range(0, N_PAIR_A, A_GROUP):
            pairs = range(first, first + A_GROUP)
            qs_ = [jnp.concatenate(_head_halves(q_ref[:, p * LANES:(p + 1) * LANES], lo), axis=0) for p in pairs]
            ss = [lax.dot_general(q, kd[2 * p // GROUP_A], _NT, preferred_element_type=F32) for q, p in zip(qs_, pairs)]
            pns = [_swa_softmax(s, valid, jnp.where(top, sink_ref[2 * p], sink_ref[2 * p + 1]))[0]
                   for s, p in zip(ss, pairs)]
            pvs = [jnp.dot(pn.astype(BF16), vd[2 * p // GROUP_A], preferred_element_type=F32) for pn, p in zip(pns, pairs)]
            for pv, p in zip(pvs, pairs):
                o_ref[:, p * LANES:(p + 1) * LANES] = jnp.where(lo, pv[:BLOCK], pv[BLOCK:]).astype(BF16)

    return pl.pallas_call(
        body, name="attn_a_fwd", grid=(NB, S // BLOCK),
        in_specs=[qs, kp, kc, vp, vc, pl.BlockSpec(memory_space=pltpu.SMEM)],
        out_specs=qs, out_shape=jax.ShapeDtypeStruct((NB, S, Q_WIDTH_A), BF16),
        compiler_params=_params(("parallel", "arbitrary")),
    )(qkn, qkn, qkn, qkn, qkn, sinks)


def _attn_a_bwd(qkn, do, sinks):
    NB, S, _ = qkn.shape
    qs, kp, kc, vp, vc = _swa_specs()
    full = pl.BlockSpec((None, S, LANES), lambda b, i: (b, 0, 0))
    sink_out = pl.BlockSpec((None, N_Q_A, LANES), lambda b, i: (b, 0, 0))

    def body(q_ref, do_ref, kp_ref, kc_ref, vp_ref, vc_ref, sink_ref, dq_ref, dk_ref, dv_ref, ds_ref, dk_s, dv_s):
        i = pl.program_id(1)

        @pl.when(i == 0)
        def _():
            dk_ref[...] = jnp.zeros_like(dk_ref)
            dv_ref[...] = jnp.zeros_like(dv_ref)
            ds_ref[...] = jnp.zeros_like(ds_ref)

        dk_s[...] = jnp.zeros_like(dk_s)
        dv_s[...] = jnp.zeros_like(dv_s)
        kd, vd = _kv_tiles(kp_ref, kc_ref, vp_ref, vc_ref)
        valid = _band_mask(i)
        lo = lax.broadcasted_iota(jnp.int32, (BLOCK, LANES), 1) < HEAD_DIM
        top = lax.broadcasted_iota(jnp.int32, (2 * BLOCK, 1), 0) < BLOCK
        for first in range(0, N_PAIR_A, A_GROUP):
            pairs = range(first, first + A_GROUP)
            kvs = [2 * p // GROUP_A for p in pairs]
            qs_ = [jnp.concatenate(_head_halves(q_ref[:, p * LANES:(p + 1) * LANES], lo), axis=0) for p in pairs]
            dos = [jnp.concatenate(_head_halves(do_ref[:, p * LANES:(p + 1) * LANES], lo), axis=0) for p in pairs]
            ss = [lax.dot_general(q, kd[kv], _NT, preferred_element_type=F32) for q, kv in zip(qs_, kvs)]
            dps = [lax.dot_general(d, vd[kv], _NT, preferred_element_type=F32) for d, kv in zip(dos, kvs)]
            sm = [_swa_softmax(s, valid, jnp.where(top, sink_ref[2 * p], sink_ref[2 * p + 1])) for s, p in zip(ss, pairs)]
            deltas = [jnp.sum(pn * dp, axis=1, keepdims=True) for (pn, _), dp in zip(sm, dps)]
            dsbs = [(pn * (dp - delta) * ATTN_SCALE).astype(BF16) for (pn, _), dp, delta in zip(sm, dps, deltas)]
            for n, p in enumerate(pairs):
                dq2 = jnp.dot(dsbs[n], kd[kvs[n]], preferred_element_type=F32)
                dq_ref[:, p * LANES:(p + 1) * LANES] = jnp.where(lo, dq2[:BLOCK], dq2[BLOCK:])
                dk_s[kvs[n]] += lax.dot_general(dsbs[n], qs_[n], _TN, preferred_element_type=F32)
                dv_s[kvs[n]] += lax.dot_general(sm[n][0].astype(BF16), dos[n], _TN, preferred_element_type=F32)
                t = sm[n][1] * deltas[n]
                for hh in range(2):
                    dsink = -jnp.sum(t[hh * BLOCK:(hh + 1) * BLOCK], axis=0, keepdims=True)
                    ds_ref[2 * p + hh:2 * p + hh + 1, :] += jnp.broadcast_to(dsink, (1, LANES))

        lo2 = lax.broadcasted_iota(jnp.int32, (2 * BLOCK, LANES), 1) < HEAD_DIM

        def fold(acc):
            halves = [acc[kv] + pltpu.roll(acc[kv], HEAD_DIM, 1) for kv in range(N_KV_A)]
            return jnp.where(lo2, halves[0], halves[1])

        dk2, dv2 = fold(dk_s), fold(dv_s)

        @pl.when(i > 0)
        def _():
            start = pl.multiple_of((i - 1) * BLOCK, BLOCK)
            dk_ref[pl.ds(start, 2 * BLOCK), :] += dk2
            dv_ref[pl.ds(start, 2 * BLOCK), :] += dv2

        @pl.when(i == 0)
        def _():
            dk_ref[0:BLOCK, :] += dk2[BLOCK:, :]
            dv_ref[0:BLOCK, :] += dv2[BLOCK:, :]

    slots = pltpu.VMEM((N_KV_A, 2 * BLOCK, LANES), F32)
    return pl.pallas_call(
        body, name="attn_a_bwd", grid=(NB, S // BLOCK),
        in_specs=[qs, qs, kp, kc, vp, vc, pl.BlockSpec(memory_space=pltpu.SMEM)],
        out_specs=[qs, full, full, sink_out],
        out_shape=[jax.ShapeDtypeStruct((NB, S, Q_WIDTH_A), F32), jax.ShapeDtypeStruct((NB, S, LANES), F32),
                   jax.ShapeDtypeStruct((NB, S, LANES), F32), jax.ShapeDtypeStruct((NB, N_Q_A, LANES), F32)],
        scratch_shapes=[slots, slots],
        compiler_params=_params(("parallel", "arbitrary")),
    )(qkn, do, qkn, qkn, qkn, qkn, sinks)


def _cumsum_mats():
    src = lax.broadcasted_iota(jnp.int32, (2 * BLOCK, 2 * BLOCK), 0) % BLOCK
    dst = lax.broadcasted_iota(jnp.int32, (2 * BLOCK, 2 * BLOCK), 1)
    ones = dst >= BLOCK
    rev = ((src > dst) | ones).astype(BF16)
    fwd = ((src < dst) | ones).astype(BF16)
    return rev, fwd


def _log_sigmoids(z):
    sp = jnp.log(1.0 + jnp.exp(-jnp.abs(z)))
    return jnp.minimum(z, 0.0) - sp, -(jnp.maximum(z, 0.0) + sp)


def _cumsum_mxu_many(vs, mat):
    parts = []
    for v in vs:
        hi = v.astype(BF16)
        parts.append(jnp.concatenate([hi, (v - hi.astype(F32)).astype(BF16)], axis=1))
    r = jnp.dot(jnp.concatenate(parts, axis=0), mat, preferred_element_type=F32)
    return [(r[n * BLOCK:(n + 1) * BLOCK, :BLOCK], r[n * BLOCK:(n + 1) * BLOCK, BLOCK:]) for n in range(len(vs))]


def _strict_mask():
    r = lax.broadcasted_iota(jnp.int32, (BLOCK, BLOCK), 0)
    c = lax.broadcasted_iota(jnp.int32, (BLOCK, BLOCK), 1)
    return c < r


def _tile(ref, j):
    return ref[pl.ds(pl.multiple_of(j * BLOCK, BLOCK), BLOCK), :]


SWEEP_EXIT = -88.0


def _head_halves(t, lo):
    zero = jnp.zeros_like(t)
    return jnp.where(lo, t, zero), jnp.where(lo, zero, t)


def _sb_specs(S, HD, width):
    n = HD // width
    blk = pl.BlockSpec((None, BLOCK, width), lambda b, p, i: (b, i, p))
    k_full = pl.BlockSpec((None, S, width), lambda b, p, i: (b, 0, n + p))
    v_full = pl.BlockSpec((None, S, width), lambda b, p, i: (b, 0, 2 * n + p))
    mat = pl.BlockSpec((2 * BLOCK, 2 * BLOCK), lambda b, p, i: (0, 0))
    return blk, k_full, v_full, mat


SB_FWD_PAIRS = 4
SB_BWD_PAIRS = 2
SB_BWD_TILES = 2
SB_BWD_VMEM_LIMIT_BYTES = 58 * 1024 * 1024


def _attn_b_fwd(qkv, rev):
    NB, S, W = qkv.shape
    HD = W // 3
    width = SB_FWD_PAIRS * LANES
    n_heads = 2 * SB_FWD_PAIRS
    blk, k_full, v_full, mat = _sb_specs(S, HD, width)

    def body(q_ref, k_ref, v_ref, rev_ref, o_ref):
        i = pl.program_id(2)
        rv = rev_ref[...]
        mask = _strict_mask()
        lo = lax.broadcasted_iota(jnp.int32, (BLOCK, LANES), 1) < HEAD_DIM
        q_all = q_ref[...]
        q_stack = [jnp.concatenate(_head_halves(q_all[:, p * LANES:(p + 1) * LANES] * ATTN_SCALE, lo), axis=0)
                   for p in range(SB_FWD_PAIRS)]

        def pair_tiles(ref, j):
            t = _tile(ref, j)
            return [t[:, p * LANES:(p + 1) * LANES] for p in range(SB_FWD_PAIRS)]

        def tile_pass(j, carries, diagonal):
            ks, vs = pair_tiles(k_ref, j), pair_tiles(v_ref, j)
            zs = []
            for p in range(SB_FWD_PAIRS):
                z2 = lax.dot_general(q_stack[p], ks[p], _NT, preferred_element_type=F32)
                zs += [z2[:BLOCK], z2[BLOCK:]]
            logs = [_log_sigmoids(z) for z in zs]
            cums = _cumsum_mxu_many([jnp.where(mask, lm, 0.0) if diagonal else lm for _, lm in logs], rv)
            probs, new_c = [], []
            for h in range(n_heads):
                after, rs = cums[h]
                if diagonal:
                    a = jnp.where(mask, jnp.exp(logs[h][0] + after), 0.0)
                    new_c.append(rs)
                else:
                    a = jnp.exp(logs[h][0] + after + carries[h])
                    new_c.append(carries[h] + rs)
                probs.append(a.astype(BF16))
            outs = []
            for p in range(SB_FWD_PAIRS):
                pv = jnp.dot(jnp.concatenate(probs[2 * p:2 * p + 2], axis=0), vs[p], preferred_element_type=F32)
                outs.append(jnp.where(lo, pv[:BLOCK], pv[BLOCK:]))
            return new_c, outs

        carries, accs = tile_pass(i, None, True)

        def live(cs):
            top = cs[0]
            for c in cs[1:]:
                top = jnp.maximum(top, c)
            return jnp.max(top) > SWEEP_EXIT

        def cond(st):
            return (st[0] < i) & st[1]

        def step(st):
            jj, _, cs, accs = st
            new_c, outs = tile_pass(i - 1 - jj, cs, False)
            return jj + 1, live(new_c), new_c, [acc + o for acc, o in zip(accs, outs)]

        st = lax.while_loop(cond, step, (jnp.int32(0), live(carries), carries, accs))
        for p in range(SB_FWD_PAIRS):
            o_ref[:, p * LANES:(p + 1) * LANES] = st[3][p].astype(BF16)

    return pl.pallas_call(
        body, name="attn_b_fwd", grid=(NB, HD // width, S // BLOCK),
        in_specs=[blk, k_full, v_full, mat], out_specs=blk,
        out_shape=jax.ShapeDtypeStruct((NB, S, HD), BF16),
        compiler_params=_params(("parallel", "parallel", "arbitrary")),
    )(qkv, qkv, qkv, rev)


def _attn_b_bwd(qkv, do, rev, fwd):
    NB, S, W = qkv.shape
    HD = W // 3
    width = SB_BWD_PAIRS * LANES
    n_heads = 2 * SB_BWD_PAIRS
    nj = S // BLOCK
    blk, k_full, v_full, mat = _sb_specs(S, HD, width)
    acc_full = pl.BlockSpec((None, S, width), lambda b, p, i: (b, 0, p))

    def body(q_ref, do_ref, k_ref, v_ref, rev_ref, fwd_ref, dq_ref, dk_ref, dv_ref, sig_s, a_s, e_s):
        i = pl.program_id(2)

        @pl.when(i == 0)
        def _():
            dk_ref[...] = jnp.zeros_like(dk_ref)
            dv_ref[...] = jnp.zeros_like(dv_ref)

        rv, fw = rev_ref[...], fwd_ref[...]
        mask = _strict_mask()
        lo = lax.broadcasted_iota(jnp.int32, (BLOCK, LANES), 1) < HEAD_DIM
        pairs = range(SB_BWD_PAIRS)

        def cols(p):
            return slice(p * LANES, (p + 1) * LANES)

        q_stack = [jnp.concatenate(_head_halves(q_ref[:, cols(p)], lo), axis=0) for p in pairs]
        qs_stack = [q * ATTN_SCALE for q in q_stack]
        do_stack = [jnp.concatenate(_head_halves(do_ref[:, cols(p)], lo), axis=0) for p in pairs]

        def sweep1_tiles(js, carries, diagonal):
            zs, das = [], []
            for j in js:
                kj, vj = _tile(k_ref, j), _tile(v_ref, j)
                for p in pairs:
                    z2 = lax.dot_general(qs_stack[p], kj[:, cols(p)], _NT, preferred_element_type=F32)
                    da2 = lax.dot_general(do_stack[p], vj[:, cols(p)], _NT, preferred_element_type=F32)
                    zs += [z2[:BLOCK], z2[BLOCK:]]
                    das += [da2[:BLOCK], da2[BLOCK:]]
            logs = [_log_sigmoids(z) for z in zs]
            cums = _cumsum_mxu_many([jnp.where(mask, lm, 0.0) if diagonal else lm for _, lm in logs], rv)
            new_c, stores = [], []
            for h in range(n_heads):
                carry = None if diagonal else carries[h]
                for t, j in enumerate(js):
                    n = t * n_heads + h
                    lb, (after, rs) = logs[n][0], cums[n]
                    if diagonal:
                        a = jnp.where(mask, jnp.exp(lb + after), 0.0)
                        carry = rs
                    else:
                        a = jnp.exp(lb + after + carry)
                        carry = carry + rs
                    stores.append((t, h, j, jnp.exp(lb), a.astype(BF16), das[n] * a))
                new_c.append(carry)
            for t, h, j, sg, ab, e in sorted(stores, key=lambda s: -s[0]):
                sig_s[h, j] = sg
                a_s[h, j] = ab
                e_s[h, j] = e
            return new_c

        carries = sweep1_tiles([i], None, True)

        def live(cs):
            top = cs[0]
            for c in cs[1:]:
                top = jnp.maximum(top, c)
            return jnp.max(top) > SWEEP_EXIT

        def cond(st):
            return (SB_BWD_TILES * st[0] < i) & st[1]

        def sweep1(st):
            first = i - 1 - SB_BWD_TILES * st[0]
            new_c = sweep1_tiles([jnp.maximum(first - t, 0) for t in range(SB_BWD_TILES)], st[2], False)
            return st[0] + 1, live(new_c), new_c

        trips = lax.while_loop(cond, sweep1, (jnp.int32(0), live(carries), carries))[0]
        lowest = jnp.maximum(i - SB_BWD_TILES * trips, 0)

        def grads(js, st, diagonal):
            prefixes, dqs = st
            es = [e_s[h, j] for j in js for h in range(n_heads)]
            cums = _cumsum_mxu_many(es, fw)
            dzs, new_p = [], []
            for h in range(n_heads):
                prefix = prefixes[h]
                for t, j in enumerate(js):
                    n = t * n_heads + h
                    sg = sig_s[h, j]
                    e_before, rs = cums[n]
                    dz = (es[n] * (1.0 - sg) - (e_before + prefix) * sg) * ATTN_SCALE
                    if diagonal:
                        dz = jnp.where(mask, dz, 0.0)
                    dzs.append((t, h, dz.astype(BF16)))
                    prefix = prefix + rs
                new_p.append(prefix)
            dz_of = {(t, h): dz for t, h, dz in dzs}
            new_dq = list(dqs)
            for t, j in enumerate(js):
                kj = _tile(k_ref, j)
                rows = pl.ds(pl.multiple_of(j * BLOCK, BLOCK), BLOCK)
                for p in pairs:
                    dz_stack = jnp.concatenate([dz_of[t, 2 * p], dz_of[t, 2 * p + 1]], axis=0)
                    a_stack = jnp.concatenate([a_s[2 * p, j], a_s[2 * p + 1, j]], axis=0)
                    dq2 = jnp.dot(dz_stack, kj[:, cols(p)], preferred_element_type=F32)
                    new_dq[p] = new_dq[p] + jnp.where(lo, dq2[:BLOCK], dq2[BLOCK:])
                    dk_ref[rows, cols(p)] += lax.dot_general(dz_stack, q_stack[p], _TN, preferred_element_type=F32)
                    dv_ref[rows, cols(p)] += lax.dot_general(a_stack, do_stack[p], _TN, preferred_element_type=F32)
            return new_p, new_dq

        zeros = jnp.zeros((BLOCK, BLOCK), F32)
        st = ([zeros] * n_heads, [zeros] * SB_BWD_PAIRS)
        count = i - lowest
        st = lax.fori_loop(0, count % SB_BWD_TILES, lambda t, st: grads([lowest + t], st, False), st)
        start = lowest + count % SB_BWD_TILES
        st = lax.fori_loop(0, count // SB_BWD_TILES,
                           lambda t, st: grads([start + SB_BWD_TILES * t + u for u in range(SB_BWD_TILES)], st, False), st)
        dqs = grads([i], st, True)[1]
        for p in pairs:
            dq_ref[:, cols(p)] = dqs[p]

    f32_stash = pltpu.VMEM((n_heads, nj, BLOCK, BLOCK), F32)
    bf16_stash = pltpu.VMEM((n_heads, nj, BLOCK, BLOCK), BF16)
    return pl.pallas_call(
        body, name="attn_b_bwd", grid=(NB, HD // width, nj),
        in_specs=[blk, blk, k_full, v_full, mat, mat], out_specs=[blk, acc_full, acc_full],
        out_shape=[jax.ShapeDtypeStruct((NB, S, HD), F32)] * 3,
        scratch_shapes=[f32_stash, bf16_stash, f32_stash],
        compiler_params=_params(("parallel", "parallel", "arbitrary"), SB_BWD_VMEM_LIMIT_BYTES),
    )(qkv, do, qkv, qkv, rev, fwd)


def _ada_fwd(c_all, w, b):
    L, D, N = w.shape
    B = c_all.shape[0]

    def body(c_ref, w_ref, b_ref, o_ref):
        cv = c_ref[...]
        cond = (cv * _sigmoid(cv)).astype(BF16)
        o_ref[...] = jnp.dot(cond, w_ref[...].astype(BF16), preferred_element_type=F32) + b_ref[...]

    return pl.pallas_call(
        body, name="ada_fwd", grid=(L,),
        in_specs=[pl.BlockSpec((B, D), lambda l: (0, 0)), pl.BlockSpec((None, D, N), lambda l: (l, 0, 0)),
                  pl.BlockSpec((None, 1, N), lambda l: (l, 0, 0))],
        out_specs=pl.BlockSpec((None, B, N), lambda l: (l, 0, 0)),
        out_shape=jax.ShapeDtypeStruct((L, B, N), F32),
        compiler_params=_params(("parallel",)),
    )(c_all, w, b)


def _ada_bwd(c_all, dmod_all, dmod_shard):
    L, B, N = dmod_shard.shape
    D = c_all.shape[1]
    N_all = dmod_all.shape[2]

    def body(c_ref, da_ref, ds_ref, gw_ref, gb_ref):
        cv = c_ref[...]
        cond = (cv * _sigmoid(cv)).astype(BF16)
        gw_ref[...] = lax.dot_general(cond, ds_ref[...].astype(BF16), _TN, preferred_element_type=F32)
        gb_ref[...] = jnp.sum(da_ref[...], axis=0, keepdims=True)

    return pl.pallas_call(
        body, name="ada_bwd", grid=(L,),
        in_specs=[pl.BlockSpec((B, D), lambda l: (0, 0)), pl.BlockSpec((None, B, N_all), lambda l: (l, 0, 0)),
                  pl.BlockSpec((None, B, N), lambda l: (l, 0, 0))],
        out_specs=[pl.BlockSpec((None, D, N), lambda l: (l, 0, 0)), pl.BlockSpec((None, 1, N_all), lambda l: (l, 0, 0))],
        out_shape=[jax.ShapeDtypeStruct((L, D, N), F32), jax.ShapeDtypeStruct((L, 1, N_all), F32)],
        compiler_params=_params(("parallel",)),
    )(c_all, dmod_all, dmod_shard)


def _adamw(w, g, m, v, name):
    shape = w.shape
    if w.ndim == 2:
        w, g, m, v = [t.reshape((1,) + shape) for t in (w, g, m, v)]
    L, R, C = w.shape
    tr = _pick(R, max(8, (1 << 18) // C), 8)
    c1 = 1.0 - ADAM_B1 ** ADAM_STEP
    c2 = 1.0 - ADAM_B2 ** ADAM_STEP

    def body(w_ref, g_ref, m_ref, v_ref, d_ref, nm_ref, nv_ref):
        gv = g_ref[...]
        nm = ADAM_B1 * m_ref[...] + (1.0 - ADAM_B1) * gv
        nv = ADAM_B2 * v_ref[...] + (1.0 - ADAM_B2) * (gv * gv)
        d_ref[...] = -ADAM_LR * ((nm / c1) / (jnp.sqrt(nv / c2) + ADAM_EPS) + ADAM_WD * w_ref[...])
        nm_ref[...] = nm
        nv_ref[...] = nv

    spec = pl.BlockSpec((None, tr, C), lambda l, r: (l, r, 0))
    out = pl.pallas_call(
        body, name=name, grid=(L, R // tr), in_specs=[spec] * 4, out_specs=[spec] * 3,
        out_shape=[jax.ShapeDtypeStruct((L, R, C), F32)] * 3,
        compiler_params=_params(("parallel", "parallel")),
    )(w, g, m, v)
    return [t.reshape(shape) for t in out]


_SHARDED = (("wqkv_a", 2), ("wo_a", 1), ("wqkv_b", 2), ("wo_b", 1), ("w_gate", 2), ("w_up", 2), ("w_down", 1))


def _pack_full(layers, axis, gate_up=None):
    L = len(layers)
    R, C = layers[0].shape

    def shards(m):
        if gate_up is not None:
            F = C // 2
            tf, Cs = _ff_tile(F), F // 4
            assert tf % Cs == 0
            starts = [(2 * (s * Cs // tf) + gate_up) * tf + s * Cs % tf for s in range(4)]
            return jnp.stack([m[:, st:st + Cs] for st in starts])
        if axis == 2:
            return m.reshape(R, 4, C // 4).transpose(1, 0, 2)
        return m.reshape(4, R // 4, C)

    halves = [jnp.stack([shards(m) for m in layers[h * (L // 2):(h + 1) * (L // 2)]], axis=1) for h in range(2)]
    return jnp.stack(halves)


def _unpack_full(gathered, axis):
    _, Lh, Rs, Cs = gathered.shape
    t = gathered.reshape(4, 2, Lh, Rs, Cs)
    layers = []
    for h in range(2):
        for l in range(Lh):
            piece = t[:, h, l]
            if axis == 2:
                layers.append(piece.transpose(1, 0, 2).reshape(Rs, 4 * Cs))
            else:
                layers.append(piece.reshape(4 * Rs, Cs))
    return layers


def _sum_slabs(own, index, recv, name, with_bf16=False):
    C = own.shape[-1]
    shape = own.shape[1:]
    out = _sum_leading(recv.reshape(recv.shape[0], -1, C), name, own=own.reshape(own.shape[0], -1, C),
                       own_index=index, with_bf16=with_bf16)
    if with_bf16:
        return out[0].reshape(shape), out[1].reshape(shape)
    return out.reshape(shape)


def _gather8(x, name):
    return _all_gather8([x], name)[0]


def _rope_tables(positions):
    half = ROT_DIM // 2
    inv_freq = jnp.power(jnp.float32(ROPE_THETA), -jnp.arange(half, dtype=F32) * 2.0 / ROT_DIM)
    ang = positions.astype(F32).reshape(-1, 1) * inv_freq
    cos, sin = jnp.cos(ang), jnp.sin(ang)
    T = ang.shape[0]
    rest = HEAD_DIM - ROT_DIM
    c64 = jnp.concatenate([cos, cos, jnp.ones((T, rest), F32)], axis=1)
    s64 = jnp.concatenate([-sin, sin, jnp.zeros((T, rest), F32)], axis=1)
    return jnp.tile(c64, (1, 2)), jnp.tile(s64, (1, 2))


def _gain_rows(q_gain, k_gain):
    q2 = jnp.tile(q_gain.reshape(1, HEAD_DIM), (GROUP_A, 2))
    k2 = jnp.tile(k_gain.reshape(1, HEAD_DIM), (1, 2))
    return jnp.concatenate([q2, k2, jnp.ones((1, LANES), F32)], axis=0)


def _local_step(x, positions, mod, norm1_g, norm2_g, q_norm_a, k_norm_a, sinks_a,
                wqkv_a, wo_a, wqkv_b, wo_b, wgu, wd, loss_target):
    NB, S, D = x.shape
    T = NB * S
    QA = N_Q_A * HEAD_DIM
    tab_c, tab_s = _rope_tables(positions)
    rev, fwd = _cumsum_mats()

    saved = []
    xc = x
    mods = [[mod[i][:, k * D:(k + 1) * D].reshape(NB, 1, D) for k in range(6)] for i in range(DEPTH)]
    h = _norm_mod_fwd(xc, norm1_g[0:1], mods[0][1], mods[0][0])
    for i in range(DEPTH):
        j = i // 2
        sh1, sc1, g1, sh2, sc2, g2 = mods[i]
        st = dict(x=xc, sc1=sc1, g1=g1, sc2=sc2, g2=g2)
        st["h"] = h.reshape(T, D)
        if i % 2 == 0:
            st["qkv"] = _matmul(st["h"], wqkv_a[j], "nn", F32, "qkv_a")
            st["gains"] = _gain_rows(q_norm_a[j], k_norm_a[j])
            st["qkn"] = _qk_prep_fwd(st["qkv"], tab_c, tab_s, st["gains"]).reshape(NB, S, -1)
            st["o"] = _attn_a_fwd(st["qkn"], sinks_a[j]).reshape(T, QA)
            y = _matmul(st["o"], wo_a[j], "nn", F32, "wo_a")
        else:
            st["qkv"] = _matmul(st["h"], wqkv_b[j], "nn", BF16, "qkv_b").reshape(NB, S, -1)
            st["o"] = _attn_b_fwd(st["qkv"], rev).reshape(T, N_H_B * HEAD_DIM)
            y = _matmul(st["o"], wo_b[j], "nn", F32, "wo_b")
        st["y"] = y.reshape(NB, S, D)
        x1, h2 = _gate_res(xc, st["y"], g1, norm=(norm2_g[i:i + 1], sc2, sh2))
        st["x1"] = x1
        st["h2"] = h2.reshape(T, D)
        st["gu"], st["act"] = _matmul(st["h2"], wgu[i], "nn", BF16, "gate_up", swiglu=True)
        st["m"] = _matmul(st["act"], wd[i], "nn", F32, "down").reshape(NB, S, D)
        if i + 1 < DEPTH:
            xc, h = _gate_res(x1, st["m"], g2, norm=(norm1_g[i + 1:i + 2], mods[i + 1][1], mods[i + 1][0]))
        else:
            xc = _gate_res(x1, st["m"], g2)
        saved.append(st)

    loss, dx = _loss_fwd_bwd(xc, loss_target)

    grads = {name: [None] * n for name, n in
             (("wqkv_a", 2), ("wo_a", 2), ("wqkv_b", 2), ("wo_b", 2), ("wgu", DEPTH), ("wd", DEPTH),
              ("norm1_g", DEPTH), ("norm2_g", DEPTH), ("q_norm_a", 2), ("k_norm_a", 2), ("sinks_a", 2))}
    dmod = [None] * DEPTH
    dm, dg2 = _gate_res_bwd(dx, saved[-1]["m"], saved[-1]["g2"])
    for i in reversed(range(DEPTH)):
        j = i // 2
        st = saved[i]
        dm = dm.reshape(T, D)
        grads["wd"][i] = _matmul(st["act"], dm, "tn", F32, "d_wd")
        dgu = _swiglu_bwd(dm, wd[i], st["gu"])
        grads["wgu"][i] = _matmul(st["h2"], dgu, "tn", F32, "d_wgu")
        dx1, dsh2, dsc2, grads["norm2_g"][i], dy, dg1 = _norm_mod_bwd(
            dgu, wgu[i], st["x1"], norm2_g[i:i + 1], st["sc2"], dx, "d_h2", y=st["y"], g=st["g1"])
        dy = dy.reshape(T, D)
        if i % 2 == 0:
            do = _matmul(dy, wo_a[j], "nt", BF16, "d_o_a").reshape(NB, S, QA)
            grads["wo_a"][j] = _matmul(st["o"], dy, "tn", F32, "d_wo_a")
            dq, dk, dv, dsink = _attn_a_bwd(st["qkn"], do, sinks_a[j])
            dqkv, dgain = _qk_prep_bwd(st["qkv"], dq.reshape(T, QA), dk.reshape(T, LANES), dv.reshape(T, LANES),
                                       tab_c, tab_s, st["gains"])
            w_in = wqkv_a[j]
            grads["wqkv_a"][j] = _matmul(st["h"], dqkv, "tn", F32, "d_wqkv_a")
            grads["q_norm_a"][j] = jnp.sum(dgain[:GROUP_A].reshape(2 * GROUP_A, HEAD_DIM), axis=0)
            grads["k_norm_a"][j] = jnp.sum(dgain[GROUP_A].reshape(2, HEAD_DIM), axis=0)
            grads["sinks_a"][j] = jnp.sum(dsink[..., 0], axis=0)
        else:
            do = _matmul(dy, wo_b[j], "nt", BF16, "d_o_b").reshape(NB, S, -1)
            grads["wo_b"][j] = _matmul(st["o"], dy, "tn", F32, "d_wo_b")
            dq, dk, dv = _attn_b_bwd(st["qkv"], do, rev, fwd)
            dqkv = jnp.concatenate([dq, dk, dv], axis=-1).reshape(T, -1).astype(BF16)
            w_in = wqkv_b[j]
            grads["wqkv_b"][j] = _matmul(st["h"], dqkv, "tn", F32, "d_wqkv_b")
        this_dg2 = dg2
        if i > 0:
            dx, dsh1, dsc1, grads["norm1_g"][i], dm, dg2 = _norm_mod_bwd(
                dqkv, w_in, st["x"], norm1_g[i:i + 1], st["sc1"], dx1, "d_h", y=saved[i - 1]["m"], g=saved[i - 1]["g2"])
        else:
            dx, dsh1, dsc1, grads["norm1_g"][i] = _norm_mod_bwd(
                dqkv, w_in, st["x"], norm1_g[i:i + 1], st["sc1"], dx1, "d_h")
        dmod[i] = jnp.concatenate([dsh1, dsc1, dg1, dsh2, dsc2, this_dg2], axis=-1).reshape(NB, 6 * D)

    matrices = ("wqkv_a", "wo_a", "wqkv_b", "wo_b", "wgu", "wd")
    grads = {name: parts if name in matrices else jnp.stack(parts) for name, parts in grads.items()}
    return loss, dx, grads, jnp.stack(dmod)


def _rows_of(flat, cols=PACK_COLS):
    n = flat.shape[0]
    pad = (-n) % (8 * cols)
    if pad:
        flat = jnp.concatenate([flat, jnp.zeros((pad,), flat.dtype)])
    return flat.reshape(-1, cols)


def kernel(x, c, positions, ada_w, ada_b, norm1_g, norm2_g, wqkv_a, q_norm_a, k_norm_a, sinks_a, wo_a, wqkv_b, wo_b, w_gate, w_up, w_down, loss_target, m_ada_w, m_ada_b, m_norm1_g, m_norm2_g, m_wqkv_a, m_q_norm_a, m_k_norm_a, m_sinks_a, m_wo_a, m_wqkv_b, m_wo_b, m_w_gate, m_w_up, m_w_down, v_ada_w, v_ada_b, v_norm1_g, v_norm2_g, v_wqkv_a, v_q_norm_a, v_k_norm_a, v_sinks_a, v_wo_a, v_wqkv_b, v_wo_b, v_w_gate, v_w_up, v_w_down):
    xi, yi, ci = lax.axis_index("x"), lax.axis_index("y"), lax.axis_index("c")
    dev = 4 * xi + 2 * yi + ci
    chip = 2 * xi + yi
    NB, S, D = x.shape
    B_all = N_DEV * NB
    L = ada_w.shape[0]
    n_mod = ada_w.shape[2] // 2

    c_all = _gather8(_rows_of(c.reshape(-1), LANES), "gather_c").reshape(N_DEV, -1)[:, :NB * D].reshape(B_all, D)
    ada_w_half = lax.dynamic_slice_in_dim(ada_w, ci * n_mod, n_mod, axis=2)
    ada_b_half = lax.dynamic_slice_in_dim(ada_b, dev * n_mod, n_mod, axis=1).reshape(L, 1, n_mod)
    mod_part = _ada_fwd(c_all, ada_w_half, ada_b_half)
    n_part = L * B_all * n_mod
    mod_all = _gather8(_rows_of(mod_part.reshape(-1)), "gather_mod").reshape(N_DEV, -1)[:, :n_part]
    mod_all = mod_all.reshape(N_DEV, L, B_all, n_mod).transpose(1, 2, 0, 3).reshape(L, B_all, N_DEV * n_mod)
    mod = lax.dynamic_slice_in_dim(mod_all, dev * NB, NB, axis=1)

    shards = dict(wqkv_a=wqkv_a, wo_a=wo_a, wqkv_b=wqkv_b, wo_b=wo_b, w_gate=w_gate, w_up=w_up, w_down=w_down)
    halves = []
    for name, _ in _SHARDED:
        w = shards[name]
        half = lax.dynamic_index_in_dim(w.reshape((2, w.shape[0] // 2) + w.shape[1:]), ci, 0, keepdims=False)
        halves.append(half.astype(BF16))
    gathered = _all_gather8(halves, "gather_weights", local_axis=1, local_chunks=8, relay_axis=1)
    full = {name: _unpack_full(t, axis) for (name, axis), t in zip(_SHARDED, gathered)}
    wgu = [_interleave(gate, up) for gate, up in zip(full["w_gate"], full["w_up"])]

    loss, grad_x, g, dmod = _local_step(
        x, positions, mod, norm1_g, norm2_g, q_norm_a, k_norm_a, sinks_a,
        full["wqkv_a"], full["wo_a"], full["wqkv_b"], full["wo_b"], wgu, full["w_down"], loss_target)

    g_full = dict(wqkv_a=g["wqkv_a"], wo_a=g["wo_a"], wqkv_b=g["wqkv_b"], wo_b=g["wo_b"],
                  w_gate=g["wgu"], w_up=g["wgu"], w_down=g["wd"])
    which = dict(w_gate=0, w_up=1)
    packed = [_pack_full(g_full[name], axis, which.get(name)) for name, axis in _SHARDED]
    from_cores = _exchange_cores(packed, "rs_cores", chunk_axis=0, chunks=4)
    chip_part = [_sum_slabs(p, ci, r, "rs_add_cores", with_bf16=True) for p, r in zip(packed, from_cores)]
    from_chips = _exchange_chips([b for _, b in chip_part], "rs_chips", relay_axis=1)
    mine = [_sum_slabs(p, chip, r, "rs_add_chips") for (p, _), r in zip(chip_part, from_chips)]
    theirs = _sibling_send(mine, "rs_halves")
    grad = {}
    for (name, _), m, t in zip(_SHARDED, mine, theirs):
        first, second = jnp.where(ci == 0, m, t), jnp.where(ci == 0, t, m)
        grad[name] = jnp.stack([first, second]).reshape(shards[name].shape)

    small_names = ("norm1_g", "norm2_g", "q_norm_a", "k_norm_a", "sinks_a")
    small = [dmod.reshape(-1)] + [g[name].reshape(-1) for name in small_names] + [loss.reshape(-1)]
    small_sizes = [t.shape[0] for t in small]
    small_rows = _rows_of(jnp.concatenate(small))
    small_all = _gather8(small_rows, "gather_small")
    small_sum = _sum_leading(small_all, "sum_small").reshape(-1)
    n_dmod = small_sizes[0]
    dmod_all = small_all.reshape(N_DEV, -1)[:, :n_dmod].reshape(N_DEV, L, NB, 6 * D)
    dmod_all = dmod_all.transpose(1, 0, 2, 3).reshape(L, B_all, 6 * D)
    off = n_dmod
    for name, sz in zip(small_names + ("loss",), small_sizes[1:]):
        grad[name] = small_sum[off:off + sz]
        off += sz
    loss_total = grad.pop("loss").reshape(())
    for name, ref in (("norm1_g", norm1_g), ("norm2_g", norm2_g), ("q_norm_a", q_norm_a),
                      ("k_norm_a", k_norm_a), ("sinks_a", sinks_a)):
        grad[name] = grad[name].reshape(ref.shape)

    n_shard = ada_w.shape[2]
    dmod_shard = lax.dynamic_slice_in_dim(dmod_all, chip * n_shard, n_shard, axis=2)
    grad["ada_w"], gb = _ada_bwd(c_all, dmod_all, dmod_shard)
    grad["ada_b"] = gb.reshape(ada_b.shape)

    weights = dict(ada_w=ada_w, ada_b=ada_b, norm1_g=norm1_g, norm2_g=norm2_g, wqkv_a=wqkv_a, q_norm_a=q_norm_a,
                   k_norm_a=k_norm_a, sinks_a=sinks_a, wo_a=wo_a, wqkv_b=wqkv_b, wo_b=wo_b, w_gate=w_gate,
                   w_up=w_up, w_down=w_down)
    m_in = dict(ada_w=m_ada_w, ada_b=m_ada_b, norm1_g=m_norm1_g, norm2_g=m_norm2_g, wqkv_a=m_wqkv_a,
                q_norm_a=m_q_norm_a, k_norm_a=m_k_norm_a, sinks_a=m_sinks_a, wo_a=m_wo_a, wqkv_b=m_wqkv_b,
                wo_b=m_wo_b, w_gate=m_w_gate, w_up=m_w_up, w_down=m_w_down)
    v_in = dict(ada_w=v_ada_w, ada_b=v_ada_b, norm1_g=v_norm1_g, norm2_g=v_norm2_g, wqkv_a=v_wqkv_a,
                q_norm_a=v_q_norm_a, k_norm_a=v_k_norm_a, sinks_a=v_sinks_a, wo_a=v_wo_a, wqkv_b=v_wqkv_b,
                wo_b=v_wo_b, w_gate=v_w_gate, w_up=v_w_up, w_down=v_w_down)
    names = list(weights)
    delta, new_m, new_v = {}, {}, {}
    for name in names:
        delta[name], new_m[name], new_v[name] = _adamw(weights[name], grad[name], m_in[name], v_in[name],
                                                       "adamw_" + name)
    return (loss_total, grad_x, *[grad[k] for k in names], *[delta[k] for k in names],
            *[new_m[k] for k in names], *[new_v[k] for k in names])
```

```python
import jax
import jax.numpy as jnp
from jax import lax
from jax.experimental import pallas as pl
from jax.experimental.pallas import tpu as pltpu

F32 = jnp.float32
BF16 = jnp.bfloat16

DEPTH = 4
HEAD_DIM = 64
N_Q_A = 16
N_KV_A = 2
GROUP_A = N_Q_A // N_KV_A
N_H_B = 16
BLOCK = 128
ROT_DIM = HEAD_DIM // 4
ROPE_THETA = 500000.0
EPS = 1e-6
ATTN_SCALE = HEAD_DIM ** -0.5
NEG_BIG = -1e30

ADAM_LR = 0.001
ADAM_B1 = 0.9
ADAM_B2 = 0.999
ADAM_EPS = 1e-08
ADAM_WD = 0.01
ADAM_STEP = 10

N_DEV = 8
LANES = 128
PACK_COLS = 1024
VMEM_LIMIT_BYTES = 48 * 1024 * 1024
MESH = pl.DeviceIdType.MESH

_NT = (((1,), (1,)), ((), ()))
_TN = (((0,), (0,)), ((), ()))
_NN = (((1,), (0,)), ((), ()))


def _params(sem=None, vmem_limit_bytes=VMEM_LIMIT_BYTES):
    return pltpu.CompilerParams(vmem_limit_bytes=vmem_limit_bytes, dimension_semantics=sem)


def _pick(n, cap, mult):
    best = None
    for t in range(mult, min(n, cap) + 1, mult):
        if n % t == 0:
            best = t
    return n if best is None else best


_ANY = pl.BlockSpec(memory_space=pl.ANY)


def _window(index, axis, q, n, shape):
    rest = [slice(None)] * len(shape)
    size = shape[axis] // n
    rest[axis] = pl.ds(q * size, size)
    return tuple(index) + tuple(rest)


def _all_gather8(xs, name, local_axis=0, local_chunks=1, relay_axis=None):
    n = len(xs)
    n_sems = 7 if relay_axis is None else 9

    def body(*refs):
        x_refs, out_refs = refs[:n], refs[n:2 * n]
        send_sems, recv_sems, local_sems = refs[2 * n:]
        xi, yi, ci = lax.axis_index("x"), lax.axis_index("y"), lax.axis_index("c")
        me, sibling = (xi, yi, ci), (xi, yi, 1 - ci)
        chips = [(1 - xi, yi), (xi, 1 - yi), (1 - xi, 1 - yi)]

        def slab(w, px, py, pc):
            return out_refs[w].at[4 * px + 2 * py + pc]

        def copy(w, k, block, to, src=None):
            return pltpu.make_async_remote_copy(
                src_ref=slab(w, *block) if src is None else src, dst_ref=slab(w, *block),
                send_sem=send_sems.at[k, w], recv_sem=recv_sems.at[k, w], device_id=to, device_id_type=MESH)

        mine = []
        for w in range(n):
            for q in range(local_chunks):
                part = _window((), local_axis, q, local_chunks, xs[w].shape)
                mine.append(pltpu.make_async_copy(x_refs[w].at[part], slab(w, *me).at[part], local_sems.at[w, q]))
                mine[-1].start()
        direct = chips if relay_axis is None else chips[:2]
        first = [copy(w, 0, me, sibling, src=x_refs[w]) for w in range(n)]
        first += [copy(w, 1 + j, me, (*chip, ci), src=x_refs[w]) for j, chip in enumerate(direct) for w in range(n)]
        for cp in first:
            cp.start()

        def relay(w, part, block, to):
            piece = _window((), relay_axis, part, 2, xs[w].shape)
            return pltpu.make_async_remote_copy(
                src_ref=slab(w, *block).at[piece], dst_ref=slab(w, *block).at[piece],
                send_sem=send_sems.at[7 + part, w], recv_sem=recv_sems.at[7 + part, w],
                device_id=to, device_id_type=MESH)

        passed = []
        for j, chip in enumerate(direct):
            for w in range(n):
                copy(w, 1 + j, (*chip, ci), me).wait_recv()
                passed.append(copy(w, 4 + j, (*chip, ci), sibling))
                passed[-1].start()
                if relay_axis is not None:
                    passed.append(relay(w, j, (*chip, ci), (*chips[1 - j], ci)))
                    passed[-1].start()
        if relay_axis is not None:
            for w in range(n):
                for part in range(2):
                    relay(w, part, (*chips[2], ci), me).wait_recv()
                passed.append(copy(w, 6, (*chips[2], ci), sibling))
                passed[-1].start()
        for w in range(n):
            copy(w, 0, sibling, me).wait_recv()
        for j, chip in enumerate(chips):
            for w in range(n):
                copy(w, 4 + j, (*chip, 1 - ci), me).wait_recv()
        for cp in first + passed:
            cp.wait_send()
        for cp in mine:
            cp.wait()

    return pl.pallas_call(
        body, name=name,
        out_shape=[jax.ShapeDtypeStruct((N_DEV,) + x.shape, x.dtype) for x in xs],
        in_specs=[_ANY] * n, out_specs=[_ANY] * n,
        scratch_shapes=[pltpu.SemaphoreType.DMA((n_sems, n)), pltpu.SemaphoreType.DMA((n_sems, n)),
                        pltpu.SemaphoreType.DMA((n, local_chunks))],
    )(*xs)


def _exchange_cores(xs, name, chunk_axis=0, chunks=1):
    n = len(xs)
    n_peers = 1

    def body(*refs):
        x_refs, out_refs = refs[:n], refs[n:2 * n]
        send_sems, recv_sems = refs[2 * n:]
        xi, yi, ci = lax.axis_index("x"), lax.axis_index("y"), lax.axis_index("c")
        peers = [(1 - ci, (xi, yi, 1 - ci))]
        copies = []
        for k, (p, dev) in enumerate(peers):
            for w in range(n):
                slab_shape = xs[w].shape[1:]
                for q in range(chunks):
                    copies.append(pltpu.make_async_remote_copy(
                        src_ref=x_refs[w].at[_window((p,), chunk_axis, q, chunks, slab_shape)],
                        dst_ref=out_refs[w].at[_window((k,), chunk_axis, q, chunks, slab_shape)],
                        send_sem=send_sems.at[k, w, q], recv_sem=recv_sems.at[k, w, q],
                        device_id=dev, device_id_type=MESH))
                    copies[-1].start()
        for cp in copies:
            cp.wait()

    return pl.pallas_call(
        body, name=name,
        out_shape=[jax.ShapeDtypeStruct((n_peers,) + x.shape[1:], x.dtype) for x in xs],
        in_specs=[_ANY] * n, out_specs=[_ANY] * n,
        scratch_shapes=[pltpu.SemaphoreType.DMA((n_peers, n, chunks)), pltpu.SemaphoreType.DMA((n_peers, n, chunks))],
    )(*xs)


def _exchange_chips(xs, name, relay_axis):
    n = len(xs)

    def half_shape(x):
        shape = list(x.shape[1:])
        shape[relay_axis] //= 2
        return tuple(shape)

    def body(*refs):
        x_refs, out_refs, hop_refs = refs[:n], refs[n:2 * n], refs[2 * n:3 * n]
        send_sems, recv_sems = refs[3 * n:]
        xi, yi, ci = lax.axis_index("x"), lax.axis_index("y"), lax.axis_index("c")
        nbr = [(1 - xi, yi, ci), (xi, 1 - yi, ci)]
        slab_of_nbr = [2 * (1 - xi) + yi, 2 * xi + (1 - yi)]
        slab_of_diag = 2 * (1 - xi) + (1 - yi)

        def copy(k, w, src, dst, to):
            return pltpu.make_async_remote_copy(src_ref=src, dst_ref=dst, send_sem=send_sems.at[k, w],
                                                recv_sem=recv_sems.at[k, w], device_id=to, device_id_type=MESH)

        def piece(w, part):
            return _window((), relay_axis, part, 2, xs[w].shape[1:])

        sent = []
        for w in range(n):
            for j in range(2):
                sent.append(copy(j, w, x_refs[w].at[slab_of_nbr[j]], out_refs[w].at[j], nbr[j]))
                sent.append(copy(2 + j, w, x_refs[w].at[(slab_of_diag,) + piece(w, j)], hop_refs[w].at[j], nbr[j]))
        for cp in sent:
            cp.start()
        for w in range(n):
            for j in range(2):
                copy(2 + j, w, hop_refs[w].at[j], hop_refs[w].at[j], nbr[j]).wait_recv()
                sent.append(copy(4 + j, w, hop_refs[w].at[j], out_refs[w].at[(2,) + piece(w, j)], nbr[1 - j]))
                sent[-1].start()
        for w in range(n):
            for j in range(2):
                copy(j, w, out_refs[w].at[j], out_refs[w].at[j], nbr[j]).wait_recv()
                half = out_refs[w].at[(2,) + piece(w, j)]
                copy(4 + j, w, half, half, nbr[1 - j]).wait_recv()
        for cp in sent:
            cp.wait_send()

    out = pl.pallas_call(
        body, name=name,
        out_shape=[jax.ShapeDtypeStruct((3,) + x.shape[1:], x.dtype) for x in xs]
        + [jax.ShapeDtypeStruct((2,) + half_shape(x), x.dtype) for x in xs],
        in_specs=[_ANY] * n, out_specs=[_ANY] * (2 * n),
        scratch_shapes=[pltpu.SemaphoreType.DMA((6, n)), pltpu.SemaphoreType.DMA((6, n))],
    )(*xs)
    return out[:n]


def _sibling_send(xs, name, chunk_axis=1, chunks=4):
    n = len(xs)

    def body(*refs):
        x_refs, out_refs = refs[:n], refs[n:2 * n]
        send_sems, recv_sems = refs[2 * n:]
        xi, yi, ci = lax.axis_index("x"), lax.axis_index("y"), lax.axis_index("c")
        copies = []
        for w in range(n):
            for q in range(chunks):
                part = _window((), chunk_axis, q, chunks, xs[w].shape)
                copies.append(pltpu.make_async_remote_copy(
                    src_ref=x_refs[w].at[part], dst_ref=out_refs[w].at[part],
                    send_sem=send_sems.at[w, q], recv_sem=recv_sems.at[w, q],
                    device_id=(xi, yi, 1 - ci), device_id_type=MESH))
                copies[-1].start()
        for cp in copies:
            cp.wait()

    return pl.pallas_call(
        body, name=name,
        out_shape=[jax.ShapeDtypeStruct(x.shape, x.dtype) for x in xs],
        in_specs=[_ANY] * n, out_specs=[_ANY] * n,
        scratch_shapes=[pltpu.SemaphoreType.DMA((n, chunks)), pltpu.SemaphoreType.DMA((n, chunks))],
    )(*xs)


def _sum_leading(x, name, own=None, with_bf16=False):
    P, R, C = x.shape
    tr = _pick(R, max(16, (1 << 19) // (C * (P + 1))), 16)

    def body(*refs):
        n_in = 1 if own is None else 2
        x_ref = refs[n_in - 1]
        acc = x_ref[0].astype(F32) if own is None else refs[0][...] + x_ref[0].astype(F32)
        for p in range(1, P):
            acc = acc + x_ref[p].astype(F32)
        refs[n_in][...] = acc
        if with_bf16:
            refs[n_in + 1][...] = acc.astype(BF16)

    flat = pl.BlockSpec((tr, C), lambda r: (r, 0))
    slabs = pl.BlockSpec((P, tr, C), lambda r: (0, r, 0))
    out = pl.pallas_call(
        body, name=name, grid=(R // tr,),
        in_specs=[slabs] if own is None else [flat, slabs],
        out_specs=[flat, flat] if with_bf16 else [flat],
        out_shape=[jax.ShapeDtypeStruct((R, C), F32)] + ([jax.ShapeDtypeStruct((R, C), BF16)] if with_bf16 else []),
        compiler_params=_params(("arbitrary",)),
    )(*([x] if own is None else [own, x]))
    return out if with_bf16 else out[0]


MATMUL_SINGLE_K = 1280
MATMUL_VMEM_BUDGET = 36 * 1024 * 1024


def _matmul(a, b, mode, out_dtype, name, swiglu=False):
    if mode == "nn":
        (M, K), N = a.shape, b.shape[1]
    elif mode == "nt":
        (M, K), N = a.shape, b.shape[0]
    else:
        (K, M), N = a.shape, b.shape[1]
    tm = _pick(M, 1024 if mode != "tn" else 1536, 128)
    tn = _pick(N, 1536, 128)
    if swiglu:
        tm, tn = _pick(M, 1024 if out_dtype == BF16 else 512, 128), 2 * _ff_tile(N // 2)
    out_bytes = jnp.dtype(out_dtype).itemsize
    tk = K
    if K > MATMUL_SINGLE_K:
        for cap in (2048, 1024, 512):
            tk = _pick(K, cap, 128)
            blocks = 2 * 2 * tk * (tm + tn) + tm * tn * (2 * out_bytes + (4 if out_dtype != F32 else 0))
            if blocks <= MATMUL_VMEM_BUDGET:
                break
    nk = K // tk
    dims = {"nn": _NN, "nt": _NT, "tn": _TN}[mode]
    use_scratch = nk > 1 and out_dtype != F32

    def body(a_ref, b_ref, *refs):
        o_ref = refs[0]

        def product():
            return lax.dot_general(a_ref[...].astype(BF16), b_ref[...].astype(BF16), dims,
                                   preferred_element_type=F32)

        if nk == 1:
            part = product()
            o_ref[...] = part.astype(o_ref.dtype)
            if swiglu:
                g = part[:, :tn // 2]
                refs[1][...] = (g * _sigmoid(g) * part[:, tn // 2:]).astype(BF16)
            return
        k = pl.program_id(2)
        acc_ref = refs[-1] if use_scratch else o_ref

        @pl.when(k == 0)
        def _():
            acc_ref[...] = jnp.zeros_like(acc_ref)

        acc_ref[...] += product()

        if use_scratch:
            @pl.when(k == nk - 1)
            def _():
                o_ref[...] = acc_ref[...].astype(o_ref.dtype)

    if mode == "tn":
        a_spec = pl.BlockSpec((tk, tm), lambda i, j, k: (k, i))
    else:
        a_spec = pl.BlockSpec((tm, tk), lambda i, j, k: (i, k))
    if mode == "nt":
        b_spec = pl.BlockSpec((tn, tk), lambda i, j, k: (j, k))
    else:
        b_spec = pl.BlockSpec((tk, tn), lambda i, j, k: (k, j))
    out_specs = [pl.BlockSpec((tm, tn), lambda i, j, k: (i, j))]
    out_shape = [jax.ShapeDtypeStruct((M, N), out_dtype)]
    if swiglu:
        assert nk == 1 and mode == "nn"
        out_specs.append(pl.BlockSpec((tm, tn // 2), lambda i, j, k: (i, j)))
        out_shape.append(jax.ShapeDtypeStruct((M, N // 2), BF16))
    out = pl.pallas_call(
        body, name=name, grid=(M // tm, N // tn, nk),
        in_specs=[a_spec, b_spec], out_specs=out_specs, out_shape=out_shape,
        scratch_shapes=[pltpu.VMEM((tm, tn), F32)] if use_scratch else [],
        compiler_params=_params(("parallel", "parallel", "arbitrary")),
    )(a, b)
    return out if swiglu else out[0]


def _row_tile(S):
    return _pick(S, 512, 8)


def _norm_mod_fwd(x, gain, sc, sh):
    NB, S, D = x.shape
    tr = _row_tile(S)

    def body(x_ref, g_ref, sc_ref, sh_ref, h_ref):
        xv = x_ref[...]
        ms = jnp.mean(xv * xv, axis=-1, keepdims=True)
        n = xv * lax.rsqrt(ms + EPS) * g_ref[...]
        h_ref[...] = (n * (1.0 + sc_ref[...]) + sh_ref[...]).astype(BF16)

    tok = pl.BlockSpec((None, tr, D), lambda b, r: (b, r, 0))
    per_ex = pl.BlockSpec((None, 1, D), lambda b, r: (b, 0, 0))
    return pl.pallas_call(
        body, name="norm_mod_fwd", grid=(NB, S // tr),
        in_specs=[tok, pl.BlockSpec((1, D), lambda b, r: (0, 0)), per_ex, per_ex],
        out_specs=tok, out_shape=jax.ShapeDtypeStruct((NB, S, D), BF16),
        compiler_params=_params(("parallel", "parallel")),
    )(x, gain, sc, sh)


def _norm_mod_bwd(a, w, x, gain, sc, dres, name, y=None, g=None):
    NB, S, D = x.shape
    T, K = a.shape
    tm = _pick(S, 512, 128)
    per_ex_tiles = S // tm
    tk = K if K <= MATMUL_SINGLE_K else _pick(K, 1536, 128)
    nk = K // tk
    gated = y is not None

    def body(*refs):
        a_ref, w_ref, x_ref, g_ref, sc_ref, dres_ref = refs[:6]
        refs = refs[6:]
        if gated:
            y_ref, gate_ref = refs[:2]
            refs = refs[2:]
        dx_ref, dsh_ref, dsc_ref, dgain_ref = refs[:4]
        acc_ref = refs[-1]
        i, k = pl.program_id(0), pl.program_id(1)

        @pl.when(k == 0)
        def _():
            acc_ref[...] = jnp.zeros_like(acc_ref)

        acc_ref[...] += lax.dot_general(a_ref[...], w_ref[...], _NT, preferred_element_type=F32)

        @pl.when(k == nk - 1)
        def _():
            first_of_example = i % per_ex_tiles == 0

            @pl.when(first_of_example)
            def _():
                dsh_ref[...] = jnp.zeros_like(dsh_ref)
                dsc_ref[...] = jnp.zeros_like(dsc_ref)
                if gated:
                    refs[5][...] = jnp.zeros_like(refs[5])

            @pl.when(i == 0)
            def _():
                dgain_ref[...] = jnp.zeros_like(dgain_ref)

            xv = x_ref[...]
            rstd = lax.rsqrt(jnp.mean(xv * xv, axis=-1, keepdims=True) + EPS)
            xh = xv * rstd
            gn = g_ref[...]
            dh = acc_ref[...]
            dsh_ref[...] += jnp.sum(dh, axis=0, keepdims=True)
            dsc_ref[...] += jnp.sum(dh * (xh * gn), axis=0, keepdims=True)
            dn = dh * (1.0 + sc_ref[...])
            dgain_ref[...] += jnp.sum(dn * xh, axis=0, keepdims=True)
            dxh = dn * gn
            proj = jnp.mean(dxh * xh, axis=-1, keepdims=True)
            dx = rstd * (dxh - xh * proj) + dres_ref[...]
            dx_ref[...] = dx
            if gated:
                refs[4][...] = (dx * gate_ref[...]).astype(BF16)
                refs[5][...] += jnp.sum(dx * y_ref[...], axis=0, keepdims=True)

    tok = pl.BlockSpec((None, tm, D), lambda i, k: (i // per_ex_tiles, i % per_ex_tiles, 0))
    per_ex = pl.BlockSpec((None, 1, D), lambda i, k: (i // per_ex_tiles, 0, 0))
    row = pl.BlockSpec((1, D), lambda i, k: (0, 0))
    in_specs = [pl.BlockSpec((tm, tk), lambda i, k: (i, k)), pl.BlockSpec((D, tk), lambda i, k: (0, k)),
                tok, row, per_ex, tok]
    out_specs = [tok, per_ex, per_ex, row]
    out_shape = [jax.ShapeDtypeStruct((NB, S, D), F32), jax.ShapeDtypeStruct((NB, 1, D), F32),
                 jax.ShapeDtypeStruct((NB, 1, D), F32), jax.ShapeDtypeStruct((1, D), F32)]
    operands = [a, w, x, gain, sc, dres]
    if gated:
        in_specs += [tok, per_ex]
        out_specs += [tok, per_ex]
        out_shape += [jax.ShapeDtypeStruct((NB, S, D), BF16), jax.ShapeDtypeStruct((NB, 1, D), F32)]
        operands += [y, g]
    return pl.pallas_call(
        body, name=name, grid=(T // tm, nk), in_specs=in_specs, out_specs=out_specs, out_shape=out_shape,
        scratch_shapes=[pltpu.VMEM((tm, D), F32)],
        compiler_params=_params(("arbitrary", "arbitrary")),
    )(*operands)


def _gate_res(x, y, g, norm=None):
    NB, S, D = x.shape
    tr = _row_tile(S)

    def body(x_ref, y_ref, g_ref, *refs):
        xo = x_ref[...] + g_ref[...] * y_ref[...]
        refs[-1 if norm is None else -2][...] = xo
        if norm is not None:
            gain_ref, sc_ref, sh_ref, _, h_ref = refs
            n = xo * lax.rsqrt(jnp.mean(xo * xo, axis=-1, keepdims=True) + EPS) * gain_ref[...]
            h_ref[...] = (n * (1.0 + sc_ref[...]) + sh_ref[...]).astype(BF16)

    tok = pl.BlockSpec((None, tr, D), lambda b, r: (b, r, 0))
    per_ex = pl.BlockSpec((None, 1, D), lambda b, r: (b, 0, 0))
    in_specs, out_specs, operands = [tok, tok, per_ex], [tok], [x, y, g]
    out_shape = [jax.ShapeDtypeStruct((NB, S, D), F32)]
    if norm is not None:
        in_specs += [pl.BlockSpec((1, D), lambda b, r: (0, 0)), per_ex, per_ex]
        out_specs.append(tok)
        out_shape.append(jax.ShapeDtypeStruct((NB, S, D), BF16))
        operands += list(norm)
    out = pl.pallas_call(
        body, name="gate_res", grid=(NB, S // tr), in_specs=in_specs, out_specs=out_specs, out_shape=out_shape,
        compiler_params=_params(("parallel", "parallel")),
    )(*operands)
    return out[0] if norm is None else out


def _gate_res_bwd(dxo, y, g):
    NB, S, D = dxo.shape
    tr = _row_tile(S)

    def body(d_ref, y_ref, g_ref, dy_ref, dg_ref):
        @pl.when(pl.program_id(1) == 0)
        def _():
            dg_ref[...] = jnp.zeros_like(dg_ref)

        d = d_ref[...]
        dy_ref[...] = (d * g_ref[...]).astype(BF16)
        dg_ref[...] += jnp.sum(d * y_ref[...], axis=0, keepdims=True)

    tok = pl.BlockSpec((None, tr, D), lambda b, r: (b, r, 0))
    per_ex = pl.BlockSpec((None, 1, D), lambda b, r: (b, 0, 0))
    return pl.pallas_call(
        body, name="gate_res_bwd", grid=(NB, S // tr), in_specs=[tok, tok, per_ex], out_specs=[tok, per_ex],
        out_shape=[jax.ShapeDtypeStruct((NB, S, D), BF16), jax.ShapeDtypeStruct((NB, 1, D), F32)],
        compiler_params=_params(("arbitrary", "arbitrary")),
    )(dxo, y, g)


def _sigmoid(v):
    return 1.0 / (1.0 + jnp.exp(-v))


def _ff_tile(F):
    return _pick(F, 1536, 128)


def _interleave(gate, up):
    F = gate.shape[-1]
    tf = _ff_tile(F)
    parts = []
    for j in range(F // tf):
        parts += [gate[..., j * tf:(j + 1) * tf], up[..., j * tf:(j + 1) * tf]]
    return jnp.concatenate(parts, axis=-1)


def _swiglu_bwd(dm, wd, gu):
    T, D = dm.shape
    F = wd.shape[0]
    tf = _ff_tile(F)
    tm = _pick(T, 1024, 128)
    assert D <= MATMUL_SINGLE_K

    def body(a_ref, b_ref, gu_ref, o_ref):
        d = lax.dot_general(a_ref[...], b_ref[...], _NT, preferred_element_type=F32)
        g, u = gu_ref[:, :tf].astype(F32), gu_ref[:, tf:].astype(F32)
        s = _sigmoid(g)
        o_ref[:, :tf] = (d * u * (s * (1.0 + g * (1.0 - s)))).astype(BF16)
        o_ref[:, tf:] = (d * (g * s)).astype(BF16)

    return pl.pallas_call(
        body, name="swiglu_bwd", grid=(T // tm, F // tf),
        in_specs=[pl.BlockSpec((tm, D), lambda i, j: (i, 0)), pl.BlockSpec((tf, D), lambda i, j: (j, 0)),
                  pl.BlockSpec((tm, 2 * tf), lambda i, j: (i, j))],
        out_specs=pl.BlockSpec((tm, 2 * tf), lambda i, j: (i, j)),
        out_shape=jax.ShapeDtypeStruct((T, 2 * F), BF16),
        compiler_params=_params(("parallel", "parallel")),
    )(dm, wd, gu)


def _loss_fwd_bwd(y, target):
    NB, S, D = y.shape
    tr = _row_tile(S)

    def body(y_ref, t_ref, l_ref, d_ref):
        @pl.when((pl.program_id(0) == 0) & (pl.program_id(1) == 0))
        def _():
            l_ref[...] = jnp.zeros_like(l_ref)

        e = y_ref[...] - t_ref[...]
        d_ref[...] = e / D
        l_ref[...] += 0.5 * jnp.sum(jnp.mean(e * e, axis=-1, keepdims=True), axis=0, keepdims=True)

    tok = pl.BlockSpec((None, tr, D), lambda b, r: (b, r, 0))
    return pl.pallas_call(
        body, name="loss", grid=(NB, S // tr), in_specs=[tok, tok],
        out_specs=[pl.BlockSpec((1, 1), lambda b, r: (0, 0)), tok],
        out_shape=[jax.ShapeDtypeStruct((1, 1), F32), jax.ShapeDtypeStruct((NB, S, D), F32)],
        compiler_params=_params(("arbitrary", "arbitrary")),
    )(y, target)


def _half_sums(v, lo):
    sa = jnp.sum(jnp.where(lo, v, 0.0), axis=-1, keepdims=True)
    sb = jnp.sum(jnp.where(lo, 0.0, v), axis=-1, keepdims=True)
    return jnp.where(lo, sa, sb)


def _rope_swap(v, lane64):
    up = pltpu.roll(v, LANES - ROT_DIM // 2, 1)
    down = pltpu.roll(v, ROT_DIM // 2, 1)
    return jnp.where(lane64 < ROT_DIM // 2, up, jnp.where(lane64 < ROT_DIM, down, 0.0))


def _qk_prep_fwd(qkv, tab_c, tab_s, gains):
    T, W = qkv.shape
    R = W // LANES
    tt = _pick(T, 256, 8)

    def body(x_ref, c_ref, s_ref, g_ref, o_ref):
        lane = lax.broadcasted_iota(jnp.int32, (tt, LANES), 1)
        lo = lane < HEAD_DIM
        lane64 = lane & (HEAD_DIM - 1)
        c, s = c_ref[...], s_ref[...]
        for j in range(R - 1):
            cols = slice(j * LANES, (j + 1) * LANES)
            xv = x_ref[:, cols]
            rstd = lax.rsqrt(_half_sums(xv * xv, lo) / HEAD_DIM + EPS)
            yn = xv * rstd * g_ref[j:j + 1, :]
            o_ref[:, cols] = (yn * c + _rope_swap(yn, lane64) * s).astype(BF16)
        o_ref[:, (R - 1) * LANES:] = x_ref[:, (R - 1) * LANES:].astype(BF16)

    tok = pl.BlockSpec((tt, W), lambda t: (t, 0))
    tab = pl.BlockSpec((tt, LANES), lambda t: (t, 0))
    return pl.pallas_call(
        body, name="qk_prep_fwd", grid=(T // tt,),
        in_specs=[tok, tab, tab, pl.BlockSpec((R, LANES), lambda t: (0, 0))],
        out_specs=tok, out_shape=jax.ShapeDtypeStruct((T, W), BF16),
        compiler_params=_params(("parallel",)),
    )(qkv, tab_c, tab_s, gains)


def _qk_prep_bwd(qkv, dq, dk, dv, tab_c, tab_s, gains):
    T, W = qkv.shape
    R = W // LANES
    QW = dq.shape[1]
    tt = _pick(T, 256, 8)

    def body(x_ref, dq_ref, dk_ref, dv_ref, c_ref, s_ref, g_ref, o_ref, dg_ref):
        @pl.when(pl.program_id(0) == 0)
        def _():
            dg_ref[...] = jnp.zeros_like(dg_ref)

        lane = lax.broadcasted_iota(jnp.int32, (tt, LANES), 1)
        lo = lane < HEAD_DIM
        lane64 = lane & (HEAD_DIM - 1)
        c, s = c_ref[...], s_ref[...]
        for j in range(R - 1):
            cols = slice(j * LANES, (j + 1) * LANES)
            xv = x_ref[:, cols]
            d = dq_ref[:, cols] if j < R - 2 else dk_ref[...]
            rstd = lax.rsqrt(_half_sums(xv * xv, lo) / HEAD_DIM + EPS)
            xh = xv * rstd
            dyn = d * c + _rope_swap(d * s, lane64)
            dg_ref[j:j + 1, :] += jnp.sum(dyn * xh, axis=0, keepdims=True)
            dxh = dyn * g_ref[j:j + 1, :]
            proj = _half_sums(dxh * xh, lo) / HEAD_DIM
            o_ref[:, cols] = (rstd * (dxh - xh * proj)).astype(BF16)
        o_ref[:, (R - 1) * LANES:] = dv_ref[...].astype(BF16)

    tok = pl.BlockSpec((tt, W), lambda t: (t, 0))
    tab = pl.BlockSpec((tt, LANES), lambda t: (t, 0))
    gsp = pl.BlockSpec((R, LANES), lambda t: (0, 0))
    return pl.pallas_call(
        body, name="qk_prep_bwd", grid=(T // tt,),
        in_specs=[tok, pl.BlockSpec((tt, QW), lambda t: (t, 0)), tab, tab, tab, tab, gsp], out_specs=[tok, gsp],
        out_shape=[jax.ShapeDtypeStruct((T, W), BF16), jax.ShapeDtypeStruct((R, LANES), F32)],
        compiler_params=_params(("arbitrary",)),
    )(qkv, dq, dk, dv, tab_c, tab_s, gains)


def _band_mask(i):
    r = lax.broadcasted_iota(jnp.int32, (2 * BLOCK, 2 * BLOCK), 0) & (BLOCK - 1)
    c = lax.broadcasted_iota(jnp.int32, (2 * BLOCK, 2 * BLOCK), 1)
    rel = r + BLOCK - c
    return (rel >= 0) & (rel < BLOCK) & ((c >= BLOCK) | (i > 0))


def _swa_softmax(s, valid, sink):
    s = jnp.where(valid, s * ATTN_SCALE, NEG_BIG)
    m = jnp.maximum(jnp.max(s, axis=1, keepdims=True), sink)
    p = jnp.exp(s - m)
    ps = jnp.exp(sink - m)
    denom = jnp.sum(p, axis=1, keepdims=True) + ps
    return p / denom, ps / denom


A_GROUP = 4


Q_WIDTH_A = N_Q_A * HEAD_DIM
N_PAIR_A = Q_WIDTH_A // LANES


def _swa_specs():
    qs = pl.BlockSpec((None, BLOCK, Q_WIDTH_A), lambda b, i: (b, i, 0))

    def kv(col, back):
        return pl.BlockSpec((None, BLOCK, LANES), lambda b, i: (b, jnp.maximum(i - back, 0), col))

    return qs, kv(N_PAIR_A, 1), kv(N_PAIR_A, 0), kv(N_PAIR_A + 1, 1), kv(N_PAIR_A + 1, 0)


def _dup_heads(t):
    lo = lax.broadcasted_iota(jnp.int32, t.shape, 1) < HEAD_DIM
    sw = pltpu.roll(t.astype(F32), HEAD_DIM, 1).astype(BF16)
    return jnp.where(lo, t, sw), jnp.where(lo, sw, t)


def _kv_tiles(kp_ref, kc_ref, vp_ref, vc_ref):
    kd = _dup_heads(jnp.concatenate([kp_ref[...], kc_ref[...]], axis=0))
    vd = _dup_heads(jnp.concatenate([vp_ref[...], vc_ref[...]], axis=0))
    return kd, vd


def _attn_a_fwd(qkn, sinks):
    NB, S, _ = qkn.shape
    qs, kp, kc, vp, vc = _swa_specs()

    def body(q_ref, kp_ref, kc_ref, vp_ref, vc_ref, sink_ref, o_ref):
        i = pl.program_id(1)
        kd, vd = _kv_tiles(kp_ref, kc_ref, vp_ref, vc_ref)
        valid = _band_mask(i)
        lo = lax.broadcasted_iota(jnp.int32, (BLOCK, LANES), 1) < HEAD_DIM
        top = lax.broadcasted_iota(jnp.int32, (2 * BLOCK, 1), 0) < BLOCK
        for first in range(0, N_PAIR_A, A_GROUP):
            pairs = range(first, first + A_GROUP)
            qs_ = [jnp.concatenate(_head_halves(q_ref[:, p * LANES:(p + 1) * LANES], lo), axis=0) for p in pairs]
            ss = [lax.dot_general(q, kd[2 * p // GROUP_A], _NT, preferred_element_type=F32) for q, p in zip(qs_, pairs)]
            pns = [_swa_softmax(s, valid, jnp.where(top, sink_ref[2 * p], sink_ref[2 * p + 1]))[0]
                   for s, p in zip(ss, pairs)]
            pvs = [jnp.dot(pn.astype(BF16), vd[2 * p // GROUP_A], preferred_element_type=F32) for pn, p in zip(pns, pairs)]
            for pv, p in zip(pvs, pairs):
                o_ref[:, p * LANES:(p + 1) * LANES] = jnp.where(lo, pv[:BLOCK], pv[BLOCK:]).astype(BF16)

    return pl.pallas_call(
        body, name="attn_a_fwd", grid=(NB, S // BLOCK),
        in_specs=[qs, kp, kc, vp, vc, pl.BlockSpec(memory_space=pltpu.SMEM)],
        out_specs=qs, out_shape=jax.ShapeDtypeStruct((NB, S, Q_WIDTH_A), BF16),
        compiler_params=_params(("parallel", "arbitrary")),
    )(qkn, qkn, qkn, qkn, qkn, sinks)


def _attn_a_bwd(qkn, do, sinks):
    NB, S, _ = qkn.shape
    qs, kp, kc, vp, vc = _swa_specs()
    full = pl.BlockSpec((None, S, LANES), lambda b, i: (b, 0, 0))
    sink_out = pl.BlockSpec((None, N_Q_A, LANES), lambda b, i: (b, 0, 0))

    def body(q_ref, do_ref, kp_ref, kc_ref, vp_ref, vc_ref, sink_ref, dq_ref, dk_ref, dv_ref, ds_ref, dk_s, dv_s):
        i = pl.program_id(1)

        @pl.when(i == 0)
        def _():
            dk_ref[...] = jnp.zeros_like(dk_ref)
            dv_ref[...] = jnp.zeros_like(dv_ref)
            ds_ref[...] = jnp.zeros_like(ds_ref)

        dk_s[...] = jnp.zeros_like(dk_s)
        dv_s[...] = jnp.zeros_like(dv_s)
        kd, vd = _kv_tiles(kp_ref, kc_ref, vp_ref, vc_ref)
        valid = _band_mask(i)
        lo = lax.broadcasted_iota(jnp.int32, (BLOCK, LANES), 1) < HEAD_DIM
        top = lax.broadcasted_iota(jnp.int32, (2 * BLOCK, 1), 0) < BLOCK
        for first in range(0, N_PAIR_A, A_GROUP):
            pairs = range(first, first + A_GROUP)
            kvs = [2 * p // GROUP_A for p in pairs]
            qs_ = [jnp.concatenate(_head_halves(q_ref[:, p * LANES:(p + 1) * LANES], lo), axis=0) for p in pairs]
            dos = [jnp.concatenate(_head_halves(do_ref[:, p * LANES:(p + 1) * LANES], lo), axis=0) for p in pairs]
            ss = [lax.dot_general(q, kd[kv], _NT, preferred_element_type=F32) for q, kv in zip(qs_, kvs)]
            dps = [lax.dot_general(d, vd[kv], _NT, preferred_element_type=F32) for d, kv in zip(dos, kvs)]
            sm = [_swa_softmax(s, valid, jnp.where(top, sink_ref[2 * p], sink_ref[2 * p + 1])) for s, p in zip(ss, pairs)]
            deltas = [jnp.sum(pn * dp, axis=1, keepdims=True) for (pn, _), dp in zip(sm, dps)]
            dsbs = [(pn * (dp - delta) * ATTN_SCALE).astype(BF16) for (pn, _), dp, delta in zip(sm, dps, deltas)]
            for n, p in enumerate(pairs):
                dq2 = jnp.dot(dsbs[n], kd[kvs[n]], preferred_element_type=F32)
                dq_ref[:, p * LANES:(p + 1) * LANES] = jnp.where(lo, dq2[:BLOCK], dq2[BLOCK:])
                dk_s[kvs[n]] += lax.dot_general(dsbs[n], qs_[n], _TN, preferred_element_type=F32)
                dv_s[kvs[n]] += lax.dot_general(sm[n][0].astype(BF16), dos[n], _TN, preferred_element_type=F32)
                t = sm[n][1] * deltas[n]
                for hh in range(2):
                    dsink = -jnp.sum(t[hh * BLOCK:(hh + 1) * BLOCK], axis=0, keepdims=True)
                    ds_ref[2 * p + hh:2 * p + hh + 1, :] += jnp.broadcast_to(dsink, (1, LANES))

        lo2 = lax.broadcasted_iota(jnp.int32, (2 * BLOCK, LANES), 1) < HEAD_DIM

        def fold(acc):
            halves = [acc[kv] + pltpu.roll(acc[kv], HEAD_DIM, 1) for kv in range(N_KV_A)]
            return jnp.where(lo2, halves[0], halves[1])

        dk2, dv2 = fold(dk_s), fold(dv_s)

        @pl.when(i > 0)
        def _():
            start = pl.multiple_of((i - 1) * BLOCK, BLOCK)
            dk_ref[pl.ds(start, 2 * BLOCK), :] += dk2
            dv_ref[pl.ds(start, 2 * BLOCK), :] += dv2

        @pl.when(i == 0)
        def _():
            dk_ref[0:BLOCK, :] += dk2[BLOCK:, :]
            dv_ref[0:BLOCK, :] += dv2[BLOCK:, :]

    slots = pltpu.VMEM((N_KV_A, 2 * BLOCK, LANES), F32)
    return pl.pallas_call(
        body, name="attn_a_bwd", grid=(NB, S // BLOCK),
        in_specs=[qs, qs, kp, kc, vp, vc, pl.BlockSpec(memory_space=pltpu.SMEM)],
        out_specs=[qs, full, full, sink_out],
        out_shape=[jax.ShapeDtypeStruct((NB, S, Q_WIDTH_A), F32), jax.ShapeDtypeStruct((NB, S, LANES), F32),
                   jax.ShapeDtypeStruct((NB, S, LANES), F32), jax.ShapeDtypeStruct((NB, N_Q_A, LANES), F32)],
        scratch_shapes=[slots, slots],
        compiler_params=_params(("parallel", "arbitrary")),
    )(qkn, do, qkn, qkn, qkn, qkn, sinks)


def _cumsum_mats():
    src = lax.broadcasted_iota(jnp.int32, (2 * BLOCK, 2 * BLOCK), 0) % BLOCK
    dst = lax.broadcasted_iota(jnp.int32, (2 * BLOCK, 2 * BLOCK), 1)
    ones = dst >= BLOCK
    rev = ((src > dst) | ones).astype(BF16)
    fwd = ((src < dst) | ones).astype(BF16)
    return rev, fwd


def _log_sigmoids(z):
    sp = jnp.log(1.0 + jnp.exp(-jnp.abs(z)))
    return jnp.minimum(z, 0.0) - sp, -(jnp.maximum(z, 0.0) + sp)


def _cumsum_mxu_many(vs, mat):
    parts = []
    for v in vs:
        hi = v.astype(BF16)
        parts.append(jnp.concatenate([hi, (v - hi.astype(F32)).astype(BF16)], axis=1))
    r = jnp.dot(jnp.concatenate(parts, axis=0), mat, preferred_element_type=F32)
    return [(r[n * BLOCK:(n + 1) * BLOCK, :BLOCK], r[n * BLOCK:(n + 1) * BLOCK, BLOCK:]) for n in range(len(vs))]


def _strict_mask():
    r = lax.broadcasted_iota(jnp.int32, (BLOCK, BLOCK), 0)
    c = lax.broadcasted_iota(jnp.int32, (BLOCK, BLOCK), 1)
    return c < r


def _tile(ref, j):
    return ref[pl.ds(pl.multiple_of(j * BLOCK, BLOCK), BLOCK), :]


SWEEP_EXIT = -88.0


def _head_halves(t, lo):
    zero = jnp.zeros_like(t)
    return jnp.where(lo, t, zero), jnp.where(lo, zero, t)


def _sb_specs(S, HD, width):
    n = HD // width
    blk = pl.BlockSpec((None, BLOCK, width), lambda b, p, i: (b, i, p))
    k_full = pl.BlockSpec((None, S, width), lambda b, p, i: (b, 0, n + p))
    v_full = pl.BlockSpec((None, S, width), lambda b, p, i: (b, 0, 2 * n + p))
    mat = pl.BlockSpec((2 * BLOCK, 2 * BLOCK), lambda b, p, i: (0, 0))
    return blk, k_full, v_full, mat


SB_FWD_PAIRS = 4
SB_BWD_PAIRS = 2
SB_BWD_TILES = 2
SB_BWD_VMEM_LIMIT_BYTES = 58 * 1024 * 1024


def _attn_b_fwd(qkv, rev):
    NB, S, W = qkv.shape
    HD = W // 3
    width = SB_FWD_PAIRS * LANES
    n_heads = 2 * SB_FWD_PAIRS
    blk, k_full, v_full, mat = _sb_specs(S, HD, width)

    def body(q_ref, k_ref, v_ref, rev_ref, o_ref):
        i = pl.program_id(2)
        rv = rev_ref[...]
        mask = _strict_mask()
        lo = lax.broadcasted_iota(jnp.int32, (BLOCK, LANES), 1) < HEAD_DIM
        q_all = q_ref[...]
        q_stack = [jnp.concatenate(_head_halves(q_all[:, p * LANES:(p + 1) * LANES] * ATTN_SCALE, lo), axis=0)
                   for p in range(SB_FWD_PAIRS)]

        def pair_tiles(ref, j):
            t = _tile(ref, j)
            return [t[:, p * LANES:(p + 1) * LANES] for p in range(SB_FWD_PAIRS)]

        def tiles_pass(js, carries, diagonal_first, last_counts=None):
            zs, v_tiles = [], []
            for j in js:
                ks = pair_tiles(k_ref, j)
                v_tiles.append(pair_tiles(v_ref, j))
                for p in range(SB_FWD_PAIRS):
                    z2 = lax.dot_general(q_stack[p], ks[p], _NT, preferred_element_type=F32)
                    zs += [z2[:BLOCK], z2[BLOCK:]]
            logs = [_log_sigmoids(z) for z in zs]
            masked = [jnp.where(mask, lm, 0.0) if diagonal_first and n < n_heads else lm
                      for n, (_, lm) in enumerate(logs)]
            cums = _cumsum_mxu_many(masked, rv)
            probs, new_c = {}, []
            for h in range(n_heads):
                carry = None if diagonal_first else carries[h]
                for t in range(len(js)):
                    n = t * n_heads + h
                    after, rs = cums[n]
                    if diagonal_first and t == 0:
                        a = jnp.where(mask, jnp.exp(logs[n][0] + after), 0.0)
                        carry = rs
                    else:
                        a = jnp.exp(logs[n][0] + after + carry)
                        carry = carry + rs
                    if last_counts is not None and t == len(js) - 1:
                        a = jnp.where(last_counts, a, 0.0)
                    probs[t, h] = a.astype(BF16)
                new_c.append(carry)
            outs = []
            for p in range(SB_FWD_PAIRS):
                total = None
                for t in range(len(js)):
                    pv = jnp.dot(jnp.concatenate([probs[t, 2 * p], probs[t, 2 * p + 1]], axis=0), v_tiles[t][p],
                                 preferred_element_type=F32)
                    part = jnp.where(lo, pv[:BLOCK], pv[BLOCK:])
                    total = part if total is None else total + part
                outs.append(total)
            return new_c, outs

        carries, accs = tiles_pass([i, jnp.maximum(i - 1, 0)], None, True, last_counts=i > 0)

        def live(cs):
            top = cs[0]
            for c in cs[1:]:
                top = jnp.maximum(top, c)
            return jnp.max(top) > SWEEP_EXIT

        def cond(st):
            return (st[0] < i - 1) & st[1]

        def step(st):
            jj, _, cs, accs = st
            new_c, outs = tiles_pass([i - 2 - jj], cs, False)
            return jj + 1, live(new_c), new_c, [acc + o for acc, o in zip(accs, outs)]

        st = lax.while_loop(cond, step, (jnp.int32(0), live(carries), carries, accs))
        for p in range(SB_FWD_PAIRS):
            o_ref[:, p * LANES:(p + 1) * LANES] = st[3][p].astype(BF16)

    return pl.pallas_call(
        body, name="attn_b_fwd", grid=(NB, HD // width, S // BLOCK),
        in_specs=[blk, k_full, v_full, mat], out_specs=blk,
        out_shape=jax.ShapeDtypeStruct((NB, S, HD), BF16),
        compiler_params=_params(("parallel", "parallel", "arbitrary")),
    )(qkv, qkv, qkv, rev)


def _attn_b_bwd(qkv, do, rev, fwd):
    NB, S, W = qkv.shape
    HD = W // 3
    width = SB_BWD_PAIRS * LANES
    n_heads = 2 * SB_BWD_PAIRS
    nj = S // BLOCK
    blk, k_full, v_full, mat = _sb_specs(S, HD, width)
    acc_full = pl.BlockSpec((None, S, width), lambda b, p, i: (b, 0, p))

    def body(q_ref, do_ref, k_ref, v_ref, rev_ref, fwd_ref, dq_ref, dk_ref, dv_ref, sig_s, a_s, e_s):
        i = pl.program_id(2)

        @pl.when(i == 0)
        def _():
            dk_ref[...] = jnp.zeros_like(dk_ref)
            dv_ref[...] = jnp.zeros_like(dv_ref)

        rv, fw = rev_ref[...], fwd_ref[...]
        mask = _strict_mask()
        lo = lax.broadcasted_iota(jnp.int32, (BLOCK, LANES), 1) < HEAD_DIM
        pairs = range(SB_BWD_PAIRS)

        def cols(p):
            return slice(p * LANES, (p + 1) * LANES)

        q_stack = [jnp.concatenate(_head_halves(q_ref[:, cols(p)], lo), axis=0) for p in pairs]
        qs_stack = [q * ATTN_SCALE for q in q_stack]
        do_stack = [jnp.concatenate(_head_halves(do_ref[:, cols(p)], lo), axis=0) for p in pairs]

        def sweep1_tiles(js, carries, diagonal_first):
            zs, das = [], []
            for j in js:
                kj, vj = _tile(k_ref, j), _tile(v_ref, j)
                for p in pairs:
                    z2 = lax.dot_general(qs_stack[p], kj[:, cols(p)], _NT, preferred_element_type=F32)
                    da2 = lax.dot_general(do_stack[p], vj[:, cols(p)], _NT, preferred_element_type=F32)
                    zs += [z2[:BLOCK], z2[BLOCK:]]
                    das += [da2[:BLOCK], da2[BLOCK:]]
            logs = [_log_sigmoids(z) for z in zs]
            cums = _cumsum_mxu_many([jnp.where(mask, lm, 0.0) if diagonal_first and n < n_heads else lm
                                     for n, (_, lm) in enumerate(logs)], rv)
            new_c, stores = [], []
            for h in range(n_heads):
                carry = None if diagonal_first else carries[h]
                for t, j in enumerate(js):
                    n = t * n_heads + h
                    lb, (after, rs) = logs[n][0], cums[n]
                    if diagonal_first and t == 0:
                        a = jnp.where(mask, jnp.exp(lb + after), 0.0)
                        carry = rs
                    else:
                        a = jnp.exp(lb + after + carry)
                        carry = carry + rs
                    stores.append((t, h, j, jnp.exp(lb), a.astype(BF16), das[n] * a))
                new_c.append(carry)
            for t, h, j, sg, ab, e in sorted(stores, key=lambda s: -s[0]):
                sig_s[h, j] = sg
                a_s[h, j] = ab
                e_s[h, j] = e
            return new_c

        carries = sweep1_tiles([jnp.maximum(i - t, 0) for t in range(SB_BWD_TILES + 1)], None, True)
        done = SB_BWD_TILES

        def live(cs):
            top = cs[0]
            for c in cs[1:]:
                top = jnp.maximum(top, c)
            return jnp.max(top) > SWEEP_EXIT

        def cond(st):
            return (done + SB_BWD_TILES * st[0] < i) & st[1]

        def sweep1(st):
            first = i - 1 - done - SB_BWD_TILES * st[0]
            new_c = sweep1_tiles([jnp.maximum(first - t, 0) for t in range(SB_BWD_TILES)], st[2], False)
            return st[0] + 1, live(new_c), new_c

        trips = lax.while_loop(cond, sweep1, (jnp.int32(0), live(carries), carries))[0]
        lowest = jnp.maximum(i - done - SB_BWD_TILES * trips, 0)

        def grads(js, st, diagonal):
            prefixes, dqs = st
            es = [e_s[h, j] for j in js for h in range(n_heads)]
            cums = _cumsum_mxu_many(es, fw)
            dzs, new_p = [], []
            for h in range(n_heads):
                prefix = prefixes[h]
                for t, j in enumerate(js):
                    n = t * n_heads + h
                    sg = sig_s[h, j]
                    e_before, rs = cums[n]
                    dz = (es[n] * (1.0 - sg) - (e_before + prefix) * sg) * ATTN_SCALE
                    if diagonal:
                        dz = jnp.where(mask, dz, 0.0)
                    dzs.append((t, h, dz.astype(BF16)))
                    prefix = prefix + rs
                new_p.append(prefix)
            dz_of = {(t, h): dz for t, h, dz in dzs}
            new_dq = list(dqs)
            for t, j in enumerate(js):
                kj = _tile(k_ref, j)
                rows = pl.ds(pl.multiple_of(j * BLOCK, BLOCK), BLOCK)
                for p in pairs:
                    dz_stack = jnp.concatenate([dz_of[t, 2 * p], dz_of[t, 2 * p + 1]], axis=0)
                    a_stack = jnp.concatenate([a_s[2 * p, j], a_s[2 * p + 1, j]], axis=0)
                    dq2 = jnp.dot(dz_stack, kj[:, cols(p)], preferred_element_type=F32)
                    new_dq[p] = new_dq[p] + jnp.where(lo, dq2[:BLOCK], dq2[BLOCK:])
                    dk_ref[rows, cols(p)] += lax.dot_general(dz_stack, q_stack[p], _TN, preferred_element_type=F32)
                    dv_ref[rows, cols(p)] += lax.dot_general(a_stack, do_stack[p], _TN, preferred_element_type=F32)
            return new_p, new_dq

        zeros = jnp.zeros((BLOCK, BLOCK), F32)
        st = ([zeros] * n_heads, [zeros] * SB_BWD_PAIRS)
        count = i - lowest
        st = lax.fori_loop(0, count % SB_BWD_TILES, lambda t, st: grads([lowest + t], st, False), st)
        start = lowest + count % SB_BWD_TILES
        st = lax.fori_loop(0, count // SB_BWD_TILES,
                           lambda t, st: grads([start + SB_BWD_TILES * t + u for u in range(SB_BWD_TILES)], st, False), st)
        dqs = grads([i], st, True)[1]
        for p in pairs:
            dq_ref[:, cols(p)] = dqs[p]

    f32_stash = pltpu.VMEM((n_heads, nj, BLOCK, BLOCK), F32)
    bf16_stash = pltpu.VMEM((n_heads, nj, BLOCK, BLOCK), BF16)
    return pl.pallas_call(
        body, name="attn_b_bwd", grid=(NB, HD // width, nj),
        in_specs=[blk, blk, k_full, v_full, mat, mat], out_specs=[blk, acc_full, acc_full],
        out_shape=[jax.ShapeDtypeStruct((NB, S, HD), F32)] * 3,
        scratch_shapes=[f32_stash, bf16_stash, f32_stash],
        compiler_params=_params(("parallel", "parallel", "arbitrary"), SB_BWD_VMEM_LIMIT_BYTES),
    )(qkv, do, qkv, qkv, rev, fwd)


def _ada_fwd(c_all, w, b):
    L, D, N = w.shape
    B = c_all.shape[0]

    def body(c_ref, w_ref, b_ref, o_ref):
        cv = c_ref[...]
        cond = (cv * _sigmoid(cv)).astype(BF16)
        o_ref[...] = jnp.dot(cond, w_ref[...].astype(BF16), preferred_element_type=F32) + b_ref[...]

    return pl.pallas_call(
        body, name="ada_fwd", grid=(L,),
        in_specs=[pl.BlockSpec((B, D), lambda l: (0, 0)), pl.BlockSpec((None, D, N), lambda l: (l, 0, 0)),
                  pl.BlockSpec((None, 1, N), lambda l: (l, 0, 0))],
        out_specs=pl.BlockSpec((None, B, N), lambda l: (l, 0, 0)),
        out_shape=jax.ShapeDtypeStruct((L, B, N), F32),
        compiler_params=_params(("parallel",)),
    )(c_all, w, b)


def _ada_bwd(c_all, dmod_all, dmod_shard):
    L, B, N = dmod_shard.shape
    D = c_all.shape[1]
    N_all = dmod_all.shape[2]

    def body(c_ref, da_ref, ds_ref, gw_ref, gb_ref):
        cv = c_ref[...]
        cond = (cv * _sigmoid(cv)).astype(BF16)
        gw_ref[...] = lax.dot_general(cond, ds_ref[...].astype(BF16), _TN, preferred_element_type=F32)
        gb_ref[...] = jnp.sum(da_ref[...], axis=0, keepdims=True)

    return pl.pallas_call(
        body, name="ada_bwd", grid=(L,),
        in_specs=[pl.BlockSpec((B, D), lambda l: (0, 0)), pl.BlockSpec((None, B, N_all), lambda l: (l, 0, 0)),
                  pl.BlockSpec((None, B, N), lambda l: (l, 0, 0))],
        out_specs=[pl.BlockSpec((None, D, N), lambda l: (l, 0, 0)), pl.BlockSpec((None, 1, N_all), lambda l: (l, 0, 0))],
        out_shape=[jax.ShapeDtypeStruct((L, D, N), F32), jax.ShapeDtypeStruct((L, 1, N_all), F32)],
        compiler_params=_params(("parallel",)),
    )(c_all, dmod_all, dmod_shard)


def _adamw(w, g, m, v, name):
    shape = w.shape
    if w.ndim == 2:
        w, g, m, v = [t.reshape((1,) + shape) for t in (w, g, m, v)]
    L, R, C = w.shape
    tr = _pick(R, max(8, (1 << 18) // C), 8)
    c1 = 1.0 - ADAM_B1 ** ADAM_STEP
    c2 = 1.0 - ADAM_B2 ** ADAM_STEP

    def body(w_ref, g_ref, m_ref, v_ref, d_ref, nm_ref, nv_ref):
        gv = g_ref[...]
        nm = ADAM_B1 * m_ref[...] + (1.0 - ADAM_B1) * gv
        nv = ADAM_B2 * v_ref[...] + (1.0 - ADAM_B2) * (gv * gv)
        d_ref[...] = -ADAM_LR * ((nm / c1) / (jnp.sqrt(nv / c2) + ADAM_EPS) + ADAM_WD * w_ref[...])
        nm_ref[...] = nm
        nv_ref[...] = nv

    spec = pl.BlockSpec((None, tr, C), lambda l, r: (l, r, 0))
    out = pl.pallas_call(
        body, name=name, grid=(L, R // tr), in_specs=[spec] * 4, out_specs=[spec] * 3,
        out_shape=[jax.ShapeDtypeStruct((L, R, C), F32)] * 3,
        compiler_params=_params(("parallel", "parallel")),
    )(w, g, m, v)
    return [t.reshape(shape) for t in out]


_SHARDED = (("wqkv_a", 2), ("wo_a", 1), ("wqkv_b", 2), ("wo_b", 1), ("w_gate", 2), ("w_up", 2), ("w_down", 1))


def _pack_full(layers, axis, gate_up=None):
    L = len(layers)
    R, C = layers[0].shape

    def shards(m):
        if gate_up is not None:
            F = C // 2
            tf, Cs = _ff_tile(F), F // 4
            assert tf % Cs == 0
            starts = [(2 * (s * Cs // tf) + gate_up) * tf + s * Cs % tf for s in range(4)]
            return jnp.stack([m[:, st:st + Cs] for st in starts])
        if axis == 2:
            return m.reshape(R, 4, C // 4).transpose(1, 0, 2)
        return m.reshape(4, R // 4, C)

    halves = [jnp.stack([shards(m) for m in layers[h * (L // 2):(h + 1) * (L // 2)]], axis=1) for h in range(2)]
    return jnp.stack(halves)


def _unpack_full(gathered, axis):
    _, Lh, Rs, Cs = gathered.shape
    t = gathered.reshape(4, 2, Lh, Rs, Cs)
    layers = []
    for h in range(2):
        for l in range(Lh):
            piece = t[:, h, l]
            if axis == 2:
                layers.append(piece.transpose(1, 0, 2).reshape(Rs, 4 * Cs))
            else:
                layers.append(piece.reshape(4 * Rs, Cs))
    return layers


def _sum_slabs(own, recv, name, with_bf16=False):
    C = own.shape[-1]
    out = _sum_leading(recv.reshape(recv.shape[0], -1, C), name, own=own.reshape(-1, C), with_bf16=with_bf16)
    if with_bf16:
        return out[0].reshape(own.shape), out[1].reshape(own.shape)
    return out.reshape(own.shape)


def _gather8(x, name):
    return _all_gather8([x], name)[0]


def _rope_tables(positions):
    half = ROT_DIM // 2
    inv_freq = jnp.power(jnp.float32(ROPE_THETA), -jnp.arange(half, dtype=F32) * 2.0 / ROT_DIM)
    ang = positions.astype(F32).reshape(-1, 1) * inv_freq
    cos, sin = jnp.cos(ang), jnp.sin(ang)
    T = ang.shape[0]
    rest = HEAD_DIM - ROT_DIM
    c64 = jnp.concatenate([cos, cos, jnp.ones((T, rest), F32)], axis=1)
    s64 = jnp.concatenate([-sin, sin, jnp.zeros((T, rest), F32)], axis=1)
    return jnp.tile(c64, (1, 2)), jnp.tile(s64, (1, 2))


def _gain_rows(q_gain, k_gain):
    q2 = jnp.tile(q_gain.reshape(1, HEAD_DIM), (GROUP_A, 2))
    k2 = jnp.tile(k_gain.reshape(1, HEAD_DIM), (1, 2))
    return jnp.concatenate([q2, k2, jnp.ones((1, LANES), F32)], axis=0)


def _local_step(x, positions, mod, norm1_g, norm2_g, q_norm_a, k_norm_a, sinks_a,
                wqkv_a, wo_a, wqkv_b, wo_b, wgu, wd, loss_target):
    NB, S, D = x.shape
    T = NB * S
    QA = N_Q_A * HEAD_DIM
    tab_c, tab_s = _rope_tables(positions)
    rev, fwd = _cumsum_mats()

    saved = []
    xc = x
    mods = [[mod[i][:, k * D:(k + 1) * D].reshape(NB, 1, D) for k in range(6)] for i in range(DEPTH)]
    h = _norm_mod_fwd(xc, norm1_g[0:1], mods[0][1], mods[0][0])
    for i in range(DEPTH):
        j = i // 2
        sh1, sc1, g1, sh2, sc2, g2 = mods[i]
        st = dict(x=xc, sc1=sc1, g1=g1, sc2=sc2, g2=g2)
        st["h"] = h.reshape(T, D)
        if i % 2 == 0:
            st["qkv"] = _matmul(st["h"], wqkv_a[j], "nn", F32, "qkv_a")
            st["gains"] = _gain_rows(q_norm_a[j], k_norm_a[j])
            st["qkn"] = _qk_prep_fwd(st["qkv"], tab_c, tab_s, st["gains"]).reshape(NB, S, -1)
            st["o"] = _attn_a_fwd(st["qkn"], sinks_a[j]).reshape(T, QA)
            y = _matmul(st["o"], wo_a[j], "nn", F32, "wo_a")
        else:
            st["qkv"] = _matmul(st["h"], wqkv_b[j], "nn", BF16, "qkv_b").reshape(NB, S, -1)
            st["o"] = _attn_b_fwd(st["qkv"], rev).reshape(T, N_H_B * HEAD_DIM)
            y = _matmul(st["o"], wo_b[j], "nn", F32, "wo_b")
        st["y"] = y.reshape(NB, S, D)
        x1, h2 = _gate_res(xc, st["y"], g1, norm=(norm2_g[i:i + 1], sc2, sh2))
        st["x1"] = x1
        st["h2"] = h2.reshape(T, D)
        st["gu"], st["act"] = _matmul(st["h2"], wgu[i], "nn", BF16, "gate_up", swiglu=True)
        st["m"] = _matmul(st["act"], wd[i], "nn", F32, "down").reshape(NB, S, D)
        if i + 1 < DEPTH:
            xc, h = _gate_res(x1, st["m"], g2, norm=(norm1_g[i + 1:i + 2], mods[i + 1][1], mods[i + 1][0]))
        else:
            xc = _gate_res(x1, st["m"], g2)
        saved.append(st)

    loss, dx = _loss_fwd_bwd(xc, loss_target)

    grads = {name: [None] * n for name, n in
             (("wqkv_a", 2), ("wo_a", 2), ("wqkv_b", 2), ("wo_b", 2), ("wgu", DEPTH), ("wd", DEPTH),
              ("norm1_g", DEPTH), ("norm2_g", DEPTH), ("q_norm_a", 2), ("k_norm_a", 2), ("sinks_a", 2))}
    dmod = [None] * DEPTH
    dm, dg2 = _gate_res_bwd(dx, saved[-1]["m"], saved[-1]["g2"])
    for i in reversed(range(DEPTH)):
        j = i // 2
        st = saved[i]
        dm = dm.reshape(T, D)
        grads["wd"][i] = _matmul(st["act"], dm, "tn", F32, "d_wd")
        dgu = _swiglu_bwd(dm, wd[i], st["gu"])
        grads["wgu"][i] = _matmul(st["h2"], dgu, "tn", F32, "d_wgu")
        dx1, dsh2, dsc2, grads["norm2_g"][i], dy, dg1 = _norm_mod_bwd(
            dgu, wgu[i], st["x1"], norm2_g[i:i + 1], st["sc2"], dx, "d_h2", y=st["y"], g=st["g1"])
        dy = dy.reshape(T, D)
        if i % 2 == 0:
            do = _matmul(dy, wo_a[j], "nt", BF16, "d_o_a").reshape(NB, S, QA)
            grads["wo_a"][j] = _matmul(st["o"], dy, "tn", F32, "d_wo_a")
            dq, dk, dv, dsink = _attn_a_bwd(st["qkn"], do, sinks_a[j])
            dqkv, dgain = _qk_prep_bwd(st["qkv"], dq.reshape(T, QA), dk.reshape(T, LANES), dv.reshape(T, LANES),
                                       tab_c, tab_s, st["gains"])
            w_in = wqkv_a[j]
            grads["wqkv_a"][j] = _matmul(st["h"], dqkv, "tn", F32, "d_wqkv_a")
            grads["q_norm_a"][j] = jnp.sum(dgain[:GROUP_A].reshape(2 * GROUP_A, HEAD_DIM), axis=0)
            grads["k_norm_a"][j] = jnp.sum(dgain[GROUP_A].reshape(2, HEAD_DIM), axis=0)
            grads["sinks_a"][j] = jnp.sum(dsink[..., 0], axis=0)
        else:
            do = _matmul(dy, wo_b[j], "nt", BF16, "d_o_b").reshape(NB, S, -1)
            grads["wo_b"][j] = _matmul(st["o"], dy, "tn", F32, "d_wo_b")
            dq, dk, dv = _attn_b_bwd(st["qkv"], do, rev, fwd)
            dqkv = jnp.concatenate([dq, dk, dv], axis=-1).reshape(T, -1).astype(BF16)
            w_in = wqkv_b[j]
            grads["wqkv_b"][j] = _matmul(st["h"], dqkv, "tn", F32, "d_wqkv_b")
        this_dg2 = dg2
        if i > 0:
            dx, dsh1, dsc1, grads["norm1_g"][i], dm, dg2 = _norm_mod_bwd(
                dqkv, w_in, st["x"], norm1_g[i:i + 1], st["sc1"], dx1, "d_h", y=saved[i - 1]["m"], g=saved[i - 1]["g2"])
        else:
            dx, dsh1, dsc1, grads["norm1_g"][i] = _norm_mod_bwd(
                dqkv, w_in, st["x"], norm1_g[i:i + 1], st["sc1"], dx1, "d_h")
        dmod[i] = jnp.concatenate([dsh1, dsc1, dg1, dsh2, dsc2, this_dg2], axis=-1).reshape(NB, 6 * D)

    matrices = ("wqkv_a", "wo_a", "wqkv_b", "wo_b", "wgu", "wd")
    grads = {name: parts if name in matrices else jnp.stack(parts) for name, parts in grads.items()}
    return loss, dx, grads, jnp.stack(dmod)


def _rows_of(flat, cols=PACK_COLS):
    n = flat.shape[0]
    pad = (-n) % (8 * cols)
    if pad:
        flat = jnp.concatenate([flat, jnp.zeros((pad,), flat.dtype)])
    return flat.reshape(-1, cols)


def kernel(x, c, positions, ada_w, ada_b, norm1_g, norm2_g, wqkv_a, q_norm_a, k_norm_a, sinks_a, wo_a, wqkv_b, wo_b, w_gate, w_up, w_down, loss_target, m_ada_w, m_ada_b, m_norm1_g, m_norm2_g, m_wqkv_a, m_q_norm_a, m_k_norm_a, m_sinks_a, m_wo_a, m_wqkv_b, m_wo_b, m_w_gate, m_w_up, m_w_down, v_ada_w, v_ada_b, v_norm1_g, v_norm2_g, v_wqkv_a, v_q_norm_a, v_k_norm_a, v_sinks_a, v_wo_a, v_wqkv_b, v_wo_b, v_w_gate, v_w_up, v_w_down):
    xi, yi, ci = lax.axis_index("x"), lax.axis_index("y"), lax.axis_index("c")
    dev = 4 * xi + 2 * yi + ci
    chip = 2 * xi + yi
    NB, S, D = x.shape
    B_all = N_DEV * NB
    L = ada_w.shape[0]
    n_mod = ada_w.shape[2] // 2

    c_all = _gather8(_rows_of(c.reshape(-1), LANES), "gather_c").reshape(N_DEV, -1)[:, :NB * D].reshape(B_all, D)
    ada_w_half = lax.dynamic_slice_in_dim(ada_w, ci * n_mod, n_mod, axis=2)
    ada_b_half = lax.dynamic_slice_in_dim(ada_b, dev * n_mod, n_mod, axis=1).reshape(L, 1, n_mod)
    mod_part = _ada_fwd(c_all, ada_w_half, ada_b_half)
    n_part = L * B_all * n_mod
    mod_all = _gather8(_rows_of(mod_part.reshape(-1)), "gather_mod").reshape(N_DEV, -1)[:, :n_part]
    mod_all = mod_all.reshape(N_DEV, L, B_all, n_mod).transpose(1, 2, 0, 3).reshape(L, B_all, N_DEV * n_mod)
    mod = lax.dynamic_slice_in_dim(mod_all, dev * NB, NB, axis=1)

    shards = dict(wqkv_a=wqkv_a, wo_a=wo_a, wqkv_b=wqkv_b, wo_b=wo_b, w_gate=w_gate, w_up=w_up, w_down=w_down)
    halves = []
    for name, _ in _SHARDED:
        w = shards[name]
        half = lax.dynamic_index_in_dim(w.reshape((2, w.shape[0] // 2) + w.shape[1:]), ci, 0, keepdims=False)
        halves.append(half.astype(BF16))
    gathered = _all_gather8(halves, "gather_weights", local_axis=1, local_chunks=8, relay_axis=1)
    full = {name: _unpack_full(t, axis) for (name, axis), t in zip(_SHARDED, gathered)}
    wgu = [_interleave(gate, up) for gate, up in zip(full["w_gate"], full["w_up"])]

    loss, grad_x, g, dmod = _local_step(
        x, positions, mod, norm1_g, norm2_g, q_norm_a, k_norm_a, sinks_a,
        full["wqkv_a"], full["wo_a"], full["wqkv_b"], full["wo_b"], wgu, full["w_down"], loss_target)

    g_full = dict(wqkv_a=g["wqkv_a"], wo_a=g["wo_a"], wqkv_b=g["wqkv_b"], wo_b=g["wo_b"],
                  w_gate=g["wgu"], w_up=g["wgu"], w_down=g["wd"])
    which = dict(w_gate=0, w_up=1)
    packed = [_pack_full(g_full[name], axis, which.get(name)) for name, axis in _SHARDED]
    def own(t, index):
        return lax.dynamic_index_in_dim(t, index, 0, keepdims=False)

    from_cores = _exchange_cores(packed, "rs_cores", chunk_axis=0, chunks=4)
    chip_part = [_sum_slabs(own(p, ci), r, "rs_add_cores", with_bf16=True) for p, r in zip(packed, from_cores)]
    from_chips = _exchange_chips([b for _, b in chip_part], "rs_chips", relay_axis=1)
    mine = [_sum_slabs(own(p, chip), r, "rs_add_chips") for (p, _), r in zip(chip_part, from_chips)]
    theirs = _sibling_send(mine, "rs_halves")
    grad = {}
    for (name, _), m, t in zip(_SHARDED, mine, theirs):
        first, second = jnp.where(ci == 0, m, t), jnp.where(ci == 0, t, m)
        grad[name] = jnp.stack([first, second]).reshape(shards[name].shape)

    small_names = ("norm1_g", "norm2_g", "q_norm_a", "k_norm_a", "sinks_a")
    small = [dmod.reshape(-1)] + [g[name].reshape(-1) for name in small_names] + [loss.reshape(-1)]
    small_sizes = [t.shape[0] for t in small]
    small_rows = _rows_of(jnp.concatenate(small))
    small_all = _gather8(small_rows, "gather_small")
    small_sum = _sum_leading(small_all, "sum_small").reshape(-1)
    n_dmod = small_sizes[0]
    dmod_all = small_all.reshape(N_DEV, -1)[:, :n_dmod].reshape(N_DEV, L, NB, 6 * D)
    dmod_all = dmod_all.transpose(1, 0, 2, 3).reshape(L, B_all, 6 * D)
    off = n_dmod
    for name, sz in zip(small_names + ("loss",), small_sizes[1:]):
        grad[name] = small_sum[off:off + sz]
        off += sz
    loss_total = grad.pop("loss").reshape(())
    for name, ref in (("norm1_g", norm1_g), ("norm2_g", norm2_g), ("q_norm_a", q_norm_a),
                      ("k_norm_a", k_norm_a), ("sinks_a", sinks_a)):
        grad[name] = grad[name].reshape(ref.shape)

    n_shard = ada_w.shape[2]
    dmod_shard = lax.dynamic_slice_in_dim(dmod_all, chip * n_shard, n_shard, axis=2)
    grad["ada_w"], gb = _ada_bwd(c_all, dmod_all, dmod_shard)
    grad["ada_b"] = gb.reshape(ada_b.shape)

    weights = dict(ada_w=ada_w, ada_b=ada_b, norm1_g=norm1_g, norm2_g=norm2_g, wqkv_a=wqkv_a, q_norm_a=q_norm_a,
                   k_norm_a=k_norm_a, sinks_a=sinks_a, wo_a=wo_a, wqkv_b=wqkv_b, wo_b=wo_b, w_gate=w_gate,
                   w_up=w_up, w_down=w_down)
    m_in = dict(ada_w=m_ada_w, ada_b=m_ada_b, norm1_g=m_norm1_g, norm2_g=m_norm2_g, wqkv_a=m_wqkv_a,
                q_norm_a=m_q_norm_a, k_norm_a=m_k_norm_a, sinks_a=m_sinks_a, wo_a=m_wo_a, wqkv_b=m_wqkv_b,
                wo_b=m_wo_b, w_gate=m_w_gate, w_up=m_w_up, w_down=m_w_down)
    v_in = dict(ada_w=v_ada_w, ada_b=v_ada_b, norm1_g=v_norm1_g, norm2_g=v_norm2_g, wqkv_a=v_wqkv_a,
                q_norm_a=v_q_norm_a, k_norm_a=v_k_norm_a, sinks_a=v_sinks_a, wo_a=v_wo_a, wqkv_b=v_wqkv_b,
                wo_b=v_wo_b, w_gate=v_w_gate, w_up=v_w_up, w_down=v_w_down)
    names = list(weights)
    delta, new_m, new_v = {}, {}, {}
    for name in names:
        delta[name], new_m[name], new_v[name] = _adamw(weights[name], grad[name], m_in[name], v_in[name],
                                                       "adamw_" + name)
    return (loss_total, grad_x, *[grad[k] for k in names], *[delta[k] for k in names],
            *[new_m[k] for k in names], *[new_v[k] for k in names])
```

```python
import jax
import jax.numpy as jnp
from jax import lax
from jax.experimental import pallas as pl
from jax.experimental.pallas import tpu as pltpu

F32 = jnp.float32
BF16 = jnp.bfloat16

DEPTH = 4
HEAD_DIM = 64
N_Q_A = 16
N_KV_A = 2
GROUP_A = N_Q_A // N_KV_A
N_H_B = 16
BLOCK = 128
ROT_DIM = HEAD_DIM // 4
ROPE_THETA = 500000.0
EPS = 1e-6
ATTN_SCALE = HEAD_DIM ** -0.5
NEG_BIG = -1e30

ADAM_LR = 0.001
ADAM_B1 = 0.9
ADAM_B2 = 0.999
ADAM_EPS = 1e-08
ADAM_WD = 0.01
ADAM_STEP = 10

N_DEV = 8
LANES = 128
PACK_COLS = 1024
VMEM_LIMIT_BYTES = 48 * 1024 * 1024
MESH = pl.DeviceIdType.MESH

_NT = (((1,), (1,)), ((), ()))
_TN = (((0,), (0,)), ((), ()))
_NN = (((1,), (0,)), ((), ()))


def _params(sem=None, vmem_limit_bytes=VMEM_LIMIT_BYTES):
    return pltpu.CompilerParams(vmem_limit_bytes=vmem_limit_bytes, dimension_semantics=sem)


def _pick(n, cap, mult):
    best = None
    for t in range(mult, min(n, cap) + 1, mult):
        if n % t == 0:
            best = t
    return n if best is None else best


_ANY = pl.BlockSpec(memory_space=pl.ANY)


def _window(index, axis, q, n, shape):
    rest = [slice(None)] * len(shape)
    size = shape[axis] // n
    rest[axis] = pl.ds(q * size, size)
    return tuple(index) + tuple(rest)


def _all_gather8(xs, name, local_axis=0, local_chunks=1, relay_axis=None):
    n = len(xs)
    n_sems = 7 if relay_axis is None else 9

    def body(*refs):
        x_refs, out_refs = refs[:n], refs[n:2 * n]
        send_sems, recv_sems, local_sems = refs[2 * n:]
        xi, yi, ci = lax.axis_index("x"), lax.axis_index("y"), lax.axis_index("c")
        me, sibling = (xi, yi, ci), (xi, yi, 1 - ci)
        chips = [(1 - xi, yi), (xi, 1 - yi), (1 - xi, 1 - yi)]

        def slab(w, px, py, pc):
            return out_refs[w].at[4 * px + 2 * py + pc]

        def copy(w, k, block, to, src=None):
            return pltpu.make_async_remote_copy(
                src_ref=slab(w, *block) if src is None else src, dst_ref=slab(w, *block),
                send_sem=send_sems.at[k, w], recv_sem=recv_sems.at[k, w], device_id=to, device_id_type=MESH)

        mine = []
        for w in range(n):
            for q in range(local_chunks):
                part = _window((), local_axis, q, local_chunks, xs[w].shape)
                mine.append(pltpu.make_async_copy(x_refs[w].at[part], slab(w, *me).at[part], local_sems.at[w, q]))
                mine[-1].start()
        direct = chips if relay_axis is None else chips[:2]
        first = [copy(w, 0, me, sibling, src=x_refs[w]) for w in range(n)]
        first += [copy(w, 1 + j, me, (*chip, ci), src=x_refs[w]) for j, chip in enumerate(direct) for w in range(n)]
        for cp in first:
            cp.start()

        def relay(w, part, block, to):
            piece = _window((), relay_axis, part, 2, xs[w].shape)
            return pltpu.make_async_remote_copy(
                src_ref=slab(w, *block).at[piece], dst_ref=slab(w, *block).at[piece],
                send_sem=send_sems.at[7 + part, w], recv_sem=recv_sems.at[7 + part, w],
                device_id=to, device_id_type=MESH)

        passed = []
        for j, chip in enumerate(direct):
            for w in range(n):
                copy(w, 1 + j, (*chip, ci), me).wait_recv()
                passed.append(copy(w, 4 + j, (*chip, ci), sibling))
                passed[-1].start()
                if relay_axis is not None:
                    passed.append(relay(w, j, (*chip, ci), (*chips[1 - j], ci)))
                    passed[-1].start()
        if relay_axis is not None:
            for w in range(n):
                for part in range(2):
                    relay(w, part, (*chips[2], ci), me).wait_recv()
                passed.append(copy(w, 6, (*chips[2], ci), sibling))
                passed[-1].start()
        for w in range(n):
            copy(w, 0, sibling, me).wait_recv()
        for j, chip in enumerate(chips):
            for w in range(n):
                copy(w, 4 + j, (*chip, 1 - ci), me).wait_recv()
        for cp in first + passed:
            cp.wait_send()
        for cp in mine:
            cp.wait()

    return pl.pallas_call(
        body, name=name,
        out_shape=[jax.ShapeDtypeStruct((N_DEV,) + x.shape, x.dtype) for x in xs],
        in_specs=[_ANY] * n, out_specs=[_ANY] * n,
        scratch_shapes=[pltpu.SemaphoreType.DMA((n_sems, n)), pltpu.SemaphoreType.DMA((n_sems, n)),
                        pltpu.SemaphoreType.DMA((n, local_chunks))],
    )(*xs)


def _exchange_cores(xs, name, chunk_axis=0, chunks=1):
    n = len(xs)
    n_peers = 1

    def body(*refs):
        x_refs, out_refs = refs[:n], refs[n:2 * n]
        send_sems, recv_sems = refs[2 * n:]
        xi, yi, ci = lax.axis_index("x"), lax.axis_index("y"), lax.axis_index("c")
        peers = [(1 - ci, (xi, yi, 1 - ci))]
        copies = []
        for k, (p, dev) in enumerate(peers):
            for w in range(n):
                slab_shape = xs[w].shape[1:]
                for q in range(chunks):
                    copies.append(pltpu.make_async_remote_copy(
                        src_ref=x_refs[w].at[_window((p,), chunk_axis, q, chunks, slab_shape)],
                        dst_ref=out_refs[w].at[_window((k,), chunk_axis, q, chunks, slab_shape)],
                        send_sem=send_sems.at[k, w, q], recv_sem=recv_sems.at[k, w, q],
                        device_id=dev, device_id_type=MESH))
                    copies[-1].start()
        for cp in copies:
            cp.wait()

    return pl.pallas_call(
        body, name=name,
        out_shape=[jax.ShapeDtypeStruct((n_peers,) + x.shape[1:], x.dtype) for x in xs],
        in_specs=[_ANY] * n, out_specs=[_ANY] * n,
        scratch_shapes=[pltpu.SemaphoreType.DMA((n_peers, n, chunks)), pltpu.SemaphoreType.DMA((n_peers, n, chunks))],
    )(*xs)


def _exchange_chips(xs, name, relay_axis):
    n = len(xs)

    def half_shape(x):
        shape = list(x.shape[1:])
        shape[relay_axis] //= 2
        return tuple(shape)

    def body(*refs):
        x_refs, out_refs, hop_refs = refs[:n], refs[n:2 * n], refs[2 * n:3 * n]
        send_sems, recv_sems = refs[3 * n:]
        xi, yi, ci = lax.axis_index("x"), lax.axis_index("y"), lax.axis_index("c")
        nbr = [(1 - xi, yi, ci), (xi, 1 - yi, ci)]
        slab_of_nbr = [2 * (1 - xi) + yi, 2 * xi + (1 - yi)]
        slab_of_diag = 2 * (1 - xi) + (1 - yi)

        def copy(k, w, src, dst, to):
            return pltpu.make_async_remote_copy(src_ref=src, dst_ref=dst, send_sem=send_sems.at[k, w],
                                                recv_sem=recv_sems.at[k, w], device_id=to, device_id_type=MESH)

        def piece(w, part):
            return _window((), relay_axis, part, 2, xs[w].shape[1:])

        sent = []
        for w in range(n):
            for j in range(2):
                sent.append(copy(j, w, x_refs[w].at[slab_of_nbr[j]], out_refs[w].at[j], nbr[j]))
                sent.append(copy(2 + j, w, x_refs[w].at[(slab_of_diag,) + piece(w, j)], hop_refs[w].at[j], nbr[j]))
        for cp in sent:
            cp.start()
        for w in range(n):
            for j in range(2):
                copy(2 + j, w, hop_refs[w].at[j], hop_refs[w].at[j], nbr[j]).wait_recv()
                sent.append(copy(4 + j, w, hop_refs[w].at[j], out_refs[w].at[(2,) + piece(w, j)], nbr[1 - j]))
                sent[-1].start()
        for w in range(n):
            for j in range(2):
                copy(j, w, out_refs[w].at[j], out_refs[w].at[j], nbr[j]).wait_recv()
                half = out_refs[w].at[(2,) + piece(w, j)]
                copy(4 + j, w, half, half, nbr[1 - j]).wait_recv()
        for cp in sent:
            cp.wait_send()

    out = pl.pallas_call(
        body, name=name,
        out_shape=[jax.ShapeDtypeStruct((3,) + x.shape[1:], x.dtype) for x in xs]
        + [jax.ShapeDtypeStruct((2,) + half_shape(x), x.dtype) for x in xs],
        in_specs=[_ANY] * n, out_specs=[_ANY] * (2 * n),
        scratch_shapes=[pltpu.SemaphoreType.DMA((6, n)), pltpu.SemaphoreType.DMA((6, n))],
    )(*xs)
    return out[:n]


def _sibling_send(xs, name, chunk_axis=1, chunks=4):
    n = len(xs)

    def body(*refs):
        x_refs, out_refs = refs[:n], refs[n:2 * n]
        send_sems, recv_sems = refs[2 * n:]
        xi, yi, ci = lax.axis_index("x"), lax.axis_index("y"), lax.axis_index("c")
        copies = []
        for w in range(n):
            for q in range(chunks):
                part = _window((), chunk_axis, q, chunks, xs[w].shape)
                copies.append(pltpu.make_async_remote_copy(
                    src_ref=x_refs[w].at[part], dst_ref=out_refs[w].at[part],
                    send_sem=send_sems.at[w, q], recv_sem=recv_sems.at[w, q],
                    device_id=(xi, yi, 1 - ci), device_id_type=MESH))
                copies[-1].start()
        for cp in copies:
            cp.wait()

    return pl.pallas_call(
        body, name=name,
        out_shape=[jax.ShapeDtypeStruct(x.shape, x.dtype) for x in xs],
        in_specs=[_ANY] * n, out_specs=[_ANY] * n,
        scratch_shapes=[pltpu.SemaphoreType.DMA((n, chunks)), pltpu.SemaphoreType.DMA((n, chunks))],
    )(*xs)


def _sum_leading(x, name, own=None, with_bf16=False):
    P, R, C = x.shape
    tr = _pick(R, max(16, (1 << 19) // (C * (P + 1))), 16)

    def body(*refs):
        n_in = 1 if own is None else 2
        x_ref = refs[n_in - 1]
        acc = x_ref[0].astype(F32) if own is None else refs[0][...] + x_ref[0].astype(F32)
        for p in range(1, P):
            acc = acc + x_ref[p].astype(F32)
        refs[n_in][...] = acc
        if with_bf16:
            refs[n_in + 1][...] = acc.astype(BF16)

    flat = pl.BlockSpec((tr, C), lambda r: (r, 0))
    slabs = pl.BlockSpec((P, tr, C), lambda r: (0, r, 0))
    out = pl.pallas_call(
        body, name=name, grid=(R // tr,),
        in_specs=[slabs] if own is None else [flat, slabs],
        out_specs=[flat, flat] if with_bf16 else [flat],
        out_shape=[jax.ShapeDtypeStruct((R, C), F32)] + ([jax.ShapeDtypeStruct((R, C), BF16)] if with_bf16 else []),
        compiler_params=_params(("arbitrary",)),
    )(*([x] if own is None else [own, x]))
    return out if with_bf16 else out[0]


MATMUL_SINGLE_K = 1280
MATMUL_VMEM_BUDGET = 36 * 1024 * 1024


def _matmul(a, b, mode, out_dtype, name, swiglu=False):
    if mode == "nn":
        (M, K), N = a.shape, b.shape[1]
    elif mode == "nt":
        (M, K), N = a.shape, b.shape[0]
    else:
        (K, M), N = a.shape, b.shape[1]
    tm = _pick(M, 1024 if mode != "tn" else 1536, 128)
    tn = _pick(N, 1536, 128)
    if swiglu:
        tm, tn = _pick(M, 1024 if out_dtype == BF16 else 512, 128), 2 * _ff_tile(N // 2)
    out_bytes = jnp.dtype(out_dtype).itemsize
    tk = K
    if K > MATMUL_SINGLE_K:
        for cap in (2048, 1024, 512):
            tk = _pick(K, cap, 128)
            blocks = 2 * 2 * tk * (tm + tn) + tm * tn * (2 * out_bytes + (4 if out_dtype != F32 else 0))
            if blocks <= MATMUL_VMEM_BUDGET:
                break
    nk = K // tk
    dims = {"nn": _NN, "nt": _NT, "tn": _TN}[mode]
    use_scratch = nk > 1 and out_dtype != F32

    def body(a_ref, b_ref, *refs):
        o_ref = refs[0]

        def product():
            return lax.dot_general(a_ref[...].astype(BF16), b_ref[...].astype(BF16), dims,
                                   preferred_element_type=F32)

        if nk == 1:
            part = product()
            o_ref[...] = part.astype(o_ref.dtype)
            if swiglu:
                g = part[:, :tn // 2]
                refs[1][...] = (g * _sigmoid(g) * part[:, tn // 2:]).astype(BF16)
            return
        k = pl.program_id(2)
        acc_ref = refs[-1] if use_scratch else o_ref

        @pl.when(k == 0)
        def _():
            acc_ref[...] = jnp.zeros_like(acc_ref)

        acc_ref[...] += product()

        if use_scratch:
            @pl.when(k == nk - 1)
            def _():
                o_ref[...] = acc_ref[...].astype(o_ref.dtype)

    if mode == "tn":
        a_spec = pl.BlockSpec((tk, tm), lambda i, j, k: (k, i))
    else:
        a_spec = pl.BlockSpec((tm, tk), lambda i, j, k: (i, k))
    if mode == "nt":
        b_spec = pl.BlockSpec((tn, tk), lambda i, j, k: (j, k))
    else:
        b_spec = pl.BlockSpec((tk, tn), lambda i, j, k: (k, j))
    out_specs = [pl.BlockSpec((tm, tn), lambda i, j, k: (i, j))]
    out_shape = [jax.ShapeDtypeStruct((M, N), out_dtype)]
    if swiglu:
        assert nk == 1 and mode == "nn"
        out_specs.append(pl.BlockSpec((tm, tn // 2), lambda i, j, k: (i, j)))
        out_shape.append(jax.ShapeDtypeStruct((M, N // 2), BF16))
    out = pl.pallas_call(
        body, name=name, grid=(M // tm, N // tn, nk),
        in_specs=[a_spec, b_spec], out_specs=out_specs, out_shape=out_shape,
        scratch_shapes=[pltpu.VMEM((tm, tn), F32)] if use_scratch else [],
        compiler_params=_params(("parallel", "parallel", "arbitrary")),
    )(a, b)
    return out if swiglu else out[0]


def _row_tile(S):
    return _pick(S, 512, 8)


def _norm_mod_fwd(x, gain, sc, sh):
    NB, S, D = x.shape
    tr = _row_tile(S)

    def body(x_ref, g_ref, sc_ref, sh_ref, h_ref):
        xv = x_ref[...]
        ms = jnp.mean(xv * xv, axis=-1, keepdims=True)
        n = xv * lax.rsqrt(ms + EPS) * g_ref[...]
        h_ref[...] = (n * (1.0 + sc_ref[...]) + sh_ref[...]).astype(BF16)

    tok = pl.BlockSpec((None, tr, D), lambda b, r: (b, r, 0))
    per_ex = pl.BlockSpec((None, 1, D), lambda b, r: (b, 0, 0))
    return pl.pallas_call(
        body, name="norm_mod_fwd", grid=(NB, S // tr),
        in_specs=[tok, pl.BlockSpec((1, D), lambda b, r: (0, 0)), per_ex, per_ex],
        out_specs=tok, out_shape=jax.ShapeDtypeStruct((NB, S, D), BF16),
        compiler_params=_params(("parallel", "parallel")),
    )(x, gain, sc, sh)


def _norm_mod_bwd(a, w, x, gain, sc, dres, name, y=None, g=None):
    NB, S, D = x.shape
    T, K = a.shape
    tm = _pick(S, 512, 128)
    per_ex_tiles = S // tm
    tk = K if K <= MATMUL_SINGLE_K else _pick(K, 1536, 128)
    nk = K // tk
    gated = y is not None

    def body(*refs):
        a_ref, w_ref, x_ref, g_ref, sc_ref, dres_ref = refs[:6]
        refs = refs[6:]
        if gated:
            y_ref, gate_ref = refs[:2]
            refs = refs[2:]
        dx_ref, dsh_ref, dsc_ref, dgain_ref = refs[:4]
        acc_ref = refs[-1]
        i, k = pl.program_id(0), pl.program_id(1)

        @pl.when(k == 0)
        def _():
            acc_ref[...] = jnp.zeros_like(acc_ref)

        acc_ref[...] += lax.dot_general(a_ref[...], w_ref[...], _NT, preferred_element_type=F32)

        @pl.when(k == nk - 1)
        def _():
            first_of_example = i % per_ex_tiles == 0

            @pl.when(first_of_example)
            def _():
                dsh_ref[...] = jnp.zeros_like(dsh_ref)
                dsc_ref[...] = jnp.zeros_like(dsc_ref)
                if gated:
                    refs[5][...] = jnp.zeros_like(refs[5])

            @pl.when(i == 0)
            def _():
                dgain_ref[...] = jnp.zeros_like(dgain_ref)

            xv = x_ref[...]
            rstd = lax.rsqrt(jnp.mean(xv * xv, axis=-1, keepdims=True) + EPS)
            xh = xv * rstd
            gn = g_ref[...]
            dh = acc_ref[...]
            dsh_ref[...] += jnp.sum(dh, axis=0, keepdims=True)
            dsc_ref[...] += jnp.sum(dh * (xh * gn), axis=0, keepdims=True)
            dn = dh * (1.0 + sc_ref[...])
            dgain_ref[...] += jnp.sum(dn * xh, axis=0, keepdims=True)
            dxh = dn * gn
            proj = jnp.mean(dxh * xh, axis=-1, keepdims=True)
            dx = rstd * (dxh - xh * proj) + dres_ref[...]
            dx_ref[...] = dx
            if gated:
                refs[4][...] = (dx * gate_ref[...]).astype(BF16)
                refs[5][...] += jnp.sum(dx * y_ref[...], axis=0, keepdims=True)

    tok = pl.BlockSpec((None, tm, D), lambda i, k: (i // per_ex_tiles, i % per_ex_tiles, 0))
    per_ex = pl.BlockSpec((None, 1, D), lambda i, k: (i // per_ex_tiles, 0, 0))
    row = pl.BlockSpec((1, D), lambda i, k: (0, 0))
    in_specs = [pl.BlockSpec((tm, tk), lambda i, k: (i, k)), pl.BlockSpec((D, tk), lambda i, k: (0, k)),
                tok, row, per_ex, tok]
    out_specs = [tok, per_ex, per_ex, row]
    out_shape = [jax.ShapeDtypeStruct((NB, S, D), F32), jax.ShapeDtypeStruct((NB, 1, D), F32),
                 jax.ShapeDtypeStruct((NB, 1, D), F32), jax.ShapeDtypeStruct((1, D), F32)]
    operands = [a, w, x, gain, sc, dres]
    if gated:
        in_specs += [tok, per_ex]
        out_specs += [tok, per_ex]
        out_shape += [jax.ShapeDtypeStruct((NB, S, D), BF16), jax.ShapeDtypeStruct((NB, 1, D), F32)]
        operands += [y, g]
    return pl.pallas_call(
        body, name=name, grid=(T // tm, nk), in_specs=in_specs, out_specs=out_specs, out_shape=out_shape,
        scratch_shapes=[pltpu.VMEM((tm, D), F32)],
        compiler_params=_params(("arbitrary", "arbitrary")),
    )(*operands)


def _gate_res(x, y, g, norm=None):
    NB, S, D = x.shape
    tr = _row_tile(S)

    def body(x_ref, y_ref, g_ref, *refs):
        xo = x_ref[...] + g_ref[...] * y_ref[...]
        refs[-1 if norm is None else -2][...] = xo
        if norm is not None:
            gain_ref, sc_ref, sh_ref, _, h_ref = refs
            n = xo * lax.rsqrt(jnp.mean(xo * xo, axis=-1, keepdims=True) + EPS) * gain_ref[...]
            h_ref[...] = (n * (1.0 + sc_ref[...]) + sh_ref[...]).astype(BF16)

    tok = pl.BlockSpec((None, tr, D), lambda b, r: (b, r, 0))
    per_ex = pl.BlockSpec((None, 1, D), lambda b, r: (b, 0, 0))
    in_specs, out_specs, operands = [tok, tok, per_ex], [tok], [x, y, g]
    out_shape = [jax.ShapeDtypeStruct((NB, S, D), F32)]
    if norm is not None:
        in_specs += [pl.BlockSpec((1, D), lambda b, r: (0, 0)), per_ex, per_ex]
        out_specs.append(tok)
        out_shape.append(jax.ShapeDtypeStruct((NB, S, D), BF16))
        operands += list(norm)
    out = pl.pallas_call(
        body, name="gate_res", grid=(NB, S // tr), in_specs=in_specs, out_specs=out_specs, out_shape=out_shape,
        compiler_params=_params(("parallel", "parallel")),
    )(*operands)
    return out[0] if norm is None else out


def _gate_res_bwd(dxo, y, g):
    NB, S, D = dxo.shape
    tr = _row_tile(S)

    def body(d_ref, y_ref, g_ref, dy_ref, dg_ref):
        @pl.when(pl.program_id(1) == 0)
        def _():
            dg_ref[...] = jnp.zeros_like(dg_ref)

        d = d_ref[...]
        dy_ref[...] = (d * g_ref[...]).astype(BF16)
        dg_ref[...] += jnp.sum(d * y_ref[...], axis=0, keepdims=True)

    tok = pl.BlockSpec((None, tr, D), lambda b, r: (b, r, 0))
    per_ex = pl.BlockSpec((None, 1, D), lambda b, r: (b, 0, 0))
    return pl.pallas_call(
        body, name="gate_res_bwd", grid=(NB, S // tr), in_specs=[tok, tok, per_ex], out_specs=[tok, per_ex],
        out_shape=[jax.ShapeDtypeStruct((NB, S, D), BF16), jax.ShapeDtypeStruct((NB, 1, D), F32)],
        compiler_params=_params(("arbitrary", "arbitrary")),
    )(dxo, y, g)


def _sigmoid(v):
    return 1.0 / (1.0 + jnp.exp(-v))


def _ff_tile(F):
    return _pick(F, 1536, 128)


def _interleave(gate, up):
    F = gate.shape[-1]
    tf = _ff_tile(F)
    parts = []
    for j in range(F // tf):
        parts += [gate[..., j * tf:(j + 1) * tf], up[..., j * tf:(j + 1) * tf]]
    return jnp.concatenate(parts, axis=-1)


def _swiglu_bwd(dm, wd, gu):
    T, D = dm.shape
    F = wd.shape[0]
    tf = _ff_tile(F)
    tm = _pick(T, 1024, 128)
    assert D <= MATMUL_SINGLE_K

    def body(a_ref, b_ref, gu_ref, o_ref):
        d = lax.dot_general(a_ref[...], b_ref[...], _NT, preferred_element_type=F32)
        g, u = gu_ref[:, :tf].astype(F32), gu_ref[:, tf:].astype(F32)
        s = _sigmoid(g)
        o_ref[:, :tf] = (d * u * (s * (1.0 + g * (1.0 - s)))).astype(BF16)
        o_ref[:, tf:] = (d * (g * s)).astype(BF16)

    return pl.pallas_call(
        body, name="swiglu_bwd", grid=(T // tm, F // tf),
        in_specs=[pl.BlockSpec((tm, D), lambda i, j: (i, 0)), pl.BlockSpec((tf, D), lambda i, j: (j, 0)),
                  pl.BlockSpec((tm, 2 * tf), lambda i, j: (i, j))],
        out_specs=pl.BlockSpec((tm, 2 * tf), lambda i, j: (i, j)),
        out_shape=jax.ShapeDtypeStruct((T, 2 * F), BF16),
        compiler_params=_params(("parallel", "parallel")),
    )(dm, wd, gu)


def _loss_fwd_bwd(y, target):
    NB, S, D = y.shape
    tr = _row_tile(S)

    def body(y_ref, t_ref, l_ref, d_ref):
        @pl.when((pl.program_id(0) == 0) & (pl.program_id(1) == 0))
        def _():
            l_ref[...] = jnp.zeros_like(l_ref)

        e = y_ref[...] - t_ref[...]
        d_ref[...] = e / D
        l_ref[...] += 0.5 * jnp.sum(jnp.mean(e * e, axis=-1, keepdims=True), axis=0, keepdims=True)

    tok = pl.BlockSpec((None, tr, D), lambda b, r: (b, r, 0))
    return pl.pallas_call(
        body, name="loss", grid=(NB, S // tr), in_specs=[tok, tok],
        out_specs=[pl.BlockSpec((1, 1), lambda b, r: (0, 0)), tok],
        out_shape=[jax.ShapeDtypeStruct((1, 1), F32), jax.ShapeDtypeStruct((NB, S, D), F32)],
        compiler_params=_params(("arbitrary", "arbitrary")),
    )(y, target)


def _half_sums(v, lo):
    sa = jnp.sum(jnp.where(lo, v, 0.0), axis=-1, keepdims=True)
    sb = jnp.sum(jnp.where(lo, 0.0, v), axis=-1, keepdims=True)
    return jnp.where(lo, sa, sb)


def _rope_swap(v, lane64):
    up = pltpu.roll(v, LANES - ROT_DIM // 2, 1)
    down = pltpu.roll(v, ROT_DIM // 2, 1)
    return jnp.where(lane64 < ROT_DIM // 2, up, jnp.where(lane64 < ROT_DIM, down, 0.0))


def _qk_prep_fwd(qkv, tab_c, tab_s, gains):
    T, W = qkv.shape
    R = W // LANES
    tt = _pick(T, 256, 8)

    def body(x_ref, c_ref, s_ref, g_ref, o_ref):
        lane = lax.broadcasted_iota(jnp.int32, (tt, LANES), 1)
        lo = lane < HEAD_DIM
        lane64 = lane & (HEAD_DIM - 1)
        c, s = c_ref[...], s_ref[...]
        for j in range(R - 1):
            cols = slice(j * LANES, (j + 1) * LANES)
            xv = x_ref[:, cols]
            rstd = lax.rsqrt(_half_sums(xv * xv, lo) / HEAD_DIM + EPS)
            yn = xv * rstd * g_ref[j:j + 1, :]
            o_ref[:, cols] = (yn * c + _rope_swap(yn, lane64) * s).astype(BF16)
        o_ref[:, (R - 1) * LANES:] = x_ref[:, (R - 1) * LANES:].astype(BF16)

    tok = pl.BlockSpec((tt, W), lambda t: (t, 0))
    tab = pl.BlockSpec((tt, LANES), lambda t: (t, 0))
    return pl.pallas_call(
        body, name="qk_prep_fwd", grid=(T // tt,),
        in_specs=[tok, tab, tab, pl.BlockSpec((R, LANES), lambda t: (0, 0))],
        out_specs=tok, out_shape=jax.ShapeDtypeStruct((T, W), BF16),
        compiler_params=_params(("parallel",)),
    )(qkv, tab_c, tab_s, gains)


def _qk_prep_bwd(qkv, dq, dk, dv, tab_c, tab_s, gains):
    T, W = qkv.shape
    R = W // LANES
    QW = dq.shape[1]
    tt = _pick(T, 256, 8)

    def body(x_ref, dq_ref, dk_ref, dv_ref, c_ref, s_ref, g_ref, o_ref, dg_ref):
        @pl.when(pl.program_id(0) == 0)
        def _():
            dg_ref[...] = jnp.zeros_like(dg_ref)

        lane = lax.broadcasted_iota(jnp.int32, (tt, LANES), 1)
        lo = lane < HEAD_DIM
        lane64 = lane & (HEAD_DIM - 1)
        c, s = c_ref[...], s_ref[...]
        for j in range(R - 1):
            cols = slice(j * LANES, (j + 1) * LANES)
            xv = x_ref[:, cols]
            d = dq_ref[:, cols] if j < R - 2 else dk_ref[...]
            rstd = lax.rsqrt(_half_sums(xv * xv, lo) / HEAD_DIM + EPS)
            xh = xv * rstd
            dyn = d * c + _rope_swap(d * s, lane64)
            dg_ref[j:j + 1, :] += jnp.sum(dyn * xh, axis=0, keepdims=True)
            dxh = dyn * g_ref[j:j + 1, :]
            proj = _half_sums(dxh * xh, lo) / HEAD_DIM
            o_ref[:, cols] = (rstd * (dxh - xh * proj)).astype(BF16)
        o_ref[:, (R - 1) * LANES:] = dv_ref[...].astype(BF16)

    tok = pl.BlockSpec((tt, W), lambda t: (t, 0))
    tab = pl.BlockSpec((tt, LANES), lambda t: (t, 0))
    gsp = pl.BlockSpec((R, LANES), lambda t: (0, 0))
    return pl.pallas_call(
        body, name="qk_prep_bwd", grid=(T // tt,),
        in_specs=[tok, pl.BlockSpec((tt, QW), lambda t: (t, 0)), tab, tab, tab, tab, gsp], out_specs=[tok, gsp],
        out_shape=[jax.ShapeDtypeStruct((T, W), BF16), jax.ShapeDtypeStruct((R, LANES), F32)],
        compiler_params=_params(("arbitrary",)),
    )(qkv, dq, dk, dv, tab_c, tab_s, gains)


def _band_mask(i):
    r = lax.broadcasted_iota(jnp.int32, (2 * BLOCK, 2 * BLOCK), 0) & (BLOCK - 1)
    c = lax.broadcasted_iota(jnp.int32, (2 * BLOCK, 2 * BLOCK), 1)
    rel = r + BLOCK - c
    return (rel >= 0) & (rel < BLOCK) & ((c >= BLOCK) | (i > 0))


def _swa_softmax(s, valid, sink):
    s = jnp.where(valid, s * ATTN_SCALE, NEG_BIG)
    m = jnp.maximum(jnp.max(s, axis=1, keepdims=True), sink)
    p = jnp.exp(s - m)
    ps = jnp.exp(sink - m)
    denom = jnp.sum(p, axis=1, keepdims=True) + ps
    return p / denom, ps / denom


A_GROUP = 4


Q_WIDTH_A = N_Q_A * HEAD_DIM
N_PAIR_A = Q_WIDTH_A // LANES


def _swa_specs():
    qs = pl.BlockSpec((None, BLOCK, Q_WIDTH_A), lambda b, i: (b, i, 0))

    def kv(col, back):
        return pl.BlockSpec((None, BLOCK, LANES), lambda b, i: (b, jnp.maximum(i - back, 0), col))

    return qs, kv(N_PAIR_A, 1), kv(N_PAIR_A, 0), kv(N_PAIR_A + 1, 1), kv(N_PAIR_A + 1, 0)


def _dup_heads(t):
    lo = lax.broadcasted_iota(jnp.int32, t.shape, 1) < HEAD_DIM
    sw = pltpu.roll(t.astype(F32), HEAD_DIM, 1).astype(BF16)
    return jnp.where(lo, t, sw), jnp.where(lo, sw, t)


def _kv_tiles(kp_ref, kc_ref, vp_ref, vc_ref):
    kd = _dup_heads(jnp.concatenate([kp_ref[...], kc_ref[...]], axis=0))
    vd = _dup_heads(jnp.concatenate([vp_ref[...], vc_ref[...]], axis=0))
    return kd, vd


def _attn_a_fwd(qkn, sinks):
    NB, S, _ = qkn.shape
    qs, kp, kc, vp, vc = _swa_specs()

    def body(q_ref, kp_ref, kc_ref, vp_ref, vc_ref, sink_ref, o_ref):
        i = pl.program_id(1)
        kd, vd = _kv_tiles(kp_ref, kc_ref, vp_ref, vc_ref)
        valid = _band_mask(i)
        lo = lax.broadcasted_iota(jnp.int32, (BLOCK, LANES), 1) < HEAD_DIM
        top = lax.broadcasted_iota(jnp.int32, (2 * BLOCK, 1), 0) < BLOCK
        for first in range(0, N_PAIR_A, A_GROUP):
            pairs = range(first, first + A_GROUP)
            qs_ = [jnp.concatenate(_head_halves(q_ref[:, p * LANES:(p + 1) * LANES], lo), axis=0) for p in pairs]
            ss = [lax.dot_general(q, kd[2 * p // GROUP_A], _NT, preferred_element_type=F32) for q, p in zip(qs_, pairs)]
            pns = [_swa_softmax(s, valid, jnp.where(top, sink_ref[2 * p], sink_ref[2 * p + 1]))[0]
                   for s, p in zip(ss, pairs)]
            pvs = [jnp.dot(pn.astype(BF16), vd[2 * p // GROUP_A], preferred_element_type=F32) for pn, p in zip(pns, pairs)]
            for pv, p in zip(pvs, pairs):
                o_ref[:, p * LANES:(p + 1) * LANES] = jnp.where(lo, pv[:BLOCK], pv[BLOCK:]).astype(BF16)

    return pl.pallas_call(
        body, name="attn_a_fwd", grid=(NB, S // BLOCK),
        in_specs=[qs, kp, kc, vp, vc, pl.BlockSpec(memory_space=pltpu.SMEM)],
        out_specs=qs, out_shape=jax.ShapeDtypeStruct((NB, S, Q_WIDTH_A), BF16),
        compiler_params=_params(("parallel", "arbitrary")),
    )(qkn, qkn, qkn, qkn, qkn, sinks)


def _attn_a_bwd(qkn, do, sinks):
    NB, S, _ = qkn.shape
    qs, kp, kc, vp, vc = _swa_specs()
    full = pl.BlockSpec((None, S, LANES), lambda b, i: (b, 0, 0))
    sink_out = pl.BlockSpec((None, N_Q_A, LANES), lambda b, i: (b, 0, 0))

    def body(q_ref, do_ref, kp_ref, kc_ref, vp_ref, vc_ref, sink_ref, dq_ref, dk_ref, dv_ref, ds_ref, dk_s, dv_s):
        i = pl.program_id(1)

        @pl.when(i == 0)
        def _():
            dk_ref[...] = jnp.zeros_like(dk_ref)
            dv_ref[...] = jnp.zeros_like(dv_ref)
            ds_ref[...] = jnp.zeros_like(ds_ref)

        dk_s[...] = jnp.zeros_like(dk_s)
        dv_s[...] = jnp.zeros_like(dv_s)
        kd, vd = _kv_tiles(kp_ref, kc_ref, vp_ref, vc_ref)
        valid = _band_mask(i)
        lo = lax.broadcasted_iota(jnp.int32, (BLOCK, LANES), 1) < HEAD_DIM
        top = lax.broadcasted_iota(jnp.int32, (2 * BLOCK, 1), 0) < BLOCK
        for first in range(0, N_PAIR_A, A_GROUP):
            pairs = range(first, first + A_GROUP)
            kvs = [2 * p // GROUP_A for p in pairs]
            qs_ = [jnp.concatenate(_head_halves(q_ref[:, p * LANES:(p + 1) * LANES], lo), axis=0) for p in pairs]
            dos = [jnp.concatenate(_head_halves(do_ref[:, p * LANES:(p + 1) * LANES], lo), axis=0) for p in pairs]
            ss = [lax.dot_general(q, kd[kv], _NT, preferred_element_type=F32) for q, kv in zip(qs_, kvs)]
            dps = [lax.dot_general(d, vd[kv], _NT, preferred_element_type=F32) for d, kv in zip(dos, kvs)]
            sm = [_swa_softmax(s, valid, jnp.where(top, sink_ref[2 * p], sink_ref[2 * p + 1])) for s, p in zip(ss, pairs)]
            deltas = [jnp.sum(pn * dp, axis=1, keepdims=True) for (pn, _), dp in zip(sm, dps)]
            dsbs = [(pn * (dp - delta) * ATTN_SCALE).astype(BF16) for (pn, _), dp, delta in zip(sm, dps, deltas)]
            for n, p in enumerate(pairs):
                dq2 = jnp.dot(dsbs[n], kd[kvs[n]], preferred_element_type=F32)
                dq_ref[:, p * LANES:(p + 1) * LANES] = jnp.where(lo, dq2[:BLOCK], dq2[BLOCK:])
                dk_s[kvs[n]] += lax.dot_general(dsbs[n], qs_[n], _TN, preferred_element_type=F32)
                dv_s[kvs[n]] += lax.dot_general(sm[n][0].astype(BF16), dos[n], _TN, preferred_element_type=F32)
                t = sm[n][1] * deltas[n]
                for hh in range(2):
                    dsink = -jnp.sum(t[hh * BLOCK:(hh + 1) * BLOCK], axis=0, keepdims=True)
                    ds_ref[2 * p + hh:2 * p + hh + 1, :] += jnp.broadcast_to(dsink, (1, LANES))

        lo2 = lax.broadcasted_iota(jnp.int32, (2 * BLOCK, LANES), 1) < HEAD_DIM

        def fold(acc):
            halves = [acc[kv] + pltpu.roll(acc[kv], HEAD_DIM, 1) for kv in range(N_KV_A)]
            return jnp.where(lo2, halves[0], halves[1])

        dk2, dv2 = fold(dk_s), fold(dv_s)

        @pl.when(i > 0)
        def _():
            start = pl.multiple_of((i - 1) * BLOCK, BLOCK)
            dk_ref[pl.ds(start, 2 * BLOCK), :] += dk2
            dv_ref[pl.ds(start, 2 * BLOCK), :] += dv2

        @pl.when(i == 0)
        def _():
            dk_ref[0:BLOCK, :] += dk2[BLOCK:, :]
            dv_ref[0:BLOCK, :] += dv2[BLOCK:, :]

    slots = pltpu.VMEM((N_KV_A, 2 * BLOCK, LANES), F32)
    return pl.pallas_call(
        body, name="attn_a_bwd", grid=(NB, S // BLOCK),
        in_specs=[qs, qs, kp, kc, vp, vc, pl.BlockSpec(memory_space=pltpu.SMEM)],
        out_specs=[qs, full, full, sink_out],
        out_shape=[jax.ShapeDtypeStruct((NB, S, Q_WIDTH_A), F32), jax.ShapeDtypeStruct((NB, S, LANES), F32),
                   jax.ShapeDtypeStruct((NB, S, LANES), F32), jax.ShapeDtypeStruct((NB, N_Q_A, LANES), F32)],
        scratch_shapes=[slots, slots],
        compiler_params=_params(("parallel", "arbitrary")),
    )(qkn, do, qkn, qkn, qkn, qkn, sinks)


def _cumsum_mats():
    src = lax.broadcasted_iota(jnp.int32, (2 * BLOCK, 2 * BLOCK), 0) % BLOCK
    dst = lax.broadcasted_iota(jnp.int32, (2 * BLOCK, 2 * BLOCK), 1)
    ones = dst >= BLOCK
    rev = ((src > dst) | ones).astype(BF16)
    fwd = ((src < dst) | ones).astype(BF16)
    return rev, fwd


def _log_sigmoids(z):
    sp = jnp.log(1.0 + jnp.exp(-jnp.abs(z)))
    return jnp.minimum(z, 0.0) - sp, -(jnp.maximum(z, 0.0) + sp)


def _cumsum_mxu_many(vs, mat):
    parts = []
    for v in vs:
        hi = v.astype(BF16)
        parts.append(jnp.concatenate([hi, (v - hi.astype(F32)).astype(BF16)], axis=1))
    r = jnp.dot(jnp.concatenate(parts, axis=0), mat, preferred_element_type=F32)
    return [(r[n * BLOCK:(n + 1) * BLOCK, :BLOCK], r[n * BLOCK:(n + 1) * BLOCK, BLOCK:]) for n in range(len(vs))]


def _strict_mask():
    r = lax.broadcasted_iota(jnp.int32, (BLOCK, BLOCK), 0)
    c = lax.broadcasted_iota(jnp.int32, (BLOCK, BLOCK), 1)
    return c < r


def _tile(ref, j):
    return ref[pl.ds(pl.multiple_of(j * BLOCK, BLOCK), BLOCK), :]


SWEEP_EXIT = -88.0


def _head_halves(t, lo):
    zero = jnp.zeros_like(t)
    return jnp.where(lo, t, zero), jnp.where(lo, zero, t)


def _sb_specs(S, HD, width):
    n = HD // width
    blk = pl.BlockSpec((None, BLOCK, width), lambda b, p, i: (b, i, p))
    k_full = pl.BlockSpec((None, S, width), lambda b, p, i: (b, 0, n + p))
    v_full = pl.BlockSpec((None, S, width), lambda b, p, i: (b, 0, 2 * n + p))
    mat = pl.BlockSpec((2 * BLOCK, 2 * BLOCK), lambda b, p, i: (0, 0))
    return blk, k_full, v_full, mat


SB_FWD_PAIRS = 4
SB_BWD_PAIRS = 2
SB_BWD_TILES = 2
SB_BWD_VMEM_LIMIT_BYTES = 58 * 1024 * 1024


def _attn_b_fwd(qkv, rev):
    NB, S, W = qkv.shape
    HD = W // 3
    width = SB_FWD_PAIRS * LANES
    n_heads = 2 * SB_FWD_PAIRS
    blk, k_full, v_full, mat = _sb_specs(S, HD, width)

    def body(q_ref, k_ref, v_ref, rev_ref, o_ref):
        i = pl.program_id(2)
        rv = rev_ref[...]
        mask = _strict_mask()
        lo = lax.broadcasted_iota(jnp.int32, (BLOCK, LANES), 1) < HEAD_DIM
        q_all = q_ref[...]
        q_stack = [jnp.concatenate(_head_halves(q_all[:, p * LANES:(p + 1) * LANES] * ATTN_SCALE, lo), axis=0)
                   for p in range(SB_FWD_PAIRS)]

        def pair_tiles(ref, j):
            t = _tile(ref, j)
            return [t[:, p * LANES:(p + 1) * LANES] for p in range(SB_FWD_PAIRS)]

        def tiles_pass(js, carries, diagonal_first, last_counts=None):
            zs, v_tiles = [], []
            for j in js:
                ks = pair_tiles(k_ref, j)
                v_tiles.append(pair_tiles(v_ref, j))
                for p in range(SB_FWD_PAIRS):
                    z2 = lax.dot_general(q_stack[p], ks[p], _NT, preferred_element_type=F32)
                    zs += [z2[:BLOCK], z2[BLOCK:]]
            logs = [_log_sigmoids(z) for z in zs]
            masked = [jnp.where(mask, lm, 0.0) if diagonal_first and n < n_heads else lm
                      for n, (_, lm) in enumerate(logs)]
            cums = _cumsum_mxu_many(masked, rv)
            probs, new_c = {}, []
            for h in range(n_heads):
                carry = None if diagonal_first else carries[h]
                for t in range(len(js)):
                    n = t * n_heads + h
                    after, rs = cums[n]
                    if diagonal_first and t == 0:
                        a = jnp.where(mask, jnp.exp(logs[n][0] + after), 0.0)
                        carry = rs
                    else:
                        a = jnp.exp(logs[n][0] + after + carry)
                        carry = carry + rs
                    if last_counts is not None and t == len(js) - 1:
                        a = jnp.where(last_counts, a, 0.0)
                    probs[t, h] = a.astype(BF16)
                new_c.append(carry)
            outs = []
            for p in range(SB_FWD_PAIRS):
                total = None
                for t in range(len(js)):
                    pv = jnp.dot(jnp.concatenate([probs[t, 2 * p], probs[t, 2 * p + 1]], axis=0), v_tiles[t][p],
                                 preferred_element_type=F32)
                    part = jnp.where(lo, pv[:BLOCK], pv[BLOCK:])
                    total = part if total is None else total + part
                outs.append(total)
            return new_c, outs

        carries, accs = tiles_pass([i, jnp.maximum(i - 1, 0)], None, True, last_counts=i > 0)

        def live(cs):
            top = cs[0]
            for c in cs[1:]:
                top = jnp.maximum(top, c)
            return jnp.max(top) > SWEEP_EXIT

        def cond(st):
            return (st[0] < i - 1) & st[1]

        def step(st):
            jj, _, cs, accs = st
            new_c, outs = tiles_pass([i - 2 - jj], cs, False)
            return jj + 1, live(new_c), new_c, [acc + o for acc, o in zip(accs, outs)]

        st = lax.while_loop(cond, step, (jnp.int32(0), live(carries), carries, accs))
        for p in range(SB_FWD_PAIRS):
            o_ref[:, p * LANES:(p + 1) * LANES] = st[3][p].astype(BF16)

    return pl.pallas_call(
        body, name="attn_b_fwd", grid=(NB, HD // width, S // BLOCK),
        in_specs=[blk, k_full, v_full, mat], out_specs=blk,
        out_shape=jax.ShapeDtypeStruct((NB, S, HD), BF16),
        compiler_params=_params(("parallel", "parallel", "arbitrary")),
    )(qkv, qkv, qkv, rev)


def _attn_b_bwd(qkv, do, rev, fwd):
    NB, S, W = qkv.shape
    HD = W // 3
    width = SB_BWD_PAIRS * LANES
    n_heads = 2 * SB_BWD_PAIRS
    nj = S // BLOCK
    blk, k_full, v_full, mat = _sb_specs(S, HD, width)
    acc_full = pl.BlockSpec((None, S, width), lambda b, p, i: (b, 0, p))

    def body(q_ref, do_ref, k_ref, v_ref, rev_ref, fwd_ref, dq_ref, dk_ref, dv_ref, sig_s, a_s, e_s):
        i = pl.program_id(2)

        @pl.when(i == 0)
        def _():
            dk_ref[...] = jnp.zeros_like(dk_ref)
            dv_ref[...] = jnp.zeros_like(dv_ref)

        rv, fw = rev_ref[...], fwd_ref[...]
        mask = _strict_mask()
        lo = lax.broadcasted_iota(jnp.int32, (BLOCK, LANES), 1) < HEAD_DIM
        pairs = range(SB_BWD_PAIRS)

        def cols(p):
            return slice(p * LANES, (p + 1) * LANES)

        q_stack = [jnp.concatenate(_head_halves(q_ref[:, cols(p)], lo), axis=0) for p in pairs]
        qs_stack = [q * ATTN_SCALE for q in q_stack]
        do_stack = [jnp.concatenate(_head_halves(do_ref[:, cols(p)], lo), axis=0) for p in pairs]

        def sweep1_tiles(js, carries, diagonal_first):
            zs, das = [], []
            for j in js:
                kj, vj = _tile(k_ref, j), _tile(v_ref, j)
                for p in pairs:
                    z2 = lax.dot_general(qs_stack[p], kj[:, cols(p)], _NT, preferred_element_type=F32)
                    da2 = lax.dot_general(do_stack[p], vj[:, cols(p)], _NT, preferred_element_type=F32)
                    zs += [z2[:BLOCK], z2[BLOCK:]]
                    das += [da2[:BLOCK], da2[BLOCK:]]
            logs = [_log_sigmoids(z) for z in zs]
            cums = _cumsum_mxu_many([jnp.where(mask, lm, 0.0) if diagonal_first and n < n_heads else lm
                                     for n, (_, lm) in enumerate(logs)], rv)
            new_c, stores = [], []
            for h in range(n_heads):
                carry = None if diagonal_first else carries[h]
                for t, j in enumerate(js):
                    n = t * n_heads + h
                    lb, (after, rs) = logs[n][0], cums[n]
                    if diagonal_first and t == 0:
                        a = jnp.where(mask, jnp.exp(lb + after), 0.0)
                        carry = rs
                    else:
                        a = jnp.exp(lb + after + carry)
                        carry = carry + rs
                    stores.append((t, h, j, jnp.exp(lb), a.astype(BF16), das[n] * a))
                new_c.append(carry)
            for t, h, j, sg, ab, e in sorted(stores, key=lambda s: -s[0]):
                sig_s[h, j] = sg
                a_s[h, j] = ab
                e_s[h, j] = e
            return new_c

        carries = sweep1_tiles([jnp.maximum(i - t, 0) for t in range(SB_BWD_TILES + 1)], None, True)
        done = SB_BWD_TILES

        def live(cs):
            top = cs[0]
            for c in cs[1:]:
                top = jnp.maximum(top, c)
            return jnp.max(top) > SWEEP_EXIT

        def cond(st):
            return (done + SB_BWD_TILES * st[0] < i) & st[1]

        def sweep1(st):
            first = i - 1 - done - SB_BWD_TILES * st[0]
            new_c = sweep1_tiles([jnp.maximum(first - t, 0) for t in range(SB_BWD_TILES)], st[2], False)
            return st[0] + 1, live(new_c), new_c

        trips = lax.while_loop(cond, sweep1, (jnp.int32(0), live(carries), carries))[0]
        lowest = jnp.maximum(i - done - SB_BWD_TILES * trips, 0)

        def grads(js, st, diagonal_last=False, counts=None):
            prefixes, dqs = st
            counts = counts or [None] * len(js)
            es = [e_s[h, j] for j in js for h in range(n_heads)]
            cums = _cumsum_mxu_many(es, fw)
            dzs, new_p = [], []
            for h in range(n_heads):
                prefix = prefixes[h]
                for t, j in enumerate(js):
                    n = t * n_heads + h
                    sg = sig_s[h, j]
                    e_before, rs = cums[n]
                    dz = (es[n] * (1.0 - sg) - (e_before + prefix) * sg) * ATTN_SCALE
                    if diagonal_last and t == len(js) - 1:
                        dz = jnp.where(mask, dz, 0.0)
                    if counts[t] is not None:
                        dz = jnp.where(counts[t], dz, 0.0)
                        rs = jnp.where(counts[t], rs, 0.0)
                    dzs.append((t, h, dz.astype(BF16)))
                    prefix = prefix + rs
                new_p.append(prefix)
            dz_of = {(t, h): dz for t, h, dz in dzs}
            new_dq = list(dqs)
            for t, j in enumerate(js):
                kj = _tile(k_ref, j)
                rows = pl.ds(pl.multiple_of(j * BLOCK, BLOCK), BLOCK)
                for p in pairs:
                    dz_stack = jnp.concatenate([dz_of[t, 2 * p], dz_of[t, 2 * p + 1]], axis=0)
                    a_stack = jnp.concatenate([a_s[2 * p, j], a_s[2 * p + 1, j]], axis=0)
                    if counts[t] is not None:
                        a_stack = jnp.where(counts[t], a_stack, jnp.zeros_like(a_stack))
                    dq2 = jnp.dot(dz_stack, kj[:, cols(p)], preferred_element_type=F32)
                    new_dq[p] = new_dq[p] + jnp.where(lo, dq2[:BLOCK], dq2[BLOCK:])
                    dk_ref[rows, cols(p)] += lax.dot_general(dz_stack, q_stack[p], _TN, preferred_element_type=F32)
                    dv_ref[rows, cols(p)] += lax.dot_general(a_stack, do_stack[p], _TN, preferred_element_type=F32)
            return new_p, new_dq

        zeros = jnp.zeros((BLOCK, BLOCK), F32)
        st = ([zeros] * n_heads, [zeros] * SB_BWD_PAIRS)
        count = jnp.maximum(i - done, 0) - lowest
        st = lax.fori_loop(0, count % SB_BWD_TILES, lambda t, st: grads([lowest + t], st), st)
        start = lowest + count % SB_BWD_TILES
        st = lax.fori_loop(0, count // SB_BWD_TILES,
                           lambda t, st: grads([start + SB_BWD_TILES * t + u for u in range(SB_BWD_TILES)], st), st)
        top = [jnp.maximum(i - t, 0) for t in range(SB_BWD_TILES, -1, -1)]
        dqs = grads(top, st, diagonal_last=True, counts=[i >= t for t in range(SB_BWD_TILES, 0, -1)] + [None])[1]
        for p in pairs:
            dq_ref[:, cols(p)] = dqs[p]

    f32_stash = pltpu.VMEM((n_heads, nj, BLOCK, BLOCK), F32)
    bf16_stash = pltpu.VMEM((n_heads, nj, BLOCK, BLOCK), BF16)
    return pl.pallas_call(
        body, name="attn_b_bwd", grid=(NB, HD // width, nj),
        in_specs=[blk, blk, k_full, v_full, mat, mat], out_specs=[blk, acc_full, acc_full],
        out_shape=[jax.ShapeDtypeStruct((NB, S, HD), F32)] * 3,
        scratch_shapes=[f32_stash, bf16_stash, f32_stash],
        compiler_params=_params(("parallel", "parallel", "arbitrary"), SB_BWD_VMEM_LIMIT_BYTES),
    )(qkv, do, qkv, qkv, rev, fwd)


def _ada_fwd(c_all, w, b):
    L, D, N = w.shape
    B = c_all.shape[0]

    def body(c_ref, w_ref, b_ref, o_ref):
        cv = c_ref[...]
        cond = (cv * _sigmoid(cv)).astype(BF16)
        o_ref[...] = jnp.dot(cond, w_ref[...].astype(BF16), preferred_element_type=F32) + b_ref[...]

    return pl.pallas_call(
        body, name="ada_fwd", grid=(L,),
        in_specs=[pl.BlockSpec((B, D), lambda l: (0, 0)), pl.BlockSpec((None, D, N), lambda l: (l, 0, 0)),
                  pl.BlockSpec((None, 1, N), lambda l: (l, 0, 0))],
        out_specs=pl.BlockSpec((None, B, N), lambda l: (l, 0, 0)),
        out_shape=jax.ShapeDtypeStruct((L, B, N), F32),
        compiler_params=_params(("parallel",)),
    )(c_all, w, b)


def _ada_bwd(c_all, dmod_all, dmod_shard):
    L, B, N = dmod_shard.shape
    D = c_all.shape[1]
    N_all = dmod_all.shape[2]

    def body(c_ref, da_ref, ds_ref, gw_ref, gb_ref):
        cv = c_ref[...]
        cond = (cv * _sigmoid(cv)).astype(BF16)
        gw_ref[...] = lax.dot_general(cond, ds_ref[...].astype(BF16), _TN, preferred_element_type=F32)
        gb_ref[...] = jnp.sum(da_ref[...], axis=0, keepdims=True)

    return pl.pallas_call(
        body, name="ada_bwd", grid=(L,),
        in_specs=[pl.BlockSpec((B, D), lambda l: (0, 0)), pl.BlockSpec((None, B, N_all), lambda l: (l, 0, 0)),
                  pl.BlockSpec((None, B, N), lambda l: (l, 0, 0))],
        out_specs=[pl.BlockSpec((None, D, N), lambda l: (l, 0, 0)), pl.BlockSpec((None, 1, N_all), lambda l: (l, 0, 0))],
        out_shape=[jax.ShapeDtypeStruct((L, D, N), F32), jax.ShapeDtypeStruct((L, 1, N_all), F32)],
        compiler_params=_params(("parallel",)),
    )(c_all, dmod_all, dmod_shard)


def _adamw(w, g, m, v, name):
    shape = w.shape
    if w.ndim == 2:
        w, g, m, v = [t.reshape((1,) + shape) for t in (w, g, m, v)]
    L, R, C = w.shape
    tr = _pick(R, max(8, (1 << 18) // C), 8)
    c1 = 1.0 - ADAM_B1 ** ADAM_STEP
    c2 = 1.0 - ADAM_B2 ** ADAM_STEP

    def body(w_ref, g_ref, m_ref, v_ref, d_ref, nm_ref, nv_ref):
        gv = g_ref[...]
        nm = ADAM_B1 * m_ref[...] + (1.0 - ADAM_B1) * gv
        nv = ADAM_B2 * v_ref[...] + (1.0 - ADAM_B2) * (gv * gv)
        d_ref[...] = -ADAM_LR * ((nm / c1) / (jnp.sqrt(nv / c2) + ADAM_EPS) + ADAM_WD * w_ref[...])
        nm_ref[...] = nm
        nv_ref[...] = nv

    spec = pl.BlockSpec((None, tr, C), lambda l, r: (l, r, 0))
    out = pl.pallas_call(
        body, name=name, grid=(L, R // tr), in_specs=[spec] * 4, out_specs=[spec] * 3,
        out_shape=[jax.ShapeDtypeStruct((L, R, C), F32)] * 3,
        compiler_params=_params(("parallel", "parallel")),
    )(w, g, m, v)
    return [t.reshape(shape) for t in out]


_SHARDED = (("wqkv_a", 2), ("wo_a", 1), ("wqkv_b", 2), ("wo_b", 1), ("w_gate", 2), ("w_up", 2), ("w_down", 1))


def _pack_full(layers, axis, gate_up=None):
    L = len(layers)
    R, C = layers[0].shape

    def shards(m):
        if gate_up is not None:
            F = C // 2
            tf, Cs = _ff_tile(F), F // 4
            assert tf % Cs == 0
            starts = [(2 * (s * Cs // tf) + gate_up) * tf + s * Cs % tf for s in range(4)]
            return jnp.stack([m[:, st:st + Cs] for st in starts])
        if axis == 2:
            return m.reshape(R, 4, C // 4).transpose(1, 0, 2)
        return m.reshape(4, R // 4, C)

    halves = [jnp.stack([shards(m) for m in layers[h * (L // 2):(h + 1) * (L // 2)]], axis=1) for h in range(2)]
    return jnp.stack(halves)


def _unpack_full(gathered, axis):
    _, Lh, Rs, Cs = gathered.shape
    t = gathered.reshape(4, 2, Lh, Rs, Cs)
    layers = []
    for h in range(2):
        for l in range(Lh):
            piece = t[:, h, l]
            if axis == 2:
                layers.append(piece.transpose(1, 0, 2).reshape(Rs, 4 * Cs))
            else:
                layers.append(piece.reshape(4 * Rs, Cs))
    return layers


def _sum_slabs(own, recv, name, with_bf16=False):
    C = own.shape[-1]
    out = _sum_leading(recv.reshape(recv.shape[0], -1, C), name, own=own.reshape(-1, C), with_bf16=with_bf16)
    if with_bf16:
        return out[0].reshape(own.shape), out[1].reshape(own.shape)
    return out.reshape(own.shape)


def _gather8(x, name):
    return _all_gather8([x], name)[0]


def _rope_tables(positions):
    half = ROT_DIM // 2
    inv_freq = jnp.power(jnp.float32(ROPE_THETA), -jnp.arange(half, dtype=F32) * 2.0 / ROT_DIM)
    ang = positions.astype(F32).reshape(-1, 1) * inv_freq
    cos, sin = jnp.cos(ang), jnp.sin(ang)
    T = ang.shape[0]
    rest = HEAD_DIM - ROT_DIM
    c64 = jnp.concatenate([cos, cos, jnp.ones((T, rest), F32)], axis=1)
    s64 = jnp.concatenate([-sin, sin, jnp.zeros((T, rest), F32)], axis=1)
    return jnp.tile(c64, (1, 2)), jnp.tile(s64, (1, 2))


def _gain_rows(q_gain, k_gain):
    q2 = jnp.tile(q_gain.reshape(1, HEAD_DIM), (GROUP_A, 2))
    k2 = jnp.tile(k_gain.reshape(1, HEAD_DIM), (1, 2))
    return jnp.concatenate([q2, k2, jnp.ones((1, LANES), F32)], axis=0)


def _local_step(x, positions, mod, norm1_g, norm2_g, q_norm_a, k_norm_a, sinks_a,
                wqkv_a, wo_a, wqkv_b, wo_b, wgu, wd, loss_target):
    NB, S, D = x.shape
    T = NB * S
    QA = N_Q_A * HEAD_DIM
    tab_c, tab_s = _rope_tables(positions)
    rev, fwd = _cumsum_mats()

    saved = []
    xc = x
    mods = [[mod[i][:, k * D:(k + 1) * D].reshape(NB, 1, D) for k in range(6)] for i in range(DEPTH)]
    h = _norm_mod_fwd(xc, norm1_g[0:1], mods[0][1], mods[0][0])
    for i in range(DEPTH):
        j = i // 2
        sh1, sc1, g1, sh2, sc2, g2 = mods[i]
        st = dict(x=xc, sc1=sc1, g1=g1, sc2=sc2, g2=g2)
        st["h"] = h.reshape(T, D)
        if i % 2 == 0:
            st["qkv"] = _matmul(st["h"], wqkv_a[j], "nn", F32, "qkv_a")
            st["gains"] = _gain_rows(q_norm_a[j], k_norm_a[j])
            st["qkn"] = _qk_prep_fwd(st["qkv"], tab_c, tab_s, st["gains"]).reshape(NB, S, -1)
            st["o"] = _attn_a_fwd(st["qkn"], sinks_a[j]).reshape(T, QA)
            y = _matmul(st["o"], wo_a[j], "nn", F32, "wo_a")
        else:
            st["qkv"] = _matmul(st["h"], wqkv_b[j], "nn", BF16, "qkv_b").reshape(NB, S, -1)
            st["o"] = _attn_b_fwd(st["qkv"], rev).reshape(T, N_H_B * HEAD_DIM)
            y = _matmul(st["o"], wo_b[j], "nn", F32, "wo_b")
        st["y"] = y.reshape(NB, S, D)
        x1, h2 = _gate_res(xc, st["y"], g1, norm=(norm2_g[i:i + 1], sc2, sh2))
        st["x1"] = x1
        st["h2"] = h2.reshape(T, D)
        st["gu"], st["act"] = _matmul(st["h2"], wgu[i], "nn", BF16, "gate_up", swiglu=True)
        st["m"] = _matmul(st["act"], wd[i], "nn", F32, "down").reshape(NB, S, D)
        if i + 1 < DEPTH:
            xc, h = _gate_res(x1, st["m"], g2, norm=(norm1_g[i + 1:i + 2], mods[i + 1][1], mods[i + 1][0]))
        else:
            xc = _gate_res(x1, st["m"], g2)
        saved.append(st)

    loss, dx = _loss_fwd_bwd(xc, loss_target)

    grads = {name: [None] * n for name, n in
             (("wqkv_a", 2), ("wo_a", 2), ("wqkv_b", 2), ("wo_b", 2), ("wgu", DEPTH), ("wd", DEPTH),
              ("norm1_g", DEPTH), ("norm2_g", DEPTH), ("q_norm_a", 2), ("k_norm_a", 2), ("sinks_a", 2))}
    dmod = [None] * DEPTH
    dm, dg2 = _gate_res_bwd(dx, saved[-1]["m"], saved[-1]["g2"])
    for i in reversed(range(DEPTH)):
        j = i // 2
        st = saved[i]
        dm = dm.reshape(T, D)
        grads["wd"][i] = _matmul(st["act"], dm, "tn", F32, "d_wd")
        dgu = _swiglu_bwd(dm, wd[i], st["gu"])
        grads["wgu"][i] = _matmul(st["h2"], dgu, "tn", F32, "d_wgu")
        dx1, dsh2, dsc2, grads["norm2_g"][i], dy, dg1 = _norm_mod_bwd(
            dgu, wgu[i], st["x1"], norm2_g[i:i + 1], st["sc2"], dx, "d_h2", y=st["y"], g=st["g1"])
        dy = dy.reshape(T, D)
        if i % 2 == 0:
            do = _matmul(dy, wo_a[j], "nt", BF16, "d_o_a").reshape(NB, S, QA)
            grads["wo_a"][j] = _matmul(st["o"], dy, "tn", F32, "d_wo_a")
            dq, dk, dv, dsink = _attn_a_bwd(st["qkn"], do, sinks_a[j])
            dqkv, dgain = _qk_prep_bwd(st["qkv"], dq.reshape(T, QA), dk.reshape(T, LANES), dv.reshape(T, LANES),
                                       tab_c, tab_s, st["gains"])
            w_in = wqkv_a[j]
            grads["wqkv_a"][j] = _matmul(st["h"], dqkv, "tn", F32, "d_wqkv_a")
            grads["q_norm_a"][j] = jnp.sum(dgain[:GROUP_A].reshape(2 * GROUP_A, HEAD_DIM), axis=0)
            grads["k_norm_a"][j] = jnp.sum(dgain[GROUP_A].reshape(2, HEAD_DIM), axis=0)
            grads["sinks_a"][j] = jnp.sum(dsink[..., 0], axis=0)
        else:
            do = _matmul(dy, wo_b[j], "nt", BF16, "d_o_b").reshape(NB, S, -1)
            grads["wo_b"][j] = _matmul(st["o"], dy, "tn", F32, "d_wo_b")
            dq, dk, dv = _attn_b_bwd(st["qkv"], do, rev, fwd)
            dqkv = jnp.concatenate([dq, dk, dv], axis=-1).reshape(T, -1).astype(BF16)
            w_in = wqkv_b[j]
            grads["wqkv_b"][j] = _matmul(st["h"], dqkv, "tn", F32, "d_wqkv_b")
        this_dg2 = dg2
        if i > 0:
            dx, dsh1, dsc1, grads["norm1_g"][i], dm, dg2 = _norm_mod_bwd(
                dqkv, w_in, st["x"], norm1_g[i:i + 1], st["sc1"], dx1, "d_h", y=saved[i - 1]["m"], g=saved[i - 1]["g2"])
        else:
            dx, dsh1, dsc1, grads["norm1_g"][i] = _norm_mod_bwd(
                dqkv, w_in, st["x"], norm1_g[i:i + 1], st["sc1"], dx1, "d_h")
        dmod[i] = jnp.concatenate([dsh1, dsc1, dg1, dsh2, dsc2, this_dg2], axis=-1).reshape(NB, 6 * D)

    matrices = ("wqkv_a", "wo_a", "wqkv_b", "wo_b", "wgu", "wd")
    grads = {name: parts if name in matrices else jnp.stack(parts) for name, parts in grads.items()}
    return loss, dx, grads, jnp.stack(dmod)


def _rows_of(flat, cols=PACK_COLS):
    n = flat.shape[0]
    pad = (-n) % (8 * cols)
    if pad:
        flat = jnp.concatenate([flat, jnp.zeros((pad,), flat.dtype)])
    return flat.reshape(-1, cols)


def kernel(x, c, positions, ada_w, ada_b, norm1_g, norm2_g, wqkv_a, q_norm_a, k_norm_a, sinks_a, wo_a, wqkv_b, wo_b, w_gate, w_up, w_down, loss_target, m_ada_w, m_ada_b, m_norm1_g, m_norm2_g, m_wqkv_a, m_q_norm_a, m_k_norm_a, m_sinks_a, m_wo_a, m_wqkv_b, m_wo_b, m_w_gate, m_w_up, m_w_down, v_ada_w, v_ada_b, v_norm1_g, v_norm2_g, v_wqkv_a, v_q_norm_a, v_k_norm_a, v_sinks_a, v_wo_a, v_wqkv_b, v_wo_b, v_w_gate, v_w_up, v_w_down):
    xi, yi, ci = lax.axis_index("x"), lax.axis_index("y"), lax.axis_index("c")
    dev = 4 * xi + 2 * yi + ci
    chip = 2 * xi + yi
    NB, S, D = x.shape
    B_all = N_DEV * NB
    L = ada_w.shape[0]
    n_mod = ada_w.shape[2] // 2

    c_all = _gather8(_rows_of(c.reshape(-1), LANES), "gather_c").reshape(N_DEV, -1)[:, :NB * D].reshape(B_all, D)
    ada_w_half = lax.dynamic_slice_in_dim(ada_w, ci * n_mod, n_mod, axis=2)
    ada_b_half = lax.dynamic_slice_in_dim(ada_b, dev * n_mod, n_mod, axis=1).reshape(L, 1, n_mod)
    mod_part = _ada_fwd(c_all, ada_w_half, ada_b_half)
    n_part = L * B_all * n_mod
    mod_all = _gather8(_rows_of(mod_part.reshape(-1)), "gather_mod").reshape(N_DEV, -1)[:, :n_part]
    mod_all = mod_all.reshape(N_DEV, L, B_all, n_mod).transpose(1, 2, 0, 3).reshape(L, B_all, N_DEV * n_mod)
    mod = lax.dynamic_slice_in_dim(mod_all, dev * NB, NB, axis=1)

    shards = dict(wqkv_a=wqkv_a, wo_a=wo_a, wqkv_b=wqkv_b, wo_b=wo_b, w_gate=w_gate, w_up=w_up, w_down=w_down)
    halves = []
    for name, _ in _SHARDED:
        w = shards[name]
        half = lax.dynamic_index_in_dim(w.reshape((2, w.shape[0] // 2) + w.shape[1:]), ci, 0, keepdims=False)
        halves.append(half.astype(BF16))
    gathered = _all_gather8(halves, "gather_weights", local_axis=1, local_chunks=8, relay_axis=1)
    full = {name: _unpack_full(t, axis) for (name, axis), t in zip(_SHARDED, gathered)}
    wgu = [_interleave(gate, up) for gate, up in zip(full["w_gate"], full["w_up"])]

    loss, grad_x, g, dmod = _local_step(
        x, positions, mod, norm1_g, norm2_g, q_norm_a, k_norm_a, sinks_a,
        full["wqkv_a"], full["wo_a"], full["wqkv_b"], full["wo_b"], wgu, full["w_down"], loss_target)

    g_full = dict(wqkv_a=g["wqkv_a"], wo_a=g["wo_a"], wqkv_b=g["wqkv_b"], wo_b=g["wo_b"],
                  w_gate=g["wgu"], w_up=g["wgu"], w_down=g["wd"])
    which = dict(w_gate=0, w_up=1)
    packed = [_pack_full(g_full[name], axis, which.get(name)) for name, axis in _SHARDED]
    def own(t, index):
        return lax.dynamic_index_in_dim(t, index, 0, keepdims=False)

    from_cores = _exchange_cores(packed, "rs_cores", chunk_axis=0, chunks=4)
    chip_part = [_sum_slabs(own(p, ci), r, "rs_add_cores", with_bf16=True) for p, r in zip(packed, from_cores)]
    from_chips = _exchange_chips([b for _, b in chip_part], "rs_chips", relay_axis=1)
    mine = [_sum_slabs(own(p, chip), r, "rs_add_chips") for (p, _), r in zip(chip_part, from_chips)]
    theirs = _sibling_send(mine, "rs_halves")
    grad = {}
    for (name, _), m, t in zip(_SHARDED, mine, theirs):
        first, second = jnp.where(ci == 0, m, t), jnp.where(ci == 0, t, m)
        grad[name] = jnp.stack([first, second]).reshape(shards[name].shape)

    small_names = ("norm1_g", "norm2_g", "q_norm_a", "k_norm_a", "sinks_a")
    small = [dmod.reshape(-1)] + [g[name].reshape(-1) for name in small_names] + [loss.reshape(-1)]
    small_sizes = [t.shape[0] for t in small]
    small_rows = _rows_of(jnp.concatenate(small))
    small_all = _gather8(small_rows, "gather_small")
    small_sum = _sum_leading(small_all, "sum_small").reshape(-1)
    n_dmod = small_sizes[0]
    dmod_all = small_all.reshape(N_DEV, -1)[:, :n_dmod].reshape(N_DEV, L, NB, 6 * D)
    dmod_all = dmod_all.transpose(1, 0, 2, 3).reshape(L, B_all, 6 * D)
    off = n_dmod
    for name, sz in zip(small_names + ("loss",), small_sizes[1:]):
        grad[name] = small_sum[off:off + sz]
        off += sz
    loss_total = grad.pop("loss").reshape(())
    for name, ref in (("norm1_g", norm1_g), ("norm2_g", norm2_g), ("q_norm_a", q_norm_a),
                      ("k_norm_a", k_norm_a), ("sinks_a", sinks_a)):
        grad[name] = grad[name].reshape(ref.shape)

    n_shard = ada_w.shape[2]
    dmod_shard = lax.dynamic_slice_in_dim(dmod_all, chip * n_shard, n_shard, axis=2)
    grad["ada_w"], gb = _ada_bwd(c_all, dmod_all, dmod_shard)
    grad["ada_b"] = gb.reshape(ada_b.shape)

    weights = dict(ada_w=ada_w, ada_b=ada_b, norm1_g=norm1_g, norm2_g=norm2_g, wqkv_a=wqkv_a, q_norm_a=q_norm_a,
                   k_norm_a=k_norm_a, sinks_a=sinks_a, wo_a=wo_a, wqkv_b=wqkv_b, wo_b=wo_b, w_gate=w_gate,
                   w_up=w_up, w_down=w_down)
    m_in = dict(ada_w=m_ada_w, ada_b=m_ada_b, norm1_g=m_norm1_g, norm2_g=m_norm2_g, wqkv_a=m_wqkv_a,
                q_norm_a=m_q_norm_a, k_norm_a=m_k_norm_a, sinks_a=m_sinks_a, wo_a=m_wo_a, wqkv_b=m_wqkv_b,
                wo_b=m_wo_b, w_gate=m_w_gate, w_up=m_w_up, w_down=m_w_down)
    v_in = dict(ada_w=v_ada_w, ada_b=v_ada_b, norm1_g=v_norm1_g, norm2_g=v_norm2_g, wqkv_a=v_wqkv_a,
                q_norm_a=v_q_norm_a, k_norm_a=v_k_norm_a, sinks_a=v_sinks_a, wo_a=v_wo_a, wqkv_b=v_wqkv_b,
                wo_b=v_wo_b, w_gate=v_w_gate, w_up=v_w_up, w_down=v_w_down)
    names = list(weights)
    delta, new_m, new_v = {}, {}, {}
    for name in names:
        delta[name], new_m[name], new_v[name] = _adamw(weights[name], grad[name], m_in[name], v_in[name],
                                                       "adamw_" + name)
    return (loss_total, grad_x, *[grad[k] for k in names], *[delta[k] for k in names],
            *[new_m[k] for k in names], *[new_v[k] for k in names])
```

```python
import jax
import jax.numpy as jnp
from jax import lax
from jax.experimental import pallas as pl
from jax.experimental.pallas import tpu as pltpu

F32 = jnp.float32
BF16 = jnp.bfloat16

DEPTH = 4
HEAD_DIM = 64
N_Q_A = 16
N_KV_A = 2
GROUP_A = N_Q_A // N_KV_A
N_H_B = 16
BLOCK = 128
ROT_DIM = HEAD_DIM // 4
ROPE_THETA = 500000.0
EPS = 1e-6
ATTN_SCALE = HEAD_DIM ** -0.5
NEG_BIG = -1e30

ADAM_LR = 0.001
ADAM_B1 = 0.9
ADAM_B2 = 0.999
ADAM_EPS = 1e-08
ADAM_WD = 0.01
ADAM_STEP = 10

N_DEV = 8
LANES = 128
PACK_COLS = 1024
VMEM_LIMIT_BYTES = 48 * 1024 * 1024
MESH = pl.DeviceIdType.MESH

_NT = (((1,), (1,)), ((), ()))
_TN = (((0,), (0,)), ((), ()))
_NN = (((1,), (0,)), ((), ()))


def _params(sem=None, vmem_limit_bytes=VMEM_LIMIT_BYTES):
    return pltpu.CompilerParams(vmem_limit_bytes=vmem_limit_bytes, dimension_semantics=sem)


def _pick(n, cap, mult):
    best = None
    for t in range(mult, min(n, cap) + 1, mult):
        if n % t == 0:
            best = t
    return n if best is None else best


_ANY = pl.BlockSpec(memory_space=pl.ANY)


def _window(index, axis, q, n, shape):
    rest = [slice(None)] * len(shape)
    size = shape[axis] // n
    rest[axis] = pl.ds(q * size, size)
    return tuple(index) + tuple(rest)


def _all_gather8(xs, name, local_axis=0, local_chunks=1, relay_axis=None):
    n = len(xs)
    n_sems = 7 if relay_axis is None else 9

    def body(*refs):
        x_refs, out_refs = refs[:n], refs[n:2 * n]
        send_sems, recv_sems, local_sems = refs[2 * n:]
        xi, yi, ci = lax.axis_index("x"), lax.axis_index("y"), lax.axis_index("c")
        me, sibling = (xi, yi, ci), (xi, yi, 1 - ci)
        chips = [(1 - xi, yi), (xi, 1 - yi), (1 - xi, 1 - yi)]

        def slab(w, px, py, pc):
            return out_refs[w].at[4 * px + 2 * py + pc]

        def copy(w, k, block, to, src=None):
            return pltpu.make_async_remote_copy(
                src_ref=slab(w, *block) if src is None else src, dst_ref=slab(w, *block),
                send_sem=send_sems.at[k, w], recv_sem=recv_sems.at[k, w], device_id=to, device_id_type=MESH)

        mine = []
        for w in range(n):
            for q in range(local_chunks):
                part = _window((), local_axis, q, local_chunks, xs[w].shape)
                mine.append(pltpu.make_async_copy(x_refs[w].at[part], slab(w, *me).at[part], local_sems.at[w, q]))
                mine[-1].start()
        direct = chips if relay_axis is None else chips[:2]
        first = [copy(w, 0, me, sibling, src=x_refs[w]) for w in range(n)]
        first += [copy(w, 1 + j, me, (*chip, ci), src=x_refs[w]) for j, chip in enumerate(direct) for w in range(n)]
        for cp in first:
            cp.start()

        def relay(w, part, block, to):
            piece = _window((), relay_axis, part, 2, xs[w].shape)
            return pltpu.make_async_remote_copy(
                src_ref=slab(w, *block).at[piece], dst_ref=slab(w, *block).at[piece],
                send_sem=send_sems.at[7 + part, w], recv_sem=recv_sems.at[7 + part, w],
                device_id=to, device_id_type=MESH)

        passed = []
        for j, chip in enumerate(direct):
            for w in range(n):
                copy(w, 1 + j, (*chip, ci), me).wait_recv()
                passed.append(copy(w, 4 + j, (*chip, ci), sibling))
                passed[-1].start()
                if relay_axis is not None:
                    passed.append(relay(w, j, (*chip, ci), (*chips[1 - j], ci)))
                    passed[-1].start()
        if relay_axis is not None:
            for w in range(n):
                for part in range(2):
                    relay(w, part, (*chips[2], ci), me).wait_recv()
                passed.append(copy(w, 6, (*chips[2], ci), sibling))
                passed[-1].start()
        for w in range(n):
            copy(w, 0, sibling, me).wait_recv()
        for j, chip in enumerate(chips):
            for w in range(n):
                copy(w, 4 + j, (*chip, 1 - ci), me).wait_recv()
        for cp in first + passed:
            cp.wait_send()
        for cp in mine:
            cp.wait()

    return pl.pallas_call(
        body, name=name,
        out_shape=[jax.ShapeDtypeStruct((N_DEV,) + x.shape, x.dtype) for x in xs],
        in_specs=[_ANY] * n, out_specs=[_ANY] * n,
        scratch_shapes=[pltpu.SemaphoreType.DMA((n_sems, n)), pltpu.SemaphoreType.DMA((n_sems, n)),
                        pltpu.SemaphoreType.DMA((n, local_chunks))],
    )(*xs)


def _exchange_cores(xs, name, chunk_axis=0, chunks=1):
    n = len(xs)
    n_peers = 1

    def body(*refs):
        x_refs, out_refs = refs[:n], refs[n:2 * n]
        send_sems, recv_sems = refs[2 * n:]
        xi, yi, ci = lax.axis_index("x"), lax.axis_index("y"), lax.axis_index("c")
        peers = [(1 - ci, (xi, yi, 1 - ci))]
        copies = []
        for k, (p, dev) in enumerate(peers):
            for w in range(n):
                slab_shape = xs[w].shape[1:]
                for q in range(chunks):
                    copies.append(pltpu.make_async_remote_copy(
                        src_ref=x_refs[w].at[_window((p,), chunk_axis, q, chunks, slab_shape)],
                        dst_ref=out_refs[w].at[_window((k,), chunk_axis, q, chunks, slab_shape)],
                        send_sem=send_sems.at[k, w, q], recv_sem=recv_sems.at[k, w, q],
                        device_id=dev, device_id_type=MESH))
                    copies[-1].start()
        for cp in copies:
            cp.wait()

    return pl.pallas_call(
        body, name=name,
        out_shape=[jax.ShapeDtypeStruct((n_peers,) + x.shape[1:], x.dtype) for x in xs],
        in_specs=[_ANY] * n, out_specs=[_ANY] * n,
        scratch_shapes=[pltpu.SemaphoreType.DMA((n_peers, n, chunks)), pltpu.SemaphoreType.DMA((n_peers, n, chunks))],
    )(*xs)


def _exchange_chips(xs, name, relay_axis):
    n = len(xs)

    def half_shape(x):
        shape = list(x.shape[1:])
        shape[relay_axis] //= 2
        return tuple(shape)

    def body(*refs):
        x_refs, out_refs, hop_refs = refs[:n], refs[n:2 * n], refs[2 * n:3 * n]
        send_sems, recv_sems = refs[3 * n:]
        xi, yi, ci = lax.axis_index("x"), lax.axis_index("y"), lax.axis_index("c")
        nbr = [(1 - xi, yi, ci), (xi, 1 - yi, ci)]
        slab_of_nbr = [2 * (1 - xi) + yi, 2 * xi + (1 - yi)]
        slab_of_diag = 2 * (1 - xi) + (1 - yi)

        def copy(k, w, src, dst, to):
            return pltpu.make_async_remote_copy(src_ref=src, dst_ref=dst, send_sem=send_sems.at[k, w],
                                                recv_sem=recv_sems.at[k, w], device_id=to, device_id_type=MESH)

        def piece(w, part):
            return _window((), relay_axis, part, 2, xs[w].shape[1:])

        sent = []
        for w in range(n):
            for j in range(2):
                sent.append(copy(j, w, x_refs[w].at[slab_of_nbr[j]], out_refs[w].at[j], nbr[j]))
                sent.append(copy(2 + j, w, x_refs[w].at[(slab_of_diag,) + piece(w, j)], hop_refs[w].at[j], nbr[j]))
        for cp in sent:
            cp.start()
        for w in range(n):
            for j in range(2):
                copy(2 + j, w, hop_refs[w].at[j], hop_refs[w].at[j], nbr[j]).wait_recv()
                sent.append(copy(4 + j, w, hop_refs[w].at[j], out_refs[w].at[(2,) + piece(w, j)], nbr[1 - j]))
                sent[-1].start()
        for w in range(n):
            for j in range(2):
                copy(j, w, out_refs[w].at[j], out_refs[w].at[j], nbr[j]).wait_recv()
                half = out_refs[w].at[(2,) + piece(w, j)]
                copy(4 + j, w, half, half, nbr[1 - j]).wait_recv()
        for cp in sent:
            cp.wait_send()

    out = pl.pallas_call(
        body, name=name,
        out_shape=[jax.ShapeDtypeStruct((3,) + x.shape[1:], x.dtype) for x in xs]
        + [jax.ShapeDtypeStruct((2,) + half_shape(x), x.dtype) for x in xs],
        in_specs=[_ANY] * n, out_specs=[_ANY] * (2 * n),
        scratch_shapes=[pltpu.SemaphoreType.DMA((6, n)), pltpu.SemaphoreType.DMA((6, n))],
    )(*xs)
    return out[:n]


def _sibling_send(xs, name, chunk_axis=1, chunks=4):
    n = len(xs)

    def body(*refs):
        x_refs, out_refs = refs[:n], refs[n:2 * n]
        send_sems, recv_sems = refs[2 * n:]
        xi, yi, ci = lax.axis_index("x"), lax.axis_index("y"), lax.axis_index("c")
        copies = []
        for w in range(n):
            for q in range(chunks):
                part = _window((), chunk_axis, q, chunks, xs[w].shape)
                copies.append(pltpu.make_async_remote_copy(
                    src_ref=x_refs[w].at[part], dst_ref=out_refs[w].at[part],
                    send_sem=send_sems.at[w, q], recv_sem=recv_sems.at[w, q],
                    device_id=(xi, yi, 1 - ci), device_id_type=MESH))
                copies[-1].start()
        for cp in copies:
            cp.wait()

    return pl.pallas_call(
        body, name=name,
        out_shape=[jax.ShapeDtypeStruct(x.shape, x.dtype) for x in xs],
        in_specs=[_ANY] * n, out_specs=[_ANY] * n,
        scratch_shapes=[pltpu.SemaphoreType.DMA((n, chunks)), pltpu.SemaphoreType.DMA((n, chunks))],
    )(*xs)


def _sum_leading(x, name, own=None, with_bf16=False):
    P, R, C = x.shape
    tr = _pick(R, max(16, (1 << 19) // (C * (P + 1))), 16)

    def body(*refs):
        n_in = 1 if own is None else 2
        x_ref = refs[n_in - 1]
        acc = x_ref[0].astype(F32) if own is None else refs[0][...] + x_ref[0].astype(F32)
        for p in range(1, P):
            acc = acc + x_ref[p].astype(F32)
        refs[n_in][...] = acc
        if with_bf16:
            refs[n_in + 1][...] = acc.astype(BF16)

    flat = pl.BlockSpec((tr, C), lambda r: (r, 0))
    slabs = pl.BlockSpec((P, tr, C), lambda r: (0, r, 0))
    out = pl.pallas_call(
        body, name=name, grid=(R // tr,),
        in_specs=[slabs] if own is None else [flat, slabs],
        out_specs=[flat, flat] if with_bf16 else [flat],
        out_shape=[jax.ShapeDtypeStruct((R, C), F32)] + ([jax.ShapeDtypeStruct((R, C), BF16)] if with_bf16 else []),
        compiler_params=_params(("arbitrary",)),
    )(*([x] if own is None else [own, x]))
    return out if with_bf16 else out[0]


MATMUL_SINGLE_K = 1280
MATMUL_VMEM_BUDGET = 36 * 1024 * 1024


def _matmul(a, b, mode, out_dtype, name, swiglu=False):
    if mode == "nn":
        (M, K), N = a.shape, b.shape[1]
    elif mode == "nt":
        (M, K), N = a.shape, b.shape[0]
    else:
        (K, M), N = a.shape, b.shape[1]
    tm = _pick(M, 1024 if mode != "tn" else 1536, 128)
    tn = _pick(N, 1536, 128)
    if swiglu:
        tm, tn = _pick(M, 1024 if out_dtype == BF16 else 512, 128), 2 * _ff_tile(N // 2)
    out_bytes = jnp.dtype(out_dtype).itemsize
    tk = K
    if K > MATMUL_SINGLE_K:
        for cap in (2048, 1024, 512):
            tk = _pick(K, cap, 128)
            blocks = 2 * 2 * tk * (tm + tn) + tm * tn * (2 * out_bytes + (4 if out_dtype != F32 else 0))
            if blocks <= MATMUL_VMEM_BUDGET:
                break
    nk = K // tk
    dims = {"nn": _NN, "nt": _NT, "tn": _TN}[mode]
    use_scratch = nk > 1 and out_dtype != F32

    def body(a_ref, b_ref, *refs):
        o_ref = refs[0]

        def product():
            return lax.dot_general(a_ref[...].astype(BF16), b_ref[...].astype(BF16), dims,
                                   preferred_element_type=F32)

        if nk == 1:
            part = product()
            o_ref[...] = part.astype(o_ref.dtype)
            if swiglu:
                g = part[:, :tn // 2]
                refs[1][...] = (g * _sigmoid(g) * part[:, tn // 2:]).astype(BF16)
            return
        k = pl.program_id(2)
        acc_ref = refs[-1] if use_scratch else o_ref

        @pl.when(k == 0)
        def _():
            acc_ref[...] = jnp.zeros_like(acc_ref)

        acc_ref[...] += product()

        if use_scratch:
            @pl.when(k == nk - 1)
            def _():
                o_ref[...] = acc_ref[...].astype(o_ref.dtype)

    if mode == "tn":
        a_spec = pl.BlockSpec((tk, tm), lambda i, j, k: (k, i))
    else:
        a_spec = pl.BlockSpec((tm, tk), lambda i, j, k: (i, k))
    if mode == "nt":
        b_spec = pl.BlockSpec((tn, tk), lambda i, j, k: (j, k))
    else:
        b_spec = pl.BlockSpec((tk, tn), lambda i, j, k: (k, j))
    out_specs = [pl.BlockSpec((tm, tn), lambda i, j, k: (i, j))]
    out_shape = [jax.ShapeDtypeStruct((M, N), out_dtype)]
    if swiglu:
        assert nk == 1 and mode == "nn"
        out_specs.append(pl.BlockSpec((tm, tn // 2), lambda i, j, k: (i, j)))
        out_shape.append(jax.ShapeDtypeStruct((M, N // 2), BF16))
    out = pl.pallas_call(
        body, name=name, grid=(M // tm, N // tn, nk),
        in_specs=[a_spec, b_spec], out_specs=out_specs, out_shape=out_shape,
        scratch_shapes=[pltpu.VMEM((tm, tn), F32)] if use_scratch else [],
        compiler_params=_params(("parallel", "parallel", "arbitrary")),
    )(a, b)
    return out if swiglu else out[0]


def _row_tile(S):
    return _pick(S, 512, 8)


def _norm_mod_fwd(x, gain, sc, sh):
    NB, S, D = x.shape
    tr = _row_tile(S)

    def body(x_ref, g_ref, sc_ref, sh_ref, h_ref):
        xv = x_ref[...]
        ms = jnp.mean(xv * xv, axis=-1, keepdims=True)
        n = xv * lax.rsqrt(ms + EPS) * g_ref[...]
        h_ref[...] = (n * (1.0 + sc_ref[...]) + sh_ref[...]).astype(BF16)

    tok = pl.BlockSpec((None, tr, D), lambda b, r: (b, r, 0))
    per_ex = pl.BlockSpec((None, 1, D), lambda b, r: (b, 0, 0))
    return pl.pallas_call(
        body, name="norm_mod_fwd", grid=(NB, S // tr),
        in_specs=[tok, pl.BlockSpec((1, D), lambda b, r: (0, 0)), per_ex, per_ex],
        out_specs=tok, out_shape=jax.ShapeDtypeStruct((NB, S, D), BF16),
        compiler_params=_params(("parallel", "parallel")),
    )(x, gain, sc, sh)


def _norm_mod_bwd(a, w, x, gain, sc, dres, name, y=None, g=None):
    NB, S, D = x.shape
    T, K = a.shape
    tm = _pick(S, 512, 128)
    per_ex_tiles = S // tm
    tk = K if K <= MATMUL_SINGLE_K else _pick(K, 2816, 128)
    nk = K // tk
    gated = y is not None

    def body(*refs):
        a_ref, w_ref, x_ref, g_ref, sc_ref, dres_ref = refs[:6]
        refs = refs[6:]
        if gated:
            y_ref, gate_ref = refs[:2]
            refs = refs[2:]
        dx_ref, dsh_ref, dsc_ref, dgain_ref = refs[:4]
        acc_ref = refs[-1]
        i, k = pl.program_id(0), pl.program_id(1)

        @pl.when(k == 0)
        def _():
            acc_ref[...] = jnp.zeros_like(acc_ref)

        acc_ref[...] += lax.dot_general(a_ref[...], w_ref[...], _NT, preferred_element_type=F32)

        @pl.when(k == nk - 1)
        def _():
            first_of_example = i % per_ex_tiles == 0

            @pl.when(first_of_example)
            def _():
                dsh_ref[...] = jnp.zeros_like(dsh_ref)
                dsc_ref[...] = jnp.zeros_like(dsc_ref)
                if gated:
                    refs[5][...] = jnp.zeros_like(refs[5])

            @pl.when(i == 0)
            def _():
                dgain_ref[...] = jnp.zeros_like(dgain_ref)

            xv = x_ref[...]
            rstd = lax.rsqrt(jnp.mean(xv * xv, axis=-1, keepdims=True) + EPS)
            xh = xv * rstd
            gn = g_ref[...]
            dh = acc_ref[...]
            dsh_ref[...] += jnp.sum(dh, axis=0, keepdims=True)
            dsc_ref[...] += jnp.sum(dh * (xh * gn), axis=0, keepdims=True)
            dn = dh * (1.0 + sc_ref[...])
            dgain_ref[...] += jnp.sum(dn * xh, axis=0, keepdims=True)
            dxh = dn * gn
            proj = jnp.mean(dxh * xh, axis=-1, keepdims=True)
            dx = rstd * (dxh - xh * proj) + dres_ref[...]
            dx_ref[...] = dx
            if gated:
                refs[4][...] = (dx * gate_ref[...]).astype(BF16)
                refs[5][...] += jnp.sum(dx * y_ref[...], axis=0, keepdims=True)

    tok = pl.BlockSpec((None, tm, D), lambda i, k: (i // per_ex_tiles, i % per_ex_tiles, 0))
    per_ex = pl.BlockSpec((None, 1, D), lambda i, k: (i // per_ex_tiles, 0, 0))
    row = pl.BlockSpec((1, D), lambda i, k: (0, 0))
    in_specs = [pl.BlockSpec((tm, tk), lambda i, k: (i, k)), pl.BlockSpec((D, tk), lambda i, k: (0, k)),
                tok, row, per_ex, tok]
    out_specs = [tok, per_ex, per_ex, row]
    out_shape = [jax.ShapeDtypeStruct((NB, S, D), F32), jax.ShapeDtypeStruct((NB, 1, D), F32),
                 jax.ShapeDtypeStruct((NB, 1, D), F32), jax.ShapeDtypeStruct((1, D), F32)]
    operands = [a, w, x, gain, sc, dres]
    if gated:
        in_specs += [tok, per_ex]
        out_specs += [tok, per_ex]
        out_shape += [jax.ShapeDtypeStruct((NB, S, D), BF16), jax.ShapeDtypeStruct((NB, 1, D), F32)]
        operands += [y, g]
    return pl.pallas_call(
        body, name=name, grid=(T // tm, nk), in_specs=in_specs, out_specs=out_specs, out_shape=out_shape,
        scratch_shapes=[pltpu.VMEM((tm, D), F32)],
        compiler_params=_params(("arbitrary", "arbitrary")),
    )(*operands)


def _gate_res(x, y, g, norm=None):
    NB, S, D = x.shape
    tr = _row_tile(S)

    def body(x_ref, y_ref, g_ref, *refs):
        xo = x_ref[...] + g_ref[...] * y_ref[...]
        refs[-1 if norm is None else -2][...] = xo
        if norm is not None:
            gain_ref, sc_ref, sh_ref, _, h_ref = refs
            n = xo * lax.rsqrt(jnp.mean(xo * xo, axis=-1, keepdims=True) + EPS) * gain_ref[...]
            h_ref[...] = (n * (1.0 + sc_ref[...]) + sh_ref[...]).astype(BF16)

    tok = pl.BlockSpec((None, tr, D), lambda b, r: (b, r, 0))
    per_ex = pl.BlockSpec((None, 1, D), lambda b, r: (b, 0, 0))
    in_specs, out_specs, operands = [tok, tok, per_ex], [tok], [x, y, g]
    out_shape = [jax.ShapeDtypeStruct((NB, S, D), F32)]
    if norm is not None:
        in_specs += [pl.BlockSpec((1, D), lambda b, r: (0, 0)), per_ex, per_ex]
        out_specs.append(tok)
        out_shape.append(jax.ShapeDtypeStruct((NB, S, D), BF16))
        operands += list(norm)
    out = pl.pallas_call(
        body, name="gate_res", grid=(NB, S // tr), in_specs=in_specs, out_specs=out_specs, out_shape=out_shape,
        compiler_params=_params(("parallel", "parallel")),
    )(*operands)
    return out[0] if norm is None else out


def _gate_res_bwd(dxo, y, g):
    NB, S, D = dxo.shape
    tr = _row_tile(S)

    def body(d_ref, y_ref, g_ref, dy_ref, dg_ref):
        @pl.when(pl.program_id(1) == 0)
        def _():
            dg_ref[...] = jnp.zeros_like(dg_ref)

        d = d_ref[...]
        dy_ref[...] = (d * g_ref[...]).astype(BF16)
        dg_ref[...] += jnp.sum(d * y_ref[...], axis=0, keepdims=True)

    tok = pl.BlockSpec((None, tr, D), lambda b, r: (b, r, 0))
    per_ex = pl.BlockSpec((None, 1, D), lambda b, r: (b, 0, 0))
    return pl.pallas_call(
        body, name="gate_res_bwd", grid=(NB, S // tr), in_specs=[tok, tok, per_ex], out_specs=[tok, per_ex],
        out_shape=[jax.ShapeDtypeStruct((NB, S, D), BF16), jax.ShapeDtypeStruct((NB, 1, D), F32)],
        compiler_params=_params(("arbitrary", "arbitrary")),
    )(dxo, y, g)


def _sigmoid(v):
    return 1.0 / (1.0 + jnp.exp(-v))


def _ff_tile(F):
    return _pick(F, 1536, 128)


def _interleave(gate, up):
    F = gate.shape[-1]
    tf = _ff_tile(F)
    parts = []
    for j in range(F // tf):
        parts += [gate[..., j * tf:(j + 1) * tf], up[..., j * tf:(j + 1) * tf]]
    return jnp.concatenate(parts, axis=-1)


def _swiglu_bwd(dm, wd, gu):
    T, D = dm.shape
    F = wd.shape[0]
    tf = _ff_tile(F)
    tm = _pick(T, 1024, 128)
    assert D <= MATMUL_SINGLE_K

    def body(a_ref, b_ref, gu_ref, o_ref):
        d = lax.dot_general(a_ref[...], b_ref[...], _NT, preferred_element_type=F32)
        g, u = gu_ref[:, :tf].astype(F32), gu_ref[:, tf:].astype(F32)
        s = _sigmoid(g)
        o_ref[:, :tf] = (d * u * (s * (1.0 + g * (1.0 - s)))).astype(BF16)
        o_ref[:, tf:] = (d * (g * s)).astype(BF16)

    return pl.pallas_call(
        body, name="swiglu_bwd", grid=(T // tm, F // tf),
        in_specs=[pl.BlockSpec((tm, D), lambda i, j: (i, 0)), pl.BlockSpec((tf, D), lambda i, j: (j, 0)),
                  pl.BlockSpec((tm, 2 * tf), lambda i, j: (i, j))],
        out_specs=pl.BlockSpec((tm, 2 * tf), lambda i, j: (i, j)),
        out_shape=jax.ShapeDtypeStruct((T, 2 * F), BF16),
        compiler_params=_params(("parallel", "parallel")),
    )(dm, wd, gu)


def _loss_fwd_bwd(y, target):
    NB, S, D = y.shape
    tr = _row_tile(S)

    def body(y_ref, t_ref, l_ref, d_ref):
        @pl.when((pl.program_id(0) == 0) & (pl.program_id(1) == 0))
        def _():
            l_ref[...] = jnp.zeros_like(l_ref)

        e = y_ref[...] - t_ref[...]
        d_ref[...] = e / D
        l_ref[...] += 0.5 * jnp.sum(jnp.mean(e * e, axis=-1, keepdims=True), axis=0, keepdims=True)

    tok = pl.BlockSpec((None, tr, D), lambda b, r: (b, r, 0))
    return pl.pallas_call(
        body, name="loss", grid=(NB, S // tr), in_specs=[tok, tok],
        out_specs=[pl.BlockSpec((1, 1), lambda b, r: (0, 0)), tok],
        out_shape=[jax.ShapeDtypeStruct((1, 1), F32), jax.ShapeDtypeStruct((NB, S, D), F32)],
        compiler_params=_params(("arbitrary", "arbitrary")),
    )(y, target)


def _half_sums(v, lo):
    sa = jnp.sum(jnp.where(lo, v, 0.0), axis=-1, keepdims=True)
    sb = jnp.sum(jnp.where(lo, 0.0, v), axis=-1, keepdims=True)
    return jnp.where(lo, sa, sb)


def _rope_swap(v, lane64):
    up = pltpu.roll(v, LANES - ROT_DIM // 2, 1)
    down = pltpu.roll(v, ROT_DIM // 2, 1)
    return jnp.where(lane64 < ROT_DIM // 2, up, jnp.where(lane64 < ROT_DIM, down, 0.0))


def _qk_prep_fwd(qkv, tab_c, tab_s, gains):
    T, W = qkv.shape
    R = W // LANES
    tt = _pick(T, 256, 8)

    def body(x_ref, c_ref, s_ref, g_ref, o_ref):
        lane = lax.broadcasted_iota(jnp.int32, (tt, LANES), 1)
        lo = lane < HEAD_DIM
        lane64 = lane & (HEAD_DIM - 1)
        c, s = c_ref[...], s_ref[...]
        for j in range(R - 1):
            cols = slice(j * LANES, (j + 1) * LANES)
            xv = x_ref[:, cols]
            rstd = lax.rsqrt(_half_sums(xv * xv, lo) / HEAD_DIM + EPS)
            yn = xv * rstd * g_ref[j:j + 1, :]
            o_ref[:, cols] = (yn * c + _rope_swap(yn, lane64) * s).astype(BF16)
        o_ref[:, (R - 1) * LANES:] = x_ref[:, (R - 1) * LANES:].astype(BF16)

    tok = pl.BlockSpec((tt, W), lambda t: (t, 0))
    tab = pl.BlockSpec((tt, LANES), lambda t: (t, 0))
    return pl.pallas_call(
        body, name="qk_prep_fwd", grid=(T // tt,),
        in_specs=[tok, tab, tab, pl.BlockSpec((R, LANES), lambda t: (0, 0))],
        out_specs=tok, out_shape=jax.ShapeDtypeStruct((T, W), BF16),
        compiler_params=_params(("parallel",)),
    )(qkv, tab_c, tab_s, gains)


def _qk_prep_bwd(qkv, dq, dk, dv, tab_c, tab_s, gains):
    T, W = qkv.shape
    R = W // LANES
    QW = dq.shape[1]
    tt = _pick(T, 256, 8)

    def body(x_ref, dq_ref, dk_ref, dv_ref, c_ref, s_ref, g_ref, o_ref, dg_ref):
        @pl.when(pl.program_id(0) == 0)
        def _():
            dg_ref[...] = jnp.zeros_like(dg_ref)

        lane = lax.broadcasted_iota(jnp.int32, (tt, LANES), 1)
        lo = lane < HEAD_DIM
        lane64 = lane & (HEAD_DIM - 1)
        c, s = c_ref[...], s_ref[...]
        for j in range(R - 1):
            cols = slice(j * LANES, (j + 1) * LANES)
            xv = x_ref[:, cols]
            d = dq_ref[:, cols] if j < R - 2 else dk_ref[...]
            rstd = lax.rsqrt(_half_sums(xv * xv, lo) / HEAD_DIM + EPS)
            xh = xv * rstd
            dyn = d * c + _rope_swap(d * s, lane64)
            dg_ref[j:j + 1, :] += jnp.sum(dyn * xh, axis=0, keepdims=True)
            dxh = dyn * g_ref[j:j + 1, :]
            proj = _half_sums(dxh * xh, lo) / HEAD_DIM
            o_ref[:, cols] = (rstd * (dxh - xh * proj)).astype(BF16)
        o_ref[:, (R - 1) * LANES:] = dv_ref[...].astype(BF16)

    tok = pl.BlockSpec((tt, W), lambda t: (t, 0))
    tab = pl.BlockSpec((tt, LANES), lambda t: (t, 0))
    gsp = pl.BlockSpec((R, LANES), lambda t: (0, 0))
    return pl.pallas_call(
        body, name="qk_prep_bwd", grid=(T // tt,),
        in_specs=[tok, pl.BlockSpec((tt, QW), lambda t: (t, 0)), tab, tab, tab, tab, gsp], out_specs=[tok, gsp],
        out_shape=[jax.ShapeDtypeStruct((T, W), BF16), jax.ShapeDtypeStruct((R, LANES), F32)],
        compiler_params=_params(("arbitrary",)),
    )(qkv, dq, dk, dv, tab_c, tab_s, gains)


def _band_mask(i):
    r = lax.broadcasted_iota(jnp.int32, (2 * BLOCK, 2 * BLOCK), 0) & (BLOCK - 1)
    c = lax.broadcasted_iota(jnp.int32, (2 * BLOCK, 2 * BLOCK), 1)
    rel = r + BLOCK - c
    return (rel >= 0) & (rel < BLOCK) & ((c >= BLOCK) | (i > 0))


def _swa_softmax(s, valid, sink):
    s = jnp.where(valid, s * ATTN_SCALE, NEG_BIG)
    m = jnp.maximum(jnp.max(s, axis=1, keepdims=True), sink)
    p = jnp.exp(s - m)
    ps = jnp.exp(sink - m)
    denom = jnp.sum(p, axis=1, keepdims=True) + ps
    return p / denom, ps / denom


A_GROUP = 4


Q_WIDTH_A = N_Q_A * HEAD_DIM
N_PAIR_A = Q_WIDTH_A // LANES


def _swa_specs():
    qs = pl.BlockSpec((None, BLOCK, Q_WIDTH_A), lambda b, i: (b, i, 0))

    def kv(col, back):
        return pl.BlockSpec((None, BLOCK, LANES), lambda b, i: (b, jnp.maximum(i - back, 0), col))

    return qs, kv(N_PAIR_A, 1), kv(N_PAIR_A, 0), kv(N_PAIR_A + 1, 1), kv(N_PAIR_A + 1, 0)


def _dup_heads(t):
    lo = lax.broadcasted_iota(jnp.int32, t.shape, 1) < HEAD_DIM
    sw = pltpu.roll(t.astype(F32), HEAD_DIM, 1).astype(BF16)
    return jnp.where(lo, t, sw), jnp.where(lo, sw, t)


def _kv_tiles(kp_ref, kc_ref, vp_ref, vc_ref):
    kd = _dup_heads(jnp.concatenate([kp_ref[...], kc_ref[...]], axis=0))
    vd = _dup_heads(jnp.concatenate([vp_ref[...], vc_ref[...]], axis=0))
    return kd, vd


def _attn_a_fwd(qkn, sinks):
    NB, S, _ = qkn.shape
    qs, kp, kc, vp, vc = _swa_specs()

    def body(q_ref, kp_ref, kc_ref, vp_ref, vc_ref, sink_ref, o_ref):
        i = pl.program_id(1)
        kd, vd = _kv_tiles(kp_ref, kc_ref, vp_ref, vc_ref)
        valid = _band_mask(i)
        lo = lax.broadcasted_iota(jnp.int32, (BLOCK, LANES), 1) < HEAD_DIM
        top = lax.broadcasted_iota(jnp.int32, (2 * BLOCK, 1), 0) < BLOCK
        for first in range(0, N_PAIR_A, A_GROUP):
            pairs = range(first, first + A_GROUP)
            qs_ = [jnp.concatenate(_head_halves(q_ref[:, p * LANES:(p + 1) * LANES], lo), axis=0) for p in pairs]
            ss = [lax.dot_general(q, kd[2 * p // GROUP_A], _NT, preferred_element_type=F32) for q, p in zip(qs_, pairs)]
            pns = [_swa_softmax(s, valid, jnp.where(top, sink_ref[2 * p], sink_ref[2 * p + 1]))[0]
                   for s, p in zip(ss, pairs)]
            pvs = [jnp.dot(pn.astype(BF16), vd[2 * p // GROUP_A], preferred_element_type=F32) for pn, p in zip(pns, pairs)]
            for pv, p in zip(pvs, pairs):
                o_ref[:, p * LANES:(p + 1) * LANES] = jnp.where(lo, pv[:BLOCK], pv[BLOCK:]).astype(BF16)

    return pl.pallas_call(
        body, name="attn_a_fwd", grid=(NB, S // BLOCK),
        in_specs=[qs, kp, kc, vp, vc, pl.BlockSpec(memory_space=pltpu.SMEM)],
        out_specs=qs, out_shape=jax.ShapeDtypeStruct((NB, S, Q_WIDTH_A), BF16),
        compiler_params=_params(("parallel", "arbitrary")),
    )(qkn, qkn, qkn, qkn, qkn, sinks)


def _attn_a_bwd(qkn, do, sinks):
    NB, S, _ = qkn.shape
    qs, kp, kc, vp, vc = _swa_specs()
    full = pl.BlockSpec((None, S, LANES), lambda b, i: (b, 0, 0))
    sink_out = pl.BlockSpec((None, N_Q_A, LANES), lambda b, i: (b, 0, 0))

    def body(q_ref, do_ref, kp_ref, kc_ref, vp_ref, vc_ref, sink_ref, dq_ref, dk_ref, dv_ref, ds_ref, dk_s, dv_s):
        i = pl.program_id(1)

        @pl.when(i == 0)
        def _():
            dk_ref[...] = jnp.zeros_like(dk_ref)
            dv_ref[...] = jnp.zeros_like(dv_ref)
            ds_ref[...] = jnp.zeros_like(ds_ref)

        dk_s[...] = jnp.zeros_like(dk_s)
        dv_s[...] = jnp.zeros_like(dv_s)
        kd, vd = _kv_tiles(kp_ref, kc_ref, vp_ref, vc_ref)
        valid = _band_mask(i)
        lo = lax.broadcasted_iota(jnp.int32, (BLOCK, LANES), 1) < HEAD_DIM
        top = lax.broadcasted_iota(jnp.int32, (2 * BLOCK, 1), 0) < BLOCK
        for first in range(0, N_PAIR_A, A_GROUP):
            pairs = range(first, first + A_GROUP)
            kvs = [2 * p // GROUP_A for p in pairs]
            qs_ = [jnp.concatenate(_head_halves(q_ref[:, p * LANES:(p + 1) * LANES], lo), axis=0) for p in pairs]
            dos = [jnp.concatenate(_head_halves(do_ref[:, p * LANES:(p + 1) * LANES], lo), axis=0) for p in pairs]
            ss = [lax.dot_general(q, kd[kv], _NT, preferred_element_type=F32) for q, kv in zip(qs_, kvs)]
            dps = [lax.dot_general(d, vd[kv], _NT, preferred_element_type=F32) for d, kv in zip(dos, kvs)]
            sm = [_swa_softmax(s, valid, jnp.where(top, sink_ref[2 * p], sink_ref[2 * p + 1])) for s, p in zip(ss, pairs)]
            deltas = [jnp.sum(pn * dp, axis=1, keepdims=True) for (pn, _), dp in zip(sm, dps)]
            dsbs = [(pn * (dp - delta) * ATTN_SCALE).astype(BF16) for (pn, _), dp, delta in zip(sm, dps, deltas)]
            for n, p in enumerate(pairs):
                dq2 = jnp.dot(dsbs[n], kd[kvs[n]], preferred_element_type=F32)
                dq_ref[:, p * LANES:(p + 1) * LANES] = jnp.where(lo, dq2[:BLOCK], dq2[BLOCK:])
                dk_s[kvs[n]] += lax.dot_general(dsbs[n], qs_[n], _TN, preferred_element_type=F32)
                dv_s[kvs[n]] += lax.dot_general(sm[n][0].astype(BF16), dos[n], _TN, preferred_element_type=F32)
                t = sm[n][1] * deltas[n]
                for hh in range(2):
                    dsink = -jnp.sum(t[hh * BLOCK:(hh + 1) * BLOCK], axis=0, keepdims=True)
                    ds_ref[2 * p + hh:2 * p + hh + 1, :] += jnp.broadcast_to(dsink, (1, LANES))

        lo2 = lax.broadcasted_iota(jnp.int32, (2 * BLOCK, LANES), 1) < HEAD_DIM

        def fold(acc):
            halves = [acc[kv] + pltpu.roll(acc[kv], HEAD_DIM, 1) for kv in range(N_KV_A)]
            return jnp.where(lo2, halves[0], halves[1])

        dk2, dv2 = fold(dk_s), fold(dv_s)

        @pl.when(i > 0)
        def _():
            start = pl.multiple_of((i - 1) * BLOCK, BLOCK)
            dk_ref[pl.ds(start, 2 * BLOCK), :] += dk2
            dv_ref[pl.ds(start, 2 * BLOCK), :] += dv2

        @pl.when(i == 0)
        def _():
            dk_ref[0:BLOCK, :] += dk2[BLOCK:, :]
            dv_ref[0:BLOCK, :] += dv2[BLOCK:, :]

    slots = pltpu.VMEM((N_KV_A, 2 * BLOCK, LANES), F32)
    return pl.pallas_call(
        body, name="attn_a_bwd", grid=(NB, S // BLOCK),
        in_specs=[qs, qs, kp, kc, vp, vc, pl.BlockSpec(memory_space=pltpu.SMEM)],
        out_specs=[qs, full, full, sink_out],
        out_shape=[jax.ShapeDtypeStruct((NB, S, Q_WIDTH_A), F32), jax.ShapeDtypeStruct((NB, S, LANES), F32),
                   jax.ShapeDtypeStruct((NB, S, LANES), F32), jax.ShapeDtypeStruct((NB, N_Q_A, LANES), F32)],
        scratch_shapes=[slots, slots],
        compiler_params=_params(("parallel", "arbitrary")),
    )(qkn, do, qkn, qkn, qkn, qkn, sinks)


def _cumsum_mats():
    src = lax.broadcasted_iota(jnp.int32, (2 * BLOCK, 2 * BLOCK), 0) % BLOCK
    dst = lax.broadcasted_iota(jnp.int32, (2 * BLOCK, 2 * BLOCK), 1)
    ones = dst >= BLOCK
    rev = ((src > dst) | ones).astype(BF16)
    fwd = ((src < dst) | ones).astype(BF16)
    return rev, fwd


def _log_sigmoids(z):
    sp = jnp.log(1.0 + jnp.exp(-jnp.abs(z)))
    return jnp.minimum(z, 0.0) - sp, -(jnp.maximum(z, 0.0) + sp)


def _cumsum_mxu_many(vs, mat):
    parts = []
    for v in vs:
        hi = v.astype(BF16)
        parts.append(jnp.concatenate([hi, (v - hi.astype(F32)).astype(BF16)], axis=1))
    r = jnp.dot(jnp.concatenate(parts, axis=0), mat, preferred_element_type=F32)
    return [(r[n * BLOCK:(n + 1) * BLOCK, :BLOCK], r[n * BLOCK:(n + 1) * BLOCK, BLOCK:]) for n in range(len(vs))]


def _strict_mask():
    r = lax.broadcasted_iota(jnp.int32, (BLOCK, BLOCK), 0)
    c = lax.broadcasted_iota(jnp.int32, (BLOCK, BLOCK), 1)
    return c < r


def _tile(ref, j):
    return ref[pl.ds(pl.multiple_of(j * BLOCK, BLOCK), BLOCK), :]


SWEEP_EXIT = -88.0


def _head_halves(t, lo):
    zero = jnp.zeros_like(t)
    return jnp.where(lo, t, zero), jnp.where(lo, zero, t)


def _sb_specs(S, HD, width):
    n = HD // width
    blk = pl.BlockSpec((None, BLOCK, width), lambda b, p, i: (b, i, p))
    k_full = pl.BlockSpec((None, S, width), lambda b, p, i: (b, 0, n + p))
    v_full = pl.BlockSpec((None, S, width), lambda b, p, i: (b, 0, 2 * n + p))
    mat = pl.BlockSpec((2 * BLOCK, 2 * BLOCK), lambda b, p, i: (0, 0))
    return blk, k_full, v_full, mat


SB_FWD_PAIRS = 4
SB_BWD_PAIRS = 2
SB_BWD_TILES = 2
SB_BWD_VMEM_LIMIT_BYTES = 58 * 1024 * 1024


def _attn_b_fwd(qkv, rev):
    NB, S, W = qkv.shape
    HD = W // 3
    width = SB_FWD_PAIRS * LANES
    n_heads = 2 * SB_FWD_PAIRS
    blk, k_full, v_full, mat = _sb_specs(S, HD, width)

    def body(q_ref, k_ref, v_ref, rev_ref, o_ref):
        i = pl.program_id(2)
        rv = rev_ref[...]
        mask = _strict_mask()
        lo = lax.broadcasted_iota(jnp.int32, (BLOCK, LANES), 1) < HEAD_DIM
        q_all = q_ref[...]
        q_stack = [jnp.concatenate(_head_halves(q_all[:, p * LANES:(p + 1) * LANES] * ATTN_SCALE, lo), axis=0)
                   for p in range(SB_FWD_PAIRS)]

        def pair_tiles(ref, j):
            t = _tile(ref, j)
            return [t[:, p * LANES:(p + 1) * LANES] for p in range(SB_FWD_PAIRS)]

        def tiles_pass(js, carries, diagonal_first, last_counts=None):
            zs, v_tiles = [], []
            for j in js:
                ks = pair_tiles(k_ref, j)
                v_tiles.append(pair_tiles(v_ref, j))
                for p in range(SB_FWD_PAIRS):
                    z2 = lax.dot_general(q_stack[p], ks[p], _NT, preferred_element_type=F32)
                    zs += [z2[:BLOCK], z2[BLOCK:]]
            logs = [_log_sigmoids(z) for z in zs]
            masked = [jnp.where(mask, lm, 0.0) if diagonal_first and n < n_heads else lm
                      for n, (_, lm) in enumerate(logs)]
            cums = _cumsum_mxu_many(masked, rv)
            probs, new_c = {}, []
            for h in range(n_heads):
                carry = None if diagonal_first else carries[h]
                for t in range(len(js)):
                    n = t * n_heads + h
                    after, rs = cums[n]
                    if diagonal_first and t == 0:
                        a = jnp.where(mask, jnp.exp(logs[n][0] + after), 0.0)
                        carry = rs
                    else:
                        a = jnp.exp(logs[n][0] + after + carry)
                        carry = carry + rs
                    if last_counts is not None and t == len(js) - 1:
                        a = jnp.where(last_counts, a, 0.0)
                    probs[t, h] = a.astype(BF16)
                new_c.append(carry)
            outs = []
            for p in range(SB_FWD_PAIRS):
                total = None
                for t in range(len(js)):
                    pv = jnp.dot(jnp.concatenate([probs[t, 2 * p], probs[t, 2 * p + 1]], axis=0), v_tiles[t][p],
                                 preferred_element_type=F32)
                    part = jnp.where(lo, pv[:BLOCK], pv[BLOCK:])
                    total = part if total is None else total + part
                outs.append(total)
            return new_c, outs

        carries, accs = tiles_pass([i, jnp.maximum(i - 1, 0)], None, True, last_counts=i > 0)

        def live(cs):
            top = cs[0]
            for c in cs[1:]:
                top = jnp.maximum(top, c)
            return jnp.max(top) > SWEEP_EXIT

        def cond(st):
            return (st[0] < i - 1) & st[1]

        def step(st):
            jj, _, cs, accs = st
            new_c, outs = tiles_pass([i - 2 - jj], cs, False)
            return jj + 1, live(new_c), new_c, [acc + o for acc, o in zip(accs, outs)]

        st = lax.while_loop(cond, step, (jnp.int32(0), live(carries), carries, accs))
        for p in range(SB_FWD_PAIRS):
            o_ref[:, p * LANES:(p + 1) * LANES] = st[3][p].astype(BF16)

    return pl.pallas_call(
        body, name="attn_b_fwd", grid=(NB, HD // width, S // BLOCK),
        in_specs=[blk, k_full, v_full, mat], out_specs=blk,
        out_shape=jax.ShapeDtypeStruct((NB, S, HD), BF16),
        compiler_params=_params(("parallel", "parallel", "arbitrary")),
    )(qkv, qkv, qkv, rev)


def _attn_b_bwd(qkv, do, rev, fwd):
    NB, S, W = qkv.shape
    HD = W // 3
    width = SB_BWD_PAIRS * LANES
    n_heads = 2 * SB_BWD_PAIRS
    nj = S // BLOCK
    blk, k_full, v_full, mat = _sb_specs(S, HD, width)
    acc_full = pl.BlockSpec((None, S, width), lambda b, p, i: (b, 0, p))

    def body(q_ref, do_ref, k_ref, v_ref, rev_ref, fwd_ref, dq_ref, dk_ref, dv_ref, sig_s, a_s, e_s):
        i = pl.program_id(2)

        @pl.when(i == 0)
        def _():
            dk_ref[...] = jnp.zeros_like(dk_ref)
            dv_ref[...] = jnp.zeros_like(dv_ref)

        rv, fw = rev_ref[...], fwd_ref[...]
        mask = _strict_mask()
        lo = lax.broadcasted_iota(jnp.int32, (BLOCK, LANES), 1) < HEAD_DIM
        pairs = range(SB_BWD_PAIRS)

        def cols(p):
            return slice(p * LANES, (p + 1) * LANES)

        q_stack = [jnp.concatenate(_head_halves(q_ref[:, cols(p)], lo), axis=0) for p in pairs]
        qs_stack = [q * ATTN_SCALE for q in q_stack]
        do_stack = [jnp.concatenate(_head_halves(do_ref[:, cols(p)], lo), axis=0) for p in pairs]

        def sweep1_tiles(js, carries, diagonal_first):
            zs, das = [], []
            for j in js:
                kj, vj = _tile(k_ref, j), _tile(v_ref, j)
                for p in pairs:
                    z2 = lax.dot_general(qs_stack[p], kj[:, cols(p)], _NT, preferred_element_type=F32)
                    da2 = lax.dot_general(do_stack[p], vj[:, cols(p)], _NT, preferred_element_type=F32)
                    zs += [z2[:BLOCK], z2[BLOCK:]]
                    das += [da2[:BLOCK], da2[BLOCK:]]
            logs = [_log_sigmoids(z) for z in zs]
            cums = _cumsum_mxu_many([jnp.where(mask, lm, 0.0) if diagonal_first and n < n_heads else lm
                                     for n, (_, lm) in enumerate(logs)], rv)
            new_c, stores = [], []
            for h in range(n_heads):
                carry = None if diagonal_first else carries[h]
                for t, j in enumerate(js):
                    n = t * n_heads + h
                    lb, (after, rs) = logs[n][0], cums[n]
                    if diagonal_first and t == 0:
                        a = jnp.where(mask, jnp.exp(lb + after), 0.0)
                        carry = rs
                    else:
                        a = jnp.exp(lb + after + carry)
                        carry = carry + rs
                    stores.append((t, h, j, jnp.exp(lb), a.astype(BF16), das[n] * a))
                new_c.append(carry)
            for t, h, j, sg, ab, e in sorted(stores, key=lambda s: -s[0]):
                sig_s[h, j] = sg
                a_s[h, j] = ab
                e_s[h, j] = e
            return new_c

        carries = sweep1_tiles([jnp.maximum(i - t, 0) for t in range(SB_BWD_TILES + 1)], None, True)
        done = SB_BWD_TILES

        def live(cs):
            top = cs[0]
            for c in cs[1:]:
                top = jnp.maximum(top, c)
            return jnp.max(top) > SWEEP_EXIT

        def cond(st):
            return (done + SB_BWD_TILES * st[0] < i) & st[1]

        def sweep1(st):
            first = i - 1 - done - SB_BWD_TILES * st[0]
            new_c = sweep1_tiles([jnp.maximum(first - t, 0) for t in range(SB_BWD_TILES)], st[2], False)
            return st[0] + 1, live(new_c), new_c

        trips = lax.while_loop(cond, sweep1, (jnp.int32(0), live(carries), carries))[0]
        lowest = jnp.maximum(i - done - SB_BWD_TILES * trips, 0)

        def grads(js, st, diagonal_last=False, counts=None):
            prefixes, dqs = st
            counts = counts or [None] * len(js)
            es = [e_s[h, j] for j in js for h in range(n_heads)]
            cums = _cumsum_mxu_many(es, fw)
            dzs, new_p = [], []
            for h in range(n_heads):
                prefix = prefixes[h]
                for t, j in enumerate(js):
                    n = t * n_heads + h
                    sg = sig_s[h, j]
                    e_before, rs = cums[n]
                    dz = (es[n] * (1.0 - sg) - (e_before + prefix) * sg) * ATTN_SCALE
                    if diagonal_last and t == len(js) - 1:
                        dz = jnp.where(mask, dz, 0.0)
                    if counts[t] is not None:
                        dz = jnp.where(counts[t], dz, 0.0)
                        rs = jnp.where(counts[t], rs, 0.0)
                    dzs.append((t, h, dz.astype(BF16)))
                    prefix = prefix + rs
                new_p.append(prefix)
            dz_of = {(t, h): dz for t, h, dz in dzs}
            new_dq = list(dqs)
            for t, j in enumerate(js):
                kj = _tile(k_ref, j)
                rows = pl.ds(pl.multiple_of(j * BLOCK, BLOCK), BLOCK)
                for p in pairs:
                    dz_stack = jnp.concatenate([dz_of[t, 2 * p], dz_of[t, 2 * p + 1]], axis=0)
                    a_stack = jnp.concatenate([a_s[2 * p, j], a_s[2 * p + 1, j]], axis=0)
                    if counts[t] is not None:
                        a_stack = jnp.where(counts[t], a_stack, jnp.zeros_like(a_stack))
                    dq2 = jnp.dot(dz_stack, kj[:, cols(p)], preferred_element_type=F32)
                    new_dq[p] = new_dq[p] + jnp.where(lo, dq2[:BLOCK], dq2[BLOCK:])
                    dk_ref[rows, cols(p)] += lax.dot_general(dz_stack, q_stack[p], _TN, preferred_element_type=F32)
                    dv_ref[rows, cols(p)] += lax.dot_general(a_stack, do_stack[p], _TN, preferred_element_type=F32)
            return new_p, new_dq

        zeros = jnp.zeros((BLOCK, BLOCK), F32)
        st = ([zeros] * n_heads, [zeros] * SB_BWD_PAIRS)
        count = jnp.maximum(i - done, 0) - lowest
        st = lax.fori_loop(0, count % SB_BWD_TILES, lambda t, st: grads([lowest + t], st), st)
        start = lowest + count % SB_BWD_TILES
        st = lax.fori_loop(0, count // SB_BWD_TILES,
                           lambda t, st: grads([start + SB_BWD_TILES * t + u for u in range(SB_BWD_TILES)], st), st)
        top = [jnp.maximum(i - t, 0) for t in range(SB_BWD_TILES, -1, -1)]
        dqs = grads(top, st, diagonal_last=True, counts=[i >= t for t in range(SB_BWD_TILES, 0, -1)] + [None])[1]
        for p in pairs:
            dq_ref[:, cols(p)] = dqs[p]

    f32_stash = pltpu.VMEM((n_heads, nj, BLOCK, BLOCK), F32)
    bf16_stash = pltpu.VMEM((n_heads, nj, BLOCK, BLOCK), BF16)
    return pl.pallas_call(
        body, name="attn_b_bwd", grid=(NB, HD // width, nj),
        in_specs=[blk, blk, k_full, v_full, mat, mat], out_specs=[blk, acc_full, acc_full],
        out_shape=[jax.ShapeDtypeStruct((NB, S, HD), F32)] * 3,
        scratch_shapes=[f32_stash, bf16_stash, f32_stash],
        compiler_params=_params(("parallel", "parallel", "arbitrary"), SB_BWD_VMEM_LIMIT_BYTES),
    )(qkv, do, qkv, qkv, rev, fwd)


def _ada_fwd(c_all, w, b):
    L, D, N = w.shape
    B = c_all.shape[0]

    def body(c_ref, w_ref, b_ref, o_ref):
        cv = c_ref[...]
        cond = (cv * _sigmoid(cv)).astype(BF16)
        o_ref[...] = jnp.dot(cond, w_ref[...].astype(BF16), preferred_element_type=F32) + b_ref[...]

    return pl.pallas_call(
        body, name="ada_fwd", grid=(L,),
        in_specs=[pl.BlockSpec((B, D), lambda l: (0, 0)), pl.BlockSpec((None, D, N), lambda l: (l, 0, 0)),
                  pl.BlockSpec((None, 1, N), lambda l: (l, 0, 0))],
        out_specs=pl.BlockSpec((None, B, N), lambda l: (l, 0, 0)),
        out_shape=jax.ShapeDtypeStruct((L, B, N), F32),
        compiler_params=_params(("parallel",)),
    )(c_all, w, b)


def _ada_bwd(c_all, dmod_all, dmod_shard):
    L, B, N = dmod_shard.shape
    D = c_all.shape[1]
    N_all = dmod_all.shape[2]

    def body(c_ref, da_ref, ds_ref, gw_ref, gb_ref):
        cv = c_ref[...]
        cond = (cv * _sigmoid(cv)).astype(BF16)
        gw_ref[...] = lax.dot_general(cond, ds_ref[...].astype(BF16), _TN, preferred_element_type=F32)
        gb_ref[...] = jnp.sum(da_ref[...], axis=0, keepdims=True)

    return pl.pallas_call(
        body, name="ada_bwd", grid=(L,),
        in_specs=[pl.BlockSpec((B, D), lambda l: (0, 0)), pl.BlockSpec((None, B, N_all), lambda l: (l, 0, 0)),
                  pl.BlockSpec((None, B, N), lambda l: (l, 0, 0))],
        out_specs=[pl.BlockSpec((None, D, N), lambda l: (l, 0, 0)), pl.BlockSpec((None, 1, N_all), lambda l: (l, 0, 0))],
        out_shape=[jax.ShapeDtypeStruct((L, D, N), F32), jax.ShapeDtypeStruct((L, 1, N_all), F32)],
        compiler_params=_params(("parallel",)),
    )(c_all, dmod_all, dmod_shard)


def _adamw(w, g, m, v, name):
    shape = w.shape
    if w.ndim == 2:
        w, g, m, v = [t.reshape((1,) + shape) for t in (w, g, m, v)]
    L, R, C = w.shape
    tr = _pick(R, max(8, (1 << 18) // C), 8)
    c1 = 1.0 - ADAM_B1 ** ADAM_STEP
    c2 = 1.0 - ADAM_B2 ** ADAM_STEP

    def body(w_ref, g_ref, m_ref, v_ref, d_ref, nm_ref, nv_ref):
        gv = g_ref[...]
        nm = ADAM_B1 * m_ref[...] + (1.0 - ADAM_B1) * gv
        nv = ADAM_B2 * v_ref[...] + (1.0 - ADAM_B2) * (gv * gv)
        d_ref[...] = -ADAM_LR * ((nm / c1) / (jnp.sqrt(nv / c2) + ADAM_EPS) + ADAM_WD * w_ref[...])
        nm_ref[...] = nm
        nv_ref[...] = nv

    spec = pl.BlockSpec((None, tr, C), lambda l, r: (l, r, 0))
    out = pl.pallas_call(
        body, name=name, grid=(L, R // tr), in_specs=[spec] * 4, out_specs=[spec] * 3,
        out_shape=[jax.ShapeDtypeStruct((L, R, C), F32)] * 3,
        compiler_params=_params(("parallel", "parallel")),
    )(w, g, m, v)
    return [t.reshape(shape) for t in out]


_SHARDED = (("wqkv_a", 2), ("wo_a", 1), ("wqkv_b", 2), ("wo_b", 1), ("w_gate", 2), ("w_up", 2), ("w_down", 1))


def _pack_full(layers, axis, gate_up=None):
    L = len(layers)
    R, C = layers[0].shape

    def shards(m):
        if gate_up is not None:
            F = C // 2
            tf, Cs = _ff_tile(F), F // 4
            assert tf % Cs == 0
            starts = [(2 * (s * Cs // tf) + gate_up) * tf + s * Cs % tf for s in range(4)]
            return jnp.stack([m[:, st:st + Cs] for st in starts])
        if axis == 2:
            return m.reshape(R, 4, C // 4).transpose(1, 0, 2)
        return m.reshape(4, R // 4, C)

    halves = [jnp.stack([shards(m) for m in layers[h * (L // 2):(h + 1) * (L // 2)]], axis=1) for h in range(2)]
    return jnp.stack(halves)


def _unpack_full(gathered, axis):
    _, Lh, Rs, Cs = gathered.shape
    t = gathered.reshape(4, 2, Lh, Rs, Cs)
    layers = []
    for h in range(2):
        for l in range(Lh):
            piece = t[:, h, l]
            if axis == 2:
                layers.append(piece.transpose(1, 0, 2).reshape(Rs, 4 * Cs))
            else:
                layers.append(piece.reshape(4 * Rs, Cs))
    return layers


def _sum_slabs(own, recv, name, with_bf16=False):
    C = own.shape[-1]
    out = _sum_leading(recv.reshape(recv.shape[0], -1, C), name, own=own.reshape(-1, C), with_bf16=with_bf16)
    if with_bf16:
        return out[0].reshape(own.shape), out[1].reshape(own.shape)
    return out.reshape(own.shape)


def _gather8(x, name):
    return _all_gather8([x], name)[0]


def _rope_tables(positions):
    half = ROT_DIM // 2
    inv_freq = jnp.power(jnp.float32(ROPE_THETA), -jnp.arange(half, dtype=F32) * 2.0 / ROT_DIM)
    ang = positions.astype(F32).reshape(-1, 1) * inv_freq
    cos, sin = jnp.cos(ang), jnp.sin(ang)
    T = ang.shape[0]
    rest = HEAD_DIM - ROT_DIM
    c64 = jnp.concatenate([cos, cos, jnp.ones((T, rest), F32)], axis=1)
    s64 = jnp.concatenate([-sin, sin, jnp.zeros((T, rest), F32)], axis=1)
    return jnp.tile(c64, (1, 2)), jnp.tile(s64, (1, 2))


def _gain_rows(q_gain, k_gain):
    q2 = jnp.tile(q_gain.reshape(1, HEAD_DIM), (GROUP_A, 2))
    k2 = jnp.tile(k_gain.reshape(1, HEAD_DIM), (1, 2))
    return jnp.concatenate([q2, k2, jnp.ones((1, LANES), F32)], axis=0)


def _local_step(x, positions, mod, norm1_g, norm2_g, q_norm_a, k_norm_a, sinks_a,
                wqkv_a, wo_a, wqkv_b, wo_b, wgu, wd, loss_target):
    NB, S, D = x.shape
    T = NB * S
    QA = N_Q_A * HEAD_DIM
    tab_c, tab_s = _rope_tables(positions)
    rev, fwd = _cumsum_mats()

    saved = []
    xc = x
    mods = [[mod[i][:, k * D:(k + 1) * D].reshape(NB, 1, D) for k in range(6)] for i in range(DEPTH)]
    h = _norm_mod_fwd(xc, norm1_g[0:1], mods[0][1], mods[0][0])
    for i in range(DEPTH):
        j = i // 2
        sh1, sc1, g1, sh2, sc2, g2 = mods[i]
        st = dict(x=xc, sc1=sc1, g1=g1, sc2=sc2, g2=g2)
        st["h"] = h.reshape(T, D)
        if i % 2 == 0:
            st["qkv"] = _matmul(st["h"], wqkv_a[j], "nn", F32, "qkv_a")
            st["gains"] = _gain_rows(q_norm_a[j], k_norm_a[j])
            st["qkn"] = _qk_prep_fwd(st["qkv"], tab_c, tab_s, st["gains"]).reshape(NB, S, -1)
            st["o"] = _attn_a_fwd(st["qkn"], sinks_a[j]).reshape(T, QA)
            y = _matmul(st["o"], wo_a[j], "nn", F32, "wo_a")
        else:
            st["qkv"] = _matmul(st["h"], wqkv_b[j], "nn", BF16, "qkv_b").reshape(NB, S, -1)
            st["o"] = _attn_b_fwd(st["qkv"], rev).reshape(T, N_H_B * HEAD_DIM)
            y = _matmul(st["o"], wo_b[j], "nn", F32, "wo_b")
        st["y"] = y.reshape(NB, S, D)
        x1, h2 = _gate_res(xc, st["y"], g1, norm=(norm2_g[i:i + 1], sc2, sh2))
        st["x1"] = x1
        st["h2"] = h2.reshape(T, D)
        st["gu"], st["act"] = _matmul(st["h2"], wgu[i], "nn", BF16, "gate_up", swiglu=True)
        st["m"] = _matmul(st["act"], wd[i], "nn", F32, "down").reshape(NB, S, D)
        if i + 1 < DEPTH:
            xc, h = _gate_res(x1, st["m"], g2, norm=(norm1_g[i + 1:i + 2], mods[i + 1][1], mods[i + 1][0]))
        else:
            xc = _gate_res(x1, st["m"], g2)
        saved.append(st)

    loss, dx = _loss_fwd_bwd(xc, loss_target)

    grads = {name: [None] * n for name, n in
             (("wqkv_a", 2), ("wo_a", 2), ("wqkv_b", 2), ("wo_b", 2), ("wgu", DEPTH), ("wd", DEPTH),
              ("norm1_g", DEPTH), ("norm2_g", DEPTH), ("q_norm_a", 2), ("k_norm_a", 2), ("sinks_a", 2))}
    dmod = [None] * DEPTH
    dm, dg2 = _gate_res_bwd(dx, saved[-1]["m"], saved[-1]["g2"])
    for i in reversed(range(DEPTH)):
        j = i // 2
        st = saved[i]
        dm = dm.reshape(T, D)
        grads["wd"][i] = _matmul(st["act"], dm, "tn", F32, "d_wd")
        dgu = _swiglu_bwd(dm, wd[i], st["gu"])
        grads["wgu"][i] = _matmul(st["h2"], dgu, "tn", F32, "d_wgu")
        dx1, dsh2, dsc2, grads["norm2_g"][i], dy, dg1 = _norm_mod_bwd(
            dgu, wgu[i], st["x1"], norm2_g[i:i + 1], st["sc2"], dx, "d_h2", y=st["y"], g=st["g1"])
        dy = dy.reshape(T, D)
        if i % 2 == 0:
            do = _matmul(dy, wo_a[j], "nt", BF16, "d_o_a").reshape(NB, S, QA)
            grads["wo_a"][j] = _matmul(st["o"], dy, "tn", F32, "d_wo_a")
            dq, dk, dv, dsink = _attn_a_bwd(st["qkn"], do, sinks_a[j])
            dqkv, dgain = _qk_prep_bwd(st["qkv"], dq.reshape(T, QA), dk.reshape(T, LANES), dv.reshape(T, LANES),
                                       tab_c, tab_s, st["gains"])
            w_in = wqkv_a[j]
            grads["wqkv_a"][j] = _matmul(st["h"], dqkv, "tn", F32, "d_wqkv_a")
            grads["q_norm_a"][j] = jnp.sum(dgain[:GROUP_A].reshape(2 * GROUP_A, HEAD_DIM), axis=0)
            grads["k_norm_a"][j] = jnp.sum(dgain[GROUP_A].reshape(2, HEAD_DIM), axis=0)
            grads["sinks_a"][j] = jnp.sum(dsink[..., 0], axis=0)
        else:
            do = _matmul(dy, wo_b[j], "nt", BF16, "d_o_b").reshape(NB, S, -1)
            grads["wo_b"][j] = _matmul(st["o"], dy, "tn", F32, "d_wo_b")
            dq, dk, dv = _attn_b_bwd(st["qkv"], do, rev, fwd)
            dqkv = jnp.concatenate([dq, dk, dv], axis=-1).reshape(T, -1).astype(BF16)
            w_in = wqkv_b[j]
            grads["wqkv_b"][j] = _matmul(st["h"], dqkv, "tn", F32, "d_wqkv_b")
        this_dg2 = dg2
        if i > 0:
            dx, dsh1, dsc1, grads["norm1_g"][i], dm, dg2 = _norm_mod_bwd(
                dqkv, w_in, st["x"], norm1_g[i:i + 1], st["sc1"], dx1, "d_h", y=saved[i - 1]["m"], g=saved[i - 1]["g2"])
        else:
            dx, dsh1, dsc1, grads["norm1_g"][i] = _norm_mod_bwd(
                dqkv, w_in, st["x"], norm1_g[i:i + 1], st["sc1"], dx1, "d_h")
        dmod[i] = jnp.concatenate([dsh1, dsc1, dg1, dsh2, dsc2, this_dg2], axis=-1).reshape(NB, 6 * D)

    matrices = ("wqkv_a", "wo_a", "wqkv_b", "wo_b", "wgu", "wd")
    grads = {name: parts if name in matrices else jnp.stack(parts) for name, parts in grads.items()}
    return loss, dx, grads, jnp.stack(dmod)


def _rows_of(flat, cols=PACK_COLS):
    n = flat.shape[0]
    pad = (-n) % (8 * cols)
    if pad:
        flat = jnp.concatenate([flat, jnp.zeros((pad,), flat.dtype)])
    return flat.reshape(-1, cols)


def kernel(x, c, positions, ada_w, ada_b, norm1_g, norm2_g, wqkv_a, q_norm_a, k_norm_a, sinks_a, wo_a, wqkv_b, wo_b, w_gate, w_up, w_down, loss_target, m_ada_w, m_ada_b, m_norm1_g, m_norm2_g, m_wqkv_a, m_q_norm_a, m_k_norm_a, m_sinks_a, m_wo_a, m_wqkv_b, m_wo_b, m_w_gate, m_w_up, m_w_down, v_ada_w, v_ada_b, v_norm1_g, v_norm2_g, v_wqkv_a, v_q_norm_a, v_k_norm_a, v_sinks_a, v_wo_a, v_wqkv_b, v_wo_b, v_w_gate, v_w_up, v_w_down):
    xi, yi, ci = lax.axis_index("x"), lax.axis_index("y"), lax.axis_index("c")
    dev = 4 * xi + 2 * yi + ci
    chip = 2 * xi + yi
    NB, S, D = x.shape
    B_all = N_DEV * NB
    L = ada_w.shape[0]
    n_mod = ada_w.shape[2] // 2

    c_all = _gather8(_rows_of(c.reshape(-1), LANES), "gather_c").reshape(N_DEV, -1)[:, :NB * D].reshape(B_all, D)
    ada_w_half = lax.dynamic_slice_in_dim(ada_w, ci * n_mod, n_mod, axis=2)
    ada_b_half = lax.dynamic_slice_in_dim(ada_b, dev * n_mod, n_mod, axis=1).reshape(L, 1, n_mod)
    mod_part = _ada_fwd(c_all, ada_w_half, ada_b_half)
    n_part = L * B_all * n_mod
    mod_all = _gather8(_rows_of(mod_part.reshape(-1)), "gather_mod").reshape(N_DEV, -1)[:, :n_part]
    mod_all = mod_all.reshape(N_DEV, L, B_all, n_mod).transpose(1, 2, 0, 3).reshape(L, B_all, N_DEV * n_mod)
    mod = lax.dynamic_slice_in_dim(mod_all, dev * NB, NB, axis=1)

    shards = dict(wqkv_a=wqkv_a, wo_a=wo_a, wqkv_b=wqkv_b, wo_b=wo_b, w_gate=w_gate, w_up=w_up, w_down=w_down)
    halves = []
    for name, _ in _SHARDED:
        w = shards[name]
        half = lax.dynamic_index_in_dim(w.reshape((2, w.shape[0] // 2) + w.shape[1:]), ci, 0, keepdims=False)
        halves.append(half.astype(BF16))
    gathered = _all_gather8(halves, "gather_weights", local_axis=1, local_chunks=8, relay_axis=1)
    full = {name: _unpack_full(t, axis) for (name, axis), t in zip(_SHARDED, gathered)}
    wgu = [_interleave(gate, up) for gate, up in zip(full["w_gate"], full["w_up"])]

    loss, grad_x, g, dmod = _local_step(
        x, positions, mod, norm1_g, norm2_g, q_norm_a, k_norm_a, sinks_a,
        full["wqkv_a"], full["wo_a"], full["wqkv_b"], full["wo_b"], wgu, full["w_down"], loss_target)

    g_full = dict(wqkv_a=g["wqkv_a"], wo_a=g["wo_a"], wqkv_b=g["wqkv_b"], wo_b=g["wo_b"],
                  w_gate=g["wgu"], w_up=g["wgu"], w_down=g["wd"])
    which = dict(w_gate=0, w_up=1)
    packed = [_pack_full(g_full[name], axis, which.get(name)) for name, axis in _SHARDED]
    def own(t, index):
        return lax.dynamic_index_in_dim(t, index, 0, keepdims=False)

    from_cores = _exchange_cores(packed, "rs_cores", chunk_axis=0, chunks=4)
    chip_part = [_sum_slabs(own(p, ci), r, "rs_add_cores", with_bf16=True) for p, r in zip(packed, from_cores)]
    from_chips = _exchange_chips([b for _, b in chip_part], "rs_chips", relay_axis=1)
    mine = [_sum_slabs(own(p, chip), r, "rs_add_chips") for (p, _), r in zip(chip_part, from_chips)]
    theirs = _sibling_send(mine, "rs_halves")
    grad = {}
    for (name, _), m, t in zip(_SHARDED, mine, theirs):
        first, second = jnp.where(ci == 0, m, t), jnp.where(ci == 0, t, m)
        grad[name] = jnp.stack([first, second]).reshape(shards[name].shape)

    small_names = ("norm1_g", "norm2_g", "q_norm_a", "k_norm_a", "sinks_a")
    small = [dmod.reshape(-1)] + [g[name].reshape(-1) for name in small_names] + [loss.reshape(-1)]
    small_sizes = [t.shape[0] for t in small]
    small_rows = _rows_of(jnp.concatenate(small))
    small_all = _gather8(small_rows, "gather_small")
    small_sum = _sum_leading(small_all, "sum_small").reshape(-1)
    n_dmod = small_sizes[0]
    dmod_all = small_all.reshape(N_DEV, -1)[:, :n_dmod].reshape(N_DEV, L, NB, 6 * D)
    dmod_all = dmod_all.transpose(1, 0, 2, 3).reshape(L, B_all, 6 * D)
    off = n_dmod
    for name, sz in zip(small_names + ("loss",), small_sizes[1:]):
        grad[name] = small_sum[off:off + sz]
        off += sz
    loss_total = grad.pop("loss").reshape(())
    for name, ref in (("norm1_g", norm1_g), ("norm2_g", norm2_g), ("q_norm_a", q_norm_a),
                      ("k_norm_a", k_norm_a), ("sinks_a", sinks_a)):
        grad[name] = grad[name].reshape(ref.shape)

    n_shard = ada_w.shape[2]
    dmod_shard = lax.dynamic_slice_in_dim(dmod_all, chip * n_shard, n_shard, axis=2)
    grad["ada_w"], gb = _ada_bwd(c_all, dmod_all, dmod_shard)
    grad["ada_b"] = gb.reshape(ada_b.shape)

    weights = dict(ada_w=ada_w, ada_b=ada_b, norm1_g=norm1_g, norm2_g=norm2_g, wqkv_a=wqkv_a, q_norm_a=q_norm_a,
                   k_norm_a=k_norm_a, sinks_a=sinks_a, wo_a=wo_a, wqkv_b=wqkv_b, wo_b=wo_b, w_gate=w_gate,
                   w_up=w_up, w_down=w_down)
    m_in = dict(ada_w=m_ada_w, ada_b=m_ada_b, norm1_g=m_norm1_g, norm2_g=m_norm2_g, wqkv_a=m_wqkv_a,
                q_norm_a=m_q_norm_a, k_norm_a=m_k_norm_a, sinks_a=m_sinks_a, wo_a=m_wo_a, wqkv_b=m_wqkv_b,
                wo_b=m_wo_b, w_gate=m_w_gate, w_up=m_w_up, w_down=m_w_down)
    v_in = dict(ada_w=v_ada_w, ada_b=v_ada_b, norm1_g=v_norm1_g, norm2_g=v_norm2_g, wqkv_a=v_wqkv_a,
                q_norm_a=v_q_norm_a, k_norm_a=v_k_norm_a, sinks_a=v_sinks_a, wo_a=v_wo_a, wqkv_b=v_wqkv_b,
                wo_b=v_wo_b, w_gate=v_w_gate, w_up=v_w_up, w_down=v_w_down)
    names = list(weights)
    delta, new_m, new_v = {}, {}, {}
    for name in names:
        delta[name], new_m[name], new_v[name] = _adamw(weights[name], grad[name], m_in[name], v_in[name],
                                                       "adamw_" + name)
    return (loss_total, grad_x, *[grad[k] for k in names], *[delta[k] for k in names],
            *[new_m[k] for k in names], *[new_v[k] for k in names])
```

```python
import jax
import jax.numpy as jnp
from jax import lax
from jax.experimental import pallas as pl
from jax.experimental.pallas import tpu as pltpu

F32 = jnp.float32
BF16 = jnp.bfloat16

DEPTH = 4
HEAD_DIM = 64
N_Q_A = 16
N_KV_A = 2
GROUP_A = N_Q_A // N_KV_A
N_H_B = 16
BLOCK = 128
ROT_DIM = HEAD_DIM // 4
ROPE_THETA = 500000.0
EPS = 1e-6
ATTN_SCALE = HEAD_DIM ** -0.5
NEG_BIG = -1e30

ADAM_LR = 0.001
ADAM_B1 = 0.9
ADAM_B2 = 0.999
ADAM_EPS = 1e-08
ADAM_WD = 0.01
ADAM_STEP = 10

N_DEV = 8
LANES = 128
PACK_COLS = 1024
VMEM_LIMIT_BYTES = 48 * 1024 * 1024
MESH = pl.DeviceIdType.MESH

_NT = (((1,), (1,)), ((), ()))
_TN = (((0,), (0,)), ((), ()))
_NN = (((1,), (0,)), ((), ()))


def _params(sem=None, vmem_limit_bytes=VMEM_LIMIT_BYTES):
    return pltpu.CompilerParams(vmem_limit_bytes=vmem_limit_bytes, dimension_semantics=sem)


def _pick(n, cap, mult):
    best = None
    for t in range(mult, min(n, cap) + 1, mult):
        if n % t == 0:
            best = t
    return n if best is None else best


_ANY = pl.BlockSpec(memory_space=pl.ANY)


def _window(index, axis, q, n, shape):
    rest = [slice(None)] * len(shape)
    size = shape[axis] // n
    rest[axis] = pl.ds(q * size, size)
    return tuple(index) + tuple(rest)


def _all_gather8(xs, name, local_axis=0, local_chunks=1, relay_axis=None):
    n = len(xs)
    n_sems = 7 if relay_axis is None else 9

    def body(*refs):
        x_refs, out_refs = refs[:n], refs[n:2 * n]
        send_sems, recv_sems, local_sems = refs[2 * n:]
        xi, yi, ci = lax.axis_index("x"), lax.axis_index("y"), lax.axis_index("c")
        me, sibling = (xi, yi, ci), (xi, yi, 1 - ci)
        chips = [(1 - xi, yi), (xi, 1 - yi), (1 - xi, 1 - yi)]

        def slab(w, px, py, pc):
            return out_refs[w].at[4 * px + 2 * py + pc]

        def copy(w, k, block, to, src=None):
            return pltpu.make_async_remote_copy(
                src_ref=slab(w, *block) if src is None else src, dst_ref=slab(w, *block),
                send_sem=send_sems.at[k, w], recv_sem=recv_sems.at[k, w], device_id=to, device_id_type=MESH)

        mine = []
        for w in range(n):
            for q in range(local_chunks):
                part = _window((), local_axis, q, local_chunks, xs[w].shape)
                mine.append(pltpu.make_async_copy(x_refs[w].at[part], slab(w, *me).at[part], local_sems.at[w, q]))
                mine[-1].start()
        direct = chips if relay_axis is None else chips[:2]
        first = [copy(w, 0, me, sibling, src=x_refs[w]) for w in range(n)]
        first += [copy(w, 1 + j, me, (*chip, ci), src=x_refs[w]) for j, chip in enumerate(direct) for w in range(n)]
        for cp in first:
            cp.start()

        def relay(w, part, block, to):
            piece = _window((), relay_axis, part, 2, xs[w].shape)
            return pltpu.make_async_remote_copy(
                src_ref=slab(w, *block).at[piece], dst_ref=slab(w, *block).at[piece],
                send_sem=send_sems.at[7 + part, w], recv_sem=recv_sems.at[7 + part, w],
                device_id=to, device_id_type=MESH)

        passed = []
        for j, chip in enumerate(direct):
            for w in range(n):
                copy(w, 1 + j, (*chip, ci), me).wait_recv()
                passed.append(copy(w, 4 + j, (*chip, ci), sibling))
                passed[-1].start()
                if relay_axis is not None:
                    passed.append(relay(w, j, (*chip, ci), (*chips[1 - j], ci)))
                    passed[-1].start()
        if relay_axis is not None:
            for w in range(n):
                for part in range(2):
                    relay(w, part, (*chips[2], ci), me).wait_recv()
                passed.append(copy(w, 6, (*chips[2], ci), sibling))
                passed[-1].start()
        for w in range(n):
            copy(w, 0, sibling, me).wait_recv()
        for j, chip in enumerate(chips):
            for w in range(n):
                copy(w, 4 + j, (*chip, 1 - ci), me).wait_recv()
        for cp in first + passed:
            cp.wait_send()
        for cp in mine:
            cp.wait()

    return pl.pallas_call(
        body, name=name,
        out_shape=[jax.ShapeDtypeStruct((N_DEV,) + x.shape, x.dtype) for x in xs],
        in_specs=[_ANY] * n, out_specs=[_ANY] * n,
        scratch_shapes=[pltpu.SemaphoreType.DMA((n_sems, n)), pltpu.SemaphoreType.DMA((n_sems, n)),
                        pltpu.SemaphoreType.DMA((n, local_chunks))],
    )(*xs)


def _exchange_cores(xs, name, chunk_axis=0, chunks=1):
    n = len(xs)
    n_peers = 1

    def body(*refs):
        x_refs, out_refs = refs[:n], refs[n:2 * n]
        send_sems, recv_sems = refs[2 * n:]
        xi, yi, ci = lax.axis_index("x"), lax.axis_index("y"), lax.axis_index("c")
        peers = [(1 - ci, (xi, yi, 1 - ci))]
        copies = []
        for k, (p, dev) in enumerate(peers):
            for w in range(n):
                slab_shape = xs[w].shape[1:]
                for q in range(chunks):
                    copies.append(pltpu.make_async_remote_copy(
                        src_ref=x_refs[w].at[_window((p,), chunk_axis, q, chunks, slab_shape)],
                        dst_ref=out_refs[w].at[_window((k,), chunk_axis, q, chunks, slab_shape)],
                        send_sem=send_sems.at[k, w, q], recv_sem=recv_sems.at[k, w, q],
                        device_id=dev, device_id_type=MESH))
                    copies[-1].start()
        for cp in copies:
            cp.wait()

    return pl.pallas_call(
        body, name=name,
        out_shape=[jax.ShapeDtypeStruct((n_peers,) + x.shape[1:], x.dtype) for x in xs],
        in_specs=[_ANY] * n, out_specs=[_ANY] * n,
        scratch_shapes=[pltpu.SemaphoreType.DMA((n_peers, n, chunks)), pltpu.SemaphoreType.DMA((n_peers, n, chunks))],
    )(*xs)


def _exchange_chips(xs, name, relay_axis):
    n = len(xs)

    def half_shape(x):
        shape = list(x.shape[1:])
        shape[relay_axis] //= 2
        return tuple(shape)

    def body(*refs):
        x_refs, out_refs, hop_refs = refs[:n], refs[n:2 * n], refs[2 * n:3 * n]
        send_sems, recv_sems = refs[3 * n:]
        xi, yi, ci = lax.axis_index("x"), lax.axis_index("y"), lax.axis_index("c")
        nbr = [(1 - xi, yi, ci), (xi, 1 - yi, ci)]
        slab_of_nbr = [2 * (1 - xi) + yi, 2 * xi + (1 - yi)]
        slab_of_diag = 2 * (1 - xi) + (1 - yi)

        def copy(k, w, src, dst, to):
            return pltpu.make_async_remote_copy(src_ref=src, dst_ref=dst, send_sem=send_sems.at[k, w],
                                                recv_sem=recv_sems.at[k, w], device_id=to, device_id_type=MESH)

        def piece(w, part):
            return _window((), relay_axis, part, 2, xs[w].shape[1:])

        sent = []
        for w in range(n):
            for j in range(2):
                sent.append(copy(j, w, x_refs[w].at[slab_of_nbr[j]], out_refs[w].at[j], nbr[j]))
                sent.append(copy(2 + j, w, x_refs[w].at[(slab_of_diag,) + piece(w, j)], hop_refs[w].at[j], nbr[j]))
        for cp in sent:
            cp.start()
        for w in range(n):
            for j in range(2):
                copy(2 + j, w, hop_refs[w].at[j], hop_refs[w].at[j], nbr[j]).wait_recv()
                sent.append(copy(4 + j, w, hop_refs[w].at[j], out_refs[w].at[(2,) + piece(w, j)], nbr[1 - j]))
                sent[-1].start()
        for w in range(n):
            for j in range(2):
                copy(j, w, out_refs[w].at[j], out_refs[w].at[j], nbr[j]).wait_recv()
                half = out_refs[w].at[(2,) + piece(w, j)]
                copy(4 + j, w, half, half, nbr[1 - j]).wait_recv()
        for cp in sent:
            cp.wait_send()

    out = pl.pallas_call(
        body, name=name,
        out_shape=[jax.ShapeDtypeStruct((3,) + x.shape[1:], x.dtype) for x in xs]
        + [jax.ShapeDtypeStruct((2,) + half_shape(x), x.dtype) for x in xs],
        in_specs=[_ANY] * n, out_specs=[_ANY] * (2 * n),
        scratch_shapes=[pltpu.SemaphoreType.DMA((6, n)), pltpu.SemaphoreType.DMA((6, n))],
    )(*xs)
    return out[:n]


def _sibling_send(xs, name, chunk_axis=1, chunks=4):
    n = len(xs)

    def body(*refs):
        x_refs, out_refs = refs[:n], refs[n:2 * n]
        send_sems, recv_sems = refs[2 * n:]
        xi, yi, ci = lax.axis_index("x"), lax.axis_index("y"), lax.axis_index("c")
        copies = []
        for w in range(n):
            for q in range(chunks):
                part = _window((), chunk_axis, q, chunks, xs[w].shape)
                copies.append(pltpu.make_async_remote_copy(
                    src_ref=x_refs[w].at[part], dst_ref=out_refs[w].at[part],
                    send_sem=send_sems.at[w, q], recv_sem=recv_sems.at[w, q],
                    device_id=(xi, yi, 1 - ci), device_id_type=MESH))
                copies[-1].start()
        for cp in copies:
            cp.wait()

    return pl.pallas_call(
        body, name=name,
        out_shape=[jax.ShapeDtypeStruct(x.shape, x.dtype) for x in xs],
        in_specs=[_ANY] * n, out_specs=[_ANY] * n,
        scratch_shapes=[pltpu.SemaphoreType.DMA((n, chunks)), pltpu.SemaphoreType.DMA((n, chunks))],
    )(*xs)


def _sum_leading(x, name, own=None, with_bf16=False):
    P, R, C = x.shape
    tr = _pick(R, max(16, (1 << 19) // (C * (P + 1))), 16)

    def body(*refs):
        n_in = 1 if own is None else 2
        x_ref = refs[n_in - 1]
        acc = x_ref[0].astype(F32) if own is None else refs[0][...] + x_ref[0].astype(F32)
        for p in range(1, P):
            acc = acc + x_ref[p].astype(F32)
        refs[n_in][...] = acc
        if with_bf16:
            refs[n_in + 1][...] = acc.astype(BF16)

    flat = pl.BlockSpec((tr, C), lambda r: (r, 0))
    slabs = pl.BlockSpec((P, tr, C), lambda r: (0, r, 0))
    out = pl.pallas_call(
        body, name=name, grid=(R // tr,),
        in_specs=[slabs] if own is None else [flat, slabs],
        out_specs=[flat, flat] if with_bf16 else [flat],
        out_shape=[jax.ShapeDtypeStruct((R, C), F32)] + ([jax.ShapeDtypeStruct((R, C), BF16)] if with_bf16 else []),
        compiler_params=_params(("arbitrary",)),
    )(*([x] if own is None else [own, x]))
    return out if with_bf16 else out[0]


MATMUL_SINGLE_K = 2816
MATMUL_VMEM_BUDGET = 36 * 1024 * 1024


def _matmul(a, b, mode, out_dtype, name, swiglu=False):
    if mode == "nn":
        (M, K), N = a.shape, b.shape[1]
    elif mode == "nt":
        (M, K), N = a.shape, b.shape[0]
    else:
        (K, M), N = a.shape, b.shape[1]
    tm = _pick(M, 1024 if mode != "tn" else 1536, 128)
    tn = _pick(N, 1536, 128)
    if swiglu:
        tm, tn = _pick(M, 1024 if out_dtype == BF16 else 512, 128), 2 * _ff_tile(N // 2)
    out_bytes = jnp.dtype(out_dtype).itemsize
    tk = K
    if K > MATMUL_SINGLE_K:
        for cap in (2048, 1024, 512):
            tk = _pick(K, cap, 128)
            blocks = 2 * 2 * tk * (tm + tn) + tm * tn * (2 * out_bytes + (4 if out_dtype != F32 else 0))
            if blocks <= MATMUL_VMEM_BUDGET:
                break
    nk = K // tk
    dims = {"nn": _NN, "nt": _NT, "tn": _TN}[mode]
    use_scratch = nk > 1 and out_dtype != F32

    def body(a_ref, b_ref, *refs):
        o_ref = refs[0]

        def product():
            return lax.dot_general(a_ref[...].astype(BF16), b_ref[...].astype(BF16), dims,
                                   preferred_element_type=F32)

        if nk == 1:
            part = product()
            o_ref[...] = part.astype(o_ref.dtype)
            if swiglu:
                g = part[:, :tn // 2]
                refs[1][...] = (g * _sigmoid(g) * part[:, tn // 2:]).astype(BF16)
            return
        k = pl.program_id(2)
        acc_ref = refs[-1] if use_scratch else o_ref

        @pl.when(k == 0)
        def _():
            acc_ref[...] = jnp.zeros_like(acc_ref)

        acc_ref[...] += product()

        if use_scratch:
            @pl.when(k == nk - 1)
            def _():
                o_ref[...] = acc_ref[...].astype(o_ref.dtype)

    if mode == "tn":
        a_spec = pl.BlockSpec((tk, tm), lambda i, j, k: (k, i))
    else:
        a_spec = pl.BlockSpec((tm, tk), lambda i, j, k: (i, k))
    if mode == "nt":
        b_spec = pl.BlockSpec((tn, tk), lambda i, j, k: (j, k))
    else:
        b_spec = pl.BlockSpec((tk, tn), lambda i, j, k: (k, j))
    out_specs = [pl.BlockSpec((tm, tn), lambda i, j, k: (i, j))]
    out_shape = [jax.ShapeDtypeStruct((M, N), out_dtype)]
    if swiglu:
        assert nk == 1 and mode == "nn"
        out_specs.append(pl.BlockSpec((tm, tn // 2), lambda i, j, k: (i, j)))
        out_shape.append(jax.ShapeDtypeStruct((M, N // 2), BF16))
    out = pl.pallas_call(
        body, name=name, grid=(M // tm, N // tn, nk),
        in_specs=[a_spec, b_spec], out_specs=out_specs, out_shape=out_shape,
        scratch_shapes=[pltpu.VMEM((tm, tn), F32)] if use_scratch else [],
        compiler_params=_params(("parallel", "parallel", "arbitrary")),
    )(a, b)
    return out if swiglu else out[0]


def _row_tile(S):
    return _pick(S, 512, 8)


def _norm_mod_fwd(x, gain, sc, sh):
    NB, S, D = x.shape
    tr = _row_tile(S)

    def body(x_ref, g_ref, sc_ref, sh_ref, h_ref):
        xv = x_ref[...]
        ms = jnp.mean(xv * xv, axis=-1, keepdims=True)
        n = xv * lax.rsqrt(ms + EPS) * g_ref[...]
        h_ref[...] = (n * (1.0 + sc_ref[...]) + sh_ref[...]).astype(BF16)

    tok = pl.BlockSpec((None, tr, D), lambda b, r: (b, r, 0))
    per_ex = pl.BlockSpec((None, 1, D), lambda b, r: (b, 0, 0))
    return pl.pallas_call(
        body, name="norm_mod_fwd", grid=(NB, S // tr),
        in_specs=[tok, pl.BlockSpec((1, D), lambda b, r: (0, 0)), per_ex, per_ex],
        out_specs=tok, out_shape=jax.ShapeDtypeStruct((NB, S, D), BF16),
        compiler_params=_params(("parallel", "parallel")),
    )(x, gain, sc, sh)


def _norm_mod_bwd(a, w, x, gain, sc, dres, name, y=None, g=None):
    NB, S, D = x.shape
    T, K = a.shape
    tm = _pick(S, 512, 128)
    per_ex_tiles = S // tm
    tk = K if K <= MATMUL_SINGLE_K else _pick(K, 3072, 128)
    nk = K // tk
    gated = y is not None

    def body(*refs):
        a_ref, w_ref, x_ref, g_ref, sc_ref, dres_ref = refs[:6]
        refs = refs[6:]
        if gated:
            y_ref, gate_ref = refs[:2]
            refs = refs[2:]
        dx_ref, dsh_ref, dsc_ref, dgain_ref = refs[:4]
        acc_ref = refs[-1]
        i, k = pl.program_id(0), pl.program_id(1)

        @pl.when(k == 0)
        def _():
            acc_ref[...] = jnp.zeros_like(acc_ref)

        acc_ref[...] += lax.dot_general(a_ref[...], w_ref[...], _NT, preferred_element_type=F32)

        @pl.when(k == nk - 1)
        def _():
            first_of_example = i % per_ex_tiles == 0

            @pl.when(first_of_example)
            def _():
                dsh_ref[...] = jnp.zeros_like(dsh_ref)
                dsc_ref[...] = jnp.zeros_like(dsc_ref)
                if gated:
                    refs[5][...] = jnp.zeros_like(refs[5])

            @pl.when(i == 0)
            def _():
                dgain_ref[...] = jnp.zeros_like(dgain_ref)

            xv = x_ref[...]
            rstd = lax.rsqrt(jnp.mean(xv * xv, axis=-1, keepdims=True) + EPS)
            xh = xv * rstd
            gn = g_ref[...]
            dh = acc_ref[...]
            dsh_ref[...] += jnp.sum(dh, axis=0, keepdims=True)
            dsc_ref[...] += jnp.sum(dh * (xh * gn), axis=0, keepdims=True)
            dn = dh * (1.0 + sc_ref[...])
            dgain_ref[...] += jnp.sum(dn * xh, axis=0, keepdims=True)
            dxh = dn * gn
            proj = jnp.mean(dxh * xh, axis=-1, keepdims=True)
            dx = rstd * (dxh - xh * proj) + dres_ref[...]
            dx_ref[...] = dx
            if gated:
                refs[4][...] = (dx * gate_ref[...]).astype(BF16)
                refs[5][...] += jnp.sum(dx * y_ref[...], axis=0, keepdims=True)

    tok = pl.BlockSpec((None, tm, D), lambda i, k: (i // per_ex_tiles, i % per_ex_tiles, 0))
    per_ex = pl.BlockSpec((None, 1, D), lambda i, k: (i // per_ex_tiles, 0, 0))
    row = pl.BlockSpec((1, D), lambda i, k: (0, 0))
    in_specs = [pl.BlockSpec((tm, tk), lambda i, k: (i, k)), pl.BlockSpec((D, tk), lambda i, k: (0, k)),
                tok, row, per_ex, tok]
    out_specs = [tok, per_ex, per_ex, row]
    out_shape = [jax.ShapeDtypeStruct((NB, S, D), F32), jax.ShapeDtypeStruct((NB, 1, D), F32),
                 jax.ShapeDtypeStruct((NB, 1, D), F32), jax.ShapeDtypeStruct((1, D), F32)]
    operands = [a, w, x, gain, sc, dres]
    if gated:
        in_specs += [tok, per_ex]
        out_specs += [tok, per_ex]
        out_shape += [jax.ShapeDtypeStruct((NB, S, D), BF16), jax.ShapeDtypeStruct((NB, 1, D), F32)]
        operands += [y, g]
    return pl.pallas_call(
        body, name=name, grid=(T // tm, nk), in_specs=in_specs, out_specs=out_specs, out_shape=out_shape,
        scratch_shapes=[pltpu.VMEM((tm, D), F32)],
        compiler_params=_params(("arbitrary", "arbitrary")),
    )(*operands)


def _gate_res(x, y, g, norm=None):
    NB, S, D = x.shape
    tr = _row_tile(S)

    def body(x_ref, y_ref, g_ref, *refs):
        xo = x_ref[...] + g_ref[...] * y_ref[...]
        refs[-1 if norm is None else -2][...] = xo
        if norm is not None:
            gain_ref, sc_ref, sh_ref, _, h_ref = refs
            n = xo * lax.rsqrt(jnp.mean(xo * xo, axis=-1, keepdims=True) + EPS) * gain_ref[...]
            h_ref[...] = (n * (1.0 + sc_ref[...]) + sh_ref[...]).astype(BF16)

    tok = pl.BlockSpec((None, tr, D), lambda b, r: (b, r, 0))
    per_ex = pl.BlockSpec((None, 1, D), lambda b, r: (b, 0, 0))
    in_specs, out_specs, operands = [tok, tok, per_ex], [tok], [x, y, g]
    out_shape = [jax.ShapeDtypeStruct((NB, S, D), F32)]
    if norm is not None:
        in_specs += [pl.BlockSpec((1, D), lambda b, r: (0, 0)), per_ex, per_ex]
        out_specs.append(tok)
        out_shape.append(jax.ShapeDtypeStruct((NB, S, D), BF16))
        operands += list(norm)
    out = pl.pallas_call(
        body, name="gate_res", grid=(NB, S // tr), in_specs=in_specs, out_specs=out_specs, out_shape=out_shape,
        compiler_params=_params(("parallel", "parallel")),
    )(*operands)
    return out[0] if norm is None else out


def _gate_res_bwd(dxo, y, g):
    NB, S, D = dxo.shape
    tr = _row_tile(S)

    def body(d_ref, y_ref, g_ref, dy_ref, dg_ref):
        @pl.when(pl.program_id(1) == 0)
        def _():
            dg_ref[...] = jnp.zeros_like(dg_ref)

        d = d_ref[...]
        dy_ref[...] = (d * g_ref[...]).astype(BF16)
        dg_ref[...] += jnp.sum(d * y_ref[...], axis=0, keepdims=True)

    tok = pl.BlockSpec((None, tr, D), lambda b, r: (b, r, 0))
    per_ex = pl.BlockSpec((None, 1, D), lambda b, r: (b, 0, 0))
    return pl.pallas_call(
        body, name="gate_res_bwd", grid=(NB, S // tr), in_specs=[tok, tok, per_ex], out_specs=[tok, per_ex],
        out_shape=[jax.ShapeDtypeStruct((NB, S, D), BF16), jax.ShapeDtypeStruct((NB, 1, D), F32)],
        compiler_params=_params(("arbitrary", "arbitrary")),
    )(dxo, y, g)


def _sigmoid(v):
    return 1.0 / (1.0 + jnp.exp(-v))


def _ff_tile(F):
    return _pick(F, 1536, 128)


def _interleave(gate, up):
    F = gate.shape[-1]
    tf = _ff_tile(F)
    parts = []
    for j in range(F // tf):
        parts += [gate[..., j * tf:(j + 1) * tf], up[..., j * tf:(j + 1) * tf]]
    return jnp.concatenate(parts, axis=-1)


def _swiglu_bwd(dm, wd, gu):
    T, D = dm.shape
    F = wd.shape[0]
    tf = _ff_tile(F)
    tm = _pick(T, 1024, 128)
    assert D <= MATMUL_SINGLE_K

    def body(a_ref, b_ref, gu_ref, o_ref):
        d = lax.dot_general(a_ref[...], b_ref[...], _NT, preferred_element_type=F32)
        g, u = gu_ref[:, :tf].astype(F32), gu_ref[:, tf:].astype(F32)
        s = _sigmoid(g)
        o_ref[:, :tf] = (d * u * (s * (1.0 + g * (1.0 - s)))).astype(BF16)
        o_ref[:, tf:] = (d * (g * s)).astype(BF16)

    return pl.pallas_call(
        body, name="swiglu_bwd", grid=(T // tm, F // tf),
        in_specs=[pl.BlockSpec((tm, D), lambda i, j: (i, 0)), pl.BlockSpec((tf, D), lambda i, j: (j, 0)),
                  pl.BlockSpec((tm, 2 * tf), lambda i, j: (i, j))],
        out_specs=pl.BlockSpec((tm, 2 * tf), lambda i, j: (i, j)),
        out_shape=jax.ShapeDtypeStruct((T, 2 * F), BF16),
        compiler_params=_params(("parallel", "parallel")),
    )(dm, wd, gu)


def _loss_fwd_bwd(y, target):
    NB, S, D = y.shape
    tr = _row_tile(S)

    def body(y_ref, t_ref, l_ref, d_ref):
        @pl.when((pl.program_id(0) == 0) & (pl.program_id(1) == 0))
        def _():
            l_ref[...] = jnp.zeros_like(l_ref)

        e = y_ref[...] - t_ref[...]
        d_ref[...] = e / D
        l_ref[...] += 0.5 * jnp.sum(jnp.mean(e * e, axis=-1, keepdims=True), axis=0, keepdims=True)

    tok = pl.BlockSpec((None, tr, D), lambda b, r: (b, r, 0))
    return pl.pallas_call(
        body, name="loss", grid=(NB, S // tr), in_specs=[tok, tok],
        out_specs=[pl.BlockSpec((1, 1), lambda b, r: (0, 0)), tok],
        out_shape=[jax.ShapeDtypeStruct((1, 1), F32), jax.ShapeDtypeStruct((NB, S, D), F32)],
        compiler_params=_params(("arbitrary", "arbitrary")),
    )(y, target)


def _half_sums(v, lo):
    sa = jnp.sum(jnp.where(lo, v, 0.0), axis=-1, keepdims=True)
    sb = jnp.sum(jnp.where(lo, 0.0, v), axis=-1, keepdims=True)
    return jnp.where(lo, sa, sb)


def _rope_swap(v, lane64):
    up = pltpu.roll(v, LANES - ROT_DIM // 2, 1)
    down = pltpu.roll(v, ROT_DIM // 2, 1)
    return jnp.where(lane64 < ROT_DIM // 2, up, jnp.where(lane64 < ROT_DIM, down, 0.0))


def _qk_prep_fwd(qkv, tab_c, tab_s, gains):
    T, W = qkv.shape
    R = W // LANES
    tt = _pick(T, 256, 8)

    def body(x_ref, c_ref, s_ref, g_ref, o_ref):
        lane = lax.broadcasted_iota(jnp.int32, (tt, LANES), 1)
        lo = lane < HEAD_DIM
        lane64 = lane & (HEAD_DIM - 1)
        c, s = c_ref[...], s_ref[...]
        for j in range(R - 1):
            cols = slice(j * LANES, (j + 1) * LANES)
            xv = x_ref[:, cols]
            rstd = lax.rsqrt(_half_sums(xv * xv, lo) / HEAD_DIM + EPS)
            yn = xv * rstd * g_ref[j:j + 1, :]
            o_ref[:, cols] = (yn * c + _rope_swap(yn, lane64) * s).astype(BF16)
        o_ref[:, (R - 1) * LANES:] = x_ref[:, (R - 1) * LANES:].astype(BF16)

    tok = pl.BlockSpec((tt, W), lambda t: (t, 0))
    tab = pl.BlockSpec((tt, LANES), lambda t: (t, 0))
    return pl.pallas_call(
        body, name="qk_prep_fwd", grid=(T // tt,),
        in_specs=[tok, tab, tab, pl.BlockSpec((R, LANES), lambda t: (0, 0))],
        out_specs=tok, out_shape=jax.ShapeDtypeStruct((T, W), BF16),
        compiler_params=_params(("parallel",)),
    )(qkv, tab_c, tab_s, gains)


def _qk_prep_bwd(qkv, dq, dk, dv, tab_c, tab_s, gains):
    T, W = qkv.shape
    R = W // LANES
    QW = dq.shape[1]
    tt = _pick(T, 256, 8)

    def body(x_ref, dq_ref, dk_ref, dv_ref, c_ref, s_ref, g_ref, o_ref, dg_ref):
        @pl.when(pl.program_id(0) == 0)
        def _():
            dg_ref[...] = jnp.zeros_like(dg_ref)

        lane = lax.broadcasted_iota(jnp.int32, (tt, LANES), 1)
        lo = lane < HEAD_DIM
        lane64 = lane & (HEAD_DIM - 1)
        c, s = c_ref[...], s_ref[...]
        for j in range(R - 1):
            cols = slice(j * LANES, (j + 1) * LANES)
            xv = x_ref[:, cols]
            d = dq_ref[:, cols] if j < R - 2 else dk_ref[...]
            rstd = lax.rsqrt(_half_sums(xv * xv, lo) / HEAD_DIM + EPS)
            xh = xv * rstd
            dyn = d * c + _rope_swap(d * s, lane64)
            dg_ref[j:j + 1, :] += jnp.sum(dyn * xh, axis=0, keepdims=True)
            dxh = dyn * g_ref[j:j + 1, :]
            proj = _half_sums(dxh * xh, lo) / HEAD_DIM
            o_ref[:, cols] = (rstd * (dxh - xh * proj)).astype(BF16)
        o_ref[:, (R - 1) * LANES:] = dv_ref[...].astype(BF16)

    tok = pl.BlockSpec((tt, W), lambda t: (t, 0))
    tab = pl.BlockSpec((tt, LANES), lambda t: (t, 0))
    gsp = pl.BlockSpec((R, LANES), lambda t: (0, 0))
    return pl.pallas_call(
        body, name="qk_prep_bwd", grid=(T // tt,),
        in_specs=[tok, pl.BlockSpec((tt, QW), lambda t: (t, 0)), tab, tab, tab, tab, gsp], out_specs=[tok, gsp],
        out_shape=[jax.ShapeDtypeStruct((T, W), BF16), jax.ShapeDtypeStruct((R, LANES), F32)],
        compiler_params=_params(("arbitrary",)),
    )(qkv, dq, dk, dv, tab_c, tab_s, gains)


def _band_mask(i):
    r = lax.broadcasted_iota(jnp.int32, (2 * BLOCK, 2 * BLOCK), 0) & (BLOCK - 1)
    c = lax.broadcasted_iota(jnp.int32, (2 * BLOCK, 2 * BLOCK), 1)
    rel = r + BLOCK - c
    return (rel >= 0) & (rel < BLOCK) & ((c >= BLOCK) | (i > 0))


def _swa_softmax(s, valid, sink):
    s = jnp.where(valid, s * ATTN_SCALE, NEG_BIG)
    m = jnp.maximum(jnp.max(s, axis=1, keepdims=True), sink)
    p = jnp.exp(s - m)
    ps = jnp.exp(sink - m)
    denom = jnp.sum(p, axis=1, keepdims=True) + ps
    return p / denom, ps / denom


A_GROUP = 4


Q_WIDTH_A = N_Q_A * HEAD_DIM
N_PAIR_A = Q_WIDTH_A // LANES


def _swa_specs():
    qs = pl.BlockSpec((None, BLOCK, Q_WIDTH_A), lambda b, i: (b, i, 0))

    def kv(col, back):
        return pl.BlockSpec((None, BLOCK, LANES), lambda b, i: (b, jnp.maximum(i - back, 0), col))

    return qs, kv(N_PAIR_A, 1), kv(N_PAIR_A, 0), kv(N_PAIR_A + 1, 1), kv(N_PAIR_A + 1, 0)


def _dup_heads(t):
    lo = lax.broadcasted_iota(jnp.int32, t.shape, 1) < HEAD_DIM
    sw = pltpu.roll(t.astype(F32), HEAD_DIM, 1).astype(BF16)
    return jnp.where(lo, t, sw), jnp.where(lo, sw, t)


def _kv_tiles(kp_ref, kc_ref, vp_ref, vc_ref):
    kd = _dup_heads(jnp.concatenate([kp_ref[...], kc_ref[...]], axis=0))
    vd = _dup_heads(jnp.concatenate([vp_ref[...], vc_ref[...]], axis=0))
    return kd, vd


def _attn_a_fwd(qkn, sinks):
    NB, S, _ = qkn.shape
    qs, kp, kc, vp, vc = _swa_specs()

    def body(q_ref, kp_ref, kc_ref, vp_ref, vc_ref, sink_ref, o_ref):
        i = pl.program_id(1)
        kd, vd = _kv_tiles(kp_ref, kc_ref, vp_ref, vc_ref)
        valid = _band_mask(i)
        lo = lax.broadcasted_iota(jnp.int32, (BLOCK, LANES), 1) < HEAD_DIM
        top = lax.broadcasted_iota(jnp.int32, (2 * BLOCK, 1), 0) < BLOCK
        for first in range(0, N_PAIR_A, A_GROUP):
            pairs = range(first, first + A_GROUP)
            qs_ = [jnp.concatenate(_head_halves(q_ref[:, p * LANES:(p + 1) * LANES], lo), axis=0) for p in pairs]
            ss = [lax.dot_general(q, kd[2 * p // GROUP_A], _NT, preferred_element_type=F32) for q, p in zip(qs_, pairs)]
            pns = [_swa_softmax(s, valid, jnp.where(top, sink_ref[2 * p], sink_ref[2 * p + 1]))[0]
                   for s, p in zip(ss, pairs)]
            pvs = [jnp.dot(pn.astype(BF16), vd[2 * p // GROUP_A], preferred_element_type=F32) for pn, p in zip(pns, pairs)]
            for pv, p in zip(pvs, pairs):
                o_ref[:, p * LANES:(p + 1) * LANES] = jnp.where(lo, pv[:BLOCK], pv[BLOCK:]).astype(BF16)

    return pl.pallas_call(
        body, name="attn_a_fwd", grid=(NB, S // BLOCK),
        in_specs=[qs, kp, kc, vp, vc, pl.BlockSpec(memory_space=pltpu.SMEM)],
        out_specs=qs, out_shape=jax.ShapeDtypeStruct((NB, S, Q_WIDTH_A), BF16),
        compiler_params=_params(("parallel", "arbitrary")),
    )(qkn, qkn, qkn, qkn, qkn, sinks)


def _attn_a_bwd(qkn, do, sinks):
    NB, S, _ = qkn.shape
    qs, kp, kc, vp, vc = _swa_specs()
    full = pl.BlockSpec((None, S, LANES), lambda b, i: (b, 0, 0))
    sink_out = pl.BlockSpec((None, N_Q_A, LANES), lambda b, i: (b, 0, 0))

    def body(q_ref, do_ref, kp_ref, kc_ref, vp_ref, vc_ref, sink_ref, dq_ref, dk_ref, dv_ref, ds_ref, dk_s, dv_s):
        i = pl.program_id(1)

        @pl.when(i == 0)
        def _():
            dk_ref[...] = jnp.zeros_like(dk_ref)
            dv_ref[...] = jnp.zeros_like(dv_ref)
            ds_ref[...] = jnp.zeros_like(ds_ref)

        dk_s[...] = jnp.zeros_like(dk_s)
        dv_s[...] = jnp.zeros_like(dv_s)
        kd, vd = _kv_tiles(kp_ref, kc_ref, vp_ref, vc_ref)
        valid = _band_mask(i)
        lo = lax.broadcasted_iota(jnp.int32, (BLOCK, LANES), 1) < HEAD_DIM
        top = lax.broadcasted_iota(jnp.int32, (2 * BLOCK, 1), 0) < BLOCK
        for first in range(0, N_PAIR_A, A_GROUP):
            pairs = range(first, first + A_GROUP)
            kvs = [2 * p // GROUP_A for p in pairs]
            qs_ = [jnp.concatenate(_head_halves(q_ref[:, p * LANES:(p + 1) * LANES], lo), axis=0) for p in pairs]
            dos = [jnp.concatenate(_head_halves(do_ref[:, p * LANES:(p + 1) * LANES], lo), axis=0) for p in pairs]
            ss = [lax.dot_general(q, kd[kv], _NT, preferred_element_type=F32) for q, kv in zip(qs_, kvs)]
            dps = [lax.dot_general(d, vd[kv], _NT, preferred_element_type=F32) for d, kv in zip(dos, kvs)]
            sm = [_swa_softmax(s, valid, jnp.where(top, sink_ref[2 * p], sink_ref[2 * p + 1])) for s, p in zip(ss, pairs)]
            deltas = [jnp.sum(pn * dp, axis=1, keepdims=True) for (pn, _), dp in zip(sm, dps)]
            dsbs = [(pn * (dp - delta) * ATTN_SCALE).astype(BF16) for (pn, _), dp, delta in zip(sm, dps, deltas)]
            for n, p in enumerate(pairs):
                dq2 = jnp.dot(dsbs[n], kd[kvs[n]], preferred_element_type=F32)
                dq_ref[:, p * LANES:(p + 1) * LANES] = jnp.where(lo, dq2[:BLOCK], dq2[BLOCK:])
                dk_s[kvs[n]] += lax.dot_general(dsbs[n], qs_[n], _TN, preferred_element_type=F32)
                dv_s[kvs[n]] += lax.dot_general(sm[n][0].astype(BF16), dos[n], _TN, preferred_element_type=F32)
                t = sm[n][1] * deltas[n]
                for hh in range(2):
                    dsink = -jnp.sum(t[hh * BLOCK:(hh + 1) * BLOCK], axis=0, keepdims=True)
                    ds_ref[2 * p + hh:2 * p + hh + 1, :] += jnp.broadcast_to(dsink, (1, LANES))

        lo2 = lax.broadcasted_iota(jnp.int32, (2 * BLOCK, LANES), 1) < HEAD_DIM

        def fold(acc):
            halves = [acc[kv] + pltpu.roll(acc[kv], HEAD_DIM, 1) for kv in range(N_KV_A)]
            return jnp.where(lo2, halves[0], halves[1])

        dk2, dv2 = fold(dk_s), fold(dv_s)

        @pl.when(i > 0)
        def _():
            start = pl.multiple_of((i - 1) * BLOCK, BLOCK)
            dk_ref[pl.ds(start, 2 * BLOCK), :] += dk2
            dv_ref[pl.ds(start, 2 * BLOCK), :] += dv2

        @pl.when(i == 0)
        def _():
            dk_ref[0:BLOCK, :] += dk2[BLOCK:, :]
            dv_ref[0:BLOCK, :] += dv2[BLOCK:, :]

    slots = pltpu.VMEM((N_KV_A, 2 * BLOCK, LANES), F32)
    return pl.pallas_call(
        body, name="attn_a_bwd", grid=(NB, S // BLOCK),
        in_specs=[qs, qs, kp, kc, vp, vc, pl.BlockSpec(memory_space=pltpu.SMEM)],
        out_specs=[qs, full, full, sink_out],
        out_shape=[jax.ShapeDtypeStruct((NB, S, Q_WIDTH_A), F32), jax.ShapeDtypeStruct((NB, S, LANES), F32),
                   jax.ShapeDtypeStruct((NB, S, LANES), F32), jax.ShapeDtypeStruct((NB, N_Q_A, LANES), F32)],
        scratch_shapes=[slots, slots],
        compiler_params=_params(("parallel", "arbitrary")),
    )(qkn, do, qkn, qkn, qkn, qkn, sinks)


def _cumsum_mats():
    src = lax.broadcasted_iota(jnp.int32, (2 * BLOCK, 2 * BLOCK), 0) % BLOCK
    dst = lax.broadcasted_iota(jnp.int32, (2 * BLOCK, 2 * BLOCK), 1)
    ones = dst >= BLOCK
    rev = ((src > dst) | ones).astype(BF16)
    fwd = ((src < dst) | ones).astype(BF16)
    return rev, fwd


def _log_sigmoids(z):
    sp = jnp.log(1.0 + jnp.exp(-jnp.abs(z)))
    return jnp.minimum(z, 0.0) - sp, -(jnp.maximum(z, 0.0) + sp)


def _cumsum_mxu_many(vs, mat):
    parts = []
    for v in vs:
        hi = v.astype(BF16)
        parts.append(jnp.concatenate([hi, (v - hi.astype(F32)).astype(BF16)], axis=1))
    r = jnp.dot(jnp.concatenate(parts, axis=0), mat, preferred_element_type=F32)
    return [(r[n * BLOCK:(n + 1) * BLOCK, :BLOCK], r[n * BLOCK:(n + 1) * BLOCK, BLOCK:]) for n in range(len(vs))]


def _strict_mask():
    r = lax.broadcasted_iota(jnp.int32, (BLOCK, BLOCK), 0)
    c = lax.broadcasted_iota(jnp.int32, (BLOCK, BLOCK), 1)
    return c < r


def _tile(ref, j):
    return ref[pl.ds(pl.multiple_of(j * BLOCK, BLOCK), BLOCK), :]


SWEEP_EXIT = -88.0


def _head_halves(t, lo):
    zero = jnp.zeros_like(t)
    return jnp.where(lo, t, zero), jnp.where(lo, zero, t)


def _sb_specs(S, HD, width):
    n = HD // width
    blk = pl.BlockSpec((None, BLOCK, width), lambda b, p, i: (b, i, p))
    k_full = pl.BlockSpec((None, S, width), lambda b, p, i: (b, 0, n + p))
    v_full = pl.BlockSpec((None, S, width), lambda b, p, i: (b, 0, 2 * n + p))
    mat = pl.BlockSpec((2 * BLOCK, 2 * BLOCK), lambda b, p, i: (0, 0))
    return blk, k_full, v_full, mat


SB_FWD_PAIRS = 4
SB_BWD_PAIRS = 2
SB_BWD_TILES = 2
SB_BWD_VMEM_LIMIT_BYTES = 58 * 1024 * 1024


def _attn_b_fwd(qkv, rev):
    NB, S, W = qkv.shape
    HD = W // 3
    width = SB_FWD_PAIRS * LANES
    n_heads = 2 * SB_FWD_PAIRS
    blk, k_full, v_full, mat = _sb_specs(S, HD, width)

    def body(q_ref, k_ref, v_ref, rev_ref, o_ref):
        i = pl.program_id(2)
        rv = rev_ref[...]
        mask = _strict_mask()
        lo = lax.broadcasted_iota(jnp.int32, (BLOCK, LANES), 1) < HEAD_DIM
        q_all = q_ref[...]
        q_stack = [jnp.concatenate(_head_halves(q_all[:, p * LANES:(p + 1) * LANES] * ATTN_SCALE, lo), axis=0)
                   for p in range(SB_FWD_PAIRS)]

        def pair_tiles(ref, j):
            t = _tile(ref, j)
            return [t[:, p * LANES:(p + 1) * LANES] for p in range(SB_FWD_PAIRS)]

        def tiles_pass(js, carries, diagonal_first, last_counts=None):
            zs, v_tiles = [], []
            for j in js:
                ks = pair_tiles(k_ref, j)
                v_tiles.append(pair_tiles(v_ref, j))
                for p in range(SB_FWD_PAIRS):
                    z2 = lax.dot_general(q_stack[p], ks[p], _NT, preferred_element_type=F32)
                    zs += [z2[:BLOCK], z2[BLOCK:]]
            logs = [_log_sigmoids(z) for z in zs]
            masked = [jnp.where(mask, lm, 0.0) if diagonal_first and n < n_heads else lm
                      for n, (_, lm) in enumerate(logs)]
            cums = _cumsum_mxu_many(masked, rv)
            probs, new_c = {}, []
            for h in range(n_heads):
                carry = None if diagonal_first else carries[h]
                for t in range(len(js)):
                    n = t * n_heads + h
                    after, rs = cums[n]
                    if diagonal_first and t == 0:
                        a = jnp.where(mask, jnp.exp(logs[n][0] + after), 0.0)
                        carry = rs
                    else:
                        a = jnp.exp(logs[n][0] + after + carry)
                        carry = carry + rs
                    if last_counts is not None and t == len(js) - 1:
                        a = jnp.where(last_counts, a, 0.0)
                    probs[t, h] = a.astype(BF16)
                new_c.append(carry)
            outs = []
            for p in range(SB_FWD_PAIRS):
                total = None
                for t in range(len(js)):
                    pv = jnp.dot(jnp.concatenate([probs[t, 2 * p], probs[t, 2 * p + 1]], axis=0), v_tiles[t][p],
                                 preferred_element_type=F32)
                    part = jnp.where(lo, pv[:BLOCK], pv[BLOCK:])
                    total = part if total is None else total + part
                outs.append(total)
            return new_c, outs

        carries, accs = tiles_pass([i, jnp.maximum(i - 1, 0)], None, True, last_counts=i > 0)

        def live(cs):
            top = cs[0]
            for c in cs[1:]:
                top = jnp.maximum(top, c)
            return jnp.max(top) > SWEEP_EXIT

        def cond(st):
            return (st[0] < i - 1) & st[1]

        def step(st):
            jj, _, cs, accs = st
            new_c, outs = tiles_pass([i - 2 - jj], cs, False)
            return jj + 1, live(new_c), new_c, [acc + o for acc, o in zip(accs, outs)]

        st = lax.while_loop(cond, step, (jnp.int32(0), live(carries), carries, accs))
        for p in range(SB_FWD_PAIRS):
            o_ref[:, p * LANES:(p + 1) * LANES] = st[3][p].astype(BF16)

    return pl.pallas_call(
        body, name="attn_b_fwd", grid=(NB, HD // width, S // BLOCK),
        in_specs=[blk, k_full, v_full, mat], out_specs=blk,
        out_shape=jax.ShapeDtypeStruct((NB, S, HD), BF16),
        compiler_params=_params(("parallel", "parallel", "arbitrary")),
    )(qkv, qkv, qkv, rev)


def _attn_b_bwd(qkv, do, rev, fwd):
    NB, S, W = qkv.shape
    HD = W // 3
    width = SB_BWD_PAIRS * LANES
    n_heads = 2 * SB_BWD_PAIRS
    nj = S // BLOCK
    blk, k_full, v_full, mat = _sb_specs(S, HD, width)
    acc_full = pl.BlockSpec((None, S, width), lambda b, p, i: (b, 0, p))

    def body(q_ref, do_ref, k_ref, v_ref, rev_ref, fwd_ref, dq_ref, dk_ref, dv_ref, sig_s, a_s, e_s):
        i = pl.program_id(2)

        @pl.when(i == 0)
        def _():
            dk_ref[...] = jnp.zeros_like(dk_ref)
            dv_ref[...] = jnp.zeros_like(dv_ref)

        rv, fw = rev_ref[...], fwd_ref[...]
        mask = _strict_mask()
        lo = lax.broadcasted_iota(jnp.int32, (BLOCK, LANES), 1) < HEAD_DIM
        pairs = range(SB_BWD_PAIRS)

        def cols(p):
            return slice(p * LANES, (p + 1) * LANES)

        q_stack = [jnp.concatenate(_head_halves(q_ref[:, cols(p)], lo), axis=0) for p in pairs]
        qs_stack = [q * ATTN_SCALE for q in q_stack]
        do_stack = [jnp.concatenate(_head_halves(do_ref[:, cols(p)], lo), axis=0) for p in pairs]

        def sweep1_tiles(js, carries, diagonal_first):
            zs, das = [], []
            for j in js:
                kj, vj = _tile(k_ref, j), _tile(v_ref, j)
                for p in pairs:
                    z2 = lax.dot_general(qs_stack[p], kj[:, cols(p)], _NT, preferred_element_type=F32)
                    da2 = lax.dot_general(do_stack[p], vj[:, cols(p)], _NT, preferred_element_type=F32)
                    zs += [z2[:BLOCK], z2[BLOCK:]]
                    das += [da2[:BLOCK], da2[BLOCK:]]
            logs = [_log_sigmoids(z) for z in zs]
            cums = _cumsum_mxu_many([jnp.where(mask, lm, 0.0) if diagonal_first and n < n_heads else lm
                                     for n, (_, lm) in enumerate(logs)], rv)
            new_c, stores = [], []
            for h in range(n_heads):
                carry = None if diagonal_first else carries[h]
                for t, j in enumerate(js):
                    n = t * n_heads + h
                    lb, (after, rs) = logs[n][0], cums[n]
                    if diagonal_first and t == 0:
                        a = jnp.where(mask, jnp.exp(lb + after), 0.0)
                        carry = rs
                    else:
                        a = jnp.exp(lb + after + carry)
                        carry = carry + rs
                    stores.append((t, h, j, jnp.exp(lb), a.astype(BF16), das[n] * a))
                new_c.append(carry)
            for t, h, j, sg, ab, e in sorted(stores, key=lambda s: -s[0]):
                sig_s[h, j] = sg
                a_s[h, j] = ab
                e_s[h, j] = e
            return new_c

        carries = sweep1_tiles([jnp.maximum(i - t, 0) for t in range(SB_BWD_TILES + 1)], None, True)
        done = SB_BWD_TILES

        def live(cs):
            top = cs[0]
            for c in cs[1:]:
                top = jnp.maximum(top, c)
            return jnp.max(top) > SWEEP_EXIT

        def cond(st):
            return (done + SB_BWD_TILES * st[0] < i) & st[1]

        def sweep1(st):
            first = i - 1 - done - SB_BWD_TILES * st[0]
            new_c = sweep1_tiles([jnp.maximum(first - t, 0) for t in range(SB_BWD_TILES)], st[2], False)
            return st[0] + 1, live(new_c), new_c

        trips = lax.while_loop(cond, sweep1, (jnp.int32(0), live(carries), carries))[0]
        lowest = jnp.maximum(i - done - SB_BWD_TILES * trips, 0)

        def grads(js, st, diagonal_last=False, counts=None):
            prefixes, dqs = st
            counts = counts or [None] * len(js)
            es = [e_s[h, j] for j in js for h in range(n_heads)]
            cums = _cumsum_mxu_many(es, fw)
            dzs, new_p = [], []
            for h in range(n_heads):
                prefix = prefixes[h]
                for t, j in enumerate(js):
                    n = t * n_heads + h
                    sg = sig_s[h, j]
                    e_before, rs = cums[n]
                    dz = (es[n] * (1.0 - sg) - (e_before + prefix) * sg) * ATTN_SCALE
                    if diagonal_last and t == len(js) - 1:
                        dz = jnp.where(mask, dz, 0.0)
                    if counts[t] is not None:
                        dz = jnp.where(counts[t], dz, 0.0)
                        rs = jnp.where(counts[t], rs, 0.0)
                    dzs.append((t, h, dz.astype(BF16)))
                    prefix = prefix + rs
                new_p.append(prefix)
            dz_of = {(t, h): dz for t, h, dz in dzs}
            new_dq = list(dqs)
            for t, j in enumerate(js):
                kj = _tile(k_ref, j)
                rows = pl.ds(pl.multiple_of(j * BLOCK, BLOCK), BLOCK)
                for p in pairs:
                    dz_stack = jnp.concatenate([dz_of[t, 2 * p], dz_of[t, 2 * p + 1]], axis=0)
                    a_stack = jnp.concatenate([a_s[2 * p, j], a_s[2 * p + 1, j]], axis=0)
                    if counts[t] is not None:
                        a_stack = jnp.where(counts[t], a_stack, jnp.zeros_like(a_stack))
                    dq2 = jnp.dot(dz_stack, kj[:, cols(p)], preferred_element_type=F32)
                    new_dq[p] = new_dq[p] + jnp.where(lo, dq2[:BLOCK], dq2[BLOCK:])
                    dk_ref[rows, cols(p)] += lax.dot_general(dz_stack, q_stack[p], _TN, preferred_element_type=F32)
                    dv_ref[rows, cols(p)] += lax.dot_general(a_stack, do_stack[p], _TN, preferred_element_type=F32)
            return new_p, new_dq

        zeros = jnp.zeros((BLOCK, BLOCK), F32)
        st = ([zeros] * n_heads, [zeros] * SB_BWD_PAIRS)
        count = jnp.maximum(i - done, 0) - lowest
        st = lax.fori_loop(0, count % SB_BWD_TILES, lambda t, st: grads([lowest + t], st), st)
        start = lowest + count % SB_BWD_TILES
        st = lax.fori_loop(0, count // SB_BWD_TILES,
                           lambda t, st: grads([start + SB_BWD_TILES * t + u for u in range(SB_BWD_TILES)], st), st)
        top = [jnp.maximum(i - t, 0) for t in range(SB_BWD_TILES, -1, -1)]
        dqs = grads(top, st, diagonal_last=True, counts=[i >= t for t in range(SB_BWD_TILES, 0, -1)] + [None])[1]
        for p in pairs:
            dq_ref[:, cols(p)] = dqs[p]

    f32_stash = pltpu.VMEM((n_heads, nj, BLOCK, BLOCK), F32)
    bf16_stash = pltpu.VMEM((n_heads, nj, BLOCK, BLOCK), BF16)
    return pl.pallas_call(
        body, name="attn_b_bwd", grid=(NB, HD // width, nj),
        in_specs=[blk, blk, k_full, v_full, mat, mat], out_specs=[blk, acc_full, acc_full],
        out_shape=[jax.ShapeDtypeStruct((NB, S, HD), F32)] * 3,
        scratch_shapes=[f32_stash, bf16_stash, f32_stash],
        compiler_params=_params(("parallel", "parallel", "arbitrary"), SB_BWD_VMEM_LIMIT_BYTES),
    )(qkv, do, qkv, qkv, rev, fwd)


def _ada_fwd(c_all, w, b):
    L, D, N = w.shape
    B = c_all.shape[0]

    def body(c_ref, w_ref, b_ref, o_ref):
        cv = c_ref[...]
        cond = (cv * _sigmoid(cv)).astype(BF16)
        o_ref[...] = jnp.dot(cond, w_ref[...].astype(BF16), preferred_element_type=F32) + b_ref[...]

    return pl.pallas_call(
        body, name="ada_fwd", grid=(L,),
        in_specs=[pl.BlockSpec((B, D), lambda l: (0, 0)), pl.BlockSpec((None, D, N), lambda l: (l, 0, 0)),
                  pl.BlockSpec((None, 1, N), lambda l: (l, 0, 0))],
        out_specs=pl.BlockSpec((None, B, N), lambda l: (l, 0, 0)),
        out_shape=jax.ShapeDtypeStruct((L, B, N), F32),
        compiler_params=_params(("parallel",)),
    )(c_all, w, b)


def _ada_bwd(c_all, dmod_all, dmod_shard):
    L, B, N = dmod_shard.shape
    D = c_all.shape[1]
    N_all = dmod_all.shape[2]

    def body(c_ref, da_ref, ds_ref, gw_ref, gb_ref):
        cv = c_ref[...]
        cond = (cv * _sigmoid(cv)).astype(BF16)
        gw_ref[...] = lax.dot_general(cond, ds_ref[...].astype(BF16), _TN, preferred_element_type=F32)
        gb_ref[...] = jnp.sum(da_ref[...], axis=0, keepdims=True)

    return pl.pallas_call(
        body, name="ada_bwd", grid=(L,),
        in_specs=[pl.BlockSpec((B, D), lambda l: (0, 0)), pl.BlockSpec((None, B, N_all), lambda l: (l, 0, 0)),
                  pl.BlockSpec((None, B, N), lambda l: (l, 0, 0))],
        out_specs=[pl.BlockSpec((None, D, N), lambda l: (l, 0, 0)), pl.BlockSpec((None, 1, N_all), lambda l: (l, 0, 0))],
        out_shape=[jax.ShapeDtypeStruct((L, D, N), F32), jax.ShapeDtypeStruct((L, 1, N_all), F32)],
        compiler_params=_params(("parallel",)),
    )(c_all, dmod_all, dmod_shard)


def _adamw(w, g, m, v, name):
    shape = w.shape
    if w.ndim == 2:
        w, g, m, v = [t.reshape((1,) + shape) for t in (w, g, m, v)]
    L, R, C = w.shape
    tr = _pick(R, max(8, (1 << 18) // C), 8)
    c1 = 1.0 - ADAM_B1 ** ADAM_STEP
    c2 = 1.0 - ADAM_B2 ** ADAM_STEP

    def body(w_ref, g_ref, m_ref, v_ref, d_ref, nm_ref, nv_ref):
        gv = g_ref[...]
        nm = ADAM_B1 * m_ref[...] + (1.0 - ADAM_B1) * gv
        nv = ADAM_B2 * v_ref[...] + (1.0 - ADAM_B2) * (gv * gv)
        d_ref[...] = -ADAM_LR * ((nm / c1) / (jnp.sqrt(nv / c2) + ADAM_EPS) + ADAM_WD * w_ref[...])
        nm_ref[...] = nm
        nv_ref[...] = nv

    spec = pl.BlockSpec((None, tr, C), lambda l, r: (l, r, 0))
    out = pl.pallas_call(
        body, name=name, grid=(L, R // tr), in_specs=[spec] * 4, out_specs=[spec] * 3,
        out_shape=[jax.ShapeDtypeStruct((L, R, C), F32)] * 3,
        compiler_params=_params(("parallel", "parallel")),
    )(w, g, m, v)
    return [t.reshape(shape) for t in out]


_SHARDED = (("wqkv_a", 2), ("wo_a", 1), ("wqkv_b", 2), ("wo_b", 1), ("w_gate", 2), ("w_up", 2), ("w_down", 1))


def _pack_full(layers, axis, gate_up=None):
    L = len(layers)
    R, C = layers[0].shape

    def shards(m):
        if gate_up is not None:
            F = C // 2
            tf, Cs = _ff_tile(F), F // 4
            assert tf % Cs == 0
            starts = [(2 * (s * Cs // tf) + gate_up) * tf + s * Cs % tf for s in range(4)]
            return jnp.stack([m[:, st:st + Cs] for st in starts])
        if axis == 2:
            return m.reshape(R, 4, C // 4).transpose(1, 0, 2)
        return m.reshape(4, R // 4, C)

    halves = [jnp.stack([shards(m) for m in layers[h * (L // 2):(h + 1) * (L // 2)]], axis=1) for h in range(2)]
    return jnp.stack(halves)


def _unpack_full(gathered, axis):
    _, Lh, Rs, Cs = gathered.shape
    t = gathered.reshape(4, 2, Lh, Rs, Cs)
    layers = []
    for h in range(2):
        for l in range(Lh):
            piece = t[:, h, l]
            if axis == 2:
                layers.append(piece.transpose(1, 0, 2).reshape(Rs, 4 * Cs))
            else:
                layers.append(piece.reshape(4 * Rs, Cs))
    return layers


def _sum_slabs(own, recv, name, with_bf16=False):
    C = own.shape[-1]
    out = _sum_leading(recv.reshape(recv.shape[0], -1, C), name, own=own.reshape(-1, C), with_bf16=with_bf16)
    if with_bf16:
        return out[0].reshape(own.shape), out[1].reshape(own.shape)
    return out.reshape(own.shape)


def _gather8(x, name):
    return _all_gather8([x], name)[0]


def _rope_tables(positions):
    half = ROT_DIM // 2
    inv_freq = jnp.power(jnp.float32(ROPE_THETA), -jnp.arange(half, dtype=F32) * 2.0 / ROT_DIM)
    ang = positions.astype(F32).reshape(-1, 1) * inv_freq
    cos, sin = jnp.cos(ang), jnp.sin(ang)
    T = ang.shape[0]
    rest = HEAD_DIM - ROT_DIM
    c64 = jnp.concatenate([cos, cos, jnp.ones((T, rest), F32)], axis=1)
    s64 = jnp.concatenate([-sin, sin, jnp.zeros((T, rest), F32)], axis=1)
    return jnp.tile(c64, (1, 2)), jnp.tile(s64, (1, 2))


def _gain_rows(q_gain, k_gain):
    q2 = jnp.tile(q_gain.reshape(1, HEAD_DIM), (GROUP_A, 2))
    k2 = jnp.tile(k_gain.reshape(1, HEAD_DIM), (1, 2))
    return jnp.concatenate([q2, k2, jnp.ones((1, LANES), F32)], axis=0)


def _local_step(x, positions, mod, norm1_g, norm2_g, q_norm_a, k_norm_a, sinks_a,
                wqkv_a, wo_a, wqkv_b, wo_b, wgu, wd, loss_target):
    NB, S, D = x.shape
    T = NB * S
    QA = N_Q_A * HEAD_DIM
    tab_c, tab_s = _rope_tables(positions)
    rev, fwd = _cumsum_mats()

    saved = []
    xc = x
    mods = [[mod[i][:, k * D:(k + 1) * D].reshape(NB, 1, D) for k in range(6)] for i in range(DEPTH)]
    h = _norm_mod_fwd(xc, norm1_g[0:1], mods[0][1], mods[0][0])
    for i in range(DEPTH):
        j = i // 2
        sh1, sc1, g1, sh2, sc2, g2 = mods[i]
        st = dict(x=xc, sc1=sc1, g1=g1, sc2=sc2, g2=g2)
        st["h"] = h.reshape(T, D)
        if i % 2 == 0:
            st["qkv"] = _matmul(st["h"], wqkv_a[j], "nn", F32, "qkv_a")
            st["gains"] = _gain_rows(q_norm_a[j], k_norm_a[j])
            st["qkn"] = _qk_prep_fwd(st["qkv"], tab_c, tab_s, st["gains"]).reshape(NB, S, -1)
            st["o"] = _attn_a_fwd(st["qkn"], sinks_a[j]).reshape(T, QA)
            y = _matmul(st["o"], wo_a[j], "nn", F32, "wo_a")
        else:
            st["qkv"] = _matmul(st["h"], wqkv_b[j], "nn", BF16, "qkv_b").reshape(NB, S, -1)
            st["o"] = _attn_b_fwd(st["qkv"], rev).reshape(T, N_H_B * HEAD_DIM)
            y = _matmul(st["o"], wo_b[j], "nn", F32, "wo_b")
        st["y"] = y.reshape(NB, S, D)
        x1, h2 = _gate_res(xc, st["y"], g1, norm=(norm2_g[i:i + 1], sc2, sh2))
        st["x1"] = x1
        st["h2"] = h2.reshape(T, D)
        st["gu"], st["act"] = _matmul(st["h2"], wgu[i], "nn", BF16, "gate_up", swiglu=True)
        st["m"] = _matmul(st["act"], wd[i], "nn", F32, "down").reshape(NB, S, D)
        if i + 1 < DEPTH:
            xc, h = _gate_res(x1, st["m"], g2, norm=(norm1_g[i + 1:i + 2], mods[i + 1][1], mods[i + 1][0]))
        else:
            xc = _gate_res(x1, st["m"], g2)
        saved.append(st)

    loss, dx = _loss_fwd_bwd(xc, loss_target)

    grads = {name: [None] * n for name, n in
             (("wqkv_a", 2), ("wo_a", 2), ("wqkv_b", 2), ("wo_b", 2), ("wgu", DEPTH), ("wd", DEPTH),
              ("norm1_g", DEPTH), ("norm2_g", DEPTH), ("q_norm_a", 2), ("k_norm_a", 2), ("sinks_a", 2))}
    dmod = [None] * DEPTH
    dm, dg2 = _gate_res_bwd(dx, saved[-1]["m"], saved[-1]["g2"])
    for i in reversed(range(DEPTH)):
        j = i // 2
        st = saved[i]
        dm = dm.reshape(T, D)
        grads["wd"][i] = _matmul(st["act"], dm, "tn", F32, "d_wd")
        dgu = _swiglu_bwd(dm, wd[i], st["gu"])
        grads["wgu"][i] = _matmul(st["h2"], dgu, "tn", F32, "d_wgu")
        dx1, dsh2, dsc2, grads["norm2_g"][i], dy, dg1 = _norm_mod_bwd(
            dgu, wgu[i], st["x1"], norm2_g[i:i + 1], st["sc2"], dx, "d_h2", y=st["y"], g=st["g1"])
        dy = dy.reshape(T, D)
        if i % 2 == 0:
            do = _matmul(dy, wo_a[j], "nt", BF16, "d_o_a").reshape(NB, S, QA)
            grads["wo_a"][j] = _matmul(st["o"], dy, "tn", F32, "d_wo_a")
            dq, dk, dv, dsink = _attn_a_bwd(st["qkn"], do, sinks_a[j])
            dqkv, dgain = _qk_prep_bwd(st["qkv"], dq.reshape(T, QA), dk.reshape(T, LANES), dv.reshape(T, LANES),
                                       tab_c, tab_s, st["gains"])
            w_in = wqkv_a[j]
            grads["wqkv_a"][j] = _matmul(st["h"], dqkv, "tn", F32, "d_wqkv_a")
            grads["q_norm_a"][j] = jnp.sum(dgain[:GROUP_A].reshape(2 * GROUP_A, HEAD_DIM), axis=0)
            grads["k_norm_a"][j] = jnp.sum(dgain[GROUP_A].reshape(2, HEAD_DIM), axis=0)
            grads["sinks_a"][j] = jnp.sum(dsink[..., 0], axis=0)
        else:
            do = _matmul(dy, wo_b[j], "nt", BF16, "d_o_b").reshape(NB, S, -1)
            grads["wo_b"][j] = _matmul(st["o"], dy, "tn", F32, "d_wo_b")
            dq, dk, dv = _attn_b_bwd(st["qkv"], do, rev, fwd)
            dqkv = jnp.concatenate([dq, dk, dv], axis=-1).reshape(T, -1).astype(BF16)
            w_in = wqkv_b[j]
            grads["wqkv_b"][j] = _matmul(st["h"], dqkv, "tn", F32, "d_wqkv_b")
        this_dg2 = dg2
        if i > 0:
            dx, dsh1, dsc1, grads["norm1_g"][i], dm, dg2 = _norm_mod_bwd(
                dqkv, w_in, st["x"], norm1_g[i:i + 1], st["sc1"], dx1, "d_h", y=saved[i - 1]["m"], g=saved[i - 1]["g2"])
        else:
            dx, dsh1, dsc1, grads["norm1_g"][i] = _norm_mod_bwd(
                dqkv, w_in, st["x"], norm1_g[i:i + 1], st["sc1"], dx1, "d_h")
        dmod[i] = jnp.concatenate([dsh1, dsc1, dg1, dsh2, dsc2, this_dg2], axis=-1).reshape(NB, 6 * D)

    matrices = ("wqkv_a", "wo_a", "wqkv_b", "wo_b", "wgu", "wd")
    grads = {name: parts if name in matrices else jnp.stack(parts) for name, parts in grads.items()}
    return loss, dx, grads, jnp.stack(dmod)


def _rows_of(flat, cols=PACK_COLS):
    n = flat.shape[0]
    pad = (-n) % (8 * cols)
    if pad:
        flat = jnp.concatenate([flat, jnp.zeros((pad,), flat.dtype)])
    return flat.reshape(-1, cols)


def kernel(x, c, positions, ada_w, ada_b, norm1_g, norm2_g, wqkv_a, q_norm_a, k_norm_a, sinks_a, wo_a, wqkv_b, wo_b, w_gate, w_up, w_down, loss_target, m_ada_w, m_ada_b, m_norm1_g, m_norm2_g, m_wqkv_a, m_q_norm_a, m_k_norm_a, m_sinks_a, m_wo_a, m_wqkv_b, m_wo_b, m_w_gate, m_w_up, m_w_down, v_ada_w, v_ada_b, v_norm1_g, v_norm2_g, v_wqkv_a, v_q_norm_a, v_k_norm_a, v_sinks_a, v_wo_a, v_wqkv_b, v_wo_b, v_w_gate, v_w_up, v_w_down):
    xi, yi, ci = lax.axis_index("x"), lax.axis_index("y"), lax.axis_index("c")
    dev = 4 * xi + 2 * yi + ci
    chip = 2 * xi + yi
    NB, S, D = x.shape
    B_all = N_DEV * NB
    L = ada_w.shape[0]
    n_mod = ada_w.shape[2] // 2

    c_all = _gather8(_rows_of(c.reshape(-1), LANES), "gather_c").reshape(N_DEV, -1)[:, :NB * D].reshape(B_all, D)
    ada_w_half = lax.dynamic_slice_in_dim(ada_w, ci * n_mod, n_mod, axis=2)
    ada_b_half = lax.dynamic_slice_in_dim(ada_b, dev * n_mod, n_mod, axis=1).reshape(L, 1, n_mod)
    mod_part = _ada_fwd(c_all, ada_w_half, ada_b_half)
    n_part = L * B_all * n_mod
    mod_all = _gather8(_rows_of(mod_part.reshape(-1)), "gather_mod").reshape(N_DEV, -1)[:, :n_part]
    mod_all = mod_all.reshape(N_DEV, L, B_all, n_mod).transpose(1, 2, 0, 3).reshape(L, B_all, N_DEV * n_mod)
    mod = lax.dynamic_slice_in_dim(mod_all, dev * NB, NB, axis=1)

    shards = dict(wqkv_a=wqkv_a, wo_a=wo_a, wqkv_b=wqkv_b, wo_b=wo_b, w_gate=w_gate, w_up=w_up, w_down=w_down)
    halves = []
    for name, _ in _SHARDED:
        w = shards[name]
        half = lax.dynamic_index_in_dim(w.reshape((2, w.shape[0] // 2) + w.shape[1:]), ci, 0, keepdims=False)
        halves.append(half.astype(BF16))
    gathered = _all_gather8(halves, "gather_weights", local_axis=1, local_chunks=8, relay_axis=1)
    full = {name: _unpack_full(t, axis) for (name, axis), t in zip(_SHARDED, gathered)}
    wgu = [_interleave(gate, up) for gate, up in zip(full["w_gate"], full["w_up"])]

    loss, grad_x, g, dmod = _local_step(
        x, positions, mod, norm1_g, norm2_g, q_norm_a, k_norm_a, sinks_a,
        full["wqkv_a"], full["wo_a"], full["wqkv_b"], full["wo_b"], wgu, full["w_down"], loss_target)

    g_full = dict(wqkv_a=g["wqkv_a"], wo_a=g["wo_a"], wqkv_b=g["wqkv_b"], wo_b=g["wo_b"],
                  w_gate=g["wgu"], w_up=g["wgu"], w_down=g["wd"])
    which = dict(w_gate=0, w_up=1)
    packed = [_pack_full(g_full[name], axis, which.get(name)) for name, axis in _SHARDED]
    def own(t, index):
        return lax.dynamic_index_in_dim(t, index, 0, keepdims=False)

    from_cores = _exchange_cores(packed, "rs_cores", chunk_axis=0, chunks=4)
    chip_part = [_sum_slabs(own(p, ci), r, "rs_add_cores", with_bf16=True) for p, r in zip(packed, from_cores)]
    from_chips = _exchange_chips([b for _, b in chip_part], "rs_chips", relay_axis=1)
    mine = [_sum_slabs(own(p, chip), r, "rs_add_chips") for (p, _), r in zip(chip_part, from_chips)]
    theirs = _sibling_send(mine, "rs_halves")
    grad = {}
    for (name, _), m, t in zip(_SHARDED, mine, theirs):
        first, second = jnp.where(ci == 0, m, t), jnp.where(ci == 0, t, m)
        grad[name] = jnp.stack([first, second]).reshape(shards[name].shape)

    small_names = ("norm1_g", "norm2_g", "q_norm_a", "k_norm_a", "sinks_a")
    small = [dmod.reshape(-1)] + [g[name].reshape(-1) for name in small_names] + [loss.reshape(-1)]
    small_sizes = [t.shape[0] for t in small]
    small_rows = _rows_of(jnp.concatenate(small))
    small_all = _gather8(small_rows, "gather_small")
    small_sum = _sum_leading(small_all, "sum_small").reshape(-1)
    n_dmod = small_sizes[0]
    dmod_all = small_all.reshape(N_DEV, -1)[:, :n_dmod].reshape(N_DEV, L, NB, 6 * D)
    dmod_all = dmod_all.transpose(1, 0, 2, 3).reshape(L, B_all, 6 * D)
    off = n_dmod
    for name, sz in zip(small_names + ("loss",), small_sizes[1:]):
        grad[name] = small_sum[off:off + sz]
        off += sz
    loss_total = grad.pop("loss").reshape(())
    for name, ref in (("norm1_g", norm1_g), ("norm2_g", norm2_g), ("q_norm_a", q_norm_a),
                      ("k_norm_a", k_norm_a), ("sinks_a", sinks_a)):
        grad[name] = grad[name].reshape(ref.shape)

    n_shard = ada_w.shape[2]
    dmod_shard = lax.dynamic_slice_in_dim(dmod_all, chip * n_shard, n_shard, axis=2)
    grad["ada_w"], gb = _ada_bwd(c_all, dmod_all, dmod_shard)
    grad["ada_b"] = gb.reshape(ada_b.shape)

    weights = dict(ada_w=ada_w, ada_b=ada_b, norm1_g=norm1_g, norm2_g=norm2_g, wqkv_a=wqkv_a, q_norm_a=q_norm_a,
                   k_norm_a=k_norm_a, sinks_a=sinks_a, wo_a=wo_a, wqkv_b=wqkv_b, wo_b=wo_b, w_gate=w_gate,
                   w_up=w_up, w_down=w_down)
    m_in = dict(ada_w=m_ada_w, ada_b=m_ada_b, norm1_g=m_norm1_g, norm2_g=m_norm2_g, wqkv_a=m_wqkv_a,
                q_norm_a=m_q_norm_a, k_norm_a=m_k_norm_a, sinks_a=m_sinks_a, wo_a=m_wo_a, wqkv_b=m_wqkv_b,
                wo_b=m_wo_b, w_gate=m_w_gate, w_up=m_w_up, w_down=m_w_down)
    v_in = dict(ada_w=v_ada_w, ada_b=v_ada_b, norm1_g=v_norm1_g, norm2_g=v_norm2_g, wqkv_a=v_wqkv_a,
                q_norm_a=v_q_norm_a, k_norm_a=v_k_norm_a, sinks_a=v_sinks_a, wo_a=v_wo_a, wqkv_b=v_wqkv_b,
                wo_b=v_wo_b, w_gate=v_w_gate, w_up=v_w_up, w_down=v_w_down)
    names = list(weights)
    delta, new_m, new_v = {}, {}, {}
    for name in names:
        delta[name], new_m[name], new_v[name] = _adamw(weights[name], grad[name], m_in[name], v_in[name],
                                                       "adamw_" + name)
    return (loss_total, grad_x, *[grad[k] for k in names], *[delta[k] for k in names],
            *[new_m[k] for k in names], *[new_v[k] for k in names])
```

```python
import jax
import jax.numpy as jnp
from jax import lax
from jax.experimental import pallas as pl
from jax.experimental.pallas import tpu as pltpu

F32 = jnp.float32
BF16 = jnp.bfloat16

DEPTH = 4
HEAD_DIM = 64
N_Q_A = 16
N_KV_A = 2
GROUP_A = N_Q_A // N_KV_A
N_H_B = 16
BLOCK = 128
ROT_DIM = HEAD_DIM // 4
ROPE_THETA = 500000.0
EPS = 1e-6
ATTN_SCALE = HEAD_DIM ** -0.5
NEG_BIG = -1e30

ADAM_LR = 0.001
ADAM_B1 = 0.9
ADAM_B2 = 0.999
ADAM_EPS = 1e-08
ADAM_WD = 0.01
ADAM_STEP = 10

N_DEV = 8
LANES = 128
PACK_COLS = 1024
VMEM_LIMIT_BYTES = 48 * 1024 * 1024
MESH = pl.DeviceIdType.MESH

_NT = (((1,), (1,)), ((), ()))
_TN = (((0,), (0,)), ((), ()))
_NN = (((1,), (0,)), ((), ()))


def _params(sem=None, vmem_limit_bytes=VMEM_LIMIT_BYTES):
    return pltpu.CompilerParams(vmem_limit_bytes=vmem_limit_bytes, dimension_semantics=sem)


def _pick(n, cap, mult):
    best = None
    for t in range(mult, min(n, cap) + 1, mult):
        if n % t == 0:
            best = t
    return n if best is None else best


_ANY = pl.BlockSpec(memory_space=pl.ANY)


def _window(index, axis, q, n, shape):
    rest = [slice(None)] * len(shape)
    size = shape[axis] // n
    rest[axis] = pl.ds(q * size, size)
    return tuple(index) + tuple(rest)


def _all_gather8(xs, name, local_axis=0, local_chunks=1, relay_axis=None):
    n = len(xs)
    n_sems = 7 if relay_axis is None else 9

    def body(*refs):
        x_refs, out_refs = refs[:n], refs[n:2 * n]
        send_sems, recv_sems, local_sems = refs[2 * n:]
        xi, yi, ci = lax.axis_index("x"), lax.axis_index("y"), lax.axis_index("c")
        me, sibling = (xi, yi, ci), (xi, yi, 1 - ci)
        chips = [(1 - xi, yi), (xi, 1 - yi), (1 - xi, 1 - yi)]

        def slab(w, px, py, pc):
            return out_refs[w].at[4 * px + 2 * py + pc]

        def copy(w, k, block, to, src=None):
            return pltpu.make_async_remote_copy(
                src_ref=slab(w, *block) if src is None else src, dst_ref=slab(w, *block),
                send_sem=send_sems.at[k, w], recv_sem=recv_sems.at[k, w], device_id=to, device_id_type=MESH)

        mine = []
        for w in range(n):
            for q in range(local_chunks):
                part = _window((), local_axis, q, local_chunks, xs[w].shape)
                mine.append(pltpu.make_async_copy(x_refs[w].at[part], slab(w, *me).at[part], local_sems.at[w, q]))
                mine[-1].start()
        direct = chips if relay_axis is None else chips[:2]
        first = [copy(w, 0, me, sibling, src=x_refs[w]) for w in range(n)]
        first += [copy(w, 1 + j, me, (*chip, ci), src=x_refs[w]) for j, chip in enumerate(direct) for w in range(n)]
        for cp in first:
            cp.start()

        def relay(w, part, block, to):
            piece = _window((), relay_axis, part, 2, xs[w].shape)
            return pltpu.make_async_remote_copy(
                src_ref=slab(w, *block).at[piece], dst_ref=slab(w, *block).at[piece],
                send_sem=send_sems.at[7 + part, w], recv_sem=recv_sems.at[7 + part, w],
                device_id=to, device_id_type=MESH)

        passed = []
        for j, chip in enumerate(direct):
            for w in range(n):
                copy(w, 1 + j, (*chip, ci), me).wait_recv()
                passed.append(copy(w, 4 + j, (*chip, ci), sibling))
                passed[-1].start()
                if relay_axis is not None:
                    passed.append(relay(w, j, (*chip, ci), (*chips[1 - j], ci)))
                    passed[-1].start()
        if relay_axis is not None:
            for w in range(n):
                for part in range(2):
                    relay(w, part, (*chips[2], ci), me).wait_recv()
                passed.append(copy(w, 6, (*chips[2], ci), sibling))
                passed[-1].start()
        for w in range(n):
            copy(w, 0, sibling, me).wait_recv()
        for j, chip in enumerate(chips):
            for w in range(n):
                copy(w, 4 + j, (*chip, 1 - ci), me).wait_recv()
        for cp in first + passed:
            cp.wait_send()
        for cp in mine:
            cp.wait()

    return pl.pallas_call(
        body, name=name,
        out_shape=[jax.ShapeDtypeStruct((N_DEV,) + x.shape, x.dtype) for x in xs],
        in_specs=[_ANY] * n, out_specs=[_ANY] * n,
        scratch_shapes=[pltpu.SemaphoreType.DMA((n_sems, n)), pltpu.SemaphoreType.DMA((n_sems, n)),
                        pltpu.SemaphoreType.DMA((n, local_chunks))],
    )(*xs)


def _exchange_cores(xs, name, chunk_axis=0, chunks=1):
    n = len(xs)
    n_peers = 1

    def body(*refs):
        x_refs, out_refs = refs[:n], refs[n:2 * n]
        send_sems, recv_sems = refs[2 * n:]
        xi, yi, ci = lax.axis_index("x"), lax.axis_index("y"), lax.axis_index("c")
        peers = [(1 - ci, (xi, yi, 1 - ci))]
        copies = []
        for k, (p, dev) in enumerate(peers):
            for w in range(n):
                slab_shape = xs[w].shape[1:]
                for q in range(chunks):
                    copies.append(pltpu.make_async_remote_copy(
                        src_ref=x_refs[w].at[_window((p,), chunk_axis, q, chunks, slab_shape)],
                        dst_ref=out_refs[w].at[_window((k,), chunk_axis, q, chunks, slab_shape)],
                        send_sem=send_sems.at[k, w, q], recv_sem=recv_sems.at[k, w, q],
                        device_id=dev, device_id_type=MESH))
                    copies[-1].start()
        for cp in copies:
            cp.wait()

    return pl.pallas_call(
        body, name=name,
        out_shape=[jax.ShapeDtypeStruct((n_peers,) + x.shape[1:], x.dtype) for x in xs],
        in_specs=[_ANY] * n, out_specs=[_ANY] * n,
        scratch_shapes=[pltpu.SemaphoreType.DMA((n_peers, n, chunks)), pltpu.SemaphoreType.DMA((n_peers, n, chunks))],
    )(*xs)


def _exchange_chips(xs, name, relay_axis):
    n = len(xs)

    def half_shape(x):
        shape = list(x.shape[1:])
        shape[relay_axis] //= 2
        return tuple(shape)

    def body(*refs):
        x_refs, out_refs, hop_refs = refs[:n], refs[n:2 * n], refs[2 * n:3 * n]
        send_sems, recv_sems = refs[3 * n:]
        xi, yi, ci = lax.axis_index("x"), lax.axis_index("y"), lax.axis_index("c")
        nbr = [(1 - xi, yi, ci), (xi, 1 - yi, ci)]
        slab_of_nbr = [2 * (1 - xi) + yi, 2 * xi + (1 - yi)]
        slab_of_diag = 2 * (1 - xi) + (1 - yi)

        def copy(k, w, src, dst, to):
            return pltpu.make_async_remote_copy(src_ref=src, dst_ref=dst, send_sem=send_sems.at[k, w],
                                                recv_sem=recv_sems.at[k, w], device_id=to, device_id_type=MESH)

        def piece(w, part):
            return _window((), relay_axis, part, 2, xs[w].shape[1:])

        sent = []
        for w in range(n):
            for j in range(2):
                sent.append(copy(j, w, x_refs[w].at[slab_of_nbr[j]], out_refs[w].at[j], nbr[j]))
                sent.append(copy(2 + j, w, x_refs[w].at[(slab_of_diag,) + piece(w, j)], hop_refs[w].at[j], nbr[j]))
        for cp in sent:
            cp.start()
        for w in range(n):
            for j in range(2):
                copy(2 + j, w, hop_refs[w].at[j], hop_refs[w].at[j], nbr[j]).wait_recv()
                sent.append(copy(4 + j, w, hop_refs[w].at[j], out_refs[w].at[(2,) + piece(w, j)], nbr[1 - j]))
                sent[-1].start()
        for w in range(n):
            for j in range(2):
                copy(j, w, out_refs[w].at[j], out_refs[w].at[j], nbr[j]).wait_recv()
                half = out_refs[w].at[(2,) + piece(w, j)]
                copy(4 + j, w, half, half, nbr[1 - j]).wait_recv()
        for cp in sent:
            cp.wait_send()

    out = pl.pallas_call(
        body, name=name,
        out_shape=[jax.ShapeDtypeStruct((3,) + x.shape[1:], x.dtype) for x in xs]
        + [jax.ShapeDtypeStruct((2,) + half_shape(x), x.dtype) for x in xs],
        in_specs=[_ANY] * n, out_specs=[_ANY] * (2 * n),
        scratch_shapes=[pltpu.SemaphoreType.DMA((6, n)), pltpu.SemaphoreType.DMA((6, n))],
    )(*xs)
    return out[:n]


def _sibling_gather(xs, name, chunk_axis=1, chunks=4):
    n = len(xs)

    def body(*refs):
        x_refs, out_refs = refs[:n], refs[n:2 * n]
        send_sems, recv_sems, local_sems = refs[2 * n:]
        xi, yi, ci = lax.axis_index("x"), lax.axis_index("y"), lax.axis_index("c")
        remote, local = [], []
        for w in range(n):
            for q in range(chunks):
                src = x_refs[w].at[_window((), chunk_axis, q, chunks, xs[w].shape)]
                dst = out_refs[w].at[_window((ci,), chunk_axis, q, chunks, xs[w].shape)]
                remote.append(pltpu.make_async_remote_copy(
                    src_ref=src, dst_ref=dst, send_sem=send_sems.at[w, q], recv_sem=recv_sems.at[w, q],
                    device_id=(xi, yi, 1 - ci), device_id_type=MESH))
                remote[-1].start()
                local.append(pltpu.make_async_copy(src, dst, local_sems.at[w, q]))
                local[-1].start()
        for cp in remote + local:
            cp.wait()

    sems = pltpu.SemaphoreType.DMA((n, chunks))
    return pl.pallas_call(
        body, name=name,
        out_shape=[jax.ShapeDtypeStruct((2,) + x.shape, x.dtype) for x in xs],
        in_specs=[_ANY] * n, out_specs=[_ANY] * n,
        scratch_shapes=[sems, sems, sems],
    )(*xs)


def _sum_leading(x, name, own=None, with_bf16=False):
    P, R, C = x.shape
    tr = _pick(R, max(16, (1 << 19) // (C * (P + 1))), 16)

    def body(*refs):
        n_in = 1 if own is None else 2
        x_ref = refs[n_in - 1]
        acc = x_ref[0].astype(F32) if own is None else refs[0][...] + x_ref[0].astype(F32)
        for p in range(1, P):
            acc = acc + x_ref[p].astype(F32)
        refs[n_in][...] = acc
        if with_bf16:
            refs[n_in + 1][...] = acc.astype(BF16)

    flat = pl.BlockSpec((tr, C), lambda r: (r, 0))
    slabs = pl.BlockSpec((P, tr, C), lambda r: (0, r, 0))
    out = pl.pallas_call(
        body, name=name, grid=(R // tr,),
        in_specs=[slabs] if own is None else [flat, slabs],
        out_specs=[flat, flat] if with_bf16 else [flat],
        out_shape=[jax.ShapeDtypeStruct((R, C), F32)] + ([jax.ShapeDtypeStruct((R, C), BF16)] if with_bf16 else []),
        compiler_params=_params(("arbitrary",)),
    )(*([x] if own is None else [own, x]))
    return out if with_bf16 else out[0]


MATMUL_SINGLE_K = 2816
MATMUL_VMEM_BUDGET = 36 * 1024 * 1024


def _matmul(a, b, mode, out_dtype, name, swiglu=False):
    if mode == "nn":
        (M, K), N = a.shape, b.shape[1]
    elif mode == "nt":
        (M, K), N = a.shape, b.shape[0]
    else:
        (K, M), N = a.shape, b.shape[1]
    tm = _pick(M, 1024 if mode != "tn" else 1536, 128)
    tn = _pick(N, 1536, 128)
    if swiglu:
        tm, tn = _pick(M, 1024 if out_dtype == BF16 else 512, 128), 2 * _ff_tile(N // 2)
    out_bytes = jnp.dtype(out_dtype).itemsize
    tk = K
    if K > MATMUL_SINGLE_K:
        for cap in (2048, 1024, 512):
            tk = _pick(K, cap, 128)
            blocks = 2 * 2 * tk * (tm + tn) + tm * tn * (2 * out_bytes + (4 if out_dtype != F32 else 0))
            if blocks <= MATMUL_VMEM_BUDGET:
                break
    nk = K // tk
    dims = {"nn": _NN, "nt": _NT, "tn": _TN}[mode]
    use_scratch = nk > 1 and out_dtype != F32

    def body(a_ref, b_ref, *refs):
        o_ref = refs[0]

        def product():
            return lax.dot_general(a_ref[...].astype(BF16), b_ref[...].astype(BF16), dims,
                                   preferred_element_type=F32)

        if nk == 1:
            part = product()
            o_ref[...] = part.astype(o_ref.dtype)
            if swiglu:
                g = part[:, :tn // 2]
                refs[1][...] = (g * _sigmoid(g) * part[:, tn // 2:]).astype(BF16)
            return
        k = pl.program_id(2)
        acc_ref = refs[-1] if use_scratch else o_ref

        @pl.when(k == 0)
        def _():
            acc_ref[...] = jnp.zeros_like(acc_ref)

        acc_ref[...] += product()

        if use_scratch:
            @pl.when(k == nk - 1)
            def _():
                o_ref[...] = acc_ref[...].astype(o_ref.dtype)

    if mode == "tn":
        a_spec = pl.BlockSpec((tk, tm), lambda i, j, k: (k, i))
    else:
        a_spec = pl.BlockSpec((tm, tk), lambda i, j, k: (i, k))
    if mode == "nt":
        b_spec = pl.BlockSpec((tn, tk), lambda i, j, k: (j, k))
    else:
        b_spec = pl.BlockSpec((tk, tn), lambda i, j, k: (k, j))
    out_specs = [pl.BlockSpec((tm, tn), lambda i, j, k: (i, j))]
    out_shape = [jax.ShapeDtypeStruct((M, N), out_dtype)]
    if swiglu:
        assert nk == 1 and mode == "nn"
        out_specs.append(pl.BlockSpec((tm, tn // 2), lambda i, j, k: (i, j)))
        out_shape.append(jax.ShapeDtypeStruct((M, N // 2), BF16))
    out = pl.pallas_call(
        body, name=name, grid=(M // tm, N // tn, nk),
        in_specs=[a_spec, b_spec], out_specs=out_specs, out_shape=out_shape,
        scratch_shapes=[pltpu.VMEM((tm, tn), F32)] if use_scratch else [],
        compiler_params=_params(("parallel", "parallel", "arbitrary")),
    )(a, b)
    return out if swiglu else out[0]


def _row_tile(S):
    return _pick(S, 512, 8)


def _norm_mod_fwd(x, gain, sc, sh):
    NB, S, D = x.shape
    tr = _row_tile(S)

    def body(x_ref, g_ref, sc_ref, sh_ref, h_ref):
        xv = x_ref[...]
        ms = jnp.mean(xv * xv, axis=-1, keepdims=True)
        n = xv * lax.rsqrt(ms + EPS) * g_ref[...]
        h_ref[...] = (n * (1.0 + sc_ref[...]) + sh_ref[...]).astype(BF16)

    tok = pl.BlockSpec((None, tr, D), lambda b, r: (b, r, 0))
    per_ex = pl.BlockSpec((None, 1, D), lambda b, r: (b, 0, 0))
    return pl.pallas_call(
        body, name="norm_mod_fwd", grid=(NB, S // tr),
        in_specs=[tok, pl.BlockSpec((1, D), lambda b, r: (0, 0)), per_ex, per_ex],
        out_specs=tok, out_shape=jax.ShapeDtypeStruct((NB, S, D), BF16),
        compiler_params=_params(("parallel", "parallel")),
    )(x, gain, sc, sh)


def _norm_mod_bwd(a, w, x, gain, sc, dres, name, y=None, g=None):
    NB, S, D = x.shape
    T, K = a.shape
    tm = _pick(S, 512, 128)
    per_ex_tiles = S // tm
    tk = K if K <= MATMUL_SINGLE_K else _pick(K, 3072, 128)
    nk = K // tk
    gated = y is not None

    def body(*refs):
        a_ref, w_ref, x_ref, g_ref, sc_ref, dres_ref = refs[:6]
        refs = refs[6:]
        if gated:
            y_ref, gate_ref = refs[:2]
            refs = refs[2:]
        dx_ref, dsh_ref, dsc_ref, dgain_ref = refs[:4]
        acc_ref = refs[-1]
        i, k = pl.program_id(0), pl.program_id(1)

        @pl.when(k == 0)
        def _():
            acc_ref[...] = jnp.zeros_like(acc_ref)

        acc_ref[...] += lax.dot_general(a_ref[...], w_ref[...], _NT, preferred_element_type=F32)

        @pl.when(k == nk - 1)
        def _():
            first_of_example = i % per_ex_tiles == 0

            @pl.when(first_of_example)
            def _():
                dsh_ref[...] = jnp.zeros_like(dsh_ref)
                dsc_ref[...] = jnp.zeros_like(dsc_ref)
                if gated:
                    refs[5][...] = jnp.zeros_like(refs[5])

            @pl.when(i == 0)
            def _():
                dgain_ref[...] = jnp.zeros_like(dgain_ref)

            xv = x_ref[...]
            rstd = lax.rsqrt(jnp.mean(xv * xv, axis=-1, keepdims=True) + EPS)
            xh = xv * rstd
            gn = g_ref[...]
            dh = acc_ref[...]
            dsh_ref[...] += jnp.sum(dh, axis=0, keepdims=True)
            dsc_ref[...] += jnp.sum(dh * (xh * gn), axis=0, keepdims=True)
            dn = dh * (1.0 + sc_ref[...])
            dgain_ref[...] += jnp.sum(dn * xh, axis=0, keepdims=True)
            dxh = dn * gn
            proj = jnp.mean(dxh * xh, axis=-1, keepdims=True)
            dx = rstd * (dxh - xh * proj) + dres_ref[...]
            dx_ref[...] = dx
            if gated:
                refs[4][...] = (dx * gate_ref[...]).astype(BF16)
                refs[5][...] += jnp.sum(dx * y_ref[...], axis=0, keepdims=True)

    tok = pl.BlockSpec((None, tm, D), lambda i, k: (i // per_ex_tiles, i % per_ex_tiles, 0))
    per_ex = pl.BlockSpec((None, 1, D), lambda i, k: (i // per_ex_tiles, 0, 0))
    row = pl.BlockSpec((1, D), lambda i, k: (0, 0))
    in_specs = [pl.BlockSpec((tm, tk), lambda i, k: (i, k)), pl.BlockSpec((D, tk), lambda i, k: (0, k)),
                tok, row, per_ex, tok]
    out_specs = [tok, per_ex, per_ex, row]
    out_shape = [jax.ShapeDtypeStruct((NB, S, D), F32), jax.ShapeDtypeStruct((NB, 1, D), F32),
                 jax.ShapeDtypeStruct((NB, 1, D), F32), jax.ShapeDtypeStruct((1, D), F32)]
    operands = [a, w, x, gain, sc, dres]
    if gated:
        in_specs += [tok, per_ex]
        out_specs += [tok, per_ex]
        out_shape += [jax.ShapeDtypeStruct((NB, S, D), BF16), jax.ShapeDtypeStruct((NB, 1, D), F32)]
        operands += [y, g]
    return pl.pallas_call(
        body, name=name, grid=(T // tm, nk), in_specs=in_specs, out_specs=out_specs, out_shape=out_shape,
        scratch_shapes=[pltpu.VMEM((tm, D), F32)],
        compiler_params=_params(("arbitrary", "arbitrary")),
    )(*operands)


def _gate_res(x, y, g, norm=None):
    NB, S, D = x.shape
    tr = _row_tile(S)

    def body(x_ref, y_ref, g_ref, *refs):
        xo = x_ref[...] + g_ref[...] * y_ref[...]
        refs[-1 if norm is None else -2][...] = xo
        if norm is not None:
            gain_ref, sc_ref, sh_ref, _, h_ref = refs
            n = xo * lax.rsqrt(jnp.mean(xo * xo, axis=-1, keepdims=True) + EPS) * gain_ref[...]
            h_ref[...] = (n * (1.0 + sc_ref[...]) + sh_ref[...]).astype(BF16)

    tok = pl.BlockSpec((None, tr, D), lambda b, r: (b, r, 0))
    per_ex = pl.BlockSpec((None, 1, D), lambda b, r: (b, 0, 0))
    in_specs, out_specs, operands = [tok, tok, per_ex], [tok], [x, y, g]
    out_shape = [jax.ShapeDtypeStruct((NB, S, D), F32)]
    if norm is not None:
        in_specs += [pl.BlockSpec((1, D), lambda b, r: (0, 0)), per_ex, per_ex]
        out_specs.append(tok)
        out_shape.append(jax.ShapeDtypeStruct((NB, S, D), BF16))
        operands += list(norm)
    out = pl.pallas_call(
        body, name="gate_res", grid=(NB, S // tr), in_specs=in_specs, out_specs=out_specs, out_shape=out_shape,
        compiler_params=_params(("parallel", "parallel")),
    )(*operands)
    return out[0] if norm is None else out


def _gate_res_bwd(dxo, y, g):
    NB, S, D = dxo.shape
    tr = _row_tile(S)

    def body(d_ref, y_ref, g_ref, dy_ref, dg_ref):
        @pl.when(pl.program_id(1) == 0)
        def _():
            dg_ref[...] = jnp.zeros_like(dg_ref)

        d = d_ref[...]
        dy_ref[...] = (d * g_ref[...]).astype(BF16)
        dg_ref[...] += jnp.sum(d * y_ref[...], axis=0, keepdims=True)

    tok = pl.BlockSpec((None, tr, D), lambda b, r: (b, r, 0))
    per_ex = pl.BlockSpec((None, 1, D), lambda b, r: (b, 0, 0))
    return pl.pallas_call(
        body, name="gate_res_bwd", grid=(NB, S // tr), in_specs=[tok, tok, per_ex], out_specs=[tok, per_ex],
        out_shape=[jax.ShapeDtypeStruct((NB, S, D), BF16), jax.ShapeDtypeStruct((NB, 1, D), F32)],
        compiler_params=_params(("arbitrary", "arbitrary")),
    )(dxo, y, g)


def _sigmoid(v):
    return 1.0 / (1.0 + jnp.exp(-v))


def _ff_tile(F):
    return _pick(F, 1536, 128)


def _interleave(gate, up):
    F = gate.shape[-1]
    tf = _ff_tile(F)
    parts = []
    for j in range(F // tf):
        parts += [gate[..., j * tf:(j + 1) * tf], up[..., j * tf:(j + 1) * tf]]
    return jnp.concatenate(parts, axis=-1)


def _swiglu_bwd(dm, wd, gu):
    T, D = dm.shape
    F = wd.shape[0]
    tf = _ff_tile(F)
    tm = _pick(T, 1024, 128)
    assert D <= MATMUL_SINGLE_K

    def body(a_ref, b_ref, gu_ref, o_ref):
        d = lax.dot_general(a_ref[...], b_ref[...], _NT, preferred_element_type=F32)
        g, u = gu_ref[:, :tf].astype(F32), gu_ref[:, tf:].astype(F32)
        s = _sigmoid(g)
        o_ref[:, :tf] = (d * u * (s * (1.0 + g * (1.0 - s)))).astype(BF16)
        o_ref[:, tf:] = (d * (g * s)).astype(BF16)

    return pl.pallas_call(
        body, name="swiglu_bwd", grid=(T // tm, F // tf),
        in_specs=[pl.BlockSpec((tm, D), lambda i, j: (i, 0)), pl.BlockSpec((tf, D), lambda i, j: (j, 0)),
                  pl.BlockSpec((tm, 2 * tf), lambda i, j: (i, j))],
        out_specs=pl.BlockSpec((tm, 2 * tf), lambda i, j: (i, j)),
        out_shape=jax.ShapeDtypeStruct((T, 2 * F), BF16),
        compiler_params=_params(("parallel", "parallel")),
    )(dm, wd, gu)


def _loss_fwd_bwd(y, target):
    NB, S, D = y.shape
    tr = _row_tile(S)

    def body(y_ref, t_ref, l_ref, d_ref):
        @pl.when((pl.program_id(0) == 0) & (pl.program_id(1) == 0))
        def _():
            l_ref[...] = jnp.zeros_like(l_ref)

        e = y_ref[...] - t_ref[...]
        d_ref[...] = e / D
        l_ref[...] += 0.5 * jnp.sum(jnp.mean(e * e, axis=-1, keepdims=True), axis=0, keepdims=True)

    tok = pl.BlockSpec((None, tr, D), lambda b, r: (b, r, 0))
    return pl.pallas_call(
        body, name="loss", grid=(NB, S // tr), in_specs=[tok, tok],
        out_specs=[pl.BlockSpec((1, 1), lambda b, r: (0, 0)), tok],
        out_shape=[jax.ShapeDtypeStruct((1, 1), F32), jax.ShapeDtypeStruct((NB, S, D), F32)],
        compiler_params=_params(("arbitrary", "arbitrary")),
    )(y, target)


def _half_sums(v, lo):
    sa = jnp.sum(jnp.where(lo, v, 0.0), axis=-1, keepdims=True)
    sb = jnp.sum(jnp.where(lo, 0.0, v), axis=-1, keepdims=True)
    return jnp.where(lo, sa, sb)


def _rope_swap(v, lane64):
    up = pltpu.roll(v, LANES - ROT_DIM // 2, 1)
    down = pltpu.roll(v, ROT_DIM // 2, 1)
    return jnp.where(lane64 < ROT_DIM // 2, up, jnp.where(lane64 < ROT_DIM, down, 0.0))


def _qk_prep_fwd(qkv, tab_c, tab_s, gains):
    T, W = qkv.shape
    R = W // LANES
    tt = _pick(T, 256, 8)

    def body(x_ref, c_ref, s_ref, g_ref, o_ref):
        lane = lax.broadcasted_iota(jnp.int32, (tt, LANES), 1)
        lo = lane < HEAD_DIM
        lane64 = lane & (HEAD_DIM - 1)
        c, s = c_ref[...], s_ref[...]
        for j in range(R - 1):
            cols = slice(j * LANES, (j + 1) * LANES)
            xv = x_ref[:, cols]
            rstd = lax.rsqrt(_half_sums(xv * xv, lo) / HEAD_DIM + EPS)
            yn = xv * rstd * g_ref[j:j + 1, :]
            o_ref[:, cols] = (yn * c + _rope_swap(yn, lane64) * s).astype(BF16)
        o_ref[:, (R - 1) * LANES:] = x_ref[:, (R - 1) * LANES:].astype(BF16)

    tok = pl.BlockSpec((tt, W), lambda t: (t, 0))
    tab = pl.BlockSpec((tt, LANES), lambda t: (t, 0))
    return pl.pallas_call(
        body, name="qk_prep_fwd", grid=(T // tt,),
        in_specs=[tok, tab, tab, pl.BlockSpec((R, LANES), lambda t: (0, 0))],
        out_specs=tok, out_shape=jax.ShapeDtypeStruct((T, W), BF16),
        compiler_params=_params(("parallel",)),
    )(qkv, tab_c, tab_s, gains)


def _qk_prep_bwd(qkv, dq, dk, dv, tab_c, tab_s, gains):
    T, W = qkv.shape
    R = W // LANES
    QW = dq.shape[1]
    tt = _pick(T, 256, 8)

    def body(x_ref, dq_ref, dk_ref, dv_ref, c_ref, s_ref, g_ref, o_ref, dg_ref):
        @pl.when(pl.program_id(0) == 0)
        def _():
            dg_ref[...] = jnp.zeros_like(dg_ref)

        lane = lax.broadcasted_iota(jnp.int32, (tt, LANES), 1)
        lo = lane < HEAD_DIM
        lane64 = lane & (HEAD_DIM - 1)
        c, s = c_ref[...], s_ref[...]
        for j in range(R - 1):
            cols = slice(j * LANES, (j + 1) * LANES)
            xv = x_ref[:, cols]
            d = dq_ref[:, cols] if j < R - 2 else dk_ref[...]
            rstd = lax.rsqrt(_half_sums(xv * xv, lo) / HEAD_DIM + EPS)
            xh = xv * rstd
            dyn = d * c + _rope_swap(d * s, lane64)
            dg_ref[j:j + 1, :] += jnp.sum(dyn * xh, axis=0, keepdims=True)
            dxh = dyn * g_ref[j:j + 1, :]
            proj = _half_sums(dxh * xh, lo) / HEAD_DIM
            o_ref[:, cols] = (rstd * (dxh - xh * proj)).astype(BF16)
        o_ref[:, (R - 1) * LANES:] = dv_ref[...].astype(BF16)

    tok = pl.BlockSpec((tt, W), lambda t: (t, 0))
    tab = pl.BlockSpec((tt, LANES), lambda t: (t, 0))
    gsp = pl.BlockSpec((R, LANES), lambda t: (0, 0))
    return pl.pallas_call(
        body, name="qk_prep_bwd", grid=(T // tt,),
        in_specs=[tok, pl.BlockSpec((tt, QW), lambda t: (t, 0)), tab, tab, tab, tab, gsp], out_specs=[tok, gsp],
        out_shape=[jax.ShapeDtypeStruct((T, W), BF16), jax.ShapeDtypeStruct((R, LANES), F32)],
        compiler_params=_params(("arbitrary",)),
    )(qkv, dq, dk, dv, tab_c, tab_s, gains)


def _band_mask(i):
    r = lax.broadcasted_iota(jnp.int32, (2 * BLOCK, 2 * BLOCK), 0) & (BLOCK - 1)
    c = lax.broadcasted_iota(jnp.int32, (2 * BLOCK, 2 * BLOCK), 1)
    rel = r + BLOCK - c
    return (rel >= 0) & (rel < BLOCK) & ((c >= BLOCK) | (i > 0))


def _swa_softmax(s, valid, sink):
    s = jnp.where(valid, s * ATTN_SCALE, NEG_BIG)
    m = jnp.maximum(jnp.max(s, axis=1, keepdims=True), sink)
    p = jnp.exp(s - m)
    ps = jnp.exp(sink - m)
    denom = jnp.sum(p, axis=1, keepdims=True) + ps
    return p / denom, ps / denom


A_GROUP = 4


Q_WIDTH_A = N_Q_A * HEAD_DIM
N_PAIR_A = Q_WIDTH_A // LANES


def _swa_specs():
    qs = pl.BlockSpec((None, BLOCK, Q_WIDTH_A), lambda b, i: (b, i, 0))

    def kv(col, back):
        return pl.BlockSpec((None, BLOCK, LANES), lambda b, i: (b, jnp.maximum(i - back, 0), col))

    return qs, kv(N_PAIR_A, 1), kv(N_PAIR_A, 0), kv(N_PAIR_A + 1, 1), kv(N_PAIR_A + 1, 0)


def _dup_heads(t):
    lo = lax.broadcasted_iota(jnp.int32, t.shape, 1) < HEAD_DIM
    sw = pltpu.roll(t.astype(F32), HEAD_DIM, 1).astype(BF16)
    return jnp.where(lo, t, sw), jnp.where(lo, sw, t)


def _kv_tiles(kp_ref, kc_ref, vp_ref, vc_ref):
    kd = _dup_heads(jnp.concatenate([kp_ref[...], kc_ref[...]], axis=0))
    vd = _dup_heads(jnp.concatenate([vp_ref[...], vc_ref[...]], axis=0))
    return kd, vd


def _attn_a_fwd(qkn, sinks):
    NB, S, _ = qkn.shape
    qs, kp, kc, vp, vc = _swa_specs()

    def body(q_ref, kp_ref, kc_ref, vp_ref, vc_ref, sink_ref, o_ref):
        i = pl.program_id(1)
        kd, vd = _kv_tiles(kp_ref, kc_ref, vp_ref, vc_ref)
        valid = _band_mask(i)
        lo = lax.broadcasted_iota(jnp.int32, (BLOCK, LANES), 1) < HEAD_DIM
        top = lax.broadcasted_iota(jnp.int32, (2 * BLOCK, 1), 0) < BLOCK
        for first in range(0, N_PAIR_A, A_GROUP):
            pairs = range(first, first + A_GROUP)
            qs_ = [jnp.concatenate(_head_halves(q_ref[:, p * LANES:(p + 1) * LANES], lo), axis=0) for p in pairs]
            ss = [lax.dot_general(q, kd[2 * p // GROUP_A], _NT, preferred_element_type=F32) for q, p in zip(qs_, pairs)]
            pns = [_swa_softmax(s, valid, jnp.where(top, sink_ref[2 * p], sink_ref[2 * p + 1]))[0]
                   for s, p in zip(ss, pairs)]
            pvs = [jnp.dot(pn.astype(BF16), vd[2 * p // GROUP_A], preferred_element_type=F32) for pn, p in zip(pns, pairs)]
            for pv, p in zip(pvs, pairs):
                o_ref[:, p * LANES:(p + 1) * LANES] = jnp.where(lo, pv[:BLOCK], pv[BLOCK:]).astype(BF16)

    return pl.pallas_call(
        body, name="attn_a_fwd", grid=(NB, S // BLOCK),
        in_specs=[qs, kp, kc, vp, vc, pl.BlockSpec(memory_space=pltpu.SMEM)],
        out_specs=qs, out_shape=jax.ShapeDtypeStruct((NB, S, Q_WIDTH_A), BF16),
        compiler_params=_params(("parallel", "arbitrary")),
    )(qkn, qkn, qkn, qkn, qkn, sinks)


def _attn_a_bwd(qkn, do, sinks):
    NB, S, _ = qkn.shape
    qs, kp, kc, vp, vc = _swa_specs()
    full = pl.BlockSpec((None, S, LANES), lambda b, i: (b, 0, 0))
    sink_out = pl.BlockSpec((None, N_Q_A, LANES), lambda b, i: (b, 0, 0))

    def body(q_ref, do_ref, kp_ref, kc_ref, vp_ref, vc_ref, sink_ref, dq_ref, dk_ref, dv_ref, ds_ref, dk_s, dv_s):
        i = pl.program_id(1)

        @pl.when(i == 0)
        def _():
            dk_ref[...] = jnp.zeros_like(dk_ref)
            dv_ref[...] = jnp.zeros_like(dv_ref)
            ds_ref[...] = jnp.zeros_like(ds_ref)

        dk_s[...] = jnp.zeros_like(dk_s)
        dv_s[...] = jnp.zeros_like(dv_s)
        kd, vd = _kv_tiles(kp_ref, kc_ref, vp_ref, vc_ref)
        valid = _band_mask(i)
        lo = lax.broadcasted_iota(jnp.int32, (BLOCK, LANES), 1) < HEAD_DIM
        top = lax.broadcasted_iota(jnp.int32, (2 * BLOCK, 1), 0) < BLOCK
        for first in range(0, N_PAIR_A, A_GROUP):
            pairs = range(first, first + A_GROUP)
            kvs = [2 * p // GROUP_A for p in pairs]
            qs_ = [jnp.concatenate(_head_halves(q_ref[:, p * LANES:(p + 1) * LANES], lo), axis=0) for p in pairs]
            dos = [jnp.concatenate(_head_halves(do_ref[:, p * LANES:(p + 1) * LANES], lo), axis=0) for p in pairs]
            ss = [lax.dot_general(q, kd[kv], _NT, preferred_element_type=F32) for q, kv in zip(qs_, kvs)]
            dps = [lax.dot_general(d, vd[kv], _NT, preferred_element_type=F32) for d, kv in zip(dos, kvs)]
            sm = [_swa_softmax(s, valid, jnp.where(top, sink_ref[2 * p], sink_ref[2 * p + 1])) for s, p in zip(ss, pairs)]
            deltas = [jnp.sum(pn * dp, axis=1, keepdims=True) for (pn, _), dp in zip(sm, dps)]
            dsbs = [(pn * (dp - delta) * ATTN_SCALE).astype(BF16) for (pn, _), dp, delta in zip(sm, dps, deltas)]
            for n, p in enumerate(pairs):
                dq2 = jnp.dot(dsbs[n], kd[kvs[n]], preferred_element_type=F32)
                dq_ref[:, p * LANES:(p + 1) * LANES] = jnp.where(lo, dq2[:BLOCK], dq2[BLOCK:])
                dk_s[kvs[n]] += lax.dot_general(dsbs[n], qs_[n], _TN, preferred_element_type=F32)
                dv_s[kvs[n]] += lax.dot_general(sm[n][0].astype(BF16), dos[n], _TN, preferred_element_type=F32)
                t = sm[n][1] * deltas[n]
                for hh in range(2):
                    dsink = -jnp.sum(t[hh * BLOCK:(hh + 1) * BLOCK], axis=0, keepdims=True)
                    ds_ref[2 * p + hh:2 * p + hh + 1, :] += jnp.broadcast_to(dsink, (1, LANES))

        lo2 = lax.broadcasted_iota(jnp.int32, (2 * BLOCK, LANES), 1) < HEAD_DIM

        def fold(acc):
            halves = [acc[kv] + pltpu.roll(acc[kv], HEAD_DIM, 1) for kv in range(N_KV_A)]
            return jnp.where(lo2, halves[0], halves[1])

        dk2, dv2 = fold(dk_s), fold(dv_s)

        @pl.when(i > 0)
        def _():
            start = pl.multiple_of((i - 1) * BLOCK, BLOCK)
            dk_ref[pl.ds(start, 2 * BLOCK), :] += dk2
            dv_ref[pl.ds(start, 2 * BLOCK), :] += dv2

        @pl.when(i == 0)
        def _():
            dk_ref[0:BLOCK, :] += dk2[BLOCK:, :]
            dv_ref[0:BLOCK, :] += dv2[BLOCK:, :]

    slots = pltpu.VMEM((N_KV_A, 2 * BLOCK, LANES), F32)
    return pl.pallas_call(
        body, name="attn_a_bwd", grid=(NB, S // BLOCK),
        in_specs=[qs, qs, kp, kc, vp, vc, pl.BlockSpec(memory_space=pltpu.SMEM)],
        out_specs=[qs, full, full, sink_out],
        out_shape=[jax.ShapeDtypeStruct((NB, S, Q_WIDTH_A), F32), jax.ShapeDtypeStruct((NB, S, LANES), F32),
                   jax.ShapeDtypeStruct((NB, S, LANES), F32), jax.ShapeDtypeStruct((NB, N_Q_A, LANES), F32)],
        scratch_shapes=[slots, slots],
        compiler_params=_params(("parallel", "arbitrary")),
    )(qkn, do, qkn, qkn, qkn, qkn, sinks)


def _cumsum_mats():
    src = lax.broadcasted_iota(jnp.int32, (2 * BLOCK, 2 * BLOCK), 0) % BLOCK
    dst = lax.broadcasted_iota(jnp.int32, (2 * BLOCK, 2 * BLOCK), 1)
    ones = dst >= BLOCK
    rev = ((src > dst) | ones).astype(BF16)
    fwd = ((src < dst) | ones).astype(BF16)
    return rev, fwd


def _log_sigmoids(z):
    sp = jnp.log(1.0 + jnp.exp(-jnp.abs(z)))
    return jnp.minimum(z, 0.0) - sp, -(jnp.maximum(z, 0.0) + sp)


def _cumsum_mxu_many(vs, mat):
    parts = []
    for v in vs:
        hi = v.astype(BF16)
        parts.append(jnp.concatenate([hi, (v - hi.astype(F32)).astype(BF16)], axis=1))
    r = jnp.dot(jnp.concatenate(parts, axis=0), mat, preferred_element_type=F32)
    return [(r[n * BLOCK:(n + 1) * BLOCK, :BLOCK], r[n * BLOCK:(n + 1) * BLOCK, BLOCK:]) for n in range(len(vs))]


def _strict_mask():
    r = lax.broadcasted_iota(jnp.int32, (BLOCK, BLOCK), 0)
    c = lax.broadcasted_iota(jnp.int32, (BLOCK, BLOCK), 1)
    return c < r


def _tile(ref, j):
    return ref[pl.ds(pl.multiple_of(j * BLOCK, BLOCK), BLOCK), :]


SWEEP_EXIT = -88.0


def _head_halves(t, lo):
    zero = jnp.zeros_like(t)
    return jnp.where(lo, t, zero), jnp.where(lo, zero, t)


def _sb_specs(S, HD, width):
    n = HD // width
    blk = pl.BlockSpec((None, BLOCK, width), lambda b, p, i: (b, i, p))
    k_full = pl.BlockSpec((None, S, width), lambda b, p, i: (b, 0, n + p))
    v_full = pl.BlockSpec((None, S, width), lambda b, p, i: (b, 0, 2 * n + p))
    mat = pl.BlockSpec((2 * BLOCK, 2 * BLOCK), lambda b, p, i: (0, 0))
    return blk, k_full, v_full, mat


SB_FWD_PAIRS = 4
SB_BWD_PAIRS = 2
SB_BWD_TILES = 2
SB_BWD_VMEM_LIMIT_BYTES = 58 * 1024 * 1024


def _attn_b_fwd(qkv, rev):
    NB, S, W = qkv.shape
    HD = W // 3
    width = SB_FWD_PAIRS * LANES
    n_heads = 2 * SB_FWD_PAIRS
    blk, k_full, v_full, mat = _sb_specs(S, HD, width)

    def body(q_ref, k_ref, v_ref, rev_ref, o_ref):
        i = pl.program_id(2)
        rv = rev_ref[...]
        mask = _strict_mask()
        lo = lax.broadcasted_iota(jnp.int32, (BLOCK, LANES), 1) < HEAD_DIM
        q_all = q_ref[...]
        q_stack = [jnp.concatenate(_head_halves(q_all[:, p * LANES:(p + 1) * LANES] * ATTN_SCALE, lo), axis=0)
                   for p in range(SB_FWD_PAIRS)]

        def pair_tiles(ref, j):
            t = _tile(ref, j)
            return [t[:, p * LANES:(p + 1) * LANES] for p in range(SB_FWD_PAIRS)]

        def tiles_pass(js, carries, diagonal_first, last_counts=None):
            zs, v_tiles = [], []
            for j in js:
                ks = pair_tiles(k_ref, j)
                v_tiles.append(pair_tiles(v_ref, j))
                for p in range(SB_FWD_PAIRS):
                    z2 = lax.dot_general(q_stack[p], ks[p], _NT, preferred_element_type=F32)
                    zs += [z2[:BLOCK], z2[BLOCK:]]
            logs = [_log_sigmoids(z) for z in zs]
            masked = [jnp.where(mask, lm, 0.0) if diagonal_first and n < n_heads else lm
                      for n, (_, lm) in enumerate(logs)]
            cums = _cumsum_mxu_many(masked, rv)
            probs, new_c = {}, []
            for h in range(n_heads):
                carry = None if diagonal_first else carries[h]
                for t in range(len(js)):
                    n = t * n_heads + h
                    after, rs = cums[n]
                    if diagonal_first and t == 0:
                        a = jnp.where(mask, jnp.exp(logs[n][0] + after), 0.0)
                        carry = rs
                    else:
                        a = jnp.exp(logs[n][0] + after + carry)
                        carry = carry + rs
                    if last_counts is not None and t == len(js) - 1:
                        a = jnp.where(last_counts, a, 0.0)
                    probs[t, h] = a.astype(BF16)
                new_c.append(carry)
            outs = []
            for p in range(SB_FWD_PAIRS):
                total = None
                for t in range(len(js)):
                    pv = jnp.dot(jnp.concatenate([probs[t, 2 * p], probs[t, 2 * p + 1]], axis=0), v_tiles[t][p],
                                 preferred_element_type=F32)
                    part = jnp.where(lo, pv[:BLOCK], pv[BLOCK:])
                    total = part if total is None else total + part
                outs.append(total)
            return new_c, outs

        carries, accs = tiles_pass([i, jnp.maximum(i - 1, 0)], None, True, last_counts=i > 0)

        def live(cs):
            top = cs[0]
            for c in cs[1:]:
                top = jnp.maximum(top, c)
            return jnp.max(top) > SWEEP_EXIT

        def cond(st):
            return (st[0] < i - 1) & st[1]

        def step(st):
            jj, _, cs, accs = st
            new_c, outs = tiles_pass([i - 2 - jj], cs, False)
            return jj + 1, live(new_c), new_c, [acc + o for acc, o in zip(accs, outs)]

        st = lax.while_loop(cond, step, (jnp.int32(0), live(carries), carries, accs))
        for p in range(SB_FWD_PAIRS):
            o_ref[:, p * LANES:(p + 1) * LANES] = st[3][p].astype(BF16)

    return pl.pallas_call(
        body, name="attn_b_fwd", grid=(NB, HD // width, S // BLOCK),
        in_specs=[blk, k_full, v_full, mat], out_specs=blk,
        out_shape=jax.ShapeDtypeStruct((NB, S, HD), BF16),
        compiler_params=_params(("parallel", "parallel", "arbitrary")),
    )(qkv, qkv, qkv, rev)


def _attn_b_bwd(qkv, do, rev, fwd):
    NB, S, W = qkv.shape
    HD = W // 3
    width = SB_BWD_PAIRS * LANES
    n_heads = 2 * SB_BWD_PAIRS
    nj = S // BLOCK
    blk, k_full, v_full, mat = _sb_specs(S, HD, width)
    acc_full = pl.BlockSpec((None, S, width), lambda b, p, i: (b, 0, p))

    def body(q_ref, do_ref, k_ref, v_ref, rev_ref, fwd_ref, dq_ref, dk_ref, dv_ref, sig_s, a_s, e_s):
        i = pl.program_id(2)

        @pl.when(i == 0)
        def _():
            dk_ref[...] = jnp.zeros_like(dk_ref)
            dv_ref[...] = jnp.zeros_like(dv_ref)

        rv, fw = rev_ref[...], fwd_ref[...]
        mask = _strict_mask()
        lo = lax.broadcasted_iota(jnp.int32, (BLOCK, LANES), 1) < HEAD_DIM
        pairs = range(SB_BWD_PAIRS)

        def cols(p):
            return slice(p * LANES, (p + 1) * LANES)

        q_stack = [jnp.concatenate(_head_halves(q_ref[:, cols(p)], lo), axis=0) for p in pairs]
        qs_stack = [q * ATTN_SCALE for q in q_stack]
        do_stack = [jnp.concatenate(_head_halves(do_ref[:, cols(p)], lo), axis=0) for p in pairs]

        def sweep1_tiles(js, carries, diagonal_first):
            zs, das = [], []
            for j in js:
                kj, vj = _tile(k_ref, j), _tile(v_ref, j)
                for p in pairs:
                    z2 = lax.dot_general(qs_stack[p], kj[:, cols(p)], _NT, preferred_element_type=F32)
                    da2 = lax.dot_general(do_stack[p], vj[:, cols(p)], _NT, preferred_element_type=F32)
                    zs += [z2[:BLOCK], z2[BLOCK:]]
                    das += [da2[:BLOCK], da2[BLOCK:]]
            logs = [_log_sigmoids(z) for z in zs]
            cums = _cumsum_mxu_many([jnp.where(mask, lm, 0.0) if diagonal_first and n < n_heads else lm
                                     for n, (_, lm) in enumerate(logs)], rv)
            new_c, stores = [], []
            for h in range(n_heads):
                carry = None if diagonal_first else carries[h]
                for t, j in enumerate(js):
                    n = t * n_heads + h
                    lb, (after, rs) = logs[n][0], cums[n]
                    if diagonal_first and t == 0:
                        a = jnp.where(mask, jnp.exp(lb + after), 0.0)
                        carry = rs
                    else:
                        a = jnp.exp(lb + after + carry)
                        carry = carry + rs
                    stores.append((t, h, j, jnp.exp(lb), a.astype(BF16), das[n] * a))
                new_c.append(carry)
            for t, h, j, sg, ab, e in sorted(stores, key=lambda s: -s[0]):
                sig_s[h, j] = sg
                a_s[h, j] = ab
                e_s[h, j] = e
            return new_c

        carries = sweep1_tiles([jnp.maximum(i - t, 0) for t in range(SB_BWD_TILES + 1)], None, True)
        done = SB_BWD_TILES

        def live(cs):
            top = cs[0]
            for c in cs[1:]:
                top = jnp.maximum(top, c)
            return jnp.max(top) > SWEEP_EXIT

        def cond(st):
            return (done + SB_BWD_TILES * st[0] < i) & st[1]

        def sweep1(st):
            first = i - 1 - done - SB_BWD_TILES * st[0]
            new_c = sweep1_tiles([jnp.maximum(first - t, 0) for t in range(SB_BWD_TILES)], st[2], False)
            return st[0] + 1, live(new_c), new_c

        trips = lax.while_loop(cond, sweep1, (jnp.int32(0), live(carries), carries))[0]
        lowest = jnp.maximum(i - done - SB_BWD_TILES * trips, 0)

        def grads(js, st, diagonal_last=False, counts=None):
            prefixes, dqs = st
            counts = counts or [None] * len(js)
            es = [e_s[h, j] for j in js for h in range(n_heads)]
            cums = _cumsum_mxu_many(es, fw)
            dzs, new_p = [], []
            for h in range(n_heads):
                prefix = prefixes[h]
                for t, j in enumerate(js):
                    n = t * n_heads + h
                    sg = sig_s[h, j]
                    e_before, rs = cums[n]
                    dz = (es[n] * (1.0 - sg) - (e_before + prefix) * sg) * ATTN_SCALE
                    if diagonal_last and t == len(js) - 1:
                        dz = jnp.where(mask, dz, 0.0)
                    if counts[t] is not None:
                        dz = jnp.where(counts[t], dz, 0.0)
                        rs = jnp.where(counts[t], rs, 0.0)
                    dzs.append((t, h, dz.astype(BF16)))
                    prefix = prefix + rs
                new_p.append(prefix)
            dz_of = {(t, h): dz for t, h, dz in dzs}
            new_dq = list(dqs)
            for t, j in enumerate(js):
                kj = _tile(k_ref, j)
                rows = pl.ds(pl.multiple_of(j * BLOCK, BLOCK), BLOCK)
                for p in pairs:
                    dz_stack = jnp.concatenate([dz_of[t, 2 * p], dz_of[t, 2 * p + 1]], axis=0)
                    a_stack = jnp.concatenate([a_s[2 * p, j], a_s[2 * p + 1, j]], axis=0)
                    if counts[t] is not None:
                        a_stack = jnp.where(counts[t], a_stack, jnp.zeros_like(a_stack))
                    dq2 = jnp.dot(dz_stack, kj[:, cols(p)], preferred_element_type=F32)
                    new_dq[p] = new_dq[p] + jnp.where(lo, dq2[:BLOCK], dq2[BLOCK:])
                    dk_ref[rows, cols(p)] += lax.dot_general(dz_stack, q_stack[p], _TN, preferred_element_type=F32)
                    dv_ref[rows, cols(p)] += lax.dot_general(a_stack, do_stack[p], _TN, preferred_element_type=F32)
            return new_p, new_dq

        zeros = jnp.zeros((BLOCK, BLOCK), F32)
        st = ([zeros] * n_heads, [zeros] * SB_BWD_PAIRS)
        count = jnp.maximum(i - done, 0) - lowest
        st = lax.fori_loop(0, count % SB_BWD_TILES, lambda t, st: grads([lowest + t], st), st)
        start = lowest + count % SB_BWD_TILES
        st = lax.fori_loop(0, count // SB_BWD_TILES,
                           lambda t, st: grads([start + SB_BWD_TILES * t + u for u in range(SB_BWD_TILES)], st), st)
        top = [jnp.maximum(i - t, 0) for t in range(SB_BWD_TILES, -1, -1)]
        dqs = grads(top, st, diagonal_last=True, counts=[i >= t for t in range(SB_BWD_TILES, 0, -1)] + [None])[1]
        for p in pairs:
            dq_ref[:, cols(p)] = dqs[p]

    f32_stash = pltpu.VMEM((n_heads, nj, BLOCK, BLOCK), F32)
    bf16_stash = pltpu.VMEM((n_heads, nj, BLOCK, BLOCK), BF16)
    return pl.pallas_call(
        body, name="attn_b_bwd", grid=(NB, HD // width, nj),
        in_specs=[blk, blk, k_full, v_full, mat, mat], out_specs=[blk, acc_full, acc_full],
        out_shape=[jax.ShapeDtypeStruct((NB, S, HD), F32)] * 3,
        scratch_shapes=[f32_stash, bf16_stash, f32_stash],
        compiler_params=_params(("parallel", "parallel", "arbitrary"), SB_BWD_VMEM_LIMIT_BYTES),
    )(qkv, do, qkv, qkv, rev, fwd)


def _ada_fwd(c_all, w, b):
    L, D, N = w.shape
    B = c_all.shape[0]

    def body(c_ref, w_ref, b_ref, o_ref):
        cv = c_ref[...]
        cond = (cv * _sigmoid(cv)).astype(BF16)
        o_ref[...] = jnp.dot(cond, w_ref[...].astype(BF16), preferred_element_type=F32) + b_ref[...]

    return pl.pallas_call(
        body, name="ada_fwd", grid=(L,),
        in_specs=[pl.BlockSpec((B, D), lambda l: (0, 0)), pl.BlockSpec((None, D, N), lambda l: (l, 0, 0)),
                  pl.BlockSpec((None, 1, N), lambda l: (l, 0, 0))],
        out_specs=pl.BlockSpec((None, B, N), lambda l: (l, 0, 0)),
        out_shape=jax.ShapeDtypeStruct((L, B, N), F32),
        compiler_params=_params(("parallel",)),
    )(c_all, w, b)


def _ada_bwd(c_all, dmod_all, dmod_shard):
    L, B, N = dmod_shard.shape
    D = c_all.shape[1]
    N_all = dmod_all.shape[2]

    def body(c_ref, da_ref, ds_ref, gw_ref, gb_ref):
        cv = c_ref[...]
        cond = (cv * _sigmoid(cv)).astype(BF16)
        gw_ref[...] = lax.dot_general(cond, ds_ref[...].astype(BF16), _TN, preferred_element_type=F32)
        gb_ref[...] = jnp.sum(da_ref[...], axis=0, keepdims=True)

    return pl.pallas_call(
        body, name="ada_bwd", grid=(L,),
        in_specs=[pl.BlockSpec((B, D), lambda l: (0, 0)), pl.BlockSpec((None, B, N_all), lambda l: (l, 0, 0)),
                  pl.BlockSpec((None, B, N), lambda l: (l, 0, 0))],
        out_specs=[pl.BlockSpec((None, D, N), lambda l: (l, 0, 0)), pl.BlockSpec((None, 1, N_all), lambda l: (l, 0, 0))],
        out_shape=[jax.ShapeDtypeStruct((L, D, N), F32), jax.ShapeDtypeStruct((L, 1, N_all), F32)],
        compiler_params=_params(("parallel",)),
    )(c_all, dmod_all, dmod_shard)


def _adamw(w, g, m, v, name):
    shape = w.shape
    if w.ndim == 2:
        w, g, m, v = [t.reshape((1,) + shape) for t in (w, g, m, v)]
    L, R, C = w.shape
    tr = _pick(R, max(8, (1 << 18) // C), 8)
    c1 = 1.0 - ADAM_B1 ** ADAM_STEP
    c2 = 1.0 - ADAM_B2 ** ADAM_STEP

    def body(w_ref, g_ref, m_ref, v_ref, d_ref, nm_ref, nv_ref):
        gv = g_ref[...]
        nm = ADAM_B1 * m_ref[...] + (1.0 - ADAM_B1) * gv
        nv = ADAM_B2 * v_ref[...] + (1.0 - ADAM_B2) * (gv * gv)
        d_ref[...] = -ADAM_LR * ((nm / c1) / (jnp.sqrt(nv / c2) + ADAM_EPS) + ADAM_WD * w_ref[...])
        nm_ref[...] = nm
        nv_ref[...] = nv

    spec = pl.BlockSpec((None, tr, C), lambda l, r: (l, r, 0))
    out = pl.pallas_call(
        body, name=name, grid=(L, R // tr), in_specs=[spec] * 4, out_specs=[spec] * 3,
        out_shape=[jax.ShapeDtypeStruct((L, R, C), F32)] * 3,
        compiler_params=_params(("parallel", "parallel")),
    )(w, g, m, v)
    return [t.reshape(shape) for t in out]


_SHARDED = (("wqkv_a", 2), ("wo_a", 1), ("wqkv_b", 2), ("wo_b", 1), ("w_gate", 2), ("w_up", 2), ("w_down", 1))


def _pack_full(layers, axis, gate_up=None):
    L = len(layers)
    R, C = layers[0].shape

    def shards(m):
        if gate_up is not None:
            F = C // 2
            tf, Cs = _ff_tile(F), F // 4
            assert tf % Cs == 0
            starts = [(2 * (s * Cs // tf) + gate_up) * tf + s * Cs % tf for s in range(4)]
            return jnp.stack([m[:, st:st + Cs] for st in starts])
        if axis == 2:
            return m.reshape(R, 4, C // 4).transpose(1, 0, 2)
        return m.reshape(4, R // 4, C)

    halves = [jnp.stack([shards(m) for m in layers[h * (L // 2):(h + 1) * (L // 2)]], axis=1) for h in range(2)]
    return jnp.stack(halves)


def _unpack_full(gathered, axis):
    _, Lh, Rs, Cs = gathered.shape
    t = gathered.reshape(4, 2, Lh, Rs, Cs)
    layers = []
    for h in range(2):
        for l in range(Lh):
            piece = t[:, h, l]
            if axis == 2:
                layers.append(piece.transpose(1, 0, 2).reshape(Rs, 4 * Cs))
            else:
                layers.append(piece.reshape(4 * Rs, Cs))
    return layers


def _sum_slabs(own, recv, name, with_bf16=False):
    C = own.shape[-1]
    out = _sum_leading(recv.reshape(recv.shape[0], -1, C), name, own=own.reshape(-1, C), with_bf16=with_bf16)
    if with_bf16:
        return out[0].reshape(own.shape), out[1].reshape(own.shape)
    return out.reshape(own.shape)


def _gather8(x, name):
    return _all_gather8([x], name)[0]


def _rope_tables(positions):
    half = ROT_DIM // 2
    inv_freq = jnp.power(jnp.float32(ROPE_THETA), -jnp.arange(half, dtype=F32) * 2.0 / ROT_DIM)
    ang = positions.astype(F32).reshape(-1, 1) * inv_freq
    cos, sin = jnp.cos(ang), jnp.sin(ang)
    T = ang.shape[0]
    rest = HEAD_DIM - ROT_DIM
    c64 = jnp.concatenate([cos, cos, jnp.ones((T, rest), F32)], axis=1)
    s64 = jnp.concatenate([-sin, sin, jnp.zeros((T, rest), F32)], axis=1)
    return jnp.tile(c64, (1, 2)), jnp.tile(s64, (1, 2))


def _gain_rows(q_gain, k_gain):
    q2 = jnp.tile(q_gain.reshape(1, HEAD_DIM), (GROUP_A, 2))
    k2 = jnp.tile(k_gain.reshape(1, HEAD_DIM), (1, 2))
    return jnp.concatenate([q2, k2, jnp.ones((1, LANES), F32)], axis=0)


def _local_step(x, positions, mod, norm1_g, norm2_g, q_norm_a, k_norm_a, sinks_a,
                wqkv_a, wo_a, wqkv_b, wo_b, wgu, wd, loss_target):
    NB, S, D = x.shape
    T = NB * S
    QA = N_Q_A * HEAD_DIM
    tab_c, tab_s = _rope_tables(positions)
    rev, fwd = _cumsum_mats()

    saved = []
    xc = x
    mods = [[mod[i][:, k * D:(k + 1) * D].reshape(NB, 1, D) for k in range(6)] for i in range(DEPTH)]
    h = _norm_mod_fwd(xc, norm1_g[0:1], mods[0][1], mods[0][0])
    for i in range(DEPTH):
        j = i // 2
        sh1, sc1, g1, sh2, sc2, g2 = mods[i]
        st = dict(x=xc, sc1=sc1, g1=g1, sc2=sc2, g2=g2)
        st["h"] = h.reshape(T, D)
        if i % 2 == 0:
            st["qkv"] = _matmul(st["h"], wqkv_a[j], "nn", F32, "qkv_a")
            st["gains"] = _gain_rows(q_norm_a[j], k_norm_a[j])
            st["qkn"] = _qk_prep_fwd(st["qkv"], tab_c, tab_s, st["gains"]).reshape(NB, S, -1)
            st["o"] = _attn_a_fwd(st["qkn"], sinks_a[j]).reshape(T, QA)
            y = _matmul(st["o"], wo_a[j], "nn", F32, "wo_a")
        else:
            st["qkv"] = _matmul(st["h"], wqkv_b[j], "nn", BF16, "qkv_b").reshape(NB, S, -1)
            st["o"] = _attn_b_fwd(st["qkv"], rev).reshape(T, N_H_B * HEAD_DIM)
            y = _matmul(st["o"], wo_b[j], "nn", F32, "wo_b")
        st["y"] = y.reshape(NB, S, D)
        x1, h2 = _gate_res(xc, st["y"], g1, norm=(norm2_g[i:i + 1], sc2, sh2))
        st["x1"] = x1
        st["h2"] = h2.reshape(T, D)
        st["gu"], st["act"] = _matmul(st["h2"], wgu[i], "nn", BF16, "gate_up", swiglu=True)
        st["m"] = _matmul(st["act"], wd[i], "nn", F32, "down").reshape(NB, S, D)
        if i + 1 < DEPTH:
            xc, h = _gate_res(x1, st["m"], g2, norm=(norm1_g[i + 1:i + 2], mods[i + 1][1], mods[i + 1][0]))
        else:
            xc = _gate_res(x1, st["m"], g2)
        saved.append(st)

    loss, dx = _loss_fwd_bwd(xc, loss_target)

    grads = {name: [None] * n for name, n in
             (("wqkv_a", 2), ("wo_a", 2), ("wqkv_b", 2), ("wo_b", 2), ("wgu", DEPTH), ("wd", DEPTH),
              ("norm1_g", DEPTH), ("norm2_g", DEPTH), ("q_norm_a", 2), ("k_norm_a", 2), ("sinks_a", 2))}
    dmod = [None] * DEPTH
    dm, dg2 = _gate_res_bwd(dx, saved[-1]["m"], saved[-1]["g2"])
    for i in reversed(range(DEPTH)):
        j = i // 2
        st = saved[i]
        dm = dm.reshape(T, D)
        grads["wd"][i] = _matmul(st["act"], dm, "tn", F32, "d_wd")
        dgu = _swiglu_bwd(dm, wd[i], st["gu"])
        grads["wgu"][i] = _matmul(st["h2"], dgu, "tn", F32, "d_wgu")
        dx1, dsh2, dsc2, grads["norm2_g"][i], dy, dg1 = _norm_mod_bwd(
            dgu, wgu[i], st["x1"], norm2_g[i:i + 1], st["sc2"], dx, "d_h2", y=st["y"], g=st["g1"])
        dy = dy.reshape(T, D)
        if i % 2 == 0:
            do = _matmul(dy, wo_a[j], "nt", BF16, "d_o_a").reshape(NB, S, QA)
            grads["wo_a"][j] = _matmul(st["o"], dy, "tn", F32, "d_wo_a")
            dq, dk, dv, dsink = _attn_a_bwd(st["qkn"], do, sinks_a[j])
            dqkv, dgain = _qk_prep_bwd(st["qkv"], dq.reshape(T, QA), dk.reshape(T, LANES), dv.reshape(T, LANES),
                                       tab_c, tab_s, st["gains"])
            w_in = wqkv_a[j]
            grads["wqkv_a"][j] = _matmul(st["h"], dqkv, "tn", F32, "d_wqkv_a")
            grads["q_norm_a"][j] = jnp.sum(dgain[:GROUP_A].reshape(2 * GROUP_A, HEAD_DIM), axis=0)
            grads["k_norm_a"][j] = jnp.sum(dgain[GROUP_A].reshape(2, HEAD_DIM), axis=0)
            grads["sinks_a"][j] = jnp.sum(dsink[..., 0], axis=0)
        else:
            do = _matmul(dy, wo_b[j], "nt", BF16, "d_o_b").reshape(NB, S, -1)
            grads["wo_b"][j] = _matmul(st["o"], dy, "tn", F32, "d_wo_b")
            dq, dk, dv = _attn_b_bwd(st["qkv"], do, rev, fwd)
            dqkv = jnp.concatenate([dq, dk, dv], axis=-1).reshape(T, -1).astype(BF16)
            w_in = wqkv_b[j]
            grads["wqkv_b"][j] = _matmul(st["h"], dqkv, "tn", F32, "d_wqkv_b")
        this_dg2 = dg2
        if i > 0:
            dx, dsh1, dsc1, grads["norm1_g"][i], dm, dg2 = _norm_mod_bwd(
                dqkv, w_in, st["x"], norm1_g[i:i + 1], st["sc1"], dx1, "d_h", y=saved[i - 1]["m"], g=saved[i - 1]["g2"])
        else:
            dx, dsh1, dsc1, grads["norm1_g"][i] = _norm_mod_bwd(
                dqkv, w_in, st["x"], norm1_g[i:i + 1], st["sc1"], dx1, "d_h")
        dmod[i] = jnp.concatenate([dsh1, dsc1, dg1, dsh2, dsc2, this_dg2], axis=-1).reshape(NB, 6 * D)

    matrices = ("wqkv_a", "wo_a", "wqkv_b", "wo_b", "wgu", "wd")
    grads = {name: parts if name in matrices else jnp.stack(parts) for name, parts in grads.items()}
    return loss, dx, grads, jnp.stack(dmod)


def _rows_of(flat, cols=PACK_COLS):
    n = flat.shape[0]
    pad = (-n) % (8 * cols)
    if pad:
        flat = jnp.concatenate([flat, jnp.zeros((pad,), flat.dtype)])
    return flat.reshape(-1, cols)


def kernel(x, c, positions, ada_w, ada_b, norm1_g, norm2_g, wqkv_a, q_norm_a, k_norm_a, sinks_a, wo_a, wqkv_b, wo_b, w_gate, w_up, w_down, loss_target, m_ada_w, m_ada_b, m_norm1_g, m_norm2_g, m_wqkv_a, m_q_norm_a, m_k_norm_a, m_sinks_a, m_wo_a, m_wqkv_b, m_wo_b, m_w_gate, m_w_up, m_w_down, v_ada_w, v_ada_b, v_norm1_g, v_norm2_g, v_wqkv_a, v_q_norm_a, v_k_norm_a, v_sinks_a, v_wo_a, v_wqkv_b, v_wo_b, v_w_gate, v_w_up, v_w_down):
    xi, yi, ci = lax.axis_index("x"), lax.axis_index("y"), lax.axis_index("c")
    dev = 4 * xi + 2 * yi + ci
    chip = 2 * xi + yi
    NB, S, D = x.shape
    B_all = N_DEV * NB
    L = ada_w.shape[0]
    n_mod = ada_w.shape[2] // 2

    c_all = _gather8(_rows_of(c.reshape(-1), LANES), "gather_c").reshape(N_DEV, -1)[:, :NB * D].reshape(B_all, D)
    ada_w_half = lax.dynamic_slice_in_dim(ada_w, ci * n_mod, n_mod, axis=2)
    ada_b_half = lax.dynamic_slice_in_dim(ada_b, dev * n_mod, n_mod, axis=1).reshape(L, 1, n_mod)
    mod_part = _ada_fwd(c_all, ada_w_half, ada_b_half)
    n_part = L * B_all * n_mod
    mod_all = _gather8(_rows_of(mod_part.reshape(-1)), "gather_mod").reshape(N_DEV, -1)[:, :n_part]
    mod_all = mod_all.reshape(N_DEV, L, B_all, n_mod).transpose(1, 2, 0, 3).reshape(L, B_all, N_DEV * n_mod)
    mod = lax.dynamic_slice_in_dim(mod_all, dev * NB, NB, axis=1)

    shards = dict(wqkv_a=wqkv_a, wo_a=wo_a, wqkv_b=wqkv_b, wo_b=wo_b, w_gate=w_gate, w_up=w_up, w_down=w_down)
    halves = []
    for name, _ in _SHARDED:
        w = shards[name]
        half = lax.dynamic_index_in_dim(w.reshape((2, w.shape[0] // 2) + w.shape[1:]), ci, 0, keepdims=False)
        halves.append(half.astype(BF16))
    gathered = _all_gather8(halves, "gather_weights", local_axis=1, local_chunks=8, relay_axis=1)
    full = {name: _unpack_full(t, axis) for (name, axis), t in zip(_SHARDED, gathered)}
    wgu = [_interleave(gate, up) for gate, up in zip(full["w_gate"], full["w_up"])]

    loss, grad_x, g, dmod = _local_step(
        x, positions, mod, norm1_g, norm2_g, q_norm_a, k_norm_a, sinks_a,
        full["wqkv_a"], full["wo_a"], full["wqkv_b"], full["wo_b"], wgu, full["w_down"], loss_target)

    g_full = dict(wqkv_a=g["wqkv_a"], wo_a=g["wo_a"], wqkv_b=g["wqkv_b"], wo_b=g["wo_b"],
                  w_gate=g["wgu"], w_up=g["wgu"], w_down=g["wd"])
    which = dict(w_gate=0, w_up=1)
    packed = [_pack_full(g_full[name], axis, which.get(name)) for name, axis in _SHARDED]
    def own(t, index):
        return lax.dynamic_index_in_dim(t, index, 0, keepdims=False)

    from_cores = _exchange_cores(packed, "rs_cores", chunk_axis=0, chunks=4)
    chip_part = [_sum_slabs(own(p, ci), r, "rs_add_cores", with_bf16=True) for p, r in zip(packed, from_cores)]
    from_chips = _exchange_chips([b for _, b in chip_part], "rs_chips", relay_axis=1)
    mine = [_sum_slabs(own(p, chip), r, "rs_add_chips") for (p, _), r in zip(chip_part, from_chips)]
    both = _sibling_gather(mine, "rs_halves")
    grad = {name: t.reshape(shards[name].shape) for (name, _), t in zip(_SHARDED, both)}

    small_names = ("norm1_g", "norm2_g", "q_norm_a", "k_norm_a", "sinks_a")
    small = [dmod.reshape(-1)] + [g[name].reshape(-1) for name in small_names] + [loss.reshape(-1)]
    small_sizes = [t.shape[0] for t in small]
    small_rows = _rows_of(jnp.concatenate(small))
    small_all = _gather8(small_rows, "gather_small")
    small_sum = _sum_leading(small_all, "sum_small").reshape(-1)
    n_dmod = small_sizes[0]
    dmod_all = small_all.reshape(N_DEV, -1)[:, :n_dmod].reshape(N_DEV, L, NB, 6 * D)
    dmod_all = dmod_all.transpose(1, 0, 2, 3).reshape(L, B_all, 6 * D)
    off = n_dmod
    for name, sz in zip(small_names + ("loss",), small_sizes[1:]):
        grad[name] = small_sum[off:off + sz]
        off += sz
    loss_total = grad.pop("loss").reshape(())
    for name, ref in (("norm1_g", norm1_g), ("norm2_g", norm2_g), ("q_norm_a", q_norm_a),
                      ("k_norm_a", k_norm_a), ("sinks_a", sinks_a)):
        grad[name] = grad[name].reshape(ref.shape)

    n_shard = ada_w.shape[2]
    dmod_shard = lax.dynamic_slice_in_dim(dmod_all, chip * n_shard, n_shard, axis=2)
    grad["ada_w"], gb = _ada_bwd(c_all, dmod_all, dmod_shard)
    grad["ada_b"] = gb.reshape(ada_b.shape)

    weights = dict(ada_w=ada_w, ada_b=ada_b, norm1_g=norm1_g, norm2_g=norm2_g, wqkv_a=wqkv_a, q_norm_a=q_norm_a,
                   k_norm_a=k_norm_a, sinks_a=sinks_a, wo_a=wo_a, wqkv_b=wqkv_b, wo_b=wo_b, w_gate=w_gate,
                   w_up=w_up, w_down=w_down)
    m_in = dict(ada_w=m_ada_w, ada_b=m_ada_b, norm1_g=m_norm1_g, norm2_g=m_norm2_g, wqkv_a=m_wqkv_a,
                q_norm_a=m_q_norm_a, k_norm_a=m_k_norm_a, sinks_a=m_sinks_a, wo_a=m_wo_a, wqkv_b=m_wqkv_b,
                wo_b=m_wo_b, w_gate=m_w_gate, w_up=m_w_up, w_down=m_w_down)
    v_in = dict(ada_w=v_ada_w, ada_b=v_ada_b, norm1_g=v_norm1_g, norm2_g=v_norm2_g, wqkv_a=v_wqkv_a,
                q_norm_a=v_q_norm_a, k_norm_a=v_k_norm_a, sinks_a=v_sinks_a, wo_a=v_wo_a, wqkv_b=v_wqkv_b,
                wo_b=v_wo_b, w_gate=v_w_gate, w_up=v_w_up, w_down=v_w_down)
    names = list(weights)
    delta, new_m, new_v = {}, {}, {}
    for name in names:
        delta[name], new_m[name], new_v[name] = _adamw(weights[name], grad[name], m_in[name], v_in[name],
                                                       "adamw_" + name)
    return (loss_total, grad_x, *[grad[k] for k in names], *[delta[k] for k in names],
            *[new_m[k] for k in names], *[new_v[k] for k in names])
```

```python
import jax
import jax.numpy as jnp
from jax import lax
from jax.experimental import pallas as pl
from jax.experimental.pallas import tpu as pltpu

F32 = jnp.float32
BF16 = jnp.bfloat16

DEPTH = 4
HEAD_DIM = 64
N_Q_A = 16
N_KV_A = 2
GROUP_A = N_Q_A // N_KV_A
N_H_B = 16
BLOCK = 128
ROT_DIM = HEAD_DIM // 4
ROPE_THETA = 500000.0
EPS = 1e-6
ATTN_SCALE = HEAD_DIM ** -0.5
NEG_BIG = -1e30

ADAM_LR = 0.001
ADAM_B1 = 0.9
ADAM_B2 = 0.999
ADAM_EPS = 1e-08
ADAM_WD = 0.01
ADAM_STEP = 10

N_DEV = 8
LANES = 128
PACK_COLS = 1024
VMEM_LIMIT_BYTES = 48 * 1024 * 1024
MESH = pl.DeviceIdType.MESH

_NT = (((1,), (1,)), ((), ()))
_TN = (((0,), (0,)), ((), ()))
_NN = (((1,), (0,)), ((), ()))


def _params(sem=None, vmem_limit_bytes=VMEM_LIMIT_BYTES):
    return pltpu.CompilerParams(vmem_limit_bytes=vmem_limit_bytes, dimension_semantics=sem)


def _pick(n, cap, mult):
    best = None
    for t in range(mult, min(n, cap) + 1, mult):
        if n % t == 0:
            best = t
    return n if best is None else best


_ANY = pl.BlockSpec(memory_space=pl.ANY)


def _window(index, axis, q, n, shape):
    rest = [slice(None)] * len(shape)
    size = shape[axis] // n
    rest[axis] = pl.ds(q * size, size)
    return tuple(index) + tuple(rest)


def _all_gather8(xs, name, local_axis=0, local_chunks=1, relay_axis=None):
    n = len(xs)
    n_sems = 7 if relay_axis is None else 9

    def body(*refs):
        x_refs, out_refs = refs[:n], refs[n:2 * n]
        send_sems, recv_sems, local_sems = refs[2 * n:]
        xi, yi, ci = lax.axis_index("x"), lax.axis_index("y"), lax.axis_index("c")
        me, sibling = (xi, yi, ci), (xi, yi, 1 - ci)
        chips = [(1 - xi, yi), (xi, 1 - yi), (1 - xi, 1 - yi)]

        def slab(w, px, py, pc):
            return out_refs[w].at[4 * px + 2 * py + pc]

        def copy(w, k, block, to, src=None):
            return pltpu.make_async_remote_copy(
                src_ref=slab(w, *block) if src is None else src, dst_ref=slab(w, *block),
                send_sem=send_sems.at[k, w], recv_sem=recv_sems.at[k, w], device_id=to, device_id_type=MESH)

        mine = []
        for w in range(n):
            for q in range(local_chunks):
                part = _window((), local_axis, q, local_chunks, xs[w].shape)
                mine.append(pltpu.make_async_copy(x_refs[w].at[part], slab(w, *me).at[part], local_sems.at[w, q]))
                mine[-1].start()
        direct = chips if relay_axis is None else chips[:2]
        first = [copy(w, 0, me, sibling, src=x_refs[w]) for w in range(n)]
        first += [copy(w, 1 + j, me, (*chip, ci), src=x_refs[w]) for j, chip in enumerate(direct) for w in range(n)]
        for cp in first:
            cp.start()

        def relay(w, part, block, to):
            piece = _window((), relay_axis, part, 2, xs[w].shape)
            return pltpu.make_async_remote_copy(
                src_ref=slab(w, *block).at[piece], dst_ref=slab(w, *block).at[piece],
                send_sem=send_sems.at[7 + part, w], recv_sem=recv_sems.at[7 + part, w],
                device_id=to, device_id_type=MESH)

        passed = []
        for j, chip in enumerate(direct):
            for w in range(n):
                copy(w, 1 + j, (*chip, ci), me).wait_recv()
                passed.append(copy(w, 4 + j, (*chip, ci), sibling))
                passed[-1].start()
                if relay_axis is not None:
                    passed.append(relay(w, j, (*chip, ci), (*chips[1 - j], ci)))
                    passed[-1].start()
        if relay_axis is not None:
            for w in range(n):
                for part in range(2):
                    relay(w, part, (*chips[2], ci), me).wait_recv()
                passed.append(copy(w, 6, (*chips[2], ci), sibling))
                passed[-1].start()
        for w in range(n):
            copy(w, 0, sibling, me).wait_recv()
        for j, chip in enumerate(chips):
            for w in range(n):
                copy(w, 4 + j, (*chip, 1 - ci), me).wait_recv()
        for cp in first + passed:
            cp.wait_send()
        for cp in mine:
            cp.wait()

    return pl.pallas_call(
        body, name=name,
        out_shape=[jax.ShapeDtypeStruct((N_DEV,) + x.shape, x.dtype) for x in xs],
        in_specs=[_ANY] * n, out_specs=[_ANY] * n,
        scratch_shapes=[pltpu.SemaphoreType.DMA((n_sems, n)), pltpu.SemaphoreType.DMA((n_sems, n)),
                        pltpu.SemaphoreType.DMA((n, local_chunks))],
    )(*xs)


def _exchange_cores(xs, name, chunk_axis=0, chunks=1):
    n = len(xs)
    n_peers = 1

    def body(*refs):
        x_refs, out_refs = refs[:n], refs[n:2 * n]
        send_sems, recv_sems = refs[2 * n:]
        xi, yi, ci = lax.axis_index("x"), lax.axis_index("y"), lax.axis_index("c")
        peers = [(1 - ci, (xi, yi, 1 - ci))]
        copies = []
        for k, (p, dev) in enumerate(peers):
            for w in range(n):
                slab_shape = xs[w].shape[1:]
                for q in range(chunks):
                    copies.append(pltpu.make_async_remote_copy(
                        src_ref=x_refs[w].at[_window((p,), chunk_axis, q, chunks, slab_shape)],
                        dst_ref=out_refs[w].at[_window((k,), chunk_axis, q, chunks, slab_shape)],
                        send_sem=send_sems.at[k, w, q], recv_sem=recv_sems.at[k, w, q],
                        device_id=dev, device_id_type=MESH))
                    copies[-1].start()
        for cp in copies:
            cp.wait()

    return pl.pallas_call(
        body, name=name,
        out_shape=[jax.ShapeDtypeStruct((n_peers,) + x.shape[1:], x.dtype) for x in xs],
        in_specs=[_ANY] * n, out_specs=[_ANY] * n,
        scratch_shapes=[pltpu.SemaphoreType.DMA((n_peers, n, chunks)), pltpu.SemaphoreType.DMA((n_peers, n, chunks))],
    )(*xs)


def _exchange_chips(xs, name, relay_axis):
    n = len(xs)

    def half_shape(x):
        shape = list(x.shape[1:])
        shape[relay_axis] //= 2
        return tuple(shape)

    def body(*refs):
        x_refs, out_refs, hop_refs = refs[:n], refs[n:2 * n], refs[2 * n:3 * n]
        send_sems, recv_sems = refs[3 * n:]
        xi, yi, ci = lax.axis_index("x"), lax.axis_index("y"), lax.axis_index("c")
        nbr = [(1 - xi, yi, ci), (xi, 1 - yi, ci)]
        slab_of_nbr = [2 * (1 - xi) + yi, 2 * xi + (1 - yi)]
        slab_of_diag = 2 * (1 - xi) + (1 - yi)

        def copy(k, w, src, dst, to):
            return pltpu.make_async_remote_copy(src_ref=src, dst_ref=dst, send_sem=send_sems.at[k, w],
                                                recv_sem=recv_sems.at[k, w], device_id=to, device_id_type=MESH)

        def piece(w, part):
            return _window((), relay_axis, part, 2, xs[w].shape[1:])

        sent = []
        for w in range(n):
            for j in range(2):
                sent.append(copy(j, w, x_refs[w].at[slab_of_nbr[j]], out_refs[w].at[j], nbr[j]))
                sent.append(copy(2 + j, w, x_refs[w].at[(slab_of_diag,) + piece(w, j)], hop_refs[w].at[j], nbr[j]))
        for cp in sent:
            cp.start()
        for w in range(n):
            for j in range(2):
                copy(2 + j, w, hop_refs[w].at[j], hop_refs[w].at[j], nbr[j]).wait_recv()
                sent.append(copy(4 + j, w, hop_refs[w].at[j], out_refs[w].at[(2,) + piece(w, j)], nbr[1 - j]))
                sent[-1].start()
        for w in range(n):
            for j in range(2):
                copy(j, w, out_refs[w].at[j], out_refs[w].at[j], nbr[j]).wait_recv()
                half = out_refs[w].at[(2,) + piece(w, j)]
                copy(4 + j, w, half, half, nbr[1 - j]).wait_recv()
        for cp in sent:
            cp.wait_send()

    out = pl.pallas_call(
        body, name=name,
        out_shape=[jax.ShapeDtypeStruct((3,) + x.shape[1:], x.dtype) for x in xs]
        + [jax.ShapeDtypeStruct((2,) + half_shape(x), x.dtype) for x in xs],
        in_specs=[_ANY] * n, out_specs=[_ANY] * (2 * n),
        scratch_shapes=[pltpu.SemaphoreType.DMA((6, n)), pltpu.SemaphoreType.DMA((6, n))],
    )(*xs)
    return out[:n]


def _sibling_send(xs, name, chunk_axis=1, chunks=4):
    n = len(xs)

    def body(*refs):
        x_refs, out_refs = refs[:n], refs[n:2 * n]
        send_sems, recv_sems = refs[2 * n:]
        xi, yi, ci = lax.axis_index("x"), lax.axis_index("y"), lax.axis_index("c")
        copies = []
        for w in range(n):
            for q in range(chunks):
                part = _window((), chunk_axis, q, chunks, xs[w].shape)
                copies.append(pltpu.make_async_remote_copy(
                    src_ref=x_refs[w].at[part], dst_ref=out_refs[w].at[part],
                    send_sem=send_sems.at[w, q], recv_sem=recv_sems.at[w, q],
                    device_id=(xi, yi, 1 - ci), device_id_type=MESH))
                copies[-1].start()
        for cp in copies:
            cp.wait()

    return pl.pallas_call(
        body, name=name,
        out_shape=[jax.ShapeDtypeStruct(x.shape, x.dtype) for x in xs],
        in_specs=[_ANY] * n, out_specs=[_ANY] * n,
        scratch_shapes=[pltpu.SemaphoreType.DMA((n, chunks)), pltpu.SemaphoreType.DMA((n, chunks))],
    )(*xs)


def _sum_leading(x, name, own=None, with_bf16=False):
    P, R, C = x.shape
    tr = _pick(R, max(16, (1 << 19) // (C * (P + 1))), 16)

    def body(*refs):
        n_in = 1 if own is None else 2
        x_ref = refs[n_in - 1]
        acc = x_ref[0].astype(F32) if own is None else refs[0][...] + x_ref[0].astype(F32)
        for p in range(1, P):
            acc = acc + x_ref[p].astype(F32)
        refs[n_in][...] = acc
        if with_bf16:
            refs[n_in + 1][...] = acc.astype(BF16)

    flat = pl.BlockSpec((tr, C), lambda r: (r, 0))
    slabs = pl.BlockSpec((P, tr, C), lambda r: (0, r, 0))
    out = pl.pallas_call(
        body, name=name, grid=(R // tr,),
        in_specs=[slabs] if own is None else [flat, slabs],
        out_specs=[flat, flat] if with_bf16 else [flat],
        out_shape=[jax.ShapeDtypeStruct((R, C), F32)] + ([jax.ShapeDtypeStruct((R, C), BF16)] if with_bf16 else []),
        compiler_params=_params(("arbitrary",)),
    )(*([x] if own is None else [own, x]))
    return out if with_bf16 else out[0]


MATMUL_SINGLE_K = 2816
MATMUL_VMEM_BUDGET = 36 * 1024 * 1024


def _matmul(a, b, mode, out_dtype, name, swiglu=False):
    if mode == "nn":
        (M, K), N = a.shape, b.shape[1]
    elif mode == "nt":
        (M, K), N = a.shape, b.shape[0]
    else:
        (K, M), N = a.shape, b.shape[1]
    tm = _pick(M, 1024 if mode != "tn" else 1536, 128)
    tn = _pick(N, 1536, 128)
    if swiglu:
        tm, tn = _pick(M, 1024 if out_dtype == BF16 else 512, 128), 2 * _ff_tile(N // 2)
    out_bytes = jnp.dtype(out_dtype).itemsize
    tk = K
    if K > MATMUL_SINGLE_K:
        for cap in (2048, 1024, 512):
            tk = _pick(K, cap, 128)
            blocks = 2 * 2 * tk * (tm + tn) + tm * tn * (2 * out_bytes + (4 if out_dtype != F32 else 0))
            if blocks <= MATMUL_VMEM_BUDGET:
                break
    nk = K // tk
    dims = {"nn": _NN, "nt": _NT, "tn": _TN}[mode]
    use_scratch = nk > 1 and out_dtype != F32

    def body(a_ref, b_ref, *refs):
        o_ref = refs[0]

        def product():
            return lax.dot_general(a_ref[...].astype(BF16), b_ref[...].astype(BF16), dims,
                                   preferred_element_type=F32)

        if nk == 1:
            part = product()
            o_ref[...] = part.astype(o_ref.dtype)
            if swiglu:
                g = part[:, :tn // 2]
                refs[1][...] = (g * _sigmoid(g) * part[:, tn // 2:]).astype(BF16)
            return
        k = pl.program_id(2)
        acc_ref = refs[-1] if use_scratch else o_ref

        @pl.when(k == 0)
        def _():
            acc_ref[...] = jnp.zeros_like(acc_ref)

        acc_ref[...] += product()

        if use_scratch:
            @pl.when(k == nk - 1)
            def _():
                o_ref[...] = acc_ref[...].astype(o_ref.dtype)

    if mode == "tn":
        a_spec = pl.BlockSpec((tk, tm), lambda i, j, k: (k, i))
    else:
        a_spec = pl.BlockSpec((tm, tk), lambda i, j, k: (i, k))
    if mode == "nt":
        b_spec = pl.BlockSpec((tn, tk), lambda i, j, k: (j, k))
    else:
        b_spec = pl.BlockSpec((tk, tn), lambda i, j, k: (k, j))
    out_specs = [pl.BlockSpec((tm, tn), lambda i, j, k: (i, j))]
    out_shape = [jax.ShapeDtypeStruct((M, N), out_dtype)]
    if swiglu:
        assert nk == 1 and mode == "nn"
        out_specs.append(pl.BlockSpec((tm, tn // 2), lambda i, j, k: (i, j)))
        out_shape.append(jax.ShapeDtypeStruct((M, N // 2), BF16))
    out = pl.pallas_call(
        body, name=name, grid=(M // tm, N // tn, nk),
        in_specs=[a_spec, b_spec], out_specs=out_specs, out_shape=out_shape,
        scratch_shapes=[pltpu.VMEM((tm, tn), F32)] if use_scratch else [],
        compiler_params=_params(("parallel", "parallel", "arbitrary")),
    )(a, b)
    return out if swiglu else out[0]


def _row_tile(S):
    return _pick(S, 512, 8)


def _norm_mod_fwd(x, gain, sc, sh):
    NB, S, D = x.shape
    tr = _row_tile(S)

    def body(x_ref, g_ref, sc_ref, sh_ref, h_ref):
        xv = x_ref[...]
        ms = jnp.mean(xv * xv, axis=-1, keepdims=True)
        n = xv * lax.rsqrt(ms + EPS) * g_ref[...]
        h_ref[...] = (n * (1.0 + sc_ref[...]) + sh_ref[...]).astype(BF16)

    tok = pl.BlockSpec((None, tr, D), lambda b, r: (b, r, 0))
    per_ex = pl.BlockSpec((None, 1, D), lambda b, r: (b, 0, 0))
    return pl.pallas_call(
        body, name="norm_mod_fwd", grid=(NB, S // tr),
        in_specs=[tok, pl.BlockSpec((1, D), lambda b, r: (0, 0)), per_ex, per_ex],
        out_specs=tok, out_shape=jax.ShapeDtypeStruct((NB, S, D), BF16),
        compiler_params=_params(("parallel", "parallel")),
    )(x, gain, sc, sh)


def _norm_mod_bwd(a, w, x, gain, sc, dres, name, y=None, g=None):
    NB, S, D = x.shape
    T, K = a.shape
    tm = _pick(S, 512, 128)
    per_ex_tiles = S // tm
    tk = K if K <= MATMUL_SINGLE_K else _pick(K, 3072, 128)
    nk = K // tk
    gated = y is not None

    def body(*refs):
        a_ref, w_ref, x_ref, g_ref, sc_ref, dres_ref = refs[:6]
        refs = refs[6:]
        if gated:
            y_ref, gate_ref = refs[:2]
            refs = refs[2:]
        dx_ref, dsh_ref, dsc_ref, dgain_ref = refs[:4]
        acc_ref = refs[-1]
        i, k = pl.program_id(0), pl.program_id(1)

        @pl.when(k == 0)
        def _():
            acc_ref[...] = jnp.zeros_like(acc_ref)

        acc_ref[...] += lax.dot_general(a_ref[...], w_ref[...], _NT, preferred_element_type=F32)

        @pl.when(k == nk - 1)
        def _():
            first_of_example = i % per_ex_tiles == 0

            @pl.when(first_of_example)
            def _():
                dsh_ref[...] = jnp.zeros_like(dsh_ref)
                dsc_ref[...] = jnp.zeros_like(dsc_ref)
                if gated:
                    refs[5][...] = jnp.zeros_like(refs[5])

            @pl.when(i == 0)
            def _():
                dgain_ref[...] = jnp.zeros_like(dgain_ref)

            xv = x_ref[...]
            rstd = lax.rsqrt(jnp.mean(xv * xv, axis=-1, keepdims=True) + EPS)
            xh = xv * rstd
            gn = g_ref[...]
            dh = acc_ref[...]
            dsh_ref[...] += jnp.sum(dh, axis=0, keepdims=True)
            dsc_ref[...] += jnp.sum(dh * (xh * gn), axis=0, keepdims=True)
            dn = dh * (1.0 + sc_ref[...])
            dgain_ref[...] += jnp.sum(dn * xh, axis=0, keepdims=True)
            dxh = dn * gn
            proj = jnp.mean(dxh * xh, axis=-1, keepdims=True)
            dx = rstd * (dxh - xh * proj) + dres_ref[...]
            dx_ref[...] = dx
            if gated:
                refs[4][...] = (dx * gate_ref[...]).astype(BF16)
                refs[5][...] += jnp.sum(dx * y_ref[...], axis=0, keepdims=True)

    tok = pl.BlockSpec((None, tm, D), lambda i, k: (i // per_ex_tiles, i % per_ex_tiles, 0))
    per_ex = pl.BlockSpec((None, 1, D), lambda i, k: (i // per_ex_tiles, 0, 0))
    row = pl.BlockSpec((1, D), lambda i, k: (0, 0))
    in_specs = [pl.BlockSpec((tm, tk), lambda i, k: (i, k)), pl.BlockSpec((D, tk), lambda i, k: (0, k)),
                tok, row, per_ex, tok]
    out_specs = [tok, per_ex, per_ex, row]
    out_shape = [jax.ShapeDtypeStruct((NB, S, D), F32), jax.ShapeDtypeStruct((NB, 1, D), F32),
                 jax.ShapeDtypeStruct((NB, 1, D), F32), jax.ShapeDtypeStruct((1, D), F32)]
    operands = [a, w, x, gain, sc, dres]
    if gated:
        in_specs += [tok, per_ex]
        out_specs += [tok, per_ex]
        out_shape += [jax.ShapeDtypeStruct((NB, S, D), BF16), jax.ShapeDtypeStruct((NB, 1, D), F32)]
        operands += [y, g]
    return pl.pallas_call(
        body, name=name, grid=(T // tm, nk), in_specs=in_specs, out_specs=out_specs, out_shape=out_shape,
        scratch_shapes=[pltpu.VMEM((tm, D), F32)],
        compiler_params=_params(("arbitrary", "arbitrary")),
    )(*operands)


def _gate_res(x, y, g, norm=None):
    NB, S, D = x.shape
    tr = _row_tile(S)

    def body(x_ref, y_ref, g_ref, *refs):
        xo = x_ref[...] + g_ref[...] * y_ref[...]
        refs[-1 if norm is None else -2][...] = xo
        if norm is not None:
            gain_ref, sc_ref, sh_ref, _, h_ref = refs
            n = xo * lax.rsqrt(jnp.mean(xo * xo, axis=-1, keepdims=True) + EPS) * gain_ref[...]
            h_ref[...] = (n * (1.0 + sc_ref[...]) + sh_ref[...]).astype(BF16)

    tok = pl.BlockSpec((None, tr, D), lambda b, r: (b, r, 0))
    per_ex = pl.BlockSpec((None, 1, D), lambda b, r: (b, 0, 0))
    in_specs, out_specs, operands = [tok, tok, per_ex], [tok], [x, y, g]
    out_shape = [jax.ShapeDtypeStruct((NB, S, D), F32)]
    if norm is not None:
        in_specs += [pl.BlockSpec((1, D), lambda b, r: (0, 0)), per_ex, per_ex]
        out_specs.append(tok)
        out_shape.append(jax.ShapeDtypeStruct((NB, S, D), BF16))
        operands += list(norm)
    out = pl.pallas_call(
        body, name="gate_res", grid=(NB, S // tr), in_specs=in_specs, out_specs=out_specs, out_shape=out_shape,
        compiler_params=_params(("parallel", "parallel")),
    )(*operands)
    return out[0] if norm is None else out


def _gate_res_bwd(dxo, y, g):
    NB, S, D = dxo.shape
    tr = _row_tile(S)

    def body(d_ref, y_ref, g_ref, dy_ref, dg_ref):
        @pl.when(pl.program_id(1) == 0)
        def _():
            dg_ref[...] = jnp.zeros_like(dg_ref)

        d = d_ref[...]
        dy_ref[...] = (d * g_ref[...]).astype(BF16)
        dg_ref[...] += jnp.sum(d * y_ref[...], axis=0, keepdims=True)

    tok = pl.BlockSpec((None, tr, D), lambda b, r: (b, r, 0))
    per_ex = pl.BlockSpec((None, 1, D), lambda b, r: (b, 0, 0))
    return pl.pallas_call(
        body, name="gate_res_bwd", grid=(NB, S // tr), in_specs=[tok, tok, per_ex], out_specs=[tok, per_ex],
        out_shape=[jax.ShapeDtypeStruct((NB, S, D), BF16), jax.ShapeDtypeStruct((NB, 1, D), F32)],
        compiler_params=_params(("arbitrary", "arbitrary")),
    )(dxo, y, g)


def _sigmoid(v):
    return 1.0 / (1.0 + jnp.exp(-v))


def _ff_tile(F):
    return _pick(F, 1536, 128)


def _interleave(gate, up):
    F = gate.shape[-1]
    tf = _ff_tile(F)
    parts = []
    for j in range(F // tf):
        parts += [gate[..., j * tf:(j + 1) * tf], up[..., j * tf:(j + 1) * tf]]
    return jnp.concatenate(parts, axis=-1)


def _swiglu_bwd(dm, wd, gu):
    T, D = dm.shape
    F = wd.shape[0]
    tf = _ff_tile(F)
    tm = _pick(T, 1024, 128)
    assert D <= MATMUL_SINGLE_K

    def body(a_ref, b_ref, gu_ref, o_ref):
        d = lax.dot_general(a_ref[...], b_ref[...], _NT, preferred_element_type=F32)
        g, u = gu_ref[:, :tf].astype(F32), gu_ref[:, tf:].astype(F32)
        s = _sigmoid(g)
        o_ref[:, :tf] = (d * u * (s * (1.0 + g * (1.0 - s)))).astype(BF16)
        o_ref[:, tf:] = (d * (g * s)).astype(BF16)

    return pl.pallas_call(
        body, name="swiglu_bwd", grid=(T // tm, F // tf),
        in_specs=[pl.BlockSpec((tm, D), lambda i, j: (i, 0)), pl.BlockSpec((tf, D), lambda i, j: (j, 0)),
                  pl.BlockSpec((tm, 2 * tf), lambda i, j: (i, j))],
        out_specs=pl.BlockSpec((tm, 2 * tf), lambda i, j: (i, j)),
        out_shape=jax.ShapeDtypeStruct((T, 2 * F), BF16),
        compiler_params=_params(("parallel", "parallel")),
    )(dm, wd, gu)


def _loss_fwd_bwd(y, target, m, g):
    NB, S, D = y.shape
    tr = _row_tile(S)

    def body(y_ref, t_ref, m_ref, g_ref, l_ref, d_ref, dm_ref, dg_ref):
        @pl.when((pl.program_id(0) == 0) & (pl.program_id(1) == 0))
        def _():
            l_ref[...] = jnp.zeros_like(l_ref)

        @pl.when(pl.program_id(1) == 0)
        def _():
            dg_ref[...] = jnp.zeros_like(dg_ref)

        e = y_ref[...] - t_ref[...]
        d = e / D
        d_ref[...] = d
        l_ref[...] += 0.5 * jnp.sum(jnp.mean(e * e, axis=-1, keepdims=True), axis=0, keepdims=True)
        dm_ref[...] = (d * g_ref[...]).astype(BF16)
        dg_ref[...] += jnp.sum(d * m_ref[...], axis=0, keepdims=True)

    tok = pl.BlockSpec((None, tr, D), lambda b, r: (b, r, 0))
    per_ex = pl.BlockSpec((None, 1, D), lambda b, r: (b, 0, 0))
    return pl.pallas_call(
        body, name="loss", grid=(NB, S // tr), in_specs=[tok, tok, tok, per_ex],
        out_specs=[pl.BlockSpec((1, 1), lambda b, r: (0, 0)), tok, tok, per_ex],
        out_shape=[jax.ShapeDtypeStruct((1, 1), F32), jax.ShapeDtypeStruct((NB, S, D), F32),
                   jax.ShapeDtypeStruct((NB, S, D), BF16), jax.ShapeDtypeStruct((NB, 1, D), F32)],
        compiler_params=_params(("arbitrary", "arbitrary")),
    )(y, target, m, g)


def _half_sums(v, lo):
    sa = jnp.sum(jnp.where(lo, v, 0.0), axis=-1, keepdims=True)
    sb = jnp.sum(jnp.where(lo, 0.0, v), axis=-1, keepdims=True)
    return jnp.where(lo, sa, sb)


def _rope_swap(v, lane64):
    up = pltpu.roll(v, LANES - ROT_DIM // 2, 1)
    down = pltpu.roll(v, ROT_DIM // 2, 1)
    return jnp.where(lane64 < ROT_DIM // 2, up, jnp.where(lane64 < ROT_DIM, down, 0.0))


def _qk_prep_fwd(qkv, tab_c, tab_s, gains):
    T, W = qkv.shape
    R = W // LANES
    tt = _pick(T, 256, 8)

    def body(x_ref, c_ref, s_ref, g_ref, o_ref):
        lane = lax.broadcasted_iota(jnp.int32, (tt, LANES), 1)
        lo = lane < HEAD_DIM
        lane64 = lane & (HEAD_DIM - 1)
        c, s = c_ref[...], s_ref[...]
        for j in range(R - 1):
            cols = slice(j * LANES, (j + 1) * LANES)
            xv = x_ref[:, cols]
            rstd = lax.rsqrt(_half_sums(xv * xv, lo) / HEAD_DIM + EPS)
            yn = xv * rstd * g_ref[j:j + 1, :]
            o_ref[:, cols] = (yn * c + _rope_swap(yn, lane64) * s).astype(BF16)
        o_ref[:, (R - 1) * LANES:] = x_ref[:, (R - 1) * LANES:].astype(BF16)

    tok = pl.BlockSpec((tt, W), lambda t: (t, 0))
    tab = pl.BlockSpec((tt, LANES), lambda t: (t, 0))
    return pl.pallas_call(
        body, name="qk_prep_fwd", grid=(T // tt,),
        in_specs=[tok, tab, tab, pl.BlockSpec((R, LANES), lambda t: (0, 0))],
        out_specs=tok, out_shape=jax.ShapeDtypeStruct((T, W), BF16),
        compiler_params=_params(("parallel",)),
    )(qkv, tab_c, tab_s, gains)


def _qk_prep_bwd(qkv, dq, dk, dv, tab_c, tab_s, gains):
    T, W = qkv.shape
    R = W // LANES
    QW = dq.shape[1]
    tt = _pick(T, 256, 8)

    def body(x_ref, dq_ref, dk_ref, dv_ref, c_ref, s_ref, g_ref, o_ref, dg_ref):
        @pl.when(pl.program_id(0) == 0)
        def _():
            dg_ref[...] = jnp.zeros_like(dg_ref)

        lane = lax.broadcasted_iota(jnp.int32, (tt, LANES), 1)
        lo = lane < HEAD_DIM
        lane64 = lane & (HEAD_DIM - 1)
        c, s = c_ref[...], s_ref[...]
        for j in range(R - 1):
            cols = slice(j * LANES, (j + 1) * LANES)
            xv = x_ref[:, cols]
            d = dq_ref[:, cols] if j < R - 2 else dk_ref[...]
            rstd = lax.rsqrt(_half_sums(xv * xv, lo) / HEAD_DIM + EPS)
            xh = xv * rstd
            dyn = d * c + _rope_swap(d * s, lane64)
            dg_ref[j:j + 1, :] += jnp.sum(dyn * xh, axis=0, keepdims=True)
            dxh = dyn * g_ref[j:j + 1, :]
            proj = _half_sums(dxh * xh, lo) / HEAD_DIM
            o_ref[:, cols] = (rstd * (dxh - xh * proj)).astype(BF16)
        o_ref[:, (R - 1) * LANES:] = dv_ref[...].astype(BF16)

    tok = pl.BlockSpec((tt, W), lambda t: (t, 0))
    tab = pl.BlockSpec((tt, LANES), lambda t: (t, 0))
    gsp = pl.BlockSpec((R, LANES), lambda t: (0, 0))
    return pl.pallas_call(
        body, name="qk_prep_bwd", grid=(T // tt,),
        in_specs=[tok, pl.BlockSpec((tt, QW), lambda t: (t, 0)), tab, tab, tab, tab, gsp], out_specs=[tok, gsp],
        out_shape=[jax.ShapeDtypeStruct((T, W), BF16), jax.ShapeDtypeStruct((R, LANES), F32)],
        compiler_params=_params(("arbitrary",)),
    )(qkv, dq, dk, dv, tab_c, tab_s, gains)


def _band_mask(i):
    r = lax.broadcasted_iota(jnp.int32, (2 * BLOCK, 2 * BLOCK), 0) & (BLOCK - 1)
    c = lax.broadcasted_iota(jnp.int32, (2 * BLOCK, 2 * BLOCK), 1)
    rel = r + BLOCK - c
    return (rel >= 0) & (rel < BLOCK) & ((c >= BLOCK) | (i > 0))


def _swa_softmax(s, valid, sink):
    s = jnp.where(valid, s * ATTN_SCALE, NEG_BIG)
    m = jnp.maximum(jnp.max(s, axis=1, keepdims=True), sink)
    p = jnp.exp(s - m)
    ps = jnp.exp(sink - m)
    denom = jnp.sum(p, axis=1, keepdims=True) + ps
    return p / denom, ps / denom


A_GROUP = 4


Q_WIDTH_A = N_Q_A * HEAD_DIM
N_PAIR_A = Q_WIDTH_A // LANES


def _swa_specs():
    qs = pl.BlockSpec((None, BLOCK, Q_WIDTH_A), lambda b, i: (b, i, 0))

    def kv(col, back):
        return pl.BlockSpec((None, BLOCK, LANES), lambda b, i: (b, jnp.maximum(i - back, 0), col))

    return qs, kv(N_PAIR_A, 1), kv(N_PAIR_A, 0), kv(N_PAIR_A + 1, 1), kv(N_PAIR_A + 1, 0)


def _dup_heads(t):
    lo = lax.broadcasted_iota(jnp.int32, t.shape, 1) < HEAD_DIM
    sw = pltpu.roll(t.astype(F32), HEAD_DIM, 1).astype(BF16)
    return jnp.where(lo, t, sw), jnp.where(lo, sw, t)


def _kv_tiles(kp_ref, kc_ref, vp_ref, vc_ref):
    kd = _dup_heads(jnp.concatenate([kp_ref[...], kc_ref[...]], axis=0))
    vd = _dup_heads(jnp.concatenate([vp_ref[...], vc_ref[...]], axis=0))
    return kd, vd


def _attn_a_fwd(qkn, sinks):
    NB, S, _ = qkn.shape
    qs, kp, kc, vp, vc = _swa_specs()

    def body(q_ref, kp_ref, kc_ref, vp_ref, vc_ref, sink_ref, o_ref):
        i = pl.program_id(1)
        kd, vd = _kv_tiles(kp_ref, kc_ref, vp_ref, vc_ref)
        valid = _band_mask(i)
        lo = lax.broadcasted_iota(jnp.int32, (BLOCK, LANES), 1) < HEAD_DIM
        top = lax.broadcasted_iota(jnp.int32, (2 * BLOCK, 1), 0) < BLOCK
        for first in range(0, N_PAIR_A, A_GROUP):
            pairs = range(first, first + A_GROUP)
            qs_ = [jnp.concatenate(_head_halves(q_ref[:, p * LANES:(p + 1) * LANES], lo), axis=0) for p in pairs]
            ss = [lax.dot_general(q, kd[2 * p // GROUP_A], _NT, preferred_element_type=F32) for q, p in zip(qs_, pairs)]
            pns = [_swa_softmax(s, valid, jnp.where(top, sink_ref[2 * p], sink_ref[2 * p + 1]))[0]
                   for s, p in zip(ss, pairs)]
            pvs = [jnp.dot(pn.astype(BF16), vd[2 * p // GROUP_A], preferred_element_type=F32) for pn, p in zip(pns, pairs)]
            for pv, p in zip(pvs, pairs):
                o_ref[:, p * LANES:(p + 1) * LANES] = jnp.where(lo, pv[:BLOCK], pv[BLOCK:]).astype(BF16)

    return pl.pallas_call(
        body, name="attn_a_fwd", grid=(NB, S // BLOCK),
        in_specs=[qs, kp, kc, vp, vc, pl.BlockSpec(memory_space=pltpu.SMEM)],
        out_specs=qs, out_shape=jax.ShapeDtypeStruct((NB, S, Q_WIDTH_A), BF16),
        compiler_params=_params(("parallel", "arbitrary")),
    )(qkn, qkn, qkn, qkn, qkn, sinks)


def _attn_a_bwd(qkn, do, sinks):
    NB, S, _ = qkn.shape
    qs, kp, kc, vp, vc = _swa_specs()
    full = pl.BlockSpec((None, S, LANES), lambda b, i: (b, 0, 0))
    sink_out = pl.BlockSpec((None, N_Q_A, LANES), lambda b, i: (b, 0, 0))

    def body(q_ref, do_ref, kp_ref, kc_ref, vp_ref, vc_ref, sink_ref, dq_ref, dk_ref, dv_ref, ds_ref, dk_s, dv_s):
        i = pl.program_id(1)

        @pl.when(i == 0)
        def _():
            dk_ref[...] = jnp.zeros_like(dk_ref)
            dv_ref[...] = jnp.zeros_like(dv_ref)
            ds_ref[...] = jnp.zeros_like(ds_ref)

        dk_s[...] = jnp.zeros_like(dk_s)
        dv_s[...] = jnp.zeros_like(dv_s)
        kd, vd = _kv_tiles(kp_ref, kc_ref, vp_ref, vc_ref)
        valid = _band_mask(i)
        lo = lax.broadcasted_iota(jnp.int32, (BLOCK, LANES), 1) < HEAD_DIM
        top = lax.broadcasted_iota(jnp.int32, (2 * BLOCK, 1), 0) < BLOCK
        for first in range(0, N_PAIR_A, A_GROUP):
            pairs = range(first, first + A_GROUP)
            kvs = [2 * p // GROUP_A for p in pairs]
            qs_ = [jnp.concatenate(_head_halves(q_ref[:, p * LANES:(p + 1) * LANES], lo), axis=0) for p in pairs]
            dos = [jnp.concatenate(_head_halves(do_ref[:, p * LANES:(p + 1) * LANES], lo), axis=0) for p in pairs]
            ss = [lax.dot_general(q, kd[kv], _NT, preferred_element_type=F32) for q, kv in zip(qs_, kvs)]
            dps = [lax.dot_general(d, vd[kv], _NT, preferred_element_type=F32) for d, kv in zip(dos, kvs)]
            sm = [_swa_softmax(s, valid, jnp.where(top, sink_ref[2 * p], sink_ref[2 * p + 1])) for s, p in zip(ss, pairs)]
            deltas = [jnp.sum(pn * dp, axis=1, keepdims=True) for (pn, _), dp in zip(sm, dps)]
            dsbs = [(pn * (dp - delta) * ATTN_SCALE).astype(BF16) for (pn, _), dp, delta in zip(sm, dps, deltas)]
            for n, p in enumerate(pairs):
                dq2 = jnp.dot(dsbs[n], kd[kvs[n]], preferred_element_type=F32)
                dq_ref[:, p * LANES:(p + 1) * LANES] = jnp.where(lo, dq2[:BLOCK], dq2[BLOCK:])
                dk_s[kvs[n]] += lax.dot_general(dsbs[n], qs_[n], _TN, preferred_element_type=F32)
                dv_s[kvs[n]] += lax.dot_general(sm[n][0].astype(BF16), dos[n], _TN, preferred_element_type=F32)
                t = sm[n][1] * deltas[n]
                for hh in range(2):
                    dsink = -jnp.sum(t[hh * BLOCK:(hh + 1) * BLOCK], axis=0, keepdims=True)
                    ds_ref[2 * p + hh:2 * p + hh + 1, :] += jnp.broadcast_to(dsink, (1, LANES))

        lo2 = lax.broadcasted_iota(jnp.int32, (2 * BLOCK, LANES), 1) < HEAD_DIM

        def fold(acc):
            halves = [acc[kv] + pltpu.roll(acc[kv], HEAD_DIM, 1) for kv in range(N_KV_A)]
            return jnp.where(lo2, halves[0], halves[1])

        dk2, dv2 = fold(dk_s), fold(dv_s)

        @pl.when(i > 0)
        def _():
            start = pl.multiple_of((i - 1) * BLOCK, BLOCK)
            dk_ref[pl.ds(start, 2 * BLOCK), :] += dk2
            dv_ref[pl.ds(start, 2 * BLOCK), :] += dv2

        @pl.when(i == 0)
        def _():
            dk_ref[0:BLOCK, :] += dk2[BLOCK:, :]
            dv_ref[0:BLOCK, :] += dv2[BLOCK:, :]

    slots = pltpu.VMEM((N_KV_A, 2 * BLOCK, LANES), F32)
    return pl.pallas_call(
        body, name="attn_a_bwd", grid=(NB, S // BLOCK),
        in_specs=[qs, qs, kp, kc, vp, vc, pl.BlockSpec(memory_space=pltpu.SMEM)],
        out_specs=[qs, full, full, sink_out],
        out_shape=[jax.ShapeDtypeStruct((NB, S, Q_WIDTH_A), F32), jax.ShapeDtypeStruct((NB, S, LANES), F32),
                   jax.ShapeDtypeStruct((NB, S, LANES), F32), jax.ShapeDtypeStruct((NB, N_Q_A, LANES), F32)],
        scratch_shapes=[slots, slots],
        compiler_params=_params(("parallel", "arbitrary")),
    )(qkn, do, qkn, qkn, qkn, qkn, sinks)


def _cumsum_mats():
    src = lax.broadcasted_iota(jnp.int32, (2 * BLOCK, 2 * BLOCK), 0) % BLOCK
    dst = lax.broadcasted_iota(jnp.int32, (2 * BLOCK, 2 * BLOCK), 1)
    ones = dst >= BLOCK
    rev = ((src > dst) | ones).astype(BF16)
    fwd = ((src < dst) | ones).astype(BF16)
    return rev, fwd


def _log_sigmoids(z):
    sp = jnp.log(1.0 + jnp.exp(-jnp.abs(z)))
    return jnp.minimum(z, 0.0) - sp, -(jnp.maximum(z, 0.0) + sp)


def _cumsum_mxu_many(vs, mat):
    parts = []
    for v in vs:
        hi = v.astype(BF16)
        parts.append(jnp.concatenate([hi, (v - hi.astype(F32)).astype(BF16)], axis=1))
    r = jnp.dot(jnp.concatenate(parts, axis=0), mat, preferred_element_type=F32)
    return [(r[n * BLOCK:(n + 1) * BLOCK, :BLOCK], r[n * BLOCK:(n + 1) * BLOCK, BLOCK:]) for n in range(len(vs))]


def _strict_mask():
    r = lax.broadcasted_iota(jnp.int32, (BLOCK, BLOCK), 0)
    c = lax.broadcasted_iota(jnp.int32, (BLOCK, BLOCK), 1)
    return c < r


def _tile(ref, j):
    return ref[pl.ds(pl.multiple_of(j * BLOCK, BLOCK), BLOCK), :]


SWEEP_EXIT = -88.0


def _head_halves(t, lo):
    zero = jnp.zeros_like(t)
    return jnp.where(lo, t, zero), jnp.where(lo, zero, t)


def _sb_specs(S, HD, width):
    n = HD // width
    blk = pl.BlockSpec((None, BLOCK, width), lambda b, p, i: (b, i, p))
    k_full = pl.BlockSpec((None, S, width), lambda b, p, i: (b, 0, n + p))
    v_full = pl.BlockSpec((None, S, width), lambda b, p, i: (b, 0, 2 * n + p))
    mat = pl.BlockSpec((2 * BLOCK, 2 * BLOCK), lambda b, p, i: (0, 0))
    return blk, k_full, v_full, mat


SB_FWD_PAIRS = 4
SB_BWD_PAIRS = 2
SB_BWD_TILES = 2
SB_BWD_VMEM_LIMIT_BYTES = 58 * 1024 * 1024


def _attn_b_fwd(qkv, rev):
    NB, S, W = qkv.shape
    HD = W // 3
    width = SB_FWD_PAIRS * LANES
    n_heads = 2 * SB_FWD_PAIRS
    blk, k_full, v_full, mat = _sb_specs(S, HD, width)

    def body(q_ref, k_ref, v_ref, rev_ref, o_ref):
        i = pl.program_id(2)
        rv = rev_ref[...]
        mask = _strict_mask()
        lo = lax.broadcasted_iota(jnp.int32, (BLOCK, LANES), 1) < HEAD_DIM
        q_all = q_ref[...]
        q_stack = [jnp.concatenate(_head_halves(q_all[:, p * LANES:(p + 1) * LANES] * ATTN_SCALE, lo), axis=0)
                   for p in range(SB_FWD_PAIRS)]

        def pair_tiles(ref, j):
            t = _tile(ref, j)
            return [t[:, p * LANES:(p + 1) * LANES] for p in range(SB_FWD_PAIRS)]

        def tiles_pass(js, carries, diagonal_first, last_counts=None):
            zs, v_tiles = [], []
            for j in js:
                ks = pair_tiles(k_ref, j)
                v_tiles.append(pair_tiles(v_ref, j))
                for p in range(SB_FWD_PAIRS):
                    z2 = lax.dot_general(q_stack[p], ks[p], _NT, preferred_element_type=F32)
                    zs += [z2[:BLOCK], z2[BLOCK:]]
            logs = [_log_sigmoids(z) for z in zs]
            masked = [jnp.where(mask, lm, 0.0) if diagonal_first and n < n_heads else lm
                      for n, (_, lm) in enumerate(logs)]
            cums = _cumsum_mxu_many(masked, rv)
            probs, new_c = {}, []
            for h in range(n_heads):
                carry = None if diagonal_first else carries[h]
                for t in range(len(js)):
                    n = t * n_heads + h
                    after, rs = cums[n]
                    if diagonal_first and t == 0:
                        a = jnp.where(mask, jnp.exp(logs[n][0] + after), 0.0)
                        carry = rs
                    else:
                        a = jnp.exp(logs[n][0] + after + carry)
                        carry = carry + rs
                    if last_counts is not None and t == len(js) - 1:
                        a = jnp.where(last_counts, a, 0.0)
                    probs[t, h] = a.astype(BF16)
                new_c.append(carry)
            outs = []
            for p in range(SB_FWD_PAIRS):
                total = None
                for t in range(len(js)):
                    pv = jnp.dot(jnp.concatenate([probs[t, 2 * p], probs[t, 2 * p + 1]], axis=0), v_tiles[t][p],
                                 preferred_element_type=F32)
                    part = jnp.where(lo, pv[:BLOCK], pv[BLOCK:])
                    total = part if total is None else total + part
                outs.append(total)
            return new_c, outs

        carries, accs = tiles_pass([i, jnp.maximum(i - 1, 0)], None, True, last_counts=i > 0)

        def live(cs):
            top = cs[0]
            for c in cs[1:]:
                top = jnp.maximum(top, c)
            return jnp.max(top) > SWEEP_EXIT

        def cond(st):
            return (st[0] < i - 1) & st[1]

        def step(st):
            jj, _, cs, accs = st
            new_c, outs = tiles_pass([i - 2 - jj], cs, False)
            return jj + 1, live(new_c), new_c, [acc + o for acc, o in zip(accs, outs)]

        st = lax.while_loop(cond, step, (jnp.int32(0), live(carries), carries, accs))
        for p in range(SB_FWD_PAIRS):
            o_ref[:, p * LANES:(p + 1) * LANES] = st[3][p].astype(BF16)

    return pl.pallas_call(
        body, name="attn_b_fwd", grid=(NB, HD // width, S // BLOCK),
        in_specs=[blk, k_full, v_full, mat], out_specs=blk,
        out_shape=jax.ShapeDtypeStruct((NB, S, HD), BF16),
        compiler_params=_params(("parallel", "parallel", "arbitrary")),
    )(qkv, qkv, qkv, rev)


def _attn_b_bwd(qkv, do, rev, fwd):
    NB, S, W = qkv.shape
    HD = W // 3
    width = SB_BWD_PAIRS * LANES
    n_heads = 2 * SB_BWD_PAIRS
    nj = S // BLOCK
    blk, k_full, v_full, mat = _sb_specs(S, HD, width)
    acc_full = pl.BlockSpec((None, S, width), lambda b, p, i: (b, 0, p))

    def body(q_ref, do_ref, k_ref, v_ref, rev_ref, fwd_ref, dq_ref, dk_ref, dv_ref, sig_s, a_s, e_s):
        i = pl.program_id(2)

        @pl.when(i == 0)
        def _():
            dk_ref[...] = jnp.zeros_like(dk_ref)
            dv_ref[...] = jnp.zeros_like(dv_ref)

        rv, fw = rev_ref[...], fwd_ref[...]
        mask = _strict_mask()
        lo = lax.broadcasted_iota(jnp.int32, (BLOCK, LANES), 1) < HEAD_DIM
        pairs = range(SB_BWD_PAIRS)

        def cols(p):
            return slice(p * LANES, (p + 1) * LANES)

        q_stack = [jnp.concatenate(_head_halves(q_ref[:, cols(p)], lo), axis=0) for p in pairs]
        qs_stack = [q * ATTN_SCALE for q in q_stack]
        do_stack = [jnp.concatenate(_head_halves(do_ref[:, cols(p)], lo), axis=0) for p in pairs]

        def sweep1_tiles(js, carries, diagonal_first):
            zs, das = [], []
            for j in js:
                kj, vj = _tile(k_ref, j), _tile(v_ref, j)
                for p in pairs:
                    z2 = lax.dot_general(qs_stack[p], kj[:, cols(p)], _NT, preferred_element_type=F32)
                    da2 = lax.dot_general(do_stack[p], vj[:, cols(p)], _NT, preferred_element_type=F32)
                    zs += [z2[:BLOCK], z2[BLOCK:]]
                    das += [da2[:BLOCK], da2[BLOCK:]]
            logs = [_log_sigmoids(z) for z in zs]
            cums = _cumsum_mxu_many([jnp.where(mask, lm, 0.0) if diagonal_first and n < n_heads else lm
                                     for n, (_, lm) in enumerate(logs)], rv)
            new_c, stores = [], []
            for h in range(n_heads):
                carry = None if diagonal_first else carries[h]
                for t, j in enumerate(js):
                    n = t * n_heads + h
                    lb, (after, rs) = logs[n][0], cums[n]
                    if diagonal_first and t == 0:
                        a = jnp.where(mask, jnp.exp(lb + after), 0.0)
                        carry = rs
                    else:
                        a = jnp.exp(lb + after + carry)
                        carry = carry + rs
                    stores.append((t, h, j, jnp.exp(lb), a.astype(BF16), das[n] * a))
                new_c.append(carry)
            for t, h, j, sg, ab, e in sorted(stores, key=lambda s: -s[0]):
                sig_s[h, j] = sg
                a_s[h, j] = ab
                e_s[h, j] = e
            return new_c

        carries = sweep1_tiles([jnp.maximum(i - t, 0) for t in range(SB_BWD_TILES + 1)], None, True)
        done = SB_BWD_TILES

        def live(cs):
            top = cs[0]
            for c in cs[1:]:
                top = jnp.maximum(top, c)
            return jnp.max(top) > SWEEP_EXIT

        def cond(st):
            return (done + SB_BWD_TILES * st[0] < i) & st[1]

        def sweep1(st):
            first = i - 1 - done - SB_BWD_TILES * st[0]
            new_c = sweep1_tiles([jnp.maximum(first - t, 0) for t in range(SB_BWD_TILES)], st[2], False)
            return st[0] + 1, live(new_c), new_c

        trips = lax.while_loop(cond, sweep1, (jnp.int32(0), live(carries), carries))[0]
        lowest = jnp.maximum(i - done - SB_BWD_TILES * trips, 0)

        def grads(js, st, diagonal_last=False, counts=None):
            prefixes, dqs = st
            counts = counts or [None] * len(js)
            es = [e_s[h, j] for j in js for h in range(n_heads)]
            cums = _cumsum_mxu_many(es, fw)
            dzs, new_p = [], []
            for h in range(n_heads):
                prefix = prefixes[h]
                for t, j in enumerate(js):
                    n = t * n_heads + h
                    sg = sig_s[h, j]
                    e_before, rs = cums[n]
                    dz = (es[n] * (1.0 - sg) - (e_before + prefix) * sg) * ATTN_SCALE
                    if diagonal_last and t == len(js) - 1:
                        dz = jnp.where(mask, dz, 0.0)
                    if counts[t] is not None:
                        dz = jnp.where(counts[t], dz, 0.0)
                        rs = jnp.where(counts[t], rs, 0.0)
                    dzs.append((t, h, dz.astype(BF16)))
                    prefix = prefix + rs
                new_p.append(prefix)
            dz_of = {(t, h): dz for t, h, dz in dzs}
            new_dq = list(dqs)
            for t, j in enumerate(js):
                kj = _tile(k_ref, j)
                rows = pl.ds(pl.multiple_of(j * BLOCK, BLOCK), BLOCK)
                for p in pairs:
                    dz_stack = jnp.concatenate([dz_of[t, 2 * p], dz_of[t, 2 * p + 1]], axis=0)
                    a_stack = jnp.concatenate([a_s[2 * p, j], a_s[2 * p + 1, j]], axis=0)
                    if counts[t] is not None:
                        a_stack = jnp.where(counts[t], a_stack, jnp.zeros_like(a_stack))
                    dq2 = jnp.dot(dz_stack, kj[:, cols(p)], preferred_element_type=F32)
                    new_dq[p] = new_dq[p] + jnp.where(lo, dq2[:BLOCK], dq2[BLOCK:])
                    dk_ref[rows, cols(p)] += lax.dot_general(dz_stack, q_stack[p], _TN, preferred_element_type=F32)
                    dv_ref[rows, cols(p)] += lax.dot_general(a_stack, do_stack[p], _TN, preferred_element_type=F32)
            return new_p, new_dq

        zeros = jnp.zeros((BLOCK, BLOCK), F32)
        st = ([zeros] * n_heads, [zeros] * SB_BWD_PAIRS)
        count = jnp.maximum(i - done, 0) - lowest
        st = lax.fori_loop(0, count % SB_BWD_TILES, lambda t, st: grads([lowest + t], st), st)
        start = lowest + count % SB_BWD_TILES
        st = lax.fori_loop(0, count // SB_BWD_TILES,
                           lambda t, st: grads([start + SB_BWD_TILES * t + u for u in range(SB_BWD_TILES)], st), st)
        top = [jnp.maximum(i - t, 0) for t in range(SB_BWD_TILES, -1, -1)]
        dqs = grads(top, st, diagonal_last=True, counts=[i >= t for t in range(SB_BWD_TILES, 0, -1)] + [None])[1]
        for p in pairs:
            dq_ref[:, cols(p)] = dqs[p]

    f32_stash = pltpu.VMEM((n_heads, nj, BLOCK, BLOCK), F32)
    bf16_stash = pltpu.VMEM((n_heads, nj, BLOCK, BLOCK), BF16)
    return pl.pallas_call(
        body, name="attn_b_bwd", grid=(NB, HD // width, nj),
        in_specs=[blk, blk, k_full, v_full, mat, mat], out_specs=[blk, acc_full, acc_full],
        out_shape=[jax.ShapeDtypeStruct((NB, S, HD), F32)] * 3,
        scratch_shapes=[f32_stash, bf16_stash, f32_stash],
        compiler_params=_params(("parallel", "parallel", "arbitrary"), SB_BWD_VMEM_LIMIT_BYTES),
    )(qkv, do, qkv, qkv, rev, fwd)


def _ada_fwd(c_all, w, b):
    L, D, N = w.shape
    B = c_all.shape[0]

    def body(c_ref, w_ref, b_ref, o_ref):
        cv = c_ref[...]
        cond = (cv * _sigmoid(cv)).astype(BF16)
        o_ref[...] = jnp.dot(cond, w_ref[...].astype(BF16), preferred_element_type=F32) + b_ref[...]

    return pl.pallas_call(
        body, name="ada_fwd", grid=(L,),
        in_specs=[pl.BlockSpec((B, D), lambda l: (0, 0)), pl.BlockSpec((None, D, N), lambda l: (l, 0, 0)),
                  pl.BlockSpec((None, 1, N), lambda l: (l, 0, 0))],
        out_specs=pl.BlockSpec((None, B, N), lambda l: (l, 0, 0)),
        out_shape=jax.ShapeDtypeStruct((L, B, N), F32),
        compiler_params=_params(("parallel",)),
    )(c_all, w, b)


def _ada_bwd(c_all, dmod_all, dmod_shard):
    L, B, N = dmod_shard.shape
    D = c_all.shape[1]
    N_all = dmod_all.shape[2]

    def body(c_ref, da_ref, ds_ref, gw_ref, gb_ref):
        cv = c_ref[...]
        cond = (cv * _sigmoid(cv)).astype(BF16)
        gw_ref[...] = lax.dot_general(cond, ds_ref[...].astype(BF16), _TN, preferred_element_type=F32)
        gb_ref[...] = jnp.sum(da_ref[...], axis=0, keepdims=True)

    return pl.pallas_call(
        body, name="ada_bwd", grid=(L,),
        in_specs=[pl.BlockSpec((B, D), lambda l: (0, 0)), pl.BlockSpec((None, B, N_all), lambda l: (l, 0, 0)),
                  pl.BlockSpec((None, B, N), lambda l: (l, 0, 0))],
        out_specs=[pl.BlockSpec((None, D, N), lambda l: (l, 0, 0)), pl.BlockSpec((None, 1, N_all), lambda l: (l, 0, 0))],
        out_shape=[jax.ShapeDtypeStruct((L, D, N), F32), jax.ShapeDtypeStruct((L, 1, N_all), F32)],
        compiler_params=_params(("parallel",)),
    )(c_all, dmod_all, dmod_shard)


def _adamw(w, g, m, v, name):
    shape = w.shape
    if w.ndim == 2:
        w, g, m, v = [t.reshape((1,) + shape) for t in (w, g, m, v)]
    L, R, C = w.shape
    tr = _pick(R, max(8, (1 << 18) // C), 8)
    c1 = 1.0 - ADAM_B1 ** ADAM_STEP
    c2 = 1.0 - ADAM_B2 ** ADAM_STEP

    def body(w_ref, g_ref, m_ref, v_ref, d_ref, nm_ref, nv_ref):
        gv = g_ref[...]
        nm = ADAM_B1 * m_ref[...] + (1.0 - ADAM_B1) * gv
        nv = ADAM_B2 * v_ref[...] + (1.0 - ADAM_B2) * (gv * gv)
        d_ref[...] = -ADAM_LR * ((nm / c1) / (jnp.sqrt(nv / c2) + ADAM_EPS) + ADAM_WD * w_ref[...])
        nm_ref[...] = nm
        nv_ref[...] = nv

    spec = pl.BlockSpec((None, tr, C), lambda l, r: (l, r, 0))
    out = pl.pallas_call(
        body, name=name, grid=(L, R // tr), in_specs=[spec] * 4, out_specs=[spec] * 3,
        out_shape=[jax.ShapeDtypeStruct((L, R, C), F32)] * 3,
        compiler_params=_params(("parallel", "parallel")),
    )(w, g, m, v)
    return [t.reshape(shape) for t in out]


_SHARDED = (("wqkv_a", 2), ("wo_a", 1), ("wqkv_b", 2), ("wo_b", 1), ("w_gate", 2), ("w_up", 2), ("w_down", 1))


def _pack_full(layers, axis, gate_up=None):
    L = len(layers)
    R, C = layers[0].shape

    def shards(m):
        if gate_up is not None:
            F = C // 2
            tf, Cs = _ff_tile(F), F // 4
            assert tf % Cs == 0
            starts = [(2 * (s * Cs // tf) + gate_up) * tf + s * Cs % tf for s in range(4)]
            return jnp.stack([m[:, st:st + Cs] for st in starts])
        if axis == 2:
            return m.reshape(R, 4, C // 4).transpose(1, 0, 2)
        return m.reshape(4, R // 4, C)

    halves = [jnp.stack([shards(m) for m in layers[h * (L // 2):(h + 1) * (L // 2)]], axis=1) for h in range(2)]
    return jnp.stack(halves)


def _unpack_full(gathered, axis):
    _, Lh, Rs, Cs = gathered.shape
    t = gathered.reshape(4, 2, Lh, Rs, Cs)
    layers = []
    for h in range(2):
        for l in range(Lh):
            piece = t[:, h, l]
            if axis == 2:
                layers.append(piece.transpose(1, 0, 2).reshape(Rs, 4 * Cs))
            else:
                layers.append(piece.reshape(4 * Rs, Cs))
    return layers


def _sum_slabs(own, recv, name, with_bf16=False):
    C = own.shape[-1]
    out = _sum_leading(recv.reshape(recv.shape[0], -1, C), name, own=own.reshape(-1, C), with_bf16=with_bf16)
    if with_bf16:
        return out[0].reshape(own.shape), out[1].reshape(own.shape)
    return out.reshape(own.shape)


def _gather8(x, name):
    return _all_gather8([x], name)[0]


def _rope_tables(positions):
    half = ROT_DIM // 2
    inv_freq = jnp.power(jnp.float32(ROPE_THETA), -jnp.arange(half, dtype=F32) * 2.0 / ROT_DIM)
    ang = positions.astype(F32).reshape(-1, 1) * inv_freq
    cos, sin = jnp.cos(ang), jnp.sin(ang)
    T = ang.shape[0]
    rest = HEAD_DIM - ROT_DIM
    c64 = jnp.concatenate([cos, cos, jnp.ones((T, rest), F32)], axis=1)
    s64 = jnp.concatenate([-sin, sin, jnp.zeros((T, rest), F32)], axis=1)
    return jnp.tile(c64, (1, 2)), jnp.tile(s64, (1, 2))


def _gain_rows(q_gain, k_gain):
    q2 = jnp.tile(q_gain.reshape(1, HEAD_DIM), (GROUP_A, 2))
    k2 = jnp.tile(k_gain.reshape(1, HEAD_DIM), (1, 2))
    return jnp.concatenate([q2, k2, jnp.ones((1, LANES), F32)], axis=0)


def _local_step(x, positions, mod, norm1_g, norm2_g, q_norm_a, k_norm_a, sinks_a,
                wqkv_a, wo_a, wqkv_b, wo_b, wgu, wd, loss_target):
    NB, S, D = x.shape
    T = NB * S
    QA = N_Q_A * HEAD_DIM
    tab_c, tab_s = _rope_tables(positions)
    rev, fwd = _cumsum_mats()

    saved = []
    xc = x
    mods = [[mod[i][:, k * D:(k + 1) * D].reshape(NB, 1, D) for k in range(6)] for i in range(DEPTH)]
    h = _norm_mod_fwd(xc, norm1_g[0:1], mods[0][1], mods[0][0])
    for i in range(DEPTH):
        j = i // 2
        sh1, sc1, g1, sh2, sc2, g2 = mods[i]
        st = dict(x=xc, sc1=sc1, g1=g1, sc2=sc2, g2=g2)
        st["h"] = h.reshape(T, D)
        if i % 2 == 0:
            st["qkv"] = _matmul(st["h"], wqkv_a[j], "nn", F32, "qkv_a")
            st["gains"] = _gain_rows(q_norm_a[j], k_norm_a[j])
            st["qkn"] = _qk_prep_fwd(st["qkv"], tab_c, tab_s, st["gains"]).reshape(NB, S, -1)
            st["o"] = _attn_a_fwd(st["qkn"], sinks_a[j]).reshape(T, QA)
            y = _matmul(st["o"], wo_a[j], "nn", F32, "wo_a")
        else:
            st["qkv"] = _matmul(st["h"], wqkv_b[j], "nn", BF16, "qkv_b").reshape(NB, S, -1)
            st["o"] = _attn_b_fwd(st["qkv"], rev).reshape(T, N_H_B * HEAD_DIM)
            y = _matmul(st["o"], wo_b[j], "nn", F32, "wo_b")
        st["y"] = y.reshape(NB, S, D)
        x1, h2 = _gate_res(xc, st["y"], g1, norm=(norm2_g[i:i + 1], sc2, sh2))
        st["x1"] = x1
        st["h2"] = h2.reshape(T, D)
        st["gu"], st["act"] = _matmul(st["h2"], wgu[i], "nn", BF16, "gate_up", swiglu=True)
        st["m"] = _matmul(st["act"], wd[i], "nn", F32, "down").reshape(NB, S, D)
        if i + 1 < DEPTH:
            xc, h = _gate_res(x1, st["m"], g2, norm=(norm1_g[i + 1:i + 2], mods[i + 1][1], mods[i + 1][0]))
        else:
            xc = _gate_res(x1, st["m"], g2)
        saved.append(st)

    loss, dx, dm, dg2 = _loss_fwd_bwd(xc, loss_target, saved[-1]["m"], saved[-1]["g2"])

    grads = {name: [None] * n for name, n in
             (("wqkv_a", 2), ("wo_a", 2), ("wqkv_b", 2), ("wo_b", 2), ("wgu", DEPTH), ("wd", DEPTH),
              ("norm1_g", DEPTH), ("norm2_g", DEPTH), ("q_norm_a", 2), ("k_norm_a", 2), ("sinks_a", 2))}
    dmod = [None] * DEPTH
    for i in reversed(range(DEPTH)):
        j = i // 2
        st = saved[i]
        dm = dm.reshape(T, D)
        grads["wd"][i] = _matmul(st["act"], dm, "tn", F32, "d_wd")
        dgu = _swiglu_bwd(dm, wd[i], st["gu"])
        grads["wgu"][i] = _matmul(st["h2"], dgu, "tn", F32, "d_wgu")
        dx1, dsh2, dsc2, grads["norm2_g"][i], dy, dg1 = _norm_mod_bwd(
            dgu, wgu[i], st["x1"], norm2_g[i:i + 1], st["sc2"], dx, "d_h2", y=st["y"], g=st["g1"])
        dy = dy.reshape(T, D)
        if i % 2 == 0:
            do = _matmul(dy, wo_a[j], "nt", BF16, "d_o_a").reshape(NB, S, QA)
            grads["wo_a"][j] = _matmul(st["o"], dy, "tn", F32, "d_wo_a")
            dq, dk, dv, dsink = _attn_a_bwd(st["qkn"], do, sinks_a[j])
            dqkv, dgain = _qk_prep_bwd(st["qkv"], dq.reshape(T, QA), dk.reshape(T, LANES), dv.reshape(T, LANES),
                                       tab_c, tab_s, st["gains"])
            w_in = wqkv_a[j]
            grads["wqkv_a"][j] = _matmul(st["h"], dqkv, "tn", F32, "d_wqkv_a")
            grads["q_norm_a"][j] = jnp.sum(dgain[:GROUP_A].reshape(2 * GROUP_A, HEAD_DIM), axis=0)
            grads["k_norm_a"][j] = jnp.sum(dgain[GROUP_A].reshape(2, HEAD_DIM), axis=0)
            grads["sinks_a"][j] = jnp.sum(dsink[..., 0], axis=0)
        else:
            do = _matmul(dy, wo_b[j], "nt", BF16, "d_o_b").reshape(NB, S, -1)
            grads["wo_b"][j] = _matmul(st["o"], dy, "tn", F32, "d_wo_b")
            dq, dk, dv = _attn_b_bwd(st["qkv"], do, rev, fwd)
            dqkv = jnp.concatenate([dq, dk, dv], axis=-1).reshape(T, -1).astype(BF16)
            w_in = wqkv_b[j]
            grads["wqkv_b"][j] = _matmul(st["h"], dqkv, "tn", F32, "d_wqkv_b")
        this_dg2 = dg2
        if i > 0:
            dx, dsh1, dsc1, grads["norm1_g"][i], dm, dg2 = _norm_mod_bwd(
                dqkv, w_in, st["x"], norm1_g[i:i + 1], st["sc1"], dx1, "d_h", y=saved[i - 1]["m"], g=saved[i - 1]["g2"])
        else:
            dx, dsh1, dsc1, grads["norm1_g"][i] = _norm_mod_bwd(
                dqkv, w_in, st["x"], norm1_g[i:i + 1], st["sc1"], dx1, "d_h")
        dmod[i] = jnp.concatenate([dsh1, dsc1, dg1, dsh2, dsc2, this_dg2], axis=-1).reshape(NB, 6 * D)

    matrices = ("wqkv_a", "wo_a", "wqkv_b", "wo_b", "wgu", "wd")
    grads = {name: parts if name in matrices else jnp.stack(parts) for name, parts in grads.items()}
    return loss, dx, grads, jnp.stack(dmod)


def _rows_of(flat, cols=PACK_COLS):
    n = flat.shape[0]
    pad = (-n) % (8 * cols)
    if pad:
        flat = jnp.concatenate([flat, jnp.zeros((pad,), flat.dtype)])
    return flat.reshape(-1, cols)


def kernel(x, c, positions, ada_w, ada_b, norm1_g, norm2_g, wqkv_a, q_norm_a, k_norm_a, sinks_a, wo_a, wqkv_b, wo_b, w_gate, w_up, w_down, loss_target, m_ada_w, m_ada_b, m_norm1_g, m_norm2_g, m_wqkv_a, m_q_norm_a, m_k_norm_a, m_sinks_a, m_wo_a, m_wqkv_b, m_wo_b, m_w_gate, m_w_up, m_w_down, v_ada_w, v_ada_b, v_norm1_g, v_norm2_g, v_wqkv_a, v_q_norm_a, v_k_norm_a, v_sinks_a, v_wo_a, v_wqkv_b, v_wo_b, v_w_gate, v_w_up, v_w_down):
    xi, yi, ci = lax.axis_index("x"), lax.axis_index("y"), lax.axis_index("c")
    dev = 4 * xi + 2 * yi + ci
    chip = 2 * xi + yi
    NB, S, D = x.shape
    B_all = N_DEV * NB
    L = ada_w.shape[0]
    n_mod = ada_w.shape[2] // 2

    c_all = _gather8(_rows_of(c.reshape(-1), LANES), "gather_c").reshape(N_DEV, -1)[:, :NB * D].reshape(B_all, D)
    ada_w_half = lax.dynamic_slice_in_dim(ada_w, ci * n_mod, n_mod, axis=2)
    ada_b_half = lax.dynamic_slice_in_dim(ada_b, dev * n_mod, n_mod, axis=1).reshape(L, 1, n_mod)
    mod_part = _ada_fwd(c_all, ada_w_half, ada_b_half)
    n_part = L * B_all * n_mod
    mod_all = _gather8(_rows_of(mod_part.reshape(-1)), "gather_mod").reshape(N_DEV, -1)[:, :n_part]
    mod_all = mod_all.reshape(N_DEV, L, B_all, n_mod).transpose(1, 2, 0, 3).reshape(L, B_all, N_DEV * n_mod)
    mod = lax.dynamic_slice_in_dim(mod_all, dev * NB, NB, axis=1)

    shards = dict(wqkv_a=wqkv_a, wo_a=wo_a, wqkv_b=wqkv_b, wo_b=wo_b, w_gate=w_gate, w_up=w_up, w_down=w_down)
    halves = []
    for name, _ in _SHARDED:
        w = shards[name]
        half = lax.dynamic_index_in_dim(w.reshape((2, w.shape[0] // 2) + w.shape[1:]), ci, 0, keepdims=False)
        halves.append(half.astype(BF16))
    gathered = _all_gather8(halves, "gather_weights", local_axis=1, local_chunks=8, relay_axis=1)
    full = {name: _unpack_full(t, axis) for (name, axis), t in zip(_SHARDED, gathered)}
    wgu = [_interleave(gate, up) for gate, up in zip(full["w_gate"], full["w_up"])]

    loss, grad_x, g, dmod = _local_step(
        x, positions, mod, norm1_g, norm2_g, q_norm_a, k_norm_a, sinks_a,
        full["wqkv_a"], full["wo_a"], full["wqkv_b"], full["wo_b"], wgu, full["w_down"], loss_target)

    g_full = dict(wqkv_a=g["wqkv_a"], wo_a=g["wo_a"], wqkv_b=g["wqkv_b"], wo_b=g["wo_b"],
                  w_gate=g["wgu"], w_up=g["wgu"], w_down=g["wd"])
    which = dict(w_gate=0, w_up=1)
    packed = [_pack_full(g_full[name], axis, which.get(name)) for name, axis in _SHARDED]
    def own(t, index):
        return lax.dynamic_index_in_dim(t, index, 0, keepdims=False)

    from_cores = _exchange_cores(packed, "rs_cores", chunk_axis=0, chunks=4)
    chip_part = [_sum_slabs(own(p, ci), r, "rs_add_cores", with_bf16=True) for p, r in zip(packed, from_cores)]
    from_chips = _exchange_chips([b for _, b in chip_part], "rs_chips", relay_axis=1)
    mine = [_sum_slabs(own(p, chip), r, "rs_add_chips") for (p, _), r in zip(chip_part, from_chips)]
    theirs = _sibling_send(mine, "rs_halves")
    grad = {}
    for (name, _), m, t in zip(_SHARDED, mine, theirs):
        first, second = jnp.where(ci == 0, m, t), jnp.where(ci == 0, t, m)
        grad[name] = jnp.stack([first, second]).reshape(shards[name].shape)

    small_names = ("norm1_g", "norm2_g", "q_norm_a", "k_norm_a", "sinks_a")
    small = [dmod.reshape(-1)] + [g[name].reshape(-1) for name in small_names] + [loss.reshape(-1)]
    small_sizes = [t.shape[0] for t in small]
    small_rows = _rows_of(jnp.concatenate(small))
    small_all = _gather8(small_rows, "gather_small")
    small_sum = _sum_leading(small_all, "sum_small").reshape(-1)
    n_dmod = small_sizes[0]
    dmod_all = small_all.reshape(N_DEV, -1)[:, :n_dmod].reshape(N_DEV, L, NB, 6 * D)
    dmod_all = dmod_all.transpose(1, 0, 2, 3).reshape(L, B_all, 6 * D)
    off = n_dmod
    for name, sz in zip(small_names + ("loss",), small_sizes[1:]):
        grad[name] = small_sum[off:off + sz]
        off += sz
    loss_total = grad.pop("loss").reshape(())
    for name, ref in (("norm1_g", norm1_g), ("norm2_g", norm2_g), ("q_norm_a", q_norm_a),
                      ("k_norm_a", k_norm_a), ("sinks_a", sinks_a)):
        grad[name] = grad[name].reshape(ref.shape)

    n_shard = ada_w.shape[2]
    dmod_shard = lax.dynamic_slice_in_dim(dmod_all, chip * n_shard, n_shard, axis=2)
    grad["ada_w"], gb = _ada_bwd(c_all, dmod_all, dmod_shard)
    grad["ada_b"] = gb.reshape(ada_b.shape)

    weights = dict(ada_w=ada_w, ada_b=ada_b, norm1_g=norm1_g, norm2_g=norm2_g, wqkv_a=wqkv_a, q_norm_a=q_norm_a,
                   k_norm_a=k_norm_a, sinks_a=sinks_a, wo_a=wo_a, wqkv_b=wqkv_b, wo_b=wo_b, w_gate=w_gate,
                   w_up=w_up, w_down=w_down)
    m_in = dict(ada_w=m_ada_w, ada_b=m_ada_b, norm1_g=m_norm1_g, norm2_g=m_norm2_g, wqkv_a=m_wqkv_a,
                q_norm_a=m_q_norm_a, k_norm_a=m_k_norm_a, sinks_a=m_sinks_a, wo_a=m_wo_a, wqkv_b=m_wqkv_b,
                wo_b=m_wo_b, w_gate=m_w_gate, w_up=m_w_up, w_down=m_w_down)
    v_in = dict(ada_w=v_ada_w, ada_b=v_ada_b, norm1_g=v_norm1_g, norm2_g=v_norm2_g, wqkv_a=v_wqkv_a,
                q_norm_a=v_q_norm_a, k_norm_a=v_k_norm_a, sinks_a=v_sinks_a, wo_a=v_wo_a, wqkv_b=v_wqkv_b,
                wo_b=v_wo_b, w_gate=v_w_gate, w_up=v_w_up, w_down=v_w_down)
    names = list(weights)
    delta, new_m, new_v = {}, {}, {}
    for name in names:
        delta[name], new_m[name], new_v[name] = _adamw(weights[name], grad[name], m_in[name], v_in[name],
                                                       "adamw_" + name)
    return (loss_total, grad_x, *[grad[k] for k in names], *[delta[k] for k in names],
            *[new_m[k] for k in names], *[new_v[k] for k in names])
```
